```python
import jax, jax.numpy as jnp
from jax import lax
import numpy as np

D_MODEL = 1024
BATCH = 8
SEQ = 2048
DEPTH = 1

D_RNN = D_MODEL
N_LRU_HEADS = 16
LRU_HEAD_DIM = D_RNN // N_LRU_HEADS
CONV_WIDTH = 4
LRU_C = 8.0
LRU_A_MIN = 0.9
LRU_A_MAX = 0.999
D_POOL = D_MODEL // 2
POOL_WINDOWS = (2, 4, 8, 16)
N_POOL_GROUPS = len(POOL_WINDOWS)
POOL_GROUP_DIM = D_POOL // N_POOL_GROUPS
N_BRANCHES = 2
D_FF = 4 * D_MODEL
NORM_EPS = 1e-6
D_IN = 2 * D_RNN + D_POOL + N_BRANCHES * D_MODEL

kernel_name = "hawk_pool_gated_hybrid_block"


def rms_norm(x, g):
    xf = x.astype(jnp.float32)
    y = xf * lax.rsqrt(jnp.mean(xf * xf, axis=-1, keepdims=True) + NORM_EPS)
    return (y * g.astype(jnp.float32)).astype(x.dtype)


def causal_depthwise_conv(x, w, b):
    S = x.shape[1]
    xp = jnp.pad(x, ((0, 0), (CONV_WIDTH - 1, 0), (0, 0)))
    out = b
    for k in range(CONV_WIDTH):
        out = out + xp[:, k:k + S] * w[k]
    return out


def rg_lru(x, w_a, b_a, w_x, b_x, lam):
    B, S, _ = x.shape
    xh = x.reshape(B, S, N_LRU_HEADS, LRU_HEAD_DIM)
    r = jnp.einsum('bshi,hij->bshj', xh, w_a).reshape(B, S, D_RNN) + b_a
    i = jnp.einsum('bshi,hij->bshj', xh, w_x).reshape(B, S, D_RNN) + b_x
    r = jax.nn.sigmoid(r.astype(jnp.float32))
    i = jax.nn.sigmoid(i.astype(jnp.float32))
    log_a = -LRU_C * r * jax.nn.softplus(-lam.astype(jnp.float32))
    a = jnp.exp(log_a)
    mult = jnp.sqrt(-jnp.expm1(2.0 * log_a))
    u = mult * (i * x.astype(jnp.float32))

    def combine(left, right):
        a1, b1 = left
        a2, b2 = right
        return a1 * a2, a2 * b1 + b2

    _, h = lax.associative_scan(combine, (a, u), axis=1)
    return h.astype(x.dtype)


def multiscale_pool(x, w_grp, scale):
    B, S, _ = x.shape
    xf = x.astype(jnp.float32)
    cs = jnp.cumsum(xf, axis=1)
    pos = jnp.arange(1, S + 1, dtype=jnp.float32)[None, :, None]
    outs = []
    for g, w in enumerate(POOL_WINDOWS):
        sl = slice(g * POOL_GROUP_DIM, (g + 1) * POOL_GROUP_DIM)
        c = cs[..., sl]
        c_prev = jnp.pad(c, ((0, 0), (w, 0), (0, 0)))[:, :S]
        cnt = jnp.minimum(pos, float(w))
        outs.append((c - c_prev) / cnt - xf[..., sl])
    p = jnp.stack(outs, axis=2)
    y = jnp.einsum('bsgi,gij->bsgj', p, w_grp.astype(jnp.float32)).reshape(B, S, D_POOL)
    return (y * scale.astype(jnp.float32)).astype(x.dtype)


def hybrid_mixer(h, w_in, b_gate, conv_w, conv_b, lru_w_a, lru_b_a, lru_w_x, lru_b_x,
                 lru_lambda, pool_w, pool_scale, w_lru_up, w_pool_up, w_o):
    proj = h @ w_in
    x_lru, g_lru, x_pool, gates = jnp.split(
        proj, [D_RNN, 2 * D_RNN, 2 * D_RNN + D_POOL], axis=-1)
    x_lru = causal_depthwise_conv(x_lru, conv_w, conv_b)
    y_lru = rg_lru(x_lru, lru_w_a, lru_b_a, lru_w_x, lru_b_x, lru_lambda) * jax.nn.gelu(g_lru)
    y_pool = multiscale_pool(x_pool, pool_w, pool_scale)
    br_a = y_lru @ w_lru_up
    br_b = y_pool @ w_pool_up
    gate_a, gate_b = jnp.split(jax.nn.sigmoid(gates + b_gate), N_BRANCHES, axis=-1)
    return (gate_a * br_a + gate_b * br_b) @ w_o


def sq_relu_mlp(h, w_ff1, w_ff2):
    return jnp.square(jax.nn.relu(h @ w_ff1)) @ w_ff2


def _fwd_setup_inputs(seed: int = 0) -> dict:
    key = jax.random.key(seed)
    ks = jax.random.split(key, 24)
    L = DEPTH
    f32 = jnp.float32

    def nrm(k, shape, fan_in):
        return jax.random.normal(k, shape, f32) * (fan_in ** -0.5)

    def gain(k, shape):
        return 1.0 + 0.02 * jax.random.normal(k, shape, f32)

    def bias(k, shape):
        return 0.01 * jax.random.normal(k, shape, f32)

    u = jax.random.uniform(ks[12], (L, D_RNN), f32, LRU_A_MIN, LRU_A_MAX)
    a_base = u ** (1.0 / LRU_C)
    lru_lambda = jnp.log(a_base) - jnp.log1p(-a_base)

    return {
        "x": jax.random.normal(ks[0], (BATCH, SEQ, D_MODEL), f32),
        "norm_mix_pre": gain(ks[1], (L, D_MODEL)),
        "norm_mix_post": gain(ks[2], (L, D_MODEL)),
        "norm_mlp_pre": gain(ks[3], (L, D_MODEL)),
        "norm_mlp_post": gain(ks[4], (L, D_MODEL)),
        "w_in": nrm(ks[5], (L, D_MODEL, D_IN), D_MODEL),
        "b_gate": bias(ks[6], (L, N_BRANCHES * D_MODEL)),
        "conv_w": nrm(ks[7], (L, CONV_WIDTH, D_RNN), CONV_WIDTH),
        "conv_b": bias(ks[8], (L, D_RNN)),
        "lru_w_a": nrm(ks[9], (L, N_LRU_HEADS, LRU_HEAD_DIM, LRU_HEAD_DIM), LRU_HEAD_DIM),
        "lru_b_a": bias(ks[10], (L, D_RNN)),
        "lru_w_x": nrm(ks[11], (L, N_LRU_HEADS, LRU_HEAD_DIM, LRU_HEAD_DIM), LRU_HEAD_DIM),
        "lru_b_x": bias(ks[13], (L, D_RNN)),
        "lru_lambda": lru_lambda,
        "pool_w": nrm(ks[14], (L, N_POOL_GROUPS, POOL_GROUP_DIM, POOL_GROUP_DIM), POOL_GROUP_DIM),
        "pool_scale": gain(ks[15], (L, D_POOL)),
        "w_lru_up": nrm(ks[16], (L, D_RNN, D_MODEL), D_RNN),
        "w_pool_up": nrm(ks[17], (L, D_POOL, D_MODEL), D_POOL),
        "w_o": nrm(ks[18], (L, D_MODEL, D_MODEL), D_MODEL),
        "w_ff1": nrm(ks[19], (L, D_MODEL, D_FF), D_MODEL),
        "w_ff2": nrm(ks[20], (L, D_FF, D_MODEL), D_FF),
    }


def _fwd_reference(x, norm_mix_pre, norm_mix_post, norm_mlp_pre, norm_mlp_post, w_in, b_gate,
              conv_w, conv_b, lru_w_a, lru_b_a, lru_w_x, lru_b_x, lru_lambda, pool_w,
              pool_scale, w_lru_up, w_pool_up, w_o, w_ff1, w_ff2):
    for l in range(DEPTH):
        h = rms_norm(x, norm_mix_pre[l])
        m = hybrid_mixer(h, w_in[l], b_gate[l], conv_w[l], conv_b[l], lru_w_a[l], lru_b_a[l],
                         lru_w_x[l], lru_b_x[l], lru_lambda[l], pool_w[l], pool_scale[l],
                         w_lru_up[l], w_pool_up[l], w_o[l])
        x = x + rms_norm(m, norm_mix_post[l])
        h = rms_norm(x, norm_mlp_pre[l])
        f = sq_relu_mlp(h, w_ff1[l], w_ff2[l])
        x = x + rms_norm(f, norm_mlp_post[l])
    return x


import jax as _jax
import jax.numpy as _jnp

TWIN_FORMAT = 'train_step'
FWD_PARAMS = ['x', 'norm_mix_pre', 'norm_mix_post', 'norm_mlp_pre', 'norm_mlp_post', 'w_in', 'b_gate', 'conv_w', 'conv_b', 'lru_w_a', 'lru_b_a', 'lru_w_x', 'lru_b_x', 'lru_lambda', 'pool_w', 'pool_scale', 'w_lru_up', 'w_pool_up', 'w_o', 'w_ff1', 'w_ff2']
TWIN_WEIGHTS = ['norm_mix_pre', 'norm_mix_post', 'norm_mlp_pre', 'norm_mlp_post', 'w_in', 'b_gate', 'conv_w', 'conv_b', 'lru_w_a', 'lru_b_a', 'lru_w_x', 'lru_b_x', 'lru_lambda', 'pool_w', 'pool_scale', 'w_lru_up', 'w_pool_up', 'w_o', 'w_ff1', 'w_ff2']
TWIN_DIFF_INPUT = 'x'
TWIN_INPUTS = ['x', 'norm_mix_pre', 'norm_mix_post', 'norm_mlp_pre', 'norm_mlp_post', 'w_in', 'b_gate', 'conv_w', 'conv_b', 'lru_w_a', 'lru_b_a', 'lru_w_x', 'lru_b_x', 'lru_lambda', 'pool_w', 'pool_scale', 'w_lru_up', 'w_pool_up', 'w_o', 'w_ff1', 'w_ff2', 'loss_target', 'm_norm_mix_pre', 'm_norm_mix_post', 'm_norm_mlp_pre', 'm_norm_mlp_post', 'm_w_in', 'm_b_gate', 'm_conv_w', 'm_conv_b', 'm_lru_w_a', 'm_lru_b_a', 'm_lru_w_x', 'm_lru_b_x', 'm_lru_lambda', 'm_pool_w', 'm_pool_scale', 'm_w_lru_up', 'm_w_pool_up', 'm_w_o', 'm_w_ff1', 'm_w_ff2', 'v_norm_mix_pre', 'v_norm_mix_post', 'v_norm_mlp_pre', 'v_norm_mlp_post', 'v_w_in', 'v_b_gate', 'v_conv_w', 'v_conv_b', 'v_lru_w_a', 'v_lru_b_a', 'v_lru_w_x', 'v_lru_b_x', 'v_lru_lambda', 'v_pool_w', 'v_pool_scale', 'v_w_lru_up', 'v_w_pool_up', 'v_w_o', 'v_w_ff1', 'v_w_ff2']
TWIN_OUTPUTS = ['loss', 'grad_x', 'grad_norm_mix_pre', 'grad_norm_mix_post', 'grad_norm_mlp_pre', 'grad_norm_mlp_post', 'grad_w_in', 'grad_b_gate', 'grad_conv_w', 'grad_conv_b', 'grad_lru_w_a', 'grad_lru_b_a', 'grad_lru_w_x', 'grad_lru_b_x', 'grad_lru_lambda', 'grad_pool_w', 'grad_pool_scale', 'grad_w_lru_up', 'grad_w_pool_up', 'grad_w_o', 'grad_w_ff1', 'grad_w_ff2', 'delta_norm_mix_pre', 'delta_norm_mix_post', 'delta_norm_mlp_pre', 'delta_norm_mlp_post', 'delta_w_in', 'delta_b_gate', 'delta_conv_w', 'delta_conv_b', 'delta_lru_w_a', 'delta_lru_b_a', 'delta_lru_w_x', 'delta_lru_b_x', 'delta_lru_lambda', 'delta_pool_w', 'delta_pool_scale', 'delta_w_lru_up', 'delta_w_pool_up', 'delta_w_o', 'delta_w_ff1', 'delta_w_ff2', 'new_m_norm_mix_pre', 'new_m_norm_mix_post', 'new_m_norm_mlp_pre', 'new_m_norm_mlp_post', 'new_m_w_in', 'new_m_b_gate', 'new_m_conv_w', 'new_m_conv_b', 'new_m_lru_w_a', 'new_m_lru_b_a', 'new_m_lru_w_x', 'new_m_lru_b_x', 'new_m_lru_lambda', 'new_m_pool_w', 'new_m_pool_scale', 'new_m_w_lru_up', 'new_m_w_pool_up', 'new_m_w_o', 'new_m_w_ff1', 'new_m_w_ff2', 'new_v_norm_mix_pre', 'new_v_norm_mix_post', 'new_v_norm_mlp_pre', 'new_v_norm_mlp_post', 'new_v_w_in', 'new_v_b_gate', 'new_v_conv_w', 'new_v_conv_b', 'new_v_lru_w_a', 'new_v_lru_b_a', 'new_v_lru_w_x', 'new_v_lru_b_x', 'new_v_lru_lambda', 'new_v_pool_w', 'new_v_pool_scale', 'new_v_w_lru_up', 'new_v_w_pool_up', 'new_v_w_o', 'new_v_w_ff1', 'new_v_w_ff2']
TWIN_LEAF_KINDS = {'loss': 'loss', 'grad_x': 'grad_x', 'grad_norm_mix_pre': 'grad_w', 'grad_norm_mix_post': 'grad_w', 'grad_norm_mlp_pre': 'grad_w', 'grad_norm_mlp_post': 'grad_w', 'grad_w_in': 'grad_w', 'grad_b_gate': 'grad_w', 'grad_conv_w': 'grad_w', 'grad_conv_b': 'grad_w', 'grad_lru_w_a': 'grad_w', 'grad_lru_b_a': 'grad_w', 'grad_lru_w_x': 'grad_w', 'grad_lru_b_x': 'grad_w', 'grad_lru_lambda': 'grad_w', 'grad_pool_w': 'grad_w', 'grad_pool_scale': 'grad_w', 'grad_w_lru_up': 'grad_w', 'grad_w_pool_up': 'grad_w', 'grad_w_o': 'grad_w', 'grad_w_ff1': 'grad_w', 'grad_w_ff2': 'grad_w', 'delta_norm_mix_pre': 'delta_w', 'delta_norm_mix_post': 'delta_w', 'delta_norm_mlp_pre': 'delta_w', 'delta_norm_mlp_post': 'delta_w', 'delta_w_in': 'delta_w', 'delta_b_gate': 'delta_w', 'delta_conv_w': 'delta_w', 'delta_conv_b': 'delta_w', 'delta_lru_w_a': 'delta_w', 'delta_lru_b_a': 'delta_w', 'delta_lru_w_x': 'delta_w', 'delta_lru_b_x': 'delta_w', 'delta_lru_lambda': 'delta_w', 'delta_pool_w': 'delta_w', 'delta_pool_scale': 'delta_w', 'delta_w_lru_up': 'delta_w', 'delta_w_pool_up': 'delta_w', 'delta_w_o': 'delta_w', 'delta_w_ff1': 'delta_w', 'delta_w_ff2': 'delta_w', 'new_m_norm_mix_pre': 'new_m', 'new_m_norm_mix_post': 'new_m', 'new_m_norm_mlp_pre': 'new_m', 'new_m_norm_mlp_post': 'new_m', 'new_m_w_in': 'new_m', 'new_m_b_gate': 'new_m', 'new_m_conv_w': 'new_m', 'new_m_conv_b': 'new_m', 'new_m_lru_w_a': 'new_m', 'new_m_lru_b_a': 'new_m', 'new_m_lru_w_x': 'new_m', 'new_m_lru_b_x': 'new_m', 'new_m_lru_lambda': 'new_m', 'new_m_pool_w': 'new_m', 'new_m_pool_scale': 'new_m', 'new_m_w_lru_up': 'new_m', 'new_m_w_pool_up': 'new_m', 'new_m_w_o': 'new_m', 'new_m_w_ff1': 'new_m', 'new_m_w_ff2': 'new_m', 'new_v_norm_mix_pre': 'new_v', 'new_v_norm_mix_post': 'new_v', 'new_v_norm_mlp_pre': 'new_v', 'new_v_norm_mlp_post': 'new_v', 'new_v_w_in': 'new_v', 'new_v_b_gate': 'new_v', 'new_v_conv_w': 'new_v', 'new_v_conv_b': 'new_v', 'new_v_lru_w_a': 'new_v', 'new_v_lru_b_a': 'new_v', 'new_v_lru_w_x': 'new_v', 'new_v_lru_b_x': 'new_v', 'new_v_lru_lambda': 'new_v', 'new_v_pool_w': 'new_v', 'new_v_pool_scale': 'new_v', 'new_v_w_lru_up': 'new_v', 'new_v_w_pool_up': 'new_v', 'new_v_w_o': 'new_v', 'new_v_w_ff1': 'new_v', 'new_v_w_ff2': 'new_v'}


def _forward(args):
    return _fwd_reference(*[args[k] for k in FWD_PARAMS])


def _output_shape():
    out = _jax.eval_shape(lambda: _forward(_fwd_setup_inputs(0)))
    return out.shape, out.dtype

N_MICROBATCH = 1
ADAM_LR = 0.001
ADAM_B1 = 0.9
ADAM_B2 = 0.999
ADAM_EPS = 1e-08
ADAM_WD = 0.01
ADAM_STEP = 10
PER_EXAMPLE_BATCH_AXIS = {'x': 0, 'loss_target': 0}
SHARED_INPUTS = []
_WEIGHT_DTYPES = {'norm_mix_pre': _jnp.float32, 'norm_mix_post': _jnp.float32, 'norm_mlp_pre': _jnp.float32, 'norm_mlp_post': _jnp.float32, 'w_in': _jnp.float32, 'b_gate': _jnp.float32, 'conv_w': _jnp.float32, 'conv_b': _jnp.float32, 'lru_w_a': _jnp.float32, 'lru_b_a': _jnp.float32, 'lru_w_x': _jnp.float32, 'lru_b_x': _jnp.float32, 'lru_lambda': _jnp.float32, 'pool_w': _jnp.float32, 'pool_scale': _jnp.float32, 'w_lru_up': _jnp.float32, 'w_pool_up': _jnp.float32, 'w_o': _jnp.float32, 'w_ff1': _jnp.float32, 'w_ff2': _jnp.float32}
MOMENT_SCALE = {'norm_mix_pre': 5.597457e-01, 'norm_mix_post': 1.598533e+01, 'norm_mlp_pre': 6.437955e-01, 'norm_mlp_post': 1.631971e+01, 'w_in': 2.423294e-01, 'b_gate': 2.582915e-01, 'conv_w': 5.862719e-01, 'conv_b': 7.788198e+00, 'lru_w_a': 2.824940e-01, 'lru_b_a': 1.491719e-01, 'lru_w_x': 5.154820e-01, 'lru_b_x': 1.267790e-01, 'lru_lambda': 2.424672e-01, 'pool_w': 7.178242e-01, 'pool_scale': 9.094477e-01, 'w_lru_up': 7.051365e-01, 'w_pool_up': 5.759223e-01, 'w_o': 9.047316e-01, 'w_ff1': 3.016048e-01, 'w_ff2': 8.807528e-01}


def _to_microbatches(a, axis):
    t = _jnp.moveaxis(a, axis, 0)
    t = t.reshape((N_MICROBATCH, t.shape[0] // N_MICROBATCH) + t.shape[1:])
    return _jnp.moveaxis(t, 1, axis + 1)


def setup_inputs(seed: int = 0) -> dict:
    inp = _fwd_setup_inputs(seed)
    key = _jax.random.fold_in(_jax.random.key(seed), 7919)
    shape, _ = _output_shape()
    out = dict(inp)
    out["loss_target"] = _jax.random.normal(_jax.random.fold_in(key, 0), shape, _jnp.float32)
    for i, name in enumerate(TWIN_WEIGHTS):
        w = inp[name].astype(_jnp.float32)
        if MOMENT_SCALE is None:
            s = _jnp.sqrt(_jnp.mean(_jnp.square(w)) + 1e-30)
        else:
            s = MOMENT_SCALE[name]
        km, kv = _jax.random.split(_jax.random.fold_in(key, i + 1))
        out[name] = w
        out["m_" + name] = s * _jax.random.normal(km, w.shape, _jnp.float32)
        out["v_" + name] = (s * s) * _jax.random.uniform(kv, w.shape, _jnp.float32, 0.5, 1.5)
    if N_MICROBATCH > 1:
        for name, axis in PER_EXAMPLE_BATCH_AXIS.items():
            out[name] = _to_microbatches(out[name], axis)
    return {'x': out['x'], 'norm_mix_pre': out['norm_mix_pre'], 'norm_mix_post': out['norm_mix_post'], 'norm_mlp_pre': out['norm_mlp_pre'], 'norm_mlp_post': out['norm_mlp_post'], 'w_in': out['w_in'], 'b_gate': out['b_gate'], 'conv_w': out['conv_w'], 'conv_b': out['conv_b'], 'lru_w_a': out['lru_w_a'], 'lru_b_a': out['lru_b_a'], 'lru_w_x': out['lru_w_x'], 'lru_b_x': out['lru_b_x'], 'lru_lambda': out['lru_lambda'], 'pool_w': out['pool_w'], 'pool_scale': out['pool_scale'], 'w_lru_up': out['w_lru_up'], 'w_pool_up': out['w_pool_up'], 'w_o': out['w_o'], 'w_ff1': out['w_ff1'], 'w_ff2': out['w_ff2'], 'loss_target': out['loss_target'], 'm_norm_mix_pre': out['m_norm_mix_pre'], 'm_norm_mix_post': out['m_norm_mix_post'], 'm_norm_mlp_pre': out['m_norm_mlp_pre'], 'm_norm_mlp_post': out['m_norm_mlp_post'], 'm_w_in': out['m_w_in'], 'm_b_gate': out['m_b_gate'], 'm_conv_w': out['m_conv_w'], 'm_conv_b': out['m_conv_b'], 'm_lru_w_a': out['m_lru_w_a'], 'm_lru_b_a': out['m_lru_b_a'], 'm_lru_w_x': out['m_lru_w_x'], 'm_lru_b_x': out['m_lru_b_x'], 'm_lru_lambda': out['m_lru_lambda'], 'm_pool_w': out['m_pool_w'], 'm_pool_scale': out['m_pool_scale'], 'm_w_lru_up': out['m_w_lru_up'], 'm_w_pool_up': out['m_w_pool_up'], 'm_w_o': out['m_w_o'], 'm_w_ff1': out['m_w_ff1'], 'm_w_ff2': out['m_w_ff2'], 'v_norm_mix_pre': out['v_norm_mix_pre'], 'v_norm_mix_post': out['v_norm_mix_post'], 'v_norm_mlp_pre': out['v_norm_mlp_pre'], 'v_norm_mlp_post': out['v_norm_mlp_post'], 'v_w_in': out['v_w_in'], 'v_b_gate': out['v_b_gate'], 'v_conv_w': out['v_conv_w'], 'v_conv_b': out['v_conv_b'], 'v_lru_w_a': out['v_lru_w_a'], 'v_lru_b_a': out['v_lru_b_a'], 'v_lru_w_x': out['v_lru_w_x'], 'v_lru_b_x': out['v_lru_b_x'], 'v_lru_lambda': out['v_lru_lambda'], 'v_pool_w': out['v_pool_w'], 'v_pool_scale': out['v_pool_scale'], 'v_w_lru_up': out['v_w_lru_up'], 'v_w_pool_up': out['v_w_pool_up'], 'v_w_o': out['v_w_o'], 'v_w_ff1': out['v_w_ff1'], 'v_w_ff2': out['v_w_ff2']}


def _loss(weights, diff, rest, loss_target):
    with _jax.named_scope("forward"):
        args = {**rest, TWIN_DIFF_INPUT: diff, **{k: w.astype(_WEIGHT_DTYPES[k]) for k, w in weights.items()}}
        y = _forward(args)
    with _jax.named_scope("loss_head"):
        err = _jnp.square(y.astype(_jnp.float32) - loss_target)
        return 0.5 * _jnp.sum(_jnp.mean(err, axis=-1)) if err.ndim else 0.5 * err


def _adamw(w, g, m, v):
    m = ADAM_B1 * m + (1.0 - ADAM_B1) * g
    v = ADAM_B2 * v + (1.0 - ADAM_B2) * _jnp.square(g)
    m_hat = m / (1.0 - ADAM_B1 ** ADAM_STEP)
    v_hat = v / (1.0 - ADAM_B2 ** ADAM_STEP)
    delta = -ADAM_LR * (m_hat / (_jnp.sqrt(v_hat) + ADAM_EPS) + ADAM_WD * w)
    return delta, m, v


def reference(x, norm_mix_pre, norm_mix_post, norm_mlp_pre, norm_mlp_post, w_in, b_gate, conv_w, conv_b, lru_w_a, lru_b_a, lru_w_x, lru_b_x, lru_lambda, pool_w, pool_scale, w_lru_up, w_pool_up, w_o, w_ff1, w_ff2, loss_target, m_norm_mix_pre, m_norm_mix_post, m_norm_mlp_pre, m_norm_mlp_post, m_w_in, m_b_gate, m_conv_w, m_conv_b, m_lru_w_a, m_lru_b_a, m_lru_w_x, m_lru_b_x, m_lru_lambda, m_pool_w, m_pool_scale, m_w_lru_up, m_w_pool_up, m_w_o, m_w_ff1, m_w_ff2, v_norm_mix_pre, v_norm_mix_post, v_norm_mlp_pre, v_norm_mlp_post, v_w_in, v_b_gate, v_conv_w, v_conv_b, v_lru_w_a, v_lru_b_a, v_lru_w_x, v_lru_b_x, v_lru_lambda, v_pool_w, v_pool_scale, v_w_lru_up, v_w_pool_up, v_w_o, v_w_ff1, v_w_ff2):
    given = dict(x=x, norm_mix_pre=norm_mix_pre, norm_mix_post=norm_mix_post, norm_mlp_pre=norm_mlp_pre, norm_mlp_post=norm_mlp_post, w_in=w_in, b_gate=b_gate, conv_w=conv_w, conv_b=conv_b, lru_w_a=lru_w_a, lru_b_a=lru_b_a, lru_w_x=lru_w_x, lru_b_x=lru_b_x, lru_lambda=lru_lambda, pool_w=pool_w, pool_scale=pool_scale, w_lru_up=w_lru_up, w_pool_up=w_pool_up, w_o=w_o, w_ff1=w_ff1, w_ff2=w_ff2, loss_target=loss_target, m_norm_mix_pre=m_norm_mix_pre, m_norm_mix_post=m_norm_mix_post, m_norm_mlp_pre=m_norm_mlp_pre, m_norm_mlp_post=m_norm_mlp_post, m_w_in=m_w_in, m_b_gate=m_b_gate, m_conv_w=m_conv_w, m_conv_b=m_conv_b, m_lru_w_a=m_lru_w_a, m_lru_b_a=m_lru_b_a, m_lru_w_x=m_lru_w_x, m_lru_b_x=m_lru_b_x, m_lru_lambda=m_lru_lambda, m_pool_w=m_pool_w, m_pool_scale=m_pool_scale, m_w_lru_up=m_w_lru_up, m_w_pool_up=m_w_pool_up, m_w_o=m_w_o, m_w_ff1=m_w_ff1, m_w_ff2=m_w_ff2, v_norm_mix_pre=v_norm_mix_pre, v_norm_mix_post=v_norm_mix_post, v_norm_mlp_pre=v_norm_mlp_pre, v_norm_mlp_post=v_norm_mlp_post, v_w_in=v_w_in, v_b_gate=v_b_gate, v_conv_w=v_conv_w, v_conv_b=v_conv_b, v_lru_w_a=v_lru_w_a, v_lru_b_a=v_lru_b_a, v_lru_w_x=v_lru_w_x, v_lru_b_x=v_lru_b_x, v_lru_lambda=v_lru_lambda, v_pool_w=v_pool_w, v_pool_scale=v_pool_scale, v_w_lru_up=v_w_lru_up, v_w_pool_up=v_w_pool_up, v_w_o=v_w_o, v_w_ff1=v_w_ff1, v_w_ff2=v_w_ff2)
    weights = {n: given[n] for n in TWIN_WEIGHTS}
    shared = {n: given[n] for n in SHARED_INPUTS}
    per_example = {n: given[n] for n in ['x']}
    grad_fn = _jax.value_and_grad(_loss, argnums=(0, 1))

    def one_microbatch(ex, loss_target):
        ex = dict(ex)
        diff = ex.pop(TWIN_DIFF_INPUT)
        return grad_fn(weights, diff, {**shared, **ex}, loss_target)

    if N_MICROBATCH == 1:
        loss, (grad_w, grad_x) = one_microbatch(per_example, given["loss_target"])
    else:
        def body(carry, xs):
            loss_sum, grad_sum = carry
            l_k, (gw_k, gx_k) = one_microbatch(xs[0], xs[1])
            with _jax.named_scope("update"):
                return (loss_sum + l_k, _jax.tree.map(_jnp.add, grad_sum, gw_k)), gx_k

        init = (_jnp.zeros((), _jnp.float32), _jax.tree.map(_jnp.zeros_like, weights))
        (loss, grad_w), grad_x = _jax.lax.scan(body, init, (per_example, given["loss_target"]))
    with _jax.named_scope("update"):
        delta_w, new_m, new_v = {}, {}, {}
        for n in TWIN_WEIGHTS:
            delta_w[n], new_m[n], new_v[n] = _adamw(weights[n], grad_w[n], given["m_" + n], given["v_" + n])
    return (loss, grad_x, *[grad_w[n] for n in TWIN_WEIGHTS], *[delta_w[n] for n in TWIN_WEIGHTS],
            *[new_m[n] for n in TWIN_WEIGHTS], *[new_v[n] for n in TWIN_WEIGHTS])
```

```python
import functools
import math

import jax
import jax.numpy as jnp
from jax import lax
from jax.experimental import pallas as pl
from jax.experimental.pallas import tpu as pltpu

F32 = jnp.float32
BF16 = jnp.bfloat16
NORM_EPS = 1e-6
LRU_C = 8.0
N_LRU_HEADS = 16
LRU_HEAD_DIM = 64
POOL_WINDOWS = (2, 4, 8, 16)
POOL_GROUP_DIM = 128
ADAM_LR = 0.001
ADAM_B1 = 0.9
ADAM_B2 = 0.999
ADAM_EPS = 1e-08
ADAM_WD = 0.01
ADAM_STEP = 10
N_DEV = 8
V7X_VMEM_LIMIT_BYTES = 48 * 1024 * 1024
LRU_CB = 256
MESH = pl.DeviceIdType.MESH
ANY = pl.BlockSpec(memory_space=pl.ANY)


def _cparams(n_axes):
    return pltpu.CompilerParams(
        dimension_semantics=("arbitrary",) * n_axes, vmem_limit_bytes=V7X_VMEM_LIMIT_BYTES
    )


def _tile(n, pref):
    t = min(n, pref)
    assert n % t == 0, (n, pref)
    return t


def _dot_nn(a, b):
    return lax.dot_general(a, b, (((1,), (0,)), ((), ())), preferred_element_type=F32)


def _dot_nt(a, b):
    return lax.dot_general(a, b, (((1,), (1,)), ((), ())), preferred_element_type=F32)


def _dot_tn(a, b):
    return lax.dot_general(a, b, (((0,), (0,)), ((), ())), preferred_element_type=F32)


def _sig(x):
    return 1.0 / (1.0 + jnp.exp(-x))


def _rms_hat(x):
    r = lax.rsqrt(jnp.mean(x * x, axis=-1, keepdims=True) + NORM_EPS)
    return x * r, r


def _rms_bwd(dn, xhat, r, g):
    q = dn * g
    dx = r * (q - xhat * jnp.mean(q * xhat, axis=-1, keepdims=True))
    dg = jnp.sum(dn * xhat, axis=0, keepdims=True)
    return dx, dg


_GELU_K = math.sqrt(2.0 / math.pi)
_GELU_C = 0.044715


def _gelu_and_grad(g):
    t = jnp.tanh(_GELU_K * (g + _GELU_C * g * g * g))
    val = 0.5 * g * (1.0 + t)
    grad = 0.5 * (1.0 + t) + 0.5 * g * (1.0 - t * t) * (_GELU_K * (1.0 + 3.0 * _GELU_C * g * g))
    return val, grad


def _softplus_neg(lam):
    z = -lam
    e = jnp.exp(-jnp.abs(z))
    u = 1.0 + e
    d = u - 1.0
    l1p = jnp.where(d == 0.0, e, jnp.log(u) * (e / jnp.where(d == 0.0, 1.0, d)))
    return jnp.maximum(z, 0.0) + l1p


def _lru_gates(xc, wa, ba, wx, bx, lam):
    xcb = xc.astype(BF16)
    r = _sig(_dot_nn(xcb, wa) + ba)
    i = _sig(_dot_nn(xcb, wx) + bx)
    sp = _softplus_neg(lam)
    log_a = (-LRU_C) * r * sp
    a = jnp.exp(log_a)
    mult = jnp.sqrt(-jnp.tanh(log_a) * (1.0 + a * a))
    return xcb, r, i, sp, log_a, a, mult


def _norm_proj(x, g1, w_int):
    t, d = x.shape
    n = w_int.shape[0]
    tt, tn = _tile(t, 512), _tile(n, 512)

    def body(x_ref, g_ref, w_ref, proj_ref, h1_ref, h1_s):
        @pl.when(pl.program_id(1) == 0)
        def _():
            xhat, _ = _rms_hat(x_ref[...])
            h = (xhat * g_ref[...]).astype(BF16)
            h1_s[...] = h
            h1_ref[...] = h

        proj_ref[...] = _dot_nt(h1_s[...], w_ref[...])

    return pl.pallas_call(
        body,
        name="norm_proj",
        grid=(t // tt, n // tn),
        in_specs=[
            pl.BlockSpec((tt, d), lambda i, j: (i, 0)),
            pl.BlockSpec((1, d), lambda i, j: (0, 0)),
            pl.BlockSpec((tn, d), lambda i, j: (j, 0)),
        ],
        out_specs=[
            pl.BlockSpec((tt, tn), lambda i, j: (i, j)),
            pl.BlockSpec((tt, d), lambda i, j: (i, 0)),
        ],
        out_shape=[jax.ShapeDtypeStruct((t, n), F32), jax.ShapeDtypeStruct((t, d), BF16)],
        scratch_shapes=[pltpu.VMEM((tt, d), BF16)],
        compiler_params=_cparams(2),
    )(x, g1, w_int)


def _lru_fwd(proj, conv_w, conv_b, wa_bd, b_a, wx_bd, b_x, lam):
    t = proj.shape[0]
    dr = conv_b.shape[1]
    cb = LRU_CB
    tc = _tile(t, 256)
    ncb, ntc = dr // cb, t // tc

    def body(xp_ref, g_ref, cw_ref, cb_ref, wa_ref, ba_ref, wx_ref, bx_ref, lam_ref,
             y_ref, h_ref, xc_ref, prevx_s, hlast_s):
        c = pl.program_id(1)

        @pl.when(c == 0)
        def _():
            prevx_s[...] = jnp.zeros_like(prevx_s)
            hlast_s[...] = jnp.zeros_like(hlast_s)

        x = xp_ref[...]
        prev = prevx_s[...]
        row = lax.broadcasted_iota(jnp.int32, x.shape, 0)

        def sh(j):
            return jnp.where(row >= j, pltpu.roll(x, j, 0), pltpu.roll(prev, j, 0))

        xc = (cb_ref[...] + cw_ref[0:1, :] * sh(3) + cw_ref[1:2, :] * sh(2)
              + cw_ref[2:3, :] * sh(1) + cw_ref[3:4, :] * x)
        prevx_s[...] = x
        xc_ref[...] = xc
        _, _, i, _, _, a, mult = _lru_gates(xc, wa_ref[...], ba_ref[...], wx_ref[...], bx_ref[...],
                                            lam_ref[...])
        av, bv = a, mult * (i * xc)
        s = 1
        while s < tc:
            a_sh = jnp.where(row >= s, pltpu.roll(av, s, 0), 1.0)
            b_sh = jnp.where(row >= s, pltpu.roll(bv, s, 0), 0.0)
            bv = av * b_sh + bv
            av = av * a_sh
            s *= 2
        h = av * hlast_s[...] + bv
        h_ref[...] = h
        hlast_s[...] = h_ref[tc - 1:tc, :]
        gel, _ = _gelu_and_grad(g_ref[...])
        y_ref[...] = (h * gel).astype(BF16)

    vec = pl.BlockSpec((1, cb), lambda j, c: (0, j))
    blk = pl.BlockSpec((tc, cb), lambda j, c: (c, j))
    return pl.pallas_call(
        body,
        name="lru_fwd",
        grid=(ncb, ntc),
        in_specs=[
            pl.BlockSpec((tc, cb), lambda j, c: (c, j)),
            pl.BlockSpec((tc, cb), lambda j, c: (c, ncb + j)),
            pl.BlockSpec((4, cb), lambda j, c: (0, j)),
            vec,
            pl.BlockSpec((None, cb, cb), lambda j, c: (j, 0, 0)),
            vec,
            pl.BlockSpec((None, cb, cb), lambda j, c: (j, 0, 0)),
            vec,
            vec,
        ],
        out_specs=[blk, blk, blk],
        out_shape=[
            jax.ShapeDtypeStruct((t, dr), BF16),
            jax.ShapeDtypeStruct((t, dr), F32),
            jax.ShapeDtypeStruct((t, dr), F32),
        ],
        scratch_shapes=[pltpu.VMEM((tc, cb), F32), pltpu.VMEM((1, cb), F32)],
        compiler_params=_cparams(2),
    )(proj, proj, conv_w, conv_b, wa_bd, b_a, wx_bd, b_x, lam)


def _pool_select(col, vals):
    out = vals[3]
    for g in (2, 1, 0):
        out = jnp.where(col < (g + 1) * POOL_GROUP_DIM, vals[g], out)
    return out


def _pool_fwd(proj, pool_w, pool_scale, col_block):
    t = proj.shape[0]
    dp = pool_scale.shape[1]
    tc = _tile(t, 256)
    ntc = t // tc

    def body(x_ref, w_ref, sc_ref, y_ref, p_ref, px, p2, p4, p8):
        c = pl.program_id(0)

        @pl.when(c == 0)
        def _():
            for s in (px, p2, p4, p8):
                s[...] = jnp.zeros_like(s)

        x = x_ref[...]
        row = lax.broadcasted_iota(jnp.int32, x.shape, 0)
        col = lax.broadcasted_iota(jnp.int32, x.shape, 1)

        def sh(v, pv, j):
            return jnp.where(row >= j, pltpu.roll(v, j, 0), pltpu.roll(pv[...], j, 0))

        s2 = x + sh(x, px, 1)
        s4 = s2 + sh(s2, p2, 2)
        s8 = s4 + sh(s4, p4, 4)
        s16 = s8 + sh(s8, p8, 8)
        px[...] = x
        p2[...] = s2
        p4[...] = s4
        p8[...] = s8
        wsum = _pool_select(col, (s2, s4, s8, s16))
        win = _pool_select(col, POOL_WINDOWS)
        cnt = jnp.minimum(c * tc + row + 1, win).astype(F32)
        p = wsum / cnt - x
        pb = p.astype(BF16)
        p_ref[...] = pb
        for g in range(len(POOL_WINDOWS)):
            sl = slice(g * POOL_GROUP_DIM, (g + 1) * POOL_GROUP_DIM)
            yg = _dot_nn(pb[:, sl], w_ref[g]) * sc_ref[:, sl]
            y_ref[:, sl] = yg.astype(BF16)

    return pl.pallas_call(
        body,
        name="pool_fwd",
        grid=(ntc,),
        in_specs=[
            pl.BlockSpec((tc, dp), lambda c: (c, col_block)),
            pl.BlockSpec(pool_w.shape, lambda c: (0, 0, 0)),
            pl.BlockSpec((1, dp), lambda c: (0, 0)),
        ],
        out_specs=[pl.BlockSpec((tc, dp), lambda c: (c, 0))] * 2,
        out_shape=[jax.ShapeDtypeStruct((t, dp), BF16)] * 2,
        scratch_shapes=[pltpu.VMEM((tc, dp), F32)] * 4,
        compiler_params=_cparams(1),
    )(proj, pool_w, pool_scale)


def _branch_mix(y_lru, y_pool, w_lru_up, w_pool_upt, proj, b_gate, ga_block, gb_block):
    t, d = y_lru.shape
    dp = y_pool.shape[1]
    tt, tn = _tile(t, 512), 512
    nj = d // tn

    def body(yl_ref, yp_ref, wl_ref, wp_ref, ga_ref, gb_ref, ba_ref, bb_ref, bra_ref, brb_ref, mix_ref):
        br_a = _dot_nn(yl_ref[...], wl_ref[...])
        br_b = _dot_nt(yp_ref[...], wp_ref[...])
        bra_ref[...] = br_a
        brb_ref[...] = br_b
        ga = _sig(ga_ref[...] + ba_ref[...])
        gb = _sig(gb_ref[...] + bb_ref[...])
        mix_ref[...] = (ga * br_a + gb * br_b).astype(BF16)

    out = pl.BlockSpec((tt, tn), lambda j, i: (i, j))
    return pl.pallas_call(
        body,
        name="branch_mix",
        grid=(nj, t // tt),
        in_specs=[
            pl.BlockSpec((tt, d), lambda j, i: (i, 0)),
            pl.BlockSpec((tt, dp), lambda j, i: (i, 0)),
            pl.BlockSpec((d, tn), lambda j, i: (0, j)),
            pl.BlockSpec((tn, dp), lambda j, i: (j, 0)),
            pl.BlockSpec((tt, tn), lambda j, i: (i, ga_block + j)),
            pl.BlockSpec((tt, tn), lambda j, i: (i, gb_block + j)),
            pl.BlockSpec((1, tn), lambda j, i: (0, j)),
            pl.BlockSpec((1, tn), lambda j, i: (0, nj + j)),
        ],
        out_specs=[out, out, out],
        out_shape=[
            jax.ShapeDtypeStruct((t, d), F32),
            jax.ShapeDtypeStruct((t, d), F32),
            jax.ShapeDtypeStruct((t, d), BF16),
        ],
        compiler_params=_cparams(2),
    )(y_lru, y_pool, w_lru_up, w_pool_upt, proj, proj, b_gate, b_gate)


def _wo_norm(mix, w_o, x, g2, g3):
    t, d = x.shape
    tt = _tile(t, 256)

    def body(mix_ref, w_ref, x_ref, g2_ref, g3_ref, m_ref, x2_ref, h3_ref):
        m = _dot_nn(mix_ref[...], w_ref[...])
        m_ref[...] = m
        mhat, _ = _rms_hat(m)
        x2 = x_ref[...] + mhat * g2_ref[...]
        x2_ref[...] = x2
        xhat, _ = _rms_hat(x2)
        h3_ref[...] = (xhat * g3_ref[...]).astype(BF16)

    row = pl.BlockSpec((tt, d), lambda i: (i, 0))
    vec = pl.BlockSpec((1, d), lambda i: (0, 0))
    return pl.pallas_call(
        body,
        name="wo_norm",
        grid=(t // tt,),
        in_specs=[row, pl.BlockSpec((d, d), lambda i: (0, 0)), row, vec, vec],
        out_specs=[row, row, row],
        out_shape=[
            jax.ShapeDtypeStruct((t, d), F32),
            jax.ShapeDtypeStruct((t, d), F32),
            jax.ShapeDtypeStruct((t, d), BF16),
        ],
        compiler_params=_cparams(1),
    )(mix, w_o, x, g2, g3)


def _ff1(h3, w_ff1t):
    t, d = h3.shape
    n = w_ff1t.shape[0]
    tt, tn = _tile(t, 512), _tile(n, 512)

    def body(h_ref, w_ref, rf_ref, act_ref):
        rf = jnp.maximum(_dot_nt(h_ref[...], w_ref[...]), 0.0)
        rf_ref[...] = rf.astype(BF16)
        act_ref[...] = (rf * rf).astype(BF16)

    out = pl.BlockSpec((tt, tn), lambda i, j: (i, j))
    return pl.pallas_call(
        body,
        name="ff1",
        grid=(t // tt, n // tn),
        in_specs=[pl.BlockSpec((tt, d), lambda i, j: (i, 0)), pl.BlockSpec((tn, d), lambda i, j: (j, 0))],
        out_specs=[out, out],
        out_shape=[jax.ShapeDtypeStruct((t, n), BF16)] * 2,
        compiler_params=_cparams(2),
    )(h3, w_ff1t)


def _ff2_loss(act, w_ff2, x2, g4, target):
    t, k = act.shape
    d = x2.shape[1]
    tt, tk = _tile(t, 256), _tile(k, 512)
    nk = k // tk

    def body(a_ref, w_ref, x2_ref, g_ref, tg_ref, dy_ref, df_ref, dg_ref, loss_ref, acc):
        i, kk = pl.program_id(0), pl.program_id(1)

        @pl.when(kk == 0)
        def _():
            acc[...] = jnp.zeros_like(acc)

        @pl.when((i == 0) & (kk == 0))
        def _():
            dg_ref[...] = jnp.zeros_like(dg_ref)
            loss_ref[...] = jnp.zeros_like(loss_ref)

        acc[...] += _dot_nn(a_ref[...], w_ref[...])

        @pl.when(kk == nk - 1)
        def _():
            fhat, r = _rms_hat(acc[...])
            g = g_ref[...]
            e = x2_ref[...] + fhat * g - tg_ref[...]
            loss_ref[...] += 0.5 * jnp.sum(jnp.mean(e * e, axis=-1, keepdims=True))
            dy = e * (1.0 / d)
            dy_ref[...] = dy
            df, dg = _rms_bwd(dy, fhat, r, g)
            df_ref[...] = df.astype(BF16)
            dg_ref[...] += dg

    row = pl.BlockSpec((tt, d), lambda i, kk: (i, 0))
    vec = pl.BlockSpec((1, d), lambda i, kk: (0, 0))
    return pl.pallas_call(
        body,
        name="ff2_loss",
        grid=(t // tt, nk),
        in_specs=[
            pl.BlockSpec((tt, tk), lambda i, kk: (i, kk)),
            pl.BlockSpec((tk, d), lambda i, kk: (kk, 0)),
            row, vec, row,
        ],
        out_specs=[row, row, vec, pl.BlockSpec((1, 128), lambda i, kk: (0, 0))],
        out_shape=[
            jax.ShapeDtypeStruct((t, d), F32),
            jax.ShapeDtypeStruct((t, d), BF16),
            jax.ShapeDtypeStruct((1, d), F32),
            jax.ShapeDtypeStruct((1, 128), F32),
        ],
        scratch_shapes=[pltpu.VMEM((tt, d), F32)],
        compiler_params=_cparams(2),
    )(act, w_ff2, x2, g4, target)


def _ff2_bwd(df, w_ff2, rf):
    t, d = df.shape
    n = w_ff2.shape[0]
    tt, tn = _tile(t, 512), _tile(n, 512)

    def body(df_ref, w_ref, rf_ref, out_ref):
        d_act = _dot_nt(df_ref[...], w_ref[...])
        out_ref[...] = (d_act * (2.0 * rf_ref[...].astype(F32))).astype(BF16)

    blk = pl.BlockSpec((tt, tn), lambda i, j: (i, j))
    return pl.pallas_call(
        body,
        name="ff2_bwd",
        grid=(t // tt, n // tn),
        in_specs=[pl.BlockSpec((tt, d), lambda i, j: (i, 0)), pl.BlockSpec((tn, d), lambda i, j: (j, 0)), blk],
        out_specs=blk,
        out_shape=jax.ShapeDtypeStruct((t, n), BF16),
        compiler_params=_cparams(2),
    )(df, w_ff2, rf)


def _wgrad(a, b, name, prev=None, row_off=0, rows=None):
    t, m = a.shape
    n = b.shape[1]
    rows = m if rows is None else rows
    tm, tk = _tile(m, 512), _tile(t, 512)
    nk = t // tk
    assert row_off % tm == 0
    off = row_off // tm

    def body(*refs):
        a_ref, b_ref = refs[0], refs[1]
        o32_ref, o16_ref, acc = refs[-3], refs[-2], refs[-1]
        kk = pl.program_id(1)

        @pl.when(kk == 0)
        def _():
            acc[...] = jnp.zeros_like(acc)

        acc[...] += _dot_tn(a_ref[...], b_ref[...])

        @pl.when(kk == nk - 1)
        def _():
            o32_ref[...] = acc[...]
            o16_ref[...] = acc[...].astype(BF16)

    in_specs = [pl.BlockSpec((tk, tm), lambda i, kk: (kk, i)), pl.BlockSpec((tk, n), lambda i, kk: (kk, 0))]
    args = [a, b]
    aliases = {}
    if prev is not None:
        in_specs += [ANY, ANY]
        args += list(prev)
        aliases = {2: 0, 3: 1}
    out = pl.BlockSpec((tm, n), lambda i, kk: (off + i, 0))
    return pl.pallas_call(
        body,
        name=name,
        grid=(m // tm, nk),
        in_specs=in_specs,
        out_specs=[out, out],
        out_shape=[jax.ShapeDtypeStruct((rows, n), F32), jax.ShapeDtypeStruct((rows, n), BF16)],
        scratch_shapes=[pltpu.VMEM((tm, n), F32)],
        input_output_aliases=aliases,
        compiler_params=_cparams(2),
    )(*args)


def _ff1_bwd_norms(d_f1, w_ff1t, dy, x2, g3, m, g2):
    t, k = d_f1.shape
    d = x2.shape[1]
    tt, tk = _tile(t, 256), _tile(k, 512)
    nk = k // tk

    def body(a_ref, w_ref, dy_ref, x2_ref, g3_ref, m_ref, g2_ref, dx2_ref, dm_ref, dg3_ref, dg2_ref, acc):
        i, kk = pl.program_id(0), pl.program_id(1)

        @pl.when(kk == 0)
        def _():
            acc[...] = jnp.zeros_like(acc)

        @pl.when((i == 0) & (kk == 0))
        def _():
            dg3_ref[...] = jnp.zeros_like(dg3_ref)
            dg2_ref[...] = jnp.zeros_like(dg2_ref)

        acc[...] += _dot_nn(a_ref[...], w_ref[...])

        @pl.when(kk == nk - 1)
        def _():
            xhat, r3 = _rms_hat(x2_ref[...])
            dx, dg3 = _rms_bwd(acc[...], xhat, r3, g3_ref[...])
            dx2 = dy_ref[...] + dx
            dx2_ref[...] = dx2
            dg3_ref[...] += dg3
            mhat, r2 = _rms_hat(m_ref[...])
            dm, dg2 = _rms_bwd(dx2, mhat, r2, g2_ref[...])
            dm_ref[...] = dm.astype(BF16)
            dg2_ref[...] += dg2

    row = pl.BlockSpec((tt, d), lambda i, kk: (i, 0))
    vec = pl.BlockSpec((1, d), lambda i, kk: (0, 0))
    return pl.pallas_call(
        body,
        name="ff1_bwd_norms",
        grid=(t // tt, nk),
        in_specs=[
            pl.BlockSpec((tt, tk), lambda i, kk: (i, kk)),
            pl.BlockSpec((tk, d), lambda i, kk: (kk, 0)),
            row, row, vec, row, vec,
        ],
        out_specs=[row, row, vec, vec],
        out_shape=[
            jax.ShapeDtypeStruct((t, d), F32),
            jax.ShapeDtypeStruct((t, d), BF16),
            jax.ShapeDtypeStruct((1, d), F32),
            jax.ShapeDtypeStruct((1, d), F32),
        ],
        scratch_shapes=[pltpu.VMEM((tt, d), F32)],
        compiler_params=_cparams(2),
    )(d_f1, w_ff1t, dy, x2, g3, m, g2)


def _wo_bwd_mix(dm, w_o, br_a, br_b, proj, b_gate, ga_block, gb_block):
    t, d = dm.shape
    tt, tn = _tile(t, 512), 512
    nj = d // tn

    def body(dm_ref, w_ref, bra_ref, brb_ref, ga_ref, gb_ref, ba_ref, bb_ref,
             dbra_ref, dbrb_ref, dga_ref, dgb_ref, dba_ref, dbb_ref):
        i = pl.program_id(1)

        @pl.when(i == 0)
        def _():
            dba_ref[...] = jnp.zeros_like(dba_ref)
            dbb_ref[...] = jnp.zeros_like(dbb_ref)

        d_mix = _dot_nt(dm_ref[...], w_ref[...])
        ga = _sig(ga_ref[...] + ba_ref[...])
        gb = _sig(gb_ref[...] + bb_ref[...])
        dbra_ref[...] = (d_mix * ga).astype(BF16)
        dbrb_ref[...] = (d_mix * gb).astype(BF16)
        dga = d_mix * bra_ref[...] * (ga * (1.0 - ga))
        dgb = d_mix * brb_ref[...] * (gb * (1.0 - gb))
        dga_ref[...] = dga.astype(BF16)
        dgb_ref[...] = dgb.astype(BF16)
        dba_ref[...] += jnp.sum(dga, axis=0, keepdims=True)
        dbb_ref[...] += jnp.sum(dgb, axis=0, keepdims=True)

    blk = pl.BlockSpec((tt, tn), lambda j, i: (i, j))
    vec = pl.BlockSpec((1, tn), lambda j, i: (0, j))
    return pl.pallas_call(
        body,
        name="wo_bwd_mix",
        grid=(nj, t // tt),
        in_specs=[
            pl.BlockSpec((tt, d), lambda j, i: (i, 0)),
            pl.BlockSpec((tn, d), lambda j, i: (j, 0)),
            blk, blk,
            pl.BlockSpec((tt, tn), lambda j, i: (i, ga_block + j)),
            pl.BlockSpec((tt, tn), lambda j, i: (i, gb_block + j)),
            vec,
            pl.BlockSpec((1, tn), lambda j, i: (0, nj + j)),
        ],
        out_specs=[blk, blk, blk, blk, vec, vec],
        out_shape=[jax.ShapeDtypeStruct((t, d), BF16)] * 4 + [jax.ShapeDtypeStruct((1, d), F32)] * 2,
        compiler_params=_cparams(2),
    )(dm, w_o, br_a, br_b, proj, proj, b_gate, b_gate)


def _lru_up_bwd(d_br_a, w_lru_up, proj, h, g_block):
    t, d = d_br_a.shape
    tt, tn = _tile(t, 512), 512

    def body(a_ref, w_ref, g_ref, h_ref, dh_ref, dg_ref):
        d_y = _dot_nt(a_ref[...], w_ref[...])
        gel, gel_grad = _gelu_and_grad(g_ref[...])
        dh_ref[...] = d_y * gel
        dg_ref[...] = (d_y * h_ref[...] * gel_grad).astype(BF16)

    blk = pl.BlockSpec((tt, tn), lambda i, j: (i, j))
    return pl.pallas_call(
        body,
        name="lru_up_bwd",
        grid=(t // tt, d // tn),
        in_specs=[
            pl.BlockSpec((tt, d), lambda i, j: (i, 0)),
            pl.BlockSpec((tn, d), lambda i, j: (j, 0)),
            pl.BlockSpec((tt, tn), lambda i, j: (i, g_block + j)),
            blk,
        ],
        out_specs=[blk, blk],
        out_shape=[jax.ShapeDtypeStruct((t, d), F32), jax.ShapeDtypeStruct((t, d), BF16)],
        compiler_params=_cparams(2),
    )(d_br_a, w_lru_up, proj, h)


def _pool_up_bwd(d_br_b, w_pool_upt):
    t, d = d_br_b.shape
    dp = w_pool_upt.shape[1]
    tt = _tile(t, 512)

    def body(a_ref, w_ref, out_ref):
        out_ref[...] = _dot_nn(a_ref[...], w_ref[...])

    return pl.pallas_call(
        body,
        name="pool_up_bwd",
        grid=(t // tt,),
        in_specs=[pl.BlockSpec((tt, d), lambda i: (i, 0)), pl.BlockSpec((d, dp), lambda i: (0, 0))],
        out_specs=pl.BlockSpec((tt, dp), lambda i: (i, 0)),
        out_shape=jax.ShapeDtypeStruct((t, dp), F32),
        compiler_params=_cparams(1),
    )(d_br_b, w_pool_upt)


def _lru_bwd(dh, xc, h, proj, conv_w, wa_bd, b_a, wx_bd, b_x, lam):
    t, dr = dh.shape
    cb = LRU_CB
    tc = _tile(t, 256)
    ncb, ntc = dr // cb, t // tc

    def body(dh_ref, xc_ref, h_ref, hp_ref, xp_ref, cw_ref, wa_ref, ba_ref, wx_ref, bx_ref, lam_ref,
             dxp_ref, dwa_ref, dba_ref, dwx_ref, dbx_ref, dlam_ref, dcw_ref, dcb_ref,
             nextd_s, anext_s, gnext_s, tmp_s):
        c = pl.program_id(1)
        rc = ntc - 1 - c

        @pl.when(c == 0)
        def _():
            nextd_s[...] = jnp.zeros_like(nextd_s)
            anext_s[...] = jnp.zeros_like(anext_s)
            gnext_s[...] = jnp.zeros_like(gnext_s)
            for ref in (dwa_ref, dba_ref, dwx_ref, dbx_ref, dlam_ref, dcw_ref, dcb_ref):
                ref[...] = jnp.zeros_like(ref)

        xc = xc_ref[...]
        wa, wx, lam = wa_ref[...], wx_ref[...], lam_ref[...]
        xcb, r, i, sp, log_a, a, mult = _lru_gates(xc, wa, ba_ref[...], wx, bx_ref[...], lam)
        row = lax.broadcasted_iota(jnp.int32, xc.shape, 0)
        h = h_ref[...]
        hp = jnp.where(rc == 0, 0.0, hp_ref[...])
        hprev = jnp.where(row >= 1, pltpu.roll(h, 1, 0), pltpu.roll(hp, 1, 0))

        def up(v, nv, j):
            return jnp.where(row < tc - j, pltpu.roll(v, tc - j, 0), nv)

        av = up(a, anext_s[...], 1)
        bv = dh_ref[...]
        s = 1
        while s < tc:
            a_sh = up(av, 1.0, s)
            b_sh = up(bv, 0.0, s)
            bv = av * b_sh + bv
            av = av * a_sh
            s *= 2
        gt = av * gnext_s[...] + bv
        tmp_s[...] = gt
        gnext_s[...] = tmp_s[0:1, :]
        tmp_s[...] = a
        anext_s[...] = tmp_s[0:1, :]

        da = gt * hprev
        ixc = i * xc
        d_mult = gt * ixc
        d_i = gt * mult * xc
        d_xc = gt * mult * i
        d_log_a = da * a - d_mult * (a * a) / mult
        d_pre_r = (d_log_a * ((-LRU_C) * sp)) * (r * (1.0 - r))
        d_pre_i = d_i * (i * (1.0 - i))
        d_sp = jnp.sum(d_log_a * ((-LRU_C) * r), axis=0, keepdims=True)
        dlam_ref[...] += d_sp * (-1.0 / (1.0 + jnp.exp(lam)))
        dpr = d_pre_r.astype(BF16)
        dpi = d_pre_i.astype(BF16)
        dba_ref[...] += jnp.sum(d_pre_r, axis=0, keepdims=True)
        dbx_ref[...] += jnp.sum(d_pre_i, axis=0, keepdims=True)
        dwa_ref[...] += _dot_tn(xcb, dpr)
        dwx_ref[...] += _dot_tn(xcb, dpi)
        d_xc = d_xc + _dot_nt(dpr, wa) + _dot_nt(dpi, wx)

        nxt = nextd_s[...]
        xp = xp_ref[...]
        dxp = cw_ref[3:4, :] * d_xc
        dcw_ref[3:4, :] += jnp.sum(xp * d_xc, axis=0, keepdims=True)
        for j in (1, 2, 3):
            uj = up(d_xc, pltpu.roll(nxt, tc - j, 0), j)
            dxp = dxp + cw_ref[3 - j:4 - j, :] * uj
            dcw_ref[3 - j:4 - j, :] += jnp.sum(xp * uj, axis=0, keepdims=True)
        dcb_ref[...] += jnp.sum(d_xc, axis=0, keepdims=True)
        nextd_s[...] = d_xc
        dxp_ref[...] = dxp.astype(BF16)

    vec = pl.BlockSpec((1, cb), lambda j, c: (0, j))
    blk = pl.BlockSpec((tc, cb), lambda j, c: (ntc - 1 - c, j))
    mat = pl.BlockSpec((None, cb, cb), lambda j, c: (j, 0, 0))
    cwb = pl.BlockSpec((4, cb), lambda j, c: (0, j))
    return pl.pallas_call(
        body,
        name="lru_bwd",
        grid=(ncb, ntc),
        in_specs=[
            blk, blk, blk,
            pl.BlockSpec((tc, cb), lambda j, c: (jnp.maximum(ntc - 2 - c, 0), j)),
            blk, cwb, mat, vec, mat, vec, vec,
        ],
        out_specs=[blk, mat, vec, mat, vec, vec, cwb, vec],
        out_shape=[
            jax.ShapeDtypeStruct((t, dr), BF16),
            jax.ShapeDtypeStruct((ncb, cb, cb), F32),
            jax.ShapeDtypeStruct((1, dr), F32),
            jax.ShapeDtypeStruct((ncb, cb, cb), F32),
            jax.ShapeDtypeStruct((1, dr), F32),
            jax.ShapeDtypeStruct((1, dr), F32),
            jax.ShapeDtypeStruct((4, dr), F32),
            jax.ShapeDtypeStruct((1, dr), F32),
        ],
        scratch_shapes=[
            pltpu.VMEM((tc, cb), F32),
            pltpu.VMEM((1, cb), F32),
            pltpu.VMEM((1, cb), F32),
            pltpu.VMEM((tc, cb), F32),
        ],
        compiler_params=_cparams(2),
    )(dh, xc, h, h, proj, conv_w, wa_bd, b_a, wx_bd, b_x, lam)


def _pool_bwd(d_y_pool, p, pool_w, pool_scale):
    t, dp = d_y_pool.shape
    tc = _tile(t, 256)
    ntc = t // tc
    ng = len(POOL_WINDOWS)

    def body(dy_ref, p_ref, w_ref, sc_ref, dx_ref, dw_ref, dsc_ref, nz, n2, n4, n8, dp_s):
        c = pl.program_id(0)
        rc = ntc - 1 - c

        @pl.when(c == 0)
        def _():
            for s in (nz, n2, n4, n8):
                s[...] = jnp.zeros_like(s)
            dw_ref[...] = jnp.zeros_like(dw_ref)
            dsc_ref[...] = jnp.zeros_like(dsc_ref)

        for g in range(ng):
            sl = slice(g * POOL_GROUP_DIM, (g + 1) * POOL_GROUP_DIM)
            pg = p_ref[:, sl]
            dyg = dy_ref[:, sl]
            wg = w_ref[g].astype(BF16)
            q = _dot_nn(pg, wg)
            dsc_ref[:, sl] += jnp.sum(dyg * q, axis=0, keepdims=True)
            dpw = (dyg * sc_ref[:, sl]).astype(BF16)
            dw_ref[g] += _dot_tn(pg, dpw)
            dp_s[:, sl] = _dot_nt(dpw, wg)

        dpv = dp_s[...]
        row = lax.broadcasted_iota(jnp.int32, dpv.shape, 0)
        col = lax.broadcasted_iota(jnp.int32, dpv.shape, 1)
        win = _pool_select(col, POOL_WINDOWS)
        cnt = jnp.minimum(rc * tc + row + 1, win).astype(F32)
        z = dpv / cnt

        def up(v, nv, j):
            return jnp.where(row < tc - j, pltpu.roll(v, tc - j, 0), pltpu.roll(nv[...], tc - j, 0))

        u2 = z + up(z, nz, 1)
        u4 = u2 + up(u2, n2, 2)
        u8 = u4 + up(u4, n4, 4)
        u16 = u8 + up(u8, n8, 8)
        nz[...] = z
        n2[...] = u2
        n4[...] = u4
        n8[...] = u8
        dx_ref[...] = (_pool_select(col, (u2, u4, u8, u16)) - dpv).astype(BF16)

    blk = pl.BlockSpec((tc, dp), lambda c: (ntc - 1 - c, 0))
    return pl.pallas_call(
        body,
        name="pool_bwd",
        grid=(ntc,),
        in_specs=[blk, blk, pl.BlockSpec(pool_w.shape, lambda c: (0, 0, 0)), pl.BlockSpec((1, dp), lambda c: (0, 0))],
        out_specs=[blk, pl.BlockSpec(pool_w.shape, lambda c: (0, 0, 0)), pl.BlockSpec((1, dp), lambda c: (0, 0))],
        out_shape=[
            jax.ShapeDtypeStruct((t, dp), BF16),
            jax.ShapeDtypeStruct(pool_w.shape, F32),
            jax.ShapeDtypeStruct((1, dp), F32),
        ],
        scratch_shapes=[pltpu.VMEM((tc, dp), F32)] * 5,
        compiler_params=_cparams(1),
    )(d_y_pool, p, pool_w, pool_scale)


def _win_bwd_norm(parts, w_int, dx2, x, g1):
    t, d = x.shape
    tk = 512
    tt = _tile(t, 256)
    bounds = []
    k0 = 0
    for part in parts:
        assert part.shape[1] % tk == 0
        bounds.append((k0, k0 + part.shape[1] // tk))
        k0 += part.shape[1] // tk
    nk = k0
    assert nk * tk == w_int.shape[0]
    np_ = len(parts)

    def body(*refs):
        p_refs = refs[:np_]
        w_ref, dx2_ref, x_ref, g_ref, gx_ref, dg_ref, acc = refs[np_:]
        i, kk = pl.program_id(0), pl.program_id(1)

        @pl.when(kk == 0)
        def _():
            acc[...] = jnp.zeros_like(acc)

        @pl.when((i == 0) & (kk == 0))
        def _():
            dg_ref[...] = jnp.zeros_like(dg_ref)

        for (lo, hi), p_ref in zip(bounds, p_refs):
            @pl.when((kk >= lo) & (kk < hi))
            def _(p_ref=p_ref):
                acc[...] += _dot_nn(p_ref[...], w_ref[...])

        @pl.when(kk == nk - 1)
        def _():
            xhat, r = _rms_hat(x_ref[...])
            dx, dg = _rms_bwd(acc[...], xhat, r, g_ref[...])
            gx_ref[...] = dx2_ref[...] + dx
            dg_ref[...] += dg

    def part_spec(lo, hi):
        return pl.BlockSpec((tt, tk), lambda i, kk: (i, jnp.clip(kk - lo, 0, hi - lo - 1)))

    row = pl.BlockSpec((tt, d), lambda i, kk: (i, 0))
    vec = pl.BlockSpec((1, d), lambda i, kk: (0, 0))
    return pl.pallas_call(
        body,
        name="win_bwd_norm",
        grid=(t // tt, nk),
        in_specs=[part_spec(lo, hi) for lo, hi in bounds]
        + [pl.BlockSpec((tk, d), lambda i, kk: (kk, 0)), row, row, vec],
        out_specs=[row, vec],
        out_shape=[jax.ShapeDtypeStruct((t, d), F32), jax.ShapeDtypeStruct((1, d), F32)],
        scratch_shapes=[pltpu.VMEM((tt, d), F32)],
        compiler_params=_cparams(2),
    )(*parts, w_int, dx2, x, g1)


def _adamw(w, g, m, v, name):
    rows, cols = w.shape
    tr = rows if rows <= 512 else _tile(rows, 256)

    def body(w_ref, g_ref, m_ref, v_ref, d_ref, nm_ref, nv_ref):
        g = g_ref[...]
        m = ADAM_B1 * m_ref[...] + (1.0 - ADAM_B1) * g
        v = ADAM_B2 * v_ref[...] + (1.0 - ADAM_B2) * (g * g)
        m_hat = m / (1.0 - ADAM_B1 ** ADAM_STEP)
        v_hat = v / (1.0 - ADAM_B2 ** ADAM_STEP)
        d_ref[...] = -ADAM_LR * (m_hat / (jnp.sqrt(v_hat) + ADAM_EPS) + ADAM_WD * w_ref[...])
        nm_ref[...] = m
        nv_ref[...] = v

    blk = pl.BlockSpec((tr, cols), lambda i: (i, 0))
    return pl.pallas_call(
        body,
        name=name,
        grid=(rows // tr,),
        in_specs=[blk] * 4,
        out_specs=[blk] * 3,
        out_shape=[jax.ShapeDtypeStruct((rows, cols), F32)] * 3,
        compiler_params=_cparams(1),
    )(w, g, m, v)


def _sum_slots(full, recv, shard_ids, slot_ids, out_dtype, name):
    n = shard_ids.shape[0]
    _, r, cols = recv.shape

    def body(sh_ref, sl_ref, full_ref, recv_ref, out_ref):
        out_ref[...] = (full_ref[...] + recv_ref[...].astype(F32)).astype(out_dtype)

    grid_spec = pltpu.PrefetchScalarGridSpec(
        num_scalar_prefetch=2,
        grid=(n,),
        in_specs=[
            pl.BlockSpec((r, cols), lambda s, sh, sl: (sh[s], 0)),
            pl.BlockSpec((None, r, cols), lambda s, sh, sl: (sl[s], 0, 0)),
        ],
        out_specs=pl.BlockSpec((None, r, cols), lambda s, sh, sl: (s, 0, 0)),
    )
    return pl.pallas_call(
        body,
        name=name,
        grid_spec=grid_spec,
        out_shape=jax.ShapeDtypeStruct((n, r, cols), out_dtype),
        compiler_params=_cparams(1),
    )(shard_ids, slot_ids, full, recv)


def _final_sum(own, recv, name):
    _, r, cols = own.shape
    tr = _tile(r, 64)

    def body(own_ref, recv_ref, out_ref):
        acc = own_ref[...]
        for k in range(3):
            acc = acc + recv_ref[k].astype(F32)
        out_ref[...] = acc

    return pl.pallas_call(
        body,
        name=name,
        grid=(r // tr,),
        in_specs=[
            pl.BlockSpec((None, tr, cols), lambda i: (0, i, 0)),
            pl.BlockSpec((3, tr, cols), lambda i: (0, i, 0)),
        ],
        out_specs=pl.BlockSpec((tr, cols), lambda i: (i, 0)),
        out_shape=jax.ShapeDtypeStruct((r, cols), F32),
        compiler_params=_cparams(1),
    )(own, recv)


def _place():
    return lax.axis_index("x"), lax.axis_index("y"), lax.axis_index("c")


def _all_gather(shards, name):
    na = len(shards)

    def body(*refs):
        ins, outs = refs[:na], refs[na:2 * na]
        send_sems, recv_sems, local_sems = refs[2 * na:]
        x, y, c = _place()
        me, sibling = (x, y, c), (x, y, 1 - c)
        chips = [(1 - x, y), (x, 1 - y), (1 - x, 1 - y)]

        def slot(a, px, py, pc):
            return outs[a].at[4 * px + 2 * py + pc]

        def copy(a, k, block, to, src=None):
            return pltpu.make_async_remote_copy(
                src_ref=slot(a, *block) if src is None else src,
                dst_ref=slot(a, *block),
                send_sem=send_sems.at[a * 7 + k],
                recv_sem=recv_sems.at[a * 7 + k],
                device_id=to,
                device_id_type=MESH,
            )

        mine = [pltpu.make_async_copy(ins[a], slot(a, *me), local_sems.at[a]) for a in range(na)]
        for cp in mine:
            cp.start()
        first = []
        for a in range(na):
            first.append(copy(a, 0, me, sibling, src=ins[a]))
            first += [copy(a, 1 + j, me, (*chip, c), src=ins[a]) for j, chip in enumerate(chips)]
        for cp in first:
            cp.start()
        passed = []
        for j, chip in enumerate(chips):
            for a in range(na):
                copy(a, 1 + j, (*chip, c), me).wait_recv()
                fwd = copy(a, 4 + j, (*chip, c), sibling)
                fwd.start()
                passed.append(fwd)
        for a in range(na):
            copy(a, 0, sibling, me).wait_recv()
            for j, chip in enumerate(chips):
                copy(a, 4 + j, (*chip, 1 - c), me).wait_recv()
        for cp in first + passed:
            cp.wait_send()
        for cp in mine:
            cp.wait()

    return pl.pallas_call(
        body,
        name=name,
        in_specs=[ANY] * na,
        out_specs=[ANY] * na,
        out_shape=[jax.ShapeDtypeStruct((N_DEV,) + s.shape, s.dtype) for s in shards],
        scratch_shapes=[
            pltpu.SemaphoreType.DMA((7 * na,)),
            pltpu.SemaphoreType.DMA((7 * na,)),
            pltpu.SemaphoreType.DMA((na,)),
        ],
    )(*shards)


def _rs_sibling(fulls, name):
    na = len(fulls)
    rs = [f.shape[0] // N_DEV for f in fulls]

    def body(*refs):
        ins, outs = refs[:na], refs[na:2 * na]
        send_sems, recv_sems = refs[2 * na:]
        x, y, c = _place()
        copies = []
        for a in range(na):
            for q in range(4):
                shard = 2 * q + (1 - c)
                copies.append(pltpu.make_async_remote_copy(
                    src_ref=ins[a].at[pl.ds(shard * rs[a], rs[a]), :],
                    dst_ref=outs[a].at[q],
                    send_sem=send_sems.at[a * 4 + q],
                    recv_sem=recv_sems.at[a * 4 + q],
                    device_id=(x, y, 1 - c),
                    device_id_type=MESH,
                ))
        for cp in copies:
            cp.start()
        for cp in copies:
            cp.wait()

    return pl.pallas_call(
        body,
        name=name,
        in_specs=[ANY] * na,
        out_specs=[ANY] * na,
        out_shape=[jax.ShapeDtypeStruct((4, r, f.shape[1]), f.dtype) for r, f in zip(rs, fulls)],
        scratch_shapes=[pltpu.SemaphoreType.DMA((4 * na,)), pltpu.SemaphoreType.DMA((4 * na,))],
    )(*fulls)


def _rs_chips(sends, name):
    na = len(sends)

    def body(*refs):
        ins, outs = refs[:na], refs[na:2 * na]
        send_sems, recv_sems = refs[2 * na:]
        x, y, c = _place()
        chips = [(1 - x, y), (x, 1 - y), (1 - x, 1 - y)]
        copies = []
        for a in range(na):
            for k, chip in enumerate(chips):
                copies.append(pltpu.make_async_remote_copy(
                    src_ref=ins[a].at[k],
                    dst_ref=outs[a].at[k],
                    send_sem=send_sems.at[a * 3 + k],
                    recv_sem=recv_sems.at[a * 3 + k],
                    device_id=(*chip, c),
                    device_id_type=MESH,
                ))
        for cp in copies:
            cp.start()
        for cp in copies:
            cp.wait()

    return pl.pallas_call(
        body,
        name=name,
        in_specs=[ANY] * na,
        out_specs=[ANY] * na,
        out_shape=[jax.ShapeDtypeStruct(s.shape, s.dtype) for s in sends],
        scratch_shapes=[pltpu.SemaphoreType.DMA((3 * na,)), pltpu.SemaphoreType.DMA((3 * na,))],
    )(*sends)


def _reduce_scatter(fulls_f32, fulls_send, tag):
    x, y, c = _place()
    recv1 = _rs_sibling(fulls_send, "rs_sibling_" + tag)
    own_q = 2 * x + y
    peer_q = jnp.stack([2 * (1 - x) + y, 2 * x + (1 - y), 2 * (1 - x) + (1 - y)]).astype(jnp.int32)
    own_ids = jnp.reshape(own_q, (1,)).astype(jnp.int32)
    own, sends = [], []
    for a, (f32, r1) in enumerate(zip(fulls_f32, recv1)):
        own.append(_sum_slots(f32, r1, 2 * own_ids + c, own_ids, F32, f"rs_own_{tag}{a}"))
        sends.append(_sum_slots(f32, r1, 2 * peer_q + c, peer_q, r1.dtype, f"rs_peers_{tag}{a}"))
    recv2 = _rs_chips(sends, "rs_chips_" + tag)
    return [_final_sum(o, r2, f"rs_final_{tag}{a}") for a, (o, r2) in enumerate(zip(own, recv2))]


def _block_diag(w, per):
    h, hd, _ = w.shape
    out = jnp.zeros((h // per, per * hd, per * hd), w.dtype)
    for k in range(per):
        out = out.at[:, k * hd:(k + 1) * hd, k * hd:(k + 1) * hd].set(w[k::per])
    return out


def _block_diag_extract(w_bd, per, hd):
    g = w_bd.shape[0]
    blocks = [w_bd[:, k * hd:(k + 1) * hd, k * hd:(k + 1) * hd] for k in range(per)]
    return jnp.stack(blocks, axis=1).reshape(g * per, hd, hd)


def kernel(x, norm_mix_pre, norm_mix_post, norm_mlp_pre, norm_mlp_post, w_in, b_gate, conv_w, conv_b, lru_w_a, lru_b_a, lru_w_x, lru_b_x, lru_lambda, pool_w, pool_scale, w_lru_up, w_pool_up, w_o, w_ff1, w_ff2, loss_target, m_norm_mix_pre, m_norm_mix_post, m_norm_mlp_pre, m_norm_mlp_post, m_w_in, m_b_gate, m_conv_w, m_conv_b, m_lru_w_a, m_lru_b_a, m_lru_w_x, m_lru_b_x, m_lru_lambda, m_pool_w, m_pool_scale, m_w_lru_up, m_w_pool_up, m_w_o, m_w_ff1, m_w_ff2, v_norm_mix_pre, v_norm_mix_post, v_norm_mlp_pre, v_norm_mlp_post, v_w_in, v_b_gate, v_conv_w, v_conv_b, v_lru_w_a, v_lru_b_a, v_lru_w_x, v_lru_b_x, v_lru_lambda, v_pool_w, v_pool_scale, v_w_lru_up, v_w_pool_up, v_w_o, v_w_ff1, v_w_ff2):
    t, d = x.shape[1], x.shape[2]
    d_rnn = conv_b.shape[1]
    d_pool = pool_scale.shape[1]
    per = LRU_CB // LRU_HEAD_DIM
    xi, yi, ci = _place()
    me = 4 * xi + 2 * yi + ci

    x2d = x[0]
    tgt = loss_target[0]

    shards = [
        w_in[0].T.astype(BF16),
        w_lru_up[0].astype(BF16),
        w_pool_up[0].T.astype(BF16),
        w_o[0].astype(BF16),
        w_ff1[0].T.astype(BF16),
        w_ff2[0].astype(BF16),
        jnp.pad(conv_w[0], ((0, 4), (0, 0))),
    ]
    gathered = _all_gather(shards, "ag_weights")
    w_int, w_lu, w_put, w_og, w_f1t, w_f2 = [
        g.reshape(g.shape[0] * g.shape[1], g.shape[2]) for g in gathered[:6]
    ]
    conv_w_full = jnp.transpose(gathered[6][:, :4, :], (1, 0, 2)).reshape(4, d_rnn)

    wa_bd = _block_diag(lru_w_a[0], per).astype(BF16)
    wx_bd = _block_diag(lru_w_x[0], per).astype(BF16)
    pw = pool_w[0]
    pw_bf = pw.astype(BF16)

    pool_block = (2 * d_rnn) // d_pool
    ga_block = (2 * d_rnn + d_pool) // 512
    gb_block = ga_block + d // 512
    g_block = d_rnn // 512

    proj, h1 = _norm_proj(x2d, norm_mix_pre, w_int)
    y_lru, h, xc = _lru_fwd(proj, conv_w_full, conv_b, wa_bd, lru_b_a, wx_bd, lru_b_x, lru_lambda)
    y_pool, p = _pool_fwd(proj, pw_bf, pool_scale, pool_block)
    br_a, br_b, mix = _branch_mix(y_lru, y_pool, w_lu, w_put, proj, b_gate, ga_block, gb_block)
    m, x2, h3 = _wo_norm(mix, w_og, x2d, norm_mix_post, norm_mlp_pre)
    rf, act = _ff1(h3, w_f1t)
    dy, df, dg4, loss_part = _ff2_loss(act, w_f2, x2, norm_mlp_post, tgt)

    d_f1 = _ff2_bwd(df, w_f2, rf)
    gw_ff2 = _wgrad(act, df, "wgrad_ff2")
    gw_ff1t = _wgrad(d_f1, h3, "wgrad_ff1")
    dx2, dm, dg3, dg2 = _ff1_bwd_norms(d_f1, w_f1t, dy, x2, norm_mlp_pre, m, norm_mix_post)
    gw_o = _wgrad(mix, dm, "wgrad_o")
    d_br_a, d_br_b, p_ga, p_gb, dbg_a, dbg_b = _wo_bwd_mix(dm, w_og, br_a, br_b, proj, b_gate, ga_block, gb_block)
    gw_lu = _wgrad(y_lru, d_br_a, "wgrad_lru_up")
    gw_put = _wgrad(d_br_b, y_pool, "wgrad_pool_up")
    dh, p_g = _lru_up_bwd(d_br_a, w_lu, proj, h, g_block)
    d_y_pool = _pool_up_bwd(d_br_b, w_put)
    p_x, dwa_bd, db_a, dwx_bd, db_x, dlam, dconv_w, dconv_b = _lru_bwd(
        dh, xc, h, proj, conv_w_full, wa_bd, lru_b_a, wx_bd, lru_b_x, lru_lambda)
    p_p, dpool_w, dpool_scale = _pool_bwd(d_y_pool, p, pw, pool_scale)
    parts = [p_x, p_g, p_p, p_ga, p_gb]
    gw_int = None
    row_off = 0
    for k, part in enumerate(parts):
        gw_int = _wgrad(part, h1, f"wgrad_in{k}", prev=gw_int, row_off=row_off, rows=w_int.shape[0])
        row_off += part.shape[1]
    grad_x, dg1 = _win_bwd_norm(parts, w_int, dx2, x2d, norm_mix_pre)

    big = [gw_int, gw_lu, gw_put, gw_o, gw_ff1t, gw_ff2]
    big32 = [b[0] for b in big]
    big16 = [b[1] for b in big]
    big32[2] = big32[2].reshape(-1, d)
    big16[2] = big16[2].reshape(-1, d)
    red = _reduce_scatter(big32, big16, "big")
    g_w_in = red[0].T
    g_w_lru_up = red[1]
    g_w_pool_up = red[2].reshape(d // N_DEV, d_pool).T
    g_w_o = red[3]
    g_w_ff1 = red[4].T
    g_w_ff2 = red[5]

    dwa = _block_diag_extract(dwa_bd, per, LRU_HEAD_DIM)
    dwx = _block_diag_extract(dwx_bd, per, LRU_HEAD_DIM)
    small_names = ["norm_mix_pre", "norm_mix_post", "norm_mlp_pre", "norm_mlp_post", "b_gate", "conv_b",
                   "lru_w_a", "lru_b_a", "lru_w_x", "lru_b_x", "lru_lambda", "pool_w", "pool_scale"]
    small_grads = [dg1, dg2, dg3, dg4, jnp.concatenate([dbg_a, dbg_b], axis=1), dconv_b,
                   dwa, db_a, dwx, db_x, dlam, dpool_w, dpool_scale]

    def pack(arrs):
        flat = [a.reshape(-1) for a in arrs]
        flat = [jnp.pad(f, (0, (-f.shape[0]) % d)) for f in flat]
        rows = jnp.concatenate(flat).reshape(-1, d)
        pad_rows = (-rows.shape[0]) % (8 * N_DEV)
        return jnp.pad(rows, ((0, pad_rows), (0, 0)))

    small_shapes = [a.shape for a in small_grads]
    packed = pack(small_grads + [dconv_w])
    red_small = _reduce_scatter([packed], [packed], "small")[0]
    small_all = _all_gather([red_small], "ag_small")[0]
    small_all = small_all.reshape(-1, d)

    def unpack(rows, shapes):
        out, r0 = [], 0
        for shp in shapes:
            n = math.prod(shp)
            nr = -(-n // d)
            out.append(rows[r0:r0 + nr].reshape(-1)[:n].reshape(shp))
            r0 += nr
        return out

    unpacked = unpack(small_all, small_shapes + [dconv_w.shape])
    g_small = dict(zip(small_names, unpacked[:-1]))
    g_conv_w = lax.dynamic_slice_in_dim(unpacked[-1], me * (d_rnn // N_DEV), d_rnn // N_DEV, axis=1)

    args = dict(norm_mix_pre=norm_mix_pre, norm_mix_post=norm_mix_post, norm_mlp_pre=norm_mlp_pre,
                norm_mlp_post=norm_mlp_post, b_gate=b_gate, conv_b=conv_b, lru_w_a=lru_w_a, lru_b_a=lru_b_a,
                lru_w_x=lru_w_x, lru_b_x=lru_b_x, lru_lambda=lru_lambda, pool_w=pool_w, pool_scale=pool_scale)
    ms = dict(norm_mix_pre=m_norm_mix_pre, norm_mix_post=m_norm_mix_post, norm_mlp_pre=m_norm_mlp_pre,
              norm_mlp_post=m_norm_mlp_post, b_gate=m_b_gate, conv_b=m_conv_b, lru_w_a=m_lru_w_a,
              lru_b_a=m_lru_b_a, lru_w_x=m_lru_w_x, lru_b_x=m_lru_b_x, lru_lambda=m_lru_lambda,
              pool_w=m_pool_w, pool_scale=m_pool_scale)
    vs = dict(norm_mix_pre=v_norm_mix_pre, norm_mix_post=v_norm_mix_post, norm_mlp_pre=v_norm_mlp_pre,
              norm_mlp_post=v_norm_mlp_post, b_gate=v_b_gate, conv_b=v_conv_b, lru_w_a=v_lru_w_a,
              lru_b_a=v_lru_b_a, lru_w_x=v_lru_w_x, lru_b_x=v_lru_b_x, lru_lambda=v_lru_lambda,
              pool_w=v_pool_w, pool_scale=v_pool_scale)
    small_w = pack([args[n] for n in small_names])
    small_m = pack([ms[n] for n in small_names])
    small_v = pack([vs[n] for n in small_names])
    small_g = pack([g_small[n] for n in small_names])
    sd, snm, snv = _adamw(small_w, small_g, small_m, small_v, "adamw_small")
    full_shapes = [args[n].shape for n in small_names]
    delta = dict(zip(small_names, unpack(sd, full_shapes)))
    new_m = dict(zip(small_names, unpack(snm, full_shapes)))
    new_v = dict(zip(small_names, unpack(snv, full_shapes)))
    grads = {n: g_small[n].reshape(args[n].shape) for n in small_names}

    def big_adam(name, w, g, mm, vv):
        dl, nm, nv = _adamw(w[0], g, mm[0], vv[0], "adamw_" + name)
        grads[name], delta[name], new_m[name], new_v[name] = g[None], dl[None], nm[None], nv[None]

    big_adam("w_in", w_in, g_w_in, m_w_in, v_w_in)
    big_adam("conv_w", conv_w, g_conv_w, m_conv_w, v_conv_w)
    big_adam("w_lru_up", w_lru_up, g_w_lru_up, m_w_lru_up, v_w_lru_up)
    big_adam("w_pool_up", w_pool_up, g_w_pool_up, m_w_pool_up, v_w_pool_up)
    big_adam("w_o", w_o, g_w_o, m_w_o, v_w_o)
    big_adam("w_ff1", w_ff1, g_w_ff1, m_w_ff1, v_w_ff1)
    big_adam("w_ff2", w_ff2, g_w_ff2, m_w_ff2, v_w_ff2)

    loss = lax.psum(loss_part[0, 0], ("x", "y", "c"))
    order = ["norm_mix_pre", "norm_mix_post", "norm_mlp_pre", "norm_mlp_post", "w_in", "b_gate", "conv_w",
             "conv_b", "lru_w_a", "lru_b_a", "lru_w_x", "lru_b_x", "lru_lambda", "pool_w", "pool_scale",
             "w_lru_up", "w_pool_up", "w_o", "w_ff1", "w_ff2"]
    return (loss, grad_x[None], *[grads[n] for n in order], *[delta[n] for n in order],
            *[new_m[n] for n in order], *[new_v[n] for n in order])
```

```python
import functools
import math
import operator
import types

import jax
import jax.numpy as jnp
from jax import lax
from jax.experimental import pallas as pl
from jax.experimental.pallas import tpu as pltpu

F32 = jnp.float32
BF16 = jnp.bfloat16
NORM_EPS = 1e-6
LRU_C = 8.0
N_LRU_HEADS = 16
LRU_HEAD_DIM = 64
POOL_WINDOWS = (2, 4, 8, 16)
POOL_GROUP_DIM = 128
ADAM_LR = 0.001
ADAM_B1 = 0.9
ADAM_B2 = 0.999
ADAM_EPS = 1e-08
ADAM_WD = 0.01
ADAM_STEP = 10
N_DEV = 8
V7X_VMEM_LIMIT_BYTES = 48 * 1024 * 1024
LRU_CB = 256
MESH = pl.DeviceIdType.MESH
ANY = pl.BlockSpec(memory_space=pl.ANY)


def _tile(n, pref):
    t = min(n, pref)
    assert n % t == 0, (n, pref)
    return t


def _dot_nn(a, b):
    return lax.dot_general(a, b, (((1,), (0,)), ((), ())), preferred_element_type=F32)


def _dot_nt(a, b):
    return lax.dot_general(a, b, (((1,), (1,)), ((), ())), preferred_element_type=F32)


def _dot_tn(a, b):
    return lax.dot_general(a, b, (((0,), (0,)), ((), ())), preferred_element_type=F32)


def _sig(x):
    return 1.0 / (1.0 + jnp.exp(-x))


def _rms_hat(x):
    r = lax.rsqrt(jnp.mean(x * x, axis=-1, keepdims=True) + NORM_EPS)
    return x * r, r


def _rms_bwd(dn, xhat, r, g):
    q = dn * g
    dx = r * (q - xhat * jnp.mean(q * xhat, axis=-1, keepdims=True))
    dg = jnp.sum(dn * xhat, axis=0, keepdims=True)
    return dx, dg


_GELU_K = math.sqrt(2.0 / math.pi)
_GELU_C = 0.044715


def _gelu_and_grad(g):
    t = jnp.tanh(_GELU_K * (g + _GELU_C * g * g * g))
    val = 0.5 * g * (1.0 + t)
    grad = 0.5 * (1.0 + t) + 0.5 * g * (1.0 - t * t) * (_GELU_K * (1.0 + 3.0 * _GELU_C * g * g))
    return val, grad


def _softplus_neg(lam):
    z = -lam
    e = jnp.exp(-jnp.abs(z))
    u = 1.0 + e
    d = u - 1.0
    l1p = jnp.where(d == 0.0, e, jnp.log(u) * (e / jnp.where(d == 0.0, 1.0, d)))
    return jnp.maximum(z, 0.0) + l1p


def _lru_gates(xc, wa, ba, wx, bx, lam):
    xcb = xc.astype(BF16)
    r = _sig(_dot_nn(xcb, wa) + ba)
    i = _sig(_dot_nn(xcb, wx) + bx)
    sp = _softplus_neg(lam)
    log_a = (-LRU_C) * r * sp
    a = jnp.exp(log_a)
    mult = jnp.sqrt(-jnp.tanh(log_a) * (1.0 + a * a))
    return xcb, r, i, sp, log_a, a, mult


def _place():
    return lax.axis_index("x"), lax.axis_index("y"), lax.axis_index("c")


def _ag_plan(shards):
    na = len(shards)

    def parts(ins, outs, sems):
        send_sems, recv_sems, local_sems = sems
        x, y, c = _place()
        me, sibling = (x, y, c), (x, y, 1 - c)
        chips = [(1 - x, y), (x, 1 - y), (1 - x, 1 - y)]

        def slot(a, px, py, pc):
            return outs[a].at[4 * px + 2 * py + pc]

        def copy(a, k, block, to, src=None):
            return pltpu.make_async_remote_copy(
                src_ref=slot(a, *block) if src is None else src,
                dst_ref=slot(a, *block),
                send_sem=send_sems.at[a * 7 + k],
                recv_sem=recv_sems.at[a * 7 + k],
                device_id=to,
                device_id_type=MESH,
            )

        mine = [pltpu.make_async_copy(ins[a], slot(a, *me), local_sems.at[a]) for a in range(na)]
        first = []
        for a in range(na):
            first.append(copy(a, 0, me, sibling, src=ins[a]))
            first += [copy(a, 1 + j, me, (*chip, c), src=ins[a]) for j, chip in enumerate(chips)]
        return me, sibling, chips, c, copy, mine, first

    def start(ins, outs, sems):
        _, _, _, _, _, mine, first = parts(ins, outs, sems)
        for cp in mine + first:
            cp.start()

    def finish(ins, outs, sems):
        me, sibling, chips, c, copy, mine, first = parts(ins, outs, sems)
        passed = []
        for j, chip in enumerate(chips):
            for a in range(na):
                copy(a, 1 + j, (*chip, c), me).wait_recv()
                fwd = copy(a, 4 + j, (*chip, c), sibling)
                fwd.start()
                passed.append(fwd)
        for a in range(na):
            copy(a, 0, sibling, me).wait_recv()
            for j, chip in enumerate(chips):
                copy(a, 4 + j, (*chip, 1 - c), me).wait_recv()
        for cp in first + passed:
            cp.wait_send()
        for cp in mine:
            cp.wait()

    return types.SimpleNamespace(
        ins=list(shards),
        out_shapes=[jax.ShapeDtypeStruct((N_DEV,) + s.shape, s.dtype) for s in shards],
        sems=[pltpu.SemaphoreType.DMA((7 * na,)), pltpu.SemaphoreType.DMA((7 * na,)),
              pltpu.SemaphoreType.DMA((na,))],
        start=start, finish=finish)


def _rs_sibling_plan(fulls):
    na = len(fulls)
    rs = [f.shape[0] // N_DEV for f in fulls]

    def copies(ins, outs, sems):
        send_sems, recv_sems = sems
        x, y, c = _place()
        out = []
        for a in range(na):
            for q in range(4):
                shard = 2 * q + (1 - c)
                out.append(pltpu.make_async_remote_copy(
                    src_ref=ins[a].at[pl.ds(shard * rs[a], rs[a]), :],
                    dst_ref=outs[a].at[q],
                    send_sem=send_sems.at[a * 4 + q],
                    recv_sem=recv_sems.at[a * 4 + q],
                    device_id=(x, y, 1 - c),
                    device_id_type=MESH,
                ))
        return out

    def start(ins, outs, sems):
        for cp in copies(ins, outs, sems):
            cp.start()

    def finish(ins, outs, sems):
        for cp in copies(ins, outs, sems):
            cp.wait()

    return types.SimpleNamespace(
        ins=list(fulls),
        out_shapes=[jax.ShapeDtypeStruct((4, r, f.shape[1]), f.dtype) for r, f in zip(rs, fulls)],
        sems=[pltpu.SemaphoreType.DMA((4 * na,)), pltpu.SemaphoreType.DMA((4 * na,))],
        start=start, finish=finish)


def _rs_chips_plan(sends):
    na = len(sends)

    def copies(ins, outs, sems):
        send_sems, recv_sems = sems
        x, y, c = _place()
        chips = [(1 - x, y), (x, 1 - y), (1 - x, 1 - y)]
        out = []
        for a in range(na):
            for k, chip in enumerate(chips):
                out.append(pltpu.make_async_remote_copy(
                    src_ref=ins[a].at[k],
                    dst_ref=outs[a].at[k],
                    send_sem=send_sems.at[a * 3 + k],
                    recv_sem=recv_sems.at[a * 3 + k],
                    device_id=(*chip, c),
                    device_id_type=MESH,
                ))
        return out

    def start(ins, outs, sems):
        for cp in copies(ins, outs, sems):
            cp.start()

    def finish(ins, outs, sems):
        for cp in copies(ins, outs, sems):
            cp.wait()

    return types.SimpleNamespace(
        ins=list(sends),
        out_shapes=[jax.ShapeDtypeStruct(s.shape, s.dtype) for s in sends],
        sems=[pltpu.SemaphoreType.DMA((3 * na,)), pltpu.SemaphoreType.DMA((3 * na,))],
        start=start, finish=finish)


def _run_plan(plan, name):
    n_in, n_out = len(plan.ins), len(plan.out_shapes)

    def body(*refs):
        ins, outs, sems = refs[:n_in], refs[n_in:n_in + n_out], refs[n_in + n_out:]
        plan.start(ins, outs, sems)
        plan.finish(ins, outs, sems)

    return pl.pallas_call(
        body,
        name=name,
        in_specs=[ANY] * n_in,
        out_specs=[ANY] * n_out,
        out_shape=plan.out_shapes,
        scratch_shapes=plan.sems,
    )(*plan.ins)


def _call(body, *, name, grid, in_specs, out_specs, out_shape, args, scratch_shapes=(), aliases=None,
          carry=None):
    n_in, n_out, n_scr = len(in_specs), len(out_shape), len(scratch_shapes)
    params = pltpu.CompilerParams(
        dimension_semantics=("arbitrary",) * len(grid), vmem_limit_bytes=V7X_VMEM_LIMIT_BYTES)
    if carry is None:
        outs = pl.pallas_call(
            body, name=name, grid=grid, in_specs=list(in_specs), out_specs=list(out_specs),
            out_shape=list(out_shape), scratch_shapes=list(scratch_shapes),
            input_output_aliases=aliases or {}, compiler_params=params)(*args)
        return list(outs), []
    c_in, c_out = len(carry.ins), len(carry.out_shapes)

    def full(*refs):
        p = 0
        ins = refs[p:p + n_in]
        p += n_in
        cins = refs[p:p + c_in]
        p += c_in
        outs = refs[p:p + n_out]
        p += n_out
        couts = refs[p:p + c_out]
        p += c_out
        scr = refs[p:p + n_scr]
        csems = refs[p + n_scr:]
        ids = [pl.program_id(a) for a in range(len(grid))]
        first = functools.reduce(operator.and_, [i == 0 for i in ids])
        last = functools.reduce(operator.and_, [i == g - 1 for i, g in zip(ids, grid)])

        @pl.when(first)
        def _():
            carry.start(cins, couts, csems)

        body(*ins, *outs, *scr)

        @pl.when(last)
        def _():
            carry.finish(cins, couts, csems)

    outs = pl.pallas_call(
        full, name=name, grid=grid,
        in_specs=list(in_specs) + [ANY] * c_in,
        out_specs=list(out_specs) + [ANY] * c_out,
        out_shape=list(out_shape) + list(carry.out_shapes),
        scratch_shapes=list(scratch_shapes) + list(carry.sems),
        input_output_aliases=aliases or {}, compiler_params=params)(*args, *carry.ins)
    return list(outs[:n_out]), list(outs[n_out:])


def _norm_proj(x, g1, w_int, carry=None):
    t, d = x.shape
    n = w_int.shape[0]
    tt, tn = _tile(t, 512), _tile(n, 512)

    def body(x_ref, g_ref, w_ref, proj_ref, h1_ref, h1_s):
        @pl.when(pl.program_id(1) == 0)
        def _():
            xhat, _ = _rms_hat(x_ref[...])
            h = (xhat * g_ref[...]).astype(BF16)
            h1_s[...] = h
            h1_ref[...] = h

        proj_ref[...] = _dot_nt(h1_s[...], w_ref[...])

    return _call(
        body, name="norm_proj", grid=(t // tt, n // tn),
        in_specs=[
            pl.BlockSpec((tt, d), lambda i, j: (i, 0)),
            pl.BlockSpec((1, d), lambda i, j: (0, 0)),
            pl.BlockSpec((tn, d), lambda i, j: (j, 0)),
        ],
        out_specs=[
            pl.BlockSpec((tt, tn), lambda i, j: (i, j)),
            pl.BlockSpec((tt, d), lambda i, j: (i, 0)),
        ],
        out_shape=[jax.ShapeDtypeStruct((t, n), F32), jax.ShapeDtypeStruct((t, d), BF16)],
        scratch_shapes=[pltpu.VMEM((tt, d), BF16)],
        args=(x, g1, w_int), carry=carry)


def _lru_fwd(proj, conv_w, conv_b, wa_bd, b_a, wx_bd, b_x, lam, carry=None):
    t = proj.shape[0]
    dr = conv_b.shape[1]
    cb = LRU_CB
    tc = _tile(t, 256)
    ncb, ntc = dr // cb, t // tc

    def body(xp_ref, g_ref, cw_ref, cb_ref, wa_ref, ba_ref, wx_ref, bx_ref, lam_ref,
             y_ref, h_ref, xc_ref, prevx_s, hlast_s):
        c = pl.program_id(1)

        @pl.when(c == 0)
        def _():
            prevx_s[...] = jnp.zeros_like(prevx_s)
            hlast_s[...] = jnp.zeros_like(hlast_s)

        x = xp_ref[...]
        prev = prevx_s[...]
        row = lax.broadcasted_iota(jnp.int32, x.shape, 0)

        def sh(j):
            return jnp.where(row >= j, pltpu.roll(x, j, 0), pltpu.roll(prev, j, 0))

        xc = (cb_ref[...] + cw_ref[0:1, :] * sh(3) + cw_ref[1:2, :] * sh(2)
              + cw_ref[2:3, :] * sh(1) + cw_ref[3:4, :] * x)
        prevx_s[...] = x
        xc_ref[...] = xc
        _, _, i, _, _, a, mult = _lru_gates(xc, wa_ref[...], ba_ref[...], wx_ref[...], bx_ref[...],
                                            lam_ref[...])
        av, bv = a, mult * (i * xc)
        s = 1
        while s < tc:
            a_sh = jnp.where(row >= s, pltpu.roll(av, s, 0), 1.0)
            b_sh = jnp.where(row >= s, pltpu.roll(bv, s, 0), 0.0)
            bv = av * b_sh + bv
            av = av * a_sh
            s *= 2
        h = av * hlast_s[...] + bv
        h_ref[...] = h
        hlast_s[...] = h_ref[tc - 1:tc, :]
        gel, _ = _gelu_and_grad(g_ref[...])
        y_ref[...] = (h * gel).astype(BF16)

    vec = pl.BlockSpec((1, cb), lambda j, c: (0, j))
    blk = pl.BlockSpec((tc, cb), lambda j, c: (c, j))
    mat = pl.BlockSpec((None, cb, cb), lambda j, c: (j, 0, 0))
    return _call(
        body, name="lru_fwd", grid=(ncb, ntc),
        in_specs=[
            blk,
            pl.BlockSpec((tc, cb), lambda j, c: (c, ncb + j)),
            pl.BlockSpec((4, cb), lambda j, c: (0, j)),
            vec, mat, vec, mat, vec, vec,
        ],
        out_specs=[blk, blk, blk],
        out_shape=[
            jax.ShapeDtypeStruct((t, dr), BF16),
            jax.ShapeDtypeStruct((t, dr), F32),
            jax.ShapeDtypeStruct((t, dr), F32),
        ],
        scratch_shapes=[pltpu.VMEM((tc, cb), F32), pltpu.VMEM((1, cb), F32)],
        args=(proj, proj, conv_w, conv_b, wa_bd, b_a, wx_bd, b_x, lam), carry=carry)


def _pool_select(col, vals):
    out = vals[3]
    for g in (2, 1, 0):
        out = jnp.where(col < (g + 1) * POOL_GROUP_DIM, vals[g], out)
    return out


def _pool_fwd(proj, pool_w, pool_scale, col_block):
    t = proj.shape[0]
    dp = pool_scale.shape[1]
    tc = _tile(t, 256)
    ntc = t // tc

    def body(x_ref, w_ref, sc_ref, y_ref, p_ref, px, p2, p4, p8):
        c = pl.program_id(0)

        @pl.when(c == 0)
        def _():
            for s in (px, p2, p4, p8):
                s[...] = jnp.zeros_like(s)

        x = x_ref[...]
        row = lax.broadcasted_iota(jnp.int32, x.shape, 0)
        col = lax.broadcasted_iota(jnp.int32, x.shape, 1)

        def sh(v, pv, j):
            return jnp.where(row >= j, pltpu.roll(v, j, 0), pltpu.roll(pv[...], j, 0))

        s2 = x + sh(x, px, 1)
        s4 = s2 + sh(s2, p2, 2)
        s8 = s4 + sh(s4, p4, 4)
        s16 = s8 + sh(s8, p8, 8)
        px[...] = x
        p2[...] = s2
        p4[...] = s4
        p8[...] = s8
        wsum = _pool_select(col, (s2, s4, s8, s16))
        win = _pool_select(col, POOL_WINDOWS)
        cnt = jnp.minimum(c * tc + row + 1, win).astype(F32)
        p = wsum / cnt - x
        pb = p.astype(BF16)
        p_ref[...] = pb
        for g in range(len(POOL_WINDOWS)):
            sl = slice(g * POOL_GROUP_DIM, (g + 1) * POOL_GROUP_DIM)
            yg = _dot_nn(pb[:, sl], w_ref[g]) * sc_ref[:, sl]
            y_ref[:, sl] = yg.astype(BF16)

    return _call(
        body, name="pool_fwd", grid=(ntc,),
        in_specs=[
            pl.BlockSpec((tc, dp), lambda c: (c, col_block)),
            pl.BlockSpec(pool_w.shape, lambda c: (0, 0, 0)),
            pl.BlockSpec((1, dp), lambda c: (0, 0)),
        ],
        out_specs=[pl.BlockSpec((tc, dp), lambda c: (c, 0))] * 2,
        out_shape=[jax.ShapeDtypeStruct((t, dp), BF16)] * 2,
        scratch_shapes=[pltpu.VMEM((tc, dp), F32)] * 4,
        args=(proj, pool_w, pool_scale))[0]


def _branch_mix(y_lru, y_pool, w_lru_up, w_pool_upt, proj, b_gate, ga_block, gb_block):
    t, d = y_lru.shape
    dp = y_pool.shape[1]
    tt, tn = _tile(t, 512), 512
    nj = d // tn

    def body(yl_ref, yp_ref, wl_ref, wp_ref, ga_ref, gb_ref, ba_ref, bb_ref, bra_ref, brb_ref, mix_ref):
        br_a = _dot_nn(yl_ref[...], wl_ref[...])
        br_b = _dot_nt(yp_ref[...], wp_ref[...])
        bra_ref[...] = br_a
        brb_ref[...] = br_b
        ga = _sig(ga_ref[...] + ba_ref[...])
        gb = _sig(gb_ref[...] + bb_ref[...])
        mix_ref[...] = (ga * br_a + gb * br_b).astype(BF16)

    out = pl.BlockSpec((tt, tn), lambda j, i: (i, j))
    return _call(
        body, name="branch_mix", grid=(nj, t // tt),
        in_specs=[
            pl.BlockSpec((tt, d), lambda j, i: (i, 0)),
            pl.BlockSpec((tt, dp), lambda j, i: (i, 0)),
            pl.BlockSpec((d, tn), lambda j, i: (0, j)),
            pl.BlockSpec((tn, dp), lambda j, i: (j, 0)),
            pl.BlockSpec((tt, tn), lambda j, i: (i, ga_block + j)),
            pl.BlockSpec((tt, tn), lambda j, i: (i, gb_block + j)),
            pl.BlockSpec((1, tn), lambda j, i: (0, j)),
            pl.BlockSpec((1, tn), lambda j, i: (0, nj + j)),
        ],
        out_specs=[out, out, out],
        out_shape=[
            jax.ShapeDtypeStruct((t, d), F32),
            jax.ShapeDtypeStruct((t, d), F32),
            jax.ShapeDtypeStruct((t, d), BF16),
        ],
        args=(y_lru, y_pool, w_lru_up, w_pool_upt, proj, proj, b_gate, b_gate))[0]


def _wo_norm(mix, w_o, x, g2, g3):
    t, d = x.shape
    tt = _tile(t, 256)

    def body(mix_ref, w_ref, x_ref, g2_ref, g3_ref, m_ref, x2_ref, h3_ref):
        m = _dot_nn(mix_ref[...], w_ref[...])
        m_ref[...] = m
        mhat, _ = _rms_hat(m)
        x2 = x_ref[...] + mhat * g2_ref[...]
        x2_ref[...] = x2
        xhat, _ = _rms_hat(x2)
        h3_ref[...] = (xhat * g3_ref[...]).astype(BF16)

    row = pl.BlockSpec((tt, d), lambda i: (i, 0))
    vec = pl.BlockSpec((1, d), lambda i: (0, 0))
    return _call(
        body, name="wo_norm", grid=(t // tt,),
        in_specs=[row, pl.BlockSpec((d, d), lambda i: (0, 0)), row, vec, vec],
        out_specs=[row, row, row],
        out_shape=[
            jax.ShapeDtypeStruct((t, d), F32),
            jax.ShapeDtypeStruct((t, d), F32),
            jax.ShapeDtypeStruct((t, d), BF16),
        ],
        args=(mix, w_o, x, g2, g3))[0]


def _ff1(h3, w_ff1t, carry=None):
    t, d = h3.shape
    n = w_ff1t.shape[0]
    tt, tn = _tile(t, 512), _tile(n, 512)

    def body(h_ref, w_ref, rf_ref, act_ref):
        rf = jnp.maximum(_dot_nt(h_ref[...], w_ref[...]), 0.0)
        rf_ref[...] = rf.astype(BF16)
        act_ref[...] = (rf * rf).astype(BF16)

    out = pl.BlockSpec((tt, tn), lambda i, j: (i, j))
    return _call(
        body, name="ff1", grid=(t // tt, n // tn),
        in_specs=[pl.BlockSpec((tt, d), lambda i, j: (i, 0)), pl.BlockSpec((tn, d), lambda i, j: (j, 0))],
        out_specs=[out, out],
        out_shape=[jax.ShapeDtypeStruct((t, n), BF16)] * 2,
        args=(h3, w_ff1t), carry=carry)


def _ff2_loss(act, w_ff2, x2, g4, target):
    t, k = act.shape
    d = x2.shape[1]
    tt, tk = _tile(t, 256), _tile(k, 512)
    nk = k // tk

    def body(a_ref, w_ref, x2_ref, g_ref, tg_ref, dy_ref, df_ref, dg_ref, loss_ref, acc):
        i, kk = pl.program_id(0), pl.program_id(1)

        @pl.when(kk == 0)
        def _():
            acc[...] = jnp.zeros_like(acc)

        @pl.when((i == 0) & (kk == 0))
        def _():
            dg_ref[...] = jnp.zeros_like(dg_ref)
            loss_ref[...] = jnp.zeros_like(loss_ref)

        acc[...] += _dot_nn(a_ref[...], w_ref[...])

        @pl.when(kk == nk - 1)
        def _():
            fhat, r = _rms_hat(acc[...])
            g = g_ref[...]
            e = x2_ref[...] + fhat * g - tg_ref[...]
            loss_ref[...] += 0.5 * jnp.sum(jnp.mean(e * e, axis=-1, keepdims=True))
            dy = e * (1.0 / d)
            dy_ref[...] = dy
            df, dg = _rms_bwd(dy, fhat, r, g)
            df_ref[...] = df.astype(BF16)
            dg_ref[...] += dg

    row = pl.BlockSpec((tt, d), lambda i, kk: (i, 0))
    vec = pl.BlockSpec((1, d), lambda i, kk: (0, 0))
    return _call(
        body, name="ff2_loss", grid=(t // tt, nk),
        in_specs=[
            pl.BlockSpec((tt, tk), lambda i, kk: (i, kk)),
            pl.BlockSpec((tk, d), lambda i, kk: (kk, 0)),
            row, vec, row,
        ],
        out_specs=[row, row, vec, pl.BlockSpec((1, 128), lambda i, kk: (0, 0))],
        out_shape=[
            jax.ShapeDtypeStruct((t, d), F32),
            jax.ShapeDtypeStruct((t, d), BF16),
            jax.ShapeDtypeStruct((1, d), F32),
            jax.ShapeDtypeStruct((1, 128), F32),
        ],
        scratch_shapes=[pltpu.VMEM((tt, d), F32)],
        args=(act, w_ff2, x2, g4, target))[0]


def _ff2_bwd(df, w_ff2, rf):
    t, d = df.shape
    n = w_ff2.shape[0]
    tt, tn = _tile(t, 512), _tile(n, 512)

    def body(df_ref, w_ref, rf_ref, out_ref):
        d_act = _dot_nt(df_ref[...], w_ref[...])
        out_ref[...] = (d_act * (2.0 * rf_ref[...].astype(F32))).astype(BF16)

    blk = pl.BlockSpec((tt, tn), lambda i, j: (i, j))
    return _call(
        body, name="ff2_bwd", grid=(t // tt, n // tn),
        in_specs=[pl.BlockSpec((tt, d), lambda i, j: (i, 0)), pl.BlockSpec((tn, d), lambda i, j: (j, 0)), blk],
        out_specs=[blk],
        out_shape=[jax.ShapeDtypeStruct((t, n), BF16)],
        args=(df, w_ff2, rf))[0][0]


def _wgrad(a, b, name, prev=None, row_off=0, rows=None, carry=None):
    t, m = a.shape
    n = b.shape[1]
    rows = m if rows is None else rows
    tm, tk = _tile(m, 512), _tile(t, 512)
    nk = t // tk
    assert row_off % tm == 0
    off = row_off // tm

    def body(*refs):
        a_ref, b_ref = refs[0], refs[1]
        o32_ref, o16_ref, acc = refs[-3], refs[-2], refs[-1]
        kk = pl.program_id(1)

        @pl.when(kk == 0)
        def _():
            acc[...] = jnp.zeros_like(acc)

        acc[...] += _dot_tn(a_ref[...], b_ref[...])

        @pl.when(kk == nk - 1)
        def _():
            o32_ref[...] = acc[...]
            o16_ref[...] = acc[...].astype(BF16)

    in_specs = [pl.BlockSpec((tk, tm), lambda i, kk: (kk, i)), pl.BlockSpec((tk, n), lambda i, kk: (kk, 0))]
    args = [a, b]
    aliases = {}
    if prev is not None:
        in_specs += [ANY, ANY]
        args += list(prev)
        aliases = {2: 0, 3: 1}
    out = pl.BlockSpec((tm, n), lambda i, kk: (off + i, 0))
    return _call(
        body, name=name, grid=(m // tm, nk),
        in_specs=in_specs, out_specs=[out, out],
        out_shape=[jax.ShapeDtypeStruct((rows, n), F32), jax.ShapeDtypeStruct((rows, n), BF16)],
        scratch_shapes=[pltpu.VMEM((tm, n), F32)],
        aliases=aliases, args=args, carry=carry)


def _ff1_bwd_norms(d_f1, w_ff1t, dy, x2, g3, m, g2, carry=None):
    t, k = d_f1.shape
    d = x2.shape[1]
    tt, tk = _tile(t, 256), _tile(k, 512)
    nk = k // tk

    def body(a_ref, w_ref, dy_ref, x2_ref, g3_ref, m_ref, g2_ref, dx2_ref, dm_ref, dg3_ref, dg2_ref, acc):
        i, kk = pl.program_id(0), pl.program_id(1)

        @pl.when(kk == 0)
        def _():
            acc[...] = jnp.zeros_like(acc)

        @pl.when((i == 0) & (kk == 0))
        def _():
            dg3_ref[...] = jnp.zeros_like(dg3_ref)
            dg2_ref[...] = jnp.zeros_like(dg2_ref)

        acc[...] += _dot_nn(a_ref[...], w_ref[...])

        @pl.when(kk == nk - 1)
        def _():
            xhat, r3 = _rms_hat(x2_ref[...])
            dx, dg3 = _rms_bwd(acc[...], xhat, r3, g3_ref[...])
            dx2 = dy_ref[...] + dx
            dx2_ref[...] = dx2
            dg3_ref[...] += dg3
            mhat, r2 = _rms_hat(m_ref[...])
            dm, dg2 = _rms_bwd(dx2, mhat, r2, g2_ref[...])
            dm_ref[...] = dm.astype(BF16)
            dg2_ref[...] += dg2

    row = pl.BlockSpec((tt, d), lambda i, kk: (i, 0))
    vec = pl.BlockSpec((1, d), lambda i, kk: (0, 0))
    return _call(
        body, name="ff1_bwd_norms", grid=(t // tt, nk),
        in_specs=[
            pl.BlockSpec((tt, tk), lambda i, kk: (i, kk)),
            pl.BlockSpec((tk, d), lambda i, kk: (kk, 0)),
            row, row, vec, row, vec,
        ],
        out_specs=[row, row, vec, vec],
        out_shape=[
            jax.ShapeDtypeStruct((t, d), F32),
            jax.ShapeDtypeStruct((t, d), BF16),
            jax.ShapeDtypeStruct((1, d), F32),
            jax.ShapeDtypeStruct((1, d), F32),
        ],
        scratch_shapes=[pltpu.VMEM((tt, d), F32)],
        args=(d_f1, w_ff1t, dy, x2, g3, m, g2), carry=carry)


def _wo_bwd_mix(dm, w_o, br_a, br_b, proj, b_gate, ga_block, gb_block):
    t, d = dm.shape
    tt, tn = _tile(t, 512), 512
    nj = d // tn

    def body(dm_ref, w_ref, bra_ref, brb_ref, ga_ref, gb_ref, ba_ref, bb_ref,
             dbra_ref, dbrb_ref, dga_ref, dgb_ref, dba_ref, dbb_ref):
        i = pl.program_id(1)

        @pl.when(i == 0)
        def _():
            dba_ref[...] = jnp.zeros_like(dba_ref)
            dbb_ref[...] = jnp.zeros_like(dbb_ref)

        d_mix = _dot_nt(dm_ref[...], w_ref[...])
        ga = _sig(ga_ref[...] + ba_ref[...])
        gb = _sig(gb_ref[...] + bb_ref[...])
        dbra_ref[...] = (d_mix * ga).astype(BF16)
        dbrb_ref[...] = (d_mix * gb).astype(BF16)
        dga = d_mix * bra_ref[...] * (ga * (1.0 - ga))
        dgb = d_mix * brb_ref[...] * (gb * (1.0 - gb))
        dga_ref[...] = dga.astype(BF16)
        dgb_ref[...] = dgb.astype(BF16)
        dba_ref[...] += jnp.sum(dga, axis=0, keepdims=True)
        dbb_ref[...] += jnp.sum(dgb, axis=0, keepdims=True)

    blk = pl.BlockSpec((tt, tn), lambda j, i: (i, j))
    vec = pl.BlockSpec((1, tn), lambda j, i: (0, j))
    return _call(
        body, name="wo_bwd_mix", grid=(nj, t // tt),
        in_specs=[
            pl.BlockSpec((tt, d), lambda j, i: (i, 0)),
            pl.BlockSpec((tn, d), lambda j, i: (j, 0)),
            blk, blk,
            pl.BlockSpec((tt, tn), lambda j, i: (i, ga_block + j)),
            pl.BlockSpec((tt, tn), lambda j, i: (i, gb_block + j)),
            vec,
            pl.BlockSpec((1, tn), lambda j, i: (0, nj + j)),
        ],
        out_specs=[blk, blk, blk, blk, vec, vec],
        out_shape=[jax.ShapeDtypeStruct((t, d), BF16)] * 4 + [jax.ShapeDtypeStruct((1, d), F32)] * 2,
        args=(dm, w_o, br_a, br_b, proj, proj, b_gate, b_gate))[0]


def _lru_up_bwd(d_br_a, w_lru_up, proj, h, g_block):
    t, d = d_br_a.shape
    tt, tn = _tile(t, 512), 512

    def body(a_ref, w_ref, g_ref, h_ref, dh_ref, dg_ref):
        d_y = _dot_nt(a_ref[...], w_ref[...])
        gel, gel_grad = _gelu_and_grad(g_ref[...])
        dh_ref[...] = d_y * gel
        dg_ref[...] = (d_y * h_ref[...] * gel_grad).astype(BF16)

    blk = pl.BlockSpec((tt, tn), lambda i, j: (i, j))
    return _call(
        body, name="lru_up_bwd", grid=(t // tt, d // tn),
        in_specs=[
            pl.BlockSpec((tt, d), lambda i, j: (i, 0)),
            pl.BlockSpec((tn, d), lambda i, j: (j, 0)),
            pl.BlockSpec((tt, tn), lambda i, j: (i, g_block + j)),
            blk,
        ],
        out_specs=[blk, blk],
        out_shape=[jax.ShapeDtypeStruct((t, d), F32), jax.ShapeDtypeStruct((t, d), BF16)],
        args=(d_br_a, w_lru_up, proj, h))[0]


def _pool_up_bwd(d_br_b, w_pool_upt):
    t, d = d_br_b.shape
    dp = w_pool_upt.shape[1]
    tt = _tile(t, 512)

    def body(a_ref, w_ref, out_ref):
        out_ref[...] = _dot_nn(a_ref[...], w_ref[...])

    return _call(
        body, name="pool_up_bwd", grid=(t // tt,),
        in_specs=[pl.BlockSpec((tt, d), lambda i: (i, 0)), pl.BlockSpec((d, dp), lambda i: (0, 0))],
        out_specs=[pl.BlockSpec((tt, dp), lambda i: (i, 0))],
        out_shape=[jax.ShapeDtypeStruct((t, dp), F32)],
        args=(d_br_b, w_pool_upt))[0][0]


def _lru_bwd(dh, xc, h, proj, conv_w, wa_bd, b_a, wx_bd, b_x, lam, carry=None):
    t, dr = dh.shape
    cb = LRU_CB
    tc = _tile(t, 256)
    ncb, ntc = dr // cb, t // tc

    def body(dh_ref, xc_ref, h_ref, hp_ref, xp_ref, cw_ref, wa_ref, ba_ref, wx_ref, bx_ref, lam_ref,
             dxp_ref, dwa_ref, dba_ref, dwx_ref, dbx_ref, dlam_ref, dcw_ref, dcb_ref,
             nextd_s, anext_s, gnext_s, tmp_s):
        c = pl.program_id(1)
        rc = ntc - 1 - c

        @pl.when(c == 0)
        def _():
            nextd_s[...] = jnp.zeros_like(nextd_s)
            anext_s[...] = jnp.zeros_like(anext_s)
            gnext_s[...] = jnp.zeros_like(gnext_s)
            for ref in (dwa_ref, dba_ref, dwx_ref, dbx_ref, dlam_ref, dcw_ref, dcb_ref):
                ref[...] = jnp.zeros_like(ref)

        xc = xc_ref[...]
        wa, wx, lam = wa_ref[...], wx_ref[...], lam_ref[...]
        xcb, r, i, sp, log_a, a, mult = _lru_gates(xc, wa, ba_ref[...], wx, bx_ref[...], lam)
        row = lax.broadcasted_iota(jnp.int32, xc.shape, 0)
        h = h_ref[...]
        hp = jnp.where(rc == 0, 0.0, hp_ref[...])
        hprev = jnp.where(row >= 1, pltpu.roll(h, 1, 0), pltpu.roll(hp, 1, 0))

        def up(v, nv, j):
            return jnp.where(row < tc - j, pltpu.roll(v, tc - j, 0), nv)

        av = up(a, anext_s[...], 1)
        bv = dh_ref[...]
        s = 1
        while s < tc:
            a_sh = up(av, 1.0, s)
            b_sh = up(bv, 0.0, s)
            bv = av * b_sh + bv
            av = av * a_sh
            s *= 2
        gt = av * gnext_s[...] + bv
        tmp_s[...] = gt
        gnext_s[...] = tmp_s[0:1, :]
        tmp_s[...] = a
        anext_s[...] = tmp_s[0:1, :]

        da = gt * hprev
        ixc = i * xc
        d_mult = gt * ixc
        d_i = gt * mult * xc
        d_xc = gt * mult * i
        d_log_a = da * a - d_mult * (a * a) / mult
        d_pre_r = (d_log_a * ((-LRU_C) * sp)) * (r * (1.0 - r))
        d_pre_i = d_i * (i * (1.0 - i))
        d_sp = jnp.sum(d_log_a * ((-LRU_C) * r), axis=0, keepdims=True)
        dlam_ref[...] += d_sp * (-1.0 / (1.0 + jnp.exp(lam)))
        dpr = d_pre_r.astype(BF16)
        dpi = d_pre_i.astype(BF16)
        dba_ref[...] += jnp.sum(d_pre_r, axis=0, keepdims=True)
        dbx_ref[...] += jnp.sum(d_pre_i, axis=0, keepdims=True)
        dwa_ref[...] += _dot_tn(xcb, dpr)
        dwx_ref[...] += _dot_tn(xcb, dpi)
        d_xc = d_xc + _dot_nt(dpr, wa) + _dot_nt(dpi, wx)

        nxt = nextd_s[...]
        xp = xp_ref[...]
        dxp = cw_ref[3:4, :] * d_xc
        dcw_ref[3:4, :] += jnp.sum(xp * d_xc, axis=0, keepdims=True)
        for j in (1, 2, 3):
            uj = up(d_xc, pltpu.roll(nxt, tc - j, 0), j)
            dxp = dxp + cw_ref[3 - j:4 - j, :] * uj
            dcw_ref[3 - j:4 - j, :] += jnp.sum(xp * uj, axis=0, keepdims=True)
        dcb_ref[...] += jnp.sum(d_xc, axis=0, keepdims=True)
        nextd_s[...] = d_xc
        dxp_ref[...] = dxp.astype(BF16)

    vec = pl.BlockSpec((1, cb), lambda j, c: (0, j))
    blk = pl.BlockSpec((tc, cb), lambda j, c: (ntc - 1 - c, j))
    mat = pl.BlockSpec((None, cb, cb), lambda j, c: (j, 0, 0))
    cwb = pl.BlockSpec((4, cb), lambda j, c: (0, j))
    return _call(
        body, name="lru_bwd", grid=(ncb, ntc),
        in_specs=[
            blk, blk, blk,
            pl.BlockSpec((tc, cb), lambda j, c: (jnp.maximum(ntc - 2 - c, 0), j)),
            blk, cwb, mat, vec, mat, vec, vec,
        ],
        out_specs=[blk, mat, vec, mat, vec, vec, cwb, vec],
        out_shape=[
            jax.ShapeDtypeStruct((t, dr), BF16),
            jax.ShapeDtypeStruct((ncb, cb, cb), F32),
            jax.ShapeDtypeStruct((1, dr), F32),
            jax.ShapeDtypeStruct((ncb, cb, cb), F32),
            jax.ShapeDtypeStruct((1, dr), F32),
            jax.ShapeDtypeStruct((1, dr), F32),
            jax.ShapeDtypeStruct((4, dr), F32),
            jax.ShapeDtypeStruct((1, dr), F32),
        ],
        scratch_shapes=[
            pltpu.VMEM((tc, cb), F32),
            pltpu.VMEM((1, cb), F32),
            pltpu.VMEM((1, cb), F32),
            pltpu.VMEM((tc, cb), F32),
        ],
        args=(dh, xc, h, h, proj, conv_w, wa_bd, b_a, wx_bd, b_x, lam), carry=carry)


def _pool_bwd(d_y_pool, p, pool_w, pool_scale):
    t, dp = d_y_pool.shape
    tc = _tile(t, 256)
    ntc = t // tc
    ng = len(POOL_WINDOWS)

    def body(dy_ref, p_ref, w_ref, sc_ref, dx_ref, dw_ref, dsc_ref, nz, n2, n4, n8, dp_s):
        c = pl.program_id(0)
        rc = ntc - 1 - c

        @pl.when(c == 0)
        def _():
            for s in (nz, n2, n4, n8):
                s[...] = jnp.zeros_like(s)
            dw_ref[...] = jnp.zeros_like(dw_ref)
            dsc_ref[...] = jnp.zeros_like(dsc_ref)

        for g in range(ng):
            sl = slice(g * POOL_GROUP_DIM, (g + 1) * POOL_GROUP_DIM)
            pg = p_ref[:, sl]
            dyg = dy_ref[:, sl]
            wg = w_ref[g].astype(BF16)
            q = _dot_nn(pg, wg)
            dsc_ref[:, sl] += jnp.sum(dyg * q, axis=0, keepdims=True)
            dpw = (dyg * sc_ref[:, sl]).astype(BF16)
            dw_ref[g] += _dot_tn(pg, dpw)
            dp_s[:, sl] = _dot_nt(dpw, wg)

        dpv = dp_s[...]
        row = lax.broadcasted_iota(jnp.int32, dpv.shape, 0)
        col = lax.broadcasted_iota(jnp.int32, dpv.shape, 1)
        win = _pool_select(col, POOL_WINDOWS)
        cnt = jnp.minimum(rc * tc + row + 1, win).astype(F32)
        z = dpv / cnt

        def up(v, nv, j):
            return jnp.where(row < tc - j, pltpu.roll(v, tc - j, 0), pltpu.roll(nv[...], tc - j, 0))

        u2 = z + up(z, nz, 1)
        u4 = u2 + up(u2, n2, 2)
        u8 = u4 + up(u4, n4, 4)
        u16 = u8 + up(u8, n8, 8)
        nz[...] = z
        n2[...] = u2
        n4[...] = u4
        n8[...] = u8
        dx_ref[...] = (_pool_select(col, (u2, u4, u8, u16)) - dpv).astype(BF16)

    blk = pl.BlockSpec((tc, dp), lambda c: (ntc - 1 - c, 0))
    full_w = pl.BlockSpec(pool_w.shape, lambda c: (0, 0, 0))
    vec = pl.BlockSpec((1, dp), lambda c: (0, 0))
    return _call(
        body, name="pool_bwd", grid=(ntc,),
        in_specs=[blk, blk, full_w, vec],
        out_specs=[blk, full_w, vec],
        out_shape=[
            jax.ShapeDtypeStruct((t, dp), BF16),
            jax.ShapeDtypeStruct(pool_w.shape, F32),
            jax.ShapeDtypeStruct((1, dp), F32),
        ],
        scratch_shapes=[pltpu.VMEM((tc, dp), F32)] * 5,
        args=(d_y_pool, p, pool_w, pool_scale))[0]


def _win_bwd_norm(parts, w_int, dx2, x, g1, carry=None):
    t, d = x.shape
    tk = 512
    tt = _tile(t, 256)
    bounds = []
    k0 = 0
    for part in parts:
        assert part.shape[1] % tk == 0
        bounds.append((k0, k0 + part.shape[1] // tk))
        k0 += part.shape[1] // tk
    nk = k0
    assert nk * tk == w_int.shape[0]
    np_ = len(parts)

    def body(*refs):
        p_refs = refs[:np_]
        w_ref, dx2_ref, x_ref, g_ref, gx_ref, dg_ref, acc = refs[np_:]
        i, kk = pl.program_id(0), pl.program_id(1)

        @pl.when(kk == 0)
        def _():
            acc[...] = jnp.zeros_like(acc)

        @pl.when((i == 0) & (kk == 0))
        def _():
            dg_ref[...] = jnp.zeros_like(dg_ref)

        for (lo, hi), p_ref in zip(bounds, p_refs):
            @pl.when((kk >= lo) & (kk < hi))
            def _(p_ref=p_ref):
                acc[...] += _dot_nn(p_ref[...], w_ref[...])

        @pl.when(kk == nk - 1)
        def _():
            xhat, r = _rms_hat(x_ref[...])
            dx, dg = _rms_bwd(acc[...], xhat, r, g_ref[...])
            gx_ref[...] = dx2_ref[...] + dx
            dg_ref[...] += dg

    def part_spec(lo, hi):
        return pl.BlockSpec((tt, tk), lambda i, kk: (i, jnp.clip(kk - lo, 0, hi - lo - 1)))

    row = pl.BlockSpec((tt, d), lambda i, kk: (i, 0))
    vec = pl.BlockSpec((1, d), lambda i, kk: (0, 0))
    return _call(
        body, name="win_bwd_norm", grid=(t // tt, nk),
        in_specs=[part_spec(lo, hi) for lo, hi in bounds]
        + [pl.BlockSpec((tk, d), lambda i, kk: (kk, 0)), row, row, vec],
        out_specs=[row, vec],
        out_shape=[jax.ShapeDtypeStruct((t, d), F32), jax.ShapeDtypeStruct((1, d), F32)],
        scratch_shapes=[pltpu.VMEM((tt, d), F32)],
        args=(*parts, w_int, dx2, x, g1), carry=carry)


def _adamw(w, g, m, v, name):
    rows, cols = w.shape
    tr = rows if rows <= 512 else _tile(rows, 256)

    def body(w_ref, g_ref, m_ref, v_ref, d_ref, nm_ref, nv_ref):
        g = g_ref[...]
        m = ADAM_B1 * m_ref[...] + (1.0 - ADAM_B1) * g
        v = ADAM_B2 * v_ref[...] + (1.0 - ADAM_B2) * (g * g)
        m_hat = m / (1.0 - ADAM_B1 ** ADAM_STEP)
        v_hat = v / (1.0 - ADAM_B2 ** ADAM_STEP)
        d_ref[...] = -ADAM_LR * (m_hat / (jnp.sqrt(v_hat) + ADAM_EPS) + ADAM_WD * w_ref[...])
        nm_ref[...] = m
        nv_ref[...] = v

    blk = pl.BlockSpec((tr, cols), lambda i: (i, 0))
    return _call(
        body, name=name, grid=(rows // tr,),
        in_specs=[blk] * 4, out_specs=[blk] * 3,
        out_shape=[jax.ShapeDtypeStruct((rows, cols), F32)] * 3,
        args=(w, g, m, v))[0]


def _rs_sum(full, recv, shard_ids, slot_ids, name):
    _, r, cols = recv.shape
    send_dtype = recv.dtype

    def body(sh_ref, sl_ref, full_ref, recv_ref, own_ref, send_ref):
        s = pl.program_id(0)
        v = full_ref[...] + recv_ref[...].astype(F32)

        @pl.when(s == 0)
        def _():
            own_ref[...] = v

        @pl.when(s > 0)
        def _():
            send_ref[...] = v.astype(send_dtype)

    grid_spec = pltpu.PrefetchScalarGridSpec(
        num_scalar_prefetch=2,
        grid=(4,),
        in_specs=[
            pl.BlockSpec((r, cols), lambda s, sh, sl: (sh[s], 0)),
            pl.BlockSpec((None, r, cols), lambda s, sh, sl: (sl[s], 0, 0)),
        ],
        out_specs=[
            pl.BlockSpec((None, r, cols), lambda s, sh, sl: (0, 0, 0)),
            pl.BlockSpec((None, r, cols), lambda s, sh, sl: (jnp.maximum(s - 1, 0), 0, 0)),
        ],
    )
    return pl.pallas_call(
        body,
        name=name,
        grid_spec=grid_spec,
        out_shape=[jax.ShapeDtypeStruct((1, r, cols), F32), jax.ShapeDtypeStruct((3, r, cols), send_dtype)],
        compiler_params=pltpu.CompilerParams(
            dimension_semantics=("arbitrary",), vmem_limit_bytes=V7X_VMEM_LIMIT_BYTES),
    )(shard_ids, slot_ids, full, recv)


def _final_sum(own, recv, name):
    _, r, cols = own.shape
    tr = _tile(r, 64)

    def body(own_ref, recv_ref, out_ref):
        acc = own_ref[...]
        for k in range(3):
            acc = acc + recv_ref[k].astype(F32)
        out_ref[...] = acc

    return _call(
        body, name=name, grid=(r // tr,),
        in_specs=[
            pl.BlockSpec((None, tr, cols), lambda i: (0, i, 0)),
            pl.BlockSpec((3, tr, cols), lambda i: (0, i, 0)),
        ],
        out_specs=[pl.BlockSpec((tr, cols), lambda i: (i, 0))],
        out_shape=[jax.ShapeDtypeStruct((r, cols), F32)],
        args=(own, recv))[0][0]


def _rs_level1(fulls_f32, fulls_send, tag):
    x, y, c = _place()
    recv1 = _run_plan(_rs_sibling_plan(fulls_send), "rs_sibling_" + tag)
    qs = jnp.stack([2 * x + y, 2 * (1 - x) + y, 2 * x + (1 - y), 2 * (1 - x) + (1 - y)]).astype(jnp.int32)
    shard_ids = 2 * qs + c
    return [_rs_sum(f32, r1, shard_ids, qs, f"rs_sum_{tag}{a}")
            for a, (f32, r1) in enumerate(zip(fulls_f32, recv1))]


def _block_diag(w, per):
    h, hd, _ = w.shape
    out = jnp.zeros((h // per, per * hd, per * hd), w.dtype)
    for k in range(per):
        out = out.at[:, k * hd:(k + 1) * hd, k * hd:(k + 1) * hd].set(w[k::per])
    return out


def _block_diag_extract(w_bd, per, hd):
    g = w_bd.shape[0]
    blocks = [w_bd[:, k * hd:(k + 1) * hd, k * hd:(k + 1) * hd] for k in range(per)]
    return jnp.stack(blocks, axis=1).reshape(g * per, hd, hd)


def _rows(g):
    return g.reshape(g.shape[0] * g.shape[1], g.shape[2])


def kernel(x, norm_mix_pre, norm_mix_post, norm_mlp_pre, norm_mlp_post, w_in, b_gate, conv_w, conv_b, lru_w_a, lru_b_a, lru_w_x, lru_b_x, lru_lambda, pool_w, pool_scale, w_lru_up, w_pool_up, w_o, w_ff1, w_ff2, loss_target, m_norm_mix_pre, m_norm_mix_post, m_norm_mlp_pre, m_norm_mlp_post, m_w_in, m_b_gate, m_conv_w, m_conv_b, m_lru_w_a, m_lru_b_a, m_lru_w_x, m_lru_b_x, m_lru_lambda, m_pool_w, m_pool_scale, m_w_lru_up, m_w_pool_up, m_w_o, m_w_ff1, m_w_ff2, v_norm_mix_pre, v_norm_mix_post, v_norm_mlp_pre, v_norm_mlp_post, v_w_in, v_b_gate, v_conv_w, v_conv_b, v_lru_w_a, v_lru_b_a, v_lru_w_x, v_lru_b_x, v_lru_lambda, v_pool_w, v_pool_scale, v_w_lru_up, v_w_pool_up, v_w_o, v_w_ff1, v_w_ff2):
    t, d = x.shape[1], x.shape[2]
    d_rnn = conv_b.shape[1]
    d_pool = pool_scale.shape[1]
    per = LRU_CB // LRU_HEAD_DIM
    xi, yi, ci = _place()
    me = 4 * xi + 2 * yi + ci

    x2d = x[0]
    tgt = loss_target[0]

    s_in = w_in[0].T.astype(BF16)
    s_lu = w_lru_up[0].astype(BF16)
    s_pu = w_pool_up[0].T.astype(BF16)
    s_o = w_o[0].astype(BF16)
    s_f1 = w_ff1[0].T.astype(BF16)
    s_f2 = w_ff2[0].astype(BF16)
    s_cw = jnp.pad(conv_w[0], ((0, 4), (0, 0)))

    g_in, g_cw = _run_plan(_ag_plan([s_in, s_cw]), "ag_w_in")
    w_int = _rows(g_in)
    conv_w_full = jnp.transpose(g_cw[:, :4, :], (1, 0, 2)).reshape(4, d_rnn)

    wa_bd = _block_diag(lru_w_a[0], per).astype(BF16)
    wx_bd = _block_diag(lru_w_x[0], per).astype(BF16)
    pw = pool_w[0]
    pw_bf = pw.astype(BF16)

    pool_block = (2 * d_rnn) // d_pool
    ga_block = (2 * d_rnn + d_pool) // 512
    gb_block = ga_block + d // 512
    g_block = d_rnn // 512

    (proj, h1), (g_lu, g_pu, g_o) = _norm_proj(x2d, norm_mix_pre, w_int, carry=_ag_plan([s_lu, s_pu, s_o]))
    w_lu, w_put, w_og = _rows(g_lu), _rows(g_pu), _rows(g_o)
    (y_lru, h, xc), (g_f1,) = _lru_fwd(proj, conv_w_full, conv_b, wa_bd, lru_b_a, wx_bd, lru_b_x, lru_lambda,
                                       carry=_ag_plan([s_f1]))
    w_f1t = _rows(g_f1)
    y_pool, p = _pool_fwd(proj, pw_bf, pool_scale, pool_block)
    br_a, br_b, mix = _branch_mix(y_lru, y_pool, w_lu, w_put, proj, b_gate, ga_block, gb_block)
    m, x2, h3 = _wo_norm(mix, w_og, x2d, norm_mix_post, norm_mlp_pre)
    (rf, act), (g_f2,) = _ff1(h3, w_f1t, carry=_ag_plan([s_f2]))
    w_f2 = _rows(g_f2)
    dy, df, dg4, loss_part = _ff2_loss(act, w_f2, x2, norm_mlp_post, tgt)

    d_f1 = _ff2_bwd(df, w_f2, rf)
    (gw_ff2_32, gw_ff2_16), _ = _wgrad(act, df, "wgrad_ff2")
    ((own_ff2, send_ff2),) = _rs_level1([gw_ff2_32], [gw_ff2_16], "ff2")
    (gw_ff1_32, gw_ff1_16), (r2_ff2,) = _wgrad(d_f1, h3, "wgrad_ff1", carry=_rs_chips_plan([send_ff2]))
    ((own_ff1, send_ff1),) = _rs_level1([gw_ff1_32], [gw_ff1_16], "ff1")
    (dx2, dm, dg3, dg2), (r2_ff1,) = _ff1_bwd_norms(d_f1, w_f1t, dy, x2, norm_mlp_pre, m, norm_mix_post,
                                                    carry=_rs_chips_plan([send_ff1]))
    (gw_o_32, gw_o_16), _ = _wgrad(mix, dm, "wgrad_o")
    d_br_a, d_br_b, p_ga, p_gb, dbg_a, dbg_b = _wo_bwd_mix(dm, w_og, br_a, br_b, proj, b_gate, ga_block, gb_block)
    (gw_lu_32, gw_lu_16), _ = _wgrad(y_lru, d_br_a, "wgrad_lru_up")
    (gw_pu_32, gw_pu_16), _ = _wgrad(d_br_b, y_pool, "wgrad_pool_up")
    mid = _rs_level1([gw_o_32, gw_lu_32, gw_pu_32.reshape(-1, d)],
                     [gw_o_16, gw_lu_16, gw_pu_16.reshape(-1, d)], "mid")
    dh, p_g = _lru_up_bwd(d_br_a, w_lu, proj, h, g_block)
    d_y_pool = _pool_up_bwd(d_br_b, w_put)
    (p_x, dwa_bd, db_a, dwx_bd, db_x, dlam, dconv_w, dconv_b), r2_mid = _lru_bwd(
        dh, xc, h, proj, conv_w_full, wa_bd, lru_b_a, wx_bd, lru_b_x, lru_lambda,
        carry=_rs_chips_plan([s for _, s in mid]))
    p_p, dpool_w, dpool_scale = _pool_bwd(d_y_pool, p, pw, pool_scale)
    parts = [p_x, p_g, p_p, p_ga, p_gb]
    gw_in = None
    row_off = 0
    for k, part in enumerate(parts):
        gw_in, _ = _wgrad(part, h1, f"wgrad_in{k}", prev=gw_in, row_off=row_off, rows=w_int.shape[0])
        row_off += part.shape[1]
    ((own_in, send_in),) = _rs_level1([gw_in[0]], [gw_in[1]], "in")
    (grad_x, dg1), (r2_in,) = _win_bwd_norm(parts, w_int, dx2, x2d, norm_mix_pre,
                                             carry=_rs_chips_plan([send_in]))

    g_w_in = _final_sum(own_in, r2_in, "rs_final_in").T
    g_w_lru_up = _final_sum(mid[1][0], r2_mid[1], "rs_final_lru_up")
    g_w_pool_up = _final_sum(mid[2][0], r2_mid[2], "rs_final_pool_up").reshape(d // N_DEV, d_pool).T
    g_w_o = _final_sum(mid[0][0], r2_mid[0], "rs_final_o")
    g_w_ff1 = _final_sum(own_ff1, r2_ff1, "rs_final_ff1").T
    g_w_ff2 = _final_sum(own_ff2, r2_ff2, "rs_final_ff2")

    dwa = _block_diag_extract(dwa_bd, per, LRU_HEAD_DIM)
    dwx = _block_diag_extract(dwx_bd, per, LRU_HEAD_DIM)
    small_names = ["norm_mix_pre", "norm_mix_post", "norm_mlp_pre", "norm_mlp_post", "b_gate", "conv_b",
                   "lru_w_a", "lru_b_a", "lru_w_x", "lru_b_x", "lru_lambda", "pool_w", "pool_scale"]
    small_grads = [dg1, dg2, dg3, dg4, jnp.concatenate([dbg_a, dbg_b], axis=1), dconv_b,
                   dwa, db_a, dwx, db_x, dlam, dpool_w, dpool_scale]

    def pack(arrs):
        flat = [a.reshape(-1) for a in arrs]
        flat = [jnp.pad(f, (0, (-f.shape[0]) % d)) for f in flat]
        rows = jnp.concatenate(flat).reshape(-1, d)
        pad_rows = (-rows.shape[0]) % (8 * N_DEV)
        return jnp.pad(rows, ((0, pad_rows), (0, 0)))

    small_shapes = [a.shape for a in small_grads]
    packed = pack(small_grads + [dconv_w])
    ((own_sm, send_sm),) = _rs_level1([packed], [packed], "small")
    (r2_sm,) = _run_plan(_rs_chips_plan([send_sm]), "rs_chips_small")
    red_small = _final_sum(own_sm, r2_sm, "rs_final_small")
    small_all = _run_plan(_ag_plan([red_small]), "ag_small")[0]
    small_all = small_all.reshape(-1, d)

    def unpack(rows, shapes):
        out, r0 = [], 0
        for shp in shapes:
            n = math.prod(shp)
            nr = -(-n // d)
            out.append(rows[r0:r0 + nr].reshape(-1)[:n].reshape(shp))
            r0 += nr
        return out

    unpacked = unpack(small_all, small_shapes + [dconv_w.shape])
    g_small = dict(zip(small_names, unpacked[:-1]))
    g_conv_w = lax.dynamic_slice_in_dim(unpacked[-1], me * (d_rnn // N_DEV), d_rnn // N_DEV, axis=1)

    args = dict(norm_mix_pre=norm_mix_pre, norm_mix_post=norm_mix_post, norm_mlp_pre=norm_mlp_pre,
                norm_mlp_post=norm_mlp_post, b_gate=b_gate, conv_b=conv_b, lru_w_a=lru_w_a, lru_b_a=lru_b_a,
                lru_w_x=lru_w_x, lru_b_x=lru_b_x, lru_lambda=lru_lambda, pool_w=pool_w, pool_scale=pool_scale)
    ms = dict(norm_mix_pre=m_norm_mix_pre, norm_mix_post=m_norm_mix_post, norm_mlp_pre=m_norm_mlp_pre,
              norm_mlp_post=m_norm_mlp_post, b_gate=m_b_gate, conv_b=m_conv_b, lru_w_a=m_lru_w_a,
              lru_b_a=m_lru_b_a, lru_w_x=m_lru_w_x, lru_b_x=m_lru_b_x, lru_lambda=m_lru_lambda,
              pool_w=m_pool_w, pool_scale=m_pool_scale)
    vs = dict(norm_mix_pre=v_norm_mix_pre, norm_mix_post=v_norm_mix_post, norm_mlp_pre=v_norm_mlp_pre,
              norm_mlp_post=v_norm_mlp_post, b_gate=v_b_gate, conv_b=v_conv_b, lru_w_a=v_lru_w_a,
              lru_b_a=v_lru_b_a, lru_w_x=v_lru_w_x, lru_b_x=v_lru_b_x, lru_lambda=v_lru_lambda,
              pool_w=v_pool_w, pool_scale=v_pool_scale)
    small_w = pack([args[n] for n in small_names])
    small_m = pack([ms[n] for n in small_names])
    small_v = pack([vs[n] for n in small_names])
    small_g = pack([g_small[n] for n in small_names])
    sd, snm, snv = _adamw(small_w, small_g, small_m, small_v, "adamw_small")
    full_shapes = [args[n].shape for n in small_names]
    delta = dict(zip(small_names, unpack(sd, full_shapes)))
    new_m = dict(zip(small_names, unpack(snm, full_shapes)))
    new_v = dict(zip(small_names, unpack(snv, full_shapes)))
    grads = {n: g_small[n].reshape(args[n].shape) for n in small_names}

    def big_adam(name, w, g, mm, vv):
        dl, nm, nv = _adamw(w[0], g, mm[0], vv[0], "adamw_" + name)
        grads[name], delta[name], new_m[name], new_v[name] = g[None], dl[None], nm[None], nv[None]

    big_adam("w_in", w_in, g_w_in, m_w_in, v_w_in)
    big_adam("conv_w", conv_w, g_conv_w, m_conv_w, v_conv_w)
    big_adam("w_lru_up", w_lru_up, g_w_lru_up, m_w_lru_up, v_w_lru_up)
    big_adam("w_pool_up", w_pool_up, g_w_pool_up, m_w_pool_up, v_w_pool_up)
    big_adam("w_o", w_o, g_w_o, m_w_o, v_w_o)
    big_adam("w_ff1", w_ff1, g_w_ff1, m_w_ff1, v_w_ff1)
    big_adam("w_ff2", w_ff2, g_w_ff2, m_w_ff2, v_w_ff2)

    loss = lax.psum(loss_part[0, 0], ("x", "y", "c"))
    order = ["norm_mix_pre", "norm_mix_post", "norm_mlp_pre", "norm_mlp_post", "w_in", "b_gate", "conv_w",
             "conv_b", "lru_w_a", "lru_b_a", "lru_w_x", "lru_b_x", "lru_lambda", "pool_w", "pool_scale",
             "w_lru_up", "w_pool_up", "w_o", "w_ff1", "w_ff2"]
    return (loss, grad_x[None], *[grads[n] for n in order], *[delta[n] for n in order],
            *[new_m[n] for n in order], *[new_v[n] for n in order])
```

```python
import functools
import math
import operator
import types

import jax
import jax.numpy as jnp
from jax import lax
from jax.experimental import pallas as pl
from jax.experimental.pallas import tpu as pltpu

F32 = jnp.float32
BF16 = jnp.bfloat16
NORM_EPS = 1e-6
LRU_C = 8.0
N_LRU_HEADS = 16
LRU_HEAD_DIM = 64
POOL_WINDOWS = (2, 4, 8, 16)
POOL_GROUP_DIM = 128
ADAM_LR = 0.001
ADAM_B1 = 0.9
ADAM_B2 = 0.999
ADAM_EPS = 1e-08
ADAM_WD = 0.01
ADAM_STEP = 10
N_DEV = 8
V7X_VMEM_LIMIT_BYTES = 48 * 1024 * 1024
LRU_CB = 256
MESH = pl.DeviceIdType.MESH
ANY = pl.BlockSpec(memory_space=pl.ANY)


def _tile(n, pref):
    t = min(n, pref)
    assert n % t == 0, (n, pref)
    return t


def _dot_nn(a, b):
    return lax.dot_general(a, b, (((1,), (0,)), ((), ())), preferred_element_type=F32)


def _dot_nt(a, b):
    return lax.dot_general(a, b, (((1,), (1,)), ((), ())), preferred_element_type=F32)


def _dot_tn(a, b):
    return lax.dot_general(a, b, (((0,), (0,)), ((), ())), preferred_element_type=F32)


def _row_chunks(n_rows, fn, chunk=256):
    chunk = min(chunk, n_rows)
    assert n_rows % chunk == 0

    def step(r, carry):
        fn(pl.ds(pl.multiple_of(r * chunk, chunk), chunk))
        return carry

    lax.fori_loop(0, n_rows // chunk, step, 0)


def _sig(x):
    return 1.0 / (1.0 + jnp.exp(-x))


def _rms_hat(x):
    r = lax.rsqrt(jnp.mean(x * x, axis=-1, keepdims=True) + NORM_EPS)
    return x * r, r


def _rms_bwd(dn, xhat, r, g):
    q = dn * g
    dx = r * (q - xhat * jnp.mean(q * xhat, axis=-1, keepdims=True))
    dg = jnp.sum(dn * xhat, axis=0, keepdims=True)
    return dx, dg


_GELU_K = math.sqrt(2.0 / math.pi)
_GELU_C = 0.044715


def _gelu_and_grad(g):
    t = jnp.tanh(_GELU_K * (g + _GELU_C * g * g * g))
    val = 0.5 * g * (1.0 + t)
    grad = 0.5 * (1.0 + t) + 0.5 * g * (1.0 - t * t) * (_GELU_K * (1.0 + 3.0 * _GELU_C * g * g))
    return val, grad


def _softplus_neg(lam):
    z = -lam
    e = jnp.exp(-jnp.abs(z))
    u = 1.0 + e
    d = u - 1.0
    l1p = jnp.where(d == 0.0, e, jnp.log(u) * (e / jnp.where(d == 0.0, 1.0, d)))
    return jnp.maximum(z, 0.0) + l1p


def _lru_gates(xc, wa, ba, wx, bx, lam):
    xcb = xc.astype(BF16)
    r = _sig(_dot_nn(xcb, wa) + ba)
    i = _sig(_dot_nn(xcb, wx) + bx)
    sp = _softplus_neg(lam)
    log_a = (-LRU_C) * r * sp
    a = jnp.exp(log_a)
    mult = jnp.sqrt(-jnp.tanh(log_a) * (1.0 + a * a))
    return xcb, r, i, sp, log_a, a, mult


def _place():
    return lax.axis_index("x"), lax.axis_index("y"), lax.axis_index("c")


def _ag_plan(shards):
    na = len(shards)

    def parts(ins, outs, sems):
        send_sems, recv_sems, local_sems = sems
        x, y, c = _place()
        me, sibling = (x, y, c), (x, y, 1 - c)
        chips = [(1 - x, y), (x, 1 - y), (1 - x, 1 - y)]

        def slot(a, px, py, pc):
            return outs[a].at[4 * px + 2 * py + pc]

        def copy(a, k, block, to, src=None):
            return pltpu.make_async_remote_copy(
                src_ref=slot(a, *block) if src is None else src,
                dst_ref=slot(a, *block),
                send_sem=send_sems.at[a * 7 + k],
                recv_sem=recv_sems.at[a * 7 + k],
                device_id=to,
                device_id_type=MESH,
            )

        mine = [pltpu.make_async_copy(ins[a], slot(a, *me), local_sems.at[a]) for a in range(na)]
        first = []
        for a in range(na):
            first.append(copy(a, 0, me, sibling, src=ins[a]))
            first += [copy(a, 1 + j, me, (*chip, c), src=ins[a]) for j, chip in enumerate(chips)]
        return me, sibling, chips, c, copy, mine, first

    def start(ins, outs, sems):
        _, _, _, _, _, mine, first = parts(ins, outs, sems)
        for cp in mine + first:
            cp.start()

    def finish(ins, outs, sems):
        me, sibling, chips, c, copy, mine, first = parts(ins, outs, sems)
        passed = []
        for j, chip in enumerate(chips):
            for a in range(na):
                copy(a, 1 + j, (*chip, c), me).wait_recv()
                fwd = copy(a, 4 + j, (*chip, c), sibling)
                fwd.start()
                passed.append(fwd)
        for a in range(na):
            copy(a, 0, sibling, me).wait_recv()
            for j, chip in enumerate(chips):
                copy(a, 4 + j, (*chip, 1 - c), me).wait_recv()
        for cp in first + passed:
            cp.wait_send()
        for cp in mine:
            cp.wait()

    return types.SimpleNamespace(
        ins=list(shards),
        out_shapes=[jax.ShapeDtypeStruct((N_DEV,) + s.shape, s.dtype) for s in shards],
        sems=[pltpu.SemaphoreType.DMA((7 * na,)), pltpu.SemaphoreType.DMA((7 * na,)),
              pltpu.SemaphoreType.DMA((na,))],
        start=start, finish=finish)


def _rs_sibling_plan(fulls):
    na = len(fulls)
    rs = [f.shape[0] // N_DEV for f in fulls]

    def copies(ins, outs, sems):
        send_sems, recv_sems = sems
        x, y, c = _place()
        out = []
        for a in range(na):
            for q in range(4):
                shard = 2 * q + (1 - c)
                out.append(pltpu.make_async_remote_copy(
                    src_ref=ins[a].at[pl.ds(shard * rs[a], rs[a]), :],
                    dst_ref=outs[a].at[q],
                    send_sem=send_sems.at[a * 4 + q],
                    recv_sem=recv_sems.at[a * 4 + q],
                    device_id=(x, y, 1 - c),
                    device_id_type=MESH,
                ))
        return out

    def start(ins, outs, sems):
        for cp in copies(ins, outs, sems):
            cp.start()

    def finish(ins, outs, sems):
        for cp in copies(ins, outs, sems):
            cp.wait()

    return types.SimpleNamespace(
        ins=list(fulls),
        out_shapes=[jax.ShapeDtypeStruct((4, r, f.shape[1]), f.dtype) for r, f in zip(rs, fulls)],
        sems=[pltpu.SemaphoreType.DMA((4 * na,)), pltpu.SemaphoreType.DMA((4 * na,))],
        start=start, finish=finish)


def _rs_chips_plan(sends):
    na = len(sends)

    def copies(ins, outs, sems):
        send_sems, recv_sems = sems
        x, y, c = _place()
        chips = [(1 - x, y), (x, 1 - y), (1 - x, 1 - y)]
        out = []
        for a in range(na):
            for k, chip in enumerate(chips):
                out.append(pltpu.make_async_remote_copy(
                    src_ref=ins[a].at[k],
                    dst_ref=outs[a].at[k],
                    send_sem=send_sems.at[a * 3 + k],
                    recv_sem=recv_sems.at[a * 3 + k],
                    device_id=(*chip, c),
                    device_id_type=MESH,
                ))
        return out

    def start(ins, outs, sems):
        for cp in copies(ins, outs, sems):
            cp.start()

    def finish(ins, outs, sems):
        for cp in copies(ins, outs, sems):
            cp.wait()

    return types.SimpleNamespace(
        ins=list(sends),
        out_shapes=[jax.ShapeDtypeStruct(s.shape, s.dtype) for s in sends],
        sems=[pltpu.SemaphoreType.DMA((3 * na,)), pltpu.SemaphoreType.DMA((3 * na,))],
        start=start, finish=finish)


def _run_plan(plan, name):
    n_in, n_out = len(plan.ins), len(plan.out_shapes)

    def body(*refs):
        ins, outs, sems = refs[:n_in], refs[n_in:n_in + n_out], refs[n_in + n_out:]
        plan.start(ins, outs, sems)
        plan.finish(ins, outs, sems)

    return pl.pallas_call(
        body,
        name=name,
        in_specs=[ANY] * n_in,
        out_specs=[ANY] * n_out,
        out_shape=plan.out_shapes,
        scratch_shapes=plan.sems,
    )(*plan.ins)


def _call(body, *, name, grid, in_specs, out_specs, out_shape, args, scratch_shapes=(), aliases=None,
          carry=None):
    n_in, n_out, n_scr = len(in_specs), len(out_shape), len(scratch_shapes)
    params = pltpu.CompilerParams(
        dimension_semantics=("arbitrary",) * len(grid), vmem_limit_bytes=V7X_VMEM_LIMIT_BYTES)
    if carry is None:
        outs = pl.pallas_call(
            body, name=name, grid=grid, in_specs=list(in_specs), out_specs=list(out_specs),
            out_shape=list(out_shape), scratch_shapes=list(scratch_shapes),
            input_output_aliases=aliases or {}, compiler_params=params)(*args)
        return list(outs), []
    c_in, c_out = len(carry.ins), len(carry.out_shapes)

    def full(*refs):
        p = 0
        ins = refs[p:p + n_in]
        p += n_in
        cins = refs[p:p + c_in]
        p += c_in
        outs = refs[p:p + n_out]
        p += n_out
        couts = refs[p:p + c_out]
        p += c_out
        scr = refs[p:p + n_scr]
        csems = refs[p + n_scr:]
        ids = [pl.program_id(a) for a in range(len(grid))]
        first = functools.reduce(operator.and_, [i == 0 for i in ids])
        last = functools.reduce(operator.and_, [i == g - 1 for i, g in zip(ids, grid)])

        @pl.when(first)
        def _():
            carry.start(cins, couts, csems)

        body(*ins, *outs, *scr)

        @pl.when(last)
        def _():
            carry.finish(cins, couts, csems)

    outs = pl.pallas_call(
        full, name=name, grid=grid,
        in_specs=list(in_specs) + [ANY] * c_in,
        out_specs=list(out_specs) + [ANY] * c_out,
        out_shape=list(out_shape) + list(carry.out_shapes),
        scratch_shapes=list(scratch_shapes) + list(carry.sems),
        input_output_aliases=aliases or {}, compiler_params=params)(*args, *carry.ins)
    return list(outs[:n_out]), list(outs[n_out:])


def _norm_proj(x, g1, w_int, carry=None):
    t, d = x.shape
    n = w_int.shape[0]
    tt, tn = _tile(t, 2048), _tile(n, 512)

    def body(x_ref, g_ref, w_ref, proj_ref, h1_ref, h1_s):
        @pl.when(pl.program_id(1) == 0)
        def _():
            def norm_rows(rows):
                xhat, _ = _rms_hat(x_ref[rows, :])
                h = (xhat * g_ref[...]).astype(BF16)
                h1_s[rows, :] = h
                h1_ref[rows, :] = h

            _row_chunks(tt, norm_rows)

        proj_ref[...] = _dot_nt(h1_s[...], w_ref[...])

    return _call(
        body, name="norm_proj", grid=(t // tt, n // tn),
        in_specs=[
            pl.BlockSpec((tt, d), lambda i, j: (i, 0)),
            pl.BlockSpec((1, d), lambda i, j: (0, 0)),
            pl.BlockSpec((tn, d), lambda i, j: (j, 0)),
        ],
        out_specs=[
            pl.BlockSpec((tt, tn), lambda i, j: (i, j)),
            pl.BlockSpec((tt, d), lambda i, j: (i, 0)),
        ],
        out_shape=[jax.ShapeDtypeStruct((t, n), F32), jax.ShapeDtypeStruct((t, d), BF16)],
        scratch_shapes=[pltpu.VMEM((tt, d), BF16)],
        args=(x, g1, w_int), carry=carry)


def _lru_fwd(proj, conv_w, conv_b, wa_bd, b_a, wx_bd, b_x, lam, carry=None):
    t = proj.shape[0]
    dr = conv_b.shape[1]
    cb = LRU_CB
    tc = _tile(t, 256)
    ncb, ntc = dr // cb, t // tc

    def body(xp_ref, g_ref, cw_ref, cb_ref, wa_ref, ba_ref, wx_ref, bx_ref, lam_ref,
             y_ref, h_ref, xc_ref, prevx_s, hlast_s):
        c = pl.program_id(1)

        @pl.when(c == 0)
        def _():
            prevx_s[...] = jnp.zeros_like(prevx_s)
            hlast_s[...] = jnp.zeros_like(hlast_s)

        x = xp_ref[...]
        prev = prevx_s[...]
        row = lax.broadcasted_iota(jnp.int32, x.shape, 0)

        def sh(j):
            return jnp.where(row >= j, pltpu.roll(x, j, 0), pltpu.roll(prev, j, 0))

        xc = (cb_ref[...] + cw_ref[0:1, :] * sh(3) + cw_ref[1:2, :] * sh(2)
              + cw_ref[2:3, :] * sh(1) + cw_ref[3:4, :] * x)
        prevx_s[...] = x
        xc_ref[...] = xc
        _, _, i, _, _, a, mult = _lru_gates(xc, wa_ref[...], ba_ref[...], wx_ref[...], bx_ref[...],
                                            lam_ref[...])
        av, bv = a, mult * (i * xc)
        s = 1
        while s < tc:
            a_sh = jnp.where(row >= s, pltpu.roll(av, s, 0), 1.0)
            b_sh = jnp.where(row >= s, pltpu.roll(bv, s, 0), 0.0)
            bv = av * b_sh + bv
            av = av * a_sh
            s *= 2
        h = av * hlast_s[...] + bv
        h_ref[...] = h
        hlast_s[...] = h_ref[tc - 1:tc, :]
        gel, _ = _gelu_and_grad(g_ref[...])
        y_ref[...] = (h * gel).astype(BF16)

    vec = pl.BlockSpec((1, cb), lambda j, c: (0, j))
    blk = pl.BlockSpec((tc, cb), lambda j, c: (c, j))
    mat = pl.BlockSpec((None, cb, cb), lambda j, c: (j, 0, 0))
    return _call(
        body, name="lru_fwd", grid=(ncb, ntc),
        in_specs=[
            blk,
            pl.BlockSpec((tc, cb), lambda j, c: (c, ncb + j)),
            pl.BlockSpec((4, cb), lambda j, c: (0, j)),
            vec, mat, vec, mat, vec, vec,
        ],
        out_specs=[blk, blk, blk],
        out_shape=[
            jax.ShapeDtypeStruct((t, dr), BF16),
            jax.ShapeDtypeStruct((t, dr), F32),
            jax.ShapeDtypeStruct((t, dr), F32),
        ],
        scratch_shapes=[pltpu.VMEM((tc, cb), F32), pltpu.VMEM((1, cb), F32)],
        args=(proj, proj, conv_w, conv_b, wa_bd, b_a, wx_bd, b_x, lam), carry=carry)


def _pool_select(col, vals):
    out = vals[3]
    for g in (2, 1, 0):
        out = jnp.where(col < (g + 1) * POOL_GROUP_DIM, vals[g], out)
    return out


def _pool_fwd(proj, pool_w, pool_scale, col_block):
    t = proj.shape[0]
    dp = pool_scale.shape[1]
    tc = _tile(t, 256)
    ntc = t // tc

    def body(x_ref, w_ref, sc_ref, y_ref, p_ref, px, p2, p4, p8):
        c = pl.program_id(0)

        @pl.when(c == 0)
        def _():
            for s in (px, p2, p4, p8):
                s[...] = jnp.zeros_like(s)

        x = x_ref[...]
        row = lax.broadcasted_iota(jnp.int32, x.shape, 0)
        col = lax.broadcasted_iota(jnp.int32, x.shape, 1)

        def sh(v, pv, j):
            return jnp.where(row >= j, pltpu.roll(v, j, 0), pltpu.roll(pv[...], j, 0))

        s2 = x + sh(x, px, 1)
        s4 = s2 + sh(s2, p2, 2)
        s8 = s4 + sh(s4, p4, 4)
        s16 = s8 + sh(s8, p8, 8)
        px[...] = x
        p2[...] = s2
        p4[...] = s4
        p8[...] = s8
        wsum = _pool_select(col, (s2, s4, s8, s16))
        win = _pool_select(col, POOL_WINDOWS)
        cnt = jnp.minimum(c * tc + row + 1, win).astype(F32)
        p = wsum / cnt - x
        pb = p.astype(BF16)
        p_ref[...] = pb
        for g in range(len(POOL_WINDOWS)):
            sl = slice(g * POOL_GROUP_DIM, (g + 1) * POOL_GROUP_DIM)
            yg = _dot_nn(pb[:, sl], w_ref[g]) * sc_ref[:, sl]
            y_ref[:, sl] = yg.astype(BF16)

    return _call(
        body, name="pool_fwd", grid=(ntc,),
        in_specs=[
            pl.BlockSpec((tc, dp), lambda c: (c, col_block)),
            pl.BlockSpec(pool_w.shape, lambda c: (0, 0, 0)),
            pl.BlockSpec((1, dp), lambda c: (0, 0)),
        ],
        out_specs=[pl.BlockSpec((tc, dp), lambda c: (c, 0))] * 2,
        out_shape=[jax.ShapeDtypeStruct((t, dp), BF16)] * 2,
        scratch_shapes=[pltpu.VMEM((tc, dp), F32)] * 4,
        args=(proj, pool_w, pool_scale))[0]


def _branch_mix(y_lru, y_pool, w_lru_up, w_pool_upt, proj, b_gate, ga_block, gb_block):
    t, d = y_lru.shape
    dp = y_pool.shape[1]
    tt, tn = _tile(t, 1024), 512
    nj = d // tn

    def body(yl_ref, yp_ref, wl_ref, wp_ref, ga_ref, gb_ref, ba_ref, bb_ref, bra_ref, brb_ref, mix_ref):
        br_a = _dot_nn(yl_ref[...], wl_ref[...])
        br_b = _dot_nt(yp_ref[...], wp_ref[...])
        bra_ref[...] = br_a
        brb_ref[...] = br_b
        ga = _sig(ga_ref[...] + ba_ref[...])
        gb = _sig(gb_ref[...] + bb_ref[...])
        mix_ref[...] = (ga * br_a + gb * br_b).astype(BF16)

    out = pl.BlockSpec((tt, tn), lambda j, i: (i, j))
    return _call(
        body, name="branch_mix", grid=(nj, t // tt),
        in_specs=[
            pl.BlockSpec((tt, d), lambda j, i: (i, 0)),
            pl.BlockSpec((tt, dp), lambda j, i: (i, 0)),
            pl.BlockSpec((d, tn), lambda j, i: (0, j)),
            pl.BlockSpec((tn, dp), lambda j, i: (j, 0)),
            pl.BlockSpec((tt, tn), lambda j, i: (i, ga_block + j)),
            pl.BlockSpec((tt, tn), lambda j, i: (i, gb_block + j)),
            pl.BlockSpec((1, tn), lambda j, i: (0, j)),
            pl.BlockSpec((1, tn), lambda j, i: (0, nj + j)),
        ],
        out_specs=[out, out, out],
        out_shape=[
            jax.ShapeDtypeStruct((t, d), F32),
            jax.ShapeDtypeStruct((t, d), F32),
            jax.ShapeDtypeStruct((t, d), BF16),
        ],
        args=(y_lru, y_pool, w_lru_up, w_pool_upt, proj, proj, b_gate, b_gate))[0]


def _wo_norm(mix, w_o, x, g2, g3):
    t, d = x.shape
    tt = _tile(t, 512)

    def body(mix_ref, w_ref, x_ref, g2_ref, g3_ref, m_ref, x2_ref, h3_ref):
        m = _dot_nn(mix_ref[...], w_ref[...])
        m_ref[...] = m
        mhat, _ = _rms_hat(m)
        x2 = x_ref[...] + mhat * g2_ref[...]
        x2_ref[...] = x2
        xhat, _ = _rms_hat(x2)
        h3_ref[...] = (xhat * g3_ref[...]).astype(BF16)

    row = pl.BlockSpec((tt, d), lambda i: (i, 0))
    vec = pl.BlockSpec((1, d), lambda i: (0, 0))
    return _call(
        body, name="wo_norm", grid=(t // tt,),
        in_specs=[row, pl.BlockSpec((d, d), lambda i: (0, 0)), row, vec, vec],
        out_specs=[row, row, row],
        out_shape=[
            jax.ShapeDtypeStruct((t, d), F32),
            jax.ShapeDtypeStruct((t, d), F32),
            jax.ShapeDtypeStruct((t, d), BF16),
        ],
        args=(mix, w_o, x, g2, g3))[0]


def _ff1(h3, w_ff1t, carry=None):
    t, d = h3.shape
    n = w_ff1t.shape[0]
    tt, tn = _tile(t, 2048), _tile(n, 512)

    def body(h_ref, w_ref, rf_ref, act_ref):
        rf = jnp.maximum(_dot_nt(h_ref[...], w_ref[...]), 0.0)
        rf_ref[...] = rf.astype(BF16)
        act_ref[...] = (rf * rf).astype(BF16)

    out = pl.BlockSpec((tt, tn), lambda i, j: (i, j))
    return _call(
        body, name="ff1", grid=(t // tt, n // tn),
        in_specs=[pl.BlockSpec((tt, d), lambda i, j: (i, 0)), pl.BlockSpec((tn, d), lambda i, j: (j, 0))],
        out_specs=[out, out],
        out_shape=[jax.ShapeDtypeStruct((t, n), BF16)] * 2,
        args=(h3, w_ff1t), carry=carry)


def _ff2_loss(act, w_ff2, x2, g4, target):
    t, k = act.shape
    d = x2.shape[1]
    tt, tk = _tile(t, 1024), _tile(k, 512)
    nk = k // tk

    def body(a_ref, w_ref, x2_ref, g_ref, tg_ref, dy_ref, df_ref, dg_ref, loss_ref, acc):
        i, kk = pl.program_id(0), pl.program_id(1)

        @pl.when(kk == 0)
        def _():
            acc[...] = jnp.zeros_like(acc)

        @pl.when((i == 0) & (kk == 0))
        def _():
            dg_ref[...] = jnp.zeros_like(dg_ref)
            loss_ref[...] = jnp.zeros_like(loss_ref)

        acc[...] += _dot_nn(a_ref[...], w_ref[...])

        @pl.when(kk == nk - 1)
        def _():
            def tail(rows):
                fhat, r = _rms_hat(acc[rows, :])
                g = g_ref[...]
                e = x2_ref[rows, :] + fhat * g - tg_ref[rows, :]
                loss_ref[...] += 0.5 * jnp.sum(jnp.mean(e * e, axis=-1, keepdims=True))
                dy = e * (1.0 / d)
                dy_ref[rows, :] = dy
                df, dg = _rms_bwd(dy, fhat, r, g)
                df_ref[rows, :] = df.astype(BF16)
                dg_ref[...] += dg

            _row_chunks(tt, tail)

    row = pl.BlockSpec((tt, d), lambda i, kk: (i, 0))
    vec = pl.BlockSpec((1, d), lambda i, kk: (0, 0))
    return _call(
        body, name="ff2_loss", grid=(t // tt, nk),
        in_specs=[
            pl.BlockSpec((tt, tk), lambda i, kk: (i, kk)),
            pl.BlockSpec((tk, d), lambda i, kk: (kk, 0)),
            row, vec, row,
        ],
        out_specs=[row, row, vec, pl.BlockSpec((1, 128), lambda i, kk: (0, 0))],
        out_shape=[
            jax.ShapeDtypeStruct((t, d), F32),
            jax.ShapeDtypeStruct((t, d), BF16),
            jax.ShapeDtypeStruct((1, d), F32),
            jax.ShapeDtypeStruct((1, 128), F32),
        ],
        scratch_shapes=[pltpu.VMEM((tt, d), F32)],
        args=(act, w_ff2, x2, g4, target))[0]


def _ff2_bwd(df, w_ff2, rf):
    t, d = df.shape
    n = w_ff2.shape[0]
    tt, tn = _tile(t, 2048), _tile(n, 512)

    def body(df_ref, w_ref, rf_ref, out_ref):
        d_act = _dot_nt(df_ref[...], w_ref[...])
        out_ref[...] = (d_act * (2.0 * rf_ref[...].astype(F32))).astype(BF16)

    blk = pl.BlockSpec((tt, tn), lambda i, j: (i, j))
    return _call(
        body, name="ff2_bwd", grid=(t // tt, n // tn),
        in_specs=[pl.BlockSpec((tt, d), lambda i, j: (i, 0)), pl.BlockSpec((tn, d), lambda i, j: (j, 0)), blk],
        out_specs=[blk],
        out_shape=[jax.ShapeDtypeStruct((t, n), BF16)],
        args=(df, w_ff2, rf))[0][0]


def _wgrad(a, b, name, prev=None, row_off=0, rows=None, carry=None):
    t, m = a.shape
    n = b.shape[1]
    rows = m if rows is None else rows
    tm, tk = _tile(m, 512), _tile(t, 2048)
    nk = t // tk
    assert row_off % tm == 0
    off = row_off // tm

    def body(*refs):
        a_ref, b_ref = refs[0], refs[1]
        o32_ref, o16_ref, acc = refs[-3], refs[-2], refs[-1]
        kk = pl.program_id(1)

        @pl.when(kk == 0)
        def _():
            acc[...] = jnp.zeros_like(acc)

        acc[...] += _dot_tn(a_ref[...], b_ref[...])

        @pl.when(kk == nk - 1)
        def _():
            o32_ref[...] = acc[...]
            o16_ref[...] = acc[...].astype(BF16)

    in_specs = [pl.BlockSpec((tk, tm), lambda i, kk: (kk, i)), pl.BlockSpec((tk, n), lambda i, kk: (kk, 0))]
    args = [a, b]
    aliases = {}
    if prev is not None:
        in_specs += [ANY, ANY]
        args += list(prev)
        aliases = {2: 0, 3: 1}
    out = pl.BlockSpec((tm, n), lambda i, kk: (off + i, 0))
    return _call(
        body, name=name, grid=(m // tm, nk),
        in_specs=in_specs, out_specs=[out, out],
        out_shape=[jax.ShapeDtypeStruct((rows, n), F32), jax.ShapeDtypeStruct((rows, n), BF16)],
        scratch_shapes=[pltpu.VMEM((tm, n), F32)],
        aliases=aliases, args=args, carry=carry)


def _ff1_bwd_norms(d_f1, w_ff1t, dy, x2, g3, m, g2, carry=None):
    t, k = d_f1.shape
    d = x2.shape[1]
    tt, tk = _tile(t, 512), _tile(k, 512)
    nk = k // tk

    def body(a_ref, w_ref, dy_ref, x2_ref, g3_ref, m_ref, g2_ref, dx2_ref, dm_ref, dg3_ref, dg2_ref, acc):
        i, kk = pl.program_id(0), pl.program_id(1)

        @pl.when(kk == 0)
        def _():
            acc[...] = jnp.zeros_like(acc)

        @pl.when((i == 0) & (kk == 0))
        def _():
            dg3_ref[...] = jnp.zeros_like(dg3_ref)
            dg2_ref[...] = jnp.zeros_like(dg2_ref)

        acc[...] += _dot_nn(a_ref[...], w_ref[...])

        @pl.when(kk == nk - 1)
        def _():
            def tail(rows):
                xhat, r3 = _rms_hat(x2_ref[rows, :])
                dx, dg3 = _rms_bwd(acc[rows, :], xhat, r3, g3_ref[...])
                dx2 = dy_ref[rows, :] + dx
                dx2_ref[rows, :] = dx2
                dg3_ref[...] += dg3
                mhat, r2 = _rms_hat(m_ref[rows, :])
                dm, dg2 = _rms_bwd(dx2, mhat, r2, g2_ref[...])
                dm_ref[rows, :] = dm.astype(BF16)
                dg2_ref[...] += dg2

            _row_chunks(tt, tail)

    row = pl.BlockSpec((tt, d), lambda i, kk: (i, 0))
    vec = pl.BlockSpec((1, d), lambda i, kk: (0, 0))
    return _call(
        body, name="ff1_bwd_norms", grid=(t // tt, nk),
        in_specs=[
            pl.BlockSpec((tt, tk), lambda i, kk: (i, kk)),
            pl.BlockSpec((tk, d), lambda i, kk: (kk, 0)),
            row, row, vec, row, vec,
        ],
        out_specs=[row, row, vec, vec],
        out_shape=[
            jax.ShapeDtypeStruct((t, d), F32),
            jax.ShapeDtypeStruct((t, d), BF16),
            jax.ShapeDtypeStruct((1, d), F32),
            jax.ShapeDtypeStruct((1, d), F32),
        ],
        scratch_shapes=[pltpu.VMEM((tt, d), F32)],
        args=(d_f1, w_ff1t, dy, x2, g3, m, g2), carry=carry)


def _wo_bwd_mix(dm, w_o, br_a, br_b, proj, b_gate, ga_block, gb_block):
    t, d = dm.shape
    tt, tn = _tile(t, 1024), 512
    nj = d // tn

    def body(dm_ref, w_ref, bra_ref, brb_ref, ga_ref, gb_ref, ba_ref, bb_ref,
             dbra_ref, dbrb_ref, dga_ref, dgb_ref, dba_ref, dbb_ref):
        i = pl.program_id(1)

        @pl.when(i == 0)
        def _():
            dba_ref[...] = jnp.zeros_like(dba_ref)
            dbb_ref[...] = jnp.zeros_like(dbb_ref)

        d_mix = _dot_nt(dm_ref[...], w_ref[...])
        ga = _sig(ga_ref[...] + ba_ref[...])
        gb = _sig(gb_ref[...] + bb_ref[...])
        dbra_ref[...] = (d_mix * ga).astype(BF16)
        dbrb_ref[...] = (d_mix * gb).astype(BF16)
        dga = d_mix * bra_ref[...] * (ga * (1.0 - ga))
        dgb = d_mix * brb_ref[...] * (gb * (1.0 - gb))
        dga_ref[...] = dga.astype(BF16)
        dgb_ref[...] = dgb.astype(BF16)
        dba_ref[...] += jnp.sum(dga, axis=0, keepdims=True)
        dbb_ref[...] += jnp.sum(dgb, axis=0, keepdims=True)

    blk = pl.BlockSpec((tt, tn), lambda j, i: (i, j))
    vec = pl.BlockSpec((1, tn), lambda j, i: (0, j))
    return _call(
        body, name="wo_bwd_mix", grid=(nj, t // tt),
        in_specs=[
            pl.BlockSpec((tt, d), lambda j, i: (i, 0)),
            pl.BlockSpec((tn, d), lambda j, i: (j, 0)),
            blk, blk,
            pl.BlockSpec((tt, tn), lambda j, i: (i, ga_block + j)),
            pl.BlockSpec((tt, tn), lambda j, i: (i, gb_block + j)),
            vec,
            pl.BlockSpec((1, tn), lambda j, i: (0, nj + j)),
        ],
        out_specs=[blk, blk, blk, blk, vec, vec],
        out_shape=[jax.ShapeDtypeStruct((t, d), BF16)] * 4 + [jax.ShapeDtypeStruct((1, d), F32)] * 2,
        args=(dm, w_o, br_a, br_b, proj, proj, b_gate, b_gate))[0]


def _lru_up_bwd(d_br_a, w_lru_up, proj, h, g_block):
    t, d = d_br_a.shape
    tt, tn = _tile(t, 1024), 512

    def body(a_ref, w_ref, g_ref, h_ref, dh_ref, dg_ref):
        d_y = _dot_nt(a_ref[...], w_ref[...])
        gel, gel_grad = _gelu_and_grad(g_ref[...])
        dh_ref[...] = d_y * gel
        dg_ref[...] = (d_y * h_ref[...] * gel_grad).astype(BF16)

    blk = pl.BlockSpec((tt, tn), lambda i, j: (i, j))
    return _call(
        body, name="lru_up_bwd", grid=(t // tt, d // tn),
        in_specs=[
            pl.BlockSpec((tt, d), lambda i, j: (i, 0)),
            pl.BlockSpec((tn, d), lambda i, j: (j, 0)),
            pl.BlockSpec((tt, tn), lambda i, j: (i, g_block + j)),
            blk,
        ],
        out_specs=[blk, blk],
        out_shape=[jax.ShapeDtypeStruct((t, d), F32), jax.ShapeDtypeStruct((t, d), BF16)],
        args=(d_br_a, w_lru_up, proj, h))[0]


def _pool_up_bwd(d_br_b, w_pool_upt):
    t, d = d_br_b.shape
    dp = w_pool_upt.shape[1]
    tt = _tile(t, 2048)

    def body(a_ref, w_ref, out_ref):
        out_ref[...] = _dot_nn(a_ref[...], w_ref[...])

    return _call(
        body, name="pool_up_bwd", grid=(t // tt,),
        in_specs=[pl.BlockSpec((tt, d), lambda i: (i, 0)), pl.BlockSpec((d, dp), lambda i: (0, 0))],
        out_specs=[pl.BlockSpec((tt, dp), lambda i: (i, 0))],
        out_shape=[jax.ShapeDtypeStruct((t, dp), F32)],
        args=(d_br_b, w_pool_upt))[0][0]


def _lru_bwd(dh, xc, h, proj, conv_w, wa_bd, b_a, wx_bd, b_x, lam, carry=None):
    t, dr = dh.shape
    cb = LRU_CB
    tc = _tile(t, 256)
    ncb, ntc = dr // cb, t // tc

    def body(dh_ref, xc_ref, h_ref, hp_ref, xp_ref, cw_ref, wa_ref, ba_ref, wx_ref, bx_ref, lam_ref,
             dxp_ref, dwa_ref, dba_ref, dwx_ref, dbx_ref, dlam_ref, dcw_ref, dcb_ref,
             nextd_s, anext_s, gnext_s, tmp_s):
        c = pl.program_id(1)
        rc = ntc - 1 - c

        @pl.when(c == 0)
        def _():
            nextd_s[...] = jnp.zeros_like(nextd_s)
            anext_s[...] = jnp.zeros_like(anext_s)
            gnext_s[...] = jnp.zeros_like(gnext_s)
            for ref in (dwa_ref, dba_ref, dwx_ref, dbx_ref, dlam_ref, dcw_ref, dcb_ref):
                ref[...] = jnp.zeros_like(ref)

        xc = xc_ref[...]
        wa, wx, lam = wa_ref[...], wx_ref[...], lam_ref[...]
        xcb, r, i, sp, log_a, a, mult = _lru_gates(xc, wa, ba_ref[...], wx, bx_ref[...], lam)
        row = lax.broadcasted_iota(jnp.int32, xc.shape, 0)
        h = h_ref[...]
        hp = jnp.where(rc == 0, 0.0, hp_ref[...])
        hprev = jnp.where(row >= 1, pltpu.roll(h, 1, 0), pltpu.roll(hp, 1, 0))

        def up(v, nv, j):
            return jnp.where(row < tc - j, pltpu.roll(v, tc - j, 0), nv)

        av = up(a, anext_s[...], 1)
        bv = dh_ref[...]
        s = 1
        while s < tc:
            a_sh = up(av, 1.0, s)
            b_sh = up(bv, 0.0, s)
            bv = av * b_sh + bv
            av = av * a_sh
            s *= 2
        gt = av * gnext_s[...] + bv
        tmp_s[...] = gt
        gnext_s[...] = tmp_s[0:1, :]
        tmp_s[...] = a
        anext_s[...] = tmp_s[0:1, :]

        da = gt * hprev
        ixc = i * xc
        d_mult = gt * ixc
        d_i = gt * mult * xc
        d_xc = gt * mult * i
        d_log_a = da * a - d_mult * (a * a) / mult
        d_pre_r = (d_log_a * ((-LRU_C) * sp)) * (r * (1.0 - r))
        d_pre_i = d_i * (i * (1.0 - i))
        d_sp = jnp.sum(d_log_a * ((-LRU_C) * r), axis=0, keepdims=True)
        dlam_ref[...] += d_sp * (-1.0 / (1.0 + jnp.exp(lam)))
        dpr = d_pre_r.astype(BF16)
        dpi = d_pre_i.astype(BF16)
        dba_ref[...] += jnp.sum(d_pre_r, axis=0, keepdims=True)
        dbx_ref[...] += jnp.sum(d_pre_i, axis=0, keepdims=True)
        dwa_ref[...] += _dot_tn(xcb, dpr)
        dwx_ref[...] += _dot_tn(xcb, dpi)
        d_xc = d_xc + _dot_nt(dpr, wa) + _dot_nt(dpi, wx)

        nxt = nextd_s[...]
        xp = xp_ref[...]
        dxp = cw_ref[3:4, :] * d_xc
        dcw_ref[3:4, :] += jnp.sum(xp * d_xc, axis=0, keepdims=True)
        for j in (1, 2, 3):
            uj = up(d_xc, pltpu.roll(nxt, tc - j, 0), j)
            dxp = dxp + cw_ref[3 - j:4 - j, :] * uj
            dcw_ref[3 - j:4 - j, :] += jnp.sum(xp * uj, axis=0, keepdims=True)
        dcb_ref[...] += jnp.sum(d_xc, axis=0, keepdims=True)
        nextd_s[...] = d_xc
        dxp_ref[...] = dxp.astype(BF16)

    vec = pl.BlockSpec((1, cb), lambda j, c: (0, j))
    blk = pl.BlockSpec((tc, cb), lambda j, c: (ntc - 1 - c, j))
    mat = pl.BlockSpec((None, cb, cb), lambda j, c: (j, 0, 0))
    cwb = pl.BlockSpec((4, cb), lambda j, c: (0, j))
    return _call(
        body, name="lru_bwd", grid=(ncb, ntc),
        in_specs=[
            blk, blk, blk,
            pl.BlockSpec((tc, cb), lambda j, c: (jnp.maximum(ntc - 2 - c, 0), j)),
            blk, cwb, mat, vec, mat, vec, vec,
        ],
        out_specs=[blk, mat, vec, mat, vec, vec, cwb, vec],
        out_shape=[
            jax.ShapeDtypeStruct((t, dr), BF16),
            jax.ShapeDtypeStruct((ncb, cb, cb), F32),
            jax.ShapeDtypeStruct((1, dr), F32),
            jax.ShapeDtypeStruct((ncb, cb, cb), F32),
            jax.ShapeDtypeStruct((1, dr), F32),
            jax.ShapeDtypeStruct((1, dr), F32),
            jax.ShapeDtypeStruct((4, dr), F32),
            jax.ShapeDtypeStruct((1, dr), F32),
        ],
        scratch_shapes=[
            pltpu.VMEM((tc, cb), F32),
            pltpu.VMEM((1, cb), F32),
            pltpu.VMEM((1, cb), F32),
            pltpu.VMEM((tc, cb), F32),
        ],
        args=(dh, xc, h, h, proj, conv_w, wa_bd, b_a, wx_bd, b_x, lam), carry=carry)


def _pool_bwd(d_y_pool, p, pool_w, pool_scale):
    t, dp = d_y_pool.shape
    tc = _tile(t, 256)
    ntc = t // tc
    ng = len(POOL_WINDOWS)

    def body(dy_ref, p_ref, w_ref, sc_ref, dx_ref, dw_ref, dsc_ref, nz, n2, n4, n8, dp_s):
        c = pl.program_id(0)
        rc = ntc - 1 - c

        @pl.when(c == 0)
        def _():
            for s in (nz, n2, n4, n8):
                s[...] = jnp.zeros_like(s)
            dw_ref[...] = jnp.zeros_like(dw_ref)
            dsc_ref[...] = jnp.zeros_like(dsc_ref)

        for g in range(ng):
            sl = slice(g * POOL_GROUP_DIM, (g + 1) * POOL_GROUP_DIM)
            pg = p_ref[:, sl]
            dyg = dy_ref[:, sl]
            wg = w_ref[g].astype(BF16)
            q = _dot_nn(pg, wg)
            dsc_ref[:, sl] += jnp.sum(dyg * q, axis=0, keepdims=True)
            dpw = (dyg * sc_ref[:, sl]).astype(BF16)
            dw_ref[g] += _dot_tn(pg, dpw)
            dp_s[:, sl] = _dot_nt(dpw, wg)

        dpv = dp_s[...]
        row = lax.broadcasted_iota(jnp.int32, dpv.shape, 0)
        col = lax.broadcasted_iota(jnp.int32, dpv.shape, 1)
        win = _pool_select(col, POOL_WINDOWS)
        cnt = jnp.minimum(rc * tc + row + 1, win).astype(F32)
        z = dpv / cnt

        def up(v, nv, j):
            return jnp.where(row < tc - j, pltpu.roll(v, tc - j, 0), pltpu.roll(nv[...], tc - j, 0))

        u2 = z + up(z, nz, 1)
        u4 = u2 + up(u2, n2, 2)
        u8 = u4 + up(u4, n4, 4)
        u16 = u8 + up(u8, n8, 8)
        nz[...] = z
        n2[...] = u2
        n4[...] = u4
        n8[...] = u8
        dx_ref[...] = (_pool_select(col, (u2, u4, u8, u16)) - dpv).astype(BF16)

    blk = pl.BlockSpec((tc, dp), lambda c: (ntc - 1 - c, 0))
    full_w = pl.BlockSpec(pool_w.shape, lambda c: (0, 0, 0))
    vec = pl.BlockSpec((1, dp), lambda c: (0, 0))
    return _call(
        body, name="pool_bwd", grid=(ntc,),
        in_specs=[blk, blk, full_w, vec],
        out_specs=[blk, full_w, vec],
        out_shape=[
            jax.ShapeDtypeStruct((t, dp), BF16),
            jax.ShapeDtypeStruct(pool_w.shape, F32),
            jax.ShapeDtypeStruct((1, dp), F32),
        ],
        scratch_shapes=[pltpu.VMEM((tc, dp), F32)] * 5,
        args=(d_y_pool, p, pool_w, pool_scale))[0]


def _win_bwd_norm(parts, w_int, dx2, x, g1, carry=None):
    t, d = x.shape
    tk = 512
    tt = _tile(t, 512)
    bounds = []
    k0 = 0
    for part in parts:
        assert part.shape[1] % tk == 0
        bounds.append((k0, k0 + part.shape[1] // tk))
        k0 += part.shape[1] // tk
    nk = k0
    assert nk * tk == w_int.shape[0]
    np_ = len(parts)

    def body(*refs):
        p_refs = refs[:np_]
        w_ref, dx2_ref, x_ref, g_ref, gx_ref, dg_ref, acc = refs[np_:]
        i, kk = pl.program_id(0), pl.program_id(1)

        @pl.when(kk == 0)
        def _():
            acc[...] = jnp.zeros_like(acc)

        @pl.when((i == 0) & (kk == 0))
        def _():
            dg_ref[...] = jnp.zeros_like(dg_ref)

        for (lo, hi), p_ref in zip(bounds, p_refs):
            @pl.when((kk >= lo) & (kk < hi))
            def _(p_ref=p_ref):
                acc[...] += _dot_nn(p_ref[...], w_ref[...])

        @pl.when(kk == nk - 1)
        def _():
            def tail(rows):
                xhat, r = _rms_hat(x_ref[rows, :])
                dx, dg = _rms_bwd(acc[rows, :], xhat, r, g_ref[...])
                gx_ref[rows, :] = dx2_ref[rows, :] + dx
                dg_ref[...] += dg

            _row_chunks(tt, tail)

    def part_spec(lo, hi):
        return pl.BlockSpec((tt, tk), lambda i, kk: (i, jnp.clip(kk - lo, 0, hi - lo - 1)))

    row = pl.BlockSpec((tt, d), lambda i, kk: (i, 0))
    vec = pl.BlockSpec((1, d), lambda i, kk: (0, 0))
    return _call(
        body, name="win_bwd_norm", grid=(t // tt, nk),
        in_specs=[part_spec(lo, hi) for lo, hi in bounds]
        + [pl.BlockSpec((tk, d), lambda i, kk: (kk, 0)), row, row, vec],
        out_specs=[row, vec],
        out_shape=[jax.ShapeDtypeStruct((t, d), F32), jax.ShapeDtypeStruct((1, d), F32)],
        scratch_shapes=[pltpu.VMEM((tt, d), F32)],
        args=(*parts, w_int, dx2, x, g1), carry=carry)


def _adamw(w, g, m, v, name):
    rows, cols = w.shape
    tr = rows if rows <= 512 else _tile(rows, 256)

    def body(w_ref, g_ref, m_ref, v_ref, d_ref, nm_ref, nv_ref):
        g = g_ref[...]
        m = ADAM_B1 * m_ref[...] + (1.0 - ADAM_B1) * g
        v = ADAM_B2 * v_ref[...] + (1.0 - ADAM_B2) * (g * g)
        m_hat = m / (1.0 - ADAM_B1 ** ADAM_STEP)
        v_hat = v / (1.0 - ADAM_B2 ** ADAM_STEP)
        d_ref[...] = -ADAM_LR * (m_hat / (jnp.sqrt(v_hat) + ADAM_EPS) + ADAM_WD * w_ref[...])
        nm_ref[...] = m
        nv_ref[...] = v

    blk = pl.BlockSpec((tr, cols), lambda i: (i, 0))
    return _call(
        body, name=name, grid=(rows // tr,),
        in_specs=[blk] * 4, out_specs=[blk] * 3,
        out_shape=[jax.ShapeDtypeStruct((rows, cols), F32)] * 3,
        args=(w, g, m, v))[0]


def _rs_sum(full, recv, shard_ids, slot_ids, name):
    _, r, cols = recv.shape
    send_dtype = recv.dtype

    def body(sh_ref, sl_ref, full_ref, recv_ref, own_ref, send_ref):
        s = pl.program_id(0)
        v = full_ref[...] + recv_ref[...].astype(F32)

        @pl.when(s == 0)
        def _():
            own_ref[...] = v

        @pl.when(s > 0)
        def _():
            send_ref[...] = v.astype(send_dtype)

    grid_spec = pltpu.PrefetchScalarGridSpec(
        num_scalar_prefetch=2,
        grid=(4,),
        in_specs=[
            pl.BlockSpec((r, cols), lambda s, sh, sl: (sh[s], 0)),
            pl.BlockSpec((None, r, cols), lambda s, sh, sl: (sl[s], 0, 0)),
        ],
        out_specs=[
            pl.BlockSpec((None, r, cols), lambda s, sh, sl: (0, 0, 0)),
            pl.BlockSpec((None, r, cols), lambda s, sh, sl: (jnp.maximum(s - 1, 0), 0, 0)),
        ],
    )
    return pl.pallas_call(
        body,
        name=name,
        grid_spec=grid_spec,
        out_shape=[jax.ShapeDtypeStruct((1, r, cols), F32), jax.ShapeDtypeStruct((3, r, cols), send_dtype)],
        compiler_params=pltpu.CompilerParams(
            dimension_semantics=("arbitrary",), vmem_limit_bytes=V7X_VMEM_LIMIT_BYTES),
    )(shard_ids, slot_ids, full, recv)


def _final_sum(own, recv, name):
    _, r, cols = own.shape
    tr = _tile(r, 64)

    def body(own_ref, recv_ref, out_ref):
        acc = own_ref[...]
        for k in range(3):
            acc = acc + recv_ref[k].astype(F32)
        out_ref[...] = acc

    return _call(
        body, name=name, grid=(r // tr,),
        in_specs=[
            pl.BlockSpec((None, tr, cols), lambda i: (0, i, 0)),
            pl.BlockSpec((3, tr, cols), lambda i: (0, i, 0)),
        ],
        out_specs=[pl.BlockSpec((tr, cols), lambda i: (i, 0))],
        out_shape=[jax.ShapeDtypeStruct((r, cols), F32)],
        args=(own, recv))[0][0]


def _rs_level1(fulls_f32, fulls_send, tag):
    x, y, c = _place()
    recv1 = _run_plan(_rs_sibling_plan(fulls_send), "rs_sibling_" + tag)
    qs = jnp.stack([2 * x + y, 2 * (1 - x) + y, 2 * x + (1 - y), 2 * (1 - x) + (1 - y)]).astype(jnp.int32)
    shard_ids = 2 * qs + c
    return [_rs_sum(f32, r1, shard_ids, qs, f"rs_sum_{tag}{a}")
            for a, (f32, r1) in enumerate(zip(fulls_f32, recv1))]


def _block_diag(w, per):
    h, hd, _ = w.shape
    w4 = w.reshape(h // per, per, hd, hd)
    eye = jnp.eye(per, dtype=w.dtype)
    out = w4[:, :, :, None, :] * eye[None, :, None, :, None]
    return out.reshape(h // per, per * hd, per * hd)


def _block_diag_extract(w_bd, per, hd):
    g = w_bd.shape[0]
    blocks = [w_bd[:, k * hd:(k + 1) * hd, k * hd:(k + 1) * hd] for k in range(per)]
    return jnp.stack(blocks, axis=1).reshape(g * per, hd, hd)


def _rows(g):
    return g.reshape(g.shape[0] * g.shape[1], g.shape[2])


def kernel(x, norm_mix_pre, norm_mix_post, norm_mlp_pre, norm_mlp_post, w_in, b_gate, conv_w, conv_b, lru_w_a, lru_b_a, lru_w_x, lru_b_x, lru_lambda, pool_w, pool_scale, w_lru_up, w_pool_up, w_o, w_ff1, w_ff2, loss_target, m_norm_mix_pre, m_norm_mix_post, m_norm_mlp_pre, m_norm_mlp_post, m_w_in, m_b_gate, m_conv_w, m_conv_b, m_lru_w_a, m_lru_b_a, m_lru_w_x, m_lru_b_x, m_lru_lambda, m_pool_w, m_pool_scale, m_w_lru_up, m_w_pool_up, m_w_o, m_w_ff1, m_w_ff2, v_norm_mix_pre, v_norm_mix_post, v_norm_mlp_pre, v_norm_mlp_post, v_w_in, v_b_gate, v_conv_w, v_conv_b, v_lru_w_a, v_lru_b_a, v_lru_w_x, v_lru_b_x, v_lru_lambda, v_pool_w, v_pool_scale, v_w_lru_up, v_w_pool_up, v_w_o, v_w_ff1, v_w_ff2):
    t, d = x.shape[1], x.shape[2]
    d_rnn = conv_b.shape[1]
    d_pool = pool_scale.shape[1]
    per = LRU_CB // LRU_HEAD_DIM
    xi, yi, ci = _place()
    me = 4 * xi + 2 * yi + ci

    x2d = x[0]
    tgt = loss_target[0]

    s_in = w_in[0].T.astype(BF16)
    s_lu = w_lru_up[0].astype(BF16)
    s_pu = w_pool_up[0].T.astype(BF16)
    s_o = w_o[0].astype(BF16)
    s_f1 = w_ff1[0].T.astype(BF16)
    s_f2 = w_ff2[0].astype(BF16)
    s_cw = jnp.pad(conv_w[0], ((0, 4), (0, 0)))

    g_in, g_cw = _run_plan(_ag_plan([s_in, s_cw]), "ag_w_in")
    w_int = _rows(g_in)
    conv_w_full = jnp.transpose(g_cw[:, :4, :], (1, 0, 2)).reshape(4, d_rnn)

    wa_bd = _block_diag(lru_w_a[0], per).astype(BF16)
    wx_bd = _block_diag(lru_w_x[0], per).astype(BF16)
    pw = pool_w[0]
    pw_bf = pw.astype(BF16)

    pool_block = (2 * d_rnn) // d_pool
    ga_block = (2 * d_rnn + d_pool) // 512
    gb_block = ga_block + d // 512
    g_block = d_rnn // 512

    (proj, h1), (g_lu, g_pu, g_o) = _norm_proj(x2d, norm_mix_pre, w_int, carry=_ag_plan([s_lu, s_pu, s_o]))
    w_lu, w_put, w_og = _rows(g_lu), _rows(g_pu), _rows(g_o)
    (y_lru, h, xc), (g_f1,) = _lru_fwd(proj, conv_w_full, conv_b, wa_bd, lru_b_a, wx_bd, lru_b_x, lru_lambda,
                                       carry=_ag_plan([s_f1]))
    w_f1t = _rows(g_f1)
    y_pool, p = _pool_fwd(proj, pw_bf, pool_scale, pool_block)
    br_a, br_b, mix = _branch_mix(y_lru, y_pool, w_lu, w_put, proj, b_gate, ga_block, gb_block)
    m, x2, h3 = _wo_norm(mix, w_og, x2d, norm_mix_post, norm_mlp_pre)
    (rf, act), (g_f2,) = _ff1(h3, w_f1t, carry=_ag_plan([s_f2]))
    w_f2 = _rows(g_f2)
    dy, df, dg4, loss_part = _ff2_loss(act, w_f2, x2, norm_mlp_post, tgt)

    d_f1 = _ff2_bwd(df, w_f2, rf)
    (gw_ff2_32, gw_ff2_16), _ = _wgrad(act, df, "wgrad_ff2")
    ((own_ff2, send_ff2),) = _rs_level1([gw_ff2_32], [gw_ff2_16], "ff2")
    (gw_ff1_32, gw_ff1_16), (r2_ff2,) = _wgrad(d_f1, h3, "wgrad_ff1", carry=_rs_chips_plan([send_ff2]))
    ((own_ff1, send_ff1),) = _rs_level1([gw_ff1_32], [gw_ff1_16], "ff1")
    (dx2, dm, dg3, dg2), (r2_ff1,) = _ff1_bwd_norms(d_f1, w_f1t, dy, x2, norm_mlp_pre, m, norm_mix_post,
                                                    carry=_rs_chips_plan([send_ff1]))
    (gw_o_32, gw_o_16), _ = _wgrad(mix, dm, "wgrad_o")
    d_br_a, d_br_b, p_ga, p_gb, dbg_a, dbg_b = _wo_bwd_mix(dm, w_og, br_a, br_b, proj, b_gate, ga_block, gb_block)
    (gw_lu_32, gw_lu_16), _ = _wgrad(y_lru, d_br_a, "wgrad_lru_up")
    (gw_pu_32, gw_pu_16), _ = _wgrad(d_br_b, y_pool, "wgrad_pool_up")
    mid = _rs_level1([gw_o_32, gw_lu_32, gw_pu_32.reshape(-1, d)],
                     [gw_o_16, gw_lu_16, gw_pu_16.reshape(-1, d)], "mid")
    dh, p_g = _lru_up_bwd(d_br_a, w_lu, proj, h, g_block)
    d_y_pool = _pool_up_bwd(d_br_b, w_put)
    (p_x, dwa_bd, db_a, dwx_bd, db_x, dlam, dconv_w, dconv_b), r2_mid = _lru_bwd(
        dh, xc, h, proj, conv_w_full, wa_bd, lru_b_a, wx_bd, lru_b_x, lru_lambda,
        carry=_rs_chips_plan([s for _, s in mid]))
    p_p, dpool_w, dpool_scale = _pool_bwd(d_y_pool, p, pw, pool_scale)
    parts = [p_x, p_g, p_p, p_ga, p_gb]
    gw_in = None
    row_off = 0
    for k, part in enumerate(parts):
        gw_in, _ = _wgrad(part, h1, f"wgrad_in{k}", prev=gw_in, row_off=row_off, rows=w_int.shape[0])
        row_off += part.shape[1]
    ((own_in, send_in),) = _rs_level1([gw_in[0]], [gw_in[1]], "in")
    (grad_x, dg1), (r2_in,) = _win_bwd_norm(parts, w_int, dx2, x2d, norm_mix_pre,
                                             carry=_rs_chips_plan([send_in]))

    g_w_in = _final_sum(own_in, r2_in, "rs_final_in").T
    g_w_lru_up = _final_sum(mid[1][0], r2_mid[1], "rs_final_lru_up")
    g_w_pool_up = _final_sum(mid[2][0], r2_mid[2], "rs_final_pool_up").reshape(d // N_DEV, d_pool).T
    g_w_o = _final_sum(mid[0][0], r2_mid[0], "rs_final_o")
    g_w_ff1 = _final_sum(own_ff1, r2_ff1, "rs_final_ff1").T
    g_w_ff2 = _final_sum(own_ff2, r2_ff2, "rs_final_ff2")

    dwa = _block_diag_extract(dwa_bd, per, LRU_HEAD_DIM)
    dwx = _block_diag_extract(dwx_bd, per, LRU_HEAD_DIM)
    small_names = ["norm_mix_pre", "norm_mix_post", "norm_mlp_pre", "norm_mlp_post", "b_gate", "conv_b",
                   "lru_w_a", "lru_b_a", "lru_w_x", "lru_b_x", "lru_lambda", "pool_w", "pool_scale"]
    small_grads = [dg1, dg2, dg3, dg4, jnp.concatenate([dbg_a, dbg_b], axis=1), dconv_b,
                   dwa, db_a, dwx, db_x, dlam, dpool_w, dpool_scale]

    def pack(arrs):
        flat = [a.reshape(-1) for a in arrs]
        flat = [jnp.pad(f, (0, (-f.shape[0]) % d)) for f in flat]
        rows = jnp.concatenate(flat).reshape(-1, d)
        pad_rows = (-rows.shape[0]) % (8 * N_DEV)
        return jnp.pad(rows, ((0, pad_rows), (0, 0)))

    small_shapes = [a.shape for a in small_grads]
    packed = pack(small_grads + [dconv_w])
    ((own_sm, send_sm),) = _rs_level1([packed], [packed], "small")
    (r2_sm,) = _run_plan(_rs_chips_plan([send_sm]), "rs_chips_small")
    red_small = _final_sum(own_sm, r2_sm, "rs_final_small")
    small_all = _run_plan(_ag_plan([red_small]), "ag_small")[0]
    small_all = small_all.reshape(-1, d)

    def unpack(rows, shapes):
        out, r0 = [], 0
        for shp in shapes:
            n = math.prod(shp)
            nr = -(-n // d)
            out.append(rows[r0:r0 + nr].reshape(-1)[:n].reshape(shp))
            r0 += nr
        return out

    unpacked = unpack(small_all, small_shapes + [dconv_w.shape])
    g_small = dict(zip(small_names, unpacked[:-1]))
    g_conv_w = lax.dynamic_slice_in_dim(unpacked[-1], me * (d_rnn // N_DEV), d_rnn // N_DEV, axis=1)

    args = dict(norm_mix_pre=norm_mix_pre, norm_mix_post=norm_mix_post, norm_mlp_pre=norm_mlp_pre,
                norm_mlp_post=norm_mlp_post, b_gate=b_gate, conv_b=conv_b, lru_w_a=lru_w_a, lru_b_a=lru_b_a,
                lru_w_x=lru_w_x, lru_b_x=lru_b_x, lru_lambda=lru_lambda, pool_w=pool_w, pool_scale=pool_scale)
    ms = dict(norm_mix_pre=m_norm_mix_pre, norm_mix_post=m_norm_mix_post, norm_mlp_pre=m_norm_mlp_pre,
              norm_mlp_post=m_norm_mlp_post, b_gate=m_b_gate, conv_b=m_conv_b, lru_w_a=m_lru_w_a,
              lru_b_a=m_lru_b_a, lru_w_x=m_lru_w_x, lru_b_x=m_lru_b_x, lru_lambda=m_lru_lambda,
              pool_w=m_pool_w, pool_scale=m_pool_scale)
    vs = dict(norm_mix_pre=v_norm_mix_pre, norm_mix_post=v_norm_mix_post, norm_mlp_pre=v_norm_mlp_pre,
              norm_mlp_post=v_norm_mlp_post, b_gate=v_b_gate, conv_b=v_conv_b, lru_w_a=v_lru_w_a,
              lru_b_a=v_lru_b_a, lru_w_x=v_lru_w_x, lru_b_x=v_lru_b_x, lru_lambda=v_lru_lambda,
              pool_w=v_pool_w, pool_scale=v_pool_scale)
    small_w = pack([args[n] for n in small_names])
    small_m = pack([ms[n] for n in small_names])
    small_v = pack([vs[n] for n in small_names])
    small_g = pack([g_small[n] for n in small_names])
    sd, snm, snv = _adamw(small_w, small_g, small_m, small_v, "adamw_small")
    full_shapes = [args[n].shape for n in small_names]
    delta = dict(zip(small_names, unpack(sd, full_shapes)))
    new_m = dict(zip(small_names, unpack(snm, full_shapes)))
    new_v = dict(zip(small_names, unpack(snv, full_shapes)))
    grads = {n: g_small[n].reshape(args[n].shape) for n in small_names}

    def big_adam(name, w, g, mm, vv):
        dl, nm, nv = _adamw(w[0], g, mm[0], vv[0], "adamw_" + name)
        grads[name], delta[name], new_m[name], new_v[name] = g[None], dl[None], nm[None], nv[None]

    big_adam("w_in", w_in, g_w_in, m_w_in, v_w_in)
    big_adam("conv_w", conv_w, g_conv_w, m_conv_w, v_conv_w)
    big_adam("w_lru_up", w_lru_up, g_w_lru_up, m_w_lru_up, v_w_lru_up)
    big_adam("w_pool_up", w_pool_up, g_w_pool_up, m_w_pool_up, v_w_pool_up)
    big_adam("w_o", w_o, g_w_o, m_w_o, v_w_o)
    big_adam("w_ff1", w_ff1, g_w_ff1, m_w_ff1, v_w_ff1)
    big_adam("w_ff2", w_ff2, g_w_ff2, m_w_ff2, v_w_ff2)

    loss = lax.psum(loss_part[0, 0], ("x", "y", "c"))
    order = ["norm_mix_pre", "norm_mix_post", "norm_mlp_pre", "norm_mlp_post", "w_in", "b_gate", "conv_w",
             "conv_b", "lru_w_a", "lru_b_a", "lru_w_x", "lru_b_x", "lru_lambda", "pool_w", "pool_scale",
             "w_lru_up", "w_pool_up", "w_o", "w_ff1", "w_ff2"]
    return (loss, grad_x[None], *[grads[n] for n in order], *[delta[n] for n in order],
            *[new_m[n] for n in order], *[new_v[n] for n in order])
```

```python
import functools
import math
import operator
import types

import jax
import jax.numpy as jnp
from jax import lax
from jax.experimental import pallas as pl
from jax.experimental.pallas import tpu as pltpu

F32 = jnp.float32
BF16 = jnp.bfloat16
NORM_EPS = 1e-6
LRU_C = 8.0
N_LRU_HEADS = 16
LRU_HEAD_DIM = 64
POOL_WINDOWS = (2, 4, 8, 16)
POOL_GROUP_DIM = 128
ADAM_LR = 0.001
ADAM_B1 = 0.9
ADAM_B2 = 0.999
ADAM_EPS = 1e-08
ADAM_WD = 0.01
ADAM_STEP = 10
N_DEV = 8
V7X_VMEM_LIMIT_BYTES = 48 * 1024 * 1024
LRU_CB = 256
MESH = pl.DeviceIdType.MESH
ANY = pl.BlockSpec(memory_space=pl.ANY)


def _tile(n, pref):
    t = min(n, pref)
    assert n % t == 0, (n, pref)
    return t


def _dot_nn(a, b):
    return lax.dot_general(a, b, (((1,), (0,)), ((), ())), preferred_element_type=F32)


def _dot_nt(a, b):
    return lax.dot_general(a, b, (((1,), (1,)), ((), ())), preferred_element_type=F32)


def _dot_tn(a, b):
    return lax.dot_general(a, b, (((0,), (0,)), ((), ())), preferred_element_type=F32)


def _row_chunks(n_rows, fn, chunk=256):
    chunk = min(chunk, n_rows)
    assert n_rows % chunk == 0

    def step(r, carry):
        fn(pl.ds(pl.multiple_of(r * chunk, chunk), chunk))
        return carry

    lax.fori_loop(0, n_rows // chunk, step, 0)


def _sig(x):
    return 1.0 / (1.0 + jnp.exp(-x))


def _rms_hat(x):
    r = lax.rsqrt(jnp.mean(x * x, axis=-1, keepdims=True) + NORM_EPS)
    return x * r, r


def _rms_bwd(dn, xhat, r, g):
    q = dn * g
    dx = r * (q - xhat * jnp.mean(q * xhat, axis=-1, keepdims=True))
    dg = jnp.sum(dn * xhat, axis=0, keepdims=True)
    return dx, dg


_GELU_K = math.sqrt(2.0 / math.pi)
_GELU_C = 0.044715


def _gelu_and_grad(g):
    t = jnp.tanh(_GELU_K * (g + _GELU_C * g * g * g))
    val = 0.5 * g * (1.0 + t)
    grad = 0.5 * (1.0 + t) + 0.5 * g * (1.0 - t * t) * (_GELU_K * (1.0 + 3.0 * _GELU_C * g * g))
    return val, grad


def _softplus_neg(lam):
    z = -lam
    e = jnp.exp(-jnp.abs(z))
    u = 1.0 + e
    d = u - 1.0
    l1p = jnp.where(d == 0.0, e, jnp.log(u) * (e / jnp.where(d == 0.0, 1.0, d)))
    return jnp.maximum(z, 0.0) + l1p


def _lru_gates(xc, wa, ba, wx, bx, lam):
    xcb = xc.astype(BF16)
    r = _sig(_dot_nn(xcb, wa) + ba)
    i = _sig(_dot_nn(xcb, wx) + bx)
    sp = _softplus_neg(lam)
    log_a = (-LRU_C) * r * sp
    a = jnp.exp(log_a)
    mult = jnp.sqrt(-jnp.tanh(log_a) * (1.0 + a * a))
    return xcb, r, i, sp, log_a, a, mult


def _place():
    return lax.axis_index("x"), lax.axis_index("y"), lax.axis_index("c")


def _ag_plan(shards):
    na = len(shards)

    def parts(ins, outs, sems):
        send_sems, recv_sems, local_sems = sems
        x, y, c = _place()
        me, sibling = (x, y, c), (x, y, 1 - c)
        chips = [(1 - x, y), (x, 1 - y), (1 - x, 1 - y)]

        def slot(a, px, py, pc):
            return outs[a].at[4 * px + 2 * py + pc]

        def copy(a, k, block, to, src=None):
            return pltpu.make_async_remote_copy(
                src_ref=slot(a, *block) if src is None else src,
                dst_ref=slot(a, *block),
                send_sem=send_sems.at[a * 7 + k],
                recv_sem=recv_sems.at[a * 7 + k],
                device_id=to,
                device_id_type=MESH,
            )

        mine = [pltpu.make_async_copy(ins[a], slot(a, *me), local_sems.at[a]) for a in range(na)]
        first = []
        for a in range(na):
            first.append(copy(a, 0, me, sibling, src=ins[a]))
            first += [copy(a, 1 + j, me, (*chip, c), src=ins[a]) for j, chip in enumerate(chips)]
        return me, sibling, chips, c, copy, mine, first

    def start(ins, outs, sems):
        _, _, _, _, _, mine, first = parts(ins, outs, sems)
        for cp in mine + first:
            cp.start()

    def finish(ins, outs, sems):
        me, sibling, chips, c, copy, mine, first = parts(ins, outs, sems)
        passed = []
        for j, chip in enumerate(chips):
            for a in range(na):
                copy(a, 1 + j, (*chip, c), me).wait_recv()
                fwd = copy(a, 4 + j, (*chip, c), sibling)
                fwd.start()
                passed.append(fwd)
        for a in range(na):
            copy(a, 0, sibling, me).wait_recv()
            for j, chip in enumerate(chips):
                copy(a, 4 + j, (*chip, 1 - c), me).wait_recv()
        for cp in first + passed:
            cp.wait_send()
        for cp in mine:
            cp.wait()

    return types.SimpleNamespace(
        ins=list(shards),
        out_shapes=[jax.ShapeDtypeStruct((N_DEV,) + s.shape, s.dtype) for s in shards],
        sems=[pltpu.SemaphoreType.DMA((7 * na,)), pltpu.SemaphoreType.DMA((7 * na,)),
              pltpu.SemaphoreType.DMA((na,))],
        start=start, finish=finish)


def _rs_sibling_plan(fulls):
    na = len(fulls)
    rs = [f.shape[0] // N_DEV for f in fulls]

    def copies(ins, outs, sems):
        send_sems, recv_sems = sems
        x, y, c = _place()
        out = []
        for a in range(na):
            for q in range(4):
                shard = 2 * q + (1 - c)
                out.append(pltpu.make_async_remote_copy(
                    src_ref=ins[a].at[pl.ds(shard * rs[a], rs[a])],
                    dst_ref=outs[a].at[q],
                    send_sem=send_sems.at[a * 4 + q],
                    recv_sem=recv_sems.at[a * 4 + q],
                    device_id=(x, y, 1 - c),
                    device_id_type=MESH,
                ))
        return out

    def start(ins, outs, sems):
        for cp in copies(ins, outs, sems):
            cp.start()

    def finish(ins, outs, sems):
        for cp in copies(ins, outs, sems):
            cp.wait()

    return types.SimpleNamespace(
        ins=list(fulls),
        out_shapes=[jax.ShapeDtypeStruct((4, r) + f.shape[1:], f.dtype) for r, f in zip(rs, fulls)],
        sems=[pltpu.SemaphoreType.DMA((4 * na,)), pltpu.SemaphoreType.DMA((4 * na,))],
        start=start, finish=finish)


def _rs_chips_plan(sends):
    na = len(sends)

    def copies(ins, outs, sems):
        send_sems, recv_sems = sems
        x, y, c = _place()
        chips = [(1 - x, y), (x, 1 - y), (1 - x, 1 - y)]
        out = []
        for a in range(na):
            for k, chip in enumerate(chips):
                out.append(pltpu.make_async_remote_copy(
                    src_ref=ins[a].at[k],
                    dst_ref=outs[a].at[k],
                    send_sem=send_sems.at[a * 3 + k],
                    recv_sem=recv_sems.at[a * 3 + k],
                    device_id=(*chip, c),
                    device_id_type=MESH,
                ))
        return out

    def start(ins, outs, sems):
        for cp in copies(ins, outs, sems):
            cp.start()

    def finish(ins, outs, sems):
        for cp in copies(ins, outs, sems):
            cp.wait()

    return types.SimpleNamespace(
        ins=list(sends),
        out_shapes=[jax.ShapeDtypeStruct(s.shape, s.dtype) for s in sends],
        sems=[pltpu.SemaphoreType.DMA((3 * na,)), pltpu.SemaphoreType.DMA((3 * na,))],
        start=start, finish=finish)


def _run_plan(plan, name):
    n_in, n_out = len(plan.ins), len(plan.out_shapes)

    def body(*refs):
        ins, outs, sems = refs[:n_in], refs[n_in:n_in + n_out], refs[n_in + n_out:]
        plan.start(ins, outs, sems)
        plan.finish(ins, outs, sems)

    return pl.pallas_call(
        body,
        name=name,
        in_specs=[ANY] * n_in,
        out_specs=[ANY] * n_out,
        out_shape=plan.out_shapes,
        scratch_shapes=plan.sems,
    )(*plan.ins)


def _call(body, *, name, grid, in_specs, out_specs, out_shape, args, scratch_shapes=(), aliases=None,
          carry=None):
    n_in, n_out, n_scr = len(in_specs), len(out_shape), len(scratch_shapes)
    params = pltpu.CompilerParams(
        dimension_semantics=("arbitrary",) * len(grid), vmem_limit_bytes=V7X_VMEM_LIMIT_BYTES)
    if carry is None:
        outs = pl.pallas_call(
            body, name=name, grid=grid, in_specs=list(in_specs), out_specs=list(out_specs),
            out_shape=list(out_shape), scratch_shapes=list(scratch_shapes),
            input_output_aliases=aliases or {}, compiler_params=params)(*args)
        return list(outs), []
    c_in, c_out = len(carry.ins), len(carry.out_shapes)

    def full(*refs):
        p = 0
        ins = refs[p:p + n_in]
        p += n_in
        cins = refs[p:p + c_in]
        p += c_in
        outs = refs[p:p + n_out]
        p += n_out
        couts = refs[p:p + c_out]
        p += c_out
        scr = refs[p:p + n_scr]
        csems = refs[p + n_scr:]
        ids = [pl.program_id(a) for a in range(len(grid))]
        first = functools.reduce(operator.and_, [i == 0 for i in ids])
        last = functools.reduce(operator.and_, [i == g - 1 for i, g in zip(ids, grid)])

        @pl.when(first)
        def _():
            carry.start(cins, couts, csems)

        body(*ins, *outs, *scr)

        @pl.when(last)
        def _():
            carry.finish(cins, couts, csems)

    outs = pl.pallas_call(
        full, name=name, grid=grid,
        in_specs=list(in_specs) + [ANY] * c_in,
        out_specs=list(out_specs) + [ANY] * c_out,
        out_shape=list(out_shape) + list(carry.out_shapes),
        scratch_shapes=list(scratch_shapes) + list(carry.sems),
        input_output_aliases=aliases or {}, compiler_params=params)(*args, *carry.ins)
    return list(outs[:n_out]), list(outs[n_out:])


def _norm_proj(x, g1, w_int, carry=None):
    t, d = x.shape
    n = w_int.shape[0]
    tt, tn = _tile(t, 2048), _tile(n, 512)

    def body(x_ref, g_ref, w_ref, proj_ref, h1_ref, h1_s):
        @pl.when(pl.program_id(1) == 0)
        def _():
            def norm_rows(rows):
                xhat, _ = _rms_hat(x_ref[rows, :])
                h = (xhat * g_ref[...]).astype(BF16)
                h1_s[rows, :] = h
                h1_ref[rows, :] = h

            _row_chunks(tt, norm_rows)

        proj_ref[...] = _dot_nt(h1_s[...], w_ref[...])

    return _call(
        body, name="norm_proj", grid=(t // tt, n // tn),
        in_specs=[
            pl.BlockSpec((tt, d), lambda i, j: (i, 0)),
            pl.BlockSpec((1, d), lambda i, j: (0, 0)),
            pl.BlockSpec((tn, d), lambda i, j: (j, 0)),
        ],
        out_specs=[
            pl.BlockSpec((tt, tn), lambda i, j: (i, j)),
            pl.BlockSpec((tt, d), lambda i, j: (i, 0)),
        ],
        out_shape=[jax.ShapeDtypeStruct((t, n), F32), jax.ShapeDtypeStruct((t, d), BF16)],
        scratch_shapes=[pltpu.VMEM((tt, d), BF16)],
        args=(x, g1, w_int), carry=carry)


def _fill_block_diag(w_ref, bd_ref):
    bd_ref[...] = jnp.zeros_like(bd_ref)
    hd = LRU_HEAD_DIM
    for k in range(w_ref.shape[0]):
        bd_ref[k * hd:(k + 1) * hd, k * hd:(k + 1) * hd] = w_ref[k].astype(BF16)


def _lru_fwd(proj, conv_w, conv_b, w_a, b_a, w_x, b_x, lam, carry=None):
    t = proj.shape[0]
    dr = conv_b.shape[1]
    cb = LRU_CB
    tc = _tile(t, 256)
    ncb, ntc = dr // cb, t // tc

    def body(xp_ref, g_ref, cw_ref, cb_ref, wa_ref, ba_ref, wx_ref, bx_ref, lam_ref,
             y_ref, h_ref, xc_ref, prevx_s, hlast_s, wa_s, wx_s):
        c = pl.program_id(1)

        @pl.when(c == 0)
        def _():
            prevx_s[...] = jnp.zeros_like(prevx_s)
            hlast_s[...] = jnp.zeros_like(hlast_s)
            _fill_block_diag(wa_ref, wa_s)
            _fill_block_diag(wx_ref, wx_s)

        x = xp_ref[...]
        prev = prevx_s[...]
        row = lax.broadcasted_iota(jnp.int32, x.shape, 0)

        def sh(j):
            return jnp.where(row >= j, pltpu.roll(x, j, 0), pltpu.roll(prev, j, 0))

        xc = (cb_ref[...] + cw_ref[0:1, :] * sh(3) + cw_ref[1:2, :] * sh(2)
              + cw_ref[2:3, :] * sh(1) + cw_ref[3:4, :] * x)
        prevx_s[...] = x
        xc_ref[...] = xc
        _, _, i, _, _, a, mult = _lru_gates(xc, wa_s[...], ba_ref[...], wx_s[...], bx_ref[...],
                                            lam_ref[...])
        av, bv = a, mult * (i * xc)
        s = 1
        while s < tc:
            a_sh = jnp.where(row >= s, pltpu.roll(av, s, 0), 1.0)
            b_sh = jnp.where(row >= s, pltpu.roll(bv, s, 0), 0.0)
            bv = av * b_sh + bv
            av = av * a_sh
            s *= 2
        h = av * hlast_s[...] + bv
        h_ref[...] = h
        hlast_s[...] = h_ref[tc - 1:tc, :]
        gel, _ = _gelu_and_grad(g_ref[...])
        y_ref[...] = (h * gel).astype(BF16)

    vec = pl.BlockSpec((1, cb), lambda j, c: (0, j))
    blk = pl.BlockSpec((tc, cb), lambda j, c: (c, j))
    mat = pl.BlockSpec((cb // LRU_HEAD_DIM, LRU_HEAD_DIM, LRU_HEAD_DIM), lambda j, c: (j, 0, 0))
    return _call(
        body, name="lru_fwd", grid=(ncb, ntc),
        in_specs=[
            blk,
            pl.BlockSpec((tc, cb), lambda j, c: (c, ncb + j)),
            pl.BlockSpec((4, cb), lambda j, c: (0, j)),
            vec, mat, vec, mat, vec, vec,
        ],
        out_specs=[blk, blk, blk],
        out_shape=[
            jax.ShapeDtypeStruct((t, dr), BF16),
            jax.ShapeDtypeStruct((t, dr), F32),
            jax.ShapeDtypeStruct((t, dr), F32),
        ],
        scratch_shapes=[pltpu.VMEM((tc, cb), F32), pltpu.VMEM((1, cb), F32),
                        pltpu.VMEM((cb, cb), BF16), pltpu.VMEM((cb, cb), BF16)],
        args=(proj, proj, conv_w, conv_b, w_a, b_a, w_x, b_x, lam), carry=carry)


def _pool_select(col, vals):
    out = vals[3]
    for g in (2, 1, 0):
        out = jnp.where(col < (g + 1) * POOL_GROUP_DIM, vals[g], out)
    return out


def _pool_fwd(proj, pool_w, pool_scale, col_block):
    t = proj.shape[0]
    dp = pool_scale.shape[1]
    tc = _tile(t, 256)
    ntc = t // tc

    def body(x_ref, w_ref, sc_ref, y_ref, p_ref, px, p2, p4, p8):
        c = pl.program_id(0)

        @pl.when(c == 0)
        def _():
            for s in (px, p2, p4, p8):
                s[...] = jnp.zeros_like(s)

        x = x_ref[...]
        row = lax.broadcasted_iota(jnp.int32, x.shape, 0)
        col = lax.broadcasted_iota(jnp.int32, x.shape, 1)

        def sh(v, pv, j):
            return jnp.where(row >= j, pltpu.roll(v, j, 0), pltpu.roll(pv[...], j, 0))

        s2 = x + sh(x, px, 1)
        s4 = s2 + sh(s2, p2, 2)
        s8 = s4 + sh(s4, p4, 4)
        s16 = s8 + sh(s8, p8, 8)
        px[...] = x
        p2[...] = s2
        p4[...] = s4
        p8[...] = s8
        wsum = _pool_select(col, (s2, s4, s8, s16))
        win = _pool_select(col, POOL_WINDOWS)
        cnt = jnp.minimum(c * tc + row + 1, win).astype(F32)
        p = wsum / cnt - x
        pb = p.astype(BF16)
        p_ref[...] = pb
        for g in range(len(POOL_WINDOWS)):
            sl = slice(g * POOL_GROUP_DIM, (g + 1) * POOL_GROUP_DIM)
            yg = _dot_nn(pb[:, sl], w_ref[g]) * sc_ref[:, sl]
            y_ref[:, sl] = yg.astype(BF16)

    return _call(
        body, name="pool_fwd", grid=(ntc,),
        in_specs=[
            pl.BlockSpec((tc, dp), lambda c: (c, col_block)),
            pl.BlockSpec(pool_w.shape, lambda c: (0, 0, 0)),
            pl.BlockSpec((1, dp), lambda c: (0, 0)),
        ],
        out_specs=[pl.BlockSpec((tc, dp), lambda c: (c, 0))] * 2,
        out_shape=[jax.ShapeDtypeStruct((t, dp), BF16)] * 2,
        scratch_shapes=[pltpu.VMEM((tc, dp), F32)] * 4,
        args=(proj, pool_w, pool_scale))[0]


def _branch_mix(y_lru, y_pool, w_lru_up, w_pool_upt, proj, b_gate, ga_block, gb_block):
    t, d = y_lru.shape
    dp = y_pool.shape[1]
    tt, tn = _tile(t, 1024), 512
    nj = d // tn

    def body(yl_ref, yp_ref, wl_ref, wp_ref, ga_ref, gb_ref, ba_ref, bb_ref, bra_ref, brb_ref, mix_ref):
        br_a = _dot_nn(yl_ref[...], wl_ref[...])
        br_b = _dot_nt(yp_ref[...], wp_ref[...])
        bra_ref[...] = br_a
        brb_ref[...] = br_b
        ga = _sig(ga_ref[...] + ba_ref[...])
        gb = _sig(gb_ref[...] + bb_ref[...])
        mix_ref[...] = (ga * br_a + gb * br_b).astype(BF16)

    out = pl.BlockSpec((tt, tn), lambda j, i: (i, j))
    return _call(
        body, name="branch_mix", grid=(nj, t // tt),
        in_specs=[
            pl.BlockSpec((tt, d), lambda j, i: (i, 0)),
            pl.BlockSpec((tt, dp), lambda j, i: (i, 0)),
            pl.BlockSpec((d, tn), lambda j, i: (0, j)),
            pl.BlockSpec((tn, dp), lambda j, i: (j, 0)),
            pl.BlockSpec((tt, tn), lambda j, i: (i, ga_block + j)),
            pl.BlockSpec((tt, tn), lambda j, i: (i, gb_block + j)),
            pl.BlockSpec((1, tn), lambda j, i: (0, j)),
            pl.BlockSpec((1, tn), lambda j, i: (0, nj + j)),
        ],
        out_specs=[out, out, out],
        out_shape=[
            jax.ShapeDtypeStruct((t, d), F32),
            jax.ShapeDtypeStruct((t, d), F32),
            jax.ShapeDtypeStruct((t, d), BF16),
        ],
        args=(y_lru, y_pool, w_lru_up, w_pool_upt, proj, proj, b_gate, b_gate))[0]


def _wo_norm(mix, w_o, x, g2, g3):
    t, d = x.shape
    tt = _tile(t, 512)

    def body(mix_ref, w_ref, x_ref, g2_ref, g3_ref, m_ref, x2_ref, h3_ref):
        m = _dot_nn(mix_ref[...], w_ref[...])
        m_ref[...] = m
        mhat, _ = _rms_hat(m)
        x2 = x_ref[...] + mhat * g2_ref[...]
        x2_ref[...] = x2
        xhat, _ = _rms_hat(x2)
        h3_ref[...] = (xhat * g3_ref[...]).astype(BF16)

    row = pl.BlockSpec((tt, d), lambda i: (i, 0))
    vec = pl.BlockSpec((1, d), lambda i: (0, 0))
    return _call(
        body, name="wo_norm", grid=(t // tt,),
        in_specs=[row, pl.BlockSpec((d, d), lambda i: (0, 0)), row, vec, vec],
        out_specs=[row, row, row],
        out_shape=[
            jax.ShapeDtypeStruct((t, d), F32),
            jax.ShapeDtypeStruct((t, d), F32),
            jax.ShapeDtypeStruct((t, d), BF16),
        ],
        args=(mix, w_o, x, g2, g3))[0]


def _ff1(h3, w_ff1t, carry=None):
    t, d = h3.shape
    n = w_ff1t.shape[0]
    tt, tn = _tile(t, 2048), _tile(n, 512)

    def body(h_ref, w_ref, rf_ref, act_ref):
        rf = jnp.maximum(_dot_nt(h_ref[...], w_ref[...]), 0.0)
        rf_ref[...] = rf.astype(BF16)
        act_ref[...] = (rf * rf).astype(BF16)

    out = pl.BlockSpec((tt, tn), lambda i, j: (i, j))
    return _call(
        body, name="ff1", grid=(t // tt, n // tn),
        in_specs=[pl.BlockSpec((tt, d), lambda i, j: (i, 0)), pl.BlockSpec((tn, d), lambda i, j: (j, 0))],
        out_specs=[out, out],
        out_shape=[jax.ShapeDtypeStruct((t, n), BF16)] * 2,
        args=(h3, w_ff1t), carry=carry)


def _ff2_loss(act, w_ff2, x2, g4, target):
    t, k = act.shape
    d = x2.shape[1]
    tt, tk = _tile(t, 1024), _tile(k, 512)
    nk = k // tk

    def body(a_ref, w_ref, x2_ref, g_ref, tg_ref, dy_ref, df_ref, dg_ref, loss_ref, acc):
        i, kk = pl.program_id(0), pl.program_id(1)

        @pl.when(kk == 0)
        def _():
            acc[...] = jnp.zeros_like(acc)

        @pl.when((i == 0) & (kk == 0))
        def _():
            dg_ref[...] = jnp.zeros_like(dg_ref)
            loss_ref[...] = jnp.zeros_like(loss_ref)

        acc[...] += _dot_nn(a_ref[...], w_ref[...])

        @pl.when(kk == nk - 1)
        def _():
            def tail(rows):
                fhat, r = _rms_hat(acc[rows, :])
                g = g_ref[...]
                e = x2_ref[rows, :] + fhat * g - tg_ref[rows, :]
                loss_ref[...] += 0.5 * jnp.sum(jnp.mean(e * e, axis=-1, keepdims=True))
                dy = e * (1.0 / d)
                dy_ref[rows, :] = dy
                df, dg = _rms_bwd(dy, fhat, r, g)
                df_ref[rows, :] = df.astype(BF16)
                dg_ref[...] += dg

            _row_chunks(tt, tail)

    row = pl.BlockSpec((tt, d), lambda i, kk: (i, 0))
    vec = pl.BlockSpec((1, d), lambda i, kk: (0, 0))
    return _call(
        body, name="ff2_loss", grid=(t // tt, nk),
        in_specs=[
            pl.BlockSpec((tt, tk), lambda i, kk: (i, kk)),
            pl.BlockSpec((tk, d), lambda i, kk: (kk, 0)),
            row, vec, row,
        ],
        out_specs=[row, row, vec, pl.BlockSpec((1, 128), lambda i, kk: (0, 0))],
        out_shape=[
            jax.ShapeDtypeStruct((t, d), F32),
            jax.ShapeDtypeStruct((t, d), BF16),
            jax.ShapeDtypeStruct((1, d), F32),
            jax.ShapeDtypeStruct((1, 128), F32),
        ],
        scratch_shapes=[pltpu.VMEM((tt, d), F32)],
        args=(act, w_ff2, x2, g4, target))[0]


def _ff2_bwd(df, w_ff2, rf):
    t, d = df.shape
    n = w_ff2.shape[0]
    tt, tn = _tile(t, 2048), _tile(n, 512)

    def body(df_ref, w_ref, rf_ref, out_ref):
        d_act = _dot_nt(df_ref[...], w_ref[...])
        out_ref[...] = (d_act * (2.0 * rf_ref[...].astype(F32))).astype(BF16)

    blk = pl.BlockSpec((tt, tn), lambda i, j: (i, j))
    return _call(
        body, name="ff2_bwd", grid=(t // tt, n // tn),
        in_specs=[pl.BlockSpec((tt, d), lambda i, j: (i, 0)), pl.BlockSpec((tn, d), lambda i, j: (j, 0)), blk],
        out_specs=[blk],
        out_shape=[jax.ShapeDtypeStruct((t, n), BF16)],
        args=(df, w_ff2, rf))[0][0]


def _wgrad(a, b, name, prev=None, row_off=0, rows=None, carry=None):
    t, m = a.shape
    n = b.shape[1]
    rows = m if rows is None else rows
    tm, tk = _tile(m, 512), _tile(t, 2048)
    nk = t // tk
    assert row_off % tm == 0
    off = row_off // tm

    def body(*refs):
        a_ref, b_ref = refs[0], refs[1]
        o32_ref, o16_ref, acc = refs[-3], refs[-2], refs[-1]
        kk = pl.program_id(1)

        @pl.when(kk == 0)
        def _():
            acc[...] = jnp.zeros_like(acc)

        acc[...] += _dot_tn(a_ref[...], b_ref[...])

        @pl.when(kk == nk - 1)
        def _():
            o32_ref[...] = acc[...]
            o16_ref[...] = acc[...].astype(BF16)

    in_specs = [pl.BlockSpec((tk, tm), lambda i, kk: (kk, i)), pl.BlockSpec((tk, n), lambda i, kk: (kk, 0))]
    args = [a, b]
    aliases = {}
    if prev is not None:
        in_specs += [ANY, ANY]
        args += list(prev)
        aliases = {2: 0, 3: 1}
    out = pl.BlockSpec((tm, n), lambda i, kk: (off + i, 0))
    return _call(
        body, name=name, grid=(m // tm, nk),
        in_specs=in_specs, out_specs=[out, out],
        out_shape=[jax.ShapeDtypeStruct((rows, n), F32), jax.ShapeDtypeStruct((rows, n), BF16)],
        scratch_shapes=[pltpu.VMEM((tm, n), F32)],
        aliases=aliases, args=args, carry=carry)


def _ff1_bwd_norms(d_f1, w_ff1t, dy, x2, g3, m, g2, carry=None):
    t, k = d_f1.shape
    d = x2.shape[1]
    tt, tk = _tile(t, 512), _tile(k, 512)
    nk = k // tk

    def body(a_ref, w_ref, dy_ref, x2_ref, g3_ref, m_ref, g2_ref, dx2_ref, dm_ref, dg3_ref, dg2_ref, acc):
        i, kk = pl.program_id(0), pl.program_id(1)

        @pl.when(kk == 0)
        def _():
            acc[...] = jnp.zeros_like(acc)

        @pl.when((i == 0) & (kk == 0))
        def _():
            dg3_ref[...] = jnp.zeros_like(dg3_ref)
            dg2_ref[...] = jnp.zeros_like(dg2_ref)

        acc[...] += _dot_nn(a_ref[...], w_ref[...])

        @pl.when(kk == nk - 1)
        def _():
            def tail(rows):
                xhat, r3 = _rms_hat(x2_ref[rows, :])
                dx, dg3 = _rms_bwd(acc[rows, :], xhat, r3, g3_ref[...])
                dx2 = dy_ref[rows, :] + dx
                dx2_ref[rows, :] = dx2
                dg3_ref[...] += dg3
                mhat, r2 = _rms_hat(m_ref[rows, :])
                dm, dg2 = _rms_bwd(dx2, mhat, r2, g2_ref[...])
                dm_ref[rows, :] = dm.astype(BF16)
                dg2_ref[...] += dg2

            _row_chunks(tt, tail)

    row = pl.BlockSpec((tt, d), lambda i, kk: (i, 0))
    vec = pl.BlockSpec((1, d), lambda i, kk: (0, 0))
    return _call(
        body, name="ff1_bwd_norms", grid=(t // tt, nk),
        in_specs=[
            pl.BlockSpec((tt, tk), lambda i, kk: (i, kk)),
            pl.BlockSpec((tk, d), lambda i, kk: (kk, 0)),
            row, row, vec, row, vec,
        ],
        out_specs=[row, row, vec, vec],
        out_shape=[
            jax.ShapeDtypeStruct((t, d), F32),
            jax.ShapeDtypeStruct((t, d), BF16),
            jax.ShapeDtypeStruct((1, d), F32),
            jax.ShapeDtypeStruct((1, d), F32),
        ],
        scratch_shapes=[pltpu.VMEM((tt, d), F32)],
        args=(d_f1, w_ff1t, dy, x2, g3, m, g2), carry=carry)


def _wo_bwd_mix(dm, w_o, br_a, br_b, proj, b_gate, ga_block, gb_block):
    t, d = dm.shape
    tt, tn = _tile(t, 1024), 512
    nj = d // tn

    def body(dm_ref, w_ref, bra_ref, brb_ref, ga_ref, gb_ref, ba_ref, bb_ref,
             dbra_ref, dbrb_ref, dga_ref, dgb_ref, dba_ref, dbb_ref):
        i = pl.program_id(1)

        @pl.when(i == 0)
        def _():
            dba_ref[...] = jnp.zeros_like(dba_ref)
            dbb_ref[...] = jnp.zeros_like(dbb_ref)

        d_mix = _dot_nt(dm_ref[...], w_ref[...])
        ga = _sig(ga_ref[...] + ba_ref[...])
        gb = _sig(gb_ref[...] + bb_ref[...])
        dbra_ref[...] = (d_mix * ga).astype(BF16)
        dbrb_ref[...] = (d_mix * gb).astype(BF16)
        dga = d_mix * bra_ref[...] * (ga * (1.0 - ga))
        dgb = d_mix * brb_ref[...] * (gb * (1.0 - gb))
        dga_ref[...] = dga.astype(BF16)
        dgb_ref[...] = dgb.astype(BF16)
        dba_ref[...] += jnp.sum(dga, axis=0, keepdims=True)
        dbb_ref[...] += jnp.sum(dgb, axis=0, keepdims=True)

    blk = pl.BlockSpec((tt, tn), lambda j, i: (i, j))
    vec = pl.BlockSpec((1, tn), lambda j, i: (0, j))
    return _call(
        body, name="wo_bwd_mix", grid=(nj, t // tt),
        in_specs=[
            pl.BlockSpec((tt, d), lambda j, i: (i, 0)),
            pl.BlockSpec((tn, d), lambda j, i: (j, 0)),
            blk, blk,
            pl.BlockSpec((tt, tn), lambda j, i: (i, ga_block + j)),
            pl.BlockSpec((tt, tn), lambda j, i: (i, gb_block + j)),
            vec,
            pl.BlockSpec((1, tn), lambda j, i: (0, nj + j)),
        ],
        out_specs=[blk, blk, blk, blk, vec, vec],
        out_shape=[jax.ShapeDtypeStruct((t, d), BF16)] * 4 + [jax.ShapeDtypeStruct((1, d), F32)] * 2,
        args=(dm, w_o, br_a, br_b, proj, proj, b_gate, b_gate))[0]


def _lru_up_bwd(d_br_a, w_lru_up, proj, h, g_block):
    t, d = d_br_a.shape
    tt, tn = _tile(t, 1024), 512

    def body(a_ref, w_ref, g_ref, h_ref, dh_ref, dg_ref):
        d_y = _dot_nt(a_ref[...], w_ref[...])
        gel, gel_grad = _gelu_and_grad(g_ref[...])
        dh_ref[...] = d_y * gel
        dg_ref[...] = (d_y * h_ref[...] * gel_grad).astype(BF16)

    blk = pl.BlockSpec((tt, tn), lambda i, j: (i, j))
    return _call(
        body, name="lru_up_bwd", grid=(t // tt, d // tn),
        in_specs=[
            pl.BlockSpec((tt, d), lambda i, j: (i, 0)),
            pl.BlockSpec((tn, d), lambda i, j: (j, 0)),
            pl.BlockSpec((tt, tn), lambda i, j: (i, g_block + j)),
            blk,
        ],
        out_specs=[blk, blk],
        out_shape=[jax.ShapeDtypeStruct((t, d), F32), jax.ShapeDtypeStruct((t, d), BF16)],
        args=(d_br_a, w_lru_up, proj, h))[0]


def _pool_up_bwd(d_br_b, w_pool_upt):
    t, d = d_br_b.shape
    dp = w_pool_upt.shape[1]
    tt = _tile(t, 2048)

    def body(a_ref, w_ref, out_ref):
        out_ref[...] = _dot_nn(a_ref[...], w_ref[...])

    return _call(
        body, name="pool_up_bwd", grid=(t // tt,),
        in_specs=[pl.BlockSpec((tt, d), lambda i: (i, 0)), pl.BlockSpec((d, dp), lambda i: (0, 0))],
        out_specs=[pl.BlockSpec((tt, dp), lambda i: (i, 0))],
        out_shape=[jax.ShapeDtypeStruct((t, dp), F32)],
        args=(d_br_b, w_pool_upt))[0][0]


def _lru_bwd(dh, xc, h, proj, conv_w, w_a, b_a, w_x, b_x, lam, carry=None):
    t, dr = dh.shape
    cb = LRU_CB
    hd = LRU_HEAD_DIM
    per = cb // hd
    tc = _tile(t, 256)
    ncb, ntc = dr // cb, t // tc

    def body(dh_ref, xc_ref, h_ref, hp_ref, xp_ref, cw_ref, wa_ref, ba_ref, wx_ref, bx_ref, lam_ref,
             dxp_ref, dwa_ref, dba_ref, dwx_ref, dbx_ref, dlam_ref, dcw_ref, dcb_ref,
             nextd_s, anext_s, gnext_s, tmp_s, wa_s, wx_s):
        c = pl.program_id(1)
        rc = ntc - 1 - c

        @pl.when(c == 0)
        def _():
            nextd_s[...] = jnp.zeros_like(nextd_s)
            anext_s[...] = jnp.zeros_like(anext_s)
            gnext_s[...] = jnp.zeros_like(gnext_s)
            for ref in (dwa_ref, dba_ref, dwx_ref, dbx_ref, dlam_ref, dcw_ref, dcb_ref):
                ref[...] = jnp.zeros_like(ref)
            _fill_block_diag(wa_ref, wa_s)
            _fill_block_diag(wx_ref, wx_s)

        xc = xc_ref[...]
        wa, wx, lam = wa_s[...], wx_s[...], lam_ref[...]
        xcb, r, i, sp, log_a, a, mult = _lru_gates(xc, wa, ba_ref[...], wx, bx_ref[...], lam)
        row = lax.broadcasted_iota(jnp.int32, xc.shape, 0)
        h = h_ref[...]
        hp = jnp.where(rc == 0, 0.0, hp_ref[...])
        hprev = jnp.where(row >= 1, pltpu.roll(h, 1, 0), pltpu.roll(hp, 1, 0))

        def up(v, nv, j):
            return jnp.where(row < tc - j, pltpu.roll(v, tc - j, 0), nv)

        av = up(a, anext_s[...], 1)
        bv = dh_ref[...]
        s = 1
        while s < tc:
            a_sh = up(av, 1.0, s)
            b_sh = up(bv, 0.0, s)
            bv = av * b_sh + bv
            av = av * a_sh
            s *= 2
        gt = av * gnext_s[...] + bv
        tmp_s[...] = gt
        gnext_s[...] = tmp_s[0:1, :]
        tmp_s[...] = a
        anext_s[...] = tmp_s[0:1, :]

        da = gt * hprev
        ixc = i * xc
        d_mult = gt * ixc
        d_i = gt * mult * xc
        d_xc = gt * mult * i
        d_log_a = da * a - d_mult * (a * a) / mult
        d_pre_r = (d_log_a * ((-LRU_C) * sp)) * (r * (1.0 - r))
        d_pre_i = d_i * (i * (1.0 - i))
        d_sp = jnp.sum(d_log_a * ((-LRU_C) * r), axis=0, keepdims=True)
        dlam_ref[...] += d_sp * (-1.0 / (1.0 + jnp.exp(lam)))
        dpr = d_pre_r.astype(BF16)
        dpi = d_pre_i.astype(BF16)
        dba_ref[...] += jnp.sum(d_pre_r, axis=0, keepdims=True)
        dbx_ref[...] += jnp.sum(d_pre_i, axis=0, keepdims=True)
        pa = _dot_tn(xcb, dpr)
        px = _dot_tn(xcb, dpi)
        for k in range(per):
            dwa_ref[k] += pa[k * hd:(k + 1) * hd, k * hd:(k + 1) * hd]
            dwx_ref[k] += px[k * hd:(k + 1) * hd, k * hd:(k + 1) * hd]
        d_xc = d_xc + _dot_nt(dpr, wa) + _dot_nt(dpi, wx)

        nxt = nextd_s[...]
        xp = xp_ref[...]
        dxp = cw_ref[3:4, :] * d_xc
        dcw_ref[3:4, :] += jnp.sum(xp * d_xc, axis=0, keepdims=True)
        for j in (1, 2, 3):
            uj = up(d_xc, pltpu.roll(nxt, tc - j, 0), j)
            dxp = dxp + cw_ref[3 - j:4 - j, :] * uj
            dcw_ref[3 - j:4 - j, :] += jnp.sum(xp * uj, axis=0, keepdims=True)
        dcb_ref[...] += jnp.sum(d_xc, axis=0, keepdims=True)
        nextd_s[...] = d_xc
        dxp_ref[...] = dxp.astype(BF16)

    vec = pl.BlockSpec((1, cb), lambda j, c: (0, j))
    blk = pl.BlockSpec((tc, cb), lambda j, c: (ntc - 1 - c, j))
    mat = pl.BlockSpec((per, hd, hd), lambda j, c: (j, 0, 0))
    cwb = pl.BlockSpec((4, cb), lambda j, c: (0, j))
    return _call(
        body, name="lru_bwd", grid=(ncb, ntc),
        in_specs=[
            blk, blk, blk,
            pl.BlockSpec((tc, cb), lambda j, c: (jnp.maximum(ntc - 2 - c, 0), j)),
            blk, cwb, mat, vec, mat, vec, vec,
        ],
        out_specs=[blk, mat, vec, mat, vec, vec, cwb, vec],
        out_shape=[
            jax.ShapeDtypeStruct((t, dr), BF16),
            jax.ShapeDtypeStruct(w_a.shape, F32),
            jax.ShapeDtypeStruct((1, dr), F32),
            jax.ShapeDtypeStruct(w_x.shape, F32),
            jax.ShapeDtypeStruct((1, dr), F32),
            jax.ShapeDtypeStruct((1, dr), F32),
            jax.ShapeDtypeStruct((4, dr), F32),
            jax.ShapeDtypeStruct((1, dr), F32),
        ],
        scratch_shapes=[
            pltpu.VMEM((tc, cb), F32),
            pltpu.VMEM((1, cb), F32),
            pltpu.VMEM((1, cb), F32),
            pltpu.VMEM((tc, cb), F32),
            pltpu.VMEM((cb, cb), BF16),
            pltpu.VMEM((cb, cb), BF16),
        ],
        args=(dh, xc, h, h, proj, conv_w, w_a, b_a, w_x, b_x, lam), carry=carry)


def _pool_bwd(d_y_pool, p, pool_w, pool_scale):
    t, dp = d_y_pool.shape
    tc = _tile(t, 256)
    ntc = t // tc
    ng = len(POOL_WINDOWS)

    def body(dy_ref, p_ref, w_ref, sc_ref, dx_ref, dw_ref, dsc_ref, nz, n2, n4, n8, dp_s):
        c = pl.program_id(0)
        rc = ntc - 1 - c

        @pl.when(c == 0)
        def _():
            for s in (nz, n2, n4, n8):
                s[...] = jnp.zeros_like(s)
            dw_ref[...] = jnp.zeros_like(dw_ref)
            dsc_ref[...] = jnp.zeros_like(dsc_ref)

        for g in range(ng):
            sl = slice(g * POOL_GROUP_DIM, (g + 1) * POOL_GROUP_DIM)
            pg = p_ref[:, sl]
            dyg = dy_ref[:, sl]
            wg = w_ref[g].astype(BF16)
            q = _dot_nn(pg, wg)
            dsc_ref[:, sl] += jnp.sum(dyg * q, axis=0, keepdims=True)
            dpw = (dyg * sc_ref[:, sl]).astype(BF16)
            dw_ref[g] += _dot_tn(pg, dpw)
            dp_s[:, sl] = _dot_nt(dpw, wg)

        dpv = dp_s[...]
        row = lax.broadcasted_iota(jnp.int32, dpv.shape, 0)
        col = lax.broadcasted_iota(jnp.int32, dpv.shape, 1)
        win = _pool_select(col, POOL_WINDOWS)
        cnt = jnp.minimum(rc * tc + row + 1, win).astype(F32)
        z = dpv / cnt

        def up(v, nv, j):
            return jnp.where(row < tc - j, pltpu.roll(v, tc - j, 0), pltpu.roll(nv[...], tc - j, 0))

        u2 = z + up(z, nz, 1)
        u4 = u2 + up(u2, n2, 2)
        u8 = u4 + up(u4, n4, 4)
        u16 = u8 + up(u8, n8, 8)
        nz[...] = z
        n2[...] = u2
        n4[...] = u4
        n8[...] = u8
        dx_ref[...] = (_pool_select(col, (u2, u4, u8, u16)) - dpv).astype(BF16)

    blk = pl.BlockSpec((tc, dp), lambda c: (ntc - 1 - c, 0))
    full_w = pl.BlockSpec(pool_w.shape, lambda c: (0, 0, 0))
    vec = pl.BlockSpec((1, dp), lambda c: (0, 0))
    return _call(
        body, name="pool_bwd", grid=(ntc,),
        in_specs=[blk, blk, full_w, vec],
        out_specs=[blk, full_w, vec],
        out_shape=[
            jax.ShapeDtypeStruct((t, dp), BF16),
            jax.ShapeDtypeStruct(pool_w.shape, F32),
            jax.ShapeDtypeStruct((1, dp), F32),
        ],
        scratch_shapes=[pltpu.VMEM((tc, dp), F32)] * 5,
        args=(d_y_pool, p, pool_w, pool_scale))[0]


def _win_bwd_norm(parts, w_int, dx2, x, g1, carry=None):
    t, d = x.shape
    tk = 512
    tt = _tile(t, 512)
    bounds = []
    k0 = 0
    for part in parts:
        assert part.shape[1] % tk == 0
        bounds.append((k0, k0 + part.shape[1] // tk))
        k0 += part.shape[1] // tk
    nk = k0
    assert nk * tk == w_int.shape[0]
    np_ = len(parts)

    def body(*refs):
        p_refs = refs[:np_]
        w_ref, dx2_ref, x_ref, g_ref, gx_ref, dg_ref, acc = refs[np_:]
        i, kk = pl.program_id(0), pl.program_id(1)

        @pl.when(kk == 0)
        def _():
            acc[...] = jnp.zeros_like(acc)

        @pl.when((i == 0) & (kk == 0))
        def _():
            dg_ref[...] = jnp.zeros_like(dg_ref)

        for (lo, hi), p_ref in zip(bounds, p_refs):
            @pl.when((kk >= lo) & (kk < hi))
            def _(p_ref=p_ref):
                acc[...] += _dot_nn(p_ref[...], w_ref[...])

        @pl.when(kk == nk - 1)
        def _():
            def tail(rows):
                xhat, r = _rms_hat(x_ref[rows, :])
                dx, dg = _rms_bwd(acc[rows, :], xhat, r, g_ref[...])
                gx_ref[rows, :] = dx2_ref[rows, :] + dx
                dg_ref[...] += dg

            _row_chunks(tt, tail)

    def part_spec(lo, hi):
        return pl.BlockSpec((tt, tk), lambda i, kk: (i, jnp.clip(kk - lo, 0, hi - lo - 1)))

    row = pl.BlockSpec((tt, d), lambda i, kk: (i, 0))
    vec = pl.BlockSpec((1, d), lambda i, kk: (0, 0))
    return _call(
        body, name="win_bwd_norm", grid=(t // tt, nk),
        in_specs=[part_spec(lo, hi) for lo, hi in bounds]
        + [pl.BlockSpec((tk, d), lambda i, kk: (kk, 0)), row, row, vec],
        out_specs=[row, vec],
        out_shape=[jax.ShapeDtypeStruct((t, d), F32), jax.ShapeDtypeStruct((1, d), F32)],
        scratch_shapes=[pltpu.VMEM((tt, d), F32)],
        args=(*parts, w_int, dx2, x, g1), carry=carry)


def _adam_math(w, g, m, v):
    m = ADAM_B1 * m + (1.0 - ADAM_B1) * g
    v = ADAM_B2 * v + (1.0 - ADAM_B2) * (g * g)
    m_hat = m / (1.0 - ADAM_B1 ** ADAM_STEP)
    v_hat = v / (1.0 - ADAM_B2 ** ADAM_STEP)
    delta = -ADAM_LR * (m_hat / (jnp.sqrt(v_hat) + ADAM_EPS) + ADAM_WD * w)
    return delta, m, v


def _adamw_big(ws, gs, ms, vs):
    n = len(ws)
    nb = 8

    def body(*refs):
        for a in range(n):
            w_ref, g_ref, m_ref, v_ref = refs[4 * a:4 * a + 4]
            d_ref, nm_ref, nv_ref = refs[4 * n + 3 * a:4 * n + 3 * a + 3]
            dl, m, v = _adam_math(w_ref[...], g_ref[...], m_ref[...], v_ref[...])
            d_ref[...] = dl
            nm_ref[...] = m
            nv_ref[...] = v

    in_specs, out_specs, out_shape, args = [], [], [], []
    for w, g, m, v in zip(ws, gs, ms, vs):
        rows, cols = w.shape
        blk = pl.BlockSpec((rows // nb, cols), lambda i: (i, 0))
        in_specs += [blk] * 4
        args += [w, g, m, v]
        out_specs += [blk] * 3
        out_shape += [jax.ShapeDtypeStruct(w.shape, F32)] * 3
    outs = _call(body, name="adamw_big", grid=(nb,), in_specs=in_specs, out_specs=out_specs,
                 out_shape=out_shape, args=args)[0]
    return [tuple(outs[3 * a:3 * a + 3]) for a in range(n)]


SMALL_ORDER = ("norm_mix_pre", "norm_mix_post", "norm_mlp_pre", "norm_mlp_post", "b_gate", "conv_w", "conv_b",
               "lru_w_a", "lru_b_a", "lru_w_x", "lru_b_x", "lru_lambda", "pool_w", "pool_scale")
VEC_ROW = dict(norm_mix_pre=0, norm_mix_post=1, norm_mlp_pre=2, norm_mlp_post=3, conv_b=6, lru_b_a=7,
               lru_b_x=8, lru_lambda=9)
ROW_B_GATE, ROW_POOL_SCALE, ROW_CONV_W, ROW_LOSS, N_VEC_ROWS = 4, 10, 11, 15, 16


def _adamw_small(vec_parts, g_pool, g_wa, g_wx, me, params):
    d = vec_parts.shape[2]
    names = SMALL_ORDER
    n = len(names)
    cw_cols = params["conv_w"][0].shape[2]

    def body(me_ref, vec_ref, vecc_ref, gp_ref, gwa_ref, gwx_ref, *refs):
        wmv = refs[:3 * n]
        loss_ref = refs[3 * n]
        outs = refs[3 * n + 1:3 * n + 1 + 4 * n]
        vs, vsc = refs[3 * n + 1 + 4 * n:]
        acc, accc = vec_ref[0], vecc_ref[0]
        for k in range(1, N_DEV):
            acc = acc + vec_ref[k]
            accc = accc + vecc_ref[k]
        vs[...] = acc
        vsc[...] = accc
        loss_ref[...] = vs[ROW_LOSS:ROW_LOSS + 1, 0:128]

        def upd(a, g, idx):
            w_ref, m_ref, v_ref = wmv[3 * a:3 * a + 3]
            g_ref, d_ref, nm_ref, nv_ref = outs[4 * a:4 * a + 4]
            dl, m, v = _adam_math(w_ref[idx], g, m_ref[idx], v_ref[idx])
            g_ref[idx] = g
            d_ref[idx] = dl
            nm_ref[idx] = m
            nv_ref[idx] = v

        for a, name in enumerate(names):
            if name in VEC_ROW:
                r = VEC_ROW[name]
                upd(a, vs[r:r + 1, :], (slice(None), slice(None)))
            elif name == "b_gate":
                for half in range(2):
                    r = ROW_B_GATE + half
                    upd(a, vs[r:r + 1, :], (slice(None), slice(half * d, (half + 1) * d)))
            elif name == "pool_scale":
                width = params[name][0].shape[1]
                upd(a, vs[ROW_POOL_SCALE:ROW_POOL_SCALE + 1, 0:width], (slice(None), slice(None)))
            elif name == "conv_w":
                upd(a, vsc[ROW_CONV_W:ROW_CONV_W + 4, :], (0,))
            elif name == "pool_w":
                upd(a, gp_ref[...], (Ellipsis,))
            elif name == "lru_w_a":
                upd(a, gwa_ref[...], (Ellipsis,))
            elif name == "lru_w_x":
                upd(a, gwx_ref[...], (Ellipsis,))
            else:
                raise ValueError(name)

    def whole(shape):
        nd = len(shape)
        return pl.BlockSpec(tuple(shape), lambda i, me_ref: (0,) * nd)

    in_specs = [
        whole(vec_parts.shape),
        pl.BlockSpec((N_DEV, N_VEC_ROWS, cw_cols), lambda i, me_ref: (0, 0, me_ref[0])),
        whole(g_pool.shape), whole(g_wa.shape), whole(g_wx.shape),
    ]
    args = [vec_parts, vec_parts, g_pool, g_wa, g_wx]
    out_specs = [whole((1, 128))]
    out_shape = [jax.ShapeDtypeStruct((1, 128), F32)]
    for name in names:
        for arr in params[name]:
            in_specs.append(whole(arr.shape))
            args.append(arr)
        shp = params[name][0].shape
        out_specs += [whole(shp)] * 4
        out_shape += [jax.ShapeDtypeStruct(shp, F32)] * 4
    grid_spec = pltpu.PrefetchScalarGridSpec(
        num_scalar_prefetch=1, grid=(1,), in_specs=in_specs, out_specs=out_specs,
        scratch_shapes=[pltpu.VMEM((N_VEC_ROWS, d), F32), pltpu.VMEM((N_VEC_ROWS, cw_cols), F32)])
    outs = pl.pallas_call(
        body, name="adamw_small", grid_spec=grid_spec, out_shape=out_shape,
        compiler_params=pltpu.CompilerParams(
            dimension_semantics=("arbitrary",), vmem_limit_bytes=V7X_VMEM_LIMIT_BYTES),
    )(me, *args)
    return outs[0], {name: tuple(outs[1 + 4 * a:5 + 4 * a]) for a, name in enumerate(names)}


def _rs_sum(full, recv, shard_ids, slot_ids, name):
    r, rest = recv.shape[1], tuple(recv.shape[2:])
    zeros = (0,) * len(rest)
    send_dtype = recv.dtype

    def body(sh_ref, sl_ref, full_ref, recv_ref, own_ref, send_ref):
        s = pl.program_id(0)
        v = full_ref[...] + recv_ref[...].astype(F32)

        @pl.when(s == 0)
        def _():
            own_ref[...] = v

        @pl.when(s > 0)
        def _():
            send_ref[...] = v.astype(send_dtype)

    grid_spec = pltpu.PrefetchScalarGridSpec(
        num_scalar_prefetch=2,
        grid=(4,),
        in_specs=[
            pl.BlockSpec((r,) + rest, lambda s, sh, sl: (sh[s],) + zeros),
            pl.BlockSpec((None, r) + rest, lambda s, sh, sl: (sl[s], 0) + zeros),
        ],
        out_specs=[
            pl.BlockSpec((None, r) + rest, lambda s, sh, sl: (0, 0) + zeros),
            pl.BlockSpec((None, r) + rest, lambda s, sh, sl: (jnp.maximum(s - 1, 0), 0) + zeros),
        ],
    )
    return pl.pallas_call(
        body,
        name=name,
        grid_spec=grid_spec,
        out_shape=[jax.ShapeDtypeStruct((1, r) + rest, F32), jax.ShapeDtypeStruct((3, r) + rest, send_dtype)],
        compiler_params=pltpu.CompilerParams(
            dimension_semantics=("arbitrary",), vmem_limit_bytes=V7X_VMEM_LIMIT_BYTES),
    )(shard_ids, slot_ids, full, recv)


def _finals(pairs):
    nb = 4
    n = len(pairs)

    def body(*refs):
        for a in range(n):
            own_ref, recv_ref = refs[2 * a], refs[2 * a + 1]
            acc = own_ref[...]
            for k in range(3):
                acc = acc + recv_ref[k].astype(F32)
            refs[2 * n + a][...] = acc

    in_specs, out_specs, out_shape, args = [], [], [], []
    for own, recv in pairs:
        _, rows, cols = own.shape
        in_specs += [pl.BlockSpec((None, rows // nb, cols), lambda i: (0, i, 0)),
                     pl.BlockSpec((3, rows // nb, cols), lambda i: (0, i, 0))]
        args += [own, recv]
        out_specs.append(pl.BlockSpec((rows // nb, cols), lambda i: (i, 0)))
        out_shape.append(jax.ShapeDtypeStruct((rows, cols), F32))
    return _call(body, name="rs_finals", grid=(nb,), in_specs=in_specs, out_specs=out_specs,
                 out_shape=out_shape, args=args)[0]


def _rs_level1(fulls_f32, fulls_send, tag):
    x, y, c = _place()
    recv1 = _run_plan(_rs_sibling_plan(fulls_send), "rs_sibling_" + tag)
    qs = jnp.stack([2 * x + y, 2 * (1 - x) + y, 2 * x + (1 - y), 2 * (1 - x) + (1 - y)]).astype(jnp.int32)
    shard_ids = 2 * qs + c
    return [_rs_sum(f32, r1, shard_ids, qs, f"rs_sum_{tag}{a}")
            for a, (f32, r1) in enumerate(zip(fulls_f32, recv1))]


def _rows(g):
    return g.reshape(g.shape[0] * g.shape[1], g.shape[2])


def kernel(x, norm_mix_pre, norm_mix_post, norm_mlp_pre, norm_mlp_post, w_in, b_gate, conv_w, conv_b, lru_w_a, lru_b_a, lru_w_x, lru_b_x, lru_lambda, pool_w, pool_scale, w_lru_up, w_pool_up, w_o, w_ff1, w_ff2, loss_target, m_norm_mix_pre, m_norm_mix_post, m_norm_mlp_pre, m_norm_mlp_post, m_w_in, m_b_gate, m_conv_w, m_conv_b, m_lru_w_a, m_lru_b_a, m_lru_w_x, m_lru_b_x, m_lru_lambda, m_pool_w, m_pool_scale, m_w_lru_up, m_w_pool_up, m_w_o, m_w_ff1, m_w_ff2, v_norm_mix_pre, v_norm_mix_post, v_norm_mlp_pre, v_norm_mlp_post, v_w_in, v_b_gate, v_conv_w, v_conv_b, v_lru_w_a, v_lru_b_a, v_lru_w_x, v_lru_b_x, v_lru_lambda, v_pool_w, v_pool_scale, v_w_lru_up, v_w_pool_up, v_w_o, v_w_ff1, v_w_ff2):
    t, d = x.shape[1], x.shape[2]
    d_rnn = conv_b.shape[1]
    d_pool = pool_scale.shape[1]
    per = LRU_CB // LRU_HEAD_DIM
    xi, yi, ci = _place()
    me = 4 * xi + 2 * yi + ci

    x2d = x[0]
    tgt = loss_target[0]

    s_in = w_in[0].T.astype(BF16)
    s_lu = w_lru_up[0].astype(BF16)
    s_pu = w_pool_up[0].T.astype(BF16)
    s_o = w_o[0].astype(BF16)
    s_f1 = w_ff1[0].T.astype(BF16)
    s_f2 = w_ff2[0].astype(BF16)
    s_cw = jnp.pad(conv_w[0], ((0, 4), (0, 0)))

    g_in, g_cw = _run_plan(_ag_plan([s_in, s_cw]), "ag_w_in")
    w_int = _rows(g_in)
    conv_w_full = jnp.transpose(g_cw[:, :4, :], (1, 0, 2)).reshape(4, d_rnn)

    wa_bd, wx_bd = lru_w_a[0], lru_w_x[0]
    pw = pool_w[0]
    pw_bf = pw.astype(BF16)

    pool_block = (2 * d_rnn) // d_pool
    ga_block = (2 * d_rnn + d_pool) // 512
    gb_block = ga_block + d // 512
    g_block = d_rnn // 512

    (proj, h1), (g_lu, g_pu, g_o) = _norm_proj(x2d, norm_mix_pre, w_int, carry=_ag_plan([s_lu, s_pu, s_o]))
    w_lu, w_put, w_og = _rows(g_lu), _rows(g_pu), _rows(g_o)
    (y_lru, h, xc), (g_f1,) = _lru_fwd(proj, conv_w_full, conv_b, wa_bd, lru_b_a, wx_bd, lru_b_x, lru_lambda,
                                       carry=_ag_plan([s_f1]))
    w_f1t = _rows(g_f1)
    y_pool, p = _pool_fwd(proj, pw_bf, pool_scale, pool_block)
    br_a, br_b, mix = _branch_mix(y_lru, y_pool, w_lu, w_put, proj, b_gate, ga_block, gb_block)
    m, x2, h3 = _wo_norm(mix, w_og, x2d, norm_mix_post, norm_mlp_pre)
    (rf, act), (g_f2,) = _ff1(h3, w_f1t, carry=_ag_plan([s_f2]))
    w_f2 = _rows(g_f2)
    dy, df, dg4, loss_part = _ff2_loss(act, w_f2, x2, norm_mlp_post, tgt)

    d_f1 = _ff2_bwd(df, w_f2, rf)
    (gw_ff2_32, gw_ff2_16), _ = _wgrad(act, df, "wgrad_ff2")
    ((own_ff2, send_ff2),) = _rs_level1([gw_ff2_32], [gw_ff2_16], "ff2")
    (gw_ff1_32, gw_ff1_16), (r2_ff2,) = _wgrad(d_f1, h3, "wgrad_ff1", carry=_rs_chips_plan([send_ff2]))
    ((own_ff1, send_ff1),) = _rs_level1([gw_ff1_32], [gw_ff1_16], "ff1")
    (dx2, dm, dg3, dg2), (r2_ff1,) = _ff1_bwd_norms(d_f1, w_f1t, dy, x2, norm_mlp_pre, m, norm_mix_post,
                                                    carry=_rs_chips_plan([send_ff1]))
    (gw_o_32, gw_o_16), _ = _wgrad(mix, dm, "wgrad_o")
    d_br_a, d_br_b, p_ga, p_gb, dbg_a, dbg_b = _wo_bwd_mix(dm, w_og, br_a, br_b, proj, b_gate, ga_block, gb_block)
    (gw_lu_32, gw_lu_16), _ = _wgrad(y_lru, d_br_a, "wgrad_lru_up")
    (gw_pu_32, gw_pu_16), _ = _wgrad(d_br_b, y_pool, "wgrad_pool_up")
    mid = _rs_level1([gw_o_32, gw_lu_32, gw_pu_32.reshape(-1, d)],
                     [gw_o_16, gw_lu_16, gw_pu_16.reshape(-1, d)], "mid")
    dh, p_g = _lru_up_bwd(d_br_a, w_lu, proj, h, g_block)
    d_y_pool = _pool_up_bwd(d_br_b, w_put)
    (p_x, dwa, db_a, dwx, db_x, dlam, dconv_w, dconv_b), r2_mid = _lru_bwd(
        dh, xc, h, proj, conv_w_full, wa_bd, lru_b_a, wx_bd, lru_b_x, lru_lambda,
        carry=_rs_chips_plan([s for _, s in mid]))
    p_p, dpool_w, dpool_scale = _pool_bwd(d_y_pool, p, pw, pool_scale)
    parts = [p_x, p_g, p_p, p_ga, p_gb]
    gw_in = None
    row_off = 0
    for k, part in enumerate(parts):
        gw_in, _ = _wgrad(part, h1, f"wgrad_in{k}", prev=gw_in, row_off=row_off, rows=w_int.shape[0])
        row_off += part.shape[1]
    tail = _rs_level1([gw_in[0], dpool_w.reshape(N_DEV, -1, POOL_GROUP_DIM), dwa, dwx],
                      [gw_in[1], dpool_w.reshape(N_DEV, -1, POOL_GROUP_DIM), dwa, dwx], "in")
    (grad_x, dg1), r2_tail = _win_bwd_norm(parts, w_int, dx2, x2d, norm_mix_pre,
                                           carry=_rs_chips_plan([s for _, s in tail]))

    def flat2(a):
        return a.reshape(a.shape[0], -1, a.shape[-1])

    fin = _finals([
        (tail[0][0], r2_tail[0]), (mid[1][0], r2_mid[1]), (mid[2][0], r2_mid[2]), (mid[0][0], r2_mid[0]),
        (own_ff1, r2_ff1), (own_ff2, r2_ff2),
        (flat2(tail[1][0]), flat2(r2_tail[1])), (flat2(tail[2][0]), flat2(r2_tail[2])),
        (flat2(tail[3][0]), flat2(r2_tail[3])),
    ])
    g_w_in = fin[0].T
    g_w_lru_up = fin[1]
    g_w_pool_up = fin[2].reshape(d // N_DEV, d_pool).T
    g_w_o = fin[3]
    g_w_ff1 = fin[4].T
    g_w_ff2 = fin[5]

    def pad_row(a):
        return jnp.pad(a, ((0, 0), (0, d - a.shape[1])))

    vecs = jnp.concatenate([dg1, dg2, dg3, dg4, dbg_a, dbg_b, dconv_b, db_a, db_x, dlam,
                            pad_row(dpool_scale), dconv_w, pad_row(loss_part)], axis=0)
    assert vecs.shape[0] == N_VEC_ROWS
    vec_parts, g_pool, g_wa, g_wx = _run_plan(_ag_plan([vecs, fin[6], fin[7], fin[8]]), "ag_tail")

    small = dict(
        norm_mix_pre=(norm_mix_pre, m_norm_mix_pre, v_norm_mix_pre),
        norm_mix_post=(norm_mix_post, m_norm_mix_post, v_norm_mix_post),
        norm_mlp_pre=(norm_mlp_pre, m_norm_mlp_pre, v_norm_mlp_pre),
        norm_mlp_post=(norm_mlp_post, m_norm_mlp_post, v_norm_mlp_post),
        b_gate=(b_gate, m_b_gate, v_b_gate), conv_w=(conv_w, m_conv_w, v_conv_w),
        conv_b=(conv_b, m_conv_b, v_conv_b), lru_w_a=(lru_w_a, m_lru_w_a, v_lru_w_a),
        lru_b_a=(lru_b_a, m_lru_b_a, v_lru_b_a), lru_w_x=(lru_w_x, m_lru_w_x, v_lru_w_x),
        lru_b_x=(lru_b_x, m_lru_b_x, v_lru_b_x), lru_lambda=(lru_lambda, m_lru_lambda, v_lru_lambda),
        pool_w=(pool_w, m_pool_w, v_pool_w), pool_scale=(pool_scale, m_pool_scale, v_pool_scale))
    loss_row, small_out = _adamw_small(
        vec_parts, g_pool.reshape(pool_w.shape), g_wa.reshape(lru_w_a.shape), g_wx.reshape(lru_w_x.shape),
        jnp.reshape(me, (1,)).astype(jnp.int32), small)
    grads = {n: o[0] for n, o in small_out.items()}
    delta = {n: o[1] for n, o in small_out.items()}
    new_m = {n: o[2] for n, o in small_out.items()}
    new_v = {n: o[3] for n, o in small_out.items()}

    big_names = ["w_in", "w_lru_up", "w_pool_up", "w_o", "w_ff1", "w_ff2"]
    big_w = [w_in, w_lru_up, w_pool_up, w_o, w_ff1, w_ff2]
    big_g = [g_w_in, g_w_lru_up, g_w_pool_up, g_w_o, g_w_ff1, g_w_ff2]
    big_m = [m_w_in, m_w_lru_up, m_w_pool_up, m_w_o, m_w_ff1, m_w_ff2]
    big_v = [v_w_in, v_w_lru_up, v_w_pool_up, v_w_o, v_w_ff1, v_w_ff2]
    big_out = _adamw_big([w[0] for w in big_w], big_g, [mm[0] for mm in big_m], [vv[0] for vv in big_v])
    for name, g, (dl, nm, nv) in zip(big_names, big_g, big_out):
        grads[name], delta[name], new_m[name], new_v[name] = g[None], dl[None], nm[None], nv[None]

    loss = loss_row[0, 0]
    order = ["norm_mix_pre", "norm_mix_post", "norm_mlp_pre", "norm_mlp_post", "w_in", "b_gate", "conv_w",
             "conv_b", "lru_w_a", "lru_b_a", "lru_w_x", "lru_b_x", "lru_lambda", "pool_w", "pool_scale",
             "w_lru_up", "w_pool_up", "w_o", "w_ff1", "w_ff2"]
    return (loss, grad_x[None], *[grads[n] for n in order], *[delta[n] for n in order],
            *[new_m[n] for n in order], *[new_v[n] for n in order])
```

```python
import functools
import math
import operator
import types

import jax
import jax.numpy as jnp
from jax import lax
from jax.experimental import pallas as pl
from jax.experimental.pallas import tpu as pltpu

F32 = jnp.float32
BF16 = jnp.bfloat16
NORM_EPS = 1e-6
LRU_C = 8.0
N_LRU_HEADS = 16
LRU_HEAD_DIM = 64
POOL_WINDOWS = (2, 4, 8, 16)
POOL_GROUP_DIM = 128
ADAM_LR = 0.001
ADAM_B1 = 0.9
ADAM_B2 = 0.999
ADAM_EPS = 1e-08
ADAM_WD = 0.01
ADAM_STEP = 10
N_DEV = 8
V7X_VMEM_LIMIT_BYTES = 56 * 1024 * 1024
LRU_CB = 256
MESH = pl.DeviceIdType.MESH
ANY = pl.BlockSpec(memory_space=pl.ANY)


def _tile(n, pref):
    t = min(n, pref)
    assert n % t == 0, (n, pref)
    return t


def _dot_nn(a, b):
    return lax.dot_general(a, b, (((1,), (0,)), ((), ())), preferred_element_type=F32)


def _dot_nt(a, b):
    return lax.dot_general(a, b, (((1,), (1,)), ((), ())), preferred_element_type=F32)


def _dot_tn(a, b):
    return lax.dot_general(a, b, (((0,), (0,)), ((), ())), preferred_element_type=F32)


def _row_chunks(n_rows, fn, chunk=256):
    chunk = min(chunk, n_rows)
    assert n_rows % chunk == 0

    def step(r, carry):
        fn(pl.ds(pl.multiple_of(r * chunk, chunk), chunk))
        return carry

    lax.fori_loop(0, n_rows // chunk, step, 0)


def _sig(x):
    return 1.0 / (1.0 + jnp.exp(-x))


def _rms_hat(x):
    r = lax.rsqrt(jnp.mean(x * x, axis=-1, keepdims=True) + NORM_EPS)
    return x * r, r


def _rms_bwd(dn, xhat, r, g):
    q = dn * g
    dx = r * (q - xhat * jnp.mean(q * xhat, axis=-1, keepdims=True))
    dg = jnp.sum(dn * xhat, axis=0, keepdims=True)
    return dx, dg


_GELU_K = math.sqrt(2.0 / math.pi)
_GELU_C = 0.044715


def _gelu_and_grad(g):
    t = jnp.tanh(_GELU_K * (g + _GELU_C * g * g * g))
    val = 0.5 * g * (1.0 + t)
    grad = 0.5 * (1.0 + t) + 0.5 * g * (1.0 - t * t) * (_GELU_K * (1.0 + 3.0 * _GELU_C * g * g))
    return val, grad


def _softplus_neg(lam):
    z = -lam
    e = jnp.exp(-jnp.abs(z))
    u = 1.0 + e
    d = u - 1.0
    l1p = jnp.where(d == 0.0, e, jnp.log(u) * (e / jnp.where(d == 0.0, 1.0, d)))
    return jnp.maximum(z, 0.0) + l1p


def _lru_gates(xc, wa, ba, wx, bx, lam):
    xcb = xc.astype(BF16)
    r = _sig(_dot_nn(xcb, wa) + ba)
    i = _sig(_dot_nn(xcb, wx) + bx)
    sp = _softplus_neg(lam)
    log_a = (-LRU_C) * r * sp
    a = jnp.exp(log_a)
    mult = jnp.sqrt(-jnp.tanh(log_a) * (1.0 + a * a))
    return xcb, r, i, sp, log_a, a, mult


def _place():
    return lax.axis_index("x"), lax.axis_index("y"), lax.axis_index("c")


def _ag_plan(shards):
    na = len(shards)

    def parts(ins, outs, sems):
        send_sems, recv_sems, local_sems = sems
        x, y, c = _place()
        me, sibling = (x, y, c), (x, y, 1 - c)
        chips = [(1 - x, y), (x, 1 - y), (1 - x, 1 - y)]

        def slot(a, px, py, pc):
            return outs[a].at[4 * px + 2 * py + pc]

        def copy(a, k, block, to, src=None):
            return pltpu.make_async_remote_copy(
                src_ref=slot(a, *block) if src is None else src,
                dst_ref=slot(a, *block),
                send_sem=send_sems.at[a * 7 + k],
                recv_sem=recv_sems.at[a * 7 + k],
                device_id=to,
                device_id_type=MESH,
            )

        mine = [pltpu.make_async_copy(ins[a], slot(a, *me), local_sems.at[a]) for a in range(na)]
        first = []
        for a in range(na):
            first.append(copy(a, 0, me, sibling, src=ins[a]))
            first += [copy(a, 1 + j, me, (*chip, c), src=ins[a]) for j, chip in enumerate(chips)]
        return me, sibling, chips, c, copy, mine, first

    def start(ins, outs, sems):
        _, _, _, _, _, mine, first = parts(ins, outs, sems)
        for cp in mine + first:
            cp.start()

    def finish(ins, outs, sems):
        me, sibling, chips, c, copy, mine, first = parts(ins, outs, sems)
        passed = []
        for j, chip in enumerate(chips):
            for a in range(na):
                copy(a, 1 + j, (*chip, c), me).wait_recv()
                fwd = copy(a, 4 + j, (*chip, c), sibling)
                fwd.start()
                passed.append(fwd)
        for a in range(na):
            copy(a, 0, sibling, me).wait_recv()
            for j, chip in enumerate(chips):
                copy(a, 4 + j, (*chip, 1 - c), me).wait_recv()
        for cp in first + passed:
            cp.wait_send()
        for cp in mine:
            cp.wait()

    return types.SimpleNamespace(
        ins=list(shards),
        out_shapes=[jax.ShapeDtypeStruct((N_DEV,) + s.shape, s.dtype) for s in shards],
        sems=[pltpu.SemaphoreType.DMA((7 * na,)), pltpu.SemaphoreType.DMA((7 * na,)),
              pltpu.SemaphoreType.DMA((na,))],
        start=start, finish=finish)


def _rs_sibling_plan(fulls):
    na = len(fulls)
    rs = [f.shape[0] // N_DEV for f in fulls]

    def copies(ins, outs, sems):
        send_sems, recv_sems = sems
        x, y, c = _place()
        out = []
        for a in range(na):
            for q in range(4):
                shard = 2 * q + (1 - c)
                out.append(pltpu.make_async_remote_copy(
                    src_ref=ins[a].at[pl.ds(shard * rs[a], rs[a])],
                    dst_ref=outs[a].at[q],
                    send_sem=send_sems.at[a * 4 + q],
                    recv_sem=recv_sems.at[a * 4 + q],
                    device_id=(x, y, 1 - c),
                    device_id_type=MESH,
                ))
        return out

    def start(ins, outs, sems):
        for cp in copies(ins, outs, sems):
            cp.start()

    def finish(ins, outs, sems):
        for cp in copies(ins, outs, sems):
            cp.wait()

    return types.SimpleNamespace(
        ins=list(fulls),
        out_shapes=[jax.ShapeDtypeStruct((4, r) + f.shape[1:], f.dtype) for r, f in zip(rs, fulls)],
        sems=[pltpu.SemaphoreType.DMA((4 * na,)), pltpu.SemaphoreType.DMA((4 * na,))],
        start=start, finish=finish)


def _rs_chips_plan(sends):
    na = len(sends)

    def copies(ins, outs, sems):
        send_sems, recv_sems = sems
        x, y, c = _place()
        chips = [(1 - x, y), (x, 1 - y), (1 - x, 1 - y)]
        out = []
        for a in range(na):
            for k, chip in enumerate(chips):
                out.append(pltpu.make_async_remote_copy(
                    src_ref=ins[a].at[k],
                    dst_ref=outs[a].at[k],
                    send_sem=send_sems.at[a * 3 + k],
                    recv_sem=recv_sems.at[a * 3 + k],
                    device_id=(*chip, c),
                    device_id_type=MESH,
                ))
        return out

    def start(ins, outs, sems):
        for cp in copies(ins, outs, sems):
            cp.start()

    def finish(ins, outs, sems):
        for cp in copies(ins, outs, sems):
            cp.wait()

    return types.SimpleNamespace(
        ins=list(sends),
        out_shapes=[jax.ShapeDtypeStruct(s.shape, s.dtype) for s in sends],
        sems=[pltpu.SemaphoreType.DMA((3 * na,)), pltpu.SemaphoreType.DMA((3 * na,))],
        start=start, finish=finish)


def _in_hbm(args):
    return [pltpu.with_memory_space_constraint(a, pltpu.HBM) for a in args]


def _run_plan(plan, name):
    n_in, n_out = len(plan.ins), len(plan.out_shapes)

    def body(*refs):
        ins, outs, sems = refs[:n_in], refs[n_in:n_in + n_out], refs[n_in + n_out:]
        plan.start(ins, outs, sems)
        plan.finish(ins, outs, sems)

    return pl.pallas_call(
        body,
        name=name,
        in_specs=[ANY] * n_in,
        out_specs=[ANY] * n_out,
        out_shape=plan.out_shapes,
        scratch_shapes=plan.sems,
    )(*_in_hbm(plan.ins))


def _call(body, *, name, grid, in_specs, out_specs, out_shape, args, scratch_shapes=(), aliases=None,
          carry=None):
    n_in, n_out, n_scr = len(in_specs), len(out_shape), len(scratch_shapes)
    params = pltpu.CompilerParams(
        dimension_semantics=("arbitrary",) * len(grid), vmem_limit_bytes=V7X_VMEM_LIMIT_BYTES)
    if carry is None:
        outs = pl.pallas_call(
            body, name=name, grid=grid, in_specs=list(in_specs), out_specs=list(out_specs),
            out_shape=list(out_shape), scratch_shapes=list(scratch_shapes),
            input_output_aliases=aliases or {}, compiler_params=params)(*_in_hbm(args))
        return list(outs), []
    c_in, c_out = len(carry.ins), len(carry.out_shapes)

    def full(*refs):
        p = 0
        ins = refs[p:p + n_in]
        p += n_in
        cins = refs[p:p + c_in]
        p += c_in
        outs = refs[p:p + n_out]
        p += n_out
        couts = refs[p:p + c_out]
        p += c_out
        scr = refs[p:p + n_scr]
        csems = refs[p + n_scr:]
        ids = [pl.program_id(a) for a in range(len(grid))]
        first = functools.reduce(operator.and_, [i == 0 for i in ids])
        last = functools.reduce(operator.and_, [i == g - 1 for i, g in zip(ids, grid)])

        @pl.when(first)
        def _():
            carry.start(cins, couts, csems)

        body(*ins, *outs, *scr)

        @pl.when(last)
        def _():
            carry.finish(cins, couts, csems)

    outs = pl.pallas_call(
        full, name=name, grid=grid,
        in_specs=list(in_specs) + [ANY] * c_in,
        out_specs=list(out_specs) + [ANY] * c_out,
        out_shape=list(out_shape) + list(carry.out_shapes),
        scratch_shapes=list(scratch_shapes) + list(carry.sems),
        input_output_aliases=aliases or {}, compiler_params=params)(*_in_hbm(args), *_in_hbm(carry.ins))
    return list(outs[:n_out]), list(outs[n_out:])


def _norm_proj(x, g1, w_int, carry=None):
    t, d = x.shape
    n = w_int.shape[0]
    tt, tn = _tile(t, 2048), _tile(n, 512)

    def body(x_ref, g_ref, w_ref, proj_ref, h1_ref, h1_s):
        @pl.when(pl.program_id(1) == 0)
        def _():
            def norm_rows(rows):
                xhat, _ = _rms_hat(x_ref[rows, :])
                h = (xhat * g_ref[...]).astype(BF16)
                h1_s[rows, :] = h
                h1_ref[rows, :] = h

            _row_chunks(tt, norm_rows)

        proj_ref[...] = _dot_nt(h1_s[...], w_ref[...])

    return _call(
        body, name="norm_proj", grid=(t // tt, n // tn),
        in_specs=[
            pl.BlockSpec((tt, d), lambda i, j: (i, 0)),
            pl.BlockSpec((1, d), lambda i, j: (0, 0)),
            pl.BlockSpec((tn, d), lambda i, j: (j, 0)),
        ],
        out_specs=[
            pl.BlockSpec((tt, tn), lambda i, j: (i, j)),
            pl.BlockSpec((tt, d), lambda i, j: (i, 0)),
        ],
        out_shape=[jax.ShapeDtypeStruct((t, n), F32), jax.ShapeDtypeStruct((t, d), BF16)],
        scratch_shapes=[pltpu.VMEM((tt, d), BF16)],
        args=(x, g1, w_int), carry=carry)


def _fill_block_diag(w_ref, bd_ref):
    bd_ref[...] = jnp.zeros_like(bd_ref)
    hd = LRU_HEAD_DIM
    for k in range(w_ref.shape[0]):
        bd_ref[k * hd:(k + 1) * hd, k * hd:(k + 1) * hd] = w_ref[k].astype(BF16)


def _lru_fwd(proj, conv_w, conv_b, w_a, b_a, w_x, b_x, lam, carry=None):
    t = proj.shape[0]
    dr = conv_b.shape[1]
    cb = LRU_CB
    tc = _tile(t, 256)
    ncb, ntc = dr // cb, t // tc

    def body(xp_ref, g_ref, cw_ref, cb_ref, wa_ref, ba_ref, wx_ref, bx_ref, lam_ref,
             y_ref, h_ref, xc_ref, prevx_s, hlast_s, wa_s, wx_s):
        c = pl.program_id(1)

        @pl.when(c == 0)
        def _():
            prevx_s[...] = jnp.zeros_like(prevx_s)
            hlast_s[...] = jnp.zeros_like(hlast_s)
            _fill_block_diag(wa_ref, wa_s)
            _fill_block_diag(wx_ref, wx_s)

        x = xp_ref[...]
        prev = prevx_s[...]
        row = lax.broadcasted_iota(jnp.int32, x.shape, 0)

        def sh(j):
            return jnp.where(row >= j, pltpu.roll(x, j, 0), pltpu.roll(prev, j, 0))

        xc = (cb_ref[...] + cw_ref[0:1, :] * sh(3) + cw_ref[1:2, :] * sh(2)
              + cw_ref[2:3, :] * sh(1) + cw_ref[3:4, :] * x)
        prevx_s[...] = x
        xc_ref[...] = xc
        _, _, i, _, _, a, mult = _lru_gates(xc, wa_s[...], ba_ref[...], wx_s[...], bx_ref[...],
                                            lam_ref[...])
        av, bv = a, mult * (i * xc)
        s = 1
        while s < tc:
            a_sh = jnp.where(row >= s, pltpu.roll(av, s, 0), 1.0)
            b_sh = jnp.where(row >= s, pltpu.roll(bv, s, 0), 0.0)
            bv = av * b_sh + bv
            av = av * a_sh
            s *= 2
        h = av * hlast_s[...] + bv
        h_ref[...] = h
        hlast_s[...] = h_ref[tc - 1:tc, :]
        gel, _ = _gelu_and_grad(g_ref[...])
        y_ref[...] = (h * gel).astype(BF16)

    vec = pl.BlockSpec((1, cb), lambda j, c: (0, j))
    blk = pl.BlockSpec((tc, cb), lambda j, c: (c, j))
    mat = pl.BlockSpec((cb // LRU_HEAD_DIM, LRU_HEAD_DIM, LRU_HEAD_DIM), lambda j, c: (j, 0, 0))
    return _call(
        body, name="lru_fwd", grid=(ncb, ntc),
        in_specs=[
            blk,
            pl.BlockSpec((tc, cb), lambda j, c: (c, ncb + j)),
            pl.BlockSpec((4, cb), lambda j, c: (0, j)),
            vec, mat, vec, mat, vec, vec,
        ],
        out_specs=[blk, blk, blk],
        out_shape=[
            jax.ShapeDtypeStruct((t, dr), BF16),
            jax.ShapeDtypeStruct((t, dr), F32),
            jax.ShapeDtypeStruct((t, dr), F32),
        ],
        scratch_shapes=[pltpu.VMEM((tc, cb), F32), pltpu.VMEM((1, cb), F32),
                        pltpu.VMEM((cb, cb), BF16), pltpu.VMEM((cb, cb), BF16)],
        args=(proj, proj, conv_w, conv_b, w_a, b_a, w_x, b_x, lam), carry=carry)


def _pool_select(col, vals):
    out = vals[3]
    for g in (2, 1, 0):
        out = jnp.where(col < (g + 1) * POOL_GROUP_DIM, vals[g], out)
    return out


def _pool_fwd(proj, pool_w, pool_scale, col_block):
    t = proj.shape[0]
    dp = pool_scale.shape[1]
    tc = _tile(t, 256)
    ntc = t // tc

    def body(x_ref, w_ref, sc_ref, y_ref, p_ref, px, p2, p4, p8):
        c = pl.program_id(0)

        @pl.when(c == 0)
        def _():
            for s in (px, p2, p4, p8):
                s[...] = jnp.zeros_like(s)

        x = x_ref[...]
        row = lax.broadcasted_iota(jnp.int32, x.shape, 0)
        col = lax.broadcasted_iota(jnp.int32, x.shape, 1)

        def sh(v, pv, j):
            return jnp.where(row >= j, pltpu.roll(v, j, 0), pltpu.roll(pv[...], j, 0))

        s2 = x + sh(x, px, 1)
        s4 = s2 + sh(s2, p2, 2)
        s8 = s4 + sh(s4, p4, 4)
        s16 = s8 + sh(s8, p8, 8)
        px[...] = x
        p2[...] = s2
        p4[...] = s4
        p8[...] = s8
        wsum = _pool_select(col, (s2, s4, s8, s16))
        win = _pool_select(col, POOL_WINDOWS)
        cnt = jnp.minimum(c * tc + row + 1, win).astype(F32)
        p = wsum / cnt - x
        pb = p.astype(BF16)
        p_ref[...] = pb
        for g in range(len(POOL_WINDOWS)):
            sl = slice(g * POOL_GROUP_DIM, (g + 1) * POOL_GROUP_DIM)
            yg = _dot_nn(pb[:, sl], w_ref[g]) * sc_ref[:, sl]
            y_ref[:, sl] = yg.astype(BF16)

    return _call(
        body, name="pool_fwd", grid=(ntc,),
        in_specs=[
            pl.BlockSpec((tc, dp), lambda c: (c, col_block)),
            pl.BlockSpec(pool_w.shape, lambda c: (0, 0, 0)),
            pl.BlockSpec((1, dp), lambda c: (0, 0)),
        ],
        out_specs=[pl.BlockSpec((tc, dp), lambda c: (c, 0))] * 2,
        out_shape=[jax.ShapeDtypeStruct((t, dp), BF16)] * 2,
        scratch_shapes=[pltpu.VMEM((tc, dp), F32)] * 4,
        args=(proj, pool_w, pool_scale))[0]


def _branch_mix(y_lru, y_pool, w_lru_up, w_pool_upt, proj, b_gate, ga_block, gb_block):
    t, d = y_lru.shape
    dp = y_pool.shape[1]
    tt, tn = _tile(t, 1024), 512
    nj = d // tn

    def body(yl_ref, yp_ref, wl_ref, wp_ref, ga_ref, gb_ref, ba_ref, bb_ref, bra_ref, brb_ref, mix_ref):
        br_a = _dot_nn(yl_ref[...], wl_ref[...])
        br_b = _dot_nt(yp_ref[...], wp_ref[...])
        bra_ref[...] = br_a
        brb_ref[...] = br_b
        ga = _sig(ga_ref[...] + ba_ref[...])
        gb = _sig(gb_ref[...] + bb_ref[...])
        mix_ref[...] = (ga * br_a + gb * br_b).astype(BF16)

    out = pl.BlockSpec((tt, tn), lambda j, i: (i, j))
    return _call(
        body, name="branch_mix", grid=(nj, t // tt),
        in_specs=[
            pl.BlockSpec((tt, d), lambda j, i: (i, 0)),
            pl.BlockSpec((tt, dp), lambda j, i: (i, 0)),
            pl.BlockSpec((d, tn), lambda j, i: (0, j)),
            pl.BlockSpec((tn, dp), lambda j, i: (j, 0)),
            pl.BlockSpec((tt, tn), lambda j, i: (i, ga_block + j)),
            pl.BlockSpec((tt, tn), lambda j, i: (i, gb_block + j)),
            pl.BlockSpec((1, tn), lambda j, i: (0, j)),
            pl.BlockSpec((1, tn), lambda j, i: (0, nj + j)),
        ],
        out_specs=[out, out, out],
        out_shape=[
            jax.ShapeDtypeStruct((t, d), F32),
            jax.ShapeDtypeStruct((t, d), F32),
            jax.ShapeDtypeStruct((t, d), BF16),
        ],
        args=(y_lru, y_pool, w_lru_up, w_pool_upt, proj, proj, b_gate, b_gate))[0]


def _wo_norm(mix, w_o, x, g2, g3):
    t, d = x.shape
    tt = _tile(t, 512)

    def body(mix_ref, w_ref, x_ref, g2_ref, g3_ref, m_ref, x2_ref, h3_ref):
        m = _dot_nn(mix_ref[...], w_ref[...])
        m_ref[...] = m
        mhat, _ = _rms_hat(m)
        x2 = x_ref[...] + mhat * g2_ref[...]
        x2_ref[...] = x2
        xhat, _ = _rms_hat(x2)
        h3_ref[...] = (xhat * g3_ref[...]).astype(BF16)

    row = pl.BlockSpec((tt, d), lambda i: (i, 0))
    vec = pl.BlockSpec((1, d), lambda i: (0, 0))
    return _call(
        body, name="wo_norm", grid=(t // tt,),
        in_specs=[row, pl.BlockSpec((d, d), lambda i: (0, 0)), row, vec, vec],
        out_specs=[row, row, row],
        out_shape=[
            jax.ShapeDtypeStruct((t, d), F32),
            jax.ShapeDtypeStruct((t, d), F32),
            jax.ShapeDtypeStruct((t, d), BF16),
        ],
        args=(mix, w_o, x, g2, g3))[0]


def _ff1(h3, w_ff1t, carry=None):
    t, d = h3.shape
    n = w_ff1t.shape[0]
    tt, tn = _tile(t, 2048), _tile(n, 512)

    def body(h_ref, w_ref, act_ref):
        rf = jnp.maximum(_dot_nt(h_ref[...], w_ref[...]), 0.0)
        act_ref[...] = (rf * rf).astype(BF16)

    out = pl.BlockSpec((tt, tn), lambda i, j: (i, j))
    return _call(
        body, name="ff1", grid=(t // tt, n // tn),
        in_specs=[pl.BlockSpec((tt, d), lambda i, j: (i, 0)), pl.BlockSpec((tn, d), lambda i, j: (j, 0))],
        out_specs=[out],
        out_shape=[jax.ShapeDtypeStruct((t, n), BF16)],
        args=(h3, w_ff1t), carry=carry)


def _ff2_loss(act, w_ff2, x2, g4, target):
    t, k = act.shape
    d = x2.shape[1]
    tt, tk = _tile(t, 1024), _tile(k, 512)
    nk = k // tk

    def body(a_ref, w_ref, x2_ref, g_ref, tg_ref, dy_ref, df_ref, dg_ref, loss_ref, acc):
        i, kk = pl.program_id(0), pl.program_id(1)

        @pl.when(kk == 0)
        def _():
            acc[...] = jnp.zeros_like(acc)

        @pl.when((i == 0) & (kk == 0))
        def _():
            dg_ref[...] = jnp.zeros_like(dg_ref)
            loss_ref[...] = jnp.zeros_like(loss_ref)

        acc[...] += _dot_nn(a_ref[...], w_ref[...])

        @pl.when(kk == nk - 1)
        def _():
            def tail(rows):
                fhat, r = _rms_hat(acc[rows, :])
                g = g_ref[...]
                e = x2_ref[rows, :] + fhat * g - tg_ref[rows, :]
                loss_ref[...] += 0.5 * jnp.sum(jnp.mean(e * e, axis=-1, keepdims=True))
                dy = e * (1.0 / d)
                dy_ref[rows, :] = dy
                df, dg = _rms_bwd(dy, fhat, r, g)
                df_ref[rows, :] = df.astype(BF16)
                dg_ref[...] += dg

            _row_chunks(tt, tail)

    row = pl.BlockSpec((tt, d), lambda i, kk: (i, 0))
    vec = pl.BlockSpec((1, d), lambda i, kk: (0, 0))
    return _call(
        body, name="ff2_loss", grid=(t // tt, nk),
        in_specs=[
            pl.BlockSpec((tt, tk), lambda i, kk: (i, kk)),
            pl.BlockSpec((tk, d), lambda i, kk: (kk, 0)),
            row, vec, row,
        ],
        out_specs=[row, row, vec, pl.BlockSpec((1, 128), lambda i, kk: (0, 0))],
        out_shape=[
            jax.ShapeDtypeStruct((t, d), F32),
            jax.ShapeDtypeStruct((t, d), BF16),
            jax.ShapeDtypeStruct((1, d), F32),
            jax.ShapeDtypeStruct((1, 128), F32),
        ],
        scratch_shapes=[pltpu.VMEM((tt, d), F32)],
        args=(act, w_ff2, x2, g4, target))[0]


def _ff2_bwd(df, w_ff2, act):
    t, d = df.shape
    n = w_ff2.shape[0]
    tt, tn = _tile(t, 2048), _tile(n, 512)

    def body(df_ref, w_ref, act_ref, out_ref):
        d_act = _dot_nt(df_ref[...], w_ref[...])
        out_ref[...] = (d_act * (2.0 * jnp.sqrt(act_ref[...].astype(F32)))).astype(BF16)

    blk = pl.BlockSpec((tt, tn), lambda i, j: (i, j))
    return _call(
        body, name="ff2_bwd", grid=(t // tt, n // tn),
        in_specs=[pl.BlockSpec((tt, d), lambda i, j: (i, 0)), pl.BlockSpec((tn, d), lambda i, j: (j, 0)), blk],
        out_specs=[blk],
        out_shape=[jax.ShapeDtypeStruct((t, n), BF16)],
        args=(df, w_ff2, act))[0][0]


def _wgrad(a, b, name, prev=None, row_off=0, rows=None, carry=None):
    t, m = a.shape
    n = b.shape[1]
    rows = m if rows is None else rows
    tm, tk = _tile(m, 512), _tile(t, 2048)
    nk = t // tk
    assert row_off % tm == 0
    off = row_off // tm

    def body(*refs):
        a_ref, b_ref = refs[0], refs[1]
        o32_ref, o16_ref, acc = refs[-3], refs[-2], refs[-1]
        kk = pl.program_id(1)

        @pl.when(kk == 0)
        def _():
            acc[...] = jnp.zeros_like(acc)

        acc[...] += _dot_tn(a_ref[...], b_ref[...])

        @pl.when(kk == nk - 1)
        def _():
            o32_ref[...] = acc[...]
            o16_ref[...] = acc[...].astype(BF16)

    in_specs = [pl.BlockSpec((tk, tm), lambda i, kk: (kk, i)), pl.BlockSpec((tk, n), lambda i, kk: (kk, 0))]
    args = [a, b]
    aliases = {}
    if prev is not None:
        in_specs += [ANY, ANY]
        args += list(prev)
        aliases = {2: 0, 3: 1}
    out = pl.BlockSpec((tm, n), lambda i, kk: (off + i, 0))
    return _call(
        body, name=name, grid=(m // tm, nk),
        in_specs=in_specs, out_specs=[out, out],
        out_shape=[jax.ShapeDtypeStruct((rows, n), F32), jax.ShapeDtypeStruct((rows, n), BF16)],
        scratch_shapes=[pltpu.VMEM((tm, n), F32)],
        aliases=aliases, args=args, carry=carry)


def _ff1_bwd_norms(d_f1, w_ff1t, dy, x2, g3, m, g2, carry=None):
    t, k = d_f1.shape
    d = x2.shape[1]
    tt, tk = _tile(t, 1024), _tile(k, 512)
    nk = k // tk

    def body(a_ref, w_ref, dy_ref, x2_ref, g3_ref, m_ref, g2_ref, dx2_ref, dm_ref, dg3_ref, dg2_ref, acc):
        i, kk = pl.program_id(0), pl.program_id(1)

        @pl.when(kk == 0)
        def _():
            acc[...] = jnp.zeros_like(acc)

        @pl.when((i == 0) & (kk == 0))
        def _():
            dg3_ref[...] = jnp.zeros_like(dg3_ref)
            dg2_ref[...] = jnp.zeros_like(dg2_ref)

        acc[...] += _dot_nn(a_ref[...], w_ref[...])

        @pl.when(kk == nk - 1)
        def _():
            def tail(rows):
                xhat, r3 = _rms_hat(x2_ref[rows, :])
                dx, dg3 = _rms_bwd(acc[rows, :], xhat, r3, g3_ref[...])
                dx2 = dy_ref[rows, :] + dx
                dx2_ref[rows, :] = dx2
                dg3_ref[...] += dg3
                mhat, r2 = _rms_hat(m_ref[rows, :])
                dm, dg2 = _rms_bwd(dx2, mhat, r2, g2_ref[...])
                dm_ref[rows, :] = dm.astype(BF16)
                dg2_ref[...] += dg2

            _row_chunks(tt, tail)

    row = pl.BlockSpec((tt, d), lambda i, kk: (i, 0))
    vec = pl.BlockSpec((1, d), lambda i, kk: (0, 0))
    return _call(
        body, name="ff1_bwd_norms", grid=(t // tt, nk),
        in_specs=[
            pl.BlockSpec((tt, tk), lambda i, kk: (i, kk)),
            pl.BlockSpec((tk, d), lambda i, kk: (kk, 0)),
            row, row, vec, row, vec,
        ],
        out_specs=[row, row, vec, vec],
        out_shape=[
            jax.ShapeDtypeStruct((t, d), F32),
            jax.ShapeDtypeStruct((t, d), BF16),
            jax.ShapeDtypeStruct((1, d), F32),
            jax.ShapeDtypeStruct((1, d), F32),
        ],
        scratch_shapes=[pltpu.VMEM((tt, d), F32)],
        args=(d_f1, w_ff1t, dy, x2, g3, m, g2), carry=carry)


def _wo_bwd_mix(dm, w_o, br_a, br_b, proj, b_gate, ga_block, gb_block):
    t, d = dm.shape
    tt, tn = _tile(t, 1024), 512
    nj = d // tn

    def body(dm_ref, w_ref, bra_ref, brb_ref, ga_ref, gb_ref, ba_ref, bb_ref,
             dbra_ref, dbrb_ref, dga_ref, dgb_ref, dba_ref, dbb_ref):
        i = pl.program_id(1)

        @pl.when(i == 0)
        def _():
            dba_ref[...] = jnp.zeros_like(dba_ref)
            dbb_ref[...] = jnp.zeros_like(dbb_ref)

        d_mix = _dot_nt(dm_ref[...], w_ref[...])
        ga = _sig(ga_ref[...] + ba_ref[...])
        gb = _sig(gb_ref[...] + bb_ref[...])
        dbra_ref[...] = (d_mix * ga).astype(BF16)
        dbrb_ref[...] = (d_mix * gb).astype(BF16)
        dga = d_mix * bra_ref[...] * (ga * (1.0 - ga))
        dgb = d_mix * brb_ref[...] * (gb * (1.0 - gb))
        dga_ref[...] = dga.astype(BF16)
        dgb_ref[...] = dgb.astype(BF16)
        dba_ref[...] += jnp.sum(dga, axis=0, keepdims=True)
        dbb_ref[...] += jnp.sum(dgb, axis=0, keepdims=True)

    blk = pl.BlockSpec((tt, tn), lambda j, i: (i, j))
    vec = pl.BlockSpec((1, tn), lambda j, i: (0, j))
    return _call(
        body, name="wo_bwd_mix", grid=(nj, t // tt),
        in_specs=[
            pl.BlockSpec((tt, d), lambda j, i: (i, 0)),
            pl.BlockSpec((tn, d), lambda j, i: (j, 0)),
            blk, blk,
            pl.BlockSpec((tt, tn), lambda j, i: (i, ga_block + j)),
            pl.BlockSpec((tt, tn), lambda j, i: (i, gb_block + j)),
            vec,
            pl.BlockSpec((1, tn), lambda j, i: (0, nj + j)),
        ],
        out_specs=[blk, blk, blk, blk, vec, vec],
        out_shape=[jax.ShapeDtypeStruct((t, d), BF16)] * 4 + [jax.ShapeDtypeStruct((1, d), F32)] * 2,
        args=(dm, w_o, br_a, br_b, proj, proj, b_gate, b_gate))[0]


def _lru_up_bwd(d_br_a, w_lru_up, proj, h, g_block):
    t, d = d_br_a.shape
    tt, tn = _tile(t, 1024), 512

    def body(a_ref, w_ref, g_ref, h_ref, dh_ref, dg_ref):
        d_y = _dot_nt(a_ref[...], w_ref[...])
        gel, gel_grad = _gelu_and_grad(g_ref[...])
        dh_ref[...] = d_y * gel
        dg_ref[...] = (d_y * h_ref[...] * gel_grad).astype(BF16)

    blk = pl.BlockSpec((tt, tn), lambda i, j: (i, j))
    return _call(
        body, name="lru_up_bwd", grid=(t // tt, d // tn),
        in_specs=[
            pl.BlockSpec((tt, d), lambda i, j: (i, 0)),
            pl.BlockSpec((tn, d), lambda i, j: (j, 0)),
            pl.BlockSpec((tt, tn), lambda i, j: (i, g_block + j)),
            blk,
        ],
        out_specs=[blk, blk],
        out_shape=[jax.ShapeDtypeStruct((t, d), F32), jax.ShapeDtypeStruct((t, d), BF16)],
        args=(d_br_a, w_lru_up, proj, h))[0]


def _pool_up_bwd(d_br_b, w_pool_upt):
    t, d = d_br_b.shape
    dp = w_pool_upt.shape[1]
    tt = _tile(t, 2048)

    def body(a_ref, w_ref, out_ref):
        out_ref[...] = _dot_nn(a_ref[...], w_ref[...])

    return _call(
        body, name="pool_up_bwd", grid=(t // tt,),
        in_specs=[pl.BlockSpec((tt, d), lambda i: (i, 0)), pl.BlockSpec((d, dp), lambda i: (0, 0))],
        out_specs=[pl.BlockSpec((tt, dp), lambda i: (i, 0))],
        out_shape=[jax.ShapeDtypeStruct((t, dp), F32)],
        args=(d_br_b, w_pool_upt))[0][0]


def _lru_bwd(dh, xc, h, proj, conv_w, w_a, b_a, w_x, b_x, lam, carry=None):
    t, dr = dh.shape
    cb = LRU_CB
    hd = LRU_HEAD_DIM
    per = cb // hd
    tc = _tile(t, 256)
    ncb, ntc = dr // cb, t // tc

    def body(dh_ref, xc_ref, h_ref, hp_ref, xp_ref, cw_ref, wa_ref, ba_ref, wx_ref, bx_ref, lam_ref,
             dxp_ref, dwa_ref, dba_ref, dwx_ref, dbx_ref, dlam_ref, dcw_ref, dcb_ref,
             nextd_s, anext_s, gnext_s, tmp_s, wa_s, wx_s):
        c = pl.program_id(1)
        rc = ntc - 1 - c

        @pl.when(c == 0)
        def _():
            nextd_s[...] = jnp.zeros_like(nextd_s)
            anext_s[...] = jnp.zeros_like(anext_s)
            gnext_s[...] = jnp.zeros_like(gnext_s)
            for ref in (dwa_ref, dba_ref, dwx_ref, dbx_ref, dlam_ref, dcw_ref, dcb_ref):
                ref[...] = jnp.zeros_like(ref)
            _fill_block_diag(wa_ref, wa_s)
            _fill_block_diag(wx_ref, wx_s)

        xc = xc_ref[...]
        wa, wx, lam = wa_s[...], wx_s[...], lam_ref[...]
        xcb, r, i, sp, log_a, a, mult = _lru_gates(xc, wa, ba_ref[...], wx, bx_ref[...], lam)
        row = lax.broadcasted_iota(jnp.int32, xc.shape, 0)
        h = h_ref[...]
        hp = jnp.where(rc == 0, 0.0, hp_ref[...])
        hprev = jnp.where(row >= 1, pltpu.roll(h, 1, 0), pltpu.roll(hp, 1, 0))

        def up(v, nv, j):
            return jnp.where(row < tc - j, pltpu.roll(v, tc - j, 0), nv)

        av = up(a, anext_s[...], 1)
        bv = dh_ref[...]
        s = 1
        while s < tc:
            a_sh = up(av, 1.0, s)
            b_sh = up(bv, 0.0, s)
            bv = av * b_sh + bv
            av = av * a_sh
            s *= 2
        gt = av * gnext_s[...] + bv
        tmp_s[...] = gt
        gnext_s[...] = tmp_s[0:1, :]
        tmp_s[...] = a
        anext_s[...] = tmp_s[0:1, :]

        da = gt * hprev
        ixc = i * xc
        d_mult = gt * ixc
        d_i = gt * mult * xc
        d_xc = gt * mult * i
        d_log_a = da * a - d_mult * (a * a) / mult
        d_pre_r = (d_log_a * ((-LRU_C) * sp)) * (r * (1.0 - r))
        d_pre_i = d_i * (i * (1.0 - i))
        d_sp = jnp.sum(d_log_a * ((-LRU_C) * r), axis=0, keepdims=True)
        dlam_ref[...] += d_sp * (-1.0 / (1.0 + jnp.exp(lam)))
        dpr = d_pre_r.astype(BF16)
        dpi = d_pre_i.astype(BF16)
        dba_ref[...] += jnp.sum(d_pre_r, axis=0, keepdims=True)
        dbx_ref[...] += jnp.sum(d_pre_i, axis=0, keepdims=True)
        pa = _dot_tn(xcb, dpr)
        px = _dot_tn(xcb, dpi)
        for k in range(per):
            dwa_ref[k] += pa[k * hd:(k + 1) * hd, k * hd:(k + 1) * hd]
            dwx_ref[k] += px[k * hd:(k + 1) * hd, k * hd:(k + 1) * hd]
        d_xc = d_xc + _dot_nt(dpr, wa) + _dot_nt(dpi, wx)

        nxt = nextd_s[...]
        xp = xp_ref[...]
        dxp = cw_ref[3:4, :] * d_xc
        dcw_ref[3:4, :] += jnp.sum(xp * d_xc, axis=0, keepdims=True)
        for j in (1, 2, 3):
            uj = up(d_xc, pltpu.roll(nxt, tc - j, 0), j)
            dxp = dxp + cw_ref[3 - j:4 - j, :] * uj
            dcw_ref[3 - j:4 - j, :] += jnp.sum(xp * uj, axis=0, keepdims=True)
        dcb_ref[...] += jnp.sum(d_xc, axis=0, keepdims=True)
        nextd_s[...] = d_xc
        dxp_ref[...] = dxp.astype(BF16)

    vec = pl.BlockSpec((1, cb), lambda j, c: (0, j))
    blk = pl.BlockSpec((tc, cb), lambda j, c: (ntc - 1 - c, j))
    mat = pl.BlockSpec((per, hd, hd), lambda j, c: (j, 0, 0))
    cwb = pl.BlockSpec((4, cb), lambda j, c: (0, j))
    return _call(
        body, name="lru_bwd", grid=(ncb, ntc),
        in_specs=[
            blk, blk, blk,
            pl.BlockSpec((tc, cb), lambda j, c: (jnp.maximum(ntc - 2 - c, 0), j)),
            blk, cwb, mat, vec, mat, vec, vec,
        ],
        out_specs=[blk, mat, vec, mat, vec, vec, cwb, vec],
        out_shape=[
            jax.ShapeDtypeStruct((t, dr), BF16),
            jax.ShapeDtypeStruct(w_a.shape, F32),
            jax.ShapeDtypeStruct((1, dr), F32),
            jax.ShapeDtypeStruct(w_x.shape, F32),
            jax.ShapeDtypeStruct((1, dr), F32),
            jax.ShapeDtypeStruct((1, dr), F32),
            jax.ShapeDtypeStruct((4, dr), F32),
            jax.ShapeDtypeStruct((1, dr), F32),
        ],
        scratch_shapes=[
            pltpu.VMEM((tc, cb), F32),
            pltpu.VMEM((1, cb), F32),
            pltpu.VMEM((1, cb), F32),
            pltpu.VMEM((tc, cb), F32),
            pltpu.VMEM((cb, cb), BF16),
            pltpu.VMEM((cb, cb), BF16),
        ],
        args=(dh, xc, h, h, proj, conv_w, w_a, b_a, w_x, b_x, lam), carry=carry)


def _pool_bwd(d_y_pool, p, pool_w, pool_scale):
    t, dp = d_y_pool.shape
    tc = _tile(t, 256)
    ntc = t // tc
    ng = len(POOL_WINDOWS)

    def body(dy_ref, p_ref, w_ref, sc_ref, dx_ref, dw_ref, dsc_ref, nz, n2, n4, n8, dp_s):
        c = pl.program_id(0)
        rc = ntc - 1 - c

        @pl.when(c == 0)
        def _():
            for s in (nz, n2, n4, n8):
                s[...] = jnp.zeros_like(s)
            dw_ref[...] = jnp.zeros_like(dw_ref)
            dsc_ref[...] = jnp.zeros_like(dsc_ref)

        for g in range(ng):
            sl = slice(g * POOL_GROUP_DIM, (g + 1) * POOL_GROUP_DIM)
            pg = p_ref[:, sl]
            dyg = dy_ref[:, sl]
            wg = w_ref[g].astype(BF16)
            q = _dot_nn(pg, wg)
            dsc_ref[:, sl] += jnp.sum(dyg * q, axis=0, keepdims=True)
            dpw = (dyg * sc_ref[:, sl]).astype(BF16)
            dw_ref[g] += _dot_tn(pg, dpw)
            dp_s[:, sl] = _dot_nt(dpw, wg)

        dpv = dp_s[...]
        row = lax.broadcasted_iota(jnp.int32, dpv.shape, 0)
        col = lax.broadcasted_iota(jnp.int32, dpv.shape, 1)
        win = _pool_select(col, POOL_WINDOWS)
        cnt = jnp.minimum(rc * tc + row + 1, win).astype(F32)
        z = dpv / cnt

        def up(v, nv, j):
            return jnp.where(row < tc - j, pltpu.roll(v, tc - j, 0), pltpu.roll(nv[...], tc - j, 0))

        u2 = z + up(z, nz, 1)
        u4 = u2 + up(u2, n2, 2)
        u8 = u4 + up(u4, n4, 4)
        u16 = u8 + up(u8, n8, 8)
        nz[...] = z
        n2[...] = u2
        n4[...] = u4
        n8[...] = u8
        dx_ref[...] = (_pool_select(col, (u2, u4, u8, u16)) - dpv).astype(BF16)

    blk = pl.BlockSpec((tc, dp), lambda c: (ntc - 1 - c, 0))
    full_w = pl.BlockSpec(pool_w.shape, lambda c: (0, 0, 0))
    vec = pl.BlockSpec((1, dp), lambda c: (0, 0))
    return _call(
        body, name="pool_bwd", grid=(ntc,),
        in_specs=[blk, blk, full_w, vec],
        out_specs=[blk, full_w, vec],
        out_shape=[
            jax.ShapeDtypeStruct((t, dp), BF16),
            jax.ShapeDtypeStruct(pool_w.shape, F32),
            jax.ShapeDtypeStruct((1, dp), F32),
        ],
        scratch_shapes=[pltpu.VMEM((tc, dp), F32)] * 5,
        args=(d_y_pool, p, pool_w, pool_scale))[0]


def _win_bwd_norm(parts, w_int, dx2, x, g1, carry=None):
    t, d = x.shape
    tk = 512
    tt = _tile(t, 1024)
    bounds = []
    k0 = 0
    for part in parts:
        assert part.shape[1] % tk == 0
        bounds.append((k0, k0 + part.shape[1] // tk))
        k0 += part.shape[1] // tk
    nk = k0
    assert nk * tk == w_int.shape[0]
    np_ = len(parts)

    def body(*refs):
        p_refs = refs[:np_]
        w_ref, dx2_ref, x_ref, g_ref, gx_ref, dg_ref, acc = refs[np_:]
        i, kk = pl.program_id(0), pl.program_id(1)

        @pl.when(kk == 0)
        def _():
            acc[...] = jnp.zeros_like(acc)

        @pl.when((i == 0) & (kk == 0))
        def _():
            dg_ref[...] = jnp.zeros_like(dg_ref)

        for (lo, hi), p_ref in zip(bounds, p_refs):
            @pl.when((kk >= lo) & (kk < hi))
            def _(p_ref=p_ref):
                acc[...] += _dot_nn(p_ref[...], w_ref[...])

        @pl.when(kk == nk - 1)
        def _():
            def tail(rows):
                xhat, r = _rms_hat(x_ref[rows, :])
                dx, dg = _rms_bwd(acc[rows, :], xhat, r, g_ref[...])
                gx_ref[rows, :] = dx2_ref[rows, :] + dx
                dg_ref[...] += dg

            _row_chunks(tt, tail)

    def part_spec(lo, hi):
        return pl.BlockSpec((tt, tk), lambda i, kk: (i, jnp.clip(kk - lo, 0, hi - lo - 1)))

    row = pl.BlockSpec((tt, d), lambda i, kk: (i, 0))
    vec = pl.BlockSpec((1, d), lambda i, kk: (0, 0))
    return _call(
        body, name="win_bwd_norm", grid=(t // tt, nk),
        in_specs=[part_spec(lo, hi) for lo, hi in bounds]
        + [pl.BlockSpec((tk, d), lambda i, kk: (kk, 0)), row, row, vec],
        out_specs=[row, vec],
        out_shape=[jax.ShapeDtypeStruct((t, d), F32), jax.ShapeDtypeStruct((1, d), F32)],
        scratch_shapes=[pltpu.VMEM((tt, d), F32)],
        args=(*parts, w_int, dx2, x, g1), carry=carry)


def _adam_math(w, g, m, v):
    m = ADAM_B1 * m + (1.0 - ADAM_B1) * g
    v = ADAM_B2 * v + (1.0 - ADAM_B2) * (g * g)
    m_hat = m / (1.0 - ADAM_B1 ** ADAM_STEP)
    v_hat = v / (1.0 - ADAM_B2 ** ADAM_STEP)
    delta = -ADAM_LR * (m_hat / (jnp.sqrt(v_hat) + ADAM_EPS) + ADAM_WD * w)
    return delta, m, v


def _adamw_big(ws, gs, ms, vs):
    n = len(ws)
    nb = 8

    def body(*refs):
        for a in range(n):
            w_ref, g_ref, m_ref, v_ref = refs[4 * a:4 * a + 4]
            d_ref, nm_ref, nv_ref = refs[4 * n + 3 * a:4 * n + 3 * a + 3]
            dl, m, v = _adam_math(w_ref[...], g_ref[...], m_ref[...], v_ref[...])
            d_ref[...] = dl
            nm_ref[...] = m
            nv_ref[...] = v

    in_specs, out_specs, out_shape, args = [], [], [], []
    for w, g, m, v in zip(ws, gs, ms, vs):
        rows, cols = w.shape
        blk = pl.BlockSpec((rows // nb, cols), lambda i: (i, 0))
        in_specs += [blk] * 4
        args += [w, g, m, v]
        out_specs += [blk] * 3
        out_shape += [jax.ShapeDtypeStruct(w.shape, F32)] * 3
    outs = _call(body, name="adamw_big", grid=(nb,), in_specs=in_specs, out_specs=out_specs,
                 out_shape=out_shape, args=args)[0]
    return [tuple(outs[3 * a:3 * a + 3]) for a in range(n)]


SMALL_ORDER = ("norm_mix_pre", "norm_mix_post", "norm_mlp_pre", "norm_mlp_post", "b_gate", "conv_w", "conv_b",
               "lru_w_a", "lru_b_a", "lru_w_x", "lru_b_x", "lru_lambda", "pool_w", "pool_scale")
VEC_ROW = dict(norm_mix_pre=0, norm_mix_post=1, norm_mlp_pre=2, norm_mlp_post=3, conv_b=6, lru_b_a=7,
               lru_b_x=8, lru_lambda=9)
ROW_B_GATE, ROW_POOL_SCALE, ROW_CONV_W, ROW_LOSS, N_VEC_ROWS = 4, 10, 11, 15, 16


def _adamw_small(vec_parts, g_pool, g_wa, g_wx, me, params):
    d = vec_parts.shape[2]
    names = SMALL_ORDER
    n = len(names)
    cw_cols = params["conv_w"][0].shape[2]

    def body(me_ref, vec_ref, vecc_ref, gp_ref, gwa_ref, gwx_ref, *refs):
        wmv = refs[:3 * n]
        loss_ref = refs[3 * n]
        outs = refs[3 * n + 1:3 * n + 1 + 4 * n]
        vs, vsc = refs[3 * n + 1 + 4 * n:]
        acc, accc = vec_ref[0], vecc_ref[0]
        for k in range(1, N_DEV):
            acc = acc + vec_ref[k]
            accc = accc + vecc_ref[k]
        vs[...] = acc
        vsc[...] = accc
        loss_ref[...] = vs[ROW_LOSS:ROW_LOSS + 1, 0:128]

        def upd(a, g, idx):
            w_ref, m_ref, v_ref = wmv[3 * a:3 * a + 3]
            g_ref, d_ref, nm_ref, nv_ref = outs[4 * a:4 * a + 4]
            dl, m, v = _adam_math(w_ref[idx], g, m_ref[idx], v_ref[idx])
            g_ref[idx] = g
            d_ref[idx] = dl
            nm_ref[idx] = m
            nv_ref[idx] = v

        for a, name in enumerate(names):
            if name in VEC_ROW:
                r = VEC_ROW[name]
                upd(a, vs[r:r + 1, :], (slice(None), slice(None)))
            elif name == "b_gate":
                for half in range(2):
                    r = ROW_B_GATE + half
                    upd(a, vs[r:r + 1, :], (slice(None), slice(half * d, (half + 1) * d)))
            elif name == "pool_scale":
                width = params[name][0].shape[1]
                upd(a, vs[ROW_POOL_SCALE:ROW_POOL_SCALE + 1, 0:width], (slice(None), slice(None)))
            elif name == "conv_w":
                upd(a, vsc[ROW_CONV_W:ROW_CONV_W + 4, :], (0,))
            elif name == "pool_w":
                upd(a, gp_ref[...], (Ellipsis,))
            elif name == "lru_w_a":
                upd(a, gwa_ref[...], (Ellipsis,))
            elif name == "lru_w_x":
                upd(a, gwx_ref[...], (Ellipsis,))
            else:
                raise ValueError(name)

    def whole(shape):
        nd = len(shape)
        return pl.BlockSpec(tuple(shape), lambda i, me_ref: (0,) * nd)

    in_specs = [
        whole(vec_parts.shape),
        pl.BlockSpec((N_DEV, N_VEC_ROWS, cw_cols), lambda i, me_ref: (0, 0, me_ref[0])),
        whole(g_pool.shape), whole(g_wa.shape), whole(g_wx.shape),
    ]
    args = [vec_parts, vec_parts, g_pool, g_wa, g_wx]
    out_specs = [whole((1, 128))]
    out_shape = [jax.ShapeDtypeStruct((1, 128), F32)]
    for name in names:
        for arr in params[name]:
            in_specs.append(whole(arr.shape))
            args.append(arr)
        shp = params[name][0].shape
        out_specs += [whole(shp)] * 4
        out_shape += [jax.ShapeDtypeStruct(shp, F32)] * 4
    grid_spec = pltpu.PrefetchScalarGridSpec(
        num_scalar_prefetch=1, grid=(1,), in_specs=in_specs, out_specs=out_specs,
        scratch_shapes=[pltpu.VMEM((N_VEC_ROWS, d), F32), pltpu.VMEM((N_VEC_ROWS, cw_cols), F32)])
    outs = pl.pallas_call(
        body, name="adamw_small", grid_spec=grid_spec, out_shape=out_shape,
        compiler_params=pltpu.CompilerParams(
            dimension_semantics=("arbitrary",), vmem_limit_bytes=V7X_VMEM_LIMIT_BYTES),
    )(me, *_in_hbm(args))
    return outs[0], {name: tuple(outs[1 + 4 * a:5 + 4 * a]) for a, name in enumerate(names)}


def _rs_sum(full, recv, shard_ids, slot_ids, name):
    r, rest = recv.shape[1], tuple(recv.shape[2:])
    zeros = (0,) * len(rest)
    send_dtype = recv.dtype

    def body(sh_ref, sl_ref, full_ref, recv_ref, own_ref, send_ref):
        s = pl.program_id(0)
        v = full_ref[...] + recv_ref[...].astype(F32)

        @pl.when(s == 0)
        def _():
            own_ref[...] = v

        @pl.when(s > 0)
        def _():
            send_ref[...] = v.astype(send_dtype)

    grid_spec = pltpu.PrefetchScalarGridSpec(
        num_scalar_prefetch=2,
        grid=(4,),
        in_specs=[
            pl.BlockSpec((r,) + rest, lambda s, sh, sl: (sh[s],) + zeros),
            pl.BlockSpec((None, r) + rest, lambda s, sh, sl: (sl[s], 0) + zeros),
        ],
        out_specs=[
            pl.BlockSpec((None, r) + rest, lambda s, sh, sl: (0, 0) + zeros),
            pl.BlockSpec((None, r) + rest, lambda s, sh, sl: (jnp.maximum(s - 1, 0), 0) + zeros),
        ],
    )
    return pl.pallas_call(
        body,
        name=name,
        grid_spec=grid_spec,
        out_shape=[jax.ShapeDtypeStruct((1, r) + rest, F32), jax.ShapeDtypeStruct((3, r) + rest, send_dtype)],
        compiler_params=pltpu.CompilerParams(
            dimension_semantics=("arbitrary",), vmem_limit_bytes=V7X_VMEM_LIMIT_BYTES),
    )(shard_ids, slot_ids, *_in_hbm([full, recv]))


def _finals(pairs):
    nb = 4
    n = len(pairs)

    def body(*refs):
        for a in range(n):
            own_ref, recv_ref = refs[2 * a], refs[2 * a + 1]
            acc = own_ref[...]
            for k in range(3):
                acc = acc + recv_ref[k].astype(F32)
            refs[2 * n + a][...] = acc

    in_specs, out_specs, out_shape, args = [], [], [], []
    for own, recv in pairs:
        _, rows, cols = own.shape
        in_specs += [pl.BlockSpec((None, rows // nb, cols), lambda i: (0, i, 0)),
                     pl.BlockSpec((3, rows // nb, cols), lambda i: (0, i, 0))]
        args += [own, recv]
        out_specs.append(pl.BlockSpec((rows // nb, cols), lambda i: (i, 0)))
        out_shape.append(jax.ShapeDtypeStruct((rows, cols), F32))
    return _call(body, name="rs_finals", grid=(nb,), in_specs=in_specs, out_specs=out_specs,
                 out_shape=out_shape, args=args)[0]


def _rs_level1(fulls_f32, fulls_send, tag):
    x, y, c = _place()
    recv1 = _run_plan(_rs_sibling_plan(fulls_send), "rs_sibling_" + tag)
    qs = jnp.stack([2 * x + y, 2 * (1 - x) + y, 2 * x + (1 - y), 2 * (1 - x) + (1 - y)]).astype(jnp.int32)
    shard_ids = 2 * qs + c
    return [_rs_sum(f32, r1, shard_ids, qs, f"rs_sum_{tag}{a}")
            for a, (f32, r1) in enumerate(zip(fulls_f32, recv1))]


def _rows(g):
    return g.reshape(g.shape[0] * g.shape[1], g.shape[2])


def kernel(x, norm_mix_pre, norm_mix_post, norm_mlp_pre, norm_mlp_post, w_in, b_gate, conv_w, conv_b, lru_w_a, lru_b_a, lru_w_x, lru_b_x, lru_lambda, pool_w, pool_scale, w_lru_up, w_pool_up, w_o, w_ff1, w_ff2, loss_target, m_norm_mix_pre, m_norm_mix_post, m_norm_mlp_pre, m_norm_mlp_post, m_w_in, m_b_gate, m_conv_w, m_conv_b, m_lru_w_a, m_lru_b_a, m_lru_w_x, m_lru_b_x, m_lru_lambda, m_pool_w, m_pool_scale, m_w_lru_up, m_w_pool_up, m_w_o, m_w_ff1, m_w_ff2, v_norm_mix_pre, v_norm_mix_post, v_norm_mlp_pre, v_norm_mlp_post, v_w_in, v_b_gate, v_conv_w, v_conv_b, v_lru_w_a, v_lru_b_a, v_lru_w_x, v_lru_b_x, v_lru_lambda, v_pool_w, v_pool_scale, v_w_lru_up, v_w_pool_up, v_w_o, v_w_ff1, v_w_ff2):
    t, d = x.shape[1], x.shape[2]
    d_rnn = conv_b.shape[1]
    d_pool = pool_scale.shape[1]
    per = LRU_CB // LRU_HEAD_DIM
    xi, yi, ci = _place()
    me = 4 * xi + 2 * yi + ci

    x2d = x[0]
    tgt = loss_target[0]

    s_in = w_in[0].T.astype(BF16)
    s_lu = w_lru_up[0].astype(BF16)
    s_pu = w_pool_up[0].T.astype(BF16)
    s_o = w_o[0].astype(BF16)
    s_f1 = w_ff1[0].T.astype(BF16)
    s_f2 = w_ff2[0].astype(BF16)
    s_cw = jnp.pad(conv_w[0], ((0, 4), (0, 0)))

    g_in, g_cw = _run_plan(_ag_plan([s_in, s_cw]), "ag_w_in")
    w_int = _rows(g_in)
    conv_w_full = jnp.transpose(g_cw[:, :4, :], (1, 0, 2)).reshape(4, d_rnn)

    wa_bd, wx_bd = lru_w_a[0], lru_w_x[0]
    pw = pool_w[0]
    pw_bf = pw.astype(BF16)

    pool_block = (2 * d_rnn) // d_pool
    ga_block = (2 * d_rnn + d_pool) // 512
    gb_block = ga_block + d // 512
    g_block = d_rnn // 512

    (proj, h1), (g_lu, g_pu, g_o) = _norm_proj(x2d, norm_mix_pre, w_int, carry=_ag_plan([s_lu, s_pu, s_o]))
    w_lu, w_put, w_og = _rows(g_lu), _rows(g_pu), _rows(g_o)
    (y_lru, h, xc), (g_f1,) = _lru_fwd(proj, conv_w_full, conv_b, wa_bd, lru_b_a, wx_bd, lru_b_x, lru_lambda,
                                       carry=_ag_plan([s_f1]))
    w_f1t = _rows(g_f1)
    y_pool, p = _pool_fwd(proj, pw_bf, pool_scale, pool_block)
    br_a, br_b, mix = _branch_mix(y_lru, y_pool, w_lu, w_put, proj, b_gate, ga_block, gb_block)
    m, x2, h3 = _wo_norm(mix, w_og, x2d, norm_mix_post, norm_mlp_pre)
    (act,), (g_f2,) = _ff1(h3, w_f1t, carry=_ag_plan([s_f2]))
    w_f2 = _rows(g_f2)
    dy, df, dg4, loss_part = _ff2_loss(act, w_f2, x2, norm_mlp_post, tgt)

    d_f1 = _ff2_bwd(df, w_f2, act)
    (gw_ff2_32, gw_ff2_16), _ = _wgrad(act, df, "wgrad_ff2")
    ((own_ff2, send_ff2),) = _rs_level1([gw_ff2_32], [gw_ff2_16], "ff2")
    (gw_ff1_32, gw_ff1_16), (r2_ff2,) = _wgrad(d_f1, h3, "wgrad_ff1", carry=_rs_chips_plan([send_ff2]))
    ((own_ff1, send_ff1),) = _rs_level1([gw_ff1_32], [gw_ff1_16], "ff1")
    (dx2, dm, dg3, dg2), (r2_ff1,) = _ff1_bwd_norms(d_f1, w_f1t, dy, x2, norm_mlp_pre, m, norm_mix_post,
                                                    carry=_rs_chips_plan([send_ff1]))
    (gw_o_32, gw_o_16), _ = _wgrad(mix, dm, "wgrad_o")
    d_br_a, d_br_b, p_ga, p_gb, dbg_a, dbg_b = _wo_bwd_mix(dm, w_og, br_a, br_b, proj, b_gate, ga_block, gb_block)
    (gw_lu_32, gw_lu_16), _ = _wgrad(y_lru, d_br_a, "wgrad_lru_up")
    (gw_pu_32, gw_pu_16), _ = _wgrad(d_br_b, y_pool, "wgrad_pool_up")
    mid = _rs_level1([gw_o_32, gw_lu_32, gw_pu_32.reshape(-1, d)],
                     [gw_o_16, gw_lu_16, gw_pu_16.reshape(-1, d)], "mid")
    dh, p_g = _lru_up_bwd(d_br_a, w_lu, proj, h, g_block)
    d_y_pool = _pool_up_bwd(d_br_b, w_put)
    (p_x, dwa, db_a, dwx, db_x, dlam, dconv_w, dconv_b), r2_mid = _lru_bwd(
        dh, xc, h, proj, conv_w_full, wa_bd, lru_b_a, wx_bd, lru_b_x, lru_lambda,
        carry=_rs_chips_plan([s for _, s in mid]))
    p_p, dpool_w, dpool_scale = _pool_bwd(d_y_pool, p, pw, pool_scale)
    parts = [p_x, p_g, p_p, p_ga, p_gb]
    gw_in = None
    row_off = 0
    for k, part in enumerate(parts):
        gw_in, _ = _wgrad(part, h1, f"wgrad_in{k}", prev=gw_in, row_off=row_off, rows=w_int.shape[0])
        row_off += part.shape[1]
    tail = _rs_level1([gw_in[0], dpool_w.reshape(N_DEV, -1, POOL_GROUP_DIM), dwa, dwx],
                      [gw_in[1], dpool_w.reshape(N_DEV, -1, POOL_GROUP_DIM), dwa, dwx], "in")
    (grad_x, dg1), r2_tail = _win_bwd_norm(parts, w_int, dx2, x2d, norm_mix_pre,
                                           carry=_rs_chips_plan([s for _, s in tail]))

    def flat2(a):
        return a.reshape(a.shape[0], -1, a.shape[-1])

    fin = _finals([
        (tail[0][0], r2_tail[0]), (mid[1][0], r2_mid[1]), (mid[2][0], r2_mid[2]), (mid[0][0], r2_mid[0]),
        (own_ff1, r2_ff1), (own_ff2, r2_ff2),
        (flat2(tail[1][0]), flat2(r2_tail[1])), (flat2(tail[2][0]), flat2(r2_tail[2])),
        (flat2(tail[3][0]), flat2(r2_tail[3])),
    ])
    g_w_in = fin[0].T
    g_w_lru_up = fin[1]
    g_w_pool_up = fin[2].reshape(d // N_DEV, d_pool).T
    g_w_o = fin[3]
    g_w_ff1 = fin[4].T
    g_w_ff2 = fin[5]

    def pad_row(a):
        return jnp.pad(a, ((0, 0), (0, d - a.shape[1])))

    vecs = jnp.concatenate([dg1, dg2, dg3, dg4, dbg_a, dbg_b, dconv_b, db_a, db_x, dlam,
                            pad_row(dpool_scale), dconv_w, pad_row(loss_part)], axis=0)
    assert vecs.shape[0] == N_VEC_ROWS
    vec_parts, g_pool, g_wa, g_wx = _run_plan(_ag_plan([vecs, fin[6], fin[7], fin[8]]), "ag_tail")

    small = dict(
        norm_mix_pre=(norm_mix_pre, m_norm_mix_pre, v_norm_mix_pre),
        norm_mix_post=(norm_mix_post, m_norm_mix_post, v_norm_mix_post),
        norm_mlp_pre=(norm_mlp_pre, m_norm_mlp_pre, v_norm_mlp_pre),
        norm_mlp_post=(norm_mlp_post, m_norm_mlp_post, v_norm_mlp_post),
        b_gate=(b_gate, m_b_gate, v_b_gate), conv_w=(conv_w, m_conv_w, v_conv_w),
        conv_b=(conv_b, m_conv_b, v_conv_b), lru_w_a=(lru_w_a, m_lru_w_a, v_lru_w_a),
        lru_b_a=(lru_b_a, m_lru_b_a, v_lru_b_a), lru_w_x=(lru_w_x, m_lru_w_x, v_lru_w_x),
        lru_b_x=(lru_b_x, m_lru_b_x, v_lru_b_x), lru_lambda=(lru_lambda, m_lru_lambda, v_lru_lambda),
        pool_w=(pool_w, m_pool_w, v_pool_w), pool_scale=(pool_scale, m_pool_scale, v_pool_scale))
    loss_row, small_out = _adamw_small(
        vec_parts, g_pool.reshape(pool_w.shape), g_wa.reshape(lru_w_a.shape), g_wx.reshape(lru_w_x.shape),
        jnp.reshape(me, (1,)).astype(jnp.int32), small)
    grads = {n: o[0] for n, o in small_out.items()}
    delta = {n: o[1] for n, o in small_out.items()}
    new_m = {n: o[2] for n, o in small_out.items()}
    new_v = {n: o[3] for n, o in small_out.items()}

    big_names = ["w_in", "w_lru_up", "w_pool_up", "w_o", "w_ff1", "w_ff2"]
    big_w = [w_in, w_lru_up, w_pool_up, w_o, w_ff1, w_ff2]
    big_g = [g_w_in, g_w_lru_up, g_w_pool_up, g_w_o, g_w_ff1, g_w_ff2]
    big_m = [m_w_in, m_w_lru_up, m_w_pool_up, m_w_o, m_w_ff1, m_w_ff2]
    big_v = [v_w_in, v_w_lru_up, v_w_pool_up, v_w_o, v_w_ff1, v_w_ff2]
    big_out = _adamw_big([w[0] for w in big_w], big_g, [mm[0] for mm in big_m], [vv[0] for vv in big_v])
    for name, g, (dl, nm, nv) in zip(big_names, big_g, big_out):
        grads[name], delta[name], new_m[name], new_v[name] = g[None], dl[None], nm[None], nv[None]

    loss = loss_row[0, 0]
    order = ["norm_mix_pre", "norm_mix_post", "norm_mlp_pre", "norm_mlp_post", "w_in", "b_gate", "conv_w",
             "conv_b", "lru_w_a", "lru_b_a", "lru_w_x", "lru_b_x", "lru_lambda", "pool_w", "pool_scale",
             "w_lru_up", "w_pool_up", "w_o", "w_ff1", "w_ff2"]
    return (loss, grad_x[None], *[grads[n] for n in order], *[delta[n] for n in order],
            *[new_m[n] for n in order], *[new_v[n] for n in order])
```

```python
import functools
import math
import operator
import types

import jax
import jax.numpy as jnp
from jax import lax
from jax.experimental import pallas as pl
from jax.experimental.pallas import tpu as pltpu

F32 = jnp.float32
BF16 = jnp.bfloat16
NORM_EPS = 1e-6
LRU_C = 8.0
N_LRU_HEADS = 16
LRU_HEAD_DIM = 64
POOL_WINDOWS = (2, 4, 8, 16)
POOL_GROUP_DIM = 128
ADAM_LR = 0.001
ADAM_B1 = 0.9
ADAM_B2 = 0.999
ADAM_EPS = 1e-08
ADAM_WD = 0.01
ADAM_STEP = 10
N_DEV = 8
V7X_VMEM_LIMIT_BYTES = 56 * 1024 * 1024
LRU_CB = 256
MESH = pl.DeviceIdType.MESH
ANY = pl.BlockSpec(memory_space=pl.ANY)


def _tile(n, pref):
    t = min(n, pref)
    assert n % t == 0, (n, pref)
    return t


def _dot_nn(a, b):
    return lax.dot_general(a, b, (((1,), (0,)), ((), ())), preferred_element_type=F32)


def _dot_nt(a, b):
    return lax.dot_general(a, b, (((1,), (1,)), ((), ())), preferred_element_type=F32)


def _dot_tn(a, b):
    return lax.dot_general(a, b, (((0,), (0,)), ((), ())), preferred_element_type=F32)


def _row_chunks(n_rows, fn, chunk=256):
    chunk = min(chunk, n_rows)
    assert n_rows % chunk == 0

    def step(r, carry):
        fn(pl.ds(pl.multiple_of(r * chunk, chunk), chunk))
        return carry

    lax.fori_loop(0, n_rows // chunk, step, 0)


def _sig(x):
    return 1.0 / (1.0 + jnp.exp(-x))


def _rms_hat(x):
    r = lax.rsqrt(jnp.mean(x * x, axis=-1, keepdims=True) + NORM_EPS)
    return x * r, r


def _rms_bwd(dn, xhat, r, g):
    q = dn * g
    dx = r * (q - xhat * jnp.mean(q * xhat, axis=-1, keepdims=True))
    dg = jnp.sum(dn * xhat, axis=0, keepdims=True)
    return dx, dg


_GELU_K = math.sqrt(2.0 / math.pi)
_GELU_C = 0.044715


def _gelu_and_grad(g):
    t = jnp.tanh(_GELU_K * (g + _GELU_C * g * g * g))
    val = 0.5 * g * (1.0 + t)
    grad = 0.5 * (1.0 + t) + 0.5 * g * (1.0 - t * t) * (_GELU_K * (1.0 + 3.0 * _GELU_C * g * g))
    return val, grad


def _softplus_neg(lam):
    z = -lam
    e = jnp.exp(-jnp.abs(z))
    u = 1.0 + e
    d = u - 1.0
    l1p = jnp.where(d == 0.0, e, jnp.log(u) * (e / jnp.where(d == 0.0, 1.0, d)))
    return jnp.maximum(z, 0.0) + l1p


def _lru_gates(xc, wa, ba, wx, bx, lam):
    xcb = xc.astype(BF16)
    r = _sig(_dot_nn(xcb, wa) + ba)
    i = _sig(_dot_nn(xcb, wx) + bx)
    sp = _softplus_neg(lam)
    log_a = (-LRU_C) * r * sp
    a = jnp.exp(log_a)
    mult = jnp.sqrt(-jnp.tanh(log_a) * (1.0 + a * a))
    return xcb, r, i, sp, log_a, a, mult


def _place():
    return lax.axis_index("x"), lax.axis_index("y"), lax.axis_index("c")


def _ag_plan(shards, pieces=None, bufs=None):
    na = len(shards)

    def parts(ins, outs, sems):
        send_sems, recv_sems, local_sems = sems
        x, y, c = _place()
        me, sibling = (x, y, c), (x, y, 1 - c)
        chips = [(1 - x, y), (x, 1 - y), (1 - x, 1 - y)]

        def own(a):
            return ins[a] if pieces is None else ins[a].at[pl.ds(*pieces[a])]

        def slot(a, px, py, pc):
            idx = 4 * px + 2 * py + pc
            return outs[a].at[idx] if pieces is None else outs[a].at[idx, pl.ds(*pieces[a])]

        def copy(a, k, block, to, src=None):
            return pltpu.make_async_remote_copy(
                src_ref=slot(a, *block) if src is None else src,
                dst_ref=slot(a, *block),
                send_sem=send_sems.at[a * 7 + k],
                recv_sem=recv_sems.at[a * 7 + k],
                device_id=to,
                device_id_type=MESH,
            )

        mine = [pltpu.make_async_copy(own(a), slot(a, *me), local_sems.at[a]) for a in range(na)]
        first = []
        for a in range(na):
            first.append(copy(a, 0, me, sibling, src=own(a)))
            first += [copy(a, 1 + j, me, (*chip, c), src=own(a)) for j, chip in enumerate(chips)]
        return me, sibling, chips, c, copy, mine, first

    def start(ins, outs, sems):
        _, _, _, _, _, mine, first = parts(ins, outs, sems)
        for cp in mine + first:
            cp.start()

    def finish(ins, outs, sems):
        me, sibling, chips, c, copy, mine, first = parts(ins, outs, sems)
        passed = []
        for j, chip in enumerate(chips):
            for a in range(na):
                copy(a, 1 + j, (*chip, c), me).wait_recv()
                fwd = copy(a, 4 + j, (*chip, c), sibling)
                fwd.start()
                passed.append(fwd)
        for a in range(na):
            copy(a, 0, sibling, me).wait_recv()
            for j, chip in enumerate(chips):
                copy(a, 4 + j, (*chip, 1 - c), me).wait_recv()
        for cp in first + passed:
            cp.wait_send()
        for cp in mine:
            cp.wait()

    return types.SimpleNamespace(
        ins=list(shards) + list(bufs or []),
        out_shapes=[jax.ShapeDtypeStruct((N_DEV,) + s.shape, s.dtype) for s in shards],
        sems=[pltpu.SemaphoreType.DMA((7 * na,)), pltpu.SemaphoreType.DMA((7 * na,)),
              pltpu.SemaphoreType.DMA((na,))],
        aliases=[(na + a, a) for a in range(na)] if bufs else [],
        start=start, finish=finish)


def _rs_sibling_plan(fulls):
    na = len(fulls)
    rs = [f.shape[0] // N_DEV for f in fulls]

    def copies(ins, outs, sems):
        send_sems, recv_sems = sems
        x, y, c = _place()
        out = []
        for a in range(na):
            for q in range(4):
                shard = 2 * q + (1 - c)
                out.append(pltpu.make_async_remote_copy(
                    src_ref=ins[a].at[pl.ds(shard * rs[a], rs[a])],
                    dst_ref=outs[a].at[q],
                    send_sem=send_sems.at[a * 4 + q],
                    recv_sem=recv_sems.at[a * 4 + q],
                    device_id=(x, y, 1 - c),
                    device_id_type=MESH,
                ))
        return out

    def start(ins, outs, sems):
        for cp in copies(ins, outs, sems):
            cp.start()

    def finish(ins, outs, sems):
        for cp in copies(ins, outs, sems):
            cp.wait()

    return types.SimpleNamespace(
        ins=list(fulls),
        out_shapes=[jax.ShapeDtypeStruct((4, r) + f.shape[1:], f.dtype) for r, f in zip(rs, fulls)],
        sems=[pltpu.SemaphoreType.DMA((4 * na,)), pltpu.SemaphoreType.DMA((4 * na,))],
        start=start, finish=finish)


def _rs_chips_plan(sends, pieces=None, bufs=None):
    na = len(sends)

    def copies(ins, outs, sems):
        send_sems, recv_sems = sems
        x, y, c = _place()
        chips = [(1 - x, y), (x, 1 - y), (1 - x, 1 - y)]
        out = []
        for a in range(na):
            for k, chip in enumerate(chips):
                rows = (k,) if pieces is None else (k, pl.ds(*pieces[a]))
                out.append(pltpu.make_async_remote_copy(
                    src_ref=ins[a].at[rows],
                    dst_ref=outs[a].at[rows],
                    send_sem=send_sems.at[a * 3 + k],
                    recv_sem=recv_sems.at[a * 3 + k],
                    device_id=(*chip, c),
                    device_id_type=MESH,
                ))
        return out

    def start(ins, outs, sems):
        for cp in copies(ins, outs, sems):
            cp.start()

    def finish(ins, outs, sems):
        for cp in copies(ins, outs, sems):
            cp.wait()

    return types.SimpleNamespace(
        ins=list(sends) + list(bufs or []),
        out_shapes=[jax.ShapeDtypeStruct(s.shape, s.dtype) for s in sends],
        sems=[pltpu.SemaphoreType.DMA((3 * na,)), pltpu.SemaphoreType.DMA((3 * na,))],
        aliases=[(na + a, a) for a in range(na)] if bufs else [],
        start=start, finish=finish)


def _join(plans):
    ins, outs, sems, aliases, offs = [], [], [], [], []
    for p in plans:
        offs.append((len(ins), len(outs), len(sems)))
        aliases += [(len(ins) + ci, len(outs) + co) for ci, co in getattr(p, "aliases", [])]
        ins += p.ins
        outs += p.out_shapes
        sems += p.sems

    def cut(p, off, i, o, s):
        return (i[off[0]:off[0] + len(p.ins)], o[off[1]:off[1] + len(p.out_shapes)],
                s[off[2]:off[2] + len(p.sems)])

    def start(i, o, s):
        for p, off in zip(plans, offs):
            p.start(*cut(p, off, i, o, s))

    def finish(i, o, s):
        for p, off in zip(plans, offs):
            p.finish(*cut(p, off, i, o, s))

    def split(results):
        return [list(results[off[1]:off[1] + len(p.out_shapes)]) for p, off in zip(plans, offs)]

    return types.SimpleNamespace(ins=ins, out_shapes=outs, sems=sems, aliases=aliases,
                                 start=start, finish=finish, split=split)


def _in_hbm(args):
    return [pltpu.with_memory_space_constraint(a, pltpu.HBM) for a in args]


def _run_plan(plan, name):
    n_in, n_out = len(plan.ins), len(plan.out_shapes)

    def body(*refs):
        ins, outs, sems = refs[:n_in], refs[n_in:n_in + n_out], refs[n_in + n_out:]
        plan.start(ins, outs, sems)
        plan.finish(ins, outs, sems)

    return pl.pallas_call(
        body,
        name=name,
        in_specs=[ANY] * n_in,
        out_specs=[ANY] * n_out,
        out_shape=plan.out_shapes,
        scratch_shapes=plan.sems,
        input_output_aliases=dict(getattr(plan, "aliases", [])),
    )(*_in_hbm(plan.ins))


def _call(body, *, name, grid, in_specs, out_specs, out_shape, args, scratch_shapes=(), aliases=None,
          carry=None):
    n_in, n_out, n_scr = len(in_specs), len(out_shape), len(scratch_shapes)
    params = pltpu.CompilerParams(
        dimension_semantics=("arbitrary",) * len(grid), vmem_limit_bytes=V7X_VMEM_LIMIT_BYTES)
    if carry is None:
        outs = pl.pallas_call(
            body, name=name, grid=grid, in_specs=list(in_specs), out_specs=list(out_specs),
            out_shape=list(out_shape), scratch_shapes=list(scratch_shapes),
            input_output_aliases=aliases or {}, compiler_params=params)(*_in_hbm(args))
        return list(outs), []
    c_in, c_out = len(carry.ins), len(carry.out_shapes)

    def full(*refs):
        p = 0
        ins = refs[p:p + n_in]
        p += n_in
        cins = refs[p:p + c_in]
        p += c_in
        outs = refs[p:p + n_out]
        p += n_out
        couts = refs[p:p + c_out]
        p += c_out
        scr = refs[p:p + n_scr]
        csems = refs[p + n_scr:]
        ids = [pl.program_id(a) for a in range(len(grid))]
        first = functools.reduce(operator.and_, [i == 0 for i in ids])
        last = functools.reduce(operator.and_, [i == g - 1 for i, g in zip(ids, grid)])

        @pl.when(first)
        def _():
            carry.start(cins, couts, csems)

        body(*ins, *outs, *scr)

        @pl.when(last)
        def _():
            carry.finish(cins, couts, csems)

    all_aliases = dict(aliases or {})
    all_aliases.update({n_in + ci: n_out + co for ci, co in getattr(carry, "aliases", [])})
    outs = pl.pallas_call(
        full, name=name, grid=grid,
        in_specs=list(in_specs) + [ANY] * c_in,
        out_specs=list(out_specs) + [ANY] * c_out,
        out_shape=list(out_shape) + list(carry.out_shapes),
        scratch_shapes=list(scratch_shapes) + list(carry.sems),
        input_output_aliases=all_aliases, compiler_params=params)(*_in_hbm(args), *_in_hbm(carry.ins))
    return list(outs[:n_out]), list(outs[n_out:])


def _norm_proj(x, g1, w_int, carry=None):
    t, d = x.shape
    n = w_int.shape[0]
    tt, tn = _tile(t, 2048), _tile(n, 512)

    def body(x_ref, g_ref, w_ref, proj_ref, h1_ref, h1_s):
        @pl.when(pl.program_id(1) == 0)
        def _():
            def norm_rows(rows):
                xhat, _ = _rms_hat(x_ref[rows, :])
                h = (xhat * g_ref[...]).astype(BF16)
                h1_s[rows, :] = h
                h1_ref[rows, :] = h

            _row_chunks(tt, norm_rows)

        proj_ref[...] = _dot_nt(h1_s[...], w_ref[...])

    return _call(
        body, name="norm_proj", grid=(t // tt, n // tn),
        in_specs=[
            pl.BlockSpec((tt, d), lambda i, j: (i, 0)),
            pl.BlockSpec((1, d), lambda i, j: (0, 0)),
            pl.BlockSpec((tn, d), lambda i, j: (j, 0)),
        ],
        out_specs=[
            pl.BlockSpec((tt, tn), lambda i, j: (i, j)),
            pl.BlockSpec((tt, d), lambda i, j: (i, 0)),
        ],
        out_shape=[jax.ShapeDtypeStruct((t, n), F32), jax.ShapeDtypeStruct((t, d), BF16)],
        scratch_shapes=[pltpu.VMEM((tt, d), BF16)],
        args=(x, g1, w_int), carry=carry)


def _fill_block_diag(w_ref, bd_ref):
    bd_ref[...] = jnp.zeros_like(bd_ref)
    hd = LRU_HEAD_DIM
    for k in range(w_ref.shape[0]):
        bd_ref[k * hd:(k + 1) * hd, k * hd:(k + 1) * hd] = w_ref[k].astype(BF16)


def _lru_fwd(proj, conv_w, conv_b, w_a, b_a, w_x, b_x, lam, carry=None):
    t = proj.shape[0]
    dr = conv_b.shape[1]
    cb = LRU_CB
    tc = _tile(t, 256)
    ncb, ntc = dr // cb, t // tc

    def body(xp_ref, g_ref, cw_ref, cb_ref, wa_ref, ba_ref, wx_ref, bx_ref, lam_ref,
             y_ref, h_ref, xc_ref, prevx_s, hlast_s, wa_s, wx_s):
        c = pl.program_id(1)

        @pl.when(c == 0)
        def _():
            prevx_s[...] = jnp.zeros_like(prevx_s)
            hlast_s[...] = jnp.zeros_like(hlast_s)
            _fill_block_diag(wa_ref, wa_s)
            _fill_block_diag(wx_ref, wx_s)

        x = xp_ref[...]
        prev = prevx_s[...]
        row = lax.broadcasted_iota(jnp.int32, x.shape, 0)

        def sh(j):
            return jnp.where(row >= j, pltpu.roll(x, j, 0), pltpu.roll(prev, j, 0))

        xc = (cb_ref[...] + cw_ref[0:1, :] * sh(3) + cw_ref[1:2, :] * sh(2)
              + cw_ref[2:3, :] * sh(1) + cw_ref[3:4, :] * x)
        prevx_s[...] = x
        xc_ref[...] = xc
        _, _, i, _, _, a, mult = _lru_gates(xc, wa_s[...], ba_ref[...], wx_s[...], bx_ref[...],
                                            lam_ref[...])
        av, bv = a, mult * (i * xc)
        s = 1
        while s < tc:
            a_sh = jnp.where(row >= s, pltpu.roll(av, s, 0), 1.0)
            b_sh = jnp.where(row >= s, pltpu.roll(bv, s, 0), 0.0)
            bv = av * b_sh + bv
            av = av * a_sh
            s *= 2
        h = av * hlast_s[...] + bv
        h_ref[...] = h
        hlast_s[...] = h_ref[tc - 1:tc, :]
        gel, _ = _gelu_and_grad(g_ref[...])
        y_ref[...] = (h * gel).astype(BF16)

    vec = pl.BlockSpec((1, cb), lambda j, c: (0, j))
    blk = pl.BlockSpec((tc, cb), lambda j, c: (c, j))
    mat = pl.BlockSpec((cb // LRU_HEAD_DIM, LRU_HEAD_DIM, LRU_HEAD_DIM), lambda j, c: (j, 0, 0))
    return _call(
        body, name="lru_fwd", grid=(ncb, ntc),
        in_specs=[
            blk,
            pl.BlockSpec((tc, cb), lambda j, c: (c, ncb + j)),
            pl.BlockSpec((4, cb), lambda j, c: (0, j)),
            vec, mat, vec, mat, vec, vec,
        ],
        out_specs=[blk, blk, blk],
        out_shape=[
            jax.ShapeDtypeStruct((t, dr), BF16),
            jax.ShapeDtypeStruct((t, dr), F32),
            jax.ShapeDtypeStruct((t, dr), F32),
        ],
        scratch_shapes=[pltpu.VMEM((tc, cb), F32), pltpu.VMEM((1, cb), F32),
                        pltpu.VMEM((cb, cb), BF16), pltpu.VMEM((cb, cb), BF16)],
        args=(proj, proj, conv_w, conv_b, w_a, b_a, w_x, b_x, lam), carry=carry)


def _pool_select(col, vals):
    out = vals[3]
    for g in (2, 1, 0):
        out = jnp.where(col < (g + 1) * POOL_GROUP_DIM, vals[g], out)
    return out


def _pool_fwd(proj, pool_w, pool_scale, col_block):
    t = proj.shape[0]
    dp = pool_scale.shape[1]
    tc = _tile(t, 256)
    ntc = t // tc

    def body(x_ref, w_ref, sc_ref, y_ref, p_ref, px, p2, p4, p8):
        c = pl.program_id(0)

        @pl.when(c == 0)
        def _():
            for s in (px, p2, p4, p8):
                s[...] = jnp.zeros_like(s)

        x = x_ref[...]
        row = lax.broadcasted_iota(jnp.int32, x.shape, 0)
        col = lax.broadcasted_iota(jnp.int32, x.shape, 1)

        def sh(v, pv, j):
            return jnp.where(row >= j, pltpu.roll(v, j, 0), pltpu.roll(pv[...], j, 0))

        s2 = x + sh(x, px, 1)
        s4 = s2 + sh(s2, p2, 2)
        s8 = s4 + sh(s4, p4, 4)
        s16 = s8 + sh(s8, p8, 8)
        px[...] = x
        p2[...] = s2
        p4[...] = s4
        p8[...] = s8
        wsum = _pool_select(col, (s2, s4, s8, s16))
        win = _pool_select(col, POOL_WINDOWS)
        cnt = jnp.minimum(c * tc + row + 1, win).astype(F32)
        p = wsum / cnt - x
        pb = p.astype(BF16)
        p_ref[...] = pb
        for g in range(len(POOL_WINDOWS)):
            sl = slice(g * POOL_GROUP_DIM, (g + 1) * POOL_GROUP_DIM)
            yg = _dot_nn(pb[:, sl], w_ref[g]) * sc_ref[:, sl]
            y_ref[:, sl] = yg.astype(BF16)

    return _call(
        body, name="pool_fwd", grid=(ntc,),
        in_specs=[
            pl.BlockSpec((tc, dp), lambda c: (c, col_block)),
            pl.BlockSpec(pool_w.shape, lambda c: (0, 0, 0)),
            pl.BlockSpec((1, dp), lambda c: (0, 0)),
        ],
        out_specs=[pl.BlockSpec((tc, dp), lambda c: (c, 0))] * 2,
        out_shape=[jax.ShapeDtypeStruct((t, dp), BF16)] * 2,
        scratch_shapes=[pltpu.VMEM((tc, dp), F32)] * 4,
        args=(proj, pool_w, pool_scale))[0]


def _branch_mix(y_lru, y_pool, w_lru_up, w_pool_upt, proj, b_gate, ga_block, gb_block, carry=None):
    t, d = y_lru.shape
    dp = y_pool.shape[1]
    tt, tn = _tile(t, 1024), 512
    nj = d // tn

    def body(yl_ref, yp_ref, wl_ref, wp_ref, ga_ref, gb_ref, ba_ref, bb_ref, bra_ref, brb_ref, mix_ref):
        br_a = _dot_nn(yl_ref[...], wl_ref[...])
        br_b = _dot_nt(yp_ref[...], wp_ref[...])
        bra_ref[...] = br_a
        brb_ref[...] = br_b
        ga = _sig(ga_ref[...] + ba_ref[...])
        gb = _sig(gb_ref[...] + bb_ref[...])
        mix_ref[...] = (ga * br_a + gb * br_b).astype(BF16)

    out = pl.BlockSpec((tt, tn), lambda j, i: (i, j))
    return _call(
        body, name="branch_mix", grid=(nj, t // tt),
        in_specs=[
            pl.BlockSpec((tt, d), lambda j, i: (i, 0)),
            pl.BlockSpec((tt, dp), lambda j, i: (i, 0)),
            pl.BlockSpec((d, tn), lambda j, i: (0, j)),
            pl.BlockSpec((tn, dp), lambda j, i: (j, 0)),
            pl.BlockSpec((tt, tn), lambda j, i: (i, ga_block + j)),
            pl.BlockSpec((tt, tn), lambda j, i: (i, gb_block + j)),
            pl.BlockSpec((1, tn), lambda j, i: (0, j)),
            pl.BlockSpec((1, tn), lambda j, i: (0, nj + j)),
        ],
        out_specs=[out, out, out],
        out_shape=[
            jax.ShapeDtypeStruct((t, d), F32),
            jax.ShapeDtypeStruct((t, d), F32),
            jax.ShapeDtypeStruct((t, d), BF16),
        ],
        args=(y_lru, y_pool, w_lru_up, w_pool_upt, proj, proj, b_gate, b_gate), carry=carry)


def _wo_norm(mix, w_o, x, g2, g3, carry=None):
    t, d = x.shape
    tt = _tile(t, 512)

    def body(mix_ref, w_ref, x_ref, g2_ref, g3_ref, m_ref, x2_ref, h3_ref):
        m = _dot_nn(mix_ref[...], w_ref[...])
        m_ref[...] = m
        mhat, _ = _rms_hat(m)
        x2 = x_ref[...] + mhat * g2_ref[...]
        x2_ref[...] = x2
        xhat, _ = _rms_hat(x2)
        h3_ref[...] = (xhat * g3_ref[...]).astype(BF16)

    row = pl.BlockSpec((tt, d), lambda i: (i, 0))
    vec = pl.BlockSpec((1, d), lambda i: (0, 0))
    return _call(
        body, name="wo_norm", grid=(t // tt,),
        in_specs=[row, pl.BlockSpec((d, d), lambda i: (0, 0)), row, vec, vec],
        out_specs=[row, row, row],
        out_shape=[
            jax.ShapeDtypeStruct((t, d), F32),
            jax.ShapeDtypeStruct((t, d), F32),
            jax.ShapeDtypeStruct((t, d), BF16),
        ],
        args=(mix, w_o, x, g2, g3), carry=carry)


def _ff1(h3, w_ff1t, carry=None):
    t, d = h3.shape
    n = w_ff1t.shape[0]
    tt, tn = _tile(t, 2048), _tile(n, 512)

    def body(h_ref, w_ref, rf_ref, act_ref):
        rf = jnp.maximum(_dot_nt(h_ref[...], w_ref[...]), 0.0)
        rf_ref[...] = rf.astype(BF16)
        act_ref[...] = (rf * rf).astype(BF16)

    out = pl.BlockSpec((tt, tn), lambda i, j: (i, j))
    return _call(
        body, name="ff1", grid=(t // tt, n // tn),
        in_specs=[pl.BlockSpec((tt, d), lambda i, j: (i, 0)), pl.BlockSpec((tn, d), lambda i, j: (j, 0))],
        out_specs=[out, out],
        out_shape=[jax.ShapeDtypeStruct((t, n), BF16)] * 2,
        args=(h3, w_ff1t), carry=carry)


def _ff2_loss(act, w_ff2, x2, g4, target):
    t, k = act.shape
    d = x2.shape[1]
    tt, tk = _tile(t, 1024), _tile(k, 512)
    nk = k // tk

    def body(a_ref, w_ref, x2_ref, g_ref, tg_ref, dy_ref, df_ref, dg_ref, loss_ref, acc):
        i, kk = pl.program_id(0), pl.program_id(1)

        @pl.when(kk == 0)
        def _():
            acc[...] = jnp.zeros_like(acc)

        @pl.when((i == 0) & (kk == 0))
        def _():
            dg_ref[...] = jnp.zeros_like(dg_ref)
            loss_ref[...] = jnp.zeros_like(loss_ref)

        acc[...] += _dot_nn(a_ref[...], w_ref[...])

        @pl.when(kk == nk - 1)
        def _():
            def tail(rows):
                fhat, r = _rms_hat(acc[rows, :])
                g = g_ref[...]
                e = x2_ref[rows, :] + fhat * g - tg_ref[rows, :]
                loss_ref[...] += 0.5 * jnp.sum(jnp.mean(e * e, axis=-1, keepdims=True))
                dy = e * (1.0 / d)
                dy_ref[rows, :] = dy
                df, dg = _rms_bwd(dy, fhat, r, g)
                df_ref[rows, :] = df.astype(BF16)
                dg_ref[...] += dg

            _row_chunks(tt, tail)

    row = pl.BlockSpec((tt, d), lambda i, kk: (i, 0))
    vec = pl.BlockSpec((1, d), lambda i, kk: (0, 0))
    return _call(
        body, name="ff2_loss", grid=(t // tt, nk),
        in_specs=[
            pl.BlockSpec((tt, tk), lambda i, kk: (i, kk)),
            pl.BlockSpec((tk, d), lambda i, kk: (kk, 0)),
            row, vec, row,
        ],
        out_specs=[row, row, vec, pl.BlockSpec((1, 128), lambda i, kk: (0, 0))],
        out_shape=[
            jax.ShapeDtypeStruct((t, d), F32),
            jax.ShapeDtypeStruct((t, d), BF16),
            jax.ShapeDtypeStruct((1, d), F32),
            jax.ShapeDtypeStruct((1, 128), F32),
        ],
        scratch_shapes=[pltpu.VMEM((tt, d), F32)],
        args=(act, w_ff2, x2, g4, target))[0]


def _ff2_bwd(df, w_ff2, rf):
    t, d = df.shape
    n = w_ff2.shape[0]
    tt, tn = _tile(t, 2048), _tile(n, 512)

    def body(df_ref, w_ref, rf_ref, out_ref):
        d_act = _dot_nt(df_ref[...], w_ref[...])
        out_ref[...] = (d_act * (2.0 * rf_ref[...].astype(F32))).astype(BF16)

    blk = pl.BlockSpec((tt, tn), lambda i, j: (i, j))
    return _call(
        body, name="ff2_bwd", grid=(t // tt, n // tn),
        in_specs=[pl.BlockSpec((tt, d), lambda i, j: (i, 0)), pl.BlockSpec((tn, d), lambda i, j: (j, 0)), blk],
        out_specs=[blk],
        out_shape=[jax.ShapeDtypeStruct((t, n), BF16)],
        args=(df, w_ff2, rf))[0][0]


def _wgrad(a, b, name, prev=None, row_off=0, rows=None, carry=None):
    t, m = a.shape
    n = b.shape[1]
    rows = m if rows is None else rows
    tm, tk = _tile(m, 512), _tile(t, 2048)
    nk = t // tk
    assert row_off % tm == 0
    off = row_off // tm

    def body(*refs):
        a_ref, b_ref = refs[0], refs[1]
        o32_ref, o16_ref, acc = refs[-3], refs[-2], refs[-1]
        kk = pl.program_id(1)

        @pl.when(kk == 0)
        def _():
            acc[...] = jnp.zeros_like(acc)

        acc[...] += _dot_tn(a_ref[...], b_ref[...])

        @pl.when(kk == nk - 1)
        def _():
            o32_ref[...] = acc[...]
            o16_ref[...] = acc[...].astype(BF16)

    in_specs = [pl.BlockSpec((tk, tm), lambda i, kk: (kk, i)), pl.BlockSpec((tk, n), lambda i, kk: (kk, 0))]
    args = [a, b]
    aliases = {}
    if prev is not None:
        in_specs += [ANY, ANY]
        args += list(prev)
        aliases = {2: 0, 3: 1}
    out = pl.BlockSpec((tm, n), lambda i, kk: (off + i, 0))
    return _call(
        body, name=name, grid=(m // tm, nk),
        in_specs=in_specs, out_specs=[out, out],
        out_shape=[jax.ShapeDtypeStruct((rows, n), F32), jax.ShapeDtypeStruct((rows, n), BF16)],
        scratch_shapes=[pltpu.VMEM((tm, n), F32)],
        aliases=aliases, args=args, carry=carry)


def _ff1_bwd_norms(d_f1, w_ff1t, dy, x2, g3, m, g2, carry=None):
    t, k = d_f1.shape
    d = x2.shape[1]
    tt, tk = _tile(t, 1024), _tile(k, 512)
    nk = k // tk

    def body(a_ref, w_ref, dy_ref, x2_ref, g3_ref, m_ref, g2_ref, dx2_ref, dm_ref, dg3_ref, dg2_ref, acc):
        i, kk = pl.program_id(0), pl.program_id(1)

        @pl.when(kk == 0)
        def _():
            acc[...] = jnp.zeros_like(acc)

        @pl.when((i == 0) & (kk == 0))
        def _():
            dg3_ref[...] = jnp.zeros_like(dg3_ref)
            dg2_ref[...] = jnp.zeros_like(dg2_ref)

        acc[...] += _dot_nn(a_ref[...], w_ref[...])

        @pl.when(kk == nk - 1)
        def _():
            def tail(rows):
                xhat, r3 = _rms_hat(x2_ref[rows, :])
                dx, dg3 = _rms_bwd(acc[rows, :], xhat, r3, g3_ref[...])
                dx2 = dy_ref[rows, :] + dx
                dx2_ref[rows, :] = dx2
                dg3_ref[...] += dg3
                mhat, r2 = _rms_hat(m_ref[rows, :])
                dm, dg2 = _rms_bwd(dx2, mhat, r2, g2_ref[...])
                dm_ref[rows, :] = dm.astype(BF16)
                dg2_ref[...] += dg2

            _row_chunks(tt, tail)

    row = pl.BlockSpec((tt, d), lambda i, kk: (i, 0))
    vec = pl.BlockSpec((1, d), lambda i, kk: (0, 0))
    return _call(
        body, name="ff1_bwd_norms", grid=(t // tt, nk),
        in_specs=[
            pl.BlockSpec((tt, tk), lambda i, kk: (i, kk)),
            pl.BlockSpec((tk, d), lambda i, kk: (kk, 0)),
            row, row, vec, row, vec,
        ],
        out_specs=[row, row, vec, vec],
        out_shape=[
            jax.ShapeDtypeStruct((t, d), F32),
            jax.ShapeDtypeStruct((t, d), BF16),
            jax.ShapeDtypeStruct((1, d), F32),
            jax.ShapeDtypeStruct((1, d), F32),
        ],
        scratch_shapes=[pltpu.VMEM((tt, d), F32)],
        args=(d_f1, w_ff1t, dy, x2, g3, m, g2), carry=carry)


def _wo_bwd_mix(dm, w_o, br_a, br_b, proj, b_gate, ga_block, gb_block, carry=None):
    t, d = dm.shape
    tt, tn = _tile(t, 1024), 512
    nj = d // tn

    def body(dm_ref, w_ref, bra_ref, brb_ref, ga_ref, gb_ref, ba_ref, bb_ref,
             dbra_ref, dbrb_ref, dga_ref, dgb_ref, dba_ref, dbb_ref):
        i = pl.program_id(1)

        @pl.when(i == 0)
        def _():
            dba_ref[...] = jnp.zeros_like(dba_ref)
            dbb_ref[...] = jnp.zeros_like(dbb_ref)

        d_mix = _dot_nt(dm_ref[...], w_ref[...])
        ga = _sig(ga_ref[...] + ba_ref[...])
        gb = _sig(gb_ref[...] + bb_ref[...])
        dbra_ref[...] = (d_mix * ga).astype(BF16)
        dbrb_ref[...] = (d_mix * gb).astype(BF16)
        dga = d_mix * bra_ref[...] * (ga * (1.0 - ga))
        dgb = d_mix * brb_ref[...] * (gb * (1.0 - gb))
        dga_ref[...] = dga.astype(BF16)
        dgb_ref[...] = dgb.astype(BF16)
        dba_ref[...] += jnp.sum(dga, axis=0, keepdims=True)
        dbb_ref[...] += jnp.sum(dgb, axis=0, keepdims=True)

    blk = pl.BlockSpec((tt, tn), lambda j, i: (i, j))
    vec = pl.BlockSpec((1, tn), lambda j, i: (0, j))
    return _call(
        body, name="wo_bwd_mix", grid=(nj, t // tt),
        in_specs=[
            pl.BlockSpec((tt, d), lambda j, i: (i, 0)),
            pl.BlockSpec((tn, d), lambda j, i: (j, 0)),
            blk, blk,
            pl.BlockSpec((tt, tn), lambda j, i: (i, ga_block + j)),
            pl.BlockSpec((tt, tn), lambda j, i: (i, gb_block + j)),
            vec,
            pl.BlockSpec((1, tn), lambda j, i: (0, nj + j)),
        ],
        out_specs=[blk, blk, blk, blk, vec, vec],
        out_shape=[jax.ShapeDtypeStruct((t, d), BF16)] * 4 + [jax.ShapeDtypeStruct((1, d), F32)] * 2,
        args=(dm, w_o, br_a, br_b, proj, proj, b_gate, b_gate), carry=carry)


def _lru_up_bwd(d_br_a, w_lru_up, proj, h, g_block, carry=None):
    t, d = d_br_a.shape
    tt, tn = _tile(t, 1024), 512

    def body(a_ref, w_ref, g_ref, h_ref, dh_ref, dg_ref):
        d_y = _dot_nt(a_ref[...], w_ref[...])
        gel, gel_grad = _gelu_and_grad(g_ref[...])
        dh_ref[...] = d_y * gel
        dg_ref[...] = (d_y * h_ref[...] * gel_grad).astype(BF16)

    blk = pl.BlockSpec((tt, tn), lambda i, j: (i, j))
    return _call(
        body, name="lru_up_bwd", grid=(t // tt, d // tn),
        in_specs=[
            pl.BlockSpec((tt, d), lambda i, j: (i, 0)),
            pl.BlockSpec((tn, d), lambda i, j: (j, 0)),
            pl.BlockSpec((tt, tn), lambda i, j: (i, g_block + j)),
            blk,
        ],
        out_specs=[blk, blk],
        out_shape=[jax.ShapeDtypeStruct((t, d), F32), jax.ShapeDtypeStruct((t, d), BF16)],
        args=(d_br_a, w_lru_up, proj, h), carry=carry)


def _pool_up_bwd(d_br_b, w_pool_upt):
    t, d = d_br_b.shape
    dp = w_pool_upt.shape[1]
    tt = _tile(t, 2048)

    def body(a_ref, w_ref, out_ref):
        out_ref[...] = _dot_nn(a_ref[...], w_ref[...])

    return _call(
        body, name="pool_up_bwd", grid=(t // tt,),
        in_specs=[pl.BlockSpec((tt, d), lambda i: (i, 0)), pl.BlockSpec((d, dp), lambda i: (0, 0))],
        out_specs=[pl.BlockSpec((tt, dp), lambda i: (i, 0))],
        out_shape=[jax.ShapeDtypeStruct((t, dp), F32)],
        args=(d_br_b, w_pool_upt))[0][0]


def _lru_bwd(dh, xc, h, proj, conv_w, w_a, b_a, w_x, b_x, lam, carry=None):
    t, dr = dh.shape
    cb = LRU_CB
    hd = LRU_HEAD_DIM
    per = cb // hd
    tc = _tile(t, 256)
    ncb, ntc = dr // cb, t // tc

    def body(dh_ref, xc_ref, h_ref, hp_ref, xp_ref, cw_ref, wa_ref, ba_ref, wx_ref, bx_ref, lam_ref,
             dxp_ref, dwa_ref, dba_ref, dwx_ref, dbx_ref, dlam_ref, dcw_ref, dcb_ref,
             nextd_s, anext_s, gnext_s, tmp_s, wa_s, wx_s):
        c = pl.program_id(1)
        rc = ntc - 1 - c

        @pl.when(c == 0)
        def _():
            nextd_s[...] = jnp.zeros_like(nextd_s)
            anext_s[...] = jnp.zeros_like(anext_s)
            gnext_s[...] = jnp.zeros_like(gnext_s)
            for ref in (dwa_ref, dba_ref, dwx_ref, dbx_ref, dlam_ref, dcw_ref, dcb_ref):
                ref[...] = jnp.zeros_like(ref)
            _fill_block_diag(wa_ref, wa_s)
            _fill_block_diag(wx_ref, wx_s)

        xc = xc_ref[...]
        wa, wx, lam = wa_s[...], wx_s[...], lam_ref[...]
        xcb, r, i, sp, log_a, a, mult = _lru_gates(xc, wa, ba_ref[...], wx, bx_ref[...], lam)
        row = lax.broadcasted_iota(jnp.int32, xc.shape, 0)
        h = h_ref[...]
        hp = jnp.where(rc == 0, 0.0, hp_ref[...])
        hprev = jnp.where(row >= 1, pltpu.roll(h, 1, 0), pltpu.roll(hp, 1, 0))

        def up(v, nv, j):
            return jnp.where(row < tc - j, pltpu.roll(v, tc - j, 0), nv)

        av = up(a, anext_s[...], 1)
        bv = dh_ref[...]
        s = 1
        while s < tc:
            a_sh = up(av, 1.0, s)
            b_sh = up(bv, 0.0, s)
            bv = av * b_sh + bv
            av = av * a_sh
            s *= 2
        gt = av * gnext_s[...] + bv
        tmp_s[...] = gt
        gnext_s[...] = tmp_s[0:1, :]
        tmp_s[...] = a
        anext_s[...] = tmp_s[0:1, :]

        da = gt * hprev
        ixc = i * xc
        d_mult = gt * ixc
        d_i = gt * mult * xc
        d_xc = gt * mult * i
        d_log_a = da * a - d_mult * (a * a) / mult
        d_pre_r = (d_log_a * ((-LRU_C) * sp)) * (r * (1.0 - r))
        d_pre_i = d_i * (i * (1.0 - i))
        d_sp = jnp.sum(d_log_a * ((-LRU_C) * r), axis=0, keepdims=True)
        dlam_ref[...] += d_sp * (-1.0 / (1.0 + jnp.exp(lam)))
        dpr = d_pre_r.astype(BF16)
        dpi = d_pre_i.astype(BF16)
        dba_ref[...] += jnp.sum(d_pre_r, axis=0, keepdims=True)
        dbx_ref[...] += jnp.sum(d_pre_i, axis=0, keepdims=True)
        pa = _dot_tn(xcb, dpr)
        px = _dot_tn(xcb, dpi)
        for k in range(per):
            dwa_ref[k] += pa[k * hd:(k + 1) * hd, k * hd:(k + 1) * hd]
            dwx_ref[k] += px[k * hd:(k + 1) * hd, k * hd:(k + 1) * hd]
        d_xc = d_xc + _dot_nt(dpr, wa) + _dot_nt(dpi, wx)

        nxt = nextd_s[...]
        xp = xp_ref[...]
        dxp = cw_ref[3:4, :] * d_xc
        dcw_ref[3:4, :] += jnp.sum(xp * d_xc, axis=0, keepdims=True)
        for j in (1, 2, 3):
            uj = up(d_xc, pltpu.roll(nxt, tc - j, 0), j)
            dxp = dxp + cw_ref[3 - j:4 - j, :] * uj
            dcw_ref[3 - j:4 - j, :] += jnp.sum(xp * uj, axis=0, keepdims=True)
        dcb_ref[...] += jnp.sum(d_xc, axis=0, keepdims=True)
        nextd_s[...] = d_xc
        dxp_ref[...] = dxp.astype(BF16)

    vec = pl.BlockSpec((1, cb), lambda j, c: (0, j))
    blk = pl.BlockSpec((tc, cb), lambda j, c: (ntc - 1 - c, j))
    mat = pl.BlockSpec((per, hd, hd), lambda j, c: (j, 0, 0))
    cwb = pl.BlockSpec((4, cb), lambda j, c: (0, j))
    return _call(
        body, name="lru_bwd", grid=(ncb, ntc),
        in_specs=[
            blk, blk, blk,
            pl.BlockSpec((tc, cb), lambda j, c: (jnp.maximum(ntc - 2 - c, 0), j)),
            blk, cwb, mat, vec, mat, vec, vec,
        ],
        out_specs=[blk, mat, vec, mat, vec, vec, cwb, vec],
        out_shape=[
            jax.ShapeDtypeStruct((t, dr), BF16),
            jax.ShapeDtypeStruct(w_a.shape, F32),
            jax.ShapeDtypeStruct((1, dr), F32),
            jax.ShapeDtypeStruct(w_x.shape, F32),
            jax.ShapeDtypeStruct((1, dr), F32),
            jax.ShapeDtypeStruct((1, dr), F32),
            jax.ShapeDtypeStruct((4, dr), F32),
            jax.ShapeDtypeStruct((1, dr), F32),
        ],
        scratch_shapes=[
            pltpu.VMEM((tc, cb), F32),
            pltpu.VMEM((1, cb), F32),
            pltpu.VMEM((1, cb), F32),
            pltpu.VMEM((tc, cb), F32),
            pltpu.VMEM((cb, cb), BF16),
            pltpu.VMEM((cb, cb), BF16),
        ],
        args=(dh, xc, h, h, proj, conv_w, w_a, b_a, w_x, b_x, lam), carry=carry)


def _pool_bwd(d_y_pool, p, pool_w, pool_scale):
    t, dp = d_y_pool.shape
    tc = _tile(t, 256)
    ntc = t // tc
    ng = len(POOL_WINDOWS)

    def body(dy_ref, p_ref, w_ref, sc_ref, dx_ref, dw_ref, dsc_ref, nz, n2, n4, n8, dp_s):
        c = pl.program_id(0)
        rc = ntc - 1 - c

        @pl.when(c == 0)
        def _():
            for s in (nz, n2, n4, n8):
                s[...] = jnp.zeros_like(s)
            dw_ref[...] = jnp.zeros_like(dw_ref)
            dsc_ref[...] = jnp.zeros_like(dsc_ref)

        for g in range(ng):
            sl = slice(g * POOL_GROUP_DIM, (g + 1) * POOL_GROUP_DIM)
            pg = p_ref[:, sl]
            dyg = dy_ref[:, sl]
            wg = w_ref[g].astype(BF16)
            q = _dot_nn(pg, wg)
            dsc_ref[:, sl] += jnp.sum(dyg * q, axis=0, keepdims=True)
            dpw = (dyg * sc_ref[:, sl]).astype(BF16)
            dw_ref[g] += _dot_tn(pg, dpw)
            dp_s[:, sl] = _dot_nt(dpw, wg)

        dpv = dp_s[...]
        row = lax.broadcasted_iota(jnp.int32, dpv.shape, 0)
        col = lax.broadcasted_iota(jnp.int32, dpv.shape, 1)
        win = _pool_select(col, POOL_WINDOWS)
        cnt = jnp.minimum(rc * tc + row + 1, win).astype(F32)
        z = dpv / cnt

        def up(v, nv, j):
            return jnp.where(row < tc - j, pltpu.roll(v, tc - j, 0), pltpu.roll(nv[...], tc - j, 0))

        u2 = z + up(z, nz, 1)
        u4 = u2 + up(u2, n2, 2)
        u8 = u4 + up(u4, n4, 4)
        u16 = u8 + up(u8, n8, 8)
        nz[...] = z
        n2[...] = u2
        n4[...] = u4
        n8[...] = u8
        dx_ref[...] = (_pool_select(col, (u2, u4, u8, u16)) - dpv).astype(BF16)

    blk = pl.BlockSpec((tc, dp), lambda c: (ntc - 1 - c, 0))
    full_w = pl.BlockSpec(pool_w.shape, lambda c: (0, 0, 0))
    vec = pl.BlockSpec((1, dp), lambda c: (0, 0))
    return _call(
        body, name="pool_bwd", grid=(ntc,),
        in_specs=[blk, blk, full_w, vec],
        out_specs=[blk, full_w, vec],
        out_shape=[
            jax.ShapeDtypeStruct((t, dp), BF16),
            jax.ShapeDtypeStruct(pool_w.shape, F32),
            jax.ShapeDtypeStruct((1, dp), F32),
        ],
        scratch_shapes=[pltpu.VMEM((tc, dp), F32)] * 5,
        args=(d_y_pool, p, pool_w, pool_scale))[0]


def _win_bwd_norm(parts, w_int, dx2, x, g1, carry=None):
    t, d = x.shape
    tk = 512
    tt = _tile(t, 1024)
    bounds = []
    k0 = 0
    for part in parts:
        assert part.shape[1] % tk == 0
        bounds.append((k0, k0 + part.shape[1] // tk))
        k0 += part.shape[1] // tk
    nk = k0
    assert nk * tk == w_int.shape[0]
    np_ = len(parts)

    def body(*refs):
        p_refs = refs[:np_]
        w_ref, dx2_ref, x_ref, g_ref, gx_ref, dg_ref, acc = refs[np_:]
        i, kk = pl.program_id(0), pl.program_id(1)

        @pl.when(kk == 0)
        def _():
            acc[...] = jnp.zeros_like(acc)

        @pl.when((i == 0) & (kk == 0))
        def _():
            dg_ref[...] = jnp.zeros_like(dg_ref)

        for (lo, hi), p_ref in zip(bounds, p_refs):
            @pl.when((kk >= lo) & (kk < hi))
            def _(p_ref=p_ref):
                acc[...] += _dot_nn(p_ref[...], w_ref[...])

        @pl.when(kk == nk - 1)
        def _():
            def tail(rows):
                xhat, r = _rms_hat(x_ref[rows, :])
                dx, dg = _rms_bwd(acc[rows, :], xhat, r, g_ref[...])
                gx_ref[rows, :] = dx2_ref[rows, :] + dx
                dg_ref[...] += dg

            _row_chunks(tt, tail)

    def part_spec(lo, hi):
        return pl.BlockSpec((tt, tk), lambda i, kk: (i, jnp.clip(kk - lo, 0, hi - lo - 1)))

    row = pl.BlockSpec((tt, d), lambda i, kk: (i, 0))
    vec = pl.BlockSpec((1, d), lambda i, kk: (0, 0))
    return _call(
        body, name="win_bwd_norm", grid=(t // tt, nk),
        in_specs=[part_spec(lo, hi) for lo, hi in bounds]
        + [pl.BlockSpec((tk, d), lambda i, kk: (kk, 0)), row, row, vec],
        out_specs=[row, vec],
        out_shape=[jax.ShapeDtypeStruct((t, d), F32), jax.ShapeDtypeStruct((1, d), F32)],
        scratch_shapes=[pltpu.VMEM((tt, d), F32)],
        args=(*parts, w_int, dx2, x, g1), carry=carry)


def _adam_math(w, g, m, v):
    m = ADAM_B1 * m + (1.0 - ADAM_B1) * g
    v = ADAM_B2 * v + (1.0 - ADAM_B2) * (g * g)
    m_hat = m / (1.0 - ADAM_B1 ** ADAM_STEP)
    v_hat = v / (1.0 - ADAM_B2 ** ADAM_STEP)
    delta = -ADAM_LR * (m_hat / (jnp.sqrt(v_hat) + ADAM_EPS) + ADAM_WD * w)
    return delta, m, v


def _adamw_big(ws, gs, ms, vs):
    n = len(ws)
    nb = 8

    def body(*refs):
        for a in range(n):
            w_ref, g_ref, m_ref, v_ref = refs[4 * a:4 * a + 4]
            d_ref, nm_ref, nv_ref = refs[4 * n + 3 * a:4 * n + 3 * a + 3]
            dl, m, v = _adam_math(w_ref[...], g_ref[...], m_ref[...], v_ref[...])
            d_ref[...] = dl
            nm_ref[...] = m
            nv_ref[...] = v

    in_specs, out_specs, out_shape, args = [], [], [], []
    for w, g, m, v in zip(ws, gs, ms, vs):
        rows, cols = w.shape
        blk = pl.BlockSpec((rows // nb, cols), lambda i: (i, 0))
        in_specs += [blk] * 4
        args += [w, g, m, v]
        out_specs += [blk] * 3
        out_shape += [jax.ShapeDtypeStruct(w.shape, F32)] * 3
    outs = _call(body, name="adamw_big", grid=(nb,), in_specs=in_specs, out_specs=out_specs,
                 out_shape=out_shape, args=args)[0]
    return [tuple(outs[3 * a:3 * a + 3]) for a in range(n)]


SMALL_ORDER = ("norm_mix_pre", "norm_mix_post", "norm_mlp_pre", "norm_mlp_post", "b_gate", "conv_w", "conv_b",
               "lru_w_a", "lru_b_a", "lru_w_x", "lru_b_x", "lru_lambda", "pool_w", "pool_scale")
VEC_ROW = dict(norm_mix_pre=0, norm_mix_post=1, norm_mlp_pre=2, norm_mlp_post=3, conv_b=6, lru_b_a=7,
               lru_b_x=8, lru_lambda=9)
ROW_B_GATE, ROW_POOL_SCALE, ROW_CONV_W, ROW_LOSS, N_VEC_ROWS = 4, 10, 11, 15, 16


def _adamw_small(vec_parts, g_pool, g_wa, g_wx, me, params):
    d = vec_parts.shape[2]
    names = SMALL_ORDER
    n = len(names)
    cw_cols = params["conv_w"][0].shape[2]

    def body(me_ref, vec_ref, vecc_ref, gp_ref, gwa_ref, gwx_ref, *refs):
        wmv = refs[:3 * n]
        loss_ref = refs[3 * n]
        outs = refs[3 * n + 1:3 * n + 1 + 4 * n]
        vs, vsc = refs[3 * n + 1 + 4 * n:]
        acc, accc = vec_ref[0], vecc_ref[0]
        for k in range(1, N_DEV):
            acc = acc + vec_ref[k]
            accc = accc + vecc_ref[k]
        vs[...] = acc
        vsc[...] = accc
        loss_ref[...] = vs[ROW_LOSS:ROW_LOSS + 1, 0:128]

        def upd(a, g, idx):
            w_ref, m_ref, v_ref = wmv[3 * a:3 * a + 3]
            g_ref, d_ref, nm_ref, nv_ref = outs[4 * a:4 * a + 4]
            dl, m, v = _adam_math(w_ref[idx], g, m_ref[idx], v_ref[idx])
            g_ref[idx] = g
            d_ref[idx] = dl
            nm_ref[idx] = m
            nv_ref[idx] = v

        for a, name in enumerate(names):
            if name in VEC_ROW:
                r = VEC_ROW[name]
                upd(a, vs[r:r + 1, :], (slice(None), slice(None)))
            elif name == "b_gate":
                for half in range(2):
                    r = ROW_B_GATE + half
                    upd(a, vs[r:r + 1, :], (slice(None), slice(half * d, (half + 1) * d)))
            elif name == "pool_scale":
                width = params[name][0].shape[1]
                upd(a, vs[ROW_POOL_SCALE:ROW_POOL_SCALE + 1, 0:width], (slice(None), slice(None)))
            elif name == "conv_w":
                upd(a, vsc[ROW_CONV_W:ROW_CONV_W + 4, :], (0,))
            elif name == "pool_w":
                upd(a, gp_ref[...], (Ellipsis,))
            elif name == "lru_w_a":
                upd(a, gwa_ref[...], (Ellipsis,))
            elif name == "lru_w_x":
                upd(a, gwx_ref[...], (Ellipsis,))
            else:
                raise ValueError(name)

    def whole(shape):
        nd = len(shape)
        return pl.BlockSpec(tuple(shape), lambda i, me_ref: (0,) * nd)

    in_specs = [
        whole(vec_parts.shape),
        pl.BlockSpec((N_DEV, N_VEC_ROWS, cw_cols), lambda i, me_ref: (0, 0, me_ref[0])),
        whole(g_pool.shape), whole(g_wa.shape), whole(g_wx.shape),
    ]
    args = [vec_parts, vec_parts, g_pool, g_wa, g_wx]
    out_specs = [whole((1, 128))]
    out_shape = [jax.ShapeDtypeStruct((1, 128), F32)]
    for name in names:
        for arr in params[name]:
            in_specs.append(whole(arr.shape))
            args.append(arr)
        shp = params[name][0].shape
        out_specs += [whole(shp)] * 4
        out_shape += [jax.ShapeDtypeStruct(shp, F32)] * 4
    grid_spec = pltpu.PrefetchScalarGridSpec(
        num_scalar_prefetch=1, grid=(1,), in_specs=in_specs, out_specs=out_specs,
        scratch_shapes=[pltpu.VMEM((N_VEC_ROWS, d), F32), pltpu.VMEM((N_VEC_ROWS, cw_cols), F32)])
    outs = pl.pallas_call(
        body, name="adamw_small", grid_spec=grid_spec, out_shape=out_shape,
        compiler_params=pltpu.CompilerParams(
            dimension_semantics=("arbitrary",), vmem_limit_bytes=V7X_VMEM_LIMIT_BYTES),
    )(me, *_in_hbm(args))
    return outs[0], {name: tuple(outs[1 + 4 * a:5 + 4 * a]) for a, name in enumerate(names)}


def _rs_sum(full, recv, shard_ids, slot_ids, name):
    r, rest = recv.shape[1], tuple(recv.shape[2:])
    zeros = (0,) * len(rest)
    send_dtype = recv.dtype

    def body(sh_ref, sl_ref, full_ref, recv_ref, own_ref, send_ref):
        s = pl.program_id(0)
        v = full_ref[...] + recv_ref[...].astype(F32)

        @pl.when(s == 0)
        def _():
            own_ref[...] = v

        @pl.when(s > 0)
        def _():
            send_ref[...] = v.astype(send_dtype)

    grid_spec = pltpu.PrefetchScalarGridSpec(
        num_scalar_prefetch=2,
        grid=(4,),
        in_specs=[
            pl.BlockSpec((r,) + rest, lambda s, sh, sl: (sh[s],) + zeros),
            pl.BlockSpec((None, r) + rest, lambda s, sh, sl: (sl[s], 0) + zeros),
        ],
        out_specs=[
            pl.BlockSpec((None, r) + rest, lambda s, sh, sl: (0, 0) + zeros),
            pl.BlockSpec((None, r) + rest, lambda s, sh, sl: (jnp.maximum(s - 1, 0), 0) + zeros),
        ],
    )
    return pl.pallas_call(
        body,
        name=name,
        grid_spec=grid_spec,
        out_shape=[jax.ShapeDtypeStruct((1, r) + rest, F32), jax.ShapeDtypeStruct((3, r) + rest, send_dtype)],
        compiler_params=pltpu.CompilerParams(
            dimension_semantics=("arbitrary",), vmem_limit_bytes=V7X_VMEM_LIMIT_BYTES),
    )(shard_ids, slot_ids, *_in_hbm([full, recv]))


def _finals(pairs):
    nb = 4
    n = len(pairs)

    def body(*refs):
        for a in range(n):
            own_ref, recv_ref = refs[2 * a], refs[2 * a + 1]
            acc = own_ref[...]
            for k in range(3):
                acc = acc + recv_ref[k].astype(F32)
            refs[2 * n + a][...] = acc

    in_specs, out_specs, out_shape, args = [], [], [], []
    for own, recv in pairs:
        _, rows, cols = own.shape
        in_specs += [pl.BlockSpec((None, rows // nb, cols), lambda i: (0, i, 0)),
                     pl.BlockSpec((3, rows // nb, cols), lambda i: (0, i, 0))]
        args += [own, recv]
        out_specs.append(pl.BlockSpec((rows // nb, cols), lambda i: (i, 0)))
        out_shape.append(jax.ShapeDtypeStruct((rows, cols), F32))
    return _call(body, name="rs_finals", grid=(nb,), in_specs=in_specs, out_specs=out_specs,
                 out_shape=out_shape, args=args)[0]


def _rs_level1(fulls_f32, fulls_send, tag):
    x, y, c = _place()
    recv1 = _run_plan(_rs_sibling_plan(fulls_send), "rs_sibling_" + tag)
    qs = jnp.stack([2 * x + y, 2 * (1 - x) + y, 2 * x + (1 - y), 2 * (1 - x) + (1 - y)]).astype(jnp.int32)
    shard_ids = 2 * qs + c
    return [_rs_sum(f32, r1, shard_ids, qs, f"rs_sum_{tag}{a}")
            for a, (f32, r1) in enumerate(zip(fulls_f32, recv1))]


def _rows(g):
    return g.reshape(g.shape[0] * g.shape[1], g.shape[2])


def kernel(x, norm_mix_pre, norm_mix_post, norm_mlp_pre, norm_mlp_post, w_in, b_gate, conv_w, conv_b, lru_w_a, lru_b_a, lru_w_x, lru_b_x, lru_lambda, pool_w, pool_scale, w_lru_up, w_pool_up, w_o, w_ff1, w_ff2, loss_target, m_norm_mix_pre, m_norm_mix_post, m_norm_mlp_pre, m_norm_mlp_post, m_w_in, m_b_gate, m_conv_w, m_conv_b, m_lru_w_a, m_lru_b_a, m_lru_w_x, m_lru_b_x, m_lru_lambda, m_pool_w, m_pool_scale, m_w_lru_up, m_w_pool_up, m_w_o, m_w_ff1, m_w_ff2, v_norm_mix_pre, v_norm_mix_post, v_norm_mlp_pre, v_norm_mlp_post, v_w_in, v_b_gate, v_conv_w, v_conv_b, v_lru_w_a, v_lru_b_a, v_lru_w_x, v_lru_b_x, v_lru_lambda, v_pool_w, v_pool_scale, v_w_lru_up, v_w_pool_up, v_w_o, v_w_ff1, v_w_ff2):
    t, d = x.shape[1], x.shape[2]
    d_rnn = conv_b.shape[1]
    d_pool = pool_scale.shape[1]
    per = LRU_CB // LRU_HEAD_DIM
    xi, yi, ci = _place()
    me = 4 * xi + 2 * yi + ci

    x2d = x[0]
    tgt = loss_target[0]

    s_in = w_in[0].T.astype(BF16)
    s_lu = w_lru_up[0].astype(BF16)
    s_pu = w_pool_up[0].T.astype(BF16)
    s_o = w_o[0].astype(BF16)
    s_f1 = w_ff1[0].T.astype(BF16)
    s_f2 = w_ff2[0].astype(BF16)
    s_cw = jnp.pad(conv_w[0], ((0, 4), (0, 0)))

    g_in, g_cw = _run_plan(_ag_plan([s_in, s_cw]), "ag_w_in")
    w_int = _rows(g_in)
    conv_w_full = jnp.transpose(g_cw[:, :4, :], (1, 0, 2)).reshape(4, d_rnn)

    wa_bd, wx_bd = lru_w_a[0], lru_w_x[0]
    pw = pool_w[0]
    pw_bf = pw.astype(BF16)

    pool_block = (2 * d_rnn) // d_pool
    ga_block = (2 * d_rnn + d_pool) // 512
    gb_block = ga_block + d // 512
    g_block = d_rnn // 512

    r_f1, r_f2 = s_f1.shape[0], s_f2.shape[0]
    f1_cut = r_f1 // 4
    f2_cuts = (0, (3 * r_f2) // 8, (5 * r_f2) // 8, r_f2)
    plan = _join([_ag_plan([s_lu, s_pu, s_o]), _ag_plan([s_f1], pieces=[(0, f1_cut)])])
    (proj, h1), got = _norm_proj(x2d, norm_mix_pre, w_int, carry=plan)
    (g_lu, g_pu, g_o), (g_f1,) = plan.split(got)
    w_lu, w_put, w_og = _rows(g_lu), _rows(g_pu), _rows(g_o)
    (y_lru, h, xc), (g_f1,) = _lru_fwd(
        proj, conv_w_full, conv_b, wa_bd, lru_b_a, wx_bd, lru_b_x, lru_lambda,
        carry=_ag_plan([s_f1], pieces=[(f1_cut, r_f1 - f1_cut)], bufs=[g_f1]))
    w_f1t = _rows(g_f1)
    y_pool, p = _pool_fwd(proj, pw_bf, pool_scale, pool_block)
    (br_a, br_b, mix), (g_f2,) = _branch_mix(
        y_lru, y_pool, w_lu, w_put, proj, b_gate, ga_block, gb_block,
        carry=_ag_plan([s_f2], pieces=[(f2_cuts[0], f2_cuts[1] - f2_cuts[0])]))
    (m, x2, h3), (g_f2,) = _wo_norm(
        mix, w_og, x2d, norm_mix_post, norm_mlp_pre,
        carry=_ag_plan([s_f2], pieces=[(f2_cuts[1], f2_cuts[2] - f2_cuts[1])], bufs=[g_f2]))
    (rf, act), (g_f2,) = _ff1(
        h3, w_f1t, carry=_ag_plan([s_f2], pieces=[(f2_cuts[2], f2_cuts[3] - f2_cuts[2])], bufs=[g_f2]))
    w_f2 = _rows(g_f2)
    dy, df, dg4, loss_part = _ff2_loss(act, w_f2, x2, norm_mlp_post, tgt)

    d_f1 = _ff2_bwd(df, w_f2, rf)
    (gw_ff2_32, gw_ff2_16), _ = _wgrad(act, df, "wgrad_ff2")
    ((own_ff2, send_ff2),) = _rs_level1([gw_ff2_32], [gw_ff2_16], "ff2")
    half = send_ff2.shape[1] // 2
    (gw_ff1_32, gw_ff1_16), (r2_ff2,) = _wgrad(
        d_f1, h3, "wgrad_ff1", carry=_rs_chips_plan([send_ff2], pieces=[(0, half)]))
    ((own_ff1, send_ff1),) = _rs_level1([gw_ff1_32], [gw_ff1_16], "ff1")
    plan = _join([_rs_chips_plan([send_ff2], pieces=[(half, half)], bufs=[r2_ff2]),
                  _rs_chips_plan([send_ff1], pieces=[(0, half)])])
    (dx2, dm, dg3, dg2), got = _ff1_bwd_norms(d_f1, w_f1t, dy, x2, norm_mlp_pre, m, norm_mix_post, carry=plan)
    (r2_ff2,), (r2_ff1,) = plan.split(got)
    (gw_o_32, gw_o_16), _ = _wgrad(mix, dm, "wgrad_o")
    (d_br_a, d_br_b, p_ga, p_gb, dbg_a, dbg_b), (r2_ff1,) = _wo_bwd_mix(
        dm, w_og, br_a, br_b, proj, b_gate, ga_block, gb_block,
        carry=_rs_chips_plan([send_ff1], pieces=[(half, half)], bufs=[r2_ff1]))
    (gw_lu_32, gw_lu_16), _ = _wgrad(y_lru, d_br_a, "wgrad_lru_up")
    (gw_pu_32, gw_pu_16), _ = _wgrad(d_br_b, y_pool, "wgrad_pool_up")
    mid = _rs_level1([gw_o_32, gw_lu_32, gw_pu_32.reshape(-1, d)],
                     [gw_o_16, gw_lu_16, gw_pu_16.reshape(-1, d)], "mid")
    (dh, p_g), (r2_o,) = _lru_up_bwd(d_br_a, w_lu, proj, h, g_block, carry=_rs_chips_plan([mid[0][1]]))
    d_y_pool = _pool_up_bwd(d_br_b, w_put)
    (p_x, dwa, db_a, dwx, db_x, dlam, dconv_w, dconv_b), (r2_lu, r2_pu) = _lru_bwd(
        dh, xc, h, proj, conv_w_full, wa_bd, lru_b_a, wx_bd, lru_b_x, lru_lambda,
        carry=_rs_chips_plan([mid[1][1], mid[2][1]]))
    r2_mid = [r2_o, r2_lu, r2_pu]
    p_p, dpool_w, dpool_scale = _pool_bwd(d_y_pool, p, pw, pool_scale)
    parts = [p_x, p_g, p_p, p_ga, p_gb]
    gw_in = None
    row_off = 0
    for k, part in enumerate(parts):
        gw_in, _ = _wgrad(part, h1, f"wgrad_in{k}", prev=gw_in, row_off=row_off, rows=w_int.shape[0])
        row_off += part.shape[1]
    tail = _rs_level1([gw_in[0], dpool_w.reshape(N_DEV, -1, POOL_GROUP_DIM), dwa, dwx],
                      [gw_in[1], dpool_w.reshape(N_DEV, -1, POOL_GROUP_DIM), dwa, dwx], "in")
    (grad_x, dg1), r2_tail = _win_bwd_norm(parts, w_int, dx2, x2d, norm_mix_pre,
                                           carry=_rs_chips_plan([s for _, s in tail]))

    def flat2(a):
        return a.reshape(a.shape[0], -1, a.shape[-1])

    fin = _finals([
        (tail[0][0], r2_tail[0]), (mid[1][0], r2_mid[1]), (mid[2][0], r2_mid[2]), (mid[0][0], r2_mid[0]),
        (own_ff1, r2_ff1), (own_ff2, r2_ff2),
        (flat2(tail[1][0]), flat2(r2_tail[1])), (flat2(tail[2][0]), flat2(r2_tail[2])),
        (flat2(tail[3][0]), flat2(r2_tail[3])),
    ])
    g_w_in = fin[0].T
    g_w_lru_up = fin[1]
    g_w_pool_up = fin[2].reshape(d // N_DEV, d_pool).T
    g_w_o = fin[3]
    g_w_ff1 = fin[4].T
    g_w_ff2 = fin[5]

    def pad_row(a):
        return jnp.pad(a, ((0, 0), (0, d - a.shape[1])))

    vecs = jnp.concatenate([dg1, dg2, dg3, dg4, dbg_a, dbg_b, dconv_b, db_a, db_x, dlam,
                            pad_row(dpool_scale), dconv_w, pad_row(loss_part)], axis=0)
    assert vecs.shape[0] == N_VEC_ROWS
    vec_parts, g_pool, g_wa, g_wx = _run_plan(_ag_plan([vecs, fin[6], fin[7], fin[8]]), "ag_tail")

    small = dict(
        norm_mix_pre=(norm_mix_pre, m_norm_mix_pre, v_norm_mix_pre),
        norm_mix_post=(norm_mix_post, m_norm_mix_post, v_norm_mix_post),
        norm_mlp_pre=(norm_mlp_pre, m_norm_mlp_pre, v_norm_mlp_pre),
        norm_mlp_post=(norm_mlp_post, m_norm_mlp_post, v_norm_mlp_post),
        b_gate=(b_gate, m_b_gate, v_b_gate), conv_w=(conv_w, m_conv_w, v_conv_w),
        conv_b=(conv_b, m_conv_b, v_conv_b), lru_w_a=(lru_w_a, m_lru_w_a, v_lru_w_a),
        lru_b_a=(lru_b_a, m_lru_b_a, v_lru_b_a), lru_w_x=(lru_w_x, m_lru_w_x, v_lru_w_x),
        lru_b_x=(lru_b_x, m_lru_b_x, v_lru_b_x), lru_lambda=(lru_lambda, m_lru_lambda, v_lru_lambda),
        pool_w=(pool_w, m_pool_w, v_pool_w), pool_scale=(pool_scale, m_pool_scale, v_pool_scale))
    loss_row, small_out = _adamw_small(
        vec_parts, g_pool.reshape(pool_w.shape), g_wa.reshape(lru_w_a.shape), g_wx.reshape(lru_w_x.shape),
        jnp.reshape(me, (1,)).astype(jnp.int32), small)
    grads = {n: o[0] for n, o in small_out.items()}
    delta = {n: o[1] for n, o in small_out.items()}
    new_m = {n: o[2] for n, o in small_out.items()}
    new_v = {n: o[3] for n, o in small_out.items()}

    big_names = ["w_in", "w_lru_up", "w_pool_up", "w_o", "w_ff1", "w_ff2"]
    big_w = [w_in, w_lru_up, w_pool_up, w_o, w_ff1, w_ff2]
    big_g = [g_w_in, g_w_lru_up, g_w_pool_up, g_w_o, g_w_ff1, g_w_ff2]
    big_m = [m_w_in, m_w_lru_up, m_w_pool_up, m_w_o, m_w_ff1, m_w_ff2]
    big_v = [v_w_in, v_w_lru_up, v_w_pool_up, v_w_o, v_w_ff1, v_w_ff2]
    big_out = _adamw_big([w[0] for w in big_w], big_g, [mm[0] for mm in big_m], [vv[0] for vv in big_v])
    for name, g, (dl, nm, nv) in zip(big_names, big_g, big_out):
        grads[name], delta[name], new_m[name], new_v[name] = g[None], dl[None], nm[None], nv[None]

    loss = loss_row[0, 0]
    order = ["norm_mix_pre", "norm_mix_post", "norm_mlp_pre", "norm_mlp_post", "w_in", "b_gate", "conv_w",
             "conv_b", "lru_w_a", "lru_b_a", "lru_w_x", "lru_b_x", "lru_lambda", "pool_w", "pool_scale",
             "w_lru_up", "w_pool_up", "w_o", "w_ff1", "w_ff2"]
    return (loss, grad_x[None], *[grads[n] for n in order], *[delta[n] for n in order],
            *[new_m[n] for n in order], *[new_v[n] for n in order])
```

```python
import functools
import math
import operator
import types

import jax
import jax.numpy as jnp
from jax import lax
from jax.experimental import pallas as pl
from jax.experimental.pallas import tpu as pltpu

F32 = jnp.float32
BF16 = jnp.bfloat16
NORM_EPS = 1e-6
LRU_C = 8.0
N_LRU_HEADS = 16
LRU_HEAD_DIM = 64
POOL_WINDOWS = (2, 4, 8, 16)
POOL_GROUP_DIM = 128
ADAM_LR = 0.001
ADAM_B1 = 0.9
ADAM_B2 = 0.999
ADAM_EPS = 1e-08
ADAM_WD = 0.01
ADAM_STEP = 10
N_DEV = 8
V7X_VMEM_LIMIT_BYTES = 56 * 1024 * 1024
LRU_CB = 256
MESH = pl.DeviceIdType.MESH
ANY = pl.BlockSpec(memory_space=pl.ANY)


def _tile(n, pref):
    t = min(n, pref)
    assert n % t == 0, (n, pref)
    return t


def _dot_nn(a, b):
    return lax.dot_general(a, b, (((1,), (0,)), ((), ())), preferred_element_type=F32)


def _dot_nt(a, b):
    return lax.dot_general(a, b, (((1,), (1,)), ((), ())), preferred_element_type=F32)


def _dot_tn(a, b):
    return lax.dot_general(a, b, (((0,), (0,)), ((), ())), preferred_element_type=F32)


def _row_chunks(n_rows, fn, chunk=256):
    chunk = min(chunk, n_rows)
    assert n_rows % chunk == 0

    def step(r, carry):
        fn(pl.ds(pl.multiple_of(r * chunk, chunk), chunk))
        return carry

    lax.fori_loop(0, n_rows // chunk, step, 0)


def _sig(x):
    return 1.0 / (1.0 + jnp.exp(-x))


def _rms_hat(x):
    r = lax.rsqrt(jnp.mean(x * x, axis=-1, keepdims=True) + NORM_EPS)
    return x * r, r


def _rms_bwd(dn, xhat, r, g):
    q = dn * g
    dx = r * (q - xhat * jnp.mean(q * xhat, axis=-1, keepdims=True))
    dg = jnp.sum(dn * xhat, axis=0, keepdims=True)
    return dx, dg


_GELU_K = math.sqrt(2.0 / math.pi)
_GELU_C = 0.044715


def _gelu_and_grad(g):
    t = jnp.tanh(_GELU_K * (g + _GELU_C * g * g * g))
    val = 0.5 * g * (1.0 + t)
    grad = 0.5 * (1.0 + t) + 0.5 * g * (1.0 - t * t) * (_GELU_K * (1.0 + 3.0 * _GELU_C * g * g))
    return val, grad


def _softplus_neg(lam):
    z = -lam
    e = jnp.exp(-jnp.abs(z))
    u = 1.0 + e
    d = u - 1.0
    l1p = jnp.where(d == 0.0, e, jnp.log(u) * (e / jnp.where(d == 0.0, 1.0, d)))
    return jnp.maximum(z, 0.0) + l1p


def _lru_gates(xc, wa, ba, wx, bx, lam):
    xcb = xc.astype(BF16)
    r = _sig(_dot_nn(xcb, wa) + ba)
    i = _sig(_dot_nn(xcb, wx) + bx)
    sp = _softplus_neg(lam)
    log_a = (-LRU_C) * r * sp
    a = jnp.exp(log_a)
    mult = jnp.sqrt(-jnp.tanh(log_a) * (1.0 + a * a))
    return xcb, r, i, sp, log_a, a, mult


def _place():
    return lax.axis_index("x"), lax.axis_index("y"), lax.axis_index("c")


def _ag_plan(shards, pieces=None, bufs=None):
    na = len(shards)

    def parts(ins, outs, sems):
        send_sems, recv_sems, local_sems = sems
        x, y, c = _place()
        me, sibling = (x, y, c), (x, y, 1 - c)
        chips = [(1 - x, y), (x, 1 - y), (1 - x, 1 - y)]

        def own(a):
            return ins[a] if pieces is None else ins[a].at[pl.ds(*pieces[a])]

        def slot(a, px, py, pc):
            idx = 4 * px + 2 * py + pc
            return outs[a].at[idx] if pieces is None else outs[a].at[idx, pl.ds(*pieces[a])]

        def copy(a, k, block, to, src=None):
            return pltpu.make_async_remote_copy(
                src_ref=slot(a, *block) if src is None else src,
                dst_ref=slot(a, *block),
                send_sem=send_sems.at[a * 7 + k],
                recv_sem=recv_sems.at[a * 7 + k],
                device_id=to,
                device_id_type=MESH,
            )

        mine = [pltpu.make_async_copy(own(a), slot(a, *me), local_sems.at[a]) for a in range(na)]
        first = []
        for a in range(na):
            first.append(copy(a, 0, me, sibling, src=own(a)))
            first += [copy(a, 1 + j, me, (*chip, c), src=own(a)) for j, chip in enumerate(chips)]
        return me, sibling, chips, c, copy, mine, first

    def start(ins, outs, sems):
        _, _, _, _, _, mine, first = parts(ins, outs, sems)
        for cp in mine + first:
            cp.start()

    def finish(ins, outs, sems):
        me, sibling, chips, c, copy, mine, first = parts(ins, outs, sems)
        passed = []
        for j, chip in enumerate(chips):
            for a in range(na):
                copy(a, 1 + j, (*chip, c), me).wait_recv()
                fwd = copy(a, 4 + j, (*chip, c), sibling)
                fwd.start()
                passed.append(fwd)
        for a in range(na):
            copy(a, 0, sibling, me).wait_recv()
            for j, chip in enumerate(chips):
                copy(a, 4 + j, (*chip, 1 - c), me).wait_recv()
        for cp in first + passed:
            cp.wait_send()
        for cp in mine:
            cp.wait()

    return types.SimpleNamespace(
        ins=list(shards) + list(bufs or []),
        out_shapes=[jax.ShapeDtypeStruct((N_DEV,) + s.shape, s.dtype) for s in shards],
        sems=[pltpu.SemaphoreType.DMA((7 * na,)), pltpu.SemaphoreType.DMA((7 * na,)),
              pltpu.SemaphoreType.DMA((na,))],
        aliases=[(na + a, a) for a in range(na)] if bufs else [],
        start=start, finish=finish)


def _rs_sibling_plan(fulls):
    na = len(fulls)
    rs = [f.shape[0] // N_DEV for f in fulls]

    def copies(ins, outs, sems):
        send_sems, recv_sems = sems
        x, y, c = _place()
        out = []
        for a in range(na):
            for q in range(4):
                shard = 2 * q + (1 - c)
                out.append(pltpu.make_async_remote_copy(
                    src_ref=ins[a].at[pl.ds(shard * rs[a], rs[a])],
                    dst_ref=outs[a].at[q],
                    send_sem=send_sems.at[a * 4 + q],
                    recv_sem=recv_sems.at[a * 4 + q],
                    device_id=(x, y, 1 - c),
                    device_id_type=MESH,
                ))
        return out

    def start(ins, outs, sems):
        for cp in copies(ins, outs, sems):
            cp.start()

    def finish(ins, outs, sems):
        for cp in copies(ins, outs, sems):
            cp.wait()

    return types.SimpleNamespace(
        ins=list(fulls),
        out_shapes=[jax.ShapeDtypeStruct((4, r) + f.shape[1:], f.dtype) for r, f in zip(rs, fulls)],
        sems=[pltpu.SemaphoreType.DMA((4 * na,)), pltpu.SemaphoreType.DMA((4 * na,))],
        start=start, finish=finish)


def _rs_chips_plan(sends, pieces=None, bufs=None):
    na = len(sends)

    def copies(ins, outs, sems):
        send_sems, recv_sems = sems
        x, y, c = _place()
        chips = [(1 - x, y), (x, 1 - y), (1 - x, 1 - y)]
        out = []
        for a in range(na):
            for k, chip in enumerate(chips):
                rows = (k,) if pieces is None else (k, pl.ds(*pieces[a]))
                out.append(pltpu.make_async_remote_copy(
                    src_ref=ins[a].at[rows],
                    dst_ref=outs[a].at[rows],
                    send_sem=send_sems.at[a * 3 + k],
                    recv_sem=recv_sems.at[a * 3 + k],
                    device_id=(*chip, c),
                    device_id_type=MESH,
                ))
        return out

    def start(ins, outs, sems):
        for cp in copies(ins, outs, sems):
            cp.start()

    def finish(ins, outs, sems):
        for cp in copies(ins, outs, sems):
            cp.wait()

    return types.SimpleNamespace(
        ins=list(sends) + list(bufs or []),
        out_shapes=[jax.ShapeDtypeStruct(s.shape, s.dtype) for s in sends],
        sems=[pltpu.SemaphoreType.DMA((3 * na,)), pltpu.SemaphoreType.DMA((3 * na,))],
        aliases=[(na + a, a) for a in range(na)] if bufs else [],
        start=start, finish=finish)


def _join(plans):
    ins, outs, sems, aliases, offs = [], [], [], [], []
    for p in plans:
        offs.append((len(ins), len(outs), len(sems)))
        aliases += [(len(ins) + ci, len(outs) + co) for ci, co in getattr(p, "aliases", [])]
        ins += p.ins
        outs += p.out_shapes
        sems += p.sems

    def cut(p, off, i, o, s):
        return (i[off[0]:off[0] + len(p.ins)], o[off[1]:off[1] + len(p.out_shapes)],
                s[off[2]:off[2] + len(p.sems)])

    def start(i, o, s):
        for p, off in zip(plans, offs):
            p.start(*cut(p, off, i, o, s))

    def finish(i, o, s):
        for p, off in zip(plans, offs):
            p.finish(*cut(p, off, i, o, s))

    def split(results):
        return [list(results[off[1]:off[1] + len(p.out_shapes)]) for p, off in zip(plans, offs)]

    return types.SimpleNamespace(ins=ins, out_shapes=outs, sems=sems, aliases=aliases,
                                 start=start, finish=finish, split=split)


def _in_hbm(args):
    return [pltpu.with_memory_space_constraint(a, pltpu.HBM) for a in args]


def _run_plan(plan, name):
    n_in, n_out = len(plan.ins), len(plan.out_shapes)

    def body(*refs):
        ins, outs, sems = refs[:n_in], refs[n_in:n_in + n_out], refs[n_in + n_out:]
        plan.start(ins, outs, sems)
        plan.finish(ins, outs, sems)

    return pl.pallas_call(
        body,
        name=name,
        in_specs=[ANY] * n_in,
        out_specs=[ANY] * n_out,
        out_shape=plan.out_shapes,
        scratch_shapes=plan.sems,
        input_output_aliases=dict(getattr(plan, "aliases", [])),
    )(*_in_hbm(plan.ins))


def _call(body, *, name, grid, in_specs, out_specs, out_shape, args, scratch_shapes=(), aliases=None,
          carry=None):
    n_in, n_out, n_scr = len(in_specs), len(out_shape), len(scratch_shapes)
    params = pltpu.CompilerParams(
        dimension_semantics=("arbitrary",) * len(grid), vmem_limit_bytes=V7X_VMEM_LIMIT_BYTES)
    if carry is None:
        outs = pl.pallas_call(
            body, name=name, grid=grid, in_specs=list(in_specs), out_specs=list(out_specs),
            out_shape=list(out_shape), scratch_shapes=list(scratch_shapes),
            input_output_aliases=aliases or {}, compiler_params=params)(*_in_hbm(args))
        return list(outs), []
    c_in, c_out = len(carry.ins), len(carry.out_shapes)

    def full(*refs):
        p = 0
        ins = refs[p:p + n_in]
        p += n_in
        cins = refs[p:p + c_in]
        p += c_in
        outs = refs[p:p + n_out]
        p += n_out
        couts = refs[p:p + c_out]
        p += c_out
        scr = refs[p:p + n_scr]
        csems = refs[p + n_scr:]
        ids = [pl.program_id(a) for a in range(len(grid))]
        first = functools.reduce(operator.and_, [i == 0 for i in ids])
        last = functools.reduce(operator.and_, [i == g - 1 for i, g in zip(ids, grid)])

        @pl.when(first)
        def _():
            carry.start(cins, couts, csems)

        body(*ins, *outs, *scr)

        @pl.when(last)
        def _():
            carry.finish(cins, couts, csems)

    all_aliases = dict(aliases or {})
    all_aliases.update({n_in + ci: n_out + co for ci, co in getattr(carry, "aliases", [])})
    outs = pl.pallas_call(
        full, name=name, grid=grid,
        in_specs=list(in_specs) + [ANY] * c_in,
        out_specs=list(out_specs) + [ANY] * c_out,
        out_shape=list(out_shape) + list(carry.out_shapes),
        scratch_shapes=list(scratch_shapes) + list(carry.sems),
        input_output_aliases=all_aliases, compiler_params=params)(*_in_hbm(args), *_in_hbm(carry.ins))
    return list(outs[:n_out]), list(outs[n_out:])


def _norm_proj(x, g1, w_int, carry=None):
    t, d = x.shape
    n = w_int.shape[0]
    tt, tn = _tile(t, 2048), _tile(n, 512)

    def body(x_ref, g_ref, w_ref, proj_ref, h1_ref, h1_s):
        @pl.when(pl.program_id(1) == 0)
        def _():
            def norm_rows(rows):
                xhat, _ = _rms_hat(x_ref[rows, :])
                h = (xhat * g_ref[...]).astype(BF16)
                h1_s[rows, :] = h
                h1_ref[rows, :] = h

            _row_chunks(tt, norm_rows)

        proj_ref[...] = _dot_nt(h1_s[...], w_ref[...]).astype(BF16)

    return _call(
        body, name="norm_proj", grid=(t // tt, n // tn),
        in_specs=[
            pl.BlockSpec((tt, d), lambda i, j: (i, 0)),
            pl.BlockSpec((1, d), lambda i, j: (0, 0)),
            pl.BlockSpec((tn, d), lambda i, j: (j, 0)),
        ],
        out_specs=[
            pl.BlockSpec((tt, tn), lambda i, j: (i, j)),
            pl.BlockSpec((tt, d), lambda i, j: (i, 0)),
        ],
        out_shape=[jax.ShapeDtypeStruct((t, n), BF16), jax.ShapeDtypeStruct((t, d), BF16)],
        scratch_shapes=[pltpu.VMEM((tt, d), BF16)],
        args=(x, g1, w_int), carry=carry)


def _fill_block_diag(w_ref, bd_ref):
    bd_ref[...] = jnp.zeros_like(bd_ref)
    hd = LRU_HEAD_DIM
    for k in range(w_ref.shape[0]):
        bd_ref[k * hd:(k + 1) * hd, k * hd:(k + 1) * hd] = w_ref[k].astype(BF16)


def _lru_fwd(proj, conv_w, conv_b, w_a, b_a, w_x, b_x, lam, carry=None):
    t = proj.shape[0]
    dr = conv_b.shape[1]
    cb = LRU_CB
    tc = _tile(t, 256)
    ncb, ntc = dr // cb, t // tc

    def body(xp_ref, g_ref, cw_ref, cb_ref, wa_ref, ba_ref, wx_ref, bx_ref, lam_ref,
             y_ref, h_ref, xc_ref, prevx_s, hlast_s, wa_s, wx_s):
        c = pl.program_id(1)

        @pl.when(c == 0)
        def _():
            prevx_s[...] = jnp.zeros_like(prevx_s)
            hlast_s[...] = jnp.zeros_like(hlast_s)
            _fill_block_diag(wa_ref, wa_s)
            _fill_block_diag(wx_ref, wx_s)

        x = xp_ref[...].astype(F32)
        prev = prevx_s[...]
        row = lax.broadcasted_iota(jnp.int32, x.shape, 0)

        def sh(j):
            return jnp.where(row >= j, pltpu.roll(x, j, 0), pltpu.roll(prev, j, 0))

        xc = (cb_ref[...] + cw_ref[0:1, :] * sh(3) + cw_ref[1:2, :] * sh(2)
              + cw_ref[2:3, :] * sh(1) + cw_ref[3:4, :] * x)
        prevx_s[...] = x
        xc_ref[...] = xc
        _, _, i, _, _, a, mult = _lru_gates(xc, wa_s[...], ba_ref[...], wx_s[...], bx_ref[...],
                                            lam_ref[...])
        av, bv = a, mult * (i * xc)
        s = 1
        while s < tc:
            a_sh = jnp.where(row >= s, pltpu.roll(av, s, 0), 1.0)
            b_sh = jnp.where(row >= s, pltpu.roll(bv, s, 0), 0.0)
            bv = av * b_sh + bv
            av = av * a_sh
            s *= 2
        h = av * hlast_s[...] + bv
        h_ref[...] = h
        hlast_s[...] = h_ref[tc - 1:tc, :]
        gel, _ = _gelu_and_grad(g_ref[...].astype(F32))
        y_ref[...] = (h * gel).astype(BF16)

    vec = pl.BlockSpec((1, cb), lambda j, c: (0, j))
    blk = pl.BlockSpec((tc, cb), lambda j, c: (c, j))
    mat = pl.BlockSpec((cb // LRU_HEAD_DIM, LRU_HEAD_DIM, LRU_HEAD_DIM), lambda j, c: (j, 0, 0))
    return _call(
        body, name="lru_fwd", grid=(ncb, ntc),
        in_specs=[
            blk,
            pl.BlockSpec((tc, cb), lambda j, c: (c, ncb + j)),
            pl.BlockSpec((4, cb), lambda j, c: (0, j)),
            vec, mat, vec, mat, vec, vec,
        ],
        out_specs=[blk, blk, blk],
        out_shape=[
            jax.ShapeDtypeStruct((t, dr), BF16),
            jax.ShapeDtypeStruct((t, dr), F32),
            jax.ShapeDtypeStruct((t, dr), F32),
        ],
        scratch_shapes=[pltpu.VMEM((tc, cb), F32), pltpu.VMEM((1, cb), F32),
                        pltpu.VMEM((cb, cb), BF16), pltpu.VMEM((cb, cb), BF16)],
        args=(proj, proj, conv_w, conv_b, w_a, b_a, w_x, b_x, lam), carry=carry)


def _pool_select(col, vals):
    out = vals[3]
    for g in (2, 1, 0):
        out = jnp.where(col < (g + 1) * POOL_GROUP_DIM, vals[g], out)
    return out


def _pool_fwd(proj, pool_w, pool_scale, col_block):
    t = proj.shape[0]
    dp = pool_scale.shape[1]
    tc = _tile(t, 256)
    ntc = t // tc

    def body(x_ref, w_ref, sc_ref, y_ref, p_ref, px, p2, p4, p8):
        c = pl.program_id(0)

        @pl.when(c == 0)
        def _():
            for s in (px, p2, p4, p8):
                s[...] = jnp.zeros_like(s)

        x = x_ref[...].astype(F32)
        row = lax.broadcasted_iota(jnp.int32, x.shape, 0)
        col = lax.broadcasted_iota(jnp.int32, x.shape, 1)

        def sh(v, pv, j):
            return jnp.where(row >= j, pltpu.roll(v, j, 0), pltpu.roll(pv[...], j, 0))

        s2 = x + sh(x, px, 1)
        s4 = s2 + sh(s2, p2, 2)
        s8 = s4 + sh(s4, p4, 4)
        s16 = s8 + sh(s8, p8, 8)
        px[...] = x
        p2[...] = s2
        p4[...] = s4
        p8[...] = s8
        wsum = _pool_select(col, (s2, s4, s8, s16))
        win = _pool_select(col, POOL_WINDOWS)
        cnt = jnp.minimum(c * tc + row + 1, win).astype(F32)
        p = wsum / cnt - x
        pb = p.astype(BF16)
        p_ref[...] = pb
        for g in range(len(POOL_WINDOWS)):
            sl = slice(g * POOL_GROUP_DIM, (g + 1) * POOL_GROUP_DIM)
            yg = _dot_nn(pb[:, sl], w_ref[g]) * sc_ref[:, sl]
            y_ref[:, sl] = yg.astype(BF16)

    return _call(
        body, name="pool_fwd", grid=(ntc,),
        in_specs=[
            pl.BlockSpec((tc, dp), lambda c: (c, col_block)),
            pl.BlockSpec(pool_w.shape, lambda c: (0, 0, 0)),
            pl.BlockSpec((1, dp), lambda c: (0, 0)),
        ],
        out_specs=[pl.BlockSpec((tc, dp), lambda c: (c, 0))] * 2,
        out_shape=[jax.ShapeDtypeStruct((t, dp), BF16)] * 2,
        scratch_shapes=[pltpu.VMEM((tc, dp), F32)] * 4,
        args=(proj, pool_w, pool_scale))[0]


def _branch_mix(y_lru, y_pool, w_lru_up, w_pool_upt, proj, b_gate, ga_block, gb_block, carry=None):
    t, d = y_lru.shape
    dp = y_pool.shape[1]
    tt, tn = _tile(t, 1024), 512
    nj = d // tn

    def body(yl_ref, yp_ref, wl_ref, wp_ref, ga_ref, gb_ref, ba_ref, bb_ref, bra_ref, brb_ref, mix_ref):
        br_a = _dot_nn(yl_ref[...], wl_ref[...])
        br_b = _dot_nt(yp_ref[...], wp_ref[...])
        bra_ref[...] = br_a
        brb_ref[...] = br_b
        ga = _sig(ga_ref[...].astype(F32) + ba_ref[...])
        gb = _sig(gb_ref[...].astype(F32) + bb_ref[...])
        mix_ref[...] = (ga * br_a + gb * br_b).astype(BF16)

    out = pl.BlockSpec((tt, tn), lambda j, i: (i, j))
    return _call(
        body, name="branch_mix", grid=(nj, t // tt),
        in_specs=[
            pl.BlockSpec((tt, d), lambda j, i: (i, 0)),
            pl.BlockSpec((tt, dp), lambda j, i: (i, 0)),
            pl.BlockSpec((d, tn), lambda j, i: (0, j)),
            pl.BlockSpec((tn, dp), lambda j, i: (j, 0)),
            pl.BlockSpec((tt, tn), lambda j, i: (i, ga_block + j)),
            pl.BlockSpec((tt, tn), lambda j, i: (i, gb_block + j)),
            pl.BlockSpec((1, tn), lambda j, i: (0, j)),
            pl.BlockSpec((1, tn), lambda j, i: (0, nj + j)),
        ],
        out_specs=[out, out, out],
        out_shape=[
            jax.ShapeDtypeStruct((t, d), F32),
            jax.ShapeDtypeStruct((t, d), F32),
            jax.ShapeDtypeStruct((t, d), BF16),
        ],
        args=(y_lru, y_pool, w_lru_up, w_pool_upt, proj, proj, b_gate, b_gate), carry=carry)


def _wo_norm(mix, w_o, x, g2, g3, carry=None):
    t, d = x.shape
    tt = _tile(t, 512)

    def body(mix_ref, w_ref, x_ref, g2_ref, g3_ref, m_ref, x2_ref, h3_ref):
        m = _dot_nn(mix_ref[...], w_ref[...])
        m_ref[...] = m
        mhat, _ = _rms_hat(m)
        x2 = x_ref[...] + mhat * g2_ref[...]
        x2_ref[...] = x2
        xhat, _ = _rms_hat(x2)
        h3_ref[...] = (xhat * g3_ref[...]).astype(BF16)

    row = pl.BlockSpec((tt, d), lambda i: (i, 0))
    vec = pl.BlockSpec((1, d), lambda i: (0, 0))
    return _call(
        body, name="wo_norm", grid=(t // tt,),
        in_specs=[row, pl.BlockSpec((d, d), lambda i: (0, 0)), row, vec, vec],
        out_specs=[row, row, row],
        out_shape=[
            jax.ShapeDtypeStruct((t, d), F32),
            jax.ShapeDtypeStruct((t, d), F32),
            jax.ShapeDtypeStruct((t, d), BF16),
        ],
        args=(mix, w_o, x, g2, g3), carry=carry)


def _ff1(h3, w_ff1t, carry=None):
    t, d = h3.shape
    n = w_ff1t.shape[0]
    tt, tn = _tile(t, 2048), _tile(n, 512)

    def body(h_ref, w_ref, rf_ref, act_ref):
        rf = jnp.maximum(_dot_nt(h_ref[...], w_ref[...]), 0.0)
        rf_ref[...] = rf.astype(BF16)
        act_ref[...] = (rf * rf).astype(BF16)

    out = pl.BlockSpec((tt, tn), lambda i, j: (i, j))
    return _call(
        body, name="ff1", grid=(t // tt, n // tn),
        in_specs=[pl.BlockSpec((tt, d), lambda i, j: (i, 0)), pl.BlockSpec((tn, d), lambda i, j: (j, 0))],
        out_specs=[out, out],
        out_shape=[jax.ShapeDtypeStruct((t, n), BF16)] * 2,
        args=(h3, w_ff1t), carry=carry)


def _ff2_loss(act, w_ff2, x2, g4, target):
    t, k = act.shape
    d = x2.shape[1]
    tt, tk = _tile(t, 1024), _tile(k, 512)
    nk = k // tk

    def body(a_ref, w_ref, x2_ref, g_ref, tg_ref, dy_ref, df_ref, dg_ref, loss_ref, acc):
        i, kk = pl.program_id(0), pl.program_id(1)

        @pl.when(kk == 0)
        def _():
            acc[...] = jnp.zeros_like(acc)

        @pl.when((i == 0) & (kk == 0))
        def _():
            dg_ref[...] = jnp.zeros_like(dg_ref)
            loss_ref[...] = jnp.zeros_like(loss_ref)

        acc[...] += _dot_nn(a_ref[...], w_ref[...])

        @pl.when(kk == nk - 1)
        def _():
            def tail(rows):
                fhat, r = _rms_hat(acc[rows, :])
                g = g_ref[...]
                e = x2_ref[rows, :] + fhat * g - tg_ref[rows, :]
                loss_ref[...] += 0.5 * jnp.sum(jnp.mean(e * e, axis=-1, keepdims=True))
                dy = e * (1.0 / d)
                dy_ref[rows, :] = dy
                df, dg = _rms_bwd(dy, fhat, r, g)
                df_ref[rows, :] = df.astype(BF16)
                dg_ref[...] += dg

            _row_chunks(tt, tail)

    row = pl.BlockSpec((tt, d), lambda i, kk: (i, 0))
    vec = pl.BlockSpec((1, d), lambda i, kk: (0, 0))
    return _call(
        body, name="ff2_loss", grid=(t // tt, nk),
        in_specs=[
            pl.BlockSpec((tt, tk), lambda i, kk: (i, kk)),
            pl.BlockSpec((tk, d), lambda i, kk: (kk, 0)),
            row, vec, row,
        ],
        out_specs=[row, row, vec, pl.BlockSpec((1, 128), lambda i, kk: (0, 0))],
        out_shape=[
            jax.ShapeDtypeStruct((t, d), F32),
            jax.ShapeDtypeStruct((t, d), BF16),
            jax.ShapeDtypeStruct((1, d), F32),
            jax.ShapeDtypeStruct((1, 128), F32),
        ],
        scratch_shapes=[pltpu.VMEM((tt, d), F32)],
        args=(act, w_ff2, x2, g4, target))[0]


def _ff2_bwd(df, w_ff2, rf):
    t, d = df.shape
    n = w_ff2.shape[0]
    tt, tn = _tile(t, 2048), _tile(n, 512)

    def body(df_ref, w_ref, rf_ref, out_ref):
        d_act = _dot_nt(df_ref[...], w_ref[...])
        out_ref[...] = (d_act * (2.0 * rf_ref[...].astype(F32))).astype(BF16)

    blk = pl.BlockSpec((tt, tn), lambda i, j: (i, j))
    return _call(
        body, name="ff2_bwd", grid=(t // tt, n // tn),
        in_specs=[pl.BlockSpec((tt, d), lambda i, j: (i, 0)), pl.BlockSpec((tn, d), lambda i, j: (j, 0)), blk],
        out_specs=[blk],
        out_shape=[jax.ShapeDtypeStruct((t, n), BF16)],
        args=(df, w_ff2, rf))[0][0]


def _wgrad(a, b, name, prev=None, row_off=0, rows=None, carry=None):
    t, m = a.shape
    n = b.shape[1]
    rows = m if rows is None else rows
    tm, tk = _tile(m, 512), _tile(t, 2048)
    nk = t // tk
    assert row_off % tm == 0
    off = row_off // tm

    def body(*refs):
        a_ref, b_ref = refs[0], refs[1]
        o32_ref, o16_ref, acc = refs[-3], refs[-2], refs[-1]
        kk = pl.program_id(1)

        @pl.when(kk == 0)
        def _():
            acc[...] = jnp.zeros_like(acc)

        acc[...] += _dot_tn(a_ref[...], b_ref[...])

        @pl.when(kk == nk - 1)
        def _():
            o32_ref[...] = acc[...]
            o16_ref[...] = acc[...].astype(BF16)

    in_specs = [pl.BlockSpec((tk, tm), lambda i, kk: (kk, i)), pl.BlockSpec((tk, n), lambda i, kk: (kk, 0))]
    args = [a, b]
    aliases = {}
    if prev is not None:
        in_specs += [ANY, ANY]
        args += list(prev)
        aliases = {2: 0, 3: 1}
    out = pl.BlockSpec((tm, n), lambda i, kk: (off + i, 0))
    return _call(
        body, name=name, grid=(m // tm, nk),
        in_specs=in_specs, out_specs=[out, out],
        out_shape=[jax.ShapeDtypeStruct((rows, n), F32), jax.ShapeDtypeStruct((rows, n), BF16)],
        scratch_shapes=[pltpu.VMEM((tm, n), F32)],
        aliases=aliases, args=args, carry=carry)


def _wgrad_parts(parts, b, name, carry=None):
    t, n = b.shape
    tm = 512
    bounds = []
    lo = 0
    for part in parts:
        assert part.shape[0] == t and part.shape[1] % tm == 0
        bounds.append((lo, lo + part.shape[1] // tm))
        lo += part.shape[1] // tm
    nm = lo
    np_ = len(parts)

    def body(*refs):
        p_refs, b_ref, o32_ref, o16_ref = refs[:np_], refs[np_], refs[np_ + 1], refs[np_ + 2]
        i = pl.program_id(0)
        for (lo_p, hi_p), p_ref in zip(bounds, p_refs):
            @pl.when((i >= lo_p) & (i < hi_p))
            def _(p_ref=p_ref):
                res = _dot_tn(p_ref[...], b_ref[...])
                o32_ref[...] = res
                o16_ref[...] = res.astype(BF16)

    def part_spec(lo_p, hi_p):
        return pl.BlockSpec((t, tm), lambda i: (0, jnp.clip(i - lo_p, 0, hi_p - lo_p - 1)))

    out = pl.BlockSpec((tm, n), lambda i: (i, 0))
    return _call(
        body, name=name, grid=(nm,),
        in_specs=[part_spec(lo_p, hi_p) for lo_p, hi_p in bounds] + [pl.BlockSpec((t, n), lambda i: (0, 0))],
        out_specs=[out, out],
        out_shape=[jax.ShapeDtypeStruct((nm * tm, n), F32), jax.ShapeDtypeStruct((nm * tm, n), BF16)],
        args=(*parts, b), carry=carry)


def _ff1_bwd_norms(d_f1, w_ff1t, dy, x2, g3, m, g2, carry=None):
    t, k = d_f1.shape
    d = x2.shape[1]
    tt, tk = _tile(t, 1024), _tile(k, 512)
    nk = k // tk

    def body(a_ref, w_ref, dy_ref, x2_ref, g3_ref, m_ref, g2_ref, dx2_ref, dm_ref, dg3_ref, dg2_ref, acc):
        i, kk = pl.program_id(0), pl.program_id(1)

        @pl.when(kk == 0)
        def _():
            acc[...] = jnp.zeros_like(acc)

        @pl.when((i == 0) & (kk == 0))
        def _():
            dg3_ref[...] = jnp.zeros_like(dg3_ref)
            dg2_ref[...] = jnp.zeros_like(dg2_ref)

        acc[...] += _dot_nn(a_ref[...], w_ref[...])

        @pl.when(kk == nk - 1)
        def _():
            def tail(rows):
                xhat, r3 = _rms_hat(x2_ref[rows, :])
                dx, dg3 = _rms_bwd(acc[rows, :], xhat, r3, g3_ref[...])
                dx2 = dy_ref[rows, :] + dx
                dx2_ref[rows, :] = dx2
                dg3_ref[...] += dg3
                mhat, r2 = _rms_hat(m_ref[rows, :])
                dm, dg2 = _rms_bwd(dx2, mhat, r2, g2_ref[...])
                dm_ref[rows, :] = dm.astype(BF16)
                dg2_ref[...] += dg2

            _row_chunks(tt, tail)

    row = pl.BlockSpec((tt, d), lambda i, kk: (i, 0))
    vec = pl.BlockSpec((1, d), lambda i, kk: (0, 0))
    return _call(
        body, name="ff1_bwd_norms", grid=(t // tt, nk),
        in_specs=[
            pl.BlockSpec((tt, tk), lambda i, kk: (i, kk)),
            pl.BlockSpec((tk, d), lambda i, kk: (kk, 0)),
            row, row, vec, row, vec,
        ],
        out_specs=[row, row, vec, vec],
        out_shape=[
            jax.ShapeDtypeStruct((t, d), F32),
            jax.ShapeDtypeStruct((t, d), BF16),
            jax.ShapeDtypeStruct((1, d), F32),
            jax.ShapeDtypeStruct((1, d), F32),
        ],
        scratch_shapes=[pltpu.VMEM((tt, d), F32)],
        args=(d_f1, w_ff1t, dy, x2, g3, m, g2), carry=carry)


def _wo_bwd_mix(dm, w_o, br_a, br_b, proj, b_gate, ga_block, gb_block, carry=None):
    t, d = dm.shape
    tt, tn = _tile(t, 1024), 512
    nj = d // tn

    def body(dm_ref, w_ref, bra_ref, brb_ref, ga_ref, gb_ref, ba_ref, bb_ref,
             dbra_ref, dbrb_ref, dga_ref, dgb_ref, dba_ref, dbb_ref):
        i = pl.program_id(1)

        @pl.when(i == 0)
        def _():
            dba_ref[...] = jnp.zeros_like(dba_ref)
            dbb_ref[...] = jnp.zeros_like(dbb_ref)

        d_mix = _dot_nt(dm_ref[...], w_ref[...])
        ga = _sig(ga_ref[...].astype(F32) + ba_ref[...])
        gb = _sig(gb_ref[...].astype(F32) + bb_ref[...])
        dbra_ref[...] = (d_mix * ga).astype(BF16)
        dbrb_ref[...] = (d_mix * gb).astype(BF16)
        dga = d_mix * bra_ref[...] * (ga * (1.0 - ga))
        dgb = d_mix * brb_ref[...] * (gb * (1.0 - gb))
        dga_ref[...] = dga.astype(BF16)
        dgb_ref[...] = dgb.astype(BF16)
        dba_ref[...] += jnp.sum(dga, axis=0, keepdims=True)
        dbb_ref[...] += jnp.sum(dgb, axis=0, keepdims=True)

    blk = pl.BlockSpec((tt, tn), lambda j, i: (i, j))
    vec = pl.BlockSpec((1, tn), lambda j, i: (0, j))
    return _call(
        body, name="wo_bwd_mix", grid=(nj, t // tt),
        in_specs=[
            pl.BlockSpec((tt, d), lambda j, i: (i, 0)),
            pl.BlockSpec((tn, d), lambda j, i: (j, 0)),
            blk, blk,
            pl.BlockSpec((tt, tn), lambda j, i: (i, ga_block + j)),
            pl.BlockSpec((tt, tn), lambda j, i: (i, gb_block + j)),
            vec,
            pl.BlockSpec((1, tn), lambda j, i: (0, nj + j)),
        ],
        out_specs=[blk, blk, blk, blk, vec, vec],
        out_shape=[jax.ShapeDtypeStruct((t, d), BF16)] * 4 + [jax.ShapeDtypeStruct((1, d), F32)] * 2,
        args=(dm, w_o, br_a, br_b, proj, proj, b_gate, b_gate), carry=carry)


def _lru_up_bwd(d_br_a, w_lru_up, proj, h, g_block, carry=None):
    t, d = d_br_a.shape
    tt, tn = _tile(t, 1024), 512

    def body(a_ref, w_ref, g_ref, h_ref, dh_ref, dg_ref):
        d_y = _dot_nt(a_ref[...], w_ref[...])
        gel, gel_grad = _gelu_and_grad(g_ref[...].astype(F32))
        dh_ref[...] = d_y * gel
        dg_ref[...] = (d_y * h_ref[...] * gel_grad).astype(BF16)

    blk = pl.BlockSpec((tt, tn), lambda i, j: (i, j))
    return _call(
        body, name="lru_up_bwd", grid=(t // tt, d // tn),
        in_specs=[
            pl.BlockSpec((tt, d), lambda i, j: (i, 0)),
            pl.BlockSpec((tn, d), lambda i, j: (j, 0)),
            pl.BlockSpec((tt, tn), lambda i, j: (i, g_block + j)),
            blk,
        ],
        out_specs=[blk, blk],
        out_shape=[jax.ShapeDtypeStruct((t, d), F32), jax.ShapeDtypeStruct((t, d), BF16)],
        args=(d_br_a, w_lru_up, proj, h), carry=carry)


def _pool_up_bwd(d_br_b, w_pool_upt):
    t, d = d_br_b.shape
    dp = w_pool_upt.shape[1]
    tt = _tile(t, 2048)

    def body(a_ref, w_ref, out_ref):
        out_ref[...] = _dot_nn(a_ref[...], w_ref[...])

    return _call(
        body, name="pool_up_bwd", grid=(t // tt,),
        in_specs=[pl.BlockSpec((tt, d), lambda i: (i, 0)), pl.BlockSpec((d, dp), lambda i: (0, 0))],
        out_specs=[pl.BlockSpec((tt, dp), lambda i: (i, 0))],
        out_shape=[jax.ShapeDtypeStruct((t, dp), F32)],
        args=(d_br_b, w_pool_upt))[0][0]


def _lru_bwd(dh, xc, h, proj, conv_w, w_a, b_a, w_x, b_x, lam, carry=None):
    t, dr = dh.shape
    cb = LRU_CB
    hd = LRU_HEAD_DIM
    per = cb // hd
    tc = _tile(t, 256)
    ncb, ntc = dr // cb, t // tc

    def body(dh_ref, xc_ref, h_ref, hp_ref, xp_ref, cw_ref, wa_ref, ba_ref, wx_ref, bx_ref, lam_ref,
             dxp_ref, dwa_ref, dba_ref, dwx_ref, dbx_ref, dlam_ref, dcw_ref, dcb_ref,
             nextd_s, anext_s, gnext_s, tmp_s, wa_s, wx_s):
        c = pl.program_id(1)
        rc = ntc - 1 - c

        @pl.when(c == 0)
        def _():
            nextd_s[...] = jnp.zeros_like(nextd_s)
            anext_s[...] = jnp.zeros_like(anext_s)
            gnext_s[...] = jnp.zeros_like(gnext_s)
            for ref in (dwa_ref, dba_ref, dwx_ref, dbx_ref, dlam_ref, dcw_ref, dcb_ref):
                ref[...] = jnp.zeros_like(ref)
            _fill_block_diag(wa_ref, wa_s)
            _fill_block_diag(wx_ref, wx_s)

        xc = xc_ref[...]
        wa, wx, lam = wa_s[...], wx_s[...], lam_ref[...]
        xcb, r, i, sp, log_a, a, mult = _lru_gates(xc, wa, ba_ref[...], wx, bx_ref[...], lam)
        row = lax.broadcasted_iota(jnp.int32, xc.shape, 0)
        h = h_ref[...]
        hp = jnp.where(rc == 0, 0.0, hp_ref[...])
        hprev = jnp.where(row >= 1, pltpu.roll(h, 1, 0), pltpu.roll(hp, 1, 0))

        def up(v, nv, j):
            return jnp.where(row < tc - j, pltpu.roll(v, tc - j, 0), nv)

        av = up(a, anext_s[...], 1)
        bv = dh_ref[...]
        s = 1
        while s < tc:
            a_sh = up(av, 1.0, s)
            b_sh = up(bv, 0.0, s)
            bv = av * b_sh + bv
            av = av * a_sh
            s *= 2
        gt = av * gnext_s[...] + bv
        tmp_s[...] = gt
        gnext_s[...] = tmp_s[0:1, :]
        tmp_s[...] = a
        anext_s[...] = tmp_s[0:1, :]

        da = gt * hprev
        ixc = i * xc
        d_mult = gt * ixc
        d_i = gt * mult * xc
        d_xc = gt * mult * i
        d_log_a = da * a - d_mult * (a * a) / mult
        d_pre_r = (d_log_a * ((-LRU_C) * sp)) * (r * (1.0 - r))
        d_pre_i = d_i * (i * (1.0 - i))
        d_sp = jnp.sum(d_log_a * ((-LRU_C) * r), axis=0, keepdims=True)
        dlam_ref[...] += d_sp * (-1.0 / (1.0 + jnp.exp(lam)))
        dpr = d_pre_r.astype(BF16)
        dpi = d_pre_i.astype(BF16)
        dba_ref[...] += jnp.sum(d_pre_r, axis=0, keepdims=True)
        dbx_ref[...] += jnp.sum(d_pre_i, axis=0, keepdims=True)
        pa = _dot_tn(xcb, dpr)
        px = _dot_tn(xcb, dpi)
        for k in range(per):
            dwa_ref[k] += pa[k * hd:(k + 1) * hd, k * hd:(k + 1) * hd]
            dwx_ref[k] += px[k * hd:(k + 1) * hd, k * hd:(k + 1) * hd]
        d_xc = d_xc + _dot_nt(dpr, wa) + _dot_nt(dpi, wx)

        nxt = nextd_s[...]
        xp = xp_ref[...].astype(F32)
        dxp = cw_ref[3:4, :] * d_xc
        dcw_ref[3:4, :] += jnp.sum(xp * d_xc, axis=0, keepdims=True)
        for j in (1, 2, 3):
            uj = up(d_xc, pltpu.roll(nxt, tc - j, 0), j)
            dxp = dxp + cw_ref[3 - j:4 - j, :] * uj
            dcw_ref[3 - j:4 - j, :] += jnp.sum(xp * uj, axis=0, keepdims=True)
        dcb_ref[...] += jnp.sum(d_xc, axis=0, keepdims=True)
        nextd_s[...] = d_xc
        dxp_ref[...] = dxp.astype(BF16)

    vec = pl.BlockSpec((1, cb), lambda j, c: (0, j))
    blk = pl.BlockSpec((tc, cb), lambda j, c: (ntc - 1 - c, j))
    mat = pl.BlockSpec((per, hd, hd), lambda j, c: (j, 0, 0))
    cwb = pl.BlockSpec((4, cb), lambda j, c: (0, j))
    return _call(
        body, name="lru_bwd", grid=(ncb, ntc),
        in_specs=[
            blk, blk, blk,
            pl.BlockSpec((tc, cb), lambda j, c: (jnp.maximum(ntc - 2 - c, 0), j)),
            blk, cwb, mat, vec, mat, vec, vec,
        ],
        out_specs=[blk, mat, vec, mat, vec, vec, cwb, vec],
        out_shape=[
            jax.ShapeDtypeStruct((t, dr), BF16),
            jax.ShapeDtypeStruct(w_a.shape, F32),
            jax.ShapeDtypeStruct((1, dr), F32),
            jax.ShapeDtypeStruct(w_x.shape, F32),
            jax.ShapeDtypeStruct((1, dr), F32),
            jax.ShapeDtypeStruct((1, dr), F32),
            jax.ShapeDtypeStruct((4, dr), F32),
            jax.ShapeDtypeStruct((1, dr), F32),
        ],
        scratch_shapes=[
            pltpu.VMEM((tc, cb), F32),
            pltpu.VMEM((1, cb), F32),
            pltpu.VMEM((1, cb), F32),
            pltpu.VMEM((tc, cb), F32),
            pltpu.VMEM((cb, cb), BF16),
            pltpu.VMEM((cb, cb), BF16),
        ],
        args=(dh, xc, h, h, proj, conv_w, w_a, b_a, w_x, b_x, lam), carry=carry)


def _pool_bwd(d_y_pool, p, pool_w, pool_scale):
    t, dp = d_y_pool.shape
    tc = _tile(t, 256)
    ntc = t // tc
    ng = len(POOL_WINDOWS)

    def body(dy_ref, p_ref, w_ref, sc_ref, dx_ref, dw_ref, dsc_ref, nz, n2, n4, n8, dp_s):
        c = pl.program_id(0)
        rc = ntc - 1 - c

        @pl.when(c == 0)
        def _():
            for s in (nz, n2, n4, n8):
                s[...] = jnp.zeros_like(s)
            dw_ref[...] = jnp.zeros_like(dw_ref)
            dsc_ref[...] = jnp.zeros_like(dsc_ref)

        for g in range(ng):
            sl = slice(g * POOL_GROUP_DIM, (g + 1) * POOL_GROUP_DIM)
            pg = p_ref[:, sl]
            dyg = dy_ref[:, sl]
            wg = w_ref[g].astype(BF16)
            q = _dot_nn(pg, wg)
            dsc_ref[:, sl] += jnp.sum(dyg * q, axis=0, keepdims=True)
            dpw = (dyg * sc_ref[:, sl]).astype(BF16)
            dw_ref[g] += _dot_tn(pg, dpw)
            dp_s[:, sl] = _dot_nt(dpw, wg)

        dpv = dp_s[...]
        row = lax.broadcasted_iota(jnp.int32, dpv.shape, 0)
        col = lax.broadcasted_iota(jnp.int32, dpv.shape, 1)
        win = _pool_select(col, POOL_WINDOWS)
        cnt = jnp.minimum(rc * tc + row + 1, win).astype(F32)
        z = dpv / cnt

        def up(v, nv, j):
            return jnp.where(row < tc - j, pltpu.roll(v, tc - j, 0), pltpu.roll(nv[...], tc - j, 0))

        u2 = z + up(z, nz, 1)
        u4 = u2 + up(u2, n2, 2)
        u8 = u4 + up(u4, n4, 4)
        u16 = u8 + up(u8, n8, 8)
        nz[...] = z
        n2[...] = u2
        n4[...] = u4
        n8[...] = u8
        dx_ref[...] = (_pool_select(col, (u2, u4, u8, u16)) - dpv).astype(BF16)

    blk = pl.BlockSpec((tc, dp), lambda c: (ntc - 1 - c, 0))
    full_w = pl.BlockSpec(pool_w.shape, lambda c: (0, 0, 0))
    vec = pl.BlockSpec((1, dp), lambda c: (0, 0))
    return _call(
        body, name="pool_bwd", grid=(ntc,),
        in_specs=[blk, blk, full_w, vec],
        out_specs=[blk, full_w, vec],
        out_shape=[
            jax.ShapeDtypeStruct((t, dp), BF16),
            jax.ShapeDtypeStruct(pool_w.shape, F32),
            jax.ShapeDtypeStruct((1, dp), F32),
        ],
        scratch_shapes=[pltpu.VMEM((tc, dp), F32)] * 5,
        args=(d_y_pool, p, pool_w, pool_scale))[0]


def _win_bwd_norm(parts, w_int, dx2, x, g1, carry=None):
    t, d = x.shape
    tk = 512
    tt = _tile(t, 1024)
    bounds = []
    k0 = 0
    for part in parts:
        assert part.shape[1] % tk == 0
        bounds.append((k0, k0 + part.shape[1] // tk))
        k0 += part.shape[1] // tk
    nk = k0
    assert nk * tk == w_int.shape[0]
    np_ = len(parts)

    def body(*refs):
        p_refs = refs[:np_]
        w_ref, dx2_ref, x_ref, g_ref, gx_ref, dg_ref, acc = refs[np_:]
        i, kk = pl.program_id(0), pl.program_id(1)

        @pl.when(kk == 0)
        def _():
            acc[...] = jnp.zeros_like(acc)

        @pl.when((i == 0) & (kk == 0))
        def _():
            dg_ref[...] = jnp.zeros_like(dg_ref)

        for (lo, hi), p_ref in zip(bounds, p_refs):
            @pl.when((kk >= lo) & (kk < hi))
            def _(p_ref=p_ref):
                acc[...] += _dot_nn(p_ref[...], w_ref[...])

        @pl.when(kk == nk - 1)
        def _():
            def tail(rows):
                xhat, r = _rms_hat(x_ref[rows, :])
                dx, dg = _rms_bwd(acc[rows, :], xhat, r, g_ref[...])
                gx_ref[rows, :] = dx2_ref[rows, :] + dx
                dg_ref[...] += dg

            _row_chunks(tt, tail)

    def part_spec(lo, hi):
        return pl.BlockSpec((tt, tk), lambda i, kk: (i, jnp.clip(kk - lo, 0, hi - lo - 1)))

    row = pl.BlockSpec((tt, d), lambda i, kk: (i, 0))
    vec = pl.BlockSpec((1, d), lambda i, kk: (0, 0))
    return _call(
        body, name="win_bwd_norm", grid=(t // tt, nk),
        in_specs=[part_spec(lo, hi) for lo, hi in bounds]
        + [pl.BlockSpec((tk, d), lambda i, kk: (kk, 0)), row, row, vec],
        out_specs=[row, vec],
        out_shape=[jax.ShapeDtypeStruct((t, d), F32), jax.ShapeDtypeStruct((1, d), F32)],
        scratch_shapes=[pltpu.VMEM((tt, d), F32)],
        args=(*parts, w_int, dx2, x, g1), carry=carry)


def _adam_math(w, g, m, v):
    m = ADAM_B1 * m + (1.0 - ADAM_B1) * g
    v = ADAM_B2 * v + (1.0 - ADAM_B2) * (g * g)
    m_hat = m / (1.0 - ADAM_B1 ** ADAM_STEP)
    v_hat = v / (1.0 - ADAM_B2 ** ADAM_STEP)
    delta = -ADAM_LR * (m_hat / (jnp.sqrt(v_hat) + ADAM_EPS) + ADAM_WD * w)
    return delta, m, v


def _adamw_big(ws, gs, ms, vs):
    n = len(ws)
    nb = 8

    def body(*refs):
        for a in range(n):
            w_ref, g_ref, m_ref, v_ref = refs[4 * a:4 * a + 4]
            d_ref, nm_ref, nv_ref = refs[4 * n + 3 * a:4 * n + 3 * a + 3]
            dl, m, v = _adam_math(w_ref[...], g_ref[...], m_ref[...], v_ref[...])
            d_ref[...] = dl
            nm_ref[...] = m
            nv_ref[...] = v

    in_specs, out_specs, out_shape, args = [], [], [], []
    for w, g, m, v in zip(ws, gs, ms, vs):
        rows, cols = w.shape
        blk = pl.BlockSpec((rows // nb, cols), lambda i: (i, 0))
        in_specs += [blk] * 4
        args += [w, g, m, v]
        out_specs += [blk] * 3
        out_shape += [jax.ShapeDtypeStruct(w.shape, F32)] * 3
    outs = _call(body, name="adamw_big", grid=(nb,), in_specs=in_specs, out_specs=out_specs,
                 out_shape=out_shape, args=args)[0]
    return [tuple(outs[3 * a:3 * a + 3]) for a in range(n)]


SMALL_ORDER = ("norm_mix_pre", "norm_mix_post", "norm_mlp_pre", "norm_mlp_post", "b_gate", "conv_w", "conv_b",
               "lru_w_a", "lru_b_a", "lru_w_x", "lru_b_x", "lru_lambda", "pool_w", "pool_scale")
VEC_ROW = dict(norm_mix_pre=0, norm_mix_post=1, norm_mlp_pre=2, norm_mlp_post=3, conv_b=6, lru_b_a=7,
               lru_b_x=8, lru_lambda=9)
ROW_B_GATE, ROW_POOL_SCALE, ROW_CONV_W, ROW_LOSS, N_VEC_ROWS = 4, 10, 11, 15, 16


def _adamw_small(vec_parts, g_pool, g_wa, g_wx, me, params):
    d = vec_parts.shape[2]
    names = SMALL_ORDER
    n = len(names)
    cw_cols = params["conv_w"][0].shape[2]

    def body(me_ref, vec_ref, vecc_ref, gp_ref, gwa_ref, gwx_ref, *refs):
        wmv = refs[:3 * n]
        loss_ref = refs[3 * n]
        outs = refs[3 * n + 1:3 * n + 1 + 4 * n]
        vs, vsc = refs[3 * n + 1 + 4 * n:]
        acc, accc = vec_ref[0], vecc_ref[0]
        for k in range(1, N_DEV):
            acc = acc + vec_ref[k]
            accc = accc + vecc_ref[k]
        vs[...] = acc
        vsc[...] = accc
        loss_ref[...] = vs[ROW_LOSS:ROW_LOSS + 1, 0:128]

        def upd(a, g, idx):
            w_ref, m_ref, v_ref = wmv[3 * a:3 * a + 3]
            g_ref, d_ref, nm_ref, nv_ref = outs[4 * a:4 * a + 4]
            dl, m, v = _adam_math(w_ref[idx], g, m_ref[idx], v_ref[idx])
            g_ref[idx] = g
            d_ref[idx] = dl
            nm_ref[idx] = m
            nv_ref[idx] = v

        for a, name in enumerate(names):
            if name in VEC_ROW:
                r = VEC_ROW[name]
                upd(a, vs[r:r + 1, :], (slice(None), slice(None)))
            elif name == "b_gate":
                for half in range(2):
                    r = ROW_B_GATE + half
                    upd(a, vs[r:r + 1, :], (slice(None), slice(half * d, (half + 1) * d)))
            elif name == "pool_scale":
                width = params[name][0].shape[1]
                upd(a, vs[ROW_POOL_SCALE:ROW_POOL_SCALE + 1, 0:width], (slice(None), slice(None)))
            elif name == "conv_w":
                upd(a, vsc[ROW_CONV_W:ROW_CONV_W + 4, :], (0,))
            elif name == "pool_w":
                upd(a, gp_ref[...], (Ellipsis,))
            elif name == "lru_w_a":
                upd(a, gwa_ref[...], (Ellipsis,))
            elif name == "lru_w_x":
                upd(a, gwx_ref[...], (Ellipsis,))
            else:
                raise ValueError(name)

    def whole(shape):
        nd = len(shape)
        return pl.BlockSpec(tuple(shape), lambda i, me_ref: (0,) * nd)

    in_specs = [
        whole(vec_parts.shape),
        pl.BlockSpec((N_DEV, N_VEC_ROWS, cw_cols), lambda i, me_ref: (0, 0, me_ref[0])),
        whole(g_pool.shape), whole(g_wa.shape), whole(g_wx.shape),
    ]
    args = [vec_parts, vec_parts, g_pool, g_wa, g_wx]
    out_specs = [whole((1, 128))]
    out_shape = [jax.ShapeDtypeStruct((1, 128), F32)]
    for name in names:
        for arr in params[name]:
            in_specs.append(whole(arr.shape))
            args.append(arr)
        shp = params[name][0].shape
        out_specs += [whole(shp)] * 4
        out_shape += [jax.ShapeDtypeStruct(shp, F32)] * 4
    grid_spec = pltpu.PrefetchScalarGridSpec(
        num_scalar_prefetch=1, grid=(1,), in_specs=in_specs, out_specs=out_specs,
        scratch_shapes=[pltpu.VMEM((N_VEC_ROWS, d), F32), pltpu.VMEM((N_VEC_ROWS, cw_cols), F32)])
    outs = pl.pallas_call(
        body, name="adamw_small", grid_spec=grid_spec, out_shape=out_shape,
        compiler_params=pltpu.CompilerParams(
            dimension_semantics=("arbitrary",), vmem_limit_bytes=V7X_VMEM_LIMIT_BYTES),
    )(me, *_in_hbm(args))
    return outs[0], {name: tuple(outs[1 + 4 * a:5 + 4 * a]) for a, name in enumerate(names)}


def _rs_sum(full, recv, shard_ids, slot_ids, name):
    r, rest = recv.shape[1], tuple(recv.shape[2:])
    zeros = (0,) * len(rest)
    send_dtype = recv.dtype

    def body(sh_ref, sl_ref, full_ref, recv_ref, own_ref, send_ref):
        s = pl.program_id(0)
        v = full_ref[...] + recv_ref[...].astype(F32)

        @pl.when(s == 0)
        def _():
            own_ref[...] = v

        @pl.when(s > 0)
        def _():
            send_ref[...] = v.astype(send_dtype)

    grid_spec = pltpu.PrefetchScalarGridSpec(
        num_scalar_prefetch=2,
        grid=(4,),
        in_specs=[
            pl.BlockSpec((r,) + rest, lambda s, sh, sl: (sh[s],) + zeros),
            pl.BlockSpec((None, r) + rest, lambda s, sh, sl: (sl[s], 0) + zeros),
        ],
        out_specs=[
            pl.BlockSpec((None, r) + rest, lambda s, sh, sl: (0, 0) + zeros),
            pl.BlockSpec((None, r) + rest, lambda s, sh, sl: (jnp.maximum(s - 1, 0), 0) + zeros),
        ],
    )
    return pl.pallas_call(
        body,
        name=name,
        grid_spec=grid_spec,
        out_shape=[jax.ShapeDtypeStruct((1, r) + rest, F32), jax.ShapeDtypeStruct((3, r) + rest, send_dtype)],
        compiler_params=pltpu.CompilerParams(
            dimension_semantics=("arbitrary",), vmem_limit_bytes=V7X_VMEM_LIMIT_BYTES),
    )(shard_ids, slot_ids, *_in_hbm([full, recv]))


def _finals(pairs):
    nb = 4
    n = len(pairs)

    def body(*refs):
        for a in range(n):
            own_ref, recv_ref = refs[2 * a], refs[2 * a + 1]
            acc = own_ref[...]
            for k in range(3):
                acc = acc + recv_ref[k].astype(F32)
            refs[2 * n + a][...] = acc

    in_specs, out_specs, out_shape, args = [], [], [], []
    for own, recv in pairs:
        _, rows, cols = own.shape
        in_specs += [pl.BlockSpec((None, rows // nb, cols), lambda i: (0, i, 0)),
                     pl.BlockSpec((3, rows // nb, cols), lambda i: (0, i, 0))]
        args += [own, recv]
        out_specs.append(pl.BlockSpec((rows // nb, cols), lambda i: (i, 0)))
        out_shape.append(jax.ShapeDtypeStruct((rows, cols), F32))
    return _call(body, name="rs_finals", grid=(nb,), in_specs=in_specs, out_specs=out_specs,
                 out_shape=out_shape, args=args)[0]


def _rs_level1(fulls_f32, fulls_send, tag):
    x, y, c = _place()
    recv1 = _run_plan(_rs_sibling_plan(fulls_send), "rs_sibling_" + tag)
    qs = jnp.stack([2 * x + y, 2 * (1 - x) + y, 2 * x + (1 - y), 2 * (1 - x) + (1 - y)]).astype(jnp.int32)
    shard_ids = 2 * qs + c
    return [_rs_sum(f32, r1, shard_ids, qs, f"rs_sum_{tag}{a}")
            for a, (f32, r1) in enumerate(zip(fulls_f32, recv1))]


def _rows(g):
    return g.reshape(g.shape[0] * g.shape[1], g.shape[2])


def kernel(x, norm_mix_pre, norm_mix_post, norm_mlp_pre, norm_mlp_post, w_in, b_gate, conv_w, conv_b, lru_w_a, lru_b_a, lru_w_x, lru_b_x, lru_lambda, pool_w, pool_scale, w_lru_up, w_pool_up, w_o, w_ff1, w_ff2, loss_target, m_norm_mix_pre, m_norm_mix_post, m_norm_mlp_pre, m_norm_mlp_post, m_w_in, m_b_gate, m_conv_w, m_conv_b, m_lru_w_a, m_lru_b_a, m_lru_w_x, m_lru_b_x, m_lru_lambda, m_pool_w, m_pool_scale, m_w_lru_up, m_w_pool_up, m_w_o, m_w_ff1, m_w_ff2, v_norm_mix_pre, v_norm_mix_post, v_norm_mlp_pre, v_norm_mlp_post, v_w_in, v_b_gate, v_conv_w, v_conv_b, v_lru_w_a, v_lru_b_a, v_lru_w_x, v_lru_b_x, v_lru_lambda, v_pool_w, v_pool_scale, v_w_lru_up, v_w_pool_up, v_w_o, v_w_ff1, v_w_ff2):
    t, d = x.shape[1], x.shape[2]
    d_rnn = conv_b.shape[1]
    d_pool = pool_scale.shape[1]
    per = LRU_CB // LRU_HEAD_DIM
    xi, yi, ci = _place()
    me = 4 * xi + 2 * yi + ci

    x2d = x[0]
    tgt = loss_target[0]

    s_in = w_in[0].T.astype(BF16)
    s_lu = w_lru_up[0].astype(BF16)
    s_pu = w_pool_up[0].T.astype(BF16)
    s_o = w_o[0].astype(BF16)
    s_f1 = w_ff1[0].T.astype(BF16)
    s_f2 = w_ff2[0].astype(BF16)
    s_cw = jnp.pad(conv_w[0], ((0, 4), (0, 0)))

    g_in, g_cw = _run_plan(_ag_plan([s_in, s_cw]), "ag_w_in")
    w_int = _rows(g_in)
    conv_w_full = jnp.transpose(g_cw[:, :4, :], (1, 0, 2)).reshape(4, d_rnn)

    wa_bd, wx_bd = lru_w_a[0], lru_w_x[0]
    pw = pool_w[0]
    pw_bf = pw.astype(BF16)

    pool_block = (2 * d_rnn) // d_pool
    ga_block = (2 * d_rnn + d_pool) // 512
    gb_block = ga_block + d // 512
    g_block = d_rnn // 512

    r_f1, r_f2 = s_f1.shape[0], s_f2.shape[0]
    f1_cut = r_f1 // 4
    f2_cut = (3 * r_f2) // 8
    plan = _join([_ag_plan([s_lu, s_pu, s_o]), _ag_plan([s_f1], pieces=[(0, f1_cut)])])
    (proj, h1), got = _norm_proj(x2d, norm_mix_pre, w_int, carry=plan)
    (g_lu, g_pu, g_o), (g_f1,) = plan.split(got)
    w_lu, w_put, w_og = _rows(g_lu), _rows(g_pu), _rows(g_o)
    (y_lru, h, xc), (g_f1,) = _lru_fwd(
        proj, conv_w_full, conv_b, wa_bd, lru_b_a, wx_bd, lru_b_x, lru_lambda,
        carry=_ag_plan([s_f1], pieces=[(f1_cut, r_f1 - f1_cut)], bufs=[g_f1]))
    w_f1t = _rows(g_f1)
    y_pool, p = _pool_fwd(proj, pw_bf, pool_scale, pool_block)
    (br_a, br_b, mix), (g_f2,) = _branch_mix(
        y_lru, y_pool, w_lu, w_put, proj, b_gate, ga_block, gb_block,
        carry=_ag_plan([s_f2], pieces=[(0, f2_cut)]))
    (m, x2, h3), _ = _wo_norm(mix, w_og, x2d, norm_mix_post, norm_mlp_pre)
    (rf, act), (g_f2,) = _ff1(
        h3, w_f1t, carry=_ag_plan([s_f2], pieces=[(f2_cut, r_f2 - f2_cut)], bufs=[g_f2]))
    w_f2 = _rows(g_f2)
    dy, df, dg4, loss_part = _ff2_loss(act, w_f2, x2, norm_mlp_post, tgt)

    d_f1 = _ff2_bwd(df, w_f2, rf)
    (gw_ff2_32, gw_ff2_16), _ = _wgrad(act, df, "wgrad_ff2")
    ((own_ff2, send_ff2),) = _rs_level1([gw_ff2_32], [gw_ff2_16], "ff2")
    half = send_ff2.shape[1] // 2
    (gw_ff1_32, gw_ff1_16), (r2_ff2,) = _wgrad(
        d_f1, h3, "wgrad_ff1", carry=_rs_chips_plan([send_ff2], pieces=[(0, half)]))
    ((own_ff1, send_ff1),) = _rs_level1([gw_ff1_32], [gw_ff1_16], "ff1")
    plan = _join([_rs_chips_plan([send_ff2], pieces=[(half, half)], bufs=[r2_ff2]),
                  _rs_chips_plan([send_ff1], pieces=[(0, half)])])
    (dx2, dm, dg3, dg2), got = _ff1_bwd_norms(d_f1, w_f1t, dy, x2, norm_mlp_pre, m, norm_mix_post, carry=plan)
    (r2_ff2,), (r2_ff1,) = plan.split(got)
    (gw_o_32, gw_o_16), _ = _wgrad(mix, dm, "wgrad_o")
    (d_br_a, d_br_b, p_ga, p_gb, dbg_a, dbg_b), (r2_ff1,) = _wo_bwd_mix(
        dm, w_og, br_a, br_b, proj, b_gate, ga_block, gb_block,
        carry=_rs_chips_plan([send_ff1], pieces=[(half, half)], bufs=[r2_ff1]))
    (gw_lu_32, gw_lu_16), _ = _wgrad(y_lru, d_br_a, "wgrad_lru_up")
    (gw_pu_32, gw_pu_16), _ = _wgrad(d_br_b, y_pool, "wgrad_pool_up")
    mid = _rs_level1([gw_o_32, gw_lu_32, gw_pu_32.reshape(-1, d)],
                     [gw_o_16, gw_lu_16, gw_pu_16.reshape(-1, d)], "mid")
    (dh, p_g), _ = _lru_up_bwd(d_br_a, w_lu, proj, h, g_block)
    d_y_pool = _pool_up_bwd(d_br_b, w_put)
    (p_x, dwa, db_a, dwx, db_x, dlam, dconv_w, dconv_b), r2_mid = _lru_bwd(
        dh, xc, h, proj, conv_w_full, wa_bd, lru_b_a, wx_bd, lru_b_x, lru_lambda,
        carry=_rs_chips_plan([s for _, s in mid]))
    p_p, dpool_w, dpool_scale = _pool_bwd(d_y_pool, p, pw, pool_scale)
    parts = [p_x, p_g, p_p, p_ga, p_gb]
    gw_in, _ = _wgrad_parts(parts, h1, "wgrad_in")
    tail = _rs_level1([gw_in[0], dpool_w.reshape(N_DEV, -1, POOL_GROUP_DIM), dwa, dwx],
                      [gw_in[1], dpool_w.reshape(N_DEV, -1, POOL_GROUP_DIM), dwa, dwx], "in")
    (grad_x, dg1), r2_tail = _win_bwd_norm(parts, w_int, dx2, x2d, norm_mix_pre,
                                           carry=_rs_chips_plan([s for _, s in tail]))

    def flat2(a):
        return a.reshape(a.shape[0], -1, a.shape[-1])

    fin = _finals([
        (tail[0][0], r2_tail[0]), (mid[1][0], r2_mid[1]), (mid[2][0], r2_mid[2]), (mid[0][0], r2_mid[0]),
        (own_ff1, r2_ff1), (own_ff2, r2_ff2),
        (flat2(tail[1][0]), flat2(r2_tail[1])), (flat2(tail[2][0]), flat2(r2_tail[2])),
        (flat2(tail[3][0]), flat2(r2_tail[3])),
    ])
    g_w_in = fin[0].T
    g_w_lru_up = fin[1]
    g_w_pool_up = fin[2].reshape(d // N_DEV, d_pool).T
    g_w_o = fin[3]
    g_w_ff1 = fin[4].T
    g_w_ff2 = fin[5]

    def pad_row(a):
        return jnp.pad(a, ((0, 0), (0, d - a.shape[1])))

    vecs = jnp.concatenate([dg1, dg2, dg3, dg4, dbg_a, dbg_b, dconv_b, db_a, db_x, dlam,
                            pad_row(dpool_scale), dconv_w, pad_row(loss_part)], axis=0)
    assert vecs.shape[0] == N_VEC_ROWS
    vec_parts, g_pool, g_wa, g_wx = _run_plan(_ag_plan([vecs, fin[6], fin[7], fin[8]]), "ag_tail")

    small = dict(
        norm_mix_pre=(norm_mix_pre, m_norm_mix_pre, v_norm_mix_pre),
        norm_mix_post=(norm_mix_post, m_norm_mix_post, v_norm_mix_post),
        norm_mlp_pre=(norm_mlp_pre, m_norm_mlp_pre, v_norm_mlp_pre),
        norm_mlp_post=(norm_mlp_post, m_norm_mlp_post, v_norm_mlp_post),
        b_gate=(b_gate, m_b_gate, v_b_gate), conv_w=(conv_w, m_conv_w, v_conv_w),
        conv_b=(conv_b, m_conv_b, v_conv_b), lru_w_a=(lru_w_a, m_lru_w_a, v_lru_w_a),
        lru_b_a=(lru_b_a, m_lru_b_a, v_lru_b_a), lru_w_x=(lru_w_x, m_lru_w_x, v_lru_w_x),
        lru_b_x=(lru_b_x, m_lru_b_x, v_lru_b_x), lru_lambda=(lru_lambda, m_lru_lambda, v_lru_lambda),
        pool_w=(pool_w, m_pool_w, v_pool_w), pool_scale=(pool_scale, m_pool_scale, v_pool_scale))
    loss_row, small_out = _adamw_small(
        vec_parts, g_pool.reshape(pool_w.shape), g_wa.reshape(lru_w_a.shape), g_wx.reshape(lru_w_x.shape),
        jnp.reshape(me, (1,)).astype(jnp.int32), small)
    grads = {n: o[0] for n, o in small_out.items()}
    delta = {n: o[1] for n, o in small_out.items()}
    new_m = {n: o[2] for n, o in small_out.items()}
    new_v = {n: o[3] for n, o in small_out.items()}

    big_names = ["w_in", "w_lru_up", "w_pool_up", "w_o", "w_ff1", "w_ff2"]
    big_w = [w_in, w_lru_up, w_pool_up, w_o, w_ff1, w_ff2]
    big_g = [g_w_in, g_w_lru_up, g_w_pool_up, g_w_o, g_w_ff1, g_w_ff2]
    big_m = [m_w_in, m_w_lru_up, m_w_pool_up, m_w_o, m_w_ff1, m_w_ff2]
    big_v = [v_w_in, v_w_lru_up, v_w_pool_up, v_w_o, v_w_ff1, v_w_ff2]
    big_out = _adamw_big([w[0] for w in big_w], big_g, [mm[0] for mm in big_m], [vv[0] for vv in big_v])
    for name, g, (dl, nm, nv) in zip(big_names, big_g, big_out):
        grads[name], delta[name], new_m[name], new_v[name] = g[None], dl[None], nm[None], nv[None]

    loss = loss_row[0, 0]
    order = ["norm_mix_pre", "norm_mix_post", "norm_mlp_pre", "norm_mlp_post", "w_in", "b_gate", "conv_w",
             "conv_b", "lru_w_a", "lru_b_a", "lru_w_x", "lru_b_x", "lru_lambda", "pool_w", "pool_scale",
             "w_lru_up", "w_pool_up", "w_o", "w_ff1", "w_ff2"]
    return (loss, grad_x[None], *[grads[n] for n in order], *[delta[n] for n in order],
            *[new_m[n] for n in order], *[new_v[n] for n in order])
```

```python
import functools
import math
import operator
import types

import jax
import jax.numpy as jnp
from jax import lax
from jax.experimental import pallas as pl
from jax.experimental.pallas import tpu as pltpu

F32 = jnp.float32
BF16 = jnp.bfloat16
NORM_EPS = 1e-6
LRU_C = 8.0
N_LRU_HEADS = 16
LRU_HEAD_DIM = 64
POOL_WINDOWS = (2, 4, 8, 16)
POOL_GROUP_DIM = 128
ADAM_LR = 0.001
ADAM_B1 = 0.9
ADAM_B2 = 0.999
ADAM_EPS = 1e-08
ADAM_WD = 0.01
ADAM_STEP = 10
N_DEV = 8
V7X_VMEM_LIMIT_BYTES = 56 * 1024 * 1024
LRU_CB = 256
MESH = pl.DeviceIdType.MESH
ANY = pl.BlockSpec(memory_space=pl.ANY)


def _tile(n, pref):
    t = min(n, pref)
    assert n % t == 0, (n, pref)
    return t


def _dot_nn(a, b):
    return lax.dot_general(a, b, (((1,), (0,)), ((), ())), preferred_element_type=F32)


def _dot_nt(a, b):
    return lax.dot_general(a, b, (((1,), (1,)), ((), ())), preferred_element_type=F32)


def _dot_tn(a, b):
    return lax.dot_general(a, b, (((0,), (0,)), ((), ())), preferred_element_type=F32)


def _row_chunks(n_rows, fn, chunk=256):
    chunk = min(chunk, n_rows)
    assert n_rows % chunk == 0

    def step(r, carry):
        fn(pl.ds(pl.multiple_of(r * chunk, chunk), chunk))
        return carry

    lax.fori_loop(0, n_rows // chunk, step, 0)


def _sig(x):
    return 1.0 / (1.0 + jnp.exp(-x))


def _rms_hat(x):
    r = lax.rsqrt(jnp.mean(x * x, axis=-1, keepdims=True) + NORM_EPS)
    return x * r, r


def _rms_bwd(dn, xhat, r, g):
    q = dn * g
    dx = r * (q - xhat * jnp.mean(q * xhat, axis=-1, keepdims=True))
    dg = jnp.sum(dn * xhat, axis=0, keepdims=True)
    return dx, dg


_GELU_K = math.sqrt(2.0 / math.pi)
_GELU_C = 0.044715


def _gelu_and_grad(g):
    t = jnp.tanh(_GELU_K * (g + _GELU_C * g * g * g))
    val = 0.5 * g * (1.0 + t)
    grad = 0.5 * (1.0 + t) + 0.5 * g * (1.0 - t * t) * (_GELU_K * (1.0 + 3.0 * _GELU_C * g * g))
    return val, grad


def _softplus_neg(lam):
    z = -lam
    e = jnp.exp(-jnp.abs(z))
    u = 1.0 + e
    d = u - 1.0
    l1p = jnp.where(d == 0.0, e, jnp.log(u) * (e / jnp.where(d == 0.0, 1.0, d)))
    return jnp.maximum(z, 0.0) + l1p


def _lru_gates(xc, wa, ba, wx, bx, lam):
    xcb = xc.astype(BF16)
    r = _sig(_dot_nn(xcb, wa) + ba)
    i = _sig(_dot_nn(xcb, wx) + bx)
    sp = _softplus_neg(lam)
    log_a = (-LRU_C) * r * sp
    a = jnp.exp(log_a)
    mult = jnp.sqrt(-jnp.tanh(log_a) * (1.0 + a * a))
    return xcb, r, i, sp, log_a, a, mult


def _place():
    return lax.axis_index("x"), lax.axis_index("y"), lax.axis_index("c")


def _ag_plan(shards, pieces=None, bufs=None):
    na = len(shards)

    def parts(ins, outs, sems):
        send_sems, recv_sems, local_sems = sems
        x, y, c = _place()
        me, sibling = (x, y, c), (x, y, 1 - c)
        chips = [(1 - x, y), (x, 1 - y), (1 - x, 1 - y)]

        def own(a):
            return ins[a] if pieces is None else ins[a].at[pl.ds(*pieces[a])]

        def slot(a, px, py, pc):
            idx = 4 * px + 2 * py + pc
            return outs[a].at[idx] if pieces is None else outs[a].at[idx, pl.ds(*pieces[a])]

        def copy(a, k, block, to, src=None):
            return pltpu.make_async_remote_copy(
                src_ref=slot(a, *block) if src is None else src,
                dst_ref=slot(a, *block),
                send_sem=send_sems.at[a * 7 + k],
                recv_sem=recv_sems.at[a * 7 + k],
                device_id=to,
                device_id_type=MESH,
            )

        mine = [pltpu.make_async_copy(own(a), slot(a, *me), local_sems.at[a]) for a in range(na)]
        first = []
        for a in range(na):
            first.append(copy(a, 0, me, sibling, src=own(a)))
            first += [copy(a, 1 + j, me, (*chip, c), src=own(a)) for j, chip in enumerate(chips)]
        return me, sibling, chips, c, copy, mine, first

    def start(ins, outs, sems):
        _, _, _, _, _, mine, first = parts(ins, outs, sems)
        for cp in mine + first:
            cp.start()

    def finish(ins, outs, sems):
        me, sibling, chips, c, copy, mine, first = parts(ins, outs, sems)
        passed = []
        for j, chip in enumerate(chips):
            for a in range(na):
                copy(a, 1 + j, (*chip, c), me).wait_recv()
                fwd = copy(a, 4 + j, (*chip, c), sibling)
                fwd.start()
                passed.append(fwd)
        for a in range(na):
            copy(a, 0, sibling, me).wait_recv()
            for j, chip in enumerate(chips):
                copy(a, 4 + j, (*chip, 1 - c), me).wait_recv()
        for cp in first + passed:
            cp.wait_send()
        for cp in mine:
            cp.wait()

    return types.SimpleNamespace(
        ins=list(shards) + list(bufs or []),
        out_shapes=[jax.ShapeDtypeStruct((N_DEV,) + s.shape, s.dtype) for s in shards],
        sems=[pltpu.SemaphoreType.DMA((7 * na,)), pltpu.SemaphoreType.DMA((7 * na,)),
              pltpu.SemaphoreType.DMA((na,))],
        aliases=[(na + a, a) for a in range(na)] if bufs else [],
        start=start, finish=finish)


def _rs_sibling_plan(fulls):
    na = len(fulls)
    rs = [f.shape[0] // N_DEV for f in fulls]

    def copies(ins, outs, sems):
        send_sems, recv_sems = sems
        x, y, c = _place()
        out = []
        for a in range(na):
            for q in range(4):
                shard = 2 * q + (1 - c)
                out.append(pltpu.make_async_remote_copy(
                    src_ref=ins[a].at[pl.ds(shard * rs[a], rs[a])],
                    dst_ref=outs[a].at[q],
                    send_sem=send_sems.at[a * 4 + q],
                    recv_sem=recv_sems.at[a * 4 + q],
                    device_id=(x, y, 1 - c),
                    device_id_type=MESH,
                ))
        return out

    def start(ins, outs, sems):
        for cp in copies(ins, outs, sems):
            cp.start()

    def finish(ins, outs, sems):
        for cp in copies(ins, outs, sems):
            cp.wait()

    return types.SimpleNamespace(
        ins=list(fulls),
        out_shapes=[jax.ShapeDtypeStruct((4, r) + f.shape[1:], f.dtype) for r, f in zip(rs, fulls)],
        sems=[pltpu.SemaphoreType.DMA((4 * na,)), pltpu.SemaphoreType.DMA((4 * na,))],
        start=start, finish=finish)


def _rs_chips_plan(sends, pieces=None, bufs=None):
    na = len(sends)

    def copies(ins, outs, sems):
        send_sems, recv_sems = sems
        x, y, c = _place()
        chips = [(1 - x, y), (x, 1 - y), (1 - x, 1 - y)]
        out = []
        for a in range(na):
            for k, chip in enumerate(chips):
                rows = (k,) if pieces is None else (k, pl.ds(*pieces[a]))
                out.append(pltpu.make_async_remote_copy(
                    src_ref=ins[a].at[rows],
                    dst_ref=outs[a].at[rows],
                    send_sem=send_sems.at[a * 3 + k],
                    recv_sem=recv_sems.at[a * 3 + k],
                    device_id=(*chip, c),
                    device_id_type=MESH,
                ))
        return out

    def start(ins, outs, sems):
        for cp in copies(ins, outs, sems):
            cp.start()

    def finish(ins, outs, sems):
        for cp in copies(ins, outs, sems):
            cp.wait()

    return types.SimpleNamespace(
        ins=list(sends) + list(bufs or []),
        out_shapes=[jax.ShapeDtypeStruct(s.shape, s.dtype) for s in sends],
        sems=[pltpu.SemaphoreType.DMA((3 * na,)), pltpu.SemaphoreType.DMA((3 * na,))],
        aliases=[(na + a, a) for a in range(na)] if bufs else [],
        start=start, finish=finish)


def _join(plans):
    ins, outs, sems, aliases, offs = [], [], [], [], []
    for p in plans:
        offs.append((len(ins), len(outs), len(sems)))
        aliases += [(len(ins) + ci, len(outs) + co) for ci, co in getattr(p, "aliases", [])]
        ins += p.ins
        outs += p.out_shapes
        sems += p.sems

    def cut(p, off, i, o, s):
        return (i[off[0]:off[0] + len(p.ins)], o[off[1]:off[1] + len(p.out_shapes)],
                s[off[2]:off[2] + len(p.sems)])

    def start(i, o, s):
        for p, off in zip(plans, offs):
            p.start(*cut(p, off, i, o, s))

    def finish(i, o, s):
        for p, off in zip(plans, offs):
            p.finish(*cut(p, off, i, o, s))

    def split(results):
        return [list(results[off[1]:off[1] + len(p.out_shapes)]) for p, off in zip(plans, offs)]

    return types.SimpleNamespace(ins=ins, out_shapes=outs, sems=sems, aliases=aliases,
                                 start=start, finish=finish, split=split)


def _in_hbm(args):
    return [pltpu.with_memory_space_constraint(a, pltpu.HBM) for a in args]


def _run_plan(plan, name):
    n_in, n_out = len(plan.ins), len(plan.out_shapes)

    def body(*refs):
        ins, outs, sems = refs[:n_in], refs[n_in:n_in + n_out], refs[n_in + n_out:]
        plan.start(ins, outs, sems)
        plan.finish(ins, outs, sems)

    return pl.pallas_call(
        body,
        name=name,
        in_specs=[ANY] * n_in,
        out_specs=[ANY] * n_out,
        out_shape=plan.out_shapes,
        scratch_shapes=plan.sems,
        input_output_aliases=dict(getattr(plan, "aliases", [])),
    )(*_in_hbm(plan.ins))


def _call(body, *, name, grid, in_specs, out_specs, out_shape, args, scratch_shapes=(), aliases=None,
          carry=None):
    n_in, n_out, n_scr = len(in_specs), len(out_shape), len(scratch_shapes)
    params = pltpu.CompilerParams(
        dimension_semantics=("arbitrary",) * len(grid), vmem_limit_bytes=V7X_VMEM_LIMIT_BYTES)
    if carry is None:
        outs = pl.pallas_call(
            body, name=name, grid=grid, in_specs=list(in_specs), out_specs=list(out_specs),
            out_shape=list(out_shape), scratch_shapes=list(scratch_shapes),
            input_output_aliases=aliases or {}, compiler_params=params)(*_in_hbm(args))
        return list(outs), []
    c_in, c_out = len(carry.ins), len(carry.out_shapes)

    def full(*refs):
        p = 0
        ins = refs[p:p + n_in]
        p += n_in
        cins = refs[p:p + c_in]
        p += c_in
        outs = refs[p:p + n_out]
        p += n_out
        couts = refs[p:p + c_out]
        p += c_out
        scr = refs[p:p + n_scr]
        csems = refs[p + n_scr:]
        ids = [pl.program_id(a) for a in range(len(grid))]
        first = functools.reduce(operator.and_, [i == 0 for i in ids])
        last = functools.reduce(operator.and_, [i == g - 1 for i, g in zip(ids, grid)])

        @pl.when(first)
        def _():
            carry.start(cins, couts, csems)

        body(*ins, *outs, *scr)

        @pl.when(last)
        def _():
            carry.finish(cins, couts, csems)

    all_aliases = dict(aliases or {})
    all_aliases.update({n_in + ci: n_out + co for ci, co in getattr(carry, "aliases", [])})
    outs = pl.pallas_call(
        full, name=name, grid=grid,
        in_specs=list(in_specs) + [ANY] * c_in,
        out_specs=list(out_specs) + [ANY] * c_out,
        out_shape=list(out_shape) + list(carry.out_shapes),
        scratch_shapes=list(scratch_shapes) + list(carry.sems),
        input_output_aliases=all_aliases, compiler_params=params)(*_in_hbm(args), *_in_hbm(carry.ins))
    return list(outs[:n_out]), list(outs[n_out:])


def _norm_proj(x, g1, w_int, carry=None):
    t, d = x.shape
    n = w_int.shape[0]
    tt, tn = _tile(t, 2048), _tile(n, 512)

    def body(x_ref, g_ref, w_ref, proj_ref, h1_ref, h1_s):
        @pl.when(pl.program_id(1) == 0)
        def _():
            def norm_rows(rows):
                xhat, _ = _rms_hat(x_ref[rows, :])
                h = (xhat * g_ref[...]).astype(BF16)
                h1_s[rows, :] = h
                h1_ref[rows, :] = h

            _row_chunks(tt, norm_rows)

        proj_ref[...] = _dot_nt(h1_s[...], w_ref[...]).astype(BF16)

    return _call(
        body, name="norm_proj", grid=(t // tt, n // tn),
        in_specs=[
            pl.BlockSpec((tt, d), lambda i, j: (i, 0)),
            pl.BlockSpec((1, d), lambda i, j: (0, 0)),
            pl.BlockSpec((tn, d), lambda i, j: (j, 0)),
        ],
        out_specs=[
            pl.BlockSpec((tt, tn), lambda i, j: (i, j)),
            pl.BlockSpec((tt, d), lambda i, j: (i, 0)),
        ],
        out_shape=[jax.ShapeDtypeStruct((t, n), BF16), jax.ShapeDtypeStruct((t, d), BF16)],
        scratch_shapes=[pltpu.VMEM((tt, d), BF16)],
        args=(x, g1, w_int), carry=carry)


def _fill_block_diag(w_ref, bd_ref):
    bd_ref[...] = jnp.zeros_like(bd_ref)
    hd = LRU_HEAD_DIM
    for k in range(w_ref.shape[0]):
        bd_ref[k * hd:(k + 1) * hd, k * hd:(k + 1) * hd] = w_ref[k].astype(BF16)


def _lru_fwd(proj, conv_w, conv_b, w_a, b_a, w_x, b_x, lam, carry=None):
    t = proj.shape[0]
    dr = conv_b.shape[1]
    cb = LRU_CB
    tc = _tile(t, 256)
    ncb, ntc = dr // cb, t // tc

    def body(xp_ref, g_ref, cw_ref, cb_ref, wa_ref, ba_ref, wx_ref, bx_ref, lam_ref,
             y_ref, h_ref, xc_ref, prevx_s, hlast_s, wa_s, wx_s):
        c = pl.program_id(1)

        @pl.when(c == 0)
        def _():
            prevx_s[...] = jnp.zeros_like(prevx_s)
            hlast_s[...] = jnp.zeros_like(hlast_s)
            _fill_block_diag(wa_ref, wa_s)
            _fill_block_diag(wx_ref, wx_s)

        x = xp_ref[...].astype(F32)
        prev = prevx_s[...]
        row = lax.broadcasted_iota(jnp.int32, x.shape, 0)

        def sh(j):
            return jnp.where(row >= j, pltpu.roll(x, j, 0), pltpu.roll(prev, j, 0))

        xc = (cb_ref[...] + cw_ref[0:1, :] * sh(3) + cw_ref[1:2, :] * sh(2)
              + cw_ref[2:3, :] * sh(1) + cw_ref[3:4, :] * x)
        prevx_s[...] = x
        xc_ref[...] = xc
        _, _, i, _, _, a, mult = _lru_gates(xc, wa_s[...], ba_ref[...], wx_s[...], bx_ref[...],
                                            lam_ref[...])
        av, bv = a, mult * (i * xc)
        s = 1
        while s < tc:
            a_sh = jnp.where(row >= s, pltpu.roll(av, s, 0), 1.0)
            b_sh = jnp.where(row >= s, pltpu.roll(bv, s, 0), 0.0)
            bv = av * b_sh + bv
            av = av * a_sh
            s *= 2
        h = av * hlast_s[...] + bv
        h_ref[...] = h
        hlast_s[...] = h_ref[tc - 1:tc, :]
        gel, _ = _gelu_and_grad(g_ref[...].astype(F32))
        y_ref[...] = (h * gel).astype(BF16)

    vec = pl.BlockSpec((1, cb), lambda j, c: (0, j))
    blk = pl.BlockSpec((tc, cb), lambda j, c: (c, j))
    mat = pl.BlockSpec((cb // LRU_HEAD_DIM, LRU_HEAD_DIM, LRU_HEAD_DIM), lambda j, c: (j, 0, 0))
    return _call(
        body, name="lru_fwd", grid=(ncb, ntc),
        in_specs=[
            blk,
            pl.BlockSpec((tc, cb), lambda j, c: (c, ncb + j)),
            pl.BlockSpec((4, cb), lambda j, c: (0, j)),
            vec, mat, vec, mat, vec, vec,
        ],
        out_specs=[blk, blk, blk],
        out_shape=[
            jax.ShapeDtypeStruct((t, dr), BF16),
            jax.ShapeDtypeStruct((t, dr), F32),
            jax.ShapeDtypeStruct((t, dr), F32),
        ],
        scratch_shapes=[pltpu.VMEM((tc, cb), F32), pltpu.VMEM((1, cb), F32),
                        pltpu.VMEM((cb, cb), BF16), pltpu.VMEM((cb, cb), BF16)],
        args=(proj, proj, conv_w, conv_b, w_a, b_a, w_x, b_x, lam), carry=carry)


def _pool_select(col, vals):
    out = vals[3]
    for g in (2, 1, 0):
        out = jnp.where(col < (g + 1) * POOL_GROUP_DIM, vals[g], out)
    return out


def _pool_fwd(proj, pool_w, pool_scale, col_block):
    t = proj.shape[0]
    dp = pool_scale.shape[1]
    tc = _tile(t, 256)
    ntc = t // tc

    def body(x_ref, w_ref, sc_ref, y_ref, p_ref, px, p2, p4, p8):
        c = pl.program_id(0)

        @pl.when(c == 0)
        def _():
            for s in (px, p2, p4, p8):
                s[...] = jnp.zeros_like(s)

        x = x_ref[...].astype(F32)
        row = lax.broadcasted_iota(jnp.int32, x.shape, 0)
        col = lax.broadcasted_iota(jnp.int32, x.shape, 1)

        def sh(v, pv, j):
            return jnp.where(row >= j, pltpu.roll(v, j, 0), pltpu.roll(pv[...], j, 0))

        s2 = x + sh(x, px, 1)
        s4 = s2 + sh(s2, p2, 2)
        s8 = s4 + sh(s4, p4, 4)
        s16 = s8 + sh(s8, p8, 8)
        px[...] = x
        p2[...] = s2
        p4[...] = s4
        p8[...] = s8
        wsum = _pool_select(col, (s2, s4, s8, s16))
        win = _pool_select(col, POOL_WINDOWS)
        cnt = jnp.minimum(c * tc + row + 1, win).astype(F32)
        p = wsum / cnt - x
        pb = p.astype(BF16)
        p_ref[...] = pb
        for g in range(len(POOL_WINDOWS)):
            sl = slice(g * POOL_GROUP_DIM, (g + 1) * POOL_GROUP_DIM)
            yg = _dot_nn(pb[:, sl], w_ref[g]) * sc_ref[:, sl]
            y_ref[:, sl] = yg.astype(BF16)

    return _call(
        body, name="pool_fwd", grid=(ntc,),
        in_specs=[
            pl.BlockSpec((tc, dp), lambda c: (c, col_block)),
            pl.BlockSpec(pool_w.shape, lambda c: (0, 0, 0)),
            pl.BlockSpec((1, dp), lambda c: (0, 0)),
        ],
        out_specs=[pl.BlockSpec((tc, dp), lambda c: (c, 0))] * 2,
        out_shape=[jax.ShapeDtypeStruct((t, dp), BF16)] * 2,
        scratch_shapes=[pltpu.VMEM((tc, dp), F32)] * 4,
        args=(proj, pool_w, pool_scale))[0]


def _branch_mix(y_lru, y_pool, w_lru_up, w_pool_upt, proj, b_gate, ga_block, gb_block, carry=None):
    t, d = y_lru.shape
    dp = y_pool.shape[1]
    tt, tn = _tile(t, 1024), 512
    nj = d // tn

    def body(yl_ref, yp_ref, wl_ref, wp_ref, ga_ref, gb_ref, ba_ref, bb_ref, bra_ref, brb_ref, mix_ref):
        br_a = _dot_nn(yl_ref[...], wl_ref[...])
        br_b = _dot_nt(yp_ref[...], wp_ref[...])
        bra_ref[...] = br_a
        brb_ref[...] = br_b
        ga = _sig(ga_ref[...].astype(F32) + ba_ref[...])
        gb = _sig(gb_ref[...].astype(F32) + bb_ref[...])
        mix_ref[...] = (ga * br_a + gb * br_b).astype(BF16)

    out = pl.BlockSpec((tt, tn), lambda j, i: (i, j))
    return _call(
        body, name="branch_mix", grid=(nj, t // tt),
        in_specs=[
            pl.BlockSpec((tt, d), lambda j, i: (i, 0)),
            pl.BlockSpec((tt, dp), lambda j, i: (i, 0)),
            pl.BlockSpec((d, tn), lambda j, i: (0, j)),
            pl.BlockSpec((tn, dp), lambda j, i: (j, 0)),
            pl.BlockSpec((tt, tn), lambda j, i: (i, ga_block + j)),
            pl.BlockSpec((tt, tn), lambda j, i: (i, gb_block + j)),
            pl.BlockSpec((1, tn), lambda j, i: (0, j)),
            pl.BlockSpec((1, tn), lambda j, i: (0, nj + j)),
        ],
        out_specs=[out, out, out],
        out_shape=[
            jax.ShapeDtypeStruct((t, d), F32),
            jax.ShapeDtypeStruct((t, d), F32),
            jax.ShapeDtypeStruct((t, d), BF16),
        ],
        args=(y_lru, y_pool, w_lru_up, w_pool_upt, proj, proj, b_gate, b_gate), carry=carry)


def _wo_norm(mix, w_o, x, g2, g3, carry=None):
    t, d = x.shape
    tt = _tile(t, 512)

    def body(mix_ref, w_ref, x_ref, g2_ref, g3_ref, m_ref, x2_ref, h3_ref):
        m = _dot_nn(mix_ref[...], w_ref[...])
        m_ref[...] = m
        mhat, _ = _rms_hat(m)
        x2 = x_ref[...] + mhat * g2_ref[...]
        x2_ref[...] = x2
        xhat, _ = _rms_hat(x2)
        h3_ref[...] = (xhat * g3_ref[...]).astype(BF16)

    row = pl.BlockSpec((tt, d), lambda i: (i, 0))
    vec = pl.BlockSpec((1, d), lambda i: (0, 0))
    return _call(
        body, name="wo_norm", grid=(t // tt,),
        in_specs=[row, pl.BlockSpec((d, d), lambda i: (0, 0)), row, vec, vec],
        out_specs=[row, row, row],
        out_shape=[
            jax.ShapeDtypeStruct((t, d), F32),
            jax.ShapeDtypeStruct((t, d), F32),
            jax.ShapeDtypeStruct((t, d), BF16),
        ],
        args=(mix, w_o, x, g2, g3), carry=carry)


def _ff1(h3, w_ff1t, carry=None):
    t, d = h3.shape
    n = w_ff1t.shape[0]
    tt, tn = _tile(t, 2048), _tile(n, 512)

    def body(h_ref, w_ref, rf_ref, act_ref):
        rf = jnp.maximum(_dot_nt(h_ref[...], w_ref[...]), 0.0)
        rf_ref[...] = rf.astype(BF16)
        act_ref[...] = (rf * rf).astype(BF16)

    out = pl.BlockSpec((tt, tn), lambda i, j: (i, j))
    return _call(
        body, name="ff1", grid=(t // tt, n // tn),
        in_specs=[pl.BlockSpec((tt, d), lambda i, j: (i, 0)), pl.BlockSpec((tn, d), lambda i, j: (j, 0))],
        out_specs=[out, out],
        out_shape=[jax.ShapeDtypeStruct((t, n), BF16)] * 2,
        args=(h3, w_ff1t), carry=carry)


def _ff2_loss(act, w_ff2, x2, g4, target):
    t, k = act.shape
    d = x2.shape[1]
    tt, tk = _tile(t, 1024), _tile(k, 512)
    nk = k // tk

    def body(a_ref, w_ref, x2_ref, g_ref, tg_ref, dy_ref, df_ref, dg_ref, loss_ref, acc):
        i, kk = pl.program_id(0), pl.program_id(1)

        @pl.when(kk == 0)
        def _():
            acc[...] = jnp.zeros_like(acc)

        @pl.when((i == 0) & (kk == 0))
        def _():
            dg_ref[...] = jnp.zeros_like(dg_ref)
            loss_ref[...] = jnp.zeros_like(loss_ref)

        acc[...] += _dot_nn(a_ref[...], w_ref[...])

        @pl.when(kk == nk - 1)
        def _():
            def tail(rows):
                fhat, r = _rms_hat(acc[rows, :])
                g = g_ref[...]
                e = x2_ref[rows, :] + fhat * g - tg_ref[rows, :]
                loss_ref[...] += 0.5 * jnp.sum(jnp.mean(e * e, axis=-1, keepdims=True))
                dy = e * (1.0 / d)
                dy_ref[rows, :] = dy
                df, dg = _rms_bwd(dy, fhat, r, g)
                df_ref[rows, :] = df.astype(BF16)
                dg_ref[...] += dg

            _row_chunks(tt, tail)

    row = pl.BlockSpec((tt, d), lambda i, kk: (i, 0))
    vec = pl.BlockSpec((1, d), lambda i, kk: (0, 0))
    return _call(
        body, name="ff2_loss", grid=(t // tt, nk),
        in_specs=[
            pl.BlockSpec((tt, tk), lambda i, kk: (i, kk)),
            pl.BlockSpec((tk, d), lambda i, kk: (kk, 0)),
            row, vec, row,
        ],
        out_specs=[row, row, vec, pl.BlockSpec((1, 128), lambda i, kk: (0, 0))],
        out_shape=[
            jax.ShapeDtypeStruct((t, d), F32),
            jax.ShapeDtypeStruct((t, d), BF16),
            jax.ShapeDtypeStruct((1, d), F32),
            jax.ShapeDtypeStruct((1, 128), F32),
        ],
        scratch_shapes=[pltpu.VMEM((tt, d), F32)],
        args=(act, w_ff2, x2, g4, target))[0]


def _ff2_bwd(df, w_ff2, rf):
    t, d = df.shape
    n = w_ff2.shape[0]
    tt, tn = _tile(t, 2048), _tile(n, 512)

    def body(df_ref, w_ref, rf_ref, out_ref):
        d_act = _dot_nt(df_ref[...], w_ref[...])
        out_ref[...] = (d_act * (2.0 * rf_ref[...].astype(F32))).astype(BF16)

    blk = pl.BlockSpec((tt, tn), lambda i, j: (i, j))
    return _call(
        body, name="ff2_bwd", grid=(t // tt, n // tn),
        in_specs=[pl.BlockSpec((tt, d), lambda i, j: (i, 0)), pl.BlockSpec((tn, d), lambda i, j: (j, 0)), blk],
        out_specs=[blk],
        out_shape=[jax.ShapeDtypeStruct((t, n), BF16)],
        args=(df, w_ff2, rf))[0][0]


def _wgrad(a, b, name, prev=None, row_off=0, rows=None, carry=None):
    t, m = a.shape
    n = b.shape[1]
    rows = m if rows is None else rows
    tm, tk = _tile(m, 512), _tile(t, 2048)
    nk = t // tk
    assert row_off % tm == 0
    off = row_off // tm

    def body(*refs):
        a_ref, b_ref = refs[0], refs[1]
        o32_ref, o16_ref, acc = refs[-3], refs[-2], refs[-1]
        kk = pl.program_id(1)

        @pl.when(kk == 0)
        def _():
            acc[...] = jnp.zeros_like(acc)

        acc[...] += _dot_tn(a_ref[...], b_ref[...])

        @pl.when(kk == nk - 1)
        def _():
            o32_ref[...] = acc[...]
            o16_ref[...] = acc[...].astype(BF16)

    in_specs = [pl.BlockSpec((tk, tm), lambda i, kk: (kk, i)), pl.BlockSpec((tk, n), lambda i, kk: (kk, 0))]
    args = [a, b]
    aliases = {}
    if prev is not None:
        in_specs += [ANY, ANY]
        args += list(prev)
        aliases = {2: 0, 3: 1}
    out = pl.BlockSpec((tm, n), lambda i, kk: (off + i, 0))
    return _call(
        body, name=name, grid=(m // tm, nk),
        in_specs=in_specs, out_specs=[out, out],
        out_shape=[jax.ShapeDtypeStruct((rows, n), F32), jax.ShapeDtypeStruct((rows, n), BF16)],
        scratch_shapes=[pltpu.VMEM((tm, n), F32)],
        aliases=aliases, args=args, carry=carry)


def _wgrad_parts(parts, b, name, carry=None):
    t, n = b.shape
    tm = 512
    bounds = []
    lo = 0
    for part in parts:
        assert part.shape[0] == t and part.shape[1] % tm == 0
        bounds.append((lo, lo + part.shape[1] // tm))
        lo += part.shape[1] // tm
    nm = lo
    np_ = len(parts)

    def body(*refs):
        p_refs, b_ref, o32_ref, o16_ref = refs[:np_], refs[np_], refs[np_ + 1], refs[np_ + 2]
        i = pl.program_id(0)
        for (lo_p, hi_p), p_ref in zip(bounds, p_refs):
            @pl.when((i >= lo_p) & (i < hi_p))
            def _(p_ref=p_ref):
                res = _dot_tn(p_ref[...], b_ref[...])
                o32_ref[...] = res
                o16_ref[...] = res.astype(BF16)

    def part_spec(lo_p, hi_p):
        return pl.BlockSpec((t, tm), lambda i: (0, jnp.clip(i - lo_p, 0, hi_p - lo_p - 1)))

    out = pl.BlockSpec((tm, n), lambda i: (i, 0))
    return _call(
        body, name=name, grid=(nm,),
        in_specs=[part_spec(lo_p, hi_p) for lo_p, hi_p in bounds] + [pl.BlockSpec((t, n), lambda i: (0, 0))],
        out_specs=[out, out],
        out_shape=[jax.ShapeDtypeStruct((nm * tm, n), F32), jax.ShapeDtypeStruct((nm * tm, n), BF16)],
        args=(*parts, b), carry=carry)


def _ff1_bwd_norms(d_f1, w_ff1t, dy, x2, g3, m, g2, carry=None):
    t, k = d_f1.shape
    d = x2.shape[1]
    tt, tk = _tile(t, 1024), _tile(k, 512)
    nk = k // tk

    def body(a_ref, w_ref, dy_ref, x2_ref, g3_ref, m_ref, g2_ref, dx2_ref, dm_ref, dg3_ref, dg2_ref, acc):
        i, kk = pl.program_id(0), pl.program_id(1)

        @pl.when(kk == 0)
        def _():
            acc[...] = jnp.zeros_like(acc)

        @pl.when((i == 0) & (kk == 0))
        def _():
            dg3_ref[...] = jnp.zeros_like(dg3_ref)
            dg2_ref[...] = jnp.zeros_like(dg2_ref)

        acc[...] += _dot_nn(a_ref[...], w_ref[...])

        @pl.when(kk == nk - 1)
        def _():
            def tail(rows):
                xhat, r3 = _rms_hat(x2_ref[rows, :])
                dx, dg3 = _rms_bwd(acc[rows, :], xhat, r3, g3_ref[...])
                dx2 = dy_ref[rows, :] + dx
                dx2_ref[rows, :] = dx2
                dg3_ref[...] += dg3
                mhat, r2 = _rms_hat(m_ref[rows, :])
                dm, dg2 = _rms_bwd(dx2, mhat, r2, g2_ref[...])
                dm_ref[rows, :] = dm.astype(BF16)
                dg2_ref[...] += dg2

            _row_chunks(tt, tail)

    row = pl.BlockSpec((tt, d), lambda i, kk: (i, 0))
    vec = pl.BlockSpec((1, d), lambda i, kk: (0, 0))
    return _call(
        body, name="ff1_bwd_norms", grid=(t // tt, nk),
        in_specs=[
            pl.BlockSpec((tt, tk), lambda i, kk: (i, kk)),
            pl.BlockSpec((tk, d), lambda i, kk: (kk, 0)),
            row, row, vec, row, vec,
        ],
        out_specs=[row, row, vec, vec],
        out_shape=[
            jax.ShapeDtypeStruct((t, d), F32),
            jax.ShapeDtypeStruct((t, d), BF16),
            jax.ShapeDtypeStruct((1, d), F32),
            jax.ShapeDtypeStruct((1, d), F32),
        ],
        scratch_shapes=[pltpu.VMEM((tt, d), F32)],
        args=(d_f1, w_ff1t, dy, x2, g3, m, g2), carry=carry)


def _wo_bwd_mix(dm, w_o, br_a, br_b, proj, b_gate, ga_block, gb_block, carry=None):
    t, d = dm.shape
    tt, tn = _tile(t, 1024), 512
    nj = d // tn

    def body(dm_ref, w_ref, bra_ref, brb_ref, ga_ref, gb_ref, ba_ref, bb_ref,
             dbra_ref, dbrb_ref, dga_ref, dgb_ref, dba_ref, dbb_ref):
        i = pl.program_id(1)

        @pl.when(i == 0)
        def _():
            dba_ref[...] = jnp.zeros_like(dba_ref)
            dbb_ref[...] = jnp.zeros_like(dbb_ref)

        d_mix = _dot_nt(dm_ref[...], w_ref[...])
        ga = _sig(ga_ref[...].astype(F32) + ba_ref[...])
        gb = _sig(gb_ref[...].astype(F32) + bb_ref[...])
        dbra_ref[...] = (d_mix * ga).astype(BF16)
        dbrb_ref[...] = (d_mix * gb).astype(BF16)
        dga = d_mix * bra_ref[...] * (ga * (1.0 - ga))
        dgb = d_mix * brb_ref[...] * (gb * (1.0 - gb))
        dga_ref[...] = dga.astype(BF16)
        dgb_ref[...] = dgb.astype(BF16)
        dba_ref[...] += jnp.sum(dga, axis=0, keepdims=True)
        dbb_ref[...] += jnp.sum(dgb, axis=0, keepdims=True)

    blk = pl.BlockSpec((tt, tn), lambda j, i: (i, j))
    vec = pl.BlockSpec((1, tn), lambda j, i: (0, j))
    return _call(
        body, name="wo_bwd_mix", grid=(nj, t // tt),
        in_specs=[
            pl.BlockSpec((tt, d), lambda j, i: (i, 0)),
            pl.BlockSpec((tn, d), lambda j, i: (j, 0)),
            blk, blk,
            pl.BlockSpec((tt, tn), lambda j, i: (i, ga_block + j)),
            pl.BlockSpec((tt, tn), lambda j, i: (i, gb_block + j)),
            vec,
            pl.BlockSpec((1, tn), lambda j, i: (0, nj + j)),
        ],
        out_specs=[blk, blk, blk, blk, vec, vec],
        out_shape=[jax.ShapeDtypeStruct((t, d), BF16)] * 4 + [jax.ShapeDtypeStruct((1, d), F32)] * 2,
        args=(dm, w_o, br_a, br_b, proj, proj, b_gate, b_gate), carry=carry)


def _lru_up_bwd(d_br_a, w_lru_up, proj, h, g_block, carry=None):
    t, d = d_br_a.shape
    tt, tn = _tile(t, 1024), 512

    def body(a_ref, w_ref, g_ref, h_ref, dh_ref, dg_ref):
        d_y = _dot_nt(a_ref[...], w_ref[...])
        gel, gel_grad = _gelu_and_grad(g_ref[...].astype(F32))
        dh_ref[...] = d_y * gel
        dg_ref[...] = (d_y * h_ref[...] * gel_grad).astype(BF16)

    blk = pl.BlockSpec((tt, tn), lambda i, j: (i, j))
    return _call(
        body, name="lru_up_bwd", grid=(t // tt, d // tn),
        in_specs=[
            pl.BlockSpec((tt, d), lambda i, j: (i, 0)),
            pl.BlockSpec((tn, d), lambda i, j: (j, 0)),
            pl.BlockSpec((tt, tn), lambda i, j: (i, g_block + j)),
            blk,
        ],
        out_specs=[blk, blk],
        out_shape=[jax.ShapeDtypeStruct((t, d), F32), jax.ShapeDtypeStruct((t, d), BF16)],
        args=(d_br_a, w_lru_up, proj, h), carry=carry)


def _pool_up_bwd(d_br_b, w_pool_upt):
    t, d = d_br_b.shape
    dp = w_pool_upt.shape[1]
    tt = _tile(t, 2048)

    def body(a_ref, w_ref, out_ref):
        out_ref[...] = _dot_nn(a_ref[...], w_ref[...])

    return _call(
        body, name="pool_up_bwd", grid=(t // tt,),
        in_specs=[pl.BlockSpec((tt, d), lambda i: (i, 0)), pl.BlockSpec((d, dp), lambda i: (0, 0))],
        out_specs=[pl.BlockSpec((tt, dp), lambda i: (i, 0))],
        out_shape=[jax.ShapeDtypeStruct((t, dp), F32)],
        args=(d_br_b, w_pool_upt))[0][0]


def _lru_bwd(dh, xc, h, proj, conv_w, w_a, b_a, w_x, b_x, lam, carry=None):
    t, dr = dh.shape
    cb = LRU_CB
    hd = LRU_HEAD_DIM
    per = cb // hd
    tc = _tile(t, 256)
    ncb, ntc = dr // cb, t // tc

    def body(dh_ref, xc_ref, h_ref, hp_ref, xp_ref, cw_ref, wa_ref, ba_ref, wx_ref, bx_ref, lam_ref,
             dxp_ref, dwa_ref, dba_ref, dwx_ref, dbx_ref, dlam_ref, dcw_ref, dcb_ref,
             nextd_s, anext_s, gnext_s, tmp_s, wa_s, wx_s):
        c = pl.program_id(1)
        rc = ntc - 1 - c

        @pl.when(c == 0)
        def _():
            nextd_s[...] = jnp.zeros_like(nextd_s)
            anext_s[...] = jnp.zeros_like(anext_s)
            gnext_s[...] = jnp.zeros_like(gnext_s)
            for ref in (dwa_ref, dba_ref, dwx_ref, dbx_ref, dlam_ref, dcw_ref, dcb_ref):
                ref[...] = jnp.zeros_like(ref)
            _fill_block_diag(wa_ref, wa_s)
            _fill_block_diag(wx_ref, wx_s)

        xc = xc_ref[...]
        wa, wx, lam = wa_s[...], wx_s[...], lam_ref[...]
        xcb, r, i, sp, log_a, a, mult = _lru_gates(xc, wa, ba_ref[...], wx, bx_ref[...], lam)
        row = lax.broadcasted_iota(jnp.int32, xc.shape, 0)
        h = h_ref[...]
        hp = jnp.where(rc == 0, 0.0, hp_ref[...])
        hprev = jnp.where(row >= 1, pltpu.roll(h, 1, 0), pltpu.roll(hp, 1, 0))

        def up(v, nv, j):
            return jnp.where(row < tc - j, pltpu.roll(v, tc - j, 0), nv)

        av = up(a, anext_s[...], 1)
        bv = dh_ref[...]
        s = 1
        while s < tc:
            a_sh = up(av, 1.0, s)
            b_sh = up(bv, 0.0, s)
            bv = av * b_sh + bv
            av = av * a_sh
            s *= 2
        gt = av * gnext_s[...] + bv
        tmp_s[...] = gt
        gnext_s[...] = tmp_s[0:1, :]
        tmp_s[...] = a
        anext_s[...] = tmp_s[0:1, :]

        da = gt * hprev
        ixc = i * xc
        d_mult = gt * ixc
        d_i = gt * mult * xc
        d_xc = gt * mult * i
        d_log_a = da * a - d_mult * (a * a) / mult
        d_pre_r = (d_log_a * ((-LRU_C) * sp)) * (r * (1.0 - r))
        d_pre_i = d_i * (i * (1.0 - i))
        d_sp = jnp.sum(d_log_a * ((-LRU_C) * r), axis=0, keepdims=True)
        dlam_ref[...] += d_sp * (-1.0 / (1.0 + jnp.exp(lam)))
        dpr = d_pre_r.astype(BF16)
        dpi = d_pre_i.astype(BF16)
        dba_ref[...] += jnp.sum(d_pre_r, axis=0, keepdims=True)
        dbx_ref[...] += jnp.sum(d_pre_i, axis=0, keepdims=True)
        pa = _dot_tn(xcb, dpr)
        px = _dot_tn(xcb, dpi)
        for k in range(per):
            dwa_ref[k] += pa[k * hd:(k + 1) * hd, k * hd:(k + 1) * hd]
            dwx_ref[k] += px[k * hd:(k + 1) * hd, k * hd:(k + 1) * hd]
        d_xc = d_xc + _dot_nt(dpr, wa) + _dot_nt(dpi, wx)

        nxt = nextd_s[...]
        xp = xp_ref[...].astype(F32)
        dxp = cw_ref[3:4, :] * d_xc
        dcw_ref[3:4, :] += jnp.sum(xp * d_xc, axis=0, keepdims=True)
        for j in (1, 2, 3):
            uj = up(d_xc, pltpu.roll(nxt, tc - j, 0), j)
            dxp = dxp + cw_ref[3 - j:4 - j, :] * uj
            dcw_ref[3 - j:4 - j, :] += jnp.sum(xp * uj, axis=0, keepdims=True)
        dcb_ref[...] += jnp.sum(d_xc, axis=0, keepdims=True)
        nextd_s[...] = d_xc
        dxp_ref[...] = dxp.astype(BF16)

    vec = pl.BlockSpec((1, cb), lambda j, c: (0, j))
    blk = pl.BlockSpec((tc, cb), lambda j, c: (ntc - 1 - c, j))
    mat = pl.BlockSpec((per, hd, hd), lambda j, c: (j, 0, 0))
    cwb = pl.BlockSpec((4, cb), lambda j, c: (0, j))
    return _call(
        body, name="lru_bwd", grid=(ncb, ntc),
        in_specs=[
            blk, blk, blk,
            pl.BlockSpec((tc, cb), lambda j, c: (jnp.maximum(ntc - 2 - c, 0), j)),
            blk, cwb, mat, vec, mat, vec, vec,
        ],
        out_specs=[blk, mat, vec, mat, vec, vec, cwb, vec],
        out_shape=[
            jax.ShapeDtypeStruct((t, dr), BF16),
            jax.ShapeDtypeStruct(w_a.shape, F32),
            jax.ShapeDtypeStruct((1, dr), F32),
            jax.ShapeDtypeStruct(w_x.shape, F32),
            jax.ShapeDtypeStruct((1, dr), F32),
            jax.ShapeDtypeStruct((1, dr), F32),
            jax.ShapeDtypeStruct((4, dr), F32),
            jax.ShapeDtypeStruct((1, dr), F32),
        ],
        scratch_shapes=[
            pltpu.VMEM((tc, cb), F32),
            pltpu.VMEM((1, cb), F32),
            pltpu.VMEM((1, cb), F32),
            pltpu.VMEM((tc, cb), F32),
            pltpu.VMEM((cb, cb), BF16),
            pltpu.VMEM((cb, cb), BF16),
        ],
        args=(dh, xc, h, h, proj, conv_w, w_a, b_a, w_x, b_x, lam), carry=carry)


def _pool_bwd(d_y_pool, p, pool_w, pool_scale):
    t, dp = d_y_pool.shape
    tc = _tile(t, 256)
    ntc = t // tc
    ng = len(POOL_WINDOWS)

    def body(dy_ref, p_ref, w_ref, sc_ref, dx_ref, dw_ref, dsc_ref, nz, n2, n4, n8, dp_s):
        c = pl.program_id(0)
        rc = ntc - 1 - c

        @pl.when(c == 0)
        def _():
            for s in (nz, n2, n4, n8):
                s[...] = jnp.zeros_like(s)
            dw_ref[...] = jnp.zeros_like(dw_ref)
            dsc_ref[...] = jnp.zeros_like(dsc_ref)

        for g in range(ng):
            sl = slice(g * POOL_GROUP_DIM, (g + 1) * POOL_GROUP_DIM)
            pg = p_ref[:, sl]
            dyg = dy_ref[:, sl]
            wg = w_ref[g].astype(BF16)
            q = _dot_nn(pg, wg)
            dsc_ref[:, sl] += jnp.sum(dyg * q, axis=0, keepdims=True)
            dpw = (dyg * sc_ref[:, sl]).astype(BF16)
            dw_ref[g] += _dot_tn(pg, dpw)
            dp_s[:, sl] = _dot_nt(dpw, wg)

        dpv = dp_s[...]
        row = lax.broadcasted_iota(jnp.int32, dpv.shape, 0)
        col = lax.broadcasted_iota(jnp.int32, dpv.shape, 1)
        win = _pool_select(col, POOL_WINDOWS)
        cnt = jnp.minimum(rc * tc + row + 1, win).astype(F32)
        z = dpv / cnt

        def up(v, nv, j):
            return jnp.where(row < tc - j, pltpu.roll(v, tc - j, 0), pltpu.roll(nv[...], tc - j, 0))

        u2 = z + up(z, nz, 1)
        u4 = u2 + up(u2, n2, 2)
        u8 = u4 + up(u4, n4, 4)
        u16 = u8 + up(u8, n8, 8)
        nz[...] = z
        n2[...] = u2
        n4[...] = u4
        n8[...] = u8
        dx_ref[...] = (_pool_select(col, (u2, u4, u8, u16)) - dpv).astype(BF16)

    blk = pl.BlockSpec((tc, dp), lambda c: (ntc - 1 - c, 0))
    full_w = pl.BlockSpec(pool_w.shape, lambda c: (0, 0, 0))
    vec = pl.BlockSpec((1, dp), lambda c: (0, 0))
    return _call(
        body, name="pool_bwd", grid=(ntc,),
        in_specs=[blk, blk, full_w, vec],
        out_specs=[blk, full_w, vec],
        out_shape=[
            jax.ShapeDtypeStruct((t, dp), BF16),
            jax.ShapeDtypeStruct(pool_w.shape, F32),
            jax.ShapeDtypeStruct((1, dp), F32),
        ],
        scratch_shapes=[pltpu.VMEM((tc, dp), F32)] * 5,
        args=(d_y_pool, p, pool_w, pool_scale))[0]


def _win_bwd_norm(parts, w_int, dx2, x, g1, carry=None):
    t, d = x.shape
    tk = 512
    tt = _tile(t, 1024)
    bounds = []
    k0 = 0
    for part in parts:
        assert part.shape[1] % tk == 0
        bounds.append((k0, k0 + part.shape[1] // tk))
        k0 += part.shape[1] // tk
    nk = k0
    assert nk * tk == w_int.shape[0]
    np_ = len(parts)

    def body(*refs):
        p_refs = refs[:np_]
        w_ref, dx2_ref, x_ref, g_ref, gx_ref, dg_ref, acc = refs[np_:]
        i, kk = pl.program_id(0), pl.program_id(1)

        @pl.when(kk == 0)
        def _():
            acc[...] = jnp.zeros_like(acc)

        @pl.when((i == 0) & (kk == 0))
        def _():
            dg_ref[...] = jnp.zeros_like(dg_ref)

        for (lo, hi), p_ref in zip(bounds, p_refs):
            @pl.when((kk >= lo) & (kk < hi))
            def _(p_ref=p_ref):
                acc[...] += _dot_nn(p_ref[...], w_ref[...])

        @pl.when(kk == nk - 1)
        def _():
            def tail(rows):
                xhat, r = _rms_hat(x_ref[rows, :])
                dx, dg = _rms_bwd(acc[rows, :], xhat, r, g_ref[...])
                gx_ref[rows, :] = dx2_ref[rows, :] + dx
                dg_ref[...] += dg

            _row_chunks(tt, tail)

    def part_spec(lo, hi):
        return pl.BlockSpec((tt, tk), lambda i, kk: (i, jnp.clip(kk - lo, 0, hi - lo - 1)))

    row = pl.BlockSpec((tt, d), lambda i, kk: (i, 0))
    vec = pl.BlockSpec((1, d), lambda i, kk: (0, 0))
    return _call(
        body, name="win_bwd_norm", grid=(t // tt, nk),
        in_specs=[part_spec(lo, hi) for lo, hi in bounds]
        + [pl.BlockSpec((tk, d), lambda i, kk: (kk, 0)), row, row, vec],
        out_specs=[row, vec],
        out_shape=[jax.ShapeDtypeStruct((t, d), F32), jax.ShapeDtypeStruct((1, d), F32)],
        scratch_shapes=[pltpu.VMEM((tt, d), F32)],
        args=(*parts, w_int, dx2, x, g1), carry=carry)


def _adam_math(w, g, m, v):
    m = ADAM_B1 * m + (1.0 - ADAM_B1) * g
    v = ADAM_B2 * v + (1.0 - ADAM_B2) * (g * g)
    m_hat = m / (1.0 - ADAM_B1 ** ADAM_STEP)
    v_hat = v / (1.0 - ADAM_B2 ** ADAM_STEP)
    delta = -ADAM_LR * (m_hat / (jnp.sqrt(v_hat) + ADAM_EPS) + ADAM_WD * w)
    return delta, m, v


def _adamw_big(ws, gs, ms, vs, carry=None):
    n = len(ws)
    nb = 8

    def body(*refs):
        for a in range(n):
            w_ref, g_ref, m_ref, v_ref = refs[4 * a:4 * a + 4]
            d_ref, nm_ref, nv_ref = refs[4 * n + 3 * a:4 * n + 3 * a + 3]
            dl, m, v = _adam_math(w_ref[...], g_ref[...], m_ref[...], v_ref[...])
            d_ref[...] = dl
            nm_ref[...] = m
            nv_ref[...] = v

    in_specs, out_specs, out_shape, args = [], [], [], []
    for w, g, m, v in zip(ws, gs, ms, vs):
        rows, cols = w.shape
        blk = pl.BlockSpec((rows // nb, cols), lambda i: (i, 0))
        in_specs += [blk] * 4
        args += [w, g, m, v]
        out_specs += [blk] * 3
        out_shape += [jax.ShapeDtypeStruct(w.shape, F32)] * 3
    outs, got = _call(body, name="adamw_big", grid=(nb,), in_specs=in_specs, out_specs=out_specs,
                      out_shape=out_shape, args=args, carry=carry)
    return [tuple(outs[3 * a:3 * a + 3]) for a in range(n)], got


SMALL_ORDER = ("norm_mix_pre", "norm_mix_post", "norm_mlp_pre", "norm_mlp_post", "b_gate", "conv_w", "conv_b",
               "lru_w_a", "lru_b_a", "lru_w_x", "lru_b_x", "lru_lambda", "pool_w", "pool_scale")
VEC_ROW = dict(norm_mix_pre=0, norm_mix_post=1, norm_mlp_pre=2, norm_mlp_post=3, conv_b=6, lru_b_a=7,
               lru_b_x=8, lru_lambda=9)
ROW_B_GATE, ROW_POOL_SCALE, ROW_CONV_W, ROW_LOSS, N_VEC_ROWS = 4, 10, 11, 15, 16


def _adamw_small(vec_parts, g_pool, g_wa, g_wx, me, params):
    d = vec_parts.shape[2]
    names = SMALL_ORDER
    n = len(names)
    cw_cols = params["conv_w"][0].shape[2]

    def body(me_ref, vec_ref, vecc_ref, gp_ref, gwa_ref, gwx_ref, *refs):
        wmv = refs[:3 * n]
        loss_ref = refs[3 * n]
        outs = refs[3 * n + 1:3 * n + 1 + 4 * n]
        vs, vsc = refs[3 * n + 1 + 4 * n:]
        acc, accc = vec_ref[0], vecc_ref[0]
        for k in range(1, N_DEV):
            acc = acc + vec_ref[k]
            accc = accc + vecc_ref[k]
        vs[...] = acc
        vsc[...] = accc
        loss_ref[...] = vs[ROW_LOSS:ROW_LOSS + 1, 0:128]

        def upd(a, g, idx):
            w_ref, m_ref, v_ref = wmv[3 * a:3 * a + 3]
            g_ref, d_ref, nm_ref, nv_ref = outs[4 * a:4 * a + 4]
            dl, m, v = _adam_math(w_ref[idx], g, m_ref[idx], v_ref[idx])
            g_ref[idx] = g
            d_ref[idx] = dl
            nm_ref[idx] = m
            nv_ref[idx] = v

        for a, name in enumerate(names):
            if name in VEC_ROW:
                r = VEC_ROW[name]
                upd(a, vs[r:r + 1, :], (slice(None), slice(None)))
            elif name == "b_gate":
                for half in range(2):
                    r = ROW_B_GATE + half
                    upd(a, vs[r:r + 1, :], (slice(None), slice(half * d, (half + 1) * d)))
            elif name == "pool_scale":
                width = params[name][0].shape[1]
                upd(a, vs[ROW_POOL_SCALE:ROW_POOL_SCALE + 1, 0:width], (slice(None), slice(None)))
            elif name == "conv_w":
                upd(a, vsc[ROW_CONV_W:ROW_CONV_W + 4, :], (0,))
            elif name == "pool_w":
                upd(a, gp_ref[...], (Ellipsis,))
            elif name == "lru_w_a":
                upd(a, gwa_ref[...], (Ellipsis,))
            elif name == "lru_w_x":
                upd(a, gwx_ref[...], (Ellipsis,))
            else:
                raise ValueError(name)

    def whole(shape):
        nd = len(shape)
        return pl.BlockSpec(tuple(shape), lambda i, me_ref: (0,) * nd)

    in_specs = [
        whole(vec_parts.shape),
        pl.BlockSpec((N_DEV, N_VEC_ROWS, cw_cols), lambda i, me_ref: (0, 0, me_ref[0])),
        whole(g_pool.shape), whole(g_wa.shape), whole(g_wx.shape),
    ]
    args = [vec_parts, vec_parts, g_pool, g_wa, g_wx]
    out_specs = [whole((1, 128))]
    out_shape = [jax.ShapeDtypeStruct((1, 128), F32)]
    for name in names:
        for arr in params[name]:
            in_specs.append(whole(arr.shape))
            args.append(arr)
        shp = params[name][0].shape
        out_specs += [whole(shp)] * 4
        out_shape += [jax.ShapeDtypeStruct(shp, F32)] * 4
    grid_spec = pltpu.PrefetchScalarGridSpec(
        num_scalar_prefetch=1, grid=(1,), in_specs=in_specs, out_specs=out_specs,
        scratch_shapes=[pltpu.VMEM((N_VEC_ROWS, d), F32), pltpu.VMEM((N_VEC_ROWS, cw_cols), F32)])
    outs = pl.pallas_call(
        body, name="adamw_small", grid_spec=grid_spec, out_shape=out_shape,
        compiler_params=pltpu.CompilerParams(
            dimension_semantics=("arbitrary",), vmem_limit_bytes=V7X_VMEM_LIMIT_BYTES),
    )(me, *_in_hbm(args))
    return outs[0], {name: tuple(outs[1 + 4 * a:5 + 4 * a]) for a, name in enumerate(names)}


def _rs_sum(full, recv, shard_ids, slot_ids, name):
    r, rest = recv.shape[1], tuple(recv.shape[2:])
    zeros = (0,) * len(rest)
    send_dtype = recv.dtype

    def body(sh_ref, sl_ref, full_ref, recv_ref, own_ref, send_ref):
        s = pl.program_id(0)
        v = full_ref[...] + recv_ref[...].astype(F32)

        @pl.when(s == 0)
        def _():
            own_ref[...] = v

        @pl.when(s > 0)
        def _():
            send_ref[...] = v.astype(send_dtype)

    grid_spec = pltpu.PrefetchScalarGridSpec(
        num_scalar_prefetch=2,
        grid=(4,),
        in_specs=[
            pl.BlockSpec((r,) + rest, lambda s, sh, sl: (sh[s],) + zeros),
            pl.BlockSpec((None, r) + rest, lambda s, sh, sl: (sl[s], 0) + zeros),
        ],
        out_specs=[
            pl.BlockSpec((None, r) + rest, lambda s, sh, sl: (0, 0) + zeros),
            pl.BlockSpec((None, r) + rest, lambda s, sh, sl: (jnp.maximum(s - 1, 0), 0) + zeros),
        ],
    )
    return pl.pallas_call(
        body,
        name=name,
        grid_spec=grid_spec,
        out_shape=[jax.ShapeDtypeStruct((1, r) + rest, F32), jax.ShapeDtypeStruct((3, r) + rest, send_dtype)],
        compiler_params=pltpu.CompilerParams(
            dimension_semantics=("arbitrary",), vmem_limit_bytes=V7X_VMEM_LIMIT_BYTES),
    )(shard_ids, slot_ids, *_in_hbm([full, recv]))


def _finals(pairs, name, carry=None):
    nb = 4
    n = len(pairs)

    def body(*refs):
        for a in range(n):
            own_ref, recv_ref = refs[2 * a], refs[2 * a + 1]
            acc = own_ref[...]
            for k in range(3):
                acc = acc + recv_ref[k].astype(F32)
            refs[2 * n + a][...] = acc

    in_specs, out_specs, out_shape, args = [], [], [], []
    for own, recv in pairs:
        _, rows, cols = own.shape
        in_specs += [pl.BlockSpec((None, rows // nb, cols), lambda i: (0, i, 0)),
                     pl.BlockSpec((3, rows // nb, cols), lambda i: (0, i, 0))]
        args += [own, recv]
        out_specs.append(pl.BlockSpec((rows // nb, cols), lambda i: (i, 0)))
        out_shape.append(jax.ShapeDtypeStruct((rows, cols), F32))
    return _call(body, name=name, grid=(nb,), in_specs=in_specs, out_specs=out_specs,
                 out_shape=out_shape, args=args, carry=carry)


def _rs_sums(fulls_f32, recv1, tag):
    x, y, c = _place()
    qs = jnp.stack([2 * x + y, 2 * (1 - x) + y, 2 * x + (1 - y), 2 * (1 - x) + (1 - y)]).astype(jnp.int32)
    shard_ids = 2 * qs + c
    return [_rs_sum(f32, r1, shard_ids, qs, f"rs_sum_{tag}{a}")
            for a, (f32, r1) in enumerate(zip(fulls_f32, recv1))]


def _rs_level1(fulls_f32, fulls_send, tag):
    recv1 = _run_plan(_rs_sibling_plan(fulls_send), "rs_sibling_" + tag)
    return _rs_sums(fulls_f32, recv1, tag)


def _rows(g):
    return g.reshape(g.shape[0] * g.shape[1], g.shape[2])


def kernel(x, norm_mix_pre, norm_mix_post, norm_mlp_pre, norm_mlp_post, w_in, b_gate, conv_w, conv_b, lru_w_a, lru_b_a, lru_w_x, lru_b_x, lru_lambda, pool_w, pool_scale, w_lru_up, w_pool_up, w_o, w_ff1, w_ff2, loss_target, m_norm_mix_pre, m_norm_mix_post, m_norm_mlp_pre, m_norm_mlp_post, m_w_in, m_b_gate, m_conv_w, m_conv_b, m_lru_w_a, m_lru_b_a, m_lru_w_x, m_lru_b_x, m_lru_lambda, m_pool_w, m_pool_scale, m_w_lru_up, m_w_pool_up, m_w_o, m_w_ff1, m_w_ff2, v_norm_mix_pre, v_norm_mix_post, v_norm_mlp_pre, v_norm_mlp_post, v_w_in, v_b_gate, v_conv_w, v_conv_b, v_lru_w_a, v_lru_b_a, v_lru_w_x, v_lru_b_x, v_lru_lambda, v_pool_w, v_pool_scale, v_w_lru_up, v_w_pool_up, v_w_o, v_w_ff1, v_w_ff2):
    t, d = x.shape[1], x.shape[2]
    d_rnn = conv_b.shape[1]
    d_pool = pool_scale.shape[1]
    per = LRU_CB // LRU_HEAD_DIM
    xi, yi, ci = _place()
    me = 4 * xi + 2 * yi + ci

    x2d = x[0]
    tgt = loss_target[0]

    s_in = w_in[0].T.astype(BF16)
    s_lu = w_lru_up[0].astype(BF16)
    s_pu = w_pool_up[0].T.astype(BF16)
    s_o = w_o[0].astype(BF16)
    s_f1 = w_ff1[0].T.astype(BF16)
    s_f2 = w_ff2[0].astype(BF16)
    s_cw = jnp.pad(conv_w[0], ((0, 4), (0, 0)))

    g_in, g_cw = _run_plan(_ag_plan([s_in, s_cw]), "ag_w_in")
    w_int = _rows(g_in)
    conv_w_full = jnp.transpose(g_cw[:, :4, :], (1, 0, 2)).reshape(4, d_rnn)

    wa_bd, wx_bd = lru_w_a[0], lru_w_x[0]
    pw = pool_w[0]
    pw_bf = pw.astype(BF16)

    pool_block = (2 * d_rnn) // d_pool
    ga_block = (2 * d_rnn + d_pool) // 512
    gb_block = ga_block + d // 512
    g_block = d_rnn // 512

    r_f1, r_f2 = s_f1.shape[0], s_f2.shape[0]
    f1_cut = r_f1 // 4
    f2_cut = (3 * r_f2) // 8
    plan = _join([_ag_plan([s_lu, s_pu, s_o]), _ag_plan([s_f1], pieces=[(0, f1_cut)])])
    (proj, h1), got = _norm_proj(x2d, norm_mix_pre, w_int, carry=plan)
    (g_lu, g_pu, g_o), (g_f1,) = plan.split(got)
    w_lu, w_put, w_og = _rows(g_lu), _rows(g_pu), _rows(g_o)
    (y_lru, h, xc), (g_f1,) = _lru_fwd(
        proj, conv_w_full, conv_b, wa_bd, lru_b_a, wx_bd, lru_b_x, lru_lambda,
        carry=_ag_plan([s_f1], pieces=[(f1_cut, r_f1 - f1_cut)], bufs=[g_f1]))
    w_f1t = _rows(g_f1)
    y_pool, p = _pool_fwd(proj, pw_bf, pool_scale, pool_block)
    (br_a, br_b, mix), (g_f2,) = _branch_mix(
        y_lru, y_pool, w_lu, w_put, proj, b_gate, ga_block, gb_block,
        carry=_ag_plan([s_f2], pieces=[(0, f2_cut)]))
    (m, x2, h3), _ = _wo_norm(mix, w_og, x2d, norm_mix_post, norm_mlp_pre)
    (rf, act), (g_f2,) = _ff1(
        h3, w_f1t, carry=_ag_plan([s_f2], pieces=[(f2_cut, r_f2 - f2_cut)], bufs=[g_f2]))
    w_f2 = _rows(g_f2)
    dy, df, dg4, loss_part = _ff2_loss(act, w_f2, x2, norm_mlp_post, tgt)

    d_f1 = _ff2_bwd(df, w_f2, rf)
    (gw_ff2_32, gw_ff2_16), _ = _wgrad(act, df, "wgrad_ff2")
    (gw_ff1_32, gw_ff1_16), r1_ff2 = _wgrad(d_f1, h3, "wgrad_ff1", carry=_rs_sibling_plan([gw_ff2_16]))
    ((own_ff2, send_ff2),) = _rs_sums([gw_ff2_32], r1_ff2, "ff2")
    plan = _join([_rs_chips_plan([send_ff2]), _rs_sibling_plan([gw_ff1_16])])
    (dx2, dm, dg3, dg2), got = _ff1_bwd_norms(d_f1, w_f1t, dy, x2, norm_mlp_pre, m, norm_mix_post, carry=plan)
    (r2_ff2,), r1_ff1 = plan.split(got)
    ((own_ff1, send_ff1),) = _rs_sums([gw_ff1_32], r1_ff1, "ff1")
    cut = send_ff1.shape[1] // 4
    (gw_o_32, gw_o_16), _ = _wgrad(mix, dm, "wgrad_o")
    (d_br_a, d_br_b, p_ga, p_gb, dbg_a, dbg_b), (r2_ff1,) = _wo_bwd_mix(
        dm, w_og, br_a, br_b, proj, b_gate, ga_block, gb_block,
        carry=_rs_chips_plan([send_ff1], pieces=[(0, cut)]))
    (gw_lu_32, gw_lu_16), _ = _wgrad(y_lru, d_br_a, "wgrad_lru_up")
    (gw_pu_32, gw_pu_16), _ = _wgrad(d_br_b, y_pool, "wgrad_pool_up")
    (dh, p_g), r1_mid = _lru_up_bwd(
        d_br_a, w_lu, proj, h, g_block,
        carry=_rs_sibling_plan([gw_o_16, gw_lu_16, gw_pu_16.reshape(-1, d)]))
    mid = _rs_sums([gw_o_32, gw_lu_32, gw_pu_32.reshape(-1, d)], r1_mid, "mid")
    d_y_pool = _pool_up_bwd(d_br_b, w_put)
    (p_x, dwa, db_a, dwx, db_x, dlam, dconv_w, dconv_b), (r2_ff1,) = _lru_bwd(
        dh, xc, h, proj, conv_w_full, wa_bd, lru_b_a, wx_bd, lru_b_x, lru_lambda,
        carry=_rs_chips_plan([send_ff1], pieces=[(cut, send_ff1.shape[1] - cut)], bufs=[r2_ff1]))
    p_p, dpool_w, dpool_scale = _pool_bwd(d_y_pool, p, pw, pool_scale)
    parts = [p_x, p_g, p_p, p_ga, p_gb]
    gw_in, r2_mid = _wgrad_parts(parts, h1, "wgrad_in", carry=_rs_chips_plan([s for _, s in mid]))
    tail = _rs_level1([gw_in[0], dpool_w.reshape(N_DEV, -1, POOL_GROUP_DIM), dwa, dwx],
                      [gw_in[1], dpool_w.reshape(N_DEV, -1, POOL_GROUP_DIM), dwa, dwx], "in")
    (grad_x, dg1), r2_tail = _win_bwd_norm(parts, w_int, dx2, x2d, norm_mix_pre,
                                           carry=_rs_chips_plan([s for _, s in tail]))

    def flat2(a):
        return a.reshape(a.shape[0], -1, a.shape[-1])

    fin_small, _ = _finals([
        (flat2(tail[1][0]), flat2(r2_tail[1])), (flat2(tail[2][0]), flat2(r2_tail[2])),
        (flat2(tail[3][0]), flat2(r2_tail[3])),
    ], "rs_finals_small")

    def pad_row(a):
        return jnp.pad(a, ((0, 0), (0, d - a.shape[1])))

    vecs = jnp.concatenate([dg1, dg2, dg3, dg4, dbg_a, dbg_b, dconv_b, db_a, db_x, dlam,
                            pad_row(dpool_scale), dconv_w, pad_row(loss_part)], axis=0)
    assert vecs.shape[0] == N_VEC_ROWS
    fin, (vec_parts, g_pool, g_wa, g_wx) = _finals([
        (tail[0][0], r2_tail[0]), (mid[1][0], r2_mid[1]), (mid[2][0], r2_mid[2]), (mid[0][0], r2_mid[0]),
        (own_ff1, r2_ff1), (own_ff2, r2_ff2),
    ], "rs_finals", carry=_ag_plan([vecs] + fin_small))
    g_w_in = fin[0].T
    g_w_lru_up = fin[1]
    g_w_pool_up = fin[2].reshape(d // N_DEV, d_pool).T
    g_w_o = fin[3]
    g_w_ff1 = fin[4].T
    g_w_ff2 = fin[5]

    big_names = ["w_in", "w_lru_up", "w_pool_up", "w_o", "w_ff1", "w_ff2"]
    big_w = [w_in, w_lru_up, w_pool_up, w_o, w_ff1, w_ff2]
    big_g = [g_w_in, g_w_lru_up, g_w_pool_up, g_w_o, g_w_ff1, g_w_ff2]
    big_m = [m_w_in, m_w_lru_up, m_w_pool_up, m_w_o, m_w_ff1, m_w_ff2]
    big_v = [v_w_in, v_w_lru_up, v_w_pool_up, v_w_o, v_w_ff1, v_w_ff2]
    big_out, _ = _adamw_big([w[0] for w in big_w], big_g, [mm[0] for mm in big_m], [vv[0] for vv in big_v])

    small = dict(
        norm_mix_pre=(norm_mix_pre, m_norm_mix_pre, v_norm_mix_pre),
        norm_mix_post=(norm_mix_post, m_norm_mix_post, v_norm_mix_post),
        norm_mlp_pre=(norm_mlp_pre, m_norm_mlp_pre, v_norm_mlp_pre),
        norm_mlp_post=(norm_mlp_post, m_norm_mlp_post, v_norm_mlp_post),
        b_gate=(b_gate, m_b_gate, v_b_gate), conv_w=(conv_w, m_conv_w, v_conv_w),
        conv_b=(conv_b, m_conv_b, v_conv_b), lru_w_a=(lru_w_a, m_lru_w_a, v_lru_w_a),
        lru_b_a=(lru_b_a, m_lru_b_a, v_lru_b_a), lru_w_x=(lru_w_x, m_lru_w_x, v_lru_w_x),
        lru_b_x=(lru_b_x, m_lru_b_x, v_lru_b_x), lru_lambda=(lru_lambda, m_lru_lambda, v_lru_lambda),
        pool_w=(pool_w, m_pool_w, v_pool_w), pool_scale=(pool_scale, m_pool_scale, v_pool_scale))
    loss_row, small_out = _adamw_small(
        vec_parts, g_pool.reshape(pool_w.shape), g_wa.reshape(lru_w_a.shape), g_wx.reshape(lru_w_x.shape),
        jnp.reshape(me, (1,)).astype(jnp.int32), small)
    grads = {n: o[0] for n, o in small_out.items()}
    delta = {n: o[1] for n, o in small_out.items()}
    new_m = {n: o[2] for n, o in small_out.items()}
    new_v = {n: o[3] for n, o in small_out.items()}

    for name, g, (dl, nm, nv) in zip(big_names, big_g, big_out):
        grads[name], delta[name], new_m[name], new_v[name] = g[None], dl[None], nm[None], nv[None]

    loss = loss_row[0, 0]
    order = ["norm_mix_pre", "norm_mix_post", "norm_mlp_pre", "norm_mlp_post", "w_in", "b_gate", "conv_w",
             "conv_b", "lru_w_a", "lru_b_a", "lru_w_x", "lru_b_x", "lru_lambda", "pool_w", "pool_scale",
             "w_lru_up", "w_pool_up", "w_o", "w_ff1", "w_ff2"]
    return (loss, grad_x[None], *[grads[n] for n in order], *[delta[n] for n in order],
            *[new_m[n] for n in order], *[new_v[n] for n in order])
```

```python
import functools
import math
import operator
import types

import jax
import jax.numpy as jnp
from jax import lax
from jax.experimental import pallas as pl
from jax.experimental.pallas import tpu as pltpu

F32 = jnp.float32
BF16 = jnp.bfloat16
NORM_EPS = 1e-6
LRU_C = 8.0
N_LRU_HEADS = 16
LRU_HEAD_DIM = 64
POOL_WINDOWS = (2, 4, 8, 16)
POOL_GROUP_DIM = 128
ADAM_LR = 0.001
ADAM_B1 = 0.9
ADAM_B2 = 0.999
ADAM_EPS = 1e-08
ADAM_WD = 0.01
ADAM_STEP = 10
N_DEV = 8
V7X_VMEM_LIMIT_BYTES = 56 * 1024 * 1024
LRU_CB = 256
MESH = pl.DeviceIdType.MESH
ANY = pl.BlockSpec(memory_space=pl.ANY)


def _tile(n, pref):
    t = min(n, pref)
    assert n % t == 0, (n, pref)
    return t


def _dot_nn(a, b):
    return lax.dot_general(a, b, (((1,), (0,)), ((), ())), preferred_element_type=F32)


def _dot_nt(a, b):
    return lax.dot_general(a, b, (((1,), (1,)), ((), ())), preferred_element_type=F32)


def _dot_tn(a, b):
    return lax.dot_general(a, b, (((0,), (0,)), ((), ())), preferred_element_type=F32)


def _row_chunks(n_rows, fn, chunk=256):
    chunk = min(chunk, n_rows)
    assert n_rows % chunk == 0

    def step(r, carry):
        fn(pl.ds(pl.multiple_of(r * chunk, chunk), chunk))
        return carry

    lax.fori_loop(0, n_rows // chunk, step, 0)


def _sig(x):
    return 1.0 / (1.0 + jnp.exp(-x))


def _rms_hat(x):
    r = lax.rsqrt(jnp.mean(x * x, axis=-1, keepdims=True) + NORM_EPS)
    return x * r, r


def _rms_bwd(dn, xhat, r, g):
    q = dn * g
    dx = r * (q - xhat * jnp.mean(q * xhat, axis=-1, keepdims=True))
    dg = jnp.sum(dn * xhat, axis=0, keepdims=True)
    return dx, dg


_GELU_K = math.sqrt(2.0 / math.pi)
_GELU_C = 0.044715


def _gelu_and_grad(g):
    t = jnp.tanh(_GELU_K * (g + _GELU_C * g * g * g))
    val = 0.5 * g * (1.0 + t)
    grad = 0.5 * (1.0 + t) + 0.5 * g * (1.0 - t * t) * (_GELU_K * (1.0 + 3.0 * _GELU_C * g * g))
    return val, grad


def _softplus_neg(lam):
    z = -lam
    e = jnp.exp(-jnp.abs(z))
    u = 1.0 + e
    d = u - 1.0
    l1p = jnp.where(d == 0.0, e, jnp.log(u) * (e / jnp.where(d == 0.0, 1.0, d)))
    return jnp.maximum(z, 0.0) + l1p


def _lru_gates(xc, wa, ba, wx, bx, lam):
    xcb = xc.astype(BF16)
    r = _sig(_dot_nn(xcb, wa) + ba)
    i = _sig(_dot_nn(xcb, wx) + bx)
    sp = _softplus_neg(lam)
    log_a = (-LRU_C) * r * sp
    a = jnp.exp(log_a)
    mult = jnp.sqrt(-jnp.tanh(log_a) * (1.0 + a * a))
    return xcb, r, i, sp, log_a, a, mult


def _place():
    return lax.axis_index("x"), lax.axis_index("y"), lax.axis_index("c")


def _ag_plan(shards, pieces=None, bufs=None):
    na = len(shards)

    def parts(ins, outs, sems):
        send_sems, recv_sems, local_sems = sems
        x, y, c = _place()
        me, sibling = (x, y, c), (x, y, 1 - c)
        chips = [(1 - x, y), (x, 1 - y), (1 - x, 1 - y)]

        def own(a):
            return ins[a] if pieces is None else ins[a].at[pl.ds(*pieces[a])]

        def slot(a, px, py, pc):
            idx = 4 * px + 2 * py + pc
            return outs[a].at[idx] if pieces is None else outs[a].at[idx, pl.ds(*pieces[a])]

        def copy(a, k, block, to, src=None):
            return pltpu.make_async_remote_copy(
                src_ref=slot(a, *block) if src is None else src,
                dst_ref=slot(a, *block),
                send_sem=send_sems.at[a * 7 + k],
                recv_sem=recv_sems.at[a * 7 + k],
                device_id=to,
                device_id_type=MESH,
            )

        mine = [pltpu.make_async_copy(own(a), slot(a, *me), local_sems.at[a]) for a in range(na)]
        first = []
        for a in range(na):
            first.append(copy(a, 0, me, sibling, src=own(a)))
            first += [copy(a, 1 + j, me, (*chip, c), src=own(a)) for j, chip in enumerate(chips)]
        return me, sibling, chips, c, copy, mine, first

    def start(ins, outs, sems):
        _, _, _, _, _, mine, first = parts(ins, outs, sems)
        for cp in mine + first:
            cp.start()

    def finish(ins, outs, sems):
        me, sibling, chips, c, copy, mine, first = parts(ins, outs, sems)
        passed = []
        for j, chip in enumerate(chips):
            for a in range(na):
                copy(a, 1 + j, (*chip, c), me).wait_recv()
                fwd = copy(a, 4 + j, (*chip, c), sibling)
                fwd.start()
                passed.append(fwd)
        for a in range(na):
            copy(a, 0, sibling, me).wait_recv()
            for j, chip in enumerate(chips):
                copy(a, 4 + j, (*chip, 1 - c), me).wait_recv()
        for cp in first + passed:
            cp.wait_send()
        for cp in mine:
            cp.wait()

    return types.SimpleNamespace(
        ins=list(shards) + list(bufs or []),
        out_shapes=[jax.ShapeDtypeStruct((N_DEV,) + s.shape, s.dtype) for s in shards],
        sems=[pltpu.SemaphoreType.DMA((7 * na,)), pltpu.SemaphoreType.DMA((7 * na,)),
              pltpu.SemaphoreType.DMA((na,))],
        aliases=[(na + a, a) for a in range(na)] if bufs else [],
        start=start, finish=finish)


def _rs_sibling_plan(fulls):
    na = len(fulls)
    rs = [f.shape[0] // N_DEV for f in fulls]

    def copies(ins, outs, sems):
        send_sems, recv_sems = sems
        x, y, c = _place()
        out = []
        for a in range(na):
            for q in range(4):
                shard = 2 * q + (1 - c)
                out.append(pltpu.make_async_remote_copy(
                    src_ref=ins[a].at[pl.ds(shard * rs[a], rs[a])],
                    dst_ref=outs[a].at[q],
                    send_sem=send_sems.at[a * 4 + q],
                    recv_sem=recv_sems.at[a * 4 + q],
                    device_id=(x, y, 1 - c),
                    device_id_type=MESH,
                ))
        return out

    def start(ins, outs, sems):
        for cp in copies(ins, outs, sems):
            cp.start()

    def finish(ins, outs, sems):
        for cp in copies(ins, outs, sems):
            cp.wait()

    return types.SimpleNamespace(
        ins=list(fulls),
        out_shapes=[jax.ShapeDtypeStruct((4, r) + f.shape[1:], f.dtype) for r, f in zip(rs, fulls)],
        sems=[pltpu.SemaphoreType.DMA((4 * na,)), pltpu.SemaphoreType.DMA((4 * na,))],
        start=start, finish=finish)


def _rs_chips_plan(sends, pieces=None, bufs=None):
    na = len(sends)

    def copies(ins, outs, sems):
        send_sems, recv_sems = sems
        x, y, c = _place()
        chips = [(1 - x, y), (x, 1 - y), (1 - x, 1 - y)]
        out = []
        for a in range(na):
            for k, chip in enumerate(chips):
                rows = (k,) if pieces is None else (k, pl.ds(*pieces[a]))
                out.append(pltpu.make_async_remote_copy(
                    src_ref=ins[a].at[rows],
                    dst_ref=outs[a].at[rows],
                    send_sem=send_sems.at[a * 3 + k],
                    recv_sem=recv_sems.at[a * 3 + k],
                    device_id=(*chip, c),
                    device_id_type=MESH,
                ))
        return out

    def start(ins, outs, sems):
        for cp in copies(ins, outs, sems):
            cp.start()

    def finish(ins, outs, sems):
        for cp in copies(ins, outs, sems):
            cp.wait()

    return types.SimpleNamespace(
        ins=list(sends) + list(bufs or []),
        out_shapes=[jax.ShapeDtypeStruct(s.shape, s.dtype) for s in sends],
        sems=[pltpu.SemaphoreType.DMA((3 * na,)), pltpu.SemaphoreType.DMA((3 * na,))],
        aliases=[(na + a, a) for a in range(na)] if bufs else [],
        start=start, finish=finish)


def _join(plans):
    ins, outs, sems, aliases, offs = [], [], [], [], []
    for p in plans:
        offs.append((len(ins), len(outs), len(sems)))
        aliases += [(len(ins) + ci, len(outs) + co) for ci, co in getattr(p, "aliases", [])]
        ins += p.ins
        outs += p.out_shapes
        sems += p.sems

    def cut(p, off, i, o, s):
        return (i[off[0]:off[0] + len(p.ins)], o[off[1]:off[1] + len(p.out_shapes)],
                s[off[2]:off[2] + len(p.sems)])

    def start(i, o, s):
        for p, off in zip(plans, offs):
            p.start(*cut(p, off, i, o, s))

    def finish(i, o, s):
        for p, off in zip(plans, offs):
            p.finish(*cut(p, off, i, o, s))

    def split(results):
        return [list(results[off[1]:off[1] + len(p.out_shapes)]) for p, off in zip(plans, offs)]

    return types.SimpleNamespace(ins=ins, out_shapes=outs, sems=sems, aliases=aliases,
                                 start=start, finish=finish, split=split)


def _in_hbm(args):
    return [pltpu.with_memory_space_constraint(a, pltpu.HBM) for a in args]


def _run_plan(plan, name):
    n_in, n_out = len(plan.ins), len(plan.out_shapes)

    def body(*refs):
        ins, outs, sems = refs[:n_in], refs[n_in:n_in + n_out], refs[n_in + n_out:]
        plan.start(ins, outs, sems)
        plan.finish(ins, outs, sems)

    return pl.pallas_call(
        body,
        name=name,
        in_specs=[ANY] * n_in,
        out_specs=[ANY] * n_out,
        out_shape=plan.out_shapes,
        scratch_shapes=plan.sems,
        input_output_aliases=dict(getattr(plan, "aliases", [])),
    )(*_in_hbm(plan.ins))


def _call(body, *, name, grid, in_specs, out_specs, out_shape, args, scratch_shapes=(), aliases=None,
          carry=None):
    n_in, n_out, n_scr = len(in_specs), len(out_shape), len(scratch_shapes)
    params = pltpu.CompilerParams(
        dimension_semantics=("arbitrary",) * len(grid), vmem_limit_bytes=V7X_VMEM_LIMIT_BYTES)
    if carry is None:
        outs = pl.pallas_call(
            body, name=name, grid=grid, in_specs=list(in_specs), out_specs=list(out_specs),
            out_shape=list(out_shape), scratch_shapes=list(scratch_shapes),
            input_output_aliases=aliases or {}, compiler_params=params)(*_in_hbm(args))
        return list(outs), []
    c_in, c_out = len(carry.ins), len(carry.out_shapes)

    def full(*refs):
        p = 0
        ins = refs[p:p + n_in]
        p += n_in
        cins = refs[p:p + c_in]
        p += c_in
        outs = refs[p:p + n_out]
        p += n_out
        couts = refs[p:p + c_out]
        p += c_out
        scr = refs[p:p + n_scr]
        csems = refs[p + n_scr:]
        ids = [pl.program_id(a) for a in range(len(grid))]
        first = functools.reduce(operator.and_, [i == 0 for i in ids])
        last = functools.reduce(operator.and_, [i == g - 1 for i, g in zip(ids, grid)])

        @pl.when(first)
        def _():
            carry.start(cins, couts, csems)

        body(*ins, *outs, *scr)

        @pl.when(last)
        def _():
            carry.finish(cins, couts, csems)

    all_aliases = dict(aliases or {})
    all_aliases.update({n_in + ci: n_out + co for ci, co in getattr(carry, "aliases", [])})
    outs = pl.pallas_call(
        full, name=name, grid=grid,
        in_specs=list(in_specs) + [ANY] * c_in,
        out_specs=list(out_specs) + [ANY] * c_out,
        out_shape=list(out_shape) + list(carry.out_shapes),
        scratch_shapes=list(scratch_shapes) + list(carry.sems),
        input_output_aliases=all_aliases, compiler_params=params)(*_in_hbm(args), *_in_hbm(carry.ins))
    return list(outs[:n_out]), list(outs[n_out:])


def _norm_proj(x, g1, w_int, carry=None):
    t, d = x.shape
    n = w_int.shape[0]
    tt, tn = _tile(t, 2048), _tile(n, 512)

    def body(x_ref, g_ref, w_ref, proj_ref, h1_ref, h1_s):
        @pl.when(pl.program_id(1) == 0)
        def _():
            def norm_rows(rows):
                xhat, _ = _rms_hat(x_ref[rows, :])
                h = (xhat * g_ref[...]).astype(BF16)
                h1_s[rows, :] = h
                h1_ref[rows, :] = h

            _row_chunks(tt, norm_rows)

        proj_ref[...] = _dot_nt(h1_s[...], w_ref[...]).astype(BF16)

    return _call(
        body, name="norm_proj", grid=(t // tt, n // tn),
        in_specs=[
            pl.BlockSpec((tt, d), lambda i, j: (i, 0)),
            pl.BlockSpec((1, d), lambda i, j: (0, 0)),
            pl.BlockSpec((tn, d), lambda i, j: (j, 0)),
        ],
        out_specs=[
            pl.BlockSpec((tt, tn), lambda i, j: (i, j)),
            pl.BlockSpec((tt, d), lambda i, j: (i, 0)),
        ],
        out_shape=[jax.ShapeDtypeStruct((t, n), BF16), jax.ShapeDtypeStruct((t, d), BF16)],
        scratch_shapes=[pltpu.VMEM((tt, d), BF16)],
        args=(x, g1, w_int), carry=carry)


def _fill_block_diag(w_ref, bd_ref):
    bd_ref[...] = jnp.zeros_like(bd_ref)
    hd = LRU_HEAD_DIM
    for k in range(w_ref.shape[0]):
        bd_ref[k * hd:(k + 1) * hd, k * hd:(k + 1) * hd] = w_ref[k].astype(BF16)


def _lru_fwd(proj, conv_w, conv_b, w_a, b_a, w_x, b_x, lam, carry=None):
    t = proj.shape[0]
    dr = conv_b.shape[1]
    cb = LRU_CB
    tc = _tile(t, 256)
    ncb, ntc = dr // cb, t // tc

    def body(xp_ref, g_ref, cw_ref, cb_ref, wa_ref, ba_ref, wx_ref, bx_ref, lam_ref,
             y_ref, h_ref, xc_ref, prevx_s, hlast_s, wa_s, wx_s):
        c = pl.program_id(1)

        @pl.when(c == 0)
        def _():
            prevx_s[...] = jnp.zeros_like(prevx_s)
            hlast_s[...] = jnp.zeros_like(hlast_s)
            _fill_block_diag(wa_ref, wa_s)
            _fill_block_diag(wx_ref, wx_s)

        x = xp_ref[...].astype(F32)
        prev = prevx_s[...]
        row = lax.broadcasted_iota(jnp.int32, x.shape, 0)

        def sh(j):
            return jnp.where(row >= j, pltpu.roll(x, j, 0), pltpu.roll(prev, j, 0))

        xc = (cb_ref[...] + cw_ref[0:1, :] * sh(3) + cw_ref[1:2, :] * sh(2)
              + cw_ref[2:3, :] * sh(1) + cw_ref[3:4, :] * x)
        prevx_s[...] = x
        xc_ref[...] = xc
        _, _, i, _, _, a, mult = _lru_gates(xc, wa_s[...], ba_ref[...], wx_s[...], bx_ref[...],
                                            lam_ref[...])
        av, bv = a, mult * (i * xc)
        s = 1
        while s < tc:
            a_sh = jnp.where(row >= s, pltpu.roll(av, s, 0), 1.0)
            b_sh = jnp.where(row >= s, pltpu.roll(bv, s, 0), 0.0)
            bv = av * b_sh + bv
            av = av * a_sh
            s *= 2
        h = av * hlast_s[...] + bv
        h_ref[...] = h
        hlast_s[...] = h_ref[tc - 1:tc, :]
        gel, _ = _gelu_and_grad(g_ref[...].astype(F32))
        y_ref[...] = (h * gel).astype(BF16)

    vec = pl.BlockSpec((1, cb), lambda j, c: (0, j))
    blk = pl.BlockSpec((tc, cb), lambda j, c: (c, j))
    mat = pl.BlockSpec((cb // LRU_HEAD_DIM, LRU_HEAD_DIM, LRU_HEAD_DIM), lambda j, c: (j, 0, 0))
    return _call(
        body, name="lru_fwd", grid=(ncb, ntc),
        in_specs=[
            blk,
            pl.BlockSpec((tc, cb), lambda j, c: (c, ncb + j)),
            pl.BlockSpec((4, cb), lambda j, c: (0, j)),
            vec, mat, vec, mat, vec, vec,
        ],
        out_specs=[blk, blk, blk],
        out_shape=[
            jax.ShapeDtypeStruct((t, dr), BF16),
            jax.ShapeDtypeStruct((t, dr), F32),
            jax.ShapeDtypeStruct((t, dr), F32),
        ],
        scratch_shapes=[pltpu.VMEM((tc, cb), F32), pltpu.VMEM((1, cb), F32),
                        pltpu.VMEM((cb, cb), BF16), pltpu.VMEM((cb, cb), BF16)],
        args=(proj, proj, conv_w, conv_b, w_a, b_a, w_x, b_x, lam), carry=carry)


def _pool_select(col, vals):
    out = vals[3]
    for g in (2, 1, 0):
        out = jnp.where(col < (g + 1) * POOL_GROUP_DIM, vals[g], out)
    return out


def _pool_fwd(proj, pool_w, pool_scale, col_block):
    t = proj.shape[0]
    dp = pool_scale.shape[1]
    tc = _tile(t, 256)
    ntc = t // tc

    def body(x_ref, w_ref, sc_ref, y_ref, p_ref, px, p2, p4, p8):
        c = pl.program_id(0)

        @pl.when(c == 0)
        def _():
            for s in (px, p2, p4, p8):
                s[...] = jnp.zeros_like(s)

        x = x_ref[...].astype(F32)
        row = lax.broadcasted_iota(jnp.int32, x.shape, 0)
        col = lax.broadcasted_iota(jnp.int32, x.shape, 1)

        def sh(v, pv, j):
            return jnp.where(row >= j, pltpu.roll(v, j, 0), pltpu.roll(pv[...], j, 0))

        s2 = x + sh(x, px, 1)
        s4 = s2 + sh(s2, p2, 2)
        s8 = s4 + sh(s4, p4, 4)
        s16 = s8 + sh(s8, p8, 8)
        px[...] = x
        p2[...] = s2
        p4[...] = s4
        p8[...] = s8
        wsum = _pool_select(col, (s2, s4, s8, s16))
        win = _pool_select(col, POOL_WINDOWS)
        cnt = jnp.minimum(c * tc + row + 1, win).astype(F32)
        p = wsum / cnt - x
        pb = p.astype(BF16)
        p_ref[...] = pb
        for g in range(len(POOL_WINDOWS)):
            sl = slice(g * POOL_GROUP_DIM, (g + 1) * POOL_GROUP_DIM)
            yg = _dot_nn(pb[:, sl], w_ref[g]) * sc_ref[:, sl]
            y_ref[:, sl] = yg.astype(BF16)

    return _call(
        body, name="pool_fwd", grid=(ntc,),
        in_specs=[
            pl.BlockSpec((tc, dp), lambda c: (c, col_block)),
            pl.BlockSpec(pool_w.shape, lambda c: (0, 0, 0)),
            pl.BlockSpec((1, dp), lambda c: (0, 0)),
        ],
        out_specs=[pl.BlockSpec((tc, dp), lambda c: (c, 0))] * 2,
        out_shape=[jax.ShapeDtypeStruct((t, dp), BF16)] * 2,
        scratch_shapes=[pltpu.VMEM((tc, dp), F32)] * 4,
        args=(proj, pool_w, pool_scale))[0]


def _branch_mix(y_lru, y_pool, w_lru_up, w_pool_upt, proj, b_gate, ga_block, gb_block, carry=None):
    t, d = y_lru.shape
    dp = y_pool.shape[1]
    tt, tn = _tile(t, 1024), 512
    nj = d // tn

    def body(yl_ref, yp_ref, wl_ref, wp_ref, ga_ref, gb_ref, ba_ref, bb_ref, bra_ref, brb_ref, mix_ref):
        br_a = _dot_nn(yl_ref[...], wl_ref[...])
        br_b = _dot_nt(yp_ref[...], wp_ref[...])
        bra_ref[...] = br_a.astype(BF16)
        brb_ref[...] = br_b.astype(BF16)
        ga = _sig(ga_ref[...].astype(F32) + ba_ref[...])
        gb = _sig(gb_ref[...].astype(F32) + bb_ref[...])
        mix_ref[...] = (ga * br_a + gb * br_b).astype(BF16)

    out = pl.BlockSpec((tt, tn), lambda j, i: (i, j))
    return _call(
        body, name="branch_mix", grid=(nj, t // tt),
        in_specs=[
            pl.BlockSpec((tt, d), lambda j, i: (i, 0)),
            pl.BlockSpec((tt, dp), lambda j, i: (i, 0)),
            pl.BlockSpec((d, tn), lambda j, i: (0, j)),
            pl.BlockSpec((tn, dp), lambda j, i: (j, 0)),
            pl.BlockSpec((tt, tn), lambda j, i: (i, ga_block + j)),
            pl.BlockSpec((tt, tn), lambda j, i: (i, gb_block + j)),
            pl.BlockSpec((1, tn), lambda j, i: (0, j)),
            pl.BlockSpec((1, tn), lambda j, i: (0, nj + j)),
        ],
        out_specs=[out, out, out],
        out_shape=[jax.ShapeDtypeStruct((t, d), BF16)] * 3,
        args=(y_lru, y_pool, w_lru_up, w_pool_upt, proj, proj, b_gate, b_gate), carry=carry)


def _wo_norm(mix, w_o, x, g2, g3, carry=None):
    t, d = x.shape
    tt = _tile(t, 512)

    def body(mix_ref, w_ref, x_ref, g2_ref, g3_ref, m_ref, x2_ref, h3_ref):
        m = _dot_nn(mix_ref[...], w_ref[...])
        m_ref[...] = m
        mhat, _ = _rms_hat(m)
        x2 = x_ref[...] + mhat * g2_ref[...]
        x2_ref[...] = x2
        xhat, _ = _rms_hat(x2)
        h3_ref[...] = (xhat * g3_ref[...]).astype(BF16)

    row = pl.BlockSpec((tt, d), lambda i: (i, 0))
    vec = pl.BlockSpec((1, d), lambda i: (0, 0))
    return _call(
        body, name="wo_norm", grid=(t // tt,),
        in_specs=[row, pl.BlockSpec((d, d), lambda i: (0, 0)), row, vec, vec],
        out_specs=[row, row, row],
        out_shape=[
            jax.ShapeDtypeStruct((t, d), F32),
            jax.ShapeDtypeStruct((t, d), F32),
            jax.ShapeDtypeStruct((t, d), BF16),
        ],
        args=(mix, w_o, x, g2, g3), carry=carry)


def _ff1(h3, w_ff1t, carry=None):
    t, d = h3.shape
    n = w_ff1t.shape[0]
    tt, tn = _tile(t, 2048), _tile(n, 512)

    def body(h_ref, w_ref, rf_ref):
        rf_ref[...] = jnp.maximum(_dot_nt(h_ref[...], w_ref[...]), 0.0).astype(BF16)

    out = pl.BlockSpec((tt, tn), lambda i, j: (i, j))
    return _call(
        body, name="ff1", grid=(t // tt, n // tn),
        in_specs=[pl.BlockSpec((tt, d), lambda i, j: (i, 0)), pl.BlockSpec((tn, d), lambda i, j: (j, 0))],
        out_specs=[out],
        out_shape=[jax.ShapeDtypeStruct((t, n), BF16)],
        args=(h3, w_ff1t), carry=carry)


def _ff2_loss(rf, w_ff2, x2, g4, target):
    t, k = rf.shape
    d = x2.shape[1]
    tt, tk = _tile(t, 1024), _tile(k, 512)
    nk = k // tk

    def body(a_ref, w_ref, x2_ref, g_ref, tg_ref, dy_ref, df_ref, dg_ref, loss_ref, acc):
        i, kk = pl.program_id(0), pl.program_id(1)

        @pl.when(kk == 0)
        def _():
            acc[...] = jnp.zeros_like(acc)

        @pl.when((i == 0) & (kk == 0))
        def _():
            dg_ref[...] = jnp.zeros_like(dg_ref)
            loss_ref[...] = jnp.zeros_like(loss_ref)

        rf_tile = a_ref[...]
        acc[...] += _dot_nn(rf_tile * rf_tile, w_ref[...])

        @pl.when(kk == nk - 1)
        def _():
            def tail(rows):
                fhat, r = _rms_hat(acc[rows, :])
                g = g_ref[...]
                e = x2_ref[rows, :] + fhat * g - tg_ref[rows, :]
                loss_ref[...] += 0.5 * jnp.sum(jnp.mean(e * e, axis=-1, keepdims=True))
                dy = e * (1.0 / d)
                dy_ref[rows, :] = dy.astype(BF16)
                df, dg = _rms_bwd(dy, fhat, r, g)
                df_ref[rows, :] = df.astype(BF16)
                dg_ref[...] += dg

            _row_chunks(tt, tail)

    row = pl.BlockSpec((tt, d), lambda i, kk: (i, 0))
    vec = pl.BlockSpec((1, d), lambda i, kk: (0, 0))
    return _call(
        body, name="ff2_loss", grid=(t // tt, nk),
        in_specs=[
            pl.BlockSpec((tt, tk), lambda i, kk: (i, kk)),
            pl.BlockSpec((tk, d), lambda i, kk: (kk, 0)),
            row, vec, row,
        ],
        out_specs=[row, row, vec, pl.BlockSpec((1, 128), lambda i, kk: (0, 0))],
        out_shape=[
            jax.ShapeDtypeStruct((t, d), BF16),
            jax.ShapeDtypeStruct((t, d), BF16),
            jax.ShapeDtypeStruct((1, d), F32),
            jax.ShapeDtypeStruct((1, 128), F32),
        ],
        scratch_shapes=[pltpu.VMEM((tt, d), F32)],
        args=(rf, w_ff2, x2, g4, target))[0]


def _ff2_bwd(df, w_ff2, rf):
    t, d = df.shape
    n = w_ff2.shape[0]
    tt, tn = _tile(t, 2048), _tile(n, 512)

    def body(df_ref, w_ref, rf_ref, out_ref):
        d_act = _dot_nt(df_ref[...], w_ref[...])
        out_ref[...] = (d_act * (2.0 * rf_ref[...].astype(F32))).astype(BF16)

    blk = pl.BlockSpec((tt, tn), lambda i, j: (i, j))
    return _call(
        body, name="ff2_bwd", grid=(t // tt, n // tn),
        in_specs=[pl.BlockSpec((tt, d), lambda i, j: (i, 0)), pl.BlockSpec((tn, d), lambda i, j: (j, 0)), blk],
        out_specs=[blk],
        out_shape=[jax.ShapeDtypeStruct((t, n), BF16)],
        args=(df, w_ff2, rf))[0][0]


def _wgrad(a, b, name, prev=None, row_off=0, rows=None, carry=None, square_a=False):
    t, m = a.shape
    n = b.shape[1]
    rows = m if rows is None else rows
    tm, tk = _tile(m, 512), _tile(t, 2048)
    nk = t // tk
    assert row_off % tm == 0
    off = row_off // tm

    def body(*refs):
        a_ref, b_ref = refs[0], refs[1]
        o32_ref, o16_ref, acc = refs[-3], refs[-2], refs[-1]
        kk = pl.program_id(1)

        @pl.when(kk == 0)
        def _():
            acc[...] = jnp.zeros_like(acc)

        a_tile = a_ref[...]
        acc[...] += _dot_tn(a_tile * a_tile if square_a else a_tile, b_ref[...])

        @pl.when(kk == nk - 1)
        def _():
            o32_ref[...] = acc[...]
            o16_ref[...] = acc[...].astype(BF16)

    in_specs = [pl.BlockSpec((tk, tm), lambda i, kk: (kk, i)), pl.BlockSpec((tk, n), lambda i, kk: (kk, 0))]
    args = [a, b]
    aliases = {}
    if prev is not None:
        in_specs += [ANY, ANY]
        args += list(prev)
        aliases = {2: 0, 3: 1}
    out = pl.BlockSpec((tm, n), lambda i, kk: (off + i, 0))
    return _call(
        body, name=name, grid=(m // tm, nk),
        in_specs=in_specs, out_specs=[out, out],
        out_shape=[jax.ShapeDtypeStruct((rows, n), F32), jax.ShapeDtypeStruct((rows, n), BF16)],
        scratch_shapes=[pltpu.VMEM((tm, n), F32)],
        aliases=aliases, args=args, carry=carry)


def _wgrad_parts(parts, b, name, carry=None):
    t, n = b.shape
    tm = 512
    bounds = []
    lo = 0
    for part in parts:
        assert part.shape[0] == t and part.shape[1] % tm == 0
        bounds.append((lo, lo + part.shape[1] // tm))
        lo += part.shape[1] // tm
    nm = lo
    np_ = len(parts)

    def body(*refs):
        p_refs, b_ref, o32_ref, o16_ref = refs[:np_], refs[np_], refs[np_ + 1], refs[np_ + 2]
        i = pl.program_id(0)
        for (lo_p, hi_p), p_ref in zip(bounds, p_refs):
            @pl.when((i >= lo_p) & (i < hi_p))
            def _(p_ref=p_ref):
                res = _dot_tn(p_ref[...], b_ref[...])
                o32_ref[...] = res
                o16_ref[...] = res.astype(BF16)

    def part_spec(lo_p, hi_p):
        return pl.BlockSpec((t, tm), lambda i: (0, jnp.clip(i - lo_p, 0, hi_p - lo_p - 1)))

    out = pl.BlockSpec((tm, n), lambda i: (i, 0))
    return _call(
        body, name=name, grid=(nm,),
        in_specs=[part_spec(lo_p, hi_p) for lo_p, hi_p in bounds] + [pl.BlockSpec((t, n), lambda i: (0, 0))],
        out_specs=[out, out],
        out_shape=[jax.ShapeDtypeStruct((nm * tm, n), F32), jax.ShapeDtypeStruct((nm * tm, n), BF16)],
        args=(*parts, b), carry=carry)


def _ff1_bwd_norms(d_f1, w_ff1t, dy, x2, g3, m, g2, carry=None):
    t, k = d_f1.shape
    d = x2.shape[1]
    tt, tk = _tile(t, 1024), _tile(k, 512)
    nk = k // tk

    def body(a_ref, w_ref, dy_ref, x2_ref, g3_ref, m_ref, g2_ref, dx2_ref, dm_ref, dg3_ref, dg2_ref, acc):
        i, kk = pl.program_id(0), pl.program_id(1)

        @pl.when(kk == 0)
        def _():
            acc[...] = jnp.zeros_like(acc)

        @pl.when((i == 0) & (kk == 0))
        def _():
            dg3_ref[...] = jnp.zeros_like(dg3_ref)
            dg2_ref[...] = jnp.zeros_like(dg2_ref)

        acc[...] += _dot_nn(a_ref[...], w_ref[...])

        @pl.when(kk == nk - 1)
        def _():
            def tail(rows):
                xhat, r3 = _rms_hat(x2_ref[rows, :])
                dx, dg3 = _rms_bwd(acc[rows, :], xhat, r3, g3_ref[...])
                dx2 = dy_ref[rows, :].astype(F32) + dx
                dx2_ref[rows, :] = dx2
                dg3_ref[...] += dg3
                mhat, r2 = _rms_hat(m_ref[rows, :])
                dm, dg2 = _rms_bwd(dx2, mhat, r2, g2_ref[...])
                dm_ref[rows, :] = dm.astype(BF16)
                dg2_ref[...] += dg2

            _row_chunks(tt, tail)

    row = pl.BlockSpec((tt, d), lambda i, kk: (i, 0))
    vec = pl.BlockSpec((1, d), lambda i, kk: (0, 0))
    return _call(
        body, name="ff1_bwd_norms", grid=(t // tt, nk),
        in_specs=[
            pl.BlockSpec((tt, tk), lambda i, kk: (i, kk)),
            pl.BlockSpec((tk, d), lambda i, kk: (kk, 0)),
            row, row, vec, row, vec,
        ],
        out_specs=[row, row, vec, vec],
        out_shape=[
            jax.ShapeDtypeStruct((t, d), F32),
            jax.ShapeDtypeStruct((t, d), BF16),
            jax.ShapeDtypeStruct((1, d), F32),
            jax.ShapeDtypeStruct((1, d), F32),
        ],
        scratch_shapes=[pltpu.VMEM((tt, d), F32)],
        args=(d_f1, w_ff1t, dy, x2, g3, m, g2), carry=carry)


def _wo_bwd_mix(dm, w_o, br_a, br_b, proj, b_gate, ga_block, gb_block, carry=None):
    t, d = dm.shape
    tt, tn = _tile(t, 1024), 512
    nj = d // tn

    def body(dm_ref, w_ref, bra_ref, brb_ref, ga_ref, gb_ref, ba_ref, bb_ref,
             dbra_ref, dbrb_ref, dga_ref, dgb_ref, dba_ref, dbb_ref):
        i = pl.program_id(1)

        @pl.when(i == 0)
        def _():
            dba_ref[...] = jnp.zeros_like(dba_ref)
            dbb_ref[...] = jnp.zeros_like(dbb_ref)

        d_mix = _dot_nt(dm_ref[...], w_ref[...])
        ga = _sig(ga_ref[...].astype(F32) + ba_ref[...])
        gb = _sig(gb_ref[...].astype(F32) + bb_ref[...])
        dbra_ref[...] = (d_mix * ga).astype(BF16)
        dbrb_ref[...] = (d_mix * gb).astype(BF16)
        dga = d_mix * bra_ref[...].astype(F32) * (ga * (1.0 - ga))
        dgb = d_mix * brb_ref[...].astype(F32) * (gb * (1.0 - gb))
        dga_ref[...] = dga.astype(BF16)
        dgb_ref[...] = dgb.astype(BF16)
        dba_ref[...] += jnp.sum(dga, axis=0, keepdims=True)
        dbb_ref[...] += jnp.sum(dgb, axis=0, keepdims=True)

    blk = pl.BlockSpec((tt, tn), lambda j, i: (i, j))
    vec = pl.BlockSpec((1, tn), lambda j, i: (0, j))
    return _call(
        body, name="wo_bwd_mix", grid=(nj, t // tt),
        in_specs=[
            pl.BlockSpec((tt, d), lambda j, i: (i, 0)),
            pl.BlockSpec((tn, d), lambda j, i: (j, 0)),
            blk, blk,
            pl.BlockSpec((tt, tn), lambda j, i: (i, ga_block + j)),
            pl.BlockSpec((tt, tn), lambda j, i: (i, gb_block + j)),
            vec,
            pl.BlockSpec((1, tn), lambda j, i: (0, nj + j)),
        ],
        out_specs=[blk, blk, blk, blk, vec, vec],
        out_shape=[jax.ShapeDtypeStruct((t, d), BF16)] * 4 + [jax.ShapeDtypeStruct((1, d), F32)] * 2,
        args=(dm, w_o, br_a, br_b, proj, proj, b_gate, b_gate), carry=carry)


def _lru_up_bwd(d_br_a, w_lru_up, proj, h, g_block, carry=None):
    t, d = d_br_a.shape
    tt, tn = _tile(t, 1024), 512

    def body(a_ref, w_ref, g_ref, h_ref, dh_ref, dg_ref):
        d_y = _dot_nt(a_ref[...], w_ref[...])
        gel, gel_grad = _gelu_and_grad(g_ref[...].astype(F32))
        dh_ref[...] = d_y * gel
        dg_ref[...] = (d_y * h_ref[...] * gel_grad).astype(BF16)

    blk = pl.BlockSpec((tt, tn), lambda i, j: (i, j))
    return _call(
        body, name="lru_up_bwd", grid=(t // tt, d // tn),
        in_specs=[
            pl.BlockSpec((tt, d), lambda i, j: (i, 0)),
            pl.BlockSpec((tn, d), lambda i, j: (j, 0)),
            pl.BlockSpec((tt, tn), lambda i, j: (i, g_block + j)),
            blk,
        ],
        out_specs=[blk, blk],
        out_shape=[jax.ShapeDtypeStruct((t, d), F32), jax.ShapeDtypeStruct((t, d), BF16)],
        args=(d_br_a, w_lru_up, proj, h), carry=carry)


def _pool_up_bwd(d_br_b, w_pool_upt):
    t, d = d_br_b.shape
    dp = w_pool_upt.shape[1]
    tt = _tile(t, 2048)

    def body(a_ref, w_ref, out_ref):
        out_ref[...] = _dot_nn(a_ref[...], w_ref[...])

    return _call(
        body, name="pool_up_bwd", grid=(t // tt,),
        in_specs=[pl.BlockSpec((tt, d), lambda i: (i, 0)), pl.BlockSpec((d, dp), lambda i: (0, 0))],
        out_specs=[pl.BlockSpec((tt, dp), lambda i: (i, 0))],
        out_shape=[jax.ShapeDtypeStruct((t, dp), F32)],
        args=(d_br_b, w_pool_upt))[0][0]


def _lru_bwd(dh, xc, h, proj, conv_w, w_a, b_a, w_x, b_x, lam, carry=None):
    t, dr = dh.shape
    cb = LRU_CB
    hd = LRU_HEAD_DIM
    per = cb // hd
    tc = _tile(t, 256)
    ncb, ntc = dr // cb, t // tc

    def body(dh_ref, xc_ref, h_ref, hp_ref, xp_ref, cw_ref, wa_ref, ba_ref, wx_ref, bx_ref, lam_ref,
             dxp_ref, dwa_ref, dba_ref, dwx_ref, dbx_ref, dlam_ref, dcw_ref, dcb_ref,
             nextd_s, anext_s, gnext_s, tmp_s, wa_s, wx_s):
        c = pl.program_id(1)
        rc = ntc - 1 - c

        @pl.when(c == 0)
        def _():
            nextd_s[...] = jnp.zeros_like(nextd_s)
            anext_s[...] = jnp.zeros_like(anext_s)
            gnext_s[...] = jnp.zeros_like(gnext_s)
            for ref in (dwa_ref, dba_ref, dwx_ref, dbx_ref, dlam_ref, dcw_ref, dcb_ref):
                ref[...] = jnp.zeros_like(ref)
            _fill_block_diag(wa_ref, wa_s)
            _fill_block_diag(wx_ref, wx_s)

        xc = xc_ref[...]
        wa, wx, lam = wa_s[...], wx_s[...], lam_ref[...]
        xcb, r, i, sp, log_a, a, mult = _lru_gates(xc, wa, ba_ref[...], wx, bx_ref[...], lam)
        row = lax.broadcasted_iota(jnp.int32, xc.shape, 0)
        h = h_ref[...]
        hp = jnp.where(rc == 0, 0.0, hp_ref[...])
        hprev = jnp.where(row >= 1, pltpu.roll(h, 1, 0), pltpu.roll(hp, 1, 0))

        def up(v, nv, j):
            return jnp.where(row < tc - j, pltpu.roll(v, tc - j, 0), nv)

        av = up(a, anext_s[...], 1)
        bv = dh_ref[...]
        s = 1
        while s < tc:
            a_sh = up(av, 1.0, s)
            b_sh = up(bv, 0.0, s)
            bv = av * b_sh + bv
            av = av * a_sh
            s *= 2
        gt = av * gnext_s[...] + bv
        tmp_s[...] = gt
        gnext_s[...] = tmp_s[0:1, :]
        tmp_s[...] = a
        anext_s[...] = tmp_s[0:1, :]

        da = gt * hprev
        ixc = i * xc
        d_mult = gt * ixc
        d_i = gt * mult * xc
        d_xc = gt * mult * i
        d_log_a = da * a - d_mult * (a * a) / mult
        d_pre_r = (d_log_a * ((-LRU_C) * sp)) * (r * (1.0 - r))
        d_pre_i = d_i * (i * (1.0 - i))
        d_sp = jnp.sum(d_log_a * ((-LRU_C) * r), axis=0, keepdims=True)
        dlam_ref[...] += d_sp * (-1.0 / (1.0 + jnp.exp(lam)))
        dpr = d_pre_r.astype(BF16)
        dpi = d_pre_i.astype(BF16)
        dba_ref[...] += jnp.sum(d_pre_r, axis=0, keepdims=True)
        dbx_ref[...] += jnp.sum(d_pre_i, axis=0, keepdims=True)
        pa = _dot_tn(xcb, dpr)
        px = _dot_tn(xcb, dpi)
        for k in range(per):
            dwa_ref[k] += pa[k * hd:(k + 1) * hd, k * hd:(k + 1) * hd]
            dwx_ref[k] += px[k * hd:(k + 1) * hd, k * hd:(k + 1) * hd]
        d_xc = d_xc + _dot_nt(dpr, wa) + _dot_nt(dpi, wx)

        nxt = nextd_s[...]
        xp = xp_ref[...].astype(F32)
        dxp = cw_ref[3:4, :] * d_xc
        dcw_ref[3:4, :] += jnp.sum(xp * d_xc, axis=0, keepdims=True)
        for j in (1, 2, 3):
            uj = up(d_xc, pltpu.roll(nxt, tc - j, 0), j)
            dxp = dxp + cw_ref[3 - j:4 - j, :] * uj
            dcw_ref[3 - j:4 - j, :] += jnp.sum(xp * uj, axis=0, keepdims=True)
        dcb_ref[...] += jnp.sum(d_xc, axis=0, keepdims=True)
        nextd_s[...] = d_xc
        dxp_ref[...] = dxp.astype(BF16)

    vec = pl.BlockSpec((1, cb), lambda j, c: (0, j))
    blk = pl.BlockSpec((tc, cb), lambda j, c: (ntc - 1 - c, j))
    mat = pl.BlockSpec((per, hd, hd), lambda j, c: (j, 0, 0))
    cwb = pl.BlockSpec((4, cb), lambda j, c: (0, j))
    return _call(
        body, name="lru_bwd", grid=(ncb, ntc),
        in_specs=[
            blk, blk, blk,
            pl.BlockSpec((tc, cb), lambda j, c: (jnp.maximum(ntc - 2 - c, 0), j)),
            blk, cwb, mat, vec, mat, vec, vec,
        ],
        out_specs=[blk, mat, vec, mat, vec, vec, cwb, vec],
        out_shape=[
            jax.ShapeDtypeStruct((t, dr), BF16),
            jax.ShapeDtypeStruct(w_a.shape, F32),
            jax.ShapeDtypeStruct((1, dr), F32),
            jax.ShapeDtypeStruct(w_x.shape, F32),
            jax.ShapeDtypeStruct((1, dr), F32),
            jax.ShapeDtypeStruct((1, dr), F32),
            jax.ShapeDtypeStruct((4, dr), F32),
            jax.ShapeDtypeStruct((1, dr), F32),
        ],
        scratch_shapes=[
            pltpu.VMEM((tc, cb), F32),
            pltpu.VMEM((1, cb), F32),
            pltpu.VMEM((1, cb), F32),
            pltpu.VMEM((tc, cb), F32),
            pltpu.VMEM((cb, cb), BF16),
            pltpu.VMEM((cb, cb), BF16),
        ],
        args=(dh, xc, h, h, proj, conv_w, w_a, b_a, w_x, b_x, lam), carry=carry)


def _pool_bwd(d_y_pool, p, pool_w, pool_scale):
    t, dp = d_y_pool.shape
    tc = _tile(t, 256)
    ntc = t // tc
    ng = len(POOL_WINDOWS)

    def body(dy_ref, p_ref, w_ref, sc_ref, dx_ref, dw_ref, dsc_ref, nz, n2, n4, n8, dp_s):
        c = pl.program_id(0)
        rc = ntc - 1 - c

        @pl.when(c == 0)
        def _():
            for s in (nz, n2, n4, n8):
                s[...] = jnp.zeros_like(s)
            dw_ref[...] = jnp.zeros_like(dw_ref)
            dsc_ref[...] = jnp.zeros_like(dsc_ref)

        for g in range(ng):
            sl = slice(g * POOL_GROUP_DIM, (g + 1) * POOL_GROUP_DIM)
            pg = p_ref[:, sl]
            dyg = dy_ref[:, sl]
            wg = w_ref[g].astype(BF16)
            q = _dot_nn(pg, wg)
            dsc_ref[:, sl] += jnp.sum(dyg * q, axis=0, keepdims=True)
            dpw = (dyg * sc_ref[:, sl]).astype(BF16)
            dw_ref[g] += _dot_tn(pg, dpw)
            dp_s[:, sl] = _dot_nt(dpw, wg)

        dpv = dp_s[...]
        row = lax.broadcasted_iota(jnp.int32, dpv.shape, 0)
        col = lax.broadcasted_iota(jnp.int32, dpv.shape, 1)
        win = _pool_select(col, POOL_WINDOWS)
        cnt = jnp.minimum(rc * tc + row + 1, win).astype(F32)
        z = dpv / cnt

        def up(v, nv, j):
            return jnp.where(row < tc - j, pltpu.roll(v, tc - j, 0), pltpu.roll(nv[...], tc - j, 0))

        u2 = z + up(z, nz, 1)
        u4 = u2 + up(u2, n2, 2)
        u8 = u4 + up(u4, n4, 4)
        u16 = u8 + up(u8, n8, 8)
        nz[...] = z
        n2[...] = u2
        n4[...] = u4
        n8[...] = u8
        dx_ref[...] = (_pool_select(col, (u2, u4, u8, u16)) - dpv).astype(BF16)

    blk = pl.BlockSpec((tc, dp), lambda c: (ntc - 1 - c, 0))
    full_w = pl.BlockSpec(pool_w.shape, lambda c: (0, 0, 0))
    vec = pl.BlockSpec((1, dp), lambda c: (0, 0))
    return _call(
        body, name="pool_bwd", grid=(ntc,),
        in_specs=[blk, blk, full_w, vec],
        out_specs=[blk, full_w, vec],
        out_shape=[
            jax.ShapeDtypeStruct((t, dp), BF16),
            jax.ShapeDtypeStruct(pool_w.shape, F32),
            jax.ShapeDtypeStruct((1, dp), F32),
        ],
        scratch_shapes=[pltpu.VMEM((tc, dp), F32)] * 5,
        args=(d_y_pool, p, pool_w, pool_scale))[0]


def _win_bwd_norm(parts, w_int, dx2, x, g1, carry=None):
    t, d = x.shape
    tk = 512
    tt = _tile(t, 1024)
    bounds = []
    k0 = 0
    for part in parts:
        assert part.shape[1] % tk == 0
        bounds.append((k0, k0 + part.shape[1] // tk))
        k0 += part.shape[1] // tk
    nk = k0
    assert nk * tk == w_int.shape[0]
    np_ = len(parts)

    def body(*refs):
        p_refs = refs[:np_]
        w_ref, dx2_ref, x_ref, g_ref, gx_ref, dg_ref, acc = refs[np_:]
        i, kk = pl.program_id(0), pl.program_id(1)

        @pl.when(kk == 0)
        def _():
            acc[...] = jnp.zeros_like(acc)

        @pl.when((i == 0) & (kk == 0))
        def _():
            dg_ref[...] = jnp.zeros_like(dg_ref)

        for (lo, hi), p_ref in zip(bounds, p_refs):
            @pl.when((kk >= lo) & (kk < hi))
            def _(p_ref=p_ref):
                acc[...] += _dot_nn(p_ref[...], w_ref[...])

        @pl.when(kk == nk - 1)
        def _():
            def tail(rows):
                xhat, r = _rms_hat(x_ref[rows, :])
                dx, dg = _rms_bwd(acc[rows, :], xhat, r, g_ref[...])
                gx_ref[rows, :] = dx2_ref[rows, :] + dx
                dg_ref[...] += dg

            _row_chunks(tt, tail)

    def part_spec(lo, hi):
        return pl.BlockSpec((tt, tk), lambda i, kk: (i, jnp.clip(kk - lo, 0, hi - lo - 1)))

    row = pl.BlockSpec((tt, d), lambda i, kk: (i, 0))
    vec = pl.BlockSpec((1, d), lambda i, kk: (0, 0))
    return _call(
        body, name="win_bwd_norm", grid=(t // tt, nk),
        in_specs=[part_spec(lo, hi) for lo, hi in bounds]
        + [pl.BlockSpec((tk, d), lambda i, kk: (kk, 0)), row, row, vec],
        out_specs=[row, vec],
        out_shape=[jax.ShapeDtypeStruct((t, d), F32), jax.ShapeDtypeStruct((1, d), F32)],
        scratch_shapes=[pltpu.VMEM((tt, d), F32)],
        args=(*parts, w_int, dx2, x, g1), carry=carry)


def _adam_math(w, g, m, v):
    m = ADAM_B1 * m + (1.0 - ADAM_B1) * g
    v = ADAM_B2 * v + (1.0 - ADAM_B2) * (g * g)
    m_hat = m / (1.0 - ADAM_B1 ** ADAM_STEP)
    v_hat = v / (1.0 - ADAM_B2 ** ADAM_STEP)
    delta = -ADAM_LR * (m_hat / (jnp.sqrt(v_hat) + ADAM_EPS) + ADAM_WD * w)
    return delta, m, v


def _adamw_big(ws, gs, ms, vs, carry=None):
    n = len(ws)
    nb = 8

    def body(*refs):
        for a in range(n):
            w_ref, g_ref, m_ref, v_ref = refs[4 * a:4 * a + 4]
            d_ref, nm_ref, nv_ref = refs[4 * n + 3 * a:4 * n + 3 * a + 3]
            dl, m, v = _adam_math(w_ref[...], g_ref[...], m_ref[...], v_ref[...])
            d_ref[...] = dl
            nm_ref[...] = m
            nv_ref[...] = v

    in_specs, out_specs, out_shape, args = [], [], [], []
    for w, g, m, v in zip(ws, gs, ms, vs):
        rows, cols = w.shape
        blk = pl.BlockSpec((rows // nb, cols), lambda i: (i, 0))
        in_specs += [blk] * 4
        args += [w, g, m, v]
        out_specs += [blk] * 3
        out_shape += [jax.ShapeDtypeStruct(w.shape, F32)] * 3
    outs, got = _call(body, name="adamw_big", grid=(nb,), in_specs=in_specs, out_specs=out_specs,
                      out_shape=out_shape, args=args, carry=carry)
    return [tuple(outs[3 * a:3 * a + 3]) for a in range(n)], got


SMALL_ORDER = ("norm_mix_pre", "norm_mix_post", "norm_mlp_pre", "norm_mlp_post", "b_gate", "conv_w", "conv_b",
               "lru_w_a", "lru_b_a", "lru_w_x", "lru_b_x", "lru_lambda", "pool_w", "pool_scale")
VEC_ROW = dict(norm_mix_pre=0, norm_mix_post=1, norm_mlp_pre=2, norm_mlp_post=3, conv_b=6, lru_b_a=7,
               lru_b_x=8, lru_lambda=9)
ROW_B_GATE, ROW_POOL_SCALE, ROW_CONV_W, ROW_LOSS, N_VEC_ROWS = 4, 10, 11, 15, 16


def _adamw_small(vec_parts, g_pool, g_wa, g_wx, me, params):
    d = vec_parts.shape[2]
    names = SMALL_ORDER
    n = len(names)
    cw_cols = params["conv_w"][0].shape[2]

    def body(me_ref, vec_ref, vecc_ref, gp_ref, gwa_ref, gwx_ref, *refs):
        wmv = refs[:3 * n]
        loss_ref = refs[3 * n]
        outs = refs[3 * n + 1:3 * n + 1 + 4 * n]
        vs, vsc = refs[3 * n + 1 + 4 * n:]
        acc, accc = vec_ref[0], vecc_ref[0]
        for k in range(1, N_DEV):
            acc = acc + vec_ref[k]
            accc = accc + vecc_ref[k]
        vs[...] = acc
        vsc[...] = accc
        loss_ref[...] = vs[ROW_LOSS:ROW_LOSS + 1, 0:128]

        def upd(a, g, idx):
            w_ref, m_ref, v_ref = wmv[3 * a:3 * a + 3]
            g_ref, d_ref, nm_ref, nv_ref = outs[4 * a:4 * a + 4]
            dl, m, v = _adam_math(w_ref[idx], g, m_ref[idx], v_ref[idx])
            g_ref[idx] = g
            d_ref[idx] = dl
            nm_ref[idx] = m
            nv_ref[idx] = v

        for a, name in enumerate(names):
            if name in VEC_ROW:
                r = VEC_ROW[name]
                upd(a, vs[r:r + 1, :], (slice(None), slice(None)))
            elif name == "b_gate":
                for half in range(2):
                    r = ROW_B_GATE + half
                    upd(a, vs[r:r + 1, :], (slice(None), slice(half * d, (half + 1) * d)))
            elif name == "pool_scale":
                width = params[name][0].shape[1]
                upd(a, vs[ROW_POOL_SCALE:ROW_POOL_SCALE + 1, 0:width], (slice(None), slice(None)))
            elif name == "conv_w":
                upd(a, vsc[ROW_CONV_W:ROW_CONV_W + 4, :], (0,))
            elif name == "pool_w":
                upd(a, gp_ref[...], (Ellipsis,))
            elif name == "lru_w_a":
                upd(a, gwa_ref[...], (Ellipsis,))
            elif name == "lru_w_x":
                upd(a, gwx_ref[...], (Ellipsis,))
            else:
                raise ValueError(name)

    def whole(shape):
        nd = len(shape)
        return pl.BlockSpec(tuple(shape), lambda i, me_ref: (0,) * nd)

    in_specs = [
        whole(vec_parts.shape),
        pl.BlockSpec((N_DEV, N_VEC_ROWS, cw_cols), lambda i, me_ref: (0, 0, me_ref[0])),
        whole(g_pool.shape), whole(g_wa.shape), whole(g_wx.shape),
    ]
    args = [vec_parts, vec_parts, g_pool, g_wa, g_wx]
    out_specs = [whole((1, 128))]
    out_shape = [jax.ShapeDtypeStruct((1, 128), F32)]
    for name in names:
        for arr in params[name]:
            in_specs.append(whole(arr.shape))
            args.append(arr)
        shp = params[name][0].shape
        out_specs += [whole(shp)] * 4
        out_shape += [jax.ShapeDtypeStruct(shp, F32)] * 4
    grid_spec = pltpu.PrefetchScalarGridSpec(
        num_scalar_prefetch=1, grid=(1,), in_specs=in_specs, out_specs=out_specs,
        scratch_shapes=[pltpu.VMEM((N_VEC_ROWS, d), F32), pltpu.VMEM((N_VEC_ROWS, cw_cols), F32)])
    outs = pl.pallas_call(
        body, name="adamw_small", grid_spec=grid_spec, out_shape=out_shape,
        compiler_params=pltpu.CompilerParams(
            dimension_semantics=("arbitrary",), vmem_limit_bytes=V7X_VMEM_LIMIT_BYTES),
    )(me, *_in_hbm(args))
    return outs[0], {name: tuple(outs[1 + 4 * a:5 + 4 * a]) for a, name in enumerate(names)}


def _rs_sum(full, recv, shard_ids, slot_ids, name):
    r, rest = recv.shape[1], tuple(recv.shape[2:])
    zeros = (0,) * len(rest)
    send_dtype = recv.dtype

    def body(sh_ref, sl_ref, full_ref, recv_ref, own_ref, send_ref):
        s = pl.program_id(0)
        v = full_ref[...] + recv_ref[...].astype(F32)

        @pl.when(s == 0)
        def _():
            own_ref[...] = v

        @pl.when(s > 0)
        def _():
            send_ref[...] = v.astype(send_dtype)

    grid_spec = pltpu.PrefetchScalarGridSpec(
        num_scalar_prefetch=2,
        grid=(4,),
        in_specs=[
            pl.BlockSpec((r,) + rest, lambda s, sh, sl: (sh[s],) + zeros),
            pl.BlockSpec((None, r) + rest, lambda s, sh, sl: (sl[s], 0) + zeros),
        ],
        out_specs=[
            pl.BlockSpec((None, r) + rest, lambda s, sh, sl: (0, 0) + zeros),
            pl.BlockSpec((None, r) + rest, lambda s, sh, sl: (jnp.maximum(s - 1, 0), 0) + zeros),
        ],
    )
    return pl.pallas_call(
        body,
        name=name,
        grid_spec=grid_spec,
        out_shape=[jax.ShapeDtypeStruct((1, r) + rest, F32), jax.ShapeDtypeStruct((3, r) + rest, send_dtype)],
        compiler_params=pltpu.CompilerParams(
            dimension_semantics=("arbitrary",), vmem_limit_bytes=V7X_VMEM_LIMIT_BYTES),
    )(shard_ids, slot_ids, *_in_hbm([full, recv]))


def _finals(pairs, name, carry=None):
    nb = 4
    n = len(pairs)

    def body(*refs):
        for a in range(n):
            own_ref, recv_ref = refs[2 * a], refs[2 * a + 1]
            acc = own_ref[...]
            for k in range(3):
                acc = acc + recv_ref[k].astype(F32)
            refs[2 * n + a][...] = acc

    in_specs, out_specs, out_shape, args = [], [], [], []
    for own, recv in pairs:
        _, rows, cols = own.shape
        in_specs += [pl.BlockSpec((None, rows // nb, cols), lambda i: (0, i, 0)),
                     pl.BlockSpec((3, rows // nb, cols), lambda i: (0, i, 0))]
        args += [own, recv]
        out_specs.append(pl.BlockSpec((rows // nb, cols), lambda i: (i, 0)))
        out_shape.append(jax.ShapeDtypeStruct((rows, cols), F32))
    return _call(body, name=name, grid=(nb,), in_specs=in_specs, out_specs=out_specs,
                 out_shape=out_shape, args=args, carry=carry)


def _rs_sums(fulls_f32, recv1, tag):
    x, y, c = _place()
    qs = jnp.stack([2 * x + y, 2 * (1 - x) + y, 2 * x + (1 - y), 2 * (1 - x) + (1 - y)]).astype(jnp.int32)
    shard_ids = 2 * qs + c
    return [_rs_sum(f32, r1, shard_ids, qs, f"rs_sum_{tag}{a}")
            for a, (f32, r1) in enumerate(zip(fulls_f32, recv1))]


def _rs_level1(fulls_f32, fulls_send, tag):
    recv1 = _run_plan(_rs_sibling_plan(fulls_send), "rs_sibling_" + tag)
    return _rs_sums(fulls_f32, recv1, tag)


def _rows(g):
    return g.reshape(g.shape[0] * g.shape[1], g.shape[2])


def kernel(x, norm_mix_pre, norm_mix_post, norm_mlp_pre, norm_mlp_post, w_in, b_gate, conv_w, conv_b, lru_w_a, lru_b_a, lru_w_x, lru_b_x, lru_lambda, pool_w, pool_scale, w_lru_up, w_pool_up, w_o, w_ff1, w_ff2, loss_target, m_norm_mix_pre, m_norm_mix_post, m_norm_mlp_pre, m_norm_mlp_post, m_w_in, m_b_gate, m_conv_w, m_conv_b, m_lru_w_a, m_lru_b_a, m_lru_w_x, m_lru_b_x, m_lru_lambda, m_pool_w, m_pool_scale, m_w_lru_up, m_w_pool_up, m_w_o, m_w_ff1, m_w_ff2, v_norm_mix_pre, v_norm_mix_post, v_norm_mlp_pre, v_norm_mlp_post, v_w_in, v_b_gate, v_conv_w, v_conv_b, v_lru_w_a, v_lru_b_a, v_lru_w_x, v_lru_b_x, v_lru_lambda, v_pool_w, v_pool_scale, v_w_lru_up, v_w_pool_up, v_w_o, v_w_ff1, v_w_ff2):
    t, d = x.shape[1], x.shape[2]
    d_rnn = conv_b.shape[1]
    d_pool = pool_scale.shape[1]
    per = LRU_CB // LRU_HEAD_DIM
    xi, yi, ci = _place()
    me = 4 * xi + 2 * yi + ci

    x2d = x[0]
    tgt = loss_target[0]

    s_in = w_in[0].T.astype(BF16)
    s_lu = w_lru_up[0].astype(BF16)
    s_pu = w_pool_up[0].T.astype(BF16)
    s_o = w_o[0].astype(BF16)
    s_f1 = w_ff1[0].T.astype(BF16)
    s_f2 = w_ff2[0].astype(BF16)
    s_cw = jnp.pad(conv_w[0], ((0, 4), (0, 0)))

    g_in, g_cw = _run_plan(_ag_plan([s_in, s_cw]), "ag_w_in")
    w_int = _rows(g_in)
    conv_w_full = jnp.transpose(g_cw[:, :4, :], (1, 0, 2)).reshape(4, d_rnn)

    wa_bd, wx_bd = lru_w_a[0], lru_w_x[0]
    pw = pool_w[0]
    pw_bf = pw.astype(BF16)

    pool_block = (2 * d_rnn) // d_pool
    ga_block = (2 * d_rnn + d_pool) // 512
    gb_block = ga_block + d // 512
    g_block = d_rnn // 512

    r_f1, r_f2 = s_f1.shape[0], s_f2.shape[0]
    f1_cut = r_f1 // 4
    f2_cut = (3 * r_f2) // 8
    plan = _join([_ag_plan([s_lu, s_pu, s_o]), _ag_plan([s_f1], pieces=[(0, f1_cut)])])
    (proj, h1), got = _norm_proj(x2d, norm_mix_pre, w_int, carry=plan)
    (g_lu, g_pu, g_o), (g_f1,) = plan.split(got)
    w_lu, w_put, w_og = _rows(g_lu), _rows(g_pu), _rows(g_o)
    (y_lru, h, xc), (g_f1,) = _lru_fwd(
        proj, conv_w_full, conv_b, wa_bd, lru_b_a, wx_bd, lru_b_x, lru_lambda,
        carry=_ag_plan([s_f1], pieces=[(f1_cut, r_f1 - f1_cut)], bufs=[g_f1]))
    w_f1t = _rows(g_f1)
    y_pool, p = _pool_fwd(proj, pw_bf, pool_scale, pool_block)
    (br_a, br_b, mix), (g_f2,) = _branch_mix(
        y_lru, y_pool, w_lu, w_put, proj, b_gate, ga_block, gb_block,
        carry=_ag_plan([s_f2], pieces=[(0, f2_cut)]))
    (m, x2, h3), _ = _wo_norm(mix, w_og, x2d, norm_mix_post, norm_mlp_pre)
    (rf,), (g_f2,) = _ff1(
        h3, w_f1t, carry=_ag_plan([s_f2], pieces=[(f2_cut, r_f2 - f2_cut)], bufs=[g_f2]))
    w_f2 = _rows(g_f2)
    dy, df, dg4, loss_part = _ff2_loss(rf, w_f2, x2, norm_mlp_post, tgt)

    d_f1 = _ff2_bwd(df, w_f2, rf)
    (gw_ff2_32, gw_ff2_16), _ = _wgrad(rf, df, "wgrad_ff2", square_a=True)
    (gw_ff1_32, gw_ff1_16), r1_ff2 = _wgrad(d_f1, h3, "wgrad_ff1", carry=_rs_sibling_plan([gw_ff2_16]))
    ((own_ff2, send_ff2),) = _rs_sums([gw_ff2_32], r1_ff2, "ff2")
    plan = _join([_rs_chips_plan([send_ff2]), _rs_sibling_plan([gw_ff1_16])])
    (dx2, dm, dg3, dg2), got = _ff1_bwd_norms(d_f1, w_f1t, dy, x2, norm_mlp_pre, m, norm_mix_post, carry=plan)
    (r2_ff2,), r1_ff1 = plan.split(got)
    ((own_ff1, send_ff1),) = _rs_sums([gw_ff1_32], r1_ff1, "ff1")
    cut = send_ff1.shape[1] // 4
    (gw_o_32, gw_o_16), _ = _wgrad(mix, dm, "wgrad_o")
    (d_br_a, d_br_b, p_ga, p_gb, dbg_a, dbg_b), (r2_ff1,) = _wo_bwd_mix(
        dm, w_og, br_a, br_b, proj, b_gate, ga_block, gb_block,
        carry=_rs_chips_plan([send_ff1], pieces=[(0, cut)]))
    (gw_lu_32, gw_lu_16), _ = _wgrad(y_lru, d_br_a, "wgrad_lru_up")
    (gw_pu_32, gw_pu_16), _ = _wgrad(d_br_b, y_pool, "wgrad_pool_up")
    (dh, p_g), r1_mid = _lru_up_bwd(
        d_br_a, w_lu, proj, h, g_block,
        carry=_rs_sibling_plan([gw_o_16, gw_lu_16, gw_pu_16.reshape(-1, d)]))
    mid = _rs_sums([gw_o_32, gw_lu_32, gw_pu_32.reshape(-1, d)], r1_mid, "mid")
    d_y_pool = _pool_up_bwd(d_br_b, w_put)
    (p_x, dwa, db_a, dwx, db_x, dlam, dconv_w, dconv_b), (r2_ff1,) = _lru_bwd(
        dh, xc, h, proj, conv_w_full, wa_bd, lru_b_a, wx_bd, lru_b_x, lru_lambda,
        carry=_rs_chips_plan([send_ff1], pieces=[(cut, send_ff1.shape[1] - cut)], bufs=[r2_ff1]))
    p_p, dpool_w, dpool_scale = _pool_bwd(d_y_pool, p, pw, pool_scale)
    parts = [p_x, p_g, p_p, p_ga, p_gb]
    gw_in, r2_mid = _wgrad_parts(parts, h1, "wgrad_in", carry=_rs_chips_plan([s for _, s in mid]))
    tail = _rs_level1([gw_in[0], dpool_w.reshape(N_DEV, -1, POOL_GROUP_DIM), dwa, dwx],
                      [gw_in[1], dpool_w.reshape(N_DEV, -1, POOL_GROUP_DIM), dwa, dwx], "in")
    (grad_x, dg1), r2_tail = _win_bwd_norm(parts, w_int, dx2, x2d, norm_mix_pre,
                                           carry=_rs_chips_plan([s for _, s in tail]))

    def flat2(a):
        return a.reshape(a.shape[0], -1, a.shape[-1])

    fin_small, _ = _finals([
        (flat2(tail[1][0]), flat2(r2_tail[1])), (flat2(tail[2][0]), flat2(r2_tail[2])),
        (flat2(tail[3][0]), flat2(r2_tail[3])),
    ], "rs_finals_small")

    def pad_row(a):
        return jnp.pad(a, ((0, 0), (0, d - a.shape[1])))

    vecs = jnp.concatenate([dg1, dg2, dg3, dg4, dbg_a, dbg_b, dconv_b, db_a, db_x, dlam,
                            pad_row(dpool_scale), dconv_w, pad_row(loss_part)], axis=0)
    assert vecs.shape[0] == N_VEC_ROWS
    fin, (vec_parts, g_pool, g_wa, g_wx) = _finals([
        (tail[0][0], r2_tail[0]), (mid[1][0], r2_mid[1]), (mid[2][0], r2_mid[2]), (mid[0][0], r2_mid[0]),
        (own_ff1, r2_ff1), (own_ff2, r2_ff2),
    ], "rs_finals", carry=_ag_plan([vecs] + fin_small))
    g_w_in = fin[0].T
    g_w_lru_up = fin[1]
    g_w_pool_up = fin[2].reshape(d // N_DEV, d_pool).T
    g_w_o = fin[3]
    g_w_ff1 = fin[4].T
    g_w_ff2 = fin[5]

    big_names = ["w_in", "w_lru_up", "w_pool_up", "w_o", "w_ff1", "w_ff2"]
    big_w = [w_in, w_lru_up, w_pool_up, w_o, w_ff1, w_ff2]
    big_g = [g_w_in, g_w_lru_up, g_w_pool_up, g_w_o, g_w_ff1, g_w_ff2]
    big_m = [m_w_in, m_w_lru_up, m_w_pool_up, m_w_o, m_w_ff1, m_w_ff2]
    big_v = [v_w_in, v_w_lru_up, v_w_pool_up, v_w_o, v_w_ff1, v_w_ff2]
    big_out, _ = _adamw_big([w[0] for w in big_w], big_g, [mm[0] for mm in big_m], [vv[0] for vv in big_v])

    small = dict(
        norm_mix_pre=(norm_mix_pre, m_norm_mix_pre, v_norm_mix_pre),
        norm_mix_post=(norm_mix_post, m_norm_mix_post, v_norm_mix_post),
        norm_mlp_pre=(norm_mlp_pre, m_norm_mlp_pre, v_norm_mlp_pre),
        norm_mlp_post=(norm_mlp_post, m_norm_mlp_post, v_norm_mlp_post),
        b_gate=(b_gate, m_b_gate, v_b_gate), conv_w=(conv_w, m_conv_w, v_conv_w),
        conv_b=(conv_b, m_conv_b, v_conv_b), lru_w_a=(lru_w_a, m_lru_w_a, v_lru_w_a),
        lru_b_a=(lru_b_a, m_lru_b_a, v_lru_b_a), lru_w_x=(lru_w_x, m_lru_w_x, v_lru_w_x),
        lru_b_x=(lru_b_x, m_lru_b_x, v_lru_b_x), lru_lambda=(lru_lambda, m_lru_lambda, v_lru_lambda),
        pool_w=(pool_w, m_pool_w, v_pool_w), pool_scale=(pool_scale, m_pool_scale, v_pool_scale))
    loss_row, small_out = _adamw_small(
        vec_parts, g_pool.reshape(pool_w.shape), g_wa.reshape(lru_w_a.shape), g_wx.reshape(lru_w_x.shape),
        jnp.reshape(me, (1,)).astype(jnp.int32), small)
    grads = {n: o[0] for n, o in small_out.items()}
    delta = {n: o[1] for n, o in small_out.items()}
    new_m = {n: o[2] for n, o in small_out.items()}
    new_v = {n: o[3] for n, o in small_out.items()}

    for name, g, (dl, nm, nv) in zip(big_names, big_g, big_out):
        grads[name], delta[name], new_m[name], new_v[name] = g[None], dl[None], nm[None], nv[None]

    loss = loss_row[0, 0]
    order = ["norm_mix_pre", "norm_mix_post", "norm_mlp_pre", "norm_mlp_post", "w_in", "b_gate", "conv_w",
             "conv_b", "lru_w_a", "lru_b_a", "lru_w_x", "lru_b_x", "lru_lambda", "pool_w", "pool_scale",
             "w_lru_up", "w_pool_up", "w_o", "w_ff1", "w_ff2"]
    return (loss, grad_x[None], *[grads[n] for n in order], *[delta[n] for n in order],
            *[new_m[n] for n in order], *[new_v[n] for n in order])
```

```python
import functools
import math
import operator
import types

import jax
import jax.numpy as jnp
from jax import lax
from jax.experimental import pallas as pl
from jax.experimental.pallas import tpu as pltpu

F32 = jnp.float32
BF16 = jnp.bfloat16
NORM_EPS = 1e-6
LRU_C = 8.0
N_LRU_HEADS = 16
LRU_HEAD_DIM = 64
POOL_WINDOWS = (2, 4, 8, 16)
POOL_GROUP_DIM = 128
ADAM_LR = 0.001
ADAM_B1 = 0.9
ADAM_B2 = 0.999
ADAM_EPS = 1e-08
ADAM_WD = 0.01
ADAM_STEP = 10
N_DEV = 8
V7X_VMEM_LIMIT_BYTES = 56 * 1024 * 1024
LRU_CB = 256
MESH = pl.DeviceIdType.MESH
ANY = pl.BlockSpec(memory_space=pl.ANY)


def _tile(n, pref):
    t = min(n, pref)
    assert n % t == 0, (n, pref)
    return t


def _dot_nn(a, b):
    return lax.dot_general(a, b, (((1,), (0,)), ((), ())), preferred_element_type=F32)


def _dot_nt(a, b):
    return lax.dot_general(a, b, (((1,), (1,)), ((), ())), preferred_element_type=F32)


def _dot_tn(a, b):
    return lax.dot_general(a, b, (((0,), (0,)), ((), ())), preferred_element_type=F32)


def _row_chunks(n_rows, fn, chunk=256):
    chunk = min(chunk, n_rows)
    assert n_rows % chunk == 0

    def step(r, carry):
        fn(pl.ds(pl.multiple_of(r * chunk, chunk), chunk))
        return carry

    lax.fori_loop(0, n_rows // chunk, step, 0)


def _sig(x):
    return 1.0 / (1.0 + jnp.exp(-x))


def _rms_hat(x):
    r = lax.rsqrt(jnp.mean(x * x, axis=-1, keepdims=True) + NORM_EPS)
    return x * r, r


def _rms_bwd(dn, xhat, r, g):
    q = dn * g
    dx = r * (q - xhat * jnp.mean(q * xhat, axis=-1, keepdims=True))
    dg = jnp.sum(dn * xhat, axis=0, keepdims=True)
    return dx, dg


_GELU_K = math.sqrt(2.0 / math.pi)
_GELU_C = 0.044715


def _gelu_and_grad(g):
    t = jnp.tanh(_GELU_K * (g + _GELU_C * g * g * g))
    val = 0.5 * g * (1.0 + t)
    grad = 0.5 * (1.0 + t) + 0.5 * g * (1.0 - t * t) * (_GELU_K * (1.0 + 3.0 * _GELU_C * g * g))
    return val, grad


def _softplus_neg(lam):
    z = -lam
    e = jnp.exp(-jnp.abs(z))
    u = 1.0 + e
    d = u - 1.0
    l1p = jnp.where(d == 0.0, e, jnp.log(u) * (e / jnp.where(d == 0.0, 1.0, d)))
    return jnp.maximum(z, 0.0) + l1p


def _lru_gates(xc, wa, ba, wx, bx, lam):
    xcb = xc.astype(BF16)
    r = _sig(_dot_nn(xcb, wa) + ba)
    i = _sig(_dot_nn(xcb, wx) + bx)
    sp = _softplus_neg(lam)
    log_a = (-LRU_C) * r * sp
    a = jnp.exp(log_a)
    mult = jnp.sqrt(-jnp.tanh(log_a) * (1.0 + a * a))
    return xcb, r, i, sp, log_a, a, mult


def _place():
    return lax.axis_index("x"), lax.axis_index("y"), lax.axis_index("c")


def _ag_plan(shards, pieces=None, bufs=None):
    na = len(shards)

    def parts(ins, outs, sems):
        send_sems, recv_sems, local_sems = sems
        x, y, c = _place()
        me, sibling = (x, y, c), (x, y, 1 - c)
        chips = [(1 - x, y), (x, 1 - y), (1 - x, 1 - y)]

        def own(a):
            return ins[a] if pieces is None else ins[a].at[pl.ds(*pieces[a])]

        def slot(a, px, py, pc):
            idx = 4 * px + 2 * py + pc
            return outs[a].at[idx] if pieces is None else outs[a].at[idx, pl.ds(*pieces[a])]

        def copy(a, k, block, to, src=None):
            return pltpu.make_async_remote_copy(
                src_ref=slot(a, *block) if src is None else src,
                dst_ref=slot(a, *block),
                send_sem=send_sems.at[a * 7 + k],
                recv_sem=recv_sems.at[a * 7 + k],
                device_id=to,
                device_id_type=MESH,
            )

        mine = [pltpu.make_async_copy(own(a), slot(a, *me), local_sems.at[a]) for a in range(na)]
        first = []
        for a in range(na):
            first.append(copy(a, 0, me, sibling, src=own(a)))
            first += [copy(a, 1 + j, me, (*chip, c), src=own(a)) for j, chip in enumerate(chips)]
        return me, sibling, chips, c, copy, mine, first

    def start(ins, outs, sems):
        _, _, _, _, _, mine, first = parts(ins, outs, sems)
        for cp in mine + first:
            cp.start()

    def finish(ins, outs, sems):
        me, sibling, chips, c, copy, mine, first = parts(ins, outs, sems)
        passed = []
        for j, chip in enumerate(chips):
            for a in range(na):
                copy(a, 1 + j, (*chip, c), me).wait_recv()
                fwd = copy(a, 4 + j, (*chip, c), sibling)
                fwd.start()
                passed.append(fwd)
        for a in range(na):
            copy(a, 0, sibling, me).wait_recv()
            for j, chip in enumerate(chips):
                copy(a, 4 + j, (*chip, 1 - c), me).wait_recv()
        for cp in first + passed:
            cp.wait_send()
        for cp in mine:
            cp.wait()

    return types.SimpleNamespace(
        ins=list(shards) + list(bufs or []),
        out_shapes=[jax.ShapeDtypeStruct((N_DEV,) + s.shape, s.dtype) for s in shards],
        sems=[pltpu.SemaphoreType.DMA((7 * na,)), pltpu.SemaphoreType.DMA((7 * na,)),
              pltpu.SemaphoreType.DMA((na,))],
        aliases=[(na + a, a) for a in range(na)] if bufs else [],
        peers=frozenset({"sibling", "chips"}), start=start, finish=finish)


def _rs_sibling_plan(fulls):
    na = len(fulls)
    rs = [f.shape[0] // N_DEV for f in fulls]

    def copies(ins, outs, sems):
        send_sems, recv_sems = sems
        x, y, c = _place()
        out = []
        for a in range(na):
            for q in range(4):
                shard = 2 * q + (1 - c)
                out.append(pltpu.make_async_remote_copy(
                    src_ref=ins[a].at[pl.ds(shard * rs[a], rs[a])],
                    dst_ref=outs[a].at[q],
                    send_sem=send_sems.at[a * 4 + q],
                    recv_sem=recv_sems.at[a * 4 + q],
                    device_id=(x, y, 1 - c),
                    device_id_type=MESH,
                ))
        return out

    def start(ins, outs, sems):
        for cp in copies(ins, outs, sems):
            cp.start()

    def finish(ins, outs, sems):
        for cp in copies(ins, outs, sems):
            cp.wait()

    return types.SimpleNamespace(
        ins=list(fulls),
        out_shapes=[jax.ShapeDtypeStruct((4, r) + f.shape[1:], f.dtype) for r, f in zip(rs, fulls)],
        sems=[pltpu.SemaphoreType.DMA((4 * na,)), pltpu.SemaphoreType.DMA((4 * na,))],
        peers=frozenset({"sibling"}), start=start, finish=finish)


def _rs_chips_plan(sends, pieces=None, bufs=None):
    na = len(sends)

    def copies(ins, outs, sems):
        send_sems, recv_sems = sems
        x, y, c = _place()
        chips = [(1 - x, y), (x, 1 - y), (1 - x, 1 - y)]
        out = []
        for a in range(na):
            for k, chip in enumerate(chips):
                rows = (k,) if pieces is None else (k, pl.ds(*pieces[a]))
                out.append(pltpu.make_async_remote_copy(
                    src_ref=ins[a].at[rows],
                    dst_ref=outs[a].at[rows],
                    send_sem=send_sems.at[a * 3 + k],
                    recv_sem=recv_sems.at[a * 3 + k],
                    device_id=(*chip, c),
                    device_id_type=MESH,
                ))
        return out

    def start(ins, outs, sems):
        for cp in copies(ins, outs, sems):
            cp.start()

    def finish(ins, outs, sems):
        for cp in copies(ins, outs, sems):
            cp.wait()

    return types.SimpleNamespace(
        ins=list(sends) + list(bufs or []),
        out_shapes=[jax.ShapeDtypeStruct(s.shape, s.dtype) for s in sends],
        sems=[pltpu.SemaphoreType.DMA((3 * na,)), pltpu.SemaphoreType.DMA((3 * na,))],
        aliases=[(na + a, a) for a in range(na)] if bufs else [],
        peers=frozenset({"chips"}), start=start, finish=finish)


def _join(plans):
    ins, outs, sems, aliases, offs = [], [], [], [], []
    for p in plans:
        offs.append((len(ins), len(outs), len(sems)))
        aliases += [(len(ins) + ci, len(outs) + co) for ci, co in getattr(p, "aliases", [])]
        ins += p.ins
        outs += p.out_shapes
        sems += p.sems

    def cut(p, off, i, o, s):
        return (i[off[0]:off[0] + len(p.ins)], o[off[1]:off[1] + len(p.out_shapes)],
                s[off[2]:off[2] + len(p.sems)])

    def start(i, o, s):
        for p, off in zip(plans, offs):
            p.start(*cut(p, off, i, o, s))

    def finish(i, o, s):
        for p, off in zip(plans, offs):
            p.finish(*cut(p, off, i, o, s))

    def split(results):
        return [list(results[off[1]:off[1] + len(p.out_shapes)]) for p, off in zip(plans, offs)]

    return types.SimpleNamespace(ins=ins, out_shapes=outs, sems=sems, aliases=aliases,
                                 peers=frozenset().union(*[p.peers for p in plans]),
                                 start=start, finish=finish, split=split)


COLLECTIVE_ID = {frozenset({"sibling"}): 0, frozenset({"chips"}): 1, frozenset({"sibling", "chips"}): 2}


def _handshake(peers):
    x, y, c = _place()
    devs = []
    if "sibling" in peers:
        devs.append((x, y, 1 - c))
    if "chips" in peers:
        devs += [(1 - x, y, c), (x, 1 - y, c), (1 - x, 1 - y, c)]
    barrier = pltpu.get_barrier_semaphore()
    for dev in devs:
        pl.semaphore_signal(barrier, inc=1, device_id=dev, device_id_type=MESH)
    pl.semaphore_wait(barrier, len(devs))


def _in_hbm(args):
    return [pltpu.with_memory_space_constraint(a, pltpu.HBM) for a in args]


def _run_plan(plan, name):
    n_in, n_out = len(plan.ins), len(plan.out_shapes)

    def body(*refs):
        ins, outs, sems = refs[:n_in], refs[n_in:n_in + n_out], refs[n_in + n_out:]
        _handshake(plan.peers)
        plan.start(ins, outs, sems)
        plan.finish(ins, outs, sems)

    return pl.pallas_call(
        body,
        name=name,
        in_specs=[ANY] * n_in,
        out_specs=[ANY] * n_out,
        out_shape=plan.out_shapes,
        scratch_shapes=plan.sems,
        input_output_aliases=dict(getattr(plan, "aliases", [])),
        compiler_params=pltpu.CompilerParams(collective_id=COLLECTIVE_ID[plan.peers]),
    )(*_in_hbm(plan.ins))


def _call(body, *, name, grid, in_specs, out_specs, out_shape, args, scratch_shapes=(), aliases=None,
          carry=None):
    n_in, n_out, n_scr = len(in_specs), len(out_shape), len(scratch_shapes)
    params = pltpu.CompilerParams(
        dimension_semantics=("arbitrary",) * len(grid), vmem_limit_bytes=V7X_VMEM_LIMIT_BYTES)
    if carry is None:
        outs = pl.pallas_call(
            body, name=name, grid=grid, in_specs=list(in_specs), out_specs=list(out_specs),
            out_shape=list(out_shape), scratch_shapes=list(scratch_shapes),
            input_output_aliases=aliases or {}, compiler_params=params)(*_in_hbm(args))
        return list(outs), []
    c_in, c_out = len(carry.ins), len(carry.out_shapes)

    def full(*refs):
        p = 0
        ins = refs[p:p + n_in]
        p += n_in
        cins = refs[p:p + c_in]
        p += c_in
        outs = refs[p:p + n_out]
        p += n_out
        couts = refs[p:p + c_out]
        p += c_out
        scr = refs[p:p + n_scr]
        csems = refs[p + n_scr:]
        ids = [pl.program_id(a) for a in range(len(grid))]
        first = functools.reduce(operator.and_, [i == 0 for i in ids])
        last = functools.reduce(operator.and_, [i == g - 1 for i, g in zip(ids, grid)])

        @pl.when(first)
        def _():
            _handshake(carry.peers)
            carry.start(cins, couts, csems)

        body(*ins, *outs, *scr)

        @pl.when(last)
        def _():
            carry.finish(cins, couts, csems)

    all_aliases = dict(aliases or {})
    all_aliases.update({n_in + ci: n_out + co for ci, co in getattr(carry, "aliases", [])})
    params = pltpu.CompilerParams(
        dimension_semantics=("arbitrary",) * len(grid), vmem_limit_bytes=V7X_VMEM_LIMIT_BYTES,
        collective_id=COLLECTIVE_ID[carry.peers])
    outs = pl.pallas_call(
        full, name=name, grid=grid,
        in_specs=list(in_specs) + [ANY] * c_in,
        out_specs=list(out_specs) + [ANY] * c_out,
        out_shape=list(out_shape) + list(carry.out_shapes),
        scratch_shapes=list(scratch_shapes) + list(carry.sems),
        input_output_aliases=all_aliases, compiler_params=params)(*_in_hbm(args), *_in_hbm(carry.ins))
    return list(outs[:n_out]), list(outs[n_out:])


def _norm_proj(x, g1, w_int, carry=None):
    t, d = x.shape
    n = w_int.shape[0]
    tt, tn = _tile(t, 2048), _tile(n, 512)

    def body(x_ref, g_ref, w_ref, proj_ref, h1_ref, h1_s):
        @pl.when(pl.program_id(1) == 0)
        def _():
            def norm_rows(rows):
                xhat, _ = _rms_hat(x_ref[rows, :])
                h = (xhat * g_ref[...]).astype(BF16)
                h1_s[rows, :] = h
                h1_ref[rows, :] = h

            _row_chunks(tt, norm_rows)

        proj_ref[...] = _dot_nt(h1_s[...], w_ref[...]).astype(BF16)

    return _call(
        body, name="norm_proj", grid=(t // tt, n // tn),
        in_specs=[
            pl.BlockSpec((tt, d), lambda i, j: (i, 0)),
            pl.BlockSpec((1, d), lambda i, j: (0, 0)),
            pl.BlockSpec((tn, d), lambda i, j: (j, 0)),
        ],
        out_specs=[
            pl.BlockSpec((tt, tn), lambda i, j: (i, j)),
            pl.BlockSpec((tt, d), lambda i, j: (i, 0)),
        ],
        out_shape=[jax.ShapeDtypeStruct((t, n), BF16), jax.ShapeDtypeStruct((t, d), BF16)],
        scratch_shapes=[pltpu.VMEM((tt, d), BF16)],
        args=(x, g1, w_int), carry=carry)


def _fill_block_diag(w_ref, bd_ref):
    bd_ref[...] = jnp.zeros_like(bd_ref)
    hd = LRU_HEAD_DIM
    for k in range(w_ref.shape[0]):
        bd_ref[k * hd:(k + 1) * hd, k * hd:(k + 1) * hd] = w_ref[k].astype(BF16)


def _lru_fwd(proj, conv_w, conv_b, w_a, b_a, w_x, b_x, lam, carry=None):
    t = proj.shape[0]
    dr = conv_b.shape[1]
    cb = LRU_CB
    tc = _tile(t, 256)
    ncb, ntc = dr // cb, t // tc

    def body(xp_ref, g_ref, cw_ref, cb_ref, wa_ref, ba_ref, wx_ref, bx_ref, lam_ref,
             y_ref, h_ref, xc_ref, prevx_s, hlast_s, wa_s, wx_s):
        c = pl.program_id(1)

        @pl.when(c == 0)
        def _():
            prevx_s[...] = jnp.zeros_like(prevx_s)
            hlast_s[...] = jnp.zeros_like(hlast_s)
            _fill_block_diag(wa_ref, wa_s)
            _fill_block_diag(wx_ref, wx_s)

        x = xp_ref[...].astype(F32)
        prev = prevx_s[...]
        row = lax.broadcasted_iota(jnp.int32, x.shape, 0)

        def sh(j):
            return jnp.where(row >= j, pltpu.roll(x, j, 0), pltpu.roll(prev, j, 0))

        xc = (cb_ref[...] + cw_ref[0:1, :] * sh(3) + cw_ref[1:2, :] * sh(2)
              + cw_ref[2:3, :] * sh(1) + cw_ref[3:4, :] * x)
        prevx_s[...] = x
        xc_ref[...] = xc
        _, _, i, _, _, a, mult = _lru_gates(xc, wa_s[...], ba_ref[...], wx_s[...], bx_ref[...],
                                            lam_ref[...])
        av, bv = a, mult * (i * xc)
        s = 1
        while s < tc:
            a_sh = jnp.where(row >= s, pltpu.roll(av, s, 0), 1.0)
            b_sh = jnp.where(row >= s, pltpu.roll(bv, s, 0), 0.0)
            bv = av * b_sh + bv
            av = av * a_sh
            s *= 2
        h = av * hlast_s[...] + bv
        h_ref[...] = h
        hlast_s[...] = h_ref[tc - 1:tc, :]
        gel, _ = _gelu_and_grad(g_ref[...].astype(F32))
        y_ref[...] = (h * gel).astype(BF16)

    vec = pl.BlockSpec((1, cb), lambda j, c: (0, j))
    blk = pl.BlockSpec((tc, cb), lambda j, c: (c, j))
    mat = pl.BlockSpec((cb // LRU_HEAD_DIM, LRU_HEAD_DIM, LRU_HEAD_DIM), lambda j, c: (j, 0, 0))
    return _call(
        body, name="lru_fwd", grid=(ncb, ntc),
        in_specs=[
            blk,
            pl.BlockSpec((tc, cb), lambda j, c: (c, ncb + j)),
            pl.BlockSpec((4, cb), lambda j, c: (0, j)),
            vec, mat, vec, mat, vec, vec,
        ],
        out_specs=[blk, blk, blk],
        out_shape=[
            jax.ShapeDtypeStruct((t, dr), BF16),
            jax.ShapeDtypeStruct((t, dr), F32),
            jax.ShapeDtypeStruct((t, dr), F32),
        ],
        scratch_shapes=[pltpu.VMEM((tc, cb), F32), pltpu.VMEM((1, cb), F32),
                        pltpu.VMEM((cb, cb), BF16), pltpu.VMEM((cb, cb), BF16)],
        args=(proj, proj, conv_w, conv_b, w_a, b_a, w_x, b_x, lam), carry=carry)


def _pool_select(col, vals):
    out = vals[3]
    for g in (2, 1, 0):
        out = jnp.where(col < (g + 1) * POOL_GROUP_DIM, vals[g], out)
    return out


def _pool_fwd(proj, pool_w, pool_scale, col_block):
    t = proj.shape[0]
    dp = pool_scale.shape[1]
    tc = _tile(t, 256)
    ntc = t // tc

    def body(x_ref, w_ref, sc_ref, y_ref, p_ref, px, p2, p4, p8):
        c = pl.program_id(0)

        @pl.when(c == 0)
        def _():
            for s in (px, p2, p4, p8):
                s[...] = jnp.zeros_like(s)

        x = x_ref[...].astype(F32)
        row = lax.broadcasted_iota(jnp.int32, x.shape, 0)
        col = lax.broadcasted_iota(jnp.int32, x.shape, 1)

        def sh(v, pv, j):
            return jnp.where(row >= j, pltpu.roll(v, j, 0), pltpu.roll(pv[...], j, 0))

        s2 = x + sh(x, px, 1)
        s4 = s2 + sh(s2, p2, 2)
        s8 = s4 + sh(s4, p4, 4)
        s16 = s8 + sh(s8, p8, 8)
        px[...] = x
        p2[...] = s2
        p4[...] = s4
        p8[...] = s8
        wsum = _pool_select(col, (s2, s4, s8, s16))
        win = _pool_select(col, POOL_WINDOWS)
        cnt = jnp.minimum(c * tc + row + 1, win).astype(F32)
        p = wsum / cnt - x
        pb = p.astype(BF16)
        p_ref[...] = pb
        for g in range(len(POOL_WINDOWS)):
            sl = slice(g * POOL_GROUP_DIM, (g + 1) * POOL_GROUP_DIM)
            yg = _dot_nn(pb[:, sl], w_ref[g]) * sc_ref[:, sl]
            y_ref[:, sl] = yg.astype(BF16)

    return _call(
        body, name="pool_fwd", grid=(ntc,),
        in_specs=[
            pl.BlockSpec((tc, dp), lambda c: (c, col_block)),
            pl.BlockSpec(pool_w.shape, lambda c: (0, 0, 0)),
            pl.BlockSpec((1, dp), lambda c: (0, 0)),
        ],
        out_specs=[pl.BlockSpec((tc, dp), lambda c: (c, 0))] * 2,
        out_shape=[jax.ShapeDtypeStruct((t, dp), BF16)] * 2,
        scratch_shapes=[pltpu.VMEM((tc, dp), F32)] * 4,
        args=(proj, pool_w, pool_scale))[0]


def _branch_mix(y_lru, y_pool, w_lru_up, w_pool_upt, proj, b_gate, ga_block, gb_block, carry=None):
    t, d = y_lru.shape
    dp = y_pool.shape[1]
    tt, tn = _tile(t, 1024), 512
    nj = d // tn

    def body(yl_ref, yp_ref, wl_ref, wp_ref, ga_ref, gb_ref, ba_ref, bb_ref, bra_ref, brb_ref, mix_ref):
        br_a = _dot_nn(yl_ref[...], wl_ref[...])
        br_b = _dot_nt(yp_ref[...], wp_ref[...])
        bra_ref[...] = br_a.astype(BF16)
        brb_ref[...] = br_b.astype(BF16)
        ga = _sig(ga_ref[...].astype(F32) + ba_ref[...])
        gb = _sig(gb_ref[...].astype(F32) + bb_ref[...])
        mix_ref[...] = (ga * br_a + gb * br_b).astype(BF16)

    out = pl.BlockSpec((tt, tn), lambda j, i: (i, j))
    return _call(
        body, name="branch_mix", grid=(nj, t // tt),
        in_specs=[
            pl.BlockSpec((tt, d), lambda j, i: (i, 0)),
            pl.BlockSpec((tt, dp), lambda j, i: (i, 0)),
            pl.BlockSpec((d, tn), lambda j, i: (0, j)),
            pl.BlockSpec((tn, dp), lambda j, i: (j, 0)),
            pl.BlockSpec((tt, tn), lambda j, i: (i, ga_block + j)),
            pl.BlockSpec((tt, tn), lambda j, i: (i, gb_block + j)),
            pl.BlockSpec((1, tn), lambda j, i: (0, j)),
            pl.BlockSpec((1, tn), lambda j, i: (0, nj + j)),
        ],
        out_specs=[out, out, out],
        out_shape=[jax.ShapeDtypeStruct((t, d), BF16)] * 3,
        args=(y_lru, y_pool, w_lru_up, w_pool_upt, proj, proj, b_gate, b_gate), carry=carry)


def _wo_norm(mix, w_o, x, g2, g3, carry=None):
    t, d = x.shape
    tt = _tile(t, 512)

    def body(mix_ref, w_ref, x_ref, g2_ref, g3_ref, m_ref, x2_ref, h3_ref):
        m = _dot_nn(mix_ref[...], w_ref[...])
        m_ref[...] = m
        mhat, _ = _rms_hat(m)
        x2 = x_ref[...] + mhat * g2_ref[...]
        x2_ref[...] = x2
        xhat, _ = _rms_hat(x2)
        h3_ref[...] = (xhat * g3_ref[...]).astype(BF16)

    row = pl.BlockSpec((tt, d), lambda i: (i, 0))
    vec = pl.BlockSpec((1, d), lambda i: (0, 0))
    return _call(
        body, name="wo_norm", grid=(t // tt,),
        in_specs=[row, pl.BlockSpec((d, d), lambda i: (0, 0)), row, vec, vec],
        out_specs=[row, row, row],
        out_shape=[
            jax.ShapeDtypeStruct((t, d), F32),
            jax.ShapeDtypeStruct((t, d), F32),
            jax.ShapeDtypeStruct((t, d), BF16),
        ],
        args=(mix, w_o, x, g2, g3), carry=carry)


def _ff1(h3, w_ff1t, carry=None):
    t, d = h3.shape
    n = w_ff1t.shape[0]
    tt, tn = _tile(t, 2048), _tile(n, 512)

    def body(h_ref, w_ref, rf_ref):
        rf_ref[...] = jnp.maximum(_dot_nt(h_ref[...], w_ref[...]), 0.0).astype(BF16)

    out = pl.BlockSpec((tt, tn), lambda i, j: (i, j))
    return _call(
        body, name="ff1", grid=(t // tt, n // tn),
        in_specs=[pl.BlockSpec((tt, d), lambda i, j: (i, 0)), pl.BlockSpec((tn, d), lambda i, j: (j, 0))],
        out_specs=[out],
        out_shape=[jax.ShapeDtypeStruct((t, n), BF16)],
        args=(h3, w_ff1t), carry=carry)


def _ff2_loss(rf, w_ff2, x2, g4, target):
    t, k = rf.shape
    d = x2.shape[1]
    tt, tk = _tile(t, 1024), _tile(k, 512)
    nk = k // tk

    def body(a_ref, w_ref, x2_ref, g_ref, tg_ref, dy_ref, df_ref, dg_ref, loss_ref, acc):
        i, kk = pl.program_id(0), pl.program_id(1)

        @pl.when(kk == 0)
        def _():
            acc[...] = jnp.zeros_like(acc)

        @pl.when((i == 0) & (kk == 0))
        def _():
            dg_ref[...] = jnp.zeros_like(dg_ref)
            loss_ref[...] = jnp.zeros_like(loss_ref)

        rf_tile = a_ref[...]
        acc[...] += _dot_nn(rf_tile * rf_tile, w_ref[...])

        @pl.when(kk == nk - 1)
        def _():
            def tail(rows):
                fhat, r = _rms_hat(acc[rows, :])
                g = g_ref[...]
                e = x2_ref[rows, :] + fhat * g - tg_ref[rows, :]
                loss_ref[...] += 0.5 * jnp.sum(jnp.mean(e * e, axis=-1, keepdims=True))
                dy = e * (1.0 / d)
                dy_ref[rows, :] = dy.astype(BF16)
                df, dg = _rms_bwd(dy, fhat, r, g)
                df_ref[rows, :] = df.astype(BF16)
                dg_ref[...] += dg

            _row_chunks(tt, tail)

    row = pl.BlockSpec((tt, d), lambda i, kk: (i, 0))
    vec = pl.BlockSpec((1, d), lambda i, kk: (0, 0))
    return _call(
        body, name="ff2_loss", grid=(t // tt, nk),
        in_specs=[
            pl.BlockSpec((tt, tk), lambda i, kk: (i, kk)),
            pl.BlockSpec((tk, d), lambda i, kk: (kk, 0)),
            row, vec, row,
        ],
        out_specs=[row, row, vec, pl.BlockSpec((1, 128), lambda i, kk: (0, 0))],
        out_shape=[
            jax.ShapeDtypeStruct((t, d), BF16),
            jax.ShapeDtypeStruct((t, d), BF16),
            jax.ShapeDtypeStruct((1, d), F32),
            jax.ShapeDtypeStruct((1, 128), F32),
        ],
        scratch_shapes=[pltpu.VMEM((tt, d), F32)],
        args=(rf, w_ff2, x2, g4, target))[0]


def _ff2_bwd(df, w_ff2, rf):
    t, d = df.shape
    n = w_ff2.shape[0]
    tt, tn = _tile(t, 2048), _tile(n, 512)

    def body(df_ref, w_ref, rf_ref, out_ref):
        d_act = _dot_nt(df_ref[...], w_ref[...])
        out_ref[...] = (d_act * (2.0 * rf_ref[...].astype(F32))).astype(BF16)

    blk = pl.BlockSpec((tt, tn), lambda i, j: (i, j))
    return _call(
        body, name="ff2_bwd", grid=(t // tt, n // tn),
        in_specs=[pl.BlockSpec((tt, d), lambda i, j: (i, 0)), pl.BlockSpec((tn, d), lambda i, j: (j, 0)), blk],
        out_specs=[blk],
        out_shape=[jax.ShapeDtypeStruct((t, n), BF16)],
        args=(df, w_ff2, rf))[0][0]


def _wgrad(a, b, name, prev=None, row_off=0, rows=None, carry=None, square_a=False):
    t, m = a.shape
    n = b.shape[1]
    rows = m if rows is None else rows
    tm, tk = _tile(m, 512), _tile(t, 2048)
    nk = t // tk
    assert row_off % tm == 0
    off = row_off // tm

    def body(*refs):
        a_ref, b_ref = refs[0], refs[1]
        o32_ref, o16_ref, acc = refs[-3], refs[-2], refs[-1]
        kk = pl.program_id(1)

        @pl.when(kk == 0)
        def _():
            acc[...] = jnp.zeros_like(acc)

        a_tile = a_ref[...]
        acc[...] += _dot_tn(a_tile * a_tile if square_a else a_tile, b_ref[...])

        @pl.when(kk == nk - 1)
        def _():
            o32_ref[...] = acc[...]
            o16_ref[...] = acc[...].astype(BF16)

    in_specs = [pl.BlockSpec((tk, tm), lambda i, kk: (kk, i)), pl.BlockSpec((tk, n), lambda i, kk: (kk, 0))]
    args = [a, b]
    aliases = {}
    if prev is not None:
        in_specs += [ANY, ANY]
        args += list(prev)
        aliases = {2: 0, 3: 1}
    out = pl.BlockSpec((tm, n), lambda i, kk: (off + i, 0))
    return _call(
        body, name=name, grid=(m // tm, nk),
        in_specs=in_specs, out_specs=[out, out],
        out_shape=[jax.ShapeDtypeStruct((rows, n), F32), jax.ShapeDtypeStruct((rows, n), BF16)],
        scratch_shapes=[pltpu.VMEM((tm, n), F32)],
        aliases=aliases, args=args, carry=carry)


def _wgrad_parts(parts, b, name, carry=None):
    t, n = b.shape
    tm = 512
    bounds = []
    lo = 0
    for part in parts:
        assert part.shape[0] == t and part.shape[1] % tm == 0
        bounds.append((lo, lo + part.shape[1] // tm))
        lo += part.shape[1] // tm
    nm = lo
    np_ = len(parts)

    def body(*refs):
        p_refs, b_ref, o32_ref, o16_ref = refs[:np_], refs[np_], refs[np_ + 1], refs[np_ + 2]
        i = pl.program_id(0)
        for (lo_p, hi_p), p_ref in zip(bounds, p_refs):
            @pl.when((i >= lo_p) & (i < hi_p))
            def _(p_ref=p_ref):
                res = _dot_tn(p_ref[...], b_ref[...])
                o32_ref[...] = res
                o16_ref[...] = res.astype(BF16)

    def part_spec(lo_p, hi_p):
        return pl.BlockSpec((t, tm), lambda i: (0, jnp.clip(i - lo_p, 0, hi_p - lo_p - 1)))

    out = pl.BlockSpec((tm, n), lambda i: (i, 0))
    return _call(
        body, name=name, grid=(nm,),
        in_specs=[part_spec(lo_p, hi_p) for lo_p, hi_p in bounds] + [pl.BlockSpec((t, n), lambda i: (0, 0))],
        out_specs=[out, out],
        out_shape=[jax.ShapeDtypeStruct((nm * tm, n), F32), jax.ShapeDtypeStruct((nm * tm, n), BF16)],
        args=(*parts, b), carry=carry)


def _ff1_bwd_norms(d_f1, w_ff1t, dy, x2, g3, m, g2, carry=None):
    t, k = d_f1.shape
    d = x2.shape[1]
    tt, tk = _tile(t, 1024), _tile(k, 512)
    nk = k // tk

    def body(a_ref, w_ref, dy_ref, x2_ref, g3_ref, m_ref, g2_ref, dx2_ref, dm_ref, dg3_ref, dg2_ref, acc):
        i, kk = pl.program_id(0), pl.program_id(1)

        @pl.when(kk == 0)
        def _():
            acc[...] = jnp.zeros_like(acc)

        @pl.when((i == 0) & (kk == 0))
        def _():
            dg3_ref[...] = jnp.zeros_like(dg3_ref)
            dg2_ref[...] = jnp.zeros_like(dg2_ref)

        acc[...] += _dot_nn(a_ref[...], w_ref[...])

        @pl.when(kk == nk - 1)
        def _():
            def tail(rows):
                xhat, r3 = _rms_hat(x2_ref[rows, :])
                dx, dg3 = _rms_bwd(acc[rows, :], xhat, r3, g3_ref[...])
                dx2 = dy_ref[rows, :].astype(F32) + dx
                dx2_ref[rows, :] = dx2
                dg3_ref[...] += dg3
                mhat, r2 = _rms_hat(m_ref[rows, :])
                dm, dg2 = _rms_bwd(dx2, mhat, r2, g2_ref[...])
                dm_ref[rows, :] = dm.astype(BF16)
                dg2_ref[...] += dg2

            _row_chunks(tt, tail)

    row = pl.BlockSpec((tt, d), lambda i, kk: (i, 0))
    vec = pl.BlockSpec((1, d), lambda i, kk: (0, 0))
    return _call(
        body, name="ff1_bwd_norms", grid=(t // tt, nk),
        in_specs=[
            pl.BlockSpec((tt, tk), lambda i, kk: (i, kk)),
            pl.BlockSpec((tk, d), lambda i, kk: (kk, 0)),
            row, row, vec, row, vec,
        ],
        out_specs=[row, row, vec, vec],
        out_shape=[
            jax.ShapeDtypeStruct((t, d), F32),
            jax.ShapeDtypeStruct((t, d), BF16),
            jax.ShapeDtypeStruct((1, d), F32),
            jax.ShapeDtypeStruct((1, d), F32),
        ],
        scratch_shapes=[pltpu.VMEM((tt, d), F32)],
        args=(d_f1, w_ff1t, dy, x2, g3, m, g2), carry=carry)


def _wo_bwd_mix(dm, w_o, br_a, br_b, proj, b_gate, ga_block, gb_block, carry=None):
    t, d = dm.shape
    tt, tn = _tile(t, 1024), 512
    nj = d // tn

    def body(dm_ref, w_ref, bra_ref, brb_ref, ga_ref, gb_ref, ba_ref, bb_ref,
             dbra_ref, dbrb_ref, dga_ref, dgb_ref, dba_ref, dbb_ref):
        i = pl.program_id(1)

        @pl.when(i == 0)
        def _():
            dba_ref[...] = jnp.zeros_like(dba_ref)
            dbb_ref[...] = jnp.zeros_like(dbb_ref)

        d_mix = _dot_nt(dm_ref[...], w_ref[...])
        ga = _sig(ga_ref[...].astype(F32) + ba_ref[...])
        gb = _sig(gb_ref[...].astype(F32) + bb_ref[...])
        dbra_ref[...] = (d_mix * ga).astype(BF16)
        dbrb_ref[...] = (d_mix * gb).astype(BF16)
        dga = d_mix * bra_ref[...].astype(F32) * (ga * (1.0 - ga))
        dgb = d_mix * brb_ref[...].astype(F32) * (gb * (1.0 - gb))
        dga_ref[...] = dga.astype(BF16)
        dgb_ref[...] = dgb.astype(BF16)
        dba_ref[...] += jnp.sum(dga, axis=0, keepdims=True)
        dbb_ref[...] += jnp.sum(dgb, axis=0, keepdims=True)

    blk = pl.BlockSpec((tt, tn), lambda j, i: (i, j))
    vec = pl.BlockSpec((1, tn), lambda j, i: (0, j))
    return _call(
        body, name="wo_bwd_mix", grid=(nj, t // tt),
        in_specs=[
            pl.BlockSpec((tt, d), lambda j, i: (i, 0)),
            pl.BlockSpec((tn, d), lambda j, i: (j, 0)),
            blk, blk,
            pl.BlockSpec((tt, tn), lambda j, i: (i, ga_block + j)),
            pl.BlockSpec((tt, tn), lambda j, i: (i, gb_block + j)),
            vec,
            pl.BlockSpec((1, tn), lambda j, i: (0, nj + j)),
        ],
        out_specs=[blk, blk, blk, blk, vec, vec],
        out_shape=[jax.ShapeDtypeStruct((t, d), BF16)] * 4 + [jax.ShapeDtypeStruct((1, d), F32)] * 2,
        args=(dm, w_o, br_a, br_b, proj, proj, b_gate, b_gate), carry=carry)


def _lru_up_bwd(d_br_a, w_lru_up, proj, h, g_block, carry=None):
    t, d = d_br_a.shape
    tt, tn = _tile(t, 1024), 512

    def body(a_ref, w_ref, g_ref, h_ref, dh_ref, dg_ref):
        d_y = _dot_nt(a_ref[...], w_ref[...])
        gel, gel_grad = _gelu_and_grad(g_ref[...].astype(F32))
        dh_ref[...] = d_y * gel
        dg_ref[...] = (d_y * h_ref[...] * gel_grad).astype(BF16)

    blk = pl.BlockSpec((tt, tn), lambda i, j: (i, j))
    return _call(
        body, name="lru_up_bwd", grid=(t // tt, d // tn),
        in_specs=[
            pl.BlockSpec((tt, d), lambda i, j: (i, 0)),
            pl.BlockSpec((tn, d), lambda i, j: (j, 0)),
            pl.BlockSpec((tt, tn), lambda i, j: (i, g_block + j)),
            blk,
        ],
        out_specs=[blk, blk],
        out_shape=[jax.ShapeDtypeStruct((t, d), F32), jax.ShapeDtypeStruct((t, d), BF16)],
        args=(d_br_a, w_lru_up, proj, h), carry=carry)


def _pool_up_bwd(d_br_b, w_pool_upt):
    t, d = d_br_b.shape
    dp = w_pool_upt.shape[1]
    tt = _tile(t, 2048)

    def body(a_ref, w_ref, out_ref):
        out_ref[...] = _dot_nn(a_ref[...], w_ref[...])

    return _call(
        body, name="pool_up_bwd", grid=(t // tt,),
        in_specs=[pl.BlockSpec((tt, d), lambda i: (i, 0)), pl.BlockSpec((d, dp), lambda i: (0, 0))],
        out_specs=[pl.BlockSpec((tt, dp), lambda i: (i, 0))],
        out_shape=[jax.ShapeDtypeStruct((t, dp), F32)],
        args=(d_br_b, w_pool_upt))[0][0]


def _lru_bwd(dh, xc, h, proj, conv_w, w_a, b_a, w_x, b_x, lam, carry=None):
    t, dr = dh.shape
    cb = LRU_CB
    hd = LRU_HEAD_DIM
    per = cb // hd
    tc = _tile(t, 256)
    ncb, ntc = dr // cb, t // tc

    def body(dh_ref, xc_ref, h_ref, hp_ref, xp_ref, cw_ref, wa_ref, ba_ref, wx_ref, bx_ref, lam_ref,
             dxp_ref, dwa_ref, dba_ref, dwx_ref, dbx_ref, dlam_ref, dcw_ref, dcb_ref,
             nextd_s, anext_s, gnext_s, tmp_s, wa_s, wx_s):
        c = pl.program_id(1)
        rc = ntc - 1 - c

        @pl.when(c == 0)
        def _():
            nextd_s[...] = jnp.zeros_like(nextd_s)
            anext_s[...] = jnp.zeros_like(anext_s)
            gnext_s[...] = jnp.zeros_like(gnext_s)
            for ref in (dwa_ref, dba_ref, dwx_ref, dbx_ref, dlam_ref, dcw_ref, dcb_ref):
                ref[...] = jnp.zeros_like(ref)
            _fill_block_diag(wa_ref, wa_s)
            _fill_block_diag(wx_ref, wx_s)

        xc = xc_ref[...]
        wa, wx, lam = wa_s[...], wx_s[...], lam_ref[...]
        xcb, r, i, sp, log_a, a, mult = _lru_gates(xc, wa, ba_ref[...], wx, bx_ref[...], lam)
        row = lax.broadcasted_iota(jnp.int32, xc.shape, 0)
        h = h_ref[...]
        hp = jnp.where(rc == 0, 0.0, hp_ref[...])
        hprev = jnp.where(row >= 1, pltpu.roll(h, 1, 0), pltpu.roll(hp, 1, 0))

        def up(v, nv, j):
            return jnp.where(row < tc - j, pltpu.roll(v, tc - j, 0), nv)

        av = up(a, anext_s[...], 1)
        bv = dh_ref[...]
        s = 1
        while s < tc:
            a_sh = up(av, 1.0, s)
            b_sh = up(bv, 0.0, s)
            bv = av * b_sh + bv
            av = av * a_sh
            s *= 2
        gt = av * gnext_s[...] + bv
        tmp_s[...] = gt
        gnext_s[...] = tmp_s[0:1, :]
        tmp_s[...] = a
        anext_s[...] = tmp_s[0:1, :]

        da = gt * hprev
        ixc = i * xc
        d_mult = gt * ixc
        d_i = gt * mult * xc
        d_xc = gt * mult * i
        d_log_a = da * a - d_mult * (a * a) / mult
        d_pre_r = (d_log_a * ((-LRU_C) * sp)) * (r * (1.0 - r))
        d_pre_i = d_i * (i * (1.0 - i))
        d_sp = jnp.sum(d_log_a * ((-LRU_C) * r), axis=0, keepdims=True)
        dlam_ref[...] += d_sp * (-1.0 / (1.0 + jnp.exp(lam)))
        dpr = d_pre_r.astype(BF16)
        dpi = d_pre_i.astype(BF16)
        dba_ref[...] += jnp.sum(d_pre_r, axis=0, keepdims=True)
        dbx_ref[...] += jnp.sum(d_pre_i, axis=0, keepdims=True)
        pa = _dot_tn(xcb, dpr)
        px = _dot_tn(xcb, dpi)
        for k in range(per):
            dwa_ref[k] += pa[k * hd:(k + 1) * hd, k * hd:(k + 1) * hd]
            dwx_ref[k] += px[k * hd:(k + 1) * hd, k * hd:(k + 1) * hd]
        d_xc = d_xc + _dot_nt(dpr, wa) + _dot_nt(dpi, wx)

        nxt = nextd_s[...]
        xp = xp_ref[...].astype(F32)
        dxp = cw_ref[3:4, :] * d_xc
        dcw_ref[3:4, :] += jnp.sum(xp * d_xc, axis=0, keepdims=True)
        for j in (1, 2, 3):
            uj = up(d_xc, pltpu.roll(nxt, tc - j, 0), j)
            dxp = dxp + cw_ref[3 - j:4 - j, :] * uj
            dcw_ref[3 - j:4 - j, :] += jnp.sum(xp * uj, axis=0, keepdims=True)
        dcb_ref[...] += jnp.sum(d_xc, axis=0, keepdims=True)
        nextd_s[...] = d_xc
        dxp_ref[...] = dxp.astype(BF16)

    vec = pl.BlockSpec((1, cb), lambda j, c: (0, j))
    blk = pl.BlockSpec((tc, cb), lambda j, c: (ntc - 1 - c, j))
    mat = pl.BlockSpec((per, hd, hd), lambda j, c: (j, 0, 0))
    cwb = pl.BlockSpec((4, cb), lambda j, c: (0, j))
    return _call(
        body, name="lru_bwd", grid=(ncb, ntc),
        in_specs=[
            blk, blk, blk,
            pl.BlockSpec((tc, cb), lambda j, c: (jnp.maximum(ntc - 2 - c, 0), j)),
            blk, cwb, mat, vec, mat, vec, vec,
        ],
        out_specs=[blk, mat, vec, mat, vec, vec, cwb, vec],
        out_shape=[
            jax.ShapeDtypeStruct((t, dr), BF16),
            jax.ShapeDtypeStruct(w_a.shape, F32),
            jax.ShapeDtypeStruct((1, dr), F32),
            jax.ShapeDtypeStruct(w_x.shape, F32),
            jax.ShapeDtypeStruct((1, dr), F32),
            jax.ShapeDtypeStruct((1, dr), F32),
            jax.ShapeDtypeStruct((4, dr), F32),
            jax.ShapeDtypeStruct((1, dr), F32),
        ],
        scratch_shapes=[
            pltpu.VMEM((tc, cb), F32),
            pltpu.VMEM((1, cb), F32),
            pltpu.VMEM((1, cb), F32),
            pltpu.VMEM((tc, cb), F32),
            pltpu.VMEM((cb, cb), BF16),
            pltpu.VMEM((cb, cb), BF16),
        ],
        args=(dh, xc, h, h, proj, conv_w, w_a, b_a, w_x, b_x, lam), carry=carry)


def _pool_bwd(d_y_pool, p, pool_w, pool_scale):
    t, dp = d_y_pool.shape
    tc = _tile(t, 256)
    ntc = t // tc
    ng = len(POOL_WINDOWS)

    def body(dy_ref, p_ref, w_ref, sc_ref, dx_ref, dw_ref, dsc_ref, nz, n2, n4, n8, dp_s):
        c = pl.program_id(0)
        rc = ntc - 1 - c

        @pl.when(c == 0)
        def _():
            for s in (nz, n2, n4, n8):
                s[...] = jnp.zeros_like(s)
            dw_ref[...] = jnp.zeros_like(dw_ref)
            dsc_ref[...] = jnp.zeros_like(dsc_ref)

        for g in range(ng):
            sl = slice(g * POOL_GROUP_DIM, (g + 1) * POOL_GROUP_DIM)
            pg = p_ref[:, sl]
            dyg = dy_ref[:, sl]
            wg = w_ref[g].astype(BF16)
            q = _dot_nn(pg, wg)
            dsc_ref[:, sl] += jnp.sum(dyg * q, axis=0, keepdims=True)
            dpw = (dyg * sc_ref[:, sl]).astype(BF16)
            dw_ref[g] += _dot_tn(pg, dpw)
            dp_s[:, sl] = _dot_nt(dpw, wg)

        dpv = dp_s[...]
        row = lax.broadcasted_iota(jnp.int32, dpv.shape, 0)
        col = lax.broadcasted_iota(jnp.int32, dpv.shape, 1)
        win = _pool_select(col, POOL_WINDOWS)
        cnt = jnp.minimum(rc * tc + row + 1, win).astype(F32)
        z = dpv / cnt

        def up(v, nv, j):
            return jnp.where(row < tc - j, pltpu.roll(v, tc - j, 0), pltpu.roll(nv[...], tc - j, 0))

        u2 = z + up(z, nz, 1)
        u4 = u2 + up(u2, n2, 2)
        u8 = u4 + up(u4, n4, 4)
        u16 = u8 + up(u8, n8, 8)
        nz[...] = z
        n2[...] = u2
        n4[...] = u4
        n8[...] = u8
        dx_ref[...] = (_pool_select(col, (u2, u4, u8, u16)) - dpv).astype(BF16)

    blk = pl.BlockSpec((tc, dp), lambda c: (ntc - 1 - c, 0))
    full_w = pl.BlockSpec(pool_w.shape, lambda c: (0, 0, 0))
    vec = pl.BlockSpec((1, dp), lambda c: (0, 0))
    return _call(
        body, name="pool_bwd", grid=(ntc,),
        in_specs=[blk, blk, full_w, vec],
        out_specs=[blk, full_w, vec],
        out_shape=[
            jax.ShapeDtypeStruct((t, dp), BF16),
            jax.ShapeDtypeStruct(pool_w.shape, F32),
            jax.ShapeDtypeStruct((1, dp), F32),
        ],
        scratch_shapes=[pltpu.VMEM((tc, dp), F32)] * 5,
        args=(d_y_pool, p, pool_w, pool_scale))[0]


def _win_bwd_norm(parts, w_int, dx2, x, g1, carry=None):
    t, d = x.shape
    tk = 512
    tt = _tile(t, 1024)
    bounds = []
    k0 = 0
    for part in parts:
        assert part.shape[1] % tk == 0
        bounds.append((k0, k0 + part.shape[1] // tk))
        k0 += part.shape[1] // tk
    nk = k0
    assert nk * tk == w_int.shape[0]
    np_ = len(parts)

    def body(*refs):
        p_refs = refs[:np_]
        w_ref, dx2_ref, x_ref, g_ref, gx_ref, dg_ref, acc = refs[np_:]
        i, kk = pl.program_id(0), pl.program_id(1)

        @pl.when(kk == 0)
        def _():
            acc[...] = jnp.zeros_like(acc)

        @pl.when((i == 0) & (kk == 0))
        def _():
            dg_ref[...] = jnp.zeros_like(dg_ref)

        for (lo, hi), p_ref in zip(bounds, p_refs):
            @pl.when((kk >= lo) & (kk < hi))
            def _(p_ref=p_ref):
                acc[...] += _dot_nn(p_ref[...], w_ref[...])

        @pl.when(kk == nk - 1)
        def _():
            def tail(rows):
                xhat, r = _rms_hat(x_ref[rows, :])
                dx, dg = _rms_bwd(acc[rows, :], xhat, r, g_ref[...])
                gx_ref[rows, :] = dx2_ref[rows, :] + dx
                dg_ref[...] += dg

            _row_chunks(tt, tail)

    def part_spec(lo, hi):
        return pl.BlockSpec((tt, tk), lambda i, kk: (i, jnp.clip(kk - lo, 0, hi - lo - 1)))

    row = pl.BlockSpec((tt, d), lambda i, kk: (i, 0))
    vec = pl.BlockSpec((1, d), lambda i, kk: (0, 0))
    return _call(
        body, name="win_bwd_norm", grid=(t // tt, nk),
        in_specs=[part_spec(lo, hi) for lo, hi in bounds]
        + [pl.BlockSpec((tk, d), lambda i, kk: (kk, 0)), row, row, vec],
        out_specs=[row, vec],
        out_shape=[jax.ShapeDtypeStruct((t, d), F32), jax.ShapeDtypeStruct((1, d), F32)],
        scratch_shapes=[pltpu.VMEM((tt, d), F32)],
        args=(*parts, w_int, dx2, x, g1), carry=carry)


def _adam_math(w, g, m, v):
    m = ADAM_B1 * m + (1.0 - ADAM_B1) * g
    v = ADAM_B2 * v + (1.0 - ADAM_B2) * (g * g)
    m_hat = m / (1.0 - ADAM_B1 ** ADAM_STEP)
    v_hat = v / (1.0 - ADAM_B2 ** ADAM_STEP)
    delta = -ADAM_LR * (m_hat / (jnp.sqrt(v_hat) + ADAM_EPS) + ADAM_WD * w)
    return delta, m, v


def _adamw_big(ws, gs, ms, vs, carry=None):
    n = len(ws)
    nb = 8

    def body(*refs):
        for a in range(n):
            w_ref, g_ref, m_ref, v_ref = refs[4 * a:4 * a + 4]
            d_ref, nm_ref, nv_ref = refs[4 * n + 3 * a:4 * n + 3 * a + 3]
            dl, m, v = _adam_math(w_ref[...], g_ref[...], m_ref[...], v_ref[...])
            d_ref[...] = dl
            nm_ref[...] = m
            nv_ref[...] = v

    in_specs, out_specs, out_shape, args = [], [], [], []
    for w, g, m, v in zip(ws, gs, ms, vs):
        rows, cols = w.shape
        blk = pl.BlockSpec((rows // nb, cols), lambda i: (i, 0))
        in_specs += [blk] * 4
        args += [w, g, m, v]
        out_specs += [blk] * 3
        out_shape += [jax.ShapeDtypeStruct(w.shape, F32)] * 3
    outs, got = _call(body, name="adamw_big", grid=(nb,), in_specs=in_specs, out_specs=out_specs,
                      out_shape=out_shape, args=args, carry=carry)
    return [tuple(outs[3 * a:3 * a + 3]) for a in range(n)], got


SMALL_ORDER = ("norm_mix_pre", "norm_mix_post", "norm_mlp_pre", "norm_mlp_post", "b_gate", "conv_w", "conv_b",
               "lru_w_a", "lru_b_a", "lru_w_x", "lru_b_x", "lru_lambda", "pool_w", "pool_scale")
VEC_ROW = dict(norm_mix_pre=0, norm_mix_post=1, norm_mlp_pre=2, norm_mlp_post=3, conv_b=6, lru_b_a=7,
               lru_b_x=8, lru_lambda=9)
ROW_B_GATE, ROW_POOL_SCALE, ROW_CONV_W, ROW_LOSS, N_VEC_ROWS = 4, 10, 11, 15, 16


def _adamw_small(vec_parts, g_pool, g_wa, g_wx, me, params):
    d = vec_parts.shape[2]
    names = SMALL_ORDER
    n = len(names)
    cw_cols = params["conv_w"][0].shape[2]

    def body(me_ref, vec_ref, vecc_ref, gp_ref, gwa_ref, gwx_ref, *refs):
        wmv = refs[:3 * n]
        loss_ref = refs[3 * n]
        outs = refs[3 * n + 1:3 * n + 1 + 4 * n]
        vs, vsc = refs[3 * n + 1 + 4 * n:]
        acc, accc = vec_ref[0], vecc_ref[0]
        for k in range(1, N_DEV):
            acc = acc + vec_ref[k]
            accc = accc + vecc_ref[k]
        vs[...] = acc
        vsc[...] = accc
        loss_ref[...] = vs[ROW_LOSS:ROW_LOSS + 1, 0:128]

        def upd(a, g, idx):
            w_ref, m_ref, v_ref = wmv[3 * a:3 * a + 3]
            g_ref, d_ref, nm_ref, nv_ref = outs[4 * a:4 * a + 4]
            dl, m, v = _adam_math(w_ref[idx], g, m_ref[idx], v_ref[idx])
            g_ref[idx] = g
            d_ref[idx] = dl
            nm_ref[idx] = m
            nv_ref[idx] = v

        for a, name in enumerate(names):
            if name in VEC_ROW:
                r = VEC_ROW[name]
                upd(a, vs[r:r + 1, :], (slice(None), slice(None)))
            elif name == "b_gate":
                for half in range(2):
                    r = ROW_B_GATE + half
                    upd(a, vs[r:r + 1, :], (slice(None), slice(half * d, (half + 1) * d)))
            elif name == "pool_scale":
                width = params[name][0].shape[1]
                upd(a, vs[ROW_POOL_SCALE:ROW_POOL_SCALE + 1, 0:width], (slice(None), slice(None)))
            elif name == "conv_w":
                upd(a, vsc[ROW_CONV_W:ROW_CONV_W + 4, :], (0,))
            elif name == "pool_w":
                upd(a, gp_ref[...], (Ellipsis,))
            elif name == "lru_w_a":
                upd(a, gwa_ref[...], (Ellipsis,))
            elif name == "lru_w_x":
                upd(a, gwx_ref[...], (Ellipsis,))
            else:
                raise ValueError(name)

    def whole(shape):
        nd = len(shape)
        return pl.BlockSpec(tuple(shape), lambda i, me_ref: (0,) * nd)

    in_specs = [
        whole(vec_parts.shape),
        pl.BlockSpec((N_DEV, N_VEC_ROWS, cw_cols), lambda i, me_ref: (0, 0, me_ref[0])),
        whole(g_pool.shape), whole(g_wa.shape), whole(g_wx.shape),
    ]
    args = [vec_parts, vec_parts, g_pool, g_wa, g_wx]
    out_specs = [whole((1, 128))]
    out_shape = [jax.ShapeDtypeStruct((1, 128), F32)]
    for name in names:
        for arr in params[name]:
            in_specs.append(whole(arr.shape))
            args.append(arr)
        shp = params[name][0].shape
        out_specs += [whole(shp)] * 4
        out_shape += [jax.ShapeDtypeStruct(shp, F32)] * 4
    grid_spec = pltpu.PrefetchScalarGridSpec(
        num_scalar_prefetch=1, grid=(1,), in_specs=in_specs, out_specs=out_specs,
        scratch_shapes=[pltpu.VMEM((N_VEC_ROWS, d), F32), pltpu.VMEM((N_VEC_ROWS, cw_cols), F32)])
    outs = pl.pallas_call(
        body, name="adamw_small", grid_spec=grid_spec, out_shape=out_shape,
        compiler_params=pltpu.CompilerParams(
            dimension_semantics=("arbitrary",), vmem_limit_bytes=V7X_VMEM_LIMIT_BYTES),
    )(me, *_in_hbm(args))
    return outs[0], {name: tuple(outs[1 + 4 * a:5 + 4 * a]) for a, name in enumerate(names)}


def _rs_sum(full, recv, shard_ids, slot_ids, name):
    r, rest = recv.shape[1], tuple(recv.shape[2:])
    zeros = (0,) * len(rest)
    send_dtype = recv.dtype

    def body(sh_ref, sl_ref, full_ref, recv_ref, own_ref, send_ref):
        s = pl.program_id(0)
        v = full_ref[...] + recv_ref[...].astype(F32)

        @pl.when(s == 0)
        def _():
            own_ref[...] = v

        @pl.when(s > 0)
        def _():
            send_ref[...] = v.astype(send_dtype)

    grid_spec = pltpu.PrefetchScalarGridSpec(
        num_scalar_prefetch=2,
        grid=(4,),
        in_specs=[
            pl.BlockSpec((r,) + rest, lambda s, sh, sl: (sh[s],) + zeros),
            pl.BlockSpec((None, r) + rest, lambda s, sh, sl: (sl[s], 0) + zeros),
        ],
        out_specs=[
            pl.BlockSpec((None, r) + rest, lambda s, sh, sl: (0, 0) + zeros),
            pl.BlockSpec((None, r) + rest, lambda s, sh, sl: (jnp.maximum(s - 1, 0), 0) + zeros),
        ],
    )
    return pl.pallas_call(
        body,
        name=name,
        grid_spec=grid_spec,
        out_shape=[jax.ShapeDtypeStruct((1, r) + rest, F32), jax.ShapeDtypeStruct((3, r) + rest, send_dtype)],
        compiler_params=pltpu.CompilerParams(
            dimension_semantics=("arbitrary",), vmem_limit_bytes=V7X_VMEM_LIMIT_BYTES),
    )(shard_ids, slot_ids, *_in_hbm([full, recv]))


def _finals(pairs, name, carry=None):
    nb = 4
    n = len(pairs)

    def body(*refs):
        for a in range(n):
            own_ref, recv_ref = refs[2 * a], refs[2 * a + 1]
            acc = own_ref[...]
            for k in range(3):
                acc = acc + recv_ref[k].astype(F32)
            refs[2 * n + a][...] = acc

    in_specs, out_specs, out_shape, args = [], [], [], []
    for own, recv in pairs:
        _, rows, cols = own.shape
        in_specs += [pl.BlockSpec((None, rows // nb, cols), lambda i: (0, i, 0)),
                     pl.BlockSpec((3, rows // nb, cols), lambda i: (0, i, 0))]
        args += [own, recv]
        out_specs.append(pl.BlockSpec((rows // nb, cols), lambda i: (i, 0)))
        out_shape.append(jax.ShapeDtypeStruct((rows, cols), F32))
    return _call(body, name=name, grid=(nb,), in_specs=in_specs, out_specs=out_specs,
                 out_shape=out_shape, args=args, carry=carry)


def _rs_sums(fulls_f32, recv1, tag):
    x, y, c = _place()
    qs = jnp.stack([2 * x + y, 2 * (1 - x) + y, 2 * x + (1 - y), 2 * (1 - x) + (1 - y)]).astype(jnp.int32)
    shard_ids = 2 * qs + c
    return [_rs_sum(f32, r1, shard_ids, qs, f"rs_sum_{tag}{a}")
            for a, (f32, r1) in enumerate(zip(fulls_f32, recv1))]


def _rs_level1(fulls_f32, fulls_send, tag):
    recv1 = _run_plan(_rs_sibling_plan(fulls_send), "rs_sibling_" + tag)
    return _rs_sums(fulls_f32, recv1, tag)


def _rows(g):
    return g.reshape(g.shape[0] * g.shape[1], g.shape[2])


def kernel(x, norm_mix_pre, norm_mix_post, norm_mlp_pre, norm_mlp_post, w_in, b_gate, conv_w, conv_b, lru_w_a, lru_b_a, lru_w_x, lru_b_x, lru_lambda, pool_w, pool_scale, w_lru_up, w_pool_up, w_o, w_ff1, w_ff2, loss_target, m_norm_mix_pre, m_norm_mix_post, m_norm_mlp_pre, m_norm_mlp_post, m_w_in, m_b_gate, m_conv_w, m_conv_b, m_lru_w_a, m_lru_b_a, m_lru_w_x, m_lru_b_x, m_lru_lambda, m_pool_w, m_pool_scale, m_w_lru_up, m_w_pool_up, m_w_o, m_w_ff1, m_w_ff2, v_norm_mix_pre, v_norm_mix_post, v_norm_mlp_pre, v_norm_mlp_post, v_w_in, v_b_gate, v_conv_w, v_conv_b, v_lru_w_a, v_lru_b_a, v_lru_w_x, v_lru_b_x, v_lru_lambda, v_pool_w, v_pool_scale, v_w_lru_up, v_w_pool_up, v_w_o, v_w_ff1, v_w_ff2):
    t, d = x.shape[1], x.shape[2]
    d_rnn = conv_b.shape[1]
    d_pool = pool_scale.shape[1]
    per = LRU_CB // LRU_HEAD_DIM
    xi, yi, ci = _place()
    me = 4 * xi + 2 * yi + ci

    x2d = x[0]
    tgt = loss_target[0]

    s_in = w_in[0].T.astype(BF16)
    s_lu = w_lru_up[0].astype(BF16)
    s_pu = w_pool_up[0].T.astype(BF16)
    s_o = w_o[0].astype(BF16)
    s_f1 = w_ff1[0].T.astype(BF16)
    s_f2 = w_ff2[0].astype(BF16)
    s_cw = jnp.pad(conv_w[0], ((0, 4), (0, 0)))

    g_in, g_cw = _run_plan(_ag_plan([s_in, s_cw]), "ag_w_in")
    w_int = _rows(g_in)
    conv_w_full = jnp.transpose(g_cw[:, :4, :], (1, 0, 2)).reshape(4, d_rnn)

    wa_bd, wx_bd = lru_w_a[0], lru_w_x[0]
    pw = pool_w[0]
    pw_bf = pw.astype(BF16)

    pool_block = (2 * d_rnn) // d_pool
    ga_block = (2 * d_rnn + d_pool) // 512
    gb_block = ga_block + d // 512
    g_block = d_rnn // 512

    r_f1, r_f2 = s_f1.shape[0], s_f2.shape[0]
    f1_cut = r_f1 // 4
    f2_cut = (3 * r_f2) // 8
    plan = _join([_ag_plan([s_lu, s_pu, s_o]), _ag_plan([s_f1], pieces=[(0, f1_cut)])])
    (proj, h1), got = _norm_proj(x2d, norm_mix_pre, w_int, carry=plan)
    (g_lu, g_pu, g_o), (g_f1,) = plan.split(got)
    w_lu, w_put, w_og = _rows(g_lu), _rows(g_pu), _rows(g_o)
    (y_lru, h, xc), (g_f1,) = _lru_fwd(
        proj, conv_w_full, conv_b, wa_bd, lru_b_a, wx_bd, lru_b_x, lru_lambda,
        carry=_ag_plan([s_f1], pieces=[(f1_cut, r_f1 - f1_cut)], bufs=[g_f1]))
    w_f1t = _rows(g_f1)
    y_pool, p = _pool_fwd(proj, pw_bf, pool_scale, pool_block)
    (br_a, br_b, mix), (g_f2,) = _branch_mix(
        y_lru, y_pool, w_lu, w_put, proj, b_gate, ga_block, gb_block,
        carry=_ag_plan([s_f2], pieces=[(0, f2_cut)]))
    (m, x2, h3), _ = _wo_norm(mix, w_og, x2d, norm_mix_post, norm_mlp_pre)
    (rf,), (g_f2,) = _ff1(
        h3, w_f1t, carry=_ag_plan([s_f2], pieces=[(f2_cut, r_f2 - f2_cut)], bufs=[g_f2]))
    w_f2 = _rows(g_f2)
    dy, df, dg4, loss_part = _ff2_loss(rf, w_f2, x2, norm_mlp_post, tgt)

    d_f1 = _ff2_bwd(df, w_f2, rf)
    (gw_ff2_32, gw_ff2_16), _ = _wgrad(rf, df, "wgrad_ff2", square_a=True)
    (gw_ff1_32, gw_ff1_16), r1_ff2 = _wgrad(d_f1, h3, "wgrad_ff1", carry=_rs_sibling_plan([gw_ff2_16]))
    ((own_ff2, send_ff2),) = _rs_sums([gw_ff2_32], r1_ff2, "ff2")
    plan = _join([_rs_chips_plan([send_ff2]), _rs_sibling_plan([gw_ff1_16])])
    (dx2, dm, dg3, dg2), got = _ff1_bwd_norms(d_f1, w_f1t, dy, x2, norm_mlp_pre, m, norm_mix_post, carry=plan)
    (r2_ff2,), r1_ff1 = plan.split(got)
    ((own_ff1, send_ff1),) = _rs_sums([gw_ff1_32], r1_ff1, "ff1")
    cut = send_ff1.shape[1] // 4
    (gw_o_32, gw_o_16), _ = _wgrad(mix, dm, "wgrad_o")
    (d_br_a, d_br_b, p_ga, p_gb, dbg_a, dbg_b), (r2_ff1,) = _wo_bwd_mix(
        dm, w_og, br_a, br_b, proj, b_gate, ga_block, gb_block,
        carry=_rs_chips_plan([send_ff1], pieces=[(0, cut)]))
    (gw_lu_32, gw_lu_16), _ = _wgrad(y_lru, d_br_a, "wgrad_lru_up")
    (gw_pu_32, gw_pu_16), _ = _wgrad(d_br_b, y_pool, "wgrad_pool_up")
    (dh, p_g), r1_mid = _lru_up_bwd(
        d_br_a, w_lu, proj, h, g_block,
        carry=_rs_sibling_plan([gw_o_16, gw_lu_16, gw_pu_16.reshape(-1, d)]))
    mid = _rs_sums([gw_o_32, gw_lu_32, gw_pu_32.reshape(-1, d)], r1_mid, "mid")
    d_y_pool = _pool_up_bwd(d_br_b, w_put)
    (p_x, dwa, db_a, dwx, db_x, dlam, dconv_w, dconv_b), (r2_ff1,) = _lru_bwd(
        dh, xc, h, proj, conv_w_full, wa_bd, lru_b_a, wx_bd, lru_b_x, lru_lambda,
        carry=_rs_chips_plan([send_ff1], pieces=[(cut, send_ff1.shape[1] - cut)], bufs=[r2_ff1]))
    p_p, dpool_w, dpool_scale = _pool_bwd(d_y_pool, p, pw, pool_scale)
    parts = [p_x, p_g, p_p, p_ga, p_gb]
    gw_in, r2_mid = _wgrad_parts(parts, h1, "wgrad_in", carry=_rs_chips_plan([s for _, s in mid]))
    tail = _rs_level1([gw_in[0], dpool_w.reshape(N_DEV, -1, POOL_GROUP_DIM), dwa, dwx],
                      [gw_in[1], dpool_w.reshape(N_DEV, -1, POOL_GROUP_DIM), dwa, dwx], "in")
    (grad_x, dg1), r2_tail = _win_bwd_norm(parts, w_int, dx2, x2d, norm_mix_pre,
                                           carry=_rs_chips_plan([s for _, s in tail]))

    def flat2(a):
        return a.reshape(a.shape[0], -1, a.shape[-1])

    fin_small, _ = _finals([
        (flat2(tail[1][0]), flat2(r2_tail[1])), (flat2(tail[2][0]), flat2(r2_tail[2])),
        (flat2(tail[3][0]), flat2(r2_tail[3])),
    ], "rs_finals_small")

    def pad_row(a):
        return jnp.pad(a, ((0, 0), (0, d - a.shape[1])))

    vecs = jnp.concatenate([dg1, dg2, dg3, dg4, dbg_a, dbg_b, dconv_b, db_a, db_x, dlam,
                            pad_row(dpool_scale), dconv_w, pad_row(loss_part)], axis=0)
    assert vecs.shape[0] == N_VEC_ROWS
    fin, (vec_parts, g_pool, g_wa, g_wx) = _finals([
        (tail[0][0], r2_tail[0]), (mid[1][0], r2_mid[1]), (mid[2][0], r2_mid[2]), (mid[0][0], r2_mid[0]),
        (own_ff1, r2_ff1), (own_ff2, r2_ff2),
    ], "rs_finals", carry=_ag_plan([vecs] + fin_small))
    g_w_in = fin[0].T
    g_w_lru_up = fin[1]
    g_w_pool_up = fin[2].reshape(d // N_DEV, d_pool).T
    g_w_o = fin[3]
    g_w_ff1 = fin[4].T
    g_w_ff2 = fin[5]

    big_names = ["w_in", "w_lru_up", "w_pool_up", "w_o", "w_ff1", "w_ff2"]
    big_w = [w_in, w_lru_up, w_pool_up, w_o, w_ff1, w_ff2]
    big_g = [g_w_in, g_w_lru_up, g_w_pool_up, g_w_o, g_w_ff1, g_w_ff2]
    big_m = [m_w_in, m_w_lru_up, m_w_pool_up, m_w_o, m_w_ff1, m_w_ff2]
    big_v = [v_w_in, v_w_lru_up, v_w_pool_up, v_w_o, v_w_ff1, v_w_ff2]
    big_out, _ = _adamw_big([w[0] for w in big_w], big_g, [mm[0] for mm in big_m], [vv[0] for vv in big_v])

    small = dict(
        norm_mix_pre=(norm_mix_pre, m_norm_mix_pre, v_norm_mix_pre),
        norm_mix_post=(norm_mix_post, m_norm_mix_post, v_norm_mix_post),
        norm_mlp_pre=(norm_mlp_pre, m_norm_mlp_pre, v_norm_mlp_pre),
        norm_mlp_post=(norm_mlp_post, m_norm_mlp_post, v_norm_mlp_post),
        b_gate=(b_gate, m_b_gate, v_b_gate), conv_w=(conv_w, m_conv_w, v_conv_w),
        conv_b=(conv_b, m_conv_b, v_conv_b), lru_w_a=(lru_w_a, m_lru_w_a, v_lru_w_a),
        lru_b_a=(lru_b_a, m_lru_b_a, v_lru_b_a), lru_w_x=(lru_w_x, m_lru_w_x, v_lru_w_x),
        lru_b_x=(lru_b_x, m_lru_b_x, v_lru_b_x), lru_lambda=(lru_lambda, m_lru_lambda, v_lru_lambda),
        pool_w=(pool_w, m_pool_w, v_pool_w), pool_scale=(pool_scale, m_pool_scale, v_pool_scale))
    loss_row, small_out = _adamw_small(
        vec_parts, g_pool.reshape(pool_w.shape), g_wa.reshape(lru_w_a.shape), g_wx.reshape(lru_w_x.shape),
        jnp.reshape(me, (1,)).astype(jnp.int32), small)
    grads = {n: o[0] for n, o in small_out.items()}
    delta = {n: o[1] for n, o in small_out.items()}
    new_m = {n: o[2] for n, o in small_out.items()}
    new_v = {n: o[3] for n, o in small_out.items()}

    for name, g, (dl, nm, nv) in zip(big_names, big_g, big_out):
        grads[name], delta[name], new_m[name], new_v[name] = g[None], dl[None], nm[None], nv[None]

    loss = loss_row[0, 0]
    order = ["norm_mix_pre", "norm_mix_post", "norm_mlp_pre", "norm_mlp_post", "w_in", "b_gate", "conv_w",
             "conv_b", "lru_w_a", "lru_b_a", "lru_w_x", "lru_b_x", "lru_lambda", "pool_w", "pool_scale",
             "w_lru_up", "w_pool_up", "w_o", "w_ff1", "w_ff2"]
    return (loss, grad_x[None], *[grads[n] for n in order], *[delta[n] for n in order],
            *[new_m[n] for n in order], *[new_v[n] for n in order])
```

```python
import functools
import math
import operator
import types

import jax
import jax.numpy as jnp
from jax import lax
from jax.experimental import pallas as pl
from jax.experimental.pallas import tpu as pltpu

F32 = jnp.float32
BF16 = jnp.bfloat16
NORM_EPS = 1e-6
LRU_C = 8.0
N_LRU_HEADS = 16
LRU_HEAD_DIM = 64
POOL_WINDOWS = (2, 4, 8, 16)
POOL_GROUP_DIM = 128
ADAM_LR = 0.001
ADAM_B1 = 0.9
ADAM_B2 = 0.999
ADAM_EPS = 1e-08
ADAM_WD = 0.01
ADAM_STEP = 10
N_DEV = 8
V7X_VMEM_LIMIT_BYTES = 56 * 1024 * 1024
LRU_CB = 256
MESH = pl.DeviceIdType.MESH
ANY = pl.BlockSpec(memory_space=pl.ANY)


def _tile(n, pref):
    t = min(n, pref)
    assert n % t == 0, (n, pref)
    return t


def _dot_nn(a, b):
    return lax.dot_general(a, b, (((1,), (0,)), ((), ())), preferred_element_type=F32)


def _dot_nt(a, b):
    return lax.dot_general(a, b, (((1,), (1,)), ((), ())), preferred_element_type=F32)


def _dot_tn(a, b):
    return lax.dot_general(a, b, (((0,), (0,)), ((), ())), preferred_element_type=F32)


def _row_chunks(n_rows, fn, chunk=256):
    chunk = min(chunk, n_rows)
    assert n_rows % chunk == 0

    def step(r, carry):
        fn(pl.ds(pl.multiple_of(r * chunk, chunk), chunk))
        return carry

    lax.fori_loop(0, n_rows // chunk, step, 0)


def _sig(x):
    return 1.0 / (1.0 + jnp.exp(-x))


def _rms_hat(x):
    r = lax.rsqrt(jnp.mean(x * x, axis=-1, keepdims=True) + NORM_EPS)
    return x * r, r


def _rms_bwd(dn, xhat, r, g):
    q = dn * g
    dx = r * (q - xhat * jnp.mean(q * xhat, axis=-1, keepdims=True))
    dg = jnp.sum(dn * xhat, axis=0, keepdims=True)
    return dx, dg


_GELU_K = math.sqrt(2.0 / math.pi)
_GELU_C = 0.044715


def _gelu_and_grad(g):
    t = jnp.tanh(_GELU_K * (g + _GELU_C * g * g * g))
    val = 0.5 * g * (1.0 + t)
    grad = 0.5 * (1.0 + t) + 0.5 * g * (1.0 - t * t) * (_GELU_K * (1.0 + 3.0 * _GELU_C * g * g))
    return val, grad


def _softplus_neg(lam):
    z = -lam
    e = jnp.exp(-jnp.abs(z))
    u = 1.0 + e
    d = u - 1.0
    l1p = jnp.where(d == 0.0, e, jnp.log(u) * (e / jnp.where(d == 0.0, 1.0, d)))
    return jnp.maximum(z, 0.0) + l1p


def _lru_gates(xc, wa, ba, wx, bx, lam):
    xcb = xc.astype(BF16)
    r = _sig(_dot_nn(xcb, wa) + ba)
    i = _sig(_dot_nn(xcb, wx) + bx)
    sp = _softplus_neg(lam)
    log_a = (-LRU_C) * r * sp
    a = jnp.exp(log_a)
    mult = jnp.sqrt(-jnp.tanh(log_a) * (1.0 + a * a))
    return xcb, r, i, sp, log_a, a, mult


def _place():
    return lax.axis_index("x"), lax.axis_index("y"), lax.axis_index("c")


def _ag_plan(shards, pieces=None, bufs=None):
    na = len(shards)
    n_kinds = 7

    def parts(ins, outs, sems):
        send_sems, recv_sems, local_sems = sems
        x, y, c = _place()
        me, sibling = (x, y, c), (x, y, 1 - c)
        x_nb, y_nb, diag = (1 - x, y), (x, 1 - y), (1 - x, 1 - y)
        relay_src = (c * (1 - x) + (1 - c) * x, c * y + (1 - c) * (1 - y))
        relay_dst = (c * x + (1 - c) * (1 - x), c * (1 - y) + (1 - c) * y)

        def own(a):
            return ins[a] if pieces is None else ins[a].at[pl.ds(*pieces[a])]

        def slot(a, px, py, pc):
            idx = 4 * px + 2 * py + pc
            return outs[a].at[idx] if pieces is None else outs[a].at[idx, pl.ds(*pieces[a])]

        def copy(a, k, block, to, src=None):
            return pltpu.make_async_remote_copy(
                src_ref=slot(a, *block) if src is None else src,
                dst_ref=slot(a, *block),
                send_sem=send_sems.at[a * n_kinds + k],
                recv_sem=recv_sems.at[a * n_kinds + k],
                device_id=to,
                device_id_type=MESH,
            )

        mine = [pltpu.make_async_copy(own(a), slot(a, *me), local_sems.at[a]) for a in range(na)]
        first, second, third = [], [], []
        for a in range(na):
            first += [copy(a, 0, me, sibling, src=own(a)), copy(a, 1, me, (*x_nb, c), src=own(a)),
                      copy(a, 2, me, (*y_nb, c), src=own(a))]
            second += [copy(a, 3, (*relay_src, c), (*relay_dst, c)), copy(a, 4, (*x_nb, c), sibling),
                       copy(a, 5, (*y_nb, c), sibling)]
            third.append(copy(a, 6, (*diag, c), sibling))
        return sibling, c, x_nb, y_nb, diag, copy, mine, first, second, third

    def start(ins, outs, sems):
        _, _, _, _, _, _, mine, first, _, _ = parts(ins, outs, sems)
        for cp in mine + first:
            cp.start()

    def middle(ins, outs, sems):
        _, c, x_nb, y_nb, _, copy, _, _, second, _ = parts(ins, outs, sems)
        for a in range(na):
            copy(a, 1, (*x_nb, c), (*x_nb, c)).wait_recv()
            copy(a, 2, (*y_nb, c), (*y_nb, c)).wait_recv()
        for cp in second:
            cp.start()

    def finish(ins, outs, sems):
        sibling, c, x_nb, y_nb, diag, copy, mine, first, second, third = parts(ins, outs, sems)
        for a in range(na):
            copy(a, 3, (*diag, c), (*diag, c)).wait_recv()
            third[a].start()
        for a in range(na):
            copy(a, 0, sibling, sibling).wait_recv()
            copy(a, 4, (*x_nb, 1 - c), sibling).wait_recv()
            copy(a, 5, (*y_nb, 1 - c), sibling).wait_recv()
            copy(a, 6, (*diag, 1 - c), sibling).wait_recv()
        for cp in first + second + third:
            cp.wait_send()
        for cp in mine:
            cp.wait()

    return types.SimpleNamespace(
        ins=list(shards) + list(bufs or []),
        out_shapes=[jax.ShapeDtypeStruct((N_DEV,) + s.shape, s.dtype) for s in shards],
        sems=[pltpu.SemaphoreType.DMA((n_kinds * na,)), pltpu.SemaphoreType.DMA((n_kinds * na,)),
              pltpu.SemaphoreType.DMA((na,))],
        aliases=[(na + a, a) for a in range(na)] if bufs else [],
        peers=frozenset({"sibling", "neighbours"}), start=start, middle=middle, finish=finish)


def _rs_sibling_plan(fulls):
    na = len(fulls)
    rs = [f.shape[0] // N_DEV for f in fulls]

    def copies(ins, outs, sems):
        send_sems, recv_sems = sems
        x, y, c = _place()
        out = []
        for a in range(na):
            for q in range(4):
                shard = 2 * q + (1 - c)
                out.append(pltpu.make_async_remote_copy(
                    src_ref=ins[a].at[pl.ds(shard * rs[a], rs[a])],
                    dst_ref=outs[a].at[q],
                    send_sem=send_sems.at[a * 4 + q],
                    recv_sem=recv_sems.at[a * 4 + q],
                    device_id=(x, y, 1 - c),
                    device_id_type=MESH,
                ))
        return out

    def start(ins, outs, sems):
        for cp in copies(ins, outs, sems):
            cp.start()

    def finish(ins, outs, sems):
        for cp in copies(ins, outs, sems):
            cp.wait()

    return types.SimpleNamespace(
        ins=list(fulls),
        out_shapes=[jax.ShapeDtypeStruct((4, r) + f.shape[1:], f.dtype) for r, f in zip(rs, fulls)],
        sems=[pltpu.SemaphoreType.DMA((4 * na,)), pltpu.SemaphoreType.DMA((4 * na,))],
        peers=frozenset({"sibling"}), start=start, finish=finish)


def _rs_chips_plan(sends, pieces=None, bufs=None):
    na = len(sends)

    def copies(ins, outs, sems):
        send_sems, recv_sems = sems
        x, y, c = _place()
        chips = [(1 - x, y), (x, 1 - y), (1 - x, 1 - y)]
        out = []
        for a in range(na):
            for k, chip in enumerate(chips):
                rows = (k,) if pieces is None else (k, pl.ds(*pieces[a]))
                out.append(pltpu.make_async_remote_copy(
                    src_ref=ins[a].at[rows],
                    dst_ref=outs[a].at[rows],
                    send_sem=send_sems.at[a * 3 + k],
                    recv_sem=recv_sems.at[a * 3 + k],
                    device_id=(*chip, c),
                    device_id_type=MESH,
                ))
        return out

    def start(ins, outs, sems):
        for cp in copies(ins, outs, sems):
            cp.start()

    def finish(ins, outs, sems):
        for cp in copies(ins, outs, sems):
            cp.wait()

    return types.SimpleNamespace(
        ins=list(sends) + list(bufs or []),
        out_shapes=[jax.ShapeDtypeStruct(s.shape, s.dtype) for s in sends],
        sems=[pltpu.SemaphoreType.DMA((3 * na,)), pltpu.SemaphoreType.DMA((3 * na,))],
        aliases=[(na + a, a) for a in range(na)] if bufs else [],
        peers=frozenset({"chips"}), start=start, finish=finish)


def _join(plans):
    ins, outs, sems, aliases, offs = [], [], [], [], []
    for p in plans:
        offs.append((len(ins), len(outs), len(sems)))
        aliases += [(len(ins) + ci, len(outs) + co) for ci, co in getattr(p, "aliases", [])]
        ins += p.ins
        outs += p.out_shapes
        sems += p.sems

    def cut(p, off, i, o, s):
        return (i[off[0]:off[0] + len(p.ins)], o[off[1]:off[1] + len(p.out_shapes)],
                s[off[2]:off[2] + len(p.sems)])

    def start(i, o, s):
        for p, off in zip(plans, offs):
            p.start(*cut(p, off, i, o, s))

    def middle(i, o, s):
        for p, off in zip(plans, offs):
            if getattr(p, "middle", None) is not None:
                p.middle(*cut(p, off, i, o, s))

    def finish(i, o, s):
        for p, off in zip(plans, offs):
            p.finish(*cut(p, off, i, o, s))

    def split(results):
        return [list(results[off[1]:off[1] + len(p.out_shapes)]) for p, off in zip(plans, offs)]

    return types.SimpleNamespace(ins=ins, out_shapes=outs, sems=sems, aliases=aliases,
                                 peers=frozenset().union(*[p.peers for p in plans]),
                                 start=start, middle=middle, finish=finish, split=split)


COLLECTIVE_ID = {frozenset({"sibling"}): 0, frozenset({"chips"}): 1, frozenset({"sibling", "chips"}): 2,
                 frozenset({"sibling", "neighbours"}): 3}


def _handshake(peers):
    x, y, c = _place()
    devs = []
    if "sibling" in peers:
        devs.append((x, y, 1 - c))
    if "neighbours" in peers:
        devs += [(1 - x, y, c), (x, 1 - y, c)]
    if "chips" in peers:
        assert "neighbours" not in peers
        devs += [(1 - x, y, c), (x, 1 - y, c), (1 - x, 1 - y, c)]
    barrier = pltpu.get_barrier_semaphore()
    for dev in devs:
        pl.semaphore_signal(barrier, inc=1, device_id=dev, device_id_type=MESH)
    pl.semaphore_wait(barrier, len(devs))


def _in_hbm(args):
    return [pltpu.with_memory_space_constraint(a, pltpu.HBM) for a in args]


def _run_plan(plan, name):
    n_in, n_out = len(plan.ins), len(plan.out_shapes)

    def body(*refs):
        ins, outs, sems = refs[:n_in], refs[n_in:n_in + n_out], refs[n_in + n_out:]
        _handshake(plan.peers)
        plan.start(ins, outs, sems)
        if getattr(plan, "middle", None) is not None:
            plan.middle(ins, outs, sems)
        plan.finish(ins, outs, sems)

    return pl.pallas_call(
        body,
        name=name,
        in_specs=[ANY] * n_in,
        out_specs=[ANY] * n_out,
        out_shape=plan.out_shapes,
        scratch_shapes=plan.sems,
        input_output_aliases=dict(getattr(plan, "aliases", [])),
        compiler_params=pltpu.CompilerParams(collective_id=COLLECTIVE_ID[plan.peers]),
    )(*_in_hbm(plan.ins))


def _call(body, *, name, grid, in_specs, out_specs, out_shape, args, scratch_shapes=(), aliases=None,
          carry=None):
    n_in, n_out, n_scr = len(in_specs), len(out_shape), len(scratch_shapes)
    params = pltpu.CompilerParams(
        dimension_semantics=("arbitrary",) * len(grid), vmem_limit_bytes=V7X_VMEM_LIMIT_BYTES)
    if carry is None:
        outs = pl.pallas_call(
            body, name=name, grid=grid, in_specs=list(in_specs), out_specs=list(out_specs),
            out_shape=list(out_shape), scratch_shapes=list(scratch_shapes),
            input_output_aliases=aliases or {}, compiler_params=params)(*_in_hbm(args))
        return list(outs), []
    c_in, c_out = len(carry.ins), len(carry.out_shapes)

    def full(*refs):
        p = 0
        ins = refs[p:p + n_in]
        p += n_in
        cins = refs[p:p + c_in]
        p += c_in
        outs = refs[p:p + n_out]
        p += n_out
        couts = refs[p:p + c_out]
        p += c_out
        scr = refs[p:p + n_scr]
        csems = refs[p + n_scr:]
        ids = [pl.program_id(a) for a in range(len(grid))]
        first = functools.reduce(operator.and_, [i == 0 for i in ids])
        last = functools.reduce(operator.and_, [i == g - 1 for i, g in zip(ids, grid)])

        @pl.when(first)
        def _():
            _handshake(carry.peers)
            carry.start(cins, couts, csems)

        if getattr(carry, "middle", None) is not None:
            n_steps = math.prod(grid)
            flat = functools.reduce(lambda acc, ig: acc * ig[1] + ig[0], zip(ids, grid), 0)

            @pl.when(flat == (2 * n_steps) // 3)
            def _():
                carry.middle(cins, couts, csems)

        body(*ins, *outs, *scr)

        @pl.when(last)
        def _():
            carry.finish(cins, couts, csems)

    all_aliases = dict(aliases or {})
    all_aliases.update({n_in + ci: n_out + co for ci, co in getattr(carry, "aliases", [])})
    params = pltpu.CompilerParams(
        dimension_semantics=("arbitrary",) * len(grid), vmem_limit_bytes=V7X_VMEM_LIMIT_BYTES,
        collective_id=COLLECTIVE_ID[carry.peers])
    outs = pl.pallas_call(
        full, name=name, grid=grid,
        in_specs=list(in_specs) + [ANY] * c_in,
        out_specs=list(out_specs) + [ANY] * c_out,
        out_shape=list(out_shape) + list(carry.out_shapes),
        scratch_shapes=list(scratch_shapes) + list(carry.sems),
        input_output_aliases=all_aliases, compiler_params=params)(*_in_hbm(args), *_in_hbm(carry.ins))
    return list(outs[:n_out]), list(outs[n_out:])


def _norm_proj(x, g1, w_int, carry=None):
    t, d = x.shape
    n = w_int.shape[0]
    tt, tn = _tile(t, 2048), _tile(n, 512)

    def body(x_ref, g_ref, w_ref, proj_ref, h1_ref, h1_s):
        @pl.when(pl.program_id(1) == 0)
        def _():
            def norm_rows(rows):
                xhat, _ = _rms_hat(x_ref[rows, :])
                h = (xhat * g_ref[...]).astype(BF16)
                h1_s[rows, :] = h
                h1_ref[rows, :] = h

            _row_chunks(tt, norm_rows)

        proj_ref[...] = _dot_nt(h1_s[...], w_ref[...]).astype(BF16)

    return _call(
        body, name="norm_proj", grid=(t // tt, n // tn),
        in_specs=[
            pl.BlockSpec((tt, d), lambda i, j: (i, 0)),
            pl.BlockSpec((1, d), lambda i, j: (0, 0)),
            pl.BlockSpec((tn, d), lambda i, j: (j, 0)),
        ],
        out_specs=[
            pl.BlockSpec((tt, tn), lambda i, j: (i, j)),
            pl.BlockSpec((tt, d), lambda i, j: (i, 0)),
        ],
        out_shape=[jax.ShapeDtypeStruct((t, n), BF16), jax.ShapeDtypeStruct((t, d), BF16)],
        scratch_shapes=[pltpu.VMEM((tt, d), BF16)],
        args=(x, g1, w_int), carry=carry)


def _fill_block_diag(w_ref, bd_ref):
    bd_ref[...] = jnp.zeros_like(bd_ref)
    hd = LRU_HEAD_DIM
    for k in range(w_ref.shape[0]):
        bd_ref[k * hd:(k + 1) * hd, k * hd:(k + 1) * hd] = w_ref[k].astype(BF16)


def _lru_fwd(proj, conv_w, conv_b, w_a, b_a, w_x, b_x, lam, carry=None):
    t = proj.shape[0]
    dr = conv_b.shape[1]
    cb = LRU_CB
    tc = _tile(t, 256)
    ncb, ntc = dr // cb, t // tc

    def body(xp_ref, g_ref, cw_ref, cb_ref, wa_ref, ba_ref, wx_ref, bx_ref, lam_ref,
             y_ref, h_ref, xc_ref, prevx_s, hlast_s, wa_s, wx_s):
        c = pl.program_id(1)

        @pl.when(c == 0)
        def _():
            prevx_s[...] = jnp.zeros_like(prevx_s)
            hlast_s[...] = jnp.zeros_like(hlast_s)
            _fill_block_diag(wa_ref, wa_s)
            _fill_block_diag(wx_ref, wx_s)

        x = xp_ref[...].astype(F32)
        prev = prevx_s[...]
        row = lax.broadcasted_iota(jnp.int32, x.shape, 0)

        def sh(j):
            return jnp.where(row >= j, pltpu.roll(x, j, 0), pltpu.roll(prev, j, 0))

        xc = (cb_ref[...] + cw_ref[0:1, :] * sh(3) + cw_ref[1:2, :] * sh(2)
              + cw_ref[2:3, :] * sh(1) + cw_ref[3:4, :] * x)
        prevx_s[...] = x
        xc_ref[...] = xc
        _, _, i, _, _, a, mult = _lru_gates(xc, wa_s[...], ba_ref[...], wx_s[...], bx_ref[...],
                                            lam_ref[...])
        av, bv = a, mult * (i * xc)
        s = 1
        while s < tc:
            a_sh = jnp.where(row >= s, pltpu.roll(av, s, 0), 1.0)
            b_sh = jnp.where(row >= s, pltpu.roll(bv, s, 0), 0.0)
            bv = av * b_sh + bv
            av = av * a_sh
            s *= 2
        h = av * hlast_s[...] + bv
        h_ref[...] = h
        hlast_s[...] = h_ref[tc - 1:tc, :]
        gel, _ = _gelu_and_grad(g_ref[...].astype(F32))
        y_ref[...] = (h * gel).astype(BF16)

    vec = pl.BlockSpec((1, cb), lambda j, c: (0, j))
    blk = pl.BlockSpec((tc, cb), lambda j, c: (c, j))
    mat = pl.BlockSpec((cb // LRU_HEAD_DIM, LRU_HEAD_DIM, LRU_HEAD_DIM), lambda j, c: (j, 0, 0))
    return _call(
        body, name="lru_fwd", grid=(ncb, ntc),
        in_specs=[
            blk,
            pl.BlockSpec((tc, cb), lambda j, c: (c, ncb + j)),
            pl.BlockSpec((4, cb), lambda j, c: (0, j)),
            vec, mat, vec, mat, vec, vec,
        ],
        out_specs=[blk, blk, blk],
        out_shape=[
            jax.ShapeDtypeStruct((t, dr), BF16),
            jax.ShapeDtypeStruct((t, dr), F32),
            jax.ShapeDtypeStruct((t, dr), F32),
        ],
        scratch_shapes=[pltpu.VMEM((tc, cb), F32), pltpu.VMEM((1, cb), F32),
                        pltpu.VMEM((cb, cb), BF16), pltpu.VMEM((cb, cb), BF16)],
        args=(proj, proj, conv_w, conv_b, w_a, b_a, w_x, b_x, lam), carry=carry)


def _pool_select(col, vals):
    out = vals[3]
    for g in (2, 1, 0):
        out = jnp.where(col < (g + 1) * POOL_GROUP_DIM, vals[g], out)
    return out


def _pool_fwd(proj, pool_w, pool_scale, col_block):
    t = proj.shape[0]
    dp = pool_scale.shape[1]
    tc = _tile(t, 256)
    ntc = t // tc

    def body(x_ref, w_ref, sc_ref, y_ref, p_ref, px, p2, p4, p8):
        c = pl.program_id(0)

        @pl.when(c == 0)
        def _():
            for s in (px, p2, p4, p8):
                s[...] = jnp.zeros_like(s)

        x = x_ref[...].astype(F32)
        row = lax.broadcasted_iota(jnp.int32, x.shape, 0)
        col = lax.broadcasted_iota(jnp.int32, x.shape, 1)

        def sh(v, pv, j):
            return jnp.where(row >= j, pltpu.roll(v, j, 0), pltpu.roll(pv[...], j, 0))

        s2 = x + sh(x, px, 1)
        s4 = s2 + sh(s2, p2, 2)
        s8 = s4 + sh(s4, p4, 4)
        s16 = s8 + sh(s8, p8, 8)
        px[...] = x
        p2[...] = s2
        p4[...] = s4
        p8[...] = s8
        wsum = _pool_select(col, (s2, s4, s8, s16))
        win = _pool_select(col, POOL_WINDOWS)
        cnt = jnp.minimum(c * tc + row + 1, win).astype(F32)
        p = wsum / cnt - x
        pb = p.astype(BF16)
        p_ref[...] = pb
        for g in range(len(POOL_WINDOWS)):
            sl = slice(g * POOL_GROUP_DIM, (g + 1) * POOL_GROUP_DIM)
            yg = _dot_nn(pb[:, sl], w_ref[g]) * sc_ref[:, sl]
            y_ref[:, sl] = yg.astype(BF16)

    return _call(
        body, name="pool_fwd", grid=(ntc,),
        in_specs=[
            pl.BlockSpec((tc, dp), lambda c: (c, col_block)),
            pl.BlockSpec(pool_w.shape, lambda c: (0, 0, 0)),
            pl.BlockSpec((1, dp), lambda c: (0, 0)),
        ],
        out_specs=[pl.BlockSpec((tc, dp), lambda c: (c, 0))] * 2,
        out_shape=[jax.ShapeDtypeStruct((t, dp), BF16)] * 2,
        scratch_shapes=[pltpu.VMEM((tc, dp), F32)] * 4,
        args=(proj, pool_w, pool_scale))[0]


def _branch_mix(y_lru, y_pool, w_lru_up, w_pool_upt, proj, b_gate, ga_block, gb_block, carry=None):
    t, d = y_lru.shape
    dp = y_pool.shape[1]
    tt, tn = _tile(t, 1024), 512
    nj = d // tn

    def body(yl_ref, yp_ref, wl_ref, wp_ref, ga_ref, gb_ref, ba_ref, bb_ref, bra_ref, brb_ref, mix_ref):
        br_a = _dot_nn(yl_ref[...], wl_ref[...])
        br_b = _dot_nt(yp_ref[...], wp_ref[...])
        bra_ref[...] = br_a.astype(BF16)
        brb_ref[...] = br_b.astype(BF16)
        ga = _sig(ga_ref[...].astype(F32) + ba_ref[...])
        gb = _sig(gb_ref[...].astype(F32) + bb_ref[...])
        mix_ref[...] = (ga * br_a + gb * br_b).astype(BF16)

    out = pl.BlockSpec((tt, tn), lambda j, i: (i, j))
    return _call(
        body, name="branch_mix", grid=(nj, t // tt),
        in_specs=[
            pl.BlockSpec((tt, d), lambda j, i: (i, 0)),
            pl.BlockSpec((tt, dp), lambda j, i: (i, 0)),
            pl.BlockSpec((d, tn), lambda j, i: (0, j)),
            pl.BlockSpec((tn, dp), lambda j, i: (j, 0)),
            pl.BlockSpec((tt, tn), lambda j, i: (i, ga_block + j)),
            pl.BlockSpec((tt, tn), lambda j, i: (i, gb_block + j)),
            pl.BlockSpec((1, tn), lambda j, i: (0, j)),
            pl.BlockSpec((1, tn), lambda j, i: (0, nj + j)),
        ],
        out_specs=[out, out, out],
        out_shape=[jax.ShapeDtypeStruct((t, d), BF16)] * 3,
        args=(y_lru, y_pool, w_lru_up, w_pool_upt, proj, proj, b_gate, b_gate), carry=carry)


def _wo_norm(mix, w_o, x, g2, g3, carry=None):
    t, d = x.shape
    tt = _tile(t, 512)

    def body(mix_ref, w_ref, x_ref, g2_ref, g3_ref, m_ref, x2_ref, h3_ref):
        m = _dot_nn(mix_ref[...], w_ref[...])
        m_ref[...] = m
        mhat, _ = _rms_hat(m)
        x2 = x_ref[...] + mhat * g2_ref[...]
        x2_ref[...] = x2
        xhat, _ = _rms_hat(x2)
        h3_ref[...] = (xhat * g3_ref[...]).astype(BF16)

    row = pl.BlockSpec((tt, d), lambda i: (i, 0))
    vec = pl.BlockSpec((1, d), lambda i: (0, 0))
    return _call(
        body, name="wo_norm", grid=(t // tt,),
        in_specs=[row, pl.BlockSpec((d, d), lambda i: (0, 0)), row, vec, vec],
        out_specs=[row, row, row],
        out_shape=[
            jax.ShapeDtypeStruct((t, d), F32),
            jax.ShapeDtypeStruct((t, d), F32),
            jax.ShapeDtypeStruct((t, d), BF16),
        ],
        args=(mix, w_o, x, g2, g3), carry=carry)


def _ff1(h3, w_ff1t, carry=None):
    t, d = h3.shape
    n = w_ff1t.shape[0]
    tt, tn = _tile(t, 2048), _tile(n, 512)

    def body(h_ref, w_ref, rf_ref):
        rf_ref[...] = jnp.maximum(_dot_nt(h_ref[...], w_ref[...]), 0.0).astype(BF16)

    out = pl.BlockSpec((tt, tn), lambda i, j: (i, j))
    return _call(
        body, name="ff1", grid=(t // tt, n // tn),
        in_specs=[pl.BlockSpec((tt, d), lambda i, j: (i, 0)), pl.BlockSpec((tn, d), lambda i, j: (j, 0))],
        out_specs=[out],
        out_shape=[jax.ShapeDtypeStruct((t, n), BF16)],
        args=(h3, w_ff1t), carry=carry)


def _ff2_loss(rf, w_ff2, x2, g4, target):
    t, k = rf.shape
    d = x2.shape[1]
    tt, tk = _tile(t, 1024), _tile(k, 512)
    nk = k // tk

    def body(a_ref, w_ref, x2_ref, g_ref, tg_ref, dy_ref, df_ref, dg_ref, loss_ref, acc):
        i, kk = pl.program_id(0), pl.program_id(1)

        @pl.when(kk == 0)
        def _():
            acc[...] = jnp.zeros_like(acc)

        @pl.when((i == 0) & (kk == 0))
        def _():
            dg_ref[...] = jnp.zeros_like(dg_ref)
            loss_ref[...] = jnp.zeros_like(loss_ref)

        rf_tile = a_ref[...]
        acc[...] += _dot_nn(rf_tile * rf_tile, w_ref[...])

        @pl.when(kk == nk - 1)
        def _():
            def tail(rows):
                fhat, r = _rms_hat(acc[rows, :])
                g = g_ref[...]
                e = x2_ref[rows, :] + fhat * g - tg_ref[rows, :]
                loss_ref[...] += 0.5 * jnp.sum(jnp.mean(e * e, axis=-1, keepdims=True))
                dy = e * (1.0 / d)
                dy_ref[rows, :] = dy.astype(BF16)
                df, dg = _rms_bwd(dy, fhat, r, g)
                df_ref[rows, :] = df.astype(BF16)
                dg_ref[...] += dg

            _row_chunks(tt, tail)

    row = pl.BlockSpec((tt, d), lambda i, kk: (i, 0))
    vec = pl.BlockSpec((1, d), lambda i, kk: (0, 0))
    return _call(
        body, name="ff2_loss", grid=(t // tt, nk),
        in_specs=[
            pl.BlockSpec((tt, tk), lambda i, kk: (i, kk)),
            pl.BlockSpec((tk, d), lambda i, kk: (kk, 0)),
            row, vec, row,
        ],
        out_specs=[row, row, vec, pl.BlockSpec((1, 128), lambda i, kk: (0, 0))],
        out_shape=[
            jax.ShapeDtypeStruct((t, d), BF16),
            jax.ShapeDtypeStruct((t, d), BF16),
            jax.ShapeDtypeStruct((1, d), F32),
            jax.ShapeDtypeStruct((1, 128), F32),
        ],
        scratch_shapes=[pltpu.VMEM((tt, d), F32)],
        args=(rf, w_ff2, x2, g4, target))[0]


def _ff2_bwd(df, w_ff2, rf):
    t, d = df.shape
    n = w_ff2.shape[0]
    tt, tn = _tile(t, 2048), _tile(n, 512)

    def body(df_ref, w_ref, rf_ref, out_ref):
        d_act = _dot_nt(df_ref[...], w_ref[...])
        out_ref[...] = (d_act * (2.0 * rf_ref[...].astype(F32))).astype(BF16)

    blk = pl.BlockSpec((tt, tn), lambda i, j: (i, j))
    return _call(
        body, name="ff2_bwd", grid=(t // tt, n // tn),
        in_specs=[pl.BlockSpec((tt, d), lambda i, j: (i, 0)), pl.BlockSpec((tn, d), lambda i, j: (j, 0)), blk],
        out_specs=[blk],
        out_shape=[jax.ShapeDtypeStruct((t, n), BF16)],
        args=(df, w_ff2, rf))[0][0]


def _wgrad(a, b, name, prev=None, row_off=0, rows=None, carry=None, square_a=False):
    t, m = a.shape
    n = b.shape[1]
    rows = m if rows is None else rows
    tm, tk = _tile(m, 512), _tile(t, 2048)
    nk = t // tk
    assert row_off % tm == 0
    off = row_off // tm

    def body(*refs):
        a_ref, b_ref = refs[0], refs[1]
        o32_ref, o16_ref, acc = refs[-3], refs[-2], refs[-1]
        kk = pl.program_id(1)

        @pl.when(kk == 0)
        def _():
            acc[...] = jnp.zeros_like(acc)

        a_tile = a_ref[...]
        acc[...] += _dot_tn(a_tile * a_tile if square_a else a_tile, b_ref[...])

        @pl.when(kk == nk - 1)
        def _():
            o32_ref[...] = acc[...]
            o16_ref[...] = acc[...].astype(BF16)

    in_specs = [pl.BlockSpec((tk, tm), lambda i, kk: (kk, i)), pl.BlockSpec((tk, n), lambda i, kk: (kk, 0))]
    args = [a, b]
    aliases = {}
    if prev is not None:
        in_specs += [ANY, ANY]
        args += list(prev)
        aliases = {2: 0, 3: 1}
    out = pl.BlockSpec((tm, n), lambda i, kk: (off + i, 0))
    return _call(
        body, name=name, grid=(m // tm, nk),
        in_specs=in_specs, out_specs=[out, out],
        out_shape=[jax.ShapeDtypeStruct((rows, n), F32), jax.ShapeDtypeStruct((rows, n), BF16)],
        scratch_shapes=[pltpu.VMEM((tm, n), F32)],
        aliases=aliases, args=args, carry=carry)


def _wgrad_parts(parts, b, name, carry=None):
    t, n = b.shape
    tm = 512
    bounds = []
    lo = 0
    for part in parts:
        assert part.shape[0] == t and part.shape[1] % tm == 0
        bounds.append((lo, lo + part.shape[1] // tm))
        lo += part.shape[1] // tm
    nm = lo
    np_ = len(parts)

    def body(*refs):
        p_refs, b_ref, o32_ref, o16_ref = refs[:np_], refs[np_], refs[np_ + 1], refs[np_ + 2]
        i = pl.program_id(0)
        for (lo_p, hi_p), p_ref in zip(bounds, p_refs):
            @pl.when((i >= lo_p) & (i < hi_p))
            def _(p_ref=p_ref):
                res = _dot_tn(p_ref[...], b_ref[...])
                o32_ref[...] = res
                o16_ref[...] = res.astype(BF16)

    def part_spec(lo_p, hi_p):
        return pl.BlockSpec((t, tm), lambda i: (0, jnp.clip(i - lo_p, 0, hi_p - lo_p - 1)))

    out = pl.BlockSpec((tm, n), lambda i: (i, 0))
    return _call(
        body, name=name, grid=(nm,),
        in_specs=[part_spec(lo_p, hi_p) for lo_p, hi_p in bounds] + [pl.BlockSpec((t, n), lambda i: (0, 0))],
        out_specs=[out, out],
        out_shape=[jax.ShapeDtypeStruct((nm * tm, n), F32), jax.ShapeDtypeStruct((nm * tm, n), BF16)],
        args=(*parts, b), carry=carry)


def _ff1_bwd_norms(d_f1, w_ff1t, dy, x2, g3, m, g2, carry=None):
    t, k = d_f1.shape
    d = x2.shape[1]
    tt, tk = _tile(t, 1024), _tile(k, 512)
    nk = k // tk

    def body(a_ref, w_ref, dy_ref, x2_ref, g3_ref, m_ref, g2_ref, dx2_ref, dm_ref, dg3_ref, dg2_ref, acc):
        i, kk = pl.program_id(0), pl.program_id(1)

        @pl.when(kk == 0)
        def _():
            acc[...] = jnp.zeros_like(acc)

        @pl.when((i == 0) & (kk == 0))
        def _():
            dg3_ref[...] = jnp.zeros_like(dg3_ref)
            dg2_ref[...] = jnp.zeros_like(dg2_ref)

        acc[...] += _dot_nn(a_ref[...], w_ref[...])

        @pl.when(kk == nk - 1)
        def _():
            def tail(rows):
                xhat, r3 = _rms_hat(x2_ref[rows, :])
                dx, dg3 = _rms_bwd(acc[rows, :], xhat, r3, g3_ref[...])
                dx2 = dy_ref[rows, :].astype(F32) + dx
                dx2_ref[rows, :] = dx2
                dg3_ref[...] += dg3
                mhat, r2 = _rms_hat(m_ref[rows, :])
                dm, dg2 = _rms_bwd(dx2, mhat, r2, g2_ref[...])
                dm_ref[rows, :] = dm.astype(BF16)
                dg2_ref[...] += dg2

            _row_chunks(tt, tail)

    row = pl.BlockSpec((tt, d), lambda i, kk: (i, 0))
    vec = pl.BlockSpec((1, d), lambda i, kk: (0, 0))
    return _call(
        body, name="ff1_bwd_norms", grid=(t // tt, nk),
        in_specs=[
            pl.BlockSpec((tt, tk), lambda i, kk: (i, kk)),
            pl.BlockSpec((tk, d), lambda i, kk: (kk, 0)),
            row, row, vec, row, vec,
        ],
        out_specs=[row, row, vec, vec],
        out_shape=[
            jax.ShapeDtypeStruct((t, d), F32),
            jax.ShapeDtypeStruct((t, d), BF16),
            jax.ShapeDtypeStruct((1, d), F32),
            jax.ShapeDtypeStruct((1, d), F32),
        ],
        scratch_shapes=[pltpu.VMEM((tt, d), F32)],
        args=(d_f1, w_ff1t, dy, x2, g3, m, g2), carry=carry)


def _wo_bwd_mix(dm, w_o, br_a, br_b, proj, b_gate, ga_block, gb_block, carry=None):
    t, d = dm.shape
    tt, tn = _tile(t, 1024), 512
    nj = d // tn

    def body(dm_ref, w_ref, bra_ref, brb_ref, ga_ref, gb_ref, ba_ref, bb_ref,
             dbra_ref, dbrb_ref, dga_ref, dgb_ref, dba_ref, dbb_ref):
        i = pl.program_id(1)

        @pl.when(i == 0)
        def _():
            dba_ref[...] = jnp.zeros_like(dba_ref)
            dbb_ref[...] = jnp.zeros_like(dbb_ref)

        d_mix = _dot_nt(dm_ref[...], w_ref[...])
        ga = _sig(ga_ref[...].astype(F32) + ba_ref[...])
        gb = _sig(gb_ref[...].astype(F32) + bb_ref[...])
        dbra_ref[...] = (d_mix * ga).astype(BF16)
        dbrb_ref[...] = (d_mix * gb).astype(BF16)
        dga = d_mix * bra_ref[...].astype(F32) * (ga * (1.0 - ga))
        dgb = d_mix * brb_ref[...].astype(F32) * (gb * (1.0 - gb))
        dga_ref[...] = dga.astype(BF16)
        dgb_ref[...] = dgb.astype(BF16)
        dba_ref[...] += jnp.sum(dga, axis=0, keepdims=True)
        dbb_ref[...] += jnp.sum(dgb, axis=0, keepdims=True)

    blk = pl.BlockSpec((tt, tn), lambda j, i: (i, j))
    vec = pl.BlockSpec((1, tn), lambda j, i: (0, j))
    return _call(
        body, name="wo_bwd_mix", grid=(nj, t // tt),
        in_specs=[
            pl.BlockSpec((tt, d), lambda j, i: (i, 0)),
            pl.BlockSpec((tn, d), lambda j, i: (j, 0)),
            blk, blk,
            pl.BlockSpec((tt, tn), lambda j, i: (i, ga_block + j)),
            pl.BlockSpec((tt, tn), lambda j, i: (i, gb_block + j)),
            vec,
            pl.BlockSpec((1, tn), lambda j, i: (0, nj + j)),
        ],
        out_specs=[blk, blk, blk, blk, vec, vec],
        out_shape=[jax.ShapeDtypeStruct((t, d), BF16)] * 4 + [jax.ShapeDtypeStruct((1, d), F32)] * 2,
        args=(dm, w_o, br_a, br_b, proj, proj, b_gate, b_gate), carry=carry)


def _lru_up_bwd(d_br_a, w_lru_up, proj, h, g_block, carry=None):
    t, d = d_br_a.shape
    tt, tn = _tile(t, 1024), 512

    def body(a_ref, w_ref, g_ref, h_ref, dh_ref, dg_ref):
        d_y = _dot_nt(a_ref[...], w_ref[...])
        gel, gel_grad = _gelu_and_grad(g_ref[...].astype(F32))
        dh_ref[...] = d_y * gel
        dg_ref[...] = (d_y * h_ref[...] * gel_grad).astype(BF16)

    blk = pl.BlockSpec((tt, tn), lambda i, j: (i, j))
    return _call(
        body, name="lru_up_bwd", grid=(t // tt, d // tn),
        in_specs=[
            pl.BlockSpec((tt, d), lambda i, j: (i, 0)),
            pl.BlockSpec((tn, d), lambda i, j: (j, 0)),
            pl.BlockSpec((tt, tn), lambda i, j: (i, g_block + j)),
            blk,
        ],
        out_specs=[blk, blk],
        out_shape=[jax.ShapeDtypeStruct((t, d), F32), jax.ShapeDtypeStruct((t, d), BF16)],
        args=(d_br_a, w_lru_up, proj, h), carry=carry)


def _pool_up_bwd(d_br_b, w_pool_upt):
    t, d = d_br_b.shape
    dp = w_pool_upt.shape[1]
    tt = _tile(t, 2048)

    def body(a_ref, w_ref, out_ref):
        out_ref[...] = _dot_nn(a_ref[...], w_ref[...])

    return _call(
        body, name="pool_up_bwd", grid=(t // tt,),
        in_specs=[pl.BlockSpec((tt, d), lambda i: (i, 0)), pl.BlockSpec((d, dp), lambda i: (0, 0))],
        out_specs=[pl.BlockSpec((tt, dp), lambda i: (i, 0))],
        out_shape=[jax.ShapeDtypeStruct((t, dp), F32)],
        args=(d_br_b, w_pool_upt))[0][0]


def _lru_bwd(dh, xc, h, proj, conv_w, w_a, b_a, w_x, b_x, lam, carry=None):
    t, dr = dh.shape
    cb = LRU_CB
    hd = LRU_HEAD_DIM
    per = cb // hd
    tc = _tile(t, 256)
    ncb, ntc = dr // cb, t // tc

    def body(dh_ref, xc_ref, h_ref, hp_ref, xp_ref, cw_ref, wa_ref, ba_ref, wx_ref, bx_ref, lam_ref,
             dxp_ref, dwa_ref, dba_ref, dwx_ref, dbx_ref, dlam_ref, dcw_ref, dcb_ref,
             nextd_s, anext_s, gnext_s, tmp_s, wa_s, wx_s):
        c = pl.program_id(1)
        rc = ntc - 1 - c

        @pl.when(c == 0)
        def _():
            nextd_s[...] = jnp.zeros_like(nextd_s)
            anext_s[...] = jnp.zeros_like(anext_s)
            gnext_s[...] = jnp.zeros_like(gnext_s)
            for ref in (dwa_ref, dba_ref, dwx_ref, dbx_ref, dlam_ref, dcw_ref, dcb_ref):
                ref[...] = jnp.zeros_like(ref)
            _fill_block_diag(wa_ref, wa_s)
            _fill_block_diag(wx_ref, wx_s)

        xc = xc_ref[...]
        wa, wx, lam = wa_s[...], wx_s[...], lam_ref[...]
        xcb, r, i, sp, log_a, a, mult = _lru_gates(xc, wa, ba_ref[...], wx, bx_ref[...], lam)
        row = lax.broadcasted_iota(jnp.int32, xc.shape, 0)
        h = h_ref[...]
        hp = jnp.where(rc == 0, 0.0, hp_ref[...])
        hprev = jnp.where(row >= 1, pltpu.roll(h, 1, 0), pltpu.roll(hp, 1, 0))

        def up(v, nv, j):
            return jnp.where(row < tc - j, pltpu.roll(v, tc - j, 0), nv)

        av = up(a, anext_s[...], 1)
        bv = dh_ref[...]
        s = 1
        while s < tc:
            a_sh = up(av, 1.0, s)
            b_sh = up(bv, 0.0, s)
            bv = av * b_sh + bv
            av = av * a_sh
            s *= 2
        gt = av * gnext_s[...] + bv
        tmp_s[...] = gt
        gnext_s[...] = tmp_s[0:1, :]
        tmp_s[...] = a
        anext_s[...] = tmp_s[0:1, :]

        da = gt * hprev
        ixc = i * xc
        d_mult = gt * ixc
        d_i = gt * mult * xc
        d_xc = gt * mult * i
        d_log_a = da * a - d_mult * (a * a) / mult
        d_pre_r = (d_log_a * ((-LRU_C) * sp)) * (r * (1.0 - r))
        d_pre_i = d_i * (i * (1.0 - i))
        d_sp = jnp.sum(d_log_a * ((-LRU_C) * r), axis=0, keepdims=True)
        dlam_ref[...] += d_sp * (-1.0 / (1.0 + jnp.exp(lam)))
        dpr = d_pre_r.astype(BF16)
        dpi = d_pre_i.astype(BF16)
        dba_ref[...] += jnp.sum(d_pre_r, axis=0, keepdims=True)
        dbx_ref[...] += jnp.sum(d_pre_i, axis=0, keepdims=True)
        pa = _dot_tn(xcb, dpr)
        px = _dot_tn(xcb, dpi)
        for k in range(per):
            dwa_ref[k] += pa[k * hd:(k + 1) * hd, k * hd:(k + 1) * hd]
            dwx_ref[k] += px[k * hd:(k + 1) * hd, k * hd:(k + 1) * hd]
        d_xc = d_xc + _dot_nt(dpr, wa) + _dot_nt(dpi, wx)

        nxt = nextd_s[...]
        xp = xp_ref[...].astype(F32)
        dxp = cw_ref[3:4, :] * d_xc
        dcw_ref[3:4, :] += jnp.sum(xp * d_xc, axis=0, keepdims=True)
        for j in (1, 2, 3):
            uj = up(d_xc, pltpu.roll(nxt, tc - j, 0), j)
            dxp = dxp + cw_ref[3 - j:4 - j, :] * uj
            dcw_ref[3 - j:4 - j, :] += jnp.sum(xp * uj, axis=0, keepdims=True)
        dcb_ref[...] += jnp.sum(d_xc, axis=0, keepdims=True)
        nextd_s[...] = d_xc
        dxp_ref[...] = dxp.astype(BF16)

    vec = pl.BlockSpec((1, cb), lambda j, c: (0, j))
    blk = pl.BlockSpec((tc, cb), lambda j, c: (ntc - 1 - c, j))
    mat = pl.BlockSpec((per, hd, hd), lambda j, c: (j, 0, 0))
    cwb = pl.BlockSpec((4, cb), lambda j, c: (0, j))
    return _call(
        body, name="lru_bwd", grid=(ncb, ntc),
        in_specs=[
            blk, blk, blk,
            pl.BlockSpec((tc, cb), lambda j, c: (jnp.maximum(ntc - 2 - c, 0), j)),
            blk, cwb, mat, vec, mat, vec, vec,
        ],
        out_specs=[blk, mat, vec, mat, vec, vec, cwb, vec],
        out_shape=[
            jax.ShapeDtypeStruct((t, dr), BF16),
            jax.ShapeDtypeStruct(w_a.shape, F32),
            jax.ShapeDtypeStruct((1, dr), F32),
            jax.ShapeDtypeStruct(w_x.shape, F32),
            jax.ShapeDtypeStruct((1, dr), F32),
            jax.ShapeDtypeStruct((1, dr), F32),
            jax.ShapeDtypeStruct((4, dr), F32),
            jax.ShapeDtypeStruct((1, dr), F32),
        ],
        scratch_shapes=[
            pltpu.VMEM((tc, cb), F32),
            pltpu.VMEM((1, cb), F32),
            pltpu.VMEM((1, cb), F32),
            pltpu.VMEM((tc, cb), F32),
            pltpu.VMEM((cb, cb), BF16),
            pltpu.VMEM((cb, cb), BF16),
        ],
        args=(dh, xc, h, h, proj, conv_w, w_a, b_a, w_x, b_x, lam), carry=carry)


def _pool_bwd(d_y_pool, p, pool_w, pool_scale):
    t, dp = d_y_pool.shape
    tc = _tile(t, 256)
    ntc = t // tc
    ng = len(POOL_WINDOWS)

    def body(dy_ref, p_ref, w_ref, sc_ref, dx_ref, dw_ref, dsc_ref, nz, n2, n4, n8, dp_s):
        c = pl.program_id(0)
        rc = ntc - 1 - c

        @pl.when(c == 0)
        def _():
            for s in (nz, n2, n4, n8):
                s[...] = jnp.zeros_like(s)
            dw_ref[...] = jnp.zeros_like(dw_ref)
            dsc_ref[...] = jnp.zeros_like(dsc_ref)

        for g in range(ng):
            sl = slice(g * POOL_GROUP_DIM, (g + 1) * POOL_GROUP_DIM)
            pg = p_ref[:, sl]
            dyg = dy_ref[:, sl]
            wg = w_ref[g].astype(BF16)
            q = _dot_nn(pg, wg)
            dsc_ref[:, sl] += jnp.sum(dyg * q, axis=0, keepdims=True)
            dpw = (dyg * sc_ref[:, sl]).astype(BF16)
            dw_ref[g] += _dot_tn(pg, dpw)
            dp_s[:, sl] = _dot_nt(dpw, wg)

        dpv = dp_s[...]
        row = lax.broadcasted_iota(jnp.int32, dpv.shape, 0)
        col = lax.broadcasted_iota(jnp.int32, dpv.shape, 1)
        win = _pool_select(col, POOL_WINDOWS)
        cnt = jnp.minimum(rc * tc + row + 1, win).astype(F32)
        z = dpv / cnt

        def up(v, nv, j):
            return jnp.where(row < tc - j, pltpu.roll(v, tc - j, 0), pltpu.roll(nv[...], tc - j, 0))

        u2 = z + up(z, nz, 1)
        u4 = u2 + up(u2, n2, 2)
        u8 = u4 + up(u4, n4, 4)
        u16 = u8 + up(u8, n8, 8)
        nz[...] = z
        n2[...] = u2
        n4[...] = u4
        n8[...] = u8
        dx_ref[...] = (_pool_select(col, (u2, u4, u8, u16)) - dpv).astype(BF16)

    blk = pl.BlockSpec((tc, dp), lambda c: (ntc - 1 - c, 0))
    full_w = pl.BlockSpec(pool_w.shape, lambda c: (0, 0, 0))
    vec = pl.BlockSpec((1, dp), lambda c: (0, 0))
    return _call(
        body, name="pool_bwd", grid=(ntc,),
        in_specs=[blk, blk, full_w, vec],
        out_specs=[blk, full_w, vec],
        out_shape=[
            jax.ShapeDtypeStruct((t, dp), BF16),
            jax.ShapeDtypeStruct(pool_w.shape, F32),
            jax.ShapeDtypeStruct((1, dp), F32),
        ],
        scratch_shapes=[pltpu.VMEM((tc, dp), F32)] * 5,
        args=(d_y_pool, p, pool_w, pool_scale))[0]


def _win_bwd_norm(parts, w_int, dx2, x, g1, carry=None):
    t, d = x.shape
    tk = 512
    tt = _tile(t, 1024)
    bounds = []
    k0 = 0
    for part in parts:
        assert part.shape[1] % tk == 0
        bounds.append((k0, k0 + part.shape[1] // tk))
        k0 += part.shape[1] // tk
    nk = k0
    assert nk * tk == w_int.shape[0]
    np_ = len(parts)

    def body(*refs):
        p_refs = refs[:np_]
        w_ref, dx2_ref, x_ref, g_ref, gx_ref, dg_ref, acc = refs[np_:]
        i, kk = pl.program_id(0), pl.program_id(1)

        @pl.when(kk == 0)
        def _():
            acc[...] = jnp.zeros_like(acc)

        @pl.when((i == 0) & (kk == 0))
        def _():
            dg_ref[...] = jnp.zeros_like(dg_ref)

        for (lo, hi), p_ref in zip(bounds, p_refs):
            @pl.when((kk >= lo) & (kk < hi))
            def _(p_ref=p_ref):
                acc[...] += _dot_nn(p_ref[...], w_ref[...])

        @pl.when(kk == nk - 1)
        def _():
            def tail(rows):
                xhat, r = _rms_hat(x_ref[rows, :])
                dx, dg = _rms_bwd(acc[rows, :], xhat, r, g_ref[...])
                gx_ref[rows, :] = dx2_ref[rows, :] + dx
                dg_ref[...] += dg

            _row_chunks(tt, tail)

    def part_spec(lo, hi):
        return pl.BlockSpec((tt, tk), lambda i, kk: (i, jnp.clip(kk - lo, 0, hi - lo - 1)))

    row = pl.BlockSpec((tt, d), lambda i, kk: (i, 0))
    vec = pl.BlockSpec((1, d), lambda i, kk: (0, 0))
    return _call(
        body, name="win_bwd_norm", grid=(t // tt, nk),
        in_specs=[part_spec(lo, hi) for lo, hi in bounds]
        + [pl.BlockSpec((tk, d), lambda i, kk: (kk, 0)), row, row, vec],
        out_specs=[row, vec],
        out_shape=[jax.ShapeDtypeStruct((t, d), F32), jax.ShapeDtypeStruct((1, d), F32)],
        scratch_shapes=[pltpu.VMEM((tt, d), F32)],
        args=(*parts, w_int, dx2, x, g1), carry=carry)


def _adam_math(w, g, m, v):
    m = ADAM_B1 * m + (1.0 - ADAM_B1) * g
    v = ADAM_B2 * v + (1.0 - ADAM_B2) * (g * g)
    m_hat = m / (1.0 - ADAM_B1 ** ADAM_STEP)
    v_hat = v / (1.0 - ADAM_B2 ** ADAM_STEP)
    delta = -ADAM_LR * (m_hat / (jnp.sqrt(v_hat) + ADAM_EPS) + ADAM_WD * w)
    return delta, m, v


def _adamw_big(ws, gs, ms, vs, carry=None):
    n = len(ws)
    nb = 8

    def body(*refs):
        for a in range(n):
            w_ref, g_ref, m_ref, v_ref = refs[4 * a:4 * a + 4]
            d_ref, nm_ref, nv_ref = refs[4 * n + 3 * a:4 * n + 3 * a + 3]
            dl, m, v = _adam_math(w_ref[...], g_ref[...], m_ref[...], v_ref[...])
            d_ref[...] = dl
            nm_ref[...] = m
            nv_ref[...] = v

    in_specs, out_specs, out_shape, args = [], [], [], []
    for w, g, m, v in zip(ws, gs, ms, vs):
        rows, cols = w.shape
        blk = pl.BlockSpec((rows // nb, cols), lambda i: (i, 0))
        in_specs += [blk] * 4
        args += [w, g, m, v]
        out_specs += [blk] * 3
        out_shape += [jax.ShapeDtypeStruct(w.shape, F32)] * 3
    outs, got = _call(body, name="adamw_big", grid=(nb,), in_specs=in_specs, out_specs=out_specs,
                      out_shape=out_shape, args=args, carry=carry)
    return [tuple(outs[3 * a:3 * a + 3]) for a in range(n)], got


SMALL_ORDER = ("norm_mix_pre", "norm_mix_post", "norm_mlp_pre", "norm_mlp_post", "b_gate", "conv_w", "conv_b",
               "lru_w_a", "lru_b_a", "lru_w_x", "lru_b_x", "lru_lambda", "pool_w", "pool_scale")
VEC_ROW = dict(norm_mix_pre=0, norm_mix_post=1, norm_mlp_pre=2, norm_mlp_post=3, conv_b=6, lru_b_a=7,
               lru_b_x=8, lru_lambda=9)
ROW_B_GATE, ROW_POOL_SCALE, ROW_CONV_W, ROW_LOSS, N_VEC_ROWS = 4, 10, 11, 15, 16


def _adamw_small(vec_parts, g_pool, g_wa, g_wx, me, params):
    d = vec_parts.shape[2]
    names = SMALL_ORDER
    n = len(names)
    cw_cols = params["conv_w"][0].shape[2]

    def body(me_ref, vec_ref, vecc_ref, gp_ref, gwa_ref, gwx_ref, *refs):
        wmv = refs[:3 * n]
        loss_ref = refs[3 * n]
        outs = refs[3 * n + 1:3 * n + 1 + 4 * n]
        vs, vsc = refs[3 * n + 1 + 4 * n:]
        acc, accc = vec_ref[0], vecc_ref[0]
        for k in range(1, N_DEV):
            acc = acc + vec_ref[k]
            accc = accc + vecc_ref[k]
        vs[...] = acc
        vsc[...] = accc
        loss_ref[...] = vs[ROW_LOSS:ROW_LOSS + 1, 0:128]

        def upd(a, g, idx):
            w_ref, m_ref, v_ref = wmv[3 * a:3 * a + 3]
            g_ref, d_ref, nm_ref, nv_ref = outs[4 * a:4 * a + 4]
            dl, m, v = _adam_math(w_ref[idx], g, m_ref[idx], v_ref[idx])
            g_ref[idx] = g
            d_ref[idx] = dl
            nm_ref[idx] = m
            nv_ref[idx] = v

        for a, name in enumerate(names):
            if name in VEC_ROW:
                r = VEC_ROW[name]
                upd(a, vs[r:r + 1, :], (slice(None), slice(None)))
            elif name == "b_gate":
                for half in range(2):
                    r = ROW_B_GATE + half
                    upd(a, vs[r:r + 1, :], (slice(None), slice(half * d, (half + 1) * d)))
            elif name == "pool_scale":
                width = params[name][0].shape[1]
                upd(a, vs[ROW_POOL_SCALE:ROW_POOL_SCALE + 1, 0:width], (slice(None), slice(None)))
            elif name == "conv_w":
                upd(a, vsc[ROW_CONV_W:ROW_CONV_W + 4, :], (0,))
            elif name == "pool_w":
                upd(a, gp_ref[...], (Ellipsis,))
            elif name == "lru_w_a":
                upd(a, gwa_ref[...], (Ellipsis,))
            elif name == "lru_w_x":
                upd(a, gwx_ref[...], (Ellipsis,))
            else:
                raise ValueError(name)

    def whole(shape):
        nd = len(shape)
        return pl.BlockSpec(tuple(shape), lambda i, me_ref: (0,) * nd)

    in_specs = [
        whole(vec_parts.shape),
        pl.BlockSpec((N_DEV, N_VEC_ROWS, cw_cols), lambda i, me_ref: (0, 0, me_ref[0])),
        whole(g_pool.shape), whole(g_wa.shape), whole(g_wx.shape),
    ]
    args = [vec_parts, vec_parts, g_pool, g_wa, g_wx]
    out_specs = [whole((1, 128))]
    out_shape = [jax.ShapeDtypeStruct((1, 128), F32)]
    for name in names:
        for arr in params[name]:
            in_specs.append(whole(arr.shape))
            args.append(arr)
        shp = params[name][0].shape
        out_specs += [whole(shp)] * 4
        out_shape += [jax.ShapeDtypeStruct(shp, F32)] * 4
    grid_spec = pltpu.PrefetchScalarGridSpec(
        num_scalar_prefetch=1, grid=(1,), in_specs=in_specs, out_specs=out_specs,
        scratch_shapes=[pltpu.VMEM((N_VEC_ROWS, d), F32), pltpu.VMEM((N_VEC_ROWS, cw_cols), F32)])
    outs = pl.pallas_call(
        body, name="adamw_small", grid_spec=grid_spec, out_shape=out_shape,
        compiler_params=pltpu.CompilerParams(
            dimension_semantics=("arbitrary",), vmem_limit_bytes=V7X_VMEM_LIMIT_BYTES),
    )(me, *_in_hbm(args))
    return outs[0], {name: tuple(outs[1 + 4 * a:5 + 4 * a]) for a, name in enumerate(names)}


def _rs_sum(full, recv, shard_ids, slot_ids, name):
    r, rest = recv.shape[1], tuple(recv.shape[2:])
    zeros = (0,) * len(rest)
    send_dtype = recv.dtype

    def body(sh_ref, sl_ref, full_ref, recv_ref, own_ref, send_ref):
        s = pl.program_id(0)
        v = full_ref[...] + recv_ref[...].astype(F32)

        @pl.when(s == 0)
        def _():
            own_ref[...] = v

        @pl.when(s > 0)
        def _():
            send_ref[...] = v.astype(send_dtype)

    grid_spec = pltpu.PrefetchScalarGridSpec(
        num_scalar_prefetch=2,
        grid=(4,),
        in_specs=[
            pl.BlockSpec((r,) + rest, lambda s, sh, sl: (sh[s],) + zeros),
            pl.BlockSpec((None, r) + rest, lambda s, sh, sl: (sl[s], 0) + zeros),
        ],
        out_specs=[
            pl.BlockSpec((None, r) + rest, lambda s, sh, sl: (0, 0) + zeros),
            pl.BlockSpec((None, r) + rest, lambda s, sh, sl: (jnp.maximum(s - 1, 0), 0) + zeros),
        ],
    )
    return pl.pallas_call(
        body,
        name=name,
        grid_spec=grid_spec,
        out_shape=[jax.ShapeDtypeStruct((1, r) + rest, F32), jax.ShapeDtypeStruct((3, r) + rest, send_dtype)],
        compiler_params=pltpu.CompilerParams(
            dimension_semantics=("arbitrary",), vmem_limit_bytes=V7X_VMEM_LIMIT_BYTES),
    )(shard_ids, slot_ids, *_in_hbm([full, recv]))


def _finals(pairs, name, carry=None):
    nb = 4
    n = len(pairs)

    def body(*refs):
        for a in range(n):
            own_ref, recv_ref = refs[2 * a], refs[2 * a + 1]
            acc = own_ref[...]
            for k in range(3):
                acc = acc + recv_ref[k].astype(F32)
            refs[2 * n + a][...] = acc

    in_specs, out_specs, out_shape, args = [], [], [], []
    for own, recv in pairs:
        _, rows, cols = own.shape
        in_specs += [pl.BlockSpec((None, rows // nb, cols), lambda i: (0, i, 0)),
                     pl.BlockSpec((3, rows // nb, cols), lambda i: (0, i, 0))]
        args += [own, recv]
        out_specs.append(pl.BlockSpec((rows // nb, cols), lambda i: (i, 0)))
        out_shape.append(jax.ShapeDtypeStruct((rows, cols), F32))
    return _call(body, name=name, grid=(nb,), in_specs=in_specs, out_specs=out_specs,
                 out_shape=out_shape, args=args, carry=carry)


def _rs_sums(fulls_f32, recv1, tag):
    x, y, c = _place()
    qs = jnp.stack([2 * x + y, 2 * (1 - x) + y, 2 * x + (1 - y), 2 * (1 - x) + (1 - y)]).astype(jnp.int32)
    shard_ids = 2 * qs + c
    return [_rs_sum(f32, r1, shard_ids, qs, f"rs_sum_{tag}{a}")
            for a, (f32, r1) in enumerate(zip(fulls_f32, recv1))]


def _rs_level1(fulls_f32, fulls_send, tag):
    recv1 = _run_plan(_rs_sibling_plan(fulls_send), "rs_sibling_" + tag)
    return _rs_sums(fulls_f32, recv1, tag)


def _rows(g):
    return g.reshape(g.shape[0] * g.shape[1], g.shape[2])


def kernel(x, norm_mix_pre, norm_mix_post, norm_mlp_pre, norm_mlp_post, w_in, b_gate, conv_w, conv_b, lru_w_a, lru_b_a, lru_w_x, lru_b_x, lru_lambda, pool_w, pool_scale, w_lru_up, w_pool_up, w_o, w_ff1, w_ff2, loss_target, m_norm_mix_pre, m_norm_mix_post, m_norm_mlp_pre, m_norm_mlp_post, m_w_in, m_b_gate, m_conv_w, m_conv_b, m_lru_w_a, m_lru_b_a, m_lru_w_x, m_lru_b_x, m_lru_lambda, m_pool_w, m_pool_scale, m_w_lru_up, m_w_pool_up, m_w_o, m_w_ff1, m_w_ff2, v_norm_mix_pre, v_norm_mix_post, v_norm_mlp_pre, v_norm_mlp_post, v_w_in, v_b_gate, v_conv_w, v_conv_b, v_lru_w_a, v_lru_b_a, v_lru_w_x, v_lru_b_x, v_lru_lambda, v_pool_w, v_pool_scale, v_w_lru_up, v_w_pool_up, v_w_o, v_w_ff1, v_w_ff2):
    t, d = x.shape[1], x.shape[2]
    d_rnn = conv_b.shape[1]
    d_pool = pool_scale.shape[1]
    per = LRU_CB // LRU_HEAD_DIM
    xi, yi, ci = _place()
    me = 4 * xi + 2 * yi + ci

    x2d = x[0]
    tgt = loss_target[0]

    s_in = w_in[0].T.astype(BF16)
    s_lu = w_lru_up[0].astype(BF16)
    s_pu = w_pool_up[0].T.astype(BF16)
    s_o = w_o[0].astype(BF16)
    s_f1 = w_ff1[0].T.astype(BF16)
    s_f2 = w_ff2[0].astype(BF16)
    s_cw = jnp.pad(conv_w[0], ((0, 4), (0, 0)))

    g_in, g_cw = _run_plan(_ag_plan([s_in, s_cw]), "ag_w_in")
    w_int = _rows(g_in)
    conv_w_full = jnp.transpose(g_cw[:, :4, :], (1, 0, 2)).reshape(4, d_rnn)

    wa_bd, wx_bd = lru_w_a[0], lru_w_x[0]
    pw = pool_w[0]
    pw_bf = pw.astype(BF16)

    pool_block = (2 * d_rnn) // d_pool
    ga_block = (2 * d_rnn + d_pool) // 512
    gb_block = ga_block + d // 512
    g_block = d_rnn // 512

    r_f1, r_f2 = s_f1.shape[0], s_f2.shape[0]
    f1_cut = r_f1 // 4
    f2_cut = (3 * r_f2) // 8
    plan = _join([_ag_plan([s_lu, s_pu, s_o]), _ag_plan([s_f1], pieces=[(0, f1_cut)])])
    (proj, h1), got = _norm_proj(x2d, norm_mix_pre, w_int, carry=plan)
    (g_lu, g_pu, g_o), (g_f1,) = plan.split(got)
    w_lu, w_put, w_og = _rows(g_lu), _rows(g_pu), _rows(g_o)
    (y_lru, h, xc), (g_f1,) = _lru_fwd(
        proj, conv_w_full, conv_b, wa_bd, lru_b_a, wx_bd, lru_b_x, lru_lambda,
        carry=_ag_plan([s_f1], pieces=[(f1_cut, r_f1 - f1_cut)], bufs=[g_f1]))
    w_f1t = _rows(g_f1)
    y_pool, p = _pool_fwd(proj, pw_bf, pool_scale, pool_block)
    (br_a, br_b, mix), (g_f2,) = _branch_mix(
        y_lru, y_pool, w_lu, w_put, proj, b_gate, ga_block, gb_block,
        carry=_ag_plan([s_f2], pieces=[(0, f2_cut)]))
    (m, x2, h3), _ = _wo_norm(mix, w_og, x2d, norm_mix_post, norm_mlp_pre)
    (rf,), (g_f2,) = _ff1(
        h3, w_f1t, carry=_ag_plan([s_f2], pieces=[(f2_cut, r_f2 - f2_cut)], bufs=[g_f2]))
    w_f2 = _rows(g_f2)
    dy, df, dg4, loss_part = _ff2_loss(rf, w_f2, x2, norm_mlp_post, tgt)

    d_f1 = _ff2_bwd(df, w_f2, rf)
    (gw_ff2_32, gw_ff2_16), _ = _wgrad(rf, df, "wgrad_ff2", square_a=True)
    (gw_ff1_32, gw_ff1_16), r1_ff2 = _wgrad(d_f1, h3, "wgrad_ff1", carry=_rs_sibling_plan([gw_ff2_16]))
    ((own_ff2, send_ff2),) = _rs_sums([gw_ff2_32], r1_ff2, "ff2")
    plan = _join([_rs_chips_plan([send_ff2]), _rs_sibling_plan([gw_ff1_16])])
    (dx2, dm, dg3, dg2), got = _ff1_bwd_norms(d_f1, w_f1t, dy, x2, norm_mlp_pre, m, norm_mix_post, carry=plan)
    (r2_ff2,), r1_ff1 = plan.split(got)
    ((own_ff1, send_ff1),) = _rs_sums([gw_ff1_32], r1_ff1, "ff1")
    cut = send_ff1.shape[1] // 4
    (gw_o_32, gw_o_16), _ = _wgrad(mix, dm, "wgrad_o")
    (d_br_a, d_br_b, p_ga, p_gb, dbg_a, dbg_b), (r2_ff1,) = _wo_bwd_mix(
        dm, w_og, br_a, br_b, proj, b_gate, ga_block, gb_block,
        carry=_rs_chips_plan([send_ff1], pieces=[(0, cut)]))
    (gw_lu_32, gw_lu_16), _ = _wgrad(y_lru, d_br_a, "wgrad_lru_up")
    (gw_pu_32, gw_pu_16), _ = _wgrad(d_br_b, y_pool, "wgrad_pool_up")
    (dh, p_g), r1_mid = _lru_up_bwd(
        d_br_a, w_lu, proj, h, g_block,
        carry=_rs_sibling_plan([gw_o_16, gw_lu_16, gw_pu_16.reshape(-1, d)]))
    mid = _rs_sums([gw_o_32, gw_lu_32, gw_pu_32.reshape(-1, d)], r1_mid, "mid")
    d_y_pool = _pool_up_bwd(d_br_b, w_put)
    (p_x, dwa, db_a, dwx, db_x, dlam, dconv_w, dconv_b), (r2_ff1,) = _lru_bwd(
        dh, xc, h, proj, conv_w_full, wa_bd, lru_b_a, wx_bd, lru_b_x, lru_lambda,
        carry=_rs_chips_plan([send_ff1], pieces=[(cut, send_ff1.shape[1] - cut)], bufs=[r2_ff1]))
    p_p, dpool_w, dpool_scale = _pool_bwd(d_y_pool, p, pw, pool_scale)
    parts = [p_x, p_g, p_p, p_ga, p_gb]
    gw_in, r2_mid = _wgrad_parts(parts, h1, "wgrad_in", carry=_rs_chips_plan([s for _, s in mid]))
    tail = _rs_level1([gw_in[0], dpool_w.reshape(N_DEV, -1, POOL_GROUP_DIM), dwa, dwx],
                      [gw_in[1], dpool_w.reshape(N_DEV, -1, POOL_GROUP_DIM), dwa, dwx], "in")
    (grad_x, dg1), r2_tail = _win_bwd_norm(parts, w_int, dx2, x2d, norm_mix_pre,
                                           carry=_rs_chips_plan([s for _, s in tail]))

    def flat2(a):
        return a.reshape(a.shape[0], -1, a.shape[-1])

    fin_small, _ = _finals([
        (flat2(tail[1][0]), flat2(r2_tail[1])), (flat2(tail[2][0]), flat2(r2_tail[2])),
        (flat2(tail[3][0]), flat2(r2_tail[3])),
    ], "rs_finals_small")

    def pad_row(a):
        return jnp.pad(a, ((0, 0), (0, d - a.shape[1])))

    vecs = jnp.concatenate([dg1, dg2, dg3, dg4, dbg_a, dbg_b, dconv_b, db_a, db_x, dlam,
                            pad_row(dpool_scale), dconv_w, pad_row(loss_part)], axis=0)
    assert vecs.shape[0] == N_VEC_ROWS
    fin, (vec_parts, g_pool, g_wa, g_wx) = _finals([
        (tail[0][0], r2_tail[0]), (mid[1][0], r2_mid[1]), (mid[2][0], r2_mid[2]), (mid[0][0], r2_mid[0]),
        (own_ff1, r2_ff1), (own_ff2, r2_ff2),
    ], "rs_finals", carry=_ag_plan([vecs] + fin_small))
    g_w_in = fin[0].T
    g_w_lru_up = fin[1]
    g_w_pool_up = fin[2].reshape(d // N_DEV, d_pool).T
    g_w_o = fin[3]
    g_w_ff1 = fin[4].T
    g_w_ff2 = fin[5]

    big_names = ["w_in", "w_lru_up", "w_pool_up", "w_o", "w_ff1", "w_ff2"]
    big_w = [w_in, w_lru_up, w_pool_up, w_o, w_ff1, w_ff2]
    big_g = [g_w_in, g_w_lru_up, g_w_pool_up, g_w_o, g_w_ff1, g_w_ff2]
    big_m = [m_w_in, m_w_lru_up, m_w_pool_up, m_w_o, m_w_ff1, m_w_ff2]
    big_v = [v_w_in, v_w_lru_up, v_w_pool_up, v_w_o, v_w_ff1, v_w_ff2]
    big_out, _ = _adamw_big([w[0] for w in big_w], big_g, [mm[0] for mm in big_m], [vv[0] for vv in big_v])

    small = dict(
        norm_mix_pre=(norm_mix_pre, m_norm_mix_pre, v_norm_mix_pre),
        norm_mix_post=(norm_mix_post, m_norm_mix_post, v_norm_mix_post),
        norm_mlp_pre=(norm_mlp_pre, m_norm_mlp_pre, v_norm_mlp_pre),
        norm_mlp_post=(norm_mlp_post, m_norm_mlp_post, v_norm_mlp_post),
        b_gate=(b_gate, m_b_gate, v_b_gate), conv_w=(conv_w, m_conv_w, v_conv_w),
        conv_b=(conv_b, m_conv_b, v_conv_b), lru_w_a=(lru_w_a, m_lru_w_a, v_lru_w_a),
        lru_b_a=(lru_b_a, m_lru_b_a, v_lru_b_a), lru_w_x=(lru_w_x, m_lru_w_x, v_lru_w_x),
        lru_b_x=(lru_b_x, m_lru_b_x, v_lru_b_x), lru_lambda=(lru_lambda, m_lru_lambda, v_lru_lambda),
        pool_w=(pool_w, m_pool_w, v_pool_w), pool_scale=(pool_scale, m_pool_scale, v_pool_scale))
    loss_row, small_out = _adamw_small(
        vec_parts, g_pool.reshape(pool_w.shape), g_wa.reshape(lru_w_a.shape), g_wx.reshape(lru_w_x.shape),
        jnp.reshape(me, (1,)).astype(jnp.int32), small)
    grads = {n: o[0] for n, o in small_out.items()}
    delta = {n: o[1] for n, o in small_out.items()}
    new_m = {n: o[2] for n, o in small_out.items()}
    new_v = {n: o[3] for n, o in small_out.items()}

    for name, g, (dl, nm, nv) in zip(big_names, big_g, big_out):
        grads[name], delta[name], new_m[name], new_v[name] = g[None], dl[None], nm[None], nv[None]

    loss = loss_row[0, 0]
    order = ["norm_mix_pre", "norm_mix_post", "norm_mlp_pre", "norm_mlp_post", "w_in", "b_gate", "conv_w",
             "conv_b", "lru_w_a", "lru_b_a", "lru_w_x", "lru_b_x", "lru_lambda", "pool_w", "pool_scale",
             "w_lru_up", "w_pool_up", "w_o", "w_ff1", "w_ff2"]
    return (loss, grad_x[None], *[grads[n] for n in order], *[delta[n] for n in order],
            *[new_m[n] for n in order], *[new_v[n] for n in order])
```

```python
import functools
import math
import operator
import types

import jax
import jax.numpy as jnp
from jax import lax
from jax.experimental import pallas as pl
from jax.experimental.pallas import tpu as pltpu

F32 = jnp.float32
BF16 = jnp.bfloat16
NORM_EPS = 1e-6
LRU_C = 8.0
N_LRU_HEADS = 16
LRU_HEAD_DIM = 64
POOL_WINDOWS = (2, 4, 8, 16)
POOL_GROUP_DIM = 128
ADAM_LR = 0.001
ADAM_B1 = 0.9
ADAM_B2 = 0.999
ADAM_EPS = 1e-08
ADAM_WD = 0.01
ADAM_STEP = 10
N_DEV = 8
V7X_VMEM_LIMIT_BYTES = 56 * 1024 * 1024
LRU_CB = 256
MESH = pl.DeviceIdType.MESH
ANY = pl.BlockSpec(memory_space=pl.ANY)


def _tile(n, pref):
    t = min(n, pref)
    assert n % t == 0, (n, pref)
    return t


def _dot_nn(a, b):
    return lax.dot_general(a, b, (((1,), (0,)), ((), ())), preferred_element_type=F32)


def _dot_nt(a, b):
    return lax.dot_general(a, b, (((1,), (1,)), ((), ())), preferred_element_type=F32)


def _dot_tn(a, b):
    return lax.dot_general(a, b, (((0,), (0,)), ((), ())), preferred_element_type=F32)


def _row_chunks(n_rows, fn, chunk=256):
    chunk = min(chunk, n_rows)
    assert n_rows % chunk == 0

    def step(r, carry):
        fn(pl.ds(pl.multiple_of(r * chunk, chunk), chunk))
        return carry

    lax.fori_loop(0, n_rows // chunk, step, 0)


def _sig(x):
    return 1.0 / (1.0 + jnp.exp(-x))


def _rms_hat(x):
    r = lax.rsqrt(jnp.mean(x * x, axis=-1, keepdims=True) + NORM_EPS)
    return x * r, r


def _rms_bwd(dn, xhat, r, g):
    q = dn * g
    dx = r * (q - xhat * jnp.mean(q * xhat, axis=-1, keepdims=True))
    dg = jnp.sum(dn * xhat, axis=0, keepdims=True)
    return dx, dg


_GELU_K = math.sqrt(2.0 / math.pi)
_GELU_C = 0.044715


def _gelu_and_grad(g):
    t = jnp.tanh(_GELU_K * (g + _GELU_C * g * g * g))
    val = 0.5 * g * (1.0 + t)
    grad = 0.5 * (1.0 + t) + 0.5 * g * (1.0 - t * t) * (_GELU_K * (1.0 + 3.0 * _GELU_C * g * g))
    return val, grad


def _softplus_neg(lam):
    z = -lam
    e = jnp.exp(-jnp.abs(z))
    u = 1.0 + e
    d = u - 1.0
    l1p = jnp.where(d == 0.0, e, jnp.log(u) * (e / jnp.where(d == 0.0, 1.0, d)))
    return jnp.maximum(z, 0.0) + l1p


def _lru_gates(xc, wa, ba, wx, bx, lam):
    xcb = xc.astype(BF16)
    r = _sig(_dot_nn(xcb, wa) + ba)
    i = _sig(_dot_nn(xcb, wx) + bx)
    sp = _softplus_neg(lam)
    log_a = (-LRU_C) * r * sp
    a = jnp.exp(log_a)
    mult = jnp.sqrt(-jnp.tanh(log_a) * (1.0 + a * a))
    return xcb, r, i, sp, log_a, a, mult


def _place():
    return lax.axis_index("x"), lax.axis_index("y"), lax.axis_index("c")


def _ag_plan(shards, pieces=None, bufs=None):
    na = len(shards)
    n_kinds = 7

    def parts(ins, outs, sems):
        send_sems, recv_sems, local_sems = sems
        x, y, c = _place()
        me, sibling = (x, y, c), (x, y, 1 - c)
        x_nb, y_nb, diag = (1 - x, y), (x, 1 - y), (1 - x, 1 - y)
        relay_src = (c * (1 - x) + (1 - c) * x, c * y + (1 - c) * (1 - y))
        relay_dst = (c * x + (1 - c) * (1 - x), c * (1 - y) + (1 - c) * y)

        def own(a):
            return ins[a] if pieces is None else ins[a].at[pl.ds(*pieces[a])]

        def slot(a, px, py, pc):
            idx = 4 * px + 2 * py + pc
            return outs[a].at[idx] if pieces is None else outs[a].at[idx, pl.ds(*pieces[a])]

        def copy(a, k, block, to, src=None):
            return pltpu.make_async_remote_copy(
                src_ref=slot(a, *block) if src is None else src,
                dst_ref=slot(a, *block),
                send_sem=send_sems.at[a * n_kinds + k],
                recv_sem=recv_sems.at[a * n_kinds + k],
                device_id=to,
                device_id_type=MESH,
            )

        mine = [pltpu.make_async_copy(own(a), slot(a, *me), local_sems.at[a]) for a in range(na)]
        first, second, third = [], [], []
        for a in range(na):
            first += [copy(a, 0, me, sibling, src=own(a)), copy(a, 1, me, (*x_nb, c), src=own(a)),
                      copy(a, 2, me, (*y_nb, c), src=own(a))]
            second += [copy(a, 3, (*relay_src, c), (*relay_dst, c)), copy(a, 4, (*x_nb, c), sibling),
                       copy(a, 5, (*y_nb, c), sibling)]
            third.append(copy(a, 6, (*diag, c), sibling))
        return sibling, c, x_nb, y_nb, diag, copy, mine, first, second, third

    def start(ins, outs, sems):
        _, _, _, _, _, _, mine, first, _, _ = parts(ins, outs, sems)
        for cp in mine + first:
            cp.start()

    def middle(ins, outs, sems):
        _, c, x_nb, y_nb, _, copy, _, _, second, _ = parts(ins, outs, sems)
        for a in range(na):
            copy(a, 1, (*x_nb, c), (*x_nb, c)).wait_recv()
            copy(a, 2, (*y_nb, c), (*y_nb, c)).wait_recv()
        for cp in second:
            cp.start()

    def finish(ins, outs, sems):
        sibling, c, x_nb, y_nb, diag, copy, mine, first, second, third = parts(ins, outs, sems)
        for a in range(na):
            copy(a, 3, (*diag, c), (*diag, c)).wait_recv()
            third[a].start()
        for a in range(na):
            copy(a, 0, sibling, sibling).wait_recv()
            copy(a, 4, (*x_nb, 1 - c), sibling).wait_recv()
            copy(a, 5, (*y_nb, 1 - c), sibling).wait_recv()
            copy(a, 6, (*diag, 1 - c), sibling).wait_recv()
        for cp in first + second + third:
            cp.wait_send()
        for cp in mine:
            cp.wait()

    return types.SimpleNamespace(
        ins=list(shards) + list(bufs or []),
        out_shapes=[jax.ShapeDtypeStruct((N_DEV,) + s.shape, s.dtype) for s in shards],
        sems=[pltpu.SemaphoreType.DMA((n_kinds * na,)), pltpu.SemaphoreType.DMA((n_kinds * na,)),
              pltpu.SemaphoreType.DMA((na,))],
        aliases=[(na + a, a) for a in range(na)] if bufs else [],
        peers=frozenset({"sibling", "neighbours"}), start=start, middle=middle, finish=finish)


def _rs_sibling_plan(fulls):
    na = len(fulls)
    rs = [f.shape[0] // N_DEV for f in fulls]

    def copies(ins, outs, sems):
        send_sems, recv_sems = sems
        x, y, c = _place()
        out = []
        for a in range(na):
            for q in range(4):
                shard = 2 * q + (1 - c)
                out.append(pltpu.make_async_remote_copy(
                    src_ref=ins[a].at[pl.ds(shard * rs[a], rs[a])],
                    dst_ref=outs[a].at[q],
                    send_sem=send_sems.at[a * 4 + q],
                    recv_sem=recv_sems.at[a * 4 + q],
                    device_id=(x, y, 1 - c),
                    device_id_type=MESH,
                ))
        return out

    def start(ins, outs, sems):
        for cp in copies(ins, outs, sems):
            cp.start()

    def finish(ins, outs, sems):
        for cp in copies(ins, outs, sems):
            cp.wait()

    return types.SimpleNamespace(
        ins=list(fulls),
        out_shapes=[jax.ShapeDtypeStruct((4, r) + f.shape[1:], f.dtype) for r, f in zip(rs, fulls)],
        sems=[pltpu.SemaphoreType.DMA((4 * na,)), pltpu.SemaphoreType.DMA((4 * na,))],
        peers=frozenset({"sibling"}), start=start, finish=finish)


def _rs_chips_plan(sends, pieces=None, bufs=None):
    na = len(sends)

    def copies(ins, outs, sems):
        send_sems, recv_sems = sems
        x, y, c = _place()
        chips = [(1 - x, y), (x, 1 - y), (1 - x, 1 - y)]
        out = []
        for a in range(na):
            for k, chip in enumerate(chips):
                rows = (k,) if pieces is None else (k, pl.ds(*pieces[a]))
                out.append(pltpu.make_async_remote_copy(
                    src_ref=ins[a].at[rows],
                    dst_ref=outs[a].at[rows],
                    send_sem=send_sems.at[a * 3 + k],
                    recv_sem=recv_sems.at[a * 3 + k],
                    device_id=(*chip, c),
                    device_id_type=MESH,
                ))
        return out

    def start(ins, outs, sems):
        for cp in copies(ins, outs, sems):
            cp.start()

    def finish(ins, outs, sems):
        for cp in copies(ins, outs, sems):
            cp.wait()

    return types.SimpleNamespace(
        ins=list(sends) + list(bufs or []),
        out_shapes=[jax.ShapeDtypeStruct(s.shape, s.dtype) for s in sends],
        sems=[pltpu.SemaphoreType.DMA((3 * na,)), pltpu.SemaphoreType.DMA((3 * na,))],
        aliases=[(na + a, a) for a in range(na)] if bufs else [],
        peers=frozenset({"chips"}), start=start, finish=finish)


def _join(plans):
    ins, outs, sems, aliases, offs = [], [], [], [], []
    for p in plans:
        offs.append((len(ins), len(outs), len(sems)))
        aliases += [(len(ins) + ci, len(outs) + co) for ci, co in getattr(p, "aliases", [])]
        ins += p.ins
        outs += p.out_shapes
        sems += p.sems

    def cut(p, off, i, o, s):
        return (i[off[0]:off[0] + len(p.ins)], o[off[1]:off[1] + len(p.out_shapes)],
                s[off[2]:off[2] + len(p.sems)])

    def start(i, o, s):
        for p, off in zip(plans, offs):
            p.start(*cut(p, off, i, o, s))

    def middle(i, o, s):
        for p, off in zip(plans, offs):
            if getattr(p, "middle", None) is not None:
                p.middle(*cut(p, off, i, o, s))

    def finish(i, o, s):
        for p, off in zip(plans, offs):
            p.finish(*cut(p, off, i, o, s))

    def split(results):
        return [list(results[off[1]:off[1] + len(p.out_shapes)]) for p, off in zip(plans, offs)]

    return types.SimpleNamespace(ins=ins, out_shapes=outs, sems=sems, aliases=aliases,
                                 peers=frozenset().union(*[p.peers for p in plans]),
                                 start=start, middle=middle, finish=finish, split=split)


COLLECTIVE_ID = {frozenset({"sibling"}): 0, frozenset({"chips"}): 1, frozenset({"sibling", "chips"}): 2,
                 frozenset({"sibling", "neighbours"}): 3}


def _handshake(peers):
    x, y, c = _place()
    devs = []
    if "sibling" in peers:
        devs.append((x, y, 1 - c))
    if "neighbours" in peers:
        devs += [(1 - x, y, c), (x, 1 - y, c)]
    if "chips" in peers:
        assert "neighbours" not in peers
        devs += [(1 - x, y, c), (x, 1 - y, c), (1 - x, 1 - y, c)]
    barrier = pltpu.get_barrier_semaphore()
    for dev in devs:
        pl.semaphore_signal(barrier, inc=1, device_id=dev, device_id_type=MESH)
    pl.semaphore_wait(barrier, len(devs))


def _in_hbm(args):
    return [pltpu.with_memory_space_constraint(a, pltpu.HBM) for a in args]


def _run_plan(plan, name):
    n_in, n_out = len(plan.ins), len(plan.out_shapes)

    def body(*refs):
        ins, outs, sems = refs[:n_in], refs[n_in:n_in + n_out], refs[n_in + n_out:]
        _handshake(plan.peers)
        plan.start(ins, outs, sems)
        if getattr(plan, "middle", None) is not None:
            plan.middle(ins, outs, sems)
        plan.finish(ins, outs, sems)

    return pl.pallas_call(
        body,
        name=name,
        in_specs=[ANY] * n_in,
        out_specs=[ANY] * n_out,
        out_shape=plan.out_shapes,
        scratch_shapes=plan.sems,
        input_output_aliases=dict(getattr(plan, "aliases", [])),
        compiler_params=pltpu.CompilerParams(collective_id=COLLECTIVE_ID[plan.peers]),
    )(*_in_hbm(plan.ins))


def _call(body, *, name, grid, in_specs, out_specs, out_shape, args, scratch_shapes=(), aliases=None,
          carry=None):
    n_in, n_out, n_scr = len(in_specs), len(out_shape), len(scratch_shapes)
    params = pltpu.CompilerParams(
        dimension_semantics=("arbitrary",) * len(grid), vmem_limit_bytes=V7X_VMEM_LIMIT_BYTES)
    if carry is None:
        outs = pl.pallas_call(
            body, name=name, grid=grid, in_specs=list(in_specs), out_specs=list(out_specs),
            out_shape=list(out_shape), scratch_shapes=list(scratch_shapes),
            input_output_aliases=aliases or {}, compiler_params=params)(*_in_hbm(args))
        return list(outs), []
    c_in, c_out = len(carry.ins), len(carry.out_shapes)

    def full(*refs):
        p = 0
        ins = refs[p:p + n_in]
        p += n_in
        cins = refs[p:p + c_in]
        p += c_in
        outs = refs[p:p + n_out]
        p += n_out
        couts = refs[p:p + c_out]
        p += c_out
        scr = refs[p:p + n_scr]
        csems = refs[p + n_scr:]
        ids = [pl.program_id(a) for a in range(len(grid))]
        first = functools.reduce(operator.and_, [i == 0 for i in ids])
        last = functools.reduce(operator.and_, [i == g - 1 for i, g in zip(ids, grid)])

        @pl.when(first)
        def _():
            _handshake(carry.peers)
            carry.start(cins, couts, csems)

        if getattr(carry, "middle", None) is not None:
            n_steps = math.prod(grid)
            flat = functools.reduce(lambda acc, ig: acc * ig[1] + ig[0], zip(ids, grid), 0)

            @pl.when(flat == (2 * n_steps) // 3)
            def _():
                carry.middle(cins, couts, csems)

        body(*ins, *outs, *scr)

        @pl.when(last)
        def _():
            carry.finish(cins, couts, csems)

    all_aliases = dict(aliases or {})
    all_aliases.update({n_in + ci: n_out + co for ci, co in getattr(carry, "aliases", [])})
    params = pltpu.CompilerParams(
        dimension_semantics=("arbitrary",) * len(grid), vmem_limit_bytes=V7X_VMEM_LIMIT_BYTES,
        collective_id=COLLECTIVE_ID[carry.peers])
    outs = pl.pallas_call(
        full, name=name, grid=grid,
        in_specs=list(in_specs) + [ANY] * c_in,
        out_specs=list(out_specs) + [ANY] * c_out,
        out_shape=list(out_shape) + list(carry.out_shapes),
        scratch_shapes=list(scratch_shapes) + list(carry.sems),
        input_output_aliases=all_aliases, compiler_params=params)(*_in_hbm(args), *_in_hbm(carry.ins))
    return list(outs[:n_out]), list(outs[n_out:])


def _norm_proj(x, g1, w_int, carry=None):
    t, d = x.shape
    n = w_int.shape[0]
    tt, tn = _tile(t, 2048), _tile(n, 512)

    def body(x_ref, g_ref, w_ref, proj_ref, h1_ref, h1_s):
        @pl.when(pl.program_id(1) == 0)
        def _():
            def norm_rows(rows):
                xhat, _ = _rms_hat(x_ref[rows, :])
                h = (xhat * g_ref[...]).astype(BF16)
                h1_s[rows, :] = h
                h1_ref[rows, :] = h

            _row_chunks(tt, norm_rows)

        proj_ref[...] = _dot_nt(h1_s[...], w_ref[...]).astype(BF16)

    return _call(
        body, name="norm_proj", grid=(t // tt, n // tn),
        in_specs=[
            pl.BlockSpec((tt, d), lambda i, j: (i, 0)),
            pl.BlockSpec((1, d), lambda i, j: (0, 0)),
            pl.BlockSpec((tn, d), lambda i, j: (j, 0)),
        ],
        out_specs=[
            pl.BlockSpec((tt, tn), lambda i, j: (i, j)),
            pl.BlockSpec((tt, d), lambda i, j: (i, 0)),
        ],
        out_shape=[jax.ShapeDtypeStruct((t, n), BF16), jax.ShapeDtypeStruct((t, d), BF16)],
        scratch_shapes=[pltpu.VMEM((tt, d), BF16)],
        args=(x, g1, w_int), carry=carry)


SUBLANES = 8
LANES = 128


def _scan_rows(av, bv, carry, edge_s, reverse):
    tc, cb = av.shape
    ng = tc // SUBLANES
    nl = cb // LANES
    row = lax.broadcasted_iota(jnp.int32, av.shape, 0)
    sub = row % SUBLANES

    def shifted(v, s, keep, n, fill):
        return jnp.where(keep, pltpu.roll(v, (n - s) if reverse else s, 0), fill)

    s = 1
    while s < SUBLANES:
        keep = (sub < SUBLANES - s) if reverse else (sub >= s)
        a_sh = shifted(av, s, keep, tc, 1.0)
        b_sh = shifted(bv, s, keep, tc, 0.0)
        bv = av * b_sh + bv
        av = av * a_sh
        s *= 2
    for j in range(nl):
        edge_s[j] = av[:, j * LANES:(j + 1) * LANES]
        edge_s[nl + j] = bv[:, j * LANES:(j + 1) * LANES]
    edge = pl.ds(0 if reverse else SUBLANES - 1, ng, stride=SUBLANES)
    ga = jnp.concatenate([edge_s[j, edge, :] for j in range(nl)], axis=1)
    gb = jnp.concatenate([edge_s[nl + j, edge, :] for j in range(nl)], axis=1)
    grow = lax.broadcasted_iota(jnp.int32, ga.shape, 0)
    s = 1
    while s < ng:
        keep = (grow < ng - s) if reverse else (grow >= s)
        a_sh = shifted(ga, s, keep, ng, 1.0)
        b_sh = shifted(gb, s, keep, ng, 0.0)
        gb = ga * b_sh + gb
        ga = ga * a_sh
        s *= 2
    ends = ga * carry + gb
    keep = (grow < ng - 1) if reverse else (grow >= 1)
    cin = jnp.where(keep, pltpu.roll(ends, (ng - 1) if reverse else 1, 0), carry)
    cfull = jnp.broadcast_to(cin[:, None, :], (ng, SUBLANES, cb)).reshape(tc, cb)
    return av * cfull + bv


def _fill_block_diag(w_ref, bd_ref):
    bd_ref[...] = jnp.zeros_like(bd_ref)
    hd = LRU_HEAD_DIM
    for k in range(w_ref.shape[0]):
        bd_ref[k * hd:(k + 1) * hd, k * hd:(k + 1) * hd] = w_ref[k].astype(BF16)


def _lru_fwd(proj, conv_w, conv_b, w_a, b_a, w_x, b_x, lam, carry=None):
    t = proj.shape[0]
    dr = conv_b.shape[1]
    cb = LRU_CB
    tc = _tile(t, 256)
    ncb, ntc = dr // cb, t // tc

    def body(xp_ref, g_ref, cw_ref, cb_ref, wa_ref, ba_ref, wx_ref, bx_ref, lam_ref,
             y_ref, h_ref, xc_ref, prevx_s, hlast_s, wa_s, wx_s, edge_s):
        c = pl.program_id(1)

        @pl.when(c == 0)
        def _():
            prevx_s[...] = jnp.zeros_like(prevx_s)
            hlast_s[...] = jnp.zeros_like(hlast_s)
            _fill_block_diag(wa_ref, wa_s)
            _fill_block_diag(wx_ref, wx_s)

        x = xp_ref[...].astype(F32)
        prev = prevx_s[...]
        row = lax.broadcasted_iota(jnp.int32, x.shape, 0)

        def sh(j):
            return jnp.where(row >= j, pltpu.roll(x, j, 0), pltpu.roll(prev, j, 0))

        xc = (cb_ref[...] + cw_ref[0:1, :] * sh(3) + cw_ref[1:2, :] * sh(2)
              + cw_ref[2:3, :] * sh(1) + cw_ref[3:4, :] * x)
        prevx_s[...] = x
        xc_ref[...] = xc
        _, _, i, _, _, a, mult = _lru_gates(xc, wa_s[...], ba_ref[...], wx_s[...], bx_ref[...],
                                            lam_ref[...])
        h = _scan_rows(a, mult * (i * xc), hlast_s[...], edge_s, reverse=False)
        h_ref[...] = h
        hlast_s[...] = h_ref[tc - 1:tc, :]
        gel, _ = _gelu_and_grad(g_ref[...].astype(F32))
        y_ref[...] = (h * gel).astype(BF16)

    vec = pl.BlockSpec((1, cb), lambda j, c: (0, j))
    blk = pl.BlockSpec((tc, cb), lambda j, c: (c, j))
    mat = pl.BlockSpec((cb // LRU_HEAD_DIM, LRU_HEAD_DIM, LRU_HEAD_DIM), lambda j, c: (j, 0, 0))
    return _call(
        body, name="lru_fwd", grid=(ncb, ntc),
        in_specs=[
            blk,
            pl.BlockSpec((tc, cb), lambda j, c: (c, ncb + j)),
            pl.BlockSpec((4, cb), lambda j, c: (0, j)),
            vec, mat, vec, mat, vec, vec,
        ],
        out_specs=[blk, blk, blk],
        out_shape=[
            jax.ShapeDtypeStruct((t, dr), BF16),
            jax.ShapeDtypeStruct((t, dr), F32),
            jax.ShapeDtypeStruct((t, dr), F32),
        ],
        scratch_shapes=[pltpu.VMEM((tc, cb), F32), pltpu.VMEM((1, cb), F32),
                        pltpu.VMEM((cb, cb), BF16), pltpu.VMEM((cb, cb), BF16),
                        pltpu.VMEM((2 * cb // LANES, tc, LANES), F32)],
        args=(proj, proj, conv_w, conv_b, w_a, b_a, w_x, b_x, lam), carry=carry)


def _pool_select(col, vals):
    out = vals[3]
    for g in (2, 1, 0):
        out = jnp.where(col < (g + 1) * POOL_GROUP_DIM, vals[g], out)
    return out


def _pool_fwd(proj, pool_w, pool_scale, col_block):
    t = proj.shape[0]
    dp = pool_scale.shape[1]
    tc = _tile(t, 256)
    ntc = t // tc

    def body(x_ref, w_ref, sc_ref, y_ref, p_ref, px, p2, p4, p8):
        c = pl.program_id(0)

        @pl.when(c == 0)
        def _():
            for s in (px, p2, p4, p8):
                s[...] = jnp.zeros_like(s)

        x = x_ref[...].astype(F32)
        row = lax.broadcasted_iota(jnp.int32, x.shape, 0)
        col = lax.broadcasted_iota(jnp.int32, x.shape, 1)

        def sh(v, pv, j):
            return jnp.where(row >= j, pltpu.roll(v, j, 0), pltpu.roll(pv[...], j, 0))

        s2 = x + sh(x, px, 1)
        s4 = s2 + sh(s2, p2, 2)
        s8 = s4 + sh(s4, p4, 4)
        s16 = s8 + sh(s8, p8, 8)
        px[...] = x
        p2[...] = s2
        p4[...] = s4
        p8[...] = s8
        wsum = _pool_select(col, (s2, s4, s8, s16))
        win = _pool_select(col, POOL_WINDOWS)
        cnt = jnp.minimum(c * tc + row + 1, win).astype(F32)
        p = wsum / cnt - x
        pb = p.astype(BF16)
        p_ref[...] = pb
        for g in range(len(POOL_WINDOWS)):
            sl = slice(g * POOL_GROUP_DIM, (g + 1) * POOL_GROUP_DIM)
            yg = _dot_nn(pb[:, sl], w_ref[g]) * sc_ref[:, sl]
            y_ref[:, sl] = yg.astype(BF16)

    return _call(
        body, name="pool_fwd", grid=(ntc,),
        in_specs=[
            pl.BlockSpec((tc, dp), lambda c: (c, col_block)),
            pl.BlockSpec(pool_w.shape, lambda c: (0, 0, 0)),
            pl.BlockSpec((1, dp), lambda c: (0, 0)),
        ],
        out_specs=[pl.BlockSpec((tc, dp), lambda c: (c, 0))] * 2,
        out_shape=[jax.ShapeDtypeStruct((t, dp), BF16)] * 2,
        scratch_shapes=[pltpu.VMEM((tc, dp), F32)] * 4,
        args=(proj, pool_w, pool_scale))[0]


def _branch_mix(y_lru, y_pool, w_lru_up, w_pool_upt, proj, b_gate, ga_block, gb_block, carry=None):
    t, d = y_lru.shape
    dp = y_pool.shape[1]
    tt, tn = _tile(t, 1024), 512
    nj = d // tn

    def body(yl_ref, yp_ref, wl_ref, wp_ref, ga_ref, gb_ref, ba_ref, bb_ref, bra_ref, brb_ref, mix_ref):
        br_a = _dot_nn(yl_ref[...], wl_ref[...])
        br_b = _dot_nt(yp_ref[...], wp_ref[...])
        bra_ref[...] = br_a.astype(BF16)
        brb_ref[...] = br_b.astype(BF16)
        ga = _sig(ga_ref[...].astype(F32) + ba_ref[...])
        gb = _sig(gb_ref[...].astype(F32) + bb_ref[...])
        mix_ref[...] = (ga * br_a + gb * br_b).astype(BF16)

    out = pl.BlockSpec((tt, tn), lambda j, i: (i, j))
    return _call(
        body, name="branch_mix", grid=(nj, t // tt),
        in_specs=[
            pl.BlockSpec((tt, d), lambda j, i: (i, 0)),
            pl.BlockSpec((tt, dp), lambda j, i: (i, 0)),
            pl.BlockSpec((d, tn), lambda j, i: (0, j)),
            pl.BlockSpec((tn, dp), lambda j, i: (j, 0)),
            pl.BlockSpec((tt, tn), lambda j, i: (i, ga_block + j)),
            pl.BlockSpec((tt, tn), lambda j, i: (i, gb_block + j)),
            pl.BlockSpec((1, tn), lambda j, i: (0, j)),
            pl.BlockSpec((1, tn), lambda j, i: (0, nj + j)),
        ],
        out_specs=[out, out, out],
        out_shape=[jax.ShapeDtypeStruct((t, d), BF16)] * 3,
        args=(y_lru, y_pool, w_lru_up, w_pool_upt, proj, proj, b_gate, b_gate), carry=carry)


def _wo_norm(mix, w_o, x, g2, g3, carry=None):
    t, d = x.shape
    tt = _tile(t, 512)

    def body(mix_ref, w_ref, x_ref, g2_ref, g3_ref, m_ref, x2_ref, h3_ref):
        m = _dot_nn(mix_ref[...], w_ref[...])
        m_ref[...] = m
        mhat, _ = _rms_hat(m)
        x2 = x_ref[...] + mhat * g2_ref[...]
        x2_ref[...] = x2
        xhat, _ = _rms_hat(x2)
        h3_ref[...] = (xhat * g3_ref[...]).astype(BF16)

    row = pl.BlockSpec((tt, d), lambda i: (i, 0))
    vec = pl.BlockSpec((1, d), lambda i: (0, 0))
    return _call(
        body, name="wo_norm", grid=(t // tt,),
        in_specs=[row, pl.BlockSpec((d, d), lambda i: (0, 0)), row, vec, vec],
        out_specs=[row, row, row],
        out_shape=[
            jax.ShapeDtypeStruct((t, d), F32),
            jax.ShapeDtypeStruct((t, d), F32),
            jax.ShapeDtypeStruct((t, d), BF16),
        ],
        args=(mix, w_o, x, g2, g3), carry=carry)


def _ff1(h3, w_ff1t, carry=None):
    t, d = h3.shape
    n = w_ff1t.shape[0]
    tt, tn = _tile(t, 2048), _tile(n, 512)

    def body(h_ref, w_ref, rf_ref):
        rf_ref[...] = jnp.maximum(_dot_nt(h_ref[...], w_ref[...]), 0.0).astype(BF16)

    out = pl.BlockSpec((tt, tn), lambda i, j: (i, j))
    return _call(
        body, name="ff1", grid=(t // tt, n // tn),
        in_specs=[pl.BlockSpec((tt, d), lambda i, j: (i, 0)), pl.BlockSpec((tn, d), lambda i, j: (j, 0))],
        out_specs=[out],
        out_shape=[jax.ShapeDtypeStruct((t, n), BF16)],
        args=(h3, w_ff1t), carry=carry)


def _ff2_loss(rf, w_ff2, x2, g4, target):
    t, k = rf.shape
    d = x2.shape[1]
    tt, tk = _tile(t, 1024), _tile(k, 512)
    nk = k // tk

    def body(a_ref, w_ref, x2_ref, g_ref, tg_ref, dy_ref, df_ref, dg_ref, loss_ref, acc):
        i, kk = pl.program_id(0), pl.program_id(1)

        @pl.when(kk == 0)
        def _():
            acc[...] = jnp.zeros_like(acc)

        @pl.when((i == 0) & (kk == 0))
        def _():
            dg_ref[...] = jnp.zeros_like(dg_ref)
            loss_ref[...] = jnp.zeros_like(loss_ref)

        rf_tile = a_ref[...]
        acc[...] += _dot_nn(rf_tile * rf_tile, w_ref[...])

        @pl.when(kk == nk - 1)
        def _():
            def tail(rows):
                fhat, r = _rms_hat(acc[rows, :])
                g = g_ref[...]
                e = x2_ref[rows, :] + fhat * g - tg_ref[rows, :]
                loss_ref[...] += 0.5 * jnp.sum(jnp.mean(e * e, axis=-1, keepdims=True))
                dy = e * (1.0 / d)
                dy_ref[rows, :] = dy.astype(BF16)
                df, dg = _rms_bwd(dy, fhat, r, g)
                df_ref[rows, :] = df.astype(BF16)
                dg_ref[...] += dg

            _row_chunks(tt, tail)

    row = pl.BlockSpec((tt, d), lambda i, kk: (i, 0))
    vec = pl.BlockSpec((1, d), lambda i, kk: (0, 0))
    return _call(
        body, name="ff2_loss", grid=(t // tt, nk),
        in_specs=[
            pl.BlockSpec((tt, tk), lambda i, kk: (i, kk)),
            pl.BlockSpec((tk, d), lambda i, kk: (kk, 0)),
            row, vec, row,
        ],
        out_specs=[row, row, vec, pl.BlockSpec((1, 128), lambda i, kk: (0, 0))],
        out_shape=[
            jax.ShapeDtypeStruct((t, d), BF16),
            jax.ShapeDtypeStruct((t, d), BF16),
            jax.ShapeDtypeStruct((1, d), F32),
            jax.ShapeDtypeStruct((1, 128), F32),
        ],
        scratch_shapes=[pltpu.VMEM((tt, d), F32)],
        args=(rf, w_ff2, x2, g4, target))[0]


def _ff2_bwd(df, w_ff2, rf, carry=None):
    t, d = df.shape
    n = w_ff2.shape[0]
    tt, tn = _tile(t, 2048), _tile(n, 512)

    def body(df_ref, w_ref, rf_ref, out_ref):
        d_act = _dot_nt(df_ref[...], w_ref[...])
        out_ref[...] = (d_act * (2.0 * rf_ref[...].astype(F32))).astype(BF16)

    blk = pl.BlockSpec((tt, tn), lambda i, j: (i, j))
    return _call(
        body, name="ff2_bwd", grid=(t // tt, n // tn),
        in_specs=[pl.BlockSpec((tt, d), lambda i, j: (i, 0)), pl.BlockSpec((tn, d), lambda i, j: (j, 0)), blk],
        out_specs=[blk],
        out_shape=[jax.ShapeDtypeStruct((t, n), BF16)],
        args=(df, w_ff2, rf), carry=carry)


def _wgrad(a, b, name, prev=None, row_off=0, rows=None, carry=None, square_a=False):
    t, m = a.shape
    n = b.shape[1]
    rows = m if rows is None else rows
    tm, tk = _tile(m, 512), _tile(t, 2048)
    nk = t // tk
    assert row_off % tm == 0
    off = row_off // tm

    def body(*refs):
        a_ref, b_ref = refs[0], refs[1]
        o32_ref, o16_ref, acc = refs[-3], refs[-2], refs[-1]
        kk = pl.program_id(1)

        @pl.when(kk == 0)
        def _():
            acc[...] = jnp.zeros_like(acc)

        a_tile = a_ref[...]
        acc[...] += _dot_tn(a_tile * a_tile if square_a else a_tile, b_ref[...])

        @pl.when(kk == nk - 1)
        def _():
            o32_ref[...] = acc[...]
            o16_ref[...] = acc[...].astype(BF16)

    in_specs = [pl.BlockSpec((tk, tm), lambda i, kk: (kk, i)), pl.BlockSpec((tk, n), lambda i, kk: (kk, 0))]
    args = [a, b]
    aliases = {}
    if prev is not None:
        in_specs += [ANY, ANY]
        args += list(prev)
        aliases = {2: 0, 3: 1}
    out = pl.BlockSpec((tm, n), lambda i, kk: (off + i, 0))
    return _call(
        body, name=name, grid=(m // tm, nk),
        in_specs=in_specs, out_specs=[out, out],
        out_shape=[jax.ShapeDtypeStruct((rows, n), F32), jax.ShapeDtypeStruct((rows, n), BF16)],
        scratch_shapes=[pltpu.VMEM((tm, n), F32)],
        aliases=aliases, args=args, carry=carry)


def _wgrad_parts(parts, b, name, carry=None):
    t, n = b.shape
    tm = 512
    bounds = []
    lo = 0
    for part in parts:
        assert part.shape[0] == t and part.shape[1] % tm == 0
        bounds.append((lo, lo + part.shape[1] // tm))
        lo += part.shape[1] // tm
    nm = lo
    np_ = len(parts)

    def body(*refs):
        p_refs, b_ref, o32_ref, o16_ref = refs[:np_], refs[np_], refs[np_ + 1], refs[np_ + 2]
        i = pl.program_id(0)
        for (lo_p, hi_p), p_ref in zip(bounds, p_refs):
            @pl.when((i >= lo_p) & (i < hi_p))
            def _(p_ref=p_ref):
                res = _dot_tn(p_ref[...], b_ref[...])
                o32_ref[...] = res
                o16_ref[...] = res.astype(BF16)

    def part_spec(lo_p, hi_p):
        return pl.BlockSpec((t, tm), lambda i: (0, jnp.clip(i - lo_p, 0, hi_p - lo_p - 1)))

    out = pl.BlockSpec((tm, n), lambda i: (i, 0))
    return _call(
        body, name=name, grid=(nm,),
        in_specs=[part_spec(lo_p, hi_p) for lo_p, hi_p in bounds] + [pl.BlockSpec((t, n), lambda i: (0, 0))],
        out_specs=[out, out],
        out_shape=[jax.ShapeDtypeStruct((nm * tm, n), F32), jax.ShapeDtypeStruct((nm * tm, n), BF16)],
        args=(*parts, b), carry=carry)


def _ff1_bwd_norms(d_f1, w_ff1t, dy, x2, g3, m, g2, carry=None):
    t, k = d_f1.shape
    d = x2.shape[1]
    tt, tk = _tile(t, 1024), _tile(k, 512)
    nk = k // tk

    def body(a_ref, w_ref, dy_ref, x2_ref, g3_ref, m_ref, g2_ref, dx2_ref, dm_ref, dg3_ref, dg2_ref, acc):
        i, kk = pl.program_id(0), pl.program_id(1)

        @pl.when(kk == 0)
        def _():
            acc[...] = jnp.zeros_like(acc)

        @pl.when((i == 0) & (kk == 0))
        def _():
            dg3_ref[...] = jnp.zeros_like(dg3_ref)
            dg2_ref[...] = jnp.zeros_like(dg2_ref)

        acc[...] += _dot_nn(a_ref[...], w_ref[...])

        @pl.when(kk == nk - 1)
        def _():
            def tail(rows):
                xhat, r3 = _rms_hat(x2_ref[rows, :])
                dx, dg3 = _rms_bwd(acc[rows, :], xhat, r3, g3_ref[...])
                dx2 = dy_ref[rows, :].astype(F32) + dx
                dx2_ref[rows, :] = dx2
                dg3_ref[...] += dg3
                mhat, r2 = _rms_hat(m_ref[rows, :])
                dm, dg2 = _rms_bwd(dx2, mhat, r2, g2_ref[...])
                dm_ref[rows, :] = dm.astype(BF16)
                dg2_ref[...] += dg2

            _row_chunks(tt, tail)

    row = pl.BlockSpec((tt, d), lambda i, kk: (i, 0))
    vec = pl.BlockSpec((1, d), lambda i, kk: (0, 0))
    return _call(
        body, name="ff1_bwd_norms", grid=(t // tt, nk),
        in_specs=[
            pl.BlockSpec((tt, tk), lambda i, kk: (i, kk)),
            pl.BlockSpec((tk, d), lambda i, kk: (kk, 0)),
            row, row, vec, row, vec,
        ],
        out_specs=[row, row, vec, vec],
        out_shape=[
            jax.ShapeDtypeStruct((t, d), F32),
            jax.ShapeDtypeStruct((t, d), BF16),
            jax.ShapeDtypeStruct((1, d), F32),
            jax.ShapeDtypeStruct((1, d), F32),
        ],
        scratch_shapes=[pltpu.VMEM((tt, d), F32)],
        args=(d_f1, w_ff1t, dy, x2, g3, m, g2), carry=carry)


def _wo_bwd_mix(dm, w_o, br_a, br_b, proj, b_gate, ga_block, gb_block, carry=None):
    t, d = dm.shape
    tt, tn = _tile(t, 1024), 512
    nj = d // tn

    def body(dm_ref, w_ref, bra_ref, brb_ref, ga_ref, gb_ref, ba_ref, bb_ref,
             dbra_ref, dbrb_ref, dga_ref, dgb_ref, dba_ref, dbb_ref):
        i = pl.program_id(1)

        @pl.when(i == 0)
        def _():
            dba_ref[...] = jnp.zeros_like(dba_ref)
            dbb_ref[...] = jnp.zeros_like(dbb_ref)

        d_mix = _dot_nt(dm_ref[...], w_ref[...])
        ga = _sig(ga_ref[...].astype(F32) + ba_ref[...])
        gb = _sig(gb_ref[...].astype(F32) + bb_ref[...])
        dbra_ref[...] = (d_mix * ga).astype(BF16)
        dbrb_ref[...] = (d_mix * gb).astype(BF16)
        dga = d_mix * bra_ref[...].astype(F32) * (ga * (1.0 - ga))
        dgb = d_mix * brb_ref[...].astype(F32) * (gb * (1.0 - gb))
        dga_ref[...] = dga.astype(BF16)
        dgb_ref[...] = dgb.astype(BF16)
        dba_ref[...] += jnp.sum(dga, axis=0, keepdims=True)
        dbb_ref[...] += jnp.sum(dgb, axis=0, keepdims=True)

    blk = pl.BlockSpec((tt, tn), lambda j, i: (i, j))
    vec = pl.BlockSpec((1, tn), lambda j, i: (0, j))
    return _call(
        body, name="wo_bwd_mix", grid=(nj, t // tt),
        in_specs=[
            pl.BlockSpec((tt, d), lambda j, i: (i, 0)),
            pl.BlockSpec((tn, d), lambda j, i: (j, 0)),
            blk, blk,
            pl.BlockSpec((tt, tn), lambda j, i: (i, ga_block + j)),
            pl.BlockSpec((tt, tn), lambda j, i: (i, gb_block + j)),
            vec,
            pl.BlockSpec((1, tn), lambda j, i: (0, nj + j)),
        ],
        out_specs=[blk, blk, blk, blk, vec, vec],
        out_shape=[jax.ShapeDtypeStruct((t, d), BF16)] * 4 + [jax.ShapeDtypeStruct((1, d), F32)] * 2,
        args=(dm, w_o, br_a, br_b, proj, proj, b_gate, b_gate), carry=carry)


def _lru_up_bwd(d_br_a, w_lru_up, proj, h, g_block, carry=None):
    t, d = d_br_a.shape
    tt, tn = _tile(t, 1024), 512

    def body(a_ref, w_ref, g_ref, h_ref, dh_ref, dg_ref):
        d_y = _dot_nt(a_ref[...], w_ref[...])
        gel, gel_grad = _gelu_and_grad(g_ref[...].astype(F32))
        dh_ref[...] = d_y * gel
        dg_ref[...] = (d_y * h_ref[...] * gel_grad).astype(BF16)

    blk = pl.BlockSpec((tt, tn), lambda i, j: (i, j))
    return _call(
        body, name="lru_up_bwd", grid=(t // tt, d // tn),
        in_specs=[
            pl.BlockSpec((tt, d), lambda i, j: (i, 0)),
            pl.BlockSpec((tn, d), lambda i, j: (j, 0)),
            pl.BlockSpec((tt, tn), lambda i, j: (i, g_block + j)),
            blk,
        ],
        out_specs=[blk, blk],
        out_shape=[jax.ShapeDtypeStruct((t, d), F32), jax.ShapeDtypeStruct((t, d), BF16)],
        args=(d_br_a, w_lru_up, proj, h), carry=carry)


def _pool_up_bwd(d_br_b, w_pool_upt):
    t, d = d_br_b.shape
    dp = w_pool_upt.shape[1]
    tt = _tile(t, 2048)

    def body(a_ref, w_ref, out_ref):
        out_ref[...] = _dot_nn(a_ref[...], w_ref[...])

    return _call(
        body, name="pool_up_bwd", grid=(t // tt,),
        in_specs=[pl.BlockSpec((tt, d), lambda i: (i, 0)), pl.BlockSpec((d, dp), lambda i: (0, 0))],
        out_specs=[pl.BlockSpec((tt, dp), lambda i: (i, 0))],
        out_shape=[jax.ShapeDtypeStruct((t, dp), F32)],
        args=(d_br_b, w_pool_upt))[0][0]


def _lru_bwd(dh, xc, h, proj, conv_w, w_a, b_a, w_x, b_x, lam, carry=None):
    t, dr = dh.shape
    cb = LRU_CB
    hd = LRU_HEAD_DIM
    per = cb // hd
    tc = _tile(t, 256)
    ncb, ntc = dr // cb, t // tc

    def body(dh_ref, xc_ref, h_ref, hp_ref, xp_ref, cw_ref, wa_ref, ba_ref, wx_ref, bx_ref, lam_ref,
             dxp_ref, dwa_ref, dba_ref, dwx_ref, dbx_ref, dlam_ref, dcw_ref, dcb_ref,
             nextd_s, anext_s, gnext_s, tmp_s, wa_s, wx_s, edge_s):
        c = pl.program_id(1)
        rc = ntc - 1 - c

        @pl.when(c == 0)
        def _():
            nextd_s[...] = jnp.zeros_like(nextd_s)
            anext_s[...] = jnp.zeros_like(anext_s)
            gnext_s[...] = jnp.zeros_like(gnext_s)
            for ref in (dwa_ref, dba_ref, dwx_ref, dbx_ref, dlam_ref, dcw_ref, dcb_ref):
                ref[...] = jnp.zeros_like(ref)
            _fill_block_diag(wa_ref, wa_s)
            _fill_block_diag(wx_ref, wx_s)

        xc = xc_ref[...]
        wa, wx, lam = wa_s[...], wx_s[...], lam_ref[...]
        xcb, r, i, sp, log_a, a, mult = _lru_gates(xc, wa, ba_ref[...], wx, bx_ref[...], lam)
        row = lax.broadcasted_iota(jnp.int32, xc.shape, 0)
        h = h_ref[...]
        hp = jnp.where(rc == 0, 0.0, hp_ref[...])
        hprev = jnp.where(row >= 1, pltpu.roll(h, 1, 0), pltpu.roll(hp, 1, 0))

        def up(v, nv, j):
            return jnp.where(row < tc - j, pltpu.roll(v, tc - j, 0), nv)

        gt = _scan_rows(up(a, anext_s[...], 1), dh_ref[...], gnext_s[...], edge_s, reverse=True)
        tmp_s[...] = gt
        gnext_s[...] = tmp_s[0:1, :]
        tmp_s[...] = a
        anext_s[...] = tmp_s[0:1, :]

        da = gt * hprev
        ixc = i * xc
        d_mult = gt * ixc
        d_i = gt * mult * xc
        d_xc = gt * mult * i
        d_log_a = da * a - d_mult * (a * a) / mult
        d_pre_r = (d_log_a * ((-LRU_C) * sp)) * (r * (1.0 - r))
        d_pre_i = d_i * (i * (1.0 - i))
        d_sp = jnp.sum(d_log_a * ((-LRU_C) * r), axis=0, keepdims=True)
        dlam_ref[...] += d_sp * (-1.0 / (1.0 + jnp.exp(lam)))
        dpr = d_pre_r.astype(BF16)
        dpi = d_pre_i.astype(BF16)
        dba_ref[...] += jnp.sum(d_pre_r, axis=0, keepdims=True)
        dbx_ref[...] += jnp.sum(d_pre_i, axis=0, keepdims=True)
        pa = _dot_tn(xcb, dpr)
        px = _dot_tn(xcb, dpi)
        for k in range(per):
            dwa_ref[k] += pa[k * hd:(k + 1) * hd, k * hd:(k + 1) * hd]
            dwx_ref[k] += px[k * hd:(k + 1) * hd, k * hd:(k + 1) * hd]
        d_xc = d_xc + _dot_nt(dpr, wa) + _dot_nt(dpi, wx)

        nxt = nextd_s[...]
        xp = xp_ref[...].astype(F32)
        dxp = cw_ref[3:4, :] * d_xc
        dcw_ref[3:4, :] += jnp.sum(xp * d_xc, axis=0, keepdims=True)
        for j in (1, 2, 3):
            uj = up(d_xc, pltpu.roll(nxt, tc - j, 0), j)
            dxp = dxp + cw_ref[3 - j:4 - j, :] * uj
            dcw_ref[3 - j:4 - j, :] += jnp.sum(xp * uj, axis=0, keepdims=True)
        dcb_ref[...] += jnp.sum(d_xc, axis=0, keepdims=True)
        nextd_s[...] = d_xc
        dxp_ref[...] = dxp.astype(BF16)

    vec = pl.BlockSpec((1, cb), lambda j, c: (0, j))
    blk = pl.BlockSpec((tc, cb), lambda j, c: (ntc - 1 - c, j))
    mat = pl.BlockSpec((per, hd, hd), lambda j, c: (j, 0, 0))
    cwb = pl.BlockSpec((4, cb), lambda j, c: (0, j))
    return _call(
        body, name="lru_bwd", grid=(ncb, ntc),
        in_specs=[
            blk, blk, blk,
            pl.BlockSpec((tc, cb), lambda j, c: (jnp.maximum(ntc - 2 - c, 0), j)),
            blk, cwb, mat, vec, mat, vec, vec,
        ],
        out_specs=[blk, mat, vec, mat, vec, vec, cwb, vec],
        out_shape=[
            jax.ShapeDtypeStruct((t, dr), BF16),
            jax.ShapeDtypeStruct(w_a.shape, F32),
            jax.ShapeDtypeStruct((1, dr), F32),
            jax.ShapeDtypeStruct(w_x.shape, F32),
            jax.ShapeDtypeStruct((1, dr), F32),
            jax.ShapeDtypeStruct((1, dr), F32),
            jax.ShapeDtypeStruct((4, dr), F32),
            jax.ShapeDtypeStruct((1, dr), F32),
        ],
        scratch_shapes=[
            pltpu.VMEM((tc, cb), F32),
            pltpu.VMEM((1, cb), F32),
            pltpu.VMEM((1, cb), F32),
            pltpu.VMEM((tc, cb), F32),
            pltpu.VMEM((cb, cb), BF16),
            pltpu.VMEM((cb, cb), BF16),
            pltpu.VMEM((2 * cb // LANES, tc, LANES), F32),
        ],
        args=(dh, xc, h, h, proj, conv_w, w_a, b_a, w_x, b_x, lam), carry=carry)


def _pool_bwd(d_y_pool, p, pool_w, pool_scale):
    t, dp = d_y_pool.shape
    tc = _tile(t, 256)
    ntc = t // tc
    ng = len(POOL_WINDOWS)

    def body(dy_ref, p_ref, w_ref, sc_ref, dx_ref, dw_ref, dsc_ref, nz, n2, n4, n8, dp_s):
        c = pl.program_id(0)
        rc = ntc - 1 - c

        @pl.when(c == 0)
        def _():
            for s in (nz, n2, n4, n8):
                s[...] = jnp.zeros_like(s)
            dw_ref[...] = jnp.zeros_like(dw_ref)
            dsc_ref[...] = jnp.zeros_like(dsc_ref)

        for g in range(ng):
            sl = slice(g * POOL_GROUP_DIM, (g + 1) * POOL_GROUP_DIM)
            pg = p_ref[:, sl]
            dyg = dy_ref[:, sl]
            wg = w_ref[g].astype(BF16)
            q = _dot_nn(pg, wg)
            dsc_ref[:, sl] += jnp.sum(dyg * q, axis=0, keepdims=True)
            dpw = (dyg * sc_ref[:, sl]).astype(BF16)
            dw_ref[g] += _dot_tn(pg, dpw)
            dp_s[:, sl] = _dot_nt(dpw, wg)

        dpv = dp_s[...]
        row = lax.broadcasted_iota(jnp.int32, dpv.shape, 0)
        col = lax.broadcasted_iota(jnp.int32, dpv.shape, 1)
        win = _pool_select(col, POOL_WINDOWS)
        cnt = jnp.minimum(rc * tc + row + 1, win).astype(F32)
        z = dpv / cnt

        def up(v, nv, j):
            return jnp.where(row < tc - j, pltpu.roll(v, tc - j, 0), pltpu.roll(nv[...], tc - j, 0))

        u2 = z + up(z, nz, 1)
        u4 = u2 + up(u2, n2, 2)
        u8 = u4 + up(u4, n4, 4)
        u16 = u8 + up(u8, n8, 8)
        nz[...] = z
        n2[...] = u2
        n4[...] = u4
        n8[...] = u8
        dx_ref[...] = (_pool_select(col, (u2, u4, u8, u16)) - dpv).astype(BF16)

    blk = pl.BlockSpec((tc, dp), lambda c: (ntc - 1 - c, 0))
    full_w = pl.BlockSpec(pool_w.shape, lambda c: (0, 0, 0))
    vec = pl.BlockSpec((1, dp), lambda c: (0, 0))
    return _call(
        body, name="pool_bwd", grid=(ntc,),
        in_specs=[blk, blk, full_w, vec],
        out_specs=[blk, full_w, vec],
        out_shape=[
            jax.ShapeDtypeStruct((t, dp), BF16),
            jax.ShapeDtypeStruct(pool_w.shape, F32),
            jax.ShapeDtypeStruct((1, dp), F32),
        ],
        scratch_shapes=[pltpu.VMEM((tc, dp), F32)] * 5,
        args=(d_y_pool, p, pool_w, pool_scale))[0]


def _win_bwd_norm(parts, w_int, dx2, x, g1, carry=None):
    t, d = x.shape
    tk = 512
    tt = _tile(t, 1024)
    bounds = []
    k0 = 0
    for part in parts:
        assert part.shape[1] % tk == 0
        bounds.append((k0, k0 + part.shape[1] // tk))
        k0 += part.shape[1] // tk
    nk = k0
    assert nk * tk == w_int.shape[0]
    np_ = len(parts)

    def body(*refs):
        p_refs = refs[:np_]
        w_ref, dx2_ref, x_ref, g_ref, gx_ref, dg_ref, acc = refs[np_:]
        i, kk = pl.program_id(0), pl.program_id(1)

        @pl.when(kk == 0)
        def _():
            acc[...] = jnp.zeros_like(acc)

        @pl.when((i == 0) & (kk == 0))
        def _():
            dg_ref[...] = jnp.zeros_like(dg_ref)

        for (lo, hi), p_ref in zip(bounds, p_refs):
            @pl.when((kk >= lo) & (kk < hi))
            def _(p_ref=p_ref):
                acc[...] += _dot_nn(p_ref[...], w_ref[...])

        @pl.when(kk == nk - 1)
        def _():
            def tail(rows):
                xhat, r = _rms_hat(x_ref[rows, :])
                dx, dg = _rms_bwd(acc[rows, :], xhat, r, g_ref[...])
                gx_ref[rows, :] = dx2_ref[rows, :] + dx
                dg_ref[...] += dg

            _row_chunks(tt, tail)

    def part_spec(lo, hi):
        return pl.BlockSpec((tt, tk), lambda i, kk: (i, jnp.clip(kk - lo, 0, hi - lo - 1)))

    row = pl.BlockSpec((tt, d), lambda i, kk: (i, 0))
    vec = pl.BlockSpec((1, d), lambda i, kk: (0, 0))
    return _call(
        body, name="win_bwd_norm", grid=(t // tt, nk),
        in_specs=[part_spec(lo, hi) for lo, hi in bounds]
        + [pl.BlockSpec((tk, d), lambda i, kk: (kk, 0)), row, row, vec],
        out_specs=[row, vec],
        out_shape=[jax.ShapeDtypeStruct((t, d), F32), jax.ShapeDtypeStruct((1, d), F32)],
        scratch_shapes=[pltpu.VMEM((tt, d), F32)],
        args=(*parts, w_int, dx2, x, g1), carry=carry)


def _adam_math(w, g, m, v):
    m = ADAM_B1 * m + (1.0 - ADAM_B1) * g
    v = ADAM_B2 * v + (1.0 - ADAM_B2) * (g * g)
    m_hat = m / (1.0 - ADAM_B1 ** ADAM_STEP)
    v_hat = v / (1.0 - ADAM_B2 ** ADAM_STEP)
    delta = -ADAM_LR * (m_hat / (jnp.sqrt(v_hat) + ADAM_EPS) + ADAM_WD * w)
    return delta, m, v


def _adamw_big(ws, gs, ms, vs, carry=None):
    n = len(ws)
    nb = 8

    def body(*refs):
        for a in range(n):
            w_ref, g_ref, m_ref, v_ref = refs[4 * a:4 * a + 4]
            d_ref, nm_ref, nv_ref = refs[4 * n + 3 * a:4 * n + 3 * a + 3]
            dl, m, v = _adam_math(w_ref[...], g_ref[...], m_ref[...], v_ref[...])
            d_ref[...] = dl
            nm_ref[...] = m
            nv_ref[...] = v

    in_specs, out_specs, out_shape, args = [], [], [], []
    for w, g, m, v in zip(ws, gs, ms, vs):
        rows, cols = w.shape
        blk = pl.BlockSpec((rows // nb, cols), lambda i: (i, 0))
        in_specs += [blk] * 4
        args += [w, g, m, v]
        out_specs += [blk] * 3
        out_shape += [jax.ShapeDtypeStruct(w.shape, F32)] * 3
    outs, got = _call(body, name="adamw_big", grid=(nb,), in_specs=in_specs, out_specs=out_specs,
                      out_shape=out_shape, args=args, carry=carry)
    return [tuple(outs[3 * a:3 * a + 3]) for a in range(n)], got


SMALL_ORDER = ("norm_mix_pre", "norm_mix_post", "norm_mlp_pre", "norm_mlp_post", "b_gate", "conv_w", "conv_b",
               "lru_w_a", "lru_b_a", "lru_w_x", "lru_b_x", "lru_lambda", "pool_w", "pool_scale")
VEC_ROW = dict(norm_mix_pre=0, norm_mix_post=1, norm_mlp_pre=2, norm_mlp_post=3, conv_b=6, lru_b_a=7,
               lru_b_x=8, lru_lambda=9)
ROW_B_GATE, ROW_POOL_SCALE, ROW_CONV_W, ROW_LOSS, N_VEC_ROWS = 4, 10, 11, 15, 16


def _adamw_small(vec_parts, g_pool, g_wa, g_wx, me, params):
    d = vec_parts.shape[2]
    names = SMALL_ORDER
    n = len(names)
    cw_cols = params["conv_w"][0].shape[2]

    def body(me_ref, vec_ref, vecc_ref, gp_ref, gwa_ref, gwx_ref, *refs):
        wmv = refs[:3 * n]
        loss_ref = refs[3 * n]
        outs = refs[3 * n + 1:3 * n + 1 + 4 * n]
        vs, vsc = refs[3 * n + 1 + 4 * n:]
        acc, accc = vec_ref[0], vecc_ref[0]
        for k in range(1, N_DEV):
            acc = acc + vec_ref[k]
            accc = accc + vecc_ref[k]
        vs[...] = acc
        vsc[...] = accc
        loss_ref[...] = vs[ROW_LOSS:ROW_LOSS + 1, 0:128]

        def upd(a, g, idx):
            w_ref, m_ref, v_ref = wmv[3 * a:3 * a + 3]
            g_ref, d_ref, nm_ref, nv_ref = outs[4 * a:4 * a + 4]
            dl, m, v = _adam_math(w_ref[idx], g, m_ref[idx], v_ref[idx])
            g_ref[idx] = g
            d_ref[idx] = dl
            nm_ref[idx] = m
            nv_ref[idx] = v

        for a, name in enumerate(names):
            if name in VEC_ROW:
                r = VEC_ROW[name]
                upd(a, vs[r:r + 1, :], (slice(None), slice(None)))
            elif name == "b_gate":
                for half in range(2):
                    r = ROW_B_GATE + half
                    upd(a, vs[r:r + 1, :], (slice(None), slice(half * d, (half + 1) * d)))
            elif name == "pool_scale":
                width = params[name][0].shape[1]
                upd(a, vs[ROW_POOL_SCALE:ROW_POOL_SCALE + 1, 0:width], (slice(None), slice(None)))
            elif name == "conv_w":
                upd(a, vsc[ROW_CONV_W:ROW_CONV_W + 4, :], (0,))
            elif name == "pool_w":
                upd(a, gp_ref[...], (Ellipsis,))
            elif name == "lru_w_a":
                upd(a, gwa_ref[...], (Ellipsis,))
            elif name == "lru_w_x":
                upd(a, gwx_ref[...], (Ellipsis,))
            else:
                raise ValueError(name)

    def whole(shape):
        nd = len(shape)
        return pl.BlockSpec(tuple(shape), lambda i, me_ref: (0,) * nd)

    in_specs = [
        whole(vec_parts.shape),
        pl.BlockSpec((N_DEV, N_VEC_ROWS, cw_cols), lambda i, me_ref: (0, 0, me_ref[0])),
        whole(g_pool.shape), whole(g_wa.shape), whole(g_wx.shape),
    ]
    args = [vec_parts, vec_parts, g_pool, g_wa, g_wx]
    out_specs = [whole((1, 128))]
    out_shape = [jax.ShapeDtypeStruct((1, 128), F32)]
    for name in names:
        for arr in params[name]:
            in_specs.append(whole(arr.shape))
            args.append(arr)
        shp = params[name][0].shape
        out_specs += [whole(shp)] * 4
        out_shape += [jax.ShapeDtypeStruct(shp, F32)] * 4
    grid_spec = pltpu.PrefetchScalarGridSpec(
        num_scalar_prefetch=1, grid=(1,), in_specs=in_specs, out_specs=out_specs,
        scratch_shapes=[pltpu.VMEM((N_VEC_ROWS, d), F32), pltpu.VMEM((N_VEC_ROWS, cw_cols), F32)])
    outs = pl.pallas_call(
        body, name="adamw_small", grid_spec=grid_spec, out_shape=out_shape,
        compiler_params=pltpu.CompilerParams(
            dimension_semantics=("arbitrary",), vmem_limit_bytes=V7X_VMEM_LIMIT_BYTES),
    )(me, *_in_hbm(args))
    return outs[0], {name: tuple(outs[1 + 4 * a:5 + 4 * a]) for a, name in enumerate(names)}


def _rs_sum(full, recv, shard_ids, slot_ids, name):
    r, rest = recv.shape[1], tuple(recv.shape[2:])
    zeros = (0,) * len(rest)
    send_dtype = recv.dtype

    def body(sh_ref, sl_ref, full_ref, recv_ref, own_ref, send_ref):
        s = pl.program_id(0)
        v = full_ref[...] + recv_ref[...].astype(F32)

        @pl.when(s == 0)
        def _():
            own_ref[...] = v

        @pl.when(s > 0)
        def _():
            send_ref[...] = v.astype(send_dtype)

    grid_spec = pltpu.PrefetchScalarGridSpec(
        num_scalar_prefetch=2,
        grid=(4,),
        in_specs=[
            pl.BlockSpec((r,) + rest, lambda s, sh, sl: (sh[s],) + zeros),
            pl.BlockSpec((None, r) + rest, lambda s, sh, sl: (sl[s], 0) + zeros),
        ],
        out_specs=[
            pl.BlockSpec((None, r) + rest, lambda s, sh, sl: (0, 0) + zeros),
            pl.BlockSpec((None, r) + rest, lambda s, sh, sl: (jnp.maximum(s - 1, 0), 0) + zeros),
        ],
    )
    return pl.pallas_call(
        body,
        name=name,
        grid_spec=grid_spec,
        out_shape=[jax.ShapeDtypeStruct((1, r) + rest, F32), jax.ShapeDtypeStruct((3, r) + rest, send_dtype)],
        compiler_params=pltpu.CompilerParams(
            dimension_semantics=("arbitrary",), vmem_limit_bytes=V7X_VMEM_LIMIT_BYTES),
    )(shard_ids, slot_ids, *_in_hbm([full, recv]))


def _finals(pairs, name, carry=None):
    nb = 4
    n = len(pairs)

    def body(*refs):
        for a in range(n):
            own_ref, recv_ref = refs[2 * a], refs[2 * a + 1]
            acc = own_ref[...]
            for k in range(3):
                acc = acc + recv_ref[k].astype(F32)
            refs[2 * n + a][...] = acc

    in_specs, out_specs, out_shape, args = [], [], [], []
    for own, recv in pairs:
        _, rows, cols = own.shape
        in_specs += [pl.BlockSpec((None, rows // nb, cols), lambda i: (0, i, 0)),
                     pl.BlockSpec((3, rows // nb, cols), lambda i: (0, i, 0))]
        args += [own, recv]
        out_specs.append(pl.BlockSpec((rows // nb, cols), lambda i: (i, 0)))
        out_shape.append(jax.ShapeDtypeStruct((rows, cols), F32))
    return _call(body, name=name, grid=(nb,), in_specs=in_specs, out_specs=out_specs,
                 out_shape=out_shape, args=args, carry=carry)


def _rs_sums(fulls_f32, recv1, tag):
    x, y, c = _place()
    qs = jnp.stack([2 * x + y, 2 * (1 - x) + y, 2 * x + (1 - y), 2 * (1 - x) + (1 - y)]).astype(jnp.int32)
    shard_ids = 2 * qs + c
    return [_rs_sum(f32, r1, shard_ids, qs, f"rs_sum_{tag}{a}")
            for a, (f32, r1) in enumerate(zip(fulls_f32, recv1))]


def _rs_level1(fulls_f32, fulls_send, tag):
    recv1 = _run_plan(_rs_sibling_plan(fulls_send), "rs_sibling_" + tag)
    return _rs_sums(fulls_f32, recv1, tag)


def _rows(g):
    return g.reshape(g.shape[0] * g.shape[1], g.shape[2])


def kernel(x, norm_mix_pre, norm_mix_post, norm_mlp_pre, norm_mlp_post, w_in, b_gate, conv_w, conv_b, lru_w_a, lru_b_a, lru_w_x, lru_b_x, lru_lambda, pool_w, pool_scale, w_lru_up, w_pool_up, w_o, w_ff1, w_ff2, loss_target, m_norm_mix_pre, m_norm_mix_post, m_norm_mlp_pre, m_norm_mlp_post, m_w_in, m_b_gate, m_conv_w, m_conv_b, m_lru_w_a, m_lru_b_a, m_lru_w_x, m_lru_b_x, m_lru_lambda, m_pool_w, m_pool_scale, m_w_lru_up, m_w_pool_up, m_w_o, m_w_ff1, m_w_ff2, v_norm_mix_pre, v_norm_mix_post, v_norm_mlp_pre, v_norm_mlp_post, v_w_in, v_b_gate, v_conv_w, v_conv_b, v_lru_w_a, v_lru_b_a, v_lru_w_x, v_lru_b_x, v_lru_lambda, v_pool_w, v_pool_scale, v_w_lru_up, v_w_pool_up, v_w_o, v_w_ff1, v_w_ff2):
    t, d = x.shape[1], x.shape[2]
    d_rnn = conv_b.shape[1]
    d_pool = pool_scale.shape[1]
    per = LRU_CB // LRU_HEAD_DIM
    xi, yi, ci = _place()
    me = 4 * xi + 2 * yi + ci

    x2d = x[0]
    tgt = loss_target[0]

    s_in = w_in[0].T.astype(BF16)
    s_lu = w_lru_up[0].astype(BF16)
    s_pu = w_pool_up[0].T.astype(BF16)
    s_o = w_o[0].astype(BF16)
    s_f1 = w_ff1[0].T.astype(BF16)
    s_f2 = w_ff2[0].astype(BF16)
    s_cw = jnp.pad(conv_w[0], ((0, 4), (0, 0)))

    g_in, g_cw = _run_plan(_ag_plan([s_in, s_cw]), "ag_w_in")
    w_int = _rows(g_in)
    conv_w_full = jnp.transpose(g_cw[:, :4, :], (1, 0, 2)).reshape(4, d_rnn)

    wa_bd, wx_bd = lru_w_a[0], lru_w_x[0]
    pw = pool_w[0]
    pw_bf = pw.astype(BF16)

    pool_block = (2 * d_rnn) // d_pool
    ga_block = (2 * d_rnn + d_pool) // 512
    gb_block = ga_block + d // 512
    g_block = d_rnn // 512

    r_f1, r_f2 = s_f1.shape[0], s_f2.shape[0]
    f1_cut = r_f1 // 4
    f2_cut = (3 * r_f2) // 8
    plan = _join([_ag_plan([s_lu, s_pu, s_o]), _ag_plan([s_f1], pieces=[(0, f1_cut)])])
    (proj, h1), got = _norm_proj(x2d, norm_mix_pre, w_int, carry=plan)
    (g_lu, g_pu, g_o), (g_f1,) = plan.split(got)
    w_lu, w_put, w_og = _rows(g_lu), _rows(g_pu), _rows(g_o)
    (y_lru, h, xc), (g_f1,) = _lru_fwd(
        proj, conv_w_full, conv_b, wa_bd, lru_b_a, wx_bd, lru_b_x, lru_lambda,
        carry=_ag_plan([s_f1], pieces=[(f1_cut, r_f1 - f1_cut)], bufs=[g_f1]))
    w_f1t = _rows(g_f1)
    y_pool, p = _pool_fwd(proj, pw_bf, pool_scale, pool_block)
    (br_a, br_b, mix), (g_f2,) = _branch_mix(
        y_lru, y_pool, w_lu, w_put, proj, b_gate, ga_block, gb_block,
        carry=_ag_plan([s_f2], pieces=[(0, f2_cut)]))
    (m, x2, h3), (g_f2,) = _wo_norm(
        mix, w_og, x2d, norm_mix_post, norm_mlp_pre,
        carry=_ag_plan([s_f2], pieces=[(f2_cut, r_f2 // 2 - f2_cut)], bufs=[g_f2]))
    (rf,), (g_f2,) = _ff1(
        h3, w_f1t, carry=_ag_plan([s_f2], pieces=[(r_f2 // 2, r_f2 - r_f2 // 2)], bufs=[g_f2]))
    w_f2 = _rows(g_f2)
    dy, df, dg4, loss_part = _ff2_loss(rf, w_f2, x2, norm_mlp_post, tgt)

    (gw_ff2_32, gw_ff2_16), _ = _wgrad(rf, df, "wgrad_ff2", square_a=True)
    (d_f1,), r1_ff2 = _ff2_bwd(df, w_f2, rf, carry=_rs_sibling_plan([gw_ff2_16]))
    ((own_ff2, send_ff2),) = _rs_sums([gw_ff2_32], r1_ff2, "ff2")
    cut2 = (5 * send_ff2.shape[1]) // 16
    (gw_ff1_32, gw_ff1_16), (r2_ff2,) = _wgrad(
        d_f1, h3, "wgrad_ff1", carry=_rs_chips_plan([send_ff2], pieces=[(0, cut2)]))
    plan = _join([_rs_chips_plan([send_ff2], pieces=[(cut2, send_ff2.shape[1] - cut2)], bufs=[r2_ff2]),
                  _rs_sibling_plan([gw_ff1_16])])
    (dx2, dm, dg3, dg2), got = _ff1_bwd_norms(d_f1, w_f1t, dy, x2, norm_mlp_pre, m, norm_mix_post, carry=plan)
    (r2_ff2,), r1_ff1 = plan.split(got)
    ((own_ff1, send_ff1),) = _rs_sums([gw_ff1_32], r1_ff1, "ff1")
    cut = send_ff1.shape[1] // 4
    (gw_o_32, gw_o_16), _ = _wgrad(mix, dm, "wgrad_o")
    (d_br_a, d_br_b, p_ga, p_gb, dbg_a, dbg_b), (r2_ff1,) = _wo_bwd_mix(
        dm, w_og, br_a, br_b, proj, b_gate, ga_block, gb_block,
        carry=_rs_chips_plan([send_ff1], pieces=[(0, cut)]))
    (gw_lu_32, gw_lu_16), _ = _wgrad(y_lru, d_br_a, "wgrad_lru_up")
    (gw_pu_32, gw_pu_16), _ = _wgrad(d_br_b, y_pool, "wgrad_pool_up")
    (dh, p_g), r1_mid = _lru_up_bwd(
        d_br_a, w_lu, proj, h, g_block,
        carry=_rs_sibling_plan([gw_o_16, gw_lu_16, gw_pu_16.reshape(-1, d)]))
    mid = _rs_sums([gw_o_32, gw_lu_32, gw_pu_32.reshape(-1, d)], r1_mid, "mid")
    d_y_pool = _pool_up_bwd(d_br_b, w_put)
    (p_x, dwa, db_a, dwx, db_x, dlam, dconv_w, dconv_b), (r2_ff1,) = _lru_bwd(
        dh, xc, h, proj, conv_w_full, wa_bd, lru_b_a, wx_bd, lru_b_x, lru_lambda,
        carry=_rs_chips_plan([send_ff1], pieces=[(cut, send_ff1.shape[1] - cut)], bufs=[r2_ff1]))
    p_p, dpool_w, dpool_scale = _pool_bwd(d_y_pool, p, pw, pool_scale)
    parts = [p_x, p_g, p_p, p_ga, p_gb]
    gw_in, r2_mid = _wgrad_parts(parts, h1, "wgrad_in", carry=_rs_chips_plan([s for _, s in mid]))
    tail = _rs_level1([gw_in[0], dpool_w.reshape(N_DEV, -1, POOL_GROUP_DIM), dwa, dwx],
                      [gw_in[1], dpool_w.reshape(N_DEV, -1, POOL_GROUP_DIM), dwa, dwx], "in")
    (grad_x, dg1), r2_tail = _win_bwd_norm(parts, w_int, dx2, x2d, norm_mix_pre,
                                           carry=_rs_chips_plan([s for _, s in tail]))

    def flat2(a):
        return a.reshape(a.shape[0], -1, a.shape[-1])

    fin_small, _ = _finals([
        (flat2(tail[1][0]), flat2(r2_tail[1])), (flat2(tail[2][0]), flat2(r2_tail[2])),
        (flat2(tail[3][0]), flat2(r2_tail[3])),
    ], "rs_finals_small")

    def pad_row(a):
        return jnp.pad(a, ((0, 0), (0, d - a.shape[1])))

    vecs = jnp.concatenate([dg1, dg2, dg3, dg4, dbg_a, dbg_b, dconv_b, db_a, db_x, dlam,
                            pad_row(dpool_scale), dconv_w, pad_row(loss_part)], axis=0)
    assert vecs.shape[0] == N_VEC_ROWS
    fin, (vec_parts, g_pool, g_wa, g_wx) = _finals([
        (tail[0][0], r2_tail[0]), (mid[1][0], r2_mid[1]), (mid[2][0], r2_mid[2]), (mid[0][0], r2_mid[0]),
        (own_ff1, r2_ff1), (own_ff2, r2_ff2),
    ], "rs_finals", carry=_ag_plan([vecs] + fin_small))
    g_w_in = fin[0].T
    g_w_lru_up = fin[1]
    g_w_pool_up = fin[2].reshape(d // N_DEV, d_pool).T
    g_w_o = fin[3]
    g_w_ff1 = fin[4].T
    g_w_ff2 = fin[5]

    big_names = ["w_in", "w_lru_up", "w_pool_up", "w_o", "w_ff1", "w_ff2"]
    big_w = [w_in, w_lru_up, w_pool_up, w_o, w_ff1, w_ff2]
    big_g = [g_w_in, g_w_lru_up, g_w_pool_up, g_w_o, g_w_ff1, g_w_ff2]
    big_m = [m_w_in, m_w_lru_up, m_w_pool_up, m_w_o, m_w_ff1, m_w_ff2]
    big_v = [v_w_in, v_w_lru_up, v_w_pool_up, v_w_o, v_w_ff1, v_w_ff2]
    big_out, _ = _adamw_big([w[0] for w in big_w], big_g, [mm[0] for mm in big_m], [vv[0] for vv in big_v])

    small = dict(
        norm_mix_pre=(norm_mix_pre, m_norm_mix_pre, v_norm_mix_pre),
        norm_mix_post=(norm_mix_post, m_norm_mix_post, v_norm_mix_post),
        norm_mlp_pre=(norm_mlp_pre, m_norm_mlp_pre, v_norm_mlp_pre),
        norm_mlp_post=(norm_mlp_post, m_norm_mlp_post, v_norm_mlp_post),
        b_gate=(b_gate, m_b_gate, v_b_gate), conv_w=(conv_w, m_conv_w, v_conv_w),
        conv_b=(conv_b, m_conv_b, v_conv_b), lru_w_a=(lru_w_a, m_lru_w_a, v_lru_w_a),
        lru_b_a=(lru_b_a, m_lru_b_a, v_lru_b_a), lru_w_x=(lru_w_x, m_lru_w_x, v_lru_w_x),
        lru_b_x=(lru_b_x, m_lru_b_x, v_lru_b_x), lru_lambda=(lru_lambda, m_lru_lambda, v_lru_lambda),
        pool_w=(pool_w, m_pool_w, v_pool_w), pool_scale=(pool_scale, m_pool_scale, v_pool_scale))
    loss_row, small_out = _adamw_small(
        vec_parts, g_pool.reshape(pool_w.shape), g_wa.reshape(lru_w_a.shape), g_wx.reshape(lru_w_x.shape),
        jnp.reshape(me, (1,)).astype(jnp.int32), small)
    grads = {n: o[0] for n, o in small_out.items()}
    delta = {n: o[1] for n, o in small_out.items()}
    new_m = {n: o[2] for n, o in small_out.items()}
    new_v = {n: o[3] for n, o in small_out.items()}

    for name, g, (dl, nm, nv) in zip(big_names, big_g, big_out):
        grads[name], delta[name], new_m[name], new_v[name] = g[None], dl[None], nm[None], nv[None]

    loss = loss_row[0, 0]
    order = ["norm_mix_pre", "norm_mix_post", "norm_mlp_pre", "norm_mlp_post", "w_in", "b_gate", "conv_w",
             "conv_b", "lru_w_a", "lru_b_a", "lru_w_x", "lru_b_x", "lru_lambda", "pool_w", "pool_scale",
             "w_lru_up", "w_pool_up", "w_o", "w_ff1", "w_ff2"]
    return (loss, grad_x[None], *[grads[n] for n in order], *[delta[n] for n in order],
            *[new_m[n] for n in order], *[new_v[n] for n in order])
```

```python
import functools
import math
import operator
import types

import jax
import jax.numpy as jnp
from jax import lax
from jax.experimental import pallas as pl
from jax.experimental.pallas import tpu as pltpu

F32 = jnp.float32
BF16 = jnp.bfloat16
NORM_EPS = 1e-6
LRU_C = 8.0
N_LRU_HEADS = 16
LRU_HEAD_DIM = 64
POOL_WINDOWS = (2, 4, 8, 16)
POOL_GROUP_DIM = 128
ADAM_LR = 0.001
ADAM_B1 = 0.9
ADAM_B2 = 0.999
ADAM_EPS = 1e-08
ADAM_WD = 0.01
ADAM_STEP = 10
N_DEV = 8
V7X_VMEM_LIMIT_BYTES = 56 * 1024 * 1024
LRU_CB = 256
MESH = pl.DeviceIdType.MESH
ANY = pl.BlockSpec(memory_space=pl.ANY)


def _tile(n, pref):
    t = min(n, pref)
    assert n % t == 0, (n, pref)
    return t


def _dot_nn(a, b):
    return lax.dot_general(a, b, (((1,), (0,)), ((), ())), preferred_element_type=F32)


def _dot_nt(a, b):
    return lax.dot_general(a, b, (((1,), (1,)), ((), ())), preferred_element_type=F32)


def _dot_tn(a, b):
    return lax.dot_general(a, b, (((0,), (0,)), ((), ())), preferred_element_type=F32)


def _row_chunks(n_rows, fn, chunk=256):
    chunk = min(chunk, n_rows)
    assert n_rows % chunk == 0

    def step(r, carry):
        fn(pl.ds(pl.multiple_of(r * chunk, chunk), chunk))
        return carry

    lax.fori_loop(0, n_rows // chunk, step, 0)


def _sig(x):
    return 1.0 / (1.0 + jnp.exp(-x))


def _rms_hat(x):
    r = lax.rsqrt(jnp.mean(x * x, axis=-1, keepdims=True) + NORM_EPS)
    return x * r, r


def _rms_bwd(dn, xhat, r, g):
    q = dn * g
    dx = r * (q - xhat * jnp.mean(q * xhat, axis=-1, keepdims=True))
    dg = jnp.sum(dn * xhat, axis=0, keepdims=True)
    return dx, dg


_GELU_K = math.sqrt(2.0 / math.pi)
_GELU_C = 0.044715


def _gelu_and_grad(g):
    t = jnp.tanh(_GELU_K * (g + _GELU_C * g * g * g))
    val = 0.5 * g * (1.0 + t)
    grad = 0.5 * (1.0 + t) + 0.5 * g * (1.0 - t * t) * (_GELU_K * (1.0 + 3.0 * _GELU_C * g * g))
    return val, grad


def _softplus_neg(lam):
    z = -lam
    e = jnp.exp(-jnp.abs(z))
    u = 1.0 + e
    d = u - 1.0
    l1p = jnp.where(d == 0.0, e, jnp.log(u) * (e / jnp.where(d == 0.0, 1.0, d)))
    return jnp.maximum(z, 0.0) + l1p


def _lru_gates(xc, wa, ba, wx, bx, lam):
    xcb = xc.astype(BF16)
    r = _sig(_dot_nn(xcb, wa) + ba)
    i = _sig(_dot_nn(xcb, wx) + bx)
    sp = _softplus_neg(lam)
    log_a = (-LRU_C) * r * sp
    a = jnp.exp(log_a)
    mult = jnp.sqrt(-jnp.tanh(log_a) * (1.0 + a * a))
    return xcb, r, i, sp, log_a, a, mult


def _place():
    return lax.axis_index("x"), lax.axis_index("y"), lax.axis_index("c")


def _ag_plan(shards, pieces=None, bufs=None):
    na = len(shards)
    n_kinds = 7

    def parts(ins, outs, sems):
        send_sems, recv_sems, local_sems = sems
        x, y, c = _place()
        me, sibling = (x, y, c), (x, y, 1 - c)
        x_nb, y_nb, diag = (1 - x, y), (x, 1 - y), (1 - x, 1 - y)
        relay_src = (c * (1 - x) + (1 - c) * x, c * y + (1 - c) * (1 - y))
        relay_dst = (c * x + (1 - c) * (1 - x), c * (1 - y) + (1 - c) * y)

        def own(a):
            return ins[a] if pieces is None else ins[a].at[pl.ds(*pieces[a])]

        def slot(a, px, py, pc):
            idx = 4 * px + 2 * py + pc
            return outs[a].at[idx] if pieces is None else outs[a].at[idx, pl.ds(*pieces[a])]

        def copy(a, k, block, to, src=None):
            return pltpu.make_async_remote_copy(
                src_ref=slot(a, *block) if src is None else src,
                dst_ref=slot(a, *block),
                send_sem=send_sems.at[a * n_kinds + k],
                recv_sem=recv_sems.at[a * n_kinds + k],
                device_id=to,
                device_id_type=MESH,
            )

        mine = [pltpu.make_async_copy(own(a), slot(a, *me), local_sems.at[a]) for a in range(na)]
        first, second, third = [], [], []
        for a in range(na):
            first += [copy(a, 0, me, sibling, src=own(a)), copy(a, 1, me, (*x_nb, c), src=own(a)),
                      copy(a, 2, me, (*y_nb, c), src=own(a))]
            second += [copy(a, 3, (*relay_src, c), (*relay_dst, c)), copy(a, 4, (*x_nb, c), sibling),
                       copy(a, 5, (*y_nb, c), sibling)]
            third.append(copy(a, 6, (*diag, c), sibling))
        return sibling, c, x_nb, y_nb, diag, copy, mine, first, second, third

    def start(ins, outs, sems):
        _, _, _, _, _, _, mine, first, _, _ = parts(ins, outs, sems)
        for cp in mine + first:
            cp.start()

    def middle(ins, outs, sems):
        _, c, x_nb, y_nb, _, copy, _, _, second, _ = parts(ins, outs, sems)
        for a in range(na):
            copy(a, 1, (*x_nb, c), (*x_nb, c)).wait_recv()
            copy(a, 2, (*y_nb, c), (*y_nb, c)).wait_recv()
        for cp in second:
            cp.start()

    def finish(ins, outs, sems):
        sibling, c, x_nb, y_nb, diag, copy, mine, first, second, third = parts(ins, outs, sems)
        for a in range(na):
            copy(a, 3, (*diag, c), (*diag, c)).wait_recv()
            third[a].start()
        for a in range(na):
            copy(a, 0, sibling, sibling).wait_recv()
            copy(a, 4, (*x_nb, 1 - c), sibling).wait_recv()
            copy(a, 5, (*y_nb, 1 - c), sibling).wait_recv()
            copy(a, 6, (*diag, 1 - c), sibling).wait_recv()
        for cp in first + second + third:
            cp.wait_send()
        for cp in mine:
            cp.wait()

    return types.SimpleNamespace(
        ins=list(shards) + list(bufs or []),
        out_shapes=[jax.ShapeDtypeStruct((N_DEV,) + s.shape, s.dtype) for s in shards],
        sems=[pltpu.SemaphoreType.DMA((n_kinds * na,)), pltpu.SemaphoreType.DMA((n_kinds * na,)),
              pltpu.SemaphoreType.DMA((na,))],
        aliases=[(na + a, a) for a in range(na)] if bufs else [],
        peers=frozenset({"sibling", "neighbours"}), start=start, middle=middle, finish=finish)


def _rs_sibling_plan(fulls):
    na = len(fulls)
    rs = [f.shape[0] // N_DEV for f in fulls]

    def copies(ins, outs, sems):
        send_sems, recv_sems = sems
        x, y, c = _place()
        out = []
        for a in range(na):
            for q in range(4):
                shard = 2 * q + (1 - c)
                out.append(pltpu.make_async_remote_copy(
                    src_ref=ins[a].at[pl.ds(shard * rs[a], rs[a])],
                    dst_ref=outs[a].at[q],
                    send_sem=send_sems.at[a * 4 + q],
                    recv_sem=recv_sems.at[a * 4 + q],
                    device_id=(x, y, 1 - c),
                    device_id_type=MESH,
                ))
        return out

    def start(ins, outs, sems):
        for cp in copies(ins, outs, sems):
            cp.start()

    def finish(ins, outs, sems):
        for cp in copies(ins, outs, sems):
            cp.wait()

    return types.SimpleNamespace(
        ins=list(fulls),
        out_shapes=[jax.ShapeDtypeStruct((4, r) + f.shape[1:], f.dtype) for r, f in zip(rs, fulls)],
        sems=[pltpu.SemaphoreType.DMA((4 * na,)), pltpu.SemaphoreType.DMA((4 * na,))],
        peers=frozenset({"sibling"}), start=start, finish=finish)


def _rs_chips_plan(sends, pieces=None, bufs=None):
    na = len(sends)

    def copies(ins, outs, sems):
        send_sems, recv_sems = sems
        x, y, c = _place()
        chips = [(1 - x, y), (x, 1 - y), (1 - x, 1 - y)]
        out = []
        for a in range(na):
            for k, chip in enumerate(chips):
                rows = (k,) if pieces is None else (k, pl.ds(*pieces[a]))
                out.append(pltpu.make_async_remote_copy(
                    src_ref=ins[a].at[rows],
                    dst_ref=outs[a].at[rows],
                    send_sem=send_sems.at[a * 3 + k],
                    recv_sem=recv_sems.at[a * 3 + k],
                    device_id=(*chip, c),
                    device_id_type=MESH,
                ))
        return out

    def start(ins, outs, sems):
        for cp in copies(ins, outs, sems):
            cp.start()

    def finish(ins, outs, sems):
        for cp in copies(ins, outs, sems):
            cp.wait()

    return types.SimpleNamespace(
        ins=list(sends) + list(bufs or []),
        out_shapes=[jax.ShapeDtypeStruct(s.shape, s.dtype) for s in sends],
        sems=[pltpu.SemaphoreType.DMA((3 * na,)), pltpu.SemaphoreType.DMA((3 * na,))],
        aliases=[(na + a, a) for a in range(na)] if bufs else [],
        peers=frozenset({"chips"}), start=start, finish=finish)


def _join(plans):
    ins, outs, sems, aliases, offs = [], [], [], [], []
    for p in plans:
        offs.append((len(ins), len(outs), len(sems)))
        aliases += [(len(ins) + ci, len(outs) + co) for ci, co in getattr(p, "aliases", [])]
        ins += p.ins
        outs += p.out_shapes
        sems += p.sems

    def cut(p, off, i, o, s):
        return (i[off[0]:off[0] + len(p.ins)], o[off[1]:off[1] + len(p.out_shapes)],
                s[off[2]:off[2] + len(p.sems)])

    def start(i, o, s):
        for p, off in zip(plans, offs):
            p.start(*cut(p, off, i, o, s))

    def middle(i, o, s):
        for p, off in zip(plans, offs):
            if getattr(p, "middle", None) is not None:
                p.middle(*cut(p, off, i, o, s))

    def finish(i, o, s):
        for p, off in zip(plans, offs):
            p.finish(*cut(p, off, i, o, s))

    def split(results):
        return [list(results[off[1]:off[1] + len(p.out_shapes)]) for p, off in zip(plans, offs)]

    return types.SimpleNamespace(ins=ins, out_shapes=outs, sems=sems, aliases=aliases,
                                 peers=frozenset().union(*[p.peers for p in plans]),
                                 start=start, middle=middle, finish=finish, split=split)


COLLECTIVE_ID = {frozenset({"sibling"}): 0, frozenset({"chips"}): 1, frozenset({"sibling", "chips"}): 2,
                 frozenset({"sibling", "neighbours"}): 3}


def _handshake(peers):
    x, y, c = _place()
    devs = []
    if "sibling" in peers:
        devs.append((x, y, 1 - c))
    if "neighbours" in peers:
        devs += [(1 - x, y, c), (x, 1 - y, c)]
    if "chips" in peers:
        assert "neighbours" not in peers
        devs += [(1 - x, y, c), (x, 1 - y, c), (1 - x, 1 - y, c)]
    barrier = pltpu.get_barrier_semaphore()
    for dev in devs:
        pl.semaphore_signal(barrier, inc=1, device_id=dev, device_id_type=MESH)
    pl.semaphore_wait(barrier, len(devs))


def _in_hbm(args):
    return [pltpu.with_memory_space_constraint(a, pltpu.HBM) for a in args]


def _run_plan(plan, name):
    n_in, n_out = len(plan.ins), len(plan.out_shapes)

    def body(*refs):
        ins, outs, sems = refs[:n_in], refs[n_in:n_in + n_out], refs[n_in + n_out:]
        _handshake(plan.peers)
        plan.start(ins, outs, sems)
        if getattr(plan, "middle", None) is not None:
            plan.middle(ins, outs, sems)
        plan.finish(ins, outs, sems)

    return pl.pallas_call(
        body,
        name=name,
        in_specs=[ANY] * n_in,
        out_specs=[ANY] * n_out,
        out_shape=plan.out_shapes,
        scratch_shapes=plan.sems,
        input_output_aliases=dict(getattr(plan, "aliases", [])),
        compiler_params=pltpu.CompilerParams(collective_id=COLLECTIVE_ID[plan.peers]),
    )(*_in_hbm(plan.ins))


def _call(body, *, name, grid, in_specs, out_specs, out_shape, args, scratch_shapes=(), aliases=None,
          carry=None):
    n_in, n_out, n_scr = len(in_specs), len(out_shape), len(scratch_shapes)
    params = pltpu.CompilerParams(
        dimension_semantics=("arbitrary",) * len(grid), vmem_limit_bytes=V7X_VMEM_LIMIT_BYTES)
    if carry is None:
        outs = pl.pallas_call(
            body, name=name, grid=grid, in_specs=list(in_specs), out_specs=list(out_specs),
            out_shape=list(out_shape), scratch_shapes=list(scratch_shapes),
            input_output_aliases=aliases or {}, compiler_params=params)(*_in_hbm(args))
        return list(outs), []
    c_in, c_out = len(carry.ins), len(carry.out_shapes)

    def full(*refs):
        p = 0
        ins = refs[p:p + n_in]
        p += n_in
        cins = refs[p:p + c_in]
        p += c_in
        outs = refs[p:p + n_out]
        p += n_out
        couts = refs[p:p + c_out]
        p += c_out
        scr = refs[p:p + n_scr]
        csems = refs[p + n_scr:]
        ids = [pl.program_id(a) for a in range(len(grid))]
        first = functools.reduce(operator.and_, [i == 0 for i in ids])
        last = functools.reduce(operator.and_, [i == g - 1 for i, g in zip(ids, grid)])

        @pl.when(first)
        def _():
            _handshake(carry.peers)
            carry.start(cins, couts, csems)

        if getattr(carry, "middle", None) is not None:
            n_steps = math.prod(grid)
            flat = functools.reduce(lambda acc, ig: acc * ig[1] + ig[0], zip(ids, grid), 0)

            @pl.when(flat == (2 * n_steps) // 3)
            def _():
                carry.middle(cins, couts, csems)

        body(*ins, *outs, *scr)

        @pl.when(last)
        def _():
            carry.finish(cins, couts, csems)

    all_aliases = dict(aliases or {})
    all_aliases.update({n_in + ci: n_out + co for ci, co in getattr(carry, "aliases", [])})
    params = pltpu.CompilerParams(
        dimension_semantics=("arbitrary",) * len(grid), vmem_limit_bytes=V7X_VMEM_LIMIT_BYTES,
        collective_id=COLLECTIVE_ID[carry.peers])
    outs = pl.pallas_call(
        full, name=name, grid=grid,
        in_specs=list(in_specs) + [ANY] * c_in,
        out_specs=list(out_specs) + [ANY] * c_out,
        out_shape=list(out_shape) + list(carry.out_shapes),
        scratch_shapes=list(scratch_shapes) + list(carry.sems),
        input_output_aliases=all_aliases, compiler_params=params)(*_in_hbm(args), *_in_hbm(carry.ins))
    return list(outs[:n_out]), list(outs[n_out:])


def _norm_proj(x, g1, w_int, carry=None):
    t, d = x.shape
    n = w_int.shape[0]
    tt, tn = _tile(t, 2048), _tile(n, 512)

    def body(x_ref, g_ref, w_ref, proj_ref, h1_ref, h1_s):
        @pl.when(pl.program_id(1) == 0)
        def _():
            def norm_rows(rows):
                xhat, _ = _rms_hat(x_ref[rows, :])
                h = (xhat * g_ref[...]).astype(BF16)
                h1_s[rows, :] = h
                h1_ref[rows, :] = h

            _row_chunks(tt, norm_rows)

        proj_ref[...] = _dot_nt(h1_s[...], w_ref[...]).astype(BF16)

    return _call(
        body, name="norm_proj", grid=(t // tt, n // tn),
        in_specs=[
            pl.BlockSpec((tt, d), lambda i, j: (i, 0)),
            pl.BlockSpec((1, d), lambda i, j: (0, 0)),
            pl.BlockSpec((tn, d), lambda i, j: (j, 0)),
        ],
        out_specs=[
            pl.BlockSpec((tt, tn), lambda i, j: (i, j)),
            pl.BlockSpec((tt, d), lambda i, j: (i, 0)),
        ],
        out_shape=[jax.ShapeDtypeStruct((t, n), BF16), jax.ShapeDtypeStruct((t, d), BF16)],
        scratch_shapes=[pltpu.VMEM((tt, d), BF16)],
        args=(x, g1, w_int), carry=carry)


def _scan_rows(av, bv, reverse):
    tc = av.shape[0]
    row = lax.broadcasted_iota(jnp.int32, av.shape, 0)
    s = 1
    while s < tc:
        if s < 8:
            keep = (row < tc - s) if reverse else (row >= s)
            shift = (tc - s) if reverse else s
            a_sh = jnp.where(keep, pltpu.roll(av, shift, 0), 1.0)
            b_sh = jnp.where(keep, pltpu.roll(bv, shift, 0), 0.0)
            bv = av * b_sh + bv
            av = av * a_sh
        elif reverse:
            bv = jnp.concatenate([av[:tc - s] * bv[s:] + bv[:tc - s], bv[tc - s:]], axis=0)
            av = jnp.concatenate([av[:tc - s] * av[s:], av[tc - s:]], axis=0)
        else:
            bv = jnp.concatenate([bv[:s], av[s:] * bv[:tc - s] + bv[s:]], axis=0)
            av = jnp.concatenate([av[:s], av[s:] * av[:tc - s]], axis=0)
        s *= 2
    return av, bv


def _fill_block_diag(w_ref, bd_ref):
    bd_ref[...] = jnp.zeros_like(bd_ref)
    hd = LRU_HEAD_DIM
    for k in range(w_ref.shape[0]):
        bd_ref[k * hd:(k + 1) * hd, k * hd:(k + 1) * hd] = w_ref[k].astype(BF16)


def _lru_fwd(proj, conv_w, conv_b, w_a, b_a, w_x, b_x, lam, carry=None):
    t = proj.shape[0]
    dr = conv_b.shape[1]
    cb = LRU_CB
    tc = _tile(t, 256)
    ncb, ntc = dr // cb, t // tc

    def body(xp_ref, g_ref, cw_ref, cb_ref, wa_ref, ba_ref, wx_ref, bx_ref, lam_ref,
             y_ref, h_ref, xc_ref, prevx_s, hlast_s, wa_s, wx_s):
        c = pl.program_id(1)

        @pl.when(c == 0)
        def _():
            prevx_s[...] = jnp.zeros_like(prevx_s)
            hlast_s[...] = jnp.zeros_like(hlast_s)
            _fill_block_diag(wa_ref, wa_s)
            _fill_block_diag(wx_ref, wx_s)

        x = xp_ref[...].astype(F32)
        prev = prevx_s[...]
        row = lax.broadcasted_iota(jnp.int32, x.shape, 0)

        def sh(j):
            return jnp.where(row >= j, pltpu.roll(x, j, 0), pltpu.roll(prev, j, 0))

        xc = (cb_ref[...] + cw_ref[0:1, :] * sh(3) + cw_ref[1:2, :] * sh(2)
              + cw_ref[2:3, :] * sh(1) + cw_ref[3:4, :] * x)
        prevx_s[...] = x
        xc_ref[...] = xc
        _, _, i, _, _, a, mult = _lru_gates(xc, wa_s[...], ba_ref[...], wx_s[...], bx_ref[...],
                                            lam_ref[...])
        av, bv = _scan_rows(a, mult * (i * xc), reverse=False)
        h = av * hlast_s[...] + bv
        h_ref[...] = h
        hlast_s[...] = h_ref[tc - 1:tc, :]
        gel, _ = _gelu_and_grad(g_ref[...].astype(F32))
        y_ref[...] = (h * gel).astype(BF16)

    vec = pl.BlockSpec((1, cb), lambda j, c: (0, j))
    blk = pl.BlockSpec((tc, cb), lambda j, c: (c, j))
    mat = pl.BlockSpec((cb // LRU_HEAD_DIM, LRU_HEAD_DIM, LRU_HEAD_DIM), lambda j, c: (j, 0, 0))
    return _call(
        body, name="lru_fwd", grid=(ncb, ntc),
        in_specs=[
            blk,
            pl.BlockSpec((tc, cb), lambda j, c: (c, ncb + j)),
            pl.BlockSpec((4, cb), lambda j, c: (0, j)),
            vec, mat, vec, mat, vec, vec,
        ],
        out_specs=[blk, blk, blk],
        out_shape=[
            jax.ShapeDtypeStruct((t, dr), BF16),
            jax.ShapeDtypeStruct((t, dr), F32),
            jax.ShapeDtypeStruct((t, dr), F32),
        ],
        scratch_shapes=[pltpu.VMEM((tc, cb), F32), pltpu.VMEM((1, cb), F32),
                        pltpu.VMEM((cb, cb), BF16), pltpu.VMEM((cb, cb), BF16)],
        args=(proj, proj, conv_w, conv_b, w_a, b_a, w_x, b_x, lam), carry=carry)


def _pool_select(col, vals):
    out = vals[3]
    for g in (2, 1, 0):
        out = jnp.where(col < (g + 1) * POOL_GROUP_DIM, vals[g], out)
    return out


def _pool_fwd(proj, pool_w, pool_scale, col_block):
    t = proj.shape[0]
    dp = pool_scale.shape[1]
    tc = _tile(t, 256)
    ntc = t // tc

    def body(x_ref, w_ref, sc_ref, y_ref, p_ref, px, p2, p4, p8):
        c = pl.program_id(0)

        @pl.when(c == 0)
        def _():
            for s in (px, p2, p4, p8):
                s[...] = jnp.zeros_like(s)

        x = x_ref[...].astype(F32)
        row = lax.broadcasted_iota(jnp.int32, x.shape, 0)
        col = lax.broadcasted_iota(jnp.int32, x.shape, 1)

        def sh(v, pv, j):
            return jnp.where(row >= j, pltpu.roll(v, j, 0), pltpu.roll(pv[...], j, 0))

        s2 = x + sh(x, px, 1)
        s4 = s2 + sh(s2, p2, 2)
        s8 = s4 + sh(s4, p4, 4)
        s16 = s8 + sh(s8, p8, 8)
        px[...] = x
        p2[...] = s2
        p4[...] = s4
        p8[...] = s8
        wsum = _pool_select(col, (s2, s4, s8, s16))
        win = _pool_select(col, POOL_WINDOWS)
        cnt = jnp.minimum(c * tc + row + 1, win).astype(F32)
        p = wsum / cnt - x
        pb = p.astype(BF16)
        p_ref[...] = pb
        for g in range(len(POOL_WINDOWS)):
            sl = slice(g * POOL_GROUP_DIM, (g + 1) * POOL_GROUP_DIM)
            yg = _dot_nn(pb[:, sl], w_ref[g]) * sc_ref[:, sl]
            y_ref[:, sl] = yg.astype(BF16)

    return _call(
        body, name="pool_fwd", grid=(ntc,),
        in_specs=[
            pl.BlockSpec((tc, dp), lambda c: (c, col_block)),
            pl.BlockSpec(pool_w.shape, lambda c: (0, 0, 0)),
            pl.BlockSpec((1, dp), lambda c: (0, 0)),
        ],
        out_specs=[pl.BlockSpec((tc, dp), lambda c: (c, 0))] * 2,
        out_shape=[jax.ShapeDtypeStruct((t, dp), BF16)] * 2,
        scratch_shapes=[pltpu.VMEM((tc, dp), F32)] * 4,
        args=(proj, pool_w, pool_scale))[0]


def _branch_mix(y_lru, y_pool, w_lru_up, w_pool_upt, proj, b_gate, ga_block, gb_block, carry=None):
    t, d = y_lru.shape
    dp = y_pool.shape[1]
    tt, tn = _tile(t, 1024), 512
    nj = d // tn

    def body(yl_ref, yp_ref, wl_ref, wp_ref, ga_ref, gb_ref, ba_ref, bb_ref, bra_ref, brb_ref, mix_ref):
        br_a = _dot_nn(yl_ref[...], wl_ref[...])
        br_b = _dot_nt(yp_ref[...], wp_ref[...])
        bra_ref[...] = br_a.astype(BF16)
        brb_ref[...] = br_b.astype(BF16)
        ga = _sig(ga_ref[...].astype(F32) + ba_ref[...])
        gb = _sig(gb_ref[...].astype(F32) + bb_ref[...])
        mix_ref[...] = (ga * br_a + gb * br_b).astype(BF16)

    out = pl.BlockSpec((tt, tn), lambda j, i: (i, j))
    return _call(
        body, name="branch_mix", grid=(nj, t // tt),
        in_specs=[
            pl.BlockSpec((tt, d), lambda j, i: (i, 0)),
            pl.BlockSpec((tt, dp), lambda j, i: (i, 0)),
            pl.BlockSpec((d, tn), lambda j, i: (0, j)),
            pl.BlockSpec((tn, dp), lambda j, i: (j, 0)),
            pl.BlockSpec((tt, tn), lambda j, i: (i, ga_block + j)),
            pl.BlockSpec((tt, tn), lambda j, i: (i, gb_block + j)),
            pl.BlockSpec((1, tn), lambda j, i: (0, j)),
            pl.BlockSpec((1, tn), lambda j, i: (0, nj + j)),
        ],
        out_specs=[out, out, out],
        out_shape=[jax.ShapeDtypeStruct((t, d), BF16)] * 3,
        args=(y_lru, y_pool, w_lru_up, w_pool_upt, proj, proj, b_gate, b_gate), carry=carry)


def _wo_norm(mix, w_o, x, g2, g3, carry=None):
    t, d = x.shape
    tt = _tile(t, 512)

    def body(mix_ref, w_ref, x_ref, g2_ref, g3_ref, m_ref, x2_ref, h3_ref):
        m = _dot_nn(mix_ref[...], w_ref[...])
        m_ref[...] = m
        mhat, _ = _rms_hat(m)
        x2 = x_ref[...] + mhat * g2_ref[...]
        x2_ref[...] = x2
        xhat, _ = _rms_hat(x2)
        h3_ref[...] = (xhat * g3_ref[...]).astype(BF16)

    row = pl.BlockSpec((tt, d), lambda i: (i, 0))
    vec = pl.BlockSpec((1, d), lambda i: (0, 0))
    return _call(
        body, name="wo_norm", grid=(t // tt,),
        in_specs=[row, pl.BlockSpec((d, d), lambda i: (0, 0)), row, vec, vec],
        out_specs=[row, row, row],
        out_shape=[
            jax.ShapeDtypeStruct((t, d), F32),
            jax.ShapeDtypeStruct((t, d), F32),
            jax.ShapeDtypeStruct((t, d), BF16),
        ],
        args=(mix, w_o, x, g2, g3), carry=carry)


def _ff1(h3, w_ff1t, carry=None):
    t, d = h3.shape
    n = w_ff1t.shape[0]
    tt, tn = _tile(t, 2048), _tile(n, 512)

    def body(h_ref, w_ref, rf_ref):
        rf_ref[...] = jnp.maximum(_dot_nt(h_ref[...], w_ref[...]), 0.0).astype(BF16)

    out = pl.BlockSpec((tt, tn), lambda i, j: (i, j))
    return _call(
        body, name="ff1", grid=(t // tt, n // tn),
        in_specs=[pl.BlockSpec((tt, d), lambda i, j: (i, 0)), pl.BlockSpec((tn, d), lambda i, j: (j, 0))],
        out_specs=[out],
        out_shape=[jax.ShapeDtypeStruct((t, n), BF16)],
        args=(h3, w_ff1t), carry=carry)


def _ff2_loss(rf, w_ff2, x2, g4, target):
    t, k = rf.shape
    d = x2.shape[1]
    tt, tk = _tile(t, 1024), _tile(k, 512)
    nk = k // tk

    def body(a_ref, w_ref, x2_ref, g_ref, tg_ref, dy_ref, df_ref, dg_ref, loss_ref, acc):
        i, kk = pl.program_id(0), pl.program_id(1)

        @pl.when(kk == 0)
        def _():
            acc[...] = jnp.zeros_like(acc)

        @pl.when((i == 0) & (kk == 0))
        def _():
            dg_ref[...] = jnp.zeros_like(dg_ref)
            loss_ref[...] = jnp.zeros_like(loss_ref)

        rf_tile = a_ref[...]
        acc[...] += _dot_nn(rf_tile * rf_tile, w_ref[...])

        @pl.when(kk == nk - 1)
        def _():
            def tail(rows):
                fhat, r = _rms_hat(acc[rows, :])
                g = g_ref[...]
                e = x2_ref[rows, :] + fhat * g - tg_ref[rows, :]
                loss_ref[...] += 0.5 * jnp.sum(jnp.mean(e * e, axis=-1, keepdims=True))
                dy = e * (1.0 / d)
                dy_ref[rows, :] = dy.astype(BF16)
                df, dg = _rms_bwd(dy, fhat, r, g)
                df_ref[rows, :] = df.astype(BF16)
                dg_ref[...] += dg

            _row_chunks(tt, tail)

    row = pl.BlockSpec((tt, d), lambda i, kk: (i, 0))
    vec = pl.BlockSpec((1, d), lambda i, kk: (0, 0))
    return _call(
        body, name="ff2_loss", grid=(t // tt, nk),
        in_specs=[
            pl.BlockSpec((tt, tk), lambda i, kk: (i, kk)),
            pl.BlockSpec((tk, d), lambda i, kk: (kk, 0)),
            row, vec, row,
        ],
        out_specs=[row, row, vec, pl.BlockSpec((1, 128), lambda i, kk: (0, 0))],
        out_shape=[
            jax.ShapeDtypeStruct((t, d), BF16),
            jax.ShapeDtypeStruct((t, d), BF16),
            jax.ShapeDtypeStruct((1, d), F32),
            jax.ShapeDtypeStruct((1, 128), F32),
        ],
        scratch_shapes=[pltpu.VMEM((tt, d), F32)],
        args=(rf, w_ff2, x2, g4, target))[0]


def _ff2_bwd(df, w_ff2, rf, carry=None):
    t, d = df.shape
    n = w_ff2.shape[0]
    tt, tn = _tile(t, 2048), _tile(n, 512)

    def body(df_ref, w_ref, rf_ref, out_ref):
        d_act = _dot_nt(df_ref[...], w_ref[...])
        out_ref[...] = (d_act * (2.0 * rf_ref[...].astype(F32))).astype(BF16)

    blk = pl.BlockSpec((tt, tn), lambda i, j: (i, j))
    return _call(
        body, name="ff2_bwd", grid=(t // tt, n // tn),
        in_specs=[pl.BlockSpec((tt, d), lambda i, j: (i, 0)), pl.BlockSpec((tn, d), lambda i, j: (j, 0)), blk],
        out_specs=[blk],
        out_shape=[jax.ShapeDtypeStruct((t, n), BF16)],
        args=(df, w_ff2, rf), carry=carry)


def _wgrad(a, b, name, prev=None, row_off=0, rows=None, carry=None, square_a=False):
    t, m = a.shape
    n = b.shape[1]
    rows = m if rows is None else rows
    tm, tk = _tile(m, 512), _tile(t, 2048)
    nk = t // tk
    assert row_off % tm == 0
    off = row_off // tm

    def body(*refs):
        a_ref, b_ref = refs[0], refs[1]
        o32_ref, o16_ref, acc = refs[-3], refs[-2], refs[-1]
        kk = pl.program_id(1)

        @pl.when(kk == 0)
        def _():
            acc[...] = jnp.zeros_like(acc)

        a_tile = a_ref[...]
        acc[...] += _dot_tn(a_tile * a_tile if square_a else a_tile, b_ref[...])

        @pl.when(kk == nk - 1)
        def _():
            o32_ref[...] = acc[...]
            o16_ref[...] = acc[...].astype(BF16)

    in_specs = [pl.BlockSpec((tk, tm), lambda i, kk: (kk, i)), pl.BlockSpec((tk, n), lambda i, kk: (kk, 0))]
    args = [a, b]
    aliases = {}
    if prev is not None:
        in_specs += [ANY, ANY]
        args += list(prev)
        aliases = {2: 0, 3: 1}
    out = pl.BlockSpec((tm, n), lambda i, kk: (off + i, 0))
    return _call(
        body, name=name, grid=(m // tm, nk),
        in_specs=in_specs, out_specs=[out, out],
        out_shape=[jax.ShapeDtypeStruct((rows, n), F32), jax.ShapeDtypeStruct((rows, n), BF16)],
        scratch_shapes=[pltpu.VMEM((tm, n), F32)],
        aliases=aliases, args=args, carry=carry)


def _wgrad_parts(parts, b, name, carry=None):
    t, n = b.shape
    tm = 512
    bounds = []
    lo = 0
    for part in parts:
        assert part.shape[0] == t and part.shape[1] % tm == 0
        bounds.append((lo, lo + part.shape[1] // tm))
        lo += part.shape[1] // tm
    nm = lo
    np_ = len(parts)

    def body(*refs):
        p_refs, b_ref, o32_ref, o16_ref = refs[:np_], refs[np_], refs[np_ + 1], refs[np_ + 2]
        i = pl.program_id(0)
        for (lo_p, hi_p), p_ref in zip(bounds, p_refs):
            @pl.when((i >= lo_p) & (i < hi_p))
            def _(p_ref=p_ref):
                res = _dot_tn(p_ref[...], b_ref[...])
                o32_ref[...] = res
                o16_ref[...] = res.astype(BF16)

    def part_spec(lo_p, hi_p):
        return pl.BlockSpec((t, tm), lambda i: (0, jnp.clip(i - lo_p, 0, hi_p - lo_p - 1)))

    out = pl.BlockSpec((tm, n), lambda i: (i, 0))
    return _call(
        body, name=name, grid=(nm,),
        in_specs=[part_spec(lo_p, hi_p) for lo_p, hi_p in bounds] + [pl.BlockSpec((t, n), lambda i: (0, 0))],
        out_specs=[out, out],
        out_shape=[jax.ShapeDtypeStruct((nm * tm, n), F32), jax.ShapeDtypeStruct((nm * tm, n), BF16)],
        args=(*parts, b), carry=carry)


def _ff1_bwd_norms(d_f1, w_ff1t, dy, x2, g3, m, g2, carry=None):
    t, k = d_f1.shape
    d = x2.shape[1]
    tt, tk = _tile(t, 1024), _tile(k, 512)
    nk = k // tk

    def body(a_ref, w_ref, dy_ref, x2_ref, g3_ref, m_ref, g2_ref, dx2_ref, dm_ref, dg3_ref, dg2_ref, acc):
        i, kk = pl.program_id(0), pl.program_id(1)

        @pl.when(kk == 0)
        def _():
            acc[...] = jnp.zeros_like(acc)

        @pl.when((i == 0) & (kk == 0))
        def _():
            dg3_ref[...] = jnp.zeros_like(dg3_ref)
            dg2_ref[...] = jnp.zeros_like(dg2_ref)

        acc[...] += _dot_nn(a_ref[...], w_ref[...])

        @pl.when(kk == nk - 1)
        def _():
            def tail(rows):
                xhat, r3 = _rms_hat(x2_ref[rows, :])
                dx, dg3 = _rms_bwd(acc[rows, :], xhat, r3, g3_ref[...])
                dx2 = dy_ref[rows, :].astype(F32) + dx
                dx2_ref[rows, :] = dx2
                dg3_ref[...] += dg3
                mhat, r2 = _rms_hat(m_ref[rows, :])
                dm, dg2 = _rms_bwd(dx2, mhat, r2, g2_ref[...])
                dm_ref[rows, :] = dm.astype(BF16)
                dg2_ref[...] += dg2

            _row_chunks(tt, tail)

    row = pl.BlockSpec((tt, d), lambda i, kk: (i, 0))
    vec = pl.BlockSpec((1, d), lambda i, kk: (0, 0))
    return _call(
        body, name="ff1_bwd_norms", grid=(t // tt, nk),
        in_specs=[
            pl.BlockSpec((tt, tk), lambda i, kk: (i, kk)),
            pl.BlockSpec((tk, d), lambda i, kk: (kk, 0)),
            row, row, vec, row, vec,
        ],
        out_specs=[row, row, vec, vec],
        out_shape=[
            jax.ShapeDtypeStruct((t, d), F32),
            jax.ShapeDtypeStruct((t, d), BF16),
            jax.ShapeDtypeStruct((1, d), F32),
            jax.ShapeDtypeStruct((1, d), F32),
        ],
        scratch_shapes=[pltpu.VMEM((tt, d), F32)],
        args=(d_f1, w_ff1t, dy, x2, g3, m, g2), carry=carry)


def _wo_bwd_mix(dm, w_o, br_a, br_b, proj, b_gate, ga_block, gb_block, carry=None):
    t, d = dm.shape
    tt, tn = _tile(t, 1024), 512
    nj = d // tn

    def body(dm_ref, w_ref, bra_ref, brb_ref, ga_ref, gb_ref, ba_ref, bb_ref,
             dbra_ref, dbrb_ref, dga_ref, dgb_ref, dba_ref, dbb_ref):
        i = pl.program_id(1)

        @pl.when(i == 0)
        def _():
            dba_ref[...] = jnp.zeros_like(dba_ref)
            dbb_ref[...] = jnp.zeros_like(dbb_ref)

        d_mix = _dot_nt(dm_ref[...], w_ref[...])
        ga = _sig(ga_ref[...].astype(F32) + ba_ref[...])
        gb = _sig(gb_ref[...].astype(F32) + bb_ref[...])
        dbra_ref[...] = (d_mix * ga).astype(BF16)
        dbrb_ref[...] = (d_mix * gb).astype(BF16)
        dga = d_mix * bra_ref[...].astype(F32) * (ga * (1.0 - ga))
        dgb = d_mix * brb_ref[...].astype(F32) * (gb * (1.0 - gb))
        dga_ref[...] = dga.astype(BF16)
        dgb_ref[...] = dgb.astype(BF16)
        dba_ref[...] += jnp.sum(dga, axis=0, keepdims=True)
        dbb_ref[...] += jnp.sum(dgb, axis=0, keepdims=True)

    blk = pl.BlockSpec((tt, tn), lambda j, i: (i, j))
    vec = pl.BlockSpec((1, tn), lambda j, i: (0, j))
    return _call(
        body, name="wo_bwd_mix", grid=(nj, t // tt),
        in_specs=[
            pl.BlockSpec((tt, d), lambda j, i: (i, 0)),
            pl.BlockSpec((tn, d), lambda j, i: (j, 0)),
            blk, blk,
            pl.BlockSpec((tt, tn), lambda j, i: (i, ga_block + j)),
            pl.BlockSpec((tt, tn), lambda j, i: (i, gb_block + j)),
            vec,
            pl.BlockSpec((1, tn), lambda j, i: (0, nj + j)),
        ],
        out_specs=[blk, blk, blk, blk, vec, vec],
        out_shape=[jax.ShapeDtypeStruct((t, d), BF16)] * 4 + [jax.ShapeDtypeStruct((1, d), F32)] * 2,
        args=(dm, w_o, br_a, br_b, proj, proj, b_gate, b_gate), carry=carry)


def _lru_up_bwd(d_br_a, w_lru_up, proj, h, g_block, carry=None):
    t, d = d_br_a.shape
    tt, tn = _tile(t, 1024), 512

    def body(a_ref, w_ref, g_ref, h_ref, dh_ref, dg_ref):
        d_y = _dot_nt(a_ref[...], w_ref[...])
        gel, gel_grad = _gelu_and_grad(g_ref[...].astype(F32))
        dh_ref[...] = d_y * gel
        dg_ref[...] = (d_y * h_ref[...] * gel_grad).astype(BF16)

    blk = pl.BlockSpec((tt, tn), lambda i, j: (i, j))
    return _call(
        body, name="lru_up_bwd", grid=(t // tt, d // tn),
        in_specs=[
            pl.BlockSpec((tt, d), lambda i, j: (i, 0)),
            pl.BlockSpec((tn, d), lambda i, j: (j, 0)),
            pl.BlockSpec((tt, tn), lambda i, j: (i, g_block + j)),
            blk,
        ],
        out_specs=[blk, blk],
        out_shape=[jax.ShapeDtypeStruct((t, d), F32), jax.ShapeDtypeStruct((t, d), BF16)],
        args=(d_br_a, w_lru_up, proj, h), carry=carry)


def _pool_up_bwd(d_br_b, w_pool_upt):
    t, d = d_br_b.shape
    dp = w_pool_upt.shape[1]
    tt = _tile(t, 2048)

    def body(a_ref, w_ref, out_ref):
        out_ref[...] = _dot_nn(a_ref[...], w_ref[...])

    return _call(
        body, name="pool_up_bwd", grid=(t // tt,),
        in_specs=[pl.BlockSpec((tt, d), lambda i: (i, 0)), pl.BlockSpec((d, dp), lambda i: (0, 0))],
        out_specs=[pl.BlockSpec((tt, dp), lambda i: (i, 0))],
        out_shape=[jax.ShapeDtypeStruct((t, dp), F32)],
        args=(d_br_b, w_pool_upt))[0][0]


def _lru_bwd(dh, xc, h, proj, conv_w, w_a, b_a, w_x, b_x, lam, carry=None):
    t, dr = dh.shape
    cb = LRU_CB
    hd = LRU_HEAD_DIM
    per = cb // hd
    tc = _tile(t, 256)
    ncb, ntc = dr // cb, t // tc

    def body(dh_ref, xc_ref, h_ref, hp_ref, xp_ref, cw_ref, wa_ref, ba_ref, wx_ref, bx_ref, lam_ref,
             dxp_ref, dwa_ref, dba_ref, dwx_ref, dbx_ref, dlam_ref, dcw_ref, dcb_ref,
             nextd_s, anext_s, gnext_s, tmp_s, wa_s, wx_s):
        c = pl.program_id(1)
        rc = ntc - 1 - c

        @pl.when(c == 0)
        def _():
            nextd_s[...] = jnp.zeros_like(nextd_s)
            anext_s[...] = jnp.zeros_like(anext_s)
            gnext_s[...] = jnp.zeros_like(gnext_s)
            for ref in (dwa_ref, dba_ref, dwx_ref, dbx_ref, dlam_ref, dcw_ref, dcb_ref):
                ref[...] = jnp.zeros_like(ref)
            _fill_block_diag(wa_ref, wa_s)
            _fill_block_diag(wx_ref, wx_s)

        xc = xc_ref[...]
        wa, wx, lam = wa_s[...], wx_s[...], lam_ref[...]
        xcb, r, i, sp, log_a, a, mult = _lru_gates(xc, wa, ba_ref[...], wx, bx_ref[...], lam)
        row = lax.broadcasted_iota(jnp.int32, xc.shape, 0)
        h = h_ref[...]
        hp = jnp.where(rc == 0, 0.0, hp_ref[...])
        hprev = jnp.where(row >= 1, pltpu.roll(h, 1, 0), pltpu.roll(hp, 1, 0))

        def up(v, nv, j):
            return jnp.where(row < tc - j, pltpu.roll(v, tc - j, 0), nv)

        av, bv = _scan_rows(up(a, anext_s[...], 1), dh_ref[...], reverse=True)
        gt = av * gnext_s[...] + bv
        tmp_s[...] = gt
        gnext_s[...] = tmp_s[0:1, :]
        tmp_s[...] = a
        anext_s[...] = tmp_s[0:1, :]

        da = gt * hprev
        ixc = i * xc
        d_mult = gt * ixc
        d_i = gt * mult * xc
        d_xc = gt * mult * i
        d_log_a = da * a - d_mult * (a * a) / mult
        d_pre_r = (d_log_a * ((-LRU_C) * sp)) * (r * (1.0 - r))
        d_pre_i = d_i * (i * (1.0 - i))
        d_sp = jnp.sum(d_log_a * ((-LRU_C) * r), axis=0, keepdims=True)
        dlam_ref[...] += d_sp * (-1.0 / (1.0 + jnp.exp(lam)))
        dpr = d_pre_r.astype(BF16)
        dpi = d_pre_i.astype(BF16)
        dba_ref[...] += jnp.sum(d_pre_r, axis=0, keepdims=True)
        dbx_ref[...] += jnp.sum(d_pre_i, axis=0, keepdims=True)
        pa = _dot_tn(xcb, dpr)
        px = _dot_tn(xcb, dpi)
        for k in range(per):
            dwa_ref[k] += pa[k * hd:(k + 1) * hd, k * hd:(k + 1) * hd]
            dwx_ref[k] += px[k * hd:(k + 1) * hd, k * hd:(k + 1) * hd]
        d_xc = d_xc + _dot_nt(dpr, wa) + _dot_nt(dpi, wx)

        nxt = nextd_s[...]
        xp = xp_ref[...].astype(F32)
        dxp = cw_ref[3:4, :] * d_xc
        dcw_ref[3:4, :] += jnp.sum(xp * d_xc, axis=0, keepdims=True)
        for j in (1, 2, 3):
            uj = up(d_xc, pltpu.roll(nxt, tc - j, 0), j)
            dxp = dxp + cw_ref[3 - j:4 - j, :] * uj
            dcw_ref[3 - j:4 - j, :] += jnp.sum(xp * uj, axis=0, keepdims=True)
        dcb_ref[...] += jnp.sum(d_xc, axis=0, keepdims=True)
        nextd_s[...] = d_xc
        dxp_ref[...] = dxp.astype(BF16)

    vec = pl.BlockSpec((1, cb), lambda j, c: (0, j))
    blk = pl.BlockSpec((tc, cb), lambda j, c: (ntc - 1 - c, j))
    mat = pl.BlockSpec((per, hd, hd), lambda j, c: (j, 0, 0))
    cwb = pl.BlockSpec((4, cb), lambda j, c: (0, j))
    return _call(
        body, name="lru_bwd", grid=(ncb, ntc),
        in_specs=[
            blk, blk, blk,
            pl.BlockSpec((tc, cb), lambda j, c: (jnp.maximum(ntc - 2 - c, 0), j)),
            blk, cwb, mat, vec, mat, vec, vec,
        ],
        out_specs=[blk, mat, vec, mat, vec, vec, cwb, vec],
        out_shape=[
            jax.ShapeDtypeStruct((t, dr), BF16),
            jax.ShapeDtypeStruct(w_a.shape, F32),
            jax.ShapeDtypeStruct((1, dr), F32),
            jax.ShapeDtypeStruct(w_x.shape, F32),
            jax.ShapeDtypeStruct((1, dr), F32),
            jax.ShapeDtypeStruct((1, dr), F32),
            jax.ShapeDtypeStruct((4, dr), F32),
            jax.ShapeDtypeStruct((1, dr), F32),
        ],
        scratch_shapes=[
            pltpu.VMEM((tc, cb), F32),
            pltpu.VMEM((1, cb), F32),
            pltpu.VMEM((1, cb), F32),
            pltpu.VMEM((tc, cb), F32),
            pltpu.VMEM((cb, cb), BF16),
            pltpu.VMEM((cb, cb), BF16),
        ],
        args=(dh, xc, h, h, proj, conv_w, w_a, b_a, w_x, b_x, lam), carry=carry)


def _pool_bwd(d_y_pool, p, pool_w, pool_scale):
    t, dp = d_y_pool.shape
    tc = _tile(t, 256)
    ntc = t // tc
    ng = len(POOL_WINDOWS)

    def body(dy_ref, p_ref, w_ref, sc_ref, dx_ref, dw_ref, dsc_ref, nz, n2, n4, n8, dp_s):
        c = pl.program_id(0)
        rc = ntc - 1 - c

        @pl.when(c == 0)
        def _():
            for s in (nz, n2, n4, n8):
                s[...] = jnp.zeros_like(s)
            dw_ref[...] = jnp.zeros_like(dw_ref)
            dsc_ref[...] = jnp.zeros_like(dsc_ref)

        for g in range(ng):
            sl = slice(g * POOL_GROUP_DIM, (g + 1) * POOL_GROUP_DIM)
            pg = p_ref[:, sl]
            dyg = dy_ref[:, sl]
            wg = w_ref[g].astype(BF16)
            q = _dot_nn(pg, wg)
            dsc_ref[:, sl] += jnp.sum(dyg * q, axis=0, keepdims=True)
            dpw = (dyg * sc_ref[:, sl]).astype(BF16)
            dw_ref[g] += _dot_tn(pg, dpw)
            dp_s[:, sl] = _dot_nt(dpw, wg)

        dpv = dp_s[...]
        row = lax.broadcasted_iota(jnp.int32, dpv.shape, 0)
        col = lax.broadcasted_iota(jnp.int32, dpv.shape, 1)
        win = _pool_select(col, POOL_WINDOWS)
        cnt = jnp.minimum(rc * tc + row + 1, win).astype(F32)
        z = dpv / cnt

        def up(v, nv, j):
            return jnp.where(row < tc - j, pltpu.roll(v, tc - j, 0), pltpu.roll(nv[...], tc - j, 0))

        u2 = z + up(z, nz, 1)
        u4 = u2 + up(u2, n2, 2)
        u8 = u4 + up(u4, n4, 4)
        u16 = u8 + up(u8, n8, 8)
        nz[...] = z
        n2[...] = u2
        n4[...] = u4
        n8[...] = u8
        dx_ref[...] = (_pool_select(col, (u2, u4, u8, u16)) - dpv).astype(BF16)

    blk = pl.BlockSpec((tc, dp), lambda c: (ntc - 1 - c, 0))
    full_w = pl.BlockSpec(pool_w.shape, lambda c: (0, 0, 0))
    vec = pl.BlockSpec((1, dp), lambda c: (0, 0))
    return _call(
        body, name="pool_bwd", grid=(ntc,),
        in_specs=[blk, blk, full_w, vec],
        out_specs=[blk, full_w, vec],
        out_shape=[
            jax.ShapeDtypeStruct((t, dp), BF16),
            jax.ShapeDtypeStruct(pool_w.shape, F32),
            jax.ShapeDtypeStruct((1, dp), F32),
        ],
        scratch_shapes=[pltpu.VMEM((tc, dp), F32)] * 5,
        args=(d_y_pool, p, pool_w, pool_scale))[0]


def _win_bwd_norm(parts, w_int, dx2, x, g1, carry=None):
    t, d = x.shape
    tk = 512
    tt = _tile(t, 1024)
    bounds = []
    k0 = 0
    for part in parts:
        assert part.shape[1] % tk == 0
        bounds.append((k0, k0 + part.shape[1] // tk))
        k0 += part.shape[1] // tk
    nk = k0
    assert nk * tk == w_int.shape[0]
    np_ = len(parts)

    def body(*refs):
        p_refs = refs[:np_]
        w_ref, dx2_ref, x_ref, g_ref, gx_ref, dg_ref, acc = refs[np_:]
        i, kk = pl.program_id(0), pl.program_id(1)

        @pl.when(kk == 0)
        def _():
            acc[...] = jnp.zeros_like(acc)

        @pl.when((i == 0) & (kk == 0))
        def _():
            dg_ref[...] = jnp.zeros_like(dg_ref)

        for (lo, hi), p_ref in zip(bounds, p_refs):
            @pl.when((kk >= lo) & (kk < hi))
            def _(p_ref=p_ref):
                acc[...] += _dot_nn(p_ref[...], w_ref[...])

        @pl.when(kk == nk - 1)
        def _():
            def tail(rows):
                xhat, r = _rms_hat(x_ref[rows, :])
                dx, dg = _rms_bwd(acc[rows, :], xhat, r, g_ref[...])
                gx_ref[rows, :] = dx2_ref[rows, :] + dx
                dg_ref[...] += dg

            _row_chunks(tt, tail)

    def part_spec(lo, hi):
        return pl.BlockSpec((tt, tk), lambda i, kk: (i, jnp.clip(kk - lo, 0, hi - lo - 1)))

    row = pl.BlockSpec((tt, d), lambda i, kk: (i, 0))
    vec = pl.BlockSpec((1, d), lambda i, kk: (0, 0))
    return _call(
        body, name="win_bwd_norm", grid=(t // tt, nk),
        in_specs=[part_spec(lo, hi) for lo, hi in bounds]
        + [pl.BlockSpec((tk, d), lambda i, kk: (kk, 0)), row, row, vec],
        out_specs=[row, vec],
        out_shape=[jax.ShapeDtypeStruct((t, d), F32), jax.ShapeDtypeStruct((1, d), F32)],
        scratch_shapes=[pltpu.VMEM((tt, d), F32)],
        args=(*parts, w_int, dx2, x, g1), carry=carry)


def _adam_math(w, g, m, v):
    m = ADAM_B1 * m + (1.0 - ADAM_B1) * g
    v = ADAM_B2 * v + (1.0 - ADAM_B2) * (g * g)
    m_hat = m / (1.0 - ADAM_B1 ** ADAM_STEP)
    v_hat = v / (1.0 - ADAM_B2 ** ADAM_STEP)
    delta = -ADAM_LR * (m_hat / (jnp.sqrt(v_hat) + ADAM_EPS) + ADAM_WD * w)
    return delta, m, v


def _adamw_big(ws, gs, ms, vs, carry=None):
    n = len(ws)
    nb = 8

    def body(*refs):
        for a in range(n):
            w_ref, g_ref, m_ref, v_ref = refs[4 * a:4 * a + 4]
            d_ref, nm_ref, nv_ref = refs[4 * n + 3 * a:4 * n + 3 * a + 3]
            dl, m, v = _adam_math(w_ref[...], g_ref[...], m_ref[...], v_ref[...])
            d_ref[...] = dl
            nm_ref[...] = m
            nv_ref[...] = v

    in_specs, out_specs, out_shape, args = [], [], [], []
    for w, g, m, v in zip(ws, gs, ms, vs):
        rows, cols = w.shape
        blk = pl.BlockSpec((rows // nb, cols), lambda i: (i, 0))
        in_specs += [blk] * 4
        args += [w, g, m, v]
        out_specs += [blk] * 3
        out_shape += [jax.ShapeDtypeStruct(w.shape, F32)] * 3
    outs, got = _call(body, name="adamw_big", grid=(nb,), in_specs=in_specs, out_specs=out_specs,
                      out_shape=out_shape, args=args, carry=carry)
    return [tuple(outs[3 * a:3 * a + 3]) for a in range(n)], got


SMALL_ORDER = ("norm_mix_pre", "norm_mix_post", "norm_mlp_pre", "norm_mlp_post", "b_gate", "conv_w", "conv_b",
               "lru_w_a", "lru_b_a", "lru_w_x", "lru_b_x", "lru_lambda", "pool_w", "pool_scale")
VEC_ROW = dict(norm_mix_pre=0, norm_mix_post=1, norm_mlp_pre=2, norm_mlp_post=3, conv_b=6, lru_b_a=7,
               lru_b_x=8, lru_lambda=9)
ROW_B_GATE, ROW_POOL_SCALE, ROW_CONV_W, ROW_LOSS, N_VEC_ROWS = 4, 10, 11, 15, 16


def _adamw_small(vec_parts, g_pool, g_wa, g_wx, me, params):
    d = vec_parts.shape[2]
    names = SMALL_ORDER
    n = len(names)
    cw_cols = params["conv_w"][0].shape[2]

    def body(me_ref, vec_ref, vecc_ref, gp_ref, gwa_ref, gwx_ref, *refs):
        wmv = refs[:3 * n]
        loss_ref = refs[3 * n]
        outs = refs[3 * n + 1:3 * n + 1 + 4 * n]
        vs, vsc = refs[3 * n + 1 + 4 * n:]
        acc, accc = vec_ref[0], vecc_ref[0]
        for k in range(1, N_DEV):
            acc = acc + vec_ref[k]
            accc = accc + vecc_ref[k]
        vs[...] = acc
        vsc[...] = accc
        loss_ref[...] = vs[ROW_LOSS:ROW_LOSS + 1, 0:128]

        def upd(a, g, idx):
            w_ref, m_ref, v_ref = wmv[3 * a:3 * a + 3]
            g_ref, d_ref, nm_ref, nv_ref = outs[4 * a:4 * a + 4]
            dl, m, v = _adam_math(w_ref[idx], g, m_ref[idx], v_ref[idx])
            g_ref[idx] = g
            d_ref[idx] = dl
            nm_ref[idx] = m
            nv_ref[idx] = v

        for a, name in enumerate(names):
            if name in VEC_ROW:
                r = VEC_ROW[name]
                upd(a, vs[r:r + 1, :], (slice(None), slice(None)))
            elif name == "b_gate":
                for half in range(2):
                    r = ROW_B_GATE + half
                    upd(a, vs[r:r + 1, :], (slice(None), slice(half * d, (half + 1) * d)))
            elif name == "pool_scale":
                width = params[name][0].shape[1]
                upd(a, vs[ROW_POOL_SCALE:ROW_POOL_SCALE + 1, 0:width], (slice(None), slice(None)))
            elif name == "conv_w":
                upd(a, vsc[ROW_CONV_W:ROW_CONV_W + 4, :], (0,))
            elif name == "pool_w":
                upd(a, gp_ref[...], (Ellipsis,))
            elif name == "lru_w_a":
                upd(a, gwa_ref[...], (Ellipsis,))
            elif name == "lru_w_x":
                upd(a, gwx_ref[...], (Ellipsis,))
            else:
                raise ValueError(name)

    def whole(shape):
        nd = len(shape)
        return pl.BlockSpec(tuple(shape), lambda i, me_ref: (0,) * nd)

    in_specs = [
        whole(vec_parts.shape),
        pl.BlockSpec((N_DEV, N_VEC_ROWS, cw_cols), lambda i, me_ref: (0, 0, me_ref[0])),
        whole(g_pool.shape), whole(g_wa.shape), whole(g_wx.shape),
    ]
    args = [vec_parts, vec_parts, g_pool, g_wa, g_wx]
    out_specs = [whole((1, 128))]
    out_shape = [jax.ShapeDtypeStruct((1, 128), F32)]
    for name in names:
        for arr in params[name]:
            in_specs.append(whole(arr.shape))
            args.append(arr)
        shp = params[name][0].shape
        out_specs += [whole(shp)] * 4
        out_shape += [jax.ShapeDtypeStruct(shp, F32)] * 4
    grid_spec = pltpu.PrefetchScalarGridSpec(
        num_scalar_prefetch=1, grid=(1,), in_specs=in_specs, out_specs=out_specs,
        scratch_shapes=[pltpu.VMEM((N_VEC_ROWS, d), F32), pltpu.VMEM((N_VEC_ROWS, cw_cols), F32)])
    outs = pl.pallas_call(
        body, name="adamw_small", grid_spec=grid_spec, out_shape=out_shape,
        compiler_params=pltpu.CompilerParams(
            dimension_semantics=("arbitrary",), vmem_limit_bytes=V7X_VMEM_LIMIT_BYTES),
    )(me, *_in_hbm(args))
    return outs[0], {name: tuple(outs[1 + 4 * a:5 + 4 * a]) for a, name in enumerate(names)}


def _rs_sum(fulls, recvs, shard_ids, slot_ids, name):
    n = len(fulls)

    def body(sh_ref, sl_ref, *refs):
        s = pl.program_id(0)
        for a in range(n):
            full_ref, recv_ref = refs[2 * a], refs[2 * a + 1]
            own_ref, send_ref = refs[2 * n + 2 * a], refs[2 * n + 2 * a + 1]
            v = full_ref[...] + recv_ref[...].astype(F32)

            @pl.when(s == 0)
            def _(own_ref=own_ref, v=v):
                own_ref[...] = v

            @pl.when(s > 0)
            def _(send_ref=send_ref, v=v):
                send_ref[...] = v.astype(send_ref.dtype)

    in_specs, out_specs, out_shape, args = [], [], [], []
    for full, recv in zip(fulls, recvs):
        r, rest = recv.shape[1], tuple(recv.shape[2:])
        zeros = (0,) * len(rest)
        in_specs += [
            pl.BlockSpec((r,) + rest, lambda s, sh, sl, zeros=zeros: (sh[s],) + zeros),
            pl.BlockSpec((None, r) + rest, lambda s, sh, sl, zeros=zeros: (sl[s], 0) + zeros),
        ]
        out_specs += [
            pl.BlockSpec((None, r) + rest, lambda s, sh, sl, zeros=zeros: (0, 0) + zeros),
            pl.BlockSpec((None, r) + rest, lambda s, sh, sl, zeros=zeros: (jnp.maximum(s - 1, 0), 0) + zeros),
        ]
        out_shape += [jax.ShapeDtypeStruct((1, r) + rest, F32), jax.ShapeDtypeStruct((3, r) + rest, recv.dtype)]
        args += [full, recv]
    grid_spec = pltpu.PrefetchScalarGridSpec(
        num_scalar_prefetch=2, grid=(4,), in_specs=in_specs, out_specs=out_specs)
    outs = pl.pallas_call(
        body,
        name=name,
        grid_spec=grid_spec,
        out_shape=out_shape,
        compiler_params=pltpu.CompilerParams(
            dimension_semantics=("arbitrary",), vmem_limit_bytes=V7X_VMEM_LIMIT_BYTES),
    )(shard_ids, slot_ids, *_in_hbm(args))
    return [(outs[2 * a], outs[2 * a + 1]) for a in range(n)]


def _finals(pairs, name, carry=None):
    nb = 4
    n = len(pairs)

    def body(*refs):
        for a in range(n):
            own_ref, recv_ref = refs[2 * a], refs[2 * a + 1]
            acc = own_ref[...]
            for k in range(3):
                acc = acc + recv_ref[k].astype(F32)
            refs[2 * n + a][...] = acc

    in_specs, out_specs, out_shape, args = [], [], [], []
    for own, recv in pairs:
        _, rows, cols = own.shape
        in_specs += [pl.BlockSpec((None, rows // nb, cols), lambda i: (0, i, 0)),
                     pl.BlockSpec((3, rows // nb, cols), lambda i: (0, i, 0))]
        args += [own, recv]
        out_specs.append(pl.BlockSpec((rows // nb, cols), lambda i: (i, 0)))
        out_shape.append(jax.ShapeDtypeStruct((rows, cols), F32))
    return _call(body, name=name, grid=(nb,), in_specs=in_specs, out_specs=out_specs,
                 out_shape=out_shape, args=args, carry=carry)


def _rs_sums(fulls_f32, recv1, tag):
    x, y, c = _place()
    qs = jnp.stack([2 * x + y, 2 * (1 - x) + y, 2 * x + (1 - y), 2 * (1 - x) + (1 - y)]).astype(jnp.int32)
    shard_ids = 2 * qs + c
    return _rs_sum(fulls_f32, recv1, shard_ids, qs, "rs_sum_" + tag)


def _rs_level1(fulls_f32, fulls_send, tag):
    recv1 = _run_plan(_rs_sibling_plan(fulls_send), "rs_sibling_" + tag)
    return _rs_sums(fulls_f32, recv1, tag)


def _rows(g):
    return g.reshape(g.shape[0] * g.shape[1], g.shape[2])


def kernel(x, norm_mix_pre, norm_mix_post, norm_mlp_pre, norm_mlp_post, w_in, b_gate, conv_w, conv_b, lru_w_a, lru_b_a, lru_w_x, lru_b_x, lru_lambda, pool_w, pool_scale, w_lru_up, w_pool_up, w_o, w_ff1, w_ff2, loss_target, m_norm_mix_pre, m_norm_mix_post, m_norm_mlp_pre, m_norm_mlp_post, m_w_in, m_b_gate, m_conv_w, m_conv_b, m_lru_w_a, m_lru_b_a, m_lru_w_x, m_lru_b_x, m_lru_lambda, m_pool_w, m_pool_scale, m_w_lru_up, m_w_pool_up, m_w_o, m_w_ff1, m_w_ff2, v_norm_mix_pre, v_norm_mix_post, v_norm_mlp_pre, v_norm_mlp_post, v_w_in, v_b_gate, v_conv_w, v_conv_b, v_lru_w_a, v_lru_b_a, v_lru_w_x, v_lru_b_x, v_lru_lambda, v_pool_w, v_pool_scale, v_w_lru_up, v_w_pool_up, v_w_o, v_w_ff1, v_w_ff2):
    t, d = x.shape[1], x.shape[2]
    d_rnn = conv_b.shape[1]
    d_pool = pool_scale.shape[1]
    per = LRU_CB // LRU_HEAD_DIM
    xi, yi, ci = _place()
    me = 4 * xi + 2 * yi + ci

    x2d = x[0]
    tgt = loss_target[0]

    s_in = w_in[0].T.astype(BF16)
    s_lu = w_lru_up[0].astype(BF16)
    s_pu = w_pool_up[0].T.astype(BF16)
    s_o = w_o[0].astype(BF16)
    s_f1 = w_ff1[0].T.astype(BF16)
    s_f2 = w_ff2[0].astype(BF16)
    s_cw = jnp.pad(conv_w[0], ((0, 4), (0, 0)))

    g_in, g_cw = _run_plan(_ag_plan([s_in, s_cw]), "ag_w_in")
    w_int = _rows(g_in)
    conv_w_full = jnp.transpose(g_cw[:, :4, :], (1, 0, 2)).reshape(4, d_rnn)

    wa_bd, wx_bd = lru_w_a[0], lru_w_x[0]
    pw = pool_w[0]
    pw_bf = pw.astype(BF16)

    pool_block = (2 * d_rnn) // d_pool
    ga_block = (2 * d_rnn + d_pool) // 512
    gb_block = ga_block + d // 512
    g_block = d_rnn // 512

    r_f1, r_f2 = s_f1.shape[0], s_f2.shape[0]
    f1_cut = r_f1 // 4
    f2_cut = (3 * r_f2) // 8
    plan = _join([_ag_plan([s_lu, s_pu, s_o]), _ag_plan([s_f1], pieces=[(0, f1_cut)])])
    (proj, h1), got = _norm_proj(x2d, norm_mix_pre, w_int, carry=plan)
    (g_lu, g_pu, g_o), (g_f1,) = plan.split(got)
    w_lu, w_put, w_og = _rows(g_lu), _rows(g_pu), _rows(g_o)
    (y_lru, h, xc), (g_f1,) = _lru_fwd(
        proj, conv_w_full, conv_b, wa_bd, lru_b_a, wx_bd, lru_b_x, lru_lambda,
        carry=_ag_plan([s_f1], pieces=[(f1_cut, r_f1 - f1_cut)], bufs=[g_f1]))
    w_f1t = _rows(g_f1)
    y_pool, p = _pool_fwd(proj, pw_bf, pool_scale, pool_block)
    (br_a, br_b, mix), (g_f2,) = _branch_mix(
        y_lru, y_pool, w_lu, w_put, proj, b_gate, ga_block, gb_block,
        carry=_ag_plan([s_f2], pieces=[(0, f2_cut)]))
    (m, x2, h3), (g_f2,) = _wo_norm(
        mix, w_og, x2d, norm_mix_post, norm_mlp_pre,
        carry=_ag_plan([s_f2], pieces=[(f2_cut, r_f2 // 2 - f2_cut)], bufs=[g_f2]))
    (rf,), (g_f2,) = _ff1(
        h3, w_f1t, carry=_ag_plan([s_f2], pieces=[(r_f2 // 2, r_f2 - r_f2 // 2)], bufs=[g_f2]))
    w_f2 = _rows(g_f2)
    dy, df, dg4, loss_part = _ff2_loss(rf, w_f2, x2, norm_mlp_post, tgt)

    (gw_ff2_32, gw_ff2_16), _ = _wgrad(rf, df, "wgrad_ff2", square_a=True)
    (d_f1,), r1_ff2 = _ff2_bwd(df, w_f2, rf, carry=_rs_sibling_plan([gw_ff2_16]))
    ((own_ff2, send_ff2),) = _rs_sums([gw_ff2_32], r1_ff2, "ff2")
    cut2 = (5 * send_ff2.shape[1]) // 16
    (gw_ff1_32, gw_ff1_16), (r2_ff2,) = _wgrad(
        d_f1, h3, "wgrad_ff1", carry=_rs_chips_plan([send_ff2], pieces=[(0, cut2)]))
    plan = _join([_rs_chips_plan([send_ff2], pieces=[(cut2, send_ff2.shape[1] - cut2)], bufs=[r2_ff2]),
                  _rs_sibling_plan([gw_ff1_16])])
    (dx2, dm, dg3, dg2), got = _ff1_bwd_norms(d_f1, w_f1t, dy, x2, norm_mlp_pre, m, norm_mix_post, carry=plan)
    (r2_ff2,), r1_ff1 = plan.split(got)
    ((own_ff1, send_ff1),) = _rs_sums([gw_ff1_32], r1_ff1, "ff1")
    cut = send_ff1.shape[1] // 4
    (gw_o_32, gw_o_16), _ = _wgrad(mix, dm, "wgrad_o")
    (d_br_a, d_br_b, p_ga, p_gb, dbg_a, dbg_b), (r2_ff1,) = _wo_bwd_mix(
        dm, w_og, br_a, br_b, proj, b_gate, ga_block, gb_block,
        carry=_rs_chips_plan([send_ff1], pieces=[(0, cut)]))
    (gw_lu_32, gw_lu_16), _ = _wgrad(y_lru, d_br_a, "wgrad_lru_up")
    (gw_pu_32, gw_pu_16), _ = _wgrad(d_br_b, y_pool, "wgrad_pool_up")
    (dh, p_g), r1_mid = _lru_up_bwd(
        d_br_a, w_lu, proj, h, g_block,
        carry=_rs_sibling_plan([gw_o_16, gw_lu_16, gw_pu_16.reshape(-1, d)]))
    mid = _rs_sums([gw_o_32, gw_lu_32, gw_pu_32.reshape(-1, d)], r1_mid, "mid")
    d_y_pool = _pool_up_bwd(d_br_b, w_put)
    (p_x, dwa, db_a, dwx, db_x, dlam, dconv_w, dconv_b), (r2_ff1,) = _lru_bwd(
        dh, xc, h, proj, conv_w_full, wa_bd, lru_b_a, wx_bd, lru_b_x, lru_lambda,
        carry=_rs_chips_plan([send_ff1], pieces=[(cut, send_ff1.shape[1] - cut)], bufs=[r2_ff1]))
    p_p, dpool_w, dpool_scale = _pool_bwd(d_y_pool, p, pw, pool_scale)
    parts = [p_x, p_g, p_p, p_ga, p_gb]
    gw_in, r2_mid = _wgrad_parts(parts, h1, "wgrad_in", carry=_rs_chips_plan([s for _, s in mid]))
    tail = _rs_level1([gw_in[0], dpool_w.reshape(N_DEV, -1, POOL_GROUP_DIM), dwa, dwx],
                      [gw_in[1], dpool_w.reshape(N_DEV, -1, POOL_GROUP_DIM), dwa, dwx], "in")
    (grad_x, dg1), r2_tail = _win_bwd_norm(parts, w_int, dx2, x2d, norm_mix_pre,
                                           carry=_rs_chips_plan([s for _, s in tail]))

    def flat2(a):
        return a.reshape(a.shape[0], -1, a.shape[-1])

    fin_small, _ = _finals([
        (flat2(tail[1][0]), flat2(r2_tail[1])), (flat2(tail[2][0]), flat2(r2_tail[2])),
        (flat2(tail[3][0]), flat2(r2_tail[3])),
    ], "rs_finals_small")

    def pad_row(a):
        return jnp.pad(a, ((0, 0), (0, d - a.shape[1])))

    vecs = jnp.concatenate([dg1, dg2, dg3, dg4, dbg_a, dbg_b, dconv_b, db_a, db_x, dlam,
                            pad_row(dpool_scale), dconv_w, pad_row(loss_part)], axis=0)
    assert vecs.shape[0] == N_VEC_ROWS
    fin, (vec_parts, g_pool, g_wa, g_wx) = _finals([
        (tail[0][0], r2_tail[0]), (mid[1][0], r2_mid[1]), (mid[2][0], r2_mid[2]), (mid[0][0], r2_mid[0]),
        (own_ff1, r2_ff1), (own_ff2, r2_ff2),
    ], "rs_finals", carry=_ag_plan([vecs] + fin_small))
    g_w_in = fin[0].T
    g_w_lru_up = fin[1]
    g_w_pool_up = fin[2].reshape(d // N_DEV, d_pool).T
    g_w_o = fin[3]
    g_w_ff1 = fin[4].T
    g_w_ff2 = fin[5]

    big_names = ["w_in", "w_lru_up", "w_pool_up", "w_o", "w_ff1", "w_ff2"]
    big_w = [w_in, w_lru_up, w_pool_up, w_o, w_ff1, w_ff2]
    big_g = [g_w_in, g_w_lru_up, g_w_pool_up, g_w_o, g_w_ff1, g_w_ff2]
    big_m = [m_w_in, m_w_lru_up, m_w_pool_up, m_w_o, m_w_ff1, m_w_ff2]
    big_v = [v_w_in, v_w_lru_up, v_w_pool_up, v_w_o, v_w_ff1, v_w_ff2]
    big_out, _ = _adamw_big([w[0] for w in big_w], big_g, [mm[0] for mm in big_m], [vv[0] for vv in big_v])

    small = dict(
        norm_mix_pre=(norm_mix_pre, m_norm_mix_pre, v_norm_mix_pre),
        norm_mix_post=(norm_mix_post, m_norm_mix_post, v_norm_mix_post),
        norm_mlp_pre=(norm_mlp_pre, m_norm_mlp_pre, v_norm_mlp_pre),
        norm_mlp_post=(norm_mlp_post, m_norm_mlp_post, v_norm_mlp_post),
        b_gate=(b_gate, m_b_gate, v_b_gate), conv_w=(conv_w, m_conv_w, v_conv_w),
        conv_b=(conv_b, m_conv_b, v_conv_b), lru_w_a=(lru_w_a, m_lru_w_a, v_lru_w_a),
        lru_b_a=(lru_b_a, m_lru_b_a, v_lru_b_a), lru_w_x=(lru_w_x, m_lru_w_x, v_lru_w_x),
        lru_b_x=(lru_b_x, m_lru_b_x, v_lru_b_x), lru_lambda=(lru_lambda, m_lru_lambda, v_lru_lambda),
        pool_w=(pool_w, m_pool_w, v_pool_w), pool_scale=(pool_scale, m_pool_scale, v_pool_scale))
    loss_row, small_out = _adamw_small(
        vec_parts, g_pool.reshape(pool_w.shape), g_wa.reshape(lru_w_a.shape), g_wx.reshape(lru_w_x.shape),
        jnp.reshape(me, (1,)).astype(jnp.int32), small)
    grads = {n: o[0] for n, o in small_out.items()}
    delta = {n: o[1] for n, o in small_out.items()}
    new_m = {n: o[2] for n, o in small_out.items()}
    new_v = {n: o[3] for n, o in small_out.items()}

    for name, g, (dl, nm, nv) in zip(big_names, big_g, big_out):
        grads[name], delta[name], new_m[name], new_v[name] = g[None], dl[None], nm[None], nv[None]

    loss = loss_row[0, 0]
    order = ["norm_mix_pre", "norm_mix_post", "norm_mlp_pre", "norm_mlp_post", "w_in", "b_gate", "conv_w",
             "conv_b", "lru_w_a", "lru_b_a", "lru_w_x", "lru_b_x", "lru_lambda", "pool_w", "pool_scale",
             "w_lru_up", "w_pool_up", "w_o", "w_ff1", "w_ff2"]
    return (loss, grad_x[None], *[grads[n] for n in order], *[delta[n] for n in order],
            *[new_m[n] for n in order], *[new_v[n] for n in order])
```

```python
import functools
import math
import operator
import types

import jax
import jax.numpy as jnp
from jax import lax
from jax.experimental import pallas as pl
from jax.experimental.pallas import tpu as pltpu

F32 = jnp.float32
BF16 = jnp.bfloat16
NORM_EPS = 1e-6
LRU_C = 8.0
N_LRU_HEADS = 16
LRU_HEAD_DIM = 64
POOL_WINDOWS = (2, 4, 8, 16)
POOL_GROUP_DIM = 128
ADAM_LR = 0.001
ADAM_B1 = 0.9
ADAM_B2 = 0.999
ADAM_EPS = 1e-08
ADAM_WD = 0.01
ADAM_STEP = 10
N_DEV = 8
V7X_VMEM_LIMIT_BYTES = 56 * 1024 * 1024
LRU_CB = 256
MESH = pl.DeviceIdType.MESH
ANY = pl.BlockSpec(memory_space=pl.ANY)


def _tile(n, pref):
    t = min(n, pref)
    assert n % t == 0, (n, pref)
    return t


def _dot_nn(a, b):
    return lax.dot_general(a, b, (((1,), (0,)), ((), ())), preferred_element_type=F32)


def _dot_nt(a, b):
    return lax.dot_general(a, b, (((1,), (1,)), ((), ())), preferred_element_type=F32)


def _dot_tn(a, b):
    return lax.dot_general(a, b, (((0,), (0,)), ((), ())), preferred_element_type=F32)


def _row_chunks(n_rows, fn, chunk=256):
    chunk = min(chunk, n_rows)
    assert n_rows % chunk == 0

    def step(r, carry):
        fn(pl.ds(pl.multiple_of(r * chunk, chunk), chunk))
        return carry

    lax.fori_loop(0, n_rows // chunk, step, 0)


def _sig(x):
    return 1.0 / (1.0 + jnp.exp(-x))


def _rms_hat(x):
    r = lax.rsqrt(jnp.mean(x * x, axis=-1, keepdims=True) + NORM_EPS)
    return x * r, r


def _rms_bwd(dn, xhat, r, g):
    q = dn * g
    dx = r * (q - xhat * jnp.mean(q * xhat, axis=-1, keepdims=True))
    dg = jnp.sum(dn * xhat, axis=0, keepdims=True)
    return dx, dg


_GELU_K = math.sqrt(2.0 / math.pi)
_GELU_C = 0.044715


def _gelu_and_grad(g):
    t = jnp.tanh(_GELU_K * (g + _GELU_C * g * g * g))
    val = 0.5 * g * (1.0 + t)
    grad = 0.5 * (1.0 + t) + 0.5 * g * (1.0 - t * t) * (_GELU_K * (1.0 + 3.0 * _GELU_C * g * g))
    return val, grad


def _softplus_neg(lam):
    z = -lam
    e = jnp.exp(-jnp.abs(z))
    u = 1.0 + e
    d = u - 1.0
    l1p = jnp.where(d == 0.0, e, jnp.log(u) * (e / jnp.where(d == 0.0, 1.0, d)))
    return jnp.maximum(z, 0.0) + l1p


def _lru_gates(xc, wa, ba, wx, bx, lam):
    xcb = xc.astype(BF16)
    r = _sig(_dot_nn(xcb, wa) + ba)
    i = _sig(_dot_nn(xcb, wx) + bx)
    sp = _softplus_neg(lam)
    log_a = (-LRU_C) * r * sp
    a = jnp.exp(log_a)
    mult = jnp.sqrt(-jnp.tanh(log_a) * (1.0 + a * a))
    return xcb, r, i, sp, log_a, a, mult


def _place():
    return lax.axis_index("x"), lax.axis_index("y"), lax.axis_index("c")


def _ag_plan(shards, pieces=None, bufs=None):
    na = len(shards)
    n_kinds = 7

    def parts(ins, outs, sems):
        send_sems, recv_sems, local_sems = sems
        x, y, c = _place()
        me, sibling = (x, y, c), (x, y, 1 - c)
        x_nb, y_nb, diag = (1 - x, y), (x, 1 - y), (1 - x, 1 - y)
        relay_src = (c * (1 - x) + (1 - c) * x, c * y + (1 - c) * (1 - y))
        relay_dst = (c * x + (1 - c) * (1 - x), c * (1 - y) + (1 - c) * y)

        def own(a):
            return ins[a] if pieces is None else ins[a].at[pl.ds(*pieces[a])]

        def slot(a, px, py, pc):
            idx = 4 * px + 2 * py + pc
            return outs[a].at[idx] if pieces is None else outs[a].at[idx, pl.ds(*pieces[a])]

        def copy(a, k, block, to, src=None):
            return pltpu.make_async_remote_copy(
                src_ref=slot(a, *block) if src is None else src,
                dst_ref=slot(a, *block),
                send_sem=send_sems.at[a * n_kinds + k],
                recv_sem=recv_sems.at[a * n_kinds + k],
                device_id=to,
                device_id_type=MESH,
            )

        mine = [pltpu.make_async_copy(own(a), slot(a, *me), local_sems.at[a]) for a in range(na)]
        first, second, third = [], [], []
        for a in range(na):
            first += [copy(a, 0, me, sibling, src=own(a)), copy(a, 1, me, (*x_nb, c), src=own(a)),
                      copy(a, 2, me, (*y_nb, c), src=own(a))]
            second += [copy(a, 3, (*relay_src, c), (*relay_dst, c)), copy(a, 4, (*x_nb, c), sibling),
                       copy(a, 5, (*y_nb, c), sibling)]
            third.append(copy(a, 6, (*diag, c), sibling))
        return sibling, c, x_nb, y_nb, diag, copy, mine, first, second, third

    def start(ins, outs, sems):
        _, _, _, _, _, _, mine, first, _, _ = parts(ins, outs, sems)
        for cp in mine + first:
            cp.start()

    def middle(ins, outs, sems):
        _, c, x_nb, y_nb, _, copy, _, _, second, _ = parts(ins, outs, sems)
        for a in range(na):
            copy(a, 1, (*x_nb, c), (*x_nb, c)).wait_recv()
            copy(a, 2, (*y_nb, c), (*y_nb, c)).wait_recv()
        for cp in second:
            cp.start()

    def finish(ins, outs, sems):
        sibling, c, x_nb, y_nb, diag, copy, mine, first, second, third = parts(ins, outs, sems)
        for a in range(na):
            copy(a, 3, (*diag, c), (*diag, c)).wait_recv()
            third[a].start()
        for a in range(na):
            copy(a, 0, sibling, sibling).wait_recv()
            copy(a, 4, (*x_nb, 1 - c), sibling).wait_recv()
            copy(a, 5, (*y_nb, 1 - c), sibling).wait_recv()
            copy(a, 6, (*diag, 1 - c), sibling).wait_recv()
        for cp in first + second + third:
            cp.wait_send()
        for cp in mine:
            cp.wait()

    return types.SimpleNamespace(
        ins=list(shards) + list(bufs or []),
        out_shapes=[jax.ShapeDtypeStruct((N_DEV,) + s.shape, s.dtype) for s in shards],
        sems=[pltpu.SemaphoreType.DMA((n_kinds * na,)), pltpu.SemaphoreType.DMA((n_kinds * na,)),
              pltpu.SemaphoreType.DMA((na,))],
        aliases=[(na + a, a) for a in range(na)] if bufs else [],
        peers=frozenset({"sibling", "neighbours"}), start=start, middle=middle, finish=finish,
        parts=parts)


def _rs_sibling_plan(fulls):
    na = len(fulls)
    rs = [f.shape[0] // N_DEV for f in fulls]

    def copies(ins, outs, sems):
        send_sems, recv_sems = sems
        x, y, c = _place()
        out = []
        for a in range(na):
            for q in range(4):
                shard = 2 * q + (1 - c)
                out.append(pltpu.make_async_remote_copy(
                    src_ref=ins[a].at[pl.ds(shard * rs[a], rs[a])],
                    dst_ref=outs[a].at[q],
                    send_sem=send_sems.at[a * 4 + q],
                    recv_sem=recv_sems.at[a * 4 + q],
                    device_id=(x, y, 1 - c),
                    device_id_type=MESH,
                ))
        return out

    def start(ins, outs, sems):
        for cp in copies(ins, outs, sems):
            cp.start()

    def finish(ins, outs, sems):
        for cp in copies(ins, outs, sems):
            cp.wait()

    return types.SimpleNamespace(
        ins=list(fulls),
        out_shapes=[jax.ShapeDtypeStruct((4, r) + f.shape[1:], f.dtype) for r, f in zip(rs, fulls)],
        sems=[pltpu.SemaphoreType.DMA((4 * na,)), pltpu.SemaphoreType.DMA((4 * na,))],
        peers=frozenset({"sibling"}), start=start, finish=finish)


def _rs_chips_plan(sends, pieces=None, bufs=None):
    na = len(sends)

    def copies(ins, outs, sems):
        send_sems, recv_sems = sems
        x, y, c = _place()
        chips = [(1 - x, y), (x, 1 - y), (1 - x, 1 - y)]
        out = []
        for a in range(na):
            for k, chip in enumerate(chips):
                rows = (k,) if pieces is None else (k, pl.ds(*pieces[a]))
                out.append(pltpu.make_async_remote_copy(
                    src_ref=ins[a].at[rows],
                    dst_ref=outs[a].at[rows],
                    send_sem=send_sems.at[a * 3 + k],
                    recv_sem=recv_sems.at[a * 3 + k],
                    device_id=(*chip, c),
                    device_id_type=MESH,
                ))
        return out

    def start(ins, outs, sems):
        for cp in copies(ins, outs, sems):
            cp.start()

    def finish(ins, outs, sems):
        for cp in copies(ins, outs, sems):
            cp.wait()

    return types.SimpleNamespace(
        ins=list(sends) + list(bufs or []),
        out_shapes=[jax.ShapeDtypeStruct(s.shape, s.dtype) for s in sends],
        sems=[pltpu.SemaphoreType.DMA((3 * na,)), pltpu.SemaphoreType.DMA((3 * na,))],
        aliases=[(na + a, a) for a in range(na)] if bufs else [],
        peers=frozenset({"chips"}), start=start, finish=finish)


def _join(plans):
    ins, outs, sems, aliases, offs = [], [], [], [], []
    for p in plans:
        offs.append((len(ins), len(outs), len(sems)))
        aliases += [(len(ins) + ci, len(outs) + co) for ci, co in getattr(p, "aliases", [])]
        ins += p.ins
        outs += p.out_shapes
        sems += p.sems

    def cut(p, off, i, o, s):
        return (i[off[0]:off[0] + len(p.ins)], o[off[1]:off[1] + len(p.out_shapes)],
                s[off[2]:off[2] + len(p.sems)])

    def start(i, o, s):
        for p, off in zip(plans, offs):
            p.start(*cut(p, off, i, o, s))

    def middle(i, o, s):
        for p, off in zip(plans, offs):
            if getattr(p, "middle", None) is not None:
                p.middle(*cut(p, off, i, o, s))

    def finish(i, o, s):
        for p, off in zip(plans, offs):
            p.finish(*cut(p, off, i, o, s))

    def split(results):
        return [list(results[off[1]:off[1] + len(p.out_shapes)]) for p, off in zip(plans, offs)]

    return types.SimpleNamespace(ins=ins, out_shapes=outs, sems=sems, aliases=aliases,
                                 peers=frozenset().union(*[p.peers for p in plans]),
                                 start=start, middle=middle, finish=finish, split=split)


COLLECTIVE_ID = {frozenset({"sibling"}): 0, frozenset({"chips"}): 1, frozenset({"sibling", "chips"}): 2,
                 frozenset({"sibling", "neighbours"}): 3}


def _handshake(peers):
    x, y, c = _place()
    devs = []
    if "sibling" in peers:
        devs.append((x, y, 1 - c))
    if "neighbours" in peers:
        devs += [(1 - x, y, c), (x, 1 - y, c)]
    if "chips" in peers:
        assert "neighbours" not in peers
        devs += [(1 - x, y, c), (x, 1 - y, c), (1 - x, 1 - y, c)]
    barrier = pltpu.get_barrier_semaphore()
    for dev in devs:
        pl.semaphore_signal(barrier, inc=1, device_id=dev, device_id_type=MESH)
    pl.semaphore_wait(barrier, len(devs))


def _in_hbm(args):
    return [pltpu.with_memory_space_constraint(a, pltpu.HBM) for a in args]


def _run_plan(plan, name):
    n_in, n_out = len(plan.ins), len(plan.out_shapes)

    def body(*refs):
        ins, outs, sems = refs[:n_in], refs[n_in:n_in + n_out], refs[n_in + n_out:]
        _handshake(plan.peers)
        plan.start(ins, outs, sems)
        if getattr(plan, "middle", None) is not None:
            plan.middle(ins, outs, sems)
        plan.finish(ins, outs, sems)

    return pl.pallas_call(
        body,
        name=name,
        in_specs=[ANY] * n_in,
        out_specs=[ANY] * n_out,
        out_shape=plan.out_shapes,
        scratch_shapes=plan.sems,
        input_output_aliases=dict(getattr(plan, "aliases", [])),
        compiler_params=pltpu.CompilerParams(collective_id=COLLECTIVE_ID[plan.peers]),
    )(*_in_hbm(plan.ins))


def _call(body, *, name, grid, in_specs, out_specs, out_shape, args, scratch_shapes=(), aliases=None,
          carry=None):
    n_in, n_out, n_scr = len(in_specs), len(out_shape), len(scratch_shapes)
    params = pltpu.CompilerParams(
        dimension_semantics=("arbitrary",) * len(grid), vmem_limit_bytes=V7X_VMEM_LIMIT_BYTES)
    if carry is None:
        outs = pl.pallas_call(
            body, name=name, grid=grid, in_specs=list(in_specs), out_specs=list(out_specs),
            out_shape=list(out_shape), scratch_shapes=list(scratch_shapes),
            input_output_aliases=aliases or {}, compiler_params=params)(*_in_hbm(args))
        return list(outs), []
    c_in, c_out = len(carry.ins), len(carry.out_shapes)

    def full(*refs):
        p = 0
        ins = refs[p:p + n_in]
        p += n_in
        cins = refs[p:p + c_in]
        p += c_in
        outs = refs[p:p + n_out]
        p += n_out
        couts = refs[p:p + c_out]
        p += c_out
        scr = refs[p:p + n_scr]
        csems = refs[p + n_scr:]
        ids = [pl.program_id(a) for a in range(len(grid))]
        first = functools.reduce(operator.and_, [i == 0 for i in ids])
        last = functools.reduce(operator.and_, [i == g - 1 for i, g in zip(ids, grid)])

        @pl.when(first)
        def _():
            _handshake(carry.peers)
            carry.start(cins, couts, csems)

        if getattr(carry, "middle", None) is not None:
            n_steps = math.prod(grid)
            flat = functools.reduce(lambda acc, ig: acc * ig[1] + ig[0], zip(ids, grid), 0)

            @pl.when(flat == (2 * n_steps) // 3)
            def _():
                carry.middle(cins, couts, csems)

        body(*ins, *outs, *scr)

        @pl.when(last)
        def _():
            carry.finish(cins, couts, csems)

    all_aliases = dict(aliases or {})
    all_aliases.update({n_in + ci: n_out + co for ci, co in getattr(carry, "aliases", [])})
    params = pltpu.CompilerParams(
        dimension_semantics=("arbitrary",) * len(grid), vmem_limit_bytes=V7X_VMEM_LIMIT_BYTES,
        collective_id=COLLECTIVE_ID[carry.peers])
    outs = pl.pallas_call(
        full, name=name, grid=grid,
        in_specs=list(in_specs) + [ANY] * c_in,
        out_specs=list(out_specs) + [ANY] * c_out,
        out_shape=list(out_shape) + list(carry.out_shapes),
        scratch_shapes=list(scratch_shapes) + list(carry.sems),
        input_output_aliases=all_aliases, compiler_params=params)(*_in_hbm(args), *_in_hbm(carry.ins))
    return list(outs[:n_out]), list(outs[n_out:])


def _norm_proj(x, g1, w_int, carry=None):
    t, d = x.shape
    n = w_int.shape[0]
    tt, tn = _tile(t, 2048), _tile(n, 512)

    def body(x_ref, g_ref, w_ref, proj_ref, h1_ref, h1_s):
        @pl.when(pl.program_id(1) == 0)
        def _():
            def norm_rows(rows):
                xhat, _ = _rms_hat(x_ref[rows, :])
                h = (xhat * g_ref[...]).astype(BF16)
                h1_s[rows, :] = h
                h1_ref[rows, :] = h

            _row_chunks(tt, norm_rows)

        proj_ref[...] = _dot_nt(h1_s[...], w_ref[...]).astype(BF16)

    return _call(
        body, name="norm_proj", grid=(t // tt, n // tn),
        in_specs=[
            pl.BlockSpec((tt, d), lambda i, j: (i, 0)),
            pl.BlockSpec((1, d), lambda i, j: (0, 0)),
            pl.BlockSpec((tn, d), lambda i, j: (j, 0)),
        ],
        out_specs=[
            pl.BlockSpec((tt, tn), lambda i, j: (i, j)),
            pl.BlockSpec((tt, d), lambda i, j: (i, 0)),
        ],
        out_shape=[jax.ShapeDtypeStruct((t, n), BF16), jax.ShapeDtypeStruct((t, d), BF16)],
        scratch_shapes=[pltpu.VMEM((tt, d), BF16)],
        args=(x, g1, w_int), carry=carry)


def _norm_proj_stream(x, g1, s_in, s_cw, later):
    t, d = x.shape
    r = s_in.shape[0]
    n_chips = N_DEV // 2
    plan = _ag_plan([s_in, s_cw])
    n_in, n_out, n_sem = len(plan.ins), len(plan.out_shapes), len(plan.sems)
    l_in, l_out = len(later.ins), len(later.out_shapes)
    xi, yi, _ = _place()
    order = jnp.stack([2 * xi + yi, 2 * (1 - xi) + yi, 2 * xi + (1 - yi), 2 * (1 - xi) + (1 - yi)]).astype(jnp.int32)

    def body(order_ref, x_ref, g_ref, *refs):
        p = 0
        ins = refs[p:p + n_in]
        p += n_in
        lins = refs[p:p + l_in]
        p += l_in
        proj_ref, h1_ref = refs[p], refs[p + 1]
        p += 2
        outs = refs[p:p + n_out]
        p += n_out
        louts = refs[p:p + l_out]
        p += l_out
        h1_s, w_s, w_sem = refs[p], refs[p + 1], refs[p + 2]
        p += 3
        sems = refs[p:p + n_sem]
        lsems = refs[p + n_sem:]
        step = pl.program_id(0)
        sibling, c, x_nb, y_nb, diag, copy, mine, first, second, third = plan.parts(ins, outs, sems)
        arrays = range(len(plan.out_shapes))

        @pl.when(step == 0)
        def _():
            _handshake(plan.peers)
            plan.start(ins, outs, sems)

            def norm_rows(rows):
                xhat, _ = _rms_hat(x_ref[rows, :])
                h = (xhat * g_ref[...]).astype(BF16)
                h1_s[rows, :] = h
                h1_ref[rows, :] = h

            _row_chunks(t, norm_rows)
            for a in arrays:
                copy(a, 0, sibling, sibling).wait_recv()
            for cp in mine:
                cp.wait()

        @pl.when(step == 1)
        def _():
            plan.middle(ins, outs, sems)
            later.start(lins, louts, lsems)
            for a in arrays:
                copy(a, 4, (*x_nb, 1 - c), sibling).wait_recv()

        @pl.when(step == 2)
        def _():
            for a in arrays:
                copy(a, 5, (*y_nb, 1 - c), sibling).wait_recv()

        @pl.when(step == 3)
        def _():
            for a in arrays:
                copy(a, 3, (*diag, c), (*diag, c)).wait_recv()
                third[a].start()
            if getattr(later, "middle", None) is not None:
                later.middle(lins, louts, lsems)
            for a in arrays:
                copy(a, 6, (*diag, 1 - c), sibling).wait_recv()

        q = order_ref[step]
        w_copy = pltpu.make_async_copy(outs[0].at[pl.ds(2 * q, 2)], w_s, w_sem)
        w_copy.start()
        w_copy.wait()
        w = w_s[...].reshape(2 * r, d)

        def mul_rows(rows):
            proj_ref[rows, :] = _dot_nt(h1_s[rows, :], w).astype(BF16)

        _row_chunks(t, mul_rows, chunk=512)

        @pl.when(step == n_chips - 1)
        def _():
            for cp in first + second + third:
                cp.wait_send()
            later.finish(lins, louts, lsems)

    def fixed(shape):
        nd = len(shape)
        return pl.BlockSpec(tuple(shape), lambda s, o: (0,) * nd)

    grid_spec = pltpu.PrefetchScalarGridSpec(
        num_scalar_prefetch=1, grid=(n_chips,),
        in_specs=[fixed((t, d)), fixed((1, d))] + [ANY] * (n_in + l_in),
        out_specs=[pl.BlockSpec((t, 2 * r), lambda s, o: (0, o[s])), fixed((t, d))] + [ANY] * (n_out + l_out),
        scratch_shapes=[pltpu.VMEM((t, d), BF16), pltpu.VMEM((2, r, d), BF16), pltpu.SemaphoreType.DMA(())]
        + list(plan.sems) + list(later.sems))
    outs = pl.pallas_call(
        body, name="norm_proj_stream", grid_spec=grid_spec,
        out_shape=[jax.ShapeDtypeStruct((t, N_DEV * r), BF16), jax.ShapeDtypeStruct((t, d), BF16)]
        + list(plan.out_shapes) + list(later.out_shapes),
        compiler_params=pltpu.CompilerParams(
            dimension_semantics=("arbitrary",), vmem_limit_bytes=V7X_VMEM_LIMIT_BYTES,
            collective_id=COLLECTIVE_ID[plan.peers]),
    )(order, *_in_hbm([x, g1] + list(plan.ins) + list(later.ins)))
    return outs[0], outs[1], outs[2], outs[3], list(outs[4:])


def _scan_rows(av, bv, reverse):
    tc = av.shape[0]
    row = lax.broadcasted_iota(jnp.int32, av.shape, 0)
    s = 1
    while s < tc:
        if s < 8:
            keep = (row < tc - s) if reverse else (row >= s)
            shift = (tc - s) if reverse else s
            a_sh = jnp.where(keep, pltpu.roll(av, shift, 0), 1.0)
            b_sh = jnp.where(keep, pltpu.roll(bv, shift, 0), 0.0)
            bv = av * b_sh + bv
            av = av * a_sh
        elif reverse:
            bv = jnp.concatenate([av[:tc - s] * bv[s:] + bv[:tc - s], bv[tc - s:]], axis=0)
            av = jnp.concatenate([av[:tc - s] * av[s:], av[tc - s:]], axis=0)
        else:
            bv = jnp.concatenate([bv[:s], av[s:] * bv[:tc - s] + bv[s:]], axis=0)
            av = jnp.concatenate([av[:s], av[s:] * av[:tc - s]], axis=0)
        s *= 2
    return av, bv


def _fill_block_diag(w_ref, bd_ref):
    bd_ref[...] = jnp.zeros_like(bd_ref)
    hd = LRU_HEAD_DIM
    for k in range(w_ref.shape[0]):
        bd_ref[k * hd:(k + 1) * hd, k * hd:(k + 1) * hd] = w_ref[k].astype(BF16)


def _lru_fwd(proj, conv_w, conv_b, w_a, b_a, w_x, b_x, lam, carry=None):
    t = proj.shape[0]
    dr = conv_b.shape[1]
    cb = LRU_CB
    tc = _tile(t, 256)
    ncb, ntc = dr // cb, t // tc

    def body(xp_ref, g_ref, cw_ref, cb_ref, wa_ref, ba_ref, wx_ref, bx_ref, lam_ref,
             y_ref, h_ref, xc_ref, prevx_s, hlast_s, wa_s, wx_s):
        c = pl.program_id(1)

        @pl.when(c == 0)
        def _():
            prevx_s[...] = jnp.zeros_like(prevx_s)
            hlast_s[...] = jnp.zeros_like(hlast_s)
            _fill_block_diag(wa_ref, wa_s)
            _fill_block_diag(wx_ref, wx_s)

        x = xp_ref[...].astype(F32)
        prev = prevx_s[...]
        row = lax.broadcasted_iota(jnp.int32, x.shape, 0)

        def sh(j):
            return jnp.where(row >= j, pltpu.roll(x, j, 0), pltpu.roll(prev, j, 0))

        xc = (cb_ref[...] + cw_ref[0:1, :] * sh(3) + cw_ref[1:2, :] * sh(2)
              + cw_ref[2:3, :] * sh(1) + cw_ref[3:4, :] * x)
        prevx_s[...] = x
        xc_ref[...] = xc
        _, _, i, _, _, a, mult = _lru_gates(xc, wa_s[...], ba_ref[...], wx_s[...], bx_ref[...],
                                            lam_ref[...])
        av, bv = _scan_rows(a, mult * (i * xc), reverse=False)
        h = av * hlast_s[...] + bv
        h_ref[...] = h
        hlast_s[...] = h_ref[tc - 1:tc, :]
        gel, _ = _gelu_and_grad(g_ref[...].astype(F32))
        y_ref[...] = (h * gel).astype(BF16)

    vec = pl.BlockSpec((1, cb), lambda j, c: (0, j))
    blk = pl.BlockSpec((tc, cb), lambda j, c: (c, j))
    mat = pl.BlockSpec((cb // LRU_HEAD_DIM, LRU_HEAD_DIM, LRU_HEAD_DIM), lambda j, c: (j, 0, 0))
    return _call(
        body, name="lru_fwd", grid=(ncb, ntc),
        in_specs=[
            blk,
            pl.BlockSpec((tc, cb), lambda j, c: (c, ncb + j)),
            pl.BlockSpec((4, cb), lambda j, c: (0, j)),
            vec, mat, vec, mat, vec, vec,
        ],
        out_specs=[blk, blk, blk],
        out_shape=[
            jax.ShapeDtypeStruct((t, dr), BF16),
            jax.ShapeDtypeStruct((t, dr), F32),
            jax.ShapeDtypeStruct((t, dr), F32),
        ],
        scratch_shapes=[pltpu.VMEM((tc, cb), F32), pltpu.VMEM((1, cb), F32),
                        pltpu.VMEM((cb, cb), BF16), pltpu.VMEM((cb, cb), BF16)],
        args=(proj, proj, conv_w, conv_b, w_a, b_a, w_x, b_x, lam), carry=carry)


def _pool_select(col, vals):
    out = vals[3]
    for g in (2, 1, 0):
        out = jnp.where(col < (g + 1) * POOL_GROUP_DIM, vals[g], out)
    return out


def _pool_fwd(proj, pool_w, pool_scale, col_block):
    t = proj.shape[0]
    dp = pool_scale.shape[1]
    tc = _tile(t, 256)
    ntc = t // tc

    def body(x_ref, w_ref, sc_ref, y_ref, p_ref, px, p2, p4, p8):
        c = pl.program_id(0)

        @pl.when(c == 0)
        def _():
            for s in (px, p2, p4, p8):
                s[...] = jnp.zeros_like(s)

        x = x_ref[...].astype(F32)
        row = lax.broadcasted_iota(jnp.int32, x.shape, 0)
        col = lax.broadcasted_iota(jnp.int32, x.shape, 1)

        def sh(v, pv, j):
            return jnp.where(row >= j, pltpu.roll(v, j, 0), pltpu.roll(pv[...], j, 0))

        s2 = x + sh(x, px, 1)
        s4 = s2 + sh(s2, p2, 2)
        s8 = s4 + sh(s4, p4, 4)
        s16 = s8 + sh(s8, p8, 8)
        px[...] = x
        p2[...] = s2
        p4[...] = s4
        p8[...] = s8
        wsum = _pool_select(col, (s2, s4, s8, s16))
        win = _pool_select(col, POOL_WINDOWS)
        cnt = jnp.minimum(c * tc + row + 1, win).astype(F32)
        p = wsum / cnt - x
        pb = p.astype(BF16)
        p_ref[...] = pb
        for g in range(len(POOL_WINDOWS)):
            sl = slice(g * POOL_GROUP_DIM, (g + 1) * POOL_GROUP_DIM)
            yg = _dot_nn(pb[:, sl], w_ref[g]) * sc_ref[:, sl]
            y_ref[:, sl] = yg.astype(BF16)

    return _call(
        body, name="pool_fwd", grid=(ntc,),
        in_specs=[
            pl.BlockSpec((tc, dp), lambda c: (c, col_block)),
            pl.BlockSpec(pool_w.shape, lambda c: (0, 0, 0)),
            pl.BlockSpec((1, dp), lambda c: (0, 0)),
        ],
        out_specs=[pl.BlockSpec((tc, dp), lambda c: (c, 0))] * 2,
        out_shape=[jax.ShapeDtypeStruct((t, dp), BF16)] * 2,
        scratch_shapes=[pltpu.VMEM((tc, dp), F32)] * 4,
        args=(proj, pool_w, pool_scale))[0]


def _branch_mix(y_lru, y_pool, w_lru_up, w_pool_upt, proj, b_gate, ga_block, gb_block, carry=None):
    t, d = y_lru.shape
    dp = y_pool.shape[1]
    tt, tn = _tile(t, 1024), 512
    nj = d // tn

    def body(yl_ref, yp_ref, wl_ref, wp_ref, ga_ref, gb_ref, ba_ref, bb_ref, bra_ref, brb_ref, mix_ref):
        br_a = _dot_nn(yl_ref[...], wl_ref[...])
        br_b = _dot_nt(yp_ref[...], wp_ref[...])
        bra_ref[...] = br_a.astype(BF16)
        brb_ref[...] = br_b.astype(BF16)
        ga = _sig(ga_ref[...].astype(F32) + ba_ref[...])
        gb = _sig(gb_ref[...].astype(F32) + bb_ref[...])
        mix_ref[...] = (ga * br_a + gb * br_b).astype(BF16)

    out = pl.BlockSpec((tt, tn), lambda j, i: (i, j))
    return _call(
        body, name="branch_mix", grid=(nj, t // tt),
        in_specs=[
            pl.BlockSpec((tt, d), lambda j, i: (i, 0)),
            pl.BlockSpec((tt, dp), lambda j, i: (i, 0)),
            pl.BlockSpec((d, tn), lambda j, i: (0, j)),
            pl.BlockSpec((tn, dp), lambda j, i: (j, 0)),
            pl.BlockSpec((tt, tn), lambda j, i: (i, ga_block + j)),
            pl.BlockSpec((tt, tn), lambda j, i: (i, gb_block + j)),
            pl.BlockSpec((1, tn), lambda j, i: (0, j)),
            pl.BlockSpec((1, tn), lambda j, i: (0, nj + j)),
        ],
        out_specs=[out, out, out],
        out_shape=[jax.ShapeDtypeStruct((t, d), BF16)] * 3,
        args=(y_lru, y_pool, w_lru_up, w_pool_upt, proj, proj, b_gate, b_gate), carry=carry)


def _wo_norm(mix, w_o, x, g2, g3, carry=None):
    t, d = x.shape
    tt = _tile(t, 512)

    def body(mix_ref, w_ref, x_ref, g2_ref, g3_ref, m_ref, x2_ref, h3_ref):
        m = _dot_nn(mix_ref[...], w_ref[...])
        m_ref[...] = m
        mhat, _ = _rms_hat(m)
        x2 = x_ref[...] + mhat * g2_ref[...]
        x2_ref[...] = x2
        xhat, _ = _rms_hat(x2)
        h3_ref[...] = (xhat * g3_ref[...]).astype(BF16)

    row = pl.BlockSpec((tt, d), lambda i: (i, 0))
    vec = pl.BlockSpec((1, d), lambda i: (0, 0))
    return _call(
        body, name="wo_norm", grid=(t // tt,),
        in_specs=[row, pl.BlockSpec((d, d), lambda i: (0, 0)), row, vec, vec],
        out_specs=[row, row, row],
        out_shape=[
            jax.ShapeDtypeStruct((t, d), F32),
            jax.ShapeDtypeStruct((t, d), F32),
            jax.ShapeDtypeStruct((t, d), BF16),
        ],
        args=(mix, w_o, x, g2, g3), carry=carry)


def _ff1(h3, w_ff1t, carry=None):
    t, d = h3.shape
    n = w_ff1t.shape[0]
    tt, tn = _tile(t, 2048), _tile(n, 512)

    def body(h_ref, w_ref, rf_ref):
        rf_ref[...] = jnp.maximum(_dot_nt(h_ref[...], w_ref[...]), 0.0).astype(BF16)

    out = pl.BlockSpec((tt, tn), lambda i, j: (i, j))
    return _call(
        body, name="ff1", grid=(t // tt, n // tn),
        in_specs=[pl.BlockSpec((tt, d), lambda i, j: (i, 0)), pl.BlockSpec((tn, d), lambda i, j: (j, 0))],
        out_specs=[out],
        out_shape=[jax.ShapeDtypeStruct((t, n), BF16)],
        args=(h3, w_ff1t), carry=carry)


def _ff2_loss(rf, w_ff2, x2, g4, target):
    t, k = rf.shape
    d = x2.shape[1]
    tt, tk = _tile(t, 1024), _tile(k, 512)
    nk = k // tk

    def body(a_ref, w_ref, x2_ref, g_ref, tg_ref, dy_ref, df_ref, dg_ref, loss_ref, acc):
        i, kk = pl.program_id(0), pl.program_id(1)

        @pl.when(kk == 0)
        def _():
            acc[...] = jnp.zeros_like(acc)

        @pl.when((i == 0) & (kk == 0))
        def _():
            dg_ref[...] = jnp.zeros_like(dg_ref)
            loss_ref[...] = jnp.zeros_like(loss_ref)

        rf_tile = a_ref[...]
        acc[...] += _dot_nn(rf_tile * rf_tile, w_ref[...])

        @pl.when(kk == nk - 1)
        def _():
            def tail(rows):
                fhat, r = _rms_hat(acc[rows, :])
                g = g_ref[...]
                e = x2_ref[rows, :] + fhat * g - tg_ref[rows, :]
                loss_ref[...] += 0.5 * jnp.sum(jnp.mean(e * e, axis=-1, keepdims=True))
                dy = e * (1.0 / d)
                dy_ref[rows, :] = dy.astype(BF16)
                df, dg = _rms_bwd(dy, fhat, r, g)
                df_ref[rows, :] = df.astype(BF16)
                dg_ref[...] += dg

            _row_chunks(tt, tail)

    row = pl.BlockSpec((tt, d), lambda i, kk: (i, 0))
    vec = pl.BlockSpec((1, d), lambda i, kk: (0, 0))
    return _call(
        body, name="ff2_loss", grid=(t // tt, nk),
        in_specs=[
            pl.BlockSpec((tt, tk), lambda i, kk: (i, kk)),
            pl.BlockSpec((tk, d), lambda i, kk: (kk, 0)),
            row, vec, row,
        ],
        out_specs=[row, row, vec, pl.BlockSpec((1, 128), lambda i, kk: (0, 0))],
        out_shape=[
            jax.ShapeDtypeStruct((t, d), BF16),
            jax.ShapeDtypeStruct((t, d), BF16),
            jax.ShapeDtypeStruct((1, d), F32),
            jax.ShapeDtypeStruct((1, 128), F32),
        ],
        scratch_shapes=[pltpu.VMEM((tt, d), F32)],
        args=(rf, w_ff2, x2, g4, target))[0]


def _ff2_bwd(df, w_ff2, rf, carry=None):
    t, d = df.shape
    n = w_ff2.shape[0]
    tt, tn = _tile(t, 2048), _tile(n, 512)

    def body(df_ref, w_ref, rf_ref, out_ref):
        d_act = _dot_nt(df_ref[...], w_ref[...])
        out_ref[...] = (d_act * (2.0 * rf_ref[...].astype(F32))).astype(BF16)

    blk = pl.BlockSpec((tt, tn), lambda i, j: (i, j))
    return _call(
        body, name="ff2_bwd", grid=(t // tt, n // tn),
        in_specs=[pl.BlockSpec((tt, d), lambda i, j: (i, 0)), pl.BlockSpec((tn, d), lambda i, j: (j, 0)), blk],
        out_specs=[blk],
        out_shape=[jax.ShapeDtypeStruct((t, n), BF16)],
        args=(df, w_ff2, rf), carry=carry)


def _wgrad(a, b, name, prev=None, row_off=0, rows=None, carry=None, square_a=False):
    t, m = a.shape
    n = b.shape[1]
    rows = m if rows is None else rows
    tm, tk = _tile(m, 512), _tile(t, 2048)
    nk = t // tk
    assert row_off % tm == 0
    off = row_off // tm

    def body(*refs):
        a_ref, b_ref = refs[0], refs[1]
        o32_ref, o16_ref, acc = refs[-3], refs[-2], refs[-1]
        kk = pl.program_id(1)

        @pl.when(kk == 0)
        def _():
            acc[...] = jnp.zeros_like(acc)

        a_tile = a_ref[...]
        acc[...] += _dot_tn(a_tile * a_tile if square_a else a_tile, b_ref[...])

        @pl.when(kk == nk - 1)
        def _():
            o32_ref[...] = acc[...]
            o16_ref[...] = acc[...].astype(BF16)

    in_specs = [pl.BlockSpec((tk, tm), lambda i, kk: (kk, i)), pl.BlockSpec((tk, n), lambda i, kk: (kk, 0))]
    args = [a, b]
    aliases = {}
    if prev is not None:
        in_specs += [ANY, ANY]
        args += list(prev)
        aliases = {2: 0, 3: 1}
    out = pl.BlockSpec((tm, n), lambda i, kk: (off + i, 0))
    return _call(
        body, name=name, grid=(m // tm, nk),
        in_specs=in_specs, out_specs=[out, out],
        out_shape=[jax.ShapeDtypeStruct((rows, n), F32), jax.ShapeDtypeStruct((rows, n), BF16)],
        scratch_shapes=[pltpu.VMEM((tm, n), F32)],
        aliases=aliases, args=args, carry=carry)


def _wgrad_parts(parts, b, name, carry=None):
    t, n = b.shape
    tm = 512
    bounds = []
    lo = 0
    for part in parts:
        assert part.shape[0] == t and part.shape[1] % tm == 0
        bounds.append((lo, lo + part.shape[1] // tm))
        lo += part.shape[1] // tm
    nm = lo
    np_ = len(parts)

    def body(*refs):
        p_refs, b_ref, o32_ref, o16_ref = refs[:np_], refs[np_], refs[np_ + 1], refs[np_ + 2]
        i = pl.program_id(0)
        for (lo_p, hi_p), p_ref in zip(bounds, p_refs):
            @pl.when((i >= lo_p) & (i < hi_p))
            def _(p_ref=p_ref):
                res = _dot_tn(p_ref[...], b_ref[...])
                o32_ref[...] = res
                o16_ref[...] = res.astype(BF16)

    def part_spec(lo_p, hi_p):
        return pl.BlockSpec((t, tm), lambda i: (0, jnp.clip(i - lo_p, 0, hi_p - lo_p - 1)))

    out = pl.BlockSpec((tm, n), lambda i: (i, 0))
    return _call(
        body, name=name, grid=(nm,),
        in_specs=[part_spec(lo_p, hi_p) for lo_p, hi_p in bounds] + [pl.BlockSpec((t, n), lambda i: (0, 0))],
        out_specs=[out, out],
        out_shape=[jax.ShapeDtypeStruct((nm * tm, n), F32), jax.ShapeDtypeStruct((nm * tm, n), BF16)],
        args=(*parts, b), carry=carry)


def _ff1_bwd_norms(d_f1, w_ff1t, dy, x2, g3, m, g2, carry=None):
    t, k = d_f1.shape
    d = x2.shape[1]
    tt, tk = _tile(t, 1024), _tile(k, 512)
    nk = k // tk

    def body(a_ref, w_ref, dy_ref, x2_ref, g3_ref, m_ref, g2_ref, dx2_ref, dm_ref, dg3_ref, dg2_ref, acc):
        i, kk = pl.program_id(0), pl.program_id(1)

        @pl.when(kk == 0)
        def _():
            acc[...] = jnp.zeros_like(acc)

        @pl.when((i == 0) & (kk == 0))
        def _():
            dg3_ref[...] = jnp.zeros_like(dg3_ref)
            dg2_ref[...] = jnp.zeros_like(dg2_ref)

        acc[...] += _dot_nn(a_ref[...], w_ref[...])

        @pl.when(kk == nk - 1)
        def _():
            def tail(rows):
                xhat, r3 = _rms_hat(x2_ref[rows, :])
                dx, dg3 = _rms_bwd(acc[rows, :], xhat, r3, g3_ref[...])
                dx2 = dy_ref[rows, :].astype(F32) + dx
                dx2_ref[rows, :] = dx2
                dg3_ref[...] += dg3
                mhat, r2 = _rms_hat(m_ref[rows, :])
                dm, dg2 = _rms_bwd(dx2, mhat, r2, g2_ref[...])
                dm_ref[rows, :] = dm.astype(BF16)
                dg2_ref[...] += dg2

            _row_chunks(tt, tail)

    row = pl.BlockSpec((tt, d), lambda i, kk: (i, 0))
    vec = pl.BlockSpec((1, d), lambda i, kk: (0, 0))
    return _call(
        body, name="ff1_bwd_norms", grid=(t // tt, nk),
        in_specs=[
            pl.BlockSpec((tt, tk), lambda i, kk: (i, kk)),
            pl.BlockSpec((tk, d), lambda i, kk: (kk, 0)),
            row, row, vec, row, vec,
        ],
        out_specs=[row, row, vec, vec],
        out_shape=[
            jax.ShapeDtypeStruct((t, d), F32),
            jax.ShapeDtypeStruct((t, d), BF16),
            jax.ShapeDtypeStruct((1, d), F32),
            jax.ShapeDtypeStruct((1, d), F32),
        ],
        scratch_shapes=[pltpu.VMEM((tt, d), F32)],
        args=(d_f1, w_ff1t, dy, x2, g3, m, g2), carry=carry)


def _wo_bwd_mix(dm, w_o, br_a, br_b, proj, b_gate, ga_block, gb_block, carry=None):
    t, d = dm.shape
    tt, tn = _tile(t, 1024), 512
    nj = d // tn

    def body(dm_ref, w_ref, bra_ref, brb_ref, ga_ref, gb_ref, ba_ref, bb_ref,
             dbra_ref, dbrb_ref, dga_ref, dgb_ref, dba_ref, dbb_ref):
        i = pl.program_id(1)

        @pl.when(i == 0)
        def _():
            dba_ref[...] = jnp.zeros_like(dba_ref)
            dbb_ref[...] = jnp.zeros_like(dbb_ref)

        d_mix = _dot_nt(dm_ref[...], w_ref[...])
        ga = _sig(ga_ref[...].astype(F32) + ba_ref[...])
        gb = _sig(gb_ref[...].astype(F32) + bb_ref[...])
        dbra_ref[...] = (d_mix * ga).astype(BF16)
        dbrb_ref[...] = (d_mix * gb).astype(BF16)
        dga = d_mix * bra_ref[...].astype(F32) * (ga * (1.0 - ga))
        dgb = d_mix * brb_ref[...].astype(F32) * (gb * (1.0 - gb))
        dga_ref[...] = dga.astype(BF16)
        dgb_ref[...] = dgb.astype(BF16)
        dba_ref[...] += jnp.sum(dga, axis=0, keepdims=True)
        dbb_ref[...] += jnp.sum(dgb, axis=0, keepdims=True)

    blk = pl.BlockSpec((tt, tn), lambda j, i: (i, j))
    vec = pl.BlockSpec((1, tn), lambda j, i: (0, j))
    return _call(
        body, name="wo_bwd_mix", grid=(nj, t // tt),
        in_specs=[
            pl.BlockSpec((tt, d), lambda j, i: (i, 0)),
            pl.BlockSpec((tn, d), lambda j, i: (j, 0)),
            blk, blk,
            pl.BlockSpec((tt, tn), lambda j, i: (i, ga_block + j)),
            pl.BlockSpec((tt, tn), lambda j, i: (i, gb_block + j)),
            vec,
            pl.BlockSpec((1, tn), lambda j, i: (0, nj + j)),
        ],
        out_specs=[blk, blk, blk, blk, vec, vec],
        out_shape=[jax.ShapeDtypeStruct((t, d), BF16)] * 4 + [jax.ShapeDtypeStruct((1, d), F32)] * 2,
        args=(dm, w_o, br_a, br_b, proj, proj, b_gate, b_gate), carry=carry)


def _lru_up_bwd(d_br_a, w_lru_up, proj, h, g_block, carry=None):
    t, d = d_br_a.shape
    tt, tn = _tile(t, 1024), 512

    def body(a_ref, w_ref, g_ref, h_ref, dh_ref, dg_ref):
        d_y = _dot_nt(a_ref[...], w_ref[...])
        gel, gel_grad = _gelu_and_grad(g_ref[...].astype(F32))
        dh_ref[...] = d_y * gel
        dg_ref[...] = (d_y * h_ref[...] * gel_grad).astype(BF16)

    blk = pl.BlockSpec((tt, tn), lambda i, j: (i, j))
    return _call(
        body, name="lru_up_bwd", grid=(t // tt, d // tn),
        in_specs=[
            pl.BlockSpec((tt, d), lambda i, j: (i, 0)),
            pl.BlockSpec((tn, d), lambda i, j: (j, 0)),
            pl.BlockSpec((tt, tn), lambda i, j: (i, g_block + j)),
            blk,
        ],
        out_specs=[blk, blk],
        out_shape=[jax.ShapeDtypeStruct((t, d), F32), jax.ShapeDtypeStruct((t, d), BF16)],
        args=(d_br_a, w_lru_up, proj, h), carry=carry)


def _pool_up_bwd(d_br_b, w_pool_upt):
    t, d = d_br_b.shape
    dp = w_pool_upt.shape[1]
    tt = _tile(t, 2048)

    def body(a_ref, w_ref, out_ref):
        out_ref[...] = _dot_nn(a_ref[...], w_ref[...])

    return _call(
        body, name="pool_up_bwd", grid=(t // tt,),
        in_specs=[pl.BlockSpec((tt, d), lambda i: (i, 0)), pl.BlockSpec((d, dp), lambda i: (0, 0))],
        out_specs=[pl.BlockSpec((tt, dp), lambda i: (i, 0))],
        out_shape=[jax.ShapeDtypeStruct((t, dp), F32)],
        args=(d_br_b, w_pool_upt))[0][0]


def _lru_bwd(dh, xc, h, proj, conv_w, w_a, b_a, w_x, b_x, lam, carry=None):
    t, dr = dh.shape
    cb = LRU_CB
    hd = LRU_HEAD_DIM
    per = cb // hd
    tc = _tile(t, 256)
    ncb, ntc = dr // cb, t // tc

    def body(dh_ref, xc_ref, h_ref, hp_ref, xp_ref, cw_ref, wa_ref, ba_ref, wx_ref, bx_ref, lam_ref,
             dxp_ref, dwa_ref, dba_ref, dwx_ref, dbx_ref, dlam_ref, dcw_ref, dcb_ref,
             nextd_s, anext_s, gnext_s, tmp_s, wa_s, wx_s):
        c = pl.program_id(1)
        rc = ntc - 1 - c

        @pl.when(c == 0)
        def _():
            nextd_s[...] = jnp.zeros_like(nextd_s)
            anext_s[...] = jnp.zeros_like(anext_s)
            gnext_s[...] = jnp.zeros_like(gnext_s)
            for ref in (dwa_ref, dba_ref, dwx_ref, dbx_ref, dlam_ref, dcw_ref, dcb_ref):
                ref[...] = jnp.zeros_like(ref)
            _fill_block_diag(wa_ref, wa_s)
            _fill_block_diag(wx_ref, wx_s)

        xc = xc_ref[...]
        wa, wx, lam = wa_s[...], wx_s[...], lam_ref[...]
        xcb, r, i, sp, log_a, a, mult = _lru_gates(xc, wa, ba_ref[...], wx, bx_ref[...], lam)
        row = lax.broadcasted_iota(jnp.int32, xc.shape, 0)
        h = h_ref[...]
        hp = jnp.where(rc == 0, 0.0, hp_ref[...])
        hprev = jnp.where(row >= 1, pltpu.roll(h, 1, 0), pltpu.roll(hp, 1, 0))

        def up(v, nv, j):
            return jnp.where(row < tc - j, pltpu.roll(v, tc - j, 0), nv)

        av, bv = _scan_rows(up(a, anext_s[...], 1), dh_ref[...], reverse=True)
        gt = av * gnext_s[...] + bv
        tmp_s[...] = gt
        gnext_s[...] = tmp_s[0:1, :]
        tmp_s[...] = a
        anext_s[...] = tmp_s[0:1, :]

        da = gt * hprev
        ixc = i * xc
        d_mult = gt * ixc
        d_i = gt * mult * xc
        d_xc = gt * mult * i
        d_log_a = da * a - d_mult * (a * a) / mult
        d_pre_r = (d_log_a * ((-LRU_C) * sp)) * (r * (1.0 - r))
        d_pre_i = d_i * (i * (1.0 - i))
        d_sp = jnp.sum(d_log_a * ((-LRU_C) * r), axis=0, keepdims=True)
        dlam_ref[...] += d_sp * (-1.0 / (1.0 + jnp.exp(lam)))
        dpr = d_pre_r.astype(BF16)
        dpi = d_pre_i.astype(BF16)
        dba_ref[...] += jnp.sum(d_pre_r, axis=0, keepdims=True)
        dbx_ref[...] += jnp.sum(d_pre_i, axis=0, keepdims=True)
        pa = _dot_tn(xcb, dpr)
        px = _dot_tn(xcb, dpi)
        for k in range(per):
            dwa_ref[k] += pa[k * hd:(k + 1) * hd, k * hd:(k + 1) * hd]
            dwx_ref[k] += px[k * hd:(k + 1) * hd, k * hd:(k + 1) * hd]
        d_xc = d_xc + _dot_nt(dpr, wa) + _dot_nt(dpi, wx)

        nxt = nextd_s[...]
        xp = xp_ref[...].astype(F32)
        dxp = cw_ref[3:4, :] * d_xc
        dcw_ref[3:4, :] += jnp.sum(xp * d_xc, axis=0, keepdims=True)
        for j in (1, 2, 3):
            uj = up(d_xc, pltpu.roll(nxt, tc - j, 0), j)
            dxp = dxp + cw_ref[3 - j:4 - j, :] * uj
            dcw_ref[3 - j:4 - j, :] += jnp.sum(xp * uj, axis=0, keepdims=True)
        dcb_ref[...] += jnp.sum(d_xc, axis=0, keepdims=True)
        nextd_s[...] = d_xc
        dxp_ref[...] = dxp.astype(BF16)

    vec = pl.BlockSpec((1, cb), lambda j, c: (0, j))
    blk = pl.BlockSpec((tc, cb), lambda j, c: (ntc - 1 - c, j))
    mat = pl.BlockSpec((per, hd, hd), lambda j, c: (j, 0, 0))
    cwb = pl.BlockSpec((4, cb), lambda j, c: (0, j))
    return _call(
        body, name="lru_bwd", grid=(ncb, ntc),
        in_specs=[
            blk, blk, blk,
            pl.BlockSpec((tc, cb), lambda j, c: (jnp.maximum(ntc - 2 - c, 0), j)),
            blk, cwb, mat, vec, mat, vec, vec,
        ],
        out_specs=[blk, mat, vec, mat, vec, vec, cwb, vec],
        out_shape=[
            jax.ShapeDtypeStruct((t, dr), BF16),
            jax.ShapeDtypeStruct(w_a.shape, F32),
            jax.ShapeDtypeStruct((1, dr), F32),
            jax.ShapeDtypeStruct(w_x.shape, F32),
            jax.ShapeDtypeStruct((1, dr), F32),
            jax.ShapeDtypeStruct((1, dr), F32),
            jax.ShapeDtypeStruct((4, dr), F32),
            jax.ShapeDtypeStruct((1, dr), F32),
        ],
        scratch_shapes=[
            pltpu.VMEM((tc, cb), F32),
            pltpu.VMEM((1, cb), F32),
            pltpu.VMEM((1, cb), F32),
            pltpu.VMEM((tc, cb), F32),
            pltpu.VMEM((cb, cb), BF16),
            pltpu.VMEM((cb, cb), BF16),
        ],
        args=(dh, xc, h, h, proj, conv_w, w_a, b_a, w_x, b_x, lam), carry=carry)


def _pool_bwd(d_y_pool, p, pool_w, pool_scale):
    t, dp = d_y_pool.shape
    tc = _tile(t, 256)
    ntc = t // tc
    ng = len(POOL_WINDOWS)

    def body(dy_ref, p_ref, w_ref, sc_ref, dx_ref, dw_ref, dsc_ref, nz, n2, n4, n8, dp_s):
        c = pl.program_id(0)
        rc = ntc - 1 - c

        @pl.when(c == 0)
        def _():
            for s in (nz, n2, n4, n8):
                s[...] = jnp.zeros_like(s)
            dw_ref[...] = jnp.zeros_like(dw_ref)
            dsc_ref[...] = jnp.zeros_like(dsc_ref)

        for g in range(ng):
            sl = slice(g * POOL_GROUP_DIM, (g + 1) * POOL_GROUP_DIM)
            pg = p_ref[:, sl]
            dyg = dy_ref[:, sl]
            wg = w_ref[g].astype(BF16)
            q = _dot_nn(pg, wg)
            dsc_ref[:, sl] += jnp.sum(dyg * q, axis=0, keepdims=True)
            dpw = (dyg * sc_ref[:, sl]).astype(BF16)
            dw_ref[g] += _dot_tn(pg, dpw)
            dp_s[:, sl] = _dot_nt(dpw, wg)

        dpv = dp_s[...]
        row = lax.broadcasted_iota(jnp.int32, dpv.shape, 0)
        col = lax.broadcasted_iota(jnp.int32, dpv.shape, 1)
        win = _pool_select(col, POOL_WINDOWS)
        cnt = jnp.minimum(rc * tc + row + 1, win).astype(F32)
        z = dpv / cnt

        def up(v, nv, j):
            return jnp.where(row < tc - j, pltpu.roll(v, tc - j, 0), pltpu.roll(nv[...], tc - j, 0))

        u2 = z + up(z, nz, 1)
        u4 = u2 + up(u2, n2, 2)
        u8 = u4 + up(u4, n4, 4)
        u16 = u8 + up(u8, n8, 8)
        nz[...] = z
        n2[...] = u2
        n4[...] = u4
        n8[...] = u8
        dx_ref[...] = (_pool_select(col, (u2, u4, u8, u16)) - dpv).astype(BF16)

    blk = pl.BlockSpec((tc, dp), lambda c: (ntc - 1 - c, 0))
    full_w = pl.BlockSpec(pool_w.shape, lambda c: (0, 0, 0))
    vec = pl.BlockSpec((1, dp), lambda c: (0, 0))
    return _call(
        body, name="pool_bwd", grid=(ntc,),
        in_specs=[blk, blk, full_w, vec],
        out_specs=[blk, full_w, vec],
        out_shape=[
            jax.ShapeDtypeStruct((t, dp), BF16),
            jax.ShapeDtypeStruct(pool_w.shape, F32),
            jax.ShapeDtypeStruct((1, dp), F32),
        ],
        scratch_shapes=[pltpu.VMEM((tc, dp), F32)] * 5,
        args=(d_y_pool, p, pool_w, pool_scale))[0]


def _win_bwd_norm(parts, w_int, dx2, x, g1, carry=None):
    t, d = x.shape
    tk = 512
    tt = _tile(t, 1024)
    bounds = []
    k0 = 0
    for part in parts:
        assert part.shape[1] % tk == 0
        bounds.append((k0, k0 + part.shape[1] // tk))
        k0 += part.shape[1] // tk
    nk = k0
    assert nk * tk == w_int.shape[0]
    np_ = len(parts)

    def body(*refs):
        p_refs = refs[:np_]
        w_ref, dx2_ref, x_ref, g_ref, gx_ref, dg_ref, acc = refs[np_:]
        i, kk = pl.program_id(0), pl.program_id(1)

        @pl.when(kk == 0)
        def _():
            acc[...] = jnp.zeros_like(acc)

        @pl.when((i == 0) & (kk == 0))
        def _():
            dg_ref[...] = jnp.zeros_like(dg_ref)

        for (lo, hi), p_ref in zip(bounds, p_refs):
            @pl.when((kk >= lo) & (kk < hi))
            def _(p_ref=p_ref):
                acc[...] += _dot_nn(p_ref[...], w_ref[...])

        @pl.when(kk == nk - 1)
        def _():
            def tail(rows):
                xhat, r = _rms_hat(x_ref[rows, :])
                dx, dg = _rms_bwd(acc[rows, :], xhat, r, g_ref[...])
                gx_ref[rows, :] = dx2_ref[rows, :] + dx
                dg_ref[...] += dg

            _row_chunks(tt, tail)

    def part_spec(lo, hi):
        return pl.BlockSpec((tt, tk), lambda i, kk: (i, jnp.clip(kk - lo, 0, hi - lo - 1)))

    row = pl.BlockSpec((tt, d), lambda i, kk: (i, 0))
    vec = pl.BlockSpec((1, d), lambda i, kk: (0, 0))
    return _call(
        body, name="win_bwd_norm", grid=(t // tt, nk),
        in_specs=[part_spec(lo, hi) for lo, hi in bounds]
        + [pl.BlockSpec((tk, d), lambda i, kk: (kk, 0)), row, row, vec],
        out_specs=[row, vec],
        out_shape=[jax.ShapeDtypeStruct((t, d), F32), jax.ShapeDtypeStruct((1, d), F32)],
        scratch_shapes=[pltpu.VMEM((tt, d), F32)],
        args=(*parts, w_int, dx2, x, g1), carry=carry)


def _adam_math(w, g, m, v):
    m = ADAM_B1 * m + (1.0 - ADAM_B1) * g
    v = ADAM_B2 * v + (1.0 - ADAM_B2) * (g * g)
    m_hat = m / (1.0 - ADAM_B1 ** ADAM_STEP)
    v_hat = v / (1.0 - ADAM_B2 ** ADAM_STEP)
    delta = -ADAM_LR * (m_hat / (jnp.sqrt(v_hat) + ADAM_EPS) + ADAM_WD * w)
    return delta, m, v


def _adamw_big(ws, gs, ms, vs, carry=None):
    n = len(ws)
    nb = 8

    def body(*refs):
        for a in range(n):
            w_ref, g_ref, m_ref, v_ref = refs[4 * a:4 * a + 4]
            d_ref, nm_ref, nv_ref = refs[4 * n + 3 * a:4 * n + 3 * a + 3]
            dl, m, v = _adam_math(w_ref[...], g_ref[...], m_ref[...], v_ref[...])
            d_ref[...] = dl
            nm_ref[...] = m
            nv_ref[...] = v

    in_specs, out_specs, out_shape, args = [], [], [], []
    for w, g, m, v in zip(ws, gs, ms, vs):
        rows, cols = w.shape
        blk = pl.BlockSpec((rows // nb, cols), lambda i: (i, 0))
        in_specs += [blk] * 4
        args += [w, g, m, v]
        out_specs += [blk] * 3
        out_shape += [jax.ShapeDtypeStruct(w.shape, F32)] * 3
    outs, got = _call(body, name="adamw_big", grid=(nb,), in_specs=in_specs, out_specs=out_specs,
                      out_shape=out_shape, args=args, carry=carry)
    return [tuple(outs[3 * a:3 * a + 3]) for a in range(n)], got


SMALL_ORDER = ("norm_mix_pre", "norm_mix_post", "norm_mlp_pre", "norm_mlp_post", "b_gate", "conv_w", "conv_b",
               "lru_w_a", "lru_b_a", "lru_w_x", "lru_b_x", "lru_lambda", "pool_w", "pool_scale")
VEC_ROW = dict(norm_mix_pre=0, norm_mix_post=1, norm_mlp_pre=2, norm_mlp_post=3, conv_b=6, lru_b_a=7,
               lru_b_x=8, lru_lambda=9)
ROW_B_GATE, ROW_POOL_SCALE, ROW_CONV_W, ROW_LOSS, N_VEC_ROWS = 4, 10, 11, 15, 16


def _adamw_small(vec_parts, g_pool, g_wa, g_wx, me, params):
    d = vec_parts.shape[2]
    names = SMALL_ORDER
    n = len(names)
    cw_cols = params["conv_w"][0].shape[2]

    def body(me_ref, vec_ref, vecc_ref, gp_ref, gwa_ref, gwx_ref, *refs):
        wmv = refs[:3 * n]
        loss_ref = refs[3 * n]
        outs = refs[3 * n + 1:3 * n + 1 + 4 * n]
        vs, vsc = refs[3 * n + 1 + 4 * n:]
        acc, accc = vec_ref[0], vecc_ref[0]
        for k in range(1, N_DEV):
            acc = acc + vec_ref[k]
            accc = accc + vecc_ref[k]
        vs[...] = acc
        vsc[...] = accc
        loss_ref[...] = vs[ROW_LOSS:ROW_LOSS + 1, 0:128]

        def upd(a, g, idx):
            w_ref, m_ref, v_ref = wmv[3 * a:3 * a + 3]
            g_ref, d_ref, nm_ref, nv_ref = outs[4 * a:4 * a + 4]
            dl, m, v = _adam_math(w_ref[idx], g, m_ref[idx], v_ref[idx])
            g_ref[idx] = g
            d_ref[idx] = dl
            nm_ref[idx] = m
            nv_ref[idx] = v

        for a, name in enumerate(names):
            if name in VEC_ROW:
                r = VEC_ROW[name]
                upd(a, vs[r:r + 1, :], (slice(None), slice(None)))
            elif name == "b_gate":
                for half in range(2):
                    r = ROW_B_GATE + half
                    upd(a, vs[r:r + 1, :], (slice(None), slice(half * d, (half + 1) * d)))
            elif name == "pool_scale":
                width = params[name][0].shape[1]
                upd(a, vs[ROW_POOL_SCALE:ROW_POOL_SCALE + 1, 0:width], (slice(None), slice(None)))
            elif name == "conv_w":
                upd(a, vsc[ROW_CONV_W:ROW_CONV_W + 4, :], (0,))
            elif name == "pool_w":
                upd(a, gp_ref[...], (Ellipsis,))
            elif name == "lru_w_a":
                upd(a, gwa_ref[...], (Ellipsis,))
            elif name == "lru_w_x":
                upd(a, gwx_ref[...], (Ellipsis,))
            else:
                raise ValueError(name)

    def whole(shape):
        nd = len(shape)
        return pl.BlockSpec(tuple(shape), lambda i, me_ref: (0,) * nd)

    in_specs = [
        whole(vec_parts.shape),
        pl.BlockSpec((N_DEV, N_VEC_ROWS, cw_cols), lambda i, me_ref: (0, 0, me_ref[0])),
        whole(g_pool.shape), whole(g_wa.shape), whole(g_wx.shape),
    ]
    args = [vec_parts, vec_parts, g_pool, g_wa, g_wx]
    out_specs = [whole((1, 128))]
    out_shape = [jax.ShapeDtypeStruct((1, 128), F32)]
    for name in names:
        for arr in params[name]:
            in_specs.append(whole(arr.shape))
            args.append(arr)
        shp = params[name][0].shape
        out_specs += [whole(shp)] * 4
        out_shape += [jax.ShapeDtypeStruct(shp, F32)] * 4
    grid_spec = pltpu.PrefetchScalarGridSpec(
        num_scalar_prefetch=1, grid=(1,), in_specs=in_specs, out_specs=out_specs,
        scratch_shapes=[pltpu.VMEM((N_VEC_ROWS, d), F32), pltpu.VMEM((N_VEC_ROWS, cw_cols), F32)])
    outs = pl.pallas_call(
        body, name="adamw_small", grid_spec=grid_spec, out_shape=out_shape,
        compiler_params=pltpu.CompilerParams(
            dimension_semantics=("arbitrary",), vmem_limit_bytes=V7X_VMEM_LIMIT_BYTES),
    )(me, *_in_hbm(args))
    return outs[0], {name: tuple(outs[1 + 4 * a:5 + 4 * a]) for a, name in enumerate(names)}


def _rs_sum(fulls, recvs, shard_ids, slot_ids, name):
    n = len(fulls)

    def body(sh_ref, sl_ref, *refs):
        s = pl.program_id(0)
        for a in range(n):
            full_ref, recv_ref = refs[2 * a], refs[2 * a + 1]
            own_ref, send_ref = refs[2 * n + 2 * a], refs[2 * n + 2 * a + 1]
            v = full_ref[...] + recv_ref[...].astype(F32)

            @pl.when(s == 0)
            def _(own_ref=own_ref, v=v):
                own_ref[...] = v

            @pl.when(s > 0)
            def _(send_ref=send_ref, v=v):
                send_ref[...] = v.astype(send_ref.dtype)

    in_specs, out_specs, out_shape, args = [], [], [], []
    for full, recv in zip(fulls, recvs):
        r, rest = recv.shape[1], tuple(recv.shape[2:])
        zeros = (0,) * len(rest)
        in_specs += [
            pl.BlockSpec((r,) + rest, lambda s, sh, sl, zeros=zeros: (sh[s],) + zeros),
            pl.BlockSpec((None, r) + rest, lambda s, sh, sl, zeros=zeros: (sl[s], 0) + zeros),
        ]
        out_specs += [
            pl.BlockSpec((None, r) + rest, lambda s, sh, sl, zeros=zeros: (0, 0) + zeros),
            pl.BlockSpec((None, r) + rest, lambda s, sh, sl, zeros=zeros: (jnp.maximum(s - 1, 0), 0) + zeros),
        ]
        out_shape += [jax.ShapeDtypeStruct((1, r) + rest, F32), jax.ShapeDtypeStruct((3, r) + rest, recv.dtype)]
        args += [full, recv]
    grid_spec = pltpu.PrefetchScalarGridSpec(
        num_scalar_prefetch=2, grid=(4,), in_specs=in_specs, out_specs=out_specs)
    outs = pl.pallas_call(
        body,
        name=name,
        grid_spec=grid_spec,
        out_shape=out_shape,
        compiler_params=pltpu.CompilerParams(
            dimension_semantics=("arbitrary",), vmem_limit_bytes=V7X_VMEM_LIMIT_BYTES),
    )(shard_ids, slot_ids, *_in_hbm(args))
    return [(outs[2 * a], outs[2 * a + 1]) for a in range(n)]


def _finals(pairs, name, carry=None):
    nb = 4
    n = len(pairs)

    def body(*refs):
        for a in range(n):
            own_ref, recv_ref = refs[2 * a], refs[2 * a + 1]
            acc = own_ref[...]
            for k in range(3):
                acc = acc + recv_ref[k].astype(F32)
            refs[2 * n + a][...] = acc

    in_specs, out_specs, out_shape, args = [], [], [], []
    for own, recv in pairs:
        _, rows, cols = own.shape
        in_specs += [pl.BlockSpec((None, rows // nb, cols), lambda i: (0, i, 0)),
                     pl.BlockSpec((3, rows // nb, cols), lambda i: (0, i, 0))]
        args += [own, recv]
        out_specs.append(pl.BlockSpec((rows // nb, cols), lambda i: (i, 0)))
        out_shape.append(jax.ShapeDtypeStruct((rows, cols), F32))
    return _call(body, name=name, grid=(nb,), in_specs=in_specs, out_specs=out_specs,
                 out_shape=out_shape, args=args, carry=carry)


def _rs_sums(fulls_f32, recv1, tag):
    x, y, c = _place()
    qs = jnp.stack([2 * x + y, 2 * (1 - x) + y, 2 * x + (1 - y), 2 * (1 - x) + (1 - y)]).astype(jnp.int32)
    shard_ids = 2 * qs + c
    return _rs_sum(fulls_f32, recv1, shard_ids, qs, "rs_sum_" + tag)


def _rs_level1(fulls_f32, fulls_send, tag):
    recv1 = _run_plan(_rs_sibling_plan(fulls_send), "rs_sibling_" + tag)
    return _rs_sums(fulls_f32, recv1, tag)


def _rows(g):
    return g.reshape(g.shape[0] * g.shape[1], g.shape[2])


def kernel(x, norm_mix_pre, norm_mix_post, norm_mlp_pre, norm_mlp_post, w_in, b_gate, conv_w, conv_b, lru_w_a, lru_b_a, lru_w_x, lru_b_x, lru_lambda, pool_w, pool_scale, w_lru_up, w_pool_up, w_o, w_ff1, w_ff2, loss_target, m_norm_mix_pre, m_norm_mix_post, m_norm_mlp_pre, m_norm_mlp_post, m_w_in, m_b_gate, m_conv_w, m_conv_b, m_lru_w_a, m_lru_b_a, m_lru_w_x, m_lru_b_x, m_lru_lambda, m_pool_w, m_pool_scale, m_w_lru_up, m_w_pool_up, m_w_o, m_w_ff1, m_w_ff2, v_norm_mix_pre, v_norm_mix_post, v_norm_mlp_pre, v_norm_mlp_post, v_w_in, v_b_gate, v_conv_w, v_conv_b, v_lru_w_a, v_lru_b_a, v_lru_w_x, v_lru_b_x, v_lru_lambda, v_pool_w, v_pool_scale, v_w_lru_up, v_w_pool_up, v_w_o, v_w_ff1, v_w_ff2):
    t, d = x.shape[1], x.shape[2]
    d_rnn = conv_b.shape[1]
    d_pool = pool_scale.shape[1]
    per = LRU_CB // LRU_HEAD_DIM
    xi, yi, ci = _place()
    me = 4 * xi + 2 * yi + ci

    x2d = x[0]
    tgt = loss_target[0]

    s_in = w_in[0].T.astype(BF16)
    s_lu = w_lru_up[0].astype(BF16)
    s_pu = w_pool_up[0].T.astype(BF16)
    s_o = w_o[0].astype(BF16)
    s_f1 = w_ff1[0].T.astype(BF16)
    s_f2 = w_ff2[0].astype(BF16)
    s_cw = jnp.pad(conv_w[0], ((0, 4), (0, 0)))

    wa_bd, wx_bd = lru_w_a[0], lru_w_x[0]
    pw = pool_w[0]
    pw_bf = pw.astype(BF16)

    pool_block = (2 * d_rnn) // d_pool
    ga_block = (2 * d_rnn + d_pool) // 512
    gb_block = ga_block + d // 512
    g_block = d_rnn // 512

    r_f1, r_f2 = s_f1.shape[0], s_f2.shape[0]
    f1_cut = r_f1 // 4
    f2_cut = (3 * r_f2) // 8
    plan = _join([_ag_plan([s_lu, s_pu, s_o]), _ag_plan([s_f1], pieces=[(0, f1_cut)])])
    proj, h1, g_in, g_cw, got = _norm_proj_stream(x2d, norm_mix_pre, s_in, s_cw, plan)
    w_int = _rows(g_in)
    conv_w_full = jnp.transpose(g_cw[:, :4, :], (1, 0, 2)).reshape(4, d_rnn)
    (g_lu, g_pu, g_o), (g_f1,) = plan.split(got)
    w_lu, w_put, w_og = _rows(g_lu), _rows(g_pu), _rows(g_o)
    (y_lru, h, xc), (g_f1,) = _lru_fwd(
        proj, conv_w_full, conv_b, wa_bd, lru_b_a, wx_bd, lru_b_x, lru_lambda,
        carry=_ag_plan([s_f1], pieces=[(f1_cut, r_f1 - f1_cut)], bufs=[g_f1]))
    w_f1t = _rows(g_f1)
    y_pool, p = _pool_fwd(proj, pw_bf, pool_scale, pool_block)
    (br_a, br_b, mix), (g_f2,) = _branch_mix(
        y_lru, y_pool, w_lu, w_put, proj, b_gate, ga_block, gb_block,
        carry=_ag_plan([s_f2], pieces=[(0, f2_cut)]))
    (m, x2, h3), (g_f2,) = _wo_norm(
        mix, w_og, x2d, norm_mix_post, norm_mlp_pre,
        carry=_ag_plan([s_f2], pieces=[(f2_cut, r_f2 // 2 - f2_cut)], bufs=[g_f2]))
    (rf,), (g_f2,) = _ff1(
        h3, w_f1t, carry=_ag_plan([s_f2], pieces=[(r_f2 // 2, r_f2 - r_f2 // 2)], bufs=[g_f2]))
    w_f2 = _rows(g_f2)
    dy, df, dg4, loss_part = _ff2_loss(rf, w_f2, x2, norm_mlp_post, tgt)

    (gw_ff2_32, gw_ff2_16), _ = _wgrad(rf, df, "wgrad_ff2", square_a=True)
    (d_f1,), r1_ff2 = _ff2_bwd(df, w_f2, rf, carry=_rs_sibling_plan([gw_ff2_16]))
    ((own_ff2, send_ff2),) = _rs_sums([gw_ff2_32], r1_ff2, "ff2")
    cut2 = (5 * send_ff2.shape[1]) // 16
    (gw_ff1_32, gw_ff1_16), (r2_ff2,) = _wgrad(
        d_f1, h3, "wgrad_ff1", carry=_rs_chips_plan([send_ff2], pieces=[(0, cut2)]))
    plan = _join([_rs_chips_plan([send_ff2], pieces=[(cut2, send_ff2.shape[1] - cut2)], bufs=[r2_ff2]),
                  _rs_sibling_plan([gw_ff1_16])])
    (dx2, dm, dg3, dg2), got = _ff1_bwd_norms(d_f1, w_f1t, dy, x2, norm_mlp_pre, m, norm_mix_post, carry=plan)
    (r2_ff2,), r1_ff1 = plan.split(got)
    ((own_ff1, send_ff1),) = _rs_sums([gw_ff1_32], r1_ff1, "ff1")
    cut = send_ff1.shape[1] // 4
    (gw_o_32, gw_o_16), _ = _wgrad(mix, dm, "wgrad_o")
    (d_br_a, d_br_b, p_ga, p_gb, dbg_a, dbg_b), (r2_ff1,) = _wo_bwd_mix(
        dm, w_og, br_a, br_b, proj, b_gate, ga_block, gb_block,
        carry=_rs_chips_plan([send_ff1], pieces=[(0, cut)]))
    (gw_lu_32, gw_lu_16), _ = _wgrad(y_lru, d_br_a, "wgrad_lru_up")
    (gw_pu_32, gw_pu_16), _ = _wgrad(d_br_b, y_pool, "wgrad_pool_up")
    (dh, p_g), r1_mid = _lru_up_bwd(
        d_br_a, w_lu, proj, h, g_block,
        carry=_rs_sibling_plan([gw_o_16, gw_lu_16, gw_pu_16.reshape(-1, d)]))
    mid = _rs_sums([gw_o_32, gw_lu_32, gw_pu_32.reshape(-1, d)], r1_mid, "mid")
    d_y_pool = _pool_up_bwd(d_br_b, w_put)
    (p_x, dwa, db_a, dwx, db_x, dlam, dconv_w, dconv_b), (r2_ff1,) = _lru_bwd(
        dh, xc, h, proj, conv_w_full, wa_bd, lru_b_a, wx_bd, lru_b_x, lru_lambda,
        carry=_rs_chips_plan([send_ff1], pieces=[(cut, send_ff1.shape[1] - cut)], bufs=[r2_ff1]))
    p_p, dpool_w, dpool_scale = _pool_bwd(d_y_pool, p, pw, pool_scale)
    parts = [p_x, p_g, p_p, p_ga, p_gb]
    gw_in, r2_mid = _wgrad_parts(parts, h1, "wgrad_in", carry=_rs_chips_plan([s for _, s in mid]))
    tail = _rs_level1([gw_in[0], dpool_w.reshape(N_DEV, -1, POOL_GROUP_DIM), dwa, dwx],
                      [gw_in[1], dpool_w.reshape(N_DEV, -1, POOL_GROUP_DIM), dwa, dwx], "in")
    (grad_x, dg1), r2_tail = _win_bwd_norm(parts, w_int, dx2, x2d, norm_mix_pre,
                                           carry=_rs_chips_plan([s for _, s in tail]))

    def flat2(a):
        return a.reshape(a.shape[0], -1, a.shape[-1])

    fin_small, _ = _finals([
        (flat2(tail[1][0]), flat2(r2_tail[1])), (flat2(tail[2][0]), flat2(r2_tail[2])),
        (flat2(tail[3][0]), flat2(r2_tail[3])),
    ], "rs_finals_small")

    def pad_row(a):
        return jnp.pad(a, ((0, 0), (0, d - a.shape[1])))

    vecs = jnp.concatenate([dg1, dg2, dg3, dg4, dbg_a, dbg_b, dconv_b, db_a, db_x, dlam,
                            pad_row(dpool_scale), dconv_w, pad_row(loss_part)], axis=0)
    assert vecs.shape[0] == N_VEC_ROWS
    fin, (vec_parts, g_pool, g_wa, g_wx) = _finals([
        (tail[0][0], r2_tail[0]), (mid[1][0], r2_mid[1]), (mid[2][0], r2_mid[2]), (mid[0][0], r2_mid[0]),
        (own_ff1, r2_ff1), (own_ff2, r2_ff2),
    ], "rs_finals", carry=_ag_plan([vecs] + fin_small))
    g_w_in = fin[0].T
    g_w_lru_up = fin[1]
    g_w_pool_up = fin[2].reshape(d // N_DEV, d_pool).T
    g_w_o = fin[3]
    g_w_ff1 = fin[4].T
    g_w_ff2 = fin[5]

    big_names = ["w_in", "w_lru_up", "w_pool_up", "w_o", "w_ff1", "w_ff2"]
    big_w = [w_in, w_lru_up, w_pool_up, w_o, w_ff1, w_ff2]
    big_g = [g_w_in, g_w_lru_up, g_w_pool_up, g_w_o, g_w_ff1, g_w_ff2]
    big_m = [m_w_in, m_w_lru_up, m_w_pool_up, m_w_o, m_w_ff1, m_w_ff2]
    big_v = [v_w_in, v_w_lru_up, v_w_pool_up, v_w_o, v_w_ff1, v_w_ff2]
    big_out, _ = _adamw_big([w[0] for w in big_w], big_g, [mm[0] for mm in big_m], [vv[0] for vv in big_v])

    small = dict(
        norm_mix_pre=(norm_mix_pre, m_norm_mix_pre, v_norm_mix_pre),
        norm_mix_post=(norm_mix_post, m_norm_mix_post, v_norm_mix_post),
        norm_mlp_pre=(norm_mlp_pre, m_norm_mlp_pre, v_norm_mlp_pre),
        norm_mlp_post=(norm_mlp_post, m_norm_mlp_post, v_norm_mlp_post),
        b_gate=(b_gate, m_b_gate, v_b_gate), conv_w=(conv_w, m_conv_w, v_conv_w),
        conv_b=(conv_b, m_conv_b, v_conv_b), lru_w_a=(lru_w_a, m_lru_w_a, v_lru_w_a),
        lru_b_a=(lru_b_a, m_lru_b_a, v_lru_b_a), lru_w_x=(lru_w_x, m_lru_w_x, v_lru_w_x),
        lru_b_x=(lru_b_x, m_lru_b_x, v_lru_b_x), lru_lambda=(lru_lambda, m_lru_lambda, v_lru_lambda),
        pool_w=(pool_w, m_pool_w, v_pool_w), pool_scale=(pool_scale, m_pool_scale, v_pool_scale))
    loss_row, small_out = _adamw_small(
        vec_parts, g_pool.reshape(pool_w.shape), g_wa.reshape(lru_w_a.shape), g_wx.reshape(lru_w_x.shape),
        jnp.reshape(me, (1,)).astype(jnp.int32), small)
    grads = {n: o[0] for n, o in small_out.items()}
    delta = {n: o[1] for n, o in small_out.items()}
    new_m = {n: o[2] for n, o in small_out.items()}
    new_v = {n: o[3] for n, o in small_out.items()}

    for name, g, (dl, nm, nv) in zip(big_names, big_g, big_out):
        grads[name], delta[name], new_m[name], new_v[name] = g[None], dl[None], nm[None], nv[None]

    loss = loss_row[0, 0]
    order = ["norm_mix_pre", "norm_mix_post", "norm_mlp_pre", "norm_mlp_post", "w_in", "b_gate", "conv_w",
             "conv_b", "lru_w_a", "lru_b_a", "lru_w_x", "lru_b_x", "lru_lambda", "pool_w", "pool_scale",
             "w_lru_up", "w_pool_up", "w_o", "w_ff1", "w_ff2"]
    return (loss, grad_x[None], *[grads[n] for n in order], *[delta[n] for n in order],
            *[new_m[n] for n in order], *[new_v[n] for n in order])
```

```python
import functools
import math
import operator
import types

import jax
import jax.numpy as jnp
from jax import lax
from jax.experimental import pallas as pl
from jax.experimental.pallas import tpu as pltpu

F32 = jnp.float32
BF16 = jnp.bfloat16
NORM_EPS = 1e-6
LRU_C = 8.0
N_LRU_HEADS = 16
LRU_HEAD_DIM = 64
POOL_WINDOWS = (2, 4, 8, 16)
POOL_GROUP_DIM = 128
ADAM_LR = 0.001
ADAM_B1 = 0.9
ADAM_B2 = 0.999
ADAM_EPS = 1e-08
ADAM_WD = 0.01
ADAM_STEP = 10
N_DEV = 8
V7X_VMEM_LIMIT_BYTES = 56 * 1024 * 1024
LRU_CB = 256
MESH = pl.DeviceIdType.MESH
ANY = pl.BlockSpec(memory_space=pl.ANY)


def _tile(n, pref):
    t = min(n, pref)
    assert n % t == 0, (n, pref)
    return t


def _dot_nn(a, b):
    return lax.dot_general(a, b, (((1,), (0,)), ((), ())), preferred_element_type=F32)


def _dot_nt(a, b):
    return lax.dot_general(a, b, (((1,), (1,)), ((), ())), preferred_element_type=F32)


def _dot_tn(a, b):
    return lax.dot_general(a, b, (((0,), (0,)), ((), ())), preferred_element_type=F32)


def _row_chunks(n_rows, fn, chunk=256):
    chunk = min(chunk, n_rows)
    assert n_rows % chunk == 0

    def step(r, carry):
        fn(pl.ds(pl.multiple_of(r * chunk, chunk), chunk))
        return carry

    lax.fori_loop(0, n_rows // chunk, step, 0)


def _sig(x):
    return 1.0 / (1.0 + jnp.exp(-x))


def _rms_hat(x):
    r = lax.rsqrt(jnp.mean(x * x, axis=-1, keepdims=True) + NORM_EPS)
    return x * r, r


def _rms_bwd(dn, xhat, r, g):
    q = dn * g
    dx = r * (q - xhat * jnp.mean(q * xhat, axis=-1, keepdims=True))
    dg = jnp.sum(dn * xhat, axis=0, keepdims=True)
    return dx, dg


_GELU_K = math.sqrt(2.0 / math.pi)
_GELU_C = 0.044715


def _gelu_and_grad(g):
    t = jnp.tanh(_GELU_K * (g + _GELU_C * g * g * g))
    val = 0.5 * g * (1.0 + t)
    grad = 0.5 * (1.0 + t) + 0.5 * g * (1.0 - t * t) * (_GELU_K * (1.0 + 3.0 * _GELU_C * g * g))
    return val, grad


def _softplus_neg(lam):
    z = -lam
    e = jnp.exp(-jnp.abs(z))
    u = 1.0 + e
    d = u - 1.0
    l1p = jnp.where(d == 0.0, e, jnp.log(u) * (e / jnp.where(d == 0.0, 1.0, d)))
    return jnp.maximum(z, 0.0) + l1p


def _lru_gates(xc, wa, ba, wx, bx, lam):
    xcb = xc.astype(BF16)
    r = _sig(_dot_nn(xcb, wa) + ba)
    i = _sig(_dot_nn(xcb, wx) + bx)
    sp = _softplus_neg(lam)
    log_a = (-LRU_C) * r * sp
    a = jnp.exp(log_a)
    mult = jnp.sqrt(-jnp.tanh(log_a) * (1.0 + a * a))
    return xcb, r, i, sp, log_a, a, mult


def _place():
    return lax.axis_index("x"), lax.axis_index("y"), lax.axis_index("c")


def _ag_plan(shards, pieces=None, bufs=None):
    na = len(shards)
    n_kinds = 7

    def parts(ins, outs, sems):
        send_sems, recv_sems, local_sems = sems
        x, y, c = _place()
        me, sibling = (x, y, c), (x, y, 1 - c)
        x_nb, y_nb, diag = (1 - x, y), (x, 1 - y), (1 - x, 1 - y)
        relay_src = (c * (1 - x) + (1 - c) * x, c * y + (1 - c) * (1 - y))
        relay_dst = (c * x + (1 - c) * (1 - x), c * (1 - y) + (1 - c) * y)

        def own(a):
            return ins[a] if pieces is None else ins[a].at[pl.ds(*pieces[a])]

        def slot(a, px, py, pc):
            idx = 4 * px + 2 * py + pc
            return outs[a].at[idx] if pieces is None else outs[a].at[idx, pl.ds(*pieces[a])]

        def copy(a, k, block, to, src=None):
            return pltpu.make_async_remote_copy(
                src_ref=slot(a, *block) if src is None else src,
                dst_ref=slot(a, *block),
                send_sem=send_sems.at[a * n_kinds + k],
                recv_sem=recv_sems.at[a * n_kinds + k],
                device_id=to,
                device_id_type=MESH,
            )

        mine = [pltpu.make_async_copy(own(a), slot(a, *me), local_sems.at[a]) for a in range(na)]
        first, second, third = [], [], []
        for a in range(na):
            first += [copy(a, 0, me, sibling, src=own(a)), copy(a, 1, me, (*x_nb, c), src=own(a)),
                      copy(a, 2, me, (*y_nb, c), src=own(a))]
            second += [copy(a, 3, (*relay_src, c), (*relay_dst, c)), copy(a, 4, (*x_nb, c), sibling),
                       copy(a, 5, (*y_nb, c), sibling)]
            third.append(copy(a, 6, (*diag, c), sibling))
        return sibling, c, x_nb, y_nb, diag, copy, mine, first, second, third

    def start(ins, outs, sems):
        _, _, _, _, _, _, mine, first, _, _ = parts(ins, outs, sems)
        for cp in mine + first:
            cp.start()

    def middle(ins, outs, sems):
        _, c, x_nb, y_nb, _, copy, _, _, second, _ = parts(ins, outs, sems)
        for a in range(na):
            copy(a, 1, (*x_nb, c), (*x_nb, c)).wait_recv()
            copy(a, 2, (*y_nb, c), (*y_nb, c)).wait_recv()
        for cp in second:
            cp.start()

    def finish(ins, outs, sems):
        sibling, c, x_nb, y_nb, diag, copy, mine, first, second, third = parts(ins, outs, sems)
        for a in range(na):
            copy(a, 3, (*diag, c), (*diag, c)).wait_recv()
            third[a].start()
        for a in range(na):
            copy(a, 0, sibling, sibling).wait_recv()
            copy(a, 4, (*x_nb, 1 - c), sibling).wait_recv()
            copy(a, 5, (*y_nb, 1 - c), sibling).wait_recv()
            copy(a, 6, (*diag, 1 - c), sibling).wait_recv()
        for cp in first + second + third:
            cp.wait_send()
        for cp in mine:
            cp.wait()

    return types.SimpleNamespace(
        ins=list(shards) + list(bufs or []),
        out_shapes=[jax.ShapeDtypeStruct((N_DEV,) + s.shape, s.dtype) for s in shards],
        sems=[pltpu.SemaphoreType.DMA((n_kinds * na,)), pltpu.SemaphoreType.DMA((n_kinds * na,)),
              pltpu.SemaphoreType.DMA((na,))],
        aliases=[(na + a, a) for a in range(na)] if bufs else [],
        peers=frozenset({"sibling", "neighbours"}), start=start, middle=middle, finish=finish)


def _rs_sibling_plan(fulls):
    na = len(fulls)
    rs = [f.shape[0] // N_DEV for f in fulls]

    def copies(ins, outs, sems):
        send_sems, recv_sems = sems
        x, y, c = _place()
        out = []
        for a in range(na):
            for q in range(4):
                shard = 2 * q + (1 - c)
                out.append(pltpu.make_async_remote_copy(
                    src_ref=ins[a].at[pl.ds(shard * rs[a], rs[a])],
                    dst_ref=outs[a].at[q],
                    send_sem=send_sems.at[a * 4 + q],
                    recv_sem=recv_sems.at[a * 4 + q],
                    device_id=(x, y, 1 - c),
                    device_id_type=MESH,
                ))
        return out

    def start(ins, outs, sems):
        for cp in copies(ins, outs, sems):
            cp.start()

    def finish(ins, outs, sems):
        for cp in copies(ins, outs, sems):
            cp.wait()

    return types.SimpleNamespace(
        ins=list(fulls),
        out_shapes=[jax.ShapeDtypeStruct((4, r) + f.shape[1:], f.dtype) for r, f in zip(rs, fulls)],
        sems=[pltpu.SemaphoreType.DMA((4 * na,)), pltpu.SemaphoreType.DMA((4 * na,))],
        peers=frozenset({"sibling"}), start=start, finish=finish)


def _rs_chips_plan(sends, pieces=None, bufs=None):
    na = len(sends)

    def copies(ins, outs, sems):
        send_sems, recv_sems = sems
        x, y, c = _place()
        chips = [(1 - x, y), (x, 1 - y), (1 - x, 1 - y)]
        out = []
        for a in range(na):
            for k, chip in enumerate(chips):
                rows = (k,) if pieces is None else (k, pl.ds(*pieces[a]))
                out.append(pltpu.make_async_remote_copy(
                    src_ref=ins[a].at[rows],
                    dst_ref=outs[a].at[rows],
                    send_sem=send_sems.at[a * 3 + k],
                    recv_sem=recv_sems.at[a * 3 + k],
                    device_id=(*chip, c),
                    device_id_type=MESH,
                ))
        return out

    def start(ins, outs, sems):
        for cp in copies(ins, outs, sems):
            cp.start()

    def finish(ins, outs, sems):
        for cp in copies(ins, outs, sems):
            cp.wait()

    return types.SimpleNamespace(
        ins=list(sends) + list(bufs or []),
        out_shapes=[jax.ShapeDtypeStruct(s.shape, s.dtype) for s in sends],
        sems=[pltpu.SemaphoreType.DMA((3 * na,)), pltpu.SemaphoreType.DMA((3 * na,))],
        aliases=[(na + a, a) for a in range(na)] if bufs else [],
        peers=frozenset({"chips"}), start=start, finish=finish)


def _join(plans):
    ins, outs, sems, aliases, offs = [], [], [], [], []
    for p in plans:
        offs.append((len(ins), len(outs), len(sems)))
        aliases += [(len(ins) + ci, len(outs) + co) for ci, co in getattr(p, "aliases", [])]
        ins += p.ins
        outs += p.out_shapes
        sems += p.sems

    def cut(p, off, i, o, s):
        return (i[off[0]:off[0] + len(p.ins)], o[off[1]:off[1] + len(p.out_shapes)],
                s[off[2]:off[2] + len(p.sems)])

    def start(i, o, s):
        for p, off in zip(plans, offs):
            p.start(*cut(p, off, i, o, s))

    def middle(i, o, s):
        for p, off in zip(plans, offs):
            if getattr(p, "middle", None) is not None:
                p.middle(*cut(p, off, i, o, s))

    def finish(i, o, s):
        for p, off in zip(plans, offs):
            p.finish(*cut(p, off, i, o, s))

    def split(results):
        return [list(results[off[1]:off[1] + len(p.out_shapes)]) for p, off in zip(plans, offs)]

    return types.SimpleNamespace(ins=ins, out_shapes=outs, sems=sems, aliases=aliases,
                                 peers=frozenset().union(*[p.peers for p in plans]),
                                 start=start, middle=middle, finish=finish, split=split)


COLLECTIVE_ID = {frozenset({"sibling"}): 0, frozenset({"chips"}): 1, frozenset({"sibling", "chips"}): 2,
                 frozenset({"sibling", "neighbours"}): 3}


def _handshake(peers):
    x, y, c = _place()
    devs = []
    if "sibling" in peers:
        devs.append((x, y, 1 - c))
    if "neighbours" in peers:
        devs += [(1 - x, y, c), (x, 1 - y, c)]
    if "chips" in peers:
        assert "neighbours" not in peers
        devs += [(1 - x, y, c), (x, 1 - y, c), (1 - x, 1 - y, c)]
    barrier = pltpu.get_barrier_semaphore()
    for dev in devs:
        pl.semaphore_signal(barrier, inc=1, device_id=dev, device_id_type=MESH)
    pl.semaphore_wait(barrier, len(devs))


def _in_hbm(args):
    return [pltpu.with_memory_space_constraint(a, pltpu.HBM) for a in args]


def _run_plan(plan, name):
    n_in, n_out = len(plan.ins), len(plan.out_shapes)

    def body(*refs):
        ins, outs, sems = refs[:n_in], refs[n_in:n_in + n_out], refs[n_in + n_out:]
        _handshake(plan.peers)
        plan.start(ins, outs, sems)
        if getattr(plan, "middle", None) is not None:
            plan.middle(ins, outs, sems)
        plan.finish(ins, outs, sems)

    return pl.pallas_call(
        body,
        name=name,
        in_specs=[ANY] * n_in,
        out_specs=[ANY] * n_out,
        out_shape=plan.out_shapes,
        scratch_shapes=plan.sems,
        input_output_aliases=dict(getattr(plan, "aliases", [])),
        compiler_params=pltpu.CompilerParams(collective_id=COLLECTIVE_ID[plan.peers]),
    )(*_in_hbm(plan.ins))


def _call(body, *, name, grid, in_specs, out_specs, out_shape, args, scratch_shapes=(), aliases=None,
          carry=None):
    n_in, n_out, n_scr = len(in_specs), len(out_shape), len(scratch_shapes)
    params = pltpu.CompilerParams(
        dimension_semantics=("arbitrary",) * len(grid), vmem_limit_bytes=V7X_VMEM_LIMIT_BYTES)
    if carry is None:
        outs = pl.pallas_call(
            body, name=name, grid=grid, in_specs=list(in_specs), out_specs=list(out_specs),
            out_shape=list(out_shape), scratch_shapes=list(scratch_shapes),
            input_output_aliases=aliases or {}, compiler_params=params)(*_in_hbm(args))
        return list(outs), []
    c_in, c_out = len(carry.ins), len(carry.out_shapes)

    def full(*refs):
        p = 0
        ins = refs[p:p + n_in]
        p += n_in
        cins = refs[p:p + c_in]
        p += c_in
        outs = refs[p:p + n_out]
        p += n_out
        couts = refs[p:p + c_out]
        p += c_out
        scr = refs[p:p + n_scr]
        csems = refs[p + n_scr:]
        ids = [pl.program_id(a) for a in range(len(grid))]
        first = functools.reduce(operator.and_, [i == 0 for i in ids])
        last = functools.reduce(operator.and_, [i == g - 1 for i, g in zip(ids, grid)])

        @pl.when(first)
        def _():
            _handshake(carry.peers)
            carry.start(cins, couts, csems)

        if getattr(carry, "middle", None) is not None:
            n_steps = math.prod(grid)
            flat = functools.reduce(lambda acc, ig: acc * ig[1] + ig[0], zip(ids, grid), 0)

            @pl.when(flat == (2 * n_steps) // 3)
            def _():
                carry.middle(cins, couts, csems)

        body(*ins, *outs, *scr)

        @pl.when(last)
        def _():
            carry.finish(cins, couts, csems)

    all_aliases = dict(aliases or {})
    all_aliases.update({n_in + ci: n_out + co for ci, co in getattr(carry, "aliases", [])})
    params = pltpu.CompilerParams(
        dimension_semantics=("arbitrary",) * len(grid), vmem_limit_bytes=V7X_VMEM_LIMIT_BYTES,
        collective_id=COLLECTIVE_ID[carry.peers])
    outs = pl.pallas_call(
        full, name=name, grid=grid,
        in_specs=list(in_specs) + [ANY] * c_in,
        out_specs=list(out_specs) + [ANY] * c_out,
        out_shape=list(out_shape) + list(carry.out_shapes),
        scratch_shapes=list(scratch_shapes) + list(carry.sems),
        input_output_aliases=all_aliases, compiler_params=params)(*_in_hbm(args), *_in_hbm(carry.ins))
    return list(outs[:n_out]), list(outs[n_out:])


def _norm_proj(x, g1, w_int, carry=None):
    t, d = x.shape
    n = w_int.shape[0]
    tt, tn = _tile(t, 2048), _tile(n, 512)

    def body(x_ref, g_ref, w_ref, proj_ref, h1_ref, h1_s):
        @pl.when(pl.program_id(1) == 0)
        def _():
            def norm_rows(rows):
                xhat, _ = _rms_hat(x_ref[rows, :])
                h = (xhat * g_ref[...]).astype(BF16)
                h1_s[rows, :] = h
                h1_ref[rows, :] = h

            _row_chunks(tt, norm_rows)

        proj_ref[...] = _dot_nt(h1_s[...], w_ref[...]).astype(BF16)

    return _call(
        body, name="norm_proj", grid=(t // tt, n // tn),
        in_specs=[
            pl.BlockSpec((tt, d), lambda i, j: (i, 0)),
            pl.BlockSpec((1, d), lambda i, j: (0, 0)),
            pl.BlockSpec((tn, d), lambda i, j: (j, 0)),
        ],
        out_specs=[
            pl.BlockSpec((tt, tn), lambda i, j: (i, j)),
            pl.BlockSpec((tt, d), lambda i, j: (i, 0)),
        ],
        out_shape=[jax.ShapeDtypeStruct((t, n), BF16), jax.ShapeDtypeStruct((t, d), BF16)],
        scratch_shapes=[pltpu.VMEM((tt, d), BF16)],
        args=(x, g1, w_int), carry=carry)


def _scan_rows(av, bv, reverse):
    tc = av.shape[0]
    row = lax.broadcasted_iota(jnp.int32, av.shape, 0)
    s = 1
    while s < tc:
        if s < 8:
            keep = (row < tc - s) if reverse else (row >= s)
            shift = (tc - s) if reverse else s
            a_sh = jnp.where(keep, pltpu.roll(av, shift, 0), 1.0)
            b_sh = jnp.where(keep, pltpu.roll(bv, shift, 0), 0.0)
            bv = av * b_sh + bv
            av = av * a_sh
        elif reverse:
            bv = jnp.concatenate([av[:tc - s] * bv[s:] + bv[:tc - s], bv[tc - s:]], axis=0)
            av = jnp.concatenate([av[:tc - s] * av[s:], av[tc - s:]], axis=0)
        else:
            bv = jnp.concatenate([bv[:s], av[s:] * bv[:tc - s] + bv[s:]], axis=0)
            av = jnp.concatenate([av[:s], av[s:] * av[:tc - s]], axis=0)
        s *= 2
    return av, bv


def _fill_block_diag(w_ref, bd_ref):
    bd_ref[...] = jnp.zeros_like(bd_ref)
    hd = LRU_HEAD_DIM
    for k in range(w_ref.shape[0]):
        bd_ref[k * hd:(k + 1) * hd, k * hd:(k + 1) * hd] = w_ref[k].astype(BF16)


def _lru_fwd(proj, conv_w, conv_b, w_a, b_a, w_x, b_x, lam, carry=None):
    t = proj.shape[0]
    dr = conv_b.shape[1]
    cb = LRU_CB
    tc = _tile(t, 256)
    ncb, ntc = dr // cb, t // tc

    def body(xp_ref, g_ref, cw_ref, cb_ref, wa_ref, ba_ref, wx_ref, bx_ref, lam_ref,
             y_ref, h_ref, xc_ref, prevx_s, hlast_s, wa_s, wx_s):
        c = pl.program_id(1)

        @pl.when(c == 0)
        def _():
            prevx_s[...] = jnp.zeros_like(prevx_s)
            hlast_s[...] = jnp.zeros_like(hlast_s)
            _fill_block_diag(wa_ref, wa_s)
            _fill_block_diag(wx_ref, wx_s)

        x = xp_ref[...].astype(F32)
        prev = prevx_s[...]
        row = lax.broadcasted_iota(jnp.int32, x.shape, 0)

        def sh(j):
            return jnp.where(row >= j, pltpu.roll(x, j, 0), pltpu.roll(prev, j, 0))

        xc = (cb_ref[...] + cw_ref[0:1, :] * sh(3) + cw_ref[1:2, :] * sh(2)
              + cw_ref[2:3, :] * sh(1) + cw_ref[3:4, :] * x)
        prevx_s[...] = x
        xc_ref[...] = xc
        _, _, i, _, _, a, mult = _lru_gates(xc, wa_s[...], ba_ref[...], wx_s[...], bx_ref[...],
                                            lam_ref[...])
        av, bv = _scan_rows(a, mult * (i * xc), reverse=False)
        h = av * hlast_s[...] + bv
        h_ref[...] = h
        hlast_s[...] = h_ref[tc - 1:tc, :]
        gel, _ = _gelu_and_grad(g_ref[...].astype(F32))
        y_ref[...] = (h * gel).astype(BF16)

    vec = pl.BlockSpec((1, cb), lambda j, c: (0, j))
    blk = pl.BlockSpec((tc, cb), lambda j, c: (c, j))
    mat = pl.BlockSpec((cb // LRU_HEAD_DIM, LRU_HEAD_DIM, LRU_HEAD_DIM), lambda j, c: (j, 0, 0))
    return _call(
        body, name="lru_fwd", grid=(ncb, ntc),
        in_specs=[
            blk,
            pl.BlockSpec((tc, cb), lambda j, c: (c, ncb + j)),
            pl.BlockSpec((4, cb), lambda j, c: (0, j)),
            vec, mat, vec, mat, vec, vec,
        ],
        out_specs=[blk, blk, blk],
        out_shape=[
            jax.ShapeDtypeStruct((t, dr), BF16),
            jax.ShapeDtypeStruct((t, dr), F32),
            jax.ShapeDtypeStruct((t, dr), F32),
        ],
        scratch_shapes=[pltpu.VMEM((tc, cb), F32), pltpu.VMEM((1, cb), F32),
                        pltpu.VMEM((cb, cb), BF16), pltpu.VMEM((cb, cb), BF16)],
        args=(proj, proj, conv_w, conv_b, w_a, b_a, w_x, b_x, lam), carry=carry)


def _pool_select(col, vals):
    out = vals[3]
    for g in (2, 1, 0):
        out = jnp.where(col < (g + 1) * POOL_GROUP_DIM, vals[g], out)
    return out


def _pool_fwd(proj, pool_w, pool_scale, col_block):
    t = proj.shape[0]
    dp = pool_scale.shape[1]
    tc = _tile(t, 256)
    ntc = t // tc

    def body(x_ref, w_ref, sc_ref, y_ref, p_ref, px, p2, p4, p8):
        c = pl.program_id(0)

        @pl.when(c == 0)
        def _():
            for s in (px, p2, p4, p8):
                s[...] = jnp.zeros_like(s)

        x = x_ref[...].astype(F32)
        row = lax.broadcasted_iota(jnp.int32, x.shape, 0)
        col = lax.broadcasted_iota(jnp.int32, x.shape, 1)

        def sh(v, pv, j):
            return jnp.where(row >= j, pltpu.roll(v, j, 0), pltpu.roll(pv[...], j, 0))

        s2 = x + sh(x, px, 1)
        s4 = s2 + sh(s2, p2, 2)
        s8 = s4 + sh(s4, p4, 4)
        s16 = s8 + sh(s8, p8, 8)
        px[...] = x
        p2[...] = s2
        p4[...] = s4
        p8[...] = s8
        wsum = _pool_select(col, (s2, s4, s8, s16))
        win = _pool_select(col, POOL_WINDOWS)
        cnt = jnp.minimum(c * tc + row + 1, win).astype(F32)
        p = wsum / cnt - x
        pb = p.astype(BF16)
        p_ref[...] = pb
        for g in range(len(POOL_WINDOWS)):
            sl = slice(g * POOL_GROUP_DIM, (g + 1) * POOL_GROUP_DIM)
            yg = _dot_nn(pb[:, sl], w_ref[g]) * sc_ref[:, sl]
            y_ref[:, sl] = yg.astype(BF16)

    return _call(
        body, name="pool_fwd", grid=(ntc,),
        in_specs=[
            pl.BlockSpec((tc, dp), lambda c: (c, col_block)),
            pl.BlockSpec(pool_w.shape, lambda c: (0, 0, 0)),
            pl.BlockSpec((1, dp), lambda c: (0, 0)),
        ],
        out_specs=[pl.BlockSpec((tc, dp), lambda c: (c, 0))] * 2,
        out_shape=[jax.ShapeDtypeStruct((t, dp), BF16)] * 2,
        scratch_shapes=[pltpu.VMEM((tc, dp), F32)] * 4,
        args=(proj, pool_w, pool_scale))[0]


def _branch_mix(y_lru, y_pool, w_lru_up, w_pool_upt, proj, b_gate, ga_block, gb_block, carry=None):
    t, d = y_lru.shape
    dp = y_pool.shape[1]
    tt, tn = _tile(t, 1024), 512
    nj = d // tn

    def body(yl_ref, yp_ref, wl_ref, wp_ref, ga_ref, gb_ref, ba_ref, bb_ref, bra_ref, brb_ref, mix_ref):
        br_a = _dot_nn(yl_ref[...], wl_ref[...])
        br_b = _dot_nt(yp_ref[...], wp_ref[...])
        bra_ref[...] = br_a.astype(BF16)
        brb_ref[...] = br_b.astype(BF16)
        ga = _sig(ga_ref[...].astype(F32) + ba_ref[...])
        gb = _sig(gb_ref[...].astype(F32) + bb_ref[...])
        mix_ref[...] = (ga * br_a + gb * br_b).astype(BF16)

    out = pl.BlockSpec((tt, tn), lambda j, i: (i, j))
    return _call(
        body, name="branch_mix", grid=(nj, t // tt),
        in_specs=[
            pl.BlockSpec((tt, d), lambda j, i: (i, 0)),
            pl.BlockSpec((tt, dp), lambda j, i: (i, 0)),
            pl.BlockSpec((d, tn), lambda j, i: (0, j)),
            pl.BlockSpec((tn, dp), lambda j, i: (j, 0)),
            pl.BlockSpec((tt, tn), lambda j, i: (i, ga_block + j)),
            pl.BlockSpec((tt, tn), lambda j, i: (i, gb_block + j)),
            pl.BlockSpec((1, tn), lambda j, i: (0, j)),
            pl.BlockSpec((1, tn), lambda j, i: (0, nj + j)),
        ],
        out_specs=[out, out, out],
        out_shape=[jax.ShapeDtypeStruct((t, d), BF16)] * 3,
        args=(y_lru, y_pool, w_lru_up, w_pool_upt, proj, proj, b_gate, b_gate), carry=carry)


def _wo_norm(mix, w_o, x, g2, g3, carry=None):
    t, d = x.shape
    tt = _tile(t, 512)

    def body(mix_ref, w_ref, x_ref, g2_ref, g3_ref, m_ref, x2_ref, h3_ref):
        m = _dot_nn(mix_ref[...], w_ref[...])
        m_ref[...] = m
        mhat, _ = _rms_hat(m)
        x2 = x_ref[...] + mhat * g2_ref[...]
        x2_ref[...] = x2
        xhat, _ = _rms_hat(x2)
        h3_ref[...] = (xhat * g3_ref[...]).astype(BF16)

    row = pl.BlockSpec((tt, d), lambda i: (i, 0))
    vec = pl.BlockSpec((1, d), lambda i: (0, 0))
    return _call(
        body, name="wo_norm", grid=(t // tt,),
        in_specs=[row, pl.BlockSpec((d, d), lambda i: (0, 0)), row, vec, vec],
        out_specs=[row, row, row],
        out_shape=[
            jax.ShapeDtypeStruct((t, d), F32),
            jax.ShapeDtypeStruct((t, d), F32),
            jax.ShapeDtypeStruct((t, d), BF16),
        ],
        args=(mix, w_o, x, g2, g3), carry=carry)


def _ff1(h3, w_ff1t, carry=None):
    t, d = h3.shape
    n = w_ff1t.shape[0]
    tt, tn = _tile(t, 2048), _tile(n, 512)

    def body(h_ref, w_ref, rf_ref):
        rf_ref[...] = jnp.maximum(_dot_nt(h_ref[...], w_ref[...]), 0.0).astype(BF16)

    out = pl.BlockSpec((tt, tn), lambda i, j: (i, j))
    return _call(
        body, name="ff1", grid=(t // tt, n // tn),
        in_specs=[pl.BlockSpec((tt, d), lambda i, j: (i, 0)), pl.BlockSpec((tn, d), lambda i, j: (j, 0))],
        out_specs=[out],
        out_shape=[jax.ShapeDtypeStruct((t, n), BF16)],
        args=(h3, w_ff1t), carry=carry)


def _ff2_loss(rf, w_ff2, x2, g4, target):
    t, k = rf.shape
    d = x2.shape[1]
    tt, tk = _tile(t, 1024), _tile(k, 512)
    nk = k // tk

    def body(a_ref, w_ref, x2_ref, g_ref, tg_ref, dy_ref, df_ref, dg_ref, loss_ref, acc):
        i, kk = pl.program_id(0), pl.program_id(1)

        @pl.when(kk == 0)
        def _():
            acc[...] = jnp.zeros_like(acc)

        @pl.when((i == 0) & (kk == 0))
        def _():
            dg_ref[...] = jnp.zeros_like(dg_ref)
            loss_ref[...] = jnp.zeros_like(loss_ref)

        rf_tile = a_ref[...]
        acc[...] += _dot_nn(rf_tile * rf_tile, w_ref[...])

        @pl.when(kk == nk - 1)
        def _():
            def tail(rows):
                fhat, r = _rms_hat(acc[rows, :])
                g = g_ref[...]
                e = x2_ref[rows, :] + fhat * g - tg_ref[rows, :]
                loss_ref[...] += 0.5 * jnp.sum(jnp.mean(e * e, axis=-1, keepdims=True))
                dy = e * (1.0 / d)
                dy_ref[rows, :] = dy.astype(BF16)
                df, dg = _rms_bwd(dy, fhat, r, g)
                df_ref[rows, :] = df.astype(BF16)
                dg_ref[...] += dg

            _row_chunks(tt, tail)

    row = pl.BlockSpec((tt, d), lambda i, kk: (i, 0))
    vec = pl.BlockSpec((1, d), lambda i, kk: (0, 0))
    return _call(
        body, name="ff2_loss", grid=(t // tt, nk),
        in_specs=[
            pl.BlockSpec((tt, tk), lambda i, kk: (i, kk)),
            pl.BlockSpec((tk, d), lambda i, kk: (kk, 0)),
            row, vec, row,
        ],
        out_specs=[row, row, vec, pl.BlockSpec((1, 128), lambda i, kk: (0, 0))],
        out_shape=[
            jax.ShapeDtypeStruct((t, d), BF16),
            jax.ShapeDtypeStruct((t, d), BF16),
            jax.ShapeDtypeStruct((1, d), F32),
            jax.ShapeDtypeStruct((1, 128), F32),
        ],
        scratch_shapes=[pltpu.VMEM((tt, d), F32)],
        args=(rf, w_ff2, x2, g4, target))[0]


def _ff2_bwd(df, w_ff2, rf, carry=None):
    t, d = df.shape
    n = w_ff2.shape[0]
    tt, tn = _tile(t, 2048), _tile(n, 512)

    def body(df_ref, w_ref, rf_ref, out_ref):
        d_act = _dot_nt(df_ref[...], w_ref[...])
        out_ref[...] = (d_act * (2.0 * rf_ref[...].astype(F32))).astype(BF16)

    blk = pl.BlockSpec((tt, tn), lambda i, j: (i, j))
    return _call(
        body, name="ff2_bwd", grid=(t // tt, n // tn),
        in_specs=[pl.BlockSpec((tt, d), lambda i, j: (i, 0)), pl.BlockSpec((tn, d), lambda i, j: (j, 0)), blk],
        out_specs=[blk],
        out_shape=[jax.ShapeDtypeStruct((t, n), BF16)],
        args=(df, w_ff2, rf), carry=carry)


def _wgrad(a, b, name, prev=None, row_off=0, rows=None, carry=None, square_a=False):
    t, m = a.shape
    n = b.shape[1]
    rows = m if rows is None else rows
    tm, tk = _tile(m, 512), _tile(t, 2048)
    nk = t // tk
    assert row_off % tm == 0
    off = row_off // tm

    def body(*refs):
        a_ref, b_ref = refs[0], refs[1]
        o32_ref, o16_ref, acc = refs[-3], refs[-2], refs[-1]
        kk = pl.program_id(1)

        @pl.when(kk == 0)
        def _():
            acc[...] = jnp.zeros_like(acc)

        a_tile = a_ref[...]
        acc[...] += _dot_tn(a_tile * a_tile if square_a else a_tile, b_ref[...])

        @pl.when(kk == nk - 1)
        def _():
            o32_ref[...] = acc[...]
            o16_ref[...] = acc[...].astype(BF16)

    in_specs = [pl.BlockSpec((tk, tm), lambda i, kk: (kk, i)), pl.BlockSpec((tk, n), lambda i, kk: (kk, 0))]
    args = [a, b]
    aliases = {}
    if prev is not None:
        in_specs += [ANY, ANY]
        args += list(prev)
        aliases = {2: 0, 3: 1}
    out = pl.BlockSpec((tm, n), lambda i, kk: (off + i, 0))
    return _call(
        body, name=name, grid=(m // tm, nk),
        in_specs=in_specs, out_specs=[out, out],
        out_shape=[jax.ShapeDtypeStruct((rows, n), F32), jax.ShapeDtypeStruct((rows, n), BF16)],
        scratch_shapes=[pltpu.VMEM((tm, n), F32)],
        aliases=aliases, args=args, carry=carry)


def _wgrad_parts(parts, b, name, carry=None):
    t, n = b.shape
    tm = 512
    bounds = []
    lo = 0
    for part in parts:
        assert part.shape[0] == t and part.shape[1] % tm == 0
        bounds.append((lo, lo + part.shape[1] // tm))
        lo += part.shape[1] // tm
    nm = lo
    np_ = len(parts)

    def body(*refs):
        p_refs, b_ref, o32_ref, o16_ref = refs[:np_], refs[np_], refs[np_ + 1], refs[np_ + 2]
        i = pl.program_id(0)
        for (lo_p, hi_p), p_ref in zip(bounds, p_refs):
            @pl.when((i >= lo_p) & (i < hi_p))
            def _(p_ref=p_ref):
                res = _dot_tn(p_ref[...], b_ref[...])
                o32_ref[...] = res
                o16_ref[...] = res.astype(BF16)

    def part_spec(lo_p, hi_p):
        return pl.BlockSpec((t, tm), lambda i: (0, jnp.clip(i - lo_p, 0, hi_p - lo_p - 1)))

    out = pl.BlockSpec((tm, n), lambda i: (i, 0))
    return _call(
        body, name=name, grid=(nm,),
        in_specs=[part_spec(lo_p, hi_p) for lo_p, hi_p in bounds] + [pl.BlockSpec((t, n), lambda i: (0, 0))],
        out_specs=[out, out],
        out_shape=[jax.ShapeDtypeStruct((nm * tm, n), F32), jax.ShapeDtypeStruct((nm * tm, n), BF16)],
        args=(*parts, b), carry=carry)


def _ff1_bwd_norms(d_f1, w_ff1t, dy, x2, g3, m, g2, carry=None):
    t, k = d_f1.shape
    d = x2.shape[1]
    tt, tk = _tile(t, 1024), _tile(k, 512)
    nk = k // tk

    def body(a_ref, w_ref, dy_ref, x2_ref, g3_ref, m_ref, g2_ref, dx2_ref, dm_ref, dg3_ref, dg2_ref, acc):
        i, kk = pl.program_id(0), pl.program_id(1)

        @pl.when(kk == 0)
        def _():
            acc[...] = jnp.zeros_like(acc)

        @pl.when((i == 0) & (kk == 0))
        def _():
            dg3_ref[...] = jnp.zeros_like(dg3_ref)
            dg2_ref[...] = jnp.zeros_like(dg2_ref)

        acc[...] += _dot_nn(a_ref[...], w_ref[...])

        @pl.when(kk == nk - 1)
        def _():
            def tail(rows):
                xhat, r3 = _rms_hat(x2_ref[rows, :])
                dx, dg3 = _rms_bwd(acc[rows, :], xhat, r3, g3_ref[...])
                dx2 = dy_ref[rows, :].astype(F32) + dx
                dx2_ref[rows, :] = dx2
                dg3_ref[...] += dg3
                mhat, r2 = _rms_hat(m_ref[rows, :])
                dm, dg2 = _rms_bwd(dx2, mhat, r2, g2_ref[...])
                dm_ref[rows, :] = dm.astype(BF16)
                dg2_ref[...] += dg2

            _row_chunks(tt, tail)

    row = pl.BlockSpec((tt, d), lambda i, kk: (i, 0))
    vec = pl.BlockSpec((1, d), lambda i, kk: (0, 0))
    return _call(
        body, name="ff1_bwd_norms", grid=(t // tt, nk),
        in_specs=[
            pl.BlockSpec((tt, tk), lambda i, kk: (i, kk)),
            pl.BlockSpec((tk, d), lambda i, kk: (kk, 0)),
            row, row, vec, row, vec,
        ],
        out_specs=[row, row, vec, vec],
        out_shape=[
            jax.ShapeDtypeStruct((t, d), F32),
            jax.ShapeDtypeStruct((t, d), BF16),
            jax.ShapeDtypeStruct((1, d), F32),
            jax.ShapeDtypeStruct((1, d), F32),
        ],
        scratch_shapes=[pltpu.VMEM((tt, d), F32)],
        args=(d_f1, w_ff1t, dy, x2, g3, m, g2), carry=carry)


def _wo_bwd_mix(dm, w_o, br_a, br_b, proj, b_gate, ga_block, gb_block, carry=None):
    t, d = dm.shape
    tt, tn = _tile(t, 1024), 512
    nj = d // tn

    def body(dm_ref, w_ref, bra_ref, brb_ref, ga_ref, gb_ref, ba_ref, bb_ref,
             dbra_ref, dbrb_ref, dga_ref, dgb_ref, dba_ref, dbb_ref):
        i = pl.program_id(1)

        @pl.when(i == 0)
        def _():
            dba_ref[...] = jnp.zeros_like(dba_ref)
            dbb_ref[...] = jnp.zeros_like(dbb_ref)

        d_mix = _dot_nt(dm_ref[...], w_ref[...])
        ga = _sig(ga_ref[...].astype(F32) + ba_ref[...])
        gb = _sig(gb_ref[...].astype(F32) + bb_ref[...])
        dbra_ref[...] = (d_mix * ga).astype(BF16)
        dbrb_ref[...] = (d_mix * gb).astype(BF16)
        dga = d_mix * bra_ref[...].astype(F32) * (ga * (1.0 - ga))
        dgb = d_mix * brb_ref[...].astype(F32) * (gb * (1.0 - gb))
        dga_ref[...] = dga.astype(BF16)
        dgb_ref[...] = dgb.astype(BF16)
        dba_ref[...] += jnp.sum(dga, axis=0, keepdims=True)
        dbb_ref[...] += jnp.sum(dgb, axis=0, keepdims=True)

    blk = pl.BlockSpec((tt, tn), lambda j, i: (i, j))
    vec = pl.BlockSpec((1, tn), lambda j, i: (0, j))
    return _call(
        body, name="wo_bwd_mix", grid=(nj, t // tt),
        in_specs=[
            pl.BlockSpec((tt, d), lambda j, i: (i, 0)),
            pl.BlockSpec((tn, d), lambda j, i: (j, 0)),
            blk, blk,
            pl.BlockSpec((tt, tn), lambda j, i: (i, ga_block + j)),
            pl.BlockSpec((tt, tn), lambda j, i: (i, gb_block + j)),
            vec,
            pl.BlockSpec((1, tn), lambda j, i: (0, nj + j)),
        ],
        out_specs=[blk, blk, blk, blk, vec, vec],
        out_shape=[jax.ShapeDtypeStruct((t, d), BF16)] * 4 + [jax.ShapeDtypeStruct((1, d), F32)] * 2,
        args=(dm, w_o, br_a, br_b, proj, proj, b_gate, b_gate), carry=carry)


def _lru_up_bwd(d_br_a, w_lru_up, proj, h, g_block, carry=None):
    t, d = d_br_a.shape
    tt, tn = _tile(t, 1024), 512

    def body(a_ref, w_ref, g_ref, h_ref, dh_ref, dg_ref):
        d_y = _dot_nt(a_ref[...], w_ref[...])
        gel, gel_grad = _gelu_and_grad(g_ref[...].astype(F32))
        dh_ref[...] = d_y * gel
        dg_ref[...] = (d_y * h_ref[...] * gel_grad).astype(BF16)

    blk = pl.BlockSpec((tt, tn), lambda i, j: (i, j))
    return _call(
        body, name="lru_up_bwd", grid=(t // tt, d // tn),
        in_specs=[
            pl.BlockSpec((tt, d), lambda i, j: (i, 0)),
            pl.BlockSpec((tn, d), lambda i, j: (j, 0)),
            pl.BlockSpec((tt, tn), lambda i, j: (i, g_block + j)),
            blk,
        ],
        out_specs=[blk, blk],
        out_shape=[jax.ShapeDtypeStruct((t, d), F32), jax.ShapeDtypeStruct((t, d), BF16)],
        args=(d_br_a, w_lru_up, proj, h), carry=carry)


def _pool_up_bwd(d_br_b, w_pool_upt):
    t, d = d_br_b.shape
    dp = w_pool_upt.shape[1]
    tt = _tile(t, 2048)

    def body(a_ref, w_ref, out_ref):
        out_ref[...] = _dot_nn(a_ref[...], w_ref[...])

    return _call(
        body, name="pool_up_bwd", grid=(t // tt,),
        in_specs=[pl.BlockSpec((tt, d), lambda i: (i, 0)), pl.BlockSpec((d, dp), lambda i: (0, 0))],
        out_specs=[pl.BlockSpec((tt, dp), lambda i: (i, 0))],
        out_shape=[jax.ShapeDtypeStruct((t, dp), F32)],
        args=(d_br_b, w_pool_upt))[0][0]


def _lru_bwd(dh, xc, h, proj, conv_w, w_a, b_a, w_x, b_x, lam, carry=None):
    t, dr = dh.shape
    cb = LRU_CB
    hd = LRU_HEAD_DIM
    per = cb // hd
    tc = _tile(t, 256)
    ncb, ntc = dr // cb, t // tc

    def body(dh_ref, xc_ref, h_ref, hp_ref, xp_ref, cw_ref, wa_ref, ba_ref, wx_ref, bx_ref, lam_ref,
             dxp_ref, dwa_ref, dba_ref, dwx_ref, dbx_ref, dlam_ref, dcw_ref, dcb_ref,
             nextd_s, anext_s, gnext_s, tmp_s, wa_s, wx_s):
        c = pl.program_id(1)
        rc = ntc - 1 - c

        @pl.when(c == 0)
        def _():
            nextd_s[...] = jnp.zeros_like(nextd_s)
            anext_s[...] = jnp.zeros_like(anext_s)
            gnext_s[...] = jnp.zeros_like(gnext_s)
            for ref in (dwa_ref, dba_ref, dwx_ref, dbx_ref, dlam_ref, dcw_ref, dcb_ref):
                ref[...] = jnp.zeros_like(ref)
            _fill_block_diag(wa_ref, wa_s)
            _fill_block_diag(wx_ref, wx_s)

        xc = xc_ref[...]
        wa, wx, lam = wa_s[...], wx_s[...], lam_ref[...]
        xcb, r, i, sp, log_a, a, mult = _lru_gates(xc, wa, ba_ref[...], wx, bx_ref[...], lam)
        row = lax.broadcasted_iota(jnp.int32, xc.shape, 0)
        h = h_ref[...]
        hp = jnp.where(rc == 0, 0.0, hp_ref[...])
        hprev = jnp.where(row >= 1, pltpu.roll(h, 1, 0), pltpu.roll(hp, 1, 0))

        def up(v, nv, j):
            return jnp.where(row < tc - j, pltpu.roll(v, tc - j, 0), nv)

        av, bv = _scan_rows(up(a, anext_s[...], 1), dh_ref[...], reverse=True)
        gt = av * gnext_s[...] + bv
        tmp_s[...] = gt
        gnext_s[...] = tmp_s[0:1, :]
        tmp_s[...] = a
        anext_s[...] = tmp_s[0:1, :]

        da = gt * hprev
        ixc = i * xc
        d_mult = gt * ixc
        d_i = gt * mult * xc
        d_xc = gt * mult * i
        d_log_a = da * a - d_mult * (a * a) / mult
        d_pre_r = (d_log_a * ((-LRU_C) * sp)) * (r * (1.0 - r))
        d_pre_i = d_i * (i * (1.0 - i))
        d_sp = jnp.sum(d_log_a * ((-LRU_C) * r), axis=0, keepdims=True)
        dlam_ref[...] += d_sp * (-1.0 / (1.0 + jnp.exp(lam)))
        dpr = d_pre_r.astype(BF16)
        dpi = d_pre_i.astype(BF16)
        dba_ref[...] += jnp.sum(d_pre_r, axis=0, keepdims=True)
        dbx_ref[...] += jnp.sum(d_pre_i, axis=0, keepdims=True)
        pa = _dot_tn(xcb, dpr)
        px = _dot_tn(xcb, dpi)
        for k in range(per):
            dwa_ref[k] += pa[k * hd:(k + 1) * hd, k * hd:(k + 1) * hd]
            dwx_ref[k] += px[k * hd:(k + 1) * hd, k * hd:(k + 1) * hd]
        d_xc = d_xc + _dot_nt(dpr, wa) + _dot_nt(dpi, wx)

        nxt = nextd_s[...]
        xp = xp_ref[...].astype(F32)
        dxp = cw_ref[3:4, :] * d_xc
        dcw_ref[3:4, :] += jnp.sum(xp * d_xc, axis=0, keepdims=True)
        for j in (1, 2, 3):
            uj = up(d_xc, pltpu.roll(nxt, tc - j, 0), j)
            dxp = dxp + cw_ref[3 - j:4 - j, :] * uj
            dcw_ref[3 - j:4 - j, :] += jnp.sum(xp * uj, axis=0, keepdims=True)
        dcb_ref[...] += jnp.sum(d_xc, axis=0, keepdims=True)
        nextd_s[...] = d_xc
        dxp_ref[...] = dxp.astype(BF16)

    vec = pl.BlockSpec((1, cb), lambda j, c: (0, j))
    blk = pl.BlockSpec((tc, cb), lambda j, c: (ntc - 1 - c, j))
    mat = pl.BlockSpec((per, hd, hd), lambda j, c: (j, 0, 0))
    cwb = pl.BlockSpec((4, cb), lambda j, c: (0, j))
    return _call(
        body, name="lru_bwd", grid=(ncb, ntc),
        in_specs=[
            blk, blk, blk,
            pl.BlockSpec((tc, cb), lambda j, c: (jnp.maximum(ntc - 2 - c, 0), j)),
            blk, cwb, mat, vec, mat, vec, vec,
        ],
        out_specs=[blk, mat, vec, mat, vec, vec, cwb, vec],
        out_shape=[
            jax.ShapeDtypeStruct((t, dr), BF16),
            jax.ShapeDtypeStruct(w_a.shape, F32),
            jax.ShapeDtypeStruct((1, dr), F32),
            jax.ShapeDtypeStruct(w_x.shape, F32),
            jax.ShapeDtypeStruct((1, dr), F32),
            jax.ShapeDtypeStruct((1, dr), F32),
            jax.ShapeDtypeStruct((4, dr), F32),
            jax.ShapeDtypeStruct((1, dr), F32),
        ],
        scratch_shapes=[
            pltpu.VMEM((tc, cb), F32),
            pltpu.VMEM((1, cb), F32),
            pltpu.VMEM((1, cb), F32),
            pltpu.VMEM((tc, cb), F32),
            pltpu.VMEM((cb, cb), BF16),
            pltpu.VMEM((cb, cb), BF16),
        ],
        args=(dh, xc, h, h, proj, conv_w, w_a, b_a, w_x, b_x, lam), carry=carry)


def _pool_bwd(d_y_pool, p, pool_w, pool_scale):
    t, dp = d_y_pool.shape
    tc = _tile(t, 256)
    ntc = t // tc
    ng = len(POOL_WINDOWS)

    def body(dy_ref, p_ref, w_ref, sc_ref, dx_ref, dw_ref, dsc_ref, nz, n2, n4, n8, dp_s):
        c = pl.program_id(0)
        rc = ntc - 1 - c

        @pl.when(c == 0)
        def _():
            for s in (nz, n2, n4, n8):
                s[...] = jnp.zeros_like(s)
            dw_ref[...] = jnp.zeros_like(dw_ref)
            dsc_ref[...] = jnp.zeros_like(dsc_ref)

        for g in range(ng):
            sl = slice(g * POOL_GROUP_DIM, (g + 1) * POOL_GROUP_DIM)
            pg = p_ref[:, sl]
            dyg = dy_ref[:, sl]
            wg = w_ref[g].astype(BF16)
            q = _dot_nn(pg, wg)
            dsc_ref[:, sl] += jnp.sum(dyg * q, axis=0, keepdims=True)
            dpw = (dyg * sc_ref[:, sl]).astype(BF16)
            dw_ref[g] += _dot_tn(pg, dpw)
            dp_s[:, sl] = _dot_nt(dpw, wg)

        dpv = dp_s[...]
        row = lax.broadcasted_iota(jnp.int32, dpv.shape, 0)
        col = lax.broadcasted_iota(jnp.int32, dpv.shape, 1)
        win = _pool_select(col, POOL_WINDOWS)
        cnt = jnp.minimum(rc * tc + row + 1, win).astype(F32)
        z = dpv / cnt

        def up(v, nv, j):
            return jnp.where(row < tc - j, pltpu.roll(v, tc - j, 0), pltpu.roll(nv[...], tc - j, 0))

        u2 = z + up(z, nz, 1)
        u4 = u2 + up(u2, n2, 2)
        u8 = u4 + up(u4, n4, 4)
        u16 = u8 + up(u8, n8, 8)
        nz[...] = z
        n2[...] = u2
        n4[...] = u4
        n8[...] = u8
        dx_ref[...] = (_pool_select(col, (u2, u4, u8, u16)) - dpv).astype(BF16)

    blk = pl.BlockSpec((tc, dp), lambda c: (ntc - 1 - c, 0))
    full_w = pl.BlockSpec(pool_w.shape, lambda c: (0, 0, 0))
    vec = pl.BlockSpec((1, dp), lambda c: (0, 0))
    return _call(
        body, name="pool_bwd", grid=(ntc,),
        in_specs=[blk, blk, full_w, vec],
        out_specs=[blk, full_w, vec],
        out_shape=[
            jax.ShapeDtypeStruct((t, dp), BF16),
            jax.ShapeDtypeStruct(pool_w.shape, F32),
            jax.ShapeDtypeStruct((1, dp), F32),
        ],
        scratch_shapes=[pltpu.VMEM((tc, dp), F32)] * 5,
        args=(d_y_pool, p, pool_w, pool_scale))[0]


def _win_bwd_norm(parts, w_int, dx2, x, g1, carry=None):
    t, d = x.shape
    tk = 512
    tt = _tile(t, 1024)
    bounds = []
    k0 = 0
    for part in parts:
        assert part.shape[1] % tk == 0
        bounds.append((k0, k0 + part.shape[1] // tk))
        k0 += part.shape[1] // tk
    nk = k0
    assert nk * tk == w_int.shape[0]
    np_ = len(parts)

    def body(*refs):
        p_refs = refs[:np_]
        w_ref, dx2_ref, x_ref, g_ref, gx_ref, dg_ref, acc = refs[np_:]
        i, kk = pl.program_id(0), pl.program_id(1)

        @pl.when(kk == 0)
        def _():
            acc[...] = jnp.zeros_like(acc)

        @pl.when((i == 0) & (kk == 0))
        def _():
            dg_ref[...] = jnp.zeros_like(dg_ref)

        for (lo, hi), p_ref in zip(bounds, p_refs):
            @pl.when((kk >= lo) & (kk < hi))
            def _(p_ref=p_ref):
                acc[...] += _dot_nn(p_ref[...], w_ref[...])

        @pl.when(kk == nk - 1)
        def _():
            def tail(rows):
                xhat, r = _rms_hat(x_ref[rows, :])
                dx, dg = _rms_bwd(acc[rows, :], xhat, r, g_ref[...])
                gx_ref[rows, :] = dx2_ref[rows, :] + dx
                dg_ref[...] += dg

            _row_chunks(tt, tail)

    def part_spec(lo, hi):
        return pl.BlockSpec((tt, tk), lambda i, kk: (i, jnp.clip(kk - lo, 0, hi - lo - 1)))

    row = pl.BlockSpec((tt, d), lambda i, kk: (i, 0))
    vec = pl.BlockSpec((1, d), lambda i, kk: (0, 0))
    return _call(
        body, name="win_bwd_norm", grid=(t // tt, nk),
        in_specs=[part_spec(lo, hi) for lo, hi in bounds]
        + [pl.BlockSpec((tk, d), lambda i, kk: (kk, 0)), row, row, vec],
        out_specs=[row, vec],
        out_shape=[jax.ShapeDtypeStruct((t, d), F32), jax.ShapeDtypeStruct((1, d), F32)],
        scratch_shapes=[pltpu.VMEM((tt, d), F32)],
        args=(*parts, w_int, dx2, x, g1), carry=carry)


def _adam_math(w, g, m, v):
    m = ADAM_B1 * m + (1.0 - ADAM_B1) * g
    v = ADAM_B2 * v + (1.0 - ADAM_B2) * (g * g)
    m_hat = m / (1.0 - ADAM_B1 ** ADAM_STEP)
    v_hat = v / (1.0 - ADAM_B2 ** ADAM_STEP)
    delta = -ADAM_LR * (m_hat / (jnp.sqrt(v_hat) + ADAM_EPS) + ADAM_WD * w)
    return delta, m, v


def _adamw_big(ws, gs, ms, vs, carry=None):
    n = len(ws)
    nb = 8

    def body(*refs):
        for a in range(n):
            w_ref, g_ref, m_ref, v_ref = refs[4 * a:4 * a + 4]
            d_ref, nm_ref, nv_ref = refs[4 * n + 3 * a:4 * n + 3 * a + 3]
            dl, m, v = _adam_math(w_ref[...], g_ref[...], m_ref[...], v_ref[...])
            d_ref[...] = dl
            nm_ref[...] = m
            nv_ref[...] = v

    in_specs, out_specs, out_shape, args = [], [], [], []
    for w, g, m, v in zip(ws, gs, ms, vs):
        rows, cols = w.shape
        blk = pl.BlockSpec((rows // nb, cols), lambda i: (i, 0))
        in_specs += [blk] * 4
        args += [w, g, m, v]
        out_specs += [blk] * 3
        out_shape += [jax.ShapeDtypeStruct(w.shape, F32)] * 3
    outs, got = _call(body, name="adamw_big", grid=(nb,), in_specs=in_specs, out_specs=out_specs,
                      out_shape=out_shape, args=args, carry=carry)
    return [tuple(outs[3 * a:3 * a + 3]) for a in range(n)], got


SMALL_ORDER = ("norm_mix_pre", "norm_mix_post", "norm_mlp_pre", "norm_mlp_post", "b_gate", "conv_w", "conv_b",
               "lru_w_a", "lru_b_a", "lru_w_x", "lru_b_x", "lru_lambda", "pool_w", "pool_scale")
VEC_ROW = dict(norm_mix_pre=0, norm_mix_post=1, norm_mlp_pre=2, norm_mlp_post=3, conv_b=6, lru_b_a=7,
               lru_b_x=8, lru_lambda=9)
ROW_B_GATE, ROW_POOL_SCALE, ROW_CONV_W, ROW_LOSS, N_VEC_ROWS = 4, 10, 11, 15, 16


def _adamw_small(vec_parts, g_pool, g_wa, g_wx, me, params):
    d = vec_parts.shape[2]
    names = SMALL_ORDER
    n = len(names)
    cw_cols = params["conv_w"][0].shape[2]

    def body(me_ref, vec_ref, vecc_ref, gp_ref, gwa_ref, gwx_ref, *refs):
        wmv = refs[:3 * n]
        loss_ref = refs[3 * n]
        outs = refs[3 * n + 1:3 * n + 1 + 4 * n]
        vs, vsc = refs[3 * n + 1 + 4 * n:]
        acc, accc = vec_ref[0], vecc_ref[0]
        for k in range(1, N_DEV):
            acc = acc + vec_ref[k]
            accc = accc + vecc_ref[k]
        vs[...] = acc
        vsc[...] = accc
        loss_ref[...] = vs[ROW_LOSS:ROW_LOSS + 1, 0:128]

        def upd(a, g, idx):
            w_ref, m_ref, v_ref = wmv[3 * a:3 * a + 3]
            g_ref, d_ref, nm_ref, nv_ref = outs[4 * a:4 * a + 4]
            dl, m, v = _adam_math(w_ref[idx], g, m_ref[idx], v_ref[idx])
            g_ref[idx] = g
            d_ref[idx] = dl
            nm_ref[idx] = m
            nv_ref[idx] = v

        for a, name in enumerate(names):
            if name in VEC_ROW:
                r = VEC_ROW[name]
                upd(a, vs[r:r + 1, :], (slice(None), slice(None)))
            elif name == "b_gate":
                for half in range(2):
                    r = ROW_B_GATE + half
                    upd(a, vs[r:r + 1, :], (slice(None), slice(half * d, (half + 1) * d)))
            elif name == "pool_scale":
                width = params[name][0].shape[1]
                upd(a, vs[ROW_POOL_SCALE:ROW_POOL_SCALE + 1, 0:width], (slice(None), slice(None)))
            elif name == "conv_w":
                upd(a, vsc[ROW_CONV_W:ROW_CONV_W + 4, :], (0,))
            elif name == "pool_w":
                upd(a, gp_ref[...], (Ellipsis,))
            elif name == "lru_w_a":
                upd(a, gwa_ref[...], (Ellipsis,))
            elif name == "lru_w_x":
                upd(a, gwx_ref[...], (Ellipsis,))
            else:
                raise ValueError(name)

    def whole(shape):
        nd = len(shape)
        return pl.BlockSpec(tuple(shape), lambda i, me_ref: (0,) * nd)

    in_specs = [
        whole(vec_parts.shape),
        pl.BlockSpec((N_DEV, N_VEC_ROWS, cw_cols), lambda i, me_ref: (0, 0, me_ref[0])),
        whole(g_pool.shape), whole(g_wa.shape), whole(g_wx.shape),
    ]
    args = [vec_parts, vec_parts, g_pool, g_wa, g_wx]
    out_specs = [whole((1, 128))]
    out_shape = [jax.ShapeDtypeStruct((1, 128), F32)]
    for name in names:
        for arr in params[name]:
            in_specs.append(whole(arr.shape))
            args.append(arr)
        shp = params[name][0].shape
        out_specs += [whole(shp)] * 4
        out_shape += [jax.ShapeDtypeStruct(shp, F32)] * 4
    grid_spec = pltpu.PrefetchScalarGridSpec(
        num_scalar_prefetch=1, grid=(1,), in_specs=in_specs, out_specs=out_specs,
        scratch_shapes=[pltpu.VMEM((N_VEC_ROWS, d), F32), pltpu.VMEM((N_VEC_ROWS, cw_cols), F32)])
    outs = pl.pallas_call(
        body, name="adamw_small", grid_spec=grid_spec, out_shape=out_shape,
        compiler_params=pltpu.CompilerParams(
            dimension_semantics=("arbitrary",), vmem_limit_bytes=V7X_VMEM_LIMIT_BYTES),
    )(me, *_in_hbm(args))
    return outs[0], {name: tuple(outs[1 + 4 * a:5 + 4 * a]) for a, name in enumerate(names)}


def _rs_sum(fulls, recvs, shard_ids, slot_ids, name):
    n = len(fulls)

    def body(sh_ref, sl_ref, *refs):
        s = pl.program_id(0)
        for a in range(n):
            full_ref, recv_ref = refs[2 * a], refs[2 * a + 1]
            own_ref, send_ref = refs[2 * n + 2 * a], refs[2 * n + 2 * a + 1]
            v = full_ref[...] + recv_ref[...].astype(F32)

            @pl.when(s == 0)
            def _(own_ref=own_ref, v=v):
                own_ref[...] = v

            @pl.when(s > 0)
            def _(send_ref=send_ref, v=v):
                send_ref[...] = v.astype(send_ref.dtype)

    in_specs, out_specs, out_shape, args = [], [], [], []
    for full, recv in zip(fulls, recvs):
        r, rest = recv.shape[1], tuple(recv.shape[2:])
        zeros = (0,) * len(rest)
        in_specs += [
            pl.BlockSpec((r,) + rest, lambda s, sh, sl, zeros=zeros: (sh[s],) + zeros),
            pl.BlockSpec((None, r) + rest, lambda s, sh, sl, zeros=zeros: (sl[s], 0) + zeros),
        ]
        out_specs += [
            pl.BlockSpec((None, r) + rest, lambda s, sh, sl, zeros=zeros: (0, 0) + zeros),
            pl.BlockSpec((None, r) + rest, lambda s, sh, sl, zeros=zeros: (jnp.maximum(s - 1, 0), 0) + zeros),
        ]
        out_shape += [jax.ShapeDtypeStruct((1, r) + rest, F32), jax.ShapeDtypeStruct((3, r) + rest, recv.dtype)]
        args += [full, recv]
    grid_spec = pltpu.PrefetchScalarGridSpec(
        num_scalar_prefetch=2, grid=(4,), in_specs=in_specs, out_specs=out_specs)
    outs = pl.pallas_call(
        body,
        name=name,
        grid_spec=grid_spec,
        out_shape=out_shape,
        compiler_params=pltpu.CompilerParams(
            dimension_semantics=("arbitrary",), vmem_limit_bytes=V7X_VMEM_LIMIT_BYTES),
    )(shard_ids, slot_ids, *_in_hbm(args))
    return [(outs[2 * a], outs[2 * a + 1]) for a in range(n)]


def _finals(pairs, name, carry=None):
    nb = 4
    n = len(pairs)

    def body(*refs):
        for a in range(n):
            own_ref, recv_ref = refs[2 * a], refs[2 * a + 1]
            acc = own_ref[...]
            for k in range(3):
                acc = acc + recv_ref[k].astype(F32)
            refs[2 * n + a][...] = acc

    in_specs, out_specs, out_shape, args = [], [], [], []
    for own, recv in pairs:
        _, rows, cols = own.shape
        in_specs += [pl.BlockSpec((None, rows // nb, cols), lambda i: (0, i, 0)),
                     pl.BlockSpec((3, rows // nb, cols), lambda i: (0, i, 0))]
        args += [own, recv]
        out_specs.append(pl.BlockSpec((rows // nb, cols), lambda i: (i, 0)))
        out_shape.append(jax.ShapeDtypeStruct((rows, cols), F32))
    return _call(body, name=name, grid=(nb,), in_specs=in_specs, out_specs=out_specs,
                 out_shape=out_shape, args=args, carry=carry)


def _rs_sums(fulls_f32, recv1, tag):
    x, y, c = _place()
    qs = jnp.stack([2 * x + y, 2 * (1 - x) + y, 2 * x + (1 - y), 2 * (1 - x) + (1 - y)]).astype(jnp.int32)
    shard_ids = 2 * qs + c
    return _rs_sum(fulls_f32, recv1, shard_ids, qs, "rs_sum_" + tag)


def _rs_level1(fulls_f32, fulls_send, tag):
    recv1 = _run_plan(_rs_sibling_plan(fulls_send), "rs_sibling_" + tag)
    return _rs_sums(fulls_f32, recv1, tag)


def _rows(g):
    return g.reshape(g.shape[0] * g.shape[1], g.shape[2])


def kernel(x, norm_mix_pre, norm_mix_post, norm_mlp_pre, norm_mlp_post, w_in, b_gate, conv_w, conv_b, lru_w_a, lru_b_a, lru_w_x, lru_b_x, lru_lambda, pool_w, pool_scale, w_lru_up, w_pool_up, w_o, w_ff1, w_ff2, loss_target, m_norm_mix_pre, m_norm_mix_post, m_norm_mlp_pre, m_norm_mlp_post, m_w_in, m_b_gate, m_conv_w, m_conv_b, m_lru_w_a, m_lru_b_a, m_lru_w_x, m_lru_b_x, m_lru_lambda, m_pool_w, m_pool_scale, m_w_lru_up, m_w_pool_up, m_w_o, m_w_ff1, m_w_ff2, v_norm_mix_pre, v_norm_mix_post, v_norm_mlp_pre, v_norm_mlp_post, v_w_in, v_b_gate, v_conv_w, v_conv_b, v_lru_w_a, v_lru_b_a, v_lru_w_x, v_lru_b_x, v_lru_lambda, v_pool_w, v_pool_scale, v_w_lru_up, v_w_pool_up, v_w_o, v_w_ff1, v_w_ff2):
    t, d = x.shape[1], x.shape[2]
    d_rnn = conv_b.shape[1]
    d_pool = pool_scale.shape[1]
    per = LRU_CB // LRU_HEAD_DIM
    xi, yi, ci = _place()
    me = 4 * xi + 2 * yi + ci

    x2d = x[0]
    tgt = loss_target[0]

    s_in = w_in[0].T.astype(BF16)
    s_lu = w_lru_up[0].astype(BF16)
    s_pu = w_pool_up[0].T.astype(BF16)
    s_o = w_o[0].astype(BF16)
    s_f1 = w_ff1[0].T.astype(BF16)
    s_f2 = w_ff2[0].astype(BF16)
    s_cw = jnp.pad(conv_w[0], ((0, 4), (0, 0)))

    g_in, g_cw = _run_plan(_ag_plan([s_in, s_cw]), "ag_w_in")
    w_int = _rows(g_in)
    conv_w_full = jnp.transpose(g_cw[:, :4, :], (1, 0, 2)).reshape(4, d_rnn)

    wa_bd, wx_bd = lru_w_a[0], lru_w_x[0]
    pw = pool_w[0]
    pw_bf = pw.astype(BF16)

    pool_block = (2 * d_rnn) // d_pool
    ga_block = (2 * d_rnn + d_pool) // 512
    gb_block = ga_block + d // 512
    g_block = d_rnn // 512

    r_f1, r_f2 = s_f1.shape[0], s_f2.shape[0]
    f1_cut = r_f1 // 4
    f2_cut = (3 * r_f2) // 8
    plan = _join([_ag_plan([s_lu, s_pu, s_o]), _ag_plan([s_f1], pieces=[(0, f1_cut)])])
    (proj, h1), got = _norm_proj(x2d, norm_mix_pre, w_int, carry=plan)
    (g_lu, g_pu, g_o), (g_f1,) = plan.split(got)
    w_lu, w_put, w_og = _rows(g_lu), _rows(g_pu), _rows(g_o)
    (y_lru, h, xc), (g_f1,) = _lru_fwd(
        proj, conv_w_full, conv_b, wa_bd, lru_b_a, wx_bd, lru_b_x, lru_lambda,
        carry=_ag_plan([s_f1], pieces=[(f1_cut, r_f1 - f1_cut)], bufs=[g_f1]))
    w_f1t = _rows(g_f1)
    y_pool, p = _pool_fwd(proj, pw_bf, pool_scale, pool_block)
    (br_a, br_b, mix), (g_f2,) = _branch_mix(
        y_lru, y_pool, w_lu, w_put, proj, b_gate, ga_block, gb_block,
        carry=_ag_plan([s_f2], pieces=[(0, f2_cut)]))
    (m, x2, h3), (g_f2,) = _wo_norm(
        mix, w_og, x2d, norm_mix_post, norm_mlp_pre,
        carry=_ag_plan([s_f2], pieces=[(f2_cut, r_f2 // 2 - f2_cut)], bufs=[g_f2]))
    (rf,), (g_f2,) = _ff1(
        h3, w_f1t, carry=_ag_plan([s_f2], pieces=[(r_f2 // 2, r_f2 - r_f2 // 2)], bufs=[g_f2]))
    w_f2 = _rows(g_f2)
    dy, df, dg4, loss_part = _ff2_loss(rf, w_f2, x2, norm_mlp_post, tgt)

    (gw_ff2_32, gw_ff2_16), _ = _wgrad(rf, df, "wgrad_ff2", square_a=True)
    (d_f1,), r1_ff2 = _ff2_bwd(df, w_f2, rf, carry=_rs_sibling_plan([gw_ff2_16]))
    ((own_ff2, send_ff2),) = _rs_sums([gw_ff2_32], r1_ff2, "ff2")
    cut2 = (5 * send_ff2.shape[1]) // 16
    (gw_ff1_32, gw_ff1_16), (r2_ff2,) = _wgrad(
        d_f1, h3, "wgrad_ff1", carry=_rs_chips_plan([send_ff2], pieces=[(0, cut2)]))
    plan = _join([_rs_chips_plan([send_ff2], pieces=[(cut2, send_ff2.shape[1] - cut2)], bufs=[r2_ff2]),
                  _rs_sibling_plan([gw_ff1_16])])
    (dx2, dm, dg3, dg2), got = _ff1_bwd_norms(d_f1, w_f1t, dy, x2, norm_mlp_pre, m, norm_mix_post, carry=plan)
    (r2_ff2,), r1_ff1 = plan.split(got)
    ((own_ff1, send_ff1),) = _rs_sums([gw_ff1_32], r1_ff1, "ff1")
    cut = send_ff1.shape[1] // 4
    (gw_o_32, gw_o_16), _ = _wgrad(mix, dm, "wgrad_o")
    (d_br_a, d_br_b, p_ga, p_gb, dbg_a, dbg_b), (r2_ff1,) = _wo_bwd_mix(
        dm, w_og, br_a, br_b, proj, b_gate, ga_block, gb_block,
        carry=_rs_chips_plan([send_ff1], pieces=[(0, cut)]))
    (gw_lu_32, gw_lu_16), _ = _wgrad(y_lru, d_br_a, "wgrad_lru_up")
    (gw_pu_32, gw_pu_16), _ = _wgrad(d_br_b, y_pool, "wgrad_pool_up")
    (dh, p_g), r1_mid = _lru_up_bwd(
        d_br_a, w_lu, proj, h, g_block,
        carry=_rs_sibling_plan([gw_o_16, gw_lu_16, gw_pu_16.reshape(-1, d)]))
    mid = _rs_sums([gw_o_32, gw_lu_32, gw_pu_32.reshape(-1, d)], r1_mid, "mid")
    d_y_pool = _pool_up_bwd(d_br_b, w_put)
    (p_x, dwa, db_a, dwx, db_x, dlam, dconv_w, dconv_b), (r2_ff1,) = _lru_bwd(
        dh, xc, h, proj, conv_w_full, wa_bd, lru_b_a, wx_bd, lru_b_x, lru_lambda,
        carry=_rs_chips_plan([send_ff1], pieces=[(cut, send_ff1.shape[1] - cut)], bufs=[r2_ff1]))
    p_p, dpool_w, dpool_scale = _pool_bwd(d_y_pool, p, pw, pool_scale)
    parts = [p_x, p_g, p_p, p_ga, p_gb]
    gw_in, r2_mid = _wgrad_parts(parts, h1, "wgrad_in", carry=_rs_chips_plan([s for _, s in mid]))
    tail = _rs_level1([gw_in[0], dpool_w.reshape(N_DEV, -1, POOL_GROUP_DIM), dwa, dwx],
                      [gw_in[1], dpool_w.reshape(N_DEV, -1, POOL_GROUP_DIM), dwa, dwx], "in")
    (grad_x, dg1), r2_tail = _win_bwd_norm(parts, w_int, dx2, x2d, norm_mix_pre,
                                           carry=_rs_chips_plan([s for _, s in tail]))

    def flat2(a):
        return a.reshape(a.shape[0], -1, a.shape[-1])

    fin_small, _ = _finals([
        (flat2(tail[1][0]), flat2(r2_tail[1])), (flat2(tail[2][0]), flat2(r2_tail[2])),
        (flat2(tail[3][0]), flat2(r2_tail[3])),
    ], "rs_finals_small")

    def pad_row(a):
        return jnp.pad(a, ((0, 0), (0, d - a.shape[1])))

    vecs = jnp.concatenate([dg1, dg2, dg3, dg4, dbg_a, dbg_b, dconv_b, db_a, db_x, dlam,
                            pad_row(dpool_scale), dconv_w, pad_row(loss_part)], axis=0)
    assert vecs.shape[0] == N_VEC_ROWS
    fin, (vec_parts, g_pool, g_wa, g_wx) = _finals([
        (tail[0][0], r2_tail[0]), (mid[1][0], r2_mid[1]), (mid[2][0], r2_mid[2]), (mid[0][0], r2_mid[0]),
        (own_ff1, r2_ff1), (own_ff2, r2_ff2),
    ], "rs_finals", carry=_ag_plan([vecs] + fin_small))
    g_w_lru_up = fin[1]
    g_w_pool_up = fin[2].reshape(d // N_DEV, d_pool).T
    g_w_o = fin[3]
    g_w_ff1 = fin[4].T
    g_w_ff2 = fin[5]

    big_names = ["w_in", "w_lru_up", "w_pool_up", "w_o", "w_ff1", "w_ff2"]
    big_w = [w_in[0].T, w_lru_up[0], w_pool_up[0], w_o[0], w_ff1[0], w_ff2[0]]
    big_g = [fin[0], g_w_lru_up, g_w_pool_up, g_w_o, g_w_ff1, g_w_ff2]
    big_m = [m_w_in[0].T, m_w_lru_up[0], m_w_pool_up[0], m_w_o[0], m_w_ff1[0], m_w_ff2[0]]
    big_v = [v_w_in[0].T, v_w_lru_up[0], v_w_pool_up[0], v_w_o[0], v_w_ff1[0], v_w_ff2[0]]
    big_out, _ = _adamw_big(big_w, big_g, big_m, big_v)
    big_g[0] = big_g[0].T
    big_out[0] = tuple(o.T for o in big_out[0])

    small = dict(
        norm_mix_pre=(norm_mix_pre, m_norm_mix_pre, v_norm_mix_pre),
        norm_mix_post=(norm_mix_post, m_norm_mix_post, v_norm_mix_post),
        norm_mlp_pre=(norm_mlp_pre, m_norm_mlp_pre, v_norm_mlp_pre),
        norm_mlp_post=(norm_mlp_post, m_norm_mlp_post, v_norm_mlp_post),
        b_gate=(b_gate, m_b_gate, v_b_gate), conv_w=(conv_w, m_conv_w, v_conv_w),
        conv_b=(conv_b, m_conv_b, v_conv_b), lru_w_a=(lru_w_a, m_lru_w_a, v_lru_w_a),
        lru_b_a=(lru_b_a, m_lru_b_a, v_lru_b_a), lru_w_x=(lru_w_x, m_lru_w_x, v_lru_w_x),
        lru_b_x=(lru_b_x, m_lru_b_x, v_lru_b_x), lru_lambda=(lru_lambda, m_lru_lambda, v_lru_lambda),
        pool_w=(pool_w, m_pool_w, v_pool_w), pool_scale=(pool_scale, m_pool_scale, v_pool_scale))
    loss_row, small_out = _adamw_small(
        vec_parts, g_pool.reshape(pool_w.shape), g_wa.reshape(lru_w_a.shape), g_wx.reshape(lru_w_x.shape),
        jnp.reshape(me, (1,)).astype(jnp.int32), small)
    grads = {n: o[0] for n, o in small_out.items()}
    delta = {n: o[1] for n, o in small_out.items()}
    new_m = {n: o[2] for n, o in small_out.items()}
    new_v = {n: o[3] for n, o in small_out.items()}

    for name, g, (dl, nm, nv) in zip(big_names, big_g, big_out):
        grads[name], delta[name], new_m[name], new_v[name] = g[None], dl[None], nm[None], nv[None]

    loss = loss_row[0, 0]
    order = ["norm_mix_pre", "norm_mix_post", "norm_mlp_pre", "norm_mlp_post", "w_in", "b_gate", "conv_w",
             "conv_b", "lru_w_a", "lru_b_a", "lru_w_x", "lru_b_x", "lru_lambda", "pool_w", "pool_scale",
             "w_lru_up", "w_pool_up", "w_o", "w_ff1", "w_ff2"]
    return (loss, grad_x[None], *[grads[n] for n in order], *[delta[n] for n in order],
            *[new_m[n] for n in order], *[new_v[n] for n in order])
```

```python
import functools
import math
import operator
import types

import jax
import jax.numpy as jnp
from jax import lax
from jax.experimental import pallas as pl
from jax.experimental.pallas import tpu as pltpu

F32 = jnp.float32
BF16 = jnp.bfloat16
NORM_EPS = 1e-6
LRU_C = 8.0
N_LRU_HEADS = 16
LRU_HEAD_DIM = 64
POOL_WINDOWS = (2, 4, 8, 16)
POOL_GROUP_DIM = 128
ADAM_LR = 0.001
ADAM_B1 = 0.9
ADAM_B2 = 0.999
ADAM_EPS = 1e-08
ADAM_WD = 0.01
ADAM_STEP = 10
N_DEV = 8
V7X_VMEM_LIMIT_BYTES = 56 * 1024 * 1024
LRU_CB = 256
MESH = pl.DeviceIdType.MESH
ANY = pl.BlockSpec(memory_space=pl.ANY)


def _tile(n, pref):
    t = min(n, pref)
    assert n % t == 0, (n, pref)
    return t


def _dot_nn(a, b):
    return lax.dot_general(a, b, (((1,), (0,)), ((), ())), preferred_element_type=F32)


def _dot_nt(a, b):
    return lax.dot_general(a, b, (((1,), (1,)), ((), ())), preferred_element_type=F32)


def _dot_tn(a, b):
    return lax.dot_general(a, b, (((0,), (0,)), ((), ())), preferred_element_type=F32)


def _row_chunks(n_rows, fn, chunk=256):
    chunk = min(chunk, n_rows)
    assert n_rows % chunk == 0

    def step(r, carry):
        fn(pl.ds(pl.multiple_of(r * chunk, chunk), chunk))
        return carry

    lax.fori_loop(0, n_rows // chunk, step, 0)


def _sig(x):
    return 1.0 / (1.0 + jnp.exp(-x))


def _rms_hat(x):
    r = lax.rsqrt(jnp.mean(x * x, axis=-1, keepdims=True) + NORM_EPS)
    return x * r, r


def _rms_bwd(dn, xhat, r, g):
    q = dn * g
    dx = r * (q - xhat * jnp.mean(q * xhat, axis=-1, keepdims=True))
    dg = jnp.sum(dn * xhat, axis=0, keepdims=True)
    return dx, dg


_GELU_K = math.sqrt(2.0 / math.pi)
_GELU_C = 0.044715


def _gelu_and_grad(g):
    t = jnp.tanh(_GELU_K * (g + _GELU_C * g * g * g))
    val = 0.5 * g * (1.0 + t)
    grad = 0.5 * (1.0 + t) + 0.5 * g * (1.0 - t * t) * (_GELU_K * (1.0 + 3.0 * _GELU_C * g * g))
    return val, grad


def _softplus_neg(lam):
    z = -lam
    e = jnp.exp(-jnp.abs(z))
    u = 1.0 + e
    d = u - 1.0
    l1p = jnp.where(d == 0.0, e, jnp.log(u) * (e / jnp.where(d == 0.0, 1.0, d)))
    return jnp.maximum(z, 0.0) + l1p


def _lru_gates(xc, wa, ba, wx, bx, lam):
    xcb = xc.astype(BF16)
    r = _sig(_dot_nn(xcb, wa) + ba)
    i = _sig(_dot_nn(xcb, wx) + bx)
    sp = _softplus_neg(lam)
    log_a = (-LRU_C) * r * sp
    a = jnp.exp(log_a)
    mult = jnp.sqrt(-jnp.tanh(log_a) * (1.0 + a * a))
    return xcb, r, i, sp, log_a, a, mult


def _place():
    return lax.axis_index("x"), lax.axis_index("y"), lax.axis_index("c")


def _ag_plan(shards, pieces=None, bufs=None):
    na = len(shards)
    n_kinds = 7

    def parts(ins, outs, sems):
        send_sems, recv_sems, local_sems = sems
        x, y, c = _place()
        me, sibling = (x, y, c), (x, y, 1 - c)
        x_nb, y_nb, diag = (1 - x, y), (x, 1 - y), (1 - x, 1 - y)
        relay_src = (c * (1 - x) + (1 - c) * x, c * y + (1 - c) * (1 - y))
        relay_dst = (c * x + (1 - c) * (1 - x), c * (1 - y) + (1 - c) * y)

        def own(a):
            return ins[a] if pieces is None else ins[a].at[pl.ds(*pieces[a])]

        def slot(a, px, py, pc):
            idx = 4 * px + 2 * py + pc
            return outs[a].at[idx] if pieces is None else outs[a].at[idx, pl.ds(*pieces[a])]

        def copy(a, k, block, to, src=None):
            return pltpu.make_async_remote_copy(
                src_ref=slot(a, *block) if src is None else src,
                dst_ref=slot(a, *block),
                send_sem=send_sems.at[a * n_kinds + k],
                recv_sem=recv_sems.at[a * n_kinds + k],
                device_id=to,
                device_id_type=MESH,
            )

        mine = [pltpu.make_async_copy(own(a), slot(a, *me), local_sems.at[a]) for a in range(na)]
        first, second, third = [], [], []
        for a in range(na):
            first += [copy(a, 0, me, sibling, src=own(a)), copy(a, 1, me, (*x_nb, c), src=own(a)),
                      copy(a, 2, me, (*y_nb, c), src=own(a))]
            second += [copy(a, 3, (*relay_src, c), (*relay_dst, c)), copy(a, 4, (*x_nb, c), sibling),
                       copy(a, 5, (*y_nb, c), sibling)]
            third.append(copy(a, 6, (*diag, c), sibling))
        return sibling, c, x_nb, y_nb, diag, copy, mine, first, second, third

    def start(ins, outs, sems):
        _, _, _, _, _, _, mine, first, _, _ = parts(ins, outs, sems)
        for cp in mine + first:
            cp.start()

    def middle(ins, outs, sems):
        _, c, x_nb, y_nb, _, copy, _, _, second, _ = parts(ins, outs, sems)
        for a in range(na):
            copy(a, 1, (*x_nb, c), (*x_nb, c)).wait_recv()
            copy(a, 2, (*y_nb, c), (*y_nb, c)).wait_recv()
        for cp in second:
            cp.start()

    def finish(ins, outs, sems):
        sibling, c, x_nb, y_nb, diag, copy, mine, first, second, third = parts(ins, outs, sems)
        for a in range(na):
            copy(a, 3, (*diag, c), (*diag, c)).wait_recv()
            third[a].start()
        for a in range(na):
            copy(a, 0, sibling, sibling).wait_recv()
            copy(a, 4, (*x_nb, 1 - c), sibling).wait_recv()
            copy(a, 5, (*y_nb, 1 - c), sibling).wait_recv()
            copy(a, 6, (*diag, 1 - c), sibling).wait_recv()
        for cp in first + second + third:
            cp.wait_send()
        for cp in mine:
            cp.wait()

    return types.SimpleNamespace(
        ins=list(shards) + list(bufs or []),
        out_shapes=[jax.ShapeDtypeStruct((N_DEV,) + s.shape, s.dtype) for s in shards],
        sems=[pltpu.SemaphoreType.DMA((n_kinds * na,)), pltpu.SemaphoreType.DMA((n_kinds * na,)),
              pltpu.SemaphoreType.DMA((na,))],
        aliases=[(na + a, a) for a in range(na)] if bufs else [],
        peers=frozenset({"sibling", "neighbours"}), start=start, middle=middle, finish=finish)


def _rs_sibling_plan(fulls):
    na = len(fulls)
    rs = [f.shape[0] // N_DEV for f in fulls]

    def copies(ins, outs, sems):
        send_sems, recv_sems = sems
        x, y, c = _place()
        out = []
        for a in range(na):
            for q in range(4):
                shard = 2 * q + (1 - c)
                out.append(pltpu.make_async_remote_copy(
                    src_ref=ins[a].at[pl.ds(shard * rs[a], rs[a])],
                    dst_ref=outs[a].at[q],
                    send_sem=send_sems.at[a * 4 + q],
                    recv_sem=recv_sems.at[a * 4 + q],
                    device_id=(x, y, 1 - c),
                    device_id_type=MESH,
                ))
        return out

    def start(ins, outs, sems):
        for cp in copies(ins, outs, sems):
            cp.start()

    def finish(ins, outs, sems):
        for cp in copies(ins, outs, sems):
            cp.wait()

    return types.SimpleNamespace(
        ins=list(fulls),
        out_shapes=[jax.ShapeDtypeStruct((4, r) + f.shape[1:], f.dtype) for r, f in zip(rs, fulls)],
        sems=[pltpu.SemaphoreType.DMA((4 * na,)), pltpu.SemaphoreType.DMA((4 * na,))],
        peers=frozenset({"sibling"}), start=start, finish=finish)


def _rs_chips_plan(sends, pieces=None, bufs=None):
    na = len(sends)

    def copies(ins, outs, sems):
        send_sems, recv_sems = sems
        x, y, c = _place()
        chips = [(1 - x, y), (x, 1 - y), (1 - x, 1 - y)]
        out = []
        for a in range(na):
            for k, chip in enumerate(chips):
                rows = (k,) if pieces is None else (k, pl.ds(*pieces[a]))
                out.append(pltpu.make_async_remote_copy(
                    src_ref=ins[a].at[rows],
                    dst_ref=outs[a].at[rows],
                    send_sem=send_sems.at[a * 3 + k],
                    recv_sem=recv_sems.at[a * 3 + k],
                    device_id=(*chip, c),
                    device_id_type=MESH,
                ))
        return out

    def start(ins, outs, sems):
        for cp in copies(ins, outs, sems):
            cp.start()

    def finish(ins, outs, sems):
        for cp in copies(ins, outs, sems):
            cp.wait()

    return types.SimpleNamespace(
        ins=list(sends) + list(bufs or []),
        out_shapes=[jax.ShapeDtypeStruct(s.shape, s.dtype) for s in sends],
        sems=[pltpu.SemaphoreType.DMA((3 * na,)), pltpu.SemaphoreType.DMA((3 * na,))],
        aliases=[(na + a, a) for a in range(na)] if bufs else [],
        peers=frozenset({"chips"}), start=start, finish=finish)


def _join(plans):
    ins, outs, sems, aliases, offs = [], [], [], [], []
    for p in plans:
        offs.append((len(ins), len(outs), len(sems)))
        aliases += [(len(ins) + ci, len(outs) + co) for ci, co in getattr(p, "aliases", [])]
        ins += p.ins
        outs += p.out_shapes
        sems += p.sems

    def cut(p, off, i, o, s):
        return (i[off[0]:off[0] + len(p.ins)], o[off[1]:off[1] + len(p.out_shapes)],
                s[off[2]:off[2] + len(p.sems)])

    def start(i, o, s):
        for p, off in zip(plans, offs):
            p.start(*cut(p, off, i, o, s))

    def middle(i, o, s):
        for p, off in zip(plans, offs):
            if getattr(p, "middle", None) is not None:
                p.middle(*cut(p, off, i, o, s))

    def finish(i, o, s):
        for p, off in zip(plans, offs):
            p.finish(*cut(p, off, i, o, s))

    def split(results):
        return [list(results[off[1]:off[1] + len(p.out_shapes)]) for p, off in zip(plans, offs)]

    return types.SimpleNamespace(ins=ins, out_shapes=outs, sems=sems, aliases=aliases,
                                 peers=frozenset().union(*[p.peers for p in plans]),
                                 start=start, middle=middle, finish=finish, split=split)


COLLECTIVE_ID = {frozenset({"sibling"}): 0, frozenset({"chips"}): 1, frozenset({"sibling", "chips"}): 2,
                 frozenset({"sibling", "neighbours"}): 3}


def _handshake(peers):
    x, y, c = _place()
    devs = []
    if "sibling" in peers:
        devs.append((x, y, 1 - c))
    if "neighbours" in peers:
        devs += [(1 - x, y, c), (x, 1 - y, c)]
    if "chips" in peers:
        assert "neighbours" not in peers
        devs += [(1 - x, y, c), (x, 1 - y, c), (1 - x, 1 - y, c)]
    barrier = pltpu.get_barrier_semaphore()
    for dev in devs:
        pl.semaphore_signal(barrier, inc=1, device_id=dev, device_id_type=MESH)
    pl.semaphore_wait(barrier, len(devs))


def _in_hbm(args):
    return [pltpu.with_memory_space_constraint(a, pltpu.HBM) for a in args]


def _run_plan(plan, name):
    n_in, n_out = len(plan.ins), len(plan.out_shapes)

    def body(*refs):
        ins, outs, sems = refs[:n_in], refs[n_in:n_in + n_out], refs[n_in + n_out:]
        _handshake(plan.peers)
        plan.start(ins, outs, sems)
        if getattr(plan, "middle", None) is not None:
            plan.middle(ins, outs, sems)
        plan.finish(ins, outs, sems)

    return pl.pallas_call(
        body,
        name=name,
        in_specs=[ANY] * n_in,
        out_specs=[ANY] * n_out,
        out_shape=plan.out_shapes,
        scratch_shapes=plan.sems,
        input_output_aliases=dict(getattr(plan, "aliases", [])),
        compiler_params=pltpu.CompilerParams(collective_id=COLLECTIVE_ID[plan.peers]),
    )(*_in_hbm(plan.ins))


def _call(body, *, name, grid, in_specs, out_specs, out_shape, args, scratch_shapes=(), aliases=None,
          carry=None):
    n_in, n_out, n_scr = len(in_specs), len(out_shape), len(scratch_shapes)
    params = pltpu.CompilerParams(
        dimension_semantics=("arbitrary",) * len(grid), vmem_limit_bytes=V7X_VMEM_LIMIT_BYTES)
    if carry is None:
        outs = pl.pallas_call(
            body, name=name, grid=grid, in_specs=list(in_specs), out_specs=list(out_specs),
            out_shape=list(out_shape), scratch_shapes=list(scratch_shapes),
            input_output_aliases=aliases or {}, compiler_params=params)(*_in_hbm(args))
        return list(outs), []
    c_in, c_out = len(carry.ins), len(carry.out_shapes)

    def full(*refs):
        p = 0
        ins = refs[p:p + n_in]
        p += n_in
        cins = refs[p:p + c_in]
        p += c_in
        outs = refs[p:p + n_out]
        p += n_out
        couts = refs[p:p + c_out]
        p += c_out
        scr = refs[p:p + n_scr]
        csems = refs[p + n_scr:]
        ids = [pl.program_id(a) for a in range(len(grid))]
        first = functools.reduce(operator.and_, [i == 0 for i in ids])
        last = functools.reduce(operator.and_, [i == g - 1 for i, g in zip(ids, grid)])

        @pl.when(first)
        def _():
            _handshake(carry.peers)
            carry.start(cins, couts, csems)

        if getattr(carry, "middle", None) is not None:
            n_steps = math.prod(grid)
            flat = functools.reduce(lambda acc, ig: acc * ig[1] + ig[0], zip(ids, grid), 0)

            @pl.when(flat == (2 * n_steps) // 3)
            def _():
                carry.middle(cins, couts, csems)

        body(*ins, *outs, *scr)

        @pl.when(last)
        def _():
            carry.finish(cins, couts, csems)

    all_aliases = dict(aliases or {})
    all_aliases.update({n_in + ci: n_out + co for ci, co in getattr(carry, "aliases", [])})
    params = pltpu.CompilerParams(
        dimension_semantics=("arbitrary",) * len(grid), vmem_limit_bytes=V7X_VMEM_LIMIT_BYTES,
        collective_id=COLLECTIVE_ID[carry.peers])
    outs = pl.pallas_call(
        full, name=name, grid=grid,
        in_specs=list(in_specs) + [ANY] * c_in,
        out_specs=list(out_specs) + [ANY] * c_out,
        out_shape=list(out_shape) + list(carry.out_shapes),
        scratch_shapes=list(scratch_shapes) + list(carry.sems),
        input_output_aliases=all_aliases, compiler_params=params)(*_in_hbm(args), *_in_hbm(carry.ins))
    return list(outs[:n_out]), list(outs[n_out:])


def _norm_proj(x, g1, w_int, carry=None):
    t, d = x.shape
    n = w_int.shape[0]
    tt, tn = _tile(t, 2048), _tile(n, 512)

    def body(x_ref, g_ref, w_ref, proj_ref, h1_ref, h1_s):
        @pl.when(pl.program_id(1) == 0)
        def _():
            def norm_rows(rows):
                xhat, _ = _rms_hat(x_ref[rows, :])
                h = (xhat * g_ref[...]).astype(BF16)
                h1_s[rows, :] = h
                h1_ref[rows, :] = h

            _row_chunks(tt, norm_rows)

        proj_ref[...] = _dot_nt(h1_s[...], w_ref[...]).astype(BF16)

    return _call(
        body, name="norm_proj", grid=(t // tt, n // tn),
        in_specs=[
            pl.BlockSpec((tt, d), lambda i, j: (i, 0)),
            pl.BlockSpec((1, d), lambda i, j: (0, 0)),
            pl.BlockSpec((tn, d), lambda i, j: (j, 0)),
        ],
        out_specs=[
            pl.BlockSpec((tt, tn), lambda i, j: (i, j)),
            pl.BlockSpec((tt, d), lambda i, j: (i, 0)),
        ],
        out_shape=[jax.ShapeDtypeStruct((t, n), BF16), jax.ShapeDtypeStruct((t, d), BF16)],
        scratch_shapes=[pltpu.VMEM((tt, d), BF16)],
        args=(x, g1, w_int), carry=carry)


def _scan_rows(av, bv, reverse):
    tc = av.shape[0]
    row = lax.broadcasted_iota(jnp.int32, av.shape, 0)
    s = 1
    while s < tc:
        if s < 8:
            keep = (row < tc - s) if reverse else (row >= s)
            shift = (tc - s) if reverse else s
            a_sh = jnp.where(keep, pltpu.roll(av, shift, 0), 1.0)
            b_sh = jnp.where(keep, pltpu.roll(bv, shift, 0), 0.0)
            bv = av * b_sh + bv
            av = av * a_sh
        elif reverse:
            bv = jnp.concatenate([av[:tc - s] * bv[s:] + bv[:tc - s], bv[tc - s:]], axis=0)
            av = jnp.concatenate([av[:tc - s] * av[s:], av[tc - s:]], axis=0)
        else:
            bv = jnp.concatenate([bv[:s], av[s:] * bv[:tc - s] + bv[s:]], axis=0)
            av = jnp.concatenate([av[:s], av[s:] * av[:tc - s]], axis=0)
        s *= 2
    return av, bv


def _fill_block_diag(w_ref, bd_ref):
    bd_ref[...] = jnp.zeros_like(bd_ref)
    hd = LRU_HEAD_DIM
    for k in range(w_ref.shape[0]):
        bd_ref[k * hd:(k + 1) * hd, k * hd:(k + 1) * hd] = w_ref[k].astype(BF16)


def _lru_fwd(proj, conv_w, conv_b, w_a, b_a, w_x, b_x, lam, carry=None):
    t = proj.shape[0]
    dr = conv_b.shape[1]
    cb = LRU_CB
    tc = _tile(t, 256)
    ncb, ntc = dr // cb, t // tc

    def body(xp_ref, g_ref, cw_ref, cb_ref, wa_ref, ba_ref, wx_ref, bx_ref, lam_ref,
             y_ref, h_ref, xc_ref, prevx_s, hlast_s, wa_s, wx_s):
        c = pl.program_id(1)

        @pl.when(c == 0)
        def _():
            prevx_s[...] = jnp.zeros_like(prevx_s)
            hlast_s[...] = jnp.zeros_like(hlast_s)
            _fill_block_diag(wa_ref, wa_s)
            _fill_block_diag(wx_ref, wx_s)

        x = xp_ref[...].astype(F32)
        prev = prevx_s[...]
        row = lax.broadcasted_iota(jnp.int32, x.shape, 0)

        def sh(j):
            return jnp.where(row >= j, pltpu.roll(x, j, 0), pltpu.roll(prev, j, 0))

        xc = (cb_ref[...] + cw_ref[0:1, :] * sh(3) + cw_ref[1:2, :] * sh(2)
              + cw_ref[2:3, :] * sh(1) + cw_ref[3:4, :] * x)
        prevx_s[...] = x
        xc_ref[...] = xc
        _, _, i, _, _, a, mult = _lru_gates(xc, wa_s[...], ba_ref[...], wx_s[...], bx_ref[...],
                                            lam_ref[...])
        av, bv = _scan_rows(a, mult * (i * xc), reverse=False)
        h = av * hlast_s[...] + bv
        h_ref[...] = h
        hlast_s[...] = h_ref[tc - 1:tc, :]
        gel, _ = _gelu_and_grad(g_ref[...].astype(F32))
        y_ref[...] = (h * gel).astype(BF16)

    vec = pl.BlockSpec((1, cb), lambda j, c: (0, j))
    blk = pl.BlockSpec((tc, cb), lambda j, c: (c, j))
    mat = pl.BlockSpec((cb // LRU_HEAD_DIM, LRU_HEAD_DIM, LRU_HEAD_DIM), lambda j, c: (j, 0, 0))
    return _call(
        body, name="lru_fwd", grid=(ncb, ntc),
        in_specs=[
            blk,
            pl.BlockSpec((tc, cb), lambda j, c: (c, ncb + j)),
            pl.BlockSpec((4, cb), lambda j, c: (0, j)),
            vec, mat, vec, mat, vec, vec,
        ],
        out_specs=[blk, blk, blk],
        out_shape=[
            jax.ShapeDtypeStruct((t, dr), BF16),
            jax.ShapeDtypeStruct((t, dr), F32),
            jax.ShapeDtypeStruct((t, dr), F32),
        ],
        scratch_shapes=[pltpu.VMEM((tc, cb), F32), pltpu.VMEM((1, cb), F32),
                        pltpu.VMEM((cb, cb), BF16), pltpu.VMEM((cb, cb), BF16)],
        args=(proj, proj, conv_w, conv_b, w_a, b_a, w_x, b_x, lam), carry=carry)


def _pool_select(col, vals):
    out = vals[3]
    for g in (2, 1, 0):
        out = jnp.where(col < (g + 1) * POOL_GROUP_DIM, vals[g], out)
    return out


def _pool_fwd(proj, pool_w, pool_scale, col_block):
    t = proj.shape[0]
    dp = pool_scale.shape[1]
    tc = _tile(t, 256)
    ntc = t // tc

    def body(x_ref, w_ref, sc_ref, y_ref, p_ref, px, p2, p4, p8):
        c = pl.program_id(0)

        @pl.when(c == 0)
        def _():
            for s in (px, p2, p4, p8):
                s[...] = jnp.zeros_like(s)

        x = x_ref[...].astype(F32)
        row = lax.broadcasted_iota(jnp.int32, x.shape, 0)
        col = lax.broadcasted_iota(jnp.int32, x.shape, 1)

        def sh(v, pv, j):
            return jnp.where(row >= j, pltpu.roll(v, j, 0), pltpu.roll(pv[...], j, 0))

        s2 = x + sh(x, px, 1)
        s4 = s2 + sh(s2, p2, 2)
        s8 = s4 + sh(s4, p4, 4)
        s16 = s8 + sh(s8, p8, 8)
        px[...] = x
        p2[...] = s2
        p4[...] = s4
        p8[...] = s8
        wsum = _pool_select(col, (s2, s4, s8, s16))
        win = _pool_select(col, POOL_WINDOWS)
        cnt = jnp.minimum(c * tc + row + 1, win).astype(F32)
        p = wsum / cnt - x
        pb = p.astype(BF16)
        p_ref[...] = pb
        for g in range(len(POOL_WINDOWS)):
            sl = slice(g * POOL_GROUP_DIM, (g + 1) * POOL_GROUP_DIM)
            yg = _dot_nn(pb[:, sl], w_ref[g]) * sc_ref[:, sl]
            y_ref[:, sl] = yg.astype(BF16)

    return _call(
        body, name="pool_fwd", grid=(ntc,),
        in_specs=[
            pl.BlockSpec((tc, dp), lambda c: (c, col_block)),
            pl.BlockSpec(pool_w.shape, lambda c: (0, 0, 0)),
            pl.BlockSpec((1, dp), lambda c: (0, 0)),
        ],
        out_specs=[pl.BlockSpec((tc, dp), lambda c: (c, 0))] * 2,
        out_shape=[jax.ShapeDtypeStruct((t, dp), BF16)] * 2,
        scratch_shapes=[pltpu.VMEM((tc, dp), F32)] * 4,
        args=(proj, pool_w, pool_scale))[0]


def _branch_mix(y_lru, y_pool, w_lru_up, w_pool_upt, proj, b_gate, ga_block, gb_block, carry=None):
    t, d = y_lru.shape
    dp = y_pool.shape[1]
    tt, tn = _tile(t, 1024), 512
    nj = d // tn

    def body(yl_ref, yp_ref, wl_ref, wp_ref, ga_ref, gb_ref, ba_ref, bb_ref, bra_ref, brb_ref, mix_ref):
        br_a = _dot_nn(yl_ref[...], wl_ref[...])
        br_b = _dot_nt(yp_ref[...], wp_ref[...])
        bra_ref[...] = br_a.astype(BF16)
        brb_ref[...] = br_b.astype(BF16)
        ga = _sig(ga_ref[...].astype(F32) + ba_ref[...])
        gb = _sig(gb_ref[...].astype(F32) + bb_ref[...])
        mix_ref[...] = (ga * br_a + gb * br_b).astype(BF16)

    out = pl.BlockSpec((tt, tn), lambda j, i: (i, j))
    return _call(
        body, name="branch_mix", grid=(nj, t // tt),
        in_specs=[
            pl.BlockSpec((tt, d), lambda j, i: (i, 0)),
            pl.BlockSpec((tt, dp), lambda j, i: (i, 0)),
            pl.BlockSpec((d, tn), lambda j, i: (0, j)),
            pl.BlockSpec((tn, dp), lambda j, i: (j, 0)),
            pl.BlockSpec((tt, tn), lambda j, i: (i, ga_block + j)),
            pl.BlockSpec((tt, tn), lambda j, i: (i, gb_block + j)),
            pl.BlockSpec((1, tn), lambda j, i: (0, j)),
            pl.BlockSpec((1, tn), lambda j, i: (0, nj + j)),
        ],
        out_specs=[out, out, out],
        out_shape=[jax.ShapeDtypeStruct((t, d), BF16)] * 3,
        args=(y_lru, y_pool, w_lru_up, w_pool_upt, proj, proj, b_gate, b_gate), carry=carry)


def _wo_norm(mix, w_o, x, g2, g3, carry=None):
    t, d = x.shape
    tt = _tile(t, 512)

    def body(mix_ref, w_ref, x_ref, g2_ref, g3_ref, m_ref, x2_ref, h3_ref):
        m = _dot_nn(mix_ref[...], w_ref[...])
        m_ref[...] = m
        mhat, _ = _rms_hat(m)
        x2 = x_ref[...] + mhat * g2_ref[...]
        x2_ref[...] = x2
        xhat, _ = _rms_hat(x2)
        h3_ref[...] = (xhat * g3_ref[...]).astype(BF16)

    row = pl.BlockSpec((tt, d), lambda i: (i, 0))
    vec = pl.BlockSpec((1, d), lambda i: (0, 0))
    return _call(
        body, name="wo_norm", grid=(t // tt,),
        in_specs=[row, pl.BlockSpec((d, d), lambda i: (0, 0)), row, vec, vec],
        out_specs=[row, row, row],
        out_shape=[
            jax.ShapeDtypeStruct((t, d), F32),
            jax.ShapeDtypeStruct((t, d), F32),
            jax.ShapeDtypeStruct((t, d), BF16),
        ],
        args=(mix, w_o, x, g2, g3), carry=carry)


def _ff1(h3, w_ff1t, carry=None):
    t, d = h3.shape
    n = w_ff1t.shape[0]
    tt, tn = _tile(t, 2048), _tile(n, 512)

    def body(h_ref, w_ref, rf_ref):
        rf_ref[...] = jnp.maximum(_dot_nt(h_ref[...], w_ref[...]), 0.0).astype(BF16)

    out = pl.BlockSpec((tt, tn), lambda i, j: (i, j))
    return _call(
        body, name="ff1", grid=(t // tt, n // tn),
        in_specs=[pl.BlockSpec((tt, d), lambda i, j: (i, 0)), pl.BlockSpec((tn, d), lambda i, j: (j, 0))],
        out_specs=[out],
        out_shape=[jax.ShapeDtypeStruct((t, n), BF16)],
        args=(h3, w_ff1t), carry=carry)


def _ff2_loss(rf, w_ff2, x2, g4, target):
    t, k = rf.shape
    d = x2.shape[1]
    tt, tk = _tile(t, 1024), _tile(k, 512)
    nk = k // tk

    def body(a_ref, w_ref, x2_ref, g_ref, tg_ref, dy_ref, df_ref, dg_ref, loss_ref, acc):
        i, kk = pl.program_id(0), pl.program_id(1)

        @pl.when(kk == 0)
        def _():
            acc[...] = jnp.zeros_like(acc)

        @pl.when((i == 0) & (kk == 0))
        def _():
            dg_ref[...] = jnp.zeros_like(dg_ref)
            loss_ref[...] = jnp.zeros_like(loss_ref)

        rf_tile = a_ref[...]
        acc[...] += _dot_nn(rf_tile * rf_tile, w_ref[...])

        @pl.when(kk == nk - 1)
        def _():
            def tail(rows):
                fhat, r = _rms_hat(acc[rows, :])
                g = g_ref[...]
                e = x2_ref[rows, :] + fhat * g - tg_ref[rows, :]
                loss_ref[...] += 0.5 * jnp.sum(jnp.mean(e * e, axis=-1, keepdims=True))
                dy = e * (1.0 / d)
                dy_ref[rows, :] = dy.astype(BF16)
                df, dg = _rms_bwd(dy, fhat, r, g)
                df_ref[rows, :] = df.astype(BF16)
                dg_ref[...] += dg

            _row_chunks(tt, tail)

    row = pl.BlockSpec((tt, d), lambda i, kk: (i, 0))
    vec = pl.BlockSpec((1, d), lambda i, kk: (0, 0))
    return _call(
        body, name="ff2_loss", grid=(t // tt, nk),
        in_specs=[
            pl.BlockSpec((tt, tk), lambda i, kk: (i, kk)),
            pl.BlockSpec((tk, d), lambda i, kk: (kk, 0)),
            row, vec, row,
        ],
        out_specs=[row, row, vec, pl.BlockSpec((1, 128), lambda i, kk: (0, 0))],
        out_shape=[
            jax.ShapeDtypeStruct((t, d), BF16),
            jax.ShapeDtypeStruct((t, d), BF16),
            jax.ShapeDtypeStruct((1, d), F32),
            jax.ShapeDtypeStruct((1, 128), F32),
        ],
        scratch_shapes=[pltpu.VMEM((tt, d), F32)],
        args=(rf, w_ff2, x2, g4, target))[0]


def _ff2_bwd(df, w_ff2, rf, carry=None):
    t, d = df.shape
    n = w_ff2.shape[0]
    tt, tn = _tile(t, 2048), _tile(n, 512)

    def body(df_ref, w_ref, rf_ref, out_ref):
        d_act = _dot_nt(df_ref[...], w_ref[...])
        out_ref[...] = (d_act * (2.0 * rf_ref[...].astype(F32))).astype(BF16)

    blk = pl.BlockSpec((tt, tn), lambda i, j: (i, j))
    return _call(
        body, name="ff2_bwd", grid=(t // tt, n // tn),
        in_specs=[pl.BlockSpec((tt, d), lambda i, j: (i, 0)), pl.BlockSpec((tn, d), lambda i, j: (j, 0)), blk],
        out_specs=[blk],
        out_shape=[jax.ShapeDtypeStruct((t, n), BF16)],
        args=(df, w_ff2, rf), carry=carry)


def _wgrad(a, b, name, prev=None, row_off=0, rows=None, carry=None, square_a=False):
    t, m = a.shape
    n = b.shape[1]
    rows = m if rows is None else rows
    tm, tk = _tile(m, 512), _tile(t, 2048)
    nk = t // tk
    assert row_off % tm == 0
    off = row_off // tm

    def body(*refs):
        a_ref, b_ref = refs[0], refs[1]
        o32_ref, o16_ref, acc = refs[-3], refs[-2], refs[-1]
        kk = pl.program_id(1)

        @pl.when(kk == 0)
        def _():
            acc[...] = jnp.zeros_like(acc)

        a_tile = a_ref[...]
        acc[...] += _dot_tn(a_tile * a_tile if square_a else a_tile, b_ref[...])

        @pl.when(kk == nk - 1)
        def _():
            o32_ref[...] = acc[...]
            o16_ref[...] = acc[...].astype(BF16)

    in_specs = [pl.BlockSpec((tk, tm), lambda i, kk: (kk, i)), pl.BlockSpec((tk, n), lambda i, kk: (kk, 0))]
    args = [a, b]
    aliases = {}
    if prev is not None:
        in_specs += [ANY, ANY]
        args += list(prev)
        aliases = {2: 0, 3: 1}
    out = pl.BlockSpec((tm, n), lambda i, kk: (off + i, 0))
    return _call(
        body, name=name, grid=(m // tm, nk),
        in_specs=in_specs, out_specs=[out, out],
        out_shape=[jax.ShapeDtypeStruct((rows, n), F32), jax.ShapeDtypeStruct((rows, n), BF16)],
        scratch_shapes=[pltpu.VMEM((tm, n), F32)],
        aliases=aliases, args=args, carry=carry)


def _wgrad_parts(parts, b, name, carry=None):
    t, n = b.shape
    tm = 512
    bounds = []
    lo = 0
    for part in parts:
        assert part.shape[0] == t and part.shape[1] % tm == 0
        bounds.append((lo, lo + part.shape[1] // tm))
        lo += part.shape[1] // tm
    nm = lo
    np_ = len(parts)

    def body(*refs):
        p_refs, b_ref, o32_ref, o16_ref = refs[:np_], refs[np_], refs[np_ + 1], refs[np_ + 2]
        i = pl.program_id(0)
        for (lo_p, hi_p), p_ref in zip(bounds, p_refs):
            @pl.when((i >= lo_p) & (i < hi_p))
            def _(p_ref=p_ref):
                res = _dot_tn(p_ref[...], b_ref[...])
                o32_ref[...] = res
                o16_ref[...] = res.astype(BF16)

    def part_spec(lo_p, hi_p):
        return pl.BlockSpec((t, tm), lambda i: (0, jnp.clip(i - lo_p, 0, hi_p - lo_p - 1)))

    out = pl.BlockSpec((tm, n), lambda i: (i, 0))
    return _call(
        body, name=name, grid=(nm,),
        in_specs=[part_spec(lo_p, hi_p) for lo_p, hi_p in bounds] + [pl.BlockSpec((t, n), lambda i: (0, 0))],
        out_specs=[out, out],
        out_shape=[jax.ShapeDtypeStruct((nm * tm, n), F32), jax.ShapeDtypeStruct((nm * tm, n), BF16)],
        args=(*parts, b), carry=carry)


def _ff1_bwd_norms(d_f1, w_ff1t, dy, x2, g3, m, g2, carry=None):
    t, k = d_f1.shape
    d = x2.shape[1]
    tt, tk = _tile(t, 1024), _tile(k, 512)
    nk = k // tk

    def body(a_ref, w_ref, dy_ref, x2_ref, g3_ref, m_ref, g2_ref, dx2_ref, dm_ref, dg3_ref, dg2_ref, acc):
        i, kk = pl.program_id(0), pl.program_id(1)

        @pl.when(kk == 0)
        def _():
            acc[...] = jnp.zeros_like(acc)

        @pl.when((i == 0) & (kk == 0))
        def _():
            dg3_ref[...] = jnp.zeros_like(dg3_ref)
            dg2_ref[...] = jnp.zeros_like(dg2_ref)

        acc[...] += _dot_nn(a_ref[...], w_ref[...])

        @pl.when(kk == nk - 1)
        def _():
            def tail(rows):
                xhat, r3 = _rms_hat(x2_ref[rows, :])
                dx, dg3 = _rms_bwd(acc[rows, :], xhat, r3, g3_ref[...])
                dx2 = dy_ref[rows, :].astype(F32) + dx
                dx2_ref[rows, :] = dx2
                dg3_ref[...] += dg3
                mhat, r2 = _rms_hat(m_ref[rows, :])
                dm, dg2 = _rms_bwd(dx2, mhat, r2, g2_ref[...])
                dm_ref[rows, :] = dm.astype(BF16)
                dg2_ref[...] += dg2

            _row_chunks(tt, tail)

    row = pl.BlockSpec((tt, d), lambda i, kk: (i, 0))
    vec = pl.BlockSpec((1, d), lambda i, kk: (0, 0))
    return _call(
        body, name="ff1_bwd_norms", grid=(t // tt, nk),
        in_specs=[
            pl.BlockSpec((tt, tk), lambda i, kk: (i, kk)),
            pl.BlockSpec((tk, d), lambda i, kk: (kk, 0)),
            row, row, vec, row, vec,
        ],
        out_specs=[row, row, vec, vec],
        out_shape=[
            jax.ShapeDtypeStruct((t, d), F32),
            jax.ShapeDtypeStruct((t, d), BF16),
            jax.ShapeDtypeStruct((1, d), F32),
            jax.ShapeDtypeStruct((1, d), F32),
        ],
        scratch_shapes=[pltpu.VMEM((tt, d), F32)],
        args=(d_f1, w_ff1t, dy, x2, g3, m, g2), carry=carry)


def _wo_bwd_mix(dm, w_o, br_a, br_b, proj, b_gate, ga_block, gb_block, carry=None):
    t, d = dm.shape
    tt, tn = _tile(t, 1024), 512
    nj = d // tn

    def body(dm_ref, w_ref, bra_ref, brb_ref, ga_ref, gb_ref, ba_ref, bb_ref,
             dbra_ref, dbrb_ref, dga_ref, dgb_ref, dba_ref, dbb_ref):
        i = pl.program_id(1)

        @pl.when(i == 0)
        def _():
            dba_ref[...] = jnp.zeros_like(dba_ref)
            dbb_ref[...] = jnp.zeros_like(dbb_ref)

        d_mix = _dot_nt(dm_ref[...], w_ref[...])
        ga = _sig(ga_ref[...].astype(F32) + ba_ref[...])
        gb = _sig(gb_ref[...].astype(F32) + bb_ref[...])
        dbra_ref[...] = (d_mix * ga).astype(BF16)
        dbrb_ref[...] = (d_mix * gb).astype(BF16)
        dga = d_mix * bra_ref[...].astype(F32) * (ga * (1.0 - ga))
        dgb = d_mix * brb_ref[...].astype(F32) * (gb * (1.0 - gb))
        dga_ref[...] = dga.astype(BF16)
        dgb_ref[...] = dgb.astype(BF16)
        dba_ref[...] += jnp.sum(dga, axis=0, keepdims=True)
        dbb_ref[...] += jnp.sum(dgb, axis=0, keepdims=True)

    blk = pl.BlockSpec((tt, tn), lambda j, i: (i, j))
    vec = pl.BlockSpec((1, tn), lambda j, i: (0, j))
    return _call(
        body, name="wo_bwd_mix", grid=(nj, t // tt),
        in_specs=[
            pl.BlockSpec((tt, d), lambda j, i: (i, 0)),
            pl.BlockSpec((tn, d), lambda j, i: (j, 0)),
            blk, blk,
            pl.BlockSpec((tt, tn), lambda j, i: (i, ga_block + j)),
            pl.BlockSpec((tt, tn), lambda j, i: (i, gb_block + j)),
            vec,
            pl.BlockSpec((1, tn), lambda j, i: (0, nj + j)),
        ],
        out_specs=[blk, blk, blk, blk, vec, vec],
        out_shape=[jax.ShapeDtypeStruct((t, d), BF16)] * 4 + [jax.ShapeDtypeStruct((1, d), F32)] * 2,
        args=(dm, w_o, br_a, br_b, proj, proj, b_gate, b_gate), carry=carry)


def _lru_up_bwd(d_br_a, w_lru_up, proj, h, g_block, carry=None):
    t, d = d_br_a.shape
    tt, tn = _tile(t, 1024), 512

    def body(a_ref, w_ref, g_ref, h_ref, dh_ref, dg_ref):
        d_y = _dot_nt(a_ref[...], w_ref[...])
        gel, gel_grad = _gelu_and_grad(g_ref[...].astype(F32))
        dh_ref[...] = d_y * gel
        dg_ref[...] = (d_y * h_ref[...] * gel_grad).astype(BF16)

    blk = pl.BlockSpec((tt, tn), lambda i, j: (i, j))
    return _call(
        body, name="lru_up_bwd", grid=(t // tt, d // tn),
        in_specs=[
            pl.BlockSpec((tt, d), lambda i, j: (i, 0)),
            pl.BlockSpec((tn, d), lambda i, j: (j, 0)),
            pl.BlockSpec((tt, tn), lambda i, j: (i, g_block + j)),
            blk,
        ],
        out_specs=[blk, blk],
        out_shape=[jax.ShapeDtypeStruct((t, d), F32), jax.ShapeDtypeStruct((t, d), BF16)],
        args=(d_br_a, w_lru_up, proj, h), carry=carry)


def _lru_bwd(dh, xc, h, proj, conv_w, w_a, b_a, w_x, b_x, lam, carry=None):
    t, dr = dh.shape
    cb = LRU_CB
    hd = LRU_HEAD_DIM
    per = cb // hd
    tc = _tile(t, 256)
    ncb, ntc = dr // cb, t // tc

    def body(dh_ref, xc_ref, h_ref, hp_ref, xp_ref, cw_ref, wa_ref, ba_ref, wx_ref, bx_ref, lam_ref,
             dxp_ref, dwa_ref, dba_ref, dwx_ref, dbx_ref, dlam_ref, dcw_ref, dcb_ref,
             nextd_s, anext_s, gnext_s, tmp_s, wa_s, wx_s):
        c = pl.program_id(1)
        rc = ntc - 1 - c

        @pl.when(c == 0)
        def _():
            nextd_s[...] = jnp.zeros_like(nextd_s)
            anext_s[...] = jnp.zeros_like(anext_s)
            gnext_s[...] = jnp.zeros_like(gnext_s)
            for ref in (dwa_ref, dba_ref, dwx_ref, dbx_ref, dlam_ref, dcw_ref, dcb_ref):
                ref[...] = jnp.zeros_like(ref)
            _fill_block_diag(wa_ref, wa_s)
            _fill_block_diag(wx_ref, wx_s)

        xc = xc_ref[...]
        wa, wx, lam = wa_s[...], wx_s[...], lam_ref[...]
        xcb, r, i, sp, log_a, a, mult = _lru_gates(xc, wa, ba_ref[...], wx, bx_ref[...], lam)
        row = lax.broadcasted_iota(jnp.int32, xc.shape, 0)
        h = h_ref[...]
        hp = jnp.where(rc == 0, 0.0, hp_ref[...])
        hprev = jnp.where(row >= 1, pltpu.roll(h, 1, 0), pltpu.roll(hp, 1, 0))

        def up(v, nv, j):
            return jnp.where(row < tc - j, pltpu.roll(v, tc - j, 0), nv)

        av, bv = _scan_rows(up(a, anext_s[...], 1), dh_ref[...], reverse=True)
        gt = av * gnext_s[...] + bv
        tmp_s[...] = gt
        gnext_s[...] = tmp_s[0:1, :]
        tmp_s[...] = a
        anext_s[...] = tmp_s[0:1, :]

        da = gt * hprev
        ixc = i * xc
        d_mult = gt * ixc
        d_i = gt * mult * xc
        d_xc = gt * mult * i
        d_log_a = da * a - d_mult * (a * a) / mult
        d_pre_r = (d_log_a * ((-LRU_C) * sp)) * (r * (1.0 - r))
        d_pre_i = d_i * (i * (1.0 - i))
        d_sp = jnp.sum(d_log_a * ((-LRU_C) * r), axis=0, keepdims=True)
        dlam_ref[...] += d_sp * (-1.0 / (1.0 + jnp.exp(lam)))
        dpr = d_pre_r.astype(BF16)
        dpi = d_pre_i.astype(BF16)
        dba_ref[...] += jnp.sum(d_pre_r, axis=0, keepdims=True)
        dbx_ref[...] += jnp.sum(d_pre_i, axis=0, keepdims=True)
        pa = _dot_tn(xcb, dpr)
        px = _dot_tn(xcb, dpi)
        for k in range(per):
            dwa_ref[k] += pa[k * hd:(k + 1) * hd, k * hd:(k + 1) * hd]
            dwx_ref[k] += px[k * hd:(k + 1) * hd, k * hd:(k + 1) * hd]
        d_xc = d_xc + _dot_nt(dpr, wa) + _dot_nt(dpi, wx)

        nxt = nextd_s[...]
        xp = xp_ref[...].astype(F32)
        dxp = cw_ref[3:4, :] * d_xc
        dcw_ref[3:4, :] += jnp.sum(xp * d_xc, axis=0, keepdims=True)
        for j in (1, 2, 3):
            uj = up(d_xc, pltpu.roll(nxt, tc - j, 0), j)
            dxp = dxp + cw_ref[3 - j:4 - j, :] * uj
            dcw_ref[3 - j:4 - j, :] += jnp.sum(xp * uj, axis=0, keepdims=True)
        dcb_ref[...] += jnp.sum(d_xc, axis=0, keepdims=True)
        nextd_s[...] = d_xc
        dxp_ref[...] = dxp.astype(BF16)

    vec = pl.BlockSpec((1, cb), lambda j, c: (0, j))
    blk = pl.BlockSpec((tc, cb), lambda j, c: (ntc - 1 - c, j))
    mat = pl.BlockSpec((per, hd, hd), lambda j, c: (j, 0, 0))
    cwb = pl.BlockSpec((4, cb), lambda j, c: (0, j))
    return _call(
        body, name="lru_bwd", grid=(ncb, ntc),
        in_specs=[
            blk, blk, blk,
            pl.BlockSpec((tc, cb), lambda j, c: (jnp.maximum(ntc - 2 - c, 0), j)),
            blk, cwb, mat, vec, mat, vec, vec,
        ],
        out_specs=[blk, mat, vec, mat, vec, vec, cwb, vec],
        out_shape=[
            jax.ShapeDtypeStruct((t, dr), BF16),
            jax.ShapeDtypeStruct(w_a.shape, F32),
            jax.ShapeDtypeStruct((1, dr), F32),
            jax.ShapeDtypeStruct(w_x.shape, F32),
            jax.ShapeDtypeStruct((1, dr), F32),
            jax.ShapeDtypeStruct((1, dr), F32),
            jax.ShapeDtypeStruct((4, dr), F32),
            jax.ShapeDtypeStruct((1, dr), F32),
        ],
        scratch_shapes=[
            pltpu.VMEM((tc, cb), F32),
            pltpu.VMEM((1, cb), F32),
            pltpu.VMEM((1, cb), F32),
            pltpu.VMEM((tc, cb), F32),
            pltpu.VMEM((cb, cb), BF16),
            pltpu.VMEM((cb, cb), BF16),
        ],
        args=(dh, xc, h, h, proj, conv_w, w_a, b_a, w_x, b_x, lam), carry=carry)


def _pool_bwd(d_br_b, w_pool_upt, p, pool_w, pool_scale):
    t, d = d_br_b.shape
    dp = w_pool_upt.shape[1]
    tc = _tile(t, 256)
    ntc = t // tc
    ng = len(POOL_WINDOWS)

    def body(db_ref, wu_ref, p_ref, w_ref, sc_ref, dx_ref, dw_ref, dsc_ref, nz, n2, n4, n8, dp_s, dy_s):
        c = pl.program_id(0)
        rc = ntc - 1 - c

        @pl.when(c == 0)
        def _():
            for s in (nz, n2, n4, n8):
                s[...] = jnp.zeros_like(s)
            dw_ref[...] = jnp.zeros_like(dw_ref)
            dsc_ref[...] = jnp.zeros_like(dsc_ref)

        dy_s[...] = _dot_nn(db_ref[...], wu_ref[...])
        for g in range(ng):
            sl = slice(g * POOL_GROUP_DIM, (g + 1) * POOL_GROUP_DIM)
            pg = p_ref[:, sl]
            dyg = dy_s[:, sl]
            wg = w_ref[g].astype(BF16)
            q = _dot_nn(pg, wg)
            dsc_ref[:, sl] += jnp.sum(dyg * q, axis=0, keepdims=True)
            dpw = (dyg * sc_ref[:, sl]).astype(BF16)
            dw_ref[g] += _dot_tn(pg, dpw)
            dp_s[:, sl] = _dot_nt(dpw, wg)

        dpv = dp_s[...]
        row = lax.broadcasted_iota(jnp.int32, dpv.shape, 0)
        col = lax.broadcasted_iota(jnp.int32, dpv.shape, 1)
        win = _pool_select(col, POOL_WINDOWS)
        cnt = jnp.minimum(rc * tc + row + 1, win).astype(F32)
        z = dpv / cnt

        def up(v, nv, j):
            return jnp.where(row < tc - j, pltpu.roll(v, tc - j, 0), pltpu.roll(nv[...], tc - j, 0))

        u2 = z + up(z, nz, 1)
        u4 = u2 + up(u2, n2, 2)
        u8 = u4 + up(u4, n4, 4)
        u16 = u8 + up(u8, n8, 8)
        nz[...] = z
        n2[...] = u2
        n4[...] = u4
        n8[...] = u8
        dx_ref[...] = (_pool_select(col, (u2, u4, u8, u16)) - dpv).astype(BF16)

    blk = pl.BlockSpec((tc, dp), lambda c: (ntc - 1 - c, 0))
    full_w = pl.BlockSpec(pool_w.shape, lambda c: (0, 0, 0))
    vec = pl.BlockSpec((1, dp), lambda c: (0, 0))
    return _call(
        body, name="pool_bwd", grid=(ntc,),
        in_specs=[pl.BlockSpec((tc, d), lambda c: (ntc - 1 - c, 0)), pl.BlockSpec((d, dp), lambda c: (0, 0)),
                  blk, full_w, vec],
        out_specs=[blk, full_w, vec],
        out_shape=[
            jax.ShapeDtypeStruct((t, dp), BF16),
            jax.ShapeDtypeStruct(pool_w.shape, F32),
            jax.ShapeDtypeStruct((1, dp), F32),
        ],
        scratch_shapes=[pltpu.VMEM((tc, dp), F32)] * 6,
        args=(d_br_b, w_pool_upt, p, pool_w, pool_scale))[0]


def _win_bwd_norm(parts, w_int, dx2, x, g1, carry=None):
    t, d = x.shape
    tk = 512
    tt = _tile(t, 1024)
    bounds = []
    k0 = 0
    for part in parts:
        assert part.shape[1] % tk == 0
        bounds.append((k0, k0 + part.shape[1] // tk))
        k0 += part.shape[1] // tk
    nk = k0
    assert nk * tk == w_int.shape[0]
    np_ = len(parts)

    def body(*refs):
        p_refs = refs[:np_]
        w_ref, dx2_ref, x_ref, g_ref, gx_ref, dg_ref, acc = refs[np_:]
        i, kk = pl.program_id(0), pl.program_id(1)

        @pl.when(kk == 0)
        def _():
            acc[...] = jnp.zeros_like(acc)

        @pl.when((i == 0) & (kk == 0))
        def _():
            dg_ref[...] = jnp.zeros_like(dg_ref)

        for (lo, hi), p_ref in zip(bounds, p_refs):
            @pl.when((kk >= lo) & (kk < hi))
            def _(p_ref=p_ref):
                acc[...] += _dot_nn(p_ref[...], w_ref[...])

        @pl.when(kk == nk - 1)
        def _():
            def tail(rows):
                xhat, r = _rms_hat(x_ref[rows, :])
                dx, dg = _rms_bwd(acc[rows, :], xhat, r, g_ref[...])
                gx_ref[rows, :] = dx2_ref[rows, :] + dx
                dg_ref[...] += dg

            _row_chunks(tt, tail)

    def part_spec(lo, hi):
        return pl.BlockSpec((tt, tk), lambda i, kk: (i, jnp.clip(kk - lo, 0, hi - lo - 1)))

    row = pl.BlockSpec((tt, d), lambda i, kk: (i, 0))
    vec = pl.BlockSpec((1, d), lambda i, kk: (0, 0))
    return _call(
        body, name="win_bwd_norm", grid=(t // tt, nk),
        in_specs=[part_spec(lo, hi) for lo, hi in bounds]
        + [pl.BlockSpec((tk, d), lambda i, kk: (kk, 0)), row, row, vec],
        out_specs=[row, vec],
        out_shape=[jax.ShapeDtypeStruct((t, d), F32), jax.ShapeDtypeStruct((1, d), F32)],
        scratch_shapes=[pltpu.VMEM((tt, d), F32)],
        args=(*parts, w_int, dx2, x, g1), carry=carry)


def _adam_math(w, g, m, v):
    m = ADAM_B1 * m + (1.0 - ADAM_B1) * g
    v = ADAM_B2 * v + (1.0 - ADAM_B2) * (g * g)
    m_hat = m / (1.0 - ADAM_B1 ** ADAM_STEP)
    v_hat = v / (1.0 - ADAM_B2 ** ADAM_STEP)
    delta = -ADAM_LR * (m_hat / (jnp.sqrt(v_hat) + ADAM_EPS) + ADAM_WD * w)
    return delta, m, v


def _adamw_big(ws, gs, ms, vs, carry=None):
    n = len(ws)
    nb = 8

    def body(*refs):
        for a in range(n):
            w_ref, g_ref, m_ref, v_ref = refs[4 * a:4 * a + 4]
            d_ref, nm_ref, nv_ref = refs[4 * n + 3 * a:4 * n + 3 * a + 3]
            dl, m, v = _adam_math(w_ref[...], g_ref[...], m_ref[...], v_ref[...])
            d_ref[...] = dl
            nm_ref[...] = m
            nv_ref[...] = v

    in_specs, out_specs, out_shape, args = [], [], [], []
    for w, g, m, v in zip(ws, gs, ms, vs):
        rows, cols = w.shape
        blk = pl.BlockSpec((rows // nb, cols), lambda i: (i, 0))
        in_specs += [blk] * 4
        args += [w, g, m, v]
        out_specs += [blk] * 3
        out_shape += [jax.ShapeDtypeStruct(w.shape, F32)] * 3
    outs, got = _call(body, name="adamw_big", grid=(nb,), in_specs=in_specs, out_specs=out_specs,
                      out_shape=out_shape, args=args, carry=carry)
    return [tuple(outs[3 * a:3 * a + 3]) for a in range(n)], got


SMALL_ORDER = ("norm_mix_pre", "norm_mix_post", "norm_mlp_pre", "norm_mlp_post", "b_gate", "conv_w", "conv_b",
               "lru_w_a", "lru_b_a", "lru_w_x", "lru_b_x", "lru_lambda", "pool_w", "pool_scale")
VEC_ROW = dict(norm_mix_pre=0, norm_mix_post=1, norm_mlp_pre=2, norm_mlp_post=3, conv_b=6, lru_b_a=7,
               lru_b_x=8, lru_lambda=9)
ROW_B_GATE, ROW_POOL_SCALE, ROW_CONV_W, ROW_LOSS, N_VEC_ROWS = 4, 10, 11, 15, 16


def _adamw_small(vec_parts, g_pool, g_wa, g_wx, me, params):
    d = vec_parts.shape[2]
    names = SMALL_ORDER
    n = len(names)
    cw_cols = params["conv_w"][0].shape[2]

    def body(me_ref, vec_ref, vecc_ref, gp_ref, gwa_ref, gwx_ref, *refs):
        wmv = refs[:3 * n]
        loss_ref = refs[3 * n]
        outs = refs[3 * n + 1:3 * n + 1 + 4 * n]
        vs, vsc = refs[3 * n + 1 + 4 * n:]
        acc, accc = vec_ref[0], vecc_ref[0]
        for k in range(1, N_DEV):
            acc = acc + vec_ref[k]
            accc = accc + vecc_ref[k]
        vs[...] = acc
        vsc[...] = accc
        loss_ref[...] = vs[ROW_LOSS:ROW_LOSS + 1, 0:128]

        def upd(a, g, idx):
            w_ref, m_ref, v_ref = wmv[3 * a:3 * a + 3]
            g_ref, d_ref, nm_ref, nv_ref = outs[4 * a:4 * a + 4]
            dl, m, v = _adam_math(w_ref[idx], g, m_ref[idx], v_ref[idx])
            g_ref[idx] = g
            d_ref[idx] = dl
            nm_ref[idx] = m
            nv_ref[idx] = v

        for a, name in enumerate(names):
            if name in VEC_ROW:
                r = VEC_ROW[name]
                upd(a, vs[r:r + 1, :], (slice(None), slice(None)))
            elif name == "b_gate":
                for half in range(2):
                    r = ROW_B_GATE + half
                    upd(a, vs[r:r + 1, :], (slice(None), slice(half * d, (half + 1) * d)))
            elif name == "pool_scale":
                width = params[name][0].shape[1]
                upd(a, vs[ROW_POOL_SCALE:ROW_POOL_SCALE + 1, 0:width], (slice(None), slice(None)))
            elif name == "conv_w":
                upd(a, vsc[ROW_CONV_W:ROW_CONV_W + 4, :], (0,))
            elif name == "pool_w":
                upd(a, gp_ref[...], (Ellipsis,))
            elif name == "lru_w_a":
                upd(a, gwa_ref[...], (Ellipsis,))
            elif name == "lru_w_x":
                upd(a, gwx_ref[...], (Ellipsis,))
            else:
                raise ValueError(name)

    def whole(shape):
        nd = len(shape)
        return pl.BlockSpec(tuple(shape), lambda i, me_ref: (0,) * nd)

    in_specs = [
        whole(vec_parts.shape),
        pl.BlockSpec((N_DEV, N_VEC_ROWS, cw_cols), lambda i, me_ref: (0, 0, me_ref[0])),
        whole(g_pool.shape), whole(g_wa.shape), whole(g_wx.shape),
    ]
    args = [vec_parts, vec_parts, g_pool, g_wa, g_wx]
    out_specs = [whole((1, 128))]
    out_shape = [jax.ShapeDtypeStruct((1, 128), F32)]
    for name in names:
        for arr in params[name]:
            in_specs.append(whole(arr.shape))
            args.append(arr)
        shp = params[name][0].shape
        out_specs += [whole(shp)] * 4
        out_shape += [jax.ShapeDtypeStruct(shp, F32)] * 4
    grid_spec = pltpu.PrefetchScalarGridSpec(
        num_scalar_prefetch=1, grid=(1,), in_specs=in_specs, out_specs=out_specs,
        scratch_shapes=[pltpu.VMEM((N_VEC_ROWS, d), F32), pltpu.VMEM((N_VEC_ROWS, cw_cols), F32)])
    outs = pl.pallas_call(
        body, name="adamw_small", grid_spec=grid_spec, out_shape=out_shape,
        compiler_params=pltpu.CompilerParams(
            dimension_semantics=("arbitrary",), vmem_limit_bytes=V7X_VMEM_LIMIT_BYTES),
    )(me, *_in_hbm(args))
    return outs[0], {name: tuple(outs[1 + 4 * a:5 + 4 * a]) for a, name in enumerate(names)}


def _rs_sum(fulls, recvs, shard_ids, slot_ids, name):
    n = len(fulls)

    def body(sh_ref, sl_ref, *refs):
        s = pl.program_id(0)
        for a in range(n):
            full_ref, recv_ref = refs[2 * a], refs[2 * a + 1]
            own_ref, send_ref = refs[2 * n + 2 * a], refs[2 * n + 2 * a + 1]
            v = full_ref[...] + recv_ref[...].astype(F32)

            @pl.when(s == 0)
            def _(own_ref=own_ref, v=v):
                own_ref[...] = v

            @pl.when(s > 0)
            def _(send_ref=send_ref, v=v):
                send_ref[...] = v.astype(send_ref.dtype)

    in_specs, out_specs, out_shape, args = [], [], [], []
    for full, recv in zip(fulls, recvs):
        r, rest = recv.shape[1], tuple(recv.shape[2:])
        zeros = (0,) * len(rest)
        in_specs += [
            pl.BlockSpec((r,) + rest, lambda s, sh, sl, zeros=zeros: (sh[s],) + zeros),
            pl.BlockSpec((None, r) + rest, lambda s, sh, sl, zeros=zeros: (sl[s], 0) + zeros),
        ]
        out_specs += [
            pl.BlockSpec((None, r) + rest, lambda s, sh, sl, zeros=zeros: (0, 0) + zeros),
            pl.BlockSpec((None, r) + rest, lambda s, sh, sl, zeros=zeros: (jnp.maximum(s - 1, 0), 0) + zeros),
        ]
        out_shape += [jax.ShapeDtypeStruct((1, r) + rest, F32), jax.ShapeDtypeStruct((3, r) + rest, recv.dtype)]
        args += [full, recv]
    grid_spec = pltpu.PrefetchScalarGridSpec(
        num_scalar_prefetch=2, grid=(4,), in_specs=in_specs, out_specs=out_specs)
    outs = pl.pallas_call(
        body,
        name=name,
        grid_spec=grid_spec,
        out_shape=out_shape,
        compiler_params=pltpu.CompilerParams(
            dimension_semantics=("arbitrary",), vmem_limit_bytes=V7X_VMEM_LIMIT_BYTES),
    )(shard_ids, slot_ids, *_in_hbm(args))
    return [(outs[2 * a], outs[2 * a + 1]) for a in range(n)]


def _finals(pairs, name, carry=None):
    nb = 4
    n = len(pairs)

    def body(*refs):
        for a in range(n):
            own_ref, recv_ref = refs[2 * a], refs[2 * a + 1]
            acc = own_ref[...]
            for k in range(3):
                acc = acc + recv_ref[k].astype(F32)
            refs[2 * n + a][...] = acc

    in_specs, out_specs, out_shape, args = [], [], [], []
    for own, recv in pairs:
        _, rows, cols = own.shape
        in_specs += [pl.BlockSpec((None, rows // nb, cols), lambda i: (0, i, 0)),
                     pl.BlockSpec((3, rows // nb, cols), lambda i: (0, i, 0))]
        args += [own, recv]
        out_specs.append(pl.BlockSpec((rows // nb, cols), lambda i: (i, 0)))
        out_shape.append(jax.ShapeDtypeStruct((rows, cols), F32))
    return _call(body, name=name, grid=(nb,), in_specs=in_specs, out_specs=out_specs,
                 out_shape=out_shape, args=args, carry=carry)


def _rs_sums(fulls_f32, recv1, tag):
    x, y, c = _place()
    qs = jnp.stack([2 * x + y, 2 * (1 - x) + y, 2 * x + (1 - y), 2 * (1 - x) + (1 - y)]).astype(jnp.int32)
    shard_ids = 2 * qs + c
    return _rs_sum(fulls_f32, recv1, shard_ids, qs, "rs_sum_" + tag)


def _rs_level1(fulls_f32, fulls_send, tag):
    recv1 = _run_plan(_rs_sibling_plan(fulls_send), "rs_sibling_" + tag)
    return _rs_sums(fulls_f32, recv1, tag)


def _rows(g):
    return g.reshape(g.shape[0] * g.shape[1], g.shape[2])


def kernel(x, norm_mix_pre, norm_mix_post, norm_mlp_pre, norm_mlp_post, w_in, b_gate, conv_w, conv_b, lru_w_a, lru_b_a, lru_w_x, lru_b_x, lru_lambda, pool_w, pool_scale, w_lru_up, w_pool_up, w_o, w_ff1, w_ff2, loss_target, m_norm_mix_pre, m_norm_mix_post, m_norm_mlp_pre, m_norm_mlp_post, m_w_in, m_b_gate, m_conv_w, m_conv_b, m_lru_w_a, m_lru_b_a, m_lru_w_x, m_lru_b_x, m_lru_lambda, m_pool_w, m_pool_scale, m_w_lru_up, m_w_pool_up, m_w_o, m_w_ff1, m_w_ff2, v_norm_mix_pre, v_norm_mix_post, v_norm_mlp_pre, v_norm_mlp_post, v_w_in, v_b_gate, v_conv_w, v_conv_b, v_lru_w_a, v_lru_b_a, v_lru_w_x, v_lru_b_x, v_lru_lambda, v_pool_w, v_pool_scale, v_w_lru_up, v_w_pool_up, v_w_o, v_w_ff1, v_w_ff2):
    t, d = x.shape[1], x.shape[2]
    d_rnn = conv_b.shape[1]
    d_pool = pool_scale.shape[1]
    per = LRU_CB // LRU_HEAD_DIM
    xi, yi, ci = _place()
    me = 4 * xi + 2 * yi + ci

    x2d = x[0]
    tgt = loss_target[0]

    s_in = w_in[0].T.astype(BF16)
    s_lu = w_lru_up[0].astype(BF16)
    s_pu = w_pool_up[0].T.astype(BF16)
    s_o = w_o[0].astype(BF16)
    s_f1 = w_ff1[0].T.astype(BF16)
    s_f2 = w_ff2[0].astype(BF16)
    s_cw = jnp.pad(conv_w[0], ((0, 4), (0, 0)))

    g_in, g_cw = _run_plan(_ag_plan([s_in, s_cw]), "ag_w_in")
    w_int = _rows(g_in)
    conv_w_full = jnp.transpose(g_cw[:, :4, :], (1, 0, 2)).reshape(4, d_rnn)

    wa_bd, wx_bd = lru_w_a[0], lru_w_x[0]
    pw = pool_w[0]
    pw_bf = pw.astype(BF16)

    pool_block = (2 * d_rnn) // d_pool
    ga_block = (2 * d_rnn + d_pool) // 512
    gb_block = ga_block + d // 512
    g_block = d_rnn // 512

    r_f1, r_f2 = s_f1.shape[0], s_f2.shape[0]
    f1_cut = r_f1 // 4
    f2_cut = (3 * r_f2) // 8
    plan = _join([_ag_plan([s_lu, s_pu, s_o]), _ag_plan([s_f1], pieces=[(0, f1_cut)])])
    (proj, h1), got = _norm_proj(x2d, norm_mix_pre, w_int, carry=plan)
    (g_lu, g_pu, g_o), (g_f1,) = plan.split(got)
    w_lu, w_put, w_og = _rows(g_lu), _rows(g_pu), _rows(g_o)
    (y_lru, h, xc), (g_f1,) = _lru_fwd(
        proj, conv_w_full, conv_b, wa_bd, lru_b_a, wx_bd, lru_b_x, lru_lambda,
        carry=_ag_plan([s_f1], pieces=[(f1_cut, r_f1 - f1_cut)], bufs=[g_f1]))
    w_f1t = _rows(g_f1)
    y_pool, p = _pool_fwd(proj, pw_bf, pool_scale, pool_block)
    (br_a, br_b, mix), (g_f2,) = _branch_mix(
        y_lru, y_pool, w_lu, w_put, proj, b_gate, ga_block, gb_block,
        carry=_ag_plan([s_f2], pieces=[(0, f2_cut)]))
    (m, x2, h3), (g_f2,) = _wo_norm(
        mix, w_og, x2d, norm_mix_post, norm_mlp_pre,
        carry=_ag_plan([s_f2], pieces=[(f2_cut, r_f2 // 2 - f2_cut)], bufs=[g_f2]))
    (rf,), (g_f2,) = _ff1(
        h3, w_f1t, carry=_ag_plan([s_f2], pieces=[(r_f2 // 2, r_f2 - r_f2 // 2)], bufs=[g_f2]))
    w_f2 = _rows(g_f2)
    dy, df, dg4, loss_part = _ff2_loss(rf, w_f2, x2, norm_mlp_post, tgt)

    (gw_ff2_32, gw_ff2_16), _ = _wgrad(rf, df, "wgrad_ff2", square_a=True)
    (d_f1,), r1_ff2 = _ff2_bwd(df, w_f2, rf, carry=_rs_sibling_plan([gw_ff2_16]))
    ((own_ff2, send_ff2),) = _rs_sums([gw_ff2_32], r1_ff2, "ff2")
    cut2 = (5 * send_ff2.shape[1]) // 16
    (gw_ff1_32, gw_ff1_16), (r2_ff2,) = _wgrad(
        d_f1, h3, "wgrad_ff1", carry=_rs_chips_plan([send_ff2], pieces=[(0, cut2)]))
    plan = _join([_rs_chips_plan([send_ff2], pieces=[(cut2, send_ff2.shape[1] - cut2)], bufs=[r2_ff2]),
                  _rs_sibling_plan([gw_ff1_16])])
    (dx2, dm, dg3, dg2), got = _ff1_bwd_norms(d_f1, w_f1t, dy, x2, norm_mlp_pre, m, norm_mix_post, carry=plan)
    (r2_ff2,), r1_ff1 = plan.split(got)
    ((own_ff1, send_ff1),) = _rs_sums([gw_ff1_32], r1_ff1, "ff1")
    cut = send_ff1.shape[1] // 4
    (gw_o_32, gw_o_16), _ = _wgrad(mix, dm, "wgrad_o")
    (d_br_a, d_br_b, p_ga, p_gb, dbg_a, dbg_b), (r2_ff1,) = _wo_bwd_mix(
        dm, w_og, br_a, br_b, proj, b_gate, ga_block, gb_block,
        carry=_rs_chips_plan([send_ff1], pieces=[(0, cut)]))
    (gw_lu_32, gw_lu_16), _ = _wgrad(y_lru, d_br_a, "wgrad_lru_up")
    (gw_pu_32, gw_pu_16), _ = _wgrad(d_br_b, y_pool, "wgrad_pool_up")
    (dh, p_g), r1_mid = _lru_up_bwd(
        d_br_a, w_lu, proj, h, g_block,
        carry=_rs_sibling_plan([gw_o_16, gw_lu_16, gw_pu_16.reshape(-1, d)]))
    mid = _rs_sums([gw_o_32, gw_lu_32, gw_pu_32.reshape(-1, d)], r1_mid, "mid")
    (p_x, dwa, db_a, dwx, db_x, dlam, dconv_w, dconv_b), (r2_ff1,) = _lru_bwd(
        dh, xc, h, proj, conv_w_full, wa_bd, lru_b_a, wx_bd, lru_b_x, lru_lambda,
        carry=_rs_chips_plan([send_ff1], pieces=[(cut, send_ff1.shape[1] - cut)], bufs=[r2_ff1]))
    p_p, dpool_w, dpool_scale = _pool_bwd(d_br_b, w_put, p, pw, pool_scale)
    parts = [p_x, p_g, p_p, p_ga, p_gb]
    gw_in, r2_mid = _wgrad_parts(parts, h1, "wgrad_in", carry=_rs_chips_plan([s for _, s in mid]))
    tail = _rs_level1([gw_in[0], dpool_w.reshape(N_DEV, -1, POOL_GROUP_DIM), dwa, dwx],
                      [gw_in[1], dpool_w.reshape(N_DEV, -1, POOL_GROUP_DIM), dwa, dwx], "in")
    (grad_x, dg1), r2_tail = _win_bwd_norm(parts, w_int, dx2, x2d, norm_mix_pre,
                                           carry=_rs_chips_plan([s for _, s in tail]))

    def flat2(a):
        return a.reshape(a.shape[0], -1, a.shape[-1])

    fin_small, _ = _finals([
        (flat2(tail[1][0]), flat2(r2_tail[1])), (flat2(tail[2][0]), flat2(r2_tail[2])),
        (flat2(tail[3][0]), flat2(r2_tail[3])),
    ], "rs_finals_small")

    def pad_row(a):
        return jnp.pad(a, ((0, 0), (0, d - a.shape[1])))

    vecs = jnp.concatenate([dg1, dg2, dg3, dg4, dbg_a, dbg_b, dconv_b, db_a, db_x, dlam,
                            pad_row(dpool_scale), dconv_w, pad_row(loss_part)], axis=0)
    assert vecs.shape[0] == N_VEC_ROWS
    fin, (vec_parts, g_pool, g_wa, g_wx) = _finals([
        (tail[0][0], r2_tail[0]), (mid[1][0], r2_mid[1]), (mid[2][0], r2_mid[2]), (mid[0][0], r2_mid[0]),
        (own_ff1, r2_ff1), (own_ff2, r2_ff2),
    ], "rs_finals", carry=_ag_plan([vecs] + fin_small))
    g_w_lru_up = fin[1]
    g_w_pool_up = fin[2].reshape(d // N_DEV, d_pool).T
    g_w_o = fin[3]
    g_w_ff1 = fin[4].T
    g_w_ff2 = fin[5]

    big_names = ["w_in", "w_lru_up", "w_pool_up", "w_o", "w_ff1", "w_ff2"]
    big_w = [w_in[0].T, w_lru_up[0], w_pool_up[0], w_o[0], w_ff1[0], w_ff2[0]]
    big_g = [fin[0], g_w_lru_up, g_w_pool_up, g_w_o, g_w_ff1, g_w_ff2]
    big_m = [m_w_in[0].T, m_w_lru_up[0], m_w_pool_up[0], m_w_o[0], m_w_ff1[0], m_w_ff2[0]]
    big_v = [v_w_in[0].T, v_w_lru_up[0], v_w_pool_up[0], v_w_o[0], v_w_ff1[0], v_w_ff2[0]]
    big_out, _ = _adamw_big(big_w, big_g, big_m, big_v)
    big_g[0] = big_g[0].T
    big_out[0] = tuple(o.T for o in big_out[0])

    small = dict(
        norm_mix_pre=(norm_mix_pre, m_norm_mix_pre, v_norm_mix_pre),
        norm_mix_post=(norm_mix_post, m_norm_mix_post, v_norm_mix_post),
        norm_mlp_pre=(norm_mlp_pre, m_norm_mlp_pre, v_norm_mlp_pre),
        norm_mlp_post=(norm_mlp_post, m_norm_mlp_post, v_norm_mlp_post),
        b_gate=(b_gate, m_b_gate, v_b_gate), conv_w=(conv_w, m_conv_w, v_conv_w),
        conv_b=(conv_b, m_conv_b, v_conv_b), lru_w_a=(lru_w_a, m_lru_w_a, v_lru_w_a),
        lru_b_a=(lru_b_a, m_lru_b_a, v_lru_b_a), lru_w_x=(lru_w_x, m_lru_w_x, v_lru_w_x),
        lru_b_x=(lru_b_x, m_lru_b_x, v_lru_b_x), lru_lambda=(lru_lambda, m_lru_lambda, v_lru_lambda),
        pool_w=(pool_w, m_pool_w, v_pool_w), pool_scale=(pool_scale, m_pool_scale, v_pool_scale))
    loss_row, small_out = _adamw_small(
        vec_parts, g_pool.reshape(pool_w.shape), g_wa.reshape(lru_w_a.shape), g_wx.reshape(lru_w_x.shape),
        jnp.reshape(me, (1,)).astype(jnp.int32), small)
    grads = {n: o[0] for n, o in small_out.items()}
    delta = {n: o[1] for n, o in small_out.items()}
    new_m = {n: o[2] for n, o in small_out.items()}
    new_v = {n: o[3] for n, o in small_out.items()}

    for name, g, (dl, nm, nv) in zip(big_names, big_g, big_out):
        grads[name], delta[name], new_m[name], new_v[name] = g[None], dl[None], nm[None], nv[None]

    loss = loss_row[0, 0]
    order = ["norm_mix_pre", "norm_mix_post", "norm_mlp_pre", "norm_mlp_post", "w_in", "b_gate", "conv_w",
             "conv_b", "lru_w_a", "lru_b_a", "lru_w_x", "lru_b_x", "lru_lambda", "pool_w", "pool_scale",
             "w_lru_up", "w_pool_up", "w_o", "w_ff1", "w_ff2"]
    return (loss, grad_x[None], *[grads[n] for n in order], *[delta[n] for n in order],
            *[new_m[n] for n in order], *[new_v[n] for n in order])
```

```python
import functools
import math
import operator
import types

import jax
import jax.numpy as jnp
from jax import lax
from jax.experimental import pallas as pl
from jax.experimental.pallas import tpu as pltpu

F32 = jnp.float32
BF16 = jnp.bfloat16
NORM_EPS = 1e-6
LRU_C = 8.0
N_LRU_HEADS = 16
LRU_HEAD_DIM = 64
POOL_WINDOWS = (2, 4, 8, 16)
POOL_GROUP_DIM = 128
ADAM_LR = 0.001
ADAM_B1 = 0.9
ADAM_B2 = 0.999
ADAM_EPS = 1e-08
ADAM_WD = 0.01
ADAM_STEP = 10
N_DEV = 8
V7X_VMEM_LIMIT_BYTES = 56 * 1024 * 1024
LRU_CB = 256
MESH = pl.DeviceIdType.MESH
ANY = pl.BlockSpec(memory_space=pl.ANY)


def _tile(n, pref):
    t = min(n, pref)
    assert n % t == 0, (n, pref)
    return t


def _dot_nn(a, b):
    return lax.dot_general(a, b, (((1,), (0,)), ((), ())), preferred_element_type=F32)


def _dot_nt(a, b):
    return lax.dot_general(a, b, (((1,), (1,)), ((), ())), preferred_element_type=F32)


def _dot_tn(a, b):
    return lax.dot_general(a, b, (((0,), (0,)), ((), ())), preferred_element_type=F32)


def _row_chunks(n_rows, fn, chunk=256):
    chunk = min(chunk, n_rows)
    assert n_rows % chunk == 0

    def step(r, carry):
        fn(pl.ds(pl.multiple_of(r * chunk, chunk), chunk))
        return carry

    lax.fori_loop(0, n_rows // chunk, step, 0)


def _sig(x):
    return 1.0 / (1.0 + jnp.exp(-x))


def _rms_hat(x):
    r = lax.rsqrt(jnp.mean(x * x, axis=-1, keepdims=True) + NORM_EPS)
    return x * r, r


def _rms_bwd(dn, xhat, r, g):
    q = dn * g
    dx = r * (q - xhat * jnp.mean(q * xhat, axis=-1, keepdims=True))
    dg = jnp.sum(dn * xhat, axis=0, keepdims=True)
    return dx, dg


_GELU_K = math.sqrt(2.0 / math.pi)
_GELU_C = 0.044715


def _gelu_and_grad(g):
    t = jnp.tanh(_GELU_K * (g + _GELU_C * g * g * g))
    val = 0.5 * g * (1.0 + t)
    grad = 0.5 * (1.0 + t) + 0.5 * g * (1.0 - t * t) * (_GELU_K * (1.0 + 3.0 * _GELU_C * g * g))
    return val, grad


def _softplus_neg(lam):
    z = -lam
    e = jnp.exp(-jnp.abs(z))
    u = 1.0 + e
    d = u - 1.0
    l1p = jnp.where(d == 0.0, e, jnp.log(u) * (e / jnp.where(d == 0.0, 1.0, d)))
    return jnp.maximum(z, 0.0) + l1p


def _lru_gates(xc, wa, ba, wx, bx, lam):
    xcb = xc.astype(BF16)
    r = _sig(_dot_nn(xcb, wa) + ba)
    i = _sig(_dot_nn(xcb, wx) + bx)
    sp = _softplus_neg(lam)
    log_a = (-LRU_C) * r * sp
    a = jnp.exp(log_a)
    mult = jnp.sqrt(-jnp.tanh(log_a) * (1.0 + a * a))
    return xcb, r, i, sp, log_a, a, mult


def _place():
    return lax.axis_index("x"), lax.axis_index("y"), lax.axis_index("c")


def _ag_plan(shards, pieces=None, bufs=None):
    na = len(shards)
    n_kinds = 7

    def parts(ins, outs, sems):
        send_sems, recv_sems, local_sems = sems
        x, y, c = _place()
        me, sibling = (x, y, c), (x, y, 1 - c)
        x_nb, y_nb, diag = (1 - x, y), (x, 1 - y), (1 - x, 1 - y)
        relay_src = (c * (1 - x) + (1 - c) * x, c * y + (1 - c) * (1 - y))
        relay_dst = (c * x + (1 - c) * (1 - x), c * (1 - y) + (1 - c) * y)

        def own(a):
            return ins[a] if pieces is None else ins[a].at[pl.ds(*pieces[a])]

        def slot(a, px, py, pc):
            idx = 4 * px + 2 * py + pc
            return outs[a].at[idx] if pieces is None else outs[a].at[idx, pl.ds(*pieces[a])]

        def copy(a, k, block, to, src=None):
            return pltpu.make_async_remote_copy(
                src_ref=slot(a, *block) if src is None else src,
                dst_ref=slot(a, *block),
                send_sem=send_sems.at[a * n_kinds + k],
                recv_sem=recv_sems.at[a * n_kinds + k],
                device_id=to,
                device_id_type=MESH,
            )

        mine = [pltpu.make_async_copy(own(a), slot(a, *me), local_sems.at[a]) for a in range(na)]
        first, second, third = [], [], []
        for a in range(na):
            first += [copy(a, 0, me, sibling, src=own(a)), copy(a, 1, me, (*x_nb, c), src=own(a)),
                      copy(a, 2, me, (*y_nb, c), src=own(a))]
            second += [copy(a, 3, (*relay_src, c), (*relay_dst, c)), copy(a, 4, (*x_nb, c), sibling),
                       copy(a, 5, (*y_nb, c), sibling)]
            third.append(copy(a, 6, (*diag, c), sibling))
        return sibling, c, x_nb, y_nb, diag, copy, mine, first, second, third

    def start(ins, outs, sems):
        _, _, _, _, _, _, mine, first, _, _ = parts(ins, outs, sems)
        for cp in mine + first:
            cp.start()

    def middle(ins, outs, sems):
        _, c, x_nb, y_nb, _, copy, _, _, second, _ = parts(ins, outs, sems)
        for a in range(na):
            copy(a, 1, (*x_nb, c), (*x_nb, c)).wait_recv()
            copy(a, 2, (*y_nb, c), (*y_nb, c)).wait_recv()
        for cp in second:
            cp.start()

    def finish(ins, outs, sems):
        sibling, c, x_nb, y_nb, diag, copy, mine, first, second, third = parts(ins, outs, sems)
        for a in range(na):
            copy(a, 3, (*diag, c), (*diag, c)).wait_recv()
            third[a].start()
        for a in range(na):
            copy(a, 0, sibling, sibling).wait_recv()
            copy(a, 4, (*x_nb, 1 - c), sibling).wait_recv()
            copy(a, 5, (*y_nb, 1 - c), sibling).wait_recv()
            copy(a, 6, (*diag, 1 - c), sibling).wait_recv()
        for cp in first + second + third:
            cp.wait_send()
        for cp in mine:
            cp.wait()

    return types.SimpleNamespace(
        ins=list(shards) + list(bufs or []),
        out_shapes=[jax.ShapeDtypeStruct((N_DEV,) + s.shape, s.dtype) for s in shards],
        sems=[pltpu.SemaphoreType.DMA((n_kinds * na,)), pltpu.SemaphoreType.DMA((n_kinds * na,)),
              pltpu.SemaphoreType.DMA((na,))],
        aliases=[(na + a, a) for a in range(na)] if bufs else [],
        peers=frozenset({"sibling", "neighbours"}), start=start, middle=middle, finish=finish)


def _rs_sibling_plan(fulls):
    na = len(fulls)
    rs = [f.shape[0] // N_DEV for f in fulls]

    def copies(ins, outs, sems):
        send_sems, recv_sems = sems
        x, y, c = _place()
        out = []
        for a in range(na):
            for q in range(4):
                shard = 2 * q + (1 - c)
                out.append(pltpu.make_async_remote_copy(
                    src_ref=ins[a].at[pl.ds(shard * rs[a], rs[a])],
                    dst_ref=outs[a].at[q],
                    send_sem=send_sems.at[a * 4 + q],
                    recv_sem=recv_sems.at[a * 4 + q],
                    device_id=(x, y, 1 - c),
                    device_id_type=MESH,
                ))
        return out

    def start(ins, outs, sems):
        for cp in copies(ins, outs, sems):
            cp.start()

    def finish(ins, outs, sems):
        for cp in copies(ins, outs, sems):
            cp.wait()

    return types.SimpleNamespace(
        ins=list(fulls),
        out_shapes=[jax.ShapeDtypeStruct((4, r) + f.shape[1:], f.dtype) for r, f in zip(rs, fulls)],
        sems=[pltpu.SemaphoreType.DMA((4 * na,)), pltpu.SemaphoreType.DMA((4 * na,))],
        peers=frozenset({"sibling"}), start=start, finish=finish)


def _rs_chips_plan(sends, pieces=None, bufs=None):
    na = len(sends)

    def copies(ins, outs, sems):
        send_sems, recv_sems = sems
        x, y, c = _place()
        chips = [(1 - x, y), (x, 1 - y), (1 - x, 1 - y)]
        out = []
        for a in range(na):
            for k, chip in enumerate(chips):
                rows = (k,) if pieces is None else (k, pl.ds(*pieces[a]))
                out.append(pltpu.make_async_remote_copy(
                    src_ref=ins[a].at[rows],
                    dst_ref=outs[a].at[rows],
                    send_sem=send_sems.at[a * 3 + k],
                    recv_sem=recv_sems.at[a * 3 + k],
                    device_id=(*chip, c),
                    device_id_type=MESH,
                ))
        return out

    def start(ins, outs, sems):
        for cp in copies(ins, outs, sems):
            cp.start()

    def finish(ins, outs, sems):
        for cp in copies(ins, outs, sems):
            cp.wait()

    return types.SimpleNamespace(
        ins=list(sends) + list(bufs or []),
        out_shapes=[jax.ShapeDtypeStruct(s.shape, s.dtype) for s in sends],
        sems=[pltpu.SemaphoreType.DMA((3 * na,)), pltpu.SemaphoreType.DMA((3 * na,))],
        aliases=[(na + a, a) for a in range(na)] if bufs else [],
        peers=frozenset({"chips"}), start=start, finish=finish)


def _join(plans):
    ins, outs, sems, aliases, offs = [], [], [], [], []
    for p in plans:
        offs.append((len(ins), len(outs), len(sems)))
        aliases += [(len(ins) + ci, len(outs) + co) for ci, co in getattr(p, "aliases", [])]
        ins += p.ins
        outs += p.out_shapes
        sems += p.sems

    def cut(p, off, i, o, s):
        return (i[off[0]:off[0] + len(p.ins)], o[off[1]:off[1] + len(p.out_shapes)],
                s[off[2]:off[2] + len(p.sems)])

    def start(i, o, s):
        for p, off in zip(plans, offs):
            p.start(*cut(p, off, i, o, s))

    def middle(i, o, s):
        for p, off in zip(plans, offs):
            if getattr(p, "middle", None) is not None:
                p.middle(*cut(p, off, i, o, s))

    def finish(i, o, s):
        for p, off in zip(plans, offs):
            p.finish(*cut(p, off, i, o, s))

    def split(results):
        return [list(results[off[1]:off[1] + len(p.out_shapes)]) for p, off in zip(plans, offs)]

    return types.SimpleNamespace(ins=ins, out_shapes=outs, sems=sems, aliases=aliases,
                                 peers=frozenset().union(*[p.peers for p in plans]),
                                 start=start, middle=middle, finish=finish, split=split)


COLLECTIVE_ID = {frozenset({"sibling"}): 0, frozenset({"chips"}): 1, frozenset({"sibling", "chips"}): 2,
                 frozenset({"sibling", "neighbours"}): 3}


def _handshake(peers):
    x, y, c = _place()
    devs = []
    if "sibling" in peers:
        devs.append((x, y, 1 - c))
    if "neighbours" in peers:
        devs += [(1 - x, y, c), (x, 1 - y, c)]
    if "chips" in peers:
        assert "neighbours" not in peers
        devs += [(1 - x, y, c), (x, 1 - y, c), (1 - x, 1 - y, c)]
    barrier = pltpu.get_barrier_semaphore()
    for dev in devs:
        pl.semaphore_signal(barrier, inc=1, device_id=dev, device_id_type=MESH)
    pl.semaphore_wait(barrier, len(devs))


def _in_hbm(args):
    return [pltpu.with_memory_space_constraint(a, pltpu.HBM) for a in args]


def _run_plan(plan, name):
    n_in, n_out = len(plan.ins), len(plan.out_shapes)

    def body(*refs):
        ins, outs, sems = refs[:n_in], refs[n_in:n_in + n_out], refs[n_in + n_out:]
        _handshake(plan.peers)
        plan.start(ins, outs, sems)
        if getattr(plan, "middle", None) is not None:
            plan.middle(ins, outs, sems)
        plan.finish(ins, outs, sems)

    return pl.pallas_call(
        body,
        name=name,
        in_specs=[ANY] * n_in,
        out_specs=[ANY] * n_out,
        out_shape=plan.out_shapes,
        scratch_shapes=plan.sems,
        input_output_aliases=dict(getattr(plan, "aliases", [])),
        compiler_params=pltpu.CompilerParams(collective_id=COLLECTIVE_ID[plan.peers]),
    )(*_in_hbm(plan.ins))


def _call(body, *, name, grid, in_specs, out_specs, out_shape, args, scratch_shapes=(), aliases=None,
          carry=None):
    n_in, n_out, n_scr = len(in_specs), len(out_shape), len(scratch_shapes)
    params = pltpu.CompilerParams(
        dimension_semantics=("arbitrary",) * len(grid), vmem_limit_bytes=V7X_VMEM_LIMIT_BYTES)
    if carry is None:
        outs = pl.pallas_call(
            body, name=name, grid=grid, in_specs=list(in_specs), out_specs=list(out_specs),
            out_shape=list(out_shape), scratch_shapes=list(scratch_shapes),
            input_output_aliases=aliases or {}, compiler_params=params)(*_in_hbm(args))
        return list(outs), []
    c_in, c_out = len(carry.ins), len(carry.out_shapes)

    def full(*refs):
        p = 0
        ins = refs[p:p + n_in]
        p += n_in
        cins = refs[p:p + c_in]
        p += c_in
        outs = refs[p:p + n_out]
        p += n_out
        couts = refs[p:p + c_out]
        p += c_out
        scr = refs[p:p + n_scr]
        csems = refs[p + n_scr:]
        ids = [pl.program_id(a) for a in range(len(grid))]
        first = functools.reduce(operator.and_, [i == 0 for i in ids])
        last = functools.reduce(operator.and_, [i == g - 1 for i, g in zip(ids, grid)])

        @pl.when(first)
        def _():
            _handshake(carry.peers)
            carry.start(cins, couts, csems)

        if getattr(carry, "middle", None) is not None:
            n_steps = math.prod(grid)
            flat = functools.reduce(lambda acc, ig: acc * ig[1] + ig[0], zip(ids, grid), 0)

            @pl.when(flat == (2 * n_steps) // 3)
            def _():
                carry.middle(cins, couts, csems)

        body(*ins, *outs, *scr)

        @pl.when(last)
        def _():
            carry.finish(cins, couts, csems)

    all_aliases = dict(aliases or {})
    all_aliases.update({n_in + ci: n_out + co for ci, co in getattr(carry, "aliases", [])})
    params = pltpu.CompilerParams(
        dimension_semantics=("arbitrary",) * len(grid), vmem_limit_bytes=V7X_VMEM_LIMIT_BYTES,
        collective_id=COLLECTIVE_ID[carry.peers])
    outs = pl.pallas_call(
        full, name=name, grid=grid,
        in_specs=list(in_specs) + [ANY] * c_in,
        out_specs=list(out_specs) + [ANY] * c_out,
        out_shape=list(out_shape) + list(carry.out_shapes),
        scratch_shapes=list(scratch_shapes) + list(carry.sems),
        input_output_aliases=all_aliases, compiler_params=params)(*_in_hbm(args), *_in_hbm(carry.ins))
    return list(outs[:n_out]), list(outs[n_out:])


def _norm_proj(x, g1, w_int, carry=None):
    t, d = x.shape
    n = w_int.shape[0]
    tt, tn = _tile(t, 2048), _tile(n, 512)

    def body(x_ref, g_ref, w_ref, proj_ref, h1_ref, h1_s):
        @pl.when(pl.program_id(1) == 0)
        def _():
            def norm_rows(rows):
                xhat, _ = _rms_hat(x_ref[rows, :])
                h = (xhat * g_ref[...]).astype(BF16)
                h1_s[rows, :] = h
                h1_ref[rows, :] = h

            _row_chunks(tt, norm_rows)

        proj_ref[...] = _dot_nt(h1_s[...], w_ref[...]).astype(BF16)

    return _call(
        body, name="norm_proj", grid=(t // tt, n // tn),
        in_specs=[
            pl.BlockSpec((tt, d), lambda i, j: (i, 0)),
            pl.BlockSpec((1, d), lambda i, j: (0, 0)),
            pl.BlockSpec((tn, d), lambda i, j: (j, 0)),
        ],
        out_specs=[
            pl.BlockSpec((tt, tn), lambda i, j: (i, j)),
            pl.BlockSpec((tt, d), lambda i, j: (i, 0)),
        ],
        out_shape=[jax.ShapeDtypeStruct((t, n), BF16), jax.ShapeDtypeStruct((t, d), BF16)],
        scratch_shapes=[pltpu.VMEM((tt, d), BF16)],
        args=(x, g1, w_int), carry=carry)


def _scan_rows(av, bv, reverse):
    tc = av.shape[0]
    row = lax.broadcasted_iota(jnp.int32, av.shape, 0)
    s = 1
    while s < tc:
        if s < 8:
            keep = (row < tc - s) if reverse else (row >= s)
            shift = (tc - s) if reverse else s
            a_sh = jnp.where(keep, pltpu.roll(av, shift, 0), 1.0)
            b_sh = jnp.where(keep, pltpu.roll(bv, shift, 0), 0.0)
            bv = av * b_sh + bv
            av = av * a_sh
        elif reverse:
            bv = jnp.concatenate([av[:tc - s] * bv[s:] + bv[:tc - s], bv[tc - s:]], axis=0)
            av = jnp.concatenate([av[:tc - s] * av[s:], av[tc - s:]], axis=0)
        else:
            bv = jnp.concatenate([bv[:s], av[s:] * bv[:tc - s] + bv[s:]], axis=0)
            av = jnp.concatenate([av[:s], av[s:] * av[:tc - s]], axis=0)
        s *= 2
    return av, bv


def _fill_block_diag(w_ref, bd_ref):
    bd_ref[...] = jnp.zeros_like(bd_ref)
    hd = LRU_HEAD_DIM
    for k in range(w_ref.shape[0]):
        bd_ref[k * hd:(k + 1) * hd, k * hd:(k + 1) * hd] = w_ref[k].astype(BF16)


def _lru_fwd(proj, conv_w, conv_b, w_a, b_a, w_x, b_x, lam, carry=None):
    t = proj.shape[0]
    dr = conv_b.shape[1]
    cb = LRU_CB
    tc = _tile(t, 256)
    ncb, ntc = dr // cb, t // tc

    def body(xp_ref, g_ref, cw_ref, cb_ref, wa_ref, ba_ref, wx_ref, bx_ref, lam_ref,
             y_ref, h_ref, xc_ref, prevx_s, hlast_s, wa_s, wx_s):
        c = pl.program_id(1)

        @pl.when(c == 0)
        def _():
            prevx_s[...] = jnp.zeros_like(prevx_s)
            hlast_s[...] = jnp.zeros_like(hlast_s)
            _fill_block_diag(wa_ref, wa_s)
            _fill_block_diag(wx_ref, wx_s)

        x = xp_ref[...].astype(F32)
        prev = prevx_s[...]
        row = lax.broadcasted_iota(jnp.int32, x.shape, 0)

        def sh(j):
            return jnp.where(row >= j, pltpu.roll(x, j, 0), pltpu.roll(prev, j, 0))

        xc = (cb_ref[...] + cw_ref[0:1, :] * sh(3) + cw_ref[1:2, :] * sh(2)
              + cw_ref[2:3, :] * sh(1) + cw_ref[3:4, :] * x)
        prevx_s[...] = x
        xc_ref[...] = xc
        _, _, i, _, _, a, mult = _lru_gates(xc, wa_s[...], ba_ref[...], wx_s[...], bx_ref[...],
                                            lam_ref[...])
        av, bv = _scan_rows(a, mult * (i * xc), reverse=False)
        h = av * hlast_s[...] + bv
        h_ref[...] = h
        hlast_s[...] = h_ref[tc - 1:tc, :]
        gel, _ = _gelu_and_grad(g_ref[...].astype(F32))
        y_ref[...] = (h * gel).astype(BF16)

    vec = pl.BlockSpec((1, cb), lambda j, c: (0, j))
    blk = pl.BlockSpec((tc, cb), lambda j, c: (c, j))
    mat = pl.BlockSpec((cb // LRU_HEAD_DIM, LRU_HEAD_DIM, LRU_HEAD_DIM), lambda j, c: (j, 0, 0))
    return _call(
        body, name="lru_fwd", grid=(ncb, ntc),
        in_specs=[
            blk,
            pl.BlockSpec((tc, cb), lambda j, c: (c, ncb + j)),
            pl.BlockSpec((4, cb), lambda j, c: (0, j)),
            vec, mat, vec, mat, vec, vec,
        ],
        out_specs=[blk, blk, blk],
        out_shape=[
            jax.ShapeDtypeStruct((t, dr), BF16),
            jax.ShapeDtypeStruct((t, dr), F32),
            jax.ShapeDtypeStruct((t, dr), F32),
        ],
        scratch_shapes=[pltpu.VMEM((tc, cb), F32), pltpu.VMEM((1, cb), F32),
                        pltpu.VMEM((cb, cb), BF16), pltpu.VMEM((cb, cb), BF16)],
        args=(proj, proj, conv_w, conv_b, w_a, b_a, w_x, b_x, lam), carry=carry)


def _pool_select(col, vals):
    out = vals[3]
    for g in (2, 1, 0):
        out = jnp.where(col < (g + 1) * POOL_GROUP_DIM, vals[g], out)
    return out


def _pool_fwd(proj, pool_w, pool_scale, col_block):
    t = proj.shape[0]
    dp = pool_scale.shape[1]
    tc = _tile(t, 256)
    ntc = t // tc

    def body(x_ref, w_ref, sc_ref, y_ref, p_ref, px, p2, p4, p8):
        c = pl.program_id(0)

        @pl.when(c == 0)
        def _():
            for s in (px, p2, p4, p8):
                s[...] = jnp.zeros_like(s)

        x = x_ref[...].astype(F32)
        row = lax.broadcasted_iota(jnp.int32, x.shape, 0)
        col = lax.broadcasted_iota(jnp.int32, x.shape, 1)

        def sh(v, pv, j):
            return jnp.where(row >= j, pltpu.roll(v, j, 0), pltpu.roll(pv[...], j, 0))

        s2 = x + sh(x, px, 1)
        s4 = s2 + sh(s2, p2, 2)
        s8 = s4 + sh(s4, p4, 4)
        s16 = s8 + sh(s8, p8, 8)
        px[...] = x
        p2[...] = s2
        p4[...] = s4
        p8[...] = s8
        wsum = _pool_select(col, (s2, s4, s8, s16))
        win = _pool_select(col, POOL_WINDOWS)
        cnt = jnp.minimum(c * tc + row + 1, win).astype(F32)
        p = wsum / cnt - x
        pb = p.astype(BF16)
        p_ref[...] = pb
        for g in range(len(POOL_WINDOWS)):
            sl = slice(g * POOL_GROUP_DIM, (g + 1) * POOL_GROUP_DIM)
            yg = _dot_nn(pb[:, sl], w_ref[g]) * sc_ref[:, sl]
            y_ref[:, sl] = yg.astype(BF16)

    return _call(
        body, name="pool_fwd", grid=(ntc,),
        in_specs=[
            pl.BlockSpec((tc, dp), lambda c: (c, col_block)),
            pl.BlockSpec(pool_w.shape, lambda c: (0, 0, 0)),
            pl.BlockSpec((1, dp), lambda c: (0, 0)),
        ],
        out_specs=[pl.BlockSpec((tc, dp), lambda c: (c, 0))] * 2,
        out_shape=[jax.ShapeDtypeStruct((t, dp), BF16)] * 2,
        scratch_shapes=[pltpu.VMEM((tc, dp), F32)] * 4,
        args=(proj, pool_w, pool_scale))[0]


def _branch_mix(y_lru, y_pool, w_lru_up, w_pool_upt, proj, b_gate, ga_block, gb_block, carry=None):
    t, d = y_lru.shape
    dp = y_pool.shape[1]
    tt, tn = _tile(t, 1024), 512
    nj = d // tn

    def body(yl_ref, yp_ref, wl_ref, wp_ref, ga_ref, gb_ref, ba_ref, bb_ref, bra_ref, brb_ref, mix_ref):
        br_a = _dot_nn(yl_ref[...], wl_ref[...])
        br_b = _dot_nt(yp_ref[...], wp_ref[...])
        bra_ref[...] = br_a.astype(BF16)
        brb_ref[...] = br_b.astype(BF16)
        ga = _sig(ga_ref[...].astype(F32) + ba_ref[...])
        gb = _sig(gb_ref[...].astype(F32) + bb_ref[...])
        mix_ref[...] = (ga * br_a + gb * br_b).astype(BF16)

    out = pl.BlockSpec((tt, tn), lambda j, i: (i, j))
    return _call(
        body, name="branch_mix", grid=(nj, t // tt),
        in_specs=[
            pl.BlockSpec((tt, d), lambda j, i: (i, 0)),
            pl.BlockSpec((tt, dp), lambda j, i: (i, 0)),
            pl.BlockSpec((d, tn), lambda j, i: (0, j)),
            pl.BlockSpec((tn, dp), lambda j, i: (j, 0)),
            pl.BlockSpec((tt, tn), lambda j, i: (i, ga_block + j)),
            pl.BlockSpec((tt, tn), lambda j, i: (i, gb_block + j)),
            pl.BlockSpec((1, tn), lambda j, i: (0, j)),
            pl.BlockSpec((1, tn), lambda j, i: (0, nj + j)),
        ],
        out_specs=[out, out, out],
        out_shape=[jax.ShapeDtypeStruct((t, d), BF16)] * 3,
        args=(y_lru, y_pool, w_lru_up, w_pool_upt, proj, proj, b_gate, b_gate), carry=carry)


def _wo_norm(mix, w_o, x, g2, g3, carry=None):
    t, d = x.shape
    tt = _tile(t, 512)

    def body(mix_ref, w_ref, x_ref, g2_ref, g3_ref, m_ref, x2_ref, h3_ref):
        m = _dot_nn(mix_ref[...], w_ref[...])
        m_ref[...] = m
        mhat, _ = _rms_hat(m)
        x2 = x_ref[...] + mhat * g2_ref[...]
        x2_ref[...] = x2
        xhat, _ = _rms_hat(x2)
        h3_ref[...] = (xhat * g3_ref[...]).astype(BF16)

    row = pl.BlockSpec((tt, d), lambda i: (i, 0))
    vec = pl.BlockSpec((1, d), lambda i: (0, 0))
    return _call(
        body, name="wo_norm", grid=(t // tt,),
        in_specs=[row, pl.BlockSpec((d, d), lambda i: (0, 0)), row, vec, vec],
        out_specs=[row, row, row],
        out_shape=[
            jax.ShapeDtypeStruct((t, d), F32),
            jax.ShapeDtypeStruct((t, d), F32),
            jax.ShapeDtypeStruct((t, d), BF16),
        ],
        args=(mix, w_o, x, g2, g3), carry=carry)


def _ff1(h3, w_ff1t, carry=None):
    t, d = h3.shape
    n = w_ff1t.shape[0]
    tt, tn = _tile(t, 2048), _tile(n, 512)

    def body(h_ref, w_ref, rf_ref):
        rf_ref[...] = jnp.maximum(_dot_nt(h_ref[...], w_ref[...]), 0.0).astype(BF16)

    out = pl.BlockSpec((tt, tn), lambda i, j: (i, j))
    return _call(
        body, name="ff1", grid=(t // tt, n // tn),
        in_specs=[pl.BlockSpec((tt, d), lambda i, j: (i, 0)), pl.BlockSpec((tn, d), lambda i, j: (j, 0))],
        out_specs=[out],
        out_shape=[jax.ShapeDtypeStruct((t, n), BF16)],
        args=(h3, w_ff1t), carry=carry)


def _ff2_loss(rf, w_ff2, x2, g4, target):
    t, k = rf.shape
    d = x2.shape[1]
    tt, tk = _tile(t, 1024), _tile(k, 1024)
    nk = k // tk

    def body(a_ref, w_ref, x2_ref, g_ref, tg_ref, dy_ref, df_ref, dg_ref, loss_ref, acc):
        i, kk = pl.program_id(0), pl.program_id(1)

        @pl.when(kk == 0)
        def _():
            acc[...] = jnp.zeros_like(acc)

        @pl.when((i == 0) & (kk == 0))
        def _():
            dg_ref[...] = jnp.zeros_like(dg_ref)
            loss_ref[...] = jnp.zeros_like(loss_ref)

        rf_tile = a_ref[...]
        acc[...] += _dot_nn(rf_tile * rf_tile, w_ref[...])

        @pl.when(kk == nk - 1)
        def _():
            def tail(rows):
                fhat, r = _rms_hat(acc[rows, :])
                g = g_ref[...]
                e = x2_ref[rows, :] + fhat * g - tg_ref[rows, :]
                loss_ref[...] += 0.5 * jnp.sum(jnp.mean(e * e, axis=-1, keepdims=True))
                dy = e * (1.0 / d)
                dy_ref[rows, :] = dy.astype(BF16)
                df, dg = _rms_bwd(dy, fhat, r, g)
                df_ref[rows, :] = df.astype(BF16)
                dg_ref[...] += dg

            _row_chunks(tt, tail)

    row = pl.BlockSpec((tt, d), lambda i, kk: (i, 0))
    vec = pl.BlockSpec((1, d), lambda i, kk: (0, 0))
    return _call(
        body, name="ff2_loss", grid=(t // tt, nk),
        in_specs=[
            pl.BlockSpec((tt, tk), lambda i, kk: (i, kk)),
            pl.BlockSpec((tk, d), lambda i, kk: (kk, 0)),
            row, vec, row,
        ],
        out_specs=[row, row, vec, pl.BlockSpec((1, 128), lambda i, kk: (0, 0))],
        out_shape=[
            jax.ShapeDtypeStruct((t, d), BF16),
            jax.ShapeDtypeStruct((t, d), BF16),
            jax.ShapeDtypeStruct((1, d), F32),
            jax.ShapeDtypeStruct((1, 128), F32),
        ],
        scratch_shapes=[pltpu.VMEM((tt, d), F32)],
        args=(rf, w_ff2, x2, g4, target))[0]


def _ff2_bwd(df, w_ff2, rf, carry=None):
    t, d = df.shape
    n = w_ff2.shape[0]
    tt, tn = _tile(t, 2048), _tile(n, 512)

    def body(df_ref, w_ref, rf_ref, out_ref):
        d_act = _dot_nt(df_ref[...], w_ref[...])
        out_ref[...] = (d_act * (2.0 * rf_ref[...].astype(F32))).astype(BF16)

    blk = pl.BlockSpec((tt, tn), lambda i, j: (i, j))
    return _call(
        body, name="ff2_bwd", grid=(t // tt, n // tn),
        in_specs=[pl.BlockSpec((tt, d), lambda i, j: (i, 0)), pl.BlockSpec((tn, d), lambda i, j: (j, 0)), blk],
        out_specs=[blk],
        out_shape=[jax.ShapeDtypeStruct((t, n), BF16)],
        args=(df, w_ff2, rf), carry=carry)


def _wgrad(a, b, name, prev=None, row_off=0, rows=None, carry=None, square_a=False):
    t, m = a.shape
    n = b.shape[1]
    rows = m if rows is None else rows
    tm, tk = _tile(m, 512), _tile(t, 2048)
    nk = t // tk
    assert row_off % tm == 0
    off = row_off // tm

    def body(*refs):
        a_ref, b_ref = refs[0], refs[1]
        o32_ref, o16_ref, acc = refs[-3], refs[-2], refs[-1]
        kk = pl.program_id(1)

        @pl.when(kk == 0)
        def _():
            acc[...] = jnp.zeros_like(acc)

        a_tile = a_ref[...]
        acc[...] += _dot_tn(a_tile * a_tile if square_a else a_tile, b_ref[...])

        @pl.when(kk == nk - 1)
        def _():
            o32_ref[...] = acc[...]
            o16_ref[...] = acc[...].astype(BF16)

    in_specs = [pl.BlockSpec((tk, tm), lambda i, kk: (kk, i)), pl.BlockSpec((tk, n), lambda i, kk: (kk, 0))]
    args = [a, b]
    aliases = {}
    if prev is not None:
        in_specs += [ANY, ANY]
        args += list(prev)
        aliases = {2: 0, 3: 1}
    out = pl.BlockSpec((tm, n), lambda i, kk: (off + i, 0))
    return _call(
        body, name=name, grid=(m // tm, nk),
        in_specs=in_specs, out_specs=[out, out],
        out_shape=[jax.ShapeDtypeStruct((rows, n), F32), jax.ShapeDtypeStruct((rows, n), BF16)],
        scratch_shapes=[pltpu.VMEM((tm, n), F32)],
        aliases=aliases, args=args, carry=carry)


def _wgrad_parts(parts, b, name, carry=None):
    t, n = b.shape
    tm = 512
    bounds = []
    lo = 0
    for part in parts:
        assert part.shape[0] == t and part.shape[1] % tm == 0
        bounds.append((lo, lo + part.shape[1] // tm))
        lo += part.shape[1] // tm
    nm = lo
    np_ = len(parts)

    def body(*refs):
        p_refs, b_ref, o32_ref, o16_ref = refs[:np_], refs[np_], refs[np_ + 1], refs[np_ + 2]
        i = pl.program_id(0)
        for (lo_p, hi_p), p_ref in zip(bounds, p_refs):
            @pl.when((i >= lo_p) & (i < hi_p))
            def _(p_ref=p_ref):
                res = _dot_tn(p_ref[...], b_ref[...])
                o32_ref[...] = res
                o16_ref[...] = res.astype(BF16)

    def part_spec(lo_p, hi_p):
        return pl.BlockSpec((t, tm), lambda i: (0, jnp.clip(i - lo_p, 0, hi_p - lo_p - 1)))

    out = pl.BlockSpec((tm, n), lambda i: (i, 0))
    return _call(
        body, name=name, grid=(nm,),
        in_specs=[part_spec(lo_p, hi_p) for lo_p, hi_p in bounds] + [pl.BlockSpec((t, n), lambda i: (0, 0))],
        out_specs=[out, out],
        out_shape=[jax.ShapeDtypeStruct((nm * tm, n), F32), jax.ShapeDtypeStruct((nm * tm, n), BF16)],
        args=(*parts, b), carry=carry)


def _ff1_bwd_norms(d_f1, w_ff1t, dy, x2, g3, m, g2, carry=None):
    t, k = d_f1.shape
    d = x2.shape[1]
    tt, tk = _tile(t, 1024), _tile(k, 1024)
    nk = k // tk

    def body(a_ref, w_ref, dy_ref, x2_ref, g3_ref, m_ref, g2_ref, dx2_ref, dm_ref, dg3_ref, dg2_ref, acc):
        i, kk = pl.program_id(0), pl.program_id(1)

        @pl.when(kk == 0)
        def _():
            acc[...] = jnp.zeros_like(acc)

        @pl.when((i == 0) & (kk == 0))
        def _():
            dg3_ref[...] = jnp.zeros_like(dg3_ref)
            dg2_ref[...] = jnp.zeros_like(dg2_ref)

        acc[...] += _dot_nn(a_ref[...], w_ref[...])

        @pl.when(kk == nk - 1)
        def _():
            def tail(rows):
                xhat, r3 = _rms_hat(x2_ref[rows, :])
                dx, dg3 = _rms_bwd(acc[rows, :], xhat, r3, g3_ref[...])
                dx2 = dy_ref[rows, :].astype(F32) + dx
                dx2_ref[rows, :] = dx2
                dg3_ref[...] += dg3
                mhat, r2 = _rms_hat(m_ref[rows, :])
                dm, dg2 = _rms_bwd(dx2, mhat, r2, g2_ref[...])
                dm_ref[rows, :] = dm.astype(BF16)
                dg2_ref[...] += dg2

            _row_chunks(tt, tail)

    row = pl.BlockSpec((tt, d), lambda i, kk: (i, 0))
    vec = pl.BlockSpec((1, d), lambda i, kk: (0, 0))
    return _call(
        body, name="ff1_bwd_norms", grid=(t // tt, nk),
        in_specs=[
            pl.BlockSpec((tt, tk), lambda i, kk: (i, kk)),
            pl.BlockSpec((tk, d), lambda i, kk: (kk, 0)),
            row, row, vec, row, vec,
        ],
        out_specs=[row, row, vec, vec],
        out_shape=[
            jax.ShapeDtypeStruct((t, d), F32),
            jax.ShapeDtypeStruct((t, d), BF16),
            jax.ShapeDtypeStruct((1, d), F32),
            jax.ShapeDtypeStruct((1, d), F32),
        ],
        scratch_shapes=[pltpu.VMEM((tt, d), F32)],
        args=(d_f1, w_ff1t, dy, x2, g3, m, g2), carry=carry)


def _wo_bwd_mix(dm, w_o, br_a, br_b, proj, b_gate, ga_block, gb_block, carry=None):
    t, d = dm.shape
    tt, tn = _tile(t, 1024), 512
    nj = d // tn

    def body(dm_ref, w_ref, bra_ref, brb_ref, ga_ref, gb_ref, ba_ref, bb_ref,
             dbra_ref, dbrb_ref, dga_ref, dgb_ref, dba_ref, dbb_ref):
        i = pl.program_id(1)

        @pl.when(i == 0)
        def _():
            dba_ref[...] = jnp.zeros_like(dba_ref)
            dbb_ref[...] = jnp.zeros_like(dbb_ref)

        d_mix = _dot_nt(dm_ref[...], w_ref[...])
        ga = _sig(ga_ref[...].astype(F32) + ba_ref[...])
        gb = _sig(gb_ref[...].astype(F32) + bb_ref[...])
        dbra_ref[...] = (d_mix * ga).astype(BF16)
        dbrb_ref[...] = (d_mix * gb).astype(BF16)
        dga = d_mix * bra_ref[...].astype(F32) * (ga * (1.0 - ga))
        dgb = d_mix * brb_ref[...].astype(F32) * (gb * (1.0 - gb))
        dga_ref[...] = dga.astype(BF16)
        dgb_ref[...] = dgb.astype(BF16)
        dba_ref[...] += jnp.sum(dga, axis=0, keepdims=True)
        dbb_ref[...] += jnp.sum(dgb, axis=0, keepdims=True)

    blk = pl.BlockSpec((tt, tn), lambda j, i: (i, j))
    vec = pl.BlockSpec((1, tn), lambda j, i: (0, j))
    return _call(
        body, name="wo_bwd_mix", grid=(nj, t // tt),
        in_specs=[
            pl.BlockSpec((tt, d), lambda j, i: (i, 0)),
            pl.BlockSpec((tn, d), lambda j, i: (j, 0)),
            blk, blk,
            pl.BlockSpec((tt, tn), lambda j, i: (i, ga_block + j)),
            pl.BlockSpec((tt, tn), lambda j, i: (i, gb_block + j)),
            vec,
            pl.BlockSpec((1, tn), lambda j, i: (0, nj + j)),
        ],
        out_specs=[blk, blk, blk, blk, vec, vec],
        out_shape=[jax.ShapeDtypeStruct((t, d), BF16)] * 4 + [jax.ShapeDtypeStruct((1, d), F32)] * 2,
        args=(dm, w_o, br_a, br_b, proj, proj, b_gate, b_gate), carry=carry)


def _lru_up_bwd(d_br_a, w_lru_up, proj, h, g_block, carry=None):
    t, d = d_br_a.shape
    tt, tn = _tile(t, 1024), 512

    def body(a_ref, w_ref, g_ref, h_ref, dh_ref, dg_ref):
        d_y = _dot_nt(a_ref[...], w_ref[...])
        gel, gel_grad = _gelu_and_grad(g_ref[...].astype(F32))
        dh_ref[...] = d_y * gel
        dg_ref[...] = (d_y * h_ref[...] * gel_grad).astype(BF16)

    blk = pl.BlockSpec((tt, tn), lambda i, j: (i, j))
    return _call(
        body, name="lru_up_bwd", grid=(t // tt, d // tn),
        in_specs=[
            pl.BlockSpec((tt, d), lambda i, j: (i, 0)),
            pl.BlockSpec((tn, d), lambda i, j: (j, 0)),
            pl.BlockSpec((tt, tn), lambda i, j: (i, g_block + j)),
            blk,
        ],
        out_specs=[blk, blk],
        out_shape=[jax.ShapeDtypeStruct((t, d), F32), jax.ShapeDtypeStruct((t, d), BF16)],
        args=(d_br_a, w_lru_up, proj, h), carry=carry)


def _lru_bwd(dh, xc, h, proj, conv_w, w_a, b_a, w_x, b_x, lam, carry=None):
    t, dr = dh.shape
    cb = LRU_CB
    hd = LRU_HEAD_DIM
    per = cb // hd
    tc = _tile(t, 256)
    ncb, ntc = dr // cb, t // tc

    def body(dh_ref, xc_ref, h_ref, hp_ref, xp_ref, cw_ref, wa_ref, ba_ref, wx_ref, bx_ref, lam_ref,
             dxp_ref, dwa_ref, dba_ref, dwx_ref, dbx_ref, dlam_ref, dcw_ref, dcb_ref,
             nextd_s, anext_s, gnext_s, tmp_s, wa_s, wx_s):
        c = pl.program_id(1)
        rc = ntc - 1 - c

        @pl.when(c == 0)
        def _():
            nextd_s[...] = jnp.zeros_like(nextd_s)
            anext_s[...] = jnp.zeros_like(anext_s)
            gnext_s[...] = jnp.zeros_like(gnext_s)
            for ref in (dwa_ref, dba_ref, dwx_ref, dbx_ref, dlam_ref, dcw_ref, dcb_ref):
                ref[...] = jnp.zeros_like(ref)
            _fill_block_diag(wa_ref, wa_s)
            _fill_block_diag(wx_ref, wx_s)

        xc = xc_ref[...]
        wa, wx, lam = wa_s[...], wx_s[...], lam_ref[...]
        xcb, r, i, sp, log_a, a, mult = _lru_gates(xc, wa, ba_ref[...], wx, bx_ref[...], lam)
        row = lax.broadcasted_iota(jnp.int32, xc.shape, 0)
        h = h_ref[...]
        hp = jnp.where(rc == 0, 0.0, hp_ref[...])
        hprev = jnp.where(row >= 1, pltpu.roll(h, 1, 0), pltpu.roll(hp, 1, 0))

        def up(v, nv, j):
            return jnp.where(row < tc - j, pltpu.roll(v, tc - j, 0), nv)

        av, bv = _scan_rows(up(a, anext_s[...], 1), dh_ref[...], reverse=True)
        gt = av * gnext_s[...] + bv
        tmp_s[...] = gt
        gnext_s[...] = tmp_s[0:1, :]
        tmp_s[...] = a
        anext_s[...] = tmp_s[0:1, :]

        da = gt * hprev
        ixc = i * xc
        d_mult = gt * ixc
        d_i = gt * mult * xc
        d_xc = gt * mult * i
        d_log_a = da * a - d_mult * (a * a) / mult
        d_pre_r = (d_log_a * ((-LRU_C) * sp)) * (r * (1.0 - r))
        d_pre_i = d_i * (i * (1.0 - i))
        d_sp = jnp.sum(d_log_a * ((-LRU_C) * r), axis=0, keepdims=True)
        dlam_ref[...] += d_sp * (-1.0 / (1.0 + jnp.exp(lam)))
        dpr = d_pre_r.astype(BF16)
        dpi = d_pre_i.astype(BF16)
        dba_ref[...] += jnp.sum(d_pre_r, axis=0, keepdims=True)
        dbx_ref[...] += jnp.sum(d_pre_i, axis=0, keepdims=True)
        pa = _dot_tn(xcb, dpr)
        px = _dot_tn(xcb, dpi)
        for k in range(per):
            dwa_ref[k] += pa[k * hd:(k + 1) * hd, k * hd:(k + 1) * hd]
            dwx_ref[k] += px[k * hd:(k + 1) * hd, k * hd:(k + 1) * hd]
        d_xc = d_xc + _dot_nt(dpr, wa) + _dot_nt(dpi, wx)

        nxt = nextd_s[...]
        xp = xp_ref[...].astype(F32)
        dxp = cw_ref[3:4, :] * d_xc
        dcw_ref[3:4, :] += jnp.sum(xp * d_xc, axis=0, keepdims=True)
        for j in (1, 2, 3):
            uj = up(d_xc, pltpu.roll(nxt, tc - j, 0), j)
            dxp = dxp + cw_ref[3 - j:4 - j, :] * uj
            dcw_ref[3 - j:4 - j, :] += jnp.sum(xp * uj, axis=0, keepdims=True)
        dcb_ref[...] += jnp.sum(d_xc, axis=0, keepdims=True)
        nextd_s[...] = d_xc
        dxp_ref[...] = dxp.astype(BF16)

    vec = pl.BlockSpec((1, cb), lambda j, c: (0, j))
    blk = pl.BlockSpec((tc, cb), lambda j, c: (ntc - 1 - c, j))
    mat = pl.BlockSpec((per, hd, hd), lambda j, c: (j, 0, 0))
    cwb = pl.BlockSpec((4, cb), lambda j, c: (0, j))
    return _call(
        body, name="lru_bwd", grid=(ncb, ntc),
        in_specs=[
            blk, blk, blk,
            pl.BlockSpec((tc, cb), lambda j, c: (jnp.maximum(ntc - 2 - c, 0), j)),
            blk, cwb, mat, vec, mat, vec, vec,
        ],
        out_specs=[blk, mat, vec, mat, vec, vec, cwb, vec],
        out_shape=[
            jax.ShapeDtypeStruct((t, dr), BF16),
            jax.ShapeDtypeStruct(w_a.shape, F32),
            jax.ShapeDtypeStruct((1, dr), F32),
            jax.ShapeDtypeStruct(w_x.shape, F32),
            jax.ShapeDtypeStruct((1, dr), F32),
            jax.ShapeDtypeStruct((1, dr), F32),
            jax.ShapeDtypeStruct((4, dr), F32),
            jax.ShapeDtypeStruct((1, dr), F32),
        ],
        scratch_shapes=[
            pltpu.VMEM((tc, cb), F32),
            pltpu.VMEM((1, cb), F32),
            pltpu.VMEM((1, cb), F32),
            pltpu.VMEM((tc, cb), F32),
            pltpu.VMEM((cb, cb), BF16),
            pltpu.VMEM((cb, cb), BF16),
        ],
        args=(dh, xc, h, h, proj, conv_w, w_a, b_a, w_x, b_x, lam), carry=carry)


def _pool_bwd(d_br_b, w_pool_upt, p, pool_w, pool_scale):
    t, d = d_br_b.shape
    dp = w_pool_upt.shape[1]
    tc = _tile(t, 256)
    ntc = t // tc
    ng = len(POOL_WINDOWS)

    def body(db_ref, wu_ref, p_ref, w_ref, sc_ref, dx_ref, dw_ref, dsc_ref, nz, n2, n4, n8, dp_s, dy_s):
        c = pl.program_id(0)
        rc = ntc - 1 - c

        @pl.when(c == 0)
        def _():
            for s in (nz, n2, n4, n8):
                s[...] = jnp.zeros_like(s)
            dw_ref[...] = jnp.zeros_like(dw_ref)
            dsc_ref[...] = jnp.zeros_like(dsc_ref)

        dy_s[...] = _dot_nn(db_ref[...], wu_ref[...])
        for g in range(ng):
            sl = slice(g * POOL_GROUP_DIM, (g + 1) * POOL_GROUP_DIM)
            pg = p_ref[:, sl]
            dyg = dy_s[:, sl]
            wg = w_ref[g].astype(BF16)
            q = _dot_nn(pg, wg)
            dsc_ref[:, sl] += jnp.sum(dyg * q, axis=0, keepdims=True)
            dpw = (dyg * sc_ref[:, sl]).astype(BF16)
            dw_ref[g] += _dot_tn(pg, dpw)
            dp_s[:, sl] = _dot_nt(dpw, wg)

        dpv = dp_s[...]
        row = lax.broadcasted_iota(jnp.int32, dpv.shape, 0)
        col = lax.broadcasted_iota(jnp.int32, dpv.shape, 1)
        win = _pool_select(col, POOL_WINDOWS)
        cnt = jnp.minimum(rc * tc + row + 1, win).astype(F32)
        z = dpv / cnt

        def up(v, nv, j):
            return jnp.where(row < tc - j, pltpu.roll(v, tc - j, 0), pltpu.roll(nv[...], tc - j, 0))

        u2 = z + up(z, nz, 1)
        u4 = u2 + up(u2, n2, 2)
        u8 = u4 + up(u4, n4, 4)
        u16 = u8 + up(u8, n8, 8)
        nz[...] = z
        n2[...] = u2
        n4[...] = u4
        n8[...] = u8
        dx_ref[...] = (_pool_select(col, (u2, u4, u8, u16)) - dpv).astype(BF16)

    blk = pl.BlockSpec((tc, dp), lambda c: (ntc - 1 - c, 0))
    full_w = pl.BlockSpec(pool_w.shape, lambda c: (0, 0, 0))
    vec = pl.BlockSpec((1, dp), lambda c: (0, 0))
    return _call(
        body, name="pool_bwd", grid=(ntc,),
        in_specs=[pl.BlockSpec((tc, d), lambda c: (ntc - 1 - c, 0)), pl.BlockSpec((d, dp), lambda c: (0, 0)),
                  blk, full_w, vec],
        out_specs=[blk, full_w, vec],
        out_shape=[
            jax.ShapeDtypeStruct((t, dp), BF16),
            jax.ShapeDtypeStruct(pool_w.shape, F32),
            jax.ShapeDtypeStruct((1, dp), F32),
        ],
        scratch_shapes=[pltpu.VMEM((tc, dp), F32)] * 6,
        args=(d_br_b, w_pool_upt, p, pool_w, pool_scale))[0]


def _win_bwd_norm(parts, w_int, dx2, x, g1, carry=None):
    t, d = x.shape
    tk = 512
    tt = _tile(t, 1024)
    bounds = []
    k0 = 0
    for part in parts:
        assert part.shape[1] % tk == 0
        bounds.append((k0, k0 + part.shape[1] // tk))
        k0 += part.shape[1] // tk
    nk = k0
    assert nk * tk == w_int.shape[0]
    np_ = len(parts)

    def body(*refs):
        p_refs = refs[:np_]
        w_ref, dx2_ref, x_ref, g_ref, gx_ref, dg_ref, acc = refs[np_:]
        i, kk = pl.program_id(0), pl.program_id(1)

        @pl.when(kk == 0)
        def _():
            acc[...] = jnp.zeros_like(acc)

        @pl.when((i == 0) & (kk == 0))
        def _():
            dg_ref[...] = jnp.zeros_like(dg_ref)

        for (lo, hi), p_ref in zip(bounds, p_refs):
            @pl.when((kk >= lo) & (kk < hi))
            def _(p_ref=p_ref):
                acc[...] += _dot_nn(p_ref[...], w_ref[...])

        @pl.when(kk == nk - 1)
        def _():
            def tail(rows):
                xhat, r = _rms_hat(x_ref[rows, :])
                dx, dg = _rms_bwd(acc[rows, :], xhat, r, g_ref[...])
                gx_ref[rows, :] = dx2_ref[rows, :] + dx
                dg_ref[...] += dg

            _row_chunks(tt, tail)

    def part_spec(lo, hi):
        return pl.BlockSpec((tt, tk), lambda i, kk: (i, jnp.clip(kk - lo, 0, hi - lo - 1)))

    row = pl.BlockSpec((tt, d), lambda i, kk: (i, 0))
    vec = pl.BlockSpec((1, d), lambda i, kk: (0, 0))
    return _call(
        body, name="win_bwd_norm", grid=(t // tt, nk),
        in_specs=[part_spec(lo, hi) for lo, hi in bounds]
        + [pl.BlockSpec((tk, d), lambda i, kk: (kk, 0)), row, row, vec],
        out_specs=[row, vec],
        out_shape=[jax.ShapeDtypeStruct((t, d), F32), jax.ShapeDtypeStruct((1, d), F32)],
        scratch_shapes=[pltpu.VMEM((tt, d), F32)],
        args=(*parts, w_int, dx2, x, g1), carry=carry)


def _adam_math(w, g, m, v):
    m = ADAM_B1 * m + (1.0 - ADAM_B1) * g
    v = ADAM_B2 * v + (1.0 - ADAM_B2) * (g * g)
    m_hat = m / (1.0 - ADAM_B1 ** ADAM_STEP)
    v_hat = v / (1.0 - ADAM_B2 ** ADAM_STEP)
    delta = -ADAM_LR * (m_hat / (jnp.sqrt(v_hat) + ADAM_EPS) + ADAM_WD * w)
    return delta, m, v


def _adamw_big(ws, gs, ms, vs, carry=None):
    n = len(ws)
    nb = 8

    def body(*refs):
        for a in range(n):
            w_ref, g_ref, m_ref, v_ref = refs[4 * a:4 * a + 4]
            d_ref, nm_ref, nv_ref = refs[4 * n + 3 * a:4 * n + 3 * a + 3]
            dl, m, v = _adam_math(w_ref[...], g_ref[...], m_ref[...], v_ref[...])
            d_ref[...] = dl
            nm_ref[...] = m
            nv_ref[...] = v

    in_specs, out_specs, out_shape, args = [], [], [], []
    for w, g, m, v in zip(ws, gs, ms, vs):
        rows, cols = w.shape
        blk = pl.BlockSpec((rows // nb, cols), lambda i: (i, 0))
        in_specs += [blk] * 4
        args += [w, g, m, v]
        out_specs += [blk] * 3
        out_shape += [jax.ShapeDtypeStruct(w.shape, F32)] * 3
    outs, got = _call(body, name="adamw_big", grid=(nb,), in_specs=in_specs, out_specs=out_specs,
                      out_shape=out_shape, args=args, carry=carry)
    return [tuple(outs[3 * a:3 * a + 3]) for a in range(n)], got


SMALL_ORDER = ("norm_mix_pre", "norm_mix_post", "norm_mlp_pre", "norm_mlp_post", "b_gate", "conv_w", "conv_b",
               "lru_w_a", "lru_b_a", "lru_w_x", "lru_b_x", "lru_lambda", "pool_w", "pool_scale")
VEC_ROW = dict(norm_mix_pre=0, norm_mix_post=1, norm_mlp_pre=2, norm_mlp_post=3, conv_b=6, lru_b_a=7,
               lru_b_x=8, lru_lambda=9)
ROW_B_GATE, ROW_POOL_SCALE, ROW_CONV_W, ROW_LOSS, N_VEC_ROWS = 4, 10, 11, 15, 16


def _adamw_small(vec_parts, g_pool, g_wa, g_wx, me, params):
    d = vec_parts.shape[2]
    names = SMALL_ORDER
    n = len(names)
    cw_cols = params["conv_w"][0].shape[2]

    def body(me_ref, vec_ref, vecc_ref, gp_ref, gwa_ref, gwx_ref, *refs):
        wmv = refs[:3 * n]
        loss_ref = refs[3 * n]
        outs = refs[3 * n + 1:3 * n + 1 + 4 * n]
        vs, vsc = refs[3 * n + 1 + 4 * n:]
        acc, accc = vec_ref[0], vecc_ref[0]
        for k in range(1, N_DEV):
            acc = acc + vec_ref[k]
            accc = accc + vecc_ref[k]
        vs[...] = acc
        vsc[...] = accc
        loss_ref[...] = vs[ROW_LOSS:ROW_LOSS + 1, 0:128]

        def upd(a, g, idx):
            w_ref, m_ref, v_ref = wmv[3 * a:3 * a + 3]
            g_ref, d_ref, nm_ref, nv_ref = outs[4 * a:4 * a + 4]
            dl, m, v = _adam_math(w_ref[idx], g, m_ref[idx], v_ref[idx])
            g_ref[idx] = g
            d_ref[idx] = dl
            nm_ref[idx] = m
            nv_ref[idx] = v

        for a, name in enumerate(names):
            if name in VEC_ROW:
                r = VEC_ROW[name]
                upd(a, vs[r:r + 1, :], (slice(None), slice(None)))
            elif name == "b_gate":
                for half in range(2):
                    r = ROW_B_GATE + half
                    upd(a, vs[r:r + 1, :], (slice(None), slice(half * d, (half + 1) * d)))
            elif name == "pool_scale":
                width = params[name][0].shape[1]
                upd(a, vs[ROW_POOL_SCALE:ROW_POOL_SCALE + 1, 0:width], (slice(None), slice(None)))
            elif name == "conv_w":
                upd(a, vsc[ROW_CONV_W:ROW_CONV_W + 4, :], (0,))
            elif name == "pool_w":
                upd(a, gp_ref[...], (Ellipsis,))
            elif name == "lru_w_a":
                upd(a, gwa_ref[...], (Ellipsis,))
            elif name == "lru_w_x":
                upd(a, gwx_ref[...], (Ellipsis,))
            else:
                raise ValueError(name)

    def whole(shape):
        nd = len(shape)
        return pl.BlockSpec(tuple(shape), lambda i, me_ref: (0,) * nd)

    in_specs = [
        whole(vec_parts.shape),
        pl.BlockSpec((N_DEV, N_VEC_ROWS, cw_cols), lambda i, me_ref: (0, 0, me_ref[0])),
        whole(g_pool.shape), whole(g_wa.shape), whole(g_wx.shape),
    ]
    args = [vec_parts, vec_parts, g_pool, g_wa, g_wx]
    out_specs = [whole((1, 128))]
    out_shape = [jax.ShapeDtypeStruct((1, 128), F32)]
    for name in names:
        for arr in params[name]:
            in_specs.append(whole(arr.shape))
            args.append(arr)
        shp = params[name][0].shape
        out_specs += [whole(shp)] * 4
        out_shape += [jax.ShapeDtypeStruct(shp, F32)] * 4
    grid_spec = pltpu.PrefetchScalarGridSpec(
        num_scalar_prefetch=1, grid=(1,), in_specs=in_specs, out_specs=out_specs,
        scratch_shapes=[pltpu.VMEM((N_VEC_ROWS, d), F32), pltpu.VMEM((N_VEC_ROWS, cw_cols), F32)])
    outs = pl.pallas_call(
        body, name="adamw_small", grid_spec=grid_spec, out_shape=out_shape,
        compiler_params=pltpu.CompilerParams(
            dimension_semantics=("arbitrary",), vmem_limit_bytes=V7X_VMEM_LIMIT_BYTES),
    )(me, *_in_hbm(args))
    return outs[0], {name: tuple(outs[1 + 4 * a:5 + 4 * a]) for a, name in enumerate(names)}


def _rs_sum(fulls, recvs, shard_ids, slot_ids, name):
    n = len(fulls)

    def body(sh_ref, sl_ref, *refs):
        s = pl.program_id(0)
        for a in range(n):
            full_ref, recv_ref = refs[2 * a], refs[2 * a + 1]
            own_ref, send_ref = refs[2 * n + 2 * a], refs[2 * n + 2 * a + 1]
            v = full_ref[...] + recv_ref[...].astype(F32)

            @pl.when(s == 0)
            def _(own_ref=own_ref, v=v):
                own_ref[...] = v

            @pl.when(s > 0)
            def _(send_ref=send_ref, v=v):
                send_ref[...] = v.astype(send_ref.dtype)

    in_specs, out_specs, out_shape, args = [], [], [], []
    for full, recv in zip(fulls, recvs):
        r, rest = recv.shape[1], tuple(recv.shape[2:])
        zeros = (0,) * len(rest)
        in_specs += [
            pl.BlockSpec((r,) + rest, lambda s, sh, sl, zeros=zeros: (sh[s],) + zeros),
            pl.BlockSpec((None, r) + rest, lambda s, sh, sl, zeros=zeros: (sl[s], 0) + zeros),
        ]
        out_specs += [
            pl.BlockSpec((None, r) + rest, lambda s, sh, sl, zeros=zeros: (0, 0) + zeros),
            pl.BlockSpec((None, r) + rest, lambda s, sh, sl, zeros=zeros: (jnp.maximum(s - 1, 0), 0) + zeros),
        ]
        out_shape += [jax.ShapeDtypeStruct((1, r) + rest, F32), jax.ShapeDtypeStruct((3, r) + rest, recv.dtype)]
        args += [full, recv]
    grid_spec = pltpu.PrefetchScalarGridSpec(
        num_scalar_prefetch=2, grid=(4,), in_specs=in_specs, out_specs=out_specs)
    outs = pl.pallas_call(
        body,
        name=name,
        grid_spec=grid_spec,
        out_shape=out_shape,
        compiler_params=pltpu.CompilerParams(
            dimension_semantics=("arbitrary",), vmem_limit_bytes=V7X_VMEM_LIMIT_BYTES),
    )(shard_ids, slot_ids, *_in_hbm(args))
    return [(outs[2 * a], outs[2 * a + 1]) for a in range(n)]


def _finals(pairs, name, carry=None):
    nb = 4
    n = len(pairs)

    def body(*refs):
        for a in range(n):
            own_ref, recv_ref = refs[2 * a], refs[2 * a + 1]
            acc = own_ref[...]
            for k in range(3):
                acc = acc + recv_ref[k].astype(F32)
            refs[2 * n + a][...] = acc

    in_specs, out_specs, out_shape, args = [], [], [], []
    for own, recv in pairs:
        _, rows, cols = own.shape
        in_specs += [pl.BlockSpec((None, rows // nb, cols), lambda i: (0, i, 0)),
                     pl.BlockSpec((3, rows // nb, cols), lambda i: (0, i, 0))]
        args += [own, recv]
        out_specs.append(pl.BlockSpec((rows // nb, cols), lambda i: (i, 0)))
        out_shape.append(jax.ShapeDtypeStruct((rows, cols), F32))
    return _call(body, name=name, grid=(nb,), in_specs=in_specs, out_specs=out_specs,
                 out_shape=out_shape, args=args, carry=carry)


def _rs_sums(fulls_f32, recv1, tag):
    x, y, c = _place()
    qs = jnp.stack([2 * x + y, 2 * (1 - x) + y, 2 * x + (1 - y), 2 * (1 - x) + (1 - y)]).astype(jnp.int32)
    shard_ids = 2 * qs + c
    return _rs_sum(fulls_f32, recv1, shard_ids, qs, "rs_sum_" + tag)


def _rs_level1(fulls_f32, fulls_send, tag):
    recv1 = _run_plan(_rs_sibling_plan(fulls_send), "rs_sibling_" + tag)
    return _rs_sums(fulls_f32, recv1, tag)


def _rows(g):
    return g.reshape(g.shape[0] * g.shape[1], g.shape[2])


def kernel(x, norm_mix_pre, norm_mix_post, norm_mlp_pre, norm_mlp_post, w_in, b_gate, conv_w, conv_b, lru_w_a, lru_b_a, lru_w_x, lru_b_x, lru_lambda, pool_w, pool_scale, w_lru_up, w_pool_up, w_o, w_ff1, w_ff2, loss_target, m_norm_mix_pre, m_norm_mix_post, m_norm_mlp_pre, m_norm_mlp_post, m_w_in, m_b_gate, m_conv_w, m_conv_b, m_lru_w_a, m_lru_b_a, m_lru_w_x, m_lru_b_x, m_lru_lambda, m_pool_w, m_pool_scale, m_w_lru_up, m_w_pool_up, m_w_o, m_w_ff1, m_w_ff2, v_norm_mix_pre, v_norm_mix_post, v_norm_mlp_pre, v_norm_mlp_post, v_w_in, v_b_gate, v_conv_w, v_conv_b, v_lru_w_a, v_lru_b_a, v_lru_w_x, v_lru_b_x, v_lru_lambda, v_pool_w, v_pool_scale, v_w_lru_up, v_w_pool_up, v_w_o, v_w_ff1, v_w_ff2):
    t, d = x.shape[1], x.shape[2]
    d_rnn = conv_b.shape[1]
    d_pool = pool_scale.shape[1]
    per = LRU_CB // LRU_HEAD_DIM
    xi, yi, ci = _place()
    me = 4 * xi + 2 * yi + ci

    x2d = x[0]
    tgt = loss_target[0]

    s_in = w_in[0].T.astype(BF16)
    s_lu = w_lru_up[0].astype(BF16)
    s_pu = w_pool_up[0].T.astype(BF16)
    s_o = w_o[0].astype(BF16)
    s_f1 = w_ff1[0].T.astype(BF16)
    s_f2 = w_ff2[0].astype(BF16)
    s_cw = jnp.pad(conv_w[0], ((0, 4), (0, 0)))

    g_in, g_cw = _run_plan(_ag_plan([s_in, s_cw]), "ag_w_in")
    w_int = _rows(g_in)
    conv_w_full = jnp.transpose(g_cw[:, :4, :], (1, 0, 2)).reshape(4, d_rnn)

    wa_bd, wx_bd = lru_w_a[0], lru_w_x[0]
    pw = pool_w[0]
    pw_bf = pw.astype(BF16)

    pool_block = (2 * d_rnn) // d_pool
    ga_block = (2 * d_rnn + d_pool) // 512
    gb_block = ga_block + d // 512
    g_block = d_rnn // 512

    r_f1, r_f2 = s_f1.shape[0], s_f2.shape[0]
    f1_cut = r_f1 // 4
    f2_cut = (3 * r_f2) // 8
    plan = _join([_ag_plan([s_lu, s_pu, s_o]), _ag_plan([s_f1], pieces=[(0, f1_cut)])])
    (proj, h1), got = _norm_proj(x2d, norm_mix_pre, w_int, carry=plan)
    (g_lu, g_pu, g_o), (g_f1,) = plan.split(got)
    w_lu, w_put, w_og = _rows(g_lu), _rows(g_pu), _rows(g_o)
    (y_lru, h, xc), (g_f1,) = _lru_fwd(
        proj, conv_w_full, conv_b, wa_bd, lru_b_a, wx_bd, lru_b_x, lru_lambda,
        carry=_ag_plan([s_f1], pieces=[(f1_cut, r_f1 - f1_cut)], bufs=[g_f1]))
    w_f1t = _rows(g_f1)
    y_pool, p = _pool_fwd(proj, pw_bf, pool_scale, pool_block)
    (br_a, br_b, mix), (g_f2,) = _branch_mix(
        y_lru, y_pool, w_lu, w_put, proj, b_gate, ga_block, gb_block,
        carry=_ag_plan([s_f2], pieces=[(0, f2_cut)]))
    (m, x2, h3), _ = _wo_norm(mix, w_og, x2d, norm_mix_post, norm_mlp_pre)
    (rf,), (g_f2,) = _ff1(
        h3, w_f1t, carry=_ag_plan([s_f2], pieces=[(f2_cut, r_f2 - f2_cut)], bufs=[g_f2]))
    w_f2 = _rows(g_f2)
    dy, df, dg4, loss_part = _ff2_loss(rf, w_f2, x2, norm_mlp_post, tgt)

    (gw_ff2_32, gw_ff2_16), _ = _wgrad(rf, df, "wgrad_ff2", square_a=True)
    (d_f1,), r1_ff2 = _ff2_bwd(df, w_f2, rf, carry=_rs_sibling_plan([gw_ff2_16]))
    ((own_ff2, send_ff2),) = _rs_sums([gw_ff2_32], r1_ff2, "ff2")
    cut2 = (5 * send_ff2.shape[1]) // 16
    (gw_ff1_32, gw_ff1_16), (r2_ff2,) = _wgrad(
        d_f1, h3, "wgrad_ff1", carry=_rs_chips_plan([send_ff2], pieces=[(0, cut2)]))
    plan = _join([_rs_chips_plan([send_ff2], pieces=[(cut2, send_ff2.shape[1] - cut2)], bufs=[r2_ff2]),
                  _rs_sibling_plan([gw_ff1_16])])
    (dx2, dm, dg3, dg2), got = _ff1_bwd_norms(d_f1, w_f1t, dy, x2, norm_mlp_pre, m, norm_mix_post, carry=plan)
    (r2_ff2,), r1_ff1 = plan.split(got)
    ((own_ff1, send_ff1),) = _rs_sums([gw_ff1_32], r1_ff1, "ff1")
    cut = send_ff1.shape[1] // 4
    (gw_o_32, gw_o_16), _ = _wgrad(mix, dm, "wgrad_o")
    (d_br_a, d_br_b, p_ga, p_gb, dbg_a, dbg_b), (r2_ff1,) = _wo_bwd_mix(
        dm, w_og, br_a, br_b, proj, b_gate, ga_block, gb_block,
        carry=_rs_chips_plan([send_ff1], pieces=[(0, cut)]))
    (gw_lu_32, gw_lu_16), _ = _wgrad(y_lru, d_br_a, "wgrad_lru_up")
    (gw_pu_32, gw_pu_16), _ = _wgrad(d_br_b, y_pool, "wgrad_pool_up")
    (dh, p_g), r1_mid = _lru_up_bwd(
        d_br_a, w_lu, proj, h, g_block,
        carry=_rs_sibling_plan([gw_o_16, gw_lu_16, gw_pu_16.reshape(-1, d)]))
    mid = _rs_sums([gw_o_32, gw_lu_32, gw_pu_32.reshape(-1, d)], r1_mid, "mid")
    (p_x, dwa, db_a, dwx, db_x, dlam, dconv_w, dconv_b), (r2_ff1,) = _lru_bwd(
        dh, xc, h, proj, conv_w_full, wa_bd, lru_b_a, wx_bd, lru_b_x, lru_lambda,
        carry=_rs_chips_plan([send_ff1], pieces=[(cut, send_ff1.shape[1] - cut)], bufs=[r2_ff1]))
    p_p, dpool_w, dpool_scale = _pool_bwd(d_br_b, w_put, p, pw, pool_scale)
    parts = [p_x, p_g, p_p, p_ga, p_gb]
    gw_in, r2_mid = _wgrad_parts(parts, h1, "wgrad_in", carry=_rs_chips_plan([s for _, s in mid]))
    tail = _rs_level1([gw_in[0], dpool_w.reshape(N_DEV, -1, POOL_GROUP_DIM), dwa, dwx],
                      [gw_in[1], dpool_w.reshape(N_DEV, -1, POOL_GROUP_DIM), dwa, dwx], "in")
    (grad_x, dg1), r2_tail = _win_bwd_norm(parts, w_int, dx2, x2d, norm_mix_pre,
                                           carry=_rs_chips_plan([s for _, s in tail]))

    def flat2(a):
        return a.reshape(a.shape[0], -1, a.shape[-1])

    fin_small, _ = _finals([
        (flat2(tail[1][0]), flat2(r2_tail[1])), (flat2(tail[2][0]), flat2(r2_tail[2])),
        (flat2(tail[3][0]), flat2(r2_tail[3])),
    ], "rs_finals_small")

    def pad_row(a):
        return jnp.pad(a, ((0, 0), (0, d - a.shape[1])))

    vecs = jnp.concatenate([dg1, dg2, dg3, dg4, dbg_a, dbg_b, dconv_b, db_a, db_x, dlam,
                            pad_row(dpool_scale), dconv_w, pad_row(loss_part)], axis=0)
    assert vecs.shape[0] == N_VEC_ROWS
    fin, (vec_parts, g_pool, g_wa, g_wx) = _finals([
        (tail[0][0], r2_tail[0]), (mid[1][0], r2_mid[1]), (mid[2][0], r2_mid[2]), (mid[0][0], r2_mid[0]),
        (own_ff1, r2_ff1), (own_ff2, r2_ff2),
    ], "rs_finals", carry=_ag_plan([vecs] + fin_small))
    g_w_lru_up = fin[1]
    g_w_pool_up = fin[2].reshape(d // N_DEV, d_pool).T
    g_w_o = fin[3]
    g_w_ff1 = fin[4].T
    g_w_ff2 = fin[5]

    big_names = ["w_in", "w_lru_up", "w_pool_up", "w_o", "w_ff1", "w_ff2"]
    big_w = [w_in[0].T, w_lru_up[0], w_pool_up[0], w_o[0], w_ff1[0], w_ff2[0]]
    big_g = [fin[0], g_w_lru_up, g_w_pool_up, g_w_o, g_w_ff1, g_w_ff2]
    big_m = [m_w_in[0].T, m_w_lru_up[0], m_w_pool_up[0], m_w_o[0], m_w_ff1[0], m_w_ff2[0]]
    big_v = [v_w_in[0].T, v_w_lru_up[0], v_w_pool_up[0], v_w_o[0], v_w_ff1[0], v_w_ff2[0]]
    big_out, _ = _adamw_big(big_w, big_g, big_m, big_v)
    big_g[0] = big_g[0].T
    big_out[0] = tuple(o.T for o in big_out[0])

    small = dict(
        norm_mix_pre=(norm_mix_pre, m_norm_mix_pre, v_norm_mix_pre),
        norm_mix_post=(norm_mix_post, m_norm_mix_post, v_norm_mix_post),
        norm_mlp_pre=(norm_mlp_pre, m_norm_mlp_pre, v_norm_mlp_pre),
        norm_mlp_post=(norm_mlp_post, m_norm_mlp_post, v_norm_mlp_post),
        b_gate=(b_gate, m_b_gate, v_b_gate), conv_w=(conv_w, m_conv_w, v_conv_w),
        conv_b=(conv_b, m_conv_b, v_conv_b), lru_w_a=(lru_w_a, m_lru_w_a, v_lru_w_a),
        lru_b_a=(lru_b_a, m_lru_b_a, v_lru_b_a), lru_w_x=(lru_w_x, m_lru_w_x, v_lru_w_x),
        lru_b_x=(lru_b_x, m_lru_b_x, v_lru_b_x), lru_lambda=(lru_lambda, m_lru_lambda, v_lru_lambda),
        pool_w=(pool_w, m_pool_w, v_pool_w), pool_scale=(pool_scale, m_pool_scale, v_pool_scale))
    loss_row, small_out = _adamw_small(
        vec_parts, g_pool.reshape(pool_w.shape), g_wa.reshape(lru_w_a.shape), g_wx.reshape(lru_w_x.shape),
        jnp.reshape(me, (1,)).astype(jnp.int32), small)
    grads = {n: o[0] for n, o in small_out.items()}
    delta = {n: o[1] for n, o in small_out.items()}
    new_m = {n: o[2] for n, o in small_out.items()}
    new_v = {n: o[3] for n, o in small_out.items()}

    for name, g, (dl, nm, nv) in zip(big_names, big_g, big_out):
        grads[name], delta[name], new_m[name], new_v[name] = g[None], dl[None], nm[None], nv[None]

    loss = loss_row[0, 0]
    order = ["norm_mix_pre", "norm_mix_post", "norm_mlp_pre", "norm_mlp_post", "w_in", "b_gate", "conv_w",
             "conv_b", "lru_w_a", "lru_b_a", "lru_w_x", "lru_b_x", "lru_lambda", "pool_w", "pool_scale",
             "w_lru_up", "w_pool_up", "w_o", "w_ff1", "w_ff2"]
    return (loss, grad_x[None], *[grads[n] for n in order], *[delta[n] for n in order],
            *[new_m[n] for n in order], *[new_v[n] for n in order])
```

```python
import functools
import math
import operator
import types

import jax
import jax.numpy as jnp
from jax import lax
from jax.experimental import pallas as pl
from jax.experimental.pallas import tpu as pltpu

F32 = jnp.float32
BF16 = jnp.bfloat16
NORM_EPS = 1e-6
LRU_C = 8.0
N_LRU_HEADS = 16
LRU_HEAD_DIM = 64
POOL_WINDOWS = (2, 4, 8, 16)
POOL_GROUP_DIM = 128
ADAM_LR = 0.001
ADAM_B1 = 0.9
ADAM_B2 = 0.999
ADAM_EPS = 1e-08
ADAM_WD = 0.01
ADAM_STEP = 10
N_DEV = 8
V7X_VMEM_LIMIT_BYTES = 56 * 1024 * 1024
LRU_CB = 256
MESH = pl.DeviceIdType.MESH
ANY = pl.BlockSpec(memory_space=pl.ANY)


def _tile(n, pref):
    t = min(n, pref)
    assert n % t == 0, (n, pref)
    return t


def _dot_nn(a, b):
    return lax.dot_general(a, b, (((1,), (0,)), ((), ())), preferred_element_type=F32)


def _dot_nt(a, b):
    return lax.dot_general(a, b, (((1,), (1,)), ((), ())), preferred_element_type=F32)


def _dot_tn(a, b):
    return lax.dot_general(a, b, (((0,), (0,)), ((), ())), preferred_element_type=F32)


def _row_chunks(n_rows, fn, chunk=256):
    chunk = min(chunk, n_rows)
    assert n_rows % chunk == 0

    def step(r, carry):
        fn(pl.ds(pl.multiple_of(r * chunk, chunk), chunk))
        return carry

    lax.fori_loop(0, n_rows // chunk, step, 0)


def _sig(x):
    return 1.0 / (1.0 + jnp.exp(-x))


def _rms_hat(x):
    r = lax.rsqrt(jnp.mean(x * x, axis=-1, keepdims=True) + NORM_EPS)
    return x * r, r


def _rms_bwd(dn, xhat, r, g):
    q = dn * g
    dx = r * (q - xhat * jnp.mean(q * xhat, axis=-1, keepdims=True))
    dg = jnp.sum(dn * xhat, axis=0, keepdims=True)
    return dx, dg


_GELU_K = math.sqrt(2.0 / math.pi)
_GELU_C = 0.044715


def _gelu_and_grad(g):
    t = jnp.tanh(_GELU_K * (g + _GELU_C * g * g * g))
    val = 0.5 * g * (1.0 + t)
    grad = 0.5 * (1.0 + t) + 0.5 * g * (1.0 - t * t) * (_GELU_K * (1.0 + 3.0 * _GELU_C * g * g))
    return val, grad


def _softplus_neg(lam):
    z = -lam
    e = jnp.exp(-jnp.abs(z))
    u = 1.0 + e
    d = u - 1.0
    l1p = jnp.where(d == 0.0, e, jnp.log(u) * (e / jnp.where(d == 0.0, 1.0, d)))
    return jnp.maximum(z, 0.0) + l1p


def _lru_gates(xc, wa, ba, wx, bx, lam):
    xcb = xc.astype(BF16)
    r = _sig(_dot_nn(xcb, wa) + ba)
    i = _sig(_dot_nn(xcb, wx) + bx)
    sp = _softplus_neg(lam)
    log_a = (-LRU_C) * r * sp
    a = jnp.exp(log_a)
    mult = jnp.sqrt(-jnp.tanh(log_a) * (1.0 + a * a))
    return xcb, r, i, sp, log_a, a, mult


def _place():
    return lax.axis_index("x"), lax.axis_index("y"), lax.axis_index("c")


def _ag_plan(shards, pieces=None, bufs=None):
    na = len(shards)
    n_kinds = 7

    def parts(ins, outs, sems):
        send_sems, recv_sems, local_sems = sems
        x, y, c = _place()
        me, sibling = (x, y, c), (x, y, 1 - c)
        x_nb, y_nb, diag = (1 - x, y), (x, 1 - y), (1 - x, 1 - y)
        relay_src = (c * (1 - x) + (1 - c) * x, c * y + (1 - c) * (1 - y))
        relay_dst = (c * x + (1 - c) * (1 - x), c * (1 - y) + (1 - c) * y)

        def own(a):
            return ins[a] if pieces is None else ins[a].at[pl.ds(*pieces[a])]

        def slot(a, px, py, pc):
            idx = 4 * px + 2 * py + pc
            return outs[a].at[idx] if pieces is None else outs[a].at[idx, pl.ds(*pieces[a])]

        def copy(a, k, block, to, src=None):
            return pltpu.make_async_remote_copy(
                src_ref=slot(a, *block) if src is None else src,
                dst_ref=slot(a, *block),
                send_sem=send_sems.at[a * n_kinds + k],
                recv_sem=recv_sems.at[a * n_kinds + k],
                device_id=to,
                device_id_type=MESH,
            )

        mine = [pltpu.make_async_copy(own(a), slot(a, *me), local_sems.at[a]) for a in range(na)]
        first, second, third = [], [], []
        for a in range(na):
            first += [copy(a, 0, me, sibling, src=own(a)), copy(a, 1, me, (*x_nb, c), src=own(a)),
                      copy(a, 2, me, (*y_nb, c), src=own(a))]
            second += [copy(a, 3, (*relay_src, c), (*relay_dst, c)), copy(a, 4, (*x_nb, c), sibling),
                       copy(a, 5, (*y_nb, c), sibling)]
            third.append(copy(a, 6, (*diag, c), sibling))
        return sibling, c, x_nb, y_nb, diag, copy, mine, first, second, third

    def start(ins, outs, sems):
        _, _, _, _, _, _, mine, first, _, _ = parts(ins, outs, sems)
        for cp in mine + first:
            cp.start()

    def middle(ins, outs, sems):
        _, c, x_nb, y_nb, _, copy, _, _, second, _ = parts(ins, outs, sems)
        for a in range(na):
            copy(a, 1, (*x_nb, c), (*x_nb, c)).wait_recv()
            copy(a, 2, (*y_nb, c), (*y_nb, c)).wait_recv()
        for cp in second:
            cp.start()

    def finish(ins, outs, sems):
        sibling, c, x_nb, y_nb, diag, copy, mine, first, second, third = parts(ins, outs, sems)
        for a in range(na):
            copy(a, 3, (*diag, c), (*diag, c)).wait_recv()
            third[a].start()
        for a in range(na):
            copy(a, 0, sibling, sibling).wait_recv()
            copy(a, 4, (*x_nb, 1 - c), sibling).wait_recv()
            copy(a, 5, (*y_nb, 1 - c), sibling).wait_recv()
            copy(a, 6, (*diag, 1 - c), sibling).wait_recv()
        for cp in first + second + third:
            cp.wait_send()
        for cp in mine:
            cp.wait()

    return types.SimpleNamespace(
        ins=list(shards) + list(bufs or []),
        out_shapes=[jax.ShapeDtypeStruct((N_DEV,) + s.shape, s.dtype) for s in shards],
        sems=[pltpu.SemaphoreType.DMA((n_kinds * na,)), pltpu.SemaphoreType.DMA((n_kinds * na,)),
              pltpu.SemaphoreType.DMA((na,))],
        aliases=[(na + a, a) for a in range(na)] if bufs else [],
        peers=frozenset({"sibling", "neighbours"}), start=start, middle=middle, finish=finish)


def _rs_sibling_plan(fulls):
    na = len(fulls)
    rs = [f.shape[0] // N_DEV for f in fulls]

    def copies(ins, outs, sems):
        send_sems, recv_sems = sems
        x, y, c = _place()
        out = []
        for a in range(na):
            for q in range(4):
                shard = 2 * q + (1 - c)
                out.append(pltpu.make_async_remote_copy(
                    src_ref=ins[a].at[pl.ds(shard * rs[a], rs[a])],
                    dst_ref=outs[a].at[q],
                    send_sem=send_sems.at[a * 4 + q],
                    recv_sem=recv_sems.at[a * 4 + q],
                    device_id=(x, y, 1 - c),
                    device_id_type=MESH,
                ))
        return out

    def start(ins, outs, sems):
        for cp in copies(ins, outs, sems):
            cp.start()

    def finish(ins, outs, sems):
        for cp in copies(ins, outs, sems):
            cp.wait()

    return types.SimpleNamespace(
        ins=list(fulls),
        out_shapes=[jax.ShapeDtypeStruct((4, r) + f.shape[1:], f.dtype) for r, f in zip(rs, fulls)],
        sems=[pltpu.SemaphoreType.DMA((4 * na,)), pltpu.SemaphoreType.DMA((4 * na,))],
        peers=frozenset({"sibling"}), start=start, finish=finish)


def _rs_chips_plan(sends, pieces=None, bufs=None):
    na = len(sends)

    def copies(ins, outs, sems):
        send_sems, recv_sems = sems
        x, y, c = _place()
        chips = [(1 - x, y), (x, 1 - y), (1 - x, 1 - y)]
        out = []
        for a in range(na):
            for k, chip in enumerate(chips):
                rows = (k,) if pieces is None else (k, pl.ds(*pieces[a]))
                out.append(pltpu.make_async_remote_copy(
                    src_ref=ins[a].at[rows],
                    dst_ref=outs[a].at[rows],
                    send_sem=send_sems.at[a * 3 + k],
                    recv_sem=recv_sems.at[a * 3 + k],
                    device_id=(*chip, c),
                    device_id_type=MESH,
                ))
        return out

    def start(ins, outs, sems):
        for cp in copies(ins, outs, sems):
            cp.start()

    def finish(ins, outs, sems):
        for cp in copies(ins, outs, sems):
            cp.wait()

    return types.SimpleNamespace(
        ins=list(sends) + list(bufs or []),
        out_shapes=[jax.ShapeDtypeStruct(s.shape, s.dtype) for s in sends],
        sems=[pltpu.SemaphoreType.DMA((3 * na,)), pltpu.SemaphoreType.DMA((3 * na,))],
        aliases=[(na + a, a) for a in range(na)] if bufs else [],
        peers=frozenset({"chips"}), start=start, finish=finish)


def _join(plans):
    ins, outs, sems, aliases, offs = [], [], [], [], []
    for p in plans:
        offs.append((len(ins), len(outs), len(sems)))
        aliases += [(len(ins) + ci, len(outs) + co) for ci, co in getattr(p, "aliases", [])]
        ins += p.ins
        outs += p.out_shapes
        sems += p.sems

    def cut(p, off, i, o, s):
        return (i[off[0]:off[0] + len(p.ins)], o[off[1]:off[1] + len(p.out_shapes)],
                s[off[2]:off[2] + len(p.sems)])

    def start(i, o, s):
        for p, off in zip(plans, offs):
            p.start(*cut(p, off, i, o, s))

    def middle(i, o, s):
        for p, off in zip(plans, offs):
            if getattr(p, "middle", None) is not None:
                p.middle(*cut(p, off, i, o, s))

    def finish(i, o, s):
        for p, off in zip(plans, offs):
            p.finish(*cut(p, off, i, o, s))

    def split(results):
        return [list(results[off[1]:off[1] + len(p.out_shapes)]) for p, off in zip(plans, offs)]

    return types.SimpleNamespace(ins=ins, out_shapes=outs, sems=sems, aliases=aliases,
                                 peers=frozenset().union(*[p.peers for p in plans]),
                                 start=start, middle=middle, finish=finish, split=split)


COLLECTIVE_ID = {frozenset({"sibling"}): 0, frozenset({"chips"}): 1, frozenset({"sibling", "chips"}): 2,
                 frozenset({"sibling", "neighbours"}): 3}


def _handshake(peers):
    x, y, c = _place()
    devs = []
    if "sibling" in peers:
        devs.append((x, y, 1 - c))
    if "neighbours" in peers:
        devs += [(1 - x, y, c), (x, 1 - y, c)]
    if "chips" in peers:
        assert "neighbours" not in peers
        devs += [(1 - x, y, c), (x, 1 - y, c), (1 - x, 1 - y, c)]
    barrier = pltpu.get_barrier_semaphore()
    for dev in devs:
        pl.semaphore_signal(barrier, inc=1, device_id=dev, device_id_type=MESH)
    pl.semaphore_wait(barrier, len(devs))


def _in_hbm(args):
    return [pltpu.with_memory_space_constraint(a, pltpu.HBM) for a in args]


def _run_plan(plan, name):
    n_in, n_out = len(plan.ins), len(plan.out_shapes)

    def body(*refs):
        ins, outs, sems = refs[:n_in], refs[n_in:n_in + n_out], refs[n_in + n_out:]
        _handshake(plan.peers)
        plan.start(ins, outs, sems)
        if getattr(plan, "middle", None) is not None:
            plan.middle(ins, outs, sems)
        plan.finish(ins, outs, sems)

    return pl.pallas_call(
        body,
        name=name,
        in_specs=[ANY] * n_in,
        out_specs=[ANY] * n_out,
        out_shape=plan.out_shapes,
        scratch_shapes=plan.sems,
        input_output_aliases=dict(getattr(plan, "aliases", [])),
        compiler_params=pltpu.CompilerParams(collective_id=COLLECTIVE_ID[plan.peers]),
    )(*_in_hbm(plan.ins))


def _call(body, *, name, grid, in_specs, out_specs, out_shape, args, scratch_shapes=(), aliases=None,
          carry=None):
    n_in, n_out, n_scr = len(in_specs), len(out_shape), len(scratch_shapes)
    params = pltpu.CompilerParams(
        dimension_semantics=("arbitrary",) * len(grid), vmem_limit_bytes=V7X_VMEM_LIMIT_BYTES)
    if carry is None:
        outs = pl.pallas_call(
            body, name=name, grid=grid, in_specs=list(in_specs), out_specs=list(out_specs),
            out_shape=list(out_shape), scratch_shapes=list(scratch_shapes),
            input_output_aliases=aliases or {}, compiler_params=params)(*_in_hbm(args))
        return list(outs), []
    c_in, c_out = len(carry.ins), len(carry.out_shapes)

    def full(*refs):
        p = 0
        ins = refs[p:p + n_in]
        p += n_in
        cins = refs[p:p + c_in]
        p += c_in
        outs = refs[p:p + n_out]
        p += n_out
        couts = refs[p:p + c_out]
        p += c_out
        scr = refs[p:p + n_scr]
        csems = refs[p + n_scr:]
        ids = [pl.program_id(a) for a in range(len(grid))]
        first = functools.reduce(operator.and_, [i == 0 for i in ids])
        last = functools.reduce(operator.and_, [i == g - 1 for i, g in zip(ids, grid)])

        @pl.when(first)
        def _():
            _handshake(carry.peers)
            carry.start(cins, couts, csems)

        if getattr(carry, "middle", None) is not None:
            n_steps = math.prod(grid)
            flat = functools.reduce(lambda acc, ig: acc * ig[1] + ig[0], zip(ids, grid), 0)

            @pl.when(flat == (2 * n_steps) // 3)
            def _():
                carry.middle(cins, couts, csems)

        body(*ins, *outs, *scr)

        @pl.when(last)
        def _():
            carry.finish(cins, couts, csems)

    all_aliases = dict(aliases or {})
    all_aliases.update({n_in + ci: n_out + co for ci, co in getattr(carry, "aliases", [])})
    params = pltpu.CompilerParams(
        dimension_semantics=("arbitrary",) * len(grid), vmem_limit_bytes=V7X_VMEM_LIMIT_BYTES,
        collective_id=COLLECTIVE_ID[carry.peers])
    outs = pl.pallas_call(
        full, name=name, grid=grid,
        in_specs=list(in_specs) + [ANY] * c_in,
        out_specs=list(out_specs) + [ANY] * c_out,
        out_shape=list(out_shape) + list(carry.out_shapes),
        scratch_shapes=list(scratch_shapes) + list(carry.sems),
        input_output_aliases=all_aliases, compiler_params=params)(*_in_hbm(args), *_in_hbm(carry.ins))
    return list(outs[:n_out]), list(outs[n_out:])


def _norm_proj(x, g1, w_int, carry=None):
    t, d = x.shape
    n = w_int.shape[0]
    tt, tn = _tile(t, 2048), _tile(n, 512)

    def body(x_ref, g_ref, w_ref, proj_ref, h1_ref, h1_s):
        @pl.when(pl.program_id(1) == 0)
        def _():
            def norm_rows(rows):
                xhat, _ = _rms_hat(x_ref[rows, :])
                h = (xhat * g_ref[...]).astype(BF16)
                h1_s[rows, :] = h
                h1_ref[rows, :] = h

            _row_chunks(tt, norm_rows)

        proj_ref[...] = _dot_nt(h1_s[...], w_ref[...]).astype(BF16)

    return _call(
        body, name="norm_proj", grid=(t // tt, n // tn),
        in_specs=[
            pl.BlockSpec((tt, d), lambda i, j: (i, 0)),
            pl.BlockSpec((1, d), lambda i, j: (0, 0)),
            pl.BlockSpec((tn, d), lambda i, j: (j, 0)),
        ],
        out_specs=[
            pl.BlockSpec((tt, tn), lambda i, j: (i, j)),
            pl.BlockSpec((tt, d), lambda i, j: (i, 0)),
        ],
        out_shape=[jax.ShapeDtypeStruct((t, n), BF16), jax.ShapeDtypeStruct((t, d), BF16)],
        scratch_shapes=[pltpu.VMEM((tt, d), BF16)],
        args=(x, g1, w_int), carry=carry)


def _scan_rows(av, bv, reverse):
    tc = av.shape[0]
    row = lax.broadcasted_iota(jnp.int32, av.shape, 0)
    s = 1
    while s < tc:
        if s < 8:
            keep = (row < tc - s) if reverse else (row >= s)
            shift = (tc - s) if reverse else s
            a_sh = jnp.where(keep, pltpu.roll(av, shift, 0), 1.0)
            b_sh = jnp.where(keep, pltpu.roll(bv, shift, 0), 0.0)
            bv = av * b_sh + bv
            av = av * a_sh
        elif reverse:
            bv = jnp.concatenate([av[:tc - s] * bv[s:] + bv[:tc - s], bv[tc - s:]], axis=0)
            av = jnp.concatenate([av[:tc - s] * av[s:], av[tc - s:]], axis=0)
        else:
            bv = jnp.concatenate([bv[:s], av[s:] * bv[:tc - s] + bv[s:]], axis=0)
            av = jnp.concatenate([av[:s], av[s:] * av[:tc - s]], axis=0)
        s *= 2
    return av, bv


def _fill_block_diag(w_ref, bd_ref):
    bd_ref[...] = jnp.zeros_like(bd_ref)
    hd = LRU_HEAD_DIM
    for k in range(w_ref.shape[0]):
        bd_ref[k * hd:(k + 1) * hd, k * hd:(k + 1) * hd] = w_ref[k].astype(BF16)


def _lru_fwd(proj, conv_w, conv_b, w_a, b_a, w_x, b_x, lam, carry=None):
    t = proj.shape[0]
    dr = conv_b.shape[1]
    cb = LRU_CB
    tc = _tile(t, 256)
    ncb, ntc = dr // cb, t // tc

    def body(xp_ref, g_ref, cw_ref, cb_ref, wa_ref, ba_ref, wx_ref, bx_ref, lam_ref,
             y_ref, h_ref, xc_ref, prevx_s, hlast_s, wa_s, wx_s):
        c = pl.program_id(1)

        @pl.when(c == 0)
        def _():
            prevx_s[...] = jnp.zeros_like(prevx_s)
            hlast_s[...] = jnp.zeros_like(hlast_s)
            _fill_block_diag(wa_ref, wa_s)
            _fill_block_diag(wx_ref, wx_s)

        x = xp_ref[...].astype(F32)
        prev = prevx_s[...]
        row = lax.broadcasted_iota(jnp.int32, x.shape, 0)

        def sh(j):
            return jnp.where(row >= j, pltpu.roll(x, j, 0), pltpu.roll(prev, j, 0))

        xc = (cb_ref[...] + cw_ref[0:1, :] * sh(3) + cw_ref[1:2, :] * sh(2)
              + cw_ref[2:3, :] * sh(1) + cw_ref[3:4, :] * x)
        prevx_s[...] = x
        xc_ref[...] = xc
        _, _, i, _, _, a, mult = _lru_gates(xc, wa_s[...], ba_ref[...], wx_s[...], bx_ref[...],
                                            lam_ref[...])
        av, bv = _scan_rows(a, mult * (i * xc), reverse=False)
        h = av * hlast_s[...] + bv
        h_ref[...] = h
        hlast_s[...] = h_ref[tc - 1:tc, :]
        gel, _ = _gelu_and_grad(g_ref[...].astype(F32))
        y_ref[...] = (h * gel).astype(BF16)

    vec = pl.BlockSpec((1, cb), lambda j, c: (0, j))
    blk = pl.BlockSpec((tc, cb), lambda j, c: (c, j))
    mat = pl.BlockSpec((cb // LRU_HEAD_DIM, LRU_HEAD_DIM, LRU_HEAD_DIM), lambda j, c: (j, 0, 0))
    return _call(
        body, name="lru_fwd", grid=(ncb, ntc),
        in_specs=[
            blk,
            pl.BlockSpec((tc, cb), lambda j, c: (c, ncb + j)),
            pl.BlockSpec((4, cb), lambda j, c: (0, j)),
            vec, mat, vec, mat, vec, vec,
        ],
        out_specs=[blk, blk, blk],
        out_shape=[
            jax.ShapeDtypeStruct((t, dr), BF16),
            jax.ShapeDtypeStruct((t, dr), F32),
            jax.ShapeDtypeStruct((t, dr), F32),
        ],
        scratch_shapes=[pltpu.VMEM((tc, cb), F32), pltpu.VMEM((1, cb), F32),
                        pltpu.VMEM((cb, cb), BF16), pltpu.VMEM((cb, cb), BF16)],
        args=(proj, proj, conv_w, conv_b, w_a, b_a, w_x, b_x, lam), carry=carry)


def _pool_select(col, vals):
    out = vals[3]
    for g in (2, 1, 0):
        out = jnp.where(col < (g + 1) * POOL_GROUP_DIM, vals[g], out)
    return out


def _pool_fwd(proj, pool_w, pool_scale, col_block):
    t = proj.shape[0]
    dp = pool_scale.shape[1]
    tc = _tile(t, 256)
    ntc = t // tc

    def body(x_ref, w_ref, sc_ref, y_ref, p_ref, px, p2, p4, p8):
        c = pl.program_id(0)

        @pl.when(c == 0)
        def _():
            for s in (px, p2, p4, p8):
                s[...] = jnp.zeros_like(s)

        x = x_ref[...].astype(F32)
        row = lax.broadcasted_iota(jnp.int32, x.shape, 0)
        col = lax.broadcasted_iota(jnp.int32, x.shape, 1)

        def sh(v, pv, j):
            return jnp.where(row >= j, pltpu.roll(v, j, 0), pltpu.roll(pv[...], j, 0))

        s2 = x + sh(x, px, 1)
        s4 = s2 + sh(s2, p2, 2)
        s8 = s4 + sh(s4, p4, 4)
        s16 = s8 + sh(s8, p8, 8)
        px[...] = x
        p2[...] = s2
        p4[...] = s4
        p8[...] = s8
        wsum = _pool_select(col, (s2, s4, s8, s16))
        win = _pool_select(col, POOL_WINDOWS)
        cnt = jnp.minimum(c * tc + row + 1, win).astype(F32)
        p = wsum / cnt - x
        pb = p.astype(BF16)
        p_ref[...] = pb
        for g in range(len(POOL_WINDOWS)):
            sl = slice(g * POOL_GROUP_DIM, (g + 1) * POOL_GROUP_DIM)
            yg = _dot_nn(pb[:, sl], w_ref[g]) * sc_ref[:, sl]
            y_ref[:, sl] = yg.astype(BF16)

    return _call(
        body, name="pool_fwd", grid=(ntc,),
        in_specs=[
            pl.BlockSpec((tc, dp), lambda c: (c, col_block)),
            pl.BlockSpec(pool_w.shape, lambda c: (0, 0, 0)),
            pl.BlockSpec((1, dp), lambda c: (0, 0)),
        ],
        out_specs=[pl.BlockSpec((tc, dp), lambda c: (c, 0))] * 2,
        out_shape=[jax.ShapeDtypeStruct((t, dp), BF16)] * 2,
        scratch_shapes=[pltpu.VMEM((tc, dp), F32)] * 4,
        args=(proj, pool_w, pool_scale))[0]


def _branch_mix(y_lru, y_pool, w_lru_up, w_pool_upt, proj, b_gate, ga_block, gb_block, carry=None):
    t, d = y_lru.shape
    dp = y_pool.shape[1]
    tt, tn = _tile(t, 1024), 512
    nj = d // tn

    def body(yl_ref, yp_ref, wl_ref, wp_ref, ga_ref, gb_ref, ba_ref, bb_ref, bra_ref, brb_ref, mix_ref):
        br_a = _dot_nn(yl_ref[...], wl_ref[...])
        br_b = _dot_nt(yp_ref[...], wp_ref[...])
        bra_ref[...] = br_a.astype(BF16)
        brb_ref[...] = br_b.astype(BF16)
        ga = _sig(ga_ref[...].astype(F32) + ba_ref[...])
        gb = _sig(gb_ref[...].astype(F32) + bb_ref[...])
        mix_ref[...] = (ga * br_a + gb * br_b).astype(BF16)

    out = pl.BlockSpec((tt, tn), lambda j, i: (i, j))
    return _call(
        body, name="branch_mix", grid=(nj, t // tt),
        in_specs=[
            pl.BlockSpec((tt, d), lambda j, i: (i, 0)),
            pl.BlockSpec((tt, dp), lambda j, i: (i, 0)),
            pl.BlockSpec((d, tn), lambda j, i: (0, j)),
            pl.BlockSpec((tn, dp), lambda j, i: (j, 0)),
            pl.BlockSpec((tt, tn), lambda j, i: (i, ga_block + j)),
            pl.BlockSpec((tt, tn), lambda j, i: (i, gb_block + j)),
            pl.BlockSpec((1, tn), lambda j, i: (0, j)),
            pl.BlockSpec((1, tn), lambda j, i: (0, nj + j)),
        ],
        out_specs=[out, out, out],
        out_shape=[jax.ShapeDtypeStruct((t, d), BF16)] * 3,
        args=(y_lru, y_pool, w_lru_up, w_pool_upt, proj, proj, b_gate, b_gate), carry=carry)


def _wo_norm(mix, w_o, x, g2, g3, carry=None):
    t, d = x.shape
    tt = _tile(t, 512)

    def body(mix_ref, w_ref, x_ref, g2_ref, g3_ref, m_ref, x2_ref, h3_ref):
        m = _dot_nn(mix_ref[...], w_ref[...])
        m_ref[...] = m
        mhat, _ = _rms_hat(m)
        x2 = x_ref[...] + mhat * g2_ref[...]
        x2_ref[...] = x2
        xhat, _ = _rms_hat(x2)
        h3_ref[...] = (xhat * g3_ref[...]).astype(BF16)

    row = pl.BlockSpec((tt, d), lambda i: (i, 0))
    vec = pl.BlockSpec((1, d), lambda i: (0, 0))
    return _call(
        body, name="wo_norm", grid=(t // tt,),
        in_specs=[row, pl.BlockSpec((d, d), lambda i: (0, 0)), row, vec, vec],
        out_specs=[row, row, row],
        out_shape=[
            jax.ShapeDtypeStruct((t, d), F32),
            jax.ShapeDtypeStruct((t, d), F32),
            jax.ShapeDtypeStruct((t, d), BF16),
        ],
        args=(mix, w_o, x, g2, g3), carry=carry)


def _ff1(h3, w_ff1t, carry=None):
    t, d = h3.shape
    n = w_ff1t.shape[0]
    tt, tn = _tile(t, 2048), _tile(n, 512)

    def body(h_ref, w_ref, rf_ref):
        rf_ref[...] = jnp.maximum(_dot_nt(h_ref[...], w_ref[...]), 0.0).astype(BF16)

    out = pl.BlockSpec((tt, tn), lambda i, j: (i, j))
    return _call(
        body, name="ff1", grid=(t // tt, n // tn),
        in_specs=[pl.BlockSpec((tt, d), lambda i, j: (i, 0)), pl.BlockSpec((tn, d), lambda i, j: (j, 0))],
        out_specs=[out],
        out_shape=[jax.ShapeDtypeStruct((t, n), BF16)],
        args=(h3, w_ff1t), carry=carry)


def _ff2_loss(rf, w_ff2, x2, g4, target):
    t, k = rf.shape
    d = x2.shape[1]
    tt, tk = _tile(t, 1024), _tile(k, 1024)
    nk = k // tk

    def body(a_ref, w_ref, x2_ref, g_ref, tg_ref, dy_ref, df_ref, dg_ref, loss_ref, acc):
        i, kk = pl.program_id(0), pl.program_id(1)

        @pl.when(kk == 0)
        def _():
            acc[...] = jnp.zeros_like(acc)

        @pl.when((i == 0) & (kk == 0))
        def _():
            dg_ref[...] = jnp.zeros_like(dg_ref)
            loss_ref[...] = jnp.zeros_like(loss_ref)

        rf_tile = a_ref[...]
        acc[...] += _dot_nn(rf_tile * rf_tile, w_ref[...])

        @pl.when(kk == nk - 1)
        def _():
            def tail(rows):
                fhat, r = _rms_hat(acc[rows, :])
                g = g_ref[...]
                e = x2_ref[rows, :] + fhat * g - tg_ref[rows, :]
                loss_ref[...] += 0.5 * jnp.sum(jnp.mean(e * e, axis=-1, keepdims=True))
                dy = e * (1.0 / d)
                dy_ref[rows, :] = dy.astype(BF16)
                df, dg = _rms_bwd(dy, fhat, r, g)
                df_ref[rows, :] = df.astype(BF16)
                dg_ref[...] += dg

            _row_chunks(tt, tail)

    row = pl.BlockSpec((tt, d), lambda i, kk: (i, 0))
    vec = pl.BlockSpec((1, d), lambda i, kk: (0, 0))
    return _call(
        body, name="ff2_loss", grid=(t // tt, nk),
        in_specs=[
            pl.BlockSpec((tt, tk), lambda i, kk: (i, kk)),
            pl.BlockSpec((tk, d), lambda i, kk: (kk, 0)),
            row, vec, row,
        ],
        out_specs=[row, row, vec, pl.BlockSpec((1, 128), lambda i, kk: (0, 0))],
        out_shape=[
            jax.ShapeDtypeStruct((t, d), BF16),
            jax.ShapeDtypeStruct((t, d), BF16),
            jax.ShapeDtypeStruct((1, d), F32),
            jax.ShapeDtypeStruct((1, 128), F32),
        ],
        scratch_shapes=[pltpu.VMEM((tt, d), F32)],
        args=(rf, w_ff2, x2, g4, target))[0]


def _ff2_bwd(df, w_ff2, rf, carry=None):
    t, d = df.shape
    n = w_ff2.shape[0]
    tt, tn = _tile(t, 2048), _tile(n, 512)

    def body(df_ref, w_ref, rf_ref, out_ref):
        d_act = _dot_nt(df_ref[...], w_ref[...])
        out_ref[...] = (d_act * (2.0 * rf_ref[...].astype(F32))).astype(BF16)

    blk = pl.BlockSpec((tt, tn), lambda i, j: (i, j))
    return _call(
        body, name="ff2_bwd", grid=(t // tt, n // tn),
        in_specs=[pl.BlockSpec((tt, d), lambda i, j: (i, 0)), pl.BlockSpec((tn, d), lambda i, j: (j, 0)), blk],
        out_specs=[blk],
        out_shape=[jax.ShapeDtypeStruct((t, n), BF16)],
        args=(df, w_ff2, rf), carry=carry)


def _wgrad(a, b, name, prev=None, row_off=0, rows=None, carry=None, square_a=False):
    t, m = a.shape
    n = b.shape[1]
    rows = m if rows is None else rows
    tm, tk = _tile(m, 512), _tile(t, 2048)
    nk = t // tk
    assert row_off % tm == 0
    off = row_off // tm

    def body(*refs):
        a_ref, b_ref = refs[0], refs[1]
        o32_ref, o16_ref, acc = refs[-3], refs[-2], refs[-1]
        kk = pl.program_id(1)

        @pl.when(kk == 0)
        def _():
            acc[...] = jnp.zeros_like(acc)

        a_tile = a_ref[...]
        acc[...] += _dot_tn(a_tile * a_tile if square_a else a_tile, b_ref[...])

        @pl.when(kk == nk - 1)
        def _():
            o32_ref[...] = acc[...]
            o16_ref[...] = acc[...].astype(BF16)

    in_specs = [pl.BlockSpec((tk, tm), lambda i, kk: (kk, i)), pl.BlockSpec((tk, n), lambda i, kk: (kk, 0))]
    args = [a, b]
    aliases = {}
    if prev is not None:
        in_specs += [ANY, ANY]
        args += list(prev)
        aliases = {2: 0, 3: 1}
    out = pl.BlockSpec((tm, n), lambda i, kk: (off + i, 0))
    return _call(
        body, name=name, grid=(m // tm, nk),
        in_specs=in_specs, out_specs=[out, out],
        out_shape=[jax.ShapeDtypeStruct((rows, n), F32), jax.ShapeDtypeStruct((rows, n), BF16)],
        scratch_shapes=[pltpu.VMEM((tm, n), F32)],
        aliases=aliases, args=args, carry=carry)


def _wgrad_parts(parts, b, name, carry=None):
    t, n = b.shape
    tm = 512
    bounds = []
    lo = 0
    for part in parts:
        assert part.shape[0] == t and part.shape[1] % tm == 0
        bounds.append((lo, lo + part.shape[1] // tm))
        lo += part.shape[1] // tm
    nm = lo
    np_ = len(parts)

    def body(*refs):
        p_refs, b_ref, o32_ref, o16_ref = refs[:np_], refs[np_], refs[np_ + 1], refs[np_ + 2]
        i = pl.program_id(0)
        for (lo_p, hi_p), p_ref in zip(bounds, p_refs):
            @pl.when((i >= lo_p) & (i < hi_p))
            def _(p_ref=p_ref):
                res = _dot_tn(p_ref[...], b_ref[...])
                o32_ref[...] = res
                o16_ref[...] = res.astype(BF16)

    def part_spec(lo_p, hi_p):
        return pl.BlockSpec((t, tm), lambda i: (0, jnp.clip(i - lo_p, 0, hi_p - lo_p - 1)))

    out = pl.BlockSpec((tm, n), lambda i: (i, 0))
    return _call(
        body, name=name, grid=(nm,),
        in_specs=[part_spec(lo_p, hi_p) for lo_p, hi_p in bounds] + [pl.BlockSpec((t, n), lambda i: (0, 0))],
        out_specs=[out, out],
        out_shape=[jax.ShapeDtypeStruct((nm * tm, n), F32), jax.ShapeDtypeStruct((nm * tm, n), BF16)],
        args=(*parts, b), carry=carry)


def _ff1_bwd_norms(d_f1, w_ff1t, dy, x2, g3, m, g2, carry=None):
    t, k = d_f1.shape
    d = x2.shape[1]
    tt, tk = _tile(t, 1024), _tile(k, 1024)
    nk = k // tk

    def body(a_ref, w_ref, dy_ref, x2_ref, g3_ref, m_ref, g2_ref, dx2_ref, dm_ref, dg3_ref, dg2_ref, acc):
        i, kk = pl.program_id(0), pl.program_id(1)

        @pl.when(kk == 0)
        def _():
            acc[...] = jnp.zeros_like(acc)

        @pl.when((i == 0) & (kk == 0))
        def _():
            dg3_ref[...] = jnp.zeros_like(dg3_ref)
            dg2_ref[...] = jnp.zeros_like(dg2_ref)

        acc[...] += _dot_nn(a_ref[...], w_ref[...])

        @pl.when(kk == nk - 1)
        def _():
            def tail(rows):
                xhat, r3 = _rms_hat(x2_ref[rows, :])
                dx, dg3 = _rms_bwd(acc[rows, :], xhat, r3, g3_ref[...])
                dx2 = dy_ref[rows, :].astype(F32) + dx
                dx2_ref[rows, :] = dx2
                dg3_ref[...] += dg3
                mhat, r2 = _rms_hat(m_ref[rows, :])
                dm, dg2 = _rms_bwd(dx2, mhat, r2, g2_ref[...])
                dm_ref[rows, :] = dm.astype(BF16)
                dg2_ref[...] += dg2

            _row_chunks(tt, tail)

    row = pl.BlockSpec((tt, d), lambda i, kk: (i, 0))
    vec = pl.BlockSpec((1, d), lambda i, kk: (0, 0))
    return _call(
        body, name="ff1_bwd_norms", grid=(t // tt, nk),
        in_specs=[
            pl.BlockSpec((tt, tk), lambda i, kk: (i, kk)),
            pl.BlockSpec((tk, d), lambda i, kk: (kk, 0)),
            row, row, vec, row, vec,
        ],
        out_specs=[row, row, vec, vec],
        out_shape=[
            jax.ShapeDtypeStruct((t, d), F32),
            jax.ShapeDtypeStruct((t, d), BF16),
            jax.ShapeDtypeStruct((1, d), F32),
            jax.ShapeDtypeStruct((1, d), F32),
        ],
        scratch_shapes=[pltpu.VMEM((tt, d), F32)],
        args=(d_f1, w_ff1t, dy, x2, g3, m, g2), carry=carry)


def _wo_bwd_mix(dm, w_o, br_a, br_b, proj, b_gate, ga_block, gb_block, carry=None):
    t, d = dm.shape
    tt, tn = _tile(t, 1024), 512
    nj = d // tn

    def body(dm_ref, w_ref, bra_ref, brb_ref, ga_ref, gb_ref, ba_ref, bb_ref,
             dbra_ref, dbrb_ref, dga_ref, dgb_ref, dba_ref, dbb_ref):
        i = pl.program_id(1)

        @pl.when(i == 0)
        def _():
            dba_ref[...] = jnp.zeros_like(dba_ref)
            dbb_ref[...] = jnp.zeros_like(dbb_ref)

        d_mix = _dot_nt(dm_ref[...], w_ref[...])
        ga = _sig(ga_ref[...].astype(F32) + ba_ref[...])
        gb = _sig(gb_ref[...].astype(F32) + bb_ref[...])
        dbra_ref[...] = (d_mix * ga).astype(BF16)
        dbrb_ref[...] = (d_mix * gb).astype(BF16)
        dga = d_mix * bra_ref[...].astype(F32) * (ga * (1.0 - ga))
        dgb = d_mix * brb_ref[...].astype(F32) * (gb * (1.0 - gb))
        dga_ref[...] = dga.astype(BF16)
        dgb_ref[...] = dgb.astype(BF16)
        dba_ref[...] += jnp.sum(dga, axis=0, keepdims=True)
        dbb_ref[...] += jnp.sum(dgb, axis=0, keepdims=True)

    blk = pl.BlockSpec((tt, tn), lambda j, i: (i, j))
    vec = pl.BlockSpec((1, tn), lambda j, i: (0, j))
    return _call(
        body, name="wo_bwd_mix", grid=(nj, t // tt),
        in_specs=[
            pl.BlockSpec((tt, d), lambda j, i: (i, 0)),
            pl.BlockSpec((tn, d), lambda j, i: (j, 0)),
            blk, blk,
            pl.BlockSpec((tt, tn), lambda j, i: (i, ga_block + j)),
            pl.BlockSpec((tt, tn), lambda j, i: (i, gb_block + j)),
            vec,
            pl.BlockSpec((1, tn), lambda j, i: (0, nj + j)),
        ],
        out_specs=[blk, blk, blk, blk, vec, vec],
        out_shape=[jax.ShapeDtypeStruct((t, d), BF16)] * 4 + [jax.ShapeDtypeStruct((1, d), F32)] * 2,
        args=(dm, w_o, br_a, br_b, proj, proj, b_gate, b_gate), carry=carry)


def _lru_up_bwd(d_br_a, w_lru_up, proj, h, g_block, carry=None):
    t, d = d_br_a.shape
    tt, tn = _tile(t, 1024), 512

    def body(a_ref, w_ref, g_ref, h_ref, dh_ref, dg_ref):
        d_y = _dot_nt(a_ref[...], w_ref[...])
        gel, gel_grad = _gelu_and_grad(g_ref[...].astype(F32))
        dh_ref[...] = d_y * gel
        dg_ref[...] = (d_y * h_ref[...] * gel_grad).astype(BF16)

    blk = pl.BlockSpec((tt, tn), lambda i, j: (i, j))
    return _call(
        body, name="lru_up_bwd", grid=(t // tt, d // tn),
        in_specs=[
            pl.BlockSpec((tt, d), lambda i, j: (i, 0)),
            pl.BlockSpec((tn, d), lambda i, j: (j, 0)),
            pl.BlockSpec((tt, tn), lambda i, j: (i, g_block + j)),
            blk,
        ],
        out_specs=[blk, blk],
        out_shape=[jax.ShapeDtypeStruct((t, d), F32), jax.ShapeDtypeStruct((t, d), BF16)],
        args=(d_br_a, w_lru_up, proj, h), carry=carry)


def _lru_bwd(dh, xc, h, proj, conv_w, w_a, b_a, w_x, b_x, lam, carry=None):
    t, dr = dh.shape
    cb = LRU_CB
    hd = LRU_HEAD_DIM
    per = cb // hd
    tc = _tile(t, 256)
    ncb, ntc = dr // cb, t // tc

    def body(dh_ref, xc_ref, h_ref, hp_ref, xp_ref, cw_ref, wa_ref, ba_ref, wx_ref, bx_ref, lam_ref,
             dxp_ref, dwa_ref, dba_ref, dwx_ref, dbx_ref, dlam_ref, dcw_ref, dcb_ref,
             nextd_s, anext_s, gnext_s, tmp_s, wa_s, wx_s):
        c = pl.program_id(1)
        rc = ntc - 1 - c

        @pl.when(c == 0)
        def _():
            nextd_s[...] = jnp.zeros_like(nextd_s)
            anext_s[...] = jnp.zeros_like(anext_s)
            gnext_s[...] = jnp.zeros_like(gnext_s)
            for ref in (dwa_ref, dba_ref, dwx_ref, dbx_ref, dlam_ref, dcw_ref, dcb_ref):
                ref[...] = jnp.zeros_like(ref)
            _fill_block_diag(wa_ref, wa_s)
            _fill_block_diag(wx_ref, wx_s)

        xc = xc_ref[...]
        wa, wx, lam = wa_s[...], wx_s[...], lam_ref[...]
        xcb, r, i, sp, log_a, a, mult = _lru_gates(xc, wa, ba_ref[...], wx, bx_ref[...], lam)
        row = lax.broadcasted_iota(jnp.int32, xc.shape, 0)
        h = h_ref[...]
        hp = jnp.where(rc == 0, 0.0, hp_ref[...])
        hprev = jnp.where(row >= 1, pltpu.roll(h, 1, 0), pltpu.roll(hp, 1, 0))

        def up(v, nv, j):
            return jnp.where(row < tc - j, pltpu.roll(v, tc - j, 0), nv)

        av, bv = _scan_rows(up(a, anext_s[...], 1), dh_ref[...], reverse=True)
        gt = av * gnext_s[...] + bv
        tmp_s[...] = gt
        gnext_s[...] = tmp_s[0:1, :]
        tmp_s[...] = a
        anext_s[...] = tmp_s[0:1, :]

        da = gt * hprev
        ixc = i * xc
        d_mult = gt * ixc
        d_i = gt * mult * xc
        d_xc = gt * mult * i
        d_log_a = da * a - d_mult * (a * a) / mult
        d_pre_r = (d_log_a * ((-LRU_C) * sp)) * (r * (1.0 - r))
        d_pre_i = d_i * (i * (1.0 - i))
        d_sp = jnp.sum(d_log_a * ((-LRU_C) * r), axis=0, keepdims=True)
        dlam_ref[...] += d_sp * (-1.0 / (1.0 + jnp.exp(lam)))
        dpr = d_pre_r.astype(BF16)
        dpi = d_pre_i.astype(BF16)
        dba_ref[...] += jnp.sum(d_pre_r, axis=0, keepdims=True)
        dbx_ref[...] += jnp.sum(d_pre_i, axis=0, keepdims=True)
        pa = _dot_tn(xcb, dpr)
        px = _dot_tn(xcb, dpi)
        for k in range(per):
            dwa_ref[k] += pa[k * hd:(k + 1) * hd, k * hd:(k + 1) * hd]
            dwx_ref[k] += px[k * hd:(k + 1) * hd, k * hd:(k + 1) * hd]
        d_xc = d_xc + _dot_nt(dpr, wa) + _dot_nt(dpi, wx)

        nxt = nextd_s[...]
        xp = xp_ref[...].astype(F32)
        dxp = cw_ref[3:4, :] * d_xc
        dcw_ref[3:4, :] += jnp.sum(xp * d_xc, axis=0, keepdims=True)
        for j in (1, 2, 3):
            uj = up(d_xc, pltpu.roll(nxt, tc - j, 0), j)
            dxp = dxp + cw_ref[3 - j:4 - j, :] * uj
            dcw_ref[3 - j:4 - j, :] += jnp.sum(xp * uj, axis=0, keepdims=True)
        dcb_ref[...] += jnp.sum(d_xc, axis=0, keepdims=True)
        nextd_s[...] = d_xc
        dxp_ref[...] = dxp.astype(BF16)

    vec = pl.BlockSpec((1, cb), lambda j, c: (0, j))
    blk = pl.BlockSpec((tc, cb), lambda j, c: (ntc - 1 - c, j))
    mat = pl.BlockSpec((per, hd, hd), lambda j, c: (j, 0, 0))
    cwb = pl.BlockSpec((4, cb), lambda j, c: (0, j))
    return _call(
        body, name="lru_bwd", grid=(ncb, ntc),
        in_specs=[
            blk, blk, blk,
            pl.BlockSpec((tc, cb), lambda j, c: (jnp.maximum(ntc - 2 - c, 0), j)),
            blk, cwb, mat, vec, mat, vec, vec,
        ],
        out_specs=[blk, mat, vec, mat, vec, vec, cwb, vec],
        out_shape=[
            jax.ShapeDtypeStruct((t, dr), BF16),
            jax.ShapeDtypeStruct(w_a.shape, F32),
            jax.ShapeDtypeStruct((1, dr), F32),
            jax.ShapeDtypeStruct(w_x.shape, F32),
            jax.ShapeDtypeStruct((1, dr), F32),
            jax.ShapeDtypeStruct((1, dr), F32),
            jax.ShapeDtypeStruct((4, dr), F32),
            jax.ShapeDtypeStruct((1, dr), F32),
        ],
        scratch_shapes=[
            pltpu.VMEM((tc, cb), F32),
            pltpu.VMEM((1, cb), F32),
            pltpu.VMEM((1, cb), F32),
            pltpu.VMEM((tc, cb), F32),
            pltpu.VMEM((cb, cb), BF16),
            pltpu.VMEM((cb, cb), BF16),
        ],
        args=(dh, xc, h, h, proj, conv_w, w_a, b_a, w_x, b_x, lam), carry=carry)


def _pool_bwd(d_br_b, w_pool_upt, p, pool_w, pool_scale):
    t, d = d_br_b.shape
    dp = w_pool_upt.shape[1]
    tc = _tile(t, 256)
    ntc = t // tc
    ng = len(POOL_WINDOWS)

    def body(db_ref, wu_ref, p_ref, w_ref, sc_ref, dx_ref, dw_ref, dsc_ref, nz, n2, n4, n8, dp_s, dy_s):
        c = pl.program_id(0)
        rc = ntc - 1 - c

        @pl.when(c == 0)
        def _():
            for s in (nz, n2, n4, n8):
                s[...] = jnp.zeros_like(s)
            dw_ref[...] = jnp.zeros_like(dw_ref)
            dsc_ref[...] = jnp.zeros_like(dsc_ref)

        dy_s[...] = _dot_nn(db_ref[...], wu_ref[...])
        for g in range(ng):
            sl = slice(g * POOL_GROUP_DIM, (g + 1) * POOL_GROUP_DIM)
            pg = p_ref[:, sl]
            dyg = dy_s[:, sl]
            wg = w_ref[g].astype(BF16)
            q = _dot_nn(pg, wg)
            dsc_ref[:, sl] += jnp.sum(dyg * q, axis=0, keepdims=True)
            dpw = (dyg * sc_ref[:, sl]).astype(BF16)
            dw_ref[g] += _dot_tn(pg, dpw)
            dp_s[:, sl] = _dot_nt(dpw, wg)

        dpv = dp_s[...]
        row = lax.broadcasted_iota(jnp.int32, dpv.shape, 0)
        col = lax.broadcasted_iota(jnp.int32, dpv.shape, 1)
        win = _pool_select(col, POOL_WINDOWS)
        cnt = jnp.minimum(rc * tc + row + 1, win).astype(F32)
        z = dpv / cnt

        def up(v, nv, j):
            return jnp.where(row < tc - j, pltpu.roll(v, tc - j, 0), pltpu.roll(nv[...], tc - j, 0))

        u2 = z + up(z, nz, 1)
        u4 = u2 + up(u2, n2, 2)
        u8 = u4 + up(u4, n4, 4)
        u16 = u8 + up(u8, n8, 8)
        nz[...] = z
        n2[...] = u2
        n4[...] = u4
        n8[...] = u8
        dx_ref[...] = (_pool_select(col, (u2, u4, u8, u16)) - dpv).astype(BF16)

    blk = pl.BlockSpec((tc, dp), lambda c: (ntc - 1 - c, 0))
    full_w = pl.BlockSpec(pool_w.shape, lambda c: (0, 0, 0))
    vec = pl.BlockSpec((1, dp), lambda c: (0, 0))
    return _call(
        body, name="pool_bwd", grid=(ntc,),
        in_specs=[pl.BlockSpec((tc, d), lambda c: (ntc - 1 - c, 0)), pl.BlockSpec((d, dp), lambda c: (0, 0)),
                  blk, full_w, vec],
        out_specs=[blk, full_w, vec],
        out_shape=[
            jax.ShapeDtypeStruct((t, dp), BF16),
            jax.ShapeDtypeStruct(pool_w.shape, F32),
            jax.ShapeDtypeStruct((1, dp), F32),
        ],
        scratch_shapes=[pltpu.VMEM((tc, dp), F32)] * 6,
        args=(d_br_b, w_pool_upt, p, pool_w, pool_scale))[0]


def _win_bwd_norm(parts, w_int, dx2, x, g1, carry=None):
    t, d = x.shape
    tk = 512
    tt = _tile(t, 1024)
    bounds = []
    k0 = 0
    for part in parts:
        assert part.shape[1] % tk == 0
        bounds.append((k0, k0 + part.shape[1] // tk))
        k0 += part.shape[1] // tk
    nk = k0
    assert nk * tk == w_int.shape[0]
    np_ = len(parts)

    def body(*refs):
        p_refs = refs[:np_]
        w_ref, dx2_ref, x_ref, g_ref, gx_ref, dg_ref, acc = refs[np_:]
        i, kk = pl.program_id(0), pl.program_id(1)

        @pl.when(kk == 0)
        def _():
            acc[...] = jnp.zeros_like(acc)

        @pl.when((i == 0) & (kk == 0))
        def _():
            dg_ref[...] = jnp.zeros_like(dg_ref)

        for (lo, hi), p_ref in zip(bounds, p_refs):
            @pl.when((kk >= lo) & (kk < hi))
            def _(p_ref=p_ref):
                acc[...] += _dot_nn(p_ref[...], w_ref[...])

        @pl.when(kk == nk - 1)
        def _():
            def tail(rows):
                xhat, r = _rms_hat(x_ref[rows, :])
                dx, dg = _rms_bwd(acc[rows, :], xhat, r, g_ref[...])
                gx_ref[rows, :] = dx2_ref[rows, :] + dx
                dg_ref[...] += dg

            _row_chunks(tt, tail)

    def part_spec(lo, hi):
        return pl.BlockSpec((tt, tk), lambda i, kk: (i, jnp.clip(kk - lo, 0, hi - lo - 1)))

    row = pl.BlockSpec((tt, d), lambda i, kk: (i, 0))
    vec = pl.BlockSpec((1, d), lambda i, kk: (0, 0))
    return _call(
        body, name="win_bwd_norm", grid=(t // tt, nk),
        in_specs=[part_spec(lo, hi) for lo, hi in bounds]
        + [pl.BlockSpec((tk, d), lambda i, kk: (kk, 0)), row, row, vec],
        out_specs=[row, vec],
        out_shape=[jax.ShapeDtypeStruct((t, d), F32), jax.ShapeDtypeStruct((1, d), F32)],
        scratch_shapes=[pltpu.VMEM((tt, d), F32)],
        args=(*parts, w_int, dx2, x, g1), carry=carry)


def _adam_math(w, g, m, v):
    m = ADAM_B1 * m + (1.0 - ADAM_B1) * g
    v = ADAM_B2 * v + (1.0 - ADAM_B2) * (g * g)
    m_hat = m / (1.0 - ADAM_B1 ** ADAM_STEP)
    v_hat = v / (1.0 - ADAM_B2 ** ADAM_STEP)
    delta = -ADAM_LR * (m_hat / (jnp.sqrt(v_hat) + ADAM_EPS) + ADAM_WD * w)
    return delta, m, v


def _adamw_big(ws, gs, ms, vs):
    n = len(ws)
    nb = 4
    pair = [isinstance(g, tuple) for g in gs]

    def body(*refs):
        p = 0
        ins = []
        for a in range(n):
            k = 5 if pair[a] else 4
            ins.append(refs[p:p + k])
            p += k
        for a in range(n):
            g_out, d_ref, nm_ref, nv_ref = refs[p + 4 * a:p + 4 * a + 4]
            if pair[a]:
                w_ref, own_ref, recv_ref, m_ref, v_ref = ins[a]
                g = own_ref[...]
                for k in range(3):
                    g = g + recv_ref[k].astype(F32)
            else:
                w_ref, g_ref, m_ref, v_ref = ins[a]
                g = g_ref[...]
            dl, m, v = _adam_math(w_ref[...], g, m_ref[...], v_ref[...])
            g_out[...] = g
            d_ref[...] = dl
            nm_ref[...] = m
            nv_ref[...] = v

    in_specs, out_specs, out_shape, args = [], [], [], []
    for a, (w, g, m, v) in enumerate(zip(ws, gs, ms, vs)):
        rows, cols = w.shape
        blk = pl.BlockSpec((rows // nb, cols), lambda i: (i, 0))
        if pair[a]:
            in_specs += [blk, pl.BlockSpec((None, rows // nb, cols), lambda i: (0, i, 0)),
                         pl.BlockSpec((3, rows // nb, cols), lambda i: (0, i, 0)), blk, blk]
            args += [w, g[0], g[1], m, v]
        else:
            in_specs += [blk] * 4
            args += [w, g, m, v]
        out_specs += [blk] * 4
        out_shape += [jax.ShapeDtypeStruct(w.shape, F32)] * 4
    outs = _call(body, name="adamw_big", grid=(nb,), in_specs=in_specs, out_specs=out_specs,
                 out_shape=out_shape, args=args)[0]
    return [tuple(outs[4 * a:4 * a + 4]) for a in range(n)]


SMALL_ORDER = ("norm_mix_pre", "norm_mix_post", "norm_mlp_pre", "norm_mlp_post", "b_gate", "conv_w", "conv_b",
               "lru_w_a", "lru_b_a", "lru_w_x", "lru_b_x", "lru_lambda", "pool_w", "pool_scale")
VEC_ROW = dict(norm_mix_pre=0, norm_mix_post=1, norm_mlp_pre=2, norm_mlp_post=3, conv_b=6, lru_b_a=7,
               lru_b_x=8, lru_lambda=9)
ROW_B_GATE, ROW_POOL_SCALE, ROW_CONV_W, ROW_LOSS, N_VEC_ROWS = 4, 10, 11, 15, 16


def _adamw_small(vec_parts, g_pool, g_wa, g_wx, me, params):
    d = vec_parts.shape[2]
    names = SMALL_ORDER
    n = len(names)
    cw_cols = params["conv_w"][0].shape[2]

    def body(me_ref, vec_ref, vecc_ref, gp_ref, gwa_ref, gwx_ref, *refs):
        wmv = refs[:3 * n]
        loss_ref = refs[3 * n]
        outs = refs[3 * n + 1:3 * n + 1 + 4 * n]
        vs, vsc = refs[3 * n + 1 + 4 * n:]
        acc, accc = vec_ref[0], vecc_ref[0]
        for k in range(1, N_DEV):
            acc = acc + vec_ref[k]
            accc = accc + vecc_ref[k]
        vs[...] = acc
        vsc[...] = accc
        loss_ref[...] = vs[ROW_LOSS:ROW_LOSS + 1, 0:128]

        def upd(a, g, idx):
            w_ref, m_ref, v_ref = wmv[3 * a:3 * a + 3]
            g_ref, d_ref, nm_ref, nv_ref = outs[4 * a:4 * a + 4]
            dl, m, v = _adam_math(w_ref[idx], g, m_ref[idx], v_ref[idx])
            g_ref[idx] = g
            d_ref[idx] = dl
            nm_ref[idx] = m
            nv_ref[idx] = v

        for a, name in enumerate(names):
            if name in VEC_ROW:
                r = VEC_ROW[name]
                upd(a, vs[r:r + 1, :], (slice(None), slice(None)))
            elif name == "b_gate":
                for half in range(2):
                    r = ROW_B_GATE + half
                    upd(a, vs[r:r + 1, :], (slice(None), slice(half * d, (half + 1) * d)))
            elif name == "pool_scale":
                width = params[name][0].shape[1]
                upd(a, vs[ROW_POOL_SCALE:ROW_POOL_SCALE + 1, 0:width], (slice(None), slice(None)))
            elif name == "conv_w":
                upd(a, vsc[ROW_CONV_W:ROW_CONV_W + 4, :], (0,))
            elif name == "pool_w":
                upd(a, gp_ref[...], (Ellipsis,))
            elif name == "lru_w_a":
                upd(a, gwa_ref[...], (Ellipsis,))
            elif name == "lru_w_x":
                upd(a, gwx_ref[...], (Ellipsis,))
            else:
                raise ValueError(name)

    def whole(shape):
        nd = len(shape)
        return pl.BlockSpec(tuple(shape), lambda i, me_ref: (0,) * nd)

    in_specs = [
        whole(vec_parts.shape),
        pl.BlockSpec((N_DEV, N_VEC_ROWS, cw_cols), lambda i, me_ref: (0, 0, me_ref[0])),
        whole(g_pool.shape), whole(g_wa.shape), whole(g_wx.shape),
    ]
    args = [vec_parts, vec_parts, g_pool, g_wa, g_wx]
    out_specs = [whole((1, 128))]
    out_shape = [jax.ShapeDtypeStruct((1, 128), F32)]
    for name in names:
        for arr in params[name]:
            in_specs.append(whole(arr.shape))
            args.append(arr)
        shp = params[name][0].shape
        out_specs += [whole(shp)] * 4
        out_shape += [jax.ShapeDtypeStruct(shp, F32)] * 4
    grid_spec = pltpu.PrefetchScalarGridSpec(
        num_scalar_prefetch=1, grid=(1,), in_specs=in_specs, out_specs=out_specs,
        scratch_shapes=[pltpu.VMEM((N_VEC_ROWS, d), F32), pltpu.VMEM((N_VEC_ROWS, cw_cols), F32)])
    outs = pl.pallas_call(
        body, name="adamw_small", grid_spec=grid_spec, out_shape=out_shape,
        compiler_params=pltpu.CompilerParams(
            dimension_semantics=("arbitrary",), vmem_limit_bytes=V7X_VMEM_LIMIT_BYTES),
    )(me, *_in_hbm(args))
    return outs[0], {name: tuple(outs[1 + 4 * a:5 + 4 * a]) for a, name in enumerate(names)}


def _rs_sum(fulls, recvs, shard_ids, slot_ids, name):
    n = len(fulls)

    def body(sh_ref, sl_ref, *refs):
        s = pl.program_id(0)
        for a in range(n):
            full_ref, recv_ref = refs[2 * a], refs[2 * a + 1]
            own_ref, send_ref = refs[2 * n + 2 * a], refs[2 * n + 2 * a + 1]
            v = full_ref[...] + recv_ref[...].astype(F32)

            @pl.when(s == 0)
            def _(own_ref=own_ref, v=v):
                own_ref[...] = v

            @pl.when(s > 0)
            def _(send_ref=send_ref, v=v):
                send_ref[...] = v.astype(send_ref.dtype)

    in_specs, out_specs, out_shape, args = [], [], [], []
    for full, recv in zip(fulls, recvs):
        r, rest = recv.shape[1], tuple(recv.shape[2:])
        zeros = (0,) * len(rest)
        in_specs += [
            pl.BlockSpec((r,) + rest, lambda s, sh, sl, zeros=zeros: (sh[s],) + zeros),
            pl.BlockSpec((None, r) + rest, lambda s, sh, sl, zeros=zeros: (sl[s], 0) + zeros),
        ]
        out_specs += [
            pl.BlockSpec((None, r) + rest, lambda s, sh, sl, zeros=zeros: (0, 0) + zeros),
            pl.BlockSpec((None, r) + rest, lambda s, sh, sl, zeros=zeros: (jnp.maximum(s - 1, 0), 0) + zeros),
        ]
        out_shape += [jax.ShapeDtypeStruct((1, r) + rest, F32), jax.ShapeDtypeStruct((3, r) + rest, recv.dtype)]
        args += [full, recv]
    grid_spec = pltpu.PrefetchScalarGridSpec(
        num_scalar_prefetch=2, grid=(4,), in_specs=in_specs, out_specs=out_specs)
    outs = pl.pallas_call(
        body,
        name=name,
        grid_spec=grid_spec,
        out_shape=out_shape,
        compiler_params=pltpu.CompilerParams(
            dimension_semantics=("arbitrary",), vmem_limit_bytes=V7X_VMEM_LIMIT_BYTES),
    )(shard_ids, slot_ids, *_in_hbm(args))
    return [(outs[2 * a], outs[2 * a + 1]) for a in range(n)]


def _finals(pairs, name, carry=None):
    nb = 4
    n = len(pairs)

    def body(*refs):
        for a in range(n):
            own_ref, recv_ref = refs[2 * a], refs[2 * a + 1]
            acc = own_ref[...]
            for k in range(3):
                acc = acc + recv_ref[k].astype(F32)
            refs[2 * n + a][...] = acc

    in_specs, out_specs, out_shape, args = [], [], [], []
    for own, recv in pairs:
        _, rows, cols = own.shape
        in_specs += [pl.BlockSpec((None, rows // nb, cols), lambda i: (0, i, 0)),
                     pl.BlockSpec((3, rows // nb, cols), lambda i: (0, i, 0))]
        args += [own, recv]
        out_specs.append(pl.BlockSpec((rows // nb, cols), lambda i: (i, 0)))
        out_shape.append(jax.ShapeDtypeStruct((rows, cols), F32))
    return _call(body, name=name, grid=(nb,), in_specs=in_specs, out_specs=out_specs,
                 out_shape=out_shape, args=args, carry=carry)


def _rs_sums(fulls_f32, recv1, tag):
    x, y, c = _place()
    qs = jnp.stack([2 * x + y, 2 * (1 - x) + y, 2 * x + (1 - y), 2 * (1 - x) + (1 - y)]).astype(jnp.int32)
    shard_ids = 2 * qs + c
    return _rs_sum(fulls_f32, recv1, shard_ids, qs, "rs_sum_" + tag)


def _rs_level1(fulls_f32, fulls_send, tag):
    recv1 = _run_plan(_rs_sibling_plan(fulls_send), "rs_sibling_" + tag)
    return _rs_sums(fulls_f32, recv1, tag)


def _rows(g):
    return g.reshape(g.shape[0] * g.shape[1], g.shape[2])


def kernel(x, norm_mix_pre, norm_mix_post, norm_mlp_pre, norm_mlp_post, w_in, b_gate, conv_w, conv_b, lru_w_a, lru_b_a, lru_w_x, lru_b_x, lru_lambda, pool_w, pool_scale, w_lru_up, w_pool_up, w_o, w_ff1, w_ff2, loss_target, m_norm_mix_pre, m_norm_mix_post, m_norm_mlp_pre, m_norm_mlp_post, m_w_in, m_b_gate, m_conv_w, m_conv_b, m_lru_w_a, m_lru_b_a, m_lru_w_x, m_lru_b_x, m_lru_lambda, m_pool_w, m_pool_scale, m_w_lru_up, m_w_pool_up, m_w_o, m_w_ff1, m_w_ff2, v_norm_mix_pre, v_norm_mix_post, v_norm_mlp_pre, v_norm_mlp_post, v_w_in, v_b_gate, v_conv_w, v_conv_b, v_lru_w_a, v_lru_b_a, v_lru_w_x, v_lru_b_x, v_lru_lambda, v_pool_w, v_pool_scale, v_w_lru_up, v_w_pool_up, v_w_o, v_w_ff1, v_w_ff2):
    t, d = x.shape[1], x.shape[2]
    d_rnn = conv_b.shape[1]
    d_pool = pool_scale.shape[1]
    per = LRU_CB // LRU_HEAD_DIM
    xi, yi, ci = _place()
    me = 4 * xi + 2 * yi + ci

    x2d = x[0]
    tgt = loss_target[0]

    s_in = w_in[0].T.astype(BF16)
    s_lu = w_lru_up[0].astype(BF16)
    s_pu = w_pool_up[0].T.astype(BF16)
    s_o = w_o[0].astype(BF16)
    s_f1 = w_ff1[0].T.astype(BF16)
    s_f2 = w_ff2[0].astype(BF16)
    s_cw = jnp.pad(conv_w[0], ((0, 4), (0, 0)))

    g_in, g_cw = _run_plan(_ag_plan([s_in, s_cw]), "ag_w_in")
    w_int = _rows(g_in)
    conv_w_full = jnp.transpose(g_cw[:, :4, :], (1, 0, 2)).reshape(4, d_rnn)

    wa_bd, wx_bd = lru_w_a[0], lru_w_x[0]
    pw = pool_w[0]
    pw_bf = pw.astype(BF16)

    pool_block = (2 * d_rnn) // d_pool
    ga_block = (2 * d_rnn + d_pool) // 512
    gb_block = ga_block + d // 512
    g_block = d_rnn // 512

    r_f1, r_f2 = s_f1.shape[0], s_f2.shape[0]
    f1_cut = r_f1 // 4
    f2_cut = (3 * r_f2) // 8
    plan = _join([_ag_plan([s_lu, s_pu, s_o]), _ag_plan([s_f1], pieces=[(0, f1_cut)])])
    (proj, h1), got = _norm_proj(x2d, norm_mix_pre, w_int, carry=plan)
    (g_lu, g_pu, g_o), (g_f1,) = plan.split(got)
    w_lu, w_put, w_og = _rows(g_lu), _rows(g_pu), _rows(g_o)
    (y_lru, h, xc), (g_f1,) = _lru_fwd(
        proj, conv_w_full, conv_b, wa_bd, lru_b_a, wx_bd, lru_b_x, lru_lambda,
        carry=_ag_plan([s_f1], pieces=[(f1_cut, r_f1 - f1_cut)], bufs=[g_f1]))
    w_f1t = _rows(g_f1)
    y_pool, p = _pool_fwd(proj, pw_bf, pool_scale, pool_block)
    (br_a, br_b, mix), (g_f2,) = _branch_mix(
        y_lru, y_pool, w_lu, w_put, proj, b_gate, ga_block, gb_block,
        carry=_ag_plan([s_f2], pieces=[(0, f2_cut)]))
    (m, x2, h3), _ = _wo_norm(mix, w_og, x2d, norm_mix_post, norm_mlp_pre)
    (rf,), (g_f2,) = _ff1(
        h3, w_f1t, carry=_ag_plan([s_f2], pieces=[(f2_cut, r_f2 - f2_cut)], bufs=[g_f2]))
    w_f2 = _rows(g_f2)
    dy, df, dg4, loss_part = _ff2_loss(rf, w_f2, x2, norm_mlp_post, tgt)

    (gw_ff2_32, gw_ff2_16), _ = _wgrad(rf, df, "wgrad_ff2", square_a=True)
    (d_f1,), r1_ff2 = _ff2_bwd(df, w_f2, rf, carry=_rs_sibling_plan([gw_ff2_16]))
    ((own_ff2, send_ff2),) = _rs_sums([gw_ff2_32], r1_ff2, "ff2")
    cut2 = (5 * send_ff2.shape[1]) // 16
    (gw_ff1_32, gw_ff1_16), (r2_ff2,) = _wgrad(
        d_f1, h3, "wgrad_ff1", carry=_rs_chips_plan([send_ff2], pieces=[(0, cut2)]))
    plan = _join([_rs_chips_plan([send_ff2], pieces=[(cut2, send_ff2.shape[1] - cut2)], bufs=[r2_ff2]),
                  _rs_sibling_plan([gw_ff1_16])])
    (dx2, dm, dg3, dg2), got = _ff1_bwd_norms(d_f1, w_f1t, dy, x2, norm_mlp_pre, m, norm_mix_post, carry=plan)
    (r2_ff2,), r1_ff1 = plan.split(got)
    ((own_ff1, send_ff1),) = _rs_sums([gw_ff1_32], r1_ff1, "ff1")
    cut = send_ff1.shape[1] // 4
    (gw_o_32, gw_o_16), _ = _wgrad(mix, dm, "wgrad_o")
    (d_br_a, d_br_b, p_ga, p_gb, dbg_a, dbg_b), (r2_ff1,) = _wo_bwd_mix(
        dm, w_og, br_a, br_b, proj, b_gate, ga_block, gb_block,
        carry=_rs_chips_plan([send_ff1], pieces=[(0, cut)]))
    (gw_lu_32, gw_lu_16), _ = _wgrad(y_lru, d_br_a, "wgrad_lru_up")
    (gw_pu_32, gw_pu_16), _ = _wgrad(d_br_b, y_pool, "wgrad_pool_up")
    (dh, p_g), r1_mid = _lru_up_bwd(
        d_br_a, w_lu, proj, h, g_block,
        carry=_rs_sibling_plan([gw_o_16, gw_lu_16, gw_pu_16.reshape(-1, d)]))
    mid = _rs_sums([gw_o_32, gw_lu_32, gw_pu_32.reshape(-1, d)], r1_mid, "mid")
    (p_x, dwa, db_a, dwx, db_x, dlam, dconv_w, dconv_b), (r2_ff1,) = _lru_bwd(
        dh, xc, h, proj, conv_w_full, wa_bd, lru_b_a, wx_bd, lru_b_x, lru_lambda,
        carry=_rs_chips_plan([send_ff1], pieces=[(cut, send_ff1.shape[1] - cut)], bufs=[r2_ff1]))
    p_p, dpool_w, dpool_scale = _pool_bwd(d_br_b, w_put, p, pw, pool_scale)
    parts = [p_x, p_g, p_p, p_ga, p_gb]
    gw_in, r2_mid = _wgrad_parts(parts, h1, "wgrad_in", carry=_rs_chips_plan([s for _, s in mid]))
    tail = _rs_level1([gw_in[0], dpool_w.reshape(N_DEV, -1, POOL_GROUP_DIM), dwa, dwx],
                      [gw_in[1], dpool_w.reshape(N_DEV, -1, POOL_GROUP_DIM), dwa, dwx], "in")
    (grad_x, dg1), r2_tail = _win_bwd_norm(parts, w_int, dx2, x2d, norm_mix_pre,
                                           carry=_rs_chips_plan([s for _, s in tail]))

    def flat2(a):
        return a.reshape(a.shape[0], -1, a.shape[-1])

    fin_small, _ = _finals([
        (flat2(tail[1][0]), flat2(r2_tail[1])), (flat2(tail[2][0]), flat2(r2_tail[2])),
        (flat2(tail[3][0]), flat2(r2_tail[3])),
    ], "rs_finals_small")

    def pad_row(a):
        return jnp.pad(a, ((0, 0), (0, d - a.shape[1])))

    vecs = jnp.concatenate([dg1, dg2, dg3, dg4, dbg_a, dbg_b, dconv_b, db_a, db_x, dlam,
                            pad_row(dpool_scale), dconv_w, pad_row(loss_part)], axis=0)
    assert vecs.shape[0] == N_VEC_ROWS
    fin, (vec_parts, g_pool, g_wa, g_wx) = _finals(
        [(mid[2][0], r2_mid[2]), (own_ff1, r2_ff1)], "rs_finals", carry=_ag_plan([vecs] + fin_small))
    g_w_pool_up = fin[0].reshape(d // N_DEV, d_pool).T
    g_w_ff1 = fin[1].T

    big_names = ["w_in", "w_lru_up", "w_pool_up", "w_o", "w_ff1", "w_ff2"]
    big_w = [w_in[0].T, w_lru_up[0], w_pool_up[0], w_o[0], w_ff1[0], w_ff2[0]]
    big_g = [(tail[0][0], r2_tail[0]), (mid[1][0], r2_mid[1]), g_w_pool_up, (mid[0][0], r2_mid[0]),
             g_w_ff1, (own_ff2, r2_ff2)]
    big_m = [m_w_in[0].T, m_w_lru_up[0], m_w_pool_up[0], m_w_o[0], m_w_ff1[0], m_w_ff2[0]]
    big_v = [v_w_in[0].T, v_w_lru_up[0], v_w_pool_up[0], v_w_o[0], v_w_ff1[0], v_w_ff2[0]]
    big_out = _adamw_big(big_w, big_g, big_m, big_v)
    big_out[0] = tuple(o.T for o in big_out[0])

    small = dict(
        norm_mix_pre=(norm_mix_pre, m_norm_mix_pre, v_norm_mix_pre),
        norm_mix_post=(norm_mix_post, m_norm_mix_post, v_norm_mix_post),
        norm_mlp_pre=(norm_mlp_pre, m_norm_mlp_pre, v_norm_mlp_pre),
        norm_mlp_post=(norm_mlp_post, m_norm_mlp_post, v_norm_mlp_post),
        b_gate=(b_gate, m_b_gate, v_b_gate), conv_w=(conv_w, m_conv_w, v_conv_w),
        conv_b=(conv_b, m_conv_b, v_conv_b), lru_w_a=(lru_w_a, m_lru_w_a, v_lru_w_a),
        lru_b_a=(lru_b_a, m_lru_b_a, v_lru_b_a), lru_w_x=(lru_w_x, m_lru_w_x, v_lru_w_x),
        lru_b_x=(lru_b_x, m_lru_b_x, v_lru_b_x), lru_lambda=(lru_lambda, m_lru_lambda, v_lru_lambda),
        pool_w=(pool_w, m_pool_w, v_pool_w), pool_scale=(pool_scale, m_pool_scale, v_pool_scale))
    loss_row, small_out = _adamw_small(
        vec_parts, g_pool.reshape(pool_w.shape), g_wa.reshape(lru_w_a.shape), g_wx.reshape(lru_w_x.shape),
        jnp.reshape(me, (1,)).astype(jnp.int32), small)
    grads = {n: o[0] for n, o in small_out.items()}
    delta = {n: o[1] for n, o in small_out.items()}
    new_m = {n: o[2] for n, o in small_out.items()}
    new_v = {n: o[3] for n, o in small_out.items()}

    for name, (g, dl, nm, nv) in zip(big_names, big_out):
        grads[name], delta[name], new_m[name], new_v[name] = g[None], dl[None], nm[None], nv[None]

    loss = loss_row[0, 0]
    order = ["norm_mix_pre", "norm_mix_post", "norm_mlp_pre", "norm_mlp_post", "w_in", "b_gate", "conv_w",
             "conv_b", "lru_w_a", "lru_b_a", "lru_w_x", "lru_b_x", "lru_lambda", "pool_w", "pool_scale",
             "w_lru_up", "w_pool_up", "w_o", "w_ff1", "w_ff2"]
    return (loss, grad_x[None], *[grads[n] for n in order], *[delta[n] for n in order],
            *[new_m[n] for n in order], *[new_v[n] for n in order])
```

```python
import functools
import math
import operator
import types

import jax
import jax.numpy as jnp
from jax import lax
from jax.experimental import pallas as pl
from jax.experimental.pallas import tpu as pltpu

F32 = jnp.float32
BF16 = jnp.bfloat16
NORM_EPS = 1e-6
LRU_C = 8.0
N_LRU_HEADS = 16
LRU_HEAD_DIM = 64
POOL_WINDOWS = (2, 4, 8, 16)
POOL_GROUP_DIM = 128
ADAM_LR = 0.001
ADAM_B1 = 0.9
ADAM_B2 = 0.999
ADAM_EPS = 1e-08
ADAM_WD = 0.01
ADAM_STEP = 10
N_DEV = 8
V7X_VMEM_LIMIT_BYTES = 56 * 1024 * 1024
LRU_CB = 256
MESH = pl.DeviceIdType.MESH
ANY = pl.BlockSpec(memory_space=pl.ANY)


def _tile(n, pref):
    t = min(n, pref)
    assert n % t == 0, (n, pref)
    return t


def _dot_nn(a, b):
    return lax.dot_general(a, b, (((1,), (0,)), ((), ())), preferred_element_type=F32)


def _dot_nt(a, b):
    return lax.dot_general(a, b, (((1,), (1,)), ((), ())), preferred_element_type=F32)


def _dot_tn(a, b):
    return lax.dot_general(a, b, (((0,), (0,)), ((), ())), preferred_element_type=F32)


def _row_chunks(n_rows, fn, chunk=256):
    chunk = min(chunk, n_rows)
    assert n_rows % chunk == 0

    def step(r, carry):
        fn(pl.ds(pl.multiple_of(r * chunk, chunk), chunk))
        return carry

    lax.fori_loop(0, n_rows // chunk, step, 0)


def _sig(x):
    return 1.0 / (1.0 + jnp.exp(-x))


def _rms_hat(x):
    r = lax.rsqrt(jnp.mean(x * x, axis=-1, keepdims=True) + NORM_EPS)
    return x * r, r


def _rms_bwd(dn, xhat, r, g):
    q = dn * g
    dx = r * (q - xhat * jnp.mean(q * xhat, axis=-1, keepdims=True))
    dg = jnp.sum(dn * xhat, axis=0, keepdims=True)
    return dx, dg


_GELU_K = math.sqrt(2.0 / math.pi)
_GELU_C = 0.044715


def _gelu_and_grad(g):
    t = jnp.tanh(_GELU_K * (g + _GELU_C * g * g * g))
    val = 0.5 * g * (1.0 + t)
    grad = 0.5 * (1.0 + t) + 0.5 * g * (1.0 - t * t) * (_GELU_K * (1.0 + 3.0 * _GELU_C * g * g))
    return val, grad


def _softplus_neg(lam):
    z = -lam
    e = jnp.exp(-jnp.abs(z))
    u = 1.0 + e
    d = u - 1.0
    l1p = jnp.where(d == 0.0, e, jnp.log(u) * (e / jnp.where(d == 0.0, 1.0, d)))
    return jnp.maximum(z, 0.0) + l1p


def _lru_gates(xc, wa, ba, wx, bx, lam):
    xcb = xc.astype(BF16)
    r = _sig(_dot_nn(xcb, wa) + ba)
    i = _sig(_dot_nn(xcb, wx) + bx)
    sp = _softplus_neg(lam)
    log_a = (-LRU_C) * r * sp
    a = jnp.exp(log_a)
    mult = jnp.sqrt(-jnp.tanh(log_a) * (1.0 + a * a))
    return xcb, r, i, sp, log_a, a, mult


def _place():
    return lax.axis_index("x"), lax.axis_index("y"), lax.axis_index("c")


def _ag_plan(shards, pieces=None, bufs=None):
    na = len(shards)
    n_kinds = 7

    def parts(ins, outs, sems):
        send_sems, recv_sems, local_sems = sems
        x, y, c = _place()
        me, sibling = (x, y, c), (x, y, 1 - c)
        x_nb, y_nb, diag = (1 - x, y), (x, 1 - y), (1 - x, 1 - y)
        relay_src = (c * (1 - x) + (1 - c) * x, c * y + (1 - c) * (1 - y))
        relay_dst = (c * x + (1 - c) * (1 - x), c * (1 - y) + (1 - c) * y)

        def own(a):
            return ins[a] if pieces is None else ins[a].at[pl.ds(*pieces[a])]

        def slot(a, px, py, pc):
            idx = 4 * px + 2 * py + pc
            return outs[a].at[idx] if pieces is None else outs[a].at[idx, pl.ds(*pieces[a])]

        def copy(a, k, block, to, src=None):
            return pltpu.make_async_remote_copy(
                src_ref=slot(a, *block) if src is None else src,
                dst_ref=slot(a, *block),
                send_sem=send_sems.at[a * n_kinds + k],
                recv_sem=recv_sems.at[a * n_kinds + k],
                device_id=to,
                device_id_type=MESH,
            )

        mine = [pltpu.make_async_copy(own(a), slot(a, *me), local_sems.at[a]) for a in range(na)]
        first, second, third = [], [], []
        for a in range(na):
            first += [copy(a, 0, me, sibling, src=own(a)), copy(a, 1, me, (*x_nb, c), src=own(a)),
                      copy(a, 2, me, (*y_nb, c), src=own(a))]
            second += [copy(a, 3, (*relay_src, c), (*relay_dst, c)), copy(a, 4, (*x_nb, c), sibling),
                       copy(a, 5, (*y_nb, c), sibling)]
            third.append(copy(a, 6, (*diag, c), sibling))
        return sibling, c, x_nb, y_nb, diag, copy, mine, first, second, third

    def start(ins, outs, sems):
        _, _, _, _, _, _, mine, first, _, _ = parts(ins, outs, sems)
        for cp in mine + first:
            cp.start()

    def middle(ins, outs, sems):
        _, c, x_nb, y_nb, _, copy, _, _, second, _ = parts(ins, outs, sems)
        for a in range(na):
            copy(a, 1, (*x_nb, c), (*x_nb, c)).wait_recv()
            copy(a, 2, (*y_nb, c), (*y_nb, c)).wait_recv()
        for cp in second:
            cp.start()

    def finish(ins, outs, sems):
        sibling, c, x_nb, y_nb, diag, copy, mine, first, second, third = parts(ins, outs, sems)
        for a in range(na):
            copy(a, 3, (*diag, c), (*diag, c)).wait_recv()
            third[a].start()
        for a in range(na):
            copy(a, 0, sibling, sibling).wait_recv()
            copy(a, 4, (*x_nb, 1 - c), sibling).wait_recv()
            copy(a, 5, (*y_nb, 1 - c), sibling).wait_recv()
            copy(a, 6, (*diag, 1 - c), sibling).wait_recv()
        for cp in first + second + third:
            cp.wait_send()
        for cp in mine:
            cp.wait()

    return types.SimpleNamespace(
        ins=list(shards) + list(bufs or []),
        out_shapes=[jax.ShapeDtypeStruct((N_DEV,) + s.shape, s.dtype) for s in shards],
        sems=[pltpu.SemaphoreType.DMA((n_kinds * na,)), pltpu.SemaphoreType.DMA((n_kinds * na,)),
              pltpu.SemaphoreType.DMA((na,))],
        aliases=[(na + a, a) for a in range(na)] if bufs else [],
        peers=frozenset({"sibling", "neighbours"}), start=start, middle=middle, finish=finish)


def _ag_direct_plan(shards):
    na = len(shards)

    def parts(ins, outs, sems):
        send_sems, recv_sems, local_sems = sems
        x, y, c = _place()
        me = 4 * x + 2 * y + c
        mine = [pltpu.make_async_copy(ins[a], outs[a].at[me], local_sems.at[a]) for a in range(na)]
        sends, recvs = [], []
        for k in range(1, N_DEV):
            fx, fy, fc = (k >> 2) & 1, (k >> 1) & 1, k & 1
            px, py, pc = (1 - x) if fx else x, (1 - y) if fy else y, (1 - c) if fc else c
            for a in range(na):
                sems_k = dict(send_sem=send_sems.at[a * N_DEV + k], recv_sem=recv_sems.at[a * N_DEV + k],
                              device_id=(px, py, pc), device_id_type=MESH)
                sends.append(pltpu.make_async_remote_copy(src_ref=ins[a], dst_ref=outs[a].at[me], **sems_k))
                recvs.append(pltpu.make_async_remote_copy(
                    src_ref=ins[a], dst_ref=outs[a].at[4 * px + 2 * py + pc], **sems_k))
        return mine, sends, recvs

    def start(ins, outs, sems):
        mine, sends, _ = parts(ins, outs, sems)
        for cp in mine + sends:
            cp.start()

    def finish(ins, outs, sems):
        mine, sends, recvs = parts(ins, outs, sems)
        for cp in recvs:
            cp.wait_recv()
        for cp in sends:
            cp.wait_send()
        for cp in mine:
            cp.wait()

    return types.SimpleNamespace(
        ins=list(shards),
        out_shapes=[jax.ShapeDtypeStruct((N_DEV,) + s.shape, s.dtype) for s in shards],
        sems=[pltpu.SemaphoreType.DMA((N_DEV * na,)), pltpu.SemaphoreType.DMA((N_DEV * na,)),
              pltpu.SemaphoreType.DMA((na,))],
        aliases=[], peers=frozenset({"all"}), start=start, finish=finish)


def _rs_sibling_plan(fulls):
    na = len(fulls)
    rs = [f.shape[0] // N_DEV for f in fulls]

    def copies(ins, outs, sems):
        send_sems, recv_sems = sems
        x, y, c = _place()
        out = []
        for a in range(na):
            for q in range(4):
                shard = 2 * q + (1 - c)
                out.append(pltpu.make_async_remote_copy(
                    src_ref=ins[a].at[pl.ds(shard * rs[a], rs[a])],
                    dst_ref=outs[a].at[q],
                    send_sem=send_sems.at[a * 4 + q],
                    recv_sem=recv_sems.at[a * 4 + q],
                    device_id=(x, y, 1 - c),
                    device_id_type=MESH,
                ))
        return out

    def start(ins, outs, sems):
        for cp in copies(ins, outs, sems):
            cp.start()

    def finish(ins, outs, sems):
        for cp in copies(ins, outs, sems):
            cp.wait()

    return types.SimpleNamespace(
        ins=list(fulls),
        out_shapes=[jax.ShapeDtypeStruct((4, r) + f.shape[1:], f.dtype) for r, f in zip(rs, fulls)],
        sems=[pltpu.SemaphoreType.DMA((4 * na,)), pltpu.SemaphoreType.DMA((4 * na,))],
        peers=frozenset({"sibling"}), start=start, finish=finish)


def _rs_chips_plan(sends, pieces=None, bufs=None):
    na = len(sends)

    def copies(ins, outs, sems):
        send_sems, recv_sems = sems
        x, y, c = _place()
        chips = [(1 - x, y), (x, 1 - y), (1 - x, 1 - y)]
        out = []
        for a in range(na):
            for k, chip in enumerate(chips):
                rows = (k,) if pieces is None else (k, pl.ds(*pieces[a]))
                out.append(pltpu.make_async_remote_copy(
                    src_ref=ins[a].at[rows],
                    dst_ref=outs[a].at[rows],
                    send_sem=send_sems.at[a * 3 + k],
                    recv_sem=recv_sems.at[a * 3 + k],
                    device_id=(*chip, c),
                    device_id_type=MESH,
                ))
        return out

    def start(ins, outs, sems):
        for cp in copies(ins, outs, sems):
            cp.start()

    def finish(ins, outs, sems):
        for cp in copies(ins, outs, sems):
            cp.wait()

    return types.SimpleNamespace(
        ins=list(sends) + list(bufs or []),
        out_shapes=[jax.ShapeDtypeStruct(s.shape, s.dtype) for s in sends],
        sems=[pltpu.SemaphoreType.DMA((3 * na,)), pltpu.SemaphoreType.DMA((3 * na,))],
        aliases=[(na + a, a) for a in range(na)] if bufs else [],
        peers=frozenset({"chips"}), start=start, finish=finish)


def _join(plans):
    ins, outs, sems, aliases, offs = [], [], [], [], []
    for p in plans:
        offs.append((len(ins), len(outs), len(sems)))
        aliases += [(len(ins) + ci, len(outs) + co) for ci, co in getattr(p, "aliases", [])]
        ins += p.ins
        outs += p.out_shapes
        sems += p.sems

    def cut(p, off, i, o, s):
        return (i[off[0]:off[0] + len(p.ins)], o[off[1]:off[1] + len(p.out_shapes)],
                s[off[2]:off[2] + len(p.sems)])

    def start(i, o, s):
        for p, off in zip(plans, offs):
            p.start(*cut(p, off, i, o, s))

    def middle(i, o, s):
        for p, off in zip(plans, offs):
            if getattr(p, "middle", None) is not None:
                p.middle(*cut(p, off, i, o, s))

    def finish(i, o, s):
        for p, off in zip(plans, offs):
            p.finish(*cut(p, off, i, o, s))

    def split(results):
        return [list(results[off[1]:off[1] + len(p.out_shapes)]) for p, off in zip(plans, offs)]

    return types.SimpleNamespace(ins=ins, out_shapes=outs, sems=sems, aliases=aliases,
                                 peers=frozenset().union(*[p.peers for p in plans]),
                                 start=start, middle=middle, finish=finish, split=split)


COLLECTIVE_ID = {frozenset({"sibling"}): 0, frozenset({"chips"}): 1, frozenset({"sibling", "chips"}): 2,
                 frozenset({"sibling", "neighbours"}): 3, frozenset({"all"}): 4}


def _handshake(peers):
    x, y, c = _place()
    devs = []
    if "all" in peers:
        assert peers == frozenset({"all"})
        devs = [(px, py, pc) for px in (x, 1 - x) for py in (y, 1 - y) for pc in (c, 1 - c)][1:]
    if "sibling" in peers:
        devs.append((x, y, 1 - c))
    if "neighbours" in peers:
        devs += [(1 - x, y, c), (x, 1 - y, c)]
    if "chips" in peers:
        assert "neighbours" not in peers
        devs += [(1 - x, y, c), (x, 1 - y, c), (1 - x, 1 - y, c)]
    barrier = pltpu.get_barrier_semaphore()
    for dev in devs:
        pl.semaphore_signal(barrier, inc=1, device_id=dev, device_id_type=MESH)
    pl.semaphore_wait(barrier, len(devs))


def _in_hbm(args):
    return [pltpu.with_memory_space_constraint(a, pltpu.HBM) for a in args]


def _run_plan(plan, name):
    n_in, n_out = len(plan.ins), len(plan.out_shapes)

    def body(*refs):
        ins, outs, sems = refs[:n_in], refs[n_in:n_in + n_out], refs[n_in + n_out:]
        _handshake(plan.peers)
        plan.start(ins, outs, sems)
        if getattr(plan, "middle", None) is not None:
            plan.middle(ins, outs, sems)
        plan.finish(ins, outs, sems)

    return pl.pallas_call(
        body,
        name=name,
        in_specs=[ANY] * n_in,
        out_specs=[ANY] * n_out,
        out_shape=plan.out_shapes,
        scratch_shapes=plan.sems,
        input_output_aliases=dict(getattr(plan, "aliases", [])),
        compiler_params=pltpu.CompilerParams(collective_id=COLLECTIVE_ID[plan.peers]),
    )(*_in_hbm(plan.ins))


def _call(body, *, name, grid, in_specs, out_specs, out_shape, args, scratch_shapes=(), aliases=None,
          carry=None):
    n_in, n_out, n_scr = len(in_specs), len(out_shape), len(scratch_shapes)
    params = pltpu.CompilerParams(
        dimension_semantics=("arbitrary",) * len(grid), vmem_limit_bytes=V7X_VMEM_LIMIT_BYTES)
    if carry is None:
        outs = pl.pallas_call(
            body, name=name, grid=grid, in_specs=list(in_specs), out_specs=list(out_specs),
            out_shape=list(out_shape), scratch_shapes=list(scratch_shapes),
            input_output_aliases=aliases or {}, compiler_params=params)(*_in_hbm(args))
        return list(outs), []
    c_in, c_out = len(carry.ins), len(carry.out_shapes)

    def full(*refs):
        p = 0
        ins = refs[p:p + n_in]
        p += n_in
        cins = refs[p:p + c_in]
        p += c_in
        outs = refs[p:p + n_out]
        p += n_out
        couts = refs[p:p + c_out]
        p += c_out
        scr = refs[p:p + n_scr]
        csems = refs[p + n_scr:]
        ids = [pl.program_id(a) for a in range(len(grid))]
        first = functools.reduce(operator.and_, [i == 0 for i in ids])
        last = functools.reduce(operator.and_, [i == g - 1 for i, g in zip(ids, grid)])

        @pl.when(first)
        def _():
            _handshake(carry.peers)
            carry.start(cins, couts, csems)

        if getattr(carry, "middle", None) is not None:
            n_steps = math.prod(grid)
            flat = functools.reduce(lambda acc, ig: acc * ig[1] + ig[0], zip(ids, grid), 0)

            @pl.when(flat == (2 * n_steps) // 3)
            def _():
                carry.middle(cins, couts, csems)

        body(*ins, *outs, *scr)

        @pl.when(last)
        def _():
            carry.finish(cins, couts, csems)

    all_aliases = dict(aliases or {})
    all_aliases.update({n_in + ci: n_out + co for ci, co in getattr(carry, "aliases", [])})
    params = pltpu.CompilerParams(
        dimension_semantics=("arbitrary",) * len(grid), vmem_limit_bytes=V7X_VMEM_LIMIT_BYTES,
        collective_id=COLLECTIVE_ID[carry.peers])
    outs = pl.pallas_call(
        full, name=name, grid=grid,
        in_specs=list(in_specs) + [ANY] * c_in,
        out_specs=list(out_specs) + [ANY] * c_out,
        out_shape=list(out_shape) + list(carry.out_shapes),
        scratch_shapes=list(scratch_shapes) + list(carry.sems),
        input_output_aliases=all_aliases, compiler_params=params)(*_in_hbm(args), *_in_hbm(carry.ins))
    return list(outs[:n_out]), list(outs[n_out:])


def _norm_proj(x, g1, w_int, carry=None):
    t, d = x.shape
    n = w_int.shape[0]
    tt, tn = _tile(t, 2048), _tile(n, 512)

    def body(x_ref, g_ref, w_ref, proj_ref, h1_ref, h1_s):
        @pl.when(pl.program_id(1) == 0)
        def _():
            def norm_rows(rows):
                xhat, _ = _rms_hat(x_ref[rows, :])
                h = (xhat * g_ref[...]).astype(BF16)
                h1_s[rows, :] = h
                h1_ref[rows, :] = h

            _row_chunks(tt, norm_rows)

        proj_ref[...] = _dot_nt(h1_s[...], w_ref[...]).astype(BF16)

    return _call(
        body, name="norm_proj", grid=(t // tt, n // tn),
        in_specs=[
            pl.BlockSpec((tt, d), lambda i, j: (i, 0)),
            pl.BlockSpec((1, d), lambda i, j: (0, 0)),
            pl.BlockSpec((tn, d), lambda i, j: (j, 0)),
        ],
        out_specs=[
            pl.BlockSpec((tt, tn), lambda i, j: (i, j)),
            pl.BlockSpec((tt, d), lambda i, j: (i, 0)),
        ],
        out_shape=[jax.ShapeDtypeStruct((t, n), BF16), jax.ShapeDtypeStruct((t, d), BF16)],
        scratch_shapes=[pltpu.VMEM((tt, d), BF16)],
        args=(x, g1, w_int), carry=carry)


def _scan_rows(av, bv, reverse):
    tc = av.shape[0]
    row = lax.broadcasted_iota(jnp.int32, av.shape, 0)
    s = 1
    while s < tc:
        if s < 8:
            keep = (row < tc - s) if reverse else (row >= s)
            shift = (tc - s) if reverse else s
            a_sh = jnp.where(keep, pltpu.roll(av, shift, 0), 1.0)
            b_sh = jnp.where(keep, pltpu.roll(bv, shift, 0), 0.0)
            bv = av * b_sh + bv
            av = av * a_sh
        elif reverse:
            bv = jnp.concatenate([av[:tc - s] * bv[s:] + bv[:tc - s], bv[tc - s:]], axis=0)
            av = jnp.concatenate([av[:tc - s] * av[s:], av[tc - s:]], axis=0)
        else:
            bv = jnp.concatenate([bv[:s], av[s:] * bv[:tc - s] + bv[s:]], axis=0)
            av = jnp.concatenate([av[:s], av[s:] * av[:tc - s]], axis=0)
        s *= 2
    return av, bv


def _fill_block_diag(w_ref, bd_ref):
    bd_ref[...] = jnp.zeros_like(bd_ref)
    hd = LRU_HEAD_DIM
    for k in range(w_ref.shape[0]):
        bd_ref[k * hd:(k + 1) * hd, k * hd:(k + 1) * hd] = w_ref[k].astype(BF16)


def _lru_fwd(proj, conv_w, conv_b, w_a, b_a, w_x, b_x, lam, carry=None):
    t = proj.shape[0]
    dr = conv_b.shape[1]
    cb = LRU_CB
    tc = _tile(t, 256)
    ncb, ntc = dr // cb, t // tc

    def body(xp_ref, g_ref, cw_ref, cb_ref, wa_ref, ba_ref, wx_ref, bx_ref, lam_ref,
             y_ref, h_ref, xc_ref, prevx_s, hlast_s, wa_s, wx_s):
        c = pl.program_id(1)

        @pl.when(c == 0)
        def _():
            prevx_s[...] = jnp.zeros_like(prevx_s)
            hlast_s[...] = jnp.zeros_like(hlast_s)
            _fill_block_diag(wa_ref, wa_s)
            _fill_block_diag(wx_ref, wx_s)

        x = xp_ref[...].astype(F32)
        prev = prevx_s[...]
        row = lax.broadcasted_iota(jnp.int32, x.shape, 0)

        def sh(j):
            return jnp.where(row >= j, pltpu.roll(x, j, 0), pltpu.roll(prev, j, 0))

        xc = (cb_ref[...] + cw_ref[0:1, :] * sh(3) + cw_ref[1:2, :] * sh(2)
              + cw_ref[2:3, :] * sh(1) + cw_ref[3:4, :] * x)
        prevx_s[...] = x
        xc_ref[...] = xc
        _, _, i, _, _, a, mult = _lru_gates(xc, wa_s[...], ba_ref[...], wx_s[...], bx_ref[...],
                                            lam_ref[...])
        av, bv = _scan_rows(a, mult * (i * xc), reverse=False)
        h = av * hlast_s[...] + bv
        h_ref[...] = h
        hlast_s[...] = h_ref[tc - 1:tc, :]
        gel, _ = _gelu_and_grad(g_ref[...].astype(F32))
        y_ref[...] = (h * gel).astype(BF16)

    vec = pl.BlockSpec((1, cb), lambda j, c: (0, j))
    blk = pl.BlockSpec((tc, cb), lambda j, c: (c, j))
    mat = pl.BlockSpec((cb // LRU_HEAD_DIM, LRU_HEAD_DIM, LRU_HEAD_DIM), lambda j, c: (j, 0, 0))
    return _call(
        body, name="lru_fwd", grid=(ncb, ntc),
        in_specs=[
            blk,
            pl.BlockSpec((tc, cb), lambda j, c: (c, ncb + j)),
            pl.BlockSpec((4, cb), lambda j, c: (0, j)),
            vec, mat, vec, mat, vec, vec,
        ],
        out_specs=[blk, blk, blk],
        out_shape=[
            jax.ShapeDtypeStruct((t, dr), BF16),
            jax.ShapeDtypeStruct((t, dr), F32),
            jax.ShapeDtypeStruct((t, dr), F32),
        ],
        scratch_shapes=[pltpu.VMEM((tc, cb), F32), pltpu.VMEM((1, cb), F32),
                        pltpu.VMEM((cb, cb), BF16), pltpu.VMEM((cb, cb), BF16)],
        args=(proj, proj, conv_w, conv_b, w_a, b_a, w_x, b_x, lam), carry=carry)


def _pool_select(col, vals):
    out = vals[3]
    for g in (2, 1, 0):
        out = jnp.where(col < (g + 1) * POOL_GROUP_DIM, vals[g], out)
    return out


def _pool_fwd(proj, pool_w, pool_scale, col_block):
    t = proj.shape[0]
    dp = pool_scale.shape[1]
    tc = _tile(t, 256)
    ntc = t // tc

    def body(x_ref, w_ref, sc_ref, y_ref, p_ref, px, p2, p4, p8):
        c = pl.program_id(0)

        @pl.when(c == 0)
        def _():
            for s in (px, p2, p4, p8):
                s[...] = jnp.zeros_like(s)

        x = x_ref[...].astype(F32)
        row = lax.broadcasted_iota(jnp.int32, x.shape, 0)
        col = lax.broadcasted_iota(jnp.int32, x.shape, 1)

        def sh(v, pv, j):
            return jnp.where(row >= j, pltpu.roll(v, j, 0), pltpu.roll(pv[...], j, 0))

        s2 = x + sh(x, px, 1)
        s4 = s2 + sh(s2, p2, 2)
        s8 = s4 + sh(s4, p4, 4)
        s16 = s8 + sh(s8, p8, 8)
        px[...] = x
        p2[...] = s2
        p4[...] = s4
        p8[...] = s8
        wsum = _pool_select(col, (s2, s4, s8, s16))
        win = _pool_select(col, POOL_WINDOWS)
        cnt = jnp.minimum(c * tc + row + 1, win).astype(F32)
        p = wsum / cnt - x
        pb = p.astype(BF16)
        p_ref[...] = pb
        for g in range(len(POOL_WINDOWS)):
            sl = slice(g * POOL_GROUP_DIM, (g + 1) * POOL_GROUP_DIM)
            yg = _dot_nn(pb[:, sl], w_ref[g]) * sc_ref[:, sl]
            y_ref[:, sl] = yg.astype(BF16)

    return _call(
        body, name="pool_fwd", grid=(ntc,),
        in_specs=[
            pl.BlockSpec((tc, dp), lambda c: (c, col_block)),
            pl.BlockSpec(pool_w.shape, lambda c: (0, 0, 0)),
            pl.BlockSpec((1, dp), lambda c: (0, 0)),
        ],
        out_specs=[pl.BlockSpec((tc, dp), lambda c: (c, 0))] * 2,
        out_shape=[jax.ShapeDtypeStruct((t, dp), BF16)] * 2,
        scratch_shapes=[pltpu.VMEM((tc, dp), F32)] * 4,
        args=(proj, pool_w, pool_scale))[0]


def _branch_mix(y_lru, y_pool, w_lru_up, w_pool_upt, proj, b_gate, ga_block, gb_block, carry=None):
    t, d = y_lru.shape
    dp = y_pool.shape[1]
    tt, tn = _tile(t, 1024), 512
    nj = d // tn

    def body(yl_ref, yp_ref, wl_ref, wp_ref, ga_ref, gb_ref, ba_ref, bb_ref, bra_ref, brb_ref, mix_ref):
        br_a = _dot_nn(yl_ref[...], wl_ref[...])
        br_b = _dot_nt(yp_ref[...], wp_ref[...])
        bra_ref[...] = br_a.astype(BF16)
        brb_ref[...] = br_b.astype(BF16)
        ga = _sig(ga_ref[...].astype(F32) + ba_ref[...])
        gb = _sig(gb_ref[...].astype(F32) + bb_ref[...])
        mix_ref[...] = (ga * br_a + gb * br_b).astype(BF16)

    out = pl.BlockSpec((tt, tn), lambda j, i: (i, j))
    return _call(
        body, name="branch_mix", grid=(nj, t // tt),
        in_specs=[
            pl.BlockSpec((tt, d), lambda j, i: (i, 0)),
            pl.BlockSpec((tt, dp), lambda j, i: (i, 0)),
            pl.BlockSpec((d, tn), lambda j, i: (0, j)),
            pl.BlockSpec((tn, dp), lambda j, i: (j, 0)),
            pl.BlockSpec((tt, tn), lambda j, i: (i, ga_block + j)),
            pl.BlockSpec((tt, tn), lambda j, i: (i, gb_block + j)),
            pl.BlockSpec((1, tn), lambda j, i: (0, j)),
            pl.BlockSpec((1, tn), lambda j, i: (0, nj + j)),
        ],
        out_specs=[out, out, out],
        out_shape=[jax.ShapeDtypeStruct((t, d), BF16)] * 3,
        args=(y_lru, y_pool, w_lru_up, w_pool_upt, proj, proj, b_gate, b_gate), carry=carry)


def _wo_norm(mix, w_o, x, g2, g3, carry=None):
    t, d = x.shape
    tt = _tile(t, 512)

    def body(mix_ref, w_ref, x_ref, g2_ref, g3_ref, m_ref, x2_ref, h3_ref):
        m = _dot_nn(mix_ref[...], w_ref[...])
        m_ref[...] = m
        mhat, _ = _rms_hat(m)
        x2 = x_ref[...] + mhat * g2_ref[...]
        x2_ref[...] = x2
        xhat, _ = _rms_hat(x2)
        h3_ref[...] = (xhat * g3_ref[...]).astype(BF16)

    row = pl.BlockSpec((tt, d), lambda i: (i, 0))
    vec = pl.BlockSpec((1, d), lambda i: (0, 0))
    return _call(
        body, name="wo_norm", grid=(t // tt,),
        in_specs=[row, pl.BlockSpec((d, d), lambda i: (0, 0)), row, vec, vec],
        out_specs=[row, row, row],
        out_shape=[
            jax.ShapeDtypeStruct((t, d), F32),
            jax.ShapeDtypeStruct((t, d), F32),
            jax.ShapeDtypeStruct((t, d), BF16),
        ],
        args=(mix, w_o, x, g2, g3), carry=carry)


def _ff1(h3, w_ff1t, carry=None):
    t, d = h3.shape
    n = w_ff1t.shape[0]
    tt, tn = _tile(t, 2048), _tile(n, 512)

    def body(h_ref, w_ref, rf_ref):
        rf_ref[...] = jnp.maximum(_dot_nt(h_ref[...], w_ref[...]), 0.0).astype(BF16)

    out = pl.BlockSpec((tt, tn), lambda i, j: (i, j))
    return _call(
        body, name="ff1", grid=(t // tt, n // tn),
        in_specs=[pl.BlockSpec((tt, d), lambda i, j: (i, 0)), pl.BlockSpec((tn, d), lambda i, j: (j, 0))],
        out_specs=[out],
        out_shape=[jax.ShapeDtypeStruct((t, n), BF16)],
        args=(h3, w_ff1t), carry=carry)


def _ff2_loss(rf, w_ff2, x2, g4, target):
    t, k = rf.shape
    d = x2.shape[1]
    tt, tk = _tile(t, 1024), _tile(k, 1024)
    nk = k // tk

    def body(a_ref, w_ref, x2_ref, g_ref, tg_ref, dy_ref, df_ref, dg_ref, loss_ref, acc):
        i, kk = pl.program_id(0), pl.program_id(1)

        @pl.when(kk == 0)
        def _():
            acc[...] = jnp.zeros_like(acc)

        @pl.when((i == 0) & (kk == 0))
        def _():
            dg_ref[...] = jnp.zeros_like(dg_ref)
            loss_ref[...] = jnp.zeros_like(loss_ref)

        rf_tile = a_ref[...]
        acc[...] += _dot_nn(rf_tile * rf_tile, w_ref[...])

        @pl.when(kk == nk - 1)
        def _():
            def tail(rows):
                fhat, r = _rms_hat(acc[rows, :])
                g = g_ref[...]
                e = x2_ref[rows, :] + fhat * g - tg_ref[rows, :]
                loss_ref[...] += 0.5 * jnp.sum(jnp.mean(e * e, axis=-1, keepdims=True))
                dy = e * (1.0 / d)
                dy_ref[rows, :] = dy.astype(BF16)
                df, dg = _rms_bwd(dy, fhat, r, g)
                df_ref[rows, :] = df.astype(BF16)
                dg_ref[...] += dg

            _row_chunks(tt, tail)

    row = pl.BlockSpec((tt, d), lambda i, kk: (i, 0))
    vec = pl.BlockSpec((1, d), lambda i, kk: (0, 0))
    return _call(
        body, name="ff2_loss", grid=(t // tt, nk),
        in_specs=[
            pl.BlockSpec((tt, tk), lambda i, kk: (i, kk)),
            pl.BlockSpec((tk, d), lambda i, kk: (kk, 0)),
            row, vec, row,
        ],
        out_specs=[row, row, vec, pl.BlockSpec((1, 128), lambda i, kk: (0, 0))],
        out_shape=[
            jax.ShapeDtypeStruct((t, d), BF16),
            jax.ShapeDtypeStruct((t, d), BF16),
            jax.ShapeDtypeStruct((1, d), F32),
            jax.ShapeDtypeStruct((1, 128), F32),
        ],
        scratch_shapes=[pltpu.VMEM((tt, d), F32)],
        args=(rf, w_ff2, x2, g4, target))[0]


def _ff2_bwd(df, w_ff2, rf, carry=None):
    t, d = df.shape
    n = w_ff2.shape[0]
    tt, tn = _tile(t, 2048), _tile(n, 512)

    def body(df_ref, w_ref, rf_ref, out_ref):
        d_act = _dot_nt(df_ref[...], w_ref[...])
        out_ref[...] = (d_act * (2.0 * rf_ref[...].astype(F32))).astype(BF16)

    blk = pl.BlockSpec((tt, tn), lambda i, j: (i, j))
    return _call(
        body, name="ff2_bwd", grid=(t // tt, n // tn),
        in_specs=[pl.BlockSpec((tt, d), lambda i, j: (i, 0)), pl.BlockSpec((tn, d), lambda i, j: (j, 0)), blk],
        out_specs=[blk],
        out_shape=[jax.ShapeDtypeStruct((t, n), BF16)],
        args=(df, w_ff2, rf), carry=carry)


def _wgrad(a, b, name, prev=None, row_off=0, rows=None, carry=None, square_a=False):
    t, m = a.shape
    n = b.shape[1]
    rows = m if rows is None else rows
    tm, tk = _tile(m, 512), _tile(t, 2048)
    nk = t // tk
    assert row_off % tm == 0
    off = row_off // tm

    def body(*refs):
        a_ref, b_ref = refs[0], refs[1]
        o32_ref, o16_ref, acc = refs[-3], refs[-2], refs[-1]
        kk = pl.program_id(1)

        @pl.when(kk == 0)
        def _():
            acc[...] = jnp.zeros_like(acc)

        a_tile = a_ref[...]
        acc[...] += _dot_tn(a_tile * a_tile if square_a else a_tile, b_ref[...])

        @pl.when(kk == nk - 1)
        def _():
            o32_ref[...] = acc[...]
            o16_ref[...] = acc[...].astype(BF16)

    in_specs = [pl.BlockSpec((tk, tm), lambda i, kk: (kk, i)), pl.BlockSpec((tk, n), lambda i, kk: (kk, 0))]
    args = [a, b]
    aliases = {}
    if prev is not None:
        in_specs += [ANY, ANY]
        args += list(prev)
        aliases = {2: 0, 3: 1}
    out = pl.BlockSpec((tm, n), lambda i, kk: (off + i, 0))
    return _call(
        body, name=name, grid=(m // tm, nk),
        in_specs=in_specs, out_specs=[out, out],
        out_shape=[jax.ShapeDtypeStruct((rows, n), F32), jax.ShapeDtypeStruct((rows, n), BF16)],
        scratch_shapes=[pltpu.VMEM((tm, n), F32)],
        aliases=aliases, args=args, carry=carry)


def _wgrad_parts(parts, b, name, carry=None):
    t, n = b.shape
    tm = 512
    bounds = []
    lo = 0
    for part in parts:
        assert part.shape[0] == t and part.shape[1] % tm == 0
        bounds.append((lo, lo + part.shape[1] // tm))
        lo += part.shape[1] // tm
    nm = lo
    np_ = len(parts)

    def body(*refs):
        p_refs, b_ref, o32_ref, o16_ref = refs[:np_], refs[np_], refs[np_ + 1], refs[np_ + 2]
        i = pl.program_id(0)
        for (lo_p, hi_p), p_ref in zip(bounds, p_refs):
            @pl.when((i >= lo_p) & (i < hi_p))
            def _(p_ref=p_ref):
                res = _dot_tn(p_ref[...], b_ref[...])
                o32_ref[...] = res
                o16_ref[...] = res.astype(BF16)

    def part_spec(lo_p, hi_p):
        return pl.BlockSpec((t, tm), lambda i: (0, jnp.clip(i - lo_p, 0, hi_p - lo_p - 1)))

    out = pl.BlockSpec((tm, n), lambda i: (i, 0))
    return _call(
        body, name=name, grid=(nm,),
        in_specs=[part_spec(lo_p, hi_p) for lo_p, hi_p in bounds] + [pl.BlockSpec((t, n), lambda i: (0, 0))],
        out_specs=[out, out],
        out_shape=[jax.ShapeDtypeStruct((nm * tm, n), F32), jax.ShapeDtypeStruct((nm * tm, n), BF16)],
        args=(*parts, b), carry=carry)


def _ff1_bwd_norms(d_f1, w_ff1t, dy, x2, g3, m, g2, carry=None):
    t, k = d_f1.shape
    d = x2.shape[1]
    tt, tk = _tile(t, 1024), _tile(k, 1024)
    nk = k // tk

    def body(a_ref, w_ref, dy_ref, x2_ref, g3_ref, m_ref, g2_ref, dx2_ref, dm_ref, dg3_ref, dg2_ref, acc):
        i, kk = pl.program_id(0), pl.program_id(1)

        @pl.when(kk == 0)
        def _():
            acc[...] = jnp.zeros_like(acc)

        @pl.when((i == 0) & (kk == 0))
        def _():
            dg3_ref[...] = jnp.zeros_like(dg3_ref)
            dg2_ref[...] = jnp.zeros_like(dg2_ref)

        acc[...] += _dot_nn(a_ref[...], w_ref[...])

        @pl.when(kk == nk - 1)
        def _():
            def tail(rows):
                xhat, r3 = _rms_hat(x2_ref[rows, :])
                dx, dg3 = _rms_bwd(acc[rows, :], xhat, r3, g3_ref[...])
                dx2 = dy_ref[rows, :].astype(F32) + dx
                dx2_ref[rows, :] = dx2
                dg3_ref[...] += dg3
                mhat, r2 = _rms_hat(m_ref[rows, :])
                dm, dg2 = _rms_bwd(dx2, mhat, r2, g2_ref[...])
                dm_ref[rows, :] = dm.astype(BF16)
                dg2_ref[...] += dg2

            _row_chunks(tt, tail)

    row = pl.BlockSpec((tt, d), lambda i, kk: (i, 0))
    vec = pl.BlockSpec((1, d), lambda i, kk: (0, 0))
    return _call(
        body, name="ff1_bwd_norms", grid=(t // tt, nk),
        in_specs=[
            pl.BlockSpec((tt, tk), lambda i, kk: (i, kk)),
            pl.BlockSpec((tk, d), lambda i, kk: (kk, 0)),
            row, row, vec, row, vec,
        ],
        out_specs=[row, row, vec, vec],
        out_shape=[
            jax.ShapeDtypeStruct((t, d), F32),
            jax.ShapeDtypeStruct((t, d), BF16),
            jax.ShapeDtypeStruct((1, d), F32),
            jax.ShapeDtypeStruct((1, d), F32),
        ],
        scratch_shapes=[pltpu.VMEM((tt, d), F32)],
        args=(d_f1, w_ff1t, dy, x2, g3, m, g2), carry=carry)


def _wo_bwd_mix(dm, w_o, br_a, br_b, proj, b_gate, ga_block, gb_block, carry=None):
    t, d = dm.shape
    tt, tn = _tile(t, 1024), 512
    nj = d // tn

    def body(dm_ref, w_ref, bra_ref, brb_ref, ga_ref, gb_ref, ba_ref, bb_ref,
             dbra_ref, dbrb_ref, dga_ref, dgb_ref, dba_ref, dbb_ref):
        i = pl.program_id(1)

        @pl.when(i == 0)
        def _():
            dba_ref[...] = jnp.zeros_like(dba_ref)
            dbb_ref[...] = jnp.zeros_like(dbb_ref)

        d_mix = _dot_nt(dm_ref[...], w_ref[...])
        ga = _sig(ga_ref[...].astype(F32) + ba_ref[...])
        gb = _sig(gb_ref[...].astype(F32) + bb_ref[...])
        dbra_ref[...] = (d_mix * ga).astype(BF16)
        dbrb_ref[...] = (d_mix * gb).astype(BF16)
        dga = d_mix * bra_ref[...].astype(F32) * (ga * (1.0 - ga))
        dgb = d_mix * brb_ref[...].astype(F32) * (gb * (1.0 - gb))
        dga_ref[...] = dga.astype(BF16)
        dgb_ref[...] = dgb.astype(BF16)
        dba_ref[...] += jnp.sum(dga, axis=0, keepdims=True)
        dbb_ref[...] += jnp.sum(dgb, axis=0, keepdims=True)

    blk = pl.BlockSpec((tt, tn), lambda j, i: (i, j))
    vec = pl.BlockSpec((1, tn), lambda j, i: (0, j))
    return _call(
        body, name="wo_bwd_mix", grid=(nj, t // tt),
        in_specs=[
            pl.BlockSpec((tt, d), lambda j, i: (i, 0)),
            pl.BlockSpec((tn, d), lambda j, i: (j, 0)),
            blk, blk,
            pl.BlockSpec((tt, tn), lambda j, i: (i, ga_block + j)),
            pl.BlockSpec((tt, tn), lambda j, i: (i, gb_block + j)),
            vec,
            pl.BlockSpec((1, tn), lambda j, i: (0, nj + j)),
        ],
        out_specs=[blk, blk, blk, blk, vec, vec],
        out_shape=[jax.ShapeDtypeStruct((t, d), BF16)] * 4 + [jax.ShapeDtypeStruct((1, d), F32)] * 2,
        args=(dm, w_o, br_a, br_b, proj, proj, b_gate, b_gate), carry=carry)


def _lru_up_bwd(d_br_a, w_lru_up, proj, h, g_block, carry=None):
    t, d = d_br_a.shape
    tt, tn = _tile(t, 1024), 512

    def body(a_ref, w_ref, g_ref, h_ref, dh_ref, dg_ref):
        d_y = _dot_nt(a_ref[...], w_ref[...])
        gel, gel_grad = _gelu_and_grad(g_ref[...].astype(F32))
        dh_ref[...] = d_y * gel
        dg_ref[...] = (d_y * h_ref[...] * gel_grad).astype(BF16)

    blk = pl.BlockSpec((tt, tn), lambda i, j: (i, j))
    return _call(
        body, name="lru_up_bwd", grid=(t // tt, d // tn),
        in_specs=[
            pl.BlockSpec((tt, d), lambda i, j: (i, 0)),
            pl.BlockSpec((tn, d), lambda i, j: (j, 0)),
            pl.BlockSpec((tt, tn), lambda i, j: (i, g_block + j)),
            blk,
        ],
        out_specs=[blk, blk],
        out_shape=[jax.ShapeDtypeStruct((t, d), F32), jax.ShapeDtypeStruct((t, d), BF16)],
        args=(d_br_a, w_lru_up, proj, h), carry=carry)


def _lru_bwd(dh, xc, h, proj, conv_w, w_a, b_a, w_x, b_x, lam, carry=None):
    t, dr = dh.shape
    cb = LRU_CB
    hd = LRU_HEAD_DIM
    per = cb // hd
    tc = _tile(t, 256)
    ncb, ntc = dr // cb, t // tc

    def body(dh_ref, xc_ref, h_ref, hp_ref, xp_ref, cw_ref, wa_ref, ba_ref, wx_ref, bx_ref, lam_ref,
             dxp_ref, dwa_ref, dba_ref, dwx_ref, dbx_ref, dlam_ref, dcw_ref, dcb_ref,
             nextd_s, anext_s, gnext_s, tmp_s, wa_s, wx_s):
        c = pl.program_id(1)
        rc = ntc - 1 - c

        @pl.when(c == 0)
        def _():
            nextd_s[...] = jnp.zeros_like(nextd_s)
            anext_s[...] = jnp.zeros_like(anext_s)
            gnext_s[...] = jnp.zeros_like(gnext_s)
            for ref in (dwa_ref, dba_ref, dwx_ref, dbx_ref, dlam_ref, dcw_ref, dcb_ref):
                ref[...] = jnp.zeros_like(ref)
            _fill_block_diag(wa_ref, wa_s)
            _fill_block_diag(wx_ref, wx_s)

        xc = xc_ref[...]
        wa, wx, lam = wa_s[...], wx_s[...], lam_ref[...]
        xcb, r, i, sp, log_a, a, mult = _lru_gates(xc, wa, ba_ref[...], wx, bx_ref[...], lam)
        row = lax.broadcasted_iota(jnp.int32, xc.shape, 0)
        h = h_ref[...]
        hp = jnp.where(rc == 0, 0.0, hp_ref[...])
        hprev = jnp.where(row >= 1, pltpu.roll(h, 1, 0), pltpu.roll(hp, 1, 0))

        def up(v, nv, j):
            return jnp.where(row < tc - j, pltpu.roll(v, tc - j, 0), nv)

        av, bv = _scan_rows(up(a, anext_s[...], 1), dh_ref[...], reverse=True)
        gt = av * gnext_s[...] + bv
        tmp_s[...] = gt
        gnext_s[...] = tmp_s[0:1, :]
        tmp_s[...] = a
        anext_s[...] = tmp_s[0:1, :]

        da = gt * hprev
        ixc = i * xc
        d_mult = gt * ixc
        d_i = gt * mult * xc
        d_xc = gt * mult * i
        d_log_a = da * a - d_mult * (a * a) / mult
        d_pre_r = (d_log_a * ((-LRU_C) * sp)) * (r * (1.0 - r))
        d_pre_i = d_i * (i * (1.0 - i))
        d_sp = jnp.sum(d_log_a * ((-LRU_C) * r), axis=0, keepdims=True)
        dlam_ref[...] += d_sp * (-1.0 / (1.0 + jnp.exp(lam)))
        dpr = d_pre_r.astype(BF16)
        dpi = d_pre_i.astype(BF16)
        dba_ref[...] += jnp.sum(d_pre_r, axis=0, keepdims=True)
        dbx_ref[...] += jnp.sum(d_pre_i, axis=0, keepdims=True)
        pa = _dot_tn(xcb, dpr)
        px = _dot_tn(xcb, dpi)
        for k in range(per):
            dwa_ref[k] += pa[k * hd:(k + 1) * hd, k * hd:(k + 1) * hd]
            dwx_ref[k] += px[k * hd:(k + 1) * hd, k * hd:(k + 1) * hd]
        d_xc = d_xc + _dot_nt(dpr, wa) + _dot_nt(dpi, wx)

        nxt = nextd_s[...]
        xp = xp_ref[...].astype(F32)
        dxp = cw_ref[3:4, :] * d_xc
        dcw_ref[3:4, :] += jnp.sum(xp * d_xc, axis=0, keepdims=True)
        for j in (1, 2, 3):
            uj = up(d_xc, pltpu.roll(nxt, tc - j, 0), j)
            dxp = dxp + cw_ref[3 - j:4 - j, :] * uj
            dcw_ref[3 - j:4 - j, :] += jnp.sum(xp * uj, axis=0, keepdims=True)
        dcb_ref[...] += jnp.sum(d_xc, axis=0, keepdims=True)
        nextd_s[...] = d_xc
        dxp_ref[...] = dxp.astype(BF16)

    vec = pl.BlockSpec((1, cb), lambda j, c: (0, j))
    blk = pl.BlockSpec((tc, cb), lambda j, c: (ntc - 1 - c, j))
    mat = pl.BlockSpec((per, hd, hd), lambda j, c: (j, 0, 0))
    cwb = pl.BlockSpec((4, cb), lambda j, c: (0, j))
    return _call(
        body, name="lru_bwd", grid=(ncb, ntc),
        in_specs=[
            blk, blk, blk,
            pl.BlockSpec((tc, cb), lambda j, c: (jnp.maximum(ntc - 2 - c, 0), j)),
            blk, cwb, mat, vec, mat, vec, vec,
        ],
        out_specs=[blk, mat, vec, mat, vec, vec, cwb, vec],
        out_shape=[
            jax.ShapeDtypeStruct((t, dr), BF16),
            jax.ShapeDtypeStruct(w_a.shape, F32),
            jax.ShapeDtypeStruct((1, dr), F32),
            jax.ShapeDtypeStruct(w_x.shape, F32),
            jax.ShapeDtypeStruct((1, dr), F32),
            jax.ShapeDtypeStruct((1, dr), F32),
            jax.ShapeDtypeStruct((4, dr), F32),
            jax.ShapeDtypeStruct((1, dr), F32),
        ],
        scratch_shapes=[
            pltpu.VMEM((tc, cb), F32),
            pltpu.VMEM((1, cb), F32),
            pltpu.VMEM((1, cb), F32),
            pltpu.VMEM((tc, cb), F32),
            pltpu.VMEM((cb, cb), BF16),
            pltpu.VMEM((cb, cb), BF16),
        ],
        args=(dh, xc, h, h, proj, conv_w, w_a, b_a, w_x, b_x, lam), carry=carry)


def _pool_bwd(d_br_b, w_pool_upt, p, pool_w, pool_scale):
    t, d = d_br_b.shape
    dp = w_pool_upt.shape[1]
    tc = _tile(t, 256)
    ntc = t // tc
    ng = len(POOL_WINDOWS)

    def body(db_ref, wu_ref, p_ref, w_ref, sc_ref, dx_ref, dw_ref, dsc_ref, nz, n2, n4, n8, dp_s, dy_s):
        c = pl.program_id(0)
        rc = ntc - 1 - c

        @pl.when(c == 0)
        def _():
            for s in (nz, n2, n4, n8):
                s[...] = jnp.zeros_like(s)
            dw_ref[...] = jnp.zeros_like(dw_ref)
            dsc_ref[...] = jnp.zeros_like(dsc_ref)

        dy_s[...] = _dot_nn(db_ref[...], wu_ref[...])
        for g in range(ng):
            sl = slice(g * POOL_GROUP_DIM, (g + 1) * POOL_GROUP_DIM)
            pg = p_ref[:, sl]
            dyg = dy_s[:, sl]
            wg = w_ref[g].astype(BF16)
            q = _dot_nn(pg, wg)
            dsc_ref[:, sl] += jnp.sum(dyg * q, axis=0, keepdims=True)
            dpw = (dyg * sc_ref[:, sl]).astype(BF16)
            dw_ref[g] += _dot_tn(pg, dpw)
            dp_s[:, sl] = _dot_nt(dpw, wg)

        dpv = dp_s[...]
        row = lax.broadcasted_iota(jnp.int32, dpv.shape, 0)
        col = lax.broadcasted_iota(jnp.int32, dpv.shape, 1)
        win = _pool_select(col, POOL_WINDOWS)
        cnt = jnp.minimum(rc * tc + row + 1, win).astype(F32)
        z = dpv / cnt

        def up(v, nv, j):
            return jnp.where(row < tc - j, pltpu.roll(v, tc - j, 0), pltpu.roll(nv[...], tc - j, 0))

        u2 = z + up(z, nz, 1)
        u4 = u2 + up(u2, n2, 2)
        u8 = u4 + up(u4, n4, 4)
        u16 = u8 + up(u8, n8, 8)
        nz[...] = z
        n2[...] = u2
        n4[...] = u4
        n8[...] = u8
        dx_ref[...] = (_pool_select(col, (u2, u4, u8, u16)) - dpv).astype(BF16)

    blk = pl.BlockSpec((tc, dp), lambda c: (ntc - 1 - c, 0))
    full_w = pl.BlockSpec(pool_w.shape, lambda c: (0, 0, 0))
    vec = pl.BlockSpec((1, dp), lambda c: (0, 0))
    return _call(
        body, name="pool_bwd", grid=(ntc,),
        in_specs=[pl.BlockSpec((tc, d), lambda c: (ntc - 1 - c, 0)), pl.BlockSpec((d, dp), lambda c: (0, 0)),
                  blk, full_w, vec],
        out_specs=[blk, full_w, vec],
        out_shape=[
            jax.ShapeDtypeStruct((t, dp), BF16),
            jax.ShapeDtypeStruct(pool_w.shape, F32),
            jax.ShapeDtypeStruct((1, dp), F32),
        ],
        scratch_shapes=[pltpu.VMEM((tc, dp), F32)] * 6,
        args=(d_br_b, w_pool_upt, p, pool_w, pool_scale))[0]


def _win_bwd_norm(parts, w_int, dx2, x, g1, carry=None):
    t, d = x.shape
    tk = 512
    tt = _tile(t, 1024)
    bounds = []
    k0 = 0
    for part in parts:
        assert part.shape[1] % tk == 0
        bounds.append((k0, k0 + part.shape[1] // tk))
        k0 += part.shape[1] // tk
    nk = k0
    assert nk * tk == w_int.shape[0]
    np_ = len(parts)

    def body(*refs):
        p_refs = refs[:np_]
        w_ref, dx2_ref, x_ref, g_ref, gx_ref, dg_ref, acc = refs[np_:]
        i, kk = pl.program_id(0), pl.program_id(1)

        @pl.when(kk == 0)
        def _():
            acc[...] = jnp.zeros_like(acc)

        @pl.when((i == 0) & (kk == 0))
        def _():
            dg_ref[...] = jnp.zeros_like(dg_ref)

        for (lo, hi), p_ref in zip(bounds, p_refs):
            @pl.when((kk >= lo) & (kk < hi))
            def _(p_ref=p_ref):
                acc[...] += _dot_nn(p_ref[...], w_ref[...])

        @pl.when(kk == nk - 1)
        def _():
            def tail(rows):
                xhat, r = _rms_hat(x_ref[rows, :])
                dx, dg = _rms_bwd(acc[rows, :], xhat, r, g_ref[...])
                gx_ref[rows, :] = dx2_ref[rows, :] + dx
                dg_ref[...] += dg

            _row_chunks(tt, tail)

    def part_spec(lo, hi):
        return pl.BlockSpec((tt, tk), lambda i, kk: (i, jnp.clip(kk - lo, 0, hi - lo - 1)))

    row = pl.BlockSpec((tt, d), lambda i, kk: (i, 0))
    vec = pl.BlockSpec((1, d), lambda i, kk: (0, 0))
    return _call(
        body, name="win_bwd_norm", grid=(t // tt, nk),
        in_specs=[part_spec(lo, hi) for lo, hi in bounds]
        + [pl.BlockSpec((tk, d), lambda i, kk: (kk, 0)), row, row, vec],
        out_specs=[row, vec],
        out_shape=[jax.ShapeDtypeStruct((t, d), F32), jax.ShapeDtypeStruct((1, d), F32)],
        scratch_shapes=[pltpu.VMEM((tt, d), F32)],
        args=(*parts, w_int, dx2, x, g1), carry=carry)


def _adam_math(w, g, m, v):
    m = ADAM_B1 * m + (1.0 - ADAM_B1) * g
    v = ADAM_B2 * v + (1.0 - ADAM_B2) * (g * g)
    m_hat = m / (1.0 - ADAM_B1 ** ADAM_STEP)
    v_hat = v / (1.0 - ADAM_B2 ** ADAM_STEP)
    delta = -ADAM_LR * (m_hat / (jnp.sqrt(v_hat) + ADAM_EPS) + ADAM_WD * w)
    return delta, m, v


def _adamw_big(ws, gs, ms, vs):
    n = len(ws)
    nb = 4
    pair = [isinstance(g, tuple) for g in gs]

    def body(*refs):
        p = 0
        ins = []
        for a in range(n):
            k = 5 if pair[a] else 4
            ins.append(refs[p:p + k])
            p += k
        for a in range(n):
            g_out, d_ref, nm_ref, nv_ref = refs[p + 4 * a:p + 4 * a + 4]
            if pair[a]:
                w_ref, own_ref, recv_ref, m_ref, v_ref = ins[a]
                g = own_ref[...]
                for k in range(3):
                    g = g + recv_ref[k].astype(F32)
            else:
                w_ref, g_ref, m_ref, v_ref = ins[a]
                g = g_ref[...]
            dl, m, v = _adam_math(w_ref[...], g, m_ref[...], v_ref[...])
            g_out[...] = g
            d_ref[...] = dl
            nm_ref[...] = m
            nv_ref[...] = v

    in_specs, out_specs, out_shape, args = [], [], [], []
    for a, (w, g, m, v) in enumerate(zip(ws, gs, ms, vs)):
        rows, cols = w.shape
        blk = pl.BlockSpec((rows // nb, cols), lambda i: (i, 0))
        if pair[a]:
            in_specs += [blk, pl.BlockSpec((None, rows // nb, cols), lambda i: (0, i, 0)),
                         pl.BlockSpec((3, rows // nb, cols), lambda i: (0, i, 0)), blk, blk]
            args += [w, g[0], g[1], m, v]
        else:
            in_specs += [blk] * 4
            args += [w, g, m, v]
        out_specs += [blk] * 4
        out_shape += [jax.ShapeDtypeStruct(w.shape, F32)] * 4
    outs = _call(body, name="adamw_big", grid=(nb,), in_specs=in_specs, out_specs=out_specs,
                 out_shape=out_shape, args=args)[0]
    return [tuple(outs[4 * a:4 * a + 4]) for a in range(n)]


SMALL_ORDER = ("norm_mix_pre", "norm_mix_post", "norm_mlp_pre", "norm_mlp_post", "b_gate", "conv_w", "conv_b",
               "lru_w_a", "lru_b_a", "lru_w_x", "lru_b_x", "lru_lambda", "pool_w", "pool_scale")
VEC_ROW = dict(norm_mix_pre=0, norm_mix_post=1, norm_mlp_pre=2, norm_mlp_post=3, conv_b=6, lru_b_a=7,
               lru_b_x=8, lru_lambda=9)
ROW_B_GATE, ROW_POOL_SCALE, ROW_CONV_W, ROW_LOSS, N_VEC_ROWS = 4, 10, 11, 15, 16


def _adamw_small(vec_parts, g_pool, g_wa, g_wx, me, params):
    d = vec_parts.shape[2]
    names = SMALL_ORDER
    n = len(names)
    cw_cols = params["conv_w"][0].shape[2]

    def body(me_ref, vec_ref, vecc_ref, gp_ref, gwa_ref, gwx_ref, *refs):
        wmv = refs[:3 * n]
        loss_ref = refs[3 * n]
        outs = refs[3 * n + 1:3 * n + 1 + 4 * n]
        vs, vsc = refs[3 * n + 1 + 4 * n:]
        acc, accc = vec_ref[0], vecc_ref[0]
        for k in range(1, N_DEV):
            acc = acc + vec_ref[k]
            accc = accc + vecc_ref[k]
        vs[...] = acc
        vsc[...] = accc
        loss_ref[...] = vs[ROW_LOSS:ROW_LOSS + 1, 0:128]

        def upd(a, g, idx):
            w_ref, m_ref, v_ref = wmv[3 * a:3 * a + 3]
            g_ref, d_ref, nm_ref, nv_ref = outs[4 * a:4 * a + 4]
            dl, m, v = _adam_math(w_ref[idx], g, m_ref[idx], v_ref[idx])
            g_ref[idx] = g
            d_ref[idx] = dl
            nm_ref[idx] = m
            nv_ref[idx] = v

        for a, name in enumerate(names):
            if name in VEC_ROW:
                r = VEC_ROW[name]
                upd(a, vs[r:r + 1, :], (slice(None), slice(None)))
            elif name == "b_gate":
                for half in range(2):
                    r = ROW_B_GATE + half
                    upd(a, vs[r:r + 1, :], (slice(None), slice(half * d, (half + 1) * d)))
            elif name == "pool_scale":
                width = params[name][0].shape[1]
                upd(a, vs[ROW_POOL_SCALE:ROW_POOL_SCALE + 1, 0:width], (slice(None), slice(None)))
            elif name == "conv_w":
                upd(a, vsc[ROW_CONV_W:ROW_CONV_W + 4, :], (0,))
            elif name == "pool_w":
                upd(a, gp_ref[...], (Ellipsis,))
            elif name == "lru_w_a":
                upd(a, gwa_ref[...], (Ellipsis,))
            elif name == "lru_w_x":
                upd(a, gwx_ref[...], (Ellipsis,))
            else:
                raise ValueError(name)

    def whole(shape):
        nd = len(shape)
        return pl.BlockSpec(tuple(shape), lambda i, me_ref: (0,) * nd)

    in_specs = [
        whole(vec_parts.shape),
        pl.BlockSpec((N_DEV, N_VEC_ROWS, cw_cols), lambda i, me_ref: (0, 0, me_ref[0])),
        whole(g_pool.shape), whole(g_wa.shape), whole(g_wx.shape),
    ]
    args = [vec_parts, vec_parts, g_pool, g_wa, g_wx]
    out_specs = [whole((1, 128))]
    out_shape = [jax.ShapeDtypeStruct((1, 128), F32)]
    for name in names:
        for arr in params[name]:
            in_specs.append(whole(arr.shape))
            args.append(arr)
        shp = params[name][0].shape
        out_specs += [whole(shp)] * 4
        out_shape += [jax.ShapeDtypeStruct(shp, F32)] * 4
    grid_spec = pltpu.PrefetchScalarGridSpec(
        num_scalar_prefetch=1, grid=(1,), in_specs=in_specs, out_specs=out_specs,
        scratch_shapes=[pltpu.VMEM((N_VEC_ROWS, d), F32), pltpu.VMEM((N_VEC_ROWS, cw_cols), F32)])
    outs = pl.pallas_call(
        body, name="adamw_small", grid_spec=grid_spec, out_shape=out_shape,
        compiler_params=pltpu.CompilerParams(
            dimension_semantics=("arbitrary",), vmem_limit_bytes=V7X_VMEM_LIMIT_BYTES),
    )(me, *_in_hbm(args))
    return outs[0], {name: tuple(outs[1 + 4 * a:5 + 4 * a]) for a, name in enumerate(names)}


def _rs_sum(fulls, recvs, shard_ids, slot_ids, name):
    n = len(fulls)

    def body(sh_ref, sl_ref, *refs):
        s = pl.program_id(0)
        for a in range(n):
            full_ref, recv_ref = refs[2 * a], refs[2 * a + 1]
            own_ref, send_ref = refs[2 * n + 2 * a], refs[2 * n + 2 * a + 1]
            v = full_ref[...] + recv_ref[...].astype(F32)

            @pl.when(s == 0)
            def _(own_ref=own_ref, v=v):
                own_ref[...] = v

            @pl.when(s > 0)
            def _(send_ref=send_ref, v=v):
                send_ref[...] = v.astype(send_ref.dtype)

    in_specs, out_specs, out_shape, args = [], [], [], []
    for full, recv in zip(fulls, recvs):
        r, rest = recv.shape[1], tuple(recv.shape[2:])
        zeros = (0,) * len(rest)
        in_specs += [
            pl.BlockSpec((r,) + rest, lambda s, sh, sl, zeros=zeros: (sh[s],) + zeros),
            pl.BlockSpec((None, r) + rest, lambda s, sh, sl, zeros=zeros: (sl[s], 0) + zeros),
        ]
        out_specs += [
            pl.BlockSpec((None, r) + rest, lambda s, sh, sl, zeros=zeros: (0, 0) + zeros),
            pl.BlockSpec((None, r) + rest, lambda s, sh, sl, zeros=zeros: (jnp.maximum(s - 1, 0), 0) + zeros),
        ]
        out_shape += [jax.ShapeDtypeStruct((1, r) + rest, F32), jax.ShapeDtypeStruct((3, r) + rest, recv.dtype)]
        args += [full, recv]
    grid_spec = pltpu.PrefetchScalarGridSpec(
        num_scalar_prefetch=2, grid=(4,), in_specs=in_specs, out_specs=out_specs)
    outs = pl.pallas_call(
        body,
        name=name,
        grid_spec=grid_spec,
        out_shape=out_shape,
        compiler_params=pltpu.CompilerParams(
            dimension_semantics=("arbitrary",), vmem_limit_bytes=V7X_VMEM_LIMIT_BYTES),
    )(shard_ids, slot_ids, *_in_hbm(args))
    return [(outs[2 * a], outs[2 * a + 1]) for a in range(n)]


def _finals(pairs, name, carry=None):
    nb = 4
    n = len(pairs)

    def body(*refs):
        for a in range(n):
            own_ref, recv_ref = refs[2 * a], refs[2 * a + 1]
            acc = own_ref[...]
            for k in range(3):
                acc = acc + recv_ref[k].astype(F32)
            refs[2 * n + a][...] = acc

    in_specs, out_specs, out_shape, args = [], [], [], []
    for own, recv in pairs:
        _, rows, cols = own.shape
        in_specs += [pl.BlockSpec((None, rows // nb, cols), lambda i: (0, i, 0)),
                     pl.BlockSpec((3, rows // nb, cols), lambda i: (0, i, 0))]
        args += [own, recv]
        out_specs.append(pl.BlockSpec((rows // nb, cols), lambda i: (i, 0)))
        out_shape.append(jax.ShapeDtypeStruct((rows, cols), F32))
    return _call(body, name=name, grid=(nb,), in_specs=in_specs, out_specs=out_specs,
                 out_shape=out_shape, args=args, carry=carry)


def _rs_sums(fulls_f32, recv1, tag):
    x, y, c = _place()
    qs = jnp.stack([2 * x + y, 2 * (1 - x) + y, 2 * x + (1 - y), 2 * (1 - x) + (1 - y)]).astype(jnp.int32)
    shard_ids = 2 * qs + c
    return _rs_sum(fulls_f32, recv1, shard_ids, qs, "rs_sum_" + tag)


def _rs_level1(fulls_f32, fulls_send, tag):
    recv1 = _run_plan(_rs_sibling_plan(fulls_send), "rs_sibling_" + tag)
    return _rs_sums(fulls_f32, recv1, tag)


def _rows(g):
    return g.reshape(g.shape[0] * g.shape[1], g.shape[2])


def kernel(x, norm_mix_pre, norm_mix_post, norm_mlp_pre, norm_mlp_post, w_in, b_gate, conv_w, conv_b, lru_w_a, lru_b_a, lru_w_x, lru_b_x, lru_lambda, pool_w, pool_scale, w_lru_up, w_pool_up, w_o, w_ff1, w_ff2, loss_target, m_norm_mix_pre, m_norm_mix_post, m_norm_mlp_pre, m_norm_mlp_post, m_w_in, m_b_gate, m_conv_w, m_conv_b, m_lru_w_a, m_lru_b_a, m_lru_w_x, m_lru_b_x, m_lru_lambda, m_pool_w, m_pool_scale, m_w_lru_up, m_w_pool_up, m_w_o, m_w_ff1, m_w_ff2, v_norm_mix_pre, v_norm_mix_post, v_norm_mlp_pre, v_norm_mlp_post, v_w_in, v_b_gate, v_conv_w, v_conv_b, v_lru_w_a, v_lru_b_a, v_lru_w_x, v_lru_b_x, v_lru_lambda, v_pool_w, v_pool_scale, v_w_lru_up, v_w_pool_up, v_w_o, v_w_ff1, v_w_ff2):
    t, d = x.shape[1], x.shape[2]
    d_rnn = conv_b.shape[1]
    d_pool = pool_scale.shape[1]
    per = LRU_CB // LRU_HEAD_DIM
    xi, yi, ci = _place()
    me = 4 * xi + 2 * yi + ci

    x2d = x[0]
    tgt = loss_target[0]

    s_in = w_in[0].T.astype(BF16)
    s_lu = w_lru_up[0].astype(BF16)
    s_pu = w_pool_up[0].T.astype(BF16)
    s_o = w_o[0].astype(BF16)
    s_f1 = w_ff1[0].T.astype(BF16)
    s_f2 = w_ff2[0].astype(BF16)
    s_cw = jnp.pad(conv_w[0], ((0, 4), (0, 0)))

    g_in, g_cw = _run_plan(_ag_plan([s_in, s_cw]), "ag_w_in")
    w_int = _rows(g_in)
    conv_w_full = jnp.transpose(g_cw[:, :4, :], (1, 0, 2)).reshape(4, d_rnn)

    wa_bd, wx_bd = lru_w_a[0], lru_w_x[0]
    pw = pool_w[0]
    pw_bf = pw.astype(BF16)

    pool_block = (2 * d_rnn) // d_pool
    ga_block = (2 * d_rnn + d_pool) // 512
    gb_block = ga_block + d // 512
    g_block = d_rnn // 512

    r_f1, r_f2 = s_f1.shape[0], s_f2.shape[0]
    f1_cut = r_f1 // 4
    f2_cut = (3 * r_f2) // 8
    plan = _join([_ag_plan([s_lu, s_pu, s_o]), _ag_plan([s_f1], pieces=[(0, f1_cut)])])
    (proj, h1), got = _norm_proj(x2d, norm_mix_pre, w_int, carry=plan)
    (g_lu, g_pu, g_o), (g_f1,) = plan.split(got)
    w_lu, w_put, w_og = _rows(g_lu), _rows(g_pu), _rows(g_o)
    (y_lru, h, xc), (g_f1,) = _lru_fwd(
        proj, conv_w_full, conv_b, wa_bd, lru_b_a, wx_bd, lru_b_x, lru_lambda,
        carry=_ag_plan([s_f1], pieces=[(f1_cut, r_f1 - f1_cut)], bufs=[g_f1]))
    w_f1t = _rows(g_f1)
    y_pool, p = _pool_fwd(proj, pw_bf, pool_scale, pool_block)
    (br_a, br_b, mix), (g_f2,) = _branch_mix(
        y_lru, y_pool, w_lu, w_put, proj, b_gate, ga_block, gb_block,
        carry=_ag_plan([s_f2], pieces=[(0, f2_cut)]))
    (m, x2, h3), _ = _wo_norm(mix, w_og, x2d, norm_mix_post, norm_mlp_pre)
    (rf,), (g_f2,) = _ff1(
        h3, w_f1t, carry=_ag_plan([s_f2], pieces=[(f2_cut, r_f2 - f2_cut)], bufs=[g_f2]))
    w_f2 = _rows(g_f2)
    dy, df, dg4, loss_part = _ff2_loss(rf, w_f2, x2, norm_mlp_post, tgt)

    (gw_ff2_32, gw_ff2_16), _ = _wgrad(rf, df, "wgrad_ff2", square_a=True)
    (d_f1,), r1_ff2 = _ff2_bwd(df, w_f2, rf, carry=_rs_sibling_plan([gw_ff2_16]))
    ((own_ff2, send_ff2),) = _rs_sums([gw_ff2_32], r1_ff2, "ff2")
    cut2 = (5 * send_ff2.shape[1]) // 16
    (gw_ff1_32, gw_ff1_16), (r2_ff2,) = _wgrad(
        d_f1, h3, "wgrad_ff1", carry=_rs_chips_plan([send_ff2], pieces=[(0, cut2)]))
    plan = _join([_rs_chips_plan([send_ff2], pieces=[(cut2, send_ff2.shape[1] - cut2)], bufs=[r2_ff2]),
                  _rs_sibling_plan([gw_ff1_16])])
    (dx2, dm, dg3, dg2), got = _ff1_bwd_norms(d_f1, w_f1t, dy, x2, norm_mlp_pre, m, norm_mix_post, carry=plan)
    (r2_ff2,), r1_ff1 = plan.split(got)
    ((own_ff1, send_ff1),) = _rs_sums([gw_ff1_32], r1_ff1, "ff1")
    cut = send_ff1.shape[1] // 4
    (gw_o_32, gw_o_16), _ = _wgrad(mix, dm, "wgrad_o")
    (d_br_a, d_br_b, p_ga, p_gb, dbg_a, dbg_b), (r2_ff1,) = _wo_bwd_mix(
        dm, w_og, br_a, br_b, proj, b_gate, ga_block, gb_block,
        carry=_rs_chips_plan([send_ff1], pieces=[(0, cut)]))
    (gw_lu_32, gw_lu_16), _ = _wgrad(y_lru, d_br_a, "wgrad_lru_up")
    (gw_pu_32, gw_pu_16), _ = _wgrad(d_br_b, y_pool, "wgrad_pool_up")
    (dh, p_g), r1_mid = _lru_up_bwd(
        d_br_a, w_lu, proj, h, g_block,
        carry=_rs_sibling_plan([gw_o_16, gw_lu_16, gw_pu_16.reshape(-1, d)]))
    mid = _rs_sums([gw_o_32, gw_lu_32, gw_pu_32.reshape(-1, d)], r1_mid, "mid")
    (p_x, dwa, db_a, dwx, db_x, dlam, dconv_w, dconv_b), (r2_ff1,) = _lru_bwd(
        dh, xc, h, proj, conv_w_full, wa_bd, lru_b_a, wx_bd, lru_b_x, lru_lambda,
        carry=_rs_chips_plan([send_ff1], pieces=[(cut, send_ff1.shape[1] - cut)], bufs=[r2_ff1]))
    p_p, dpool_w, dpool_scale = _pool_bwd(d_br_b, w_put, p, pw, pool_scale)
    parts = [p_x, p_g, p_p, p_ga, p_gb]
    gw_in, r2_mid = _wgrad_parts(parts, h1, "wgrad_in", carry=_rs_chips_plan([s for _, s in mid]))
    tail = _rs_level1([gw_in[0], dpool_w.reshape(N_DEV, -1, POOL_GROUP_DIM), dwa, dwx],
                      [gw_in[1], dpool_w.reshape(N_DEV, -1, POOL_GROUP_DIM), dwa, dwx], "in")
    (grad_x, dg1), r2_tail = _win_bwd_norm(parts, w_int, dx2, x2d, norm_mix_pre,
                                           carry=_rs_chips_plan([s for _, s in tail]))

    def flat2(a):
        return a.reshape(a.shape[0], -1, a.shape[-1])

    fin_small, _ = _finals([
        (flat2(tail[1][0]), flat2(r2_tail[1])), (flat2(tail[2][0]), flat2(r2_tail[2])),
        (flat2(tail[3][0]), flat2(r2_tail[3])),
    ], "rs_finals_small")

    def pad_row(a):
        return jnp.pad(a, ((0, 0), (0, d - a.shape[1])))

    vecs = jnp.concatenate([dg1, dg2, dg3, dg4, dbg_a, dbg_b, dconv_b, db_a, db_x, dlam,
                            pad_row(dpool_scale), dconv_w, pad_row(loss_part)], axis=0)
    assert vecs.shape[0] == N_VEC_ROWS
    fin, (vec_parts, g_pool, g_wa, g_wx) = _finals(
        [(mid[2][0], r2_mid[2]), (own_ff1, r2_ff1)], "rs_finals", carry=_ag_direct_plan([vecs] + fin_small))
    g_w_pool_up = fin[0].reshape(d // N_DEV, d_pool).T
    g_w_ff1 = fin[1].T

    big_names = ["w_in", "w_lru_up", "w_pool_up", "w_o", "w_ff1", "w_ff2"]
    big_w = [w_in[0].T, w_lru_up[0], w_pool_up[0], w_o[0], w_ff1[0], w_ff2[0]]
    big_g = [(tail[0][0], r2_tail[0]), (mid[1][0], r2_mid[1]), g_w_pool_up, (mid[0][0], r2_mid[0]),
             g_w_ff1, (own_ff2, r2_ff2)]
    big_m = [m_w_in[0].T, m_w_lru_up[0], m_w_pool_up[0], m_w_o[0], m_w_ff1[0], m_w_ff2[0]]
    big_v = [v_w_in[0].T, v_w_lru_up[0], v_w_pool_up[0], v_w_o[0], v_w_ff1[0], v_w_ff2[0]]
    big_out = _adamw_big(big_w, big_g, big_m, big_v)
    big_out[0] = tuple(o.T for o in big_out[0])

    small = dict(
        norm_mix_pre=(norm_mix_pre, m_norm_mix_pre, v_norm_mix_pre),
        norm_mix_post=(norm_mix_post, m_norm_mix_post, v_norm_mix_post),
        norm_mlp_pre=(norm_mlp_pre, m_norm_mlp_pre, v_norm_mlp_pre),
        norm_mlp_post=(norm_mlp_post, m_norm_mlp_post, v_norm_mlp_post),
        b_gate=(b_gate, m_b_gate, v_b_gate), conv_w=(conv_w, m_conv_w, v_conv_w),
        conv_b=(conv_b, m_conv_b, v_conv_b), lru_w_a=(lru_w_a, m_lru_w_a, v_lru_w_a),
        lru_b_a=(lru_b_a, m_lru_b_a, v_lru_b_a), lru_w_x=(lru_w_x, m_lru_w_x, v_lru_w_x),
        lru_b_x=(lru_b_x, m_lru_b_x, v_lru_b_x), lru_lambda=(lru_lambda, m_lru_lambda, v_lru_lambda),
        pool_w=(pool_w, m_pool_w, v_pool_w), pool_scale=(pool_scale, m_pool_scale, v_pool_scale))
    loss_row, small_out = _adamw_small(
        vec_parts, g_pool.reshape(pool_w.shape), g_wa.reshape(lru_w_a.shape), g_wx.reshape(lru_w_x.shape),
        jnp.reshape(me, (1,)).astype(jnp.int32), small)
    grads = {n: o[0] for n, o in small_out.items()}
    delta = {n: o[1] for n, o in small_out.items()}
    new_m = {n: o[2] for n, o in small_out.items()}
    new_v = {n: o[3] for n, o in small_out.items()}

    for name, (g, dl, nm, nv) in zip(big_names, big_out):
        grads[name], delta[name], new_m[name], new_v[name] = g[None], dl[None], nm[None], nv[None]

    loss = loss_row[0, 0]
    order = ["norm_mix_pre", "norm_mix_post", "norm_mlp_pre", "norm_mlp_post", "w_in", "b_gate", "conv_w",
             "conv_b", "lru_w_a", "lru_b_a", "lru_w_x", "lru_b_x", "lru_lambda", "pool_w", "pool_scale",
             "w_lru_up", "w_pool_up", "w_o", "w_ff1", "w_ff2"]
    return (loss, grad_x[None], *[grads[n] for n in order], *[delta[n] for n in order],
            *[new_m[n] for n in order], *[new_v[n] for n in order])
```

```python
import functools
import math
import operator
import types

import jax
import jax.numpy as jnp
from jax import lax
from jax.experimental import pallas as pl
from jax.experimental.pallas import tpu as pltpu

F32 = jnp.float32
BF16 = jnp.bfloat16
NORM_EPS = 1e-6
LRU_C = 8.0
N_LRU_HEADS = 16
LRU_HEAD_DIM = 64
POOL_WINDOWS = (2, 4, 8, 16)
POOL_GROUP_DIM = 128
ADAM_LR = 0.001
ADAM_B1 = 0.9
ADAM_B2 = 0.999
ADAM_EPS = 1e-08
ADAM_WD = 0.01
ADAM_STEP = 10
N_DEV = 8
V7X_VMEM_LIMIT_BYTES = 56 * 1024 * 1024
LRU_CB = 256
MESH = pl.DeviceIdType.MESH
ANY = pl.BlockSpec(memory_space=pl.ANY)


def _tile(n, pref):
    t = min(n, pref)
    assert n % t == 0, (n, pref)
    return t


def _dot_nn(a, b):
    return lax.dot_general(a, b, (((1,), (0,)), ((), ())), preferred_element_type=F32)


def _dot_nt(a, b):
    return lax.dot_general(a, b, (((1,), (1,)), ((), ())), preferred_element_type=F32)


def _dot_tn(a, b):
    return lax.dot_general(a, b, (((0,), (0,)), ((), ())), preferred_element_type=F32)


def _row_chunks(n_rows, fn, chunk=256):
    chunk = min(chunk, n_rows)
    assert n_rows % chunk == 0

    def step(r, carry):
        fn(pl.ds(pl.multiple_of(r * chunk, chunk), chunk))
        return carry

    lax.fori_loop(0, n_rows // chunk, step, 0)


def _sig(x):
    return 1.0 / (1.0 + jnp.exp(-x))


def _rms_hat(x):
    r = lax.rsqrt(jnp.mean(x * x, axis=-1, keepdims=True) + NORM_EPS)
    return x * r, r


def _rms_bwd(dn, xhat, r, g):
    q = dn * g
    dx = r * (q - xhat * jnp.mean(q * xhat, axis=-1, keepdims=True))
    dg = jnp.sum(dn * xhat, axis=0, keepdims=True)
    return dx, dg


_GELU_K = math.sqrt(2.0 / math.pi)
_GELU_C = 0.044715


def _gelu_and_grad(g):
    t = jnp.tanh(_GELU_K * (g + _GELU_C * g * g * g))
    val = 0.5 * g * (1.0 + t)
    grad = 0.5 * (1.0 + t) + 0.5 * g * (1.0 - t * t) * (_GELU_K * (1.0 + 3.0 * _GELU_C * g * g))
    return val, grad


def _softplus_neg(lam):
    z = -lam
    e = jnp.exp(-jnp.abs(z))
    u = 1.0 + e
    d = u - 1.0
    l1p = jnp.where(d == 0.0, e, jnp.log(u) * (e / jnp.where(d == 0.0, 1.0, d)))
    return jnp.maximum(z, 0.0) + l1p


def _lru_gates(xc, wa, ba, wx, bx, lam):
    xcb = xc.astype(BF16)
    r = _sig(_dot_nn(xcb, wa) + ba)
    i = _sig(_dot_nn(xcb, wx) + bx)
    sp = _softplus_neg(lam)
    log_a = (-LRU_C) * r * sp
    a = jnp.exp(log_a)
    mult = jnp.sqrt(-jnp.tanh(log_a) * (1.0 + a * a))
    return xcb, r, i, sp, log_a, a, mult


def _place():
    return lax.axis_index("x"), lax.axis_index("y"), lax.axis_index("c")


def _ag_plan(shards, pieces=None, bufs=None):
    na = len(shards)
    n_kinds = 7

    def parts(ins, outs, sems):
        send_sems, recv_sems, local_sems = sems
        x, y, c = _place()
        me, sibling = (x, y, c), (x, y, 1 - c)
        x_nb, y_nb, diag = (1 - x, y), (x, 1 - y), (1 - x, 1 - y)
        relay_src = (c * (1 - x) + (1 - c) * x, c * y + (1 - c) * (1 - y))
        relay_dst = (c * x + (1 - c) * (1 - x), c * (1 - y) + (1 - c) * y)

        def own(a):
            return ins[a] if pieces is None else ins[a].at[pl.ds(*pieces[a])]

        def slot(a, px, py, pc):
            idx = 4 * px + 2 * py + pc
            return outs[a].at[idx] if pieces is None else outs[a].at[idx, pl.ds(*pieces[a])]

        def copy(a, k, block, to, src=None):
            return pltpu.make_async_remote_copy(
                src_ref=slot(a, *block) if src is None else src,
                dst_ref=slot(a, *block),
                send_sem=send_sems.at[a * n_kinds + k],
                recv_sem=recv_sems.at[a * n_kinds + k],
                device_id=to,
                device_id_type=MESH,
            )

        mine = [pltpu.make_async_copy(own(a), slot(a, *me), local_sems.at[a]) for a in range(na)]
        first, second, third = [], [], []
        for a in range(na):
            first += [copy(a, 0, me, sibling, src=own(a)), copy(a, 1, me, (*x_nb, c), src=own(a)),
                      copy(a, 2, me, (*y_nb, c), src=own(a))]
            second += [copy(a, 3, (*relay_src, c), (*relay_dst, c)), copy(a, 4, (*x_nb, c), sibling),
                       copy(a, 5, (*y_nb, c), sibling)]
            third.append(copy(a, 6, (*diag, c), sibling))
        return sibling, c, x_nb, y_nb, diag, copy, mine, first, second, third

    def start(ins, outs, sems):
        _, _, _, _, _, _, mine, first, _, _ = parts(ins, outs, sems)
        for cp in mine + first:
            cp.start()

    def middle(ins, outs, sems):
        _, c, x_nb, y_nb, _, copy, _, _, second, _ = parts(ins, outs, sems)
        for a in range(na):
            copy(a, 1, (*x_nb, c), (*x_nb, c)).wait_recv()
            copy(a, 2, (*y_nb, c), (*y_nb, c)).wait_recv()
        for cp in second:
            cp.start()

    def finish(ins, outs, sems):
        sibling, c, x_nb, y_nb, diag, copy, mine, first, second, third = parts(ins, outs, sems)
        for a in range(na):
            copy(a, 3, (*diag, c), (*diag, c)).wait_recv()
            third[a].start()
        for a in range(na):
            copy(a, 0, sibling, sibling).wait_recv()
            copy(a, 4, (*x_nb, 1 - c), sibling).wait_recv()
            copy(a, 5, (*y_nb, 1 - c), sibling).wait_recv()
            copy(a, 6, (*diag, 1 - c), sibling).wait_recv()
        for cp in first + second + third:
            cp.wait_send()
        for cp in mine:
            cp.wait()

    return types.SimpleNamespace(
        ins=list(shards) + list(bufs or []),
        out_shapes=[jax.ShapeDtypeStruct((N_DEV,) + s.shape, s.dtype) for s in shards],
        sems=[pltpu.SemaphoreType.DMA((n_kinds * na,)), pltpu.SemaphoreType.DMA((n_kinds * na,)),
              pltpu.SemaphoreType.DMA((na,))],
        aliases=[(na + a, a) for a in range(na)] if bufs else [],
        peers=frozenset({"sibling", "neighbours"}), start=start, middle=middle, finish=finish)


def _rs_sibling_plan(fulls):
    na = len(fulls)
    rs = [f.shape[0] // N_DEV for f in fulls]

    def copies(ins, outs, sems):
        send_sems, recv_sems = sems
        x, y, c = _place()
        out = []
        for a in range(na):
            for q in range(4):
                shard = 2 * q + (1 - c)
                out.append(pltpu.make_async_remote_copy(
                    src_ref=ins[a].at[pl.ds(shard * rs[a], rs[a])],
                    dst_ref=outs[a].at[q],
                    send_sem=send_sems.at[a * 4 + q],
                    recv_sem=recv_sems.at[a * 4 + q],
                    device_id=(x, y, 1 - c),
                    device_id_type=MESH,
                ))
        return out

    def start(ins, outs, sems):
        for cp in copies(ins, outs, sems):
            cp.start()

    def finish(ins, outs, sems):
        for cp in copies(ins, outs, sems):
            cp.wait()

    return types.SimpleNamespace(
        ins=list(fulls),
        out_shapes=[jax.ShapeDtypeStruct((4, r) + f.shape[1:], f.dtype) for r, f in zip(rs, fulls)],
        sems=[pltpu.SemaphoreType.DMA((4 * na,)), pltpu.SemaphoreType.DMA((4 * na,))],
        peers=frozenset({"sibling"}), start=start, finish=finish)


def _rs_chips_plan(sends, pieces=None, bufs=None):
    na = len(sends)

    def copies(ins, outs, sems):
        send_sems, recv_sems = sems
        x, y, c = _place()
        chips = [(1 - x, y), (x, 1 - y), (1 - x, 1 - y)]
        out = []
        for a in range(na):
            for k, chip in enumerate(chips):
                rows = (k,) if pieces is None else (k, pl.ds(*pieces[a]))
                out.append(pltpu.make_async_remote_copy(
                    src_ref=ins[a].at[rows],
                    dst_ref=outs[a].at[rows],
                    send_sem=send_sems.at[a * 3 + k],
                    recv_sem=recv_sems.at[a * 3 + k],
                    device_id=(*chip, c),
                    device_id_type=MESH,
                ))
        return out

    def start(ins, outs, sems):
        for cp in copies(ins, outs, sems):
            cp.start()

    def finish(ins, outs, sems):
        for cp in copies(ins, outs, sems):
            cp.wait()

    return types.SimpleNamespace(
        ins=list(sends) + list(bufs or []),
        out_shapes=[jax.ShapeDtypeStruct(s.shape, s.dtype) for s in sends],
        sems=[pltpu.SemaphoreType.DMA((3 * na,)), pltpu.SemaphoreType.DMA((3 * na,))],
        aliases=[(na + a, a) for a in range(na)] if bufs else [],
        peers=frozenset({"chips"}), start=start, finish=finish)


def _join(plans):
    ins, outs, sems, aliases, offs = [], [], [], [], []
    for p in plans:
        offs.append((len(ins), len(outs), len(sems)))
        aliases += [(len(ins) + ci, len(outs) + co) for ci, co in getattr(p, "aliases", [])]
        ins += p.ins
        outs += p.out_shapes
        sems += p.sems

    def cut(p, off, i, o, s):
        return (i[off[0]:off[0] + len(p.ins)], o[off[1]:off[1] + len(p.out_shapes)],
                s[off[2]:off[2] + len(p.sems)])

    def start(i, o, s):
        for p, off in zip(plans, offs):
            p.start(*cut(p, off, i, o, s))

    def middle(i, o, s):
        for p, off in zip(plans, offs):
            if getattr(p, "middle", None) is not None:
                p.middle(*cut(p, off, i, o, s))

    def finish(i, o, s):
        for p, off in zip(plans, offs):
            p.finish(*cut(p, off, i, o, s))

    def split(results):
        return [list(results[off[1]:off[1] + len(p.out_shapes)]) for p, off in zip(plans, offs)]

    return types.SimpleNamespace(ins=ins, out_shapes=outs, sems=sems, aliases=aliases,
                                 peers=frozenset().union(*[p.peers for p in plans]),
                                 start=start, middle=middle, finish=finish, split=split)


COLLECTIVE_ID = {frozenset({"sibling"}): 0, frozenset({"chips"}): 1, frozenset({"sibling", "chips"}): 2,
                 frozenset({"sibling", "neighbours"}): 3}


def _handshake(peers):
    x, y, c = _place()
    devs = []
    if "sibling" in peers:
        devs.append((x, y, 1 - c))
    if "neighbours" in peers:
        devs += [(1 - x, y, c), (x, 1 - y, c)]
    if "chips" in peers:
        assert "neighbours" not in peers
        devs += [(1 - x, y, c), (x, 1 - y, c), (1 - x, 1 - y, c)]
    barrier = pltpu.get_barrier_semaphore()
    for dev in devs:
        pl.semaphore_signal(barrier, inc=1, device_id=dev, device_id_type=MESH)
    pl.semaphore_wait(barrier, len(devs))


def _in_hbm(args):
    return [pltpu.with_memory_space_constraint(a, pltpu.HBM) for a in args]


def _run_plan(plan, name):
    n_in, n_out = len(plan.ins), len(plan.out_shapes)

    def body(*refs):
        ins, outs, sems = refs[:n_in], refs[n_in:n_in + n_out], refs[n_in + n_out:]
        _handshake(plan.peers)
        plan.start(ins, outs, sems)
        if getattr(plan, "middle", None) is not None:
            plan.middle(ins, outs, sems)
        plan.finish(ins, outs, sems)

    return pl.pallas_call(
        body,
        name=name,
        in_specs=[ANY] * n_in,
        out_specs=[ANY] * n_out,
        out_shape=plan.out_shapes,
        scratch_shapes=plan.sems,
        input_output_aliases=dict(getattr(plan, "aliases", [])),
        compiler_params=pltpu.CompilerParams(collective_id=COLLECTIVE_ID[plan.peers]),
    )(*_in_hbm(plan.ins))


def _call(body, *, name, grid, in_specs, out_specs, out_shape, args, scratch_shapes=(), aliases=None,
          carry=None):
    n_in, n_out, n_scr = len(in_specs), len(out_shape), len(scratch_shapes)
    params = pltpu.CompilerParams(
        dimension_semantics=("arbitrary",) * len(grid), vmem_limit_bytes=V7X_VMEM_LIMIT_BYTES)
    if carry is None:
        outs = pl.pallas_call(
            body, name=name, grid=grid, in_specs=list(in_specs), out_specs=list(out_specs),
            out_shape=list(out_shape), scratch_shapes=list(scratch_shapes),
            input_output_aliases=aliases or {}, compiler_params=params)(*_in_hbm(args))
        return list(outs), []
    c_in, c_out = len(carry.ins), len(carry.out_shapes)

    def full(*refs):
        p = 0
        ins = refs[p:p + n_in]
        p += n_in
        cins = refs[p:p + c_in]
        p += c_in
        outs = refs[p:p + n_out]
        p += n_out
        couts = refs[p:p + c_out]
        p += c_out
        scr = refs[p:p + n_scr]
        csems = refs[p + n_scr:]
        ids = [pl.program_id(a) for a in range(len(grid))]
        first = functools.reduce(operator.and_, [i == 0 for i in ids])
        last = functools.reduce(operator.and_, [i == g - 1 for i, g in zip(ids, grid)])

        @pl.when(first)
        def _():
            _handshake(carry.peers)
            carry.start(cins, couts, csems)

        if getattr(carry, "middle", None) is not None:
            n_steps = math.prod(grid)
            flat = functools.reduce(lambda acc, ig: acc * ig[1] + ig[0], zip(ids, grid), 0)

            @pl.when(flat == (2 * n_steps) // 3)
            def _():
                carry.middle(cins, couts, csems)

        body(*ins, *outs, *scr)

        @pl.when(last)
        def _():
            carry.finish(cins, couts, csems)

    all_aliases = dict(aliases or {})
    all_aliases.update({n_in + ci: n_out + co for ci, co in getattr(carry, "aliases", [])})
    params = pltpu.CompilerParams(
        dimension_semantics=("arbitrary",) * len(grid), vmem_limit_bytes=V7X_VMEM_LIMIT_BYTES,
        collective_id=COLLECTIVE_ID[carry.peers])
    outs = pl.pallas_call(
        full, name=name, grid=grid,
        in_specs=list(in_specs) + [ANY] * c_in,
        out_specs=list(out_specs) + [ANY] * c_out,
        out_shape=list(out_shape) + list(carry.out_shapes),
        scratch_shapes=list(scratch_shapes) + list(carry.sems),
        input_output_aliases=all_aliases, compiler_params=params)(*_in_hbm(args), *_in_hbm(carry.ins))
    return list(outs[:n_out]), list(outs[n_out:])


def _norm_proj(x, g1, w_int, carry=None):
    t, d = x.shape
    n = w_int.shape[0]
    tt, tn = _tile(t, 2048), _tile(n, 512)

    def body(x_ref, g_ref, w_ref, proj_ref, h1_ref, h1_s):
        @pl.when(pl.program_id(1) == 0)
        def _():
            def norm_rows(rows):
                xhat, _ = _rms_hat(x_ref[rows, :])
                h = (xhat * g_ref[...]).astype(BF16)
                h1_s[rows, :] = h
                h1_ref[rows, :] = h

            _row_chunks(tt, norm_rows)

        proj_ref[...] = _dot_nt(h1_s[...], w_ref[...]).astype(BF16)

    return _call(
        body, name="norm_proj", grid=(t // tt, n // tn),
        in_specs=[
            pl.BlockSpec((tt, d), lambda i, j: (i, 0)),
            pl.BlockSpec((1, d), lambda i, j: (0, 0)),
            pl.BlockSpec((tn, d), lambda i, j: (j, 0)),
        ],
        out_specs=[
            pl.BlockSpec((tt, tn), lambda i, j: (i, j)),
            pl.BlockSpec((tt, d), lambda i, j: (i, 0)),
        ],
        out_shape=[jax.ShapeDtypeStruct((t, n), BF16), jax.ShapeDtypeStruct((t, d), BF16)],
        scratch_shapes=[pltpu.VMEM((tt, d), BF16)],
        args=(x, g1, w_int), carry=carry)


def _scan_rows(av, bv, reverse):
    tc = av.shape[0]
    row = lax.broadcasted_iota(jnp.int32, av.shape, 0)
    s = 1
    while s < tc:
        if s < 8:
            keep = (row < tc - s) if reverse else (row >= s)
            shift = (tc - s) if reverse else s
            a_sh = jnp.where(keep, pltpu.roll(av, shift, 0), 1.0)
            b_sh = jnp.where(keep, pltpu.roll(bv, shift, 0), 0.0)
            bv = av * b_sh + bv
            av = av * a_sh
        elif reverse:
            bv = jnp.concatenate([av[:tc - s] * bv[s:] + bv[:tc - s], bv[tc - s:]], axis=0)
            av = jnp.concatenate([av[:tc - s] * av[s:], av[tc - s:]], axis=0)
        else:
            bv = jnp.concatenate([bv[:s], av[s:] * bv[:tc - s] + bv[s:]], axis=0)
            av = jnp.concatenate([av[:s], av[s:] * av[:tc - s]], axis=0)
        s *= 2
    return av, bv


def _fill_block_diag(w_ref, bd_ref):
    bd_ref[...] = jnp.zeros_like(bd_ref)
    hd = LRU_HEAD_DIM
    for k in range(w_ref.shape[0]):
        bd_ref[k * hd:(k + 1) * hd, k * hd:(k + 1) * hd] = w_ref[k].astype(BF16)


def _lru_fwd(proj, conv_w, conv_b, w_a, b_a, w_x, b_x, lam, carry=None):
    t = proj.shape[0]
    dr = conv_b.shape[1]
    cb = LRU_CB
    tc = _tile(t, 256)
    ncb, ntc = dr // cb, t // tc

    def body(xp_ref, g_ref, cw_ref, cb_ref, wa_ref, ba_ref, wx_ref, bx_ref, lam_ref,
             y_ref, h_ref, xc_ref, prevx_s, hlast_s, wa_s, wx_s):
        c = pl.program_id(1)

        @pl.when(c == 0)
        def _():
            prevx_s[...] = jnp.zeros_like(prevx_s)
            hlast_s[...] = jnp.zeros_like(hlast_s)
            _fill_block_diag(wa_ref, wa_s)
            _fill_block_diag(wx_ref, wx_s)

        x = xp_ref[...].astype(F32)
        prev = prevx_s[...]
        row = lax.broadcasted_iota(jnp.int32, x.shape, 0)

        def sh(j):
            return jnp.where(row >= j, pltpu.roll(x, j, 0), pltpu.roll(prev, j, 0))

        xc = (cb_ref[...] + cw_ref[0:1, :] * sh(3) + cw_ref[1:2, :] * sh(2)
              + cw_ref[2:3, :] * sh(1) + cw_ref[3:4, :] * x)
        prevx_s[...] = x
        xc_ref[...] = xc
        _, _, i, _, _, a, mult = _lru_gates(xc, wa_s[...], ba_ref[...], wx_s[...], bx_ref[...],
                                            lam_ref[...])
        av, bv = _scan_rows(a, mult * (i * xc), reverse=False)
        h = av * hlast_s[...] + bv
        h_ref[...] = h
        hlast_s[...] = h_ref[tc - 1:tc, :]
        gel, _ = _gelu_and_grad(g_ref[...].astype(F32))
        y_ref[...] = (h * gel).astype(BF16)

    vec = pl.BlockSpec((1, cb), lambda j, c: (0, j))
    blk = pl.BlockSpec((tc, cb), lambda j, c: (c, j))
    mat = pl.BlockSpec((cb // LRU_HEAD_DIM, LRU_HEAD_DIM, LRU_HEAD_DIM), lambda j, c: (j, 0, 0))
    return _call(
        body, name="lru_fwd", grid=(ncb, ntc),
        in_specs=[
            blk,
            pl.BlockSpec((tc, cb), lambda j, c: (c, ncb + j)),
            pl.BlockSpec((4, cb), lambda j, c: (0, j)),
            vec, mat, vec, mat, vec, vec,
        ],
        out_specs=[blk, blk, blk],
        out_shape=[
            jax.ShapeDtypeStruct((t, dr), BF16),
            jax.ShapeDtypeStruct((t, dr), F32),
            jax.ShapeDtypeStruct((t, dr), F32),
        ],
        scratch_shapes=[pltpu.VMEM((tc, cb), F32), pltpu.VMEM((1, cb), F32),
                        pltpu.VMEM((cb, cb), BF16), pltpu.VMEM((cb, cb), BF16)],
        args=(proj, proj, conv_w, conv_b, w_a, b_a, w_x, b_x, lam), carry=carry)


def _pool_select(col, vals):
    out = vals[3]
    for g in (2, 1, 0):
        out = jnp.where(col < (g + 1) * POOL_GROUP_DIM, vals[g], out)
    return out


def _pool_fwd(proj, pool_w, pool_scale, col_block):
    t = proj.shape[0]
    dp = pool_scale.shape[1]
    tc = _tile(t, 256)
    ntc = t // tc

    def body(x_ref, w_ref, sc_ref, y_ref, p_ref, px, p2, p4, p8):
        c = pl.program_id(0)

        @pl.when(c == 0)
        def _():
            for s in (px, p2, p4, p8):
                s[...] = jnp.zeros_like(s)

        x = x_ref[...].astype(F32)
        row = lax.broadcasted_iota(jnp.int32, x.shape, 0)
        col = lax.broadcasted_iota(jnp.int32, x.shape, 1)

        def sh(v, pv, j):
            return jnp.where(row >= j, pltpu.roll(v, j, 0), pltpu.roll(pv[...], j, 0))

        s2 = x + sh(x, px, 1)
        s4 = s2 + sh(s2, p2, 2)
        s8 = s4 + sh(s4, p4, 4)
        s16 = s8 + sh(s8, p8, 8)
        px[...] = x
        p2[...] = s2
        p4[...] = s4
        p8[...] = s8
        wsum = _pool_select(col, (s2, s4, s8, s16))
        win = _pool_select(col, POOL_WINDOWS)
        cnt = jnp.minimum(c * tc + row + 1, win).astype(F32)
        p = wsum / cnt - x
        pb = p.astype(BF16)
        p_ref[...] = pb
        for g in range(len(POOL_WINDOWS)):
            sl = slice(g * POOL_GROUP_DIM, (g + 1) * POOL_GROUP_DIM)
            yg = _dot_nn(pb[:, sl], w_ref[g]) * sc_ref[:, sl]
            y_ref[:, sl] = yg.astype(BF16)

    return _call(
        body, name="pool_fwd", grid=(ntc,),
        in_specs=[
            pl.BlockSpec((tc, dp), lambda c: (c, col_block)),
            pl.BlockSpec(pool_w.shape, lambda c: (0, 0, 0)),
            pl.BlockSpec((1, dp), lambda c: (0, 0)),
        ],
        out_specs=[pl.BlockSpec((tc, dp), lambda c: (c, 0))] * 2,
        out_shape=[jax.ShapeDtypeStruct((t, dp), BF16)] * 2,
        scratch_shapes=[pltpu.VMEM((tc, dp), F32)] * 4,
        args=(proj, pool_w, pool_scale))[0]


def _branch_mix(y_lru, y_pool, w_lru_up, w_pool_upb, proj, b_gate, ga_block, gb_block, carry=None):
    t, d = y_lru.shape
    dp = y_pool.shape[1]
    bw = w_pool_upb.shape[2]
    tt, tn = _tile(t, 1024), 512
    nj = d // tn

    def body(yl_ref, yp_ref, wl_ref, wp_ref, ga_ref, gb_ref, ba_ref, bb_ref, bra_ref, brb_ref, mix_ref):
        br_a = _dot_nn(yl_ref[...], wl_ref[...])
        wp = jnp.concatenate([wp_ref[b] for b in range(tn // bw)], axis=1)
        br_b = _dot_nn(yp_ref[...], wp)
        bra_ref[...] = br_a.astype(BF16)
        brb_ref[...] = br_b.astype(BF16)
        ga = _sig(ga_ref[...].astype(F32) + ba_ref[...])
        gb = _sig(gb_ref[...].astype(F32) + bb_ref[...])
        mix_ref[...] = (ga * br_a + gb * br_b).astype(BF16)

    out = pl.BlockSpec((tt, tn), lambda j, i: (i, j))
    return _call(
        body, name="branch_mix", grid=(nj, t // tt),
        in_specs=[
            pl.BlockSpec((tt, d), lambda j, i: (i, 0)),
            pl.BlockSpec((tt, dp), lambda j, i: (i, 0)),
            pl.BlockSpec((d, tn), lambda j, i: (0, j)),
            pl.BlockSpec((tn // bw, dp, bw), lambda j, i: (j, 0, 0)),
            pl.BlockSpec((tt, tn), lambda j, i: (i, ga_block + j)),
            pl.BlockSpec((tt, tn), lambda j, i: (i, gb_block + j)),
            pl.BlockSpec((1, tn), lambda j, i: (0, j)),
            pl.BlockSpec((1, tn), lambda j, i: (0, nj + j)),
        ],
        out_specs=[out, out, out],
        out_shape=[jax.ShapeDtypeStruct((t, d), BF16)] * 3,
        args=(y_lru, y_pool, w_lru_up, w_pool_upb, proj, proj, b_gate, b_gate), carry=carry)


def _wo_norm(mix, w_o, x, g2, g3, carry=None):
    t, d = x.shape
    tt = _tile(t, 512)

    def body(mix_ref, w_ref, x_ref, g2_ref, g3_ref, m_ref, x2_ref, h3_ref):
        m = _dot_nn(mix_ref[...], w_ref[...])
        m_ref[...] = m
        mhat, _ = _rms_hat(m)
        x2 = x_ref[...] + mhat * g2_ref[...]
        x2_ref[...] = x2
        xhat, _ = _rms_hat(x2)
        h3_ref[...] = (xhat * g3_ref[...]).astype(BF16)

    row = pl.BlockSpec((tt, d), lambda i: (i, 0))
    vec = pl.BlockSpec((1, d), lambda i: (0, 0))
    return _call(
        body, name="wo_norm", grid=(t // tt,),
        in_specs=[row, pl.BlockSpec((d, d), lambda i: (0, 0)), row, vec, vec],
        out_specs=[row, row, row],
        out_shape=[
            jax.ShapeDtypeStruct((t, d), F32),
            jax.ShapeDtypeStruct((t, d), F32),
            jax.ShapeDtypeStruct((t, d), BF16),
        ],
        args=(mix, w_o, x, g2, g3), carry=carry)


def _ff1(h3, w_ff1b, carry=None):
    t, d = h3.shape
    nb, _, tn = w_ff1b.shape
    tt = _tile(t, 2048)

    def body(h_ref, w_ref, rf_ref):
        rf_ref[...] = jnp.maximum(_dot_nn(h_ref[...], w_ref[...]), 0.0).astype(BF16)

    out = pl.BlockSpec((tt, tn), lambda i, j: (i, j))
    return _call(
        body, name="ff1", grid=(t // tt, nb),
        in_specs=[pl.BlockSpec((tt, d), lambda i, j: (i, 0)), pl.BlockSpec((None, d, tn), lambda i, j: (j, 0, 0))],
        out_specs=[out],
        out_shape=[jax.ShapeDtypeStruct((t, nb * tn), BF16)],
        args=(h3, w_ff1b), carry=carry)


def _ff2_loss(rf, w_ff2, x2, g4, target):
    t, k = rf.shape
    d = x2.shape[1]
    tt, tk = _tile(t, 1024), _tile(k, 1024)
    nk = k // tk

    def body(a_ref, w_ref, x2_ref, g_ref, tg_ref, dy_ref, df_ref, dg_ref, loss_ref, acc):
        i, kk = pl.program_id(0), pl.program_id(1)

        @pl.when(kk == 0)
        def _():
            acc[...] = jnp.zeros_like(acc)

        @pl.when((i == 0) & (kk == 0))
        def _():
            dg_ref[...] = jnp.zeros_like(dg_ref)
            loss_ref[...] = jnp.zeros_like(loss_ref)

        rf_tile = a_ref[...]
        acc[...] += _dot_nn(rf_tile * rf_tile, w_ref[...])

        @pl.when(kk == nk - 1)
        def _():
            def tail(rows):
                fhat, r = _rms_hat(acc[rows, :])
                g = g_ref[...]
                e = x2_ref[rows, :] + fhat * g - tg_ref[rows, :]
                loss_ref[...] += 0.5 * jnp.sum(jnp.mean(e * e, axis=-1, keepdims=True))
                dy = e * (1.0 / d)
                dy_ref[rows, :] = dy.astype(BF16)
                df, dg = _rms_bwd(dy, fhat, r, g)
                df_ref[rows, :] = df.astype(BF16)
                dg_ref[...] += dg

            _row_chunks(tt, tail)

    row = pl.BlockSpec((tt, d), lambda i, kk: (i, 0))
    vec = pl.BlockSpec((1, d), lambda i, kk: (0, 0))
    return _call(
        body, name="ff2_loss", grid=(t // tt, nk),
        in_specs=[
            pl.BlockSpec((tt, tk), lambda i, kk: (i, kk)),
            pl.BlockSpec((tk, d), lambda i, kk: (kk, 0)),
            row, vec, row,
        ],
        out_specs=[row, row, vec, pl.BlockSpec((1, 128), lambda i, kk: (0, 0))],
        out_shape=[
            jax.ShapeDtypeStruct((t, d), BF16),
            jax.ShapeDtypeStruct((t, d), BF16),
            jax.ShapeDtypeStruct((1, d), F32),
            jax.ShapeDtypeStruct((1, 128), F32),
        ],
        scratch_shapes=[pltpu.VMEM((tt, d), F32)],
        args=(rf, w_ff2, x2, g4, target))[0]


def _ff2_bwd(df, w_ff2, rf, carry=None):
    t, d = df.shape
    n = w_ff2.shape[0]
    tt, tn = _tile(t, 2048), _tile(n, 512)

    def body(df_ref, w_ref, rf_ref, out_ref):
        d_act = _dot_nt(df_ref[...], w_ref[...])
        out_ref[...] = (d_act * (2.0 * rf_ref[...].astype(F32))).astype(BF16)

    blk = pl.BlockSpec((tt, tn), lambda i, j: (i, j))
    return _call(
        body, name="ff2_bwd", grid=(t // tt, n // tn),
        in_specs=[pl.BlockSpec((tt, d), lambda i, j: (i, 0)), pl.BlockSpec((tn, d), lambda i, j: (j, 0)), blk],
        out_specs=[blk],
        out_shape=[jax.ShapeDtypeStruct((t, n), BF16)],
        args=(df, w_ff2, rf), carry=carry)


def _wgrad(a, b, name, prev=None, row_off=0, rows=None, carry=None, square_a=False):
    t, m = a.shape
    n = b.shape[1]
    rows = m if rows is None else rows
    tm, tk = _tile(m, 512), _tile(t, 2048)
    nk = t // tk
    assert row_off % tm == 0
    off = row_off // tm

    def body(*refs):
        a_ref, b_ref = refs[0], refs[1]
        o32_ref, o16_ref, acc = refs[-3], refs[-2], refs[-1]
        kk = pl.program_id(1)

        @pl.when(kk == 0)
        def _():
            acc[...] = jnp.zeros_like(acc)

        a_tile = a_ref[...]
        acc[...] += _dot_tn(a_tile * a_tile if square_a else a_tile, b_ref[...])

        @pl.when(kk == nk - 1)
        def _():
            o32_ref[...] = acc[...]
            o16_ref[...] = acc[...].astype(BF16)

    in_specs = [pl.BlockSpec((tk, tm), lambda i, kk: (kk, i)), pl.BlockSpec((tk, n), lambda i, kk: (kk, 0))]
    args = [a, b]
    aliases = {}
    if prev is not None:
        in_specs += [ANY, ANY]
        args += list(prev)
        aliases = {2: 0, 3: 1}
    out = pl.BlockSpec((tm, n), lambda i, kk: (off + i, 0))
    return _call(
        body, name=name, grid=(m // tm, nk),
        in_specs=in_specs, out_specs=[out, out],
        out_shape=[jax.ShapeDtypeStruct((rows, n), F32), jax.ShapeDtypeStruct((rows, n), BF16)],
        scratch_shapes=[pltpu.VMEM((tm, n), F32)],
        aliases=aliases, args=args, carry=carry)


def _wgrad_parts(parts, b, name, carry=None):
    t, n = b.shape
    tm = 512
    bounds = []
    lo = 0
    for part in parts:
        assert part.shape[0] == t and part.shape[1] % tm == 0
        bounds.append((lo, lo + part.shape[1] // tm))
        lo += part.shape[1] // tm
    nm = lo
    np_ = len(parts)

    def body(*refs):
        p_refs, b_ref, o32_ref, o16_ref = refs[:np_], refs[np_], refs[np_ + 1], refs[np_ + 2]
        i = pl.program_id(0)
        for (lo_p, hi_p), p_ref in zip(bounds, p_refs):
            @pl.when((i >= lo_p) & (i < hi_p))
            def _(p_ref=p_ref):
                res = _dot_tn(p_ref[...], b_ref[...])
                o32_ref[...] = res
                o16_ref[...] = res.astype(BF16)

    def part_spec(lo_p, hi_p):
        return pl.BlockSpec((t, tm), lambda i: (0, jnp.clip(i - lo_p, 0, hi_p - lo_p - 1)))

    out = pl.BlockSpec((tm, n), lambda i: (i, 0))
    return _call(
        body, name=name, grid=(nm,),
        in_specs=[part_spec(lo_p, hi_p) for lo_p, hi_p in bounds] + [pl.BlockSpec((t, n), lambda i: (0, 0))],
        out_specs=[out, out],
        out_shape=[jax.ShapeDtypeStruct((nm * tm, n), F32), jax.ShapeDtypeStruct((nm * tm, n), BF16)],
        args=(*parts, b), carry=carry)


def _wgrad_cols(a, b, bw, tn, name, carry=None):
    t, m = a.shape
    n = b.shape[1]
    per_step = tn // bw

    def body(a_ref, b_ref, o32_ref, o16_ref):
        res = _dot_tn(a_ref[...], b_ref[...])
        for blk in range(per_step):
            part = res[:, blk * bw:(blk + 1) * bw]
            o32_ref[blk] = part
            o16_ref[blk] = part.astype(BF16)

    out = pl.BlockSpec((per_step, m, bw), lambda j: (j, 0, 0))
    return _call(
        body, name=name, grid=(n // tn,),
        in_specs=[pl.BlockSpec((t, m), lambda j: (0, 0)), pl.BlockSpec((t, tn), lambda j: (0, j))],
        out_specs=[out, out],
        out_shape=[jax.ShapeDtypeStruct((n // bw, m, bw), F32), jax.ShapeDtypeStruct((n // bw, m, bw), BF16)],
        args=(a, b), carry=carry)


def _ff1_bwd_norms(d_f1, w_ff1b, dy, x2, g3, m, g2, carry=None):
    t, k = d_f1.shape
    d = x2.shape[1]
    bw = w_ff1b.shape[2]
    per_step = 2
    tt, tk = _tile(t, 1024), per_step * bw
    nk = k // tk

    def body(a_ref, w_ref, dy_ref, x2_ref, g3_ref, m_ref, g2_ref, dx2_ref, dm_ref, dg3_ref, dg2_ref, acc):
        i, kk = pl.program_id(0), pl.program_id(1)

        @pl.when(kk == 0)
        def _():
            acc[...] = jnp.zeros_like(acc)

        @pl.when((i == 0) & (kk == 0))
        def _():
            dg3_ref[...] = jnp.zeros_like(dg3_ref)
            dg2_ref[...] = jnp.zeros_like(dg2_ref)

        a_tile = a_ref[...]
        for b in range(per_step):
            acc[...] += _dot_nt(a_tile[:, b * bw:(b + 1) * bw], w_ref[b])

        @pl.when(kk == nk - 1)
        def _():
            def tail(rows):
                xhat, r3 = _rms_hat(x2_ref[rows, :])
                dx, dg3 = _rms_bwd(acc[rows, :], xhat, r3, g3_ref[...])
                dx2 = dy_ref[rows, :].astype(F32) + dx
                dx2_ref[rows, :] = dx2
                dg3_ref[...] += dg3
                mhat, r2 = _rms_hat(m_ref[rows, :])
                dm, dg2 = _rms_bwd(dx2, mhat, r2, g2_ref[...])
                dm_ref[rows, :] = dm.astype(BF16)
                dg2_ref[...] += dg2

            _row_chunks(tt, tail)

    row = pl.BlockSpec((tt, d), lambda i, kk: (i, 0))
    vec = pl.BlockSpec((1, d), lambda i, kk: (0, 0))
    return _call(
        body, name="ff1_bwd_norms", grid=(t // tt, nk),
        in_specs=[
            pl.BlockSpec((tt, tk), lambda i, kk: (i, kk)),
            pl.BlockSpec((per_step, d, bw), lambda i, kk: (kk, 0, 0)),
            row, row, vec, row, vec,
        ],
        out_specs=[row, row, vec, vec],
        out_shape=[
            jax.ShapeDtypeStruct((t, d), F32),
            jax.ShapeDtypeStruct((t, d), BF16),
            jax.ShapeDtypeStruct((1, d), F32),
            jax.ShapeDtypeStruct((1, d), F32),
        ],
        scratch_shapes=[pltpu.VMEM((tt, d), F32)],
        args=(d_f1, w_ff1b, dy, x2, g3, m, g2), carry=carry)


def _wo_bwd_mix(dm, w_o, br_a, br_b, proj, b_gate, ga_block, gb_block, carry=None):
    t, d = dm.shape
    tt, tn = _tile(t, 1024), 512
    nj = d // tn

    def body(dm_ref, w_ref, bra_ref, brb_ref, ga_ref, gb_ref, ba_ref, bb_ref,
             dbra_ref, dbrb_ref, dga_ref, dgb_ref, dba_ref, dbb_ref):
        i = pl.program_id(1)

        @pl.when(i == 0)
        def _():
            dba_ref[...] = jnp.zeros_like(dba_ref)
            dbb_ref[...] = jnp.zeros_like(dbb_ref)

        d_mix = _dot_nt(dm_ref[...], w_ref[...])
        ga = _sig(ga_ref[...].astype(F32) + ba_ref[...])
        gb = _sig(gb_ref[...].astype(F32) + bb_ref[...])
        dbra_ref[...] = (d_mix * ga).astype(BF16)
        dbrb_ref[...] = (d_mix * gb).astype(BF16)
        dga = d_mix * bra_ref[...].astype(F32) * (ga * (1.0 - ga))
        dgb = d_mix * brb_ref[...].astype(F32) * (gb * (1.0 - gb))
        dga_ref[...] = dga.astype(BF16)
        dgb_ref[...] = dgb.astype(BF16)
        dba_ref[...] += jnp.sum(dga, axis=0, keepdims=True)
        dbb_ref[...] += jnp.sum(dgb, axis=0, keepdims=True)

    blk = pl.BlockSpec((tt, tn), lambda j, i: (i, j))
    vec = pl.BlockSpec((1, tn), lambda j, i: (0, j))
    return _call(
        body, name="wo_bwd_mix", grid=(nj, t // tt),
        in_specs=[
            pl.BlockSpec((tt, d), lambda j, i: (i, 0)),
            pl.BlockSpec((tn, d), lambda j, i: (j, 0)),
            blk, blk,
            pl.BlockSpec((tt, tn), lambda j, i: (i, ga_block + j)),
            pl.BlockSpec((tt, tn), lambda j, i: (i, gb_block + j)),
            vec,
            pl.BlockSpec((1, tn), lambda j, i: (0, nj + j)),
        ],
        out_specs=[blk, blk, blk, blk, vec, vec],
        out_shape=[jax.ShapeDtypeStruct((t, d), BF16)] * 4 + [jax.ShapeDtypeStruct((1, d), F32)] * 2,
        args=(dm, w_o, br_a, br_b, proj, proj, b_gate, b_gate), carry=carry)


def _lru_up_bwd(d_br_a, w_lru_up, proj, h, g_block, carry=None):
    t, d = d_br_a.shape
    tt, tn = _tile(t, 1024), 512

    def body(a_ref, w_ref, g_ref, h_ref, dh_ref, dg_ref):
        d_y = _dot_nt(a_ref[...], w_ref[...])
        gel, gel_grad = _gelu_and_grad(g_ref[...].astype(F32))
        dh_ref[...] = d_y * gel
        dg_ref[...] = (d_y * h_ref[...] * gel_grad).astype(BF16)

    blk = pl.BlockSpec((tt, tn), lambda i, j: (i, j))
    return _call(
        body, name="lru_up_bwd", grid=(t // tt, d // tn),
        in_specs=[
            pl.BlockSpec((tt, d), lambda i, j: (i, 0)),
            pl.BlockSpec((tn, d), lambda i, j: (j, 0)),
            pl.BlockSpec((tt, tn), lambda i, j: (i, g_block + j)),
            blk,
        ],
        out_specs=[blk, blk],
        out_shape=[jax.ShapeDtypeStruct((t, d), F32), jax.ShapeDtypeStruct((t, d), BF16)],
        args=(d_br_a, w_lru_up, proj, h), carry=carry)


def _lru_bwd(dh, xc, h, proj, conv_w, w_a, b_a, w_x, b_x, lam, carry=None):
    t, dr = dh.shape
    cb = LRU_CB
    hd = LRU_HEAD_DIM
    per = cb // hd
    tc = _tile(t, 256)
    ncb, ntc = dr // cb, t // tc

    def body(dh_ref, xc_ref, h_ref, hp_ref, xp_ref, cw_ref, wa_ref, ba_ref, wx_ref, bx_ref, lam_ref,
             dxp_ref, dwa_ref, dba_ref, dwx_ref, dbx_ref, dlam_ref, dcw_ref, dcb_ref,
             nextd_s, anext_s, gnext_s, tmp_s, wa_s, wx_s):
        c = pl.program_id(1)
        rc = ntc - 1 - c

        @pl.when(c == 0)
        def _():
            nextd_s[...] = jnp.zeros_like(nextd_s)
            anext_s[...] = jnp.zeros_like(anext_s)
            gnext_s[...] = jnp.zeros_like(gnext_s)
            for ref in (dwa_ref, dba_ref, dwx_ref, dbx_ref, dlam_ref, dcw_ref, dcb_ref):
                ref[...] = jnp.zeros_like(ref)
            _fill_block_diag(wa_ref, wa_s)
            _fill_block_diag(wx_ref, wx_s)

        xc = xc_ref[...]
        wa, wx, lam = wa_s[...], wx_s[...], lam_ref[...]
        xcb, r, i, sp, log_a, a, mult = _lru_gates(xc, wa, ba_ref[...], wx, bx_ref[...], lam)
        row = lax.broadcasted_iota(jnp.int32, xc.shape, 0)
        h = h_ref[...]
        hp = jnp.where(rc == 0, 0.0, hp_ref[...])
        hprev = jnp.where(row >= 1, pltpu.roll(h, 1, 0), pltpu.roll(hp, 1, 0))

        def up(v, nv, j):
            return jnp.where(row < tc - j, pltpu.roll(v, tc - j, 0), nv)

        av, bv = _scan_rows(up(a, anext_s[...], 1), dh_ref[...], reverse=True)
        gt = av * gnext_s[...] + bv
        tmp_s[...] = gt
        gnext_s[...] = tmp_s[0:1, :]
        tmp_s[...] = a
        anext_s[...] = tmp_s[0:1, :]

        da = gt * hprev
        ixc = i * xc
        d_mult = gt * ixc
        d_i = gt * mult * xc
        d_xc = gt * mult * i
        d_log_a = da * a - d_mult * (a * a) / mult
        d_pre_r = (d_log_a * ((-LRU_C) * sp)) * (r * (1.0 - r))
        d_pre_i = d_i * (i * (1.0 - i))
        d_sp = jnp.sum(d_log_a * ((-LRU_C) * r), axis=0, keepdims=True)
        dlam_ref[...] += d_sp * (-1.0 / (1.0 + jnp.exp(lam)))
        dpr = d_pre_r.astype(BF16)
        dpi = d_pre_i.astype(BF16)
        dba_ref[...] += jnp.sum(d_pre_r, axis=0, keepdims=True)
        dbx_ref[...] += jnp.sum(d_pre_i, axis=0, keepdims=True)
        pa = _dot_tn(xcb, dpr)
        px = _dot_tn(xcb, dpi)
        for k in range(per):
            dwa_ref[k] += pa[k * hd:(k + 1) * hd, k * hd:(k + 1) * hd]
            dwx_ref[k] += px[k * hd:(k + 1) * hd, k * hd:(k + 1) * hd]
        d_xc = d_xc + _dot_nt(dpr, wa) + _dot_nt(dpi, wx)

        nxt = nextd_s[...]
        xp = xp_ref[...].astype(F32)
        dxp = cw_ref[3:4, :] * d_xc
        dcw_ref[3:4, :] += jnp.sum(xp * d_xc, axis=0, keepdims=True)
        for j in (1, 2, 3):
            uj = up(d_xc, pltpu.roll(nxt, tc - j, 0), j)
            dxp = dxp + cw_ref[3 - j:4 - j, :] * uj
            dcw_ref[3 - j:4 - j, :] += jnp.sum(xp * uj, axis=0, keepdims=True)
        dcb_ref[...] += jnp.sum(d_xc, axis=0, keepdims=True)
        nextd_s[...] = d_xc
        dxp_ref[...] = dxp.astype(BF16)

    vec = pl.BlockSpec((1, cb), lambda j, c: (0, j))
    blk = pl.BlockSpec((tc, cb), lambda j, c: (ntc - 1 - c, j))
    mat = pl.BlockSpec((per, hd, hd), lambda j, c: (j, 0, 0))
    cwb = pl.BlockSpec((4, cb), lambda j, c: (0, j))
    return _call(
        body, name="lru_bwd", grid=(ncb, ntc),
        in_specs=[
            blk, blk, blk,
            pl.BlockSpec((tc, cb), lambda j, c: (jnp.maximum(ntc - 2 - c, 0), j)),
            blk, cwb, mat, vec, mat, vec, vec,
        ],
        out_specs=[blk, mat, vec, mat, vec, vec, cwb, vec],
        out_shape=[
            jax.ShapeDtypeStruct((t, dr), BF16),
            jax.ShapeDtypeStruct(w_a.shape, F32),
            jax.ShapeDtypeStruct((1, dr), F32),
            jax.ShapeDtypeStruct(w_x.shape, F32),
            jax.ShapeDtypeStruct((1, dr), F32),
            jax.ShapeDtypeStruct((1, dr), F32),
            jax.ShapeDtypeStruct((4, dr), F32),
            jax.ShapeDtypeStruct((1, dr), F32),
        ],
        scratch_shapes=[
            pltpu.VMEM((tc, cb), F32),
            pltpu.VMEM((1, cb), F32),
            pltpu.VMEM((1, cb), F32),
            pltpu.VMEM((tc, cb), F32),
            pltpu.VMEM((cb, cb), BF16),
            pltpu.VMEM((cb, cb), BF16),
        ],
        args=(dh, xc, h, h, proj, conv_w, w_a, b_a, w_x, b_x, lam), carry=carry)


def _pool_bwd(d_br_b, w_pool_upb, p, pool_w, pool_scale):
    t, d = d_br_b.shape
    nwb, dp, _ = w_pool_upb.shape
    tc = _tile(t, 256)
    ntc = t // tc
    ng = len(POOL_WINDOWS)

    def body(db_ref, wu_ref, p_ref, w_ref, sc_ref, dx_ref, dw_ref, dsc_ref, nz, n2, n4, n8, dp_s, dy_s):
        c = pl.program_id(0)
        rc = ntc - 1 - c

        @pl.when(c == 0)
        def _():
            for s in (nz, n2, n4, n8):
                s[...] = jnp.zeros_like(s)
            dw_ref[...] = jnp.zeros_like(dw_ref)
            dsc_ref[...] = jnp.zeros_like(dsc_ref)

        wu = jnp.concatenate([wu_ref[b] for b in range(nwb)], axis=1)
        dy_s[...] = _dot_nt(db_ref[...], wu)
        for g in range(ng):
            sl = slice(g * POOL_GROUP_DIM, (g + 1) * POOL_GROUP_DIM)
            pg = p_ref[:, sl]
            dyg = dy_s[:, sl]
            wg = w_ref[g].astype(BF16)
            q = _dot_nn(pg, wg)
            dsc_ref[:, sl] += jnp.sum(dyg * q, axis=0, keepdims=True)
            dpw = (dyg * sc_ref[:, sl]).astype(BF16)
            dw_ref[g] += _dot_tn(pg, dpw)
            dp_s[:, sl] = _dot_nt(dpw, wg)

        dpv = dp_s[...]
        row = lax.broadcasted_iota(jnp.int32, dpv.shape, 0)
        col = lax.broadcasted_iota(jnp.int32, dpv.shape, 1)
        win = _pool_select(col, POOL_WINDOWS)
        cnt = jnp.minimum(rc * tc + row + 1, win).astype(F32)
        z = dpv / cnt

        def up(v, nv, j):
            return jnp.where(row < tc - j, pltpu.roll(v, tc - j, 0), pltpu.roll(nv[...], tc - j, 0))

        u2 = z + up(z, nz, 1)
        u4 = u2 + up(u2, n2, 2)
        u8 = u4 + up(u4, n4, 4)
        u16 = u8 + up(u8, n8, 8)
        nz[...] = z
        n2[...] = u2
        n4[...] = u4
        n8[...] = u8
        dx_ref[...] = (_pool_select(col, (u2, u4, u8, u16)) - dpv).astype(BF16)

    blk = pl.BlockSpec((tc, dp), lambda c: (ntc - 1 - c, 0))
    full_w = pl.BlockSpec(pool_w.shape, lambda c: (0, 0, 0))
    vec = pl.BlockSpec((1, dp), lambda c: (0, 0))
    return _call(
        body, name="pool_bwd", grid=(ntc,),
        in_specs=[pl.BlockSpec((tc, d), lambda c: (ntc - 1 - c, 0)),
                  pl.BlockSpec(w_pool_upb.shape, lambda c: (0, 0, 0)), blk, full_w, vec],
        out_specs=[blk, full_w, vec],
        out_shape=[
            jax.ShapeDtypeStruct((t, dp), BF16),
            jax.ShapeDtypeStruct(pool_w.shape, F32),
            jax.ShapeDtypeStruct((1, dp), F32),
        ],
        scratch_shapes=[pltpu.VMEM((tc, dp), F32)] * 6,
        args=(d_br_b, w_pool_upb, p, pool_w, pool_scale))[0]


def _win_bwd_norm(parts, w_int, dx2, x, g1, carry=None):
    t, d = x.shape
    tk = 512
    tt = _tile(t, 1024)
    bounds = []
    k0 = 0
    for part in parts:
        assert part.shape[1] % tk == 0
        bounds.append((k0, k0 + part.shape[1] // tk))
        k0 += part.shape[1] // tk
    nk = k0
    assert nk * tk == w_int.shape[0]
    np_ = len(parts)

    def body(*refs):
        p_refs = refs[:np_]
        w_ref, dx2_ref, x_ref, g_ref, gx_ref, dg_ref, acc = refs[np_:]
        i, kk = pl.program_id(0), pl.program_id(1)

        @pl.when(kk == 0)
        def _():
            acc[...] = jnp.zeros_like(acc)

        @pl.when((i == 0) & (kk == 0))
        def _():
            dg_ref[...] = jnp.zeros_like(dg_ref)

        for (lo, hi), p_ref in zip(bounds, p_refs):
            @pl.when((kk >= lo) & (kk < hi))
            def _(p_ref=p_ref):
                acc[...] += _dot_nn(p_ref[...], w_ref[...])

        @pl.when(kk == nk - 1)
        def _():
            def tail(rows):
                xhat, r = _rms_hat(x_ref[rows, :])
                dx, dg = _rms_bwd(acc[rows, :], xhat, r, g_ref[...])
                gx_ref[rows, :] = dx2_ref[rows, :] + dx
                dg_ref[...] += dg

            _row_chunks(tt, tail)

    def part_spec(lo, hi):
        return pl.BlockSpec((tt, tk), lambda i, kk: (i, jnp.clip(kk - lo, 0, hi - lo - 1)))

    row = pl.BlockSpec((tt, d), lambda i, kk: (i, 0))
    vec = pl.BlockSpec((1, d), lambda i, kk: (0, 0))
    return _call(
        body, name="win_bwd_norm", grid=(t // tt, nk),
        in_specs=[part_spec(lo, hi) for lo, hi in bounds]
        + [pl.BlockSpec((tk, d), lambda i, kk: (kk, 0)), row, row, vec],
        out_specs=[row, vec],
        out_shape=[jax.ShapeDtypeStruct((t, d), F32), jax.ShapeDtypeStruct((1, d), F32)],
        scratch_shapes=[pltpu.VMEM((tt, d), F32)],
        args=(*parts, w_int, dx2, x, g1), carry=carry)


def _adam_math(w, g, m, v):
    m = ADAM_B1 * m + (1.0 - ADAM_B1) * g
    v = ADAM_B2 * v + (1.0 - ADAM_B2) * (g * g)
    m_hat = m / (1.0 - ADAM_B1 ** ADAM_STEP)
    v_hat = v / (1.0 - ADAM_B2 ** ADAM_STEP)
    delta = -ADAM_LR * (m_hat / (jnp.sqrt(v_hat) + ADAM_EPS) + ADAM_WD * w)
    return delta, m, v


def _adamw_big(ws, gs, ms, vs):
    n = len(ws)
    nb = 4
    pair = [isinstance(g, tuple) for g in gs]

    def body(*refs):
        p = 0
        ins = []
        for a in range(n):
            k = 5 if pair[a] else 4
            ins.append(refs[p:p + k])
            p += k
        for a in range(n):
            g_out, d_ref, nm_ref, nv_ref = refs[p + 4 * a:p + 4 * a + 4]
            if pair[a]:
                w_ref, own_ref, recv_ref, m_ref, v_ref = ins[a]
                g = own_ref[...]
                for k in range(3):
                    g = g + recv_ref[k].astype(F32)
            else:
                w_ref, g_ref, m_ref, v_ref = ins[a]
                g = g_ref[...]
            dl, m, v = _adam_math(w_ref[...], g, m_ref[...], v_ref[...])
            g_out[...] = g
            d_ref[...] = dl
            nm_ref[...] = m
            nv_ref[...] = v

    in_specs, out_specs, out_shape, args = [], [], [], []
    for a, (w, g, m, v) in enumerate(zip(ws, gs, ms, vs)):
        rows, cols = w.shape
        blk = pl.BlockSpec((rows // nb, cols), lambda i: (i, 0))
        if pair[a]:
            in_specs += [blk, pl.BlockSpec((None, rows // nb, cols), lambda i: (0, i, 0)),
                         pl.BlockSpec((3, rows // nb, cols), lambda i: (0, i, 0)), blk, blk]
            args += [w, g[0], g[1], m, v]
        else:
            in_specs += [blk] * 4
            args += [w, g, m, v]
        out_specs += [blk] * 4
        out_shape += [jax.ShapeDtypeStruct(w.shape, F32)] * 4
    outs = _call(body, name="adamw_big", grid=(nb,), in_specs=in_specs, out_specs=out_specs,
                 out_shape=out_shape, args=args)[0]
    return [tuple(outs[4 * a:4 * a + 4]) for a in range(n)]


SMALL_ORDER = ("norm_mix_pre", "norm_mix_post", "norm_mlp_pre", "norm_mlp_post", "b_gate", "conv_w", "conv_b",
               "lru_w_a", "lru_b_a", "lru_w_x", "lru_b_x", "lru_lambda", "pool_w", "pool_scale")
VEC_ROW = dict(norm_mix_pre=0, norm_mix_post=1, norm_mlp_pre=2, norm_mlp_post=3, conv_b=6, lru_b_a=7,
               lru_b_x=8, lru_lambda=9)
ROW_B_GATE, ROW_POOL_SCALE, ROW_CONV_W, ROW_LOSS, N_VEC_ROWS = 4, 10, 11, 15, 16


def _adamw_small(vec_parts, g_pool, g_wa, g_wx, me, params):
    d = vec_parts.shape[2]
    names = SMALL_ORDER
    n = len(names)
    cw_cols = params["conv_w"][0].shape[2]

    def body(me_ref, vec_ref, vecc_ref, gp_ref, gwa_ref, gwx_ref, *refs):
        wmv = refs[:3 * n]
        loss_ref = refs[3 * n]
        outs = refs[3 * n + 1:3 * n + 1 + 4 * n]
        vs, vsc = refs[3 * n + 1 + 4 * n:]
        acc, accc = vec_ref[0], vecc_ref[0]
        for k in range(1, N_DEV):
            acc = acc + vec_ref[k]
            accc = accc + vecc_ref[k]
        vs[...] = acc
        vsc[...] = accc
        loss_ref[...] = vs[ROW_LOSS:ROW_LOSS + 1, 0:128]

        def upd(a, g, idx):
            w_ref, m_ref, v_ref = wmv[3 * a:3 * a + 3]
            g_ref, d_ref, nm_ref, nv_ref = outs[4 * a:4 * a + 4]
            dl, m, v = _adam_math(w_ref[idx], g, m_ref[idx], v_ref[idx])
            g_ref[idx] = g
            d_ref[idx] = dl
            nm_ref[idx] = m
            nv_ref[idx] = v

        for a, name in enumerate(names):
            if name in VEC_ROW:
                r = VEC_ROW[name]
                upd(a, vs[r:r + 1, :], (slice(None), slice(None)))
            elif name == "b_gate":
                for half in range(2):
                    r = ROW_B_GATE + half
                    upd(a, vs[r:r + 1, :], (slice(None), slice(half * d, (half + 1) * d)))
            elif name == "pool_scale":
                width = params[name][0].shape[1]
                upd(a, vs[ROW_POOL_SCALE:ROW_POOL_SCALE + 1, 0:width], (slice(None), slice(None)))
            elif name == "conv_w":
                upd(a, vsc[ROW_CONV_W:ROW_CONV_W + 4, :], (0,))
            elif name == "pool_w":
                upd(a, gp_ref[...], (Ellipsis,))
            elif name == "lru_w_a":
                upd(a, gwa_ref[...], (Ellipsis,))
            elif name == "lru_w_x":
                upd(a, gwx_ref[...], (Ellipsis,))
            else:
                raise ValueError(name)

    def whole(shape):
        nd = len(shape)
        return pl.BlockSpec(tuple(shape), lambda i, me_ref: (0,) * nd)

    in_specs = [
        whole(vec_parts.shape),
        pl.BlockSpec((N_DEV, N_VEC_ROWS, cw_cols), lambda i, me_ref: (0, 0, me_ref[0])),
        whole(g_pool.shape), whole(g_wa.shape), whole(g_wx.shape),
    ]
    args = [vec_parts, vec_parts, g_pool, g_wa, g_wx]
    out_specs = [whole((1, 128))]
    out_shape = [jax.ShapeDtypeStruct((1, 128), F32)]
    for name in names:
        for arr in params[name]:
            in_specs.append(whole(arr.shape))
            args.append(arr)
        shp = params[name][0].shape
        out_specs += [whole(shp)] * 4
        out_shape += [jax.ShapeDtypeStruct(shp, F32)] * 4
    grid_spec = pltpu.PrefetchScalarGridSpec(
        num_scalar_prefetch=1, grid=(1,), in_specs=in_specs, out_specs=out_specs,
        scratch_shapes=[pltpu.VMEM((N_VEC_ROWS, d), F32), pltpu.VMEM((N_VEC_ROWS, cw_cols), F32)])
    outs = pl.pallas_call(
        body, name="adamw_small", grid_spec=grid_spec, out_shape=out_shape,
        compiler_params=pltpu.CompilerParams(
            dimension_semantics=("arbitrary",), vmem_limit_bytes=V7X_VMEM_LIMIT_BYTES),
    )(me, *_in_hbm(args))
    return outs[0], {name: tuple(outs[1 + 4 * a:5 + 4 * a]) for a, name in enumerate(names)}


def _rs_sum(fulls, recvs, shard_ids, slot_ids, name):
    n = len(fulls)

    def body(sh_ref, sl_ref, *refs):
        s = pl.program_id(0)
        for a in range(n):
            full_ref, recv_ref = refs[2 * a], refs[2 * a + 1]
            own_ref, send_ref = refs[2 * n + 2 * a], refs[2 * n + 2 * a + 1]
            v = full_ref[...] + recv_ref[...].astype(F32)

            @pl.when(s == 0)
            def _(own_ref=own_ref, v=v):
                own_ref[...] = v

            @pl.when(s > 0)
            def _(send_ref=send_ref, v=v):
                send_ref[...] = v.astype(send_ref.dtype)

    in_specs, out_specs, out_shape, args = [], [], [], []
    for full, recv in zip(fulls, recvs):
        r, rest = recv.shape[1], tuple(recv.shape[2:])
        zeros = (0,) * len(rest)
        in_specs += [
            pl.BlockSpec((r,) + rest, lambda s, sh, sl, zeros=zeros: (sh[s],) + zeros),
            pl.BlockSpec((None, r) + rest, lambda s, sh, sl, zeros=zeros: (sl[s], 0) + zeros),
        ]
        out_specs += [
            pl.BlockSpec((None, r) + rest, lambda s, sh, sl, zeros=zeros: (0, 0) + zeros),
            pl.BlockSpec((None, r) + rest, lambda s, sh, sl, zeros=zeros: (jnp.maximum(s - 1, 0), 0) + zeros),
        ]
        out_shape += [jax.ShapeDtypeStruct((1, r) + rest, F32), jax.ShapeDtypeStruct((3, r) + rest, recv.dtype)]
        args += [full, recv]
    grid_spec = pltpu.PrefetchScalarGridSpec(
        num_scalar_prefetch=2, grid=(4,), in_specs=in_specs, out_specs=out_specs)
    outs = pl.pallas_call(
        body,
        name=name,
        grid_spec=grid_spec,
        out_shape=out_shape,
        compiler_params=pltpu.CompilerParams(
            dimension_semantics=("arbitrary",), vmem_limit_bytes=V7X_VMEM_LIMIT_BYTES),
    )(shard_ids, slot_ids, *_in_hbm(args))
    return [(outs[2 * a], outs[2 * a + 1]) for a in range(n)]


def _finals(pairs, name, carry=None):
    nb = 4
    n = len(pairs)

    def body(*refs):
        for a in range(n):
            own_ref, recv_ref = refs[2 * a], refs[2 * a + 1]
            acc = own_ref[...]
            for k in range(3):
                acc = acc + recv_ref[k].astype(F32)
            refs[2 * n + a][...] = acc

    in_specs, out_specs, out_shape, args = [], [], [], []
    for own, recv in pairs:
        _, rows, cols = own.shape
        in_specs += [pl.BlockSpec((None, rows // nb, cols), lambda i: (0, i, 0)),
                     pl.BlockSpec((3, rows // nb, cols), lambda i: (0, i, 0))]
        args += [own, recv]
        out_specs.append(pl.BlockSpec((rows // nb, cols), lambda i: (i, 0)))
        out_shape.append(jax.ShapeDtypeStruct((rows, cols), F32))
    return _call(body, name=name, grid=(nb,), in_specs=in_specs, out_specs=out_specs,
                 out_shape=out_shape, args=args, carry=carry)


def _rs_sums(fulls_f32, recv1, tag):
    x, y, c = _place()
    qs = jnp.stack([2 * x + y, 2 * (1 - x) + y, 2 * x + (1 - y), 2 * (1 - x) + (1 - y)]).astype(jnp.int32)
    shard_ids = 2 * qs + c
    return _rs_sum(fulls_f32, recv1, shard_ids, qs, "rs_sum_" + tag)


def _rs_level1(fulls_f32, fulls_send, tag):
    recv1 = _run_plan(_rs_sibling_plan(fulls_send), "rs_sibling_" + tag)
    return _rs_sums(fulls_f32, recv1, tag)


def _rows(g):
    return g.reshape(g.shape[0] * g.shape[1], g.shape[2])


def kernel(x, norm_mix_pre, norm_mix_post, norm_mlp_pre, norm_mlp_post, w_in, b_gate, conv_w, conv_b, lru_w_a, lru_b_a, lru_w_x, lru_b_x, lru_lambda, pool_w, pool_scale, w_lru_up, w_pool_up, w_o, w_ff1, w_ff2, loss_target, m_norm_mix_pre, m_norm_mix_post, m_norm_mlp_pre, m_norm_mlp_post, m_w_in, m_b_gate, m_conv_w, m_conv_b, m_lru_w_a, m_lru_b_a, m_lru_w_x, m_lru_b_x, m_lru_lambda, m_pool_w, m_pool_scale, m_w_lru_up, m_w_pool_up, m_w_o, m_w_ff1, m_w_ff2, v_norm_mix_pre, v_norm_mix_post, v_norm_mlp_pre, v_norm_mlp_post, v_w_in, v_b_gate, v_conv_w, v_conv_b, v_lru_w_a, v_lru_b_a, v_lru_w_x, v_lru_b_x, v_lru_lambda, v_pool_w, v_pool_scale, v_w_lru_up, v_w_pool_up, v_w_o, v_w_ff1, v_w_ff2):
    t, d = x.shape[1], x.shape[2]
    d_rnn = conv_b.shape[1]
    d_pool = pool_scale.shape[1]
    per = LRU_CB // LRU_HEAD_DIM
    xi, yi, ci = _place()
    me = 4 * xi + 2 * yi + ci

    x2d = x[0]
    tgt = loss_target[0]

    s_in = w_in[0].T.astype(BF16)
    s_lu = w_lru_up[0].astype(BF16)
    s_pu = w_pool_up[0].astype(BF16)
    s_o = w_o[0].astype(BF16)
    s_f1 = w_ff1[0].astype(BF16)
    s_f2 = w_ff2[0].astype(BF16)
    s_cw = jnp.pad(conv_w[0], ((0, 4), (0, 0)))

    g_in, g_cw = _run_plan(_ag_plan([s_in, s_cw]), "ag_w_in")
    w_int = _rows(g_in)
    conv_w_full = jnp.transpose(g_cw[:, :4, :], (1, 0, 2)).reshape(4, d_rnn)

    wa_bd, wx_bd = lru_w_a[0], lru_w_x[0]
    pw = pool_w[0]
    pw_bf = pw.astype(BF16)

    pool_block = (2 * d_rnn) // d_pool
    ga_block = (2 * d_rnn + d_pool) // 512
    gb_block = ga_block + d // 512
    g_block = d_rnn // 512

    r_f1, r_f2 = s_f1.shape[0], s_f2.shape[0]
    f1_cut = r_f1 // 4
    f2_cut = (3 * r_f2) // 8
    plan = _join([_ag_plan([s_lu, s_pu, s_o]), _ag_plan([s_f1], pieces=[(0, f1_cut)])])
    (proj, h1), got = _norm_proj(x2d, norm_mix_pre, w_int, carry=plan)
    (g_lu, g_pu, g_o), (g_f1,) = plan.split(got)
    w_lu, w_og = _rows(g_lu), _rows(g_o)
    (y_lru, h, xc), (g_f1,) = _lru_fwd(
        proj, conv_w_full, conv_b, wa_bd, lru_b_a, wx_bd, lru_b_x, lru_lambda,
        carry=_ag_plan([s_f1], pieces=[(f1_cut, r_f1 - f1_cut)], bufs=[g_f1]))
    y_pool, p = _pool_fwd(proj, pw_bf, pool_scale, pool_block)
    (br_a, br_b, mix), (g_f2,) = _branch_mix(
        y_lru, y_pool, w_lu, g_pu, proj, b_gate, ga_block, gb_block,
        carry=_ag_plan([s_f2], pieces=[(0, f2_cut)]))
    (m, x2, h3), _ = _wo_norm(mix, w_og, x2d, norm_mix_post, norm_mlp_pre)
    (rf,), (g_f2,) = _ff1(
        h3, g_f1, carry=_ag_plan([s_f2], pieces=[(f2_cut, r_f2 - f2_cut)], bufs=[g_f2]))
    w_f2 = _rows(g_f2)
    dy, df, dg4, loss_part = _ff2_loss(rf, w_f2, x2, norm_mlp_post, tgt)

    (gw_ff2_32, gw_ff2_16), _ = _wgrad(rf, df, "wgrad_ff2", square_a=True)
    (d_f1,), r1_ff2 = _ff2_bwd(df, w_f2, rf, carry=_rs_sibling_plan([gw_ff2_16]))
    ((own_ff2, send_ff2),) = _rs_sums([gw_ff2_32], r1_ff2, "ff2")
    cut2 = (5 * send_ff2.shape[1]) // 16
    (gw_ff1_32, gw_ff1_16), (r2_ff2,) = _wgrad_cols(
        h3, d_f1, s_f1.shape[1], s_f1.shape[1], "wgrad_ff1",
        carry=_rs_chips_plan([send_ff2], pieces=[(0, cut2)]))
    plan = _join([_rs_chips_plan([send_ff2], pieces=[(cut2, send_ff2.shape[1] - cut2)], bufs=[r2_ff2]),
                  _rs_sibling_plan([gw_ff1_16])])
    (dx2, dm, dg3, dg2), got = _ff1_bwd_norms(d_f1, g_f1, dy, x2, norm_mlp_pre, m, norm_mix_post, carry=plan)
    (r2_ff2,), r1_ff1 = plan.split(got)
    ((own_ff1, send_ff1),) = _rs_sums([gw_ff1_32], r1_ff1, "ff1")
    own_ff1, send_ff1 = own_ff1.reshape((1,) + s_f1.shape), send_ff1.reshape((3,) + s_f1.shape)
    cut = send_ff1.shape[1] // 4
    (gw_o_32, gw_o_16), _ = _wgrad(mix, dm, "wgrad_o")
    (d_br_a, d_br_b, p_ga, p_gb, dbg_a, dbg_b), (r2_ff1,) = _wo_bwd_mix(
        dm, w_og, br_a, br_b, proj, b_gate, ga_block, gb_block,
        carry=_rs_chips_plan([send_ff1], pieces=[(0, cut)]))
    (gw_lu_32, gw_lu_16), _ = _wgrad(y_lru, d_br_a, "wgrad_lru_up")
    (gw_pu_32, gw_pu_16), _ = _wgrad_cols(y_pool, d_br_b, s_pu.shape[1], d, "wgrad_pool_up")
    (dh, p_g), r1_mid = _lru_up_bwd(
        d_br_a, w_lu, proj, h, g_block,
        carry=_rs_sibling_plan([gw_o_16, gw_lu_16, gw_pu_16]))
    mid = _rs_sums([gw_o_32, gw_lu_32, gw_pu_32], r1_mid, "mid")
    (p_x, dwa, db_a, dwx, db_x, dlam, dconv_w, dconv_b), (r2_ff1,) = _lru_bwd(
        dh, xc, h, proj, conv_w_full, wa_bd, lru_b_a, wx_bd, lru_b_x, lru_lambda,
        carry=_rs_chips_plan([send_ff1], pieces=[(cut, send_ff1.shape[1] - cut)], bufs=[r2_ff1]))
    p_p, dpool_w, dpool_scale = _pool_bwd(d_br_b, g_pu, p, pw, pool_scale)
    parts = [p_x, p_g, p_p, p_ga, p_gb]
    gw_in, r2_mid = _wgrad_parts(parts, h1, "wgrad_in", carry=_rs_chips_plan([s for _, s in mid]))
    tail = _rs_level1([gw_in[0], dpool_w.reshape(N_DEV, -1, POOL_GROUP_DIM), dwa, dwx],
                      [gw_in[1], dpool_w.reshape(N_DEV, -1, POOL_GROUP_DIM), dwa, dwx], "in")
    (grad_x, dg1), r2_tail = _win_bwd_norm(parts, w_int, dx2, x2d, norm_mix_pre,
                                           carry=_rs_chips_plan([s for _, s in tail]))

    def flat2(a):
        return a.reshape(a.shape[0], -1, a.shape[-1])

    fin_small, _ = _finals([
        (flat2(tail[1][0]), flat2(r2_tail[1])), (flat2(tail[2][0]), flat2(r2_tail[2])),
        (flat2(tail[3][0]), flat2(r2_tail[3])),
    ], "rs_finals_small")

    def pad_row(a):
        return jnp.pad(a, ((0, 0), (0, d - a.shape[1])))

    vecs = jnp.concatenate([dg1, dg2, dg3, dg4, dbg_a, dbg_b, dconv_b, db_a, db_x, dlam,
                            pad_row(dpool_scale), dconv_w, pad_row(loss_part)], axis=0)
    assert vecs.shape[0] == N_VEC_ROWS
    vec_parts, g_pool, g_wa, g_wx = _run_plan(_ag_plan([vecs] + fin_small), "ag_tail")

    big_names = ["w_in", "w_lru_up", "w_pool_up", "w_o", "w_ff1", "w_ff2"]
    big_w = [w_in[0].T, w_lru_up[0], w_pool_up[0], w_o[0], w_ff1[0], w_ff2[0]]
    big_g = [(tail[0][0], r2_tail[0]), (mid[1][0], r2_mid[1]),
             (mid[2][0].reshape((1,) + s_pu.shape), r2_mid[2].reshape((3,) + s_pu.shape)),
             (mid[0][0], r2_mid[0]), (own_ff1, r2_ff1), (own_ff2, r2_ff2)]
    big_m = [m_w_in[0].T, m_w_lru_up[0], m_w_pool_up[0], m_w_o[0], m_w_ff1[0], m_w_ff2[0]]
    big_v = [v_w_in[0].T, v_w_lru_up[0], v_w_pool_up[0], v_w_o[0], v_w_ff1[0], v_w_ff2[0]]
    big_out = _adamw_big(big_w, big_g, big_m, big_v)
    big_out[0] = tuple(o.T for o in big_out[0])

    small = dict(
        norm_mix_pre=(norm_mix_pre, m_norm_mix_pre, v_norm_mix_pre),
        norm_mix_post=(norm_mix_post, m_norm_mix_post, v_norm_mix_post),
        norm_mlp_pre=(norm_mlp_pre, m_norm_mlp_pre, v_norm_mlp_pre),
        norm_mlp_post=(norm_mlp_post, m_norm_mlp_post, v_norm_mlp_post),
        b_gate=(b_gate, m_b_gate, v_b_gate), conv_w=(conv_w, m_conv_w, v_conv_w),
        conv_b=(conv_b, m_conv_b, v_conv_b), lru_w_a=(lru_w_a, m_lru_w_a, v_lru_w_a),
        lru_b_a=(lru_b_a, m_lru_b_a, v_lru_b_a), lru_w_x=(lru_w_x, m_lru_w_x, v_lru_w_x),
        lru_b_x=(lru_b_x, m_lru_b_x, v_lru_b_x), lru_lambda=(lru_lambda, m_lru_lambda, v_lru_lambda),
        pool_w=(pool_w, m_pool_w, v_pool_w), pool_scale=(pool_scale, m_pool_scale, v_pool_scale))
    loss_row, small_out = _adamw_small(
        vec_parts, g_pool.reshape(pool_w.shape), g_wa.reshape(lru_w_a.shape), g_wx.reshape(lru_w_x.shape),
        jnp.reshape(me, (1,)).astype(jnp.int32), small)
    grads = {n: o[0] for n, o in small_out.items()}
    delta = {n: o[1] for n, o in small_out.items()}
    new_m = {n: o[2] for n, o in small_out.items()}
    new_v = {n: o[3] for n, o in small_out.items()}

    for name, (g, dl, nm, nv) in zip(big_names, big_out):
        grads[name], delta[name], new_m[name], new_v[name] = g[None], dl[None], nm[None], nv[None]

    loss = loss_row[0, 0]
    order = ["norm_mix_pre", "norm_mix_post", "norm_mlp_pre", "norm_mlp_post", "w_in", "b_gate", "conv_w",
             "conv_b", "lru_w_a", "lru_b_a", "lru_w_x", "lru_b_x", "lru_lambda", "pool_w", "pool_scale",
             "w_lru_up", "w_pool_up", "w_o", "w_ff1", "w_ff2"]
    return (loss, grad_x[None], *[grads[n] for n in order], *[delta[n] for n in order],
            *[new_m[n] for n in order], *[new_v[n] for n in order])
```

```python
import functools
import math
import operator
import types

import jax
import jax.numpy as jnp
from jax import lax
from jax.experimental import pallas as pl
from jax.experimental.pallas import tpu as pltpu

F32 = jnp.float32
BF16 = jnp.bfloat16
NORM_EPS = 1e-6
LRU_C = 8.0
N_LRU_HEADS = 16
LRU_HEAD_DIM = 64
POOL_WINDOWS = (2, 4, 8, 16)
POOL_GROUP_DIM = 128
ADAM_LR = 0.001
ADAM_B1 = 0.9
ADAM_B2 = 0.999
ADAM_EPS = 1e-08
ADAM_WD = 0.01
ADAM_STEP = 10
N_DEV = 8
V7X_VMEM_LIMIT_BYTES = 56 * 1024 * 1024
LRU_CB = 256
MESH = pl.DeviceIdType.MESH
ANY = pl.BlockSpec(memory_space=pl.ANY)


def _tile(n, pref):
    t = min(n, pref)
    assert n % t == 0, (n, pref)
    return t


def _dot_nn(a, b):
    return lax.dot_general(a, b, (((1,), (0,)), ((), ())), preferred_element_type=F32)


def _dot_nt(a, b):
    return lax.dot_general(a, b, (((1,), (1,)), ((), ())), preferred_element_type=F32)


def _dot_tn(a, b):
    return lax.dot_general(a, b, (((0,), (0,)), ((), ())), preferred_element_type=F32)


def _row_chunks(n_rows, fn, chunk=256):
    chunk = min(chunk, n_rows)
    assert n_rows % chunk == 0

    def step(r, carry):
        fn(pl.ds(pl.multiple_of(r * chunk, chunk), chunk))
        return carry

    lax.fori_loop(0, n_rows // chunk, step, 0)


def _sig(x):
    return 1.0 / (1.0 + jnp.exp(-x))


def _rms_hat(x):
    r = lax.rsqrt(jnp.mean(x * x, axis=-1, keepdims=True) + NORM_EPS)
    return x * r, r


def _rms_bwd(dn, xhat, r, g):
    q = dn * g
    dx = r * (q - xhat * jnp.mean(q * xhat, axis=-1, keepdims=True))
    dg = jnp.sum(dn * xhat, axis=0, keepdims=True)
    return dx, dg


_GELU_K = math.sqrt(2.0 / math.pi)
_GELU_C = 0.044715


def _gelu_and_grad(g):
    t = jnp.tanh(_GELU_K * (g + _GELU_C * g * g * g))
    val = 0.5 * g * (1.0 + t)
    grad = 0.5 * (1.0 + t) + 0.5 * g * (1.0 - t * t) * (_GELU_K * (1.0 + 3.0 * _GELU_C * g * g))
    return val, grad


def _softplus_neg(lam):
    z = -lam
    e = jnp.exp(-jnp.abs(z))
    u = 1.0 + e
    d = u - 1.0
    l1p = jnp.where(d == 0.0, e, jnp.log(u) * (e / jnp.where(d == 0.0, 1.0, d)))
    return jnp.maximum(z, 0.0) + l1p


def _lru_gates(xc, wa, ba, wx, bx, lam):
    xcb = xc.astype(BF16)
    r = _sig(_dot_nn(xcb, wa) + ba)
    i = _sig(_dot_nn(xcb, wx) + bx)
    sp = _softplus_neg(lam)
    log_a = (-LRU_C) * r * sp
    a = jnp.exp(log_a)
    mult = jnp.sqrt(-jnp.tanh(log_a) * (1.0 + a * a))
    return xcb, r, i, sp, log_a, a, mult


def _place():
    return lax.axis_index("x"), lax.axis_index("y"), lax.axis_index("c")


def _ag_plan(shards, pieces=None, bufs=None):
    na = len(shards)
    n_kinds = 7

    def parts(ins, outs, sems):
        send_sems, recv_sems, local_sems = sems
        x, y, c = _place()
        me, sibling = (x, y, c), (x, y, 1 - c)
        x_nb, y_nb, diag = (1 - x, y), (x, 1 - y), (1 - x, 1 - y)
        relay_src = (c * (1 - x) + (1 - c) * x, c * y + (1 - c) * (1 - y))
        relay_dst = (c * x + (1 - c) * (1 - x), c * (1 - y) + (1 - c) * y)

        def own(a):
            return ins[a] if pieces is None else ins[a].at[pl.ds(*pieces[a])]

        def slot(a, px, py, pc):
            idx = 4 * px + 2 * py + pc
            return outs[a].at[idx] if pieces is None else outs[a].at[idx, pl.ds(*pieces[a])]

        def copy(a, k, block, to, src=None):
            return pltpu.make_async_remote_copy(
                src_ref=slot(a, *block) if src is None else src,
                dst_ref=slot(a, *block),
                send_sem=send_sems.at[a * n_kinds + k],
                recv_sem=recv_sems.at[a * n_kinds + k],
                device_id=to,
                device_id_type=MESH,
            )

        mine = [pltpu.make_async_copy(own(a), slot(a, *me), local_sems.at[a]) for a in range(na)]
        first, second, third = [], [], []
        for a in range(na):
            first += [copy(a, 0, me, sibling, src=own(a)), copy(a, 1, me, (*x_nb, c), src=own(a)),
                      copy(a, 2, me, (*y_nb, c), src=own(a))]
            second += [copy(a, 3, (*relay_src, c), (*relay_dst, c)), copy(a, 4, (*x_nb, c), sibling),
                       copy(a, 5, (*y_nb, c), sibling)]
            third.append(copy(a, 6, (*diag, c), sibling))
        return sibling, c, x_nb, y_nb, diag, copy, mine, first, second, third

    def start(ins, outs, sems):
        _, _, _, _, _, _, mine, first, _, _ = parts(ins, outs, sems)
        for cp in mine + first:
            cp.start()

    def middle(ins, outs, sems):
        _, c, x_nb, y_nb, _, copy, _, _, second, _ = parts(ins, outs, sems)
        for a in range(na):
            copy(a, 1, (*x_nb, c), (*x_nb, c)).wait_recv()
            copy(a, 2, (*y_nb, c), (*y_nb, c)).wait_recv()
        for cp in second:
            cp.start()

    def finish(ins, outs, sems):
        sibling, c, x_nb, y_nb, diag, copy, mine, first, second, third = parts(ins, outs, sems)
        for a in range(na):
            copy(a, 3, (*diag, c), (*diag, c)).wait_recv()
            third[a].start()
        for a in range(na):
            copy(a, 0, sibling, sibling).wait_recv()
            copy(a, 4, (*x_nb, 1 - c), sibling).wait_recv()
            copy(a, 5, (*y_nb, 1 - c), sibling).wait_recv()
            copy(a, 6, (*diag, 1 - c), sibling).wait_recv()
        for cp in first + second + third:
            cp.wait_send()
        for cp in mine:
            cp.wait()

    return types.SimpleNamespace(
        ins=list(shards) + list(bufs or []),
        out_shapes=[jax.ShapeDtypeStruct((N_DEV,) + s.shape, s.dtype) for s in shards],
        sems=[pltpu.SemaphoreType.DMA((n_kinds * na,)), pltpu.SemaphoreType.DMA((n_kinds * na,)),
              pltpu.SemaphoreType.DMA((na,))],
        aliases=[(na + a, a) for a in range(na)] if bufs else [],
        peers=frozenset({"sibling", "neighbours"}), start=start, middle=middle, finish=finish)


def _rs_sibling_plan(fulls):
    na = len(fulls)
    rs = [f.shape[0] // N_DEV for f in fulls]

    def copies(ins, outs, sems):
        send_sems, recv_sems = sems
        x, y, c = _place()
        out = []
        for a in range(na):
            for q in range(4):
                shard = 2 * q + (1 - c)
                out.append(pltpu.make_async_remote_copy(
                    src_ref=ins[a].at[pl.ds(shard * rs[a], rs[a])],
                    dst_ref=outs[a].at[q],
                    send_sem=send_sems.at[a * 4 + q],
                    recv_sem=recv_sems.at[a * 4 + q],
                    device_id=(x, y, 1 - c),
                    device_id_type=MESH,
                ))
        return out

    def start(ins, outs, sems):
        for cp in copies(ins, outs, sems):
            cp.start()

    def finish(ins, outs, sems):
        for cp in copies(ins, outs, sems):
            cp.wait()

    return types.SimpleNamespace(
        ins=list(fulls),
        out_shapes=[jax.ShapeDtypeStruct((4, r) + f.shape[1:], f.dtype) for r, f in zip(rs, fulls)],
        sems=[pltpu.SemaphoreType.DMA((4 * na,)), pltpu.SemaphoreType.DMA((4 * na,))],
        peers=frozenset({"sibling"}), start=start, finish=finish)


def _rs_chips_plan(sends, pieces=None, bufs=None):
    na = len(sends)

    def copies(ins, outs, sems):
        send_sems, recv_sems = sems
        x, y, c = _place()
        chips = [(1 - x, y), (x, 1 - y), (1 - x, 1 - y)]
        out = []
        for a in range(na):
            for k, chip in enumerate(chips):
                rows = (k,) if pieces is None else (k, pl.ds(*pieces[a]))
                out.append(pltpu.make_async_remote_copy(
                    src_ref=ins[a].at[rows],
                    dst_ref=outs[a].at[rows],
                    send_sem=send_sems.at[a * 3 + k],
                    recv_sem=recv_sems.at[a * 3 + k],
                    device_id=(*chip, c),
                    device_id_type=MESH,
                ))
        return out

    def start(ins, outs, sems):
        for cp in copies(ins, outs, sems):
            cp.start()

    def finish(ins, outs, sems):
        for cp in copies(ins, outs, sems):
            cp.wait()

    return types.SimpleNamespace(
        ins=list(sends) + list(bufs or []),
        out_shapes=[jax.ShapeDtypeStruct(s.shape, s.dtype) for s in sends],
        sems=[pltpu.SemaphoreType.DMA((3 * na,)), pltpu.SemaphoreType.DMA((3 * na,))],
        aliases=[(na + a, a) for a in range(na)] if bufs else [],
        peers=frozenset({"chips"}), start=start, finish=finish)


def _join(plans):
    ins, outs, sems, aliases, offs = [], [], [], [], []
    for p in plans:
        offs.append((len(ins), len(outs), len(sems)))
        aliases += [(len(ins) + ci, len(outs) + co) for ci, co in getattr(p, "aliases", [])]
        ins += p.ins
        outs += p.out_shapes
        sems += p.sems

    def cut(p, off, i, o, s):
        return (i[off[0]:off[0] + len(p.ins)], o[off[1]:off[1] + len(p.out_shapes)],
                s[off[2]:off[2] + len(p.sems)])

    def start(i, o, s):
        for p, off in zip(plans, offs):
            p.start(*cut(p, off, i, o, s))

    def middle(i, o, s):
        for p, off in zip(plans, offs):
            if getattr(p, "middle", None) is not None:
                p.middle(*cut(p, off, i, o, s))

    def finish(i, o, s):
        for p, off in zip(plans, offs):
            p.finish(*cut(p, off, i, o, s))

    def split(results):
        return [list(results[off[1]:off[1] + len(p.out_shapes)]) for p, off in zip(plans, offs)]

    return types.SimpleNamespace(ins=ins, out_shapes=outs, sems=sems, aliases=aliases,
                                 peers=frozenset().union(*[p.peers for p in plans]),
                                 start=start, middle=middle, finish=finish, split=split)


COLLECTIVE_ID = {frozenset({"sibling"}): 0, frozenset({"chips"}): 1, frozenset({"sibling", "chips"}): 2,
                 frozenset({"sibling", "neighbours"}): 3}


def _handshake(peers):
    x, y, c = _place()
    devs = []
    if "sibling" in peers:
        devs.append((x, y, 1 - c))
    if "neighbours" in peers:
        devs += [(1 - x, y, c), (x, 1 - y, c)]
    if "chips" in peers:
        assert "neighbours" not in peers
        devs += [(1 - x, y, c), (x, 1 - y, c), (1 - x, 1 - y, c)]
    barrier = pltpu.get_barrier_semaphore()
    for dev in devs:
        pl.semaphore_signal(barrier, inc=1, device_id=dev, device_id_type=MESH)
    pl.semaphore_wait(barrier, len(devs))


def _in_hbm(args):
    return [pltpu.with_memory_space_constraint(a, pltpu.HBM) for a in args]


def _run_plan(plan, name):
    n_in, n_out = len(plan.ins), len(plan.out_shapes)

    def body(*refs):
        ins, outs, sems = refs[:n_in], refs[n_in:n_in + n_out], refs[n_in + n_out:]
        _handshake(plan.peers)
        plan.start(ins, outs, sems)
        if getattr(plan, "middle", None) is not None:
            plan.middle(ins, outs, sems)
        plan.finish(ins, outs, sems)

    return pl.pallas_call(
        body,
        name=name,
        in_specs=[ANY] * n_in,
        out_specs=[ANY] * n_out,
        out_shape=plan.out_shapes,
        scratch_shapes=plan.sems,
        input_output_aliases=dict(getattr(plan, "aliases", [])),
        compiler_params=pltpu.CompilerParams(collective_id=COLLECTIVE_ID[plan.peers]),
    )(*_in_hbm(plan.ins))


def _call(body, *, name, grid, in_specs, out_specs, out_shape, args, scratch_shapes=(), aliases=None,
          carry=None):
    n_in, n_out, n_scr = len(in_specs), len(out_shape), len(scratch_shapes)
    params = pltpu.CompilerParams(
        dimension_semantics=("arbitrary",) * len(grid), vmem_limit_bytes=V7X_VMEM_LIMIT_BYTES)
    if carry is None:
        outs = pl.pallas_call(
            body, name=name, grid=grid, in_specs=list(in_specs), out_specs=list(out_specs),
            out_shape=list(out_shape), scratch_shapes=list(scratch_shapes),
            input_output_aliases=aliases or {}, compiler_params=params)(*_in_hbm(args))
        return list(outs), []
    c_in, c_out = len(carry.ins), len(carry.out_shapes)

    def full(*refs):
        p = 0
        ins = refs[p:p + n_in]
        p += n_in
        cins = refs[p:p + c_in]
        p += c_in
        outs = refs[p:p + n_out]
        p += n_out
        couts = refs[p:p + c_out]
        p += c_out
        scr = refs[p:p + n_scr]
        csems = refs[p + n_scr:]
        ids = [pl.program_id(a) for a in range(len(grid))]
        first = functools.reduce(operator.and_, [i == 0 for i in ids])
        last = functools.reduce(operator.and_, [i == g - 1 for i, g in zip(ids, grid)])

        @pl.when(first)
        def _():
            _handshake(carry.peers)
            carry.start(cins, couts, csems)

        if getattr(carry, "middle", None) is not None:
            n_steps = math.prod(grid)
            flat = functools.reduce(lambda acc, ig: acc * ig[1] + ig[0], zip(ids, grid), 0)

            @pl.when(flat == (2 * n_steps) // 3)
            def _():
                carry.middle(cins, couts, csems)

        body(*ins, *outs, *scr)

        @pl.when(last)
        def _():
            carry.finish(cins, couts, csems)

    all_aliases = dict(aliases or {})
    all_aliases.update({n_in + ci: n_out + co for ci, co in getattr(carry, "aliases", [])})
    params = pltpu.CompilerParams(
        dimension_semantics=("arbitrary",) * len(grid), vmem_limit_bytes=V7X_VMEM_LIMIT_BYTES,
        collective_id=COLLECTIVE_ID[carry.peers])
    outs = pl.pallas_call(
        full, name=name, grid=grid,
        in_specs=list(in_specs) + [ANY] * c_in,
        out_specs=list(out_specs) + [ANY] * c_out,
        out_shape=list(out_shape) + list(carry.out_shapes),
        scratch_shapes=list(scratch_shapes) + list(carry.sems),
        input_output_aliases=all_aliases, compiler_params=params)(*_in_hbm(args), *_in_hbm(carry.ins))
    return list(outs[:n_out]), list(outs[n_out:])


def _norm_proj(x, g1, w_int, carry=None):
    t, d = x.shape
    n = w_int.shape[0]
    tt, tn = _tile(t, 2048), _tile(n, 512)

    def body(x_ref, g_ref, w_ref, proj_ref, h1_ref, h1_s):
        @pl.when(pl.program_id(1) == 0)
        def _():
            def norm_rows(rows):
                xhat, _ = _rms_hat(x_ref[rows, :])
                h = (xhat * g_ref[...]).astype(BF16)
                h1_s[rows, :] = h
                h1_ref[rows, :] = h

            _row_chunks(tt, norm_rows)

        proj_ref[...] = _dot_nt(h1_s[...], w_ref[...]).astype(BF16)

    return _call(
        body, name="norm_proj", grid=(t // tt, n // tn),
        in_specs=[
            pl.BlockSpec((tt, d), lambda i, j: (i, 0)),
            pl.BlockSpec((1, d), lambda i, j: (0, 0)),
            pl.BlockSpec((tn, d), lambda i, j: (j, 0)),
        ],
        out_specs=[
            pl.BlockSpec((tt, tn), lambda i, j: (i, j)),
            pl.BlockSpec((tt, d), lambda i, j: (i, 0)),
        ],
        out_shape=[jax.ShapeDtypeStruct((t, n), BF16), jax.ShapeDtypeStruct((t, d), BF16)],
        scratch_shapes=[pltpu.VMEM((tt, d), BF16)],
        args=(x, g1, w_int), carry=carry)


def _scan_rows(av, bv, reverse):
    tc = av.shape[0]
    row = lax.broadcasted_iota(jnp.int32, av.shape, 0)
    s = 1
    while s < tc:
        if s < 8:
            keep = (row < tc - s) if reverse else (row >= s)
            shift = (tc - s) if reverse else s
            a_sh = jnp.where(keep, pltpu.roll(av, shift, 0), 1.0)
            b_sh = jnp.where(keep, pltpu.roll(bv, shift, 0), 0.0)
            bv = av * b_sh + bv
            av = av * a_sh
        elif reverse:
            bv = jnp.concatenate([av[:tc - s] * bv[s:] + bv[:tc - s], bv[tc - s:]], axis=0)
            av = jnp.concatenate([av[:tc - s] * av[s:], av[tc - s:]], axis=0)
        else:
            bv = jnp.concatenate([bv[:s], av[s:] * bv[:tc - s] + bv[s:]], axis=0)
            av = jnp.concatenate([av[:s], av[s:] * av[:tc - s]], axis=0)
        s *= 2
    return av, bv


def _fill_block_diag(w_ref, bd_ref):
    bd_ref[...] = jnp.zeros_like(bd_ref)
    hd = LRU_HEAD_DIM
    for k in range(w_ref.shape[0]):
        bd_ref[k * hd:(k + 1) * hd, k * hd:(k + 1) * hd] = w_ref[k].astype(BF16)


def _lru_fwd(proj, conv_w, conv_b, w_a, b_a, w_x, b_x, lam, carry=None):
    t = proj.shape[0]
    dr = conv_b.shape[1]
    cb = LRU_CB
    tc = _tile(t, 256)
    ncb, ntc = dr // cb, t // tc

    def body(xp_ref, g_ref, cw_ref, cb_ref, wa_ref, ba_ref, wx_ref, bx_ref, lam_ref,
             y_ref, h_ref, xc_ref, r_ref, i_ref, a_ref, mult_ref, prevx_s, hlast_s, wa_s, wx_s):
        c = pl.program_id(1)

        @pl.when(c == 0)
        def _():
            prevx_s[...] = jnp.zeros_like(prevx_s)
            hlast_s[...] = jnp.zeros_like(hlast_s)
            _fill_block_diag(wa_ref, wa_s)
            _fill_block_diag(wx_ref, wx_s)

        x = xp_ref[...].astype(F32)
        prev = prevx_s[...]
        row = lax.broadcasted_iota(jnp.int32, x.shape, 0)

        def sh(j):
            return jnp.where(row >= j, pltpu.roll(x, j, 0), pltpu.roll(prev, j, 0))

        xc = (cb_ref[...] + cw_ref[0:1, :] * sh(3) + cw_ref[1:2, :] * sh(2)
              + cw_ref[2:3, :] * sh(1) + cw_ref[3:4, :] * x)
        prevx_s[...] = x
        xc_ref[...] = xc
        _, r, i, _, _, a, mult = _lru_gates(xc, wa_s[...], ba_ref[...], wx_s[...], bx_ref[...],
                                            lam_ref[...])
        r_ref[...] = r
        i_ref[...] = i
        a_ref[...] = a
        mult_ref[...] = mult
        av, bv = _scan_rows(a, mult * (i * xc), reverse=False)
        h = av * hlast_s[...] + bv
        h_ref[...] = h
        hlast_s[...] = h_ref[tc - 1:tc, :]
        gel, _ = _gelu_and_grad(g_ref[...].astype(F32))
        y_ref[...] = (h * gel).astype(BF16)

    vec = pl.BlockSpec((1, cb), lambda j, c: (0, j))
    blk = pl.BlockSpec((tc, cb), lambda j, c: (c, j))
    mat = pl.BlockSpec((cb // LRU_HEAD_DIM, LRU_HEAD_DIM, LRU_HEAD_DIM), lambda j, c: (j, 0, 0))
    return _call(
        body, name="lru_fwd", grid=(ncb, ntc),
        in_specs=[
            blk,
            pl.BlockSpec((tc, cb), lambda j, c: (c, ncb + j)),
            pl.BlockSpec((4, cb), lambda j, c: (0, j)),
            vec, mat, vec, mat, vec, vec,
        ],
        out_specs=[blk] * 7,
        out_shape=[jax.ShapeDtypeStruct((t, dr), BF16)] + [jax.ShapeDtypeStruct((t, dr), F32)] * 6,
        scratch_shapes=[pltpu.VMEM((tc, cb), F32), pltpu.VMEM((1, cb), F32),
                        pltpu.VMEM((cb, cb), BF16), pltpu.VMEM((cb, cb), BF16)],
        args=(proj, proj, conv_w, conv_b, w_a, b_a, w_x, b_x, lam), carry=carry)


def _pool_select(col, vals):
    out = vals[3]
    for g in (2, 1, 0):
        out = jnp.where(col < (g + 1) * POOL_GROUP_DIM, vals[g], out)
    return out


def _pool_fwd(proj, pool_w, pool_scale, col_block):
    t = proj.shape[0]
    dp = pool_scale.shape[1]
    tc = _tile(t, 256)
    ntc = t // tc

    def body(x_ref, w_ref, sc_ref, y_ref, p_ref, px, p2, p4, p8):
        c = pl.program_id(0)

        @pl.when(c == 0)
        def _():
            for s in (px, p2, p4, p8):
                s[...] = jnp.zeros_like(s)

        x = x_ref[...].astype(F32)
        row = lax.broadcasted_iota(jnp.int32, x.shape, 0)
        col = lax.broadcasted_iota(jnp.int32, x.shape, 1)

        def sh(v, pv, j):
            return jnp.where(row >= j, pltpu.roll(v, j, 0), pltpu.roll(pv[...], j, 0))

        s2 = x + sh(x, px, 1)
        s4 = s2 + sh(s2, p2, 2)
        s8 = s4 + sh(s4, p4, 4)
        s16 = s8 + sh(s8, p8, 8)
        px[...] = x
        p2[...] = s2
        p4[...] = s4
        p8[...] = s8
        wsum = _pool_select(col, (s2, s4, s8, s16))
        win = _pool_select(col, POOL_WINDOWS)
        cnt = jnp.minimum(c * tc + row + 1, win).astype(F32)
        p = wsum / cnt - x
        pb = p.astype(BF16)
        p_ref[...] = pb
        for g in range(len(POOL_WINDOWS)):
            sl = slice(g * POOL_GROUP_DIM, (g + 1) * POOL_GROUP_DIM)
            yg = _dot_nn(pb[:, sl], w_ref[g]) * sc_ref[:, sl]
            y_ref[:, sl] = yg.astype(BF16)

    return _call(
        body, name="pool_fwd", grid=(ntc,),
        in_specs=[
            pl.BlockSpec((tc, dp), lambda c: (c, col_block)),
            pl.BlockSpec(pool_w.shape, lambda c: (0, 0, 0)),
            pl.BlockSpec((1, dp), lambda c: (0, 0)),
        ],
        out_specs=[pl.BlockSpec((tc, dp), lambda c: (c, 0))] * 2,
        out_shape=[jax.ShapeDtypeStruct((t, dp), BF16)] * 2,
        scratch_shapes=[pltpu.VMEM((tc, dp), F32)] * 4,
        args=(proj, pool_w, pool_scale))[0]


def _branch_mix(y_lru, y_pool, w_lru_up, w_pool_upb, proj, b_gate, ga_block, gb_block, carry=None):
    t, d = y_lru.shape
    dp = y_pool.shape[1]
    bw = w_pool_upb.shape[2]
    tt, tn = _tile(t, 1024), 512
    nj = d // tn

    def body(yl_ref, yp_ref, wl_ref, wp_ref, ga_ref, gb_ref, ba_ref, bb_ref, bra_ref, brb_ref, mix_ref):
        br_a = _dot_nn(yl_ref[...], wl_ref[...])
        wp = jnp.concatenate([wp_ref[b] for b in range(tn // bw)], axis=1)
        br_b = _dot_nn(yp_ref[...], wp)
        bra_ref[...] = br_a.astype(BF16)
        brb_ref[...] = br_b.astype(BF16)
        ga = _sig(ga_ref[...].astype(F32) + ba_ref[...])
        gb = _sig(gb_ref[...].astype(F32) + bb_ref[...])
        mix_ref[...] = (ga * br_a + gb * br_b).astype(BF16)

    out = pl.BlockSpec((tt, tn), lambda j, i: (i, j))
    return _call(
        body, name="branch_mix", grid=(nj, t // tt),
        in_specs=[
            pl.BlockSpec((tt, d), lambda j, i: (i, 0)),
            pl.BlockSpec((tt, dp), lambda j, i: (i, 0)),
            pl.BlockSpec((d, tn), lambda j, i: (0, j)),
            pl.BlockSpec((tn // bw, dp, bw), lambda j, i: (j, 0, 0)),
            pl.BlockSpec((tt, tn), lambda j, i: (i, ga_block + j)),
            pl.BlockSpec((tt, tn), lambda j, i: (i, gb_block + j)),
            pl.BlockSpec((1, tn), lambda j, i: (0, j)),
            pl.BlockSpec((1, tn), lambda j, i: (0, nj + j)),
        ],
        out_specs=[out, out, out],
        out_shape=[jax.ShapeDtypeStruct((t, d), BF16)] * 3,
        args=(y_lru, y_pool, w_lru_up, w_pool_upb, proj, proj, b_gate, b_gate), carry=carry)


def _wo_norm(mix, w_o, x, g2, g3, carry=None):
    t, d = x.shape
    tt = _tile(t, 512)

    def body(mix_ref, w_ref, x_ref, g2_ref, g3_ref, m_ref, x2_ref, h3_ref):
        m = _dot_nn(mix_ref[...], w_ref[...])
        m_ref[...] = m
        mhat, _ = _rms_hat(m)
        x2 = x_ref[...] + mhat * g2_ref[...]
        x2_ref[...] = x2
        xhat, _ = _rms_hat(x2)
        h3_ref[...] = (xhat * g3_ref[...]).astype(BF16)

    row = pl.BlockSpec((tt, d), lambda i: (i, 0))
    vec = pl.BlockSpec((1, d), lambda i: (0, 0))
    return _call(
        body, name="wo_norm", grid=(t // tt,),
        in_specs=[row, pl.BlockSpec((d, d), lambda i: (0, 0)), row, vec, vec],
        out_specs=[row, row, row],
        out_shape=[
            jax.ShapeDtypeStruct((t, d), F32),
            jax.ShapeDtypeStruct((t, d), F32),
            jax.ShapeDtypeStruct((t, d), BF16),
        ],
        args=(mix, w_o, x, g2, g3), carry=carry)


def _ff1(h3, w_ff1b, carry=None):
    t, d = h3.shape
    nb, _, tn = w_ff1b.shape
    tt = _tile(t, 2048)

    def body(h_ref, w_ref, rf_ref):
        rf_ref[...] = jnp.maximum(_dot_nn(h_ref[...], w_ref[...]), 0.0).astype(BF16)

    out = pl.BlockSpec((tt, tn), lambda i, j: (i, j))
    return _call(
        body, name="ff1", grid=(t // tt, nb),
        in_specs=[pl.BlockSpec((tt, d), lambda i, j: (i, 0)), pl.BlockSpec((None, d, tn), lambda i, j: (j, 0, 0))],
        out_specs=[out],
        out_shape=[jax.ShapeDtypeStruct((t, nb * tn), BF16)],
        args=(h3, w_ff1b), carry=carry)


def _ff2_loss(rf, w_ff2, x2, g4, target):
    t, k = rf.shape
    d = x2.shape[1]
    tt, tk = _tile(t, 1024), _tile(k, 1024)
    nk = k // tk

    def body(a_ref, w_ref, x2_ref, g_ref, tg_ref, dy_ref, df_ref, dg_ref, loss_ref, acc):
        i, kk = pl.program_id(0), pl.program_id(1)

        @pl.when(kk == 0)
        def _():
            acc[...] = jnp.zeros_like(acc)

        @pl.when((i == 0) & (kk == 0))
        def _():
            dg_ref[...] = jnp.zeros_like(dg_ref)
            loss_ref[...] = jnp.zeros_like(loss_ref)

        rf_tile = a_ref[...]
        acc[...] += _dot_nn(rf_tile * rf_tile, w_ref[...])

        @pl.when(kk == nk - 1)
        def _():
            def tail(rows):
                fhat, r = _rms_hat(acc[rows, :])
                g = g_ref[...]
                e = x2_ref[rows, :] + fhat * g - tg_ref[rows, :]
                loss_ref[...] += 0.5 * jnp.sum(jnp.mean(e * e, axis=-1, keepdims=True))
                dy = e * (1.0 / d)
                dy_ref[rows, :] = dy.astype(BF16)
                df, dg = _rms_bwd(dy, fhat, r, g)
                df_ref[rows, :] = df.astype(BF16)
                dg_ref[...] += dg

            _row_chunks(tt, tail)

    row = pl.BlockSpec((tt, d), lambda i, kk: (i, 0))
    vec = pl.BlockSpec((1, d), lambda i, kk: (0, 0))
    return _call(
        body, name="ff2_loss", grid=(t // tt, nk),
        in_specs=[
            pl.BlockSpec((tt, tk), lambda i, kk: (i, kk)),
            pl.BlockSpec((tk, d), lambda i, kk: (kk, 0)),
            row, vec, row,
        ],
        out_specs=[row, row, vec, pl.BlockSpec((1, 128), lambda i, kk: (0, 0))],
        out_shape=[
            jax.ShapeDtypeStruct((t, d), BF16),
            jax.ShapeDtypeStruct((t, d), BF16),
            jax.ShapeDtypeStruct((1, d), F32),
            jax.ShapeDtypeStruct((1, 128), F32),
        ],
        scratch_shapes=[pltpu.VMEM((tt, d), F32)],
        args=(rf, w_ff2, x2, g4, target))[0]


def _ff2_bwd(df, w_ff2, rf, carry=None):
    t, d = df.shape
    n = w_ff2.shape[0]
    tt, tn = _tile(t, 2048), _tile(n, 512)

    def body(df_ref, w_ref, rf_ref, out_ref):
        d_act = _dot_nt(df_ref[...], w_ref[...])
        out_ref[...] = (d_act * (2.0 * rf_ref[...].astype(F32))).astype(BF16)

    blk = pl.BlockSpec((tt, tn), lambda i, j: (i, j))
    return _call(
        body, name="ff2_bwd", grid=(t // tt, n // tn),
        in_specs=[pl.BlockSpec((tt, d), lambda i, j: (i, 0)), pl.BlockSpec((tn, d), lambda i, j: (j, 0)), blk],
        out_specs=[blk],
        out_shape=[jax.ShapeDtypeStruct((t, n), BF16)],
        args=(df, w_ff2, rf), carry=carry)


def _wgrad(a, b, name, prev=None, row_off=0, rows=None, carry=None, square_a=False):
    t, m = a.shape
    n = b.shape[1]
    rows = m if rows is None else rows
    tm, tk = _tile(m, 512), _tile(t, 2048)
    nk = t // tk
    assert row_off % tm == 0
    off = row_off // tm

    def body(*refs):
        a_ref, b_ref = refs[0], refs[1]
        o32_ref, o16_ref, acc = refs[-3], refs[-2], refs[-1]
        kk = pl.program_id(1)

        @pl.when(kk == 0)
        def _():
            acc[...] = jnp.zeros_like(acc)

        a_tile = a_ref[...]
        acc[...] += _dot_tn(a_tile * a_tile if square_a else a_tile, b_ref[...])

        @pl.when(kk == nk - 1)
        def _():
            o32_ref[...] = acc[...]
            o16_ref[...] = acc[...].astype(BF16)

    in_specs = [pl.BlockSpec((tk, tm), lambda i, kk: (kk, i)), pl.BlockSpec((tk, n), lambda i, kk: (kk, 0))]
    args = [a, b]
    aliases = {}
    if prev is not None:
        in_specs += [ANY, ANY]
        args += list(prev)
        aliases = {2: 0, 3: 1}
    out = pl.BlockSpec((tm, n), lambda i, kk: (off + i, 0))
    return _call(
        body, name=name, grid=(m // tm, nk),
        in_specs=in_specs, out_specs=[out, out],
        out_shape=[jax.ShapeDtypeStruct((rows, n), F32), jax.ShapeDtypeStruct((rows, n), BF16)],
        scratch_shapes=[pltpu.VMEM((tm, n), F32)],
        aliases=aliases, args=args, carry=carry)


def _wgrad_parts(parts, b, name, carry=None):
    t, n = b.shape
    tm = 512
    bounds = []
    lo = 0
    for part in parts:
        assert part.shape[0] == t and part.shape[1] % tm == 0
        bounds.append((lo, lo + part.shape[1] // tm))
        lo += part.shape[1] // tm
    nm = lo
    np_ = len(parts)

    def body(*refs):
        p_refs, b_ref, o32_ref, o16_ref = refs[:np_], refs[np_], refs[np_ + 1], refs[np_ + 2]
        i = pl.program_id(0)
        for (lo_p, hi_p), p_ref in zip(bounds, p_refs):
            @pl.when((i >= lo_p) & (i < hi_p))
            def _(p_ref=p_ref):
                res = _dot_tn(p_ref[...], b_ref[...])
                o32_ref[...] = res
                o16_ref[...] = res.astype(BF16)

    def part_spec(lo_p, hi_p):
        return pl.BlockSpec((t, tm), lambda i: (0, jnp.clip(i - lo_p, 0, hi_p - lo_p - 1)))

    out = pl.BlockSpec((tm, n), lambda i: (i, 0))
    return _call(
        body, name=name, grid=(nm,),
        in_specs=[part_spec(lo_p, hi_p) for lo_p, hi_p in bounds] + [pl.BlockSpec((t, n), lambda i: (0, 0))],
        out_specs=[out, out],
        out_shape=[jax.ShapeDtypeStruct((nm * tm, n), F32), jax.ShapeDtypeStruct((nm * tm, n), BF16)],
        args=(*parts, b), carry=carry)


def _wgrad_cols(a, b, bw, tn, name, carry=None):
    t, m = a.shape
    n = b.shape[1]
    per_step = tn // bw

    def body(a_ref, b_ref, o32_ref, o16_ref):
        res = _dot_tn(a_ref[...], b_ref[...])
        for blk in range(per_step):
            part = res[:, blk * bw:(blk + 1) * bw]
            o32_ref[blk] = part
            o16_ref[blk] = part.astype(BF16)

    out = pl.BlockSpec((per_step, m, bw), lambda j: (j, 0, 0))
    return _call(
        body, name=name, grid=(n // tn,),
        in_specs=[pl.BlockSpec((t, m), lambda j: (0, 0)), pl.BlockSpec((t, tn), lambda j: (0, j))],
        out_specs=[out, out],
        out_shape=[jax.ShapeDtypeStruct((n // bw, m, bw), F32), jax.ShapeDtypeStruct((n // bw, m, bw), BF16)],
        args=(a, b), carry=carry)


def _ff1_bwd_norms(d_f1, w_ff1b, dy, x2, g3, m, g2, carry=None):
    t, k = d_f1.shape
    d = x2.shape[1]
    bw = w_ff1b.shape[2]
    per_step = 2
    tt, tk = _tile(t, 1024), per_step * bw
    nk = k // tk

    def body(a_ref, w_ref, dy_ref, x2_ref, g3_ref, m_ref, g2_ref, dx2_ref, dm_ref, dg3_ref, dg2_ref, acc):
        i, kk = pl.program_id(0), pl.program_id(1)

        @pl.when(kk == 0)
        def _():
            acc[...] = jnp.zeros_like(acc)

        @pl.when((i == 0) & (kk == 0))
        def _():
            dg3_ref[...] = jnp.zeros_like(dg3_ref)
            dg2_ref[...] = jnp.zeros_like(dg2_ref)

        a_tile = a_ref[...]
        for b in range(per_step):
            acc[...] += _dot_nt(a_tile[:, b * bw:(b + 1) * bw], w_ref[b])

        @pl.when(kk == nk - 1)
        def _():
            def tail(rows):
                xhat, r3 = _rms_hat(x2_ref[rows, :])
                dx, dg3 = _rms_bwd(acc[rows, :], xhat, r3, g3_ref[...])
                dx2 = dy_ref[rows, :].astype(F32) + dx
                dx2_ref[rows, :] = dx2
                dg3_ref[...] += dg3
                mhat, r2 = _rms_hat(m_ref[rows, :])
                dm, dg2 = _rms_bwd(dx2, mhat, r2, g2_ref[...])
                dm_ref[rows, :] = dm.astype(BF16)
                dg2_ref[...] += dg2

            _row_chunks(tt, tail)

    row = pl.BlockSpec((tt, d), lambda i, kk: (i, 0))
    vec = pl.BlockSpec((1, d), lambda i, kk: (0, 0))
    return _call(
        body, name="ff1_bwd_norms", grid=(t // tt, nk),
        in_specs=[
            pl.BlockSpec((tt, tk), lambda i, kk: (i, kk)),
            pl.BlockSpec((per_step, d, bw), lambda i, kk: (kk, 0, 0)),
            row, row, vec, row, vec,
        ],
        out_specs=[row, row, vec, vec],
        out_shape=[
            jax.ShapeDtypeStruct((t, d), F32),
            jax.ShapeDtypeStruct((t, d), BF16),
            jax.ShapeDtypeStruct((1, d), F32),
            jax.ShapeDtypeStruct((1, d), F32),
        ],
        scratch_shapes=[pltpu.VMEM((tt, d), F32)],
        args=(d_f1, w_ff1b, dy, x2, g3, m, g2), carry=carry)


def _wo_bwd_mix(dm, w_o, br_a, br_b, proj, b_gate, ga_block, gb_block, carry=None):
    t, d = dm.shape
    tt, tn = _tile(t, 1024), 512
    nj = d // tn

    def body(dm_ref, w_ref, bra_ref, brb_ref, ga_ref, gb_ref, ba_ref, bb_ref,
             dbra_ref, dbrb_ref, dga_ref, dgb_ref, dba_ref, dbb_ref):
        i = pl.program_id(1)

        @pl.when(i == 0)
        def _():
            dba_ref[...] = jnp.zeros_like(dba_ref)
            dbb_ref[...] = jnp.zeros_like(dbb_ref)

        d_mix = _dot_nt(dm_ref[...], w_ref[...])
        ga = _sig(ga_ref[...].astype(F32) + ba_ref[...])
        gb = _sig(gb_ref[...].astype(F32) + bb_ref[...])
        dbra_ref[...] = (d_mix * ga).astype(BF16)
        dbrb_ref[...] = (d_mix * gb).astype(BF16)
        dga = d_mix * bra_ref[...].astype(F32) * (ga * (1.0 - ga))
        dgb = d_mix * brb_ref[...].astype(F32) * (gb * (1.0 - gb))
        dga_ref[...] = dga.astype(BF16)
        dgb_ref[...] = dgb.astype(BF16)
        dba_ref[...] += jnp.sum(dga, axis=0, keepdims=True)
        dbb_ref[...] += jnp.sum(dgb, axis=0, keepdims=True)

    blk = pl.BlockSpec((tt, tn), lambda j, i: (i, j))
    vec = pl.BlockSpec((1, tn), lambda j, i: (0, j))
    return _call(
        body, name="wo_bwd_mix", grid=(nj, t // tt),
        in_specs=[
            pl.BlockSpec((tt, d), lambda j, i: (i, 0)),
            pl.BlockSpec((tn, d), lambda j, i: (j, 0)),
            blk, blk,
            pl.BlockSpec((tt, tn), lambda j, i: (i, ga_block + j)),
            pl.BlockSpec((tt, tn), lambda j, i: (i, gb_block + j)),
            vec,
            pl.BlockSpec((1, tn), lambda j, i: (0, nj + j)),
        ],
        out_specs=[blk, blk, blk, blk, vec, vec],
        out_shape=[jax.ShapeDtypeStruct((t, d), BF16)] * 4 + [jax.ShapeDtypeStruct((1, d), F32)] * 2,
        args=(dm, w_o, br_a, br_b, proj, proj, b_gate, b_gate), carry=carry)


def _lru_up_bwd(d_br_a, w_lru_up, proj, h, g_block, carry=None):
    t, d = d_br_a.shape
    tt, tn = _tile(t, 1024), 512

    def body(a_ref, w_ref, g_ref, h_ref, dh_ref, dg_ref):
        d_y = _dot_nt(a_ref[...], w_ref[...])
        gel, gel_grad = _gelu_and_grad(g_ref[...].astype(F32))
        dh_ref[...] = d_y * gel
        dg_ref[...] = (d_y * h_ref[...] * gel_grad).astype(BF16)

    blk = pl.BlockSpec((tt, tn), lambda i, j: (i, j))
    return _call(
        body, name="lru_up_bwd", grid=(t // tt, d // tn),
        in_specs=[
            pl.BlockSpec((tt, d), lambda i, j: (i, 0)),
            pl.BlockSpec((tn, d), lambda i, j: (j, 0)),
            pl.BlockSpec((tt, tn), lambda i, j: (i, g_block + j)),
            blk,
        ],
        out_specs=[blk, blk],
        out_shape=[jax.ShapeDtypeStruct((t, d), F32), jax.ShapeDtypeStruct((t, d), BF16)],
        args=(d_br_a, w_lru_up, proj, h), carry=carry)


def _lru_bwd(dh, xc, h, gates, proj, conv_w, w_a, w_x, lam, carry=None):
    t, dr = dh.shape
    cb = LRU_CB
    hd = LRU_HEAD_DIM
    per = cb // hd
    tc = _tile(t, 256)
    ncb, ntc = dr // cb, t // tc

    def body(dh_ref, xc_ref, h_ref, hp_ref, r_ref, i_ref, a_ref, mult_ref, xp_ref, cw_ref, wa_ref, wx_ref,
             lam_ref, dxp_ref, dwa_ref, dba_ref, dwx_ref, dbx_ref, dlam_ref, dcw_ref, dcb_ref,
             nextd_s, anext_s, gnext_s, tmp_s, wa_s, wx_s):
        c = pl.program_id(1)
        rc = ntc - 1 - c

        @pl.when(c == 0)
        def _():
            nextd_s[...] = jnp.zeros_like(nextd_s)
            anext_s[...] = jnp.zeros_like(anext_s)
            gnext_s[...] = jnp.zeros_like(gnext_s)
            for ref in (dwa_ref, dba_ref, dwx_ref, dbx_ref, dlam_ref, dcw_ref, dcb_ref):
                ref[...] = jnp.zeros_like(ref)
            _fill_block_diag(wa_ref, wa_s)
            _fill_block_diag(wx_ref, wx_s)

        xc = xc_ref[...]
        wa, wx, lam = wa_s[...], wx_s[...], lam_ref[...]
        xcb = xc.astype(BF16)
        r, i, a, mult = r_ref[...], i_ref[...], a_ref[...], mult_ref[...]
        sp = _softplus_neg(lam)
        row = lax.broadcasted_iota(jnp.int32, xc.shape, 0)
        h = h_ref[...]
        hp = jnp.where(rc == 0, 0.0, hp_ref[...])
        hprev = jnp.where(row >= 1, pltpu.roll(h, 1, 0), pltpu.roll(hp, 1, 0))

        def up(v, nv, j):
            return jnp.where(row < tc - j, pltpu.roll(v, tc - j, 0), nv)

        av, bv = _scan_rows(up(a, anext_s[...], 1), dh_ref[...], reverse=True)
        gt = av * gnext_s[...] + bv
        tmp_s[...] = gt
        gnext_s[...] = tmp_s[0:1, :]
        tmp_s[...] = a
        anext_s[...] = tmp_s[0:1, :]

        da = gt * hprev
        ixc = i * xc
        d_mult = gt * ixc
        d_i = gt * mult * xc
        d_xc = gt * mult * i
        d_log_a = da * a - d_mult * (a * a) / mult
        d_pre_r = (d_log_a * ((-LRU_C) * sp)) * (r * (1.0 - r))
        d_pre_i = d_i * (i * (1.0 - i))
        d_sp = jnp.sum(d_log_a * ((-LRU_C) * r), axis=0, keepdims=True)
        dlam_ref[...] += d_sp * (-1.0 / (1.0 + jnp.exp(lam)))
        dpr = d_pre_r.astype(BF16)
        dpi = d_pre_i.astype(BF16)
        dba_ref[...] += jnp.sum(d_pre_r, axis=0, keepdims=True)
        dbx_ref[...] += jnp.sum(d_pre_i, axis=0, keepdims=True)
        pa = _dot_tn(xcb, dpr)
        px = _dot_tn(xcb, dpi)
        for k in range(per):
            dwa_ref[k] += pa[k * hd:(k + 1) * hd, k * hd:(k + 1) * hd]
            dwx_ref[k] += px[k * hd:(k + 1) * hd, k * hd:(k + 1) * hd]
        d_xc = d_xc + _dot_nt(dpr, wa) + _dot_nt(dpi, wx)

        nxt = nextd_s[...]
        xp = xp_ref[...].astype(F32)
        dxp = cw_ref[3:4, :] * d_xc
        dcw_ref[3:4, :] += jnp.sum(xp * d_xc, axis=0, keepdims=True)
        for j in (1, 2, 3):
            uj = up(d_xc, pltpu.roll(nxt, tc - j, 0), j)
            dxp = dxp + cw_ref[3 - j:4 - j, :] * uj
            dcw_ref[3 - j:4 - j, :] += jnp.sum(xp * uj, axis=0, keepdims=True)
        dcb_ref[...] += jnp.sum(d_xc, axis=0, keepdims=True)
        nextd_s[...] = d_xc
        dxp_ref[...] = dxp.astype(BF16)

    vec = pl.BlockSpec((1, cb), lambda j, c: (0, j))
    blk = pl.BlockSpec((tc, cb), lambda j, c: (ntc - 1 - c, j))
    mat = pl.BlockSpec((per, hd, hd), lambda j, c: (j, 0, 0))
    cwb = pl.BlockSpec((4, cb), lambda j, c: (0, j))
    return _call(
        body, name="lru_bwd", grid=(ncb, ntc),
        in_specs=[
            blk, blk, blk,
            pl.BlockSpec((tc, cb), lambda j, c: (jnp.maximum(ntc - 2 - c, 0), j)),
            blk, blk, blk, blk,
            blk, cwb, mat, mat, vec,
        ],
        out_specs=[blk, mat, vec, mat, vec, vec, cwb, vec],
        out_shape=[
            jax.ShapeDtypeStruct((t, dr), BF16),
            jax.ShapeDtypeStruct(w_a.shape, F32),
            jax.ShapeDtypeStruct((1, dr), F32),
            jax.ShapeDtypeStruct(w_x.shape, F32),
            jax.ShapeDtypeStruct((1, dr), F32),
            jax.ShapeDtypeStruct((1, dr), F32),
            jax.ShapeDtypeStruct((4, dr), F32),
            jax.ShapeDtypeStruct((1, dr), F32),
        ],
        scratch_shapes=[
            pltpu.VMEM((tc, cb), F32),
            pltpu.VMEM((1, cb), F32),
            pltpu.VMEM((1, cb), F32),
            pltpu.VMEM((tc, cb), F32),
            pltpu.VMEM((cb, cb), BF16),
            pltpu.VMEM((cb, cb), BF16),
        ],
        args=(dh, xc, h, h, *gates, proj, conv_w, w_a, w_x, lam), carry=carry)


def _pool_bwd(d_br_b, w_pool_upb, p, pool_w, pool_scale):
    t, d = d_br_b.shape
    nwb, dp, _ = w_pool_upb.shape
    tc = _tile(t, 256)
    ntc = t // tc
    ng = len(POOL_WINDOWS)

    def body(db_ref, wu_ref, p_ref, w_ref, sc_ref, dx_ref, dw_ref, dsc_ref, nz, n2, n4, n8, dp_s, dy_s):
        c = pl.program_id(0)
        rc = ntc - 1 - c

        @pl.when(c == 0)
        def _():
            for s in (nz, n2, n4, n8):
                s[...] = jnp.zeros_like(s)
            dw_ref[...] = jnp.zeros_like(dw_ref)
            dsc_ref[...] = jnp.zeros_like(dsc_ref)

        wu = jnp.concatenate([wu_ref[b] for b in range(nwb)], axis=1)
        dy_s[...] = _dot_nt(db_ref[...], wu)
        for g in range(ng):
            sl = slice(g * POOL_GROUP_DIM, (g + 1) * POOL_GROUP_DIM)
            pg = p_ref[:, sl]
            dyg = dy_s[:, sl]
            wg = w_ref[g].astype(BF16)
            q = _dot_nn(pg, wg)
            dsc_ref[:, sl] += jnp.sum(dyg * q, axis=0, keepdims=True)
            dpw = (dyg * sc_ref[:, sl]).astype(BF16)
            dw_ref[g] += _dot_tn(pg, dpw)
            dp_s[:, sl] = _dot_nt(dpw, wg)

        dpv = dp_s[...]
        row = lax.broadcasted_iota(jnp.int32, dpv.shape, 0)
        col = lax.broadcasted_iota(jnp.int32, dpv.shape, 1)
        win = _pool_select(col, POOL_WINDOWS)
        cnt = jnp.minimum(rc * tc + row + 1, win).astype(F32)
        z = dpv / cnt

        def up(v, nv, j):
            return jnp.where(row < tc - j, pltpu.roll(v, tc - j, 0), pltpu.roll(nv[...], tc - j, 0))

        u2 = z + up(z, nz, 1)
        u4 = u2 + up(u2, n2, 2)
        u8 = u4 + up(u4, n4, 4)
        u16 = u8 + up(u8, n8, 8)
        nz[...] = z
        n2[...] = u2
        n4[...] = u4
        n8[...] = u8
        dx_ref[...] = (_pool_select(col, (u2, u4, u8, u16)) - dpv).astype(BF16)

    blk = pl.BlockSpec((tc, dp), lambda c: (ntc - 1 - c, 0))
    full_w = pl.BlockSpec(pool_w.shape, lambda c: (0, 0, 0))
    vec = pl.BlockSpec((1, dp), lambda c: (0, 0))
    return _call(
        body, name="pool_bwd", grid=(ntc,),
        in_specs=[pl.BlockSpec((tc, d), lambda c: (ntc - 1 - c, 0)),
                  pl.BlockSpec(w_pool_upb.shape, lambda c: (0, 0, 0)), blk, full_w, vec],
        out_specs=[blk, full_w, vec],
        out_shape=[
            jax.ShapeDtypeStruct((t, dp), BF16),
            jax.ShapeDtypeStruct(pool_w.shape, F32),
            jax.ShapeDtypeStruct((1, dp), F32),
        ],
        scratch_shapes=[pltpu.VMEM((tc, dp), F32)] * 6,
        args=(d_br_b, w_pool_upb, p, pool_w, pool_scale))[0]


def _win_bwd_norm(parts, w_int, dx2, x, g1, carry=None):
    t, d = x.shape
    tk = 512
    tt = _tile(t, 1024)
    bounds = []
    k0 = 0
    for part in parts:
        assert part.shape[1] % tk == 0
        bounds.append((k0, k0 + part.shape[1] // tk))
        k0 += part.shape[1] // tk
    nk = k0
    assert nk * tk == w_int.shape[0]
    np_ = len(parts)

    def body(*refs):
        p_refs = refs[:np_]
        w_ref, dx2_ref, x_ref, g_ref, gx_ref, dg_ref, acc = refs[np_:]
        i, kk = pl.program_id(0), pl.program_id(1)

        @pl.when(kk == 0)
        def _():
            acc[...] = jnp.zeros_like(acc)

        @pl.when((i == 0) & (kk == 0))
        def _():
            dg_ref[...] = jnp.zeros_like(dg_ref)

        for (lo, hi), p_ref in zip(bounds, p_refs):
            @pl.when((kk >= lo) & (kk < hi))
            def _(p_ref=p_ref):
                acc[...] += _dot_nn(p_ref[...], w_ref[...])

        @pl.when(kk == nk - 1)
        def _():
            def tail(rows):
                xhat, r = _rms_hat(x_ref[rows, :])
                dx, dg = _rms_bwd(acc[rows, :], xhat, r, g_ref[...])
                gx_ref[rows, :] = dx2_ref[rows, :] + dx
                dg_ref[...] += dg

            _row_chunks(tt, tail)

    def part_spec(lo, hi):
        return pl.BlockSpec((tt, tk), lambda i, kk: (i, jnp.clip(kk - lo, 0, hi - lo - 1)))

    row = pl.BlockSpec((tt, d), lambda i, kk: (i, 0))
    vec = pl.BlockSpec((1, d), lambda i, kk: (0, 0))
    return _call(
        body, name="win_bwd_norm", grid=(t // tt, nk),
        in_specs=[part_spec(lo, hi) for lo, hi in bounds]
        + [pl.BlockSpec((tk, d), lambda i, kk: (kk, 0)), row, row, vec],
        out_specs=[row, vec],
        out_shape=[jax.ShapeDtypeStruct((t, d), F32), jax.ShapeDtypeStruct((1, d), F32)],
        scratch_shapes=[pltpu.VMEM((tt, d), F32)],
        args=(*parts, w_int, dx2, x, g1), carry=carry)


def _adam_math(w, g, m, v):
    m = ADAM_B1 * m + (1.0 - ADAM_B1) * g
    v = ADAM_B2 * v + (1.0 - ADAM_B2) * (g * g)
    m_hat = m / (1.0 - ADAM_B1 ** ADAM_STEP)
    v_hat = v / (1.0 - ADAM_B2 ** ADAM_STEP)
    delta = -ADAM_LR * (m_hat / (jnp.sqrt(v_hat) + ADAM_EPS) + ADAM_WD * w)
    return delta, m, v


def _adamw_big(ws, gs, ms, vs):
    n = len(ws)
    nb = 4
    pair = [isinstance(g, tuple) for g in gs]

    def body(*refs):
        p = 0
        ins = []
        for a in range(n):
            k = 5 if pair[a] else 4
            ins.append(refs[p:p + k])
            p += k
        for a in range(n):
            g_out, d_ref, nm_ref, nv_ref = refs[p + 4 * a:p + 4 * a + 4]
            if pair[a]:
                w_ref, own_ref, recv_ref, m_ref, v_ref = ins[a]
                g = own_ref[...]
                for k in range(3):
                    g = g + recv_ref[k].astype(F32)
            else:
                w_ref, g_ref, m_ref, v_ref = ins[a]
                g = g_ref[...]
            dl, m, v = _adam_math(w_ref[...], g, m_ref[...], v_ref[...])
            g_out[...] = g
            d_ref[...] = dl
            nm_ref[...] = m
            nv_ref[...] = v

    in_specs, out_specs, out_shape, args = [], [], [], []
    for a, (w, g, m, v) in enumerate(zip(ws, gs, ms, vs)):
        rows, cols = w.shape
        blk = pl.BlockSpec((rows // nb, cols), lambda i: (i, 0))
        if pair[a]:
            in_specs += [blk, pl.BlockSpec((None, rows // nb, cols), lambda i: (0, i, 0)),
                         pl.BlockSpec((3, rows // nb, cols), lambda i: (0, i, 0)), blk, blk]
            args += [w, g[0], g[1], m, v]
        else:
            in_specs += [blk] * 4
            args += [w, g, m, v]
        out_specs += [blk] * 4
        out_shape += [jax.ShapeDtypeStruct(w.shape, F32)] * 4
    outs = _call(body, name="adamw_big", grid=(nb,), in_specs=in_specs, out_specs=out_specs,
                 out_shape=out_shape, args=args)[0]
    return [tuple(outs[4 * a:4 * a + 4]) for a in range(n)]


SMALL_ORDER = ("norm_mix_pre", "norm_mix_post", "norm_mlp_pre", "norm_mlp_post", "b_gate", "conv_w", "conv_b",
               "lru_w_a", "lru_b_a", "lru_w_x", "lru_b_x", "lru_lambda", "pool_w", "pool_scale")
VEC_ROW = dict(norm_mix_pre=0, norm_mix_post=1, norm_mlp_pre=2, norm_mlp_post=3, conv_b=6, lru_b_a=7,
               lru_b_x=8, lru_lambda=9)
ROW_B_GATE, ROW_POOL_SCALE, ROW_CONV_W, ROW_LOSS, N_VEC_ROWS = 4, 10, 11, 15, 16


def _adamw_small(vec_parts, g_pool, g_wa, g_wx, me, params):
    d = vec_parts.shape[2]
    names = SMALL_ORDER
    n = len(names)
    cw_cols = params["conv_w"][0].shape[2]

    def body(me_ref, vec_ref, vecc_ref, gp_ref, gwa_ref, gwx_ref, *refs):
        wmv = refs[:3 * n]
        loss_ref = refs[3 * n]
        outs = refs[3 * n + 1:3 * n + 1 + 4 * n]
        vs, vsc = refs[3 * n + 1 + 4 * n:]
        acc, accc = vec_ref[0], vecc_ref[0]
        for k in range(1, N_DEV):
            acc = acc + vec_ref[k]
            accc = accc + vecc_ref[k]
        vs[...] = acc
        vsc[...] = accc
        loss_ref[...] = vs[ROW_LOSS:ROW_LOSS + 1, 0:128]

        def upd(a, g, idx):
            w_ref, m_ref, v_ref = wmv[3 * a:3 * a + 3]
            g_ref, d_ref, nm_ref, nv_ref = outs[4 * a:4 * a + 4]
            dl, m, v = _adam_math(w_ref[idx], g, m_ref[idx], v_ref[idx])
            g_ref[idx] = g
            d_ref[idx] = dl
            nm_ref[idx] = m
            nv_ref[idx] = v

        for a, name in enumerate(names):
            if name in VEC_ROW:
                r = VEC_ROW[name]
                upd(a, vs[r:r + 1, :], (slice(None), slice(None)))
            elif name == "b_gate":
                for half in range(2):
                    r = ROW_B_GATE + half
                    upd(a, vs[r:r + 1, :], (slice(None), slice(half * d, (half + 1) * d)))
            elif name == "pool_scale":
                width = params[name][0].shape[1]
                upd(a, vs[ROW_POOL_SCALE:ROW_POOL_SCALE + 1, 0:width], (slice(None), slice(None)))
            elif name == "conv_w":
                upd(a, vsc[ROW_CONV_W:ROW_CONV_W + 4, :], (0,))
            elif name == "pool_w":
                upd(a, gp_ref[...], (Ellipsis,))
            elif name == "lru_w_a":
                upd(a, gwa_ref[...], (Ellipsis,))
            elif name == "lru_w_x":
                upd(a, gwx_ref[...], (Ellipsis,))
            else:
                raise ValueError(name)

    def whole(shape):
        nd = len(shape)
        return pl.BlockSpec(tuple(shape), lambda i, me_ref: (0,) * nd)

    in_specs = [
        whole(vec_parts.shape),
        pl.BlockSpec((N_DEV, N_VEC_ROWS, cw_cols), lambda i, me_ref: (0, 0, me_ref[0])),
        whole(g_pool.shape), whole(g_wa.shape), whole(g_wx.shape),
    ]
    args = [vec_parts, vec_parts, g_pool, g_wa, g_wx]
    out_specs = [whole((1, 128))]
    out_shape = [jax.ShapeDtypeStruct((1, 128), F32)]
    for name in names:
        for arr in params[name]:
            in_specs.append(whole(arr.shape))
            args.append(arr)
        shp = params[name][0].shape
        out_specs += [whole(shp)] * 4
        out_shape += [jax.ShapeDtypeStruct(shp, F32)] * 4
    grid_spec = pltpu.PrefetchScalarGridSpec(
        num_scalar_prefetch=1, grid=(1,), in_specs=in_specs, out_specs=out_specs,
        scratch_shapes=[pltpu.VMEM((N_VEC_ROWS, d), F32), pltpu.VMEM((N_VEC_ROWS, cw_cols), F32)])
    outs = pl.pallas_call(
        body, name="adamw_small", grid_spec=grid_spec, out_shape=out_shape,
        compiler_params=pltpu.CompilerParams(
            dimension_semantics=("arbitrary",), vmem_limit_bytes=V7X_VMEM_LIMIT_BYTES),
    )(me, *_in_hbm(args))
    return outs[0], {name: tuple(outs[1 + 4 * a:5 + 4 * a]) for a, name in enumerate(names)}


def _rs_sum(fulls, recvs, shard_ids, slot_ids, name):
    n = len(fulls)

    def body(sh_ref, sl_ref, *refs):
        s = pl.program_id(0)
        for a in range(n):
            full_ref, recv_ref = refs[2 * a], refs[2 * a + 1]
            own_ref, send_ref = refs[2 * n + 2 * a], refs[2 * n + 2 * a + 1]
            v = full_ref[...] + recv_ref[...].astype(F32)

            @pl.when(s == 0)
            def _(own_ref=own_ref, v=v):
                own_ref[...] = v

            @pl.when(s > 0)
            def _(send_ref=send_ref, v=v):
                send_ref[...] = v.astype(send_ref.dtype)

    in_specs, out_specs, out_shape, args = [], [], [], []
    for full, recv in zip(fulls, recvs):
        r, rest = recv.shape[1], tuple(recv.shape[2:])
        zeros = (0,) * len(rest)
        in_specs += [
            pl.BlockSpec((r,) + rest, lambda s, sh, sl, zeros=zeros: (sh[s],) + zeros),
            pl.BlockSpec((None, r) + rest, lambda s, sh, sl, zeros=zeros: (sl[s], 0) + zeros),
        ]
        out_specs += [
            pl.BlockSpec((None, r) + rest, lambda s, sh, sl, zeros=zeros: (0, 0) + zeros),
            pl.BlockSpec((None, r) + rest, lambda s, sh, sl, zeros=zeros: (jnp.maximum(s - 1, 0), 0) + zeros),
        ]
        out_shape += [jax.ShapeDtypeStruct((1, r) + rest, F32), jax.ShapeDtypeStruct((3, r) + rest, recv.dtype)]
        args += [full, recv]
    grid_spec = pltpu.PrefetchScalarGridSpec(
        num_scalar_prefetch=2, grid=(4,), in_specs=in_specs, out_specs=out_specs)
    outs = pl.pallas_call(
        body,
        name=name,
        grid_spec=grid_spec,
        out_shape=out_shape,
        compiler_params=pltpu.CompilerParams(
            dimension_semantics=("arbitrary",), vmem_limit_bytes=V7X_VMEM_LIMIT_BYTES),
    )(shard_ids, slot_ids, *_in_hbm(args))
    return [(outs[2 * a], outs[2 * a + 1]) for a in range(n)]


def _finals(pairs, name, carry=None):
    nb = 4
    n = len(pairs)

    def body(*refs):
        for a in range(n):
            own_ref, recv_ref = refs[2 * a], refs[2 * a + 1]
            acc = own_ref[...]
            for k in range(3):
                acc = acc + recv_ref[k].astype(F32)
            refs[2 * n + a][...] = acc

    in_specs, out_specs, out_shape, args = [], [], [], []
    for own, recv in pairs:
        _, rows, cols = own.shape
        in_specs += [pl.BlockSpec((None, rows // nb, cols), lambda i: (0, i, 0)),
                     pl.BlockSpec((3, rows // nb, cols), lambda i: (0, i, 0))]
        args += [own, recv]
        out_specs.append(pl.BlockSpec((rows // nb, cols), lambda i: (i, 0)))
        out_shape.append(jax.ShapeDtypeStruct((rows, cols), F32))
    return _call(body, name=name, grid=(nb,), in_specs=in_specs, out_specs=out_specs,
                 out_shape=out_shape, args=args, carry=carry)


def _rs_sums(fulls_f32, recv1, tag):
    x, y, c = _place()
    qs = jnp.stack([2 * x + y, 2 * (1 - x) + y, 2 * x + (1 - y), 2 * (1 - x) + (1 - y)]).astype(jnp.int32)
    shard_ids = 2 * qs + c
    return _rs_sum(fulls_f32, recv1, shard_ids, qs, "rs_sum_" + tag)


def _rs_level1(fulls_f32, fulls_send, tag):
    recv1 = _run_plan(_rs_sibling_plan(fulls_send), "rs_sibling_" + tag)
    return _rs_sums(fulls_f32, recv1, tag)


def _rows(g):
    return g.reshape(g.shape[0] * g.shape[1], g.shape[2])


def kernel(x, norm_mix_pre, norm_mix_post, norm_mlp_pre, norm_mlp_post, w_in, b_gate, conv_w, conv_b, lru_w_a, lru_b_a, lru_w_x, lru_b_x, lru_lambda, pool_w, pool_scale, w_lru_up, w_pool_up, w_o, w_ff1, w_ff2, loss_target, m_norm_mix_pre, m_norm_mix_post, m_norm_mlp_pre, m_norm_mlp_post, m_w_in, m_b_gate, m_conv_w, m_conv_b, m_lru_w_a, m_lru_b_a, m_lru_w_x, m_lru_b_x, m_lru_lambda, m_pool_w, m_pool_scale, m_w_lru_up, m_w_pool_up, m_w_o, m_w_ff1, m_w_ff2, v_norm_mix_pre, v_norm_mix_post, v_norm_mlp_pre, v_norm_mlp_post, v_w_in, v_b_gate, v_conv_w, v_conv_b, v_lru_w_a, v_lru_b_a, v_lru_w_x, v_lru_b_x, v_lru_lambda, v_pool_w, v_pool_scale, v_w_lru_up, v_w_pool_up, v_w_o, v_w_ff1, v_w_ff2):
    t, d = x.shape[1], x.shape[2]
    d_rnn = conv_b.shape[1]
    d_pool = pool_scale.shape[1]
    per = LRU_CB // LRU_HEAD_DIM
    xi, yi, ci = _place()
    me = 4 * xi + 2 * yi + ci

    x2d = x[0]
    tgt = loss_target[0]

    s_in = w_in[0].T.astype(BF16)
    s_lu = w_lru_up[0].astype(BF16)
    s_pu = w_pool_up[0].astype(BF16)
    s_o = w_o[0].astype(BF16)
    s_f1 = w_ff1[0].astype(BF16)
    s_f2 = w_ff2[0].astype(BF16)
    s_cw = jnp.pad(conv_w[0], ((0, 4), (0, 0)))

    g_in, g_cw = _run_plan(_ag_plan([s_in, s_cw]), "ag_w_in")
    w_int = _rows(g_in)
    conv_w_full = jnp.transpose(g_cw[:, :4, :], (1, 0, 2)).reshape(4, d_rnn)

    wa_bd, wx_bd = lru_w_a[0], lru_w_x[0]
    pw = pool_w[0]
    pw_bf = pw.astype(BF16)

    pool_block = (2 * d_rnn) // d_pool
    ga_block = (2 * d_rnn + d_pool) // 512
    gb_block = ga_block + d // 512
    g_block = d_rnn // 512

    r_f1, r_f2 = s_f1.shape[0], s_f2.shape[0]
    f1_cut = r_f1 // 4
    f2_cut = (3 * r_f2) // 8
    plan = _join([_ag_plan([s_lu, s_pu, s_o]), _ag_plan([s_f1], pieces=[(0, f1_cut)])])
    (proj, h1), got = _norm_proj(x2d, norm_mix_pre, w_int, carry=plan)
    (g_lu, g_pu, g_o), (g_f1,) = plan.split(got)
    w_lu, w_og = _rows(g_lu), _rows(g_o)
    (y_lru, h, xc, *lru_gates), (g_f1,) = _lru_fwd(
        proj, conv_w_full, conv_b, wa_bd, lru_b_a, wx_bd, lru_b_x, lru_lambda,
        carry=_ag_plan([s_f1], pieces=[(f1_cut, r_f1 - f1_cut)], bufs=[g_f1]))
    y_pool, p = _pool_fwd(proj, pw_bf, pool_scale, pool_block)
    (br_a, br_b, mix), (g_f2,) = _branch_mix(
        y_lru, y_pool, w_lu, g_pu, proj, b_gate, ga_block, gb_block,
        carry=_ag_plan([s_f2], pieces=[(0, f2_cut)]))
    (m, x2, h3), _ = _wo_norm(mix, w_og, x2d, norm_mix_post, norm_mlp_pre)
    (rf,), (g_f2,) = _ff1(
        h3, g_f1, carry=_ag_plan([s_f2], pieces=[(f2_cut, r_f2 - f2_cut)], bufs=[g_f2]))
    w_f2 = _rows(g_f2)
    dy, df, dg4, loss_part = _ff2_loss(rf, w_f2, x2, norm_mlp_post, tgt)

    (gw_ff2_32, gw_ff2_16), _ = _wgrad(rf, df, "wgrad_ff2", square_a=True)
    (d_f1,), r1_ff2 = _ff2_bwd(df, w_f2, rf, carry=_rs_sibling_plan([gw_ff2_16]))
    ((own_ff2, send_ff2),) = _rs_sums([gw_ff2_32], r1_ff2, "ff2")
    cut2 = (5 * send_ff2.shape[1]) // 16
    (gw_ff1_32, gw_ff1_16), (r2_ff2,) = _wgrad_cols(
        h3, d_f1, s_f1.shape[1], s_f1.shape[1], "wgrad_ff1",
        carry=_rs_chips_plan([send_ff2], pieces=[(0, cut2)]))
    plan = _join([_rs_chips_plan([send_ff2], pieces=[(cut2, send_ff2.shape[1] - cut2)], bufs=[r2_ff2]),
                  _rs_sibling_plan([gw_ff1_16])])
    (dx2, dm, dg3, dg2), got = _ff1_bwd_norms(d_f1, g_f1, dy, x2, norm_mlp_pre, m, norm_mix_post, carry=plan)
    (r2_ff2,), r1_ff1 = plan.split(got)
    ((own_ff1, send_ff1),) = _rs_sums([gw_ff1_32], r1_ff1, "ff1")
    own_ff1, send_ff1 = own_ff1.reshape((1,) + s_f1.shape), send_ff1.reshape((3,) + s_f1.shape)
    cut = send_ff1.shape[1] // 4
    (gw_o_32, gw_o_16), _ = _wgrad(mix, dm, "wgrad_o")
    (d_br_a, d_br_b, p_ga, p_gb, dbg_a, dbg_b), (r2_ff1,) = _wo_bwd_mix(
        dm, w_og, br_a, br_b, proj, b_gate, ga_block, gb_block,
        carry=_rs_chips_plan([send_ff1], pieces=[(0, cut)]))
    (gw_lu_32, gw_lu_16), _ = _wgrad(y_lru, d_br_a, "wgrad_lru_up")
    (gw_pu_32, gw_pu_16), _ = _wgrad_cols(y_pool, d_br_b, s_pu.shape[1], d, "wgrad_pool_up")
    (dh, p_g), r1_mid = _lru_up_bwd(
        d_br_a, w_lu, proj, h, g_block,
        carry=_rs_sibling_plan([gw_o_16, gw_lu_16, gw_pu_16]))
    mid = _rs_sums([gw_o_32, gw_lu_32, gw_pu_32], r1_mid, "mid")
    (p_x, dwa, db_a, dwx, db_x, dlam, dconv_w, dconv_b), (r2_ff1,) = _lru_bwd(
        dh, xc, h, lru_gates, proj, conv_w_full, wa_bd, wx_bd, lru_lambda,
        carry=_rs_chips_plan([send_ff1], pieces=[(cut, send_ff1.shape[1] - cut)], bufs=[r2_ff1]))
    p_p, dpool_w, dpool_scale = _pool_bwd(d_br_b, g_pu, p, pw, pool_scale)
    parts = [p_x, p_g, p_p, p_ga, p_gb]
    gw_in, r2_mid = _wgrad_parts(parts, h1, "wgrad_in", carry=_rs_chips_plan([s for _, s in mid]))
    tail = _rs_level1([gw_in[0], dpool_w.reshape(N_DEV, -1, POOL_GROUP_DIM), dwa, dwx],
                      [gw_in[1], dpool_w.reshape(N_DEV, -1, POOL_GROUP_DIM), dwa, dwx], "in")
    (grad_x, dg1), r2_tail = _win_bwd_norm(parts, w_int, dx2, x2d, norm_mix_pre,
                                           carry=_rs_chips_plan([s for _, s in tail]))

    def flat2(a):
        return a.reshape(a.shape[0], -1, a.shape[-1])

    fin_small, _ = _finals([
        (flat2(tail[1][0]), flat2(r2_tail[1])), (flat2(tail[2][0]), flat2(r2_tail[2])),
        (flat2(tail[3][0]), flat2(r2_tail[3])),
    ], "rs_finals_small")

    def pad_row(a):
        return jnp.pad(a, ((0, 0), (0, d - a.shape[1])))

    vecs = jnp.concatenate([dg1, dg2, dg3, dg4, dbg_a, dbg_b, dconv_b, db_a, db_x, dlam,
                            pad_row(dpool_scale), dconv_w, pad_row(loss_part)], axis=0)
    assert vecs.shape[0] == N_VEC_ROWS
    vec_parts, g_pool, g_wa, g_wx = _run_plan(_ag_plan([vecs] + fin_small), "ag_tail")

    big_names = ["w_in", "w_lru_up", "w_pool_up", "w_o", "w_ff1", "w_ff2"]
    big_w = [w_in[0].T, w_lru_up[0], w_pool_up[0], w_o[0], w_ff1[0], w_ff2[0]]
    big_g = [(tail[0][0], r2_tail[0]), (mid[1][0], r2_mid[1]),
             (mid[2][0].reshape((1,) + s_pu.shape), r2_mid[2].reshape((3,) + s_pu.shape)),
             (mid[0][0], r2_mid[0]), (own_ff1, r2_ff1), (own_ff2, r2_ff2)]
    big_m = [m_w_in[0].T, m_w_lru_up[0], m_w_pool_up[0], m_w_o[0], m_w_ff1[0], m_w_ff2[0]]
    big_v = [v_w_in[0].T, v_w_lru_up[0], v_w_pool_up[0], v_w_o[0], v_w_ff1[0], v_w_ff2[0]]
    big_out = _adamw_big(big_w, big_g, big_m, big_v)
    big_out[0] = tuple(o.T for o in big_out[0])

    small = dict(
        norm_mix_pre=(norm_mix_pre, m_norm_mix_pre, v_norm_mix_pre),
        norm_mix_post=(norm_mix_post, m_norm_mix_post, v_norm_mix_post),
        norm_mlp_pre=(norm_mlp_pre, m_norm_mlp_pre, v_norm_mlp_pre),
        norm_mlp_post=(norm_mlp_post, m_norm_mlp_post, v_norm_mlp_post),
        b_gate=(b_gate, m_b_gate, v_b_gate), conv_w=(conv_w, m_conv_w, v_conv_w),
        conv_b=(conv_b, m_conv_b, v_conv_b), lru_w_a=(lru_w_a, m_lru_w_a, v_lru_w_a),
        lru_b_a=(lru_b_a, m_lru_b_a, v_lru_b_a), lru_w_x=(lru_w_x, m_lru_w_x, v_lru_w_x),
        lru_b_x=(lru_b_x, m_lru_b_x, v_lru_b_x), lru_lambda=(lru_lambda, m_lru_lambda, v_lru_lambda),
        pool_w=(pool_w, m_pool_w, v_pool_w), pool_scale=(pool_scale, m_pool_scale, v_pool_scale))
    loss_row, small_out = _adamw_small(
        vec_parts, g_pool.reshape(pool_w.shape), g_wa.reshape(lru_w_a.shape), g_wx.reshape(lru_w_x.shape),
        jnp.reshape(me, (1,)).astype(jnp.int32), small)
    grads = {n: o[0] for n, o in small_out.items()}
    delta = {n: o[1] for n, o in small_out.items()}
    new_m = {n: o[2] for n, o in small_out.items()}
    new_v = {n: o[3] for n, o in small_out.items()}

    for name, (g, dl, nm, nv) in zip(big_names, big_out):
        grads[name], delta[name], new_m[name], new_v[name] = g[None], dl[None], nm[None], nv[None]

    loss = loss_row[0, 0]
    order = ["norm_mix_pre", "norm_mix_post", "norm_mlp_pre", "norm_mlp_post", "w_in", "b_gate", "conv_w",
             "conv_b", "lru_w_a", "lru_b_a", "lru_w_x", "lru_b_x", "lru_lambda", "pool_w", "pool_scale",
             "w_lru_up", "w_pool_up", "w_o", "w_ff1", "w_ff2"]
    return (loss, grad_x[None], *[grads[n] for n in order], *[delta[n] for n in order],
            *[new_m[n] for n in order], *[new_v[n] for n in order])
```

```python
import functools
import math
import operator
import types

import jax
import jax.numpy as jnp
from jax import lax
from jax.experimental import pallas as pl
from jax.experimental.pallas import tpu as pltpu

F32 = jnp.float32
BF16 = jnp.bfloat16
NORM_EPS = 1e-6
LRU_C = 8.0
N_LRU_HEADS = 16
LRU_HEAD_DIM = 64
POOL_WINDOWS = (2, 4, 8, 16)
POOL_GROUP_DIM = 128
ADAM_LR = 0.001
ADAM_B1 = 0.9
ADAM_B2 = 0.999
ADAM_EPS = 1e-08
ADAM_WD = 0.01
ADAM_STEP = 10
N_DEV = 8
V7X_VMEM_LIMIT_BYTES = 56 * 1024 * 1024
LRU_CB = 256
MESH = pl.DeviceIdType.MESH
ANY = pl.BlockSpec(memory_space=pl.ANY)


def _tile(n, pref):
    t = min(n, pref)
    assert n % t == 0, (n, pref)
    return t


def _dot_nn(a, b):
    return lax.dot_general(a, b, (((1,), (0,)), ((), ())), preferred_element_type=F32)


def _dot_nt(a, b):
    return lax.dot_general(a, b, (((1,), (1,)), ((), ())), preferred_element_type=F32)


def _dot_tn(a, b):
    return lax.dot_general(a, b, (((0,), (0,)), ((), ())), preferred_element_type=F32)


def _row_chunks(n_rows, fn, chunk=256):
    chunk = min(chunk, n_rows)
    assert n_rows % chunk == 0

    def step(r, carry):
        fn(pl.ds(pl.multiple_of(r * chunk, chunk), chunk))
        return carry

    lax.fori_loop(0, n_rows // chunk, step, 0)


def _sig(x):
    return 1.0 / (1.0 + jnp.exp(-x))


def _rms_hat(x):
    r = lax.rsqrt(jnp.mean(x * x, axis=-1, keepdims=True) + NORM_EPS)
    return x * r, r


def _rms_bwd(dn, xhat, r, g):
    q = dn * g
    dx = r * (q - xhat * jnp.mean(q * xhat, axis=-1, keepdims=True))
    dg = jnp.sum(dn * xhat, axis=0, keepdims=True)
    return dx, dg


_GELU_K = math.sqrt(2.0 / math.pi)
_GELU_C = 0.044715


def _gelu_and_grad(g):
    t = jnp.tanh(_GELU_K * (g + _GELU_C * g * g * g))
    val = 0.5 * g * (1.0 + t)
    grad = 0.5 * (1.0 + t) + 0.5 * g * (1.0 - t * t) * (_GELU_K * (1.0 + 3.0 * _GELU_C * g * g))
    return val, grad


def _softplus_neg(lam):
    z = -lam
    e = jnp.exp(-jnp.abs(z))
    u = 1.0 + e
    d = u - 1.0
    l1p = jnp.where(d == 0.0, e, jnp.log(u) * (e / jnp.where(d == 0.0, 1.0, d)))
    return jnp.maximum(z, 0.0) + l1p


def _lru_gates(xc, wa, ba, wx, bx, lam):
    xcb = xc.astype(BF16)
    r = _sig(_dot_nn(xcb, wa) + ba)
    i = _sig(_dot_nn(xcb, wx) + bx)
    sp = _softplus_neg(lam)
    log_a = (-LRU_C) * r * sp
    a = jnp.exp(log_a)
    mult = jnp.sqrt(-jnp.tanh(log_a) * (1.0 + a * a))
    return xcb, r, i, sp, log_a, a, mult


def _place():
    return lax.axis_index("x"), lax.axis_index("y"), lax.axis_index("c")


def _ag_plan(shards, pieces=None, bufs=None):
    na = len(shards)
    n_kinds = 7

    def parts(ins, outs, sems):
        send_sems, recv_sems, local_sems = sems
        x, y, c = _place()
        me, sibling = (x, y, c), (x, y, 1 - c)
        x_nb, y_nb, diag = (1 - x, y), (x, 1 - y), (1 - x, 1 - y)
        relay_src = (c * (1 - x) + (1 - c) * x, c * y + (1 - c) * (1 - y))
        relay_dst = (c * x + (1 - c) * (1 - x), c * (1 - y) + (1 - c) * y)

        def own(a):
            return ins[a] if pieces is None else ins[a].at[pl.ds(*pieces[a])]

        def slot(a, px, py, pc):
            idx = 4 * px + 2 * py + pc
            return outs[a].at[idx] if pieces is None else outs[a].at[idx, pl.ds(*pieces[a])]

        def copy(a, k, block, to, src=None):
            return pltpu.make_async_remote_copy(
                src_ref=slot(a, *block) if src is None else src,
                dst_ref=slot(a, *block),
                send_sem=send_sems.at[a * n_kinds + k],
                recv_sem=recv_sems.at[a * n_kinds + k],
                device_id=to,
                device_id_type=MESH,
            )

        mine = [pltpu.make_async_copy(own(a), slot(a, *me), local_sems.at[a]) for a in range(na)]
        first, second, third = [], [], []
        for a in range(na):
            first += [copy(a, 0, me, sibling, src=own(a)), copy(a, 1, me, (*x_nb, c), src=own(a)),
                      copy(a, 2, me, (*y_nb, c), src=own(a))]
            second += [copy(a, 3, (*relay_src, c), (*relay_dst, c)), copy(a, 4, (*x_nb, c), sibling),
                       copy(a, 5, (*y_nb, c), sibling)]
            third.append(copy(a, 6, (*diag, c), sibling))
        return sibling, c, x_nb, y_nb, diag, copy, mine, first, second, third

    def start(ins, outs, sems):
        _, _, _, _, _, _, mine, first, _, _ = parts(ins, outs, sems)
        for cp in mine + first:
            cp.start()

    def middle(ins, outs, sems):
        _, c, x_nb, y_nb, _, copy, _, _, second, _ = parts(ins, outs, sems)
        for a in range(na):
            copy(a, 1, (*x_nb, c), (*x_nb, c)).wait_recv()
            copy(a, 2, (*y_nb, c), (*y_nb, c)).wait_recv()
        for cp in second:
            cp.start()

    def finish(ins, outs, sems):
        sibling, c, x_nb, y_nb, diag, copy, mine, first, second, third = parts(ins, outs, sems)
        for a in range(na):
            copy(a, 3, (*diag, c), (*diag, c)).wait_recv()
            third[a].start()
        for a in range(na):
            copy(a, 0, sibling, sibling).wait_recv()
            copy(a, 4, (*x_nb, 1 - c), sibling).wait_recv()
            copy(a, 5, (*y_nb, 1 - c), sibling).wait_recv()
            copy(a, 6, (*diag, 1 - c), sibling).wait_recv()
        for cp in first + second + third:
            cp.wait_send()
        for cp in mine:
            cp.wait()

    return types.SimpleNamespace(
        ins=list(shards) + list(bufs or []),
        out_shapes=[jax.ShapeDtypeStruct((N_DEV,) + s.shape, s.dtype) for s in shards],
        sems=[pltpu.SemaphoreType.DMA((n_kinds * na,)), pltpu.SemaphoreType.DMA((n_kinds * na,)),
              pltpu.SemaphoreType.DMA((na,))],
        aliases=[(na + a, a) for a in range(na)] if bufs else [],
        peers=frozenset({"sibling", "neighbours"}), start=start, middle=middle, finish=finish)


def _rs_sibling_plan(fulls):
    na = len(fulls)
    rs = [f.shape[0] // N_DEV for f in fulls]

    def copies(ins, outs, sems):
        send_sems, recv_sems = sems
        x, y, c = _place()
        out = []
        for a in range(na):
            for q in range(4):
                shard = 2 * q + (1 - c)
                out.append(pltpu.make_async_remote_copy(
                    src_ref=ins[a].at[pl.ds(shard * rs[a], rs[a])],
                    dst_ref=outs[a].at[q],
                    send_sem=send_sems.at[a * 4 + q],
                    recv_sem=recv_sems.at[a * 4 + q],
                    device_id=(x, y, 1 - c),
                    device_id_type=MESH,
                ))
        return out

    def start(ins, outs, sems):
        for cp in copies(ins, outs, sems):
            cp.start()

    def finish(ins, outs, sems):
        for cp in copies(ins, outs, sems):
            cp.wait()

    return types.SimpleNamespace(
        ins=list(fulls),
        out_shapes=[jax.ShapeDtypeStruct((4, r) + f.shape[1:], f.dtype) for r, f in zip(rs, fulls)],
        sems=[pltpu.SemaphoreType.DMA((4 * na,)), pltpu.SemaphoreType.DMA((4 * na,))],
        peers=frozenset({"sibling"}), start=start, finish=finish)


def _rs_chips_plan(sends, pieces=None, bufs=None):
    na = len(sends)

    def copies(ins, outs, sems):
        send_sems, recv_sems = sems
        x, y, c = _place()
        chips = [(1 - x, y), (x, 1 - y), (1 - x, 1 - y)]
        out = []
        for a in range(na):
            for k, chip in enumerate(chips):
                rows = (k,) if pieces is None else (k, pl.ds(*pieces[a]))
                out.append(pltpu.make_async_remote_copy(
                    src_ref=ins[a].at[rows],
                    dst_ref=outs[a].at[rows],
                    send_sem=send_sems.at[a * 3 + k],
                    recv_sem=recv_sems.at[a * 3 + k],
                    device_id=(*chip, c),
                    device_id_type=MESH,
                ))
        return out

    def start(ins, outs, sems):
        for cp in copies(ins, outs, sems):
            cp.start()

    def finish(ins, outs, sems):
        for cp in copies(ins, outs, sems):
            cp.wait()

    return types.SimpleNamespace(
        ins=list(sends) + list(bufs or []),
        out_shapes=[jax.ShapeDtypeStruct(s.shape, s.dtype) for s in sends],
        sems=[pltpu.SemaphoreType.DMA((3 * na,)), pltpu.SemaphoreType.DMA((3 * na,))],
        aliases=[(na + a, a) for a in range(na)] if bufs else [],
        peers=frozenset({"chips"}), start=start, finish=finish)


def _join(plans):
    ins, outs, sems, aliases, offs = [], [], [], [], []
    for p in plans:
        offs.append((len(ins), len(outs), len(sems)))
        aliases += [(len(ins) + ci, len(outs) + co) for ci, co in getattr(p, "aliases", [])]
        ins += p.ins
        outs += p.out_shapes
        sems += p.sems

    def cut(p, off, i, o, s):
        return (i[off[0]:off[0] + len(p.ins)], o[off[1]:off[1] + len(p.out_shapes)],
                s[off[2]:off[2] + len(p.sems)])

    def start(i, o, s):
        for p, off in zip(plans, offs):
            p.start(*cut(p, off, i, o, s))

    def middle(i, o, s):
        for p, off in zip(plans, offs):
            if getattr(p, "middle", None) is not None:
                p.middle(*cut(p, off, i, o, s))

    def finish(i, o, s):
        for p, off in zip(plans, offs):
            p.finish(*cut(p, off, i, o, s))

    def split(results):
        return [list(results[off[1]:off[1] + len(p.out_shapes)]) for p, off in zip(plans, offs)]

    return types.SimpleNamespace(ins=ins, out_shapes=outs, sems=sems, aliases=aliases,
                                 peers=frozenset().union(*[p.peers for p in plans]),
                                 start=start, middle=middle, finish=finish, split=split)


COLLECTIVE_ID = {frozenset({"sibling"}): 0, frozenset({"chips"}): 1, frozenset({"sibling", "chips"}): 2,
                 frozenset({"sibling", "neighbours"}): 3}


def _handshake(peers):
    x, y, c = _place()
    devs = []
    if "sibling" in peers:
        devs.append((x, y, 1 - c))
    if "neighbours" in peers:
        devs += [(1 - x, y, c), (x, 1 - y, c)]
    if "chips" in peers:
        assert "neighbours" not in peers
        devs += [(1 - x, y, c), (x, 1 - y, c), (1 - x, 1 - y, c)]
    barrier = pltpu.get_barrier_semaphore()
    for dev in devs:
        pl.semaphore_signal(barrier, inc=1, device_id=dev, device_id_type=MESH)
    pl.semaphore_wait(barrier, len(devs))


def _in_hbm(args):
    return [pltpu.with_memory_space_constraint(a, pltpu.HBM) for a in args]


def _run_plan(plan, name):
    n_in, n_out = len(plan.ins), len(plan.out_shapes)

    def body(*refs):
        ins, outs, sems = refs[:n_in], refs[n_in:n_in + n_out], refs[n_in + n_out:]
        _handshake(plan.peers)
        plan.start(ins, outs, sems)
        if getattr(plan, "middle", None) is not None:
            plan.middle(ins, outs, sems)
        plan.finish(ins, outs, sems)

    return pl.pallas_call(
        body,
        name=name,
        in_specs=[ANY] * n_in,
        out_specs=[ANY] * n_out,
        out_shape=plan.out_shapes,
        scratch_shapes=plan.sems,
        input_output_aliases=dict(getattr(plan, "aliases", [])),
        compiler_params=pltpu.CompilerParams(collective_id=COLLECTIVE_ID[plan.peers]),
    )(*_in_hbm(plan.ins))


def _call(body, *, name, grid, in_specs, out_specs, out_shape, args, scratch_shapes=(), aliases=None,
          carry=None):
    n_in, n_out, n_scr = len(in_specs), len(out_shape), len(scratch_shapes)
    params = pltpu.CompilerParams(
        dimension_semantics=("arbitrary",) * len(grid), vmem_limit_bytes=V7X_VMEM_LIMIT_BYTES)
    if carry is None:
        outs = pl.pallas_call(
            body, name=name, grid=grid, in_specs=list(in_specs), out_specs=list(out_specs),
            out_shape=list(out_shape), scratch_shapes=list(scratch_shapes),
            input_output_aliases=aliases or {}, compiler_params=params)(*_in_hbm(args))
        return list(outs), []
    c_in, c_out = len(carry.ins), len(carry.out_shapes)

    def full(*refs):
        p = 0
        ins = refs[p:p + n_in]
        p += n_in
        cins = refs[p:p + c_in]
        p += c_in
        outs = refs[p:p + n_out]
        p += n_out
        couts = refs[p:p + c_out]
        p += c_out
        scr = refs[p:p + n_scr]
        csems = refs[p + n_scr:]
        ids = [pl.program_id(a) for a in range(len(grid))]
        first = functools.reduce(operator.and_, [i == 0 for i in ids])
        last = functools.reduce(operator.and_, [i == g - 1 for i, g in zip(ids, grid)])

        @pl.when(first)
        def _():
            _handshake(carry.peers)
            carry.start(cins, couts, csems)

        if getattr(carry, "middle", None) is not None:
            n_steps = math.prod(grid)
            flat = functools.reduce(lambda acc, ig: acc * ig[1] + ig[0], zip(ids, grid), 0)

            @pl.when(flat == (2 * n_steps) // 3)
            def _():
                carry.middle(cins, couts, csems)

        body(*ins, *outs, *scr)

        @pl.when(last)
        def _():
            carry.finish(cins, couts, csems)

    all_aliases = dict(aliases or {})
    all_aliases.update({n_in + ci: n_out + co for ci, co in getattr(carry, "aliases", [])})
    params = pltpu.CompilerParams(
        dimension_semantics=("arbitrary",) * len(grid), vmem_limit_bytes=V7X_VMEM_LIMIT_BYTES,
        collective_id=COLLECTIVE_ID[carry.peers])
    outs = pl.pallas_call(
        full, name=name, grid=grid,
        in_specs=list(in_specs) + [ANY] * c_in,
        out_specs=list(out_specs) + [ANY] * c_out,
        out_shape=list(out_shape) + list(carry.out_shapes),
        scratch_shapes=list(scratch_shapes) + list(carry.sems),
        input_output_aliases=all_aliases, compiler_params=params)(*_in_hbm(args), *_in_hbm(carry.ins))
    return list(outs[:n_out]), list(outs[n_out:])


def _norm_proj(x, g1, w_int, carry=None):
    t, d = x.shape
    n = w_int.shape[0]
    tt, tn = _tile(t, 2048), _tile(n, 512)

    def body(x_ref, g_ref, w_ref, proj_ref, h1_ref, h1_s):
        @pl.when(pl.program_id(1) == 0)
        def _():
            def norm_rows(rows):
                xhat, _ = _rms_hat(x_ref[rows, :])
                h = (xhat * g_ref[...]).astype(BF16)
                h1_s[rows, :] = h
                h1_ref[rows, :] = h

            _row_chunks(tt, norm_rows)

        proj_ref[...] = _dot_nt(h1_s[...], w_ref[...]).astype(BF16)

    return _call(
        body, name="norm_proj", grid=(t // tt, n // tn),
        in_specs=[
            pl.BlockSpec((tt, d), lambda i, j: (i, 0)),
            pl.BlockSpec((1, d), lambda i, j: (0, 0)),
            pl.BlockSpec((tn, d), lambda i, j: (j, 0)),
        ],
        out_specs=[
            pl.BlockSpec((tt, tn), lambda i, j: (i, j)),
            pl.BlockSpec((tt, d), lambda i, j: (i, 0)),
        ],
        out_shape=[jax.ShapeDtypeStruct((t, n), BF16), jax.ShapeDtypeStruct((t, d), BF16)],
        scratch_shapes=[pltpu.VMEM((tt, d), BF16)],
        args=(x, g1, w_int), carry=carry)


def _scan_rows(av, bv, reverse):
    tc = av.shape[0]
    row = lax.broadcasted_iota(jnp.int32, av.shape, 0)
    s = 1
    while s < tc:
        if s < 8:
            keep = (row < tc - s) if reverse else (row >= s)
            shift = (tc - s) if reverse else s
            a_sh = jnp.where(keep, pltpu.roll(av, shift, 0), 1.0)
            b_sh = jnp.where(keep, pltpu.roll(bv, shift, 0), 0.0)
            bv = av * b_sh + bv
            av = av * a_sh
        elif reverse:
            bv = jnp.concatenate([av[:tc - s] * bv[s:] + bv[:tc - s], bv[tc - s:]], axis=0)
            av = jnp.concatenate([av[:tc - s] * av[s:], av[tc - s:]], axis=0)
        else:
            bv = jnp.concatenate([bv[:s], av[s:] * bv[:tc - s] + bv[s:]], axis=0)
            av = jnp.concatenate([av[:s], av[s:] * av[:tc - s]], axis=0)
        s *= 2
    return av, bv


N_LRU_SAVED = 5


def _fill_block_diag(w_ref, bd_ref):
    bd_ref[...] = jnp.zeros_like(bd_ref)
    hd = LRU_HEAD_DIM
    for k in range(w_ref.shape[0]):
        bd_ref[k * hd:(k + 1) * hd, k * hd:(k + 1) * hd] = w_ref[k].astype(BF16)


def _lru_fwd(proj, conv_w, conv_b, w_a, b_a, w_x, b_x, lam, carry=None):
    t = proj.shape[0]
    dr = conv_b.shape[1]
    cb = LRU_CB
    tc = _tile(t, 256)
    ncb, ntc = dr // cb, t // tc

    def body(xp_ref, g_ref, cw_ref, cb_ref, wa_ref, ba_ref, wx_ref, bx_ref, lam_ref,
             y_ref, h_ref, saved_ref, prevx_s, hlast_s, wa_s, wx_s):
        c = pl.program_id(1)

        @pl.when(c == 0)
        def _():
            prevx_s[...] = jnp.zeros_like(prevx_s)
            hlast_s[...] = jnp.zeros_like(hlast_s)
            _fill_block_diag(wa_ref, wa_s)
            _fill_block_diag(wx_ref, wx_s)

        x = xp_ref[...].astype(F32)
        prev = prevx_s[...]
        row = lax.broadcasted_iota(jnp.int32, x.shape, 0)

        def sh(j):
            return jnp.where(row >= j, pltpu.roll(x, j, 0), pltpu.roll(prev, j, 0))

        xc = (cb_ref[...] + cw_ref[0:1, :] * sh(3) + cw_ref[1:2, :] * sh(2)
              + cw_ref[2:3, :] * sh(1) + cw_ref[3:4, :] * x)
        prevx_s[...] = x
        _, r, i, _, _, a, mult = _lru_gates(xc, wa_s[...], ba_ref[...], wx_s[...], bx_ref[...],
                                            lam_ref[...])
        for k, val in enumerate((xc, r, i, a, mult)):
            saved_ref[:, k * cb:(k + 1) * cb] = val
        av, bv = _scan_rows(a, mult * (i * xc), reverse=False)
        h = av * hlast_s[...] + bv
        h_ref[...] = h
        hlast_s[...] = h_ref[tc - 1:tc, :]
        gel, _ = _gelu_and_grad(g_ref[...].astype(F32))
        y_ref[...] = (h * gel).astype(BF16)

    vec = pl.BlockSpec((1, cb), lambda j, c: (0, j))
    blk = pl.BlockSpec((tc, cb), lambda j, c: (c, j))
    mat = pl.BlockSpec((cb // LRU_HEAD_DIM, LRU_HEAD_DIM, LRU_HEAD_DIM), lambda j, c: (j, 0, 0))
    return _call(
        body, name="lru_fwd", grid=(ncb, ntc),
        in_specs=[
            blk,
            pl.BlockSpec((tc, cb), lambda j, c: (c, ncb + j)),
            pl.BlockSpec((4, cb), lambda j, c: (0, j)),
            vec, mat, vec, mat, vec, vec,
        ],
        out_specs=[blk, blk, pl.BlockSpec((tc, N_LRU_SAVED * cb), lambda j, c: (c, j))],
        out_shape=[jax.ShapeDtypeStruct((t, dr), BF16), jax.ShapeDtypeStruct((t, dr), F32),
                   jax.ShapeDtypeStruct((t, N_LRU_SAVED * dr), F32)],
        scratch_shapes=[pltpu.VMEM((tc, cb), F32), pltpu.VMEM((1, cb), F32),
                        pltpu.VMEM((cb, cb), BF16), pltpu.VMEM((cb, cb), BF16)],
        args=(proj, proj, conv_w, conv_b, w_a, b_a, w_x, b_x, lam), carry=carry)


def _pool_select(col, vals):
    out = vals[3]
    for g in (2, 1, 0):
        out = jnp.where(col < (g + 1) * POOL_GROUP_DIM, vals[g], out)
    return out


def _pool_fwd(proj, pool_w, pool_scale, col_block):
    t = proj.shape[0]
    dp = pool_scale.shape[1]
    tc = _tile(t, 256)
    ntc = t // tc

    def body(x_ref, w_ref, sc_ref, y_ref, p_ref, px, p2, p4, p8):
        c = pl.program_id(0)

        @pl.when(c == 0)
        def _():
            for s in (px, p2, p4, p8):
                s[...] = jnp.zeros_like(s)

        x = x_ref[...].astype(F32)
        row = lax.broadcasted_iota(jnp.int32, x.shape, 0)
        col = lax.broadcasted_iota(jnp.int32, x.shape, 1)

        def sh(v, pv, j):
            return jnp.where(row >= j, pltpu.roll(v, j, 0), pltpu.roll(pv[...], j, 0))

        s2 = x + sh(x, px, 1)
        s4 = s2 + sh(s2, p2, 2)
        s8 = s4 + sh(s4, p4, 4)
        s16 = s8 + sh(s8, p8, 8)
        px[...] = x
        p2[...] = s2
        p4[...] = s4
        p8[...] = s8
        wsum = _pool_select(col, (s2, s4, s8, s16))
        win = _pool_select(col, POOL_WINDOWS)
        cnt = jnp.minimum(c * tc + row + 1, win).astype(F32)
        p = wsum / cnt - x
        pb = p.astype(BF16)
        p_ref[...] = pb
        for g in range(len(POOL_WINDOWS)):
            sl = slice(g * POOL_GROUP_DIM, (g + 1) * POOL_GROUP_DIM)
            yg = _dot_nn(pb[:, sl], w_ref[g]) * sc_ref[:, sl]
            y_ref[:, sl] = yg.astype(BF16)

    return _call(
        body, name="pool_fwd", grid=(ntc,),
        in_specs=[
            pl.BlockSpec((tc, dp), lambda c: (c, col_block)),
            pl.BlockSpec(pool_w.shape, lambda c: (0, 0, 0)),
            pl.BlockSpec((1, dp), lambda c: (0, 0)),
        ],
        out_specs=[pl.BlockSpec((tc, dp), lambda c: (c, 0))] * 2,
        out_shape=[jax.ShapeDtypeStruct((t, dp), BF16)] * 2,
        scratch_shapes=[pltpu.VMEM((tc, dp), F32)] * 4,
        args=(proj, pool_w, pool_scale))[0]


def _branch_mix(y_lru, y_pool, w_lru_up, w_pool_upb, proj, b_gate, ga_block, gb_block, carry=None):
    t, d = y_lru.shape
    dp = y_pool.shape[1]
    bw = w_pool_upb.shape[2]
    tt, tn = _tile(t, 1024), 512
    nj = d // tn

    def body(yl_ref, yp_ref, wl_ref, wp_ref, ga_ref, gb_ref, ba_ref, bb_ref, bra_ref, brb_ref, mix_ref):
        br_a = _dot_nn(yl_ref[...], wl_ref[...])
        wp = jnp.concatenate([wp_ref[b] for b in range(tn // bw)], axis=1)
        br_b = _dot_nn(yp_ref[...], wp)
        bra_ref[...] = br_a.astype(BF16)
        brb_ref[...] = br_b.astype(BF16)
        ga = _sig(ga_ref[...].astype(F32) + ba_ref[...])
        gb = _sig(gb_ref[...].astype(F32) + bb_ref[...])
        mix_ref[...] = (ga * br_a + gb * br_b).astype(BF16)

    out = pl.BlockSpec((tt, tn), lambda j, i: (i, j))
    return _call(
        body, name="branch_mix", grid=(nj, t // tt),
        in_specs=[
            pl.BlockSpec((tt, d), lambda j, i: (i, 0)),
            pl.BlockSpec((tt, dp), lambda j, i: (i, 0)),
            pl.BlockSpec((d, tn), lambda j, i: (0, j)),
            pl.BlockSpec((tn // bw, dp, bw), lambda j, i: (j, 0, 0)),
            pl.BlockSpec((tt, tn), lambda j, i: (i, ga_block + j)),
            pl.BlockSpec((tt, tn), lambda j, i: (i, gb_block + j)),
            pl.BlockSpec((1, tn), lambda j, i: (0, j)),
            pl.BlockSpec((1, tn), lambda j, i: (0, nj + j)),
        ],
        out_specs=[out, out, out],
        out_shape=[jax.ShapeDtypeStruct((t, d), BF16)] * 3,
        args=(y_lru, y_pool, w_lru_up, w_pool_upb, proj, proj, b_gate, b_gate), carry=carry)


def _wo_norm(mix, w_o, x, g2, g3, carry=None):
    t, d = x.shape
    tt = _tile(t, 512)

    def body(mix_ref, w_ref, x_ref, g2_ref, g3_ref, m_ref, x2_ref, h3_ref):
        m = _dot_nn(mix_ref[...], w_ref[...])
        m_ref[...] = m
        mhat, _ = _rms_hat(m)
        x2 = x_ref[...] + mhat * g2_ref[...]
        x2_ref[...] = x2
        xhat, _ = _rms_hat(x2)
        h3_ref[...] = (xhat * g3_ref[...]).astype(BF16)

    row = pl.BlockSpec((tt, d), lambda i: (i, 0))
    vec = pl.BlockSpec((1, d), lambda i: (0, 0))
    return _call(
        body, name="wo_norm", grid=(t // tt,),
        in_specs=[row, pl.BlockSpec((d, d), lambda i: (0, 0)), row, vec, vec],
        out_specs=[row, row, row],
        out_shape=[
            jax.ShapeDtypeStruct((t, d), F32),
            jax.ShapeDtypeStruct((t, d), F32),
            jax.ShapeDtypeStruct((t, d), BF16),
        ],
        args=(mix, w_o, x, g2, g3), carry=carry)


def _ff1(h3, w_ff1b, carry=None):
    t, d = h3.shape
    nb, _, tn = w_ff1b.shape
    tt = _tile(t, 2048)

    def body(h_ref, w_ref, rf_ref):
        rf_ref[...] = jnp.maximum(_dot_nn(h_ref[...], w_ref[...]), 0.0).astype(BF16)

    out = pl.BlockSpec((tt, tn), lambda i, j: (i, j))
    return _call(
        body, name="ff1", grid=(t // tt, nb),
        in_specs=[pl.BlockSpec((tt, d), lambda i, j: (i, 0)), pl.BlockSpec((None, d, tn), lambda i, j: (j, 0, 0))],
        out_specs=[out],
        out_shape=[jax.ShapeDtypeStruct((t, nb * tn), BF16)],
        args=(h3, w_ff1b), carry=carry)


def _ff2_loss(rf, w_ff2, x2, g4, target):
    t, k = rf.shape
    d = x2.shape[1]
    tt, tk = _tile(t, 1024), _tile(k, 1024)
    nk = k // tk

    def body(a_ref, w_ref, x2_ref, g_ref, tg_ref, dy_ref, df_ref, dg_ref, loss_ref, acc):
        i, kk = pl.program_id(0), pl.program_id(1)

        @pl.when(kk == 0)
        def _():
            acc[...] = jnp.zeros_like(acc)

        @pl.when((i == 0) & (kk == 0))
        def _():
            dg_ref[...] = jnp.zeros_like(dg_ref)
            loss_ref[...] = jnp.zeros_like(loss_ref)

        rf_tile = a_ref[...]
        acc[...] += _dot_nn(rf_tile * rf_tile, w_ref[...])

        @pl.when(kk == nk - 1)
        def _():
            def tail(rows):
                fhat, r = _rms_hat(acc[rows, :])
                g = g_ref[...]
                e = x2_ref[rows, :] + fhat * g - tg_ref[rows, :]
                loss_ref[...] += 0.5 * jnp.sum(jnp.mean(e * e, axis=-1, keepdims=True))
                dy = e * (1.0 / d)
                dy_ref[rows, :] = dy.astype(BF16)
                df, dg = _rms_bwd(dy, fhat, r, g)
                df_ref[rows, :] = df.astype(BF16)
                dg_ref[...] += dg

            _row_chunks(tt, tail)

    row = pl.BlockSpec((tt, d), lambda i, kk: (i, 0))
    vec = pl.BlockSpec((1, d), lambda i, kk: (0, 0))
    return _call(
        body, name="ff2_loss", grid=(t // tt, nk),
        in_specs=[
            pl.BlockSpec((tt, tk), lambda i, kk: (i, kk)),
            pl.BlockSpec((tk, d), lambda i, kk: (kk, 0)),
            row, vec, row,
        ],
        out_specs=[row, row, vec, pl.BlockSpec((1, 128), lambda i, kk: (0, 0))],
        out_shape=[
            jax.ShapeDtypeStruct((t, d), BF16),
            jax.ShapeDtypeStruct((t, d), BF16),
            jax.ShapeDtypeStruct((1, d), F32),
            jax.ShapeDtypeStruct((1, 128), F32),
        ],
        scratch_shapes=[pltpu.VMEM((tt, d), F32)],
        args=(rf, w_ff2, x2, g4, target))[0]


def _ff2_bwd(df, w_ff2, rf, carry=None):
    t, d = df.shape
    n = w_ff2.shape[0]
    tt, tn = _tile(t, 2048), _tile(n, 512)

    def body(df_ref, w_ref, rf_ref, out_ref):
        d_act = _dot_nt(df_ref[...], w_ref[...])
        out_ref[...] = (d_act * (2.0 * rf_ref[...].astype(F32))).astype(BF16)

    blk = pl.BlockSpec((tt, tn), lambda i, j: (i, j))
    return _call(
        body, name="ff2_bwd", grid=(t // tt, n // tn),
        in_specs=[pl.BlockSpec((tt, d), lambda i, j: (i, 0)), pl.BlockSpec((tn, d), lambda i, j: (j, 0)), blk],
        out_specs=[blk],
        out_shape=[jax.ShapeDtypeStruct((t, n), BF16)],
        args=(df, w_ff2, rf), carry=carry)


def _wgrad(a, b, name, prev=None, row_off=0, rows=None, carry=None, square_a=False):
    t, m = a.shape
    n = b.shape[1]
    rows = m if rows is None else rows
    tm, tk = _tile(m, 512), _tile(t, 2048)
    nk = t // tk
    assert row_off % tm == 0
    off = row_off // tm

    def body(*refs):
        a_ref, b_ref = refs[0], refs[1]
        o32_ref, o16_ref, acc = refs[-3], refs[-2], refs[-1]
        kk = pl.program_id(1)

        @pl.when(kk == 0)
        def _():
            acc[...] = jnp.zeros_like(acc)

        a_tile = a_ref[...]
        acc[...] += _dot_tn(a_tile * a_tile if square_a else a_tile, b_ref[...])

        @pl.when(kk == nk - 1)
        def _():
            o32_ref[...] = acc[...]
            o16_ref[...] = acc[...].astype(BF16)

    in_specs = [pl.BlockSpec((tk, tm), lambda i, kk: (kk, i)), pl.BlockSpec((tk, n), lambda i, kk: (kk, 0))]
    args = [a, b]
    aliases = {}
    if prev is not None:
        in_specs += [ANY, ANY]
        args += list(prev)
        aliases = {2: 0, 3: 1}
    out = pl.BlockSpec((tm, n), lambda i, kk: (off + i, 0))
    return _call(
        body, name=name, grid=(m // tm, nk),
        in_specs=in_specs, out_specs=[out, out],
        out_shape=[jax.ShapeDtypeStruct((rows, n), F32), jax.ShapeDtypeStruct((rows, n), BF16)],
        scratch_shapes=[pltpu.VMEM((tm, n), F32)],
        aliases=aliases, args=args, carry=carry)


def _wgrad_parts(parts, b, name, carry=None):
    t, n = b.shape
    tm = 512
    bounds = []
    lo = 0
    for part in parts:
        assert part.shape[0] == t and part.shape[1] % tm == 0
        bounds.append((lo, lo + part.shape[1] // tm))
        lo += part.shape[1] // tm
    nm = lo
    np_ = len(parts)

    def body(*refs):
        p_refs, b_ref, o32_ref, o16_ref = refs[:np_], refs[np_], refs[np_ + 1], refs[np_ + 2]
        i = pl.program_id(0)
        for (lo_p, hi_p), p_ref in zip(bounds, p_refs):
            @pl.when((i >= lo_p) & (i < hi_p))
            def _(p_ref=p_ref):
                res = _dot_tn(p_ref[...], b_ref[...])
                o32_ref[...] = res
                o16_ref[...] = res.astype(BF16)

    def part_spec(lo_p, hi_p):
        return pl.BlockSpec((t, tm), lambda i: (0, jnp.clip(i - lo_p, 0, hi_p - lo_p - 1)))

    out = pl.BlockSpec((tm, n), lambda i: (i, 0))
    return _call(
        body, name=name, grid=(nm,),
        in_specs=[part_spec(lo_p, hi_p) for lo_p, hi_p in bounds] + [pl.BlockSpec((t, n), lambda i: (0, 0))],
        out_specs=[out, out],
        out_shape=[jax.ShapeDtypeStruct((nm * tm, n), F32), jax.ShapeDtypeStruct((nm * tm, n), BF16)],
        args=(*parts, b), carry=carry)


def _wgrad_cols(a, b, bw, tn, name, carry=None):
    t, m = a.shape
    n = b.shape[1]
    per_step = tn // bw

    def body(a_ref, b_ref, o32_ref, o16_ref):
        res = _dot_tn(a_ref[...], b_ref[...])
        for blk in range(per_step):
            part = res[:, blk * bw:(blk + 1) * bw]
            o32_ref[blk] = part
            o16_ref[blk] = part.astype(BF16)

    out = pl.BlockSpec((per_step, m, bw), lambda j: (j, 0, 0))
    return _call(
        body, name=name, grid=(n // tn,),
        in_specs=[pl.BlockSpec((t, m), lambda j: (0, 0)), pl.BlockSpec((t, tn), lambda j: (0, j))],
        out_specs=[out, out],
        out_shape=[jax.ShapeDtypeStruct((n // bw, m, bw), F32), jax.ShapeDtypeStruct((n // bw, m, bw), BF16)],
        args=(a, b), carry=carry)


def _ff1_bwd_norms(d_f1, w_ff1b, dy, x2, g3, m, g2, carry=None):
    t, k = d_f1.shape
    d = x2.shape[1]
    bw = w_ff1b.shape[2]
    per_step = 2
    tt, tk = _tile(t, 1024), per_step * bw
    nk = k // tk

    def body(a_ref, w_ref, dy_ref, x2_ref, g3_ref, m_ref, g2_ref, dx2_ref, dm_ref, dg3_ref, dg2_ref, acc):
        i, kk = pl.program_id(0), pl.program_id(1)

        @pl.when(kk == 0)
        def _():
            acc[...] = jnp.zeros_like(acc)

        @pl.when((i == 0) & (kk == 0))
        def _():
            dg3_ref[...] = jnp.zeros_like(dg3_ref)
            dg2_ref[...] = jnp.zeros_like(dg2_ref)

        a_tile = a_ref[...]
        for b in range(per_step):
            acc[...] += _dot_nt(a_tile[:, b * bw:(b + 1) * bw], w_ref[b])

        @pl.when(kk == nk - 1)
        def _():
            def tail(rows):
                xhat, r3 = _rms_hat(x2_ref[rows, :])
                dx, dg3 = _rms_bwd(acc[rows, :], xhat, r3, g3_ref[...])
                dx2 = dy_ref[rows, :].astype(F32) + dx
                dx2_ref[rows, :] = dx2
                dg3_ref[...] += dg3
                mhat, r2 = _rms_hat(m_ref[rows, :])
                dm, dg2 = _rms_bwd(dx2, mhat, r2, g2_ref[...])
                dm_ref[rows, :] = dm.astype(BF16)
                dg2_ref[...] += dg2

            _row_chunks(tt, tail)

    row = pl.BlockSpec((tt, d), lambda i, kk: (i, 0))
    vec = pl.BlockSpec((1, d), lambda i, kk: (0, 0))
    return _call(
        body, name="ff1_bwd_norms", grid=(t // tt, nk),
        in_specs=[
            pl.BlockSpec((tt, tk), lambda i, kk: (i, kk)),
            pl.BlockSpec((per_step, d, bw), lambda i, kk: (kk, 0, 0)),
            row, row, vec, row, vec,
        ],
        out_specs=[row, row, vec, vec],
        out_shape=[
            jax.ShapeDtypeStruct((t, d), F32),
            jax.ShapeDtypeStruct((t, d), BF16),
            jax.ShapeDtypeStruct((1, d), F32),
            jax.ShapeDtypeStruct((1, d), F32),
        ],
        scratch_shapes=[pltpu.VMEM((tt, d), F32)],
        args=(d_f1, w_ff1b, dy, x2, g3, m, g2), carry=carry)


def _wo_bwd_mix(dm, w_o, br_a, br_b, proj, b_gate, ga_block, gb_block, carry=None):
    t, d = dm.shape
    tt, tn = _tile(t, 1024), 512
    nj = d // tn

    def body(dm_ref, w_ref, bra_ref, brb_ref, ga_ref, gb_ref, ba_ref, bb_ref,
             dbra_ref, dbrb_ref, dga_ref, dgb_ref, dba_ref, dbb_ref):
        i = pl.program_id(1)

        @pl.when(i == 0)
        def _():
            dba_ref[...] = jnp.zeros_like(dba_ref)
            dbb_ref[...] = jnp.zeros_like(dbb_ref)

        d_mix = _dot_nt(dm_ref[...], w_ref[...])
        ga = _sig(ga_ref[...].astype(F32) + ba_ref[...])
        gb = _sig(gb_ref[...].astype(F32) + bb_ref[...])
        dbra_ref[...] = (d_mix * ga).astype(BF16)
        dbrb_ref[...] = (d_mix * gb).astype(BF16)
        dga = d_mix * bra_ref[...].astype(F32) * (ga * (1.0 - ga))
        dgb = d_mix * brb_ref[...].astype(F32) * (gb * (1.0 - gb))
        dga_ref[...] = dga.astype(BF16)
        dgb_ref[...] = dgb.astype(BF16)
        dba_ref[...] += jnp.sum(dga, axis=0, keepdims=True)
        dbb_ref[...] += jnp.sum(dgb, axis=0, keepdims=True)

    blk = pl.BlockSpec((tt, tn), lambda j, i: (i, j))
    vec = pl.BlockSpec((1, tn), lambda j, i: (0, j))
    return _call(
        body, name="wo_bwd_mix", grid=(nj, t // tt),
        in_specs=[
            pl.BlockSpec((tt, d), lambda j, i: (i, 0)),
            pl.BlockSpec((tn, d), lambda j, i: (j, 0)),
            blk, blk,
            pl.BlockSpec((tt, tn), lambda j, i: (i, ga_block + j)),
            pl.BlockSpec((tt, tn), lambda j, i: (i, gb_block + j)),
            vec,
            pl.BlockSpec((1, tn), lambda j, i: (0, nj + j)),
        ],
        out_specs=[blk, blk, blk, blk, vec, vec],
        out_shape=[jax.ShapeDtypeStruct((t, d), BF16)] * 4 + [jax.ShapeDtypeStruct((1, d), F32)] * 2,
        args=(dm, w_o, br_a, br_b, proj, proj, b_gate, b_gate), carry=carry)


def _lru_up_bwd(d_br_a, w_lru_up, proj, h, g_block, carry=None):
    t, d = d_br_a.shape
    tt, tn = _tile(t, 1024), 512

    def body(a_ref, w_ref, g_ref, h_ref, dh_ref, dg_ref):
        d_y = _dot_nt(a_ref[...], w_ref[...])
        gel, gel_grad = _gelu_and_grad(g_ref[...].astype(F32))
        dh_ref[...] = d_y * gel
        dg_ref[...] = (d_y * h_ref[...] * gel_grad).astype(BF16)

    blk = pl.BlockSpec((tt, tn), lambda i, j: (i, j))
    return _call(
        body, name="lru_up_bwd", grid=(t // tt, d // tn),
        in_specs=[
            pl.BlockSpec((tt, d), lambda i, j: (i, 0)),
            pl.BlockSpec((tn, d), lambda i, j: (j, 0)),
            pl.BlockSpec((tt, tn), lambda i, j: (i, g_block + j)),
            blk,
        ],
        out_specs=[blk, blk],
        out_shape=[jax.ShapeDtypeStruct((t, d), F32), jax.ShapeDtypeStruct((t, d), BF16)],
        args=(d_br_a, w_lru_up, proj, h), carry=carry)


def _lru_bwd(dh, h, saved, proj, conv_w, w_a, w_x, lam, carry=None):
    t, dr = dh.shape
    cb = LRU_CB
    hd = LRU_HEAD_DIM
    per = cb // hd
    tc = _tile(t, 256)
    ncb, ntc = dr // cb, t // tc

    def body(dh_ref, h_ref, hp_ref, saved_ref, xp_ref, cw_ref, wa_ref, wx_ref,
             lam_ref, dxp_ref, dwa_ref, dba_ref, dwx_ref, dbx_ref, dlam_ref, dcw_ref, dcb_ref,
             nextd_s, anext_s, gnext_s, tmp_s, wa_s, wx_s):
        c = pl.program_id(1)
        rc = ntc - 1 - c

        @pl.when(c == 0)
        def _():
            nextd_s[...] = jnp.zeros_like(nextd_s)
            anext_s[...] = jnp.zeros_like(anext_s)
            gnext_s[...] = jnp.zeros_like(gnext_s)
            for ref in (dwa_ref, dba_ref, dwx_ref, dbx_ref, dlam_ref, dcw_ref, dcb_ref):
                ref[...] = jnp.zeros_like(ref)
            _fill_block_diag(wa_ref, wa_s)
            _fill_block_diag(wx_ref, wx_s)

        xc, r, i, a, mult = [saved_ref[:, k * cb:(k + 1) * cb] for k in range(N_LRU_SAVED)]
        wa, wx, lam = wa_s[...], wx_s[...], lam_ref[...]
        xcb = xc.astype(BF16)
        sp = _softplus_neg(lam)
        row = lax.broadcasted_iota(jnp.int32, xc.shape, 0)
        h = h_ref[...]
        hp = jnp.where(rc == 0, 0.0, hp_ref[...])
        hprev = jnp.where(row >= 1, pltpu.roll(h, 1, 0), pltpu.roll(hp, 1, 0))

        def up(v, nv, j):
            return jnp.where(row < tc - j, pltpu.roll(v, tc - j, 0), nv)

        av, bv = _scan_rows(up(a, anext_s[...], 1), dh_ref[...], reverse=True)
        gt = av * gnext_s[...] + bv
        tmp_s[...] = gt
        gnext_s[...] = tmp_s[0:1, :]
        tmp_s[...] = a
        anext_s[...] = tmp_s[0:1, :]

        da = gt * hprev
        ixc = i * xc
        d_mult = gt * ixc
        d_i = gt * mult * xc
        d_xc = gt * mult * i
        d_log_a = da * a - d_mult * (a * a) / mult
        d_pre_r = (d_log_a * ((-LRU_C) * sp)) * (r * (1.0 - r))
        d_pre_i = d_i * (i * (1.0 - i))
        d_sp = jnp.sum(d_log_a * ((-LRU_C) * r), axis=0, keepdims=True)
        dlam_ref[...] += d_sp * (-1.0 / (1.0 + jnp.exp(lam)))
        dpr = d_pre_r.astype(BF16)
        dpi = d_pre_i.astype(BF16)
        dba_ref[...] += jnp.sum(d_pre_r, axis=0, keepdims=True)
        dbx_ref[...] += jnp.sum(d_pre_i, axis=0, keepdims=True)
        pa = _dot_tn(xcb, dpr)
        px = _dot_tn(xcb, dpi)
        for k in range(per):
            dwa_ref[k] += pa[k * hd:(k + 1) * hd, k * hd:(k + 1) * hd]
            dwx_ref[k] += px[k * hd:(k + 1) * hd, k * hd:(k + 1) * hd]
        d_xc = d_xc + _dot_nt(dpr, wa) + _dot_nt(dpi, wx)

        nxt = nextd_s[...]
        xp = xp_ref[...].astype(F32)
        dxp = cw_ref[3:4, :] * d_xc
        dcw_ref[3:4, :] += jnp.sum(xp * d_xc, axis=0, keepdims=True)
        for j in (1, 2, 3):
            uj = up(d_xc, pltpu.roll(nxt, tc - j, 0), j)
            dxp = dxp + cw_ref[3 - j:4 - j, :] * uj
            dcw_ref[3 - j:4 - j, :] += jnp.sum(xp * uj, axis=0, keepdims=True)
        dcb_ref[...] += jnp.sum(d_xc, axis=0, keepdims=True)
        nextd_s[...] = d_xc
        dxp_ref[...] = dxp.astype(BF16)

    vec = pl.BlockSpec((1, cb), lambda j, c: (0, j))
    blk = pl.BlockSpec((tc, cb), lambda j, c: (ntc - 1 - c, j))
    mat = pl.BlockSpec((per, hd, hd), lambda j, c: (j, 0, 0))
    cwb = pl.BlockSpec((4, cb), lambda j, c: (0, j))
    return _call(
        body, name="lru_bwd", grid=(ncb, ntc),
        in_specs=[
            blk, blk,
            pl.BlockSpec((tc, cb), lambda j, c: (jnp.maximum(ntc - 2 - c, 0), j)),
            pl.BlockSpec((tc, N_LRU_SAVED * cb), lambda j, c: (ntc - 1 - c, j)),
            blk, cwb, mat, mat, vec,
        ],
        out_specs=[blk, mat, vec, mat, vec, vec, cwb, vec],
        out_shape=[
            jax.ShapeDtypeStruct((t, dr), BF16),
            jax.ShapeDtypeStruct(w_a.shape, F32),
            jax.ShapeDtypeStruct((1, dr), F32),
            jax.ShapeDtypeStruct(w_x.shape, F32),
            jax.ShapeDtypeStruct((1, dr), F32),
            jax.ShapeDtypeStruct((1, dr), F32),
            jax.ShapeDtypeStruct((4, dr), F32),
            jax.ShapeDtypeStruct((1, dr), F32),
        ],
        scratch_shapes=[
            pltpu.VMEM((tc, cb), F32),
            pltpu.VMEM((1, cb), F32),
            pltpu.VMEM((1, cb), F32),
            pltpu.VMEM((tc, cb), F32),
            pltpu.VMEM((cb, cb), BF16),
            pltpu.VMEM((cb, cb), BF16),
        ],
        args=(dh, h, h, saved, proj, conv_w, w_a, w_x, lam), carry=carry)


def _pool_bwd(d_br_b, w_pool_upb, p, pool_w, pool_scale):
    t, d = d_br_b.shape
    nwb, dp, _ = w_pool_upb.shape
    tc = _tile(t, 256)
    ntc = t // tc
    ng = len(POOL_WINDOWS)

    def body(db_ref, wu_ref, p_ref, w_ref, sc_ref, dx_ref, dw_ref, dsc_ref, nz, n2, n4, n8, dp_s, dy_s):
        c = pl.program_id(0)
        rc = ntc - 1 - c

        @pl.when(c == 0)
        def _():
            for s in (nz, n2, n4, n8):
                s[...] = jnp.zeros_like(s)
            dw_ref[...] = jnp.zeros_like(dw_ref)
            dsc_ref[...] = jnp.zeros_like(dsc_ref)

        wu = jnp.concatenate([wu_ref[b] for b in range(nwb)], axis=1)
        dy_s[...] = _dot_nt(db_ref[...], wu)
        for g in range(ng):
            sl = slice(g * POOL_GROUP_DIM, (g + 1) * POOL_GROUP_DIM)
            pg = p_ref[:, sl]
            dyg = dy_s[:, sl]
            wg = w_ref[g].astype(BF16)
            q = _dot_nn(pg, wg)
            dsc_ref[:, sl] += jnp.sum(dyg * q, axis=0, keepdims=True)
            dpw = (dyg * sc_ref[:, sl]).astype(BF16)
            dw_ref[g] += _dot_tn(pg, dpw)
            dp_s[:, sl] = _dot_nt(dpw, wg)

        dpv = dp_s[...]
        row = lax.broadcasted_iota(jnp.int32, dpv.shape, 0)
        col = lax.broadcasted_iota(jnp.int32, dpv.shape, 1)
        win = _pool_select(col, POOL_WINDOWS)
        cnt = jnp.minimum(rc * tc + row + 1, win).astype(F32)
        z = dpv / cnt

        def up(v, nv, j):
            return jnp.where(row < tc - j, pltpu.roll(v, tc - j, 0), pltpu.roll(nv[...], tc - j, 0))

        u2 = z + up(z, nz, 1)
        u4 = u2 + up(u2, n2, 2)
        u8 = u4 + up(u4, n4, 4)
        u16 = u8 + up(u8, n8, 8)
        nz[...] = z
        n2[...] = u2
        n4[...] = u4
        n8[...] = u8
        dx_ref[...] = (_pool_select(col, (u2, u4, u8, u16)) - dpv).astype(BF16)

    blk = pl.BlockSpec((tc, dp), lambda c: (ntc - 1 - c, 0))
    full_w = pl.BlockSpec(pool_w.shape, lambda c: (0, 0, 0))
    vec = pl.BlockSpec((1, dp), lambda c: (0, 0))
    return _call(
        body, name="pool_bwd", grid=(ntc,),
        in_specs=[pl.BlockSpec((tc, d), lambda c: (ntc - 1 - c, 0)),
                  pl.BlockSpec(w_pool_upb.shape, lambda c: (0, 0, 0)), blk, full_w, vec],
        out_specs=[blk, full_w, vec],
        out_shape=[
            jax.ShapeDtypeStruct((t, dp), BF16),
            jax.ShapeDtypeStruct(pool_w.shape, F32),
            jax.ShapeDtypeStruct((1, dp), F32),
        ],
        scratch_shapes=[pltpu.VMEM((tc, dp), F32)] * 6,
        args=(d_br_b, w_pool_upb, p, pool_w, pool_scale))[0]


def _win_bwd_norm(parts, w_int, dx2, x, g1, carry=None):
    t, d = x.shape
    tk = 512
    tt = _tile(t, 1024)
    bounds = []
    k0 = 0
    for part in parts:
        assert part.shape[1] % tk == 0
        bounds.append((k0, k0 + part.shape[1] // tk))
        k0 += part.shape[1] // tk
    nk = k0
    assert nk * tk == w_int.shape[0]
    np_ = len(parts)

    def body(*refs):
        p_refs = refs[:np_]
        w_ref, dx2_ref, x_ref, g_ref, gx_ref, dg_ref, acc = refs[np_:]
        i, kk = pl.program_id(0), pl.program_id(1)

        @pl.when(kk == 0)
        def _():
            acc[...] = jnp.zeros_like(acc)

        @pl.when((i == 0) & (kk == 0))
        def _():
            dg_ref[...] = jnp.zeros_like(dg_ref)

        for (lo, hi), p_ref in zip(bounds, p_refs):
            @pl.when((kk >= lo) & (kk < hi))
            def _(p_ref=p_ref):
                acc[...] += _dot_nn(p_ref[...], w_ref[...])

        @pl.when(kk == nk - 1)
        def _():
            def tail(rows):
                xhat, r = _rms_hat(x_ref[rows, :])
                dx, dg = _rms_bwd(acc[rows, :], xhat, r, g_ref[...])
                gx_ref[rows, :] = dx2_ref[rows, :] + dx
                dg_ref[...] += dg

            _row_chunks(tt, tail)

    def part_spec(lo, hi):
        return pl.BlockSpec((tt, tk), lambda i, kk: (i, jnp.clip(kk - lo, 0, hi - lo - 1)))

    row = pl.BlockSpec((tt, d), lambda i, kk: (i, 0))
    vec = pl.BlockSpec((1, d), lambda i, kk: (0, 0))
    return _call(
        body, name="win_bwd_norm", grid=(t // tt, nk),
        in_specs=[part_spec(lo, hi) for lo, hi in bounds]
        + [pl.BlockSpec((tk, d), lambda i, kk: (kk, 0)), row, row, vec],
        out_specs=[row, vec],
        out_shape=[jax.ShapeDtypeStruct((t, d), F32), jax.ShapeDtypeStruct((1, d), F32)],
        scratch_shapes=[pltpu.VMEM((tt, d), F32)],
        args=(*parts, w_int, dx2, x, g1), carry=carry)


def _adam_math(w, g, m, v):
    m = ADAM_B1 * m + (1.0 - ADAM_B1) * g
    v = ADAM_B2 * v + (1.0 - ADAM_B2) * (g * g)
    m_hat = m / (1.0 - ADAM_B1 ** ADAM_STEP)
    v_hat = v / (1.0 - ADAM_B2 ** ADAM_STEP)
    delta = -ADAM_LR * (m_hat / (jnp.sqrt(v_hat) + ADAM_EPS) + ADAM_WD * w)
    return delta, m, v


def _adamw_big(ws, gs, ms, vs):
    n = len(ws)
    nb = 4
    pair = [isinstance(g, tuple) for g in gs]

    def body(*refs):
        p = 0
        ins = []
        for a in range(n):
            k = 5 if pair[a] else 4
            ins.append(refs[p:p + k])
            p += k
        for a in range(n):
            g_out, d_ref, nm_ref, nv_ref = refs[p + 4 * a:p + 4 * a + 4]
            if pair[a]:
                w_ref, own_ref, recv_ref, m_ref, v_ref = ins[a]
                g = own_ref[...]
                for k in range(3):
                    g = g + recv_ref[k].astype(F32)
            else:
                w_ref, g_ref, m_ref, v_ref = ins[a]
                g = g_ref[...]
            dl, m, v = _adam_math(w_ref[...], g, m_ref[...], v_ref[...])
            g_out[...] = g
            d_ref[...] = dl
            nm_ref[...] = m
            nv_ref[...] = v

    in_specs, out_specs, out_shape, args = [], [], [], []
    for a, (w, g, m, v) in enumerate(zip(ws, gs, ms, vs)):
        rows, cols = w.shape
        blk = pl.BlockSpec((rows // nb, cols), lambda i: (i, 0))
        if pair[a]:
            in_specs += [blk, pl.BlockSpec((None, rows // nb, cols), lambda i: (0, i, 0)),
                         pl.BlockSpec((3, rows // nb, cols), lambda i: (0, i, 0)), blk, blk]
            args += [w, g[0], g[1], m, v]
        else:
            in_specs += [blk] * 4
            args += [w, g, m, v]
        out_specs += [blk] * 4
        out_shape += [jax.ShapeDtypeStruct(w.shape, F32)] * 4
    outs = _call(body, name="adamw_big", grid=(nb,), in_specs=in_specs, out_specs=out_specs,
                 out_shape=out_shape, args=args)[0]
    return [tuple(outs[4 * a:4 * a + 4]) for a in range(n)]


SMALL_ORDER = ("norm_mix_pre", "norm_mix_post", "norm_mlp_pre", "norm_mlp_post", "b_gate", "conv_w", "conv_b",
               "lru_w_a", "lru_b_a", "lru_w_x", "lru_b_x", "lru_lambda", "pool_w", "pool_scale")
VEC_ROW = dict(norm_mix_pre=0, norm_mix_post=1, norm_mlp_pre=2, norm_mlp_post=3, conv_b=6, lru_b_a=7,
               lru_b_x=8, lru_lambda=9)
ROW_B_GATE, ROW_POOL_SCALE, ROW_CONV_W, ROW_LOSS, N_VEC_ROWS = 4, 10, 11, 15, 16


def _adamw_small(vec_parts, g_pool, g_wa, g_wx, me, params):
    d = vec_parts.shape[2]
    names = SMALL_ORDER
    n = len(names)
    cw_cols = params["conv_w"][0].shape[2]

    def body(me_ref, vec_ref, vecc_ref, gp_ref, gwa_ref, gwx_ref, *refs):
        wmv = refs[:3 * n]
        loss_ref = refs[3 * n]
        outs = refs[3 * n + 1:3 * n + 1 + 4 * n]
        vs, vsc = refs[3 * n + 1 + 4 * n:]
        acc, accc = vec_ref[0], vecc_ref[0]
        for k in range(1, N_DEV):
            acc = acc + vec_ref[k]
            accc = accc + vecc_ref[k]
        vs[...] = acc
        vsc[...] = accc
        loss_ref[...] = vs[ROW_LOSS:ROW_LOSS + 1, 0:128]

        def upd(a, g, idx):
            w_ref, m_ref, v_ref = wmv[3 * a:3 * a + 3]
            g_ref, d_ref, nm_ref, nv_ref = outs[4 * a:4 * a + 4]
            dl, m, v = _adam_math(w_ref[idx], g, m_ref[idx], v_ref[idx])
            g_ref[idx] = g
            d_ref[idx] = dl
            nm_ref[idx] = m
            nv_ref[idx] = v

        for a, name in enumerate(names):
            if name in VEC_ROW:
                r = VEC_ROW[name]
                upd(a, vs[r:r + 1, :], (slice(None), slice(None)))
            elif name == "b_gate":
                for half in range(2):
                    r = ROW_B_GATE + half
                    upd(a, vs[r:r + 1, :], (slice(None), slice(half * d, (half + 1) * d)))
            elif name == "pool_scale":
                width = params[name][0].shape[1]
                upd(a, vs[ROW_POOL_SCALE:ROW_POOL_SCALE + 1, 0:width], (slice(None), slice(None)))
            elif name == "conv_w":
                upd(a, vsc[ROW_CONV_W:ROW_CONV_W + 4, :], (0,))
            elif name == "pool_w":
                upd(a, gp_ref[...], (Ellipsis,))
            elif name == "lru_w_a":
                upd(a, gwa_ref[...], (Ellipsis,))
            elif name == "lru_w_x":
                upd(a, gwx_ref[...], (Ellipsis,))
            else:
                raise ValueError(name)

    def whole(shape):
        nd = len(shape)
        return pl.BlockSpec(tuple(shape), lambda i, me_ref: (0,) * nd)

    in_specs = [
        whole(vec_parts.shape),
        pl.BlockSpec((N_DEV, N_VEC_ROWS, cw_cols), lambda i, me_ref: (0, 0, me_ref[0])),
        whole(g_pool.shape), whole(g_wa.shape), whole(g_wx.shape),
    ]
    args = [vec_parts, vec_parts, g_pool, g_wa, g_wx]
    out_specs = [whole((1, 128))]
    out_shape = [jax.ShapeDtypeStruct((1, 128), F32)]
    for name in names:
        for arr in params[name]:
            in_specs.append(whole(arr.shape))
            args.append(arr)
        shp = params[name][0].shape
        out_specs += [whole(shp)] * 4
        out_shape += [jax.ShapeDtypeStruct(shp, F32)] * 4
    grid_spec = pltpu.PrefetchScalarGridSpec(
        num_scalar_prefetch=1, grid=(1,), in_specs=in_specs, out_specs=out_specs,
        scratch_shapes=[pltpu.VMEM((N_VEC_ROWS, d), F32), pltpu.VMEM((N_VEC_ROWS, cw_cols), F32)])
    outs = pl.pallas_call(
        body, name="adamw_small", grid_spec=grid_spec, out_shape=out_shape,
        compiler_params=pltpu.CompilerParams(
            dimension_semantics=("arbitrary",), vmem_limit_bytes=V7X_VMEM_LIMIT_BYTES),
    )(me, *_in_hbm(args))
    return outs[0], {name: tuple(outs[1 + 4 * a:5 + 4 * a]) for a, name in enumerate(names)}


def _rs_sum(fulls, recvs, shard_ids, slot_ids, name):
    n = len(fulls)

    def body(sh_ref, sl_ref, *refs):
        s = pl.program_id(0)
        for a in range(n):
            full_ref, recv_ref = refs[2 * a], refs[2 * a + 1]
            own_ref, send_ref = refs[2 * n + 2 * a], refs[2 * n + 2 * a + 1]
            v = full_ref[...] + recv_ref[...].astype(F32)

            @pl.when(s == 0)
            def _(own_ref=own_ref, v=v):
                own_ref[...] = v

            @pl.when(s > 0)
            def _(send_ref=send_ref, v=v):
                send_ref[...] = v.astype(send_ref.dtype)

    in_specs, out_specs, out_shape, args = [], [], [], []
    for full, recv in zip(fulls, recvs):
        r, rest = recv.shape[1], tuple(recv.shape[2:])
        zeros = (0,) * len(rest)
        in_specs += [
            pl.BlockSpec((r,) + rest, lambda s, sh, sl, zeros=zeros: (sh[s],) + zeros),
            pl.BlockSpec((None, r) + rest, lambda s, sh, sl, zeros=zeros: (sl[s], 0) + zeros),
        ]
        out_specs += [
            pl.BlockSpec((None, r) + rest, lambda s, sh, sl, zeros=zeros: (0, 0) + zeros),
            pl.BlockSpec((None, r) + rest, lambda s, sh, sl, zeros=zeros: (jnp.maximum(s - 1, 0), 0) + zeros),
        ]
        out_shape += [jax.ShapeDtypeStruct((1, r) + rest, F32), jax.ShapeDtypeStruct((3, r) + rest, recv.dtype)]
        args += [full, recv]
    grid_spec = pltpu.PrefetchScalarGridSpec(
        num_scalar_prefetch=2, grid=(4,), in_specs=in_specs, out_specs=out_specs)
    outs = pl.pallas_call(
        body,
        name=name,
        grid_spec=grid_spec,
        out_shape=out_shape,
        compiler_params=pltpu.CompilerParams(
            dimension_semantics=("arbitrary",), vmem_limit_bytes=V7X_VMEM_LIMIT_BYTES),
    )(shard_ids, slot_ids, *_in_hbm(args))
    return [(outs[2 * a], outs[2 * a + 1]) for a in range(n)]


def _finals(pairs, name, carry=None):
    nb = 4
    n = len(pairs)

    def body(*refs):
        for a in range(n):
            own_ref, recv_ref = refs[2 * a], refs[2 * a + 1]
            acc = own_ref[...]
            for k in range(3):
                acc = acc + recv_ref[k].astype(F32)
            refs[2 * n + a][...] = acc

    in_specs, out_specs, out_shape, args = [], [], [], []
    for own, recv in pairs:
        _, rows, cols = own.shape
        in_specs += [pl.BlockSpec((None, rows // nb, cols), lambda i: (0, i, 0)),
                     pl.BlockSpec((3, rows // nb, cols), lambda i: (0, i, 0))]
        args += [own, recv]
        out_specs.append(pl.BlockSpec((rows // nb, cols), lambda i: (i, 0)))
        out_shape.append(jax.ShapeDtypeStruct((rows, cols), F32))
    return _call(body, name=name, grid=(nb,), in_specs=in_specs, out_specs=out_specs,
                 out_shape=out_shape, args=args, carry=carry)


def _rs_sums(fulls_f32, recv1, tag):
    x, y, c = _place()
    qs = jnp.stack([2 * x + y, 2 * (1 - x) + y, 2 * x + (1 - y), 2 * (1 - x) + (1 - y)]).astype(jnp.int32)
    shard_ids = 2 * qs + c
    return _rs_sum(fulls_f32, recv1, shard_ids, qs, "rs_sum_" + tag)


def _rs_level1(fulls_f32, fulls_send, tag):
    recv1 = _run_plan(_rs_sibling_plan(fulls_send), "rs_sibling_" + tag)
    return _rs_sums(fulls_f32, recv1, tag)


def _rows(g):
    return g.reshape(g.shape[0] * g.shape[1], g.shape[2])


def kernel(x, norm_mix_pre, norm_mix_post, norm_mlp_pre, norm_mlp_post, w_in, b_gate, conv_w, conv_b, lru_w_a, lru_b_a, lru_w_x, lru_b_x, lru_lambda, pool_w, pool_scale, w_lru_up, w_pool_up, w_o, w_ff1, w_ff2, loss_target, m_norm_mix_pre, m_norm_mix_post, m_norm_mlp_pre, m_norm_mlp_post, m_w_in, m_b_gate, m_conv_w, m_conv_b, m_lru_w_a, m_lru_b_a, m_lru_w_x, m_lru_b_x, m_lru_lambda, m_pool_w, m_pool_scale, m_w_lru_up, m_w_pool_up, m_w_o, m_w_ff1, m_w_ff2, v_norm_mix_pre, v_norm_mix_post, v_norm_mlp_pre, v_norm_mlp_post, v_w_in, v_b_gate, v_conv_w, v_conv_b, v_lru_w_a, v_lru_b_a, v_lru_w_x, v_lru_b_x, v_lru_lambda, v_pool_w, v_pool_scale, v_w_lru_up, v_w_pool_up, v_w_o, v_w_ff1, v_w_ff2):
    t, d = x.shape[1], x.shape[2]
    d_rnn = conv_b.shape[1]
    d_pool = pool_scale.shape[1]
    per = LRU_CB // LRU_HEAD_DIM
    xi, yi, ci = _place()
    me = 4 * xi + 2 * yi + ci

    x2d = x[0]
    tgt = loss_target[0]

    s_in = w_in[0].T.astype(BF16)
    s_lu = w_lru_up[0].astype(BF16)
    s_pu = w_pool_up[0].astype(BF16)
    s_o = w_o[0].astype(BF16)
    s_f1 = w_ff1[0].astype(BF16)
    s_f2 = w_ff2[0].astype(BF16)
    s_cw = jnp.pad(conv_w[0], ((0, 4), (0, 0)))

    g_in, g_cw = _run_plan(_ag_plan([s_in, s_cw]), "ag_w_in")
    w_int = _rows(g_in)
    conv_w_full = jnp.transpose(g_cw[:, :4, :], (1, 0, 2)).reshape(4, d_rnn)

    wa_bd, wx_bd = lru_w_a[0], lru_w_x[0]
    pw = pool_w[0]
    pw_bf = pw.astype(BF16)

    pool_block = (2 * d_rnn) // d_pool
    ga_block = (2 * d_rnn + d_pool) // 512
    gb_block = ga_block + d // 512
    g_block = d_rnn // 512

    r_f1, r_f2 = s_f1.shape[0], s_f2.shape[0]
    f1_cut = r_f1 // 4
    f2_cut = (3 * r_f2) // 8
    plan = _join([_ag_plan([s_lu, s_pu, s_o]), _ag_plan([s_f1], pieces=[(0, f1_cut)])])
    (proj, h1), got = _norm_proj(x2d, norm_mix_pre, w_int, carry=plan)
    (g_lu, g_pu, g_o), (g_f1,) = plan.split(got)
    w_lu, w_og = _rows(g_lu), _rows(g_o)
    (y_lru, h, lru_saved), (g_f1,) = _lru_fwd(
        proj, conv_w_full, conv_b, wa_bd, lru_b_a, wx_bd, lru_b_x, lru_lambda,
        carry=_ag_plan([s_f1], pieces=[(f1_cut, r_f1 - f1_cut)], bufs=[g_f1]))
    y_pool, p = _pool_fwd(proj, pw_bf, pool_scale, pool_block)
    (br_a, br_b, mix), (g_f2,) = _branch_mix(
        y_lru, y_pool, w_lu, g_pu, proj, b_gate, ga_block, gb_block,
        carry=_ag_plan([s_f2], pieces=[(0, f2_cut)]))
    (m, x2, h3), _ = _wo_norm(mix, w_og, x2d, norm_mix_post, norm_mlp_pre)
    (rf,), (g_f2,) = _ff1(
        h3, g_f1, carry=_ag_plan([s_f2], pieces=[(f2_cut, r_f2 - f2_cut)], bufs=[g_f2]))
    w_f2 = _rows(g_f2)
    dy, df, dg4, loss_part = _ff2_loss(rf, w_f2, x2, norm_mlp_post, tgt)

    (gw_ff2_32, gw_ff2_16), _ = _wgrad(rf, df, "wgrad_ff2", square_a=True)
    (d_f1,), r1_ff2 = _ff2_bwd(df, w_f2, rf, carry=_rs_sibling_plan([gw_ff2_16]))
    ((own_ff2, send_ff2),) = _rs_sums([gw_ff2_32], r1_ff2, "ff2")
    cut2 = (5 * send_ff2.shape[1]) // 16
    (gw_ff1_32, gw_ff1_16), (r2_ff2,) = _wgrad_cols(
        h3, d_f1, s_f1.shape[1], s_f1.shape[1], "wgrad_ff1",
        carry=_rs_chips_plan([send_ff2], pieces=[(0, cut2)]))
    plan = _join([_rs_chips_plan([send_ff2], pieces=[(cut2, send_ff2.shape[1] - cut2)], bufs=[r2_ff2]),
                  _rs_sibling_plan([gw_ff1_16])])
    (dx2, dm, dg3, dg2), got = _ff1_bwd_norms(d_f1, g_f1, dy, x2, norm_mlp_pre, m, norm_mix_post, carry=plan)
    (r2_ff2,), r1_ff1 = plan.split(got)
    ((own_ff1, send_ff1),) = _rs_sums([gw_ff1_32], r1_ff1, "ff1")
    own_ff1, send_ff1 = own_ff1.reshape((1,) + s_f1.shape), send_ff1.reshape((3,) + s_f1.shape)
    cut = send_ff1.shape[1] // 4
    (gw_o_32, gw_o_16), _ = _wgrad(mix, dm, "wgrad_o")
    (d_br_a, d_br_b, p_ga, p_gb, dbg_a, dbg_b), (r2_ff1,) = _wo_bwd_mix(
        dm, w_og, br_a, br_b, proj, b_gate, ga_block, gb_block,
        carry=_rs_chips_plan([send_ff1], pieces=[(0, cut)]))
    (gw_lu_32, gw_lu_16), _ = _wgrad(y_lru, d_br_a, "wgrad_lru_up")
    (gw_pu_32, gw_pu_16), _ = _wgrad_cols(y_pool, d_br_b, s_pu.shape[1], d, "wgrad_pool_up")
    (dh, p_g), r1_mid = _lru_up_bwd(
        d_br_a, w_lu, proj, h, g_block,
        carry=_rs_sibling_plan([gw_o_16, gw_lu_16, gw_pu_16]))
    mid = _rs_sums([gw_o_32, gw_lu_32, gw_pu_32], r1_mid, "mid")
    (p_x, dwa, db_a, dwx, db_x, dlam, dconv_w, dconv_b), (r2_ff1,) = _lru_bwd(
        dh, h, lru_saved, proj, conv_w_full, wa_bd, wx_bd, lru_lambda,
        carry=_rs_chips_plan([send_ff1], pieces=[(cut, send_ff1.shape[1] - cut)], bufs=[r2_ff1]))
    p_p, dpool_w, dpool_scale = _pool_bwd(d_br_b, g_pu, p, pw, pool_scale)
    parts = [p_x, p_g, p_p, p_ga, p_gb]
    gw_in, r2_mid = _wgrad_parts(parts, h1, "wgrad_in", carry=_rs_chips_plan([s for _, s in mid]))
    tail = _rs_level1([gw_in[0], dpool_w.reshape(N_DEV, -1, POOL_GROUP_DIM), dwa, dwx],
                      [gw_in[1], dpool_w.reshape(N_DEV, -1, POOL_GROUP_DIM), dwa, dwx], "in")
    (grad_x, dg1), r2_tail = _win_bwd_norm(parts, w_int, dx2, x2d, norm_mix_pre,
                                           carry=_rs_chips_plan([s for _, s in tail]))

    def flat2(a):
        return a.reshape(a.shape[0], -1, a.shape[-1])

    fin_small, _ = _finals([
        (flat2(tail[1][0]), flat2(r2_tail[1])), (flat2(tail[2][0]), flat2(r2_tail[2])),
        (flat2(tail[3][0]), flat2(r2_tail[3])),
    ], "rs_finals_small")

    def pad_row(a):
        return jnp.pad(a, ((0, 0), (0, d - a.shape[1])))

    vecs = jnp.concatenate([dg1, dg2, dg3, dg4, dbg_a, dbg_b, dconv_b, db_a, db_x, dlam,
                            pad_row(dpool_scale), dconv_w, pad_row(loss_part)], axis=0)
    assert vecs.shape[0] == N_VEC_ROWS
    vec_parts, g_pool, g_wa, g_wx = _run_plan(_ag_plan([vecs] + fin_small), "ag_tail")

    big_names = ["w_in", "w_lru_up", "w_pool_up", "w_o", "w_ff1", "w_ff2"]
    big_w = [w_in[0].T, w_lru_up[0], w_pool_up[0], w_o[0], w_ff1[0], w_ff2[0]]
    big_g = [(tail[0][0], r2_tail[0]), (mid[1][0], r2_mid[1]),
             (mid[2][0].reshape((1,) + s_pu.shape), r2_mid[2].reshape((3,) + s_pu.shape)),
             (mid[0][0], r2_mid[0]), (own_ff1, r2_ff1), (own_ff2, r2_ff2)]
    big_m = [m_w_in[0].T, m_w_lru_up[0], m_w_pool_up[0], m_w_o[0], m_w_ff1[0], m_w_ff2[0]]
    big_v = [v_w_in[0].T, v_w_lru_up[0], v_w_pool_up[0], v_w_o[0], v_w_ff1[0], v_w_ff2[0]]
    big_out = _adamw_big(big_w, big_g, big_m, big_v)
    big_out[0] = tuple(o.T for o in big_out[0])

    small = dict(
        norm_mix_pre=(norm_mix_pre, m_norm_mix_pre, v_norm_mix_pre),
        norm_mix_post=(norm_mix_post, m_norm_mix_post, v_norm_mix_post),
        norm_mlp_pre=(norm_mlp_pre, m_norm_mlp_pre, v_norm_mlp_pre),
        norm_mlp_post=(norm_mlp_post, m_norm_mlp_post, v_norm_mlp_post),
        b_gate=(b_gate, m_b_gate, v_b_gate), conv_w=(conv_w, m_conv_w, v_conv_w),
        conv_b=(conv_b, m_conv_b, v_conv_b), lru_w_a=(lru_w_a, m_lru_w_a, v_lru_w_a),
        lru_b_a=(lru_b_a, m_lru_b_a, v_lru_b_a), lru_w_x=(lru_w_x, m_lru_w_x, v_lru_w_x),
        lru_b_x=(lru_b_x, m_lru_b_x, v_lru_b_x), lru_lambda=(lru_lambda, m_lru_lambda, v_lru_lambda),
        pool_w=(pool_w, m_pool_w, v_pool_w), pool_scale=(pool_scale, m_pool_scale, v_pool_scale))
    loss_row, small_out = _adamw_small(
        vec_parts, g_pool.reshape(pool_w.shape), g_wa.reshape(lru_w_a.shape), g_wx.reshape(lru_w_x.shape),
        jnp.reshape(me, (1,)).astype(jnp.int32), small)
    grads = {n: o[0] for n, o in small_out.items()}
    delta = {n: o[1] for n, o in small_out.items()}
    new_m = {n: o[2] for n, o in small_out.items()}
    new_v = {n: o[3] for n, o in small_out.items()}

    for name, (g, dl, nm, nv) in zip(big_names, big_out):
        grads[name], delta[name], new_m[name], new_v[name] = g[None], dl[None], nm[None], nv[None]

    loss = loss_row[0, 0]
    order = ["norm_mix_pre", "norm_mix_post", "norm_mlp_pre", "norm_mlp_post", "w_in", "b_gate", "conv_w",
             "conv_b", "lru_w_a", "lru_b_a", "lru_w_x", "lru_b_x", "lru_lambda", "pool_w", "pool_scale",
             "w_lru_up", "w_pool_up", "w_o", "w_ff1", "w_ff2"]
    return (loss, grad_x[None], *[grads[n] for n in order], *[delta[n] for n in order],
            *[new_m[n] for n in order], *[new_v[n] for n in order])
```

```python
import functools
import math
import operator
import types

import jax
import jax.numpy as jnp
from jax import lax
from jax.experimental import pallas as pl
from jax.experimental.pallas import tpu as pltpu

F32 = jnp.float32
BF16 = jnp.bfloat16
NORM_EPS = 1e-6
LRU_C = 8.0
N_LRU_HEADS = 16
LRU_HEAD_DIM = 64
POOL_WINDOWS = (2, 4, 8, 16)
POOL_GROUP_DIM = 128
ADAM_LR = 0.001
ADAM_B1 = 0.9
ADAM_B2 = 0.999
ADAM_EPS = 1e-08
ADAM_WD = 0.01
ADAM_STEP = 10
N_DEV = 8
V7X_VMEM_LIMIT_BYTES = 56 * 1024 * 1024
LRU_CB = 256
MESH = pl.DeviceIdType.MESH
ANY = pl.BlockSpec(memory_space=pl.ANY)


def _tile(n, pref):
    t = min(n, pref)
    assert n % t == 0, (n, pref)
    return t


def _dot_nn(a, b):
    return lax.dot_general(a, b, (((1,), (0,)), ((), ())), preferred_element_type=F32)


def _dot_nt(a, b):
    return lax.dot_general(a, b, (((1,), (1,)), ((), ())), preferred_element_type=F32)


def _dot_tn(a, b):
    return lax.dot_general(a, b, (((0,), (0,)), ((), ())), preferred_element_type=F32)


def _row_chunks(n_rows, fn, chunk=256):
    chunk = min(chunk, n_rows)
    assert n_rows % chunk == 0

    def step(r, carry):
        fn(pl.ds(pl.multiple_of(r * chunk, chunk), chunk))
        return carry

    lax.fori_loop(0, n_rows // chunk, step, 0)


def _sig(x):
    return 1.0 / (1.0 + jnp.exp(-x))


def _rms_hat(x):
    r = lax.rsqrt(jnp.mean(x * x, axis=-1, keepdims=True) + NORM_EPS)
    return x * r, r


def _rms_bwd(dn, xhat, r, g):
    q = dn * g
    dx = r * (q - xhat * jnp.mean(q * xhat, axis=-1, keepdims=True))
    dg = jnp.sum(dn * xhat, axis=0, keepdims=True)
    return dx, dg


_GELU_K = math.sqrt(2.0 / math.pi)
_GELU_C = 0.044715


def _gelu_and_grad(g):
    t = jnp.tanh(_GELU_K * (g + _GELU_C * g * g * g))
    val = 0.5 * g * (1.0 + t)
    grad = 0.5 * (1.0 + t) + 0.5 * g * (1.0 - t * t) * (_GELU_K * (1.0 + 3.0 * _GELU_C * g * g))
    return val, grad


def _softplus_neg(lam):
    z = -lam
    e = jnp.exp(-jnp.abs(z))
    u = 1.0 + e
    d = u - 1.0
    l1p = jnp.where(d == 0.0, e, jnp.log(u) * (e / jnp.where(d == 0.0, 1.0, d)))
    return jnp.maximum(z, 0.0) + l1p


def _lru_gates(xc, wa, ba, wx, bx, lam):
    xcb = xc.astype(BF16)
    r = _sig(_dot_nn(xcb, wa) + ba)
    i = _sig(_dot_nn(xcb, wx) + bx)
    sp = _softplus_neg(lam)
    log_a = (-LRU_C) * r * sp
    a = jnp.exp(log_a)
    mult = jnp.sqrt(-jnp.tanh(log_a) * (1.0 + a * a))
    return xcb, r, i, sp, log_a, a, mult


def _place():
    return lax.axis_index("x"), lax.axis_index("y"), lax.axis_index("c")


def _ag_plan(shards, pieces=None, bufs=None):
    na = len(shards)
    n_kinds = 7

    def parts(ins, outs, sems):
        send_sems, recv_sems, local_sems = sems
        x, y, c = _place()
        me, sibling = (x, y, c), (x, y, 1 - c)
        x_nb, y_nb, diag = (1 - x, y), (x, 1 - y), (1 - x, 1 - y)
        relay_src = (c * (1 - x) + (1 - c) * x, c * y + (1 - c) * (1 - y))
        relay_dst = (c * x + (1 - c) * (1 - x), c * (1 - y) + (1 - c) * y)

        def own(a):
            return ins[a] if pieces is None else ins[a].at[pl.ds(*pieces[a])]

        def slot(a, px, py, pc):
            idx = 4 * px + 2 * py + pc
            return outs[a].at[idx] if pieces is None else outs[a].at[idx, pl.ds(*pieces[a])]

        def copy(a, k, block, to, src=None):
            return pltpu.make_async_remote_copy(
                src_ref=slot(a, *block) if src is None else src,
                dst_ref=slot(a, *block),
                send_sem=send_sems.at[a * n_kinds + k],
                recv_sem=recv_sems.at[a * n_kinds + k],
                device_id=to,
                device_id_type=MESH,
            )

        mine = [pltpu.make_async_copy(own(a), slot(a, *me), local_sems.at[a]) for a in range(na)]
        first, second, third = [], [], []
        for a in range(na):
            first += [copy(a, 0, me, sibling, src=own(a)), copy(a, 1, me, (*x_nb, c), src=own(a)),
                      copy(a, 2, me, (*y_nb, c), src=own(a))]
            second += [copy(a, 3, (*relay_src, c), (*relay_dst, c)), copy(a, 4, (*x_nb, c), sibling),
                       copy(a, 5, (*y_nb, c), sibling)]
            third.append(copy(a, 6, (*diag, c), sibling))
        return sibling, c, x_nb, y_nb, diag, copy, mine, first, second, third

    def start(ins, outs, sems):
        _, _, _, _, _, _, mine, first, _, _ = parts(ins, outs, sems)
        for cp in mine + first:
            cp.start()

    def middle(ins, outs, sems):
        _, c, x_nb, y_nb, _, copy, _, _, second, _ = parts(ins, outs, sems)
        for a in range(na):
            copy(a, 1, (*x_nb, c), (*x_nb, c)).wait_recv()
            copy(a, 2, (*y_nb, c), (*y_nb, c)).wait_recv()
        for cp in second:
            cp.start()

    def finish(ins, outs, sems):
        sibling, c, x_nb, y_nb, diag, copy, mine, first, second, third = parts(ins, outs, sems)
        for a in range(na):
            copy(a, 3, (*diag, c), (*diag, c)).wait_recv()
            third[a].start()
        for a in range(na):
            copy(a, 0, sibling, sibling).wait_recv()
            copy(a, 4, (*x_nb, 1 - c), sibling).wait_recv()
            copy(a, 5, (*y_nb, 1 - c), sibling).wait_recv()
            copy(a, 6, (*diag, 1 - c), sibling).wait_recv()
        for cp in first + second + third:
            cp.wait_send()
        for cp in mine:
            cp.wait()

    return types.SimpleNamespace(
        ins=list(shards) + list(bufs or []),
        out_shapes=[jax.ShapeDtypeStruct((N_DEV,) + s.shape, s.dtype) for s in shards],
        sems=[pltpu.SemaphoreType.DMA((n_kinds * na,)), pltpu.SemaphoreType.DMA((n_kinds * na,)),
              pltpu.SemaphoreType.DMA((na,))],
        aliases=[(na + a, a) for a in range(na)] if bufs else [],
        peers=frozenset({"sibling", "neighbours"}), start=start, middle=middle, finish=finish)


def _rs_sibling_plan(fulls):
    na = len(fulls)
    rs = [f.shape[0] // N_DEV for f in fulls]

    def copies(ins, outs, sems):
        send_sems, recv_sems = sems
        x, y, c = _place()
        out = []
        for a in range(na):
            for q in range(4):
                shard = 2 * q + (1 - c)
                out.append(pltpu.make_async_remote_copy(
                    src_ref=ins[a].at[pl.ds(shard * rs[a], rs[a])],
                    dst_ref=outs[a].at[q],
                    send_sem=send_sems.at[a * 4 + q],
                    recv_sem=recv_sems.at[a * 4 + q],
                    device_id=(x, y, 1 - c),
                    device_id_type=MESH,
                ))
        return out

    def start(ins, outs, sems):
        for cp in copies(ins, outs, sems):
            cp.start()

    def finish(ins, outs, sems):
        for cp in copies(ins, outs, sems):
            cp.wait()

    return types.SimpleNamespace(
        ins=list(fulls),
        out_shapes=[jax.ShapeDtypeStruct((4, r) + f.shape[1:], f.dtype) for r, f in zip(rs, fulls)],
        sems=[pltpu.SemaphoreType.DMA((4 * na,)), pltpu.SemaphoreType.DMA((4 * na,))],
        peers=frozenset({"sibling"}), start=start, finish=finish)


def _rs_chips_plan(sends, pieces=None, bufs=None):
    na = len(sends)

    def copies(ins, outs, sems):
        send_sems, recv_sems = sems
        x, y, c = _place()
        chips = [(1 - x, y), (x, 1 - y), (1 - x, 1 - y)]
        out = []
        for a in range(na):
            for k, chip in enumerate(chips):
                rows = (k,) if pieces is None else (k, pl.ds(*pieces[a]))
                out.append(pltpu.make_async_remote_copy(
                    src_ref=ins[a].at[rows],
                    dst_ref=outs[a].at[rows],
                    send_sem=send_sems.at[a * 3 + k],
                    recv_sem=recv_sems.at[a * 3 + k],
                    device_id=(*chip, c),
                    device_id_type=MESH,
                ))
        return out

    def start(ins, outs, sems):
        for cp in copies(ins, outs, sems):
            cp.start()

    def finish(ins, outs, sems):
        for cp in copies(ins, outs, sems):
            cp.wait()

    return types.SimpleNamespace(
        ins=list(sends) + list(bufs or []),
        out_shapes=[jax.ShapeDtypeStruct(s.shape, s.dtype) for s in sends],
        sems=[pltpu.SemaphoreType.DMA((3 * na,)), pltpu.SemaphoreType.DMA((3 * na,))],
        aliases=[(na + a, a) for a in range(na)] if bufs else [],
        peers=frozenset({"chips"}), start=start, finish=finish)


def _join(plans):
    ins, outs, sems, aliases, offs = [], [], [], [], []
    for p in plans:
        offs.append((len(ins), len(outs), len(sems)))
        aliases += [(len(ins) + ci, len(outs) + co) for ci, co in getattr(p, "aliases", [])]
        ins += p.ins
        outs += p.out_shapes
        sems += p.sems

    def cut(p, off, i, o, s):
        return (i[off[0]:off[0] + len(p.ins)], o[off[1]:off[1] + len(p.out_shapes)],
                s[off[2]:off[2] + len(p.sems)])

    def start(i, o, s):
        for p, off in zip(plans, offs):
            p.start(*cut(p, off, i, o, s))

    def middle(i, o, s):
        for p, off in zip(plans, offs):
            if getattr(p, "middle", None) is not None:
                p.middle(*cut(p, off, i, o, s))

    def finish(i, o, s):
        for p, off in zip(plans, offs):
            p.finish(*cut(p, off, i, o, s))

    def split(results):
        return [list(results[off[1]:off[1] + len(p.out_shapes)]) for p, off in zip(plans, offs)]

    return types.SimpleNamespace(ins=ins, out_shapes=outs, sems=sems, aliases=aliases,
                                 peers=frozenset().union(*[p.peers for p in plans]),
                                 start=start, middle=middle, finish=finish, split=split)


COLLECTIVE_ID = {frozenset({"sibling"}): 0, frozenset({"chips"}): 1, frozenset({"sibling", "chips"}): 2,
                 frozenset({"sibling", "neighbours"}): 3}


def _handshake(peers):
    x, y, c = _place()
    devs = []
    if "sibling" in peers:
        devs.append((x, y, 1 - c))
    if "neighbours" in peers:
        devs += [(1 - x, y, c), (x, 1 - y, c)]
    if "chips" in peers:
        assert "neighbours" not in peers
        devs += [(1 - x, y, c), (x, 1 - y, c), (1 - x, 1 - y, c)]
    barrier = pltpu.get_barrier_semaphore()
    for dev in devs:
        pl.semaphore_signal(barrier, inc=1, device_id=dev, device_id_type=MESH)
    pl.semaphore_wait(barrier, len(devs))


def _in_hbm(args):
    return [pltpu.with_memory_space_constraint(a, pltpu.HBM) for a in args]


def _run_plan(plan, name):
    n_in, n_out = len(plan.ins), len(plan.out_shapes)

    def body(*refs):
        ins, outs, sems = refs[:n_in], refs[n_in:n_in + n_out], refs[n_in + n_out:]
        _handshake(plan.peers)
        plan.start(ins, outs, sems)
        if getattr(plan, "middle", None) is not None:
            plan.middle(ins, outs, sems)
        plan.finish(ins, outs, sems)

    return pl.pallas_call(
        body,
        name=name,
        in_specs=[ANY] * n_in,
        out_specs=[ANY] * n_out,
        out_shape=plan.out_shapes,
        scratch_shapes=plan.sems,
        input_output_aliases=dict(getattr(plan, "aliases", [])),
        compiler_params=pltpu.CompilerParams(collective_id=COLLECTIVE_ID[plan.peers]),
    )(*_in_hbm(plan.ins))


def _call(body, *, name, grid, in_specs, out_specs, out_shape, args, scratch_shapes=(), aliases=None,
          carry=None):
    n_in, n_out, n_scr = len(in_specs), len(out_shape), len(scratch_shapes)
    params = pltpu.CompilerParams(
        dimension_semantics=("arbitrary",) * len(grid), vmem_limit_bytes=V7X_VMEM_LIMIT_BYTES)
    if carry is None:
        outs = pl.pallas_call(
            body, name=name, grid=grid, in_specs=list(in_specs), out_specs=list(out_specs),
            out_shape=list(out_shape), scratch_shapes=list(scratch_shapes),
            input_output_aliases=aliases or {}, compiler_params=params)(*_in_hbm(args))
        return list(outs), []
    c_in, c_out = len(carry.ins), len(carry.out_shapes)

    def full(*refs):
        p = 0
        ins = refs[p:p + n_in]
        p += n_in
        cins = refs[p:p + c_in]
        p += c_in
        outs = refs[p:p + n_out]
        p += n_out
        couts = refs[p:p + c_out]
        p += c_out
        scr = refs[p:p + n_scr]
        csems = refs[p + n_scr:]
        ids = [pl.program_id(a) for a in range(len(grid))]
        first = functools.reduce(operator.and_, [i == 0 for i in ids])
        last = functools.reduce(operator.and_, [i == g - 1 for i, g in zip(ids, grid)])

        @pl.when(first)
        def _():
            _handshake(carry.peers)
            carry.start(cins, couts, csems)

        if getattr(carry, "middle", None) is not None:
            n_steps = math.prod(grid)
            flat = functools.reduce(lambda acc, ig: acc * ig[1] + ig[0], zip(ids, grid), 0)

            @pl.when(flat == (2 * n_steps) // 3)
            def _():
                carry.middle(cins, couts, csems)

        body(*ins, *outs, *scr)

        @pl.when(last)
        def _():
            carry.finish(cins, couts, csems)

    all_aliases = dict(aliases or {})
    all_aliases.update({n_in + ci: n_out + co for ci, co in getattr(carry, "aliases", [])})
    params = pltpu.CompilerParams(
        dimension_semantics=("arbitrary",) * len(grid), vmem_limit_bytes=V7X_VMEM_LIMIT_BYTES,
        collective_id=COLLECTIVE_ID[carry.peers])
    outs = pl.pallas_call(
        full, name=name, grid=grid,
        in_specs=list(in_specs) + [ANY] * c_in,
        out_specs=list(out_specs) + [ANY] * c_out,
        out_shape=list(out_shape) + list(carry.out_shapes),
        scratch_shapes=list(scratch_shapes) + list(carry.sems),
        input_output_aliases=all_aliases, compiler_params=params)(*_in_hbm(args), *_in_hbm(carry.ins))
    return list(outs[:n_out]), list(outs[n_out:])


def _norm_proj(x, g1, w_int, carry=None):
    t, d = x.shape
    n = w_int.shape[0]
    tt, tn = _tile(t, 2048), _tile(n, 512)

    def body(x_ref, g_ref, w_ref, proj_ref, h1_ref, h1_s):
        @pl.when(pl.program_id(1) == 0)
        def _():
            def norm_rows(rows):
                xhat, _ = _rms_hat(x_ref[rows, :])
                h = (xhat * g_ref[...]).astype(BF16)
                h1_s[rows, :] = h
                h1_ref[rows, :] = h

            _row_chunks(tt, norm_rows)

        proj_ref[...] = _dot_nt(h1_s[...], w_ref[...]).astype(BF16)

    return _call(
        body, name="norm_proj", grid=(t // tt, n // tn),
        in_specs=[
            pl.BlockSpec((tt, d), lambda i, j: (i, 0)),
            pl.BlockSpec((1, d), lambda i, j: (0, 0)),
            pl.BlockSpec((tn, d), lambda i, j: (j, 0)),
        ],
        out_specs=[
            pl.BlockSpec((tt, tn), lambda i, j: (i, j)),
            pl.BlockSpec((tt, d), lambda i, j: (i, 0)),
        ],
        out_shape=[jax.ShapeDtypeStruct((t, n), BF16), jax.ShapeDtypeStruct((t, d), BF16)],
        scratch_shapes=[pltpu.VMEM((tt, d), BF16)],
        args=(x, g1, w_int), carry=carry)


def _scan_rows(av, bv, reverse):
    tc = av.shape[0]
    row = lax.broadcasted_iota(jnp.int32, av.shape, 0)
    s = 1
    while s < tc:
        if s < 8:
            keep = (row < tc - s) if reverse else (row >= s)
            shift = (tc - s) if reverse else s
            a_sh = jnp.where(keep, pltpu.roll(av, shift, 0), 1.0)
            b_sh = jnp.where(keep, pltpu.roll(bv, shift, 0), 0.0)
            bv = av * b_sh + bv
            av = av * a_sh
        elif reverse:
            bv = jnp.concatenate([av[:tc - s] * bv[s:] + bv[:tc - s], bv[tc - s:]], axis=0)
            av = jnp.concatenate([av[:tc - s] * av[s:], av[tc - s:]], axis=0)
        else:
            bv = jnp.concatenate([bv[:s], av[s:] * bv[:tc - s] + bv[s:]], axis=0)
            av = jnp.concatenate([av[:s], av[s:] * av[:tc - s]], axis=0)
        s *= 2
    return av, bv


N_LRU_SAVED = 5


def _fill_block_diag(w_ref, bd_ref):
    bd_ref[...] = jnp.zeros_like(bd_ref)
    hd = LRU_HEAD_DIM
    for k in range(w_ref.shape[0]):
        bd_ref[k * hd:(k + 1) * hd, k * hd:(k + 1) * hd] = w_ref[k].astype(BF16)


def _lru_fwd(proj, conv_w, conv_b, w_a, b_a, w_x, b_x, lam, carry=None):
    t = proj.shape[0]
    dr = conv_b.shape[1]
    cb = LRU_CB
    tc = _tile(t, 256)
    ncb, ntc = dr // cb, t // tc

    def body(xp_ref, g_ref, cw_ref, cb_ref, wa_ref, ba_ref, wx_ref, bx_ref, lam_ref,
             y_ref, h_ref, saved_ref, prevx_s, hlast_s, wa_s, wx_s):
        c = pl.program_id(1)

        @pl.when(c == 0)
        def _():
            prevx_s[...] = jnp.zeros_like(prevx_s)
            hlast_s[...] = jnp.zeros_like(hlast_s)
            _fill_block_diag(wa_ref, wa_s)
            _fill_block_diag(wx_ref, wx_s)

        x = xp_ref[...].astype(F32)
        prev = prevx_s[...]
        row = lax.broadcasted_iota(jnp.int32, x.shape, 0)

        def sh(j):
            return jnp.where(row >= j, pltpu.roll(x, j, 0), pltpu.roll(prev, j, 0))

        xc = (cb_ref[...] + cw_ref[0:1, :] * sh(3) + cw_ref[1:2, :] * sh(2)
              + cw_ref[2:3, :] * sh(1) + cw_ref[3:4, :] * x)
        prevx_s[...] = x
        _, r, i, _, _, a, mult = _lru_gates(xc, wa_s[...], ba_ref[...], wx_s[...], bx_ref[...],
                                            lam_ref[...])
        for k, val in enumerate((xc, r, i, a, mult)):
            saved_ref[:, k * cb:(k + 1) * cb] = val
        av, bv = _scan_rows(a, mult * (i * xc), reverse=False)
        h = av * hlast_s[...] + bv
        h_ref[...] = h
        hlast_s[...] = h_ref[tc - 1:tc, :]
        gel, _ = _gelu_and_grad(g_ref[...].astype(F32))
        y_ref[...] = (h * gel).astype(BF16)

    vec = pl.BlockSpec((1, cb), lambda j, c: (0, j))
    blk = pl.BlockSpec((tc, cb), lambda j, c: (c, j))
    mat = pl.BlockSpec((cb // LRU_HEAD_DIM, LRU_HEAD_DIM, LRU_HEAD_DIM), lambda j, c: (j, 0, 0))
    return _call(
        body, name="lru_fwd", grid=(ncb, ntc),
        in_specs=[
            blk,
            pl.BlockSpec((tc, cb), lambda j, c: (c, ncb + j)),
            pl.BlockSpec((4, cb), lambda j, c: (0, j)),
            vec, mat, vec, mat, vec, vec,
        ],
        out_specs=[blk, blk, pl.BlockSpec((tc, N_LRU_SAVED * cb), lambda j, c: (c, j))],
        out_shape=[jax.ShapeDtypeStruct((t, dr), BF16), jax.ShapeDtypeStruct((t, dr), F32),
                   jax.ShapeDtypeStruct((t, N_LRU_SAVED * dr), F32)],
        scratch_shapes=[pltpu.VMEM((tc, cb), F32), pltpu.VMEM((1, cb), F32),
                        pltpu.VMEM((cb, cb), BF16), pltpu.VMEM((cb, cb), BF16)],
        args=(proj, proj, conv_w, conv_b, w_a, b_a, w_x, b_x, lam), carry=carry)


def _pool_select(col, vals):
    out = vals[3]
    for g in (2, 1, 0):
        out = jnp.where(col < (g + 1) * POOL_GROUP_DIM, vals[g], out)
    return out


def _pool_fwd(proj, pool_w, pool_scale, col_block):
    t = proj.shape[0]
    dp = pool_scale.shape[1]
    tc = _tile(t, 256)
    ntc = t // tc

    def body(x_ref, w_ref, sc_ref, y_ref, p_ref, px, p2, p4, p8):
        c = pl.program_id(0)

        @pl.when(c == 0)
        def _():
            for s in (px, p2, p4, p8):
                s[...] = jnp.zeros_like(s)

        x = x_ref[...].astype(F32)
        row = lax.broadcasted_iota(jnp.int32, x.shape, 0)
        col = lax.broadcasted_iota(jnp.int32, x.shape, 1)

        def sh(v, pv, j):
            return jnp.where(row >= j, pltpu.roll(v, j, 0), pltpu.roll(pv[...], j, 0))

        s2 = x + sh(x, px, 1)
        s4 = s2 + sh(s2, p2, 2)
        s8 = s4 + sh(s4, p4, 4)
        s16 = s8 + sh(s8, p8, 8)
        px[...] = x
        p2[...] = s2
        p4[...] = s4
        p8[...] = s8
        wsum = _pool_select(col, (s2, s4, s8, s16))
        win = _pool_select(col, POOL_WINDOWS)
        cnt = jnp.minimum(c * tc + row + 1, win).astype(F32)
        p = wsum / cnt - x
        pb = p.astype(BF16)
        p_ref[...] = pb
        for g in range(len(POOL_WINDOWS)):
            sl = slice(g * POOL_GROUP_DIM, (g + 1) * POOL_GROUP_DIM)
            yg = _dot_nn(pb[:, sl], w_ref[g]) * sc_ref[:, sl]
            y_ref[:, sl] = yg.astype(BF16)

    return _call(
        body, name="pool_fwd", grid=(ntc,),
        in_specs=[
            pl.BlockSpec((tc, dp), lambda c: (c, col_block)),
            pl.BlockSpec(pool_w.shape, lambda c: (0, 0, 0)),
            pl.BlockSpec((1, dp), lambda c: (0, 0)),
        ],
        out_specs=[pl.BlockSpec((tc, dp), lambda c: (c, 0))] * 2,
        out_shape=[jax.ShapeDtypeStruct((t, dp), BF16)] * 2,
        scratch_shapes=[pltpu.VMEM((tc, dp), F32)] * 4,
        args=(proj, pool_w, pool_scale))[0]


def _branch_mix(y_lru, y_pool, w_lru_up, w_pool_upb, proj, b_gate, ga_block, gb_block, carry=None):
    t, d = y_lru.shape
    dp = y_pool.shape[1]
    bw = w_pool_upb.shape[2]
    tt, tn = _tile(t, 1024), 512
    nj = d // tn

    def body(yl_ref, yp_ref, wl_ref, wp_ref, ga_ref, gb_ref, ba_ref, bb_ref, bra_ref, brb_ref, mix_ref):
        br_a = _dot_nn(yl_ref[...], wl_ref[...])
        wp = jnp.concatenate([wp_ref[b] for b in range(tn // bw)], axis=1)
        br_b = _dot_nn(yp_ref[...], wp)
        bra_ref[...] = br_a.astype(BF16)
        brb_ref[...] = br_b.astype(BF16)
        ga = _sig(ga_ref[...].astype(F32) + ba_ref[...])
        gb = _sig(gb_ref[...].astype(F32) + bb_ref[...])
        mix_ref[...] = (ga * br_a + gb * br_b).astype(BF16)

    out = pl.BlockSpec((tt, tn), lambda j, i: (i, j))
    return _call(
        body, name="branch_mix", grid=(nj, t // tt),
        in_specs=[
            pl.BlockSpec((tt, d), lambda j, i: (i, 0)),
            pl.BlockSpec((tt, dp), lambda j, i: (i, 0)),
            pl.BlockSpec((d, tn), lambda j, i: (0, j)),
            pl.BlockSpec((tn // bw, dp, bw), lambda j, i: (j, 0, 0)),
            pl.BlockSpec((tt, tn), lambda j, i: (i, ga_block + j)),
            pl.BlockSpec((tt, tn), lambda j, i: (i, gb_block + j)),
            pl.BlockSpec((1, tn), lambda j, i: (0, j)),
            pl.BlockSpec((1, tn), lambda j, i: (0, nj + j)),
        ],
        out_specs=[out, out, out],
        out_shape=[jax.ShapeDtypeStruct((t, d), BF16)] * 3,
        args=(y_lru, y_pool, w_lru_up, w_pool_upb, proj, proj, b_gate, b_gate), carry=carry)


def _wo_norm(mix, w_o, x, g2, g3, carry=None):
    t, d = x.shape
    tt = _tile(t, 512)

    def body(mix_ref, w_ref, x_ref, g2_ref, g3_ref, m_ref, x2_ref, h3_ref):
        m = _dot_nn(mix_ref[...], w_ref[...])
        m_ref[...] = m
        mhat, _ = _rms_hat(m)
        x2 = x_ref[...] + mhat * g2_ref[...]
        x2_ref[...] = x2
        xhat, _ = _rms_hat(x2)
        h3_ref[...] = (xhat * g3_ref[...]).astype(BF16)

    row = pl.BlockSpec((tt, d), lambda i: (i, 0))
    vec = pl.BlockSpec((1, d), lambda i: (0, 0))
    return _call(
        body, name="wo_norm", grid=(t // tt,),
        in_specs=[row, pl.BlockSpec((d, d), lambda i: (0, 0)), row, vec, vec],
        out_specs=[row, row, row],
        out_shape=[
            jax.ShapeDtypeStruct((t, d), F32),
            jax.ShapeDtypeStruct((t, d), F32),
            jax.ShapeDtypeStruct((t, d), BF16),
        ],
        args=(mix, w_o, x, g2, g3), carry=carry)


def _ff1(h3, w_ff1b, carry=None):
    t, d = h3.shape
    nb, _, tn = w_ff1b.shape
    tt = _tile(t, 2048)

    def body(h_ref, w_ref, rf_ref):
        rf_ref[...] = jnp.maximum(_dot_nn(h_ref[...], w_ref[...]), 0.0).astype(BF16)

    out = pl.BlockSpec((tt, tn), lambda i, j: (i, j))
    return _call(
        body, name="ff1", grid=(t // tt, nb),
        in_specs=[pl.BlockSpec((tt, d), lambda i, j: (i, 0)), pl.BlockSpec((None, d, tn), lambda i, j: (j, 0, 0))],
        out_specs=[out],
        out_shape=[jax.ShapeDtypeStruct((t, nb * tn), BF16)],
        args=(h3, w_ff1b), carry=carry)


def _ff2_loss(rf, w_ff2, x2, g4, target):
    t, k = rf.shape
    d = x2.shape[1]
    tt, tk = _tile(t, 1024), _tile(k, 1024)
    nk = k // tk

    def body(a_ref, w_ref, x2_ref, g_ref, tg_ref, dy_ref, df_ref, dg_ref, loss_ref, acc):
        i, kk = pl.program_id(0), pl.program_id(1)

        @pl.when(kk == 0)
        def _():
            acc[...] = jnp.zeros_like(acc)

        @pl.when((i == 0) & (kk == 0))
        def _():
            dg_ref[...] = jnp.zeros_like(dg_ref)
            loss_ref[...] = jnp.zeros_like(loss_ref)

        rf_tile = a_ref[...]
        acc[...] += _dot_nn(rf_tile * rf_tile, w_ref[...])

        @pl.when(kk == nk - 1)
        def _():
            def tail(rows):
                fhat, r = _rms_hat(acc[rows, :])
                g = g_ref[...]
                e = x2_ref[rows, :] + fhat * g - tg_ref[rows, :]
                loss_ref[...] += 0.5 * jnp.sum(jnp.mean(e * e, axis=-1, keepdims=True))
                dy = e * (1.0 / d)
                dy_ref[rows, :] = dy.astype(BF16)
                df, dg = _rms_bwd(dy, fhat, r, g)
                df_ref[rows, :] = df.astype(BF16)
                dg_ref[...] += dg

            _row_chunks(tt, tail)

    row = pl.BlockSpec((tt, d), lambda i, kk: (i, 0))
    vec = pl.BlockSpec((1, d), lambda i, kk: (0, 0))
    return _call(
        body, name="ff2_loss", grid=(t // tt, nk),
        in_specs=[
            pl.BlockSpec((tt, tk), lambda i, kk: (i, kk)),
            pl.BlockSpec((tk, d), lambda i, kk: (kk, 0)),
            row, vec, row,
        ],
        out_specs=[row, row, vec, pl.BlockSpec((1, 128), lambda i, kk: (0, 0))],
        out_shape=[
            jax.ShapeDtypeStruct((t, d), BF16),
            jax.ShapeDtypeStruct((t, d), BF16),
            jax.ShapeDtypeStruct((1, d), F32),
            jax.ShapeDtypeStruct((1, 128), F32),
        ],
        scratch_shapes=[pltpu.VMEM((tt, d), F32)],
        args=(rf, w_ff2, x2, g4, target))[0]


def _ff2_bwd(df, w_ff2, rf, carry=None):
    t, d = df.shape
    n = w_ff2.shape[0]
    tt, tn = _tile(t, 2048), _tile(n, 512)

    def body(df_ref, w_ref, rf_ref, out_ref):
        d_act = _dot_nt(df_ref[...], w_ref[...])
        out_ref[...] = (d_act * (2.0 * rf_ref[...].astype(F32))).astype(BF16)

    blk = pl.BlockSpec((tt, tn), lambda i, j: (i, j))
    return _call(
        body, name="ff2_bwd", grid=(t // tt, n // tn),
        in_specs=[pl.BlockSpec((tt, d), lambda i, j: (i, 0)), pl.BlockSpec((tn, d), lambda i, j: (j, 0)), blk],
        out_specs=[blk],
        out_shape=[jax.ShapeDtypeStruct((t, n), BF16)],
        args=(df, w_ff2, rf), carry=carry)


def _wgrad(a, b, name, prev=None, row_off=0, rows=None, carry=None, square_a=False):
    t, m = a.shape
    n = b.shape[1]
    rows = m if rows is None else rows
    tm, tk = _tile(m, 512), _tile(t, 2048)
    nk = t // tk
    assert row_off % tm == 0
    off = row_off // tm

    def body(*refs):
        a_ref, b_ref = refs[0], refs[1]
        o32_ref, o16_ref, acc = refs[-3], refs[-2], refs[-1]
        kk = pl.program_id(1)

        @pl.when(kk == 0)
        def _():
            acc[...] = jnp.zeros_like(acc)

        a_tile = a_ref[...]
        acc[...] += _dot_tn(a_tile * a_tile if square_a else a_tile, b_ref[...])

        @pl.when(kk == nk - 1)
        def _():
            o32_ref[...] = acc[...]
            o16_ref[...] = acc[...].astype(BF16)

    in_specs = [pl.BlockSpec((tk, tm), lambda i, kk: (kk, i)), pl.BlockSpec((tk, n), lambda i, kk: (kk, 0))]
    args = [a, b]
    aliases = {}
    if prev is not None:
        in_specs += [ANY, ANY]
        args += list(prev)
        aliases = {2: 0, 3: 1}
    out = pl.BlockSpec((tm, n), lambda i, kk: (off + i, 0))
    return _call(
        body, name=name, grid=(m // tm, nk),
        in_specs=in_specs, out_specs=[out, out],
        out_shape=[jax.ShapeDtypeStruct((rows, n), F32), jax.ShapeDtypeStruct((rows, n), BF16)],
        scratch_shapes=[pltpu.VMEM((tm, n), F32)],
        aliases=aliases, args=args, carry=carry)


def _wgrad_parts(parts, b, name, carry=None):
    t, n = b.shape
    tm = 512
    bounds = []
    lo = 0
    for part in parts:
        assert part.shape[0] == t and part.shape[1] % tm == 0
        bounds.append((lo, lo + part.shape[1] // tm))
        lo += part.shape[1] // tm
    nm = lo
    np_ = len(parts)

    def body(*refs):
        p_refs, b_ref, o32_ref, o16_ref = refs[:np_], refs[np_], refs[np_ + 1], refs[np_ + 2]
        i = pl.program_id(0)
        for (lo_p, hi_p), p_ref in zip(bounds, p_refs):
            @pl.when((i >= lo_p) & (i < hi_p))
            def _(p_ref=p_ref):
                res = _dot_tn(p_ref[...], b_ref[...])
                o32_ref[...] = res
                o16_ref[...] = res.astype(BF16)

    def part_spec(lo_p, hi_p):
        return pl.BlockSpec((t, tm), lambda i: (0, jnp.clip(i - lo_p, 0, hi_p - lo_p - 1)))

    out = pl.BlockSpec((tm, n), lambda i: (i, 0))
    return _call(
        body, name=name, grid=(nm,),
        in_specs=[part_spec(lo_p, hi_p) for lo_p, hi_p in bounds] + [pl.BlockSpec((t, n), lambda i: (0, 0))],
        out_specs=[out, out],
        out_shape=[jax.ShapeDtypeStruct((nm * tm, n), F32), jax.ShapeDtypeStruct((nm * tm, n), BF16)],
        args=(*parts, b), carry=carry)


def _wgrad_cols(a, b, bw, tn, name, carry=None):
    t, m = a.shape
    n = b.shape[1]
    per_step = tn // bw

    def body(a_ref, b_ref, o32_ref, o16_ref):
        res = _dot_tn(a_ref[...], b_ref[...])
        for blk in range(per_step):
            part = res[:, blk * bw:(blk + 1) * bw]
            o32_ref[blk] = part
            o16_ref[blk] = part.astype(BF16)

    out = pl.BlockSpec((per_step, m, bw), lambda j: (j, 0, 0))
    return _call(
        body, name=name, grid=(n // tn,),
        in_specs=[pl.BlockSpec((t, m), lambda j: (0, 0)), pl.BlockSpec((t, tn), lambda j: (0, j))],
        out_specs=[out, out],
        out_shape=[jax.ShapeDtypeStruct((n // bw, m, bw), F32), jax.ShapeDtypeStruct((n // bw, m, bw), BF16)],
        args=(a, b), carry=carry)


def _ff1_bwd_norms(d_f1, w_ff1b, dy, x2, g3, m, g2, carry=None):
    t, k = d_f1.shape
    d = x2.shape[1]
    bw = w_ff1b.shape[2]
    per_step = 2
    tt, tk = _tile(t, 1024), per_step * bw
    nk = k // tk

    def body(a_ref, w_ref, dy_ref, x2_ref, g3_ref, m_ref, g2_ref, dx2_ref, dm_ref, dg3_ref, dg2_ref, acc):
        i, kk = pl.program_id(0), pl.program_id(1)

        @pl.when(kk == 0)
        def _():
            acc[...] = jnp.zeros_like(acc)

        @pl.when((i == 0) & (kk == 0))
        def _():
            dg3_ref[...] = jnp.zeros_like(dg3_ref)
            dg2_ref[...] = jnp.zeros_like(dg2_ref)

        a_tile = a_ref[...]
        for b in range(per_step):
            acc[...] += _dot_nt(a_tile[:, b * bw:(b + 1) * bw], w_ref[b])

        @pl.when(kk == nk - 1)
        def _():
            def tail(rows):
                xhat, r3 = _rms_hat(x2_ref[rows, :])
                dx, dg3 = _rms_bwd(acc[rows, :], xhat, r3, g3_ref[...])
                dx2 = dy_ref[rows, :].astype(F32) + dx
                dx2_ref[rows, :] = dx2
                dg3_ref[...] += dg3
                mhat, r2 = _rms_hat(m_ref[rows, :])
                dm, dg2 = _rms_bwd(dx2, mhat, r2, g2_ref[...])
                dm_ref[rows, :] = dm.astype(BF16)
                dg2_ref[...] += dg2

            _row_chunks(tt, tail)

    row = pl.BlockSpec((tt, d), lambda i, kk: (i, 0))
    vec = pl.BlockSpec((1, d), lambda i, kk: (0, 0))
    return _call(
        body, name="ff1_bwd_norms", grid=(t // tt, nk),
        in_specs=[
            pl.BlockSpec((tt, tk), lambda i, kk: (i, kk)),
            pl.BlockSpec((per_step, d, bw), lambda i, kk: (kk, 0, 0)),
            row, row, vec, row, vec,
        ],
        out_specs=[row, row, vec, vec],
        out_shape=[
            jax.ShapeDtypeStruct((t, d), F32),
            jax.ShapeDtypeStruct((t, d), BF16),
            jax.ShapeDtypeStruct((1, d), F32),
            jax.ShapeDtypeStruct((1, d), F32),
        ],
        scratch_shapes=[pltpu.VMEM((tt, d), F32)],
        args=(d_f1, w_ff1b, dy, x2, g3, m, g2), carry=carry)


def _wo_bwd_mix(dm, w_o, br_a, br_b, proj, b_gate, ga_block, gb_block, carry=None):
    t, d = dm.shape
    tt, tn = _tile(t, 1024), 512
    nj = d // tn

    def body(dm_ref, w_ref, bra_ref, brb_ref, ga_ref, gb_ref, ba_ref, bb_ref,
             dbra_ref, dbrb_ref, dga_ref, dgb_ref, dba_ref, dbb_ref):
        i = pl.program_id(1)

        @pl.when(i == 0)
        def _():
            dba_ref[...] = jnp.zeros_like(dba_ref)
            dbb_ref[...] = jnp.zeros_like(dbb_ref)

        d_mix = _dot_nt(dm_ref[...], w_ref[...])
        ga = _sig(ga_ref[...].astype(F32) + ba_ref[...])
        gb = _sig(gb_ref[...].astype(F32) + bb_ref[...])
        dbra_ref[...] = (d_mix * ga).astype(BF16)
        dbrb_ref[...] = (d_mix * gb).astype(BF16)
        dga = d_mix * bra_ref[...].astype(F32) * (ga * (1.0 - ga))
        dgb = d_mix * brb_ref[...].astype(F32) * (gb * (1.0 - gb))
        dga_ref[...] = dga.astype(BF16)
        dgb_ref[...] = dgb.astype(BF16)
        dba_ref[...] += jnp.sum(dga, axis=0, keepdims=True)
        dbb_ref[...] += jnp.sum(dgb, axis=0, keepdims=True)

    blk = pl.BlockSpec((tt, tn), lambda j, i: (i, j))
    vec = pl.BlockSpec((1, tn), lambda j, i: (0, j))
    return _call(
        body, name="wo_bwd_mix", grid=(nj, t // tt),
        in_specs=[
            pl.BlockSpec((tt, d), lambda j, i: (i, 0)),
            pl.BlockSpec((tn, d), lambda j, i: (j, 0)),
            blk, blk,
            pl.BlockSpec((tt, tn), lambda j, i: (i, ga_block + j)),
            pl.BlockSpec((tt, tn), lambda j, i: (i, gb_block + j)),
            vec,
            pl.BlockSpec((1, tn), lambda j, i: (0, nj + j)),
        ],
        out_specs=[blk, blk, blk, blk, vec, vec],
        out_shape=[jax.ShapeDtypeStruct((t, d), BF16)] * 4 + [jax.ShapeDtypeStruct((1, d), F32)] * 2,
        args=(dm, w_o, br_a, br_b, proj, proj, b_gate, b_gate), carry=carry)


def _lru_up_bwd(d_br_a, w_lru_up, proj, h, g_block, carry=None):
    t, d = d_br_a.shape
    tt, tn = _tile(t, 1024), 512

    def body(a_ref, w_ref, g_ref, h_ref, dh_ref, dg_ref):
        d_y = _dot_nt(a_ref[...], w_ref[...])
        gel, gel_grad = _gelu_and_grad(g_ref[...].astype(F32))
        dh_ref[...] = d_y * gel
        dg_ref[...] = (d_y * h_ref[...] * gel_grad).astype(BF16)

    blk = pl.BlockSpec((tt, tn), lambda i, j: (i, j))
    return _call(
        body, name="lru_up_bwd", grid=(t // tt, d // tn),
        in_specs=[
            pl.BlockSpec((tt, d), lambda i, j: (i, 0)),
            pl.BlockSpec((tn, d), lambda i, j: (j, 0)),
            pl.BlockSpec((tt, tn), lambda i, j: (i, g_block + j)),
            blk,
        ],
        out_specs=[blk, blk],
        out_shape=[jax.ShapeDtypeStruct((t, d), F32), jax.ShapeDtypeStruct((t, d), BF16)],
        args=(d_br_a, w_lru_up, proj, h), carry=carry)


def _lru_bwd(dh, h, saved, proj, conv_w, w_a, w_x, lam, carry=None):
    t, dr = dh.shape
    cb = LRU_CB
    hd = LRU_HEAD_DIM
    per = cb // hd
    tc = _tile(t, 256)
    ncb, ntc = dr // cb, t // tc

    def body(dh_ref, h_ref, hp_ref, saved_ref, xp_ref, cw_ref, wa_ref, wx_ref,
             lam_ref, dxp_ref, dwa_ref, dba_ref, dwx_ref, dbx_ref, dlam_ref, dcw_ref, dcb_ref,
             nextd_s, anext_s, gnext_s, tmp_s, wa_s, wx_s):
        c = pl.program_id(1)
        rc = ntc - 1 - c

        @pl.when(c == 0)
        def _():
            nextd_s[...] = jnp.zeros_like(nextd_s)
            anext_s[...] = jnp.zeros_like(anext_s)
            gnext_s[...] = jnp.zeros_like(gnext_s)
            for ref in (dwa_ref, dba_ref, dwx_ref, dbx_ref, dlam_ref, dcw_ref, dcb_ref):
                ref[...] = jnp.zeros_like(ref)
            _fill_block_diag(wa_ref, wa_s)
            _fill_block_diag(wx_ref, wx_s)

        xc, r, i, a, mult = [saved_ref[:, k * cb:(k + 1) * cb] for k in range(N_LRU_SAVED)]
        wa, wx, lam = wa_s[...], wx_s[...], lam_ref[...]
        xcb = xc.astype(BF16)
        sp = _softplus_neg(lam)
        row = lax.broadcasted_iota(jnp.int32, xc.shape, 0)
        h = h_ref[...]
        hp = jnp.where(rc == 0, 0.0, hp_ref[...])
        hprev = jnp.where(row >= 1, pltpu.roll(h, 1, 0), pltpu.roll(hp, 1, 0))

        def up(v, nv, j):
            return jnp.where(row < tc - j, pltpu.roll(v, tc - j, 0), nv)

        av, bv = _scan_rows(up(a, anext_s[...], 1), dh_ref[...], reverse=True)
        gt = av * gnext_s[...] + bv
        tmp_s[...] = gt
        gnext_s[...] = tmp_s[0:1, :]
        tmp_s[...] = a
        anext_s[...] = tmp_s[0:1, :]

        da = gt * hprev
        ixc = i * xc
        d_mult = gt * ixc
        d_i = gt * mult * xc
        d_xc = gt * mult * i
        d_log_a = da * a - d_mult * (a * a) / mult
        d_pre_r = (d_log_a * ((-LRU_C) * sp)) * (r * (1.0 - r))
        d_pre_i = d_i * (i * (1.0 - i))
        d_sp = jnp.sum(d_log_a * ((-LRU_C) * r), axis=0, keepdims=True)
        dlam_ref[...] += d_sp * (-1.0 / (1.0 + jnp.exp(lam)))
        dpr = d_pre_r.astype(BF16)
        dpi = d_pre_i.astype(BF16)
        dba_ref[...] += jnp.sum(d_pre_r, axis=0, keepdims=True)
        dbx_ref[...] += jnp.sum(d_pre_i, axis=0, keepdims=True)
        pa = _dot_tn(xcb, dpr)
        px = _dot_tn(xcb, dpi)
        for k in range(per):
            dwa_ref[k] += pa[k * hd:(k + 1) * hd, k * hd:(k + 1) * hd]
            dwx_ref[k] += px[k * hd:(k + 1) * hd, k * hd:(k + 1) * hd]
        d_xc = d_xc + _dot_nt(dpr, wa) + _dot_nt(dpi, wx)

        nxt = nextd_s[...]
        xp = xp_ref[...].astype(F32)
        dxp = cw_ref[3:4, :] * d_xc
        dcw_ref[3:4, :] += jnp.sum(xp * d_xc, axis=0, keepdims=True)
        for j in (1, 2, 3):
            uj = up(d_xc, pltpu.roll(nxt, tc - j, 0), j)
            dxp = dxp + cw_ref[3 - j:4 - j, :] * uj
            dcw_ref[3 - j:4 - j, :] += jnp.sum(xp * uj, axis=0, keepdims=True)
        dcb_ref[...] += jnp.sum(d_xc, axis=0, keepdims=True)
        nextd_s[...] = d_xc
        dxp_ref[...] = dxp.astype(BF16)

    vec = pl.BlockSpec((1, cb), lambda j, c: (0, j))
    blk = pl.BlockSpec((tc, cb), lambda j, c: (ntc - 1 - c, j))
    mat = pl.BlockSpec((per, hd, hd), lambda j, c: (j, 0, 0))
    cwb = pl.BlockSpec((4, cb), lambda j, c: (0, j))
    return _call(
        body, name="lru_bwd", grid=(ncb, ntc),
        in_specs=[
            blk, blk,
            pl.BlockSpec((tc, cb), lambda j, c: (jnp.maximum(ntc - 2 - c, 0), j)),
            pl.BlockSpec((tc, N_LRU_SAVED * cb), lambda j, c: (ntc - 1 - c, j)),
            blk, cwb, mat, mat, vec,
        ],
        out_specs=[blk, mat, vec, mat, vec, vec, cwb, vec],
        out_shape=[
            jax.ShapeDtypeStruct((t, dr), BF16),
            jax.ShapeDtypeStruct(w_a.shape, F32),
            jax.ShapeDtypeStruct((1, dr), F32),
            jax.ShapeDtypeStruct(w_x.shape, F32),
            jax.ShapeDtypeStruct((1, dr), F32),
            jax.ShapeDtypeStruct((1, dr), F32),
            jax.ShapeDtypeStruct((4, dr), F32),
            jax.ShapeDtypeStruct((1, dr), F32),
        ],
        scratch_shapes=[
            pltpu.VMEM((tc, cb), F32),
            pltpu.VMEM((1, cb), F32),
            pltpu.VMEM((1, cb), F32),
            pltpu.VMEM((tc, cb), F32),
            pltpu.VMEM((cb, cb), BF16),
            pltpu.VMEM((cb, cb), BF16),
        ],
        args=(dh, h, h, saved, proj, conv_w, w_a, w_x, lam), carry=carry)


def _pool_bwd(d_br_b, w_pool_upb, p, pool_w, pool_scale):
    t, d = d_br_b.shape
    nwb, dp, _ = w_pool_upb.shape
    tc = _tile(t, 256)
    ntc = t // tc
    ng = len(POOL_WINDOWS)

    def body(db_ref, wu_ref, p_ref, w_ref, sc_ref, dx_ref, dw_ref, dsc_ref, nz, n2, n4, n8, dp_s, dy_s):
        c = pl.program_id(0)
        rc = ntc - 1 - c

        @pl.when(c == 0)
        def _():
            for s in (nz, n2, n4, n8):
                s[...] = jnp.zeros_like(s)
            dw_ref[...] = jnp.zeros_like(dw_ref)
            dsc_ref[...] = jnp.zeros_like(dsc_ref)

        wu = jnp.concatenate([wu_ref[b] for b in range(nwb)], axis=1)
        dy_s[...] = _dot_nt(db_ref[...], wu)
        for g in range(ng):
            sl = slice(g * POOL_GROUP_DIM, (g + 1) * POOL_GROUP_DIM)
            pg = p_ref[:, sl]
            dyg = dy_s[:, sl]
            wg = w_ref[g].astype(BF16)
            q = _dot_nn(pg, wg)
            dsc_ref[:, sl] += jnp.sum(dyg * q, axis=0, keepdims=True)
            dpw = (dyg * sc_ref[:, sl]).astype(BF16)
            dw_ref[g] += _dot_tn(pg, dpw)
            dp_s[:, sl] = _dot_nt(dpw, wg)

        dpv = dp_s[...]
        row = lax.broadcasted_iota(jnp.int32, dpv.shape, 0)
        col = lax.broadcasted_iota(jnp.int32, dpv.shape, 1)
        win = _pool_select(col, POOL_WINDOWS)
        cnt = jnp.minimum(rc * tc + row + 1, win).astype(F32)
        z = dpv / cnt

        def up(v, nv, j):
            return jnp.where(row < tc - j, pltpu.roll(v, tc - j, 0), pltpu.roll(nv[...], tc - j, 0))

        u2 = z + up(z, nz, 1)
        u4 = u2 + up(u2, n2, 2)
        u8 = u4 + up(u4, n4, 4)
        u16 = u8 + up(u8, n8, 8)
        nz[...] = z
        n2[...] = u2
        n4[...] = u4
        n8[...] = u8
        dx_ref[...] = (_pool_select(col, (u2, u4, u8, u16)) - dpv).astype(BF16)

    blk = pl.BlockSpec((tc, dp), lambda c: (ntc - 1 - c, 0))
    full_w = pl.BlockSpec(pool_w.shape, lambda c: (0, 0, 0))
    vec = pl.BlockSpec((1, dp), lambda c: (0, 0))
    return _call(
        body, name="pool_bwd", grid=(ntc,),
        in_specs=[pl.BlockSpec((tc, d), lambda c: (ntc - 1 - c, 0)),
                  pl.BlockSpec(w_pool_upb.shape, lambda c: (0, 0, 0)), blk, full_w, vec],
        out_specs=[blk, full_w, vec],
        out_shape=[
            jax.ShapeDtypeStruct((t, dp), BF16),
            jax.ShapeDtypeStruct(pool_w.shape, F32),
            jax.ShapeDtypeStruct((1, dp), F32),
        ],
        scratch_shapes=[pltpu.VMEM((tc, dp), F32)] * 6,
        args=(d_br_b, w_pool_upb, p, pool_w, pool_scale))[0]


def _win_bwd_norm(parts, w_int, dx2, x, g1, carry=None):
    t, d = x.shape
    tk = 512
    tt = _tile(t, 1024)
    bounds = []
    k0 = 0
    for part in parts:
        assert part.shape[1] % tk == 0
        bounds.append((k0, k0 + part.shape[1] // tk))
        k0 += part.shape[1] // tk
    nk = k0
    assert nk * tk == w_int.shape[0]
    np_ = len(parts)

    def body(*refs):
        p_refs = refs[:np_]
        w_ref, dx2_ref, x_ref, g_ref, gx_ref, dg_ref, acc = refs[np_:]
        i, kk = pl.program_id(0), pl.program_id(1)

        @pl.when(kk == 0)
        def _():
            acc[...] = jnp.zeros_like(acc)

        @pl.when((i == 0) & (kk == 0))
        def _():
            dg_ref[...] = jnp.zeros_like(dg_ref)

        for (lo, hi), p_ref in zip(bounds, p_refs):
            @pl.when((kk >= lo) & (kk < hi))
            def _(p_ref=p_ref):
                acc[...] += _dot_nn(p_ref[...], w_ref[...])

        @pl.when(kk == nk - 1)
        def _():
            def tail(rows):
                xhat, r = _rms_hat(x_ref[rows, :])
                dx, dg = _rms_bwd(acc[rows, :], xhat, r, g_ref[...])
                gx_ref[rows, :] = dx2_ref[rows, :] + dx
                dg_ref[...] += dg

            _row_chunks(tt, tail)

    def part_spec(lo, hi):
        return pl.BlockSpec((tt, tk), lambda i, kk: (i, jnp.clip(kk - lo, 0, hi - lo - 1)))

    row = pl.BlockSpec((tt, d), lambda i, kk: (i, 0))
    vec = pl.BlockSpec((1, d), lambda i, kk: (0, 0))
    return _call(
        body, name="win_bwd_norm", grid=(t // tt, nk),
        in_specs=[part_spec(lo, hi) for lo, hi in bounds]
        + [pl.BlockSpec((tk, d), lambda i, kk: (kk, 0)), row, row, vec],
        out_specs=[row, vec],
        out_shape=[jax.ShapeDtypeStruct((t, d), F32), jax.ShapeDtypeStruct((1, d), F32)],
        scratch_shapes=[pltpu.VMEM((tt, d), F32)],
        args=(*parts, w_int, dx2, x, g1), carry=carry)


def _adam_math(w, g, m, v):
    m = ADAM_B1 * m + (1.0 - ADAM_B1) * g
    v = ADAM_B2 * v + (1.0 - ADAM_B2) * (g * g)
    m_hat = m / (1.0 - ADAM_B1 ** ADAM_STEP)
    v_hat = v / (1.0 - ADAM_B2 ** ADAM_STEP)
    delta = -ADAM_LR * (m_hat / (jnp.sqrt(v_hat) + ADAM_EPS) + ADAM_WD * w)
    return delta, m, v


def _adamw_big(ws, gs, ms, vs):
    n = len(ws)
    nb = 4
    pair = [isinstance(g, tuple) for g in gs]

    def body(*refs):
        p = 0
        ins = []
        for a in range(n):
            k = 5 if pair[a] else 4
            ins.append(refs[p:p + k])
            p += k
        for a in range(n):
            g_out, d_ref, nm_ref, nv_ref = refs[p + 4 * a:p + 4 * a + 4]
            if pair[a]:
                w_ref, own_ref, recv_ref, m_ref, v_ref = ins[a]
                g = own_ref[...]
                for k in range(3):
                    g = g + recv_ref[k].astype(F32)
            else:
                w_ref, g_ref, m_ref, v_ref = ins[a]
                g = g_ref[...]
            dl, m, v = _adam_math(w_ref[...], g, m_ref[...], v_ref[...])
            g_out[...] = g
            d_ref[...] = dl
            nm_ref[...] = m
            nv_ref[...] = v

    in_specs, out_specs, out_shape, args = [], [], [], []
    for a, (w, g, m, v) in enumerate(zip(ws, gs, ms, vs)):
        rows, cols = w.shape
        blk = pl.BlockSpec((rows // nb, cols), lambda i: (i, 0))
        if pair[a]:
            in_specs += [blk, pl.BlockSpec((None, rows // nb, cols), lambda i: (0, i, 0)),
                         pl.BlockSpec((3, rows // nb, cols), lambda i: (0, i, 0)), blk, blk]
            args += [w, g[0], g[1], m, v]
        else:
            in_specs += [blk] * 4
            args += [w, g, m, v]
        out_specs += [blk] * 4
        out_shape += [jax.ShapeDtypeStruct(w.shape, F32)] * 4
    outs = _call(body, name="adamw_big", grid=(nb,), in_specs=in_specs, out_specs=out_specs,
                 out_shape=out_shape, args=args)[0]
    return [tuple(outs[4 * a:4 * a + 4]) for a in range(n)]


SMALL_ORDER = ("norm_mix_pre", "norm_mix_post", "norm_mlp_pre", "norm_mlp_post", "b_gate", "conv_w", "conv_b",
               "lru_w_a", "lru_b_a", "lru_w_x", "lru_b_x", "lru_lambda", "pool_w", "pool_scale")
VEC_ROW = dict(norm_mix_pre=0, norm_mix_post=1, norm_mlp_pre=2, norm_mlp_post=3, conv_b=6, lru_b_a=7,
               lru_b_x=8, lru_lambda=9)
ROW_B_GATE, ROW_POOL_SCALE, ROW_CONV_W, ROW_LOSS, N_VEC_ROWS = 4, 10, 11, 15, 16


def _adamw_small(vec_parts, g_pool, g_wa, g_wx, me, params):
    d = vec_parts.shape[2]
    names = SMALL_ORDER
    n = len(names)
    cw_cols = params["conv_w"][0].shape[2]

    def body(me_ref, vec_ref, vecc_ref, gp_ref, gwa_ref, gwx_ref, *refs):
        wmv = refs[:3 * n]
        loss_ref = refs[3 * n]
        outs = refs[3 * n + 1:3 * n + 1 + 4 * n]
        vs, vsc = refs[3 * n + 1 + 4 * n:]
        acc, accc = vec_ref[0], vecc_ref[0]
        for k in range(1, N_DEV):
            acc = acc + vec_ref[k]
            accc = accc + vecc_ref[k]
        vs[...] = acc
        vsc[...] = accc
        loss_ref[...] = vs[ROW_LOSS:ROW_LOSS + 1, 0:128]

        def upd(a, g, idx):
            w_ref, m_ref, v_ref = wmv[3 * a:3 * a + 3]
            g_ref, d_ref, nm_ref, nv_ref = outs[4 * a:4 * a + 4]
            dl, m, v = _adam_math(w_ref[idx], g, m_ref[idx], v_ref[idx])
            g_ref[idx] = g
            d_ref[idx] = dl
            nm_ref[idx] = m
            nv_ref[idx] = v

        for a, name in enumerate(names):
            if name in VEC_ROW:
                r = VEC_ROW[name]
                upd(a, vs[r:r + 1, :], (slice(None), slice(None)))
            elif name == "b_gate":
                for half in range(2):
                    r = ROW_B_GATE + half
                    upd(a, vs[r:r + 1, :], (slice(None), slice(half * d, (half + 1) * d)))
            elif name == "pool_scale":
                width = params[name][0].shape[1]
                upd(a, vs[ROW_POOL_SCALE:ROW_POOL_SCALE + 1, 0:width], (slice(None), slice(None)))
            elif name == "conv_w":
                upd(a, vsc[ROW_CONV_W:ROW_CONV_W + 4, :], (0,))
            elif name == "pool_w":
                upd(a, gp_ref[...], (Ellipsis,))
            elif name == "lru_w_a":
                upd(a, gwa_ref[...], (Ellipsis,))
            elif name == "lru_w_x":
                upd(a, gwx_ref[...], (Ellipsis,))
            else:
                raise ValueError(name)

    def whole(shape):
        nd = len(shape)
        return pl.BlockSpec(tuple(shape), lambda i, me_ref: (0,) * nd)

    in_specs = [
        whole(vec_parts.shape),
        pl.BlockSpec((N_DEV, N_VEC_ROWS, cw_cols), lambda i, me_ref: (0, 0, me_ref[0])),
        whole(g_pool.shape), whole(g_wa.shape), whole(g_wx.shape),
    ]
    args = [vec_parts, vec_parts, g_pool, g_wa, g_wx]
    out_specs = [whole((1, 128))]
    out_shape = [jax.ShapeDtypeStruct((1, 128), F32)]
    for name in names:
        for arr in params[name]:
            in_specs.append(whole(arr.shape))
            args.append(arr)
        shp = params[name][0].shape
        out_specs += [whole(shp)] * 4
        out_shape += [jax.ShapeDtypeStruct(shp, F32)] * 4
    grid_spec = pltpu.PrefetchScalarGridSpec(
        num_scalar_prefetch=1, grid=(1,), in_specs=in_specs, out_specs=out_specs,
        scratch_shapes=[pltpu.VMEM((N_VEC_ROWS, d), F32), pltpu.VMEM((N_VEC_ROWS, cw_cols), F32)])
    outs = pl.pallas_call(
        body, name="adamw_small", grid_spec=grid_spec, out_shape=out_shape,
        compiler_params=pltpu.CompilerParams(
            dimension_semantics=("arbitrary",), vmem_limit_bytes=V7X_VMEM_LIMIT_BYTES),
    )(me, *_in_hbm(args))
    return outs[0], {name: tuple(outs[1 + 4 * a:5 + 4 * a]) for a, name in enumerate(names)}


def _rs_sum(fulls, recvs, shard_ids, slot_ids, name):
    n = len(fulls)

    def body(sh_ref, sl_ref, *refs):
        s = pl.program_id(0)
        for a in range(n):
            full_ref, recv_ref = refs[2 * a], refs[2 * a + 1]
            own_ref, send_ref = refs[2 * n + 2 * a], refs[2 * n + 2 * a + 1]
            v = full_ref[...] + recv_ref[...].astype(F32)

            @pl.when(s == 0)
            def _(own_ref=own_ref, v=v):
                own_ref[...] = v

            @pl.when(s > 0)
            def _(send_ref=send_ref, v=v):
                send_ref[...] = v.astype(send_ref.dtype)

    in_specs, out_specs, out_shape, args = [], [], [], []
    for full, recv in zip(fulls, recvs):
        r, rest = recv.shape[1], tuple(recv.shape[2:])
        zeros = (0,) * len(rest)
        in_specs += [
            pl.BlockSpec((r,) + rest, lambda s, sh, sl, zeros=zeros: (sh[s],) + zeros),
            pl.BlockSpec((None, r) + rest, lambda s, sh, sl, zeros=zeros: (sl[s], 0) + zeros),
        ]
        out_specs += [
            pl.BlockSpec((None, r) + rest, lambda s, sh, sl, zeros=zeros: (0, 0) + zeros),
            pl.BlockSpec((None, r) + rest, lambda s, sh, sl, zeros=zeros: (jnp.maximum(s - 1, 0), 0) + zeros),
        ]
        out_shape += [jax.ShapeDtypeStruct((1, r) + rest, F32), jax.ShapeDtypeStruct((3, r) + rest, recv.dtype)]
        args += [full, recv]
    grid_spec = pltpu.PrefetchScalarGridSpec(
        num_scalar_prefetch=2, grid=(4,), in_specs=in_specs, out_specs=out_specs)
    outs = pl.pallas_call(
        body,
        name=name,
        grid_spec=grid_spec,
        out_shape=out_shape,
        compiler_params=pltpu.CompilerParams(
            dimension_semantics=("arbitrary",), vmem_limit_bytes=V7X_VMEM_LIMIT_BYTES),
    )(shard_ids, slot_ids, *_in_hbm(args))
    return [(outs[2 * a], outs[2 * a + 1]) for a in range(n)]


def _finals(pairs, name, carry=None):
    nb = 4
    n = len(pairs)

    def body(*refs):
        for a in range(n):
            own_ref, recv_ref = refs[2 * a], refs[2 * a + 1]
            acc = own_ref[...]
            for k in range(3):
                acc = acc + recv_ref[k].astype(F32)
            refs[2 * n + a][...] = acc

    in_specs, out_specs, out_shape, args = [], [], [], []
    for own, recv in pairs:
        _, rows, cols = own.shape
        in_specs += [pl.BlockSpec((None, rows // nb, cols), lambda i: (0, i, 0)),
                     pl.BlockSpec((3, rows // nb, cols), lambda i: (0, i, 0))]
        args += [own, recv]
        out_specs.append(pl.BlockSpec((rows // nb, cols), lambda i: (i, 0)))
        out_shape.append(jax.ShapeDtypeStruct((rows, cols), F32))
    return _call(body, name=name, grid=(nb,), in_specs=in_specs, out_specs=out_specs,
                 out_shape=out_shape, args=args, carry=carry)


def _rs_sums(fulls_f32, recv1, tag):
    x, y, c = _place()
    qs = jnp.stack([2 * x + y, 2 * (1 - x) + y, 2 * x + (1 - y), 2 * (1 - x) + (1 - y)]).astype(jnp.int32)
    shard_ids = 2 * qs + c
    return _rs_sum(fulls_f32, recv1, shard_ids, qs, "rs_sum_" + tag)


def _rs_level1(fulls_f32, fulls_send, tag):
    recv1 = _run_plan(_rs_sibling_plan(fulls_send), "rs_sibling_" + tag)
    return _rs_sums(fulls_f32, recv1, tag)


def _rows(g):
    return g.reshape(g.shape[0] * g.shape[1], g.shape[2])


def kernel(x, norm_mix_pre, norm_mix_post, norm_mlp_pre, norm_mlp_post, w_in, b_gate, conv_w, conv_b, lru_w_a, lru_b_a, lru_w_x, lru_b_x, lru_lambda, pool_w, pool_scale, w_lru_up, w_pool_up, w_o, w_ff1, w_ff2, loss_target, m_norm_mix_pre, m_norm_mix_post, m_norm_mlp_pre, m_norm_mlp_post, m_w_in, m_b_gate, m_conv_w, m_conv_b, m_lru_w_a, m_lru_b_a, m_lru_w_x, m_lru_b_x, m_lru_lambda, m_pool_w, m_pool_scale, m_w_lru_up, m_w_pool_up, m_w_o, m_w_ff1, m_w_ff2, v_norm_mix_pre, v_norm_mix_post, v_norm_mlp_pre, v_norm_mlp_post, v_w_in, v_b_gate, v_conv_w, v_conv_b, v_lru_w_a, v_lru_b_a, v_lru_w_x, v_lru_b_x, v_lru_lambda, v_pool_w, v_pool_scale, v_w_lru_up, v_w_pool_up, v_w_o, v_w_ff1, v_w_ff2):
    t, d = x.shape[1], x.shape[2]
    d_rnn = conv_b.shape[1]
    d_pool = pool_scale.shape[1]
    per = LRU_CB // LRU_HEAD_DIM
    xi, yi, ci = _place()
    me = 4 * xi + 2 * yi + ci

    x2d = x[0]
    tgt = loss_target[0]

    s_in = w_in[0].T.astype(BF16)
    s_lu = w_lru_up[0].astype(BF16)
    s_pu = w_pool_up[0].astype(BF16)
    s_o = w_o[0].astype(BF16)
    s_f1 = w_ff1[0].astype(BF16)
    s_f2 = w_ff2[0].astype(BF16)
    s_cw = jnp.pad(conv_w[0], ((0, 4), (0, 0)))

    g_in, g_cw = _run_plan(_ag_plan([s_in, s_cw]), "ag_w_in")
    w_int = _rows(g_in)
    conv_w_full = jnp.transpose(g_cw[:, :4, :], (1, 0, 2)).reshape(4, d_rnn)

    wa_bd, wx_bd = lru_w_a[0], lru_w_x[0]
    pw = pool_w[0]
    pw_bf = pw.astype(BF16)

    pool_block = (2 * d_rnn) // d_pool
    ga_block = (2 * d_rnn + d_pool) // 512
    gb_block = ga_block + d // 512
    g_block = d_rnn // 512

    r_f1, r_f2 = s_f1.shape[0], s_f2.shape[0]
    f1_cut = r_f1 // 4
    f2_cut = (3 * r_f2) // 8
    plan = _join([_ag_plan([s_lu, s_pu, s_o]), _ag_plan([s_f1], pieces=[(0, f1_cut)])])
    (proj, h1), got = _norm_proj(x2d, norm_mix_pre, w_int, carry=plan)
    (g_lu, g_pu, g_o), (g_f1,) = plan.split(got)
    w_lu, w_og = _rows(g_lu), _rows(g_o)
    (y_lru, h, lru_saved), (g_f1,) = _lru_fwd(
        proj, conv_w_full, conv_b, wa_bd, lru_b_a, wx_bd, lru_b_x, lru_lambda,
        carry=_ag_plan([s_f1], pieces=[(f1_cut, r_f1 - f1_cut)], bufs=[g_f1]))
    y_pool, p = _pool_fwd(proj, pw_bf, pool_scale, pool_block)
    (br_a, br_b, mix), (g_f2,) = _branch_mix(
        y_lru, y_pool, w_lu, g_pu, proj, b_gate, ga_block, gb_block,
        carry=_ag_plan([s_f2], pieces=[(0, f2_cut)]))
    (m, x2, h3), _ = _wo_norm(mix, w_og, x2d, norm_mix_post, norm_mlp_pre)
    (rf,), (g_f2,) = _ff1(
        h3, g_f1, carry=_ag_plan([s_f2], pieces=[(f2_cut, r_f2 - f2_cut)], bufs=[g_f2]))
    w_f2 = _rows(g_f2)
    dy, df, dg4, loss_part = _ff2_loss(rf, w_f2, x2, norm_mlp_post, tgt)

    (gw_ff2_32, gw_ff2_16), _ = _wgrad(rf, df, "wgrad_ff2", square_a=True)
    (d_f1,), r1_ff2 = _ff2_bwd(df, w_f2, rf, carry=_rs_sibling_plan([gw_ff2_16]))
    ((own_ff2, send_ff2),) = _rs_sums([gw_ff2_32], r1_ff2, "ff2")
    cut2 = (5 * send_ff2.shape[1]) // 16
    (gw_ff1_32, gw_ff1_16), (r2_ff2,) = _wgrad_cols(
        h3, d_f1, s_f1.shape[1], s_f1.shape[1], "wgrad_ff1",
        carry=_rs_chips_plan([send_ff2], pieces=[(0, cut2)]))
    plan = _join([_rs_chips_plan([send_ff2], pieces=[(cut2, send_ff2.shape[1] - cut2)], bufs=[r2_ff2]),
                  _rs_sibling_plan([gw_ff1_16])])
    (dx2, dm, dg3, dg2), got = _ff1_bwd_norms(d_f1, g_f1, dy, x2, norm_mlp_pre, m, norm_mix_post, carry=plan)
    (r2_ff2,), r1_ff1 = plan.split(got)
    ((own_ff1, send_ff1),) = _rs_sums([gw_ff1_32], r1_ff1, "ff1")
    own_ff1, send_ff1 = own_ff1.reshape((1,) + s_f1.shape), send_ff1.reshape((3,) + s_f1.shape)
    cut = send_ff1.shape[1] // 4
    (gw_o_32, gw_o_16), _ = _wgrad(mix, dm, "wgrad_o")
    (d_br_a, d_br_b, p_ga, p_gb, dbg_a, dbg_b), (r2_ff1,) = _wo_bwd_mix(
        dm, w_og, br_a, br_b, proj, b_gate, ga_block, gb_block,
        carry=_rs_chips_plan([send_ff1], pieces=[(0, cut)]))
    (gw_lu_32, gw_lu_16), _ = _wgrad(y_lru, d_br_a, "wgrad_lru_up")
    (gw_pu_32, gw_pu_16), _ = _wgrad_cols(y_pool, d_br_b, s_pu.shape[1], d, "wgrad_pool_up")
    (dh, p_g), r1_mid = _lru_up_bwd(
        d_br_a, w_lu, proj, h, g_block,
        carry=_rs_sibling_plan([gw_o_16, gw_lu_16, gw_pu_16]))
    mid = _rs_sums([gw_o_32, gw_lu_32, gw_pu_32], r1_mid, "mid")
    plan = _join([_rs_chips_plan([send_ff1], pieces=[(cut, send_ff1.shape[1] - cut)], bufs=[r2_ff1]),
                  _rs_chips_plan([mid[0][1]])])
    (p_x, dwa, db_a, dwx, db_x, dlam, dconv_w, dconv_b), got = _lru_bwd(
        dh, h, lru_saved, proj, conv_w_full, wa_bd, wx_bd, lru_lambda, carry=plan)
    (r2_ff1,), (r2_o,) = plan.split(got)
    p_p, dpool_w, dpool_scale = _pool_bwd(d_br_b, g_pu, p, pw, pool_scale)
    parts = [p_x, p_g, p_p, p_ga, p_gb]
    gw_in, (r2_lu, r2_pu) = _wgrad_parts(
        parts, h1, "wgrad_in", carry=_rs_chips_plan([mid[1][1], mid[2][1]]))
    r2_mid = [r2_o, r2_lu, r2_pu]
    tail = _rs_level1([gw_in[0], dpool_w.reshape(N_DEV, -1, POOL_GROUP_DIM), dwa, dwx],
                      [gw_in[1], dpool_w.reshape(N_DEV, -1, POOL_GROUP_DIM), dwa, dwx], "in")
    (grad_x, dg1), r2_tail = _win_bwd_norm(parts, w_int, dx2, x2d, norm_mix_pre,
                                           carry=_rs_chips_plan([s for _, s in tail]))

    def flat2(a):
        return a.reshape(a.shape[0], -1, a.shape[-1])

    fin_small, _ = _finals([
        (flat2(tail[1][0]), flat2(r2_tail[1])), (flat2(tail[2][0]), flat2(r2_tail[2])),
        (flat2(tail[3][0]), flat2(r2_tail[3])),
    ], "rs_finals_small")

    def pad_row(a):
        return jnp.pad(a, ((0, 0), (0, d - a.shape[1])))

    vecs = jnp.concatenate([dg1, dg2, dg3, dg4, dbg_a, dbg_b, dconv_b, db_a, db_x, dlam,
                            pad_row(dpool_scale), dconv_w, pad_row(loss_part)], axis=0)
    assert vecs.shape[0] == N_VEC_ROWS
    vec_parts, g_pool, g_wa, g_wx = _run_plan(_ag_plan([vecs] + fin_small), "ag_tail")

    big_names = ["w_in", "w_lru_up", "w_pool_up", "w_o", "w_ff1", "w_ff2"]
    big_w = [w_in[0].T, w_lru_up[0], w_pool_up[0], w_o[0], w_ff1[0], w_ff2[0]]
    big_g = [(tail[0][0], r2_tail[0]), (mid[1][0], r2_mid[1]),
             (mid[2][0].reshape((1,) + s_pu.shape), r2_mid[2].reshape((3,) + s_pu.shape)),
             (mid[0][0], r2_mid[0]), (own_ff1, r2_ff1), (own_ff2, r2_ff2)]
    big_m = [m_w_in[0].T, m_w_lru_up[0], m_w_pool_up[0], m_w_o[0], m_w_ff1[0], m_w_ff2[0]]
    big_v = [v_w_in[0].T, v_w_lru_up[0], v_w_pool_up[0], v_w_o[0], v_w_ff1[0], v_w_ff2[0]]
    big_out = _adamw_big(big_w, big_g, big_m, big_v)
    big_out[0] = tuple(o.T for o in big_out[0])

    small = dict(
        norm_mix_pre=(norm_mix_pre, m_norm_mix_pre, v_norm_mix_pre),
        norm_mix_post=(norm_mix_post, m_norm_mix_post, v_norm_mix_post),
        norm_mlp_pre=(norm_mlp_pre, m_norm_mlp_pre, v_norm_mlp_pre),
        norm_mlp_post=(norm_mlp_post, m_norm_mlp_post, v_norm_mlp_post),
        b_gate=(b_gate, m_b_gate, v_b_gate), conv_w=(conv_w, m_conv_w, v_conv_w),
        conv_b=(conv_b, m_conv_b, v_conv_b), lru_w_a=(lru_w_a, m_lru_w_a, v_lru_w_a),
        lru_b_a=(lru_b_a, m_lru_b_a, v_lru_b_a), lru_w_x=(lru_w_x, m_lru_w_x, v_lru_w_x),
        lru_b_x=(lru_b_x, m_lru_b_x, v_lru_b_x), lru_lambda=(lru_lambda, m_lru_lambda, v_lru_lambda),
        pool_w=(pool_w, m_pool_w, v_pool_w), pool_scale=(pool_scale, m_pool_scale, v_pool_scale))
    loss_row, small_out = _adamw_small(
        vec_parts, g_pool.reshape(pool_w.shape), g_wa.reshape(lru_w_a.shape), g_wx.reshape(lru_w_x.shape),
        jnp.reshape(me, (1,)).astype(jnp.int32), small)
    grads = {n: o[0] for n, o in small_out.items()}
    delta = {n: o[1] for n, o in small_out.items()}
    new_m = {n: o[2] for n, o in small_out.items()}
    new_v = {n: o[3] for n, o in small_out.items()}

    for name, (g, dl, nm, nv) in zip(big_names, big_out):
        grads[name], delta[name], new_m[name], new_v[name] = g[None], dl[None], nm[None], nv[None]

    loss = loss_row[0, 0]
    order = ["norm_mix_pre", "norm_mix_post", "norm_mlp_pre", "norm_mlp_post", "w_in", "b_gate", "conv_w",
             "conv_b", "lru_w_a", "lru_b_a", "lru_w_x", "lru_b_x", "lru_lambda", "pool_w", "pool_scale",
             "w_lru_up", "w_pool_up", "w_o", "w_ff1", "w_ff2"]
    return (loss, grad_x[None], *[grads[n] for n in order], *[delta[n] for n in order],
            *[new_m[n] for n in order], *[new_v[n] for n in order])
```

```python
import functools
import math
import operator
import types

import jax
import jax.numpy as jnp
from jax import lax
from jax.experimental import pallas as pl
from jax.experimental.pallas import tpu as pltpu

F32 = jnp.float32
BF16 = jnp.bfloat16
NORM_EPS = 1e-6
LRU_C = 8.0
N_LRU_HEADS = 16
LRU_HEAD_DIM = 64
POOL_WINDOWS = (2, 4, 8, 16)
POOL_GROUP_DIM = 128
ADAM_LR = 0.001
ADAM_B1 = 0.9
ADAM_B2 = 0.999
ADAM_EPS = 1e-08
ADAM_WD = 0.01
ADAM_STEP = 10
N_DEV = 8
V7X_VMEM_LIMIT_BYTES = 56 * 1024 * 1024
LRU_CB = 256
MESH = pl.DeviceIdType.MESH
ANY = pl.BlockSpec(memory_space=pl.ANY)


def _tile(n, pref):
    t = min(n, pref)
    assert n % t == 0, (n, pref)
    return t


def _dot_nn(a, b):
    return lax.dot_general(a, b, (((1,), (0,)), ((), ())), preferred_element_type=F32)


def _dot_nt(a, b):
    return lax.dot_general(a, b, (((1,), (1,)), ((), ())), preferred_element_type=F32)


def _dot_tn(a, b):
    return lax.dot_general(a, b, (((0,), (0,)), ((), ())), preferred_element_type=F32)


def _row_chunks(n_rows, fn, chunk=256):
    chunk = min(chunk, n_rows)
    assert n_rows % chunk == 0

    def step(r, carry):
        fn(pl.ds(pl.multiple_of(r * chunk, chunk), chunk))
        return carry

    lax.fori_loop(0, n_rows // chunk, step, 0)


def _sig(x):
    return 1.0 / (1.0 + jnp.exp(-x))


def _rms_hat(x):
    r = lax.rsqrt(jnp.mean(x * x, axis=-1, keepdims=True) + NORM_EPS)
    return x * r, r


def _rms_bwd(dn, xhat, r, g):
    q = dn * g
    dx = r * (q - xhat * jnp.mean(q * xhat, axis=-1, keepdims=True))
    dg = jnp.sum(dn * xhat, axis=0, keepdims=True)
    return dx, dg


_GELU_K = math.sqrt(2.0 / math.pi)
_GELU_C = 0.044715


def _gelu_and_grad(g):
    t = jnp.tanh(_GELU_K * (g + _GELU_C * g * g * g))
    val = 0.5 * g * (1.0 + t)
    grad = 0.5 * (1.0 + t) + 0.5 * g * (1.0 - t * t) * (_GELU_K * (1.0 + 3.0 * _GELU_C * g * g))
    return val, grad


def _softplus_neg(lam):
    z = -lam
    e = jnp.exp(-jnp.abs(z))
    u = 1.0 + e
    d = u - 1.0
    l1p = jnp.where(d == 0.0, e, jnp.log(u) * (e / jnp.where(d == 0.0, 1.0, d)))
    return jnp.maximum(z, 0.0) + l1p


def _lru_gates(xc, wa, ba, wx, bx, lam):
    xcb = xc.astype(BF16)
    r = _sig(_dot_nn(xcb, wa) + ba)
    i = _sig(_dot_nn(xcb, wx) + bx)
    sp = _softplus_neg(lam)
    log_a = (-LRU_C) * r * sp
    a = jnp.exp(log_a)
    mult = jnp.sqrt(-jnp.tanh(log_a) * (1.0 + a * a))
    return xcb, r, i, sp, log_a, a, mult


def _place():
    return lax.axis_index("x"), lax.axis_index("y"), lax.axis_index("c")


def _ag_plan(shards, pieces=None, bufs=None):
    na = len(shards)
    n_kinds = 7

    def parts(ins, outs, sems):
        send_sems, recv_sems, local_sems = sems
        x, y, c = _place()
        me, sibling = (x, y, c), (x, y, 1 - c)
        x_nb, y_nb, diag = (1 - x, y), (x, 1 - y), (1 - x, 1 - y)
        relay_src = (c * (1 - x) + (1 - c) * x, c * y + (1 - c) * (1 - y))
        relay_dst = (c * x + (1 - c) * (1 - x), c * (1 - y) + (1 - c) * y)

        def own(a):
            return ins[a] if pieces is None else ins[a].at[pl.ds(*pieces[a])]

        def slot(a, px, py, pc):
            idx = 4 * px + 2 * py + pc
            return outs[a].at[idx] if pieces is None else outs[a].at[idx, pl.ds(*pieces[a])]

        def copy(a, k, block, to, src=None):
            return pltpu.make_async_remote_copy(
                src_ref=slot(a, *block) if src is None else src,
                dst_ref=slot(a, *block),
                send_sem=send_sems.at[a * n_kinds + k],
                recv_sem=recv_sems.at[a * n_kinds + k],
                device_id=to,
                device_id_type=MESH,
            )

        mine = [pltpu.make_async_copy(own(a), slot(a, *me), local_sems.at[a]) for a in range(na)]
        first, second, third = [], [], []
        for a in range(na):
            first += [copy(a, 0, me, sibling, src=own(a)), copy(a, 1, me, (*x_nb, c), src=own(a)),
                      copy(a, 2, me, (*y_nb, c), src=own(a))]
            second += [copy(a, 3, (*relay_src, c), (*relay_dst, c)), copy(a, 4, (*x_nb, c), sibling),
                       copy(a, 5, (*y_nb, c), sibling)]
            third.append(copy(a, 6, (*diag, c), sibling))
        return sibling, c, x_nb, y_nb, diag, copy, mine, first, second, third

    def start(ins, outs, sems):
        _, _, _, _, _, _, mine, first, _, _ = parts(ins, outs, sems)
        for cp in mine + first:
            cp.start()

    def middle(ins, outs, sems):
        _, c, x_nb, y_nb, _, copy, _, _, second, _ = parts(ins, outs, sems)
        for a in range(na):
            copy(a, 1, (*x_nb, c), (*x_nb, c)).wait_recv()
            copy(a, 2, (*y_nb, c), (*y_nb, c)).wait_recv()
        for cp in second:
            cp.start()

    def finish(ins, outs, sems):
        sibling, c, x_nb, y_nb, diag, copy, mine, first, second, third = parts(ins, outs, sems)
        for a in range(na):
            copy(a, 3, (*diag, c), (*diag, c)).wait_recv()
            third[a].start()
        for a in range(na):
            copy(a, 0, sibling, sibling).wait_recv()
            copy(a, 4, (*x_nb, 1 - c), sibling).wait_recv()
            copy(a, 5, (*y_nb, 1 - c), sibling).wait_recv()
            copy(a, 6, (*diag, 1 - c), sibling).wait_recv()
        for cp in first + second + third:
            cp.wait_send()
        for cp in mine:
            cp.wait()

    return types.SimpleNamespace(
        ins=list(shards) + list(bufs or []),
        out_shapes=[jax.ShapeDtypeStruct((N_DEV,) + s.shape, s.dtype) for s in shards],
        sems=[pltpu.SemaphoreType.DMA((n_kinds * na,)), pltpu.SemaphoreType.DMA((n_kinds * na,)),
              pltpu.SemaphoreType.DMA((na,))],
        aliases=[(na + a, a) for a in range(na)] if bufs else [],
        peers=frozenset({"sibling", "neighbours"}), start=start, middle=middle, finish=finish)


def _rs_sibling_plan(fulls):
    na = len(fulls)
    rs = [f.shape[0] // N_DEV for f in fulls]

    def copies(ins, outs, sems):
        send_sems, recv_sems = sems
        x, y, c = _place()
        out = []
        for a in range(na):
            for q in range(4):
                shard = 2 * q + (1 - c)
                out.append(pltpu.make_async_remote_copy(
                    src_ref=ins[a].at[pl.ds(shard * rs[a], rs[a])],
                    dst_ref=outs[a].at[q],
                    send_sem=send_sems.at[a * 4 + q],
                    recv_sem=recv_sems.at[a * 4 + q],
                    device_id=(x, y, 1 - c),
                    device_id_type=MESH,
                ))
        return out

    def start(ins, outs, sems):
        for cp in copies(ins, outs, sems):
            cp.start()

    def finish(ins, outs, sems):
        for cp in copies(ins, outs, sems):
            cp.wait()

    return types.SimpleNamespace(
        ins=list(fulls),
        out_shapes=[jax.ShapeDtypeStruct((4, r) + f.shape[1:], f.dtype) for r, f in zip(rs, fulls)],
        sems=[pltpu.SemaphoreType.DMA((4 * na,)), pltpu.SemaphoreType.DMA((4 * na,))],
        peers=frozenset({"sibling"}), start=start, finish=finish)


def _rs_chips_plan(sends, pieces=None, bufs=None):
    na = len(sends)

    def copies(ins, outs, sems):
        send_sems, recv_sems = sems
        x, y, c = _place()
        chips = [(1 - x, y), (x, 1 - y), (1 - x, 1 - y)]
        out = []
        for a in range(na):
            for k, chip in enumerate(chips):
                rows = (k,) if pieces is None else (k, pl.ds(*pieces[a]))
                out.append(pltpu.make_async_remote_copy(
                    src_ref=ins[a].at[rows],
                    dst_ref=outs[a].at[rows],
                    send_sem=send_sems.at[a * 3 + k],
                    recv_sem=recv_sems.at[a * 3 + k],
                    device_id=(*chip, c),
                    device_id_type=MESH,
                ))
        return out

    def start(ins, outs, sems):
        for cp in copies(ins, outs, sems):
            cp.start()

    def finish(ins, outs, sems):
        for cp in copies(ins, outs, sems):
            cp.wait()

    return types.SimpleNamespace(
        ins=list(sends) + list(bufs or []),
        out_shapes=[jax.ShapeDtypeStruct(s.shape, s.dtype) for s in sends],
        sems=[pltpu.SemaphoreType.DMA((3 * na,)), pltpu.SemaphoreType.DMA((3 * na,))],
        aliases=[(na + a, a) for a in range(na)] if bufs else [],
        peers=frozenset({"chips"}), start=start, finish=finish)


def _join(plans):
    ins, outs, sems, aliases, offs = [], [], [], [], []
    for p in plans:
        offs.append((len(ins), len(outs), len(sems)))
        aliases += [(len(ins) + ci, len(outs) + co) for ci, co in getattr(p, "aliases", [])]
        ins += p.ins
        outs += p.out_shapes
        sems += p.sems

    def cut(p, off, i, o, s):
        return (i[off[0]:off[0] + len(p.ins)], o[off[1]:off[1] + len(p.out_shapes)],
                s[off[2]:off[2] + len(p.sems)])

    def start(i, o, s):
        for p, off in zip(plans, offs):
            p.start(*cut(p, off, i, o, s))

    def middle(i, o, s):
        for p, off in zip(plans, offs):
            if getattr(p, "middle", None) is not None:
                p.middle(*cut(p, off, i, o, s))

    def finish(i, o, s):
        for p, off in zip(plans, offs):
            p.finish(*cut(p, off, i, o, s))

    def split(results):
        return [list(results[off[1]:off[1] + len(p.out_shapes)]) for p, off in zip(plans, offs)]

    return types.SimpleNamespace(ins=ins, out_shapes=outs, sems=sems, aliases=aliases,
                                 peers=frozenset().union(*[p.peers for p in plans]),
                                 start=start, middle=middle, finish=finish, split=split)


COLLECTIVE_ID = {frozenset({"sibling"}): 0, frozenset({"chips"}): 1, frozenset({"sibling", "chips"}): 2,
                 frozenset({"sibling", "neighbours"}): 3}


def _handshake(peers):
    x, y, c = _place()
    devs = []
    if "sibling" in peers:
        devs.append((x, y, 1 - c))
    if "neighbours" in peers:
        devs += [(1 - x, y, c), (x, 1 - y, c)]
    if "chips" in peers:
        assert "neighbours" not in peers
        devs += [(1 - x, y, c), (x, 1 - y, c), (1 - x, 1 - y, c)]
    barrier = pltpu.get_barrier_semaphore()
    for dev in devs:
        pl.semaphore_signal(barrier, inc=1, device_id=dev, device_id_type=MESH)
    pl.semaphore_wait(barrier, len(devs))


def _in_hbm(args):
    return [pltpu.with_memory_space_constraint(a, pltpu.HBM) for a in args]


def _run_plan(plan, name):
    n_in, n_out = len(plan.ins), len(plan.out_shapes)

    def body(*refs):
        ins, outs, sems = refs[:n_in], refs[n_in:n_in + n_out], refs[n_in + n_out:]
        _handshake(plan.peers)
        plan.start(ins, outs, sems)
        if getattr(plan, "middle", None) is not None:
            plan.middle(ins, outs, sems)
        plan.finish(ins, outs, sems)

    return pl.pallas_call(
        body,
        name=name,
        in_specs=[ANY] * n_in,
        out_specs=[ANY] * n_out,
        out_shape=plan.out_shapes,
        scratch_shapes=plan.sems,
        input_output_aliases=dict(getattr(plan, "aliases", [])),
        compiler_params=pltpu.CompilerParams(collective_id=COLLECTIVE_ID[plan.peers]),
    )(*_in_hbm(plan.ins))


def _call(body, *, name, grid, in_specs, out_specs, out_shape, args, scratch_shapes=(), aliases=None,
          carry=None):
    n_in, n_out, n_scr = len(in_specs), len(out_shape), len(scratch_shapes)
    params = pltpu.CompilerParams(
        dimension_semantics=("arbitrary",) * len(grid), vmem_limit_bytes=V7X_VMEM_LIMIT_BYTES)
    if carry is None:
        outs = pl.pallas_call(
            body, name=name, grid=grid, in_specs=list(in_specs), out_specs=list(out_specs),
            out_shape=list(out_shape), scratch_shapes=list(scratch_shapes),
            input_output_aliases=aliases or {}, compiler_params=params)(*_in_hbm(args))
        return list(outs), []
    c_in, c_out = len(carry.ins), len(carry.out_shapes)

    def full(*refs):
        p = 0
        ins = refs[p:p + n_in]
        p += n_in
        cins = refs[p:p + c_in]
        p += c_in
        outs = refs[p:p + n_out]
        p += n_out
        couts = refs[p:p + c_out]
        p += c_out
        scr = refs[p:p + n_scr]
        csems = refs[p + n_scr:]
        ids = [pl.program_id(a) for a in range(len(grid))]
        first = functools.reduce(operator.and_, [i == 0 for i in ids])
        last = functools.reduce(operator.and_, [i == g - 1 for i, g in zip(ids, grid)])

        @pl.when(first)
        def _():
            _handshake(carry.peers)
            carry.start(cins, couts, csems)

        if getattr(carry, "middle", None) is not None:
            n_steps = math.prod(grid)
            flat = functools.reduce(lambda acc, ig: acc * ig[1] + ig[0], zip(ids, grid), 0)

            @pl.when(flat == (2 * n_steps) // 3)
            def _():
                carry.middle(cins, couts, csems)

        body(*ins, *outs, *scr)

        @pl.when(last)
        def _():
            carry.finish(cins, couts, csems)

    all_aliases = dict(aliases or {})
    all_aliases.update({n_in + ci: n_out + co for ci, co in getattr(carry, "aliases", [])})
    params = pltpu.CompilerParams(
        dimension_semantics=("arbitrary",) * len(grid), vmem_limit_bytes=V7X_VMEM_LIMIT_BYTES,
        collective_id=COLLECTIVE_ID[carry.peers])
    outs = pl.pallas_call(
        full, name=name, grid=grid,
        in_specs=list(in_specs) + [ANY] * c_in,
        out_specs=list(out_specs) + [ANY] * c_out,
        out_shape=list(out_shape) + list(carry.out_shapes),
        scratch_shapes=list(scratch_shapes) + list(carry.sems),
        input_output_aliases=all_aliases, compiler_params=params)(*_in_hbm(args), *_in_hbm(carry.ins))
    return list(outs[:n_out]), list(outs[n_out:])


def _norm_proj(x, g1, w_int, carry=None):
    t, d = x.shape
    n = w_int.shape[0]
    tt, tn = _tile(t, 2048), _tile(n, 512)

    def body(x_ref, g_ref, w_ref, proj_ref, h1_ref, h1_s):
        @pl.when(pl.program_id(1) == 0)
        def _():
            def norm_rows(rows):
                xhat, _ = _rms_hat(x_ref[rows, :])
                h = (xhat * g_ref[...]).astype(BF16)
                h1_s[rows, :] = h
                h1_ref[rows, :] = h

            _row_chunks(tt, norm_rows)

        proj_ref[...] = _dot_nt(h1_s[...], w_ref[...]).astype(BF16)

    return _call(
        body, name="norm_proj", grid=(t // tt, n // tn),
        in_specs=[
            pl.BlockSpec((tt, d), lambda i, j: (i, 0)),
            pl.BlockSpec((1, d), lambda i, j: (0, 0)),
            pl.BlockSpec((tn, d), lambda i, j: (j, 0)),
        ],
        out_specs=[
            pl.BlockSpec((tt, tn), lambda i, j: (i, j)),
            pl.BlockSpec((tt, d), lambda i, j: (i, 0)),
        ],
        out_shape=[jax.ShapeDtypeStruct((t, n), BF16), jax.ShapeDtypeStruct((t, d), BF16)],
        scratch_shapes=[pltpu.VMEM((tt, d), BF16)],
        args=(x, g1, w_int), carry=carry)


def _scan_rows(av, bv, reverse):
    tc = av.shape[0]
    row = lax.broadcasted_iota(jnp.int32, av.shape, 0)
    s = 1
    while s < tc:
        if s < 8:
            keep = (row < tc - s) if reverse else (row >= s)
            shift = (tc - s) if reverse else s
            a_sh = jnp.where(keep, pltpu.roll(av, shift, 0), 1.0)
            b_sh = jnp.where(keep, pltpu.roll(bv, shift, 0), 0.0)
            bv = av * b_sh + bv
            av = av * a_sh
        elif reverse:
            bv = jnp.concatenate([av[:tc - s] * bv[s:] + bv[:tc - s], bv[tc - s:]], axis=0)
            av = jnp.concatenate([av[:tc - s] * av[s:], av[tc - s:]], axis=0)
        else:
            bv = jnp.concatenate([bv[:s], av[s:] * bv[:tc - s] + bv[s:]], axis=0)
            av = jnp.concatenate([av[:s], av[s:] * av[:tc - s]], axis=0)
        s *= 2
    return av, bv


N_LRU_SAVED = 5


def _fill_block_diag(w_ref, bd_ref):
    bd_ref[...] = jnp.zeros_like(bd_ref)
    hd = LRU_HEAD_DIM
    for k in range(w_ref.shape[0]):
        bd_ref[k * hd:(k + 1) * hd, k * hd:(k + 1) * hd] = w_ref[k].astype(BF16)


def _lru_fwd(proj, conv_w, conv_b, w_a, b_a, w_x, b_x, lam, carry=None):
    t = proj.shape[0]
    dr = conv_b.shape[1]
    cb = LRU_CB
    tc = _tile(t, 256)
    ncb, ntc = dr // cb, t // tc

    def body(xp_ref, g_ref, cw_ref, cb_ref, wa_ref, ba_ref, wx_ref, bx_ref, lam_ref,
             y_ref, h_ref, saved_ref, prevx_s, hlast_s, wa_s, wx_s):
        c = pl.program_id(1)

        @pl.when(c == 0)
        def _():
            prevx_s[...] = jnp.zeros_like(prevx_s)
            hlast_s[...] = jnp.zeros_like(hlast_s)
            _fill_block_diag(wa_ref, wa_s)
            _fill_block_diag(wx_ref, wx_s)

        x = xp_ref[...].astype(F32)
        prev = prevx_s[...]
        row = lax.broadcasted_iota(jnp.int32, x.shape, 0)

        def sh(j):
            return jnp.where(row >= j, pltpu.roll(x, j, 0), pltpu.roll(prev, j, 0))

        xc = (cb_ref[...] + cw_ref[0:1, :] * sh(3) + cw_ref[1:2, :] * sh(2)
              + cw_ref[2:3, :] * sh(1) + cw_ref[3:4, :] * x)
        prevx_s[...] = x
        _, r, i, _, _, a, mult = _lru_gates(xc, wa_s[...], ba_ref[...], wx_s[...], bx_ref[...],
                                            lam_ref[...])
        for k, val in enumerate((xc, r, i, a, mult)):
            saved_ref[:, k * cb:(k + 1) * cb] = val
        av, bv = _scan_rows(a, mult * (i * xc), reverse=False)
        h = av * hlast_s[...] + bv
        h_ref[...] = h
        hlast_s[...] = h_ref[tc - 1:tc, :]
        gel, _ = _gelu_and_grad(g_ref[...].astype(F32))
        y_ref[...] = (h * gel).astype(BF16)

    vec = pl.BlockSpec((1, cb), lambda j, c: (0, j))
    blk = pl.BlockSpec((tc, cb), lambda j, c: (c, j))
    mat = pl.BlockSpec((cb // LRU_HEAD_DIM, LRU_HEAD_DIM, LRU_HEAD_DIM), lambda j, c: (j, 0, 0))
    return _call(
        body, name="lru_fwd", grid=(ncb, ntc),
        in_specs=[
            blk,
            pl.BlockSpec((tc, cb), lambda j, c: (c, ncb + j)),
            pl.BlockSpec((4, cb), lambda j, c: (0, j)),
            vec, mat, vec, mat, vec, vec,
        ],
        out_specs=[blk, blk, pl.BlockSpec((tc, N_LRU_SAVED * cb), lambda j, c: (c, j))],
        out_shape=[jax.ShapeDtypeStruct((t, dr), BF16), jax.ShapeDtypeStruct((t, dr), F32),
                   jax.ShapeDtypeStruct((t, N_LRU_SAVED * dr), F32)],
        scratch_shapes=[pltpu.VMEM((tc, cb), F32), pltpu.VMEM((1, cb), F32),
                        pltpu.VMEM((cb, cb), BF16), pltpu.VMEM((cb, cb), BF16)],
        args=(proj, proj, conv_w, conv_b, w_a, b_a, w_x, b_x, lam), carry=carry)


def _pool_select(col, vals):
    out = vals[3]
    for g in (2, 1, 0):
        out = jnp.where(col < (g + 1) * POOL_GROUP_DIM, vals[g], out)
    return out


def _pool_fwd(proj, pool_w, pool_scale, col_block):
    t = proj.shape[0]
    dp = pool_scale.shape[1]
    tc = _tile(t, 256)
    ntc = t // tc

    def body(x_ref, w_ref, sc_ref, y_ref, p_ref, px, p2, p4, p8):
        c = pl.program_id(0)

        @pl.when(c == 0)
        def _():
            for s in (px, p2, p4, p8):
                s[...] = jnp.zeros_like(s)

        x = x_ref[...].astype(F32)
        row = lax.broadcasted_iota(jnp.int32, x.shape, 0)
        col = lax.broadcasted_iota(jnp.int32, x.shape, 1)

        def sh(v, pv, j):
            return jnp.where(row >= j, pltpu.roll(v, j, 0), pltpu.roll(pv[...], j, 0))

        s2 = x + sh(x, px, 1)
        s4 = s2 + sh(s2, p2, 2)
        s8 = s4 + sh(s4, p4, 4)
        s16 = s8 + sh(s8, p8, 8)
        px[...] = x
        p2[...] = s2
        p4[...] = s4
        p8[...] = s8
        wsum = _pool_select(col, (s2, s4, s8, s16))
        win = _pool_select(col, POOL_WINDOWS)
        cnt = jnp.minimum(c * tc + row + 1, win).astype(F32)
        p = wsum / cnt - x
        pb = p.astype(BF16)
        p_ref[...] = pb
        for g in range(len(POOL_WINDOWS)):
            sl = slice(g * POOL_GROUP_DIM, (g + 1) * POOL_GROUP_DIM)
            yg = _dot_nn(pb[:, sl], w_ref[g]) * sc_ref[:, sl]
            y_ref[:, sl] = yg.astype(BF16)

    return _call(
        body, name="pool_fwd", grid=(ntc,),
        in_specs=[
            pl.BlockSpec((tc, dp), lambda c: (c, col_block)),
            pl.BlockSpec(pool_w.shape, lambda c: (0, 0, 0)),
            pl.BlockSpec((1, dp), lambda c: (0, 0)),
        ],
        out_specs=[pl.BlockSpec((tc, dp), lambda c: (c, 0))] * 2,
        out_shape=[jax.ShapeDtypeStruct((t, dp), BF16)] * 2,
        scratch_shapes=[pltpu.VMEM((tc, dp), F32)] * 4,
        args=(proj, pool_w, pool_scale))[0]


def _branch_mix(y_lru, y_pool, w_lru_up, w_pool_upb, proj, b_gate, ga_block, gb_block, carry=None):
    t, d = y_lru.shape
    dp = y_pool.shape[1]
    bw = w_pool_upb.shape[2]
    tt, tn = _tile(t, 1024), 512
    nj = d // tn

    def body(yl_ref, yp_ref, wl_ref, wp_ref, ga_ref, gb_ref, ba_ref, bb_ref, bra_ref, brb_ref, mix_ref):
        br_a = _dot_nn(yl_ref[...], wl_ref[...])
        wp = jnp.concatenate([wp_ref[b] for b in range(tn // bw)], axis=1)
        br_b = _dot_nn(yp_ref[...], wp)
        bra_ref[...] = br_a.astype(BF16)
        brb_ref[...] = br_b.astype(BF16)
        ga = _sig(ga_ref[...].astype(F32) + ba_ref[...])
        gb = _sig(gb_ref[...].astype(F32) + bb_ref[...])
        mix_ref[...] = (ga * br_a + gb * br_b).astype(BF16)

    out = pl.BlockSpec((tt, tn), lambda j, i: (i, j))
    return _call(
        body, name="branch_mix", grid=(nj, t // tt),
        in_specs=[
            pl.BlockSpec((tt, d), lambda j, i: (i, 0)),
            pl.BlockSpec((tt, dp), lambda j, i: (i, 0)),
            pl.BlockSpec((d, tn), lambda j, i: (0, j)),
            pl.BlockSpec((tn // bw, dp, bw), lambda j, i: (j, 0, 0)),
            pl.BlockSpec((tt, tn), lambda j, i: (i, ga_block + j)),
            pl.BlockSpec((tt, tn), lambda j, i: (i, gb_block + j)),
            pl.BlockSpec((1, tn), lambda j, i: (0, j)),
            pl.BlockSpec((1, tn), lambda j, i: (0, nj + j)),
        ],
        out_specs=[out, out, out],
        out_shape=[jax.ShapeDtypeStruct((t, d), BF16)] * 3,
        args=(y_lru, y_pool, w_lru_up, w_pool_upb, proj, proj, b_gate, b_gate), carry=carry)


def _wo_norm(mix, w_o, x, g2, g3, carry=None):
    t, d = x.shape
    tt = _tile(t, 512)

    def body(mix_ref, w_ref, x_ref, g2_ref, g3_ref, m_ref, x2_ref, h3_ref):
        m = _dot_nn(mix_ref[...], w_ref[...])
        m_ref[...] = m
        mhat, _ = _rms_hat(m)
        x2 = x_ref[...] + mhat * g2_ref[...]
        x2_ref[...] = x2
        xhat, _ = _rms_hat(x2)
        h3_ref[...] = (xhat * g3_ref[...]).astype(BF16)

    row = pl.BlockSpec((tt, d), lambda i: (i, 0))
    vec = pl.BlockSpec((1, d), lambda i: (0, 0))
    return _call(
        body, name="wo_norm", grid=(t // tt,),
        in_specs=[row, pl.BlockSpec((d, d), lambda i: (0, 0)), row, vec, vec],
        out_specs=[row, row, row],
        out_shape=[
            jax.ShapeDtypeStruct((t, d), F32),
            jax.ShapeDtypeStruct((t, d), F32),
            jax.ShapeDtypeStruct((t, d), BF16),
        ],
        args=(mix, w_o, x, g2, g3), carry=carry)


def _ff1(h3, w_ff1b, carry=None):
    t, d = h3.shape
    nb, _, tn = w_ff1b.shape
    tt = _tile(t, 2048)

    def body(h_ref, w_ref, rf_ref):
        rf_ref[...] = jnp.maximum(_dot_nn(h_ref[...], w_ref[...]), 0.0).astype(BF16)

    out = pl.BlockSpec((tt, tn), lambda i, j: (i, j))
    return _call(
        body, name="ff1", grid=(t // tt, nb),
        in_specs=[pl.BlockSpec((tt, d), lambda i, j: (i, 0)), pl.BlockSpec((None, d, tn), lambda i, j: (j, 0, 0))],
        out_specs=[out],
        out_shape=[jax.ShapeDtypeStruct((t, nb * tn), BF16)],
        args=(h3, w_ff1b), carry=carry)


def _ff2_loss(rf, w_ff2, x2, g4, target):
    t, k = rf.shape
    d = x2.shape[1]
    tt, tk = _tile(t, 1024), _tile(k, 1024)
    nk = k // tk

    def body(a_ref, w_ref, x2_ref, g_ref, tg_ref, dy_ref, df_ref, dg_ref, loss_ref, acc):
        i, kk = pl.program_id(0), pl.program_id(1)

        @pl.when(kk == 0)
        def _():
            acc[...] = jnp.zeros_like(acc)

        @pl.when((i == 0) & (kk == 0))
        def _():
            dg_ref[...] = jnp.zeros_like(dg_ref)
            loss_ref[...] = jnp.zeros_like(loss_ref)

        rf_tile = a_ref[...]
        acc[...] += _dot_nn(rf_tile * rf_tile, w_ref[...])

        @pl.when(kk == nk - 1)
        def _():
            def tail(rows):
                fhat, r = _rms_hat(acc[rows, :])
                g = g_ref[...]
                e = x2_ref[rows, :] + fhat * g - tg_ref[rows, :]
                loss_ref[...] += 0.5 * jnp.sum(jnp.mean(e * e, axis=-1, keepdims=True))
                dy = e * (1.0 / d)
                dy_ref[rows, :] = dy.astype(BF16)
                df, dg = _rms_bwd(dy, fhat, r, g)
                df_ref[rows, :] = df.astype(BF16)
                dg_ref[...] += dg

            _row_chunks(tt, tail)

    row = pl.BlockSpec((tt, d), lambda i, kk: (i, 0))
    vec = pl.BlockSpec((1, d), lambda i, kk: (0, 0))
    return _call(
        body, name="ff2_loss", grid=(t // tt, nk),
        in_specs=[
            pl.BlockSpec((tt, tk), lambda i, kk: (i, kk)),
            pl.BlockSpec((tk, d), lambda i, kk: (kk, 0)),
            row, vec, row,
        ],
        out_specs=[row, row, vec, pl.BlockSpec((1, 128), lambda i, kk: (0, 0))],
        out_shape=[
            jax.ShapeDtypeStruct((t, d), BF16),
            jax.ShapeDtypeStruct((t, d), BF16),
            jax.ShapeDtypeStruct((1, d), F32),
            jax.ShapeDtypeStruct((1, 128), F32),
        ],
        scratch_shapes=[pltpu.VMEM((tt, d), F32)],
        args=(rf, w_ff2, x2, g4, target))[0]


def _ff2_bwd(df, w_ff2, rf, carry=None):
    t, d = df.shape
    n = w_ff2.shape[0]
    tt, tn = _tile(t, 2048), _tile(n, 512)

    def body(df_ref, w_ref, rf_ref, out_ref):
        d_act = _dot_nt(df_ref[...], w_ref[...])
        out_ref[...] = (d_act * (2.0 * rf_ref[...].astype(F32))).astype(BF16)

    blk = pl.BlockSpec((tt, tn), lambda i, j: (i, j))
    return _call(
        body, name="ff2_bwd", grid=(t // tt, n // tn),
        in_specs=[pl.BlockSpec((tt, d), lambda i, j: (i, 0)), pl.BlockSpec((tn, d), lambda i, j: (j, 0)), blk],
        out_specs=[blk],
        out_shape=[jax.ShapeDtypeStruct((t, n), BF16)],
        args=(df, w_ff2, rf), carry=carry)


def _wgrad(a, b, name, prev=None, row_off=0, rows=None, carry=None, square_a=False):
    t, m = a.shape
    n = b.shape[1]
    rows = m if rows is None else rows
    tm, tk = _tile(m, 512), _tile(t, 2048)
    nk = t // tk
    assert row_off % tm == 0
    off = row_off // tm

    def body(*refs):
        a_ref, b_ref = refs[0], refs[1]
        o32_ref, o16_ref, acc = refs[-3], refs[-2], refs[-1]
        kk = pl.program_id(1)

        @pl.when(kk == 0)
        def _():
            acc[...] = jnp.zeros_like(acc)

        a_tile = a_ref[...]
        acc[...] += _dot_tn(a_tile * a_tile if square_a else a_tile, b_ref[...])

        @pl.when(kk == nk - 1)
        def _():
            o32_ref[...] = acc[...]
            o16_ref[...] = acc[...].astype(BF16)

    in_specs = [pl.BlockSpec((tk, tm), lambda i, kk: (kk, i)), pl.BlockSpec((tk, n), lambda i, kk: (kk, 0))]
    args = [a, b]
    aliases = {}
    if prev is not None:
        in_specs += [ANY, ANY]
        args += list(prev)
        aliases = {2: 0, 3: 1}
    out = pl.BlockSpec((tm, n), lambda i, kk: (off + i, 0))
    return _call(
        body, name=name, grid=(m // tm, nk),
        in_specs=in_specs, out_specs=[out, out],
        out_shape=[jax.ShapeDtypeStruct((rows, n), F32), jax.ShapeDtypeStruct((rows, n), BF16)],
        scratch_shapes=[pltpu.VMEM((tm, n), F32)],
        aliases=aliases, args=args, carry=carry)


def _wgrad_parts(parts, b, name, carry=None):
    t, n = b.shape
    tm = 512
    bounds = []
    lo = 0
    for part in parts:
        assert part.shape[0] == t and part.shape[1] % tm == 0
        bounds.append((lo, lo + part.shape[1] // tm))
        lo += part.shape[1] // tm
    nm = lo
    np_ = len(parts)

    def body(*refs):
        p_refs, b_ref, o32_ref, o16_ref = refs[:np_], refs[np_], refs[np_ + 1], refs[np_ + 2]
        i = pl.program_id(0)
        for (lo_p, hi_p), p_ref in zip(bounds, p_refs):
            @pl.when((i >= lo_p) & (i < hi_p))
            def _(p_ref=p_ref):
                res = _dot_tn(p_ref[...], b_ref[...])
                o32_ref[...] = res
                o16_ref[...] = res.astype(BF16)

    def part_spec(lo_p, hi_p):
        return pl.BlockSpec((t, tm), lambda i: (0, jnp.clip(i - lo_p, 0, hi_p - lo_p - 1)))

    out = pl.BlockSpec((tm, n), lambda i: (i, 0))
    return _call(
        body, name=name, grid=(nm,),
        in_specs=[part_spec(lo_p, hi_p) for lo_p, hi_p in bounds] + [pl.BlockSpec((t, n), lambda i: (0, 0))],
        out_specs=[out, out],
        out_shape=[jax.ShapeDtypeStruct((nm * tm, n), F32), jax.ShapeDtypeStruct((nm * tm, n), BF16)],
        args=(*parts, b), carry=carry)


def _wgrad_cols(a, b, bw, tn, name, carry=None):
    t, m = a.shape
    n = b.shape[1]
    per_step = tn // bw

    def body(a_ref, b_ref, o32_ref, o16_ref):
        res = _dot_tn(a_ref[...], b_ref[...])
        for blk in range(per_step):
            part = res[:, blk * bw:(blk + 1) * bw]
            o32_ref[blk] = part
            o16_ref[blk] = part.astype(BF16)

    out = pl.BlockSpec((per_step, m, bw), lambda j: (j, 0, 0))
    return _call(
        body, name=name, grid=(n // tn,),
        in_specs=[pl.BlockSpec((t, m), lambda j: (0, 0)), pl.BlockSpec((t, tn), lambda j: (0, j))],
        out_specs=[out, out],
        out_shape=[jax.ShapeDtypeStruct((n // bw, m, bw), F32), jax.ShapeDtypeStruct((n // bw, m, bw), BF16)],
        args=(a, b), carry=carry)


def _ff1_bwd_norms(d_f1, w_ff1b, dy, x2, g3, m, g2, carry=None):
    t, k = d_f1.shape
    d = x2.shape[1]
    bw = w_ff1b.shape[2]
    per_step = 2
    tt, tk = _tile(t, 1024), per_step * bw
    nk = k // tk

    def body(a_ref, w_ref, dy_ref, x2_ref, g3_ref, m_ref, g2_ref, dx2_ref, dm_ref, dg3_ref, dg2_ref, acc):
        i, kk = pl.program_id(0), pl.program_id(1)

        @pl.when(kk == 0)
        def _():
            acc[...] = jnp.zeros_like(acc)

        @pl.when((i == 0) & (kk == 0))
        def _():
            dg3_ref[...] = jnp.zeros_like(dg3_ref)
            dg2_ref[...] = jnp.zeros_like(dg2_ref)

        a_tile = a_ref[...]
        for b in range(per_step):
            acc[...] += _dot_nt(a_tile[:, b * bw:(b + 1) * bw], w_ref[b])

        @pl.when(kk == nk - 1)
        def _():
            def tail(rows):
                xhat, r3 = _rms_hat(x2_ref[rows, :])
                dx, dg3 = _rms_bwd(acc[rows, :], xhat, r3, g3_ref[...])
                dx2 = dy_ref[rows, :].astype(F32) + dx
                dx2_ref[rows, :] = dx2
                dg3_ref[...] += dg3
                mhat, r2 = _rms_hat(m_ref[rows, :])
                dm, dg2 = _rms_bwd(dx2, mhat, r2, g2_ref[...])
                dm_ref[rows, :] = dm.astype(BF16)
                dg2_ref[...] += dg2

            _row_chunks(tt, tail)

    row = pl.BlockSpec((tt, d), lambda i, kk: (i, 0))
    vec = pl.BlockSpec((1, d), lambda i, kk: (0, 0))
    return _call(
        body, name="ff1_bwd_norms", grid=(t // tt, nk),
        in_specs=[
            pl.BlockSpec((tt, tk), lambda i, kk: (i, kk)),
            pl.BlockSpec((per_step, d, bw), lambda i, kk: (kk, 0, 0)),
            row, row, vec, row, vec,
        ],
        out_specs=[row, row, vec, vec],
        out_shape=[
            jax.ShapeDtypeStruct((t, d), F32),
            jax.ShapeDtypeStruct((t, d), BF16),
            jax.ShapeDtypeStruct((1, d), F32),
            jax.ShapeDtypeStruct((1, d), F32),
        ],
        scratch_shapes=[pltpu.VMEM((tt, d), F32)],
        args=(d_f1, w_ff1b, dy, x2, g3, m, g2), carry=carry)


def _wo_bwd_mix(dm, w_o, br_a, br_b, proj, b_gate, ga_block, gb_block, carry=None):
    t, d = dm.shape
    tt, tn = _tile(t, 1024), 512
    nj = d // tn

    def body(dm_ref, w_ref, bra_ref, brb_ref, ga_ref, gb_ref, ba_ref, bb_ref,
             dbra_ref, dbrb_ref, dga_ref, dgb_ref, dba_ref, dbb_ref):
        i = pl.program_id(1)

        @pl.when(i == 0)
        def _():
            dba_ref[...] = jnp.zeros_like(dba_ref)
            dbb_ref[...] = jnp.zeros_like(dbb_ref)

        d_mix = _dot_nt(dm_ref[...], w_ref[...])
        ga = _sig(ga_ref[...].astype(F32) + ba_ref[...])
        gb = _sig(gb_ref[...].astype(F32) + bb_ref[...])
        dbra_ref[...] = (d_mix * ga).astype(BF16)
        dbrb_ref[...] = (d_mix * gb).astype(BF16)
        dga = d_mix * bra_ref[...].astype(F32) * (ga * (1.0 - ga))
        dgb = d_mix * brb_ref[...].astype(F32) * (gb * (1.0 - gb))
        dga_ref[...] = dga.astype(BF16)
        dgb_ref[...] = dgb.astype(BF16)
        dba_ref[...] += jnp.sum(dga, axis=0, keepdims=True)
        dbb_ref[...] += jnp.sum(dgb, axis=0, keepdims=True)

    blk = pl.BlockSpec((tt, tn), lambda j, i: (i, j))
    vec = pl.BlockSpec((1, tn), lambda j, i: (0, j))
    return _call(
        body, name="wo_bwd_mix", grid=(nj, t // tt),
        in_specs=[
            pl.BlockSpec((tt, d), lambda j, i: (i, 0)),
            pl.BlockSpec((tn, d), lambda j, i: (j, 0)),
            blk, blk,
            pl.BlockSpec((tt, tn), lambda j, i: (i, ga_block + j)),
            pl.BlockSpec((tt, tn), lambda j, i: (i, gb_block + j)),
            vec,
            pl.BlockSpec((1, tn), lambda j, i: (0, nj + j)),
        ],
        out_specs=[blk, blk, blk, blk, vec, vec],
        out_shape=[jax.ShapeDtypeStruct((t, d), BF16)] * 4 + [jax.ShapeDtypeStruct((1, d), F32)] * 2,
        args=(dm, w_o, br_a, br_b, proj, proj, b_gate, b_gate), carry=carry)


def _lru_up_bwd(d_br_a, w_lru_up, proj, h, g_block, carry=None):
    t, d = d_br_a.shape
    tt, tn = _tile(t, 1024), 512

    def body(a_ref, w_ref, g_ref, h_ref, dh_ref, dg_ref):
        d_y = _dot_nt(a_ref[...], w_ref[...])
        gel, gel_grad = _gelu_and_grad(g_ref[...].astype(F32))
        dh_ref[...] = d_y * gel
        dg_ref[...] = (d_y * h_ref[...] * gel_grad).astype(BF16)

    blk = pl.BlockSpec((tt, tn), lambda i, j: (i, j))
    return _call(
        body, name="lru_up_bwd", grid=(t // tt, d // tn),
        in_specs=[
            pl.BlockSpec((tt, d), lambda i, j: (i, 0)),
            pl.BlockSpec((tn, d), lambda i, j: (j, 0)),
            pl.BlockSpec((tt, tn), lambda i, j: (i, g_block + j)),
            blk,
        ],
        out_specs=[blk, blk],
        out_shape=[jax.ShapeDtypeStruct((t, d), F32), jax.ShapeDtypeStruct((t, d), BF16)],
        args=(d_br_a, w_lru_up, proj, h), carry=carry)


def _lru_bwd(dh, h, saved, proj, conv_w, w_a, w_x, lam, carry=None):
    t, dr = dh.shape
    cb = LRU_CB
    hd = LRU_HEAD_DIM
    per = cb // hd
    tc = _tile(t, 256)
    ncb, ntc = dr // cb, t // tc

    def body(dh_ref, h_ref, hp_ref, saved_ref, xp_ref, cw_ref, wa_ref, wx_ref,
             lam_ref, dxp_ref, dwa_ref, dba_ref, dwx_ref, dbx_ref, dlam_ref, dcw_ref, dcb_ref,
             nextd_s, anext_s, gnext_s, tmp_s, wa_s, wx_s):
        c = pl.program_id(1)
        rc = ntc - 1 - c

        @pl.when(c == 0)
        def _():
            nextd_s[...] = jnp.zeros_like(nextd_s)
            anext_s[...] = jnp.zeros_like(anext_s)
            gnext_s[...] = jnp.zeros_like(gnext_s)
            for ref in (dwa_ref, dba_ref, dwx_ref, dbx_ref, dlam_ref, dcw_ref, dcb_ref):
                ref[...] = jnp.zeros_like(ref)
            _fill_block_diag(wa_ref, wa_s)
            _fill_block_diag(wx_ref, wx_s)

        xc, r, i, a, mult = [saved_ref[:, k * cb:(k + 1) * cb] for k in range(N_LRU_SAVED)]
        wa, wx, lam = wa_s[...], wx_s[...], lam_ref[...]
        xcb = xc.astype(BF16)
        sp = _softplus_neg(lam)
        row = lax.broadcasted_iota(jnp.int32, xc.shape, 0)
        h = h_ref[...]
        hp = jnp.where(rc == 0, 0.0, hp_ref[...])
        hprev = jnp.where(row >= 1, pltpu.roll(h, 1, 0), pltpu.roll(hp, 1, 0))

        def up(v, nv, j):
            return jnp.where(row < tc - j, pltpu.roll(v, tc - j, 0), nv)

        av, bv = _scan_rows(up(a, anext_s[...], 1), dh_ref[...], reverse=True)
        gt = av * gnext_s[...] + bv
        tmp_s[...] = gt
        gnext_s[...] = tmp_s[0:1, :]
        tmp_s[...] = a
        anext_s[...] = tmp_s[0:1, :]

        da = gt * hprev
        ixc = i * xc
        d_mult = gt * ixc
        d_i = gt * mult * xc
        d_xc = gt * mult * i
        d_log_a = da * a - d_mult * (a * a) / mult
        d_pre_r = (d_log_a * ((-LRU_C) * sp)) * (r * (1.0 - r))
        d_pre_i = d_i * (i * (1.0 - i))
        d_sp = jnp.sum(d_log_a * ((-LRU_C) * r), axis=0, keepdims=True)
        dlam_ref[...] += d_sp * (-1.0 / (1.0 + jnp.exp(lam)))
        dpr = d_pre_r.astype(BF16)
        dpi = d_pre_i.astype(BF16)
        dba_ref[...] += jnp.sum(d_pre_r, axis=0, keepdims=True)
        dbx_ref[...] += jnp.sum(d_pre_i, axis=0, keepdims=True)
        pa = _dot_tn(xcb, dpr)
        px = _dot_tn(xcb, dpi)
        for k in range(per):
            dwa_ref[k] += pa[k * hd:(k + 1) * hd, k * hd:(k + 1) * hd]
            dwx_ref[k] += px[k * hd:(k + 1) * hd, k * hd:(k + 1) * hd]
        d_xc = d_xc + _dot_nt(dpr, wa) + _dot_nt(dpi, wx)

        nxt = nextd_s[...]
        xp = xp_ref[...].astype(F32)
        dxp = cw_ref[3:4, :] * d_xc
        dcw_ref[3:4, :] += jnp.sum(xp * d_xc, axis=0, keepdims=True)
        for j in (1, 2, 3):
            uj = up(d_xc, pltpu.roll(nxt, tc - j, 0), j)
            dxp = dxp + cw_ref[3 - j:4 - j, :] * uj
            dcw_ref[3 - j:4 - j, :] += jnp.sum(xp * uj, axis=0, keepdims=True)
        dcb_ref[...] += jnp.sum(d_xc, axis=0, keepdims=True)
        nextd_s[...] = d_xc
        dxp_ref[...] = dxp.astype(BF16)

    vec = pl.BlockSpec((1, cb), lambda j, c: (0, j))
    blk = pl.BlockSpec((tc, cb), lambda j, c: (ntc - 1 - c, j))
    mat = pl.BlockSpec((per, hd, hd), lambda j, c: (j, 0, 0))
    cwb = pl.BlockSpec((4, cb), lambda j, c: (0, j))
    return _call(
        body, name="lru_bwd", grid=(ncb, ntc),
        in_specs=[
            blk, blk,
            pl.BlockSpec((tc, cb), lambda j, c: (jnp.maximum(ntc - 2 - c, 0), j)),
            pl.BlockSpec((tc, N_LRU_SAVED * cb), lambda j, c: (ntc - 1 - c, j)),
            blk, cwb, mat, mat, vec,
        ],
        out_specs=[blk, mat, vec, mat, vec, vec, cwb, vec],
        out_shape=[
            jax.ShapeDtypeStruct((t, dr), BF16),
            jax.ShapeDtypeStruct(w_a.shape, F32),
            jax.ShapeDtypeStruct((1, dr), F32),
            jax.ShapeDtypeStruct(w_x.shape, F32),
            jax.ShapeDtypeStruct((1, dr), F32),
            jax.ShapeDtypeStruct((1, dr), F32),
            jax.ShapeDtypeStruct((4, dr), F32),
            jax.ShapeDtypeStruct((1, dr), F32),
        ],
        scratch_shapes=[
            pltpu.VMEM((tc, cb), F32),
            pltpu.VMEM((1, cb), F32),
            pltpu.VMEM((1, cb), F32),
            pltpu.VMEM((tc, cb), F32),
            pltpu.VMEM((cb, cb), BF16),
            pltpu.VMEM((cb, cb), BF16),
        ],
        args=(dh, h, h, saved, proj, conv_w, w_a, w_x, lam), carry=carry)


def _pool_bwd(d_br_b, w_pool_upb, p, pool_w, pool_scale):
    t, d = d_br_b.shape
    nwb, dp, _ = w_pool_upb.shape
    tc = _tile(t, 256)
    ntc = t // tc
    ng = len(POOL_WINDOWS)

    def body(db_ref, wu_ref, p_ref, w_ref, sc_ref, dx_ref, dw_ref, dsc_ref, nz, n2, n4, n8, dp_s, dy_s):
        c = pl.program_id(0)
        rc = ntc - 1 - c

        @pl.when(c == 0)
        def _():
            for s in (nz, n2, n4, n8):
                s[...] = jnp.zeros_like(s)
            dw_ref[...] = jnp.zeros_like(dw_ref)
            dsc_ref[...] = jnp.zeros_like(dsc_ref)

        wu = jnp.concatenate([wu_ref[b] for b in range(nwb)], axis=1)
        dy_s[...] = _dot_nt(db_ref[...], wu)
        for g in range(ng):
            sl = slice(g * POOL_GROUP_DIM, (g + 1) * POOL_GROUP_DIM)
            pg = p_ref[:, sl]
            dyg = dy_s[:, sl]
            wg = w_ref[g].astype(BF16)
            q = _dot_nn(pg, wg)
            dsc_ref[:, sl] += jnp.sum(dyg * q, axis=0, keepdims=True)
            dpw = (dyg * sc_ref[:, sl]).astype(BF16)
            dw_ref[g] += _dot_tn(pg, dpw)
            dp_s[:, sl] = _dot_nt(dpw, wg)

        dpv = dp_s[...]
        row = lax.broadcasted_iota(jnp.int32, dpv.shape, 0)
        col = lax.broadcasted_iota(jnp.int32, dpv.shape, 1)
        win = _pool_select(col, POOL_WINDOWS)
        cnt = jnp.minimum(rc * tc + row + 1, win).astype(F32)
        z = dpv / cnt

        def up(v, nv, j):
            return jnp.where(row < tc - j, pltpu.roll(v, tc - j, 0), pltpu.roll(nv[...], tc - j, 0))

        u2 = z + up(z, nz, 1)
        u4 = u2 + up(u2, n2, 2)
        u8 = u4 + up(u4, n4, 4)
        u16 = u8 + up(u8, n8, 8)
        nz[...] = z
        n2[...] = u2
        n4[...] = u4
        n8[...] = u8
        dx_ref[...] = (_pool_select(col, (u2, u4, u8, u16)) - dpv).astype(BF16)

    blk = pl.BlockSpec((tc, dp), lambda c: (ntc - 1 - c, 0))
    full_w = pl.BlockSpec(pool_w.shape, lambda c: (0, 0, 0))
    vec = pl.BlockSpec((1, dp), lambda c: (0, 0))
    return _call(
        body, name="pool_bwd", grid=(ntc,),
        in_specs=[pl.BlockSpec((tc, d), lambda c: (ntc - 1 - c, 0)),
                  pl.BlockSpec(w_pool_upb.shape, lambda c: (0, 0, 0)), blk, full_w, vec],
        out_specs=[blk, full_w, vec],
        out_shape=[
            jax.ShapeDtypeStruct((t, dp), BF16),
            jax.ShapeDtypeStruct(pool_w.shape, F32),
            jax.ShapeDtypeStruct((1, dp), F32),
        ],
        scratch_shapes=[pltpu.VMEM((tc, dp), F32)] * 6,
        args=(d_br_b, w_pool_upb, p, pool_w, pool_scale))[0]


def _win_bwd_norm(parts, w_int, dx2, x, g1, carry=None):
    t, d = x.shape
    tk = 512
    tt = _tile(t, 1024)
    bounds = []
    k0 = 0
    for part in parts:
        assert part.shape[1] % tk == 0
        bounds.append((k0, k0 + part.shape[1] // tk))
        k0 += part.shape[1] // tk
    nk = k0
    assert nk * tk == w_int.shape[0]
    np_ = len(parts)

    def body(*refs):
        p_refs = refs[:np_]
        w_ref, dx2_ref, x_ref, g_ref, gx_ref, dg_ref, acc = refs[np_:]
        i, kk = pl.program_id(0), pl.program_id(1)

        @pl.when(kk == 0)
        def _():
            acc[...] = jnp.zeros_like(acc)

        @pl.when((i == 0) & (kk == 0))
        def _():
            dg_ref[...] = jnp.zeros_like(dg_ref)

        for (lo, hi), p_ref in zip(bounds, p_refs):
            @pl.when((kk >= lo) & (kk < hi))
            def _(p_ref=p_ref):
                acc[...] += _dot_nn(p_ref[...], w_ref[...])

        @pl.when(kk == nk - 1)
        def _():
            def tail(rows):
                xhat, r = _rms_hat(x_ref[rows, :])
                dx, dg = _rms_bwd(acc[rows, :], xhat, r, g_ref[...])
                gx_ref[rows, :] = dx2_ref[rows, :] + dx
                dg_ref[...] += dg

            _row_chunks(tt, tail)

    def part_spec(lo, hi):
        return pl.BlockSpec((tt, tk), lambda i, kk: (i, jnp.clip(kk - lo, 0, hi - lo - 1)))

    row = pl.BlockSpec((tt, d), lambda i, kk: (i, 0))
    vec = pl.BlockSpec((1, d), lambda i, kk: (0, 0))
    return _call(
        body, name="win_bwd_norm", grid=(t // tt, nk),
        in_specs=[part_spec(lo, hi) for lo, hi in bounds]
        + [pl.BlockSpec((tk, d), lambda i, kk: (kk, 0)), row, row, vec],
        out_specs=[row, vec],
        out_shape=[jax.ShapeDtypeStruct((t, d), F32), jax.ShapeDtypeStruct((1, d), F32)],
        scratch_shapes=[pltpu.VMEM((tt, d), F32)],
        args=(*parts, w_int, dx2, x, g1), carry=carry)


def _adam_math(w, g, m, v):
    m = ADAM_B1 * m + (1.0 - ADAM_B1) * g
    v = ADAM_B2 * v + (1.0 - ADAM_B2) * (g * g)
    m_hat = m / (1.0 - ADAM_B1 ** ADAM_STEP)
    v_hat = v / (1.0 - ADAM_B2 ** ADAM_STEP)
    delta = -ADAM_LR * (m_hat / (jnp.sqrt(v_hat) + ADAM_EPS) + ADAM_WD * w)
    return delta, m, v


def _adamw_big(ws, gs, ms, vs):
    n = len(ws)
    nb = 4
    pair = [isinstance(g, tuple) for g in gs]

    def body(*refs):
        p = 0
        ins = []
        for a in range(n):
            k = 5 if pair[a] else 4
            ins.append(refs[p:p + k])
            p += k
        for a in range(n):
            g_out, d_ref, nm_ref, nv_ref = refs[p + 4 * a:p + 4 * a + 4]
            if pair[a]:
                w_ref, own_ref, recv_ref, m_ref, v_ref = ins[a]
                g = own_ref[...]
                for k in range(3):
                    g = g + recv_ref[k].astype(F32)
            else:
                w_ref, g_ref, m_ref, v_ref = ins[a]
                g = g_ref[...]
            dl, m, v = _adam_math(w_ref[...], g, m_ref[...], v_ref[...])
            g_out[...] = g
            d_ref[...] = dl
            nm_ref[...] = m
            nv_ref[...] = v

    in_specs, out_specs, out_shape, args = [], [], [], []
    for a, (w, g, m, v) in enumerate(zip(ws, gs, ms, vs)):
        rows, cols = w.shape
        blk = pl.BlockSpec((rows // nb, cols), lambda i: (i, 0))
        if pair[a]:
            in_specs += [blk, pl.BlockSpec((None, rows // nb, cols), lambda i: (0, i, 0)),
                         pl.BlockSpec((3, rows // nb, cols), lambda i: (0, i, 0)), blk, blk]
            args += [w, g[0], g[1], m, v]
        else:
            in_specs += [blk] * 4
            args += [w, g, m, v]
        out_specs += [blk] * 4
        out_shape += [jax.ShapeDtypeStruct(w.shape, F32)] * 4
    outs = _call(body, name="adamw_big", grid=(nb,), in_specs=in_specs, out_specs=out_specs,
                 out_shape=out_shape, args=args)[0]
    return [tuple(outs[4 * a:4 * a + 4]) for a in range(n)]


SMALL_ORDER = ("norm_mix_pre", "norm_mix_post", "norm_mlp_pre", "norm_mlp_post", "b_gate", "conv_w", "conv_b",
               "lru_w_a", "lru_b_a", "lru_w_x", "lru_b_x", "lru_lambda", "pool_w", "pool_scale")
VEC_ROW = dict(norm_mix_pre=0, norm_mix_post=1, norm_mlp_pre=2, norm_mlp_post=3, conv_b=6, lru_b_a=7,
               lru_b_x=8, lru_lambda=9)
ROW_B_GATE, ROW_POOL_SCALE, ROW_CONV_W, ROW_LOSS, N_VEC_ROWS = 4, 10, 11, 15, 16


def _adamw_small(vec_parts, g_pool, g_wa, g_wx, me, params):
    d = vec_parts.shape[2]
    names = SMALL_ORDER
    n = len(names)
    cw_cols = params["conv_w"][0].shape[2]

    def body(me_ref, vec_ref, vecc_ref, gp_ref, gwa_ref, gwx_ref, *refs):
        wmv = refs[:3 * n]
        loss_ref = refs[3 * n]
        outs = refs[3 * n + 1:3 * n + 1 + 4 * n]
        vs, vsc = refs[3 * n + 1 + 4 * n:]
        acc, accc = vec_ref[0], vecc_ref[0]
        for k in range(1, N_DEV):
            acc = acc + vec_ref[k]
            accc = accc + vecc_ref[k]
        vs[...] = acc
        vsc[...] = accc
        loss_ref[...] = vs[ROW_LOSS:ROW_LOSS + 1, 0:128]

        def upd(a, g, idx):
            w_ref, m_ref, v_ref = wmv[3 * a:3 * a + 3]
            g_ref, d_ref, nm_ref, nv_ref = outs[4 * a:4 * a + 4]
            dl, m, v = _adam_math(w_ref[idx], g, m_ref[idx], v_ref[idx])
            g_ref[idx] = g
            d_ref[idx] = dl
            nm_ref[idx] = m
            nv_ref[idx] = v

        for a, name in enumerate(names):
            if name in VEC_ROW:
                r = VEC_ROW[name]
                upd(a, vs[r:r + 1, :], (slice(None), slice(None)))
            elif name == "b_gate":
                for half in range(2):
                    r = ROW_B_GATE + half
                    upd(a, vs[r:r + 1, :], (slice(None), slice(half * d, (half + 1) * d)))
            elif name == "pool_scale":
                width = params[name][0].shape[1]
                upd(a, vs[ROW_POOL_SCALE:ROW_POOL_SCALE + 1, 0:width], (slice(None), slice(None)))
            elif name == "conv_w":
                upd(a, vsc[ROW_CONV_W:ROW_CONV_W + 4, :], (0,))
            elif name == "pool_w":
                upd(a, gp_ref[...], (Ellipsis,))
            elif name == "lru_w_a":
                upd(a, gwa_ref[...], (Ellipsis,))
            elif name == "lru_w_x":
                upd(a, gwx_ref[...], (Ellipsis,))
            else:
                raise ValueError(name)

    def whole(shape):
        nd = len(shape)
        return pl.BlockSpec(tuple(shape), lambda i, me_ref: (0,) * nd)

    in_specs = [
        whole(vec_parts.shape),
        pl.BlockSpec((N_DEV, N_VEC_ROWS, cw_cols), lambda i, me_ref: (0, 0, me_ref[0])),
        whole(g_pool.shape), whole(g_wa.shape), whole(g_wx.shape),
    ]
    args = [vec_parts, vec_parts, g_pool, g_wa, g_wx]
    out_specs = [whole((1, 128))]
    out_shape = [jax.ShapeDtypeStruct((1, 128), F32)]
    for name in names:
        for arr in params[name]:
            in_specs.append(whole(arr.shape))
            args.append(arr)
        shp = params[name][0].shape
        out_specs += [whole(shp)] * 4
        out_shape += [jax.ShapeDtypeStruct(shp, F32)] * 4
    grid_spec = pltpu.PrefetchScalarGridSpec(
        num_scalar_prefetch=1, grid=(1,), in_specs=in_specs, out_specs=out_specs,
        scratch_shapes=[pltpu.VMEM((N_VEC_ROWS, d), F32), pltpu.VMEM((N_VEC_ROWS, cw_cols), F32)])
    outs = pl.pallas_call(
        body, name="adamw_small", grid_spec=grid_spec, out_shape=out_shape,
        compiler_params=pltpu.CompilerParams(
            dimension_semantics=("arbitrary",), vmem_limit_bytes=V7X_VMEM_LIMIT_BYTES),
    )(me, *_in_hbm(args))
    return outs[0], {name: tuple(outs[1 + 4 * a:5 + 4 * a]) for a, name in enumerate(names)}


def _rs_sum(fulls, recvs, shard_ids, slot_ids, name):
    n = len(fulls)

    def body(sh_ref, sl_ref, *refs):
        s = pl.program_id(0)
        for a in range(n):
            full_ref, recv_ref = refs[2 * a], refs[2 * a + 1]
            own_ref, send_ref = refs[2 * n + 2 * a], refs[2 * n + 2 * a + 1]
            v = full_ref[...] + recv_ref[...].astype(F32)

            @pl.when(s == 0)
            def _(own_ref=own_ref, v=v):
                own_ref[...] = v

            @pl.when(s > 0)
            def _(send_ref=send_ref, v=v):
                send_ref[...] = v.astype(send_ref.dtype)

    in_specs, out_specs, out_shape, args = [], [], [], []
    for full, recv in zip(fulls, recvs):
        r, rest = recv.shape[1], tuple(recv.shape[2:])
        zeros = (0,) * len(rest)
        in_specs += [
            pl.BlockSpec((r,) + rest, lambda s, sh, sl, zeros=zeros: (sh[s],) + zeros),
            pl.BlockSpec((None, r) + rest, lambda s, sh, sl, zeros=zeros: (sl[s], 0) + zeros),
        ]
        out_specs += [
            pl.BlockSpec((None, r) + rest, lambda s, sh, sl, zeros=zeros: (0, 0) + zeros),
            pl.BlockSpec((None, r) + rest, lambda s, sh, sl, zeros=zeros: (jnp.maximum(s - 1, 0), 0) + zeros),
        ]
        out_shape += [jax.ShapeDtypeStruct((1, r) + rest, F32), jax.ShapeDtypeStruct((3, r) + rest, recv.dtype)]
        args += [full, recv]
    grid_spec = pltpu.PrefetchScalarGridSpec(
        num_scalar_prefetch=2, grid=(4,), in_specs=in_specs, out_specs=out_specs)
    outs = pl.pallas_call(
        body,
        name=name,
        grid_spec=grid_spec,
        out_shape=out_shape,
        compiler_params=pltpu.CompilerParams(
            dimension_semantics=("arbitrary",), vmem_limit_bytes=V7X_VMEM_LIMIT_BYTES),
    )(shard_ids, slot_ids, *_in_hbm(args))
    return [(outs[2 * a], outs[2 * a + 1]) for a in range(n)]


def _finals(pairs, name, carry=None):
    nb = 4
    n = len(pairs)

    def body(*refs):
        for a in range(n):
            own_ref, recv_ref = refs[2 * a], refs[2 * a + 1]
            acc = own_ref[...]
            for k in range(3):
                acc = acc + recv_ref[k].astype(F32)
            refs[2 * n + a][...] = acc

    in_specs, out_specs, out_shape, args = [], [], [], []
    for own, recv in pairs:
        _, rows, cols = own.shape
        in_specs += [pl.BlockSpec((None, rows // nb, cols), lambda i: (0, i, 0)),
                     pl.BlockSpec((3, rows // nb, cols), lambda i: (0, i, 0))]
        args += [own, recv]
        out_specs.append(pl.BlockSpec((rows // nb, cols), lambda i: (i, 0)))
        out_shape.append(jax.ShapeDtypeStruct((rows, cols), F32))
    return _call(body, name=name, grid=(nb,), in_specs=in_specs, out_specs=out_specs,
                 out_shape=out_shape, args=args, carry=carry)


def _rs_sums(fulls_f32, recv1, tag):
    x, y, c = _place()
    qs = jnp.stack([2 * x + y, 2 * (1 - x) + y, 2 * x + (1 - y), 2 * (1 - x) + (1 - y)]).astype(jnp.int32)
    shard_ids = 2 * qs + c
    return _rs_sum(fulls_f32, recv1, shard_ids, qs, "rs_sum_" + tag)


def _rs_level1(fulls_f32, fulls_send, tag):
    recv1 = _run_plan(_rs_sibling_plan(fulls_send), "rs_sibling_" + tag)
    return _rs_sums(fulls_f32, recv1, tag)


def _rows(g):
    return g.reshape(g.shape[0] * g.shape[1], g.shape[2])


def kernel(x, norm_mix_pre, norm_mix_post, norm_mlp_pre, norm_mlp_post, w_in, b_gate, conv_w, conv_b, lru_w_a, lru_b_a, lru_w_x, lru_b_x, lru_lambda, pool_w, pool_scale, w_lru_up, w_pool_up, w_o, w_ff1, w_ff2, loss_target, m_norm_mix_pre, m_norm_mix_post, m_norm_mlp_pre, m_norm_mlp_post, m_w_in, m_b_gate, m_conv_w, m_conv_b, m_lru_w_a, m_lru_b_a, m_lru_w_x, m_lru_b_x, m_lru_lambda, m_pool_w, m_pool_scale, m_w_lru_up, m_w_pool_up, m_w_o, m_w_ff1, m_w_ff2, v_norm_mix_pre, v_norm_mix_post, v_norm_mlp_pre, v_norm_mlp_post, v_w_in, v_b_gate, v_conv_w, v_conv_b, v_lru_w_a, v_lru_b_a, v_lru_w_x, v_lru_b_x, v_lru_lambda, v_pool_w, v_pool_scale, v_w_lru_up, v_w_pool_up, v_w_o, v_w_ff1, v_w_ff2):
    t, d = x.shape[1], x.shape[2]
    d_rnn = conv_b.shape[1]
    d_pool = pool_scale.shape[1]
    per = LRU_CB // LRU_HEAD_DIM
    xi, yi, ci = _place()
    me = 4 * xi + 2 * yi + ci

    x2d = x[0]
    tgt = loss_target[0]

    s_in = w_in[0].T.astype(BF16)
    s_lu = w_lru_up[0].astype(BF16)
    s_pu = w_pool_up[0].astype(BF16)
    s_o = w_o[0].astype(BF16)
    s_f1 = w_ff1[0].astype(BF16)
    s_f2 = w_ff2[0].astype(BF16)
    s_cw = jnp.pad(conv_w[0], ((0, 4), (0, 0)))

    g_in, g_cw = _run_plan(_ag_plan([s_in, s_cw]), "ag_w_in")
    w_int = _rows(g_in)
    conv_w_full = jnp.transpose(g_cw[:, :4, :], (1, 0, 2)).reshape(4, d_rnn)

    wa_bd, wx_bd = lru_w_a[0], lru_w_x[0]
    pw = pool_w[0]
    pw_bf = pw.astype(BF16)

    pool_block = (2 * d_rnn) // d_pool
    ga_block = (2 * d_rnn + d_pool) // 512
    gb_block = ga_block + d // 512
    g_block = d_rnn // 512

    r_f1, r_f2 = s_f1.shape[0], s_f2.shape[0]
    f1_cut = r_f1 // 4
    f2_cut = (3 * r_f2) // 8
    plan = _join([_ag_plan([s_lu, s_pu]), _ag_plan([s_f1], pieces=[(0, f1_cut)])])
    (proj, h1), got = _norm_proj(x2d, norm_mix_pre, w_int, carry=plan)
    (g_lu, g_pu), (g_f1,) = plan.split(got)
    plan = _join([_ag_plan([s_f1], pieces=[(f1_cut, r_f1 - f1_cut)], bufs=[g_f1]), _ag_plan([s_o])])
    (y_lru, h, lru_saved), got = _lru_fwd(
        proj, conv_w_full, conv_b, wa_bd, lru_b_a, wx_bd, lru_b_x, lru_lambda, carry=plan)
    (g_f1,), (g_o,) = plan.split(got)
    w_lu, w_og = _rows(g_lu), _rows(g_o)
    y_pool, p = _pool_fwd(proj, pw_bf, pool_scale, pool_block)
    (br_a, br_b, mix), (g_f2,) = _branch_mix(
        y_lru, y_pool, w_lu, g_pu, proj, b_gate, ga_block, gb_block,
        carry=_ag_plan([s_f2], pieces=[(0, f2_cut)]))
    (m, x2, h3), _ = _wo_norm(mix, w_og, x2d, norm_mix_post, norm_mlp_pre)
    (rf,), (g_f2,) = _ff1(
        h3, g_f1, carry=_ag_plan([s_f2], pieces=[(f2_cut, r_f2 - f2_cut)], bufs=[g_f2]))
    w_f2 = _rows(g_f2)
    dy, df, dg4, loss_part = _ff2_loss(rf, w_f2, x2, norm_mlp_post, tgt)

    (gw_ff2_32, gw_ff2_16), _ = _wgrad(rf, df, "wgrad_ff2", square_a=True)
    (d_f1,), r1_ff2 = _ff2_bwd(df, w_f2, rf, carry=_rs_sibling_plan([gw_ff2_16]))
    ((own_ff2, send_ff2),) = _rs_sums([gw_ff2_32], r1_ff2, "ff2")
    cut2 = (5 * send_ff2.shape[1]) // 16
    (gw_ff1_32, gw_ff1_16), (r2_ff2,) = _wgrad_cols(
        h3, d_f1, s_f1.shape[1], s_f1.shape[1], "wgrad_ff1",
        carry=_rs_chips_plan([send_ff2], pieces=[(0, cut2)]))
    plan = _join([_rs_chips_plan([send_ff2], pieces=[(cut2, send_ff2.shape[1] - cut2)], bufs=[r2_ff2]),
                  _rs_sibling_plan([gw_ff1_16])])
    (dx2, dm, dg3, dg2), got = _ff1_bwd_norms(d_f1, g_f1, dy, x2, norm_mlp_pre, m, norm_mix_post, carry=plan)
    (r2_ff2,), r1_ff1 = plan.split(got)
    ((own_ff1, send_ff1),) = _rs_sums([gw_ff1_32], r1_ff1, "ff1")
    own_ff1, send_ff1 = own_ff1.reshape((1,) + s_f1.shape), send_ff1.reshape((3,) + s_f1.shape)
    cut = send_ff1.shape[1] // 4
    (gw_o_32, gw_o_16), _ = _wgrad(mix, dm, "wgrad_o")
    (d_br_a, d_br_b, p_ga, p_gb, dbg_a, dbg_b), (r2_ff1,) = _wo_bwd_mix(
        dm, w_og, br_a, br_b, proj, b_gate, ga_block, gb_block,
        carry=_rs_chips_plan([send_ff1], pieces=[(0, cut)]))
    (gw_lu_32, gw_lu_16), _ = _wgrad(y_lru, d_br_a, "wgrad_lru_up")
    (gw_pu_32, gw_pu_16), _ = _wgrad_cols(y_pool, d_br_b, s_pu.shape[1], d, "wgrad_pool_up")
    (dh, p_g), r1_mid = _lru_up_bwd(
        d_br_a, w_lu, proj, h, g_block,
        carry=_rs_sibling_plan([gw_o_16, gw_lu_16, gw_pu_16]))
    mid = _rs_sums([gw_o_32, gw_lu_32, gw_pu_32], r1_mid, "mid")
    plan = _join([_rs_chips_plan([send_ff1], pieces=[(cut, send_ff1.shape[1] - cut)], bufs=[r2_ff1]),
                  _rs_chips_plan([mid[0][1]])])
    (p_x, dwa, db_a, dwx, db_x, dlam, dconv_w, dconv_b), got = _lru_bwd(
        dh, h, lru_saved, proj, conv_w_full, wa_bd, wx_bd, lru_lambda, carry=plan)
    (r2_ff1,), (r2_o,) = plan.split(got)
    p_p, dpool_w, dpool_scale = _pool_bwd(d_br_b, g_pu, p, pw, pool_scale)
    parts = [p_x, p_g, p_p, p_ga, p_gb]
    gw_in, (r2_lu, r2_pu) = _wgrad_parts(
        parts, h1, "wgrad_in", carry=_rs_chips_plan([mid[1][1], mid[2][1]]))
    r2_mid = [r2_o, r2_lu, r2_pu]
    tail = _rs_level1([gw_in[0], dpool_w.reshape(N_DEV, -1, POOL_GROUP_DIM), dwa, dwx],
                      [gw_in[1], dpool_w.reshape(N_DEV, -1, POOL_GROUP_DIM), dwa, dwx], "in")
    (grad_x, dg1), r2_tail = _win_bwd_norm(parts, w_int, dx2, x2d, norm_mix_pre,
                                           carry=_rs_chips_plan([s for _, s in tail]))

    def flat2(a):
        return a.reshape(a.shape[0], -1, a.shape[-1])

    fin_small, _ = _finals([
        (flat2(tail[1][0]), flat2(r2_tail[1])), (flat2(tail[2][0]), flat2(r2_tail[2])),
        (flat2(tail[3][0]), flat2(r2_tail[3])),
    ], "rs_finals_small")

    def pad_row(a):
        return jnp.pad(a, ((0, 0), (0, d - a.shape[1])))

    vecs = jnp.concatenate([dg1, dg2, dg3, dg4, dbg_a, dbg_b, dconv_b, db_a, db_x, dlam,
                            pad_row(dpool_scale), dconv_w, pad_row(loss_part)], axis=0)
    assert vecs.shape[0] == N_VEC_ROWS
    vec_parts, g_pool, g_wa, g_wx = _run_plan(_ag_plan([vecs] + fin_small), "ag_tail")

    big_names = ["w_in", "w_lru_up", "w_pool_up", "w_o", "w_ff1", "w_ff2"]
    big_w = [w_in[0].T, w_lru_up[0], w_pool_up[0], w_o[0], w_ff1[0], w_ff2[0]]
    big_g = [(tail[0][0], r2_tail[0]), (mid[1][0], r2_mid[1]),
             (mid[2][0].reshape((1,) + s_pu.shape), r2_mid[2].reshape((3,) + s_pu.shape)),
             (mid[0][0], r2_mid[0]), (own_ff1, r2_ff1), (own_ff2, r2_ff2)]
    big_m = [m_w_in[0].T, m_w_lru_up[0], m_w_pool_up[0], m_w_o[0], m_w_ff1[0], m_w_ff2[0]]
    big_v = [v_w_in[0].T, v_w_lru_up[0], v_w_pool_up[0], v_w_o[0], v_w_ff1[0], v_w_ff2[0]]
    big_out = _adamw_big(big_w, big_g, big_m, big_v)
    big_out[0] = tuple(o.T for o in big_out[0])

    small = dict(
        norm_mix_pre=(norm_mix_pre, m_norm_mix_pre, v_norm_mix_pre),
        norm_mix_post=(norm_mix_post, m_norm_mix_post, v_norm_mix_post),
        norm_mlp_pre=(norm_mlp_pre, m_norm_mlp_pre, v_norm_mlp_pre),
        norm_mlp_post=(norm_mlp_post, m_norm_mlp_post, v_norm_mlp_post),
        b_gate=(b_gate, m_b_gate, v_b_gate), conv_w=(conv_w, m_conv_w, v_conv_w),
        conv_b=(conv_b, m_conv_b, v_conv_b), lru_w_a=(lru_w_a, m_lru_w_a, v_lru_w_a),
        lru_b_a=(lru_b_a, m_lru_b_a, v_lru_b_a), lru_w_x=(lru_w_x, m_lru_w_x, v_lru_w_x),
        lru_b_x=(lru_b_x, m_lru_b_x, v_lru_b_x), lru_lambda=(lru_lambda, m_lru_lambda, v_lru_lambda),
        pool_w=(pool_w, m_pool_w, v_pool_w), pool_scale=(pool_scale, m_pool_scale, v_pool_scale))
    loss_row, small_out = _adamw_small(
        vec_parts, g_pool.reshape(pool_w.shape), g_wa.reshape(lru_w_a.shape), g_wx.reshape(lru_w_x.shape),
        jnp.reshape(me, (1,)).astype(jnp.int32), small)
    grads = {n: o[0] for n, o in small_out.items()}
    delta = {n: o[1] for n, o in small_out.items()}
    new_m = {n: o[2] for n, o in small_out.items()}
    new_v = {n: o[3] for n, o in small_out.items()}

    for name, (g, dl, nm, nv) in zip(big_names, big_out):
        grads[name], delta[name], new_m[name], new_v[name] = g[None], dl[None], nm[None], nv[None]

    loss = loss_row[0, 0]
    order = ["norm_mix_pre", "norm_mix_post", "norm_mlp_pre", "norm_mlp_post", "w_in", "b_gate", "conv_w",
             "conv_b", "lru_w_a", "lru_b_a", "lru_w_x", "lru_b_x", "lru_lambda", "pool_w", "pool_scale",
             "w_lru_up", "w_pool_up", "w_o", "w_ff1", "w_ff2"]
    return (loss, grad_x[None], *[grads[n] for n in order], *[delta[n] for n in order],
            *[new_m[n] for n in order], *[new_v[n] for n in order])
```

```python
import functools
import math
import operator
import types

import jax
import jax.numpy as jnp
from jax import lax
from jax.experimental import pallas as pl
from jax.experimental.pallas import tpu as pltpu

F32 = jnp.float32
BF16 = jnp.bfloat16
NORM_EPS = 1e-6
LRU_C = 8.0
N_LRU_HEADS = 16
LRU_HEAD_DIM = 64
POOL_WINDOWS = (2, 4, 8, 16)
POOL_GROUP_DIM = 128
ADAM_LR = 0.001
ADAM_B1 = 0.9
ADAM_B2 = 0.999
ADAM_EPS = 1e-08
ADAM_WD = 0.01
ADAM_STEP = 10
N_DEV = 8
V7X_VMEM_LIMIT_BYTES = 56 * 1024 * 1024
LRU_CB = 256
MESH = pl.DeviceIdType.MESH
ANY = pl.BlockSpec(memory_space=pl.ANY)


def _tile(n, pref):
    t = min(n, pref)
    assert n % t == 0, (n, pref)
    return t


def _dot_nn(a, b):
    return lax.dot_general(a, b, (((1,), (0,)), ((), ())), preferred_element_type=F32)


def _dot_nt(a, b):
    return lax.dot_general(a, b, (((1,), (1,)), ((), ())), preferred_element_type=F32)


def _dot_tn(a, b):
    return lax.dot_general(a, b, (((0,), (0,)), ((), ())), preferred_element_type=F32)


def _row_chunks(n_rows, fn, chunk=256):
    chunk = min(chunk, n_rows)
    assert n_rows % chunk == 0

    def step(r, carry):
        fn(pl.ds(pl.multiple_of(r * chunk, chunk), chunk))
        return carry

    lax.fori_loop(0, n_rows // chunk, step, 0)


def _sig(x):
    return 1.0 / (1.0 + jnp.exp(-x))


def _rms_hat(x):
    r = lax.rsqrt(jnp.mean(x * x, axis=-1, keepdims=True) + NORM_EPS)
    return x * r, r


def _rms_bwd(dn, xhat, r, g):
    q = dn * g
    dx = r * (q - xhat * jnp.mean(q * xhat, axis=-1, keepdims=True))
    dg = jnp.sum(dn * xhat, axis=0, keepdims=True)
    return dx, dg


_GELU_K = math.sqrt(2.0 / math.pi)
_GELU_C = 0.044715


def _gelu_and_grad(g):
    t = jnp.tanh(_GELU_K * (g + _GELU_C * g * g * g))
    val = 0.5 * g * (1.0 + t)
    grad = 0.5 * (1.0 + t) + 0.5 * g * (1.0 - t * t) * (_GELU_K * (1.0 + 3.0 * _GELU_C * g * g))
    return val, grad


def _softplus_neg(lam):
    z = -lam
    e = jnp.exp(-jnp.abs(z))
    u = 1.0 + e
    d = u - 1.0
    l1p = jnp.where(d == 0.0, e, jnp.log(u) * (e / jnp.where(d == 0.0, 1.0, d)))
    return jnp.maximum(z, 0.0) + l1p


def _lru_gates(xc, wa, ba, wx, bx, lam):
    xcb = xc.astype(BF16)
    r = _sig(_dot_nn(xcb, wa) + ba)
    i = _sig(_dot_nn(xcb, wx) + bx)
    sp = _softplus_neg(lam)
    log_a = (-LRU_C) * r * sp
    a = jnp.exp(log_a)
    mult = jnp.sqrt(-jnp.tanh(log_a) * (1.0 + a * a))
    return xcb, r, i, sp, log_a, a, mult


def _place():
    return lax.axis_index("x"), lax.axis_index("y"), lax.axis_index("c")


def _ag_plan(shards, pieces=None, bufs=None):
    na = len(shards)
    n_kinds = 7

    def parts(ins, outs, sems):
        send_sems, recv_sems, local_sems = sems
        x, y, c = _place()
        me, sibling = (x, y, c), (x, y, 1 - c)
        x_nb, y_nb, diag = (1 - x, y), (x, 1 - y), (1 - x, 1 - y)
        relay_src = (c * (1 - x) + (1 - c) * x, c * y + (1 - c) * (1 - y))
        relay_dst = (c * x + (1 - c) * (1 - x), c * (1 - y) + (1 - c) * y)

        def own(a):
            return ins[a] if pieces is None else ins[a].at[pl.ds(*pieces[a])]

        def slot(a, px, py, pc):
            idx = 4 * px + 2 * py + pc
            return outs[a].at[idx] if pieces is None else outs[a].at[idx, pl.ds(*pieces[a])]

        def copy(a, k, block, to, src=None):
            return pltpu.make_async_remote_copy(
                src_ref=slot(a, *block) if src is None else src,
                dst_ref=slot(a, *block),
                send_sem=send_sems.at[a * n_kinds + k],
                recv_sem=recv_sems.at[a * n_kinds + k],
                device_id=to,
                device_id_type=MESH,
            )

        mine = [pltpu.make_async_copy(own(a), slot(a, *me), local_sems.at[a]) for a in range(na)]
        first, second, third = [], [], []
        for a in range(na):
            first += [copy(a, 0, me, sibling, src=own(a)), copy(a, 1, me, (*x_nb, c), src=own(a)),
                      copy(a, 2, me, (*y_nb, c), src=own(a))]
            second += [copy(a, 3, (*relay_src, c), (*relay_dst, c)), copy(a, 4, (*x_nb, c), sibling),
                       copy(a, 5, (*y_nb, c), sibling)]
            third.append(copy(a, 6, (*diag, c), sibling))
        return sibling, c, x_nb, y_nb, diag, copy, mine, first, second, third

    def start(ins, outs, sems):
        _, _, _, _, _, _, mine, first, _, _ = parts(ins, outs, sems)
        for cp in mine + first:
            cp.start()

    def middle(ins, outs, sems):
        _, c, x_nb, y_nb, _, copy, _, _, second, _ = parts(ins, outs, sems)
        for a in range(na):
            copy(a, 1, (*x_nb, c), (*x_nb, c)).wait_recv()
            copy(a, 2, (*y_nb, c), (*y_nb, c)).wait_recv()
        for cp in second:
            cp.start()

    def finish(ins, outs, sems):
        sibling, c, x_nb, y_nb, diag, copy, mine, first, second, third = parts(ins, outs, sems)
        for a in range(na):
            copy(a, 3, (*diag, c), (*diag, c)).wait_recv()
            third[a].start()
        for a in range(na):
            copy(a, 0, sibling, sibling).wait_recv()
            copy(a, 4, (*x_nb, 1 - c), sibling).wait_recv()
            copy(a, 5, (*y_nb, 1 - c), sibling).wait_recv()
            copy(a, 6, (*diag, 1 - c), sibling).wait_recv()
        for cp in first + second + third:
            cp.wait_send()
        for cp in mine:
            cp.wait()

    return types.SimpleNamespace(
        ins=list(shards) + list(bufs or []),
        out_shapes=[jax.ShapeDtypeStruct((N_DEV,) + s.shape, s.dtype) for s in shards],
        sems=[pltpu.SemaphoreType.DMA((n_kinds * na,)), pltpu.SemaphoreType.DMA((n_kinds * na,)),
              pltpu.SemaphoreType.DMA((na,))],
        aliases=[(na + a, a) for a in range(na)] if bufs else [],
        peers=frozenset({"sibling", "neighbours"}), start=start, middle=middle, finish=finish)


def _rs_sibling_plan(fulls):
    na = len(fulls)
    rs = [f.shape[0] // N_DEV for f in fulls]

    def copies(ins, outs, sems):
        send_sems, recv_sems = sems
        x, y, c = _place()
        out = []
        for a in range(na):
            for q in range(4):
                shard = 2 * q + (1 - c)
                out.append(pltpu.make_async_remote_copy(
                    src_ref=ins[a].at[pl.ds(shard * rs[a], rs[a])],
                    dst_ref=outs[a].at[q],
                    send_sem=send_sems.at[a * 4 + q],
                    recv_sem=recv_sems.at[a * 4 + q],
                    device_id=(x, y, 1 - c),
                    device_id_type=MESH,
                ))
        return out

    def start(ins, outs, sems):
        for cp in copies(ins, outs, sems):
            cp.start()

    def finish(ins, outs, sems):
        for cp in copies(ins, outs, sems):
            cp.wait()

    return types.SimpleNamespace(
        ins=list(fulls),
        out_shapes=[jax.ShapeDtypeStruct((4, r) + f.shape[1:], f.dtype) for r, f in zip(rs, fulls)],
        sems=[pltpu.SemaphoreType.DMA((4 * na,)), pltpu.SemaphoreType.DMA((4 * na,))],
        peers=frozenset({"sibling"}), start=start, finish=finish)


def _rs_chips_plan(sends, pieces=None, bufs=None):
    na = len(sends)

    def copies(ins, outs, sems):
        send_sems, recv_sems = sems
        x, y, c = _place()
        chips = [(1 - x, y), (x, 1 - y), (1 - x, 1 - y)]
        out = []
        for a in range(na):
            for k, chip in enumerate(chips):
                rows = (k,) if pieces is None else (k, pl.ds(*pieces[a]))
                out.append(pltpu.make_async_remote_copy(
                    src_ref=ins[a].at[rows],
                    dst_ref=outs[a].at[rows],
                    send_sem=send_sems.at[a * 3 + k],
                    recv_sem=recv_sems.at[a * 3 + k],
                    device_id=(*chip, c),
                    device_id_type=MESH,
                ))
        return out

    def start(ins, outs, sems):
        for cp in copies(ins, outs, sems):
            cp.start()

    def finish(ins, outs, sems):
        for cp in copies(ins, outs, sems):
            cp.wait()

    return types.SimpleNamespace(
        ins=list(sends) + list(bufs or []),
        out_shapes=[jax.ShapeDtypeStruct(s.shape, s.dtype) for s in sends],
        sems=[pltpu.SemaphoreType.DMA((3 * na,)), pltpu.SemaphoreType.DMA((3 * na,))],
        aliases=[(na + a, a) for a in range(na)] if bufs else [],
        peers=frozenset({"chips"}), start=start, finish=finish)


def _join(plans):
    ins, outs, sems, aliases, offs = [], [], [], [], []
    for p in plans:
        offs.append((len(ins), len(outs), len(sems)))
        aliases += [(len(ins) + ci, len(outs) + co) for ci, co in getattr(p, "aliases", [])]
        ins += p.ins
        outs += p.out_shapes
        sems += p.sems

    def cut(p, off, i, o, s):
        return (i[off[0]:off[0] + len(p.ins)], o[off[1]:off[1] + len(p.out_shapes)],
                s[off[2]:off[2] + len(p.sems)])

    def start(i, o, s):
        for p, off in zip(plans, offs):
            p.start(*cut(p, off, i, o, s))

    def middle(i, o, s):
        for p, off in zip(plans, offs):
            if getattr(p, "middle", None) is not None:
                p.middle(*cut(p, off, i, o, s))

    def finish(i, o, s):
        for p, off in zip(plans, offs):
            p.finish(*cut(p, off, i, o, s))

    def split(results):
        return [list(results[off[1]:off[1] + len(p.out_shapes)]) for p, off in zip(plans, offs)]

    return types.SimpleNamespace(ins=ins, out_shapes=outs, sems=sems, aliases=aliases,
                                 peers=frozenset().union(*[p.peers for p in plans]),
                                 start=start, middle=middle, finish=finish, split=split)


COLLECTIVE_ID = {frozenset({"sibling"}): 0, frozenset({"chips"}): 1, frozenset({"sibling", "chips"}): 2,
                 frozenset({"sibling", "neighbours"}): 3}


def _handshake(peers):
    x, y, c = _place()
    devs = []
    if "sibling" in peers:
        devs.append((x, y, 1 - c))
    if "neighbours" in peers:
        devs += [(1 - x, y, c), (x, 1 - y, c)]
    if "chips" in peers:
        assert "neighbours" not in peers
        devs += [(1 - x, y, c), (x, 1 - y, c), (1 - x, 1 - y, c)]
    barrier = pltpu.get_barrier_semaphore()
    for dev in devs:
        pl.semaphore_signal(barrier, inc=1, device_id=dev, device_id_type=MESH)
    pl.semaphore_wait(barrier, len(devs))


def _in_hbm(args):
    return [pltpu.with_memory_space_constraint(a, pltpu.HBM) for a in args]


def _run_plan(plan, name):
    n_in, n_out = len(plan.ins), len(plan.out_shapes)

    def body(*refs):
        ins, outs, sems = refs[:n_in], refs[n_in:n_in + n_out], refs[n_in + n_out:]
        _handshake(plan.peers)
        plan.start(ins, outs, sems)
        if getattr(plan, "middle", None) is not None:
            plan.middle(ins, outs, sems)
        plan.finish(ins, outs, sems)

    return pl.pallas_call(
        body,
        name=name,
        in_specs=[ANY] * n_in,
        out_specs=[ANY] * n_out,
        out_shape=plan.out_shapes,
        scratch_shapes=plan.sems,
        input_output_aliases=dict(getattr(plan, "aliases", [])),
        compiler_params=pltpu.CompilerParams(collective_id=COLLECTIVE_ID[plan.peers]),
    )(*_in_hbm(plan.ins))


def _call(body, *, name, grid, in_specs, out_specs, out_shape, args, scratch_shapes=(), aliases=None,
          carry=None):
    n_in, n_out, n_scr = len(in_specs), len(out_shape), len(scratch_shapes)
    params = pltpu.CompilerParams(
        dimension_semantics=("arbitrary",) * len(grid), vmem_limit_bytes=V7X_VMEM_LIMIT_BYTES)
    if carry is None:
        outs = pl.pallas_call(
            body, name=name, grid=grid, in_specs=list(in_specs), out_specs=list(out_specs),
            out_shape=list(out_shape), scratch_shapes=list(scratch_shapes),
            input_output_aliases=aliases or {}, compiler_params=params)(*_in_hbm(args))
        return list(outs), []
    c_in, c_out = len(carry.ins), len(carry.out_shapes)

    def full(*refs):
        p = 0
        ins = refs[p:p + n_in]
        p += n_in
        cins = refs[p:p + c_in]
        p += c_in
        outs = refs[p:p + n_out]
        p += n_out
        couts = refs[p:p + c_out]
        p += c_out
        scr = refs[p:p + n_scr]
        csems = refs[p + n_scr:]
        ids = [pl.program_id(a) for a in range(len(grid))]
        first = functools.reduce(operator.and_, [i == 0 for i in ids])
        last = functools.reduce(operator.and_, [i == g - 1 for i, g in zip(ids, grid)])

        @pl.when(first)
        def _():
            _handshake(carry.peers)
            carry.start(cins, couts, csems)

        if getattr(carry, "middle", None) is not None:
            n_steps = math.prod(grid)
            flat = functools.reduce(lambda acc, ig: acc * ig[1] + ig[0], zip(ids, grid), 0)

            @pl.when(flat == (2 * n_steps) // 3)
            def _():
                carry.middle(cins, couts, csems)

        body(*ins, *outs, *scr)

        @pl.when(last)
        def _():
            carry.finish(cins, couts, csems)

    all_aliases = dict(aliases or {})
    all_aliases.update({n_in + ci: n_out + co for ci, co in getattr(carry, "aliases", [])})
    params = pltpu.CompilerParams(
        dimension_semantics=("arbitrary",) * len(grid), vmem_limit_bytes=V7X_VMEM_LIMIT_BYTES,
        collective_id=COLLECTIVE_ID[carry.peers])
    outs = pl.pallas_call(
        full, name=name, grid=grid,
        in_specs=list(in_specs) + [ANY] * c_in,
        out_specs=list(out_specs) + [ANY] * c_out,
        out_shape=list(out_shape) + list(carry.out_shapes),
        scratch_shapes=list(scratch_shapes) + list(carry.sems),
        input_output_aliases=all_aliases, compiler_params=params)(*_in_hbm(args), *_in_hbm(carry.ins))
    return list(outs[:n_out]), list(outs[n_out:])


def _norm_proj(x, g1, w_int, carry=None):
    t, d = x.shape
    n = w_int.shape[0]
    tt, tn = _tile(t, 2048), _tile(n, 512)

    def body(x_ref, g_ref, w_ref, proj_ref, h1_ref, h1_s):
        @pl.when(pl.program_id(1) == 0)
        def _():
            def norm_rows(rows):
                xhat, _ = _rms_hat(x_ref[rows, :])
                h = (xhat * g_ref[...]).astype(BF16)
                h1_s[rows, :] = h
                h1_ref[rows, :] = h

            _row_chunks(tt, norm_rows)

        proj_ref[...] = _dot_nt(h1_s[...], w_ref[...]).astype(BF16)

    return _call(
        body, name="norm_proj", grid=(t // tt, n // tn),
        in_specs=[
            pl.BlockSpec((tt, d), lambda i, j: (i, 0)),
            pl.BlockSpec((1, d), lambda i, j: (0, 0)),
            pl.BlockSpec((tn, d), lambda i, j: (j, 0)),
        ],
        out_specs=[
            pl.BlockSpec((tt, tn), lambda i, j: (i, j)),
            pl.BlockSpec((tt, d), lambda i, j: (i, 0)),
        ],
        out_shape=[jax.ShapeDtypeStruct((t, n), BF16), jax.ShapeDtypeStruct((t, d), BF16)],
        scratch_shapes=[pltpu.VMEM((tt, d), BF16)],
        args=(x, g1, w_int), carry=carry)


def _scan_rows(av, bv, reverse):
    tc = av.shape[0]
    row = lax.broadcasted_iota(jnp.int32, av.shape, 0)
    s = 1
    while s < tc:
        if s < 8:
            keep = (row < tc - s) if reverse else (row >= s)
            shift = (tc - s) if reverse else s
            a_sh = jnp.where(keep, pltpu.roll(av, shift, 0), 1.0)
            b_sh = jnp.where(keep, pltpu.roll(bv, shift, 0), 0.0)
            bv = av * b_sh + bv
            av = av * a_sh
        elif reverse:
            bv = jnp.concatenate([av[:tc - s] * bv[s:] + bv[:tc - s], bv[tc - s:]], axis=0)
            av = jnp.concatenate([av[:tc - s] * av[s:], av[tc - s:]], axis=0)
        else:
            bv = jnp.concatenate([bv[:s], av[s:] * bv[:tc - s] + bv[s:]], axis=0)
            av = jnp.concatenate([av[:s], av[s:] * av[:tc - s]], axis=0)
        s *= 2
    return av, bv


N_LRU_SAVED = 5


def _fill_block_diag(w_ref, bd_ref):
    bd_ref[...] = jnp.zeros_like(bd_ref)
    hd = LRU_HEAD_DIM
    for k in range(w_ref.shape[0]):
        bd_ref[k * hd:(k + 1) * hd, k * hd:(k + 1) * hd] = w_ref[k].astype(BF16)


def _lru_fwd(proj, conv_w, conv_b, w_a, b_a, w_x, b_x, lam, carry=None):
    t = proj.shape[0]
    dr = conv_b.shape[1]
    cb = LRU_CB
    tc = _tile(t, 256)
    ncb, ntc = dr // cb, t // tc

    def body(xp_ref, g_ref, cw_ref, cb_ref, wa_ref, ba_ref, wx_ref, bx_ref, lam_ref,
             y_ref, h_ref, saved_ref, prevx_s, hlast_s, wa_s, wx_s):
        c = pl.program_id(1)

        @pl.when(c == 0)
        def _():
            prevx_s[...] = jnp.zeros_like(prevx_s)
            hlast_s[...] = jnp.zeros_like(hlast_s)
            _fill_block_diag(wa_ref, wa_s)
            _fill_block_diag(wx_ref, wx_s)

        x = xp_ref[...].astype(F32)
        prev = prevx_s[...]
        row = lax.broadcasted_iota(jnp.int32, x.shape, 0)

        def sh(j):
            return jnp.where(row >= j, pltpu.roll(x, j, 0), pltpu.roll(prev, j, 0))

        xc = (cb_ref[...] + cw_ref[0:1, :] * sh(3) + cw_ref[1:2, :] * sh(2)
              + cw_ref[2:3, :] * sh(1) + cw_ref[3:4, :] * x)
        prevx_s[...] = x
        _, r, i, _, _, a, mult = _lru_gates(xc, wa_s[...], ba_ref[...], wx_s[...], bx_ref[...],
                                            lam_ref[...])
        for k, val in enumerate((xc, r, i, a, mult)):
            saved_ref[:, k * cb:(k + 1) * cb] = val
        av, bv = _scan_rows(a, mult * (i * xc), reverse=False)
        h = av * hlast_s[...] + bv
        h_ref[...] = h
        hlast_s[...] = h_ref[tc - 1:tc, :]
        gel, _ = _gelu_and_grad(g_ref[...].astype(F32))
        y_ref[...] = (h * gel).astype(BF16)

    vec = pl.BlockSpec((1, cb), lambda j, c: (0, j))
    blk = pl.BlockSpec((tc, cb), lambda j, c: (c, j))
    mat = pl.BlockSpec((cb // LRU_HEAD_DIM, LRU_HEAD_DIM, LRU_HEAD_DIM), lambda j, c: (j, 0, 0))
    return _call(
        body, name="lru_fwd", grid=(ncb, ntc),
        in_specs=[
            blk,
            pl.BlockSpec((tc, cb), lambda j, c: (c, ncb + j)),
            pl.BlockSpec((4, cb), lambda j, c: (0, j)),
            vec, mat, vec, mat, vec, vec,
        ],
        out_specs=[blk, blk, pl.BlockSpec((tc, N_LRU_SAVED * cb), lambda j, c: (c, j))],
        out_shape=[jax.ShapeDtypeStruct((t, dr), BF16), jax.ShapeDtypeStruct((t, dr), F32),
                   jax.ShapeDtypeStruct((t, N_LRU_SAVED * dr), F32)],
        scratch_shapes=[pltpu.VMEM((tc, cb), F32), pltpu.VMEM((1, cb), F32),
                        pltpu.VMEM((cb, cb), BF16), pltpu.VMEM((cb, cb), BF16)],
        args=(proj, proj, conv_w, conv_b, w_a, b_a, w_x, b_x, lam), carry=carry)


def _pool_select(col, vals):
    out = vals[3]
    for g in (2, 1, 0):
        out = jnp.where(col < (g + 1) * POOL_GROUP_DIM, vals[g], out)
    return out


def _pool_fwd(proj, pool_w, pool_scale, col_block):
    t = proj.shape[0]
    dp = pool_scale.shape[1]
    tc = _tile(t, 256)
    ntc = t // tc

    def body(x_ref, w_ref, sc_ref, y_ref, p_ref, px, p2, p4, p8):
        c = pl.program_id(0)

        @pl.when(c == 0)
        def _():
            for s in (px, p2, p4, p8):
                s[...] = jnp.zeros_like(s)

        x = x_ref[...].astype(F32)
        row = lax.broadcasted_iota(jnp.int32, x.shape, 0)
        col = lax.broadcasted_iota(jnp.int32, x.shape, 1)

        def sh(v, pv, j):
            return jnp.where(row >= j, pltpu.roll(v, j, 0), pltpu.roll(pv[...], j, 0))

        s2 = x + sh(x, px, 1)
        s4 = s2 + sh(s2, p2, 2)
        s8 = s4 + sh(s4, p4, 4)
        s16 = s8 + sh(s8, p8, 8)
        px[...] = x
        p2[...] = s2
        p4[...] = s4
        p8[...] = s8
        wsum = _pool_select(col, (s2, s4, s8, s16))
        win = _pool_select(col, POOL_WINDOWS)
        cnt = jnp.minimum(c * tc + row + 1, win).astype(F32)
        p = wsum / cnt - x
        pb = p.astype(BF16)
        p_ref[...] = pb
        for g in range(len(POOL_WINDOWS)):
            sl = slice(g * POOL_GROUP_DIM, (g + 1) * POOL_GROUP_DIM)
            yg = _dot_nn(pb[:, sl], w_ref[g]) * sc_ref[:, sl]
            y_ref[:, sl] = yg.astype(BF16)

    return _call(
        body, name="pool_fwd", grid=(ntc,),
        in_specs=[
            pl.BlockSpec((tc, dp), lambda c: (c, col_block)),
            pl.BlockSpec(pool_w.shape, lambda c: (0, 0, 0)),
            pl.BlockSpec((1, dp), lambda c: (0, 0)),
        ],
        out_specs=[pl.BlockSpec((tc, dp), lambda c: (c, 0))] * 2,
        out_shape=[jax.ShapeDtypeStruct((t, dp), BF16)] * 2,
        scratch_shapes=[pltpu.VMEM((tc, dp), F32)] * 4,
        args=(proj, pool_w, pool_scale))[0]


def _branch_mix(y_lru, y_pool, w_lru_up, w_pool_upb, proj, b_gate, ga_block, gb_block, carry=None):
    t, d = y_lru.shape
    dp = y_pool.shape[1]
    bw = w_pool_upb.shape[2]
    tt, tn = _tile(t, 1024), 512
    nj = d // tn

    def body(yl_ref, yp_ref, wl_ref, wp_ref, ga_ref, gb_ref, ba_ref, bb_ref, bra_ref, brb_ref, mix_ref):
        br_a = _dot_nn(yl_ref[...], wl_ref[...])
        wp = jnp.concatenate([wp_ref[b] for b in range(tn // bw)], axis=1)
        br_b = _dot_nn(yp_ref[...], wp)
        bra_ref[...] = br_a.astype(BF16)
        brb_ref[...] = br_b.astype(BF16)
        ga = _sig(ga_ref[...].astype(F32) + ba_ref[...])
        gb = _sig(gb_ref[...].astype(F32) + bb_ref[...])
        mix_ref[...] = (ga * br_a + gb * br_b).astype(BF16)

    out = pl.BlockSpec((tt, tn), lambda j, i: (i, j))
    return _call(
        body, name="branch_mix", grid=(nj, t // tt),
        in_specs=[
            pl.BlockSpec((tt, d), lambda j, i: (i, 0)),
            pl.BlockSpec((tt, dp), lambda j, i: (i, 0)),
            pl.BlockSpec((d, tn), lambda j, i: (0, j)),
            pl.BlockSpec((tn // bw, dp, bw), lambda j, i: (j, 0, 0)),
            pl.BlockSpec((tt, tn), lambda j, i: (i, ga_block + j)),
            pl.BlockSpec((tt, tn), lambda j, i: (i, gb_block + j)),
            pl.BlockSpec((1, tn), lambda j, i: (0, j)),
            pl.BlockSpec((1, tn), lambda j, i: (0, nj + j)),
        ],
        out_specs=[out, out, out],
        out_shape=[jax.ShapeDtypeStruct((t, d), BF16)] * 3,
        args=(y_lru, y_pool, w_lru_up, w_pool_upb, proj, proj, b_gate, b_gate), carry=carry)


def _wo_norm(mix, w_o, x, g2, g3, carry=None):
    t, d = x.shape
    tt = _tile(t, 512)

    def body(mix_ref, w_ref, x_ref, g2_ref, g3_ref, m_ref, x2_ref, h3_ref):
        m = _dot_nn(mix_ref[...], w_ref[...])
        m_ref[...] = m
        mhat, _ = _rms_hat(m)
        x2 = x_ref[...] + mhat * g2_ref[...]
        x2_ref[...] = x2
        xhat, _ = _rms_hat(x2)
        h3_ref[...] = (xhat * g3_ref[...]).astype(BF16)

    row = pl.BlockSpec((tt, d), lambda i: (i, 0))
    vec = pl.BlockSpec((1, d), lambda i: (0, 0))
    return _call(
        body, name="wo_norm", grid=(t // tt,),
        in_specs=[row, pl.BlockSpec((d, d), lambda i: (0, 0)), row, vec, vec],
        out_specs=[row, row, row],
        out_shape=[
            jax.ShapeDtypeStruct((t, d), F32),
            jax.ShapeDtypeStruct((t, d), F32),
            jax.ShapeDtypeStruct((t, d), BF16),
        ],
        args=(mix, w_o, x, g2, g3), carry=carry)


def _ff1(h3, w_ff1b, carry=None):
    t, d = h3.shape
    nb, _, tn = w_ff1b.shape
    tt = _tile(t, 2048)

    def body(h_ref, w_ref, rf_ref):
        rf_ref[...] = jnp.maximum(_dot_nn(h_ref[...], w_ref[...]), 0.0).astype(BF16)

    out = pl.BlockSpec((tt, tn), lambda i, j: (i, j))
    return _call(
        body, name="ff1", grid=(t // tt, nb),
        in_specs=[pl.BlockSpec((tt, d), lambda i, j: (i, 0)), pl.BlockSpec((None, d, tn), lambda i, j: (j, 0, 0))],
        out_specs=[out],
        out_shape=[jax.ShapeDtypeStruct((t, nb * tn), BF16)],
        args=(h3, w_ff1b), carry=carry)


def _ff2_loss(rf, w_ff2, x2, g4, target):
    t, k = rf.shape
    d = x2.shape[1]
    tt, tk = _tile(t, 1024), _tile(k, 1024)
    nk = k // tk

    def body(a_ref, w_ref, x2_ref, g_ref, tg_ref, dy_ref, df_ref, dg_ref, loss_ref, acc):
        i, kk = pl.program_id(0), pl.program_id(1)

        @pl.when(kk == 0)
        def _():
            acc[...] = jnp.zeros_like(acc)

        @pl.when((i == 0) & (kk == 0))
        def _():
            dg_ref[...] = jnp.zeros_like(dg_ref)
            loss_ref[...] = jnp.zeros_like(loss_ref)

        rf_tile = a_ref[...]
        acc[...] += _dot_nn(rf_tile * rf_tile, w_ref[...])

        @pl.when(kk == nk - 1)
        def _():
            def tail(rows):
                fhat, r = _rms_hat(acc[rows, :])
                g = g_ref[...]
                e = x2_ref[rows, :] + fhat * g - tg_ref[rows, :]
                loss_ref[...] += 0.5 * jnp.sum(jnp.mean(e * e, axis=-1, keepdims=True))
                dy = e * (1.0 / d)
                dy_ref[rows, :] = dy.astype(BF16)
                df, dg = _rms_bwd(dy, fhat, r, g)
                df_ref[rows, :] = df.astype(BF16)
                dg_ref[...] += dg

            _row_chunks(tt, tail)

    row = pl.BlockSpec((tt, d), lambda i, kk: (i, 0))
    vec = pl.BlockSpec((1, d), lambda i, kk: (0, 0))
    return _call(
        body, name="ff2_loss", grid=(t // tt, nk),
        in_specs=[
            pl.BlockSpec((tt, tk), lambda i, kk: (i, kk)),
            pl.BlockSpec((tk, d), lambda i, kk: (kk, 0)),
            row, vec, row,
        ],
        out_specs=[row, row, vec, pl.BlockSpec((1, 128), lambda i, kk: (0, 0))],
        out_shape=[
            jax.ShapeDtypeStruct((t, d), BF16),
            jax.ShapeDtypeStruct((t, d), BF16),
            jax.ShapeDtypeStruct((1, d), F32),
            jax.ShapeDtypeStruct((1, 128), F32),
        ],
        scratch_shapes=[pltpu.VMEM((tt, d), F32)],
        args=(rf, w_ff2, x2, g4, target))[0]


def _ff2_bwd(df, w_ff2, rf, carry=None):
    t, d = df.shape
    n = w_ff2.shape[0]
    tt, tn = _tile(t, 2048), _tile(n, 512)

    def body(df_ref, w_ref, rf_ref, out_ref):
        d_act = _dot_nt(df_ref[...], w_ref[...])
        out_ref[...] = (d_act * (2.0 * rf_ref[...].astype(F32))).astype(BF16)

    blk = pl.BlockSpec((tt, tn), lambda i, j: (i, j))
    return _call(
        body, name="ff2_bwd", grid=(t // tt, n // tn),
        in_specs=[pl.BlockSpec((tt, d), lambda i, j: (i, 0)), pl.BlockSpec((tn, d), lambda i, j: (j, 0)), blk],
        out_specs=[blk],
        out_shape=[jax.ShapeDtypeStruct((t, n), BF16)],
        args=(df, w_ff2, rf), carry=carry)


def _wgrad(a, b, name, prev=None, row_off=0, rows=None, carry=None, square_a=False):
    t, m = a.shape
    n = b.shape[1]
    rows = m if rows is None else rows
    tm, tk = _tile(m, 512), _tile(t, 2048)
    nk = t // tk
    assert row_off % tm == 0
    off = row_off // tm

    def body(*refs):
        a_ref, b_ref = refs[0], refs[1]
        o32_ref, o16_ref, acc = refs[-3], refs[-2], refs[-1]
        kk = pl.program_id(1)

        @pl.when(kk == 0)
        def _():
            acc[...] = jnp.zeros_like(acc)

        a_tile = a_ref[...]
        acc[...] += _dot_tn(a_tile * a_tile if square_a else a_tile, b_ref[...])

        @pl.when(kk == nk - 1)
        def _():
            o32_ref[...] = acc[...]
            o16_ref[...] = acc[...].astype(BF16)

    in_specs = [pl.BlockSpec((tk, tm), lambda i, kk: (kk, i)), pl.BlockSpec((tk, n), lambda i, kk: (kk, 0))]
    args = [a, b]
    aliases = {}
    if prev is not None:
        in_specs += [ANY, ANY]
        args += list(prev)
        aliases = {2: 0, 3: 1}
    out = pl.BlockSpec((tm, n), lambda i, kk: (off + i, 0))
    return _call(
        body, name=name, grid=(m // tm, nk),
        in_specs=in_specs, out_specs=[out, out],
        out_shape=[jax.ShapeDtypeStruct((rows, n), F32), jax.ShapeDtypeStruct((rows, n), BF16)],
        scratch_shapes=[pltpu.VMEM((tm, n), F32)],
        aliases=aliases, args=args, carry=carry)


def _wgrad_parts(parts, b, name, carry=None):
    t, n = b.shape
    tm = 512
    bounds = []
    lo = 0
    for part in parts:
        assert part.shape[0] == t and part.shape[1] % tm == 0
        bounds.append((lo, lo + part.shape[1] // tm))
        lo += part.shape[1] // tm
    nm = lo
    np_ = len(parts)

    def body(*refs):
        p_refs, b_ref, o32_ref, o16_ref = refs[:np_], refs[np_], refs[np_ + 1], refs[np_ + 2]
        i = pl.program_id(0)
        for (lo_p, hi_p), p_ref in zip(bounds, p_refs):
            @pl.when((i >= lo_p) & (i < hi_p))
            def _(p_ref=p_ref):
                res = _dot_tn(p_ref[...], b_ref[...])
                o32_ref[...] = res
                o16_ref[...] = res.astype(BF16)

    def part_spec(lo_p, hi_p):
        return pl.BlockSpec((t, tm), lambda i: (0, jnp.clip(i - lo_p, 0, hi_p - lo_p - 1)))

    out = pl.BlockSpec((tm, n), lambda i: (i, 0))
    return _call(
        body, name=name, grid=(nm,),
        in_specs=[part_spec(lo_p, hi_p) for lo_p, hi_p in bounds] + [pl.BlockSpec((t, n), lambda i: (0, 0))],
        out_specs=[out, out],
        out_shape=[jax.ShapeDtypeStruct((nm * tm, n), F32), jax.ShapeDtypeStruct((nm * tm, n), BF16)],
        args=(*parts, b), carry=carry)


def _wgrad_group(triples, name, carry=None):
    t, n = triples[0][1].shape
    tm, nk = 512, 2
    tk = t // nk
    bounds = []
    lo = 0
    for a, b, bw in triples:
        assert a.shape[0] == t and b.shape == (t, n) and a.shape[1] % tm == 0
        assert bw is None or a.shape[1] == tm
        bounds.append((lo, lo + a.shape[1] // tm))
        lo += a.shape[1] // tm
    n_tiles = lo
    np_ = len(triples)

    def body(*refs):
        ins, outs, acc = refs[:2 * np_], refs[2 * np_:4 * np_], refs[4 * np_]
        i, k = pl.program_id(0), pl.program_id(1)
        for p, ((lo_p, hi_p), (_, _, bw)) in enumerate(zip(bounds, triples)):
            a_ref, b_ref, o32_ref, o16_ref = ins[2 * p], ins[2 * p + 1], outs[2 * p], outs[2 * p + 1]
            mine = (i >= lo_p) & (i < hi_p)

            @pl.when(mine & (k == 0))
            def _(a_ref=a_ref, b_ref=b_ref):
                acc[...] = _dot_tn(a_ref[...], b_ref[...])

            @pl.when(mine & (k == nk - 1))
            def _(a_ref=a_ref, b_ref=b_ref, o32_ref=o32_ref, o16_ref=o16_ref, bw=bw):
                res = acc[...] + _dot_tn(a_ref[...], b_ref[...])
                if bw is None:
                    o32_ref[...] = res
                    o16_ref[...] = res.astype(BF16)
                else:
                    for blk in range(n // bw):
                        part = res[:, blk * bw:(blk + 1) * bw]
                        o32_ref[blk] = part
                        o16_ref[blk] = part.astype(BF16)

    in_specs, out_specs, out_shape, args = [], [], [], []
    for (lo_p, hi_p), (a, b, bw) in zip(bounds, triples):
        def tok(i, k, lo_p=lo_p, hi_p=hi_p):
            return jnp.where(i < lo_p, 0, jnp.where(i < hi_p, k, nk - 1))

        def tile(i, lo_p=lo_p, hi_p=hi_p):
            return jnp.clip(i - lo_p, 0, hi_p - lo_p - 1)

        in_specs += [pl.BlockSpec((tk, tm), lambda i, k, tok=tok, tile=tile: (tok(i, k), tile(i))),
                     pl.BlockSpec((tk, n), lambda i, k, tok=tok: (tok(i, k), 0))]
        args += [a, b]
        m = a.shape[1]
        if bw is None:
            spec = pl.BlockSpec((tm, n), lambda i, k, tile=tile: (tile(i), 0))
            shape = (m, n)
        else:
            spec = pl.BlockSpec((n // bw, m, bw), lambda i, k: (0, 0, 0))
            shape = (n // bw, m, bw)
        out_specs += [spec, spec]
        out_shape += [jax.ShapeDtypeStruct(shape, F32), jax.ShapeDtypeStruct(shape, BF16)]
    outs, carried = _call(
        body, name=name, grid=(n_tiles, nk), in_specs=in_specs, out_specs=out_specs, out_shape=out_shape,
        scratch_shapes=[pltpu.VMEM((tm, n), F32)], args=args, carry=carry)
    return [(outs[2 * p], outs[2 * p + 1]) for p in range(np_)], carried


def _wgrad_cols(a, b, bw, tn, name, carry=None):
    t, m = a.shape
    n = b.shape[1]
    per_step = tn // bw

    def body(a_ref, b_ref, o32_ref, o16_ref):
        res = _dot_tn(a_ref[...], b_ref[...])
        for blk in range(per_step):
            part = res[:, blk * bw:(blk + 1) * bw]
            o32_ref[blk] = part
            o16_ref[blk] = part.astype(BF16)

    out = pl.BlockSpec((per_step, m, bw), lambda j: (j, 0, 0))
    return _call(
        body, name=name, grid=(n // tn,),
        in_specs=[pl.BlockSpec((t, m), lambda j: (0, 0)), pl.BlockSpec((t, tn), lambda j: (0, j))],
        out_specs=[out, out],
        out_shape=[jax.ShapeDtypeStruct((n // bw, m, bw), F32), jax.ShapeDtypeStruct((n // bw, m, bw), BF16)],
        args=(a, b), carry=carry)


def _ff1_bwd_norms(d_f1, w_ff1b, dy, x2, g3, m, g2, carry=None):
    t, k = d_f1.shape
    d = x2.shape[1]
    bw = w_ff1b.shape[2]
    per_step = 2
    tt, tk = _tile(t, 1024), per_step * bw
    nk = k // tk

    def body(a_ref, w_ref, dy_ref, x2_ref, g3_ref, m_ref, g2_ref, dx2_ref, dm_ref, dg3_ref, dg2_ref, acc):
        i, kk = pl.program_id(0), pl.program_id(1)

        @pl.when(kk == 0)
        def _():
            acc[...] = jnp.zeros_like(acc)

        @pl.when((i == 0) & (kk == 0))
        def _():
            dg3_ref[...] = jnp.zeros_like(dg3_ref)
            dg2_ref[...] = jnp.zeros_like(dg2_ref)

        a_tile = a_ref[...]
        for b in range(per_step):
            acc[...] += _dot_nt(a_tile[:, b * bw:(b + 1) * bw], w_ref[b])

        @pl.when(kk == nk - 1)
        def _():
            def tail(rows):
                xhat, r3 = _rms_hat(x2_ref[rows, :])
                dx, dg3 = _rms_bwd(acc[rows, :], xhat, r3, g3_ref[...])
                dx2 = dy_ref[rows, :].astype(F32) + dx
                dx2_ref[rows, :] = dx2
                dg3_ref[...] += dg3
                mhat, r2 = _rms_hat(m_ref[rows, :])
                dm, dg2 = _rms_bwd(dx2, mhat, r2, g2_ref[...])
                dm_ref[rows, :] = dm.astype(BF16)
                dg2_ref[...] += dg2

            _row_chunks(tt, tail)

    row = pl.BlockSpec((tt, d), lambda i, kk: (i, 0))
    vec = pl.BlockSpec((1, d), lambda i, kk: (0, 0))
    return _call(
        body, name="ff1_bwd_norms", grid=(t // tt, nk),
        in_specs=[
            pl.BlockSpec((tt, tk), lambda i, kk: (i, kk)),
            pl.BlockSpec((per_step, d, bw), lambda i, kk: (kk, 0, 0)),
            row, row, vec, row, vec,
        ],
        out_specs=[row, row, vec, vec],
        out_shape=[
            jax.ShapeDtypeStruct((t, d), F32),
            jax.ShapeDtypeStruct((t, d), BF16),
            jax.ShapeDtypeStruct((1, d), F32),
            jax.ShapeDtypeStruct((1, d), F32),
        ],
        scratch_shapes=[pltpu.VMEM((tt, d), F32)],
        args=(d_f1, w_ff1b, dy, x2, g3, m, g2), carry=carry)


def _wo_bwd_mix(dm, w_o, br_a, br_b, proj, b_gate, ga_block, gb_block, carry=None):
    t, d = dm.shape
    tt, tn = _tile(t, 1024), 512
    nj = d // tn

    def body(dm_ref, w_ref, bra_ref, brb_ref, ga_ref, gb_ref, ba_ref, bb_ref,
             dbra_ref, dbrb_ref, dga_ref, dgb_ref, dba_ref, dbb_ref):
        i = pl.program_id(1)

        @pl.when(i == 0)
        def _():
            dba_ref[...] = jnp.zeros_like(dba_ref)
            dbb_ref[...] = jnp.zeros_like(dbb_ref)

        d_mix = _dot_nt(dm_ref[...], w_ref[...])
        ga = _sig(ga_ref[...].astype(F32) + ba_ref[...])
        gb = _sig(gb_ref[...].astype(F32) + bb_ref[...])
        dbra_ref[...] = (d_mix * ga).astype(BF16)
        dbrb_ref[...] = (d_mix * gb).astype(BF16)
        dga = d_mix * bra_ref[...].astype(F32) * (ga * (1.0 - ga))
        dgb = d_mix * brb_ref[...].astype(F32) * (gb * (1.0 - gb))
        dga_ref[...] = dga.astype(BF16)
        dgb_ref[...] = dgb.astype(BF16)
        dba_ref[...] += jnp.sum(dga, axis=0, keepdims=True)
        dbb_ref[...] += jnp.sum(dgb, axis=0, keepdims=True)

    blk = pl.BlockSpec((tt, tn), lambda j, i: (i, j))
    vec = pl.BlockSpec((1, tn), lambda j, i: (0, j))
    return _call(
        body, name="wo_bwd_mix", grid=(nj, t // tt),
        in_specs=[
            pl.BlockSpec((tt, d), lambda j, i: (i, 0)),
            pl.BlockSpec((tn, d), lambda j, i: (j, 0)),
            blk, blk,
            pl.BlockSpec((tt, tn), lambda j, i: (i, ga_block + j)),
            pl.BlockSpec((tt, tn), lambda j, i: (i, gb_block + j)),
            vec,
            pl.BlockSpec((1, tn), lambda j, i: (0, nj + j)),
        ],
        out_specs=[blk, blk, blk, blk, vec, vec],
        out_shape=[jax.ShapeDtypeStruct((t, d), BF16)] * 4 + [jax.ShapeDtypeStruct((1, d), F32)] * 2,
        args=(dm, w_o, br_a, br_b, proj, proj, b_gate, b_gate), carry=carry)


def _lru_up_bwd(d_br_a, w_lru_up, proj, h, g_block, carry=None):
    t, d = d_br_a.shape
    tt, tn = _tile(t, 1024), 512

    def body(a_ref, w_ref, g_ref, h_ref, dh_ref, dg_ref):
        d_y = _dot_nt(a_ref[...], w_ref[...])
        gel, gel_grad = _gelu_and_grad(g_ref[...].astype(F32))
        dh_ref[...] = d_y * gel
        dg_ref[...] = (d_y * h_ref[...] * gel_grad).astype(BF16)

    blk = pl.BlockSpec((tt, tn), lambda i, j: (i, j))
    return _call(
        body, name="lru_up_bwd", grid=(t // tt, d // tn),
        in_specs=[
            pl.BlockSpec((tt, d), lambda i, j: (i, 0)),
            pl.BlockSpec((tn, d), lambda i, j: (j, 0)),
            pl.BlockSpec((tt, tn), lambda i, j: (i, g_block + j)),
            blk,
        ],
        out_specs=[blk, blk],
        out_shape=[jax.ShapeDtypeStruct((t, d), F32), jax.ShapeDtypeStruct((t, d), BF16)],
        args=(d_br_a, w_lru_up, proj, h), carry=carry)


def _lru_bwd(dh, h, saved, proj, conv_w, w_a, w_x, lam, carry=None):
    t, dr = dh.shape
    cb = LRU_CB
    hd = LRU_HEAD_DIM
    per = cb // hd
    tc = _tile(t, 256)
    ncb, ntc = dr // cb, t // tc

    def body(dh_ref, h_ref, hp_ref, saved_ref, xp_ref, cw_ref, wa_ref, wx_ref,
             lam_ref, dxp_ref, dwa_ref, dba_ref, dwx_ref, dbx_ref, dlam_ref, dcw_ref, dcb_ref,
             nextd_s, anext_s, gnext_s, tmp_s, wa_s, wx_s):
        c = pl.program_id(1)
        rc = ntc - 1 - c

        @pl.when(c == 0)
        def _():
            nextd_s[...] = jnp.zeros_like(nextd_s)
            anext_s[...] = jnp.zeros_like(anext_s)
            gnext_s[...] = jnp.zeros_like(gnext_s)
            for ref in (dwa_ref, dba_ref, dwx_ref, dbx_ref, dlam_ref, dcw_ref, dcb_ref):
                ref[...] = jnp.zeros_like(ref)
            _fill_block_diag(wa_ref, wa_s)
            _fill_block_diag(wx_ref, wx_s)

        xc, r, i, a, mult = [saved_ref[:, k * cb:(k + 1) * cb] for k in range(N_LRU_SAVED)]
        wa, wx, lam = wa_s[...], wx_s[...], lam_ref[...]
        xcb = xc.astype(BF16)
        sp = _softplus_neg(lam)
        row = lax.broadcasted_iota(jnp.int32, xc.shape, 0)
        h = h_ref[...]
        hp = jnp.where(rc == 0, 0.0, hp_ref[...])
        hprev = jnp.where(row >= 1, pltpu.roll(h, 1, 0), pltpu.roll(hp, 1, 0))

        def up(v, nv, j):
            return jnp.where(row < tc - j, pltpu.roll(v, tc - j, 0), nv)

        av, bv = _scan_rows(up(a, anext_s[...], 1), dh_ref[...], reverse=True)
        gt = av * gnext_s[...] + bv
        tmp_s[...] = gt
        gnext_s[...] = tmp_s[0:1, :]
        tmp_s[...] = a
        anext_s[...] = tmp_s[0:1, :]

        da = gt * hprev
        ixc = i * xc
        d_mult = gt * ixc
        d_i = gt * mult * xc
        d_xc = gt * mult * i
        d_log_a = da * a - d_mult * (a * a) / mult
        d_pre_r = (d_log_a * ((-LRU_C) * sp)) * (r * (1.0 - r))
        d_pre_i = d_i * (i * (1.0 - i))
        d_sp = jnp.sum(d_log_a * ((-LRU_C) * r), axis=0, keepdims=True)
        dlam_ref[...] += d_sp * (-1.0 / (1.0 + jnp.exp(lam)))
        dpr = d_pre_r.astype(BF16)
        dpi = d_pre_i.astype(BF16)
        dba_ref[...] += jnp.sum(d_pre_r, axis=0, keepdims=True)
        dbx_ref[...] += jnp.sum(d_pre_i, axis=0, keepdims=True)
        pa = _dot_tn(xcb, dpr)
        px = _dot_tn(xcb, dpi)
        for k in range(per):
            dwa_ref[k] += pa[k * hd:(k + 1) * hd, k * hd:(k + 1) * hd]
            dwx_ref[k] += px[k * hd:(k + 1) * hd, k * hd:(k + 1) * hd]
        d_xc = d_xc + _dot_nt(dpr, wa) + _dot_nt(dpi, wx)

        nxt = nextd_s[...]
        xp = xp_ref[...].astype(F32)
        dxp = cw_ref[3:4, :] * d_xc
        dcw_ref[3:4, :] += jnp.sum(xp * d_xc, axis=0, keepdims=True)
        for j in (1, 2, 3):
            uj = up(d_xc, pltpu.roll(nxt, tc - j, 0), j)
            dxp = dxp + cw_ref[3 - j:4 - j, :] * uj
            dcw_ref[3 - j:4 - j, :] += jnp.sum(xp * uj, axis=0, keepdims=True)
        dcb_ref[...] += jnp.sum(d_xc, axis=0, keepdims=True)
        nextd_s[...] = d_xc
        dxp_ref[...] = dxp.astype(BF16)

    vec = pl.BlockSpec((1, cb), lambda j, c: (0, j))
    blk = pl.BlockSpec((tc, cb), lambda j, c: (ntc - 1 - c, j))
    mat = pl.BlockSpec((per, hd, hd), lambda j, c: (j, 0, 0))
    cwb = pl.BlockSpec((4, cb), lambda j, c: (0, j))
    return _call(
        body, name="lru_bwd", grid=(ncb, ntc),
        in_specs=[
            blk, blk,
            pl.BlockSpec((tc, cb), lambda j, c: (jnp.maximum(ntc - 2 - c, 0), j)),
            pl.BlockSpec((tc, N_LRU_SAVED * cb), lambda j, c: (ntc - 1 - c, j)),
            blk, cwb, mat, mat, vec,
        ],
        out_specs=[blk, mat, vec, mat, vec, vec, cwb, vec],
        out_shape=[
            jax.ShapeDtypeStruct((t, dr), BF16),
            jax.ShapeDtypeStruct(w_a.shape, F32),
            jax.ShapeDtypeStruct((1, dr), F32),
            jax.ShapeDtypeStruct(w_x.shape, F32),
            jax.ShapeDtypeStruct((1, dr), F32),
            jax.ShapeDtypeStruct((1, dr), F32),
            jax.ShapeDtypeStruct((4, dr), F32),
            jax.ShapeDtypeStruct((1, dr), F32),
        ],
        scratch_shapes=[
            pltpu.VMEM((tc, cb), F32),
            pltpu.VMEM((1, cb), F32),
            pltpu.VMEM((1, cb), F32),
            pltpu.VMEM((tc, cb), F32),
            pltpu.VMEM((cb, cb), BF16),
            pltpu.VMEM((cb, cb), BF16),
        ],
        args=(dh, h, h, saved, proj, conv_w, w_a, w_x, lam), carry=carry)


def _pool_bwd(d_br_b, w_pool_upb, p, pool_w, pool_scale):
    t, d = d_br_b.shape
    nwb, dp, _ = w_pool_upb.shape
    tc = _tile(t, 256)
    ntc = t // tc
    ng = len(POOL_WINDOWS)

    def body(db_ref, wu_ref, p_ref, w_ref, sc_ref, dx_ref, dw_ref, dsc_ref, nz, n2, n4, n8, dp_s, dy_s):
        c = pl.program_id(0)
        rc = ntc - 1 - c

        @pl.when(c == 0)
        def _():
            for s in (nz, n2, n4, n8):
                s[...] = jnp.zeros_like(s)
            dw_ref[...] = jnp.zeros_like(dw_ref)
            dsc_ref[...] = jnp.zeros_like(dsc_ref)

        wu = jnp.concatenate([wu_ref[b] for b in range(nwb)], axis=1)
        dy_s[...] = _dot_nt(db_ref[...], wu)
        for g in range(ng):
            sl = slice(g * POOL_GROUP_DIM, (g + 1) * POOL_GROUP_DIM)
            pg = p_ref[:, sl]
            dyg = dy_s[:, sl]
            wg = w_ref[g].astype(BF16)
            q = _dot_nn(pg, wg)
            dsc_ref[:, sl] += jnp.sum(dyg * q, axis=0, keepdims=True)
            dpw = (dyg * sc_ref[:, sl]).astype(BF16)
            dw_ref[g] += _dot_tn(pg, dpw)
            dp_s[:, sl] = _dot_nt(dpw, wg)

        dpv = dp_s[...]
        row = lax.broadcasted_iota(jnp.int32, dpv.shape, 0)
        col = lax.broadcasted_iota(jnp.int32, dpv.shape, 1)
        win = _pool_select(col, POOL_WINDOWS)
        cnt = jnp.minimum(rc * tc + row + 1, win).astype(F32)
        z = dpv / cnt

        def up(v, nv, j):
            return jnp.where(row < tc - j, pltpu.roll(v, tc - j, 0), pltpu.roll(nv[...], tc - j, 0))

        u2 = z + up(z, nz, 1)
        u4 = u2 + up(u2, n2, 2)
        u8 = u4 + up(u4, n4, 4)
        u16 = u8 + up(u8, n8, 8)
        nz[...] = z
        n2[...] = u2
        n4[...] = u4
        n8[...] = u8
        dx_ref[...] = (_pool_select(col, (u2, u4, u8, u16)) - dpv).astype(BF16)

    blk = pl.BlockSpec((tc, dp), lambda c: (ntc - 1 - c, 0))
    full_w = pl.BlockSpec(pool_w.shape, lambda c: (0, 0, 0))
    vec = pl.BlockSpec((1, dp), lambda c: (0, 0))
    return _call(
        body, name="pool_bwd", grid=(ntc,),
        in_specs=[pl.BlockSpec((tc, d), lambda c: (ntc - 1 - c, 0)),
                  pl.BlockSpec(w_pool_upb.shape, lambda c: (0, 0, 0)), blk, full_w, vec],
        out_specs=[blk, full_w, vec],
        out_shape=[
            jax.ShapeDtypeStruct((t, dp), BF16),
            jax.ShapeDtypeStruct(pool_w.shape, F32),
            jax.ShapeDtypeStruct((1, dp), F32),
        ],
        scratch_shapes=[pltpu.VMEM((tc, dp), F32)] * 6,
        args=(d_br_b, w_pool_upb, p, pool_w, pool_scale))[0]


def _win_bwd_norm(parts, w_int, dx2, x, g1, carry=None):
    t, d = x.shape
    tk = 512
    tt = _tile(t, 1024)
    bounds = []
    k0 = 0
    for part in parts:
        assert part.shape[1] % tk == 0
        bounds.append((k0, k0 + part.shape[1] // tk))
        k0 += part.shape[1] // tk
    nk = k0
    assert nk * tk == w_int.shape[0]
    np_ = len(parts)

    def body(*refs):
        p_refs = refs[:np_]
        w_ref, dx2_ref, x_ref, g_ref, gx_ref, dg_ref, acc = refs[np_:]
        i, kk = pl.program_id(0), pl.program_id(1)

        @pl.when(kk == 0)
        def _():
            acc[...] = jnp.zeros_like(acc)

        @pl.when((i == 0) & (kk == 0))
        def _():
            dg_ref[...] = jnp.zeros_like(dg_ref)

        for (lo, hi), p_ref in zip(bounds, p_refs):
            @pl.when((kk >= lo) & (kk < hi))
            def _(p_ref=p_ref):
                acc[...] += _dot_nn(p_ref[...], w_ref[...])

        @pl.when(kk == nk - 1)
        def _():
            def tail(rows):
                xhat, r = _rms_hat(x_ref[rows, :])
                dx, dg = _rms_bwd(acc[rows, :], xhat, r, g_ref[...])
                gx_ref[rows, :] = dx2_ref[rows, :] + dx
                dg_ref[...] += dg

            _row_chunks(tt, tail)

    def part_spec(lo, hi):
        return pl.BlockSpec((tt, tk), lambda i, kk: (i, jnp.clip(kk - lo, 0, hi - lo - 1)))

    row = pl.BlockSpec((tt, d), lambda i, kk: (i, 0))
    vec = pl.BlockSpec((1, d), lambda i, kk: (0, 0))
    return _call(
        body, name="win_bwd_norm", grid=(t // tt, nk),
        in_specs=[part_spec(lo, hi) for lo, hi in bounds]
        + [pl.BlockSpec((tk, d), lambda i, kk: (kk, 0)), row, row, vec],
        out_specs=[row, vec],
        out_shape=[jax.ShapeDtypeStruct((t, d), F32), jax.ShapeDtypeStruct((1, d), F32)],
        scratch_shapes=[pltpu.VMEM((tt, d), F32)],
        args=(*parts, w_int, dx2, x, g1), carry=carry)


def _adam_math(w, g, m, v):
    m = ADAM_B1 * m + (1.0 - ADAM_B1) * g
    v = ADAM_B2 * v + (1.0 - ADAM_B2) * (g * g)
    m_hat = m / (1.0 - ADAM_B1 ** ADAM_STEP)
    v_hat = v / (1.0 - ADAM_B2 ** ADAM_STEP)
    delta = -ADAM_LR * (m_hat / (jnp.sqrt(v_hat) + ADAM_EPS) + ADAM_WD * w)
    return delta, m, v


def _adamw_big(ws, gs, ms, vs):
    n = len(ws)
    nb = 4
    pair = [isinstance(g, tuple) for g in gs]

    def body(*refs):
        p = 0
        ins = []
        for a in range(n):
            k = 5 if pair[a] else 4
            ins.append(refs[p:p + k])
            p += k
        for a in range(n):
            g_out, d_ref, nm_ref, nv_ref = refs[p + 4 * a:p + 4 * a + 4]
            if pair[a]:
                w_ref, own_ref, recv_ref, m_ref, v_ref = ins[a]
                g = own_ref[...]
                for k in range(3):
                    g = g + recv_ref[k].astype(F32)
            else:
                w_ref, g_ref, m_ref, v_ref = ins[a]
                g = g_ref[...]
            dl, m, v = _adam_math(w_ref[...], g, m_ref[...], v_ref[...])
            g_out[...] = g
            d_ref[...] = dl
            nm_ref[...] = m
            nv_ref[...] = v

    in_specs, out_specs, out_shape, args = [], [], [], []
    for a, (w, g, m, v) in enumerate(zip(ws, gs, ms, vs)):
        rows, cols = w.shape
        blk = pl.BlockSpec((rows // nb, cols), lambda i: (i, 0))
        if pair[a]:
            in_specs += [blk, pl.BlockSpec((None, rows // nb, cols), lambda i: (0, i, 0)),
                         pl.BlockSpec((3, rows // nb, cols), lambda i: (0, i, 0)), blk, blk]
            args += [w, g[0], g[1], m, v]
        else:
            in_specs += [blk] * 4
            args += [w, g, m, v]
        out_specs += [blk] * 4
        out_shape += [jax.ShapeDtypeStruct(w.shape, F32)] * 4
    outs = _call(body, name="adamw_big", grid=(nb,), in_specs=in_specs, out_specs=out_specs,
                 out_shape=out_shape, args=args)[0]
    return [tuple(outs[4 * a:4 * a + 4]) for a in range(n)]


SMALL_ORDER = ("norm_mix_pre", "norm_mix_post", "norm_mlp_pre", "norm_mlp_post", "b_gate", "conv_w", "conv_b",
               "lru_w_a", "lru_b_a", "lru_w_x", "lru_b_x", "lru_lambda", "pool_w", "pool_scale")
VEC_ROW = dict(norm_mix_pre=0, norm_mix_post=1, norm_mlp_pre=2, norm_mlp_post=3, conv_b=6, lru_b_a=7,
               lru_b_x=8, lru_lambda=9)
ROW_B_GATE, ROW_POOL_SCALE, ROW_CONV_W, ROW_LOSS, N_VEC_ROWS = 4, 10, 11, 15, 16


def _adamw_small(vec_parts, g_pool, g_wa, g_wx, me, params):
    d = vec_parts.shape[2]
    names = SMALL_ORDER
    n = len(names)
    cw_cols = params["conv_w"][0].shape[2]

    def body(me_ref, vec_ref, vecc_ref, gp_ref, gwa_ref, gwx_ref, *refs):
        wmv = refs[:3 * n]
        loss_ref = refs[3 * n]
        outs = refs[3 * n + 1:3 * n + 1 + 4 * n]
        vs, vsc = refs[3 * n + 1 + 4 * n:]
        acc, accc = vec_ref[0], vecc_ref[0]
        for k in range(1, N_DEV):
            acc = acc + vec_ref[k]
            accc = accc + vecc_ref[k]
        vs[...] = acc
        vsc[...] = accc
        loss_ref[...] = vs[ROW_LOSS:ROW_LOSS + 1, 0:128]

        def upd(a, g, idx):
            w_ref, m_ref, v_ref = wmv[3 * a:3 * a + 3]
            g_ref, d_ref, nm_ref, nv_ref = outs[4 * a:4 * a + 4]
            dl, m, v = _adam_math(w_ref[idx], g, m_ref[idx], v_ref[idx])
            g_ref[idx] = g
            d_ref[idx] = dl
            nm_ref[idx] = m
            nv_ref[idx] = v

        for a, name in enumerate(names):
            if name in VEC_ROW:
                r = VEC_ROW[name]
                upd(a, vs[r:r + 1, :], (slice(None), slice(None)))
            elif name == "b_gate":
                for half in range(2):
                    r = ROW_B_GATE + half
                    upd(a, vs[r:r + 1, :], (slice(None), slice(half * d, (half + 1) * d)))
            elif name == "pool_scale":
                width = params[name][0].shape[1]
                upd(a, vs[ROW_POOL_SCALE:ROW_POOL_SCALE + 1, 0:width], (slice(None), slice(None)))
            elif name == "conv_w":
                upd(a, vsc[ROW_CONV_W:ROW_CONV_W + 4, :], (0,))
            elif name == "pool_w":
                upd(a, gp_ref[...], (Ellipsis,))
            elif name == "lru_w_a":
                upd(a, gwa_ref[...], (Ellipsis,))
            elif name == "lru_w_x":
                upd(a, gwx_ref[...], (Ellipsis,))
            else:
                raise ValueError(name)

    def whole(shape):
        nd = len(shape)
        return pl.BlockSpec(tuple(shape), lambda i, me_ref: (0,) * nd)

    in_specs = [
        whole(vec_parts.shape),
        pl.BlockSpec((N_DEV, N_VEC_ROWS, cw_cols), lambda i, me_ref: (0, 0, me_ref[0])),
        whole(g_pool.shape), whole(g_wa.shape), whole(g_wx.shape),
    ]
    args = [vec_parts, vec_parts, g_pool, g_wa, g_wx]
    out_specs = [whole((1, 128))]
    out_shape = [jax.ShapeDtypeStruct((1, 128), F32)]
    for name in names:
        for arr in params[name]:
            in_specs.append(whole(arr.shape))
            args.append(arr)
        shp = params[name][0].shape
        out_specs += [whole(shp)] * 4
        out_shape += [jax.ShapeDtypeStruct(shp, F32)] * 4
    grid_spec = pltpu.PrefetchScalarGridSpec(
        num_scalar_prefetch=1, grid=(1,), in_specs=in_specs, out_specs=out_specs,
        scratch_shapes=[pltpu.VMEM((N_VEC_ROWS, d), F32), pltpu.VMEM((N_VEC_ROWS, cw_cols), F32)])
    outs = pl.pallas_call(
        body, name="adamw_small", grid_spec=grid_spec, out_shape=out_shape,
        compiler_params=pltpu.CompilerParams(
            dimension_semantics=("arbitrary",), vmem_limit_bytes=V7X_VMEM_LIMIT_BYTES),
    )(me, *_in_hbm(args))
    return outs[0], {name: tuple(outs[1 + 4 * a:5 + 4 * a]) for a, name in enumerate(names)}


def _rs_sum(fulls, recvs, shard_ids, slot_ids, name):
    n = len(fulls)

    def body(sh_ref, sl_ref, *refs):
        s = pl.program_id(0)
        for a in range(n):
            full_ref, recv_ref = refs[2 * a], refs[2 * a + 1]
            own_ref, send_ref = refs[2 * n + 2 * a], refs[2 * n + 2 * a + 1]
            v = full_ref[...] + recv_ref[...].astype(F32)

            @pl.when(s == 0)
            def _(own_ref=own_ref, v=v):
                own_ref[...] = v

            @pl.when(s > 0)
            def _(send_ref=send_ref, v=v):
                send_ref[...] = v.astype(send_ref.dtype)

    in_specs, out_specs, out_shape, args = [], [], [], []
    for full, recv in zip(fulls, recvs):
        r, rest = recv.shape[1], tuple(recv.shape[2:])
        zeros = (0,) * len(rest)
        in_specs += [
            pl.BlockSpec((r,) + rest, lambda s, sh, sl, zeros=zeros: (sh[s],) + zeros),
            pl.BlockSpec((None, r) + rest, lambda s, sh, sl, zeros=zeros: (sl[s], 0) + zeros),
        ]
        out_specs += [
            pl.BlockSpec((None, r) + rest, lambda s, sh, sl, zeros=zeros: (0, 0) + zeros),
            pl.BlockSpec((None, r) + rest, lambda s, sh, sl, zeros=zeros: (jnp.maximum(s - 1, 0), 0) + zeros),
        ]
        out_shape += [jax.ShapeDtypeStruct((1, r) + rest, F32), jax.ShapeDtypeStruct((3, r) + rest, recv.dtype)]
        args += [full, recv]
    grid_spec = pltpu.PrefetchScalarGridSpec(
        num_scalar_prefetch=2, grid=(4,), in_specs=in_specs, out_specs=out_specs)
    outs = pl.pallas_call(
        body,
        name=name,
        grid_spec=grid_spec,
        out_shape=out_shape,
        compiler_params=pltpu.CompilerParams(
            dimension_semantics=("arbitrary",), vmem_limit_bytes=V7X_VMEM_LIMIT_BYTES),
    )(shard_ids, slot_ids, *_in_hbm(args))
    return [(outs[2 * a], outs[2 * a + 1]) for a in range(n)]


def _finals(pairs, name, carry=None):
    nb = 4
    n = len(pairs)

    def body(*refs):
        for a in range(n):
            own_ref, recv_ref = refs[2 * a], refs[2 * a + 1]
            acc = own_ref[...]
            for k in range(3):
                acc = acc + recv_ref[k].astype(F32)
            refs[2 * n + a][...] = acc

    in_specs, out_specs, out_shape, args = [], [], [], []
    for own, recv in pairs:
        _, rows, cols = own.shape
        in_specs += [pl.BlockSpec((None, rows // nb, cols), lambda i: (0, i, 0)),
                     pl.BlockSpec((3, rows // nb, cols), lambda i: (0, i, 0))]
        args += [own, recv]
        out_specs.append(pl.BlockSpec((rows // nb, cols), lambda i: (i, 0)))
        out_shape.append(jax.ShapeDtypeStruct((rows, cols), F32))
    return _call(body, name=name, grid=(nb,), in_specs=in_specs, out_specs=out_specs,
                 out_shape=out_shape, args=args, carry=carry)


def _rs_sums(fulls_f32, recv1, tag):
    x, y, c = _place()
    qs = jnp.stack([2 * x + y, 2 * (1 - x) + y, 2 * x + (1 - y), 2 * (1 - x) + (1 - y)]).astype(jnp.int32)
    shard_ids = 2 * qs + c
    return _rs_sum(fulls_f32, recv1, shard_ids, qs, "rs_sum_" + tag)


def _rs_level1(fulls_f32, fulls_send, tag):
    recv1 = _run_plan(_rs_sibling_plan(fulls_send), "rs_sibling_" + tag)
    return _rs_sums(fulls_f32, recv1, tag)


def _rows(g):
    return g.reshape(g.shape[0] * g.shape[1], g.shape[2])


def kernel(x, norm_mix_pre, norm_mix_post, norm_mlp_pre, norm_mlp_post, w_in, b_gate, conv_w, conv_b, lru_w_a, lru_b_a, lru_w_x, lru_b_x, lru_lambda, pool_w, pool_scale, w_lru_up, w_pool_up, w_o, w_ff1, w_ff2, loss_target, m_norm_mix_pre, m_norm_mix_post, m_norm_mlp_pre, m_norm_mlp_post, m_w_in, m_b_gate, m_conv_w, m_conv_b, m_lru_w_a, m_lru_b_a, m_lru_w_x, m_lru_b_x, m_lru_lambda, m_pool_w, m_pool_scale, m_w_lru_up, m_w_pool_up, m_w_o, m_w_ff1, m_w_ff2, v_norm_mix_pre, v_norm_mix_post, v_norm_mlp_pre, v_norm_mlp_post, v_w_in, v_b_gate, v_conv_w, v_conv_b, v_lru_w_a, v_lru_b_a, v_lru_w_x, v_lru_b_x, v_lru_lambda, v_pool_w, v_pool_scale, v_w_lru_up, v_w_pool_up, v_w_o, v_w_ff1, v_w_ff2):
    t, d = x.shape[1], x.shape[2]
    d_rnn = conv_b.shape[1]
    d_pool = pool_scale.shape[1]
    per = LRU_CB // LRU_HEAD_DIM
    xi, yi, ci = _place()
    me = 4 * xi + 2 * yi + ci

    x2d = x[0]
    tgt = loss_target[0]

    s_in = w_in[0].T.astype(BF16)
    s_lu = w_lru_up[0].astype(BF16)
    s_pu = w_pool_up[0].astype(BF16)
    s_o = w_o[0].astype(BF16)
    s_f1 = w_ff1[0].astype(BF16)
    s_f2 = w_ff2[0].astype(BF16)
    s_cw = jnp.pad(conv_w[0], ((0, 4), (0, 0)))

    g_in, g_cw = _run_plan(_ag_plan([s_in, s_cw]), "ag_w_in")
    w_int = _rows(g_in)
    conv_w_full = jnp.transpose(g_cw[:, :4, :], (1, 0, 2)).reshape(4, d_rnn)

    wa_bd, wx_bd = lru_w_a[0], lru_w_x[0]
    pw = pool_w[0]
    pw_bf = pw.astype(BF16)

    pool_block = (2 * d_rnn) // d_pool
    ga_block = (2 * d_rnn + d_pool) // 512
    gb_block = ga_block + d // 512
    g_block = d_rnn // 512

    r_f1, r_f2 = s_f1.shape[0], s_f2.shape[0]
    f1_cut = r_f1 // 4
    f2_cut = (3 * r_f2) // 8
    plan = _join([_ag_plan([s_lu, s_pu]), _ag_plan([s_f1], pieces=[(0, f1_cut)])])
    (proj, h1), got = _norm_proj(x2d, norm_mix_pre, w_int, carry=plan)
    (g_lu, g_pu), (g_f1,) = plan.split(got)
    plan = _join([_ag_plan([s_f1], pieces=[(f1_cut, r_f1 - f1_cut)], bufs=[g_f1]), _ag_plan([s_o])])
    (y_lru, h, lru_saved), got = _lru_fwd(
        proj, conv_w_full, conv_b, wa_bd, lru_b_a, wx_bd, lru_b_x, lru_lambda, carry=plan)
    (g_f1,), (g_o,) = plan.split(got)
    w_lu, w_og = _rows(g_lu), _rows(g_o)
    y_pool, p = _pool_fwd(proj, pw_bf, pool_scale, pool_block)
    (br_a, br_b, mix), (g_f2,) = _branch_mix(
        y_lru, y_pool, w_lu, g_pu, proj, b_gate, ga_block, gb_block,
        carry=_ag_plan([s_f2], pieces=[(0, f2_cut)]))
    (m, x2, h3), _ = _wo_norm(mix, w_og, x2d, norm_mix_post, norm_mlp_pre)
    (rf,), (g_f2,) = _ff1(
        h3, g_f1, carry=_ag_plan([s_f2], pieces=[(f2_cut, r_f2 - f2_cut)], bufs=[g_f2]))
    w_f2 = _rows(g_f2)
    dy, df, dg4, loss_part = _ff2_loss(rf, w_f2, x2, norm_mlp_post, tgt)

    (gw_ff2_32, gw_ff2_16), _ = _wgrad(rf, df, "wgrad_ff2", square_a=True)
    (d_f1,), r1_ff2 = _ff2_bwd(df, w_f2, rf, carry=_rs_sibling_plan([gw_ff2_16]))
    ((own_ff2, send_ff2),) = _rs_sums([gw_ff2_32], r1_ff2, "ff2")
    cut2 = (5 * send_ff2.shape[1]) // 16
    (gw_ff1_32, gw_ff1_16), (r2_ff2,) = _wgrad_cols(
        h3, d_f1, s_f1.shape[1], s_f1.shape[1], "wgrad_ff1",
        carry=_rs_chips_plan([send_ff2], pieces=[(0, cut2)]))
    plan = _join([_rs_chips_plan([send_ff2], pieces=[(cut2, send_ff2.shape[1] - cut2)], bufs=[r2_ff2]),
                  _rs_sibling_plan([gw_ff1_16])])
    (dx2, dm, dg3, dg2), got = _ff1_bwd_norms(d_f1, g_f1, dy, x2, norm_mlp_pre, m, norm_mix_post, carry=plan)
    (r2_ff2,), r1_ff1 = plan.split(got)
    ((own_ff1, send_ff1),) = _rs_sums([gw_ff1_32], r1_ff1, "ff1")
    own_ff1, send_ff1 = own_ff1.reshape((1,) + s_f1.shape), send_ff1.reshape((3,) + s_f1.shape)
    cut = send_ff1.shape[1] // 4
    (d_br_a, d_br_b, p_ga, p_gb, dbg_a, dbg_b), (r2_ff1,) = _wo_bwd_mix(
        dm, w_og, br_a, br_b, proj, b_gate, ga_block, gb_block,
        carry=_rs_chips_plan([send_ff1], pieces=[(0, cut)]))
    ((gw_o_32, gw_o_16), (gw_lu_32, gw_lu_16), (gw_pu_32, gw_pu_16)), (r2_ff1,) = _wgrad_group(
        [(mix, dm, None), (y_lru, d_br_a, None), (y_pool, d_br_b, s_pu.shape[1])], "wgrad_mid",
        carry=_rs_chips_plan([send_ff1], pieces=[(cut, cut)], bufs=[r2_ff1]))
    (dh, p_g), r1_mid = _lru_up_bwd(
        d_br_a, w_lu, proj, h, g_block,
        carry=_rs_sibling_plan([gw_o_16, gw_lu_16, gw_pu_16]))
    mid = _rs_sums([gw_o_32, gw_lu_32, gw_pu_32], r1_mid, "mid")
    plan = _join([_rs_chips_plan([send_ff1], pieces=[(2 * cut, send_ff1.shape[1] - 2 * cut)], bufs=[r2_ff1]),
                  _rs_chips_plan([mid[0][1]])])
    (p_x, dwa, db_a, dwx, db_x, dlam, dconv_w, dconv_b), got = _lru_bwd(
        dh, h, lru_saved, proj, conv_w_full, wa_bd, wx_bd, lru_lambda, carry=plan)
    (r2_ff1,), (r2_o,) = plan.split(got)
    p_p, dpool_w, dpool_scale = _pool_bwd(d_br_b, g_pu, p, pw, pool_scale)
    parts = [p_x, p_g, p_p, p_ga, p_gb]
    gw_in, (r2_lu, r2_pu) = _wgrad_parts(
        parts, h1, "wgrad_in", carry=_rs_chips_plan([mid[1][1], mid[2][1]]))
    r2_mid = [r2_o, r2_lu, r2_pu]
    tail = _rs_level1([gw_in[0], dpool_w.reshape(N_DEV, -1, POOL_GROUP_DIM), dwa, dwx],
                      [gw_in[1], dpool_w.reshape(N_DEV, -1, POOL_GROUP_DIM), dwa, dwx], "in")
    (grad_x, dg1), r2_tail = _win_bwd_norm(parts, w_int, dx2, x2d, norm_mix_pre,
                                           carry=_rs_chips_plan([s for _, s in tail]))

    def flat2(a):
        return a.reshape(a.shape[0], -1, a.shape[-1])

    fin_small, _ = _finals([
        (flat2(tail[1][0]), flat2(r2_tail[1])), (flat2(tail[2][0]), flat2(r2_tail[2])),
        (flat2(tail[3][0]), flat2(r2_tail[3])),
    ], "rs_finals_small")

    def pad_row(a):
        return jnp.pad(a, ((0, 0), (0, d - a.shape[1])))

    vecs = jnp.concatenate([dg1, dg2, dg3, dg4, dbg_a, dbg_b, dconv_b, db_a, db_x, dlam,
                            pad_row(dpool_scale), dconv_w, pad_row(loss_part)], axis=0)
    assert vecs.shape[0] == N_VEC_ROWS
    vec_parts, g_pool, g_wa, g_wx = _run_plan(_ag_plan([vecs] + fin_small), "ag_tail")

    big_names = ["w_in", "w_lru_up", "w_pool_up", "w_o", "w_ff1", "w_ff2"]
    big_w = [w_in[0].T, w_lru_up[0], w_pool_up[0], w_o[0], w_ff1[0], w_ff2[0]]
    big_g = [(tail[0][0], r2_tail[0]), (mid[1][0], r2_mid[1]),
             (mid[2][0].reshape((1,) + s_pu.shape), r2_mid[2].reshape((3,) + s_pu.shape)),
             (mid[0][0], r2_mid[0]), (own_ff1, r2_ff1), (own_ff2, r2_ff2)]
    big_m = [m_w_in[0].T, m_w_lru_up[0], m_w_pool_up[0], m_w_o[0], m_w_ff1[0], m_w_ff2[0]]
    big_v = [v_w_in[0].T, v_w_lru_up[0], v_w_pool_up[0], v_w_o[0], v_w_ff1[0], v_w_ff2[0]]
    big_out = _adamw_big(big_w, big_g, big_m, big_v)
    big_out[0] = tuple(o.T for o in big_out[0])

    small = dict(
        norm_mix_pre=(norm_mix_pre, m_norm_mix_pre, v_norm_mix_pre),
        norm_mix_post=(norm_mix_post, m_norm_mix_post, v_norm_mix_post),
        norm_mlp_pre=(norm_mlp_pre, m_norm_mlp_pre, v_norm_mlp_pre),
        norm_mlp_post=(norm_mlp_post, m_norm_mlp_post, v_norm_mlp_post),
        b_gate=(b_gate, m_b_gate, v_b_gate), conv_w=(conv_w, m_conv_w, v_conv_w),
        conv_b=(conv_b, m_conv_b, v_conv_b), lru_w_a=(lru_w_a, m_lru_w_a, v_lru_w_a),
        lru_b_a=(lru_b_a, m_lru_b_a, v_lru_b_a), lru_w_x=(lru_w_x, m_lru_w_x, v_lru_w_x),
        lru_b_x=(lru_b_x, m_lru_b_x, v_lru_b_x), lru_lambda=(lru_lambda, m_lru_lambda, v_lru_lambda),
        pool_w=(pool_w, m_pool_w, v_pool_w), pool_scale=(pool_scale, m_pool_scale, v_pool_scale))
    loss_row, small_out = _adamw_small(
        vec_parts, g_pool.reshape(pool_w.shape), g_wa.reshape(lru_w_a.shape), g_wx.reshape(lru_w_x.shape),
        jnp.reshape(me, (1,)).astype(jnp.int32), small)
    grads = {n: o[0] for n, o in small_out.items()}
    delta = {n: o[1] for n, o in small_out.items()}
    new_m = {n: o[2] for n, o in small_out.items()}
    new_v = {n: o[3] for n, o in small_out.items()}

    for name, (g, dl, nm, nv) in zip(big_names, big_out):
        grads[name], delta[name], new_m[name], new_v[name] = g[None], dl[None], nm[None], nv[None]

    loss = loss_row[0, 0]
    order = ["norm_mix_pre", "norm_mix_post", "norm_mlp_pre", "norm_mlp_post", "w_in", "b_gate", "conv_w",
             "conv_b", "lru_w_a", "lru_b_a", "lru_w_x", "lru_b_x", "lru_lambda", "pool_w", "pool_scale",
             "w_lru_up", "w_pool_up", "w_o", "w_ff1", "w_ff2"]
    return (loss, grad_x[None], *[grads[n] for n in order], *[delta[n] for n in order],
            *[new_m[n] for n in order], *[new_v[n] for n in order])
```

```python
import functools
import math
import operator
import types

import jax
import jax.numpy as jnp
from jax import lax
from jax.experimental import pallas as pl
from jax.experimental.pallas import tpu as pltpu

F32 = jnp.float32
BF16 = jnp.bfloat16
NORM_EPS = 1e-6
LRU_C = 8.0
N_LRU_HEADS = 16
LRU_HEAD_DIM = 64
POOL_WINDOWS = (2, 4, 8, 16)
POOL_GROUP_DIM = 128
ADAM_LR = 0.001
ADAM_B1 = 0.9
ADAM_B2 = 0.999
ADAM_EPS = 1e-08
ADAM_WD = 0.01
ADAM_STEP = 10
N_DEV = 8
V7X_VMEM_LIMIT_BYTES = 56 * 1024 * 1024
LRU_CB = 256
MESH = pl.DeviceIdType.MESH
ANY = pl.BlockSpec(memory_space=pl.ANY)


def _tile(n, pref):
    t = min(n, pref)
    assert n % t == 0, (n, pref)
    return t


def _dot_nn(a, b):
    return lax.dot_general(a, b, (((1,), (0,)), ((), ())), preferred_element_type=F32)


def _dot_nt(a, b):
    return lax.dot_general(a, b, (((1,), (1,)), ((), ())), preferred_element_type=F32)


def _dot_tn(a, b):
    return lax.dot_general(a, b, (((0,), (0,)), ((), ())), preferred_element_type=F32)


def _row_chunks(n_rows, fn, chunk=256):
    chunk = min(chunk, n_rows)
    assert n_rows % chunk == 0

    def step(r, carry):
        fn(pl.ds(pl.multiple_of(r * chunk, chunk), chunk))
        return carry

    lax.fori_loop(0, n_rows // chunk, step, 0)


def _sig(x):
    return 1.0 / (1.0 + jnp.exp(-x))


def _rms_hat(x):
    r = lax.rsqrt(jnp.mean(x * x, axis=-1, keepdims=True) + NORM_EPS)
    return x * r, r


def _rms_bwd(dn, xhat, r, g):
    q = dn * g
    dx = r * (q - xhat * jnp.mean(q * xhat, axis=-1, keepdims=True))
    dg = jnp.sum(dn * xhat, axis=0, keepdims=True)
    return dx, dg


_GELU_K = math.sqrt(2.0 / math.pi)
_GELU_C = 0.044715


def _gelu_and_grad(g):
    t = jnp.tanh(_GELU_K * (g + _GELU_C * g * g * g))
    val = 0.5 * g * (1.0 + t)
    grad = 0.5 * (1.0 + t) + 0.5 * g * (1.0 - t * t) * (_GELU_K * (1.0 + 3.0 * _GELU_C * g * g))
    return val, grad


def _softplus_neg(lam):
    z = -lam
    e = jnp.exp(-jnp.abs(z))
    u = 1.0 + e
    d = u - 1.0
    l1p = jnp.where(d == 0.0, e, jnp.log(u) * (e / jnp.where(d == 0.0, 1.0, d)))
    return jnp.maximum(z, 0.0) + l1p


def _lru_gates(xc, wa, ba, wx, bx, lam):
    xcb = xc.astype(BF16)
    r = _sig(_dot_nn(xcb, wa) + ba)
    i = _sig(_dot_nn(xcb, wx) + bx)
    sp = _softplus_neg(lam)
    log_a = (-LRU_C) * r * sp
    a = jnp.exp(log_a)
    mult = jnp.sqrt(-jnp.tanh(log_a) * (1.0 + a * a))
    return xcb, r, i, sp, log_a, a, mult


def _place():
    return lax.axis_index("x"), lax.axis_index("y"), lax.axis_index("c")


def _ag_plan(shards, pieces=None, bufs=None):
    na = len(shards)
    n_kinds = 7

    def parts(ins, outs, sems):
        send_sems, recv_sems, local_sems = sems
        x, y, c = _place()
        me, sibling = (x, y, c), (x, y, 1 - c)
        x_nb, y_nb, diag = (1 - x, y), (x, 1 - y), (1 - x, 1 - y)
        relay_src = (c * (1 - x) + (1 - c) * x, c * y + (1 - c) * (1 - y))
        relay_dst = (c * x + (1 - c) * (1 - x), c * (1 - y) + (1 - c) * y)

        def own(a):
            return ins[a] if pieces is None else ins[a].at[pl.ds(*pieces[a])]

        def slot(a, px, py, pc):
            idx = 4 * px + 2 * py + pc
            return outs[a].at[idx] if pieces is None else outs[a].at[idx, pl.ds(*pieces[a])]

        def copy(a, k, block, to, src=None):
            return pltpu.make_async_remote_copy(
                src_ref=slot(a, *block) if src is None else src,
                dst_ref=slot(a, *block),
                send_sem=send_sems.at[a * n_kinds + k],
                recv_sem=recv_sems.at[a * n_kinds + k],
                device_id=to,
                device_id_type=MESH,
            )

        mine = [pltpu.make_async_copy(own(a), slot(a, *me), local_sems.at[a]) for a in range(na)]
        first, second, third = [], [], []
        for a in range(na):
            first += [copy(a, 0, me, sibling, src=own(a)), copy(a, 1, me, (*x_nb, c), src=own(a)),
                      copy(a, 2, me, (*y_nb, c), src=own(a))]
            second += [copy(a, 3, (*relay_src, c), (*relay_dst, c)), copy(a, 4, (*x_nb, c), sibling),
                       copy(a, 5, (*y_nb, c), sibling)]
            third.append(copy(a, 6, (*diag, c), sibling))
        return sibling, c, x_nb, y_nb, diag, copy, mine, first, second, third

    def start(ins, outs, sems):
        _, _, _, _, _, _, mine, first, _, _ = parts(ins, outs, sems)
        for cp in mine + first:
            cp.start()

    def middle(ins, outs, sems):
        _, c, x_nb, y_nb, _, copy, _, _, second, _ = parts(ins, outs, sems)
        for a in range(na):
            copy(a, 1, (*x_nb, c), (*x_nb, c)).wait_recv()
            copy(a, 2, (*y_nb, c), (*y_nb, c)).wait_recv()
        for cp in second:
            cp.start()

    def finish(ins, outs, sems):
        sibling, c, x_nb, y_nb, diag, copy, mine, first, second, third = parts(ins, outs, sems)
        for a in range(na):
            copy(a, 3, (*diag, c), (*diag, c)).wait_recv()
            third[a].start()
        for a in range(na):
            copy(a, 0, sibling, sibling).wait_recv()
            copy(a, 4, (*x_nb, 1 - c), sibling).wait_recv()
            copy(a, 5, (*y_nb, 1 - c), sibling).wait_recv()
            copy(a, 6, (*diag, 1 - c), sibling).wait_recv()
        for cp in first + second + third:
            cp.wait_send()
        for cp in mine:
            cp.wait()

    return types.SimpleNamespace(
        ins=list(shards) + list(bufs or []),
        out_shapes=[jax.ShapeDtypeStruct((N_DEV,) + s.shape, s.dtype) for s in shards],
        sems=[pltpu.SemaphoreType.DMA((n_kinds * na,)), pltpu.SemaphoreType.DMA((n_kinds * na,)),
              pltpu.SemaphoreType.DMA((na,))],
        aliases=[(na + a, a) for a in range(na)] if bufs else [],
        peers=frozenset({"sibling", "neighbours"}), start=start, middle=middle, finish=finish)


def _rs_sibling_plan(fulls):
    na = len(fulls)
    rs = [f.shape[0] // N_DEV for f in fulls]

    def copies(ins, outs, sems):
        send_sems, recv_sems = sems
        x, y, c = _place()
        out = []
        for a in range(na):
            for q in range(4):
                shard = 2 * q + (1 - c)
                out.append(pltpu.make_async_remote_copy(
                    src_ref=ins[a].at[pl.ds(shard * rs[a], rs[a])],
                    dst_ref=outs[a].at[q],
                    send_sem=send_sems.at[a * 4 + q],
                    recv_sem=recv_sems.at[a * 4 + q],
                    device_id=(x, y, 1 - c),
                    device_id_type=MESH,
                ))
        return out

    def start(ins, outs, sems):
        for cp in copies(ins, outs, sems):
            cp.start()

    def finish(ins, outs, sems):
        for cp in copies(ins, outs, sems):
            cp.wait()

    return types.SimpleNamespace(
        ins=list(fulls),
        out_shapes=[jax.ShapeDtypeStruct((4, r) + f.shape[1:], f.dtype) for r, f in zip(rs, fulls)],
        sems=[pltpu.SemaphoreType.DMA((4 * na,)), pltpu.SemaphoreType.DMA((4 * na,))],
        peers=frozenset({"sibling"}), start=start, finish=finish)


def _rs_chips_plan(sends, pieces=None, bufs=None):
    na = len(sends)

    def copies(ins, outs, sems):
        send_sems, recv_sems = sems
        x, y, c = _place()
        chips = [(1 - x, y), (x, 1 - y), (1 - x, 1 - y)]
        out = []
        for a in range(na):
            for k, chip in enumerate(chips):
                rows = (k,) if pieces is None else (k, pl.ds(*pieces[a]))
                out.append(pltpu.make_async_remote_copy(
                    src_ref=ins[a].at[rows],
                    dst_ref=outs[a].at[rows],
                    send_sem=send_sems.at[a * 3 + k],
                    recv_sem=recv_sems.at[a * 3 + k],
                    device_id=(*chip, c),
                    device_id_type=MESH,
                ))
        return out

    def start(ins, outs, sems):
        for cp in copies(ins, outs, sems):
            cp.start()

    def finish(ins, outs, sems):
        for cp in copies(ins, outs, sems):
            cp.wait()

    return types.SimpleNamespace(
        ins=list(sends) + list(bufs or []),
        out_shapes=[jax.ShapeDtypeStruct(s.shape, s.dtype) for s in sends],
        sems=[pltpu.SemaphoreType.DMA((3 * na,)), pltpu.SemaphoreType.DMA((3 * na,))],
        aliases=[(na + a, a) for a in range(na)] if bufs else [],
        peers=frozenset({"chips"}), start=start, finish=finish)


def _join(plans):
    ins, outs, sems, aliases, offs = [], [], [], [], []
    for p in plans:
        offs.append((len(ins), len(outs), len(sems)))
        aliases += [(len(ins) + ci, len(outs) + co) for ci, co in getattr(p, "aliases", [])]
        ins += p.ins
        outs += p.out_shapes
        sems += p.sems

    def cut(p, off, i, o, s):
        return (i[off[0]:off[0] + len(p.ins)], o[off[1]:off[1] + len(p.out_shapes)],
                s[off[2]:off[2] + len(p.sems)])

    def start(i, o, s):
        for p, off in zip(plans, offs):
            p.start(*cut(p, off, i, o, s))

    def middle(i, o, s):
        for p, off in zip(plans, offs):
            if getattr(p, "middle", None) is not None:
                p.middle(*cut(p, off, i, o, s))

    def finish(i, o, s):
        for p, off in zip(plans, offs):
            p.finish(*cut(p, off, i, o, s))

    def split(results):
        return [list(results[off[1]:off[1] + len(p.out_shapes)]) for p, off in zip(plans, offs)]

    return types.SimpleNamespace(ins=ins, out_shapes=outs, sems=sems, aliases=aliases,
                                 peers=frozenset().union(*[p.peers for p in plans]),
                                 start=start, middle=middle, finish=finish, split=split)


COLLECTIVE_ID = {frozenset({"sibling"}): 0, frozenset({"chips"}): 1, frozenset({"sibling", "chips"}): 2,
                 frozenset({"sibling", "neighbours"}): 3}


def _handshake(peers):
    x, y, c = _place()
    devs = []
    if "sibling" in peers:
        devs.append((x, y, 1 - c))
    if "neighbours" in peers:
        devs += [(1 - x, y, c), (x, 1 - y, c)]
    if "chips" in peers:
        assert "neighbours" not in peers
        devs += [(1 - x, y, c), (x, 1 - y, c), (1 - x, 1 - y, c)]
    barrier = pltpu.get_barrier_semaphore()
    for dev in devs:
        pl.semaphore_signal(barrier, inc=1, device_id=dev, device_id_type=MESH)
    pl.semaphore_wait(barrier, len(devs))


def _in_hbm(args):
    return [pltpu.with_memory_space_constraint(a, pltpu.HBM) for a in args]


def _run_plan(plan, name):
    n_in, n_out = len(plan.ins), len(plan.out_shapes)

    def body(*refs):
        ins, outs, sems = refs[:n_in], refs[n_in:n_in + n_out], refs[n_in + n_out:]
        _handshake(plan.peers)
        plan.start(ins, outs, sems)
        if getattr(plan, "middle", None) is not None:
            plan.middle(ins, outs, sems)
        plan.finish(ins, outs, sems)

    return pl.pallas_call(
        body,
        name=name,
        in_specs=[ANY] * n_in,
        out_specs=[ANY] * n_out,
        out_shape=plan.out_shapes,
        scratch_shapes=plan.sems,
        input_output_aliases=dict(getattr(plan, "aliases", [])),
        compiler_params=pltpu.CompilerParams(collective_id=COLLECTIVE_ID[plan.peers]),
    )(*_in_hbm(plan.ins))


def _call(body, *, name, grid, in_specs, out_specs, out_shape, args, scratch_shapes=(), aliases=None,
          carry=None):
    n_in, n_out, n_scr = len(in_specs), len(out_shape), len(scratch_shapes)
    params = pltpu.CompilerParams(
        dimension_semantics=("arbitrary",) * len(grid), vmem_limit_bytes=V7X_VMEM_LIMIT_BYTES)
    if carry is None:
        outs = pl.pallas_call(
            body, name=name, grid=grid, in_specs=list(in_specs), out_specs=list(out_specs),
            out_shape=list(out_shape), scratch_shapes=list(scratch_shapes),
            input_output_aliases=aliases or {}, compiler_params=params)(*_in_hbm(args))
        return list(outs), []
    c_in, c_out = len(carry.ins), len(carry.out_shapes)

    def full(*refs):
        p = 0
        ins = refs[p:p + n_in]
        p += n_in
        cins = refs[p:p + c_in]
        p += c_in
        outs = refs[p:p + n_out]
        p += n_out
        couts = refs[p:p + c_out]
        p += c_out
        scr = refs[p:p + n_scr]
        csems = refs[p + n_scr:]
        ids = [pl.program_id(a) for a in range(len(grid))]
        first = functools.reduce(operator.and_, [i == 0 for i in ids])
        last = functools.reduce(operator.and_, [i == g - 1 for i, g in zip(ids, grid)])

        @pl.when(first)
        def _():
            _handshake(carry.peers)
            carry.start(cins, couts, csems)

        if getattr(carry, "middle", None) is not None:
            n_steps = math.prod(grid)
            flat = functools.reduce(lambda acc, ig: acc * ig[1] + ig[0], zip(ids, grid), 0)

            @pl.when(flat == (2 * n_steps) // 3)
            def _():
                carry.middle(cins, couts, csems)

        body(*ins, *outs, *scr)

        @pl.when(last)
        def _():
            carry.finish(cins, couts, csems)

    all_aliases = dict(aliases or {})
    all_aliases.update({n_in + ci: n_out + co for ci, co in getattr(carry, "aliases", [])})
    params = pltpu.CompilerParams(
        dimension_semantics=("arbitrary",) * len(grid), vmem_limit_bytes=V7X_VMEM_LIMIT_BYTES,
        collective_id=COLLECTIVE_ID[carry.peers])
    outs = pl.pallas_call(
        full, name=name, grid=grid,
        in_specs=list(in_specs) + [ANY] * c_in,
        out_specs=list(out_specs) + [ANY] * c_out,
        out_shape=list(out_shape) + list(carry.out_shapes),
        scratch_shapes=list(scratch_shapes) + list(carry.sems),
        input_output_aliases=all_aliases, compiler_params=params)(*_in_hbm(args), *_in_hbm(carry.ins))
    return list(outs[:n_out]), list(outs[n_out:])


def _to_bf16(arrs, carry=None):
    n = len(arrs)

    def body(*refs):
        for src, dst in zip(refs[:n], refs[n:]):
            dst[...] = src[...].astype(BF16)

    specs = [pl.BlockSpec(a.shape, lambda i: (0, 0)) for a in arrs]
    return _call(body, name="cast_weights", grid=(1,), in_specs=specs, out_specs=specs,
                 out_shape=[jax.ShapeDtypeStruct(a.shape, BF16) for a in arrs], args=arrs, carry=carry)


def _norm_proj(x, g1, w_int, carry=None):
    t, d = x.shape
    n = w_int.shape[0]
    tt, tn = _tile(t, 2048), _tile(n, 512)

    def body(x_ref, g_ref, w_ref, proj_ref, h1_ref, h1_s):
        @pl.when(pl.program_id(1) == 0)
        def _():
            def norm_rows(rows):
                xhat, _ = _rms_hat(x_ref[rows, :])
                h = (xhat * g_ref[...]).astype(BF16)
                h1_s[rows, :] = h
                h1_ref[rows, :] = h

            _row_chunks(tt, norm_rows)

        proj_ref[...] = _dot_nt(h1_s[...], w_ref[...]).astype(BF16)

    return _call(
        body, name="norm_proj", grid=(t // tt, n // tn),
        in_specs=[
            pl.BlockSpec((tt, d), lambda i, j: (i, 0)),
            pl.BlockSpec((1, d), lambda i, j: (0, 0)),
            pl.BlockSpec((tn, d), lambda i, j: (j, 0)),
        ],
        out_specs=[
            pl.BlockSpec((tt, tn), lambda i, j: (i, j)),
            pl.BlockSpec((tt, d), lambda i, j: (i, 0)),
        ],
        out_shape=[jax.ShapeDtypeStruct((t, n), BF16), jax.ShapeDtypeStruct((t, d), BF16)],
        scratch_shapes=[pltpu.VMEM((tt, d), BF16)],
        args=(x, g1, w_int), carry=carry)


def _scan_rows(av, bv, reverse):
    tc = av.shape[0]
    row = lax.broadcasted_iota(jnp.int32, av.shape, 0)
    s = 1
    while s < tc:
        if s < 8:
            keep = (row < tc - s) if reverse else (row >= s)
            shift = (tc - s) if reverse else s
            a_sh = jnp.where(keep, pltpu.roll(av, shift, 0), 1.0)
            b_sh = jnp.where(keep, pltpu.roll(bv, shift, 0), 0.0)
            bv = av * b_sh + bv
            av = av * a_sh
        elif reverse:
            bv = jnp.concatenate([av[:tc - s] * bv[s:] + bv[:tc - s], bv[tc - s:]], axis=0)
            av = jnp.concatenate([av[:tc - s] * av[s:], av[tc - s:]], axis=0)
        else:
            bv = jnp.concatenate([bv[:s], av[s:] * bv[:tc - s] + bv[s:]], axis=0)
            av = jnp.concatenate([av[:s], av[s:] * av[:tc - s]], axis=0)
        s *= 2
    return av, bv


N_LRU_SAVED = 5


def _fill_block_diag(w_ref, bd_ref):
    bd_ref[...] = jnp.zeros_like(bd_ref)
    hd = LRU_HEAD_DIM
    for k in range(w_ref.shape[0]):
        bd_ref[k * hd:(k + 1) * hd, k * hd:(k + 1) * hd] = w_ref[k].astype(BF16)


def _lru_fwd(proj, conv_w, conv_b, w_a, b_a, w_x, b_x, lam, carry=None):
    t = proj.shape[0]
    dr = conv_b.shape[1]
    cb = LRU_CB
    tc = _tile(t, 256)
    ncb, ntc = dr // cb, t // tc

    def body(xp_ref, g_ref, cw_ref, cb_ref, wa_ref, ba_ref, wx_ref, bx_ref, lam_ref,
             y_ref, h_ref, saved_ref, prevx_s, hlast_s, wa_s, wx_s):
        c = pl.program_id(1)

        @pl.when(c == 0)
        def _():
            prevx_s[...] = jnp.zeros_like(prevx_s)
            hlast_s[...] = jnp.zeros_like(hlast_s)
            _fill_block_diag(wa_ref, wa_s)
            _fill_block_diag(wx_ref, wx_s)

        x = xp_ref[...].astype(F32)
        prev = prevx_s[...]
        row = lax.broadcasted_iota(jnp.int32, x.shape, 0)

        def sh(j):
            return jnp.where(row >= j, pltpu.roll(x, j, 0), pltpu.roll(prev, j, 0))

        xc = (cb_ref[...] + cw_ref[0:1, :] * sh(3) + cw_ref[1:2, :] * sh(2)
              + cw_ref[2:3, :] * sh(1) + cw_ref[3:4, :] * x)
        prevx_s[...] = x
        _, r, i, _, _, a, mult = _lru_gates(xc, wa_s[...], ba_ref[...], wx_s[...], bx_ref[...],
                                            lam_ref[...])
        for k, val in enumerate((xc, r, i, a, mult)):
            saved_ref[:, k * cb:(k + 1) * cb] = val
        av, bv = _scan_rows(a, mult * (i * xc), reverse=False)
        h = av * hlast_s[...] + bv
        h_ref[...] = h
        hlast_s[...] = h_ref[tc - 1:tc, :]
        gel, _ = _gelu_and_grad(g_ref[...].astype(F32))
        y_ref[...] = (h * gel).astype(BF16)

    vec = pl.BlockSpec((1, cb), lambda j, c: (0, j))
    blk = pl.BlockSpec((tc, cb), lambda j, c: (c, j))
    mat = pl.BlockSpec((cb // LRU_HEAD_DIM, LRU_HEAD_DIM, LRU_HEAD_DIM), lambda j, c: (j, 0, 0))
    return _call(
        body, name="lru_fwd", grid=(ncb, ntc),
        in_specs=[
            blk,
            pl.BlockSpec((tc, cb), lambda j, c: (c, ncb + j)),
            pl.BlockSpec((4, cb), lambda j, c: (0, j)),
            vec, mat, vec, mat, vec, vec,
        ],
        out_specs=[blk, blk, pl.BlockSpec((tc, N_LRU_SAVED * cb), lambda j, c: (c, j))],
        out_shape=[jax.ShapeDtypeStruct((t, dr), BF16), jax.ShapeDtypeStruct((t, dr), F32),
                   jax.ShapeDtypeStruct((t, N_LRU_SAVED * dr), F32)],
        scratch_shapes=[pltpu.VMEM((tc, cb), F32), pltpu.VMEM((1, cb), F32),
                        pltpu.VMEM((cb, cb), BF16), pltpu.VMEM((cb, cb), BF16)],
        args=(proj, proj, conv_w, conv_b, w_a, b_a, w_x, b_x, lam), carry=carry)


def _pool_select(col, vals):
    out = vals[3]
    for g in (2, 1, 0):
        out = jnp.where(col < (g + 1) * POOL_GROUP_DIM, vals[g], out)
    return out


def _pool_fwd(proj, pool_w, pool_scale, col_block):
    t = proj.shape[0]
    dp = pool_scale.shape[1]
    tc = _tile(t, 256)
    ntc = t // tc

    def body(x_ref, w_ref, sc_ref, y_ref, p_ref, px, p2, p4, p8):
        c = pl.program_id(0)

        @pl.when(c == 0)
        def _():
            for s in (px, p2, p4, p8):
                s[...] = jnp.zeros_like(s)

        x = x_ref[...].astype(F32)
        row = lax.broadcasted_iota(jnp.int32, x.shape, 0)
        col = lax.broadcasted_iota(jnp.int32, x.shape, 1)

        def sh(v, pv, j):
            return jnp.where(row >= j, pltpu.roll(v, j, 0), pltpu.roll(pv[...], j, 0))

        s2 = x + sh(x, px, 1)
        s4 = s2 + sh(s2, p2, 2)
        s8 = s4 + sh(s4, p4, 4)
        s16 = s8 + sh(s8, p8, 8)
        px[...] = x
        p2[...] = s2
        p4[...] = s4
        p8[...] = s8
        wsum = _pool_select(col, (s2, s4, s8, s16))
        win = _pool_select(col, POOL_WINDOWS)
        cnt = jnp.minimum(c * tc + row + 1, win).astype(F32)
        p = wsum / cnt - x
        pb = p.astype(BF16)
        p_ref[...] = pb
        for g in range(len(POOL_WINDOWS)):
            sl = slice(g * POOL_GROUP_DIM, (g + 1) * POOL_GROUP_DIM)
            yg = _dot_nn(pb[:, sl], w_ref[g]) * sc_ref[:, sl]
            y_ref[:, sl] = yg.astype(BF16)

    return _call(
        body, name="pool_fwd", grid=(ntc,),
        in_specs=[
            pl.BlockSpec((tc, dp), lambda c: (c, col_block)),
            pl.BlockSpec(pool_w.shape, lambda c: (0, 0, 0)),
            pl.BlockSpec((1, dp), lambda c: (0, 0)),
        ],
        out_specs=[pl.BlockSpec((tc, dp), lambda c: (c, 0))] * 2,
        out_shape=[jax.ShapeDtypeStruct((t, dp), BF16)] * 2,
        scratch_shapes=[pltpu.VMEM((tc, dp), F32)] * 4,
        args=(proj, pool_w, pool_scale))[0]


def _branch_mix(y_lru, y_pool, w_lru_up, w_pool_upb, proj, b_gate, ga_block, gb_block, carry=None):
    t, d = y_lru.shape
    dp = y_pool.shape[1]
    bw = w_pool_upb.shape[2]
    tt, tn = _tile(t, 1024), 512
    nj = d // tn

    def body(yl_ref, yp_ref, wl_ref, wp_ref, ga_ref, gb_ref, ba_ref, bb_ref, bra_ref, brb_ref, mix_ref):
        br_a = _dot_nn(yl_ref[...], wl_ref[...])
        wp = jnp.concatenate([wp_ref[b] for b in range(tn // bw)], axis=1)
        br_b = _dot_nn(yp_ref[...], wp)
        bra_ref[...] = br_a.astype(BF16)
        brb_ref[...] = br_b.astype(BF16)
        ga = _sig(ga_ref[...].astype(F32) + ba_ref[...])
        gb = _sig(gb_ref[...].astype(F32) + bb_ref[...])
        mix_ref[...] = (ga * br_a + gb * br_b).astype(BF16)

    out = pl.BlockSpec((tt, tn), lambda j, i: (i, j))
    return _call(
        body, name="branch_mix", grid=(nj, t // tt),
        in_specs=[
            pl.BlockSpec((tt, d), lambda j, i: (i, 0)),
            pl.BlockSpec((tt, dp), lambda j, i: (i, 0)),
            pl.BlockSpec((d, tn), lambda j, i: (0, j)),
            pl.BlockSpec((tn // bw, dp, bw), lambda j, i: (j, 0, 0)),
            pl.BlockSpec((tt, tn), lambda j, i: (i, ga_block + j)),
            pl.BlockSpec((tt, tn), lambda j, i: (i, gb_block + j)),
            pl.BlockSpec((1, tn), lambda j, i: (0, j)),
            pl.BlockSpec((1, tn), lambda j, i: (0, nj + j)),
        ],
        out_specs=[out, out, out],
        out_shape=[jax.ShapeDtypeStruct((t, d), BF16)] * 3,
        args=(y_lru, y_pool, w_lru_up, w_pool_upb, proj, proj, b_gate, b_gate), carry=carry)


def _wo_norm(mix, w_o, x, g2, g3, carry=None):
    t, d = x.shape
    tt = _tile(t, 512)

    def body(mix_ref, w_ref, x_ref, g2_ref, g3_ref, m_ref, x2_ref, h3_ref):
        m = _dot_nn(mix_ref[...], w_ref[...])
        m_ref[...] = m
        mhat, _ = _rms_hat(m)
        x2 = x_ref[...] + mhat * g2_ref[...]
        x2_ref[...] = x2
        xhat, _ = _rms_hat(x2)
        h3_ref[...] = (xhat * g3_ref[...]).astype(BF16)

    row = pl.BlockSpec((tt, d), lambda i: (i, 0))
    vec = pl.BlockSpec((1, d), lambda i: (0, 0))
    return _call(
        body, name="wo_norm", grid=(t // tt,),
        in_specs=[row, pl.BlockSpec((d, d), lambda i: (0, 0)), row, vec, vec],
        out_specs=[row, row, row],
        out_shape=[
            jax.ShapeDtypeStruct((t, d), F32),
            jax.ShapeDtypeStruct((t, d), F32),
            jax.ShapeDtypeStruct((t, d), BF16),
        ],
        args=(mix, w_o, x, g2, g3), carry=carry)


def _ff1(h3, w_ff1b, carry=None):
    t, d = h3.shape
    nb, _, tn = w_ff1b.shape
    tt = _tile(t, 2048)

    def body(h_ref, w_ref, rf_ref):
        rf_ref[...] = jnp.maximum(_dot_nn(h_ref[...], w_ref[...]), 0.0).astype(BF16)

    out = pl.BlockSpec((tt, tn), lambda i, j: (i, j))
    return _call(
        body, name="ff1", grid=(t // tt, nb),
        in_specs=[pl.BlockSpec((tt, d), lambda i, j: (i, 0)), pl.BlockSpec((None, d, tn), lambda i, j: (j, 0, 0))],
        out_specs=[out],
        out_shape=[jax.ShapeDtypeStruct((t, nb * tn), BF16)],
        args=(h3, w_ff1b), carry=carry)


def _ff2_loss(rf, w_ff2, x2, g4, target):
    t, k = rf.shape
    d = x2.shape[1]
    tt, tk = _tile(t, 1024), _tile(k, 1024)
    nk = k // tk

    def body(a_ref, w_ref, x2_ref, g_ref, tg_ref, dy_ref, df_ref, dg_ref, loss_ref, acc):
        i, kk = pl.program_id(0), pl.program_id(1)

        @pl.when(kk == 0)
        def _():
            acc[...] = jnp.zeros_like(acc)

        @pl.when((i == 0) & (kk == 0))
        def _():
            dg_ref[...] = jnp.zeros_like(dg_ref)
            loss_ref[...] = jnp.zeros_like(loss_ref)

        rf_tile = a_ref[...]
        acc[...] += _dot_nn(rf_tile * rf_tile, w_ref[...])

        @pl.when(kk == nk - 1)
        def _():
            def tail(rows):
                fhat, r = _rms_hat(acc[rows, :])
                g = g_ref[...]
                e = x2_ref[rows, :] + fhat * g - tg_ref[rows, :]
                loss_ref[...] += 0.5 * jnp.sum(jnp.mean(e * e, axis=-1, keepdims=True))
                dy = e * (1.0 / d)
                dy_ref[rows, :] = dy.astype(BF16)
                df, dg = _rms_bwd(dy, fhat, r, g)
                df_ref[rows, :] = df.astype(BF16)
                dg_ref[...] += dg

            _row_chunks(tt, tail)

    row = pl.BlockSpec((tt, d), lambda i, kk: (i, 0))
    vec = pl.BlockSpec((1, d), lambda i, kk: (0, 0))
    return _call(
        body, name="ff2_loss", grid=(t // tt, nk),
        in_specs=[
            pl.BlockSpec((tt, tk), lambda i, kk: (i, kk)),
            pl.BlockSpec((tk, d), lambda i, kk: (kk, 0)),
            row, vec, row,
        ],
        out_specs=[row, row, vec, pl.BlockSpec((1, 128), lambda i, kk: (0, 0))],
        out_shape=[
            jax.ShapeDtypeStruct((t, d), BF16),
            jax.ShapeDtypeStruct((t, d), BF16),
            jax.ShapeDtypeStruct((1, d), F32),
            jax.ShapeDtypeStruct((1, 128), F32),
        ],
        scratch_shapes=[pltpu.VMEM((tt, d), F32)],
        args=(rf, w_ff2, x2, g4, target))[0]


def _ff2_bwd(df, w_ff2, rf, carry=None):
    t, d = df.shape
    n = w_ff2.shape[0]
    tt, tn = _tile(t, 2048), _tile(n, 512)

    def body(df_ref, w_ref, rf_ref, out_ref):
        d_act = _dot_nt(df_ref[...], w_ref[...])
        out_ref[...] = (d_act * (2.0 * rf_ref[...].astype(F32))).astype(BF16)

    blk = pl.BlockSpec((tt, tn), lambda i, j: (i, j))
    return _call(
        body, name="ff2_bwd", grid=(t // tt, n // tn),
        in_specs=[pl.BlockSpec((tt, d), lambda i, j: (i, 0)), pl.BlockSpec((tn, d), lambda i, j: (j, 0)), blk],
        out_specs=[blk],
        out_shape=[jax.ShapeDtypeStruct((t, n), BF16)],
        args=(df, w_ff2, rf), carry=carry)


def _wgrad(a, b, name, prev=None, row_off=0, rows=None, carry=None, square_a=False):
    t, m = a.shape
    n = b.shape[1]
    rows = m if rows is None else rows
    tm, tk = _tile(m, 512), _tile(t, 2048)
    nk = t // tk
    assert row_off % tm == 0
    off = row_off // tm

    def body(*refs):
        a_ref, b_ref = refs[0], refs[1]
        o32_ref, o16_ref, acc = refs[-3], refs[-2], refs[-1]
        kk = pl.program_id(1)

        @pl.when(kk == 0)
        def _():
            acc[...] = jnp.zeros_like(acc)

        a_tile = a_ref[...]
        acc[...] += _dot_tn(a_tile * a_tile if square_a else a_tile, b_ref[...])

        @pl.when(kk == nk - 1)
        def _():
            o32_ref[...] = acc[...]
            o16_ref[...] = acc[...].astype(BF16)

    in_specs = [pl.BlockSpec((tk, tm), lambda i, kk: (kk, i)), pl.BlockSpec((tk, n), lambda i, kk: (kk, 0))]
    args = [a, b]
    aliases = {}
    if prev is not None:
        in_specs += [ANY, ANY]
        args += list(prev)
        aliases = {2: 0, 3: 1}
    out = pl.BlockSpec((tm, n), lambda i, kk: (off + i, 0))
    return _call(
        body, name=name, grid=(m // tm, nk),
        in_specs=in_specs, out_specs=[out, out],
        out_shape=[jax.ShapeDtypeStruct((rows, n), F32), jax.ShapeDtypeStruct((rows, n), BF16)],
        scratch_shapes=[pltpu.VMEM((tm, n), F32)],
        aliases=aliases, args=args, carry=carry)


def _wgrad_parts(parts, b, name, carry=None):
    t, n = b.shape
    tm = 512
    bounds = []
    lo = 0
    for part in parts:
        assert part.shape[0] == t and part.shape[1] % tm == 0
        bounds.append((lo, lo + part.shape[1] // tm))
        lo += part.shape[1] // tm
    nm = lo
    np_ = len(parts)

    def body(*refs):
        p_refs, b_ref, o32_ref, o16_ref = refs[:np_], refs[np_], refs[np_ + 1], refs[np_ + 2]
        i = pl.program_id(0)
        for (lo_p, hi_p), p_ref in zip(bounds, p_refs):
            @pl.when((i >= lo_p) & (i < hi_p))
            def _(p_ref=p_ref):
                res = _dot_tn(p_ref[...], b_ref[...])
                o32_ref[...] = res
                o16_ref[...] = res.astype(BF16)

    def part_spec(lo_p, hi_p):
        return pl.BlockSpec((t, tm), lambda i: (0, jnp.clip(i - lo_p, 0, hi_p - lo_p - 1)))

    out = pl.BlockSpec((tm, n), lambda i: (i, 0))
    return _call(
        body, name=name, grid=(nm,),
        in_specs=[part_spec(lo_p, hi_p) for lo_p, hi_p in bounds] + [pl.BlockSpec((t, n), lambda i: (0, 0))],
        out_specs=[out, out],
        out_shape=[jax.ShapeDtypeStruct((nm * tm, n), F32), jax.ShapeDtypeStruct((nm * tm, n), BF16)],
        args=(*parts, b), carry=carry)


def _wgrad_cols(a, b, bw, tn, name, carry=None):
    t, m = a.shape
    n = b.shape[1]
    per_step = tn // bw

    def body(a_ref, b_ref, o32_ref, o16_ref):
        res = _dot_tn(a_ref[...], b_ref[...])
        for blk in range(per_step):
            part = res[:, blk * bw:(blk + 1) * bw]
            o32_ref[blk] = part
            o16_ref[blk] = part.astype(BF16)

    out = pl.BlockSpec((per_step, m, bw), lambda j: (j, 0, 0))
    return _call(
        body, name=name, grid=(n // tn,),
        in_specs=[pl.BlockSpec((t, m), lambda j: (0, 0)), pl.BlockSpec((t, tn), lambda j: (0, j))],
        out_specs=[out, out],
        out_shape=[jax.ShapeDtypeStruct((n // bw, m, bw), F32), jax.ShapeDtypeStruct((n // bw, m, bw), BF16)],
        args=(a, b), carry=carry)


def _ff1_bwd_norms(d_f1, w_ff1b, dy, x2, g3, m, g2, carry=None):
    t, k = d_f1.shape
    d = x2.shape[1]
    bw = w_ff1b.shape[2]
    per_step = 2
    tt, tk = _tile(t, 1024), per_step * bw
    nk = k // tk

    def body(a_ref, w_ref, dy_ref, x2_ref, g3_ref, m_ref, g2_ref, dx2_ref, dm_ref, dg3_ref, dg2_ref, acc):
        i, kk = pl.program_id(0), pl.program_id(1)

        @pl.when(kk == 0)
        def _():
            acc[...] = jnp.zeros_like(acc)

        @pl.when((i == 0) & (kk == 0))
        def _():
            dg3_ref[...] = jnp.zeros_like(dg3_ref)
            dg2_ref[...] = jnp.zeros_like(dg2_ref)

        a_tile = a_ref[...]
        for b in range(per_step):
            acc[...] += _dot_nt(a_tile[:, b * bw:(b + 1) * bw], w_ref[b])

        @pl.when(kk == nk - 1)
        def _():
            def tail(rows):
                xhat, r3 = _rms_hat(x2_ref[rows, :])
                dx, dg3 = _rms_bwd(acc[rows, :], xhat, r3, g3_ref[...])
                dx2 = dy_ref[rows, :].astype(F32) + dx
                dx2_ref[rows, :] = dx2
                dg3_ref[...] += dg3
                mhat, r2 = _rms_hat(m_ref[rows, :])
                dm, dg2 = _rms_bwd(dx2, mhat, r2, g2_ref[...])
                dm_ref[rows, :] = dm.astype(BF16)
                dg2_ref[...] += dg2

            _row_chunks(tt, tail)

    row = pl.BlockSpec((tt, d), lambda i, kk: (i, 0))
    vec = pl.BlockSpec((1, d), lambda i, kk: (0, 0))
    return _call(
        body, name="ff1_bwd_norms", grid=(t // tt, nk),
        in_specs=[
            pl.BlockSpec((tt, tk), lambda i, kk: (i, kk)),
            pl.BlockSpec((per_step, d, bw), lambda i, kk: (kk, 0, 0)),
            row, row, vec, row, vec,
        ],
        out_specs=[row, row, vec, vec],
        out_shape=[
            jax.ShapeDtypeStruct((t, d), F32),
            jax.ShapeDtypeStruct((t, d), BF16),
            jax.ShapeDtypeStruct((1, d), F32),
            jax.ShapeDtypeStruct((1, d), F32),
        ],
        scratch_shapes=[pltpu.VMEM((tt, d), F32)],
        args=(d_f1, w_ff1b, dy, x2, g3, m, g2), carry=carry)


def _wo_bwd_mix(dm, w_o, br_a, br_b, proj, b_gate, ga_block, gb_block, carry=None):
    t, d = dm.shape
    tt, tn = _tile(t, 1024), 512
    nj = d // tn

    def body(dm_ref, w_ref, bra_ref, brb_ref, ga_ref, gb_ref, ba_ref, bb_ref,
             dbra_ref, dbrb_ref, dga_ref, dgb_ref, dba_ref, dbb_ref):
        i = pl.program_id(1)

        @pl.when(i == 0)
        def _():
            dba_ref[...] = jnp.zeros_like(dba_ref)
            dbb_ref[...] = jnp.zeros_like(dbb_ref)

        d_mix = _dot_nt(dm_ref[...], w_ref[...])
        ga = _sig(ga_ref[...].astype(F32) + ba_ref[...])
        gb = _sig(gb_ref[...].astype(F32) + bb_ref[...])
        dbra_ref[...] = (d_mix * ga).astype(BF16)
        dbrb_ref[...] = (d_mix * gb).astype(BF16)
        dga = d_mix * bra_ref[...].astype(F32) * (ga * (1.0 - ga))
        dgb = d_mix * brb_ref[...].astype(F32) * (gb * (1.0 - gb))
        dga_ref[...] = dga.astype(BF16)
        dgb_ref[...] = dgb.astype(BF16)
        dba_ref[...] += jnp.sum(dga, axis=0, keepdims=True)
        dbb_ref[...] += jnp.sum(dgb, axis=0, keepdims=True)

    blk = pl.BlockSpec((tt, tn), lambda j, i: (i, j))
    vec = pl.BlockSpec((1, tn), lambda j, i: (0, j))
    return _call(
        body, name="wo_bwd_mix", grid=(nj, t // tt),
        in_specs=[
            pl.BlockSpec((tt, d), lambda j, i: (i, 0)),
            pl.BlockSpec((tn, d), lambda j, i: (j, 0)),
            blk, blk,
            pl.BlockSpec((tt, tn), lambda j, i: (i, ga_block + j)),
            pl.BlockSpec((tt, tn), lambda j, i: (i, gb_block + j)),
            vec,
            pl.BlockSpec((1, tn), lambda j, i: (0, nj + j)),
        ],
        out_specs=[blk, blk, blk, blk, vec, vec],
        out_shape=[jax.ShapeDtypeStruct((t, d), BF16)] * 4 + [jax.ShapeDtypeStruct((1, d), F32)] * 2,
        args=(dm, w_o, br_a, br_b, proj, proj, b_gate, b_gate), carry=carry)


def _lru_up_bwd(d_br_a, w_lru_up, proj, h, g_block, carry=None):
    t, d = d_br_a.shape
    tt, tn = _tile(t, 1024), 512

    def body(a_ref, w_ref, g_ref, h_ref, dh_ref, dg_ref):
        d_y = _dot_nt(a_ref[...], w_ref[...])
        gel, gel_grad = _gelu_and_grad(g_ref[...].astype(F32))
        dh_ref[...] = d_y * gel
        dg_ref[...] = (d_y * h_ref[...] * gel_grad).astype(BF16)

    blk = pl.BlockSpec((tt, tn), lambda i, j: (i, j))
    return _call(
        body, name="lru_up_bwd", grid=(t // tt, d // tn),
        in_specs=[
            pl.BlockSpec((tt, d), lambda i, j: (i, 0)),
            pl.BlockSpec((tn, d), lambda i, j: (j, 0)),
            pl.BlockSpec((tt, tn), lambda i, j: (i, g_block + j)),
            blk,
        ],
        out_specs=[blk, blk],
        out_shape=[jax.ShapeDtypeStruct((t, d), F32), jax.ShapeDtypeStruct((t, d), BF16)],
        args=(d_br_a, w_lru_up, proj, h), carry=carry)


def _lru_bwd(dh, h, saved, proj, conv_w, w_a, w_x, lam, carry=None):
    t, dr = dh.shape
    cb = LRU_CB
    hd = LRU_HEAD_DIM
    per = cb // hd
    tc = _tile(t, 256)
    ncb, ntc = dr // cb, t // tc

    def body(dh_ref, h_ref, hp_ref, saved_ref, xp_ref, cw_ref, wa_ref, wx_ref,
             lam_ref, dxp_ref, dwa_ref, dba_ref, dwx_ref, dbx_ref, dlam_ref, dcw_ref, dcb_ref,
             nextd_s, anext_s, gnext_s, tmp_s, wa_s, wx_s):
        c = pl.program_id(1)
        rc = ntc - 1 - c

        @pl.when(c == 0)
        def _():
            nextd_s[...] = jnp.zeros_like(nextd_s)
            anext_s[...] = jnp.zeros_like(anext_s)
            gnext_s[...] = jnp.zeros_like(gnext_s)
            for ref in (dwa_ref, dba_ref, dwx_ref, dbx_ref, dlam_ref, dcw_ref, dcb_ref):
                ref[...] = jnp.zeros_like(ref)
            _fill_block_diag(wa_ref, wa_s)
            _fill_block_diag(wx_ref, wx_s)

        xc, r, i, a, mult = [saved_ref[:, k * cb:(k + 1) * cb] for k in range(N_LRU_SAVED)]
        wa, wx, lam = wa_s[...], wx_s[...], lam_ref[...]
        xcb = xc.astype(BF16)
        sp = _softplus_neg(lam)
        row = lax.broadcasted_iota(jnp.int32, xc.shape, 0)
        h = h_ref[...]
        hp = jnp.where(rc == 0, 0.0, hp_ref[...])
        hprev = jnp.where(row >= 1, pltpu.roll(h, 1, 0), pltpu.roll(hp, 1, 0))

        def up(v, nv, j):
            return jnp.where(row < tc - j, pltpu.roll(v, tc - j, 0), nv)

        av, bv = _scan_rows(up(a, anext_s[...], 1), dh_ref[...], reverse=True)
        gt = av * gnext_s[...] + bv
        tmp_s[...] = gt
        gnext_s[...] = tmp_s[0:1, :]
        tmp_s[...] = a
        anext_s[...] = tmp_s[0:1, :]

        da = gt * hprev
        ixc = i * xc
        d_mult = gt * ixc
        d_i = gt * mult * xc
        d_xc = gt * mult * i
        d_log_a = da * a - d_mult * (a * a) / mult
        d_pre_r = (d_log_a * ((-LRU_C) * sp)) * (r * (1.0 - r))
        d_pre_i = d_i * (i * (1.0 - i))
        d_sp = jnp.sum(d_log_a * ((-LRU_C) * r), axis=0, keepdims=True)
        dlam_ref[...] += d_sp * (-1.0 / (1.0 + jnp.exp(lam)))
        dpr = d_pre_r.astype(BF16)
        dpi = d_pre_i.astype(BF16)
        dba_ref[...] += jnp.sum(d_pre_r, axis=0, keepdims=True)
        dbx_ref[...] += jnp.sum(d_pre_i, axis=0, keepdims=True)
        pa = _dot_tn(xcb, dpr)
        px = _dot_tn(xcb, dpi)
        for k in range(per):
            dwa_ref[k] += pa[k * hd:(k + 1) * hd, k * hd:(k + 1) * hd]
            dwx_ref[k] += px[k * hd:(k + 1) * hd, k * hd:(k + 1) * hd]
        d_xc = d_xc + _dot_nt(dpr, wa) + _dot_nt(dpi, wx)

        nxt = nextd_s[...]
        xp = xp_ref[...].astype(F32)
        dxp = cw_ref[3:4, :] * d_xc
        dcw_ref[3:4, :] += jnp.sum(xp * d_xc, axis=0, keepdims=True)
        for j in (1, 2, 3):
            uj = up(d_xc, pltpu.roll(nxt, tc - j, 0), j)
            dxp = dxp + cw_ref[3 - j:4 - j, :] * uj
            dcw_ref[3 - j:4 - j, :] += jnp.sum(xp * uj, axis=0, keepdims=True)
        dcb_ref[...] += jnp.sum(d_xc, axis=0, keepdims=True)
        nextd_s[...] = d_xc
        dxp_ref[...] = dxp.astype(BF16)

    vec = pl.BlockSpec((1, cb), lambda j, c: (0, j))
    blk = pl.BlockSpec((tc, cb), lambda j, c: (ntc - 1 - c, j))
    mat = pl.BlockSpec((per, hd, hd), lambda j, c: (j, 0, 0))
    cwb = pl.BlockSpec((4, cb), lambda j, c: (0, j))
    return _call(
        body, name="lru_bwd", grid=(ncb, ntc),
        in_specs=[
            blk, blk,
            pl.BlockSpec((tc, cb), lambda j, c: (jnp.maximum(ntc - 2 - c, 0), j)),
            pl.BlockSpec((tc, N_LRU_SAVED * cb), lambda j, c: (ntc - 1 - c, j)),
            blk, cwb, mat, mat, vec,
        ],
        out_specs=[blk, mat, vec, mat, vec, vec, cwb, vec],
        out_shape=[
            jax.ShapeDtypeStruct((t, dr), BF16),
            jax.ShapeDtypeStruct(w_a.shape, F32),
            jax.ShapeDtypeStruct((1, dr), F32),
            jax.ShapeDtypeStruct(w_x.shape, F32),
            jax.ShapeDtypeStruct((1, dr), F32),
            jax.ShapeDtypeStruct((1, dr), F32),
            jax.ShapeDtypeStruct((4, dr), F32),
            jax.ShapeDtypeStruct((1, dr), F32),
        ],
        scratch_shapes=[
            pltpu.VMEM((tc, cb), F32),
            pltpu.VMEM((1, cb), F32),
            pltpu.VMEM((1, cb), F32),
            pltpu.VMEM((tc, cb), F32),
            pltpu.VMEM((cb, cb), BF16),
            pltpu.VMEM((cb, cb), BF16),
        ],
        args=(dh, h, h, saved, proj, conv_w, w_a, w_x, lam), carry=carry)


def _pool_bwd(d_br_b, w_pool_upb, p, pool_w, pool_scale):
    t, d = d_br_b.shape
    nwb, dp, _ = w_pool_upb.shape
    tc = _tile(t, 256)
    ntc = t // tc
    ng = len(POOL_WINDOWS)

    def body(db_ref, wu_ref, p_ref, w_ref, sc_ref, dx_ref, dw_ref, dsc_ref, nz, n2, n4, n8, dp_s, dy_s):
        c = pl.program_id(0)
        rc = ntc - 1 - c

        @pl.when(c == 0)
        def _():
            for s in (nz, n2, n4, n8):
                s[...] = jnp.zeros_like(s)
            dw_ref[...] = jnp.zeros_like(dw_ref)
            dsc_ref[...] = jnp.zeros_like(dsc_ref)

        wu = jnp.concatenate([wu_ref[b] for b in range(nwb)], axis=1)
        dy_s[...] = _dot_nt(db_ref[...], wu)
        for g in range(ng):
            sl = slice(g * POOL_GROUP_DIM, (g + 1) * POOL_GROUP_DIM)
            pg = p_ref[:, sl]
            dyg = dy_s[:, sl]
            wg = w_ref[g].astype(BF16)
            q = _dot_nn(pg, wg)
            dsc_ref[:, sl] += jnp.sum(dyg * q, axis=0, keepdims=True)
            dpw = (dyg * sc_ref[:, sl]).astype(BF16)
            dw_ref[g] += _dot_tn(pg, dpw)
            dp_s[:, sl] = _dot_nt(dpw, wg)

        dpv = dp_s[...]
        row = lax.broadcasted_iota(jnp.int32, dpv.shape, 0)
        col = lax.broadcasted_iota(jnp.int32, dpv.shape, 1)
        win = _pool_select(col, POOL_WINDOWS)
        cnt = jnp.minimum(rc * tc + row + 1, win).astype(F32)
        z = dpv / cnt

        def up(v, nv, j):
            return jnp.where(row < tc - j, pltpu.roll(v, tc - j, 0), pltpu.roll(nv[...], tc - j, 0))

        u2 = z + up(z, nz, 1)
        u4 = u2 + up(u2, n2, 2)
        u8 = u4 + up(u4, n4, 4)
        u16 = u8 + up(u8, n8, 8)
        nz[...] = z
        n2[...] = u2
        n4[...] = u4
        n8[...] = u8
        dx_ref[...] = (_pool_select(col, (u2, u4, u8, u16)) - dpv).astype(BF16)

    blk = pl.BlockSpec((tc, dp), lambda c: (ntc - 1 - c, 0))
    full_w = pl.BlockSpec(pool_w.shape, lambda c: (0, 0, 0))
    vec = pl.BlockSpec((1, dp), lambda c: (0, 0))
    return _call(
        body, name="pool_bwd", grid=(ntc,),
        in_specs=[pl.BlockSpec((tc, d), lambda c: (ntc - 1 - c, 0)),
                  pl.BlockSpec(w_pool_upb.shape, lambda c: (0, 0, 0)), blk, full_w, vec],
        out_specs=[blk, full_w, vec],
        out_shape=[
            jax.ShapeDtypeStruct((t, dp), BF16),
            jax.ShapeDtypeStruct(pool_w.shape, F32),
            jax.ShapeDtypeStruct((1, dp), F32),
        ],
        scratch_shapes=[pltpu.VMEM((tc, dp), F32)] * 6,
        args=(d_br_b, w_pool_upb, p, pool_w, pool_scale))[0]


def _win_bwd_norm(parts, w_int, dx2, x, g1, carry=None):
    t, d = x.shape
    tk = 512
    tt = _tile(t, 1024)
    bounds = []
    k0 = 0
    for part in parts:
        assert part.shape[1] % tk == 0
        bounds.append((k0, k0 + part.shape[1] // tk))
        k0 += part.shape[1] // tk
    nk = k0
    assert nk * tk == w_int.shape[0]
    np_ = len(parts)

    def body(*refs):
        p_refs = refs[:np_]
        w_ref, dx2_ref, x_ref, g_ref, gx_ref, dg_ref, acc = refs[np_:]
        i, kk = pl.program_id(0), pl.program_id(1)

        @pl.when(kk == 0)
        def _():
            acc[...] = jnp.zeros_like(acc)

        @pl.when((i == 0) & (kk == 0))
        def _():
            dg_ref[...] = jnp.zeros_like(dg_ref)

        for (lo, hi), p_ref in zip(bounds, p_refs):
            @pl.when((kk >= lo) & (kk < hi))
            def _(p_ref=p_ref):
                acc[...] += _dot_nn(p_ref[...], w_ref[...])

        @pl.when(kk == nk - 1)
        def _():
            def tail(rows):
                xhat, r = _rms_hat(x_ref[rows, :])
                dx, dg = _rms_bwd(acc[rows, :], xhat, r, g_ref[...])
                gx_ref[rows, :] = dx2_ref[rows, :] + dx
                dg_ref[...] += dg

            _row_chunks(tt, tail)

    def part_spec(lo, hi):
        return pl.BlockSpec((tt, tk), lambda i, kk: (i, jnp.clip(kk - lo, 0, hi - lo - 1)))

    row = pl.BlockSpec((tt, d), lambda i, kk: (i, 0))
    vec = pl.BlockSpec((1, d), lambda i, kk: (0, 0))
    return _call(
        body, name="win_bwd_norm", grid=(t // tt, nk),
        in_specs=[part_spec(lo, hi) for lo, hi in bounds]
        + [pl.BlockSpec((tk, d), lambda i, kk: (kk, 0)), row, row, vec],
        out_specs=[row, vec],
        out_shape=[jax.ShapeDtypeStruct((t, d), F32), jax.ShapeDtypeStruct((1, d), F32)],
        scratch_shapes=[pltpu.VMEM((tt, d), F32)],
        args=(*parts, w_int, dx2, x, g1), carry=carry)


def _adam_math(w, g, m, v):
    m = ADAM_B1 * m + (1.0 - ADAM_B1) * g
    v = ADAM_B2 * v + (1.0 - ADAM_B2) * (g * g)
    m_hat = m / (1.0 - ADAM_B1 ** ADAM_STEP)
    v_hat = v / (1.0 - ADAM_B2 ** ADAM_STEP)
    delta = -ADAM_LR * (m_hat / (jnp.sqrt(v_hat) + ADAM_EPS) + ADAM_WD * w)
    return delta, m, v


def _adamw_big(ws, gs, ms, vs):
    n = len(ws)
    nb = 4
    pair = [isinstance(g, tuple) for g in gs]

    def body(*refs):
        p = 0
        ins = []
        for a in range(n):
            k = 5 if pair[a] else 4
            ins.append(refs[p:p + k])
            p += k
        for a in range(n):
            g_out, d_ref, nm_ref, nv_ref = refs[p + 4 * a:p + 4 * a + 4]
            if pair[a]:
                w_ref, own_ref, recv_ref, m_ref, v_ref = ins[a]
                g = own_ref[...]
                for k in range(3):
                    g = g + recv_ref[k].astype(F32)
            else:
                w_ref, g_ref, m_ref, v_ref = ins[a]
                g = g_ref[...]
            dl, m, v = _adam_math(w_ref[...], g, m_ref[...], v_ref[...])
            g_out[...] = g
            d_ref[...] = dl
            nm_ref[...] = m
            nv_ref[...] = v

    in_specs, out_specs, out_shape, args = [], [], [], []
    for a, (w, g, m, v) in enumerate(zip(ws, gs, ms, vs)):
        rows, cols = w.shape
        blk = pl.BlockSpec((rows // nb, cols), lambda i: (i, 0))
        if pair[a]:
            in_specs += [blk, pl.BlockSpec((None, rows // nb, cols), lambda i: (0, i, 0)),
                         pl.BlockSpec((3, rows // nb, cols), lambda i: (0, i, 0)), blk, blk]
            args += [w, g[0], g[1], m, v]
        else:
            in_specs += [blk] * 4
            args += [w, g, m, v]
        out_specs += [blk] * 4
        out_shape += [jax.ShapeDtypeStruct(w.shape, F32)] * 4
    outs = _call(body, name="adamw_big", grid=(nb,), in_specs=in_specs, out_specs=out_specs,
                 out_shape=out_shape, args=args)[0]
    return [tuple(outs[4 * a:4 * a + 4]) for a in range(n)]


SMALL_ORDER = ("norm_mix_pre", "norm_mix_post", "norm_mlp_pre", "norm_mlp_post", "b_gate", "conv_w", "conv_b",
               "lru_w_a", "lru_b_a", "lru_w_x", "lru_b_x", "lru_lambda", "pool_w", "pool_scale")
VEC_ROW = dict(norm_mix_pre=0, norm_mix_post=1, norm_mlp_pre=2, norm_mlp_post=3, conv_b=6, lru_b_a=7,
               lru_b_x=8, lru_lambda=9)
ROW_B_GATE, ROW_POOL_SCALE, ROW_CONV_W, ROW_LOSS, N_VEC_ROWS = 4, 10, 11, 15, 16


def _adamw_small(vec_parts, g_pool, g_wa, g_wx, me, params):
    d = vec_parts.shape[2]
    names = SMALL_ORDER
    n = len(names)
    cw_cols = params["conv_w"][0].shape[2]

    def body(me_ref, vec_ref, vecc_ref, gp_ref, gwa_ref, gwx_ref, *refs):
        wmv = refs[:3 * n]
        loss_ref = refs[3 * n]
        outs = refs[3 * n + 1:3 * n + 1 + 4 * n]
        vs, vsc = refs[3 * n + 1 + 4 * n:]
        acc, accc = vec_ref[0], vecc_ref[0]
        for k in range(1, N_DEV):
            acc = acc + vec_ref[k]
            accc = accc + vecc_ref[k]
        vs[...] = acc
        vsc[...] = accc
        loss_ref[...] = vs[ROW_LOSS:ROW_LOSS + 1, 0:128]

        def upd(a, g, idx):
            w_ref, m_ref, v_ref = wmv[3 * a:3 * a + 3]
            g_ref, d_ref, nm_ref, nv_ref = outs[4 * a:4 * a + 4]
            dl, m, v = _adam_math(w_ref[idx], g, m_ref[idx], v_ref[idx])
            g_ref[idx] = g
            d_ref[idx] = dl
            nm_ref[idx] = m
            nv_ref[idx] = v

        for a, name in enumerate(names):
            if name in VEC_ROW:
                r = VEC_ROW[name]
                upd(a, vs[r:r + 1, :], (slice(None), slice(None)))
            elif name == "b_gate":
                for half in range(2):
                    r = ROW_B_GATE + half
                    upd(a, vs[r:r + 1, :], (slice(None), slice(half * d, (half + 1) * d)))
            elif name == "pool_scale":
                width = params[name][0].shape[1]
                upd(a, vs[ROW_POOL_SCALE:ROW_POOL_SCALE + 1, 0:width], (slice(None), slice(None)))
            elif name == "conv_w":
                upd(a, vsc[ROW_CONV_W:ROW_CONV_W + 4, :], (0,))
            elif name == "pool_w":
                upd(a, gp_ref[...], (Ellipsis,))
            elif name == "lru_w_a":
                upd(a, gwa_ref[...], (Ellipsis,))
            elif name == "lru_w_x":
                upd(a, gwx_ref[...], (Ellipsis,))
            else:
                raise ValueError(name)

    def whole(shape):
        nd = len(shape)
        return pl.BlockSpec(tuple(shape), lambda i, me_ref: (0,) * nd)

    in_specs = [
        whole(vec_parts.shape),
        pl.BlockSpec((N_DEV, N_VEC_ROWS, cw_cols), lambda i, me_ref: (0, 0, me_ref[0])),
        whole(g_pool.shape), whole(g_wa.shape), whole(g_wx.shape),
    ]
    args = [vec_parts, vec_parts, g_pool, g_wa, g_wx]
    out_specs = [whole((1, 128))]
    out_shape = [jax.ShapeDtypeStruct((1, 128), F32)]
    for name in names:
        for arr in params[name]:
            in_specs.append(whole(arr.shape))
            args.append(arr)
        shp = params[name][0].shape
        out_specs += [whole(shp)] * 4
        out_shape += [jax.ShapeDtypeStruct(shp, F32)] * 4
    grid_spec = pltpu.PrefetchScalarGridSpec(
        num_scalar_prefetch=1, grid=(1,), in_specs=in_specs, out_specs=out_specs,
        scratch_shapes=[pltpu.VMEM((N_VEC_ROWS, d), F32), pltpu.VMEM((N_VEC_ROWS, cw_cols), F32)])
    outs = pl.pallas_call(
        body, name="adamw_small", grid_spec=grid_spec, out_shape=out_shape,
        compiler_params=pltpu.CompilerParams(
            dimension_semantics=("arbitrary",), vmem_limit_bytes=V7X_VMEM_LIMIT_BYTES),
    )(me, *_in_hbm(args))
    return outs[0], {name: tuple(outs[1 + 4 * a:5 + 4 * a]) for a, name in enumerate(names)}


def _rs_sum(fulls, recvs, shard_ids, slot_ids, name):
    n = len(fulls)

    def body(sh_ref, sl_ref, *refs):
        s = pl.program_id(0)
        for a in range(n):
            full_ref, recv_ref = refs[2 * a], refs[2 * a + 1]
            own_ref, send_ref = refs[2 * n + 2 * a], refs[2 * n + 2 * a + 1]
            v = full_ref[...] + recv_ref[...].astype(F32)

            @pl.when(s == 0)
            def _(own_ref=own_ref, v=v):
                own_ref[...] = v

            @pl.when(s > 0)
            def _(send_ref=send_ref, v=v):
                send_ref[...] = v.astype(send_ref.dtype)

    in_specs, out_specs, out_shape, args = [], [], [], []
    for full, recv in zip(fulls, recvs):
        r, rest = recv.shape[1], tuple(recv.shape[2:])
        zeros = (0,) * len(rest)
        in_specs += [
            pl.BlockSpec((r,) + rest, lambda s, sh, sl, zeros=zeros: (sh[s],) + zeros),
            pl.BlockSpec((None, r) + rest, lambda s, sh, sl, zeros=zeros: (sl[s], 0) + zeros),
        ]
        out_specs += [
            pl.BlockSpec((None, r) + rest, lambda s, sh, sl, zeros=zeros: (0, 0) + zeros),
            pl.BlockSpec((None, r) + rest, lambda s, sh, sl, zeros=zeros: (jnp.maximum(s - 1, 0), 0) + zeros),
        ]
        out_shape += [jax.ShapeDtypeStruct((1, r) + rest, F32), jax.ShapeDtypeStruct((3, r) + rest, recv.dtype)]
        args += [full, recv]
    grid_spec = pltpu.PrefetchScalarGridSpec(
        num_scalar_prefetch=2, grid=(4,), in_specs=in_specs, out_specs=out_specs)
    outs = pl.pallas_call(
        body,
        name=name,
        grid_spec=grid_spec,
        out_shape=out_shape,
        compiler_params=pltpu.CompilerParams(
            dimension_semantics=("arbitrary",), vmem_limit_bytes=V7X_VMEM_LIMIT_BYTES),
    )(shard_ids, slot_ids, *_in_hbm(args))
    return [(outs[2 * a], outs[2 * a + 1]) for a in range(n)]


def _finals(pairs, name, carry=None):
    nb = 4
    n = len(pairs)

    def body(*refs):
        for a in range(n):
            own_ref, recv_ref = refs[2 * a], refs[2 * a + 1]
            acc = own_ref[...]
            for k in range(3):
                acc = acc + recv_ref[k].astype(F32)
            refs[2 * n + a][...] = acc

    in_specs, out_specs, out_shape, args = [], [], [], []
    for own, recv in pairs:
        _, rows, cols = own.shape
        in_specs += [pl.BlockSpec((None, rows // nb, cols), lambda i: (0, i, 0)),
                     pl.BlockSpec((3, rows // nb, cols), lambda i: (0, i, 0))]
        args += [own, recv]
        out_specs.append(pl.BlockSpec((rows // nb, cols), lambda i: (i, 0)))
        out_shape.append(jax.ShapeDtypeStruct((rows, cols), F32))
    return _call(body, name=name, grid=(nb,), in_specs=in_specs, out_specs=out_specs,
                 out_shape=out_shape, args=args, carry=carry)


def _rs_sums(fulls_f32, recv1, tag):
    x, y, c = _place()
    qs = jnp.stack([2 * x + y, 2 * (1 - x) + y, 2 * x + (1 - y), 2 * (1 - x) + (1 - y)]).astype(jnp.int32)
    shard_ids = 2 * qs + c
    return _rs_sum(fulls_f32, recv1, shard_ids, qs, "rs_sum_" + tag)


def _rs_level1(fulls_f32, fulls_send, tag):
    recv1 = _run_plan(_rs_sibling_plan(fulls_send), "rs_sibling_" + tag)
    return _rs_sums(fulls_f32, recv1, tag)


def _rows(g):
    return g.reshape(g.shape[0] * g.shape[1], g.shape[2])


def kernel(x, norm_mix_pre, norm_mix_post, norm_mlp_pre, norm_mlp_post, w_in, b_gate, conv_w, conv_b, lru_w_a, lru_b_a, lru_w_x, lru_b_x, lru_lambda, pool_w, pool_scale, w_lru_up, w_pool_up, w_o, w_ff1, w_ff2, loss_target, m_norm_mix_pre, m_norm_mix_post, m_norm_mlp_pre, m_norm_mlp_post, m_w_in, m_b_gate, m_conv_w, m_conv_b, m_lru_w_a, m_lru_b_a, m_lru_w_x, m_lru_b_x, m_lru_lambda, m_pool_w, m_pool_scale, m_w_lru_up, m_w_pool_up, m_w_o, m_w_ff1, m_w_ff2, v_norm_mix_pre, v_norm_mix_post, v_norm_mlp_pre, v_norm_mlp_post, v_w_in, v_b_gate, v_conv_w, v_conv_b, v_lru_w_a, v_lru_b_a, v_lru_w_x, v_lru_b_x, v_lru_lambda, v_pool_w, v_pool_scale, v_w_lru_up, v_w_pool_up, v_w_o, v_w_ff1, v_w_ff2):
    t, d = x.shape[1], x.shape[2]
    d_rnn = conv_b.shape[1]
    d_pool = pool_scale.shape[1]
    per = LRU_CB // LRU_HEAD_DIM
    xi, yi, ci = _place()
    me = 4 * xi + 2 * yi + ci

    x2d = x[0]
    tgt = loss_target[0]

    s_in = w_in[0].T.astype(BF16)
    s_cw = jnp.pad(conv_w[0], ((0, 4), (0, 0)))
    (s_lu, s_pu, s_o, s_f1, s_f2), (g_in, g_cw) = _to_bf16(
        [w_lru_up[0], w_pool_up[0], w_o[0], w_ff1[0], w_ff2[0]], carry=_ag_plan([s_in, s_cw]))
    w_int = _rows(g_in)
    conv_w_full = jnp.transpose(g_cw[:, :4, :], (1, 0, 2)).reshape(4, d_rnn)

    wa_bd, wx_bd = lru_w_a[0], lru_w_x[0]
    pw = pool_w[0]
    pw_bf = pw.astype(BF16)

    pool_block = (2 * d_rnn) // d_pool
    ga_block = (2 * d_rnn + d_pool) // 512
    gb_block = ga_block + d // 512
    g_block = d_rnn // 512

    r_f1, r_f2 = s_f1.shape[0], s_f2.shape[0]
    f1_cut = r_f1 // 4
    f2_cut = (3 * r_f2) // 8
    plan = _join([_ag_plan([s_lu, s_pu]), _ag_plan([s_f1], pieces=[(0, f1_cut)])])
    (proj, h1), got = _norm_proj(x2d, norm_mix_pre, w_int, carry=plan)
    (g_lu, g_pu), (g_f1,) = plan.split(got)
    plan = _join([_ag_plan([s_f1], pieces=[(f1_cut, r_f1 - f1_cut)], bufs=[g_f1]), _ag_plan([s_o])])
    (y_lru, h, lru_saved), got = _lru_fwd(
        proj, conv_w_full, conv_b, wa_bd, lru_b_a, wx_bd, lru_b_x, lru_lambda, carry=plan)
    (g_f1,), (g_o,) = plan.split(got)
    w_lu, w_og = _rows(g_lu), _rows(g_o)
    y_pool, p = _pool_fwd(proj, pw_bf, pool_scale, pool_block)
    (br_a, br_b, mix), (g_f2,) = _branch_mix(
        y_lru, y_pool, w_lu, g_pu, proj, b_gate, ga_block, gb_block,
        carry=_ag_plan([s_f2], pieces=[(0, f2_cut)]))
    (m, x2, h3), _ = _wo_norm(mix, w_og, x2d, norm_mix_post, norm_mlp_pre)
    (rf,), (g_f2,) = _ff1(
        h3, g_f1, carry=_ag_plan([s_f2], pieces=[(f2_cut, r_f2 - f2_cut)], bufs=[g_f2]))
    w_f2 = _rows(g_f2)
    dy, df, dg4, loss_part = _ff2_loss(rf, w_f2, x2, norm_mlp_post, tgt)

    (gw_ff2_32, gw_ff2_16), _ = _wgrad(rf, df, "wgrad_ff2", square_a=True)
    (d_f1,), r1_ff2 = _ff2_bwd(df, w_f2, rf, carry=_rs_sibling_plan([gw_ff2_16]))
    ((own_ff2, send_ff2),) = _rs_sums([gw_ff2_32], r1_ff2, "ff2")
    cut2 = (5 * send_ff2.shape[1]) // 16
    (gw_ff1_32, gw_ff1_16), (r2_ff2,) = _wgrad_cols(
        h3, d_f1, s_f1.shape[1], s_f1.shape[1], "wgrad_ff1",
        carry=_rs_chips_plan([send_ff2], pieces=[(0, cut2)]))
    plan = _join([_rs_chips_plan([send_ff2], pieces=[(cut2, send_ff2.shape[1] - cut2)], bufs=[r2_ff2]),
                  _rs_sibling_plan([gw_ff1_16])])
    (dx2, dm, dg3, dg2), got = _ff1_bwd_norms(d_f1, g_f1, dy, x2, norm_mlp_pre, m, norm_mix_post, carry=plan)
    (r2_ff2,), r1_ff1 = plan.split(got)
    ((own_ff1, send_ff1),) = _rs_sums([gw_ff1_32], r1_ff1, "ff1")
    own_ff1, send_ff1 = own_ff1.reshape((1,) + s_f1.shape), send_ff1.reshape((3,) + s_f1.shape)
    cut = send_ff1.shape[1] // 4
    (gw_o_32, gw_o_16), _ = _wgrad(mix, dm, "wgrad_o")
    (d_br_a, d_br_b, p_ga, p_gb, dbg_a, dbg_b), (r2_ff1,) = _wo_bwd_mix(
        dm, w_og, br_a, br_b, proj, b_gate, ga_block, gb_block,
        carry=_rs_chips_plan([send_ff1], pieces=[(0, cut)]))
    (gw_lu_32, gw_lu_16), _ = _wgrad(y_lru, d_br_a, "wgrad_lru_up")
    (gw_pu_32, gw_pu_16), _ = _wgrad_cols(y_pool, d_br_b, s_pu.shape[1], d, "wgrad_pool_up")
    (dh, p_g), r1_mid = _lru_up_bwd(
        d_br_a, w_lu, proj, h, g_block,
        carry=_rs_sibling_plan([gw_o_16, gw_lu_16, gw_pu_16]))
    mid = _rs_sums([gw_o_32, gw_lu_32, gw_pu_32], r1_mid, "mid")
    plan = _join([_rs_chips_plan([send_ff1], pieces=[(cut, send_ff1.shape[1] - cut)], bufs=[r2_ff1]),
                  _rs_chips_plan([mid[0][1]])])
    (p_x, dwa, db_a, dwx, db_x, dlam, dconv_w, dconv_b), got = _lru_bwd(
        dh, h, lru_saved, proj, conv_w_full, wa_bd, wx_bd, lru_lambda, carry=plan)
    (r2_ff1,), (r2_o,) = plan.split(got)
    p_p, dpool_w, dpool_scale = _pool_bwd(d_br_b, g_pu, p, pw, pool_scale)
    parts = [p_x, p_g, p_p, p_ga, p_gb]
    gw_in, (r2_lu, r2_pu) = _wgrad_parts(
        parts, h1, "wgrad_in", carry=_rs_chips_plan([mid[1][1], mid[2][1]]))
    r2_mid = [r2_o, r2_lu, r2_pu]
    tail = _rs_level1([gw_in[0], dpool_w.reshape(N_DEV, -1, POOL_GROUP_DIM), dwa, dwx],
                      [gw_in[1], dpool_w.reshape(N_DEV, -1, POOL_GROUP_DIM), dwa, dwx], "in")
    (grad_x, dg1), r2_tail = _win_bwd_norm(parts, w_int, dx2, x2d, norm_mix_pre,
                                           carry=_rs_chips_plan([s for _, s in tail]))

    def flat2(a):
        return a.reshape(a.shape[0], -1, a.shape[-1])

    fin_small, _ = _finals([
        (flat2(tail[1][0]), flat2(r2_tail[1])), (flat2(tail[2][0]), flat2(r2_tail[2])),
        (flat2(tail[3][0]), flat2(r2_tail[3])),
    ], "rs_finals_small")

    def pad_row(a):
        return jnp.pad(a, ((0, 0), (0, d - a.shape[1])))

    vecs = jnp.concatenate([dg1, dg2, dg3, dg4, dbg_a, dbg_b, dconv_b, db_a, db_x, dlam,
                            pad_row(dpool_scale), dconv_w, pad_row(loss_part)], axis=0)
    assert vecs.shape[0] == N_VEC_ROWS
    vec_parts, g_pool, g_wa, g_wx = _run_plan(_ag_plan([vecs] + fin_small), "ag_tail")

    big_names = ["w_in", "w_lru_up", "w_pool_up", "w_o", "w_ff1", "w_ff2"]
    big_w = [w_in[0].T, w_lru_up[0], w_pool_up[0], w_o[0], w_ff1[0], w_ff2[0]]
    big_g = [(tail[0][0], r2_tail[0]), (mid[1][0], r2_mid[1]),
             (mid[2][0].reshape((1,) + s_pu.shape), r2_mid[2].reshape((3,) + s_pu.shape)),
             (mid[0][0], r2_mid[0]), (own_ff1, r2_ff1), (own_ff2, r2_ff2)]
    big_m = [m_w_in[0].T, m_w_lru_up[0], m_w_pool_up[0], m_w_o[0], m_w_ff1[0], m_w_ff2[0]]
    big_v = [v_w_in[0].T, v_w_lru_up[0], v_w_pool_up[0], v_w_o[0], v_w_ff1[0], v_w_ff2[0]]
    big_out = _adamw_big(big_w, big_g, big_m, big_v)
    big_out[0] = tuple(o.T for o in big_out[0])

    small = dict(
        norm_mix_pre=(norm_mix_pre, m_norm_mix_pre, v_norm_mix_pre),
        norm_mix_post=(norm_mix_post, m_norm_mix_post, v_norm_mix_post),
        norm_mlp_pre=(norm_mlp_pre, m_norm_mlp_pre, v_norm_mlp_pre),
        norm_mlp_post=(norm_mlp_post, m_norm_mlp_post, v_norm_mlp_post),
        b_gate=(b_gate, m_b_gate, v_b_gate), conv_w=(conv_w, m_conv_w, v_conv_w),
        conv_b=(conv_b, m_conv_b, v_conv_b), lru_w_a=(lru_w_a, m_lru_w_a, v_lru_w_a),
        lru_b_a=(lru_b_a, m_lru_b_a, v_lru_b_a), lru_w_x=(lru_w_x, m_lru_w_x, v_lru_w_x),
        lru_b_x=(lru_b_x, m_lru_b_x, v_lru_b_x), lru_lambda=(lru_lambda, m_lru_lambda, v_lru_lambda),
        pool_w=(pool_w, m_pool_w, v_pool_w), pool_scale=(pool_scale, m_pool_scale, v_pool_scale))
    loss_row, small_out = _adamw_small(
        vec_parts, g_pool.reshape(pool_w.shape), g_wa.reshape(lru_w_a.shape), g_wx.reshape(lru_w_x.shape),
        jnp.reshape(me, (1,)).astype(jnp.int32), small)
    grads = {n: o[0] for n, o in small_out.items()}
    delta = {n: o[1] for n, o in small_out.items()}
    new_m = {n: o[2] for n, o in small_out.items()}
    new_v = {n: o[3] for n, o in small_out.items()}

    for name, (g, dl, nm, nv) in zip(big_names, big_out):
        grads[name], delta[name], new_m[name], new_v[name] = g[None], dl[None], nm[None], nv[None]

    loss = loss_row[0, 0]
    order = ["norm_mix_pre", "norm_mix_post", "norm_mlp_pre", "norm_mlp_post", "w_in", "b_gate", "conv_w",
             "conv_b", "lru_w_a", "lru_b_a", "lru_w_x", "lru_b_x", "lru_lambda", "pool_w", "pool_scale",
             "w_lru_up", "w_pool_up", "w_o", "w_ff1", "w_ff2"]
    return (loss, grad_x[None], *[grads[n] for n in order], *[delta[n] for n in order],
            *[new_m[n] for n in order], *[new_v[n] for n in order])
```

```python
import functools
import math
import operator
import types

import jax
import jax.numpy as jnp
from jax import lax
from jax.experimental import pallas as pl
from jax.experimental.pallas import tpu as pltpu

F32 = jnp.float32
BF16 = jnp.bfloat16
NORM_EPS = 1e-6
LRU_C = 8.0
N_LRU_HEADS = 16
LRU_HEAD_DIM = 64
POOL_WINDOWS = (2, 4, 8, 16)
POOL_GROUP_DIM = 128
ADAM_LR = 0.001
ADAM_B1 = 0.9
ADAM_B2 = 0.999
ADAM_EPS = 1e-08
ADAM_WD = 0.01
ADAM_STEP = 10
N_DEV = 8
V7X_VMEM_LIMIT_BYTES = 56 * 1024 * 1024
LRU_CB = 256
MESH = pl.DeviceIdType.MESH
ANY = pl.BlockSpec(memory_space=pl.ANY)


def _tile(n, pref):
    t = min(n, pref)
    assert n % t == 0, (n, pref)
    return t


def _dot_nn(a, b):
    return lax.dot_general(a, b, (((1,), (0,)), ((), ())), preferred_element_type=F32)


def _dot_nt(a, b):
    return lax.dot_general(a, b, (((1,), (1,)), ((), ())), preferred_element_type=F32)


def _dot_tn(a, b):
    return lax.dot_general(a, b, (((0,), (0,)), ((), ())), preferred_element_type=F32)


def _row_chunks(n_rows, fn, chunk=256):
    chunk = min(chunk, n_rows)
    assert n_rows % chunk == 0

    def step(r, carry):
        fn(pl.ds(pl.multiple_of(r * chunk, chunk), chunk))
        return carry

    lax.fori_loop(0, n_rows // chunk, step, 0)


def _sig(x):
    return 1.0 / (1.0 + jnp.exp(-x))


def _rms_hat(x):
    r = lax.rsqrt(jnp.mean(x * x, axis=-1, keepdims=True) + NORM_EPS)
    return x * r, r


def _rms_bwd(dn, xhat, r, g):
    q = dn * g
    dx = r * (q - xhat * jnp.mean(q * xhat, axis=-1, keepdims=True))
    dg = jnp.sum(dn * xhat, axis=0, keepdims=True)
    return dx, dg


_GELU_K = math.sqrt(2.0 / math.pi)
_GELU_C = 0.044715


def _gelu_and_grad(g):
    t = jnp.tanh(_GELU_K * (g + _GELU_C * g * g * g))
    val = 0.5 * g * (1.0 + t)
    grad = 0.5 * (1.0 + t) + 0.5 * g * (1.0 - t * t) * (_GELU_K * (1.0 + 3.0 * _GELU_C * g * g))
    return val, grad


def _softplus_neg(lam):
    z = -lam
    e = jnp.exp(-jnp.abs(z))
    u = 1.0 + e
    d = u - 1.0
    l1p = jnp.where(d == 0.0, e, jnp.log(u) * (e / jnp.where(d == 0.0, 1.0, d)))
    return jnp.maximum(z, 0.0) + l1p


def _lru_gates(xc, wa, ba, wx, bx, lam):
    xcb = xc.astype(BF16)
    r = _sig(_dot_nn(xcb, wa) + ba)
    i = _sig(_dot_nn(xcb, wx) + bx)
    sp = _softplus_neg(lam)
    log_a = (-LRU_C) * r * sp
    a = jnp.exp(log_a)
    mult = jnp.sqrt(-jnp.tanh(log_a) * (1.0 + a * a))
    return xcb, r, i, sp, log_a, a, mult


def _place():
    return lax.axis_index("x"), lax.axis_index("y"), lax.axis_index("c")


def _ag_plan(shards, pieces=None, bufs=None):
    na = len(shards)
    n_kinds = 7

    def parts(ins, outs, sems):
        send_sems, recv_sems, local_sems = sems
        x, y, c = _place()
        me, sibling = (x, y, c), (x, y, 1 - c)
        x_nb, y_nb, diag = (1 - x, y), (x, 1 - y), (1 - x, 1 - y)
        relay_src = (c * (1 - x) + (1 - c) * x, c * y + (1 - c) * (1 - y))
        relay_dst = (c * x + (1 - c) * (1 - x), c * (1 - y) + (1 - c) * y)

        def own(a):
            return ins[a] if pieces is None else ins[a].at[pl.ds(*pieces[a])]

        def slot(a, px, py, pc):
            idx = 4 * px + 2 * py + pc
            return outs[a].at[idx] if pieces is None else outs[a].at[idx, pl.ds(*pieces[a])]

        def copy(a, k, block, to, src=None):
            return pltpu.make_async_remote_copy(
                src_ref=slot(a, *block) if src is None else src,
                dst_ref=slot(a, *block),
                send_sem=send_sems.at[a * n_kinds + k],
                recv_sem=recv_sems.at[a * n_kinds + k],
                device_id=to,
                device_id_type=MESH,
            )

        mine = [pltpu.make_async_copy(own(a), slot(a, *me), local_sems.at[a]) for a in range(na)]
        first, second, third = [], [], []
        for a in range(na):
            first += [copy(a, 0, me, sibling, src=own(a)), copy(a, 1, me, (*x_nb, c), src=own(a)),
                      copy(a, 2, me, (*y_nb, c), src=own(a))]
            second += [copy(a, 3, (*relay_src, c), (*relay_dst, c)), copy(a, 4, (*x_nb, c), sibling),
                       copy(a, 5, (*y_nb, c), sibling)]
            third.append(copy(a, 6, (*diag, c), sibling))
        return sibling, c, x_nb, y_nb, diag, copy, mine, first, second, third

    def start(ins, outs, sems):
        _, _, _, _, _, _, mine, first, _, _ = parts(ins, outs, sems)
        for cp in mine + first:
            cp.start()

    def middle(ins, outs, sems):
        _, c, x_nb, y_nb, _, copy, _, _, second, _ = parts(ins, outs, sems)
        for a in range(na):
            copy(a, 1, (*x_nb, c), (*x_nb, c)).wait_recv()
            copy(a, 2, (*y_nb, c), (*y_nb, c)).wait_recv()
        for cp in second:
            cp.start()

    def finish(ins, outs, sems):
        sibling, c, x_nb, y_nb, diag, copy, mine, first, second, third = parts(ins, outs, sems)
        for a in range(na):
            copy(a, 3, (*diag, c), (*diag, c)).wait_recv()
            third[a].start()
        for a in range(na):
            copy(a, 0, sibling, sibling).wait_recv()
            copy(a, 4, (*x_nb, 1 - c), sibling).wait_recv()
            copy(a, 5, (*y_nb, 1 - c), sibling).wait_recv()
            copy(a, 6, (*diag, 1 - c), sibling).wait_recv()
        for cp in first + second + third:
            cp.wait_send()
        for cp in mine:
            cp.wait()

    return types.SimpleNamespace(
        ins=list(shards) + list(bufs or []),
        out_shapes=[jax.ShapeDtypeStruct((N_DEV,) + s.shape, s.dtype) for s in shards],
        sems=[pltpu.SemaphoreType.DMA((n_kinds * na,)), pltpu.SemaphoreType.DMA((n_kinds * na,)),
              pltpu.SemaphoreType.DMA((na,))],
        aliases=[(na + a, a) for a in range(na)] if bufs else [],
        peers=frozenset({"sibling", "neighbours"}), start=start, middle=middle, finish=finish)


def _rs_sibling_plan(fulls):
    na = len(fulls)
    rs = [f.shape[0] // N_DEV for f in fulls]

    def copies(ins, outs, sems):
        send_sems, recv_sems = sems
        x, y, c = _place()
        out = []
        for a in range(na):
            for q in range(4):
                shard = 2 * q + (1 - c)
                out.append(pltpu.make_async_remote_copy(
                    src_ref=ins[a].at[pl.ds(shard * rs[a], rs[a])],
                    dst_ref=outs[a].at[q],
                    send_sem=send_sems.at[a * 4 + q],
                    recv_sem=recv_sems.at[a * 4 + q],
                    device_id=(x, y, 1 - c),
                    device_id_type=MESH,
                ))
        return out

    def start(ins, outs, sems):
        for cp in copies(ins, outs, sems):
            cp.start()

    def finish(ins, outs, sems):
        for cp in copies(ins, outs, sems):
            cp.wait()

    return types.SimpleNamespace(
        ins=list(fulls),
        out_shapes=[jax.ShapeDtypeStruct((4, r) + f.shape[1:], f.dtype) for r, f in zip(rs, fulls)],
        sems=[pltpu.SemaphoreType.DMA((4 * na,)), pltpu.SemaphoreType.DMA((4 * na,))],
        peers=frozenset({"sibling"}), start=start, finish=finish)


def _rs_chips_plan(sends, pieces=None, bufs=None):
    na = len(sends)

    def copies(ins, outs, sems):
        send_sems, recv_sems = sems
        x, y, c = _place()
        chips = [(1 - x, y), (x, 1 - y), (1 - x, 1 - y)]
        out = []
        for a in range(na):
            for k, chip in enumerate(chips):
                rows = (k,) if pieces is None else (k, pl.ds(*pieces[a]))
                out.append(pltpu.make_async_remote_copy(
                    src_ref=ins[a].at[rows],
                    dst_ref=outs[a].at[rows],
                    send_sem=send_sems.at[a * 3 + k],
                    recv_sem=recv_sems.at[a * 3 + k],
                    device_id=(*chip, c),
                    device_id_type=MESH,
                ))
        return out

    def start(ins, outs, sems):
        for cp in copies(ins, outs, sems):
            cp.start()

    def finish(ins, outs, sems):
        for cp in copies(ins, outs, sems):
            cp.wait()

    return types.SimpleNamespace(
        ins=list(sends) + list(bufs or []),
        out_shapes=[jax.ShapeDtypeStruct(s.shape, s.dtype) for s in sends],
        sems=[pltpu.SemaphoreType.DMA((3 * na,)), pltpu.SemaphoreType.DMA((3 * na,))],
        aliases=[(na + a, a) for a in range(na)] if bufs else [],
        peers=frozenset({"chips"}), start=start, finish=finish)


def _join(plans):
    ins, outs, sems, aliases, offs = [], [], [], [], []
    for p in plans:
        offs.append((len(ins), len(outs), len(sems)))
        aliases += [(len(ins) + ci, len(outs) + co) for ci, co in getattr(p, "aliases", [])]
        ins += p.ins
        outs += p.out_shapes
        sems += p.sems

    def cut(p, off, i, o, s):
        return (i[off[0]:off[0] + len(p.ins)], o[off[1]:off[1] + len(p.out_shapes)],
                s[off[2]:off[2] + len(p.sems)])

    def start(i, o, s):
        for p, off in zip(plans, offs):
            p.start(*cut(p, off, i, o, s))

    def middle(i, o, s):
        for p, off in zip(plans, offs):
            if getattr(p, "middle", None) is not None:
                p.middle(*cut(p, off, i, o, s))

    def finish(i, o, s):
        for p, off in zip(plans, offs):
            p.finish(*cut(p, off, i, o, s))

    def split(results):
        return [list(results[off[1]:off[1] + len(p.out_shapes)]) for p, off in zip(plans, offs)]

    return types.SimpleNamespace(ins=ins, out_shapes=outs, sems=sems, aliases=aliases,
                                 peers=frozenset().union(*[p.peers for p in plans]),
                                 start=start, middle=middle, finish=finish, split=split)


COLLECTIVE_ID = {frozenset({"sibling"}): 0, frozenset({"chips"}): 1, frozenset({"sibling", "chips"}): 2,
                 frozenset({"sibling", "neighbours"}): 3}


def _handshake(peers):
    x, y, c = _place()
    devs = []
    if "sibling" in peers:
        devs.append((x, y, 1 - c))
    if "neighbours" in peers:
        devs += [(1 - x, y, c), (x, 1 - y, c)]
    if "chips" in peers:
        assert "neighbours" not in peers
        devs += [(1 - x, y, c), (x, 1 - y, c), (1 - x, 1 - y, c)]
    barrier = pltpu.get_barrier_semaphore()
    for dev in devs:
        pl.semaphore_signal(barrier, inc=1, device_id=dev, device_id_type=MESH)
    pl.semaphore_wait(barrier, len(devs))


def _in_hbm(args):
    return [pltpu.with_memory_space_constraint(a, pltpu.HBM) for a in args]


def _run_plan(plan, name):
    n_in, n_out = len(plan.ins), len(plan.out_shapes)

    def body(*refs):
        ins, outs, sems = refs[:n_in], refs[n_in:n_in + n_out], refs[n_in + n_out:]
        _handshake(plan.peers)
        plan.start(ins, outs, sems)
        if getattr(plan, "middle", None) is not None:
            plan.middle(ins, outs, sems)
        plan.finish(ins, outs, sems)

    return pl.pallas_call(
        body,
        name=name,
        in_specs=[ANY] * n_in,
        out_specs=[ANY] * n_out,
        out_shape=plan.out_shapes,
        scratch_shapes=plan.sems,
        input_output_aliases=dict(getattr(plan, "aliases", [])),
        compiler_params=pltpu.CompilerParams(collective_id=COLLECTIVE_ID[plan.peers]),
    )(*_in_hbm(plan.ins))


def _call(body, *, name, grid, in_specs, out_specs, out_shape, args, scratch_shapes=(), aliases=None,
          carry=None):
    n_in, n_out, n_scr = len(in_specs), len(out_shape), len(scratch_shapes)
    params = pltpu.CompilerParams(
        dimension_semantics=("arbitrary",) * len(grid), vmem_limit_bytes=V7X_VMEM_LIMIT_BYTES)
    if carry is None:
        outs = pl.pallas_call(
            body, name=name, grid=grid, in_specs=list(in_specs), out_specs=list(out_specs),
            out_shape=list(out_shape), scratch_shapes=list(scratch_shapes),
            input_output_aliases=aliases or {}, compiler_params=params)(*_in_hbm(args))
        return list(outs), []
    c_in, c_out = len(carry.ins), len(carry.out_shapes)

    def full(*refs):
        p = 0
        ins = refs[p:p + n_in]
        p += n_in
        cins = refs[p:p + c_in]
        p += c_in
        outs = refs[p:p + n_out]
        p += n_out
        couts = refs[p:p + c_out]
        p += c_out
        scr = refs[p:p + n_scr]
        csems = refs[p + n_scr:]
        ids = [pl.program_id(a) for a in range(len(grid))]
        first = functools.reduce(operator.and_, [i == 0 for i in ids])
        last = functools.reduce(operator.and_, [i == g - 1 for i, g in zip(ids, grid)])

        @pl.when(first)
        def _():
            _handshake(carry.peers)
            carry.start(cins, couts, csems)

        if getattr(carry, "middle", None) is not None:
            n_steps = math.prod(grid)
            flat = functools.reduce(lambda acc, ig: acc * ig[1] + ig[0], zip(ids, grid), 0)

            @pl.when(flat == (2 * n_steps) // 3)
            def _():
                carry.middle(cins, couts, csems)

        body(*ins, *outs, *scr)

        @pl.when(last)
        def _():
            carry.finish(cins, couts, csems)

    all_aliases = dict(aliases or {})
    all_aliases.update({n_in + ci: n_out + co for ci, co in getattr(carry, "aliases", [])})
    params = pltpu.CompilerParams(
        dimension_semantics=("arbitrary",) * len(grid), vmem_limit_bytes=V7X_VMEM_LIMIT_BYTES,
        collective_id=COLLECTIVE_ID[carry.peers])
    outs = pl.pallas_call(
        full, name=name, grid=grid,
        in_specs=list(in_specs) + [ANY] * c_in,
        out_specs=list(out_specs) + [ANY] * c_out,
        out_shape=list(out_shape) + list(carry.out_shapes),
        scratch_shapes=list(scratch_shapes) + list(carry.sems),
        input_output_aliases=all_aliases, compiler_params=params)(*_in_hbm(args), *_in_hbm(carry.ins))
    return list(outs[:n_out]), list(outs[n_out:])


def _to_bf16(arrs, carry=None):
    n = len(arrs)

    def body(*refs):
        @pl.when(pl.program_id(0) == 0)
        def _():
            for src, dst in zip(refs[:n], refs[n:]):
                dst[...] = src[...].astype(BF16)

    specs = [pl.BlockSpec(a.shape, lambda i: (0, 0)) for a in arrs]
    return _call(body, name="cast_weights", grid=(2,), in_specs=specs, out_specs=specs,
                 out_shape=[jax.ShapeDtypeStruct(a.shape, BF16) for a in arrs], args=arrs, carry=carry)


def _norm_proj(x, g1, w_int, carry=None):
    t, d = x.shape
    n = w_int.shape[0]
    tt, tn = _tile(t, 2048), _tile(n, 512)

    def body(x_ref, g_ref, w_ref, proj_ref, h1_ref, h1_s):
        @pl.when(pl.program_id(1) == 0)
        def _():
            def norm_rows(rows):
                xhat, _ = _rms_hat(x_ref[rows, :])
                h = (xhat * g_ref[...]).astype(BF16)
                h1_s[rows, :] = h
                h1_ref[rows, :] = h

            _row_chunks(tt, norm_rows)

        proj_ref[...] = _dot_nt(h1_s[...], w_ref[...]).astype(BF16)

    return _call(
        body, name="norm_proj", grid=(t // tt, n // tn),
        in_specs=[
            pl.BlockSpec((tt, d), lambda i, j: (i, 0)),
            pl.BlockSpec((1, d), lambda i, j: (0, 0)),
            pl.BlockSpec((tn, d), lambda i, j: (j, 0)),
        ],
        out_specs=[
            pl.BlockSpec((tt, tn), lambda i, j: (i, j)),
            pl.BlockSpec((tt, d), lambda i, j: (i, 0)),
        ],
        out_shape=[jax.ShapeDtypeStruct((t, n), BF16), jax.ShapeDtypeStruct((t, d), BF16)],
        scratch_shapes=[pltpu.VMEM((tt, d), BF16)],
        args=(x, g1, w_int), carry=carry)


def _scan_rows(av, bv, reverse):
    tc = av.shape[0]
    row = lax.broadcasted_iota(jnp.int32, av.shape, 0)
    s = 1
    while s < tc:
        if s < 8:
            keep = (row < tc - s) if reverse else (row >= s)
            shift = (tc - s) if reverse else s
            a_sh = jnp.where(keep, pltpu.roll(av, shift, 0), 1.0)
            b_sh = jnp.where(keep, pltpu.roll(bv, shift, 0), 0.0)
            bv = av * b_sh + bv
            av = av * a_sh
        elif reverse:
            bv = jnp.concatenate([av[:tc - s] * bv[s:] + bv[:tc - s], bv[tc - s:]], axis=0)
            av = jnp.concatenate([av[:tc - s] * av[s:], av[tc - s:]], axis=0)
        else:
            bv = jnp.concatenate([bv[:s], av[s:] * bv[:tc - s] + bv[s:]], axis=0)
            av = jnp.concatenate([av[:s], av[s:] * av[:tc - s]], axis=0)
        s *= 2
    return av, bv


N_LRU_SAVED = 5


def _fill_block_diag(w_ref, bd_ref):
    bd_ref[...] = jnp.zeros_like(bd_ref)
    hd = LRU_HEAD_DIM
    for k in range(w_ref.shape[0]):
        bd_ref[k * hd:(k + 1) * hd, k * hd:(k + 1) * hd] = w_ref[k].astype(BF16)


def _lru_fwd(proj, conv_w, conv_b, w_a, b_a, w_x, b_x, lam, carry=None):
    t = proj.shape[0]
    dr = conv_b.shape[1]
    cb = LRU_CB
    tc = _tile(t, 256)
    ncb, ntc = dr // cb, t // tc

    def body(xp_ref, g_ref, cw_ref, cb_ref, wa_ref, ba_ref, wx_ref, bx_ref, lam_ref,
             y_ref, h_ref, saved_ref, prevx_s, hlast_s, wa_s, wx_s):
        c = pl.program_id(1)

        @pl.when(c == 0)
        def _():
            prevx_s[...] = jnp.zeros_like(prevx_s)
            hlast_s[...] = jnp.zeros_like(hlast_s)
            _fill_block_diag(wa_ref, wa_s)
            _fill_block_diag(wx_ref, wx_s)

        x = xp_ref[...].astype(F32)
        prev = prevx_s[...]
        row = lax.broadcasted_iota(jnp.int32, x.shape, 0)

        def sh(j):
            return jnp.where(row >= j, pltpu.roll(x, j, 0), pltpu.roll(prev, j, 0))

        xc = (cb_ref[...] + cw_ref[0:1, :] * sh(3) + cw_ref[1:2, :] * sh(2)
              + cw_ref[2:3, :] * sh(1) + cw_ref[3:4, :] * x)
        prevx_s[...] = x
        _, r, i, _, _, a, mult = _lru_gates(xc, wa_s[...], ba_ref[...], wx_s[...], bx_ref[...],
                                            lam_ref[...])
        for k, val in enumerate((xc, r, i, a, mult)):
            saved_ref[:, k * cb:(k + 1) * cb] = val
        av, bv = _scan_rows(a, mult * (i * xc), reverse=False)
        h = av * hlast_s[...] + bv
        h_ref[...] = h
        hlast_s[...] = h_ref[tc - 1:tc, :]
        gel, _ = _gelu_and_grad(g_ref[...].astype(F32))
        y_ref[...] = (h * gel).astype(BF16)

    vec = pl.BlockSpec((1, cb), lambda j, c: (0, j))
    blk = pl.BlockSpec((tc, cb), lambda j, c: (c, j))
    mat = pl.BlockSpec((cb // LRU_HEAD_DIM, LRU_HEAD_DIM, LRU_HEAD_DIM), lambda j, c: (j, 0, 0))
    return _call(
        body, name="lru_fwd", grid=(ncb, ntc),
        in_specs=[
            blk,
            pl.BlockSpec((tc, cb), lambda j, c: (c, ncb + j)),
            pl.BlockSpec((4, cb), lambda j, c: (0, j)),
            vec, mat, vec, mat, vec, vec,
        ],
        out_specs=[blk, blk, pl.BlockSpec((tc, N_LRU_SAVED * cb), lambda j, c: (c, j))],
        out_shape=[jax.ShapeDtypeStruct((t, dr), BF16), jax.ShapeDtypeStruct((t, dr), F32),
                   jax.ShapeDtypeStruct((t, N_LRU_SAVED * dr), F32)],
        scratch_shapes=[pltpu.VMEM((tc, cb), F32), pltpu.VMEM((1, cb), F32),
                        pltpu.VMEM((cb, cb), BF16), pltpu.VMEM((cb, cb), BF16)],
        args=(proj, proj, conv_w, conv_b, w_a, b_a, w_x, b_x, lam), carry=carry)


def _pool_select(col, vals):
    out = vals[3]
    for g in (2, 1, 0):
        out = jnp.where(col < (g + 1) * POOL_GROUP_DIM, vals[g], out)
    return out


def _pool_fwd(proj, pool_w, pool_scale, col_block):
    t = proj.shape[0]
    dp = pool_scale.shape[1]
    tc = _tile(t, 256)
    ntc = t // tc

    def body(x_ref, w_ref, sc_ref, y_ref, p_ref, px, p2, p4, p8):
        c = pl.program_id(0)

        @pl.when(c == 0)
        def _():
            for s in (px, p2, p4, p8):
                s[...] = jnp.zeros_like(s)

        x = x_ref[...].astype(F32)
        row = lax.broadcasted_iota(jnp.int32, x.shape, 0)
        col = lax.broadcasted_iota(jnp.int32, x.shape, 1)

        def sh(v, pv, j):
            return jnp.where(row >= j, pltpu.roll(v, j, 0), pltpu.roll(pv[...], j, 0))

        s2 = x + sh(x, px, 1)
        s4 = s2 + sh(s2, p2, 2)
        s8 = s4 + sh(s4, p4, 4)
        s16 = s8 + sh(s8, p8, 8)
        px[...] = x
        p2[...] = s2
        p4[...] = s4
        p8[...] = s8
        wsum = _pool_select(col, (s2, s4, s8, s16))
        win = _pool_select(col, POOL_WINDOWS)
        cnt = jnp.minimum(c * tc + row + 1, win).astype(F32)
        p = wsum / cnt - x
        pb = p.astype(BF16)
        p_ref[...] = pb
        for g in range(len(POOL_WINDOWS)):
            sl = slice(g * POOL_GROUP_DIM, (g + 1) * POOL_GROUP_DIM)
            yg = _dot_nn(pb[:, sl], w_ref[g]) * sc_ref[:, sl]
            y_ref[:, sl] = yg.astype(BF16)

    return _call(
        body, name="pool_fwd", grid=(ntc,),
        in_specs=[
            pl.BlockSpec((tc, dp), lambda c: (c, col_block)),
            pl.BlockSpec(pool_w.shape, lambda c: (0, 0, 0)),
            pl.BlockSpec((1, dp), lambda c: (0, 0)),
        ],
        out_specs=[pl.BlockSpec((tc, dp), lambda c: (c, 0))] * 2,
        out_shape=[jax.ShapeDtypeStruct((t, dp), BF16)] * 2,
        scratch_shapes=[pltpu.VMEM((tc, dp), F32)] * 4,
        args=(proj, pool_w, pool_scale))[0]


def _branch_mix(y_lru, y_pool, w_lru_up, w_pool_upb, proj, b_gate, ga_block, gb_block, carry=None):
    t, d = y_lru.shape
    dp = y_pool.shape[1]
    bw = w_pool_upb.shape[2]
    tt, tn = _tile(t, 1024), 512
    nj = d // tn

    def body(yl_ref, yp_ref, wl_ref, wp_ref, ga_ref, gb_ref, ba_ref, bb_ref, bra_ref, brb_ref, mix_ref):
        br_a = _dot_nn(yl_ref[...], wl_ref[...])
        wp = jnp.concatenate([wp_ref[b] for b in range(tn // bw)], axis=1)
        br_b = _dot_nn(yp_ref[...], wp)
        bra_ref[...] = br_a.astype(BF16)
        brb_ref[...] = br_b.astype(BF16)
        ga = _sig(ga_ref[...].astype(F32) + ba_ref[...])
        gb = _sig(gb_ref[...].astype(F32) + bb_ref[...])
        mix_ref[...] = (ga * br_a + gb * br_b).astype(BF16)

    out = pl.BlockSpec((tt, tn), lambda j, i: (i, j))
    return _call(
        body, name="branch_mix", grid=(nj, t // tt),
        in_specs=[
            pl.BlockSpec((tt, d), lambda j, i: (i, 0)),
            pl.BlockSpec((tt, dp), lambda j, i: (i, 0)),
            pl.BlockSpec((d, tn), lambda j, i: (0, j)),
            pl.BlockSpec((tn // bw, dp, bw), lambda j, i: (j, 0, 0)),
            pl.BlockSpec((tt, tn), lambda j, i: (i, ga_block + j)),
            pl.BlockSpec((tt, tn), lambda j, i: (i, gb_block + j)),
            pl.BlockSpec((1, tn), lambda j, i: (0, j)),
            pl.BlockSpec((1, tn), lambda j, i: (0, nj + j)),
        ],
        out_specs=[out, out, out],
        out_shape=[jax.ShapeDtypeStruct((t, d), BF16)] * 3,
        args=(y_lru, y_pool, w_lru_up, w_pool_upb, proj, proj, b_gate, b_gate), carry=carry)


def _wo_norm(mix, w_o, x, g2, g3, carry=None):
    t, d = x.shape
    tt = _tile(t, 512)

    def body(mix_ref, w_ref, x_ref, g2_ref, g3_ref, m_ref, x2_ref, h3_ref):
        m = _dot_nn(mix_ref[...], w_ref[...])
        m_ref[...] = m
        mhat, _ = _rms_hat(m)
        x2 = x_ref[...] + mhat * g2_ref[...]
        x2_ref[...] = x2
        xhat, _ = _rms_hat(x2)
        h3_ref[...] = (xhat * g3_ref[...]).astype(BF16)

    row = pl.BlockSpec((tt, d), lambda i: (i, 0))
    vec = pl.BlockSpec((1, d), lambda i: (0, 0))
    return _call(
        body, name="wo_norm", grid=(t // tt,),
        in_specs=[row, pl.BlockSpec((d, d), lambda i: (0, 0)), row, vec, vec],
        out_specs=[row, row, row],
        out_shape=[
            jax.ShapeDtypeStruct((t, d), F32),
            jax.ShapeDtypeStruct((t, d), F32),
            jax.ShapeDtypeStruct((t, d), BF16),
        ],
        args=(mix, w_o, x, g2, g3), carry=carry)


def _ff1(h3, w_ff1b, carry=None):
    t, d = h3.shape
    nb, _, tn = w_ff1b.shape
    tt = _tile(t, 2048)

    def body(h_ref, w_ref, rf_ref):
        rf_ref[...] = jnp.maximum(_dot_nn(h_ref[...], w_ref[...]), 0.0).astype(BF16)

    out = pl.BlockSpec((tt, tn), lambda i, j: (i, j))
    return _call(
        body, name="ff1", grid=(t // tt, nb),
        in_specs=[pl.BlockSpec((tt, d), lambda i, j: (i, 0)), pl.BlockSpec((None, d, tn), lambda i, j: (j, 0, 0))],
        out_specs=[out],
        out_shape=[jax.ShapeDtypeStruct((t, nb * tn), BF16)],
        args=(h3, w_ff1b), carry=carry)


def _ff2_loss(rf, w_ff2, x2, g4, target):
    t, k = rf.shape
    d = x2.shape[1]
    tt, tk = _tile(t, 1024), _tile(k, 1024)
    nk = k // tk

    def body(a_ref, w_ref, x2_ref, g_ref, tg_ref, dy_ref, df_ref, dg_ref, loss_ref, acc):
        i, kk = pl.program_id(0), pl.program_id(1)

        @pl.when(kk == 0)
        def _():
            acc[...] = jnp.zeros_like(acc)

        @pl.when((i == 0) & (kk == 0))
        def _():
            dg_ref[...] = jnp.zeros_like(dg_ref)
            loss_ref[...] = jnp.zeros_like(loss_ref)

        rf_tile = a_ref[...]
        acc[...] += _dot_nn(rf_tile * rf_tile, w_ref[...])

        @pl.when(kk == nk - 1)
        def _():
            def tail(rows):
                fhat, r = _rms_hat(acc[rows, :])
                g = g_ref[...]
                e = x2_ref[rows, :] + fhat * g - tg_ref[rows, :]
                loss_ref[...] += 0.5 * jnp.sum(jnp.mean(e * e, axis=-1, keepdims=True))
                dy = e * (1.0 / d)
                dy_ref[rows, :] = dy.astype(BF16)
                df, dg = _rms_bwd(dy, fhat, r, g)
                df_ref[rows, :] = df.astype(BF16)
                dg_ref[...] += dg

            _row_chunks(tt, tail)

    row = pl.BlockSpec((tt, d), lambda i, kk: (i, 0))
    vec = pl.BlockSpec((1, d), lambda i, kk: (0, 0))
    return _call(
        body, name="ff2_loss", grid=(t // tt, nk),
        in_specs=[
            pl.BlockSpec((tt, tk), lambda i, kk: (i, kk)),
            pl.BlockSpec((tk, d), lambda i, kk: (kk, 0)),
            row, vec, row,
        ],
        out_specs=[row, row, vec, pl.BlockSpec((1, 128), lambda i, kk: (0, 0))],
        out_shape=[
            jax.ShapeDtypeStruct((t, d), BF16),
            jax.ShapeDtypeStruct((t, d), BF16),
            jax.ShapeDtypeStruct((1, d), F32),
            jax.ShapeDtypeStruct((1, 128), F32),
        ],
        scratch_shapes=[pltpu.VMEM((tt, d), F32)],
        args=(rf, w_ff2, x2, g4, target))[0]


def _ff2_bwd(df, w_ff2, rf, carry=None):
    t, d = df.shape
    n = w_ff2.shape[0]
    tt, tn = _tile(t, 2048), _tile(n, 512)

    def body(df_ref, w_ref, rf_ref, out_ref):
        d_act = _dot_nt(df_ref[...], w_ref[...])
        out_ref[...] = (d_act * (2.0 * rf_ref[...].astype(F32))).astype(BF16)

    blk = pl.BlockSpec((tt, tn), lambda i, j: (i, j))
    return _call(
        body, name="ff2_bwd", grid=(t // tt, n // tn),
        in_specs=[pl.BlockSpec((tt, d), lambda i, j: (i, 0)), pl.BlockSpec((tn, d), lambda i, j: (j, 0)), blk],
        out_specs=[blk],
        out_shape=[jax.ShapeDtypeStruct((t, n), BF16)],
        args=(df, w_ff2, rf), carry=carry)


def _wgrad(a, b, name, prev=None, row_off=0, rows=None, carry=None, square_a=False):
    t, m = a.shape
    n = b.shape[1]
    rows = m if rows is None else rows
    tm, tk = _tile(m, 512), _tile(t, 2048)
    nk = t // tk
    assert row_off % tm == 0
    off = row_off // tm

    def body(*refs):
        a_ref, b_ref = refs[0], refs[1]
        o32_ref, o16_ref, acc = refs[-3], refs[-2], refs[-1]
        kk = pl.program_id(1)

        @pl.when(kk == 0)
        def _():
            acc[...] = jnp.zeros_like(acc)

        a_tile = a_ref[...]
        acc[...] += _dot_tn(a_tile * a_tile if square_a else a_tile, b_ref[...])

        @pl.when(kk == nk - 1)
        def _():
            o32_ref[...] = acc[...]
            o16_ref[...] = acc[...].astype(BF16)

    in_specs = [pl.BlockSpec((tk, tm), lambda i, kk: (kk, i)), pl.BlockSpec((tk, n), lambda i, kk: (kk, 0))]
    args = [a, b]
    aliases = {}
    if prev is not None:
        in_specs += [ANY, ANY]
        args += list(prev)
        aliases = {2: 0, 3: 1}
    out = pl.BlockSpec((tm, n), lambda i, kk: (off + i, 0))
    return _call(
        body, name=name, grid=(m // tm, nk),
        in_specs=in_specs, out_specs=[out, out],
        out_shape=[jax.ShapeDtypeStruct((rows, n), F32), jax.ShapeDtypeStruct((rows, n), BF16)],
        scratch_shapes=[pltpu.VMEM((tm, n), F32)],
        aliases=aliases, args=args, carry=carry)


def _wgrad_parts(parts, b, name, carry=None):
    t, n = b.shape
    tm = 512
    bounds = []
    lo = 0
    for part in parts:
        assert part.shape[0] == t and part.shape[1] % tm == 0
        bounds.append((lo, lo + part.shape[1] // tm))
        lo += part.shape[1] // tm
    nm = lo
    np_ = len(parts)

    def body(*refs):
        p_refs, b_ref, o32_ref, o16_ref = refs[:np_], refs[np_], refs[np_ + 1], refs[np_ + 2]
        i = pl.program_id(0)
        for (lo_p, hi_p), p_ref in zip(bounds, p_refs):
            @pl.when((i >= lo_p) & (i < hi_p))
            def _(p_ref=p_ref):
                res = _dot_tn(p_ref[...], b_ref[...])
                o32_ref[...] = res
                o16_ref[...] = res.astype(BF16)

    def part_spec(lo_p, hi_p):
        return pl.BlockSpec((t, tm), lambda i: (0, jnp.clip(i - lo_p, 0, hi_p - lo_p - 1)))

    out = pl.BlockSpec((tm, n), lambda i: (i, 0))
    return _call(
        body, name=name, grid=(nm,),
        in_specs=[part_spec(lo_p, hi_p) for lo_p, hi_p in bounds] + [pl.BlockSpec((t, n), lambda i: (0, 0))],
        out_specs=[out, out],
        out_shape=[jax.ShapeDtypeStruct((nm * tm, n), F32), jax.ShapeDtypeStruct((nm * tm, n), BF16)],
        args=(*parts, b), carry=carry)


def _wgrad_cols(a, b, bw, tn, name, carry=None):
    t, m = a.shape
    n = b.shape[1]
    per_step = tn // bw

    def body(a_ref, b_ref, o32_ref, o16_ref):
        res = _dot_tn(a_ref[...], b_ref[...])
        for blk in range(per_step):
            part = res[:, blk * bw:(blk + 1) * bw]
            o32_ref[blk] = part
            o16_ref[blk] = part.astype(BF16)

    out = pl.BlockSpec((per_step, m, bw), lambda j: (j, 0, 0))
    return _call(
        body, name=name, grid=(n // tn,),
        in_specs=[pl.BlockSpec((t, m), lambda j: (0, 0)), pl.BlockSpec((t, tn), lambda j: (0, j))],
        out_specs=[out, out],
        out_shape=[jax.ShapeDtypeStruct((n // bw, m, bw), F32), jax.ShapeDtypeStruct((n // bw, m, bw), BF16)],
        args=(a, b), carry=carry)


def _ff1_bwd_norms(d_f1, w_ff1b, dy, x2, g3, m, g2, carry=None):
    t, k = d_f1.shape
    d = x2.shape[1]
    bw = w_ff1b.shape[2]
    per_step = 2
    tt, tk = _tile(t, 1024), per_step * bw
    nk = k // tk

    def body(a_ref, w_ref, dy_ref, x2_ref, g3_ref, m_ref, g2_ref, dx2_ref, dm_ref, dg3_ref, dg2_ref, acc):
        i, kk = pl.program_id(0), pl.program_id(1)

        @pl.when(kk == 0)
        def _():
            acc[...] = jnp.zeros_like(acc)

        @pl.when((i == 0) & (kk == 0))
        def _():
            dg3_ref[...] = jnp.zeros_like(dg3_ref)
            dg2_ref[...] = jnp.zeros_like(dg2_ref)

        a_tile = a_ref[...]
        for b in range(per_step):
            acc[...] += _dot_nt(a_tile[:, b * bw:(b + 1) * bw], w_ref[b])

        @pl.when(kk == nk - 1)
        def _():
            def tail(rows):
                xhat, r3 = _rms_hat(x2_ref[rows, :])
                dx, dg3 = _rms_bwd(acc[rows, :], xhat, r3, g3_ref[...])
                dx2 = dy_ref[rows, :].astype(F32) + dx
                dx2_ref[rows, :] = dx2
                dg3_ref[...] += dg3
                mhat, r2 = _rms_hat(m_ref[rows, :])
                dm, dg2 = _rms_bwd(dx2, mhat, r2, g2_ref[...])
                dm_ref[rows, :] = dm.astype(BF16)
                dg2_ref[...] += dg2

            _row_chunks(tt, tail)

    row = pl.BlockSpec((tt, d), lambda i, kk: (i, 0))
    vec = pl.BlockSpec((1, d), lambda i, kk: (0, 0))
    return _call(
        body, name="ff1_bwd_norms", grid=(t // tt, nk),
        in_specs=[
            pl.BlockSpec((tt, tk), lambda i, kk: (i, kk)),
            pl.BlockSpec((per_step, d, bw), lambda i, kk: (kk, 0, 0)),
            row, row, vec, row, vec,
        ],
        out_specs=[row, row, vec, vec],
        out_shape=[
            jax.ShapeDtypeStruct((t, d), F32),
            jax.ShapeDtypeStruct((t, d), BF16),
            jax.ShapeDtypeStruct((1, d), F32),
            jax.ShapeDtypeStruct((1, d), F32),
        ],
        scratch_shapes=[pltpu.VMEM((tt, d), F32)],
        args=(d_f1, w_ff1b, dy, x2, g3, m, g2), carry=carry)


def _wo_bwd_mix(dm, w_o, br_a, br_b, proj, b_gate, ga_block, gb_block, carry=None):
    t, d = dm.shape
    tt, tn = _tile(t, 1024), 512
    nj = d // tn

    def body(dm_ref, w_ref, bra_ref, brb_ref, ga_ref, gb_ref, ba_ref, bb_ref,
             dbra_ref, dbrb_ref, dga_ref, dgb_ref, dba_ref, dbb_ref):
        i = pl.program_id(1)

        @pl.when(i == 0)
        def _():
            dba_ref[...] = jnp.zeros_like(dba_ref)
            dbb_ref[...] = jnp.zeros_like(dbb_ref)

        d_mix = _dot_nt(dm_ref[...], w_ref[...])
        ga = _sig(ga_ref[...].astype(F32) + ba_ref[...])
        gb = _sig(gb_ref[...].astype(F32) + bb_ref[...])
        dbra_ref[...] = (d_mix * ga).astype(BF16)
        dbrb_ref[...] = (d_mix * gb).astype(BF16)
        dga = d_mix * bra_ref[...].astype(F32) * (ga * (1.0 - ga))
        dgb = d_mix * brb_ref[...].astype(F32) * (gb * (1.0 - gb))
        dga_ref[...] = dga.astype(BF16)
        dgb_ref[...] = dgb.astype(BF16)
        dba_ref[...] += jnp.sum(dga, axis=0, keepdims=True)
        dbb_ref[...] += jnp.sum(dgb, axis=0, keepdims=True)

    blk = pl.BlockSpec((tt, tn), lambda j, i: (i, j))
    vec = pl.BlockSpec((1, tn), lambda j, i: (0, j))
    return _call(
        body, name="wo_bwd_mix", grid=(nj, t // tt),
        in_specs=[
            pl.BlockSpec((tt, d), lambda j, i: (i, 0)),
            pl.BlockSpec((tn, d), lambda j, i: (j, 0)),
            blk, blk,
            pl.BlockSpec((tt, tn), lambda j, i: (i, ga_block + j)),
            pl.BlockSpec((tt, tn), lambda j, i: (i, gb_block + j)),
            vec,
            pl.BlockSpec((1, tn), lambda j, i: (0, nj + j)),
        ],
        out_specs=[blk, blk, blk, blk, vec, vec],
        out_shape=[jax.ShapeDtypeStruct((t, d), BF16)] * 4 + [jax.ShapeDtypeStruct((1, d), F32)] * 2,
        args=(dm, w_o, br_a, br_b, proj, proj, b_gate, b_gate), carry=carry)


def _lru_up_bwd(d_br_a, w_lru_up, proj, h, g_block, carry=None):
    t, d = d_br_a.shape
    tt, tn = _tile(t, 1024), 512

    def body(a_ref, w_ref, g_ref, h_ref, dh_ref, dg_ref):
        d_y = _dot_nt(a_ref[...], w_ref[...])
        gel, gel_grad = _gelu_and_grad(g_ref[...].astype(F32))
        dh_ref[...] = d_y * gel
        dg_ref[...] = (d_y * h_ref[...] * gel_grad).astype(BF16)

    blk = pl.BlockSpec((tt, tn), lambda i, j: (i, j))
    return _call(
        body, name="lru_up_bwd", grid=(t // tt, d // tn),
        in_specs=[
            pl.BlockSpec((tt, d), lambda i, j: (i, 0)),
            pl.BlockSpec((tn, d), lambda i, j: (j, 0)),
            pl.BlockSpec((tt, tn), lambda i, j: (i, g_block + j)),
            blk,
        ],
        out_specs=[blk, blk],
        out_shape=[jax.ShapeDtypeStruct((t, d), F32), jax.ShapeDtypeStruct((t, d), BF16)],
        args=(d_br_a, w_lru_up, proj, h), carry=carry)


def _lru_bwd(dh, h, saved, proj, conv_w, w_a, w_x, lam, carry=None):
    t, dr = dh.shape
    cb = LRU_CB
    hd = LRU_HEAD_DIM
    per = cb // hd
    tc = _tile(t, 256)
    ncb, ntc = dr // cb, t // tc

    def body(dh_ref, h_ref, hp_ref, saved_ref, xp_ref, cw_ref, wa_ref, wx_ref,
             lam_ref, dxp_ref, dwa_ref, dba_ref, dwx_ref, dbx_ref, dlam_ref, dcw_ref, dcb_ref,
             nextd_s, anext_s, gnext_s, tmp_s, wa_s, wx_s):
        c = pl.program_id(1)
        rc = ntc - 1 - c

        @pl.when(c == 0)
        def _():
            nextd_s[...] = jnp.zeros_like(nextd_s)
            anext_s[...] = jnp.zeros_like(anext_s)
            gnext_s[...] = jnp.zeros_like(gnext_s)
            for ref in (dwa_ref, dba_ref, dwx_ref, dbx_ref, dlam_ref, dcw_ref, dcb_ref):
                ref[...] = jnp.zeros_like(ref)
            _fill_block_diag(wa_ref, wa_s)
            _fill_block_diag(wx_ref, wx_s)

        xc, r, i, a, mult = [saved_ref[:, k * cb:(k + 1) * cb] for k in range(N_LRU_SAVED)]
        wa, wx, lam = wa_s[...], wx_s[...], lam_ref[...]
        xcb = xc.astype(BF16)
        sp = _softplus_neg(lam)
        row = lax.broadcasted_iota(jnp.int32, xc.shape, 0)
        h = h_ref[...]
        hp = jnp.where(rc == 0, 0.0, hp_ref[...])
        hprev = jnp.where(row >= 1, pltpu.roll(h, 1, 0), pltpu.roll(hp, 1, 0))

        def up(v, nv, j):
            return jnp.where(row < tc - j, pltpu.roll(v, tc - j, 0), nv)

        av, bv = _scan_rows(up(a, anext_s[...], 1), dh_ref[...], reverse=True)
        gt = av * gnext_s[...] + bv
        tmp_s[...] = gt
        gnext_s[...] = tmp_s[0:1, :]
        tmp_s[...] = a
        anext_s[...] = tmp_s[0:1, :]

        da = gt * hprev
        ixc = i * xc
        d_mult = gt * ixc
        d_i = gt * mult * xc
        d_xc = gt * mult * i
        d_log_a = da * a - d_mult * (a * a) / mult
        d_pre_r = (d_log_a * ((-LRU_C) * sp)) * (r * (1.0 - r))
        d_pre_i = d_i * (i * (1.0 - i))
        d_sp = jnp.sum(d_log_a * ((-LRU_C) * r), axis=0, keepdims=True)
        dlam_ref[...] += d_sp * (-1.0 / (1.0 + jnp.exp(lam)))
        dpr = d_pre_r.astype(BF16)
        dpi = d_pre_i.astype(BF16)
        dba_ref[...] += jnp.sum(d_pre_r, axis=0, keepdims=True)
        dbx_ref[...] += jnp.sum(d_pre_i, axis=0, keepdims=True)
        pa = _dot_tn(xcb, dpr)
        px = _dot_tn(xcb, dpi)
        for k in range(per):
            dwa_ref[k] += pa[k * hd:(k + 1) * hd, k * hd:(k + 1) * hd]
            dwx_ref[k] += px[k * hd:(k + 1) * hd, k * hd:(k + 1) * hd]
        d_xc = d_xc + _dot_nt(dpr, wa) + _dot_nt(dpi, wx)

        nxt = nextd_s[...]
        xp = xp_ref[...].astype(F32)
        dxp = cw_ref[3:4, :] * d_xc
        dcw_ref[3:4, :] += jnp.sum(xp * d_xc, axis=0, keepdims=True)
        for j in (1, 2, 3):
            uj = up(d_xc, pltpu.roll(nxt, tc - j, 0), j)
            dxp = dxp + cw_ref[3 - j:4 - j, :] * uj
            dcw_ref[3 - j:4 - j, :] += jnp.sum(xp * uj, axis=0, keepdims=True)
        dcb_ref[...] += jnp.sum(d_xc, axis=0, keepdims=True)
        nextd_s[...] = d_xc
        dxp_ref[...] = dxp.astype(BF16)

    vec = pl.BlockSpec((1, cb), lambda j, c: (0, j))
    blk = pl.BlockSpec((tc, cb), lambda j, c: (ntc - 1 - c, j))
    mat = pl.BlockSpec((per, hd, hd), lambda j, c: (j, 0, 0))
    cwb = pl.BlockSpec((4, cb), lambda j, c: (0, j))
    return _call(
        body, name="lru_bwd", grid=(ncb, ntc),
        in_specs=[
            blk, blk,
            pl.BlockSpec((tc, cb), lambda j, c: (jnp.maximum(ntc - 2 - c, 0), j)),
            pl.BlockSpec((tc, N_LRU_SAVED * cb), lambda j, c: (ntc - 1 - c, j)),
            blk, cwb, mat, mat, vec,
        ],
        out_specs=[blk, mat, vec, mat, vec, vec, cwb, vec],
        out_shape=[
            jax.ShapeDtypeStruct((t, dr), BF16),
            jax.ShapeDtypeStruct(w_a.shape, F32),
            jax.ShapeDtypeStruct((1, dr), F32),
            jax.ShapeDtypeStruct(w_x.shape, F32),
            jax.ShapeDtypeStruct((1, dr), F32),
            jax.ShapeDtypeStruct((1, dr), F32),
            jax.ShapeDtypeStruct((4, dr), F32),
            jax.ShapeDtypeStruct((1, dr), F32),
        ],
        scratch_shapes=[
            pltpu.VMEM((tc, cb), F32),
            pltpu.VMEM((1, cb), F32),
            pltpu.VMEM((1, cb), F32),
            pltpu.VMEM((tc, cb), F32),
            pltpu.VMEM((cb, cb), BF16),
            pltpu.VMEM((cb, cb), BF16),
        ],
        args=(dh, h, h, saved, proj, conv_w, w_a, w_x, lam), carry=carry)


def _pool_bwd(d_br_b, w_pool_upb, p, pool_w, pool_scale):
    t, d = d_br_b.shape
    nwb, dp, _ = w_pool_upb.shape
    tc = _tile(t, 256)
    ntc = t // tc
    ng = len(POOL_WINDOWS)

    def body(db_ref, wu_ref, p_ref, w_ref, sc_ref, dx_ref, dw_ref, dsc_ref, nz, n2, n4, n8, dp_s, dy_s):
        c = pl.program_id(0)
        rc = ntc - 1 - c

        @pl.when(c == 0)
        def _():
            for s in (nz, n2, n4, n8):
                s[...] = jnp.zeros_like(s)
            dw_ref[...] = jnp.zeros_like(dw_ref)
            dsc_ref[...] = jnp.zeros_like(dsc_ref)

        wu = jnp.concatenate([wu_ref[b] for b in range(nwb)], axis=1)
        dy_s[...] = _dot_nt(db_ref[...], wu)
        for g in range(ng):
            sl = slice(g * POOL_GROUP_DIM, (g + 1) * POOL_GROUP_DIM)
            pg = p_ref[:, sl]
            dyg = dy_s[:, sl]
            wg = w_ref[g].astype(BF16)
            q = _dot_nn(pg, wg)
            dsc_ref[:, sl] += jnp.sum(dyg * q, axis=0, keepdims=True)
            dpw = (dyg * sc_ref[:, sl]).astype(BF16)
            dw_ref[g] += _dot_tn(pg, dpw)
            dp_s[:, sl] = _dot_nt(dpw, wg)

        dpv = dp_s[...]
        row = lax.broadcasted_iota(jnp.int32, dpv.shape, 0)
        col = lax.broadcasted_iota(jnp.int32, dpv.shape, 1)
        win = _pool_select(col, POOL_WINDOWS)
        cnt = jnp.minimum(rc * tc + row + 1, win).astype(F32)
        z = dpv / cnt

        def up(v, nv, j):
            return jnp.where(row < tc - j, pltpu.roll(v, tc - j, 0), pltpu.roll(nv[...], tc - j, 0))

        u2 = z + up(z, nz, 1)
        u4 = u2 + up(u2, n2, 2)
        u8 = u4 + up(u4, n4, 4)
        u16 = u8 + up(u8, n8, 8)
        nz[...] = z
        n2[...] = u2
        n4[...] = u4
        n8[...] = u8
        dx_ref[...] = (_pool_select(col, (u2, u4, u8, u16)) - dpv).astype(BF16)

    blk = pl.BlockSpec((tc, dp), lambda c: (ntc - 1 - c, 0))
    full_w = pl.BlockSpec(pool_w.shape, lambda c: (0, 0, 0))
    vec = pl.BlockSpec((1, dp), lambda c: (0, 0))
    return _call(
        body, name="pool_bwd", grid=(ntc,),
        in_specs=[pl.BlockSpec((tc, d), lambda c: (ntc - 1 - c, 0)),
                  pl.BlockSpec(w_pool_upb.shape, lambda c: (0, 0, 0)), blk, full_w, vec],
        out_specs=[blk, full_w, vec],
        out_shape=[
            jax.ShapeDtypeStruct((t, dp), BF16),
            jax.ShapeDtypeStruct(pool_w.shape, F32),
            jax.ShapeDtypeStruct((1, dp), F32),
        ],
        scratch_shapes=[pltpu.VMEM((tc, dp), F32)] * 6,
        args=(d_br_b, w_pool_upb, p, pool_w, pool_scale))[0]


def _win_bwd_norm(parts, w_int, dx2, x, g1, carry=None):
    t, d = x.shape
    tk = 512
    tt = _tile(t, 1024)
    bounds = []
    k0 = 0
    for part in parts:
        assert part.shape[1] % tk == 0
        bounds.append((k0, k0 + part.shape[1] // tk))
        k0 += part.shape[1] // tk
    nk = k0
    assert nk * tk == w_int.shape[0]
    np_ = len(parts)

    def body(*refs):
        p_refs = refs[:np_]
        w_ref, dx2_ref, x_ref, g_ref, gx_ref, dg_ref, acc = refs[np_:]
        i, kk = pl.program_id(0), pl.program_id(1)

        @pl.when(kk == 0)
        def _():
            acc[...] = jnp.zeros_like(acc)

        @pl.when((i == 0) & (kk == 0))
        def _():
            dg_ref[...] = jnp.zeros_like(dg_ref)

        for (lo, hi), p_ref in zip(bounds, p_refs):
            @pl.when((kk >= lo) & (kk < hi))
            def _(p_ref=p_ref):
                acc[...] += _dot_nn(p_ref[...], w_ref[...])

        @pl.when(kk == nk - 1)
        def _():
            def tail(rows):
                xhat, r = _rms_hat(x_ref[rows, :])
                dx, dg = _rms_bwd(acc[rows, :], xhat, r, g_ref[...])
                gx_ref[rows, :] = dx2_ref[rows, :] + dx
                dg_ref[...] += dg

            _row_chunks(tt, tail)

    def part_spec(lo, hi):
        return pl.BlockSpec((tt, tk), lambda i, kk: (i, jnp.clip(kk - lo, 0, hi - lo - 1)))

    row = pl.BlockSpec((tt, d), lambda i, kk: (i, 0))
    vec = pl.BlockSpec((1, d), lambda i, kk: (0, 0))
    return _call(
        body, name="win_bwd_norm", grid=(t // tt, nk),
        in_specs=[part_spec(lo, hi) for lo, hi in bounds]
        + [pl.BlockSpec((tk, d), lambda i, kk: (kk, 0)), row, row, vec],
        out_specs=[row, vec],
        out_shape=[jax.ShapeDtypeStruct((t, d), F32), jax.ShapeDtypeStruct((1, d), F32)],
        scratch_shapes=[pltpu.VMEM((tt, d), F32)],
        args=(*parts, w_int, dx2, x, g1), carry=carry)


def _adam_math(w, g, m, v):
    m = ADAM_B1 * m + (1.0 - ADAM_B1) * g
    v = ADAM_B2 * v + (1.0 - ADAM_B2) * (g * g)
    m_hat = m / (1.0 - ADAM_B1 ** ADAM_STEP)
    v_hat = v / (1.0 - ADAM_B2 ** ADAM_STEP)
    delta = -ADAM_LR * (m_hat / (jnp.sqrt(v_hat) + ADAM_EPS) + ADAM_WD * w)
    return delta, m, v


def _adamw_big(ws, gs, ms, vs):
    n = len(ws)
    nb = 4
    pair = [isinstance(g, tuple) for g in gs]

    def body(*refs):
        p = 0
        ins = []
        for a in range(n):
            k = 5 if pair[a] else 4
            ins.append(refs[p:p + k])
            p += k
        for a in range(n):
            g_out, d_ref, nm_ref, nv_ref = refs[p + 4 * a:p + 4 * a + 4]
            if pair[a]:
                w_ref, own_ref, recv_ref, m_ref, v_ref = ins[a]
                g = own_ref[...]
                for k in range(3):
                    g = g + recv_ref[k].astype(F32)
            else:
                w_ref, g_ref, m_ref, v_ref = ins[a]
                g = g_ref[...]
            dl, m, v = _adam_math(w_ref[...], g, m_ref[...], v_ref[...])
            g_out[...] = g
            d_ref[...] = dl
            nm_ref[...] = m
            nv_ref[...] = v

    in_specs, out_specs, out_shape, args = [], [], [], []
    for a, (w, g, m, v) in enumerate(zip(ws, gs, ms, vs)):
        rows, cols = w.shape
        blk = pl.BlockSpec((rows // nb, cols), lambda i: (i, 0))
        if pair[a]:
            in_specs += [blk, pl.BlockSpec((None, rows // nb, cols), lambda i: (0, i, 0)),
                         pl.BlockSpec((3, rows // nb, cols), lambda i: (0, i, 0)), blk, blk]
            args += [w, g[0], g[1], m, v]
        else:
            in_specs += [blk] * 4
            args += [w, g, m, v]
        out_specs += [blk] * 4
        out_shape += [jax.ShapeDtypeStruct(w.shape, F32)] * 4
    outs = _call(body, name="adamw_big", grid=(nb,), in_specs=in_specs, out_specs=out_specs,
                 out_shape=out_shape, args=args)[0]
    return [tuple(outs[4 * a:4 * a + 4]) for a in range(n)]


SMALL_ORDER = ("norm_mix_pre", "norm_mix_post", "norm_mlp_pre", "norm_mlp_post", "b_gate", "conv_w", "conv_b",
               "lru_w_a", "lru_b_a", "lru_w_x", "lru_b_x", "lru_lambda", "pool_w", "pool_scale")
VEC_ROW = dict(norm_mix_pre=0, norm_mix_post=1, norm_mlp_pre=2, norm_mlp_post=3, conv_b=6, lru_b_a=7,
               lru_b_x=8, lru_lambda=9)
ROW_B_GATE, ROW_POOL_SCALE, ROW_CONV_W, ROW_LOSS, N_VEC_ROWS = 4, 10, 11, 15, 16


def _adamw_small(vec_parts, g_pool, g_wa, g_wx, me, params):
    d = vec_parts.shape[2]
    names = SMALL_ORDER
    n = len(names)
    cw_cols = params["conv_w"][0].shape[2]

    def body(me_ref, vec_ref, vecc_ref, gp_ref, gwa_ref, gwx_ref, *refs):
        wmv = refs[:3 * n]
        loss_ref = refs[3 * n]
        outs = refs[3 * n + 1:3 * n + 1 + 4 * n]
        vs, vsc = refs[3 * n + 1 + 4 * n:]
        acc, accc = vec_ref[0], vecc_ref[0]
        for k in range(1, N_DEV):
            acc = acc + vec_ref[k]
            accc = accc + vecc_ref[k]
        vs[...] = acc
        vsc[...] = accc
        loss_ref[...] = vs[ROW_LOSS:ROW_LOSS + 1, 0:128]

        def upd(a, g, idx):
            w_ref, m_ref, v_ref = wmv[3 * a:3 * a + 3]
            g_ref, d_ref, nm_ref, nv_ref = outs[4 * a:4 * a + 4]
            dl, m, v = _adam_math(w_ref[idx], g, m_ref[idx], v_ref[idx])
            g_ref[idx] = g
            d_ref[idx] = dl
            nm_ref[idx] = m
            nv_ref[idx] = v

        for a, name in enumerate(names):
            if name in VEC_ROW:
                r = VEC_ROW[name]
                upd(a, vs[r:r + 1, :], (slice(None), slice(None)))
            elif name == "b_gate":
                for half in range(2):
                    r = ROW_B_GATE + half
                    upd(a, vs[r:r + 1, :], (slice(None), slice(half * d, (half + 1) * d)))
            elif name == "pool_scale":
                width = params[name][0].shape[1]
                upd(a, vs[ROW_POOL_SCALE:ROW_POOL_SCALE + 1, 0:width], (slice(None), slice(None)))
            elif name == "conv_w":
                upd(a, vsc[ROW_CONV_W:ROW_CONV_W + 4, :], (0,))
            elif name == "pool_w":
                upd(a, gp_ref[...], (Ellipsis,))
            elif name == "lru_w_a":
                upd(a, gwa_ref[...], (Ellipsis,))
            elif name == "lru_w_x":
                upd(a, gwx_ref[...], (Ellipsis,))
            else:
                raise ValueError(name)

    def whole(shape):
        nd = len(shape)
        return pl.BlockSpec(tuple(shape), lambda i, me_ref: (0,) * nd)

    in_specs = [
        whole(vec_parts.shape),
        pl.BlockSpec((N_DEV, N_VEC_ROWS, cw_cols), lambda i, me_ref: (0, 0, me_ref[0])),
        whole(g_pool.shape), whole(g_wa.shape), whole(g_wx.shape),
    ]
    args = [vec_parts, vec_parts, g_pool, g_wa, g_wx]
    out_specs = [whole((1, 128))]
    out_shape = [jax.ShapeDtypeStruct((1, 128), F32)]
    for name in names:
        for arr in params[name]:
            in_specs.append(whole(arr.shape))
            args.append(arr)
        shp = params[name][0].shape
        out_specs += [whole(shp)] * 4
        out_shape += [jax.ShapeDtypeStruct(shp, F32)] * 4
    grid_spec = pltpu.PrefetchScalarGridSpec(
        num_scalar_prefetch=1, grid=(1,), in_specs=in_specs, out_specs=out_specs,
        scratch_shapes=[pltpu.VMEM((N_VEC_ROWS, d), F32), pltpu.VMEM((N_VEC_ROWS, cw_cols), F32)])
    outs = pl.pallas_call(
        body, name="adamw_small", grid_spec=grid_spec, out_shape=out_shape,
        compiler_params=pltpu.CompilerParams(
            dimension_semantics=("arbitrary",), vmem_limit_bytes=V7X_VMEM_LIMIT_BYTES),
    )(me, *_in_hbm(args))
    return outs[0], {name: tuple(outs[1 + 4 * a:5 + 4 * a]) for a, name in enumerate(names)}


def _rs_sum(fulls, recvs, shard_ids, slot_ids, name):
    n = len(fulls)

    def body(sh_ref, sl_ref, *refs):
        s = pl.program_id(0)
        for a in range(n):
            full_ref, recv_ref = refs[2 * a], refs[2 * a + 1]
            own_ref, send_ref = refs[2 * n + 2 * a], refs[2 * n + 2 * a + 1]
            v = full_ref[...] + recv_ref[...].astype(F32)

            @pl.when(s == 0)
            def _(own_ref=own_ref, v=v):
                own_ref[...] = v

            @pl.when(s > 0)
            def _(send_ref=send_ref, v=v):
                send_ref[...] = v.astype(send_ref.dtype)

    in_specs, out_specs, out_shape, args = [], [], [], []
    for full, recv in zip(fulls, recvs):
        r, rest = recv.shape[1], tuple(recv.shape[2:])
        zeros = (0,) * len(rest)
        in_specs += [
            pl.BlockSpec((r,) + rest, lambda s, sh, sl, zeros=zeros: (sh[s],) + zeros),
            pl.BlockSpec((None, r) + rest, lambda s, sh, sl, zeros=zeros: (sl[s], 0) + zeros),
        ]
        out_specs += [
            pl.BlockSpec((None, r) + rest, lambda s, sh, sl, zeros=zeros: (0, 0) + zeros),
            pl.BlockSpec((None, r) + rest, lambda s, sh, sl, zeros=zeros: (jnp.maximum(s - 1, 0), 0) + zeros),
        ]
        out_shape += [jax.ShapeDtypeStruct((1, r) + rest, F32), jax.ShapeDtypeStruct((3, r) + rest, recv.dtype)]
        args += [full, recv]
    grid_spec = pltpu.PrefetchScalarGridSpec(
        num_scalar_prefetch=2, grid=(4,), in_specs=in_specs, out_specs=out_specs)
    outs = pl.pallas_call(
        body,
        name=name,
        grid_spec=grid_spec,
        out_shape=out_shape,
        compiler_params=pltpu.CompilerParams(
            dimension_semantics=("arbitrary",), vmem_limit_bytes=V7X_VMEM_LIMIT_BYTES),
    )(shard_ids, slot_ids, *_in_hbm(args))
    return [(outs[2 * a], outs[2 * a + 1]) for a in range(n)]


def _finals(pairs, name, carry=None):
    nb = 4
    n = len(pairs)

    def body(*refs):
        for a in range(n):
            own_ref, recv_ref = refs[2 * a], refs[2 * a + 1]
            acc = own_ref[...]
            for k in range(3):
                acc = acc + recv_ref[k].astype(F32)
            refs[2 * n + a][...] = acc

    in_specs, out_specs, out_shape, args = [], [], [], []
    for own, recv in pairs:
        _, rows, cols = own.shape
        in_specs += [pl.BlockSpec((None, rows // nb, cols), lambda i: (0, i, 0)),
                     pl.BlockSpec((3, rows // nb, cols), lambda i: (0, i, 0))]
        args += [own, recv]
        out_specs.append(pl.BlockSpec((rows // nb, cols), lambda i: (i, 0)))
        out_shape.append(jax.ShapeDtypeStruct((rows, cols), F32))
    return _call(body, name=name, grid=(nb,), in_specs=in_specs, out_specs=out_specs,
                 out_shape=out_shape, args=args, carry=carry)


def _rs_sums(fulls_f32, recv1, tag):
    x, y, c = _place()
    qs = jnp.stack([2 * x + y, 2 * (1 - x) + y, 2 * x + (1 - y), 2 * (1 - x) + (1 - y)]).astype(jnp.int32)
    shard_ids = 2 * qs + c
    return _rs_sum(fulls_f32, recv1, shard_ids, qs, "rs_sum_" + tag)


def _rs_level1(fulls_f32, fulls_send, tag):
    recv1 = _run_plan(_rs_sibling_plan(fulls_send), "rs_sibling_" + tag)
    return _rs_sums(fulls_f32, recv1, tag)


def _rows(g):
    return g.reshape(g.shape[0] * g.shape[1], g.shape[2])


def kernel(x, norm_mix_pre, norm_mix_post, norm_mlp_pre, norm_mlp_post, w_in, b_gate, conv_w, conv_b, lru_w_a, lru_b_a, lru_w_x, lru_b_x, lru_lambda, pool_w, pool_scale, w_lru_up, w_pool_up, w_o, w_ff1, w_ff2, loss_target, m_norm_mix_pre, m_norm_mix_post, m_norm_mlp_pre, m_norm_mlp_post, m_w_in, m_b_gate, m_conv_w, m_conv_b, m_lru_w_a, m_lru_b_a, m_lru_w_x, m_lru_b_x, m_lru_lambda, m_pool_w, m_pool_scale, m_w_lru_up, m_w_pool_up, m_w_o, m_w_ff1, m_w_ff2, v_norm_mix_pre, v_norm_mix_post, v_norm_mlp_pre, v_norm_mlp_post, v_w_in, v_b_gate, v_conv_w, v_conv_b, v_lru_w_a, v_lru_b_a, v_lru_w_x, v_lru_b_x, v_lru_lambda, v_pool_w, v_pool_scale, v_w_lru_up, v_w_pool_up, v_w_o, v_w_ff1, v_w_ff2):
    t, d = x.shape[1], x.shape[2]
    d_rnn = conv_b.shape[1]
    d_pool = pool_scale.shape[1]
    per = LRU_CB // LRU_HEAD_DIM
    xi, yi, ci = _place()
    me = 4 * xi + 2 * yi + ci

    x2d = x[0]
    tgt = loss_target[0]

    s_in = w_in[0].T.astype(BF16)
    s_cw = jnp.pad(conv_w[0], ((0, 4), (0, 0)))
    (s_lu, s_pu, s_o, s_f1, s_f2), (g_in, g_cw) = _to_bf16(
        [w_lru_up[0], w_pool_up[0], w_o[0], w_ff1[0], w_ff2[0]], carry=_ag_plan([s_in, s_cw]))
    w_int = _rows(g_in)
    conv_w_full = jnp.transpose(g_cw[:, :4, :], (1, 0, 2)).reshape(4, d_rnn)

    wa_bd, wx_bd = lru_w_a[0], lru_w_x[0]
    pw = pool_w[0]
    pw_bf = pw.astype(BF16)

    pool_block = (2 * d_rnn) // d_pool
    ga_block = (2 * d_rnn + d_pool) // 512
    gb_block = ga_block + d // 512
    g_block = d_rnn // 512

    r_f1, r_f2 = s_f1.shape[0], s_f2.shape[0]
    f1_cut = r_f1 // 4
    f2_cut = (3 * r_f2) // 8
    plan = _join([_ag_plan([s_lu, s_pu]), _ag_plan([s_f1], pieces=[(0, f1_cut)])])
    (proj, h1), got = _norm_proj(x2d, norm_mix_pre, w_int, carry=plan)
    (g_lu, g_pu), (g_f1,) = plan.split(got)
    plan = _join([_ag_plan([s_f1], pieces=[(f1_cut, r_f1 - f1_cut)], bufs=[g_f1]), _ag_plan([s_o])])
    (y_lru, h, lru_saved), got = _lru_fwd(
        proj, conv_w_full, conv_b, wa_bd, lru_b_a, wx_bd, lru_b_x, lru_lambda, carry=plan)
    (g_f1,), (g_o,) = plan.split(got)
    w_lu, w_og = _rows(g_lu), _rows(g_o)
    y_pool, p = _pool_fwd(proj, pw_bf, pool_scale, pool_block)
    (br_a, br_b, mix), (g_f2,) = _branch_mix(
        y_lru, y_pool, w_lu, g_pu, proj, b_gate, ga_block, gb_block,
        carry=_ag_plan([s_f2], pieces=[(0, f2_cut)]))
    (m, x2, h3), _ = _wo_norm(mix, w_og, x2d, norm_mix_post, norm_mlp_pre)
    (rf,), (g_f2,) = _ff1(
        h3, g_f1, carry=_ag_plan([s_f2], pieces=[(f2_cut, r_f2 - f2_cut)], bufs=[g_f2]))
    w_f2 = _rows(g_f2)
    dy, df, dg4, loss_part = _ff2_loss(rf, w_f2, x2, norm_mlp_post, tgt)

    (gw_ff2_32, gw_ff2_16), _ = _wgrad(rf, df, "wgrad_ff2", square_a=True)
    (d_f1,), r1_ff2 = _ff2_bwd(df, w_f2, rf, carry=_rs_sibling_plan([gw_ff2_16]))
    ((own_ff2, send_ff2),) = _rs_sums([gw_ff2_32], r1_ff2, "ff2")
    cut2 = (5 * send_ff2.shape[1]) // 16
    (gw_ff1_32, gw_ff1_16), (r2_ff2,) = _wgrad_cols(
        h3, d_f1, s_f1.shape[1], s_f1.shape[1], "wgrad_ff1",
        carry=_rs_chips_plan([send_ff2], pieces=[(0, cut2)]))
    plan = _join([_rs_chips_plan([send_ff2], pieces=[(cut2, send_ff2.shape[1] - cut2)], bufs=[r2_ff2]),
                  _rs_sibling_plan([gw_ff1_16])])
    (dx2, dm, dg3, dg2), got = _ff1_bwd_norms(d_f1, g_f1, dy, x2, norm_mlp_pre, m, norm_mix_post, carry=plan)
    (r2_ff2,), r1_ff1 = plan.split(got)
    ((own_ff1, send_ff1),) = _rs_sums([gw_ff1_32], r1_ff1, "ff1")
    own_ff1, send_ff1 = own_ff1.reshape((1,) + s_f1.shape), send_ff1.reshape((3,) + s_f1.shape)
    cut = send_ff1.shape[1] // 4
    (gw_o_32, gw_o_16), _ = _wgrad(mix, dm, "wgrad_o")
    (d_br_a, d_br_b, p_ga, p_gb, dbg_a, dbg_b), (r2_ff1,) = _wo_bwd_mix(
        dm, w_og, br_a, br_b, proj, b_gate, ga_block, gb_block,
        carry=_rs_chips_plan([send_ff1], pieces=[(0, cut)]))
    (gw_lu_32, gw_lu_16), _ = _wgrad(y_lru, d_br_a, "wgrad_lru_up")
    (gw_pu_32, gw_pu_16), _ = _wgrad_cols(y_pool, d_br_b, s_pu.shape[1], d, "wgrad_pool_up")
    (dh, p_g), r1_mid = _lru_up_bwd(
        d_br_a, w_lu, proj, h, g_block,
        carry=_rs_sibling_plan([gw_o_16, gw_lu_16, gw_pu_16]))
    mid = _rs_sums([gw_o_32, gw_lu_32, gw_pu_32], r1_mid, "mid")
    plan = _join([_rs_chips_plan([send_ff1], pieces=[(cut, send_ff1.shape[1] - cut)], bufs=[r2_ff1]),
                  _rs_chips_plan([mid[0][1]])])
    (p_x, dwa, db_a, dwx, db_x, dlam, dconv_w, dconv_b), got = _lru_bwd(
        dh, h, lru_saved, proj, conv_w_full, wa_bd, wx_bd, lru_lambda, carry=plan)
    (r2_ff1,), (r2_o,) = plan.split(got)
    p_p, dpool_w, dpool_scale = _pool_bwd(d_br_b, g_pu, p, pw, pool_scale)
    parts = [p_x, p_g, p_p, p_ga, p_gb]
    gw_in, (r2_lu, r2_pu) = _wgrad_parts(
        parts, h1, "wgrad_in", carry=_rs_chips_plan([mid[1][1], mid[2][1]]))
    r2_mid = [r2_o, r2_lu, r2_pu]
    tail = _rs_level1([gw_in[0], dpool_w.reshape(N_DEV, -1, POOL_GROUP_DIM), dwa, dwx],
                      [gw_in[1], dpool_w.reshape(N_DEV, -1, POOL_GROUP_DIM), dwa, dwx], "in")
    (grad_x, dg1), r2_tail = _win_bwd_norm(parts, w_int, dx2, x2d, norm_mix_pre,
                                           carry=_rs_chips_plan([s for _, s in tail]))

    def flat2(a):
        return a.reshape(a.shape[0], -1, a.shape[-1])

    fin_small, _ = _finals([
        (flat2(tail[1][0]), flat2(r2_tail[1])), (flat2(tail[2][0]), flat2(r2_tail[2])),
        (flat2(tail[3][0]), flat2(r2_tail[3])),
    ], "rs_finals_small")

    def pad_row(a):
        return jnp.pad(a, ((0, 0), (0, d - a.shape[1])))

    vecs = jnp.concatenate([dg1, dg2, dg3, dg4, dbg_a, dbg_b, dconv_b, db_a, db_x, dlam,
                            pad_row(dpool_scale), dconv_w, pad_row(loss_part)], axis=0)
    assert vecs.shape[0] == N_VEC_ROWS
    vec_parts, g_pool, g_wa, g_wx = _run_plan(_ag_plan([vecs] + fin_small), "ag_tail")

    big_names = ["w_in", "w_lru_up", "w_pool_up", "w_o", "w_ff1", "w_ff2"]
    big_w = [w_in[0].T, w_lru_up[0], w_pool_up[0], w_o[0], w_ff1[0], w_ff2[0]]
    big_g = [(tail[0][0], r2_tail[0]), (mid[1][0], r2_mid[1]),
             (mid[2][0].reshape((1,) + s_pu.shape), r2_mid[2].reshape((3,) + s_pu.shape)),
             (mid[0][0], r2_mid[0]), (own_ff1, r2_ff1), (own_ff2, r2_ff2)]
    big_m = [m_w_in[0].T, m_w_lru_up[0], m_w_pool_up[0], m_w_o[0], m_w_ff1[0], m_w_ff2[0]]
    big_v = [v_w_in[0].T, v_w_lru_up[0], v_w_pool_up[0], v_w_o[0], v_w_ff1[0], v_w_ff2[0]]
    big_out = _adamw_big(big_w, big_g, big_m, big_v)
    big_out[0] = tuple(o.T for o in big_out[0])

    small = dict(
        norm_mix_pre=(norm_mix_pre, m_norm_mix_pre, v_norm_mix_pre),
        norm_mix_post=(norm_mix_post, m_norm_mix_post, v_norm_mix_post),
        norm_mlp_pre=(norm_mlp_pre, m_norm_mlp_pre, v_norm_mlp_pre),
        norm_mlp_post=(norm_mlp_post, m_norm_mlp_post, v_norm_mlp_post),
        b_gate=(b_gate, m_b_gate, v_b_gate), conv_w=(conv_w, m_conv_w, v_conv_w),
        conv_b=(conv_b, m_conv_b, v_conv_b), lru_w_a=(lru_w_a, m_lru_w_a, v_lru_w_a),
        lru_b_a=(lru_b_a, m_lru_b_a, v_lru_b_a), lru_w_x=(lru_w_x, m_lru_w_x, v_lru_w_x),
        lru_b_x=(lru_b_x, m_lru_b_x, v_lru_b_x), lru_lambda=(lru_lambda, m_lru_lambda, v_lru_lambda),
        pool_w=(pool_w, m_pool_w, v_pool_w), pool_scale=(pool_scale, m_pool_scale, v_pool_scale))
    loss_row, small_out = _adamw_small(
        vec_parts, g_pool.reshape(pool_w.shape), g_wa.reshape(lru_w_a.shape), g_wx.reshape(lru_w_x.shape),
        jnp.reshape(me, (1,)).astype(jnp.int32), small)
    grads = {n: o[0] for n, o in small_out.items()}
    delta = {n: o[1] for n, o in small_out.items()}
    new_m = {n: o[2] for n, o in small_out.items()}
    new_v = {n: o[3] for n, o in small_out.items()}

    for name, (g, dl, nm, nv) in zip(big_names, big_out):
        grads[name], delta[name], new_m[name], new_v[name] = g[None], dl[None], nm[None], nv[None]

    loss = loss_row[0, 0]
    order = ["norm_mix_pre", "norm_mix_post", "norm_mlp_pre", "norm_mlp_post", "w_in", "b_gate", "conv_w",
             "conv_b", "lru_w_a", "lru_b_a", "lru_w_x", "lru_b_x", "lru_lambda", "pool_w", "pool_scale",
             "w_lru_up", "w_pool_up", "w_o", "w_ff1", "w_ff2"]
    return (loss, grad_x[None], *[grads[n] for n in order], *[delta[n] for n in order],
            *[new_m[n] for n in order], *[new_v[n] for n in order])
```

```python
import functools
import math
import operator
import types

import jax
import jax.numpy as jnp
from jax import lax
from jax.experimental import pallas as pl
from jax.experimental.pallas import tpu as pltpu

F32 = jnp.float32
BF16 = jnp.bfloat16
NORM_EPS = 1e-6
LRU_C = 8.0
N_LRU_HEADS = 16
LRU_HEAD_DIM = 64
POOL_WINDOWS = (2, 4, 8, 16)
POOL_GROUP_DIM = 128
ADAM_LR = 0.001
ADAM_B1 = 0.9
ADAM_B2 = 0.999
ADAM_EPS = 1e-08
ADAM_WD = 0.01
ADAM_STEP = 10
N_DEV = 8
V7X_VMEM_LIMIT_BYTES = 56 * 1024 * 1024
LRU_CB = 256
MESH = pl.DeviceIdType.MESH
ANY = pl.BlockSpec(memory_space=pl.ANY)


def _tile(n, pref):
    t = min(n, pref)
    assert n % t == 0, (n, pref)
    return t


def _dot_nn(a, b):
    return lax.dot_general(a, b, (((1,), (0,)), ((), ())), preferred_element_type=F32)


def _dot_nt(a, b):
    return lax.dot_general(a, b, (((1,), (1,)), ((), ())), preferred_element_type=F32)


def _dot_tn(a, b):
    return lax.dot_general(a, b, (((0,), (0,)), ((), ())), preferred_element_type=F32)


def _row_chunks(n_rows, fn, chunk=256):
    chunk = min(chunk, n_rows)
    assert n_rows % chunk == 0

    def step(r, carry):
        fn(pl.ds(pl.multiple_of(r * chunk, chunk), chunk))
        return carry

    lax.fori_loop(0, n_rows // chunk, step, 0)


def _sig(x):
    return 1.0 / (1.0 + jnp.exp(-x))


def _rms_hat(x):
    r = lax.rsqrt(jnp.mean(x * x, axis=-1, keepdims=True) + NORM_EPS)
    return x * r, r


def _rms_bwd(dn, xhat, r, g):
    q = dn * g
    dx = r * (q - xhat * jnp.mean(q * xhat, axis=-1, keepdims=True))
    dg = jnp.sum(dn * xhat, axis=0, keepdims=True)
    return dx, dg


_GELU_K = math.sqrt(2.0 / math.pi)
_GELU_C = 0.044715


def _gelu_and_grad(g):
    t = jnp.tanh(_GELU_K * (g + _GELU_C * g * g * g))
    val = 0.5 * g * (1.0 + t)
    grad = 0.5 * (1.0 + t) + 0.5 * g * (1.0 - t * t) * (_GELU_K * (1.0 + 3.0 * _GELU_C * g * g))
    return val, grad


def _softplus_neg(lam):
    z = -lam
    e = jnp.exp(-jnp.abs(z))
    u = 1.0 + e
    d = u - 1.0
    l1p = jnp.where(d == 0.0, e, jnp.log(u) * (e / jnp.where(d == 0.0, 1.0, d)))
    return jnp.maximum(z, 0.0) + l1p


def _lru_gates(xc, wa, ba, wx, bx, lam):
    xcb = xc.astype(BF16)
    r = _sig(_dot_nn(xcb, wa) + ba)
    i = _sig(_dot_nn(xcb, wx) + bx)
    sp = _softplus_neg(lam)
    log_a = (-LRU_C) * r * sp
    a = jnp.exp(log_a)
    mult = jnp.sqrt(-jnp.tanh(log_a) * (1.0 + a * a))
    return xcb, r, i, sp, log_a, a, mult


def _place():
    return lax.axis_index("x"), lax.axis_index("y"), lax.axis_index("c")


def _ag_plan(shards, pieces=None, bufs=None):
    na = len(shards)
    n_kinds = 7

    def parts(ins, outs, sems):
        send_sems, recv_sems, local_sems = sems
        x, y, c = _place()
        me, sibling = (x, y, c), (x, y, 1 - c)
        x_nb, y_nb, diag = (1 - x, y), (x, 1 - y), (1 - x, 1 - y)
        relay_src = (c * (1 - x) + (1 - c) * x, c * y + (1 - c) * (1 - y))
        relay_dst = (c * x + (1 - c) * (1 - x), c * (1 - y) + (1 - c) * y)

        def own(a):
            return ins[a] if pieces is None else ins[a].at[pl.ds(*pieces[a])]

        def slot(a, px, py, pc):
            idx = 4 * px + 2 * py + pc
            return outs[a].at[idx] if pieces is None else outs[a].at[idx, pl.ds(*pieces[a])]

        def copy(a, k, block, to, src=None):
            return pltpu.make_async_remote_copy(
                src_ref=slot(a, *block) if src is None else src,
                dst_ref=slot(a, *block),
                send_sem=send_sems.at[a * n_kinds + k],
                recv_sem=recv_sems.at[a * n_kinds + k],
                device_id=to,
                device_id_type=MESH,
            )

        mine = [pltpu.make_async_copy(own(a), slot(a, *me), local_sems.at[a]) for a in range(na)]
        first, second, third = [], [], []
        for a in range(na):
            first += [copy(a, 0, me, sibling, src=own(a)), copy(a, 1, me, (*x_nb, c), src=own(a)),
                      copy(a, 2, me, (*y_nb, c), src=own(a))]
            second += [copy(a, 3, (*relay_src, c), (*relay_dst, c)), copy(a, 4, (*x_nb, c), sibling),
                       copy(a, 5, (*y_nb, c), sibling)]
            third.append(copy(a, 6, (*diag, c), sibling))
        return sibling, c, x_nb, y_nb, diag, copy, mine, first, second, third

    def start(ins, outs, sems):
        _, _, _, _, _, _, mine, first, _, _ = parts(ins, outs, sems)
        for cp in mine + first:
            cp.start()

    def middle(ins, outs, sems):
        _, c, x_nb, y_nb, _, copy, _, _, second, _ = parts(ins, outs, sems)
        for a in range(na):
            copy(a, 1, (*x_nb, c), (*x_nb, c)).wait_recv()
            copy(a, 2, (*y_nb, c), (*y_nb, c)).wait_recv()
        for cp in second:
            cp.start()

    def finish(ins, outs, sems):
        sibling, c, x_nb, y_nb, diag, copy, mine, first, second, third = parts(ins, outs, sems)
        for a in range(na):
            copy(a, 3, (*diag, c), (*diag, c)).wait_recv()
            third[a].start()
        for a in range(na):
            copy(a, 0, sibling, sibling).wait_recv()
            copy(a, 4, (*x_nb, 1 - c), sibling).wait_recv()
            copy(a, 5, (*y_nb, 1 - c), sibling).wait_recv()
            copy(a, 6, (*diag, 1 - c), sibling).wait_recv()
        for cp in first + second + third:
            cp.wait_send()
        for cp in mine:
            cp.wait()

    return types.SimpleNamespace(
        ins=list(shards) + list(bufs or []),
        out_shapes=[jax.ShapeDtypeStruct((N_DEV,) + s.shape, s.dtype) for s in shards],
        sems=[pltpu.SemaphoreType.DMA((n_kinds * na,)), pltpu.SemaphoreType.DMA((n_kinds * na,)),
              pltpu.SemaphoreType.DMA((na,))],
        aliases=[(na + a, a) for a in range(na)] if bufs else [],
        peers=frozenset({"sibling", "neighbours"}), start=start, middle=middle, finish=finish)


def _rs_sibling_plan(fulls):
    na = len(fulls)
    rs = [f.shape[0] // N_DEV for f in fulls]

    def copies(ins, outs, sems):
        send_sems, recv_sems = sems
        x, y, c = _place()
        out = []
        for a in range(na):
            for q in range(4):
                shard = 2 * q + (1 - c)
                out.append(pltpu.make_async_remote_copy(
                    src_ref=ins[a].at[pl.ds(shard * rs[a], rs[a])],
                    dst_ref=outs[a].at[q],
                    send_sem=send_sems.at[a * 4 + q],
                    recv_sem=recv_sems.at[a * 4 + q],
                    device_id=(x, y, 1 - c),
                    device_id_type=MESH,
                ))
        return out

    def start(ins, outs, sems):
        for cp in copies(ins, outs, sems):
            cp.start()

    def finish(ins, outs, sems):
        for cp in copies(ins, outs, sems):
            cp.wait()

    return types.SimpleNamespace(
        ins=list(fulls),
        out_shapes=[jax.ShapeDtypeStruct((4, r) + f.shape[1:], f.dtype) for r, f in zip(rs, fulls)],
        sems=[pltpu.SemaphoreType.DMA((4 * na,)), pltpu.SemaphoreType.DMA((4 * na,))],
        peers=frozenset({"sibling"}), start=start, finish=finish)


def _rs_chips_plan(sends, pieces=None, bufs=None):
    na = len(sends)

    def copies(ins, outs, sems):
        send_sems, recv_sems = sems
        x, y, c = _place()
        chips = [(1 - x, y), (x, 1 - y), (1 - x, 1 - y)]
        out = []
        for a in range(na):
            for k, chip in enumerate(chips):
                rows = (k,) if pieces is None else (k, pl.ds(*pieces[a]))
                out.append(pltpu.make_async_remote_copy(
                    src_ref=ins[a].at[rows],
                    dst_ref=outs[a].at[rows],
                    send_sem=send_sems.at[a * 3 + k],
                    recv_sem=recv_sems.at[a * 3 + k],
                    device_id=(*chip, c),
                    device_id_type=MESH,
                ))
        return out

    def start(ins, outs, sems):
        for cp in copies(ins, outs, sems):
            cp.start()

    def finish(ins, outs, sems):
        for cp in copies(ins, outs, sems):
            cp.wait()

    return types.SimpleNamespace(
        ins=list(sends) + list(bufs or []),
        out_shapes=[jax.ShapeDtypeStruct(s.shape, s.dtype) for s in sends],
        sems=[pltpu.SemaphoreType.DMA((3 * na,)), pltpu.SemaphoreType.DMA((3 * na,))],
        aliases=[(na + a, a) for a in range(na)] if bufs else [],
        peers=frozenset({"chips"}), start=start, finish=finish)


def _join(plans):
    ins, outs, sems, aliases, offs = [], [], [], [], []
    for p in plans:
        offs.append((len(ins), len(outs), len(sems)))
        aliases += [(len(ins) + ci, len(outs) + co) for ci, co in getattr(p, "aliases", [])]
        ins += p.ins
        outs += p.out_shapes
        sems += p.sems

    def cut(p, off, i, o, s):
        return (i[off[0]:off[0] + len(p.ins)], o[off[1]:off[1] + len(p.out_shapes)],
                s[off[2]:off[2] + len(p.sems)])

    def start(i, o, s):
        for p, off in zip(plans, offs):
            p.start(*cut(p, off, i, o, s))

    def middle(i, o, s):
        for p, off in zip(plans, offs):
            if getattr(p, "middle", None) is not None:
                p.middle(*cut(p, off, i, o, s))

    def finish(i, o, s):
        for p, off in zip(plans, offs):
            p.finish(*cut(p, off, i, o, s))

    def split(results):
        return [list(results[off[1]:off[1] + len(p.out_shapes)]) for p, off in zip(plans, offs)]

    return types.SimpleNamespace(ins=ins, out_shapes=outs, sems=sems, aliases=aliases,
                                 peers=frozenset().union(*[p.peers for p in plans]),
                                 start=start, middle=middle, finish=finish, split=split)


COLLECTIVE_ID = {frozenset({"sibling"}): 0, frozenset({"chips"}): 1, frozenset({"sibling", "chips"}): 2,
                 frozenset({"sibling", "neighbours"}): 3}


def _handshake(peers):
    x, y, c = _place()
    devs = []
    if "sibling" in peers:
        devs.append((x, y, 1 - c))
    if "neighbours" in peers:
        devs += [(1 - x, y, c), (x, 1 - y, c)]
    if "chips" in peers:
        assert "neighbours" not in peers
        devs += [(1 - x, y, c), (x, 1 - y, c), (1 - x, 1 - y, c)]
    barrier = pltpu.get_barrier_semaphore()
    for dev in devs:
        pl.semaphore_signal(barrier, inc=1, device_id=dev, device_id_type=MESH)
    pl.semaphore_wait(barrier, len(devs))


def _in_hbm(args):
    return [pltpu.with_memory_space_constraint(a, pltpu.HBM) for a in args]


def _run_plan(plan, name):
    n_in, n_out = len(plan.ins), len(plan.out_shapes)

    def body(*refs):
        ins, outs, sems = refs[:n_in], refs[n_in:n_in + n_out], refs[n_in + n_out:]
        _handshake(plan.peers)
        plan.start(ins, outs, sems)
        if getattr(plan, "middle", None) is not None:
            plan.middle(ins, outs, sems)
        plan.finish(ins, outs, sems)

    return pl.pallas_call(
        body,
        name=name,
        in_specs=[ANY] * n_in,
        out_specs=[ANY] * n_out,
        out_shape=plan.out_shapes,
        scratch_shapes=plan.sems,
        input_output_aliases=dict(getattr(plan, "aliases", [])),
        compiler_params=pltpu.CompilerParams(collective_id=COLLECTIVE_ID[plan.peers]),
    )(*_in_hbm(plan.ins))


def _call(body, *, name, grid, in_specs, out_specs, out_shape, args, scratch_shapes=(), aliases=None,
          carry=None, relay_at=(2, 3)):
    n_in, n_out, n_scr = len(in_specs), len(out_shape), len(scratch_shapes)
    params = pltpu.CompilerParams(
        dimension_semantics=("arbitrary",) * len(grid), vmem_limit_bytes=V7X_VMEM_LIMIT_BYTES)
    if carry is None:
        outs = pl.pallas_call(
            body, name=name, grid=grid, in_specs=list(in_specs), out_specs=list(out_specs),
            out_shape=list(out_shape), scratch_shapes=list(scratch_shapes),
            input_output_aliases=aliases or {}, compiler_params=params)(*_in_hbm(args))
        return list(outs), []
    c_in, c_out = len(carry.ins), len(carry.out_shapes)

    def full(*refs):
        p = 0
        ins = refs[p:p + n_in]
        p += n_in
        cins = refs[p:p + c_in]
        p += c_in
        outs = refs[p:p + n_out]
        p += n_out
        couts = refs[p:p + c_out]
        p += c_out
        scr = refs[p:p + n_scr]
        csems = refs[p + n_scr:]
        ids = [pl.program_id(a) for a in range(len(grid))]
        first = functools.reduce(operator.and_, [i == 0 for i in ids])
        last = functools.reduce(operator.and_, [i == g - 1 for i, g in zip(ids, grid)])

        @pl.when(first)
        def _():
            _handshake(carry.peers)
            carry.start(cins, couts, csems)

        if getattr(carry, "middle", None) is not None:
            n_steps = math.prod(grid)
            flat = functools.reduce(lambda acc, ig: acc * ig[1] + ig[0], zip(ids, grid), 0)

            @pl.when(flat == (relay_at[0] * n_steps) // relay_at[1])
            def _():
                carry.middle(cins, couts, csems)

        body(*ins, *outs, *scr)

        @pl.when(last)
        def _():
            carry.finish(cins, couts, csems)

    all_aliases = dict(aliases or {})
    all_aliases.update({n_in + ci: n_out + co for ci, co in getattr(carry, "aliases", [])})
    params = pltpu.CompilerParams(
        dimension_semantics=("arbitrary",) * len(grid), vmem_limit_bytes=V7X_VMEM_LIMIT_BYTES,
        collective_id=COLLECTIVE_ID[carry.peers])
    outs = pl.pallas_call(
        full, name=name, grid=grid,
        in_specs=list(in_specs) + [ANY] * c_in,
        out_specs=list(out_specs) + [ANY] * c_out,
        out_shape=list(out_shape) + list(carry.out_shapes),
        scratch_shapes=list(scratch_shapes) + list(carry.sems),
        input_output_aliases=all_aliases, compiler_params=params)(*_in_hbm(args), *_in_hbm(carry.ins))
    return list(outs[:n_out]), list(outs[n_out:])


def _norm_proj(x, g1, w_int, carry=None):
    t, d = x.shape
    n = w_int.shape[0]
    tt, tn = _tile(t, 2048), _tile(n, 512)

    def body(x_ref, g_ref, w_ref, proj_ref, h1_ref, h1_s):
        @pl.when(pl.program_id(1) == 0)
        def _():
            def norm_rows(rows):
                xhat, _ = _rms_hat(x_ref[rows, :])
                h = (xhat * g_ref[...]).astype(BF16)
                h1_s[rows, :] = h
                h1_ref[rows, :] = h

            _row_chunks(tt, norm_rows)

        proj_ref[...] = _dot_nt(h1_s[...], w_ref[...]).astype(BF16)

    return _call(
        body, name="norm_proj", grid=(t // tt, n // tn),
        in_specs=[
            pl.BlockSpec((tt, d), lambda i, j: (i, 0)),
            pl.BlockSpec((1, d), lambda i, j: (0, 0)),
            pl.BlockSpec((tn, d), lambda i, j: (j, 0)),
        ],
        out_specs=[
            pl.BlockSpec((tt, tn), lambda i, j: (i, j)),
            pl.BlockSpec((tt, d), lambda i, j: (i, 0)),
        ],
        out_shape=[jax.ShapeDtypeStruct((t, n), BF16), jax.ShapeDtypeStruct((t, d), BF16)],
        scratch_shapes=[pltpu.VMEM((tt, d), BF16)],
        args=(x, g1, w_int), carry=carry)


def _scan_rows(av, bv, reverse):
    tc = av.shape[0]
    row = lax.broadcasted_iota(jnp.int32, av.shape, 0)
    s = 1
    while s < tc:
        if s < 8:
            keep = (row < tc - s) if reverse else (row >= s)
            shift = (tc - s) if reverse else s
            a_sh = jnp.where(keep, pltpu.roll(av, shift, 0), 1.0)
            b_sh = jnp.where(keep, pltpu.roll(bv, shift, 0), 0.0)
            bv = av * b_sh + bv
            av = av * a_sh
        elif reverse:
            bv = jnp.concatenate([av[:tc - s] * bv[s:] + bv[:tc - s], bv[tc - s:]], axis=0)
            av = jnp.concatenate([av[:tc - s] * av[s:], av[tc - s:]], axis=0)
        else:
            bv = jnp.concatenate([bv[:s], av[s:] * bv[:tc - s] + bv[s:]], axis=0)
            av = jnp.concatenate([av[:s], av[s:] * av[:tc - s]], axis=0)
        s *= 2
    return av, bv


N_LRU_SAVED = 5


def _fill_block_diag(w_ref, bd_ref):
    bd_ref[...] = jnp.zeros_like(bd_ref)
    hd = LRU_HEAD_DIM
    for k in range(w_ref.shape[0]):
        bd_ref[k * hd:(k + 1) * hd, k * hd:(k + 1) * hd] = w_ref[k].astype(BF16)


def _lru_fwd(proj, conv_w, conv_b, w_a, b_a, w_x, b_x, lam, carry=None):
    t = proj.shape[0]
    dr = conv_b.shape[1]
    cb = LRU_CB
    tc = _tile(t, 256)
    ncb, ntc = dr // cb, t // tc

    def body(xp_ref, g_ref, cw_ref, cb_ref, wa_ref, ba_ref, wx_ref, bx_ref, lam_ref,
             y_ref, h_ref, saved_ref, prevx_s, hlast_s, wa_s, wx_s):
        c = pl.program_id(1)

        @pl.when(c == 0)
        def _():
            prevx_s[...] = jnp.zeros_like(prevx_s)
            hlast_s[...] = jnp.zeros_like(hlast_s)
            _fill_block_diag(wa_ref, wa_s)
            _fill_block_diag(wx_ref, wx_s)

        x = xp_ref[...].astype(F32)
        prev = prevx_s[...]
        row = lax.broadcasted_iota(jnp.int32, x.shape, 0)

        def sh(j):
            return jnp.where(row >= j, pltpu.roll(x, j, 0), pltpu.roll(prev, j, 0))

        xc = (cb_ref[...] + cw_ref[0:1, :] * sh(3) + cw_ref[1:2, :] * sh(2)
              + cw_ref[2:3, :] * sh(1) + cw_ref[3:4, :] * x)
        prevx_s[...] = x
        _, r, i, _, _, a, mult = _lru_gates(xc, wa_s[...], ba_ref[...], wx_s[...], bx_ref[...],
                                            lam_ref[...])
        for k, val in enumerate((xc, r, i, a, mult)):
            saved_ref[:, k * cb:(k + 1) * cb] = val
        av, bv = _scan_rows(a, mult * (i * xc), reverse=False)
        h = av * hlast_s[...] + bv
        h_ref[...] = h
        hlast_s[...] = h_ref[tc - 1:tc, :]
        gel, _ = _gelu_and_grad(g_ref[...].astype(F32))
        y_ref[...] = (h * gel).astype(BF16)

    vec = pl.BlockSpec((1, cb), lambda j, c: (0, j))
    blk = pl.BlockSpec((tc, cb), lambda j, c: (c, j))
    mat = pl.BlockSpec((cb // LRU_HEAD_DIM, LRU_HEAD_DIM, LRU_HEAD_DIM), lambda j, c: (j, 0, 0))
    return _call(
        body, name="lru_fwd", grid=(ncb, ntc),
        in_specs=[
            blk,
            pl.BlockSpec((tc, cb), lambda j, c: (c, ncb + j)),
            pl.BlockSpec((4, cb), lambda j, c: (0, j)),
            vec, mat, vec, mat, vec, vec,
        ],
        out_specs=[blk, blk, pl.BlockSpec((tc, N_LRU_SAVED * cb), lambda j, c: (c, j))],
        out_shape=[jax.ShapeDtypeStruct((t, dr), BF16), jax.ShapeDtypeStruct((t, dr), F32),
                   jax.ShapeDtypeStruct((t, N_LRU_SAVED * dr), F32)],
        scratch_shapes=[pltpu.VMEM((tc, cb), F32), pltpu.VMEM((1, cb), F32),
                        pltpu.VMEM((cb, cb), BF16), pltpu.VMEM((cb, cb), BF16)],
        args=(proj, proj, conv_w, conv_b, w_a, b_a, w_x, b_x, lam), carry=carry, relay_at=(1, 2))


def _pool_select(col, vals):
    out = vals[3]
    for g in (2, 1, 0):
        out = jnp.where(col < (g + 1) * POOL_GROUP_DIM, vals[g], out)
    return out


def _pool_fwd(proj, pool_w, pool_scale, col_block):
    t = proj.shape[0]
    dp = pool_scale.shape[1]
    tc = _tile(t, 256)
    ntc = t // tc

    def body(x_ref, w_ref, sc_ref, y_ref, p_ref, px, p2, p4, p8):
        c = pl.program_id(0)

        @pl.when(c == 0)
        def _():
            for s in (px, p2, p4, p8):
                s[...] = jnp.zeros_like(s)

        x = x_ref[...].astype(F32)
        row = lax.broadcasted_iota(jnp.int32, x.shape, 0)
        col = lax.broadcasted_iota(jnp.int32, x.shape, 1)

        def sh(v, pv, j):
            return jnp.where(row >= j, pltpu.roll(v, j, 0), pltpu.roll(pv[...], j, 0))

        s2 = x + sh(x, px, 1)
        s4 = s2 + sh(s2, p2, 2)
        s8 = s4 + sh(s4, p4, 4)
        s16 = s8 + sh(s8, p8, 8)
        px[...] = x
        p2[...] = s2
        p4[...] = s4
        p8[...] = s8
        wsum = _pool_select(col, (s2, s4, s8, s16))
        win = _pool_select(col, POOL_WINDOWS)
        cnt = jnp.minimum(c * tc + row + 1, win).astype(F32)
        p = wsum / cnt - x
        pb = p.astype(BF16)
        p_ref[...] = pb
        for g in range(len(POOL_WINDOWS)):
            sl = slice(g * POOL_GROUP_DIM, (g + 1) * POOL_GROUP_DIM)
            yg = _dot_nn(pb[:, sl], w_ref[g]) * sc_ref[:, sl]
            y_ref[:, sl] = yg.astype(BF16)

    return _call(
        body, name="pool_fwd", grid=(ntc,),
        in_specs=[
            pl.BlockSpec((tc, dp), lambda c: (c, col_block)),
            pl.BlockSpec(pool_w.shape, lambda c: (0, 0, 0)),
            pl.BlockSpec((1, dp), lambda c: (0, 0)),
        ],
        out_specs=[pl.BlockSpec((tc, dp), lambda c: (c, 0))] * 2,
        out_shape=[jax.ShapeDtypeStruct((t, dp), BF16)] * 2,
        scratch_shapes=[pltpu.VMEM((tc, dp), F32)] * 4,
        args=(proj, pool_w, pool_scale))[0]


def _branch_mix(y_lru, y_pool, w_lru_up, w_pool_upb, proj, b_gate, ga_block, gb_block, carry=None):
    t, d = y_lru.shape
    dp = y_pool.shape[1]
    bw = w_pool_upb.shape[2]
    tt, tn = _tile(t, 1024), 512
    nj = d // tn

    def body(yl_ref, yp_ref, wl_ref, wp_ref, ga_ref, gb_ref, ba_ref, bb_ref, bra_ref, brb_ref, mix_ref):
        br_a = _dot_nn(yl_ref[...], wl_ref[...])
        wp = jnp.concatenate([wp_ref[b] for b in range(tn // bw)], axis=1)
        br_b = _dot_nn(yp_ref[...], wp)
        bra_ref[...] = br_a.astype(BF16)
        brb_ref[...] = br_b.astype(BF16)
        ga = _sig(ga_ref[...].astype(F32) + ba_ref[...])
        gb = _sig(gb_ref[...].astype(F32) + bb_ref[...])
        mix_ref[...] = (ga * br_a + gb * br_b).astype(BF16)

    out = pl.BlockSpec((tt, tn), lambda j, i: (i, j))
    return _call(
        body, name="branch_mix", grid=(nj, t // tt),
        in_specs=[
            pl.BlockSpec((tt, d), lambda j, i: (i, 0)),
            pl.BlockSpec((tt, dp), lambda j, i: (i, 0)),
            pl.BlockSpec((d, tn), lambda j, i: (0, j)),
            pl.BlockSpec((tn // bw, dp, bw), lambda j, i: (j, 0, 0)),
            pl.BlockSpec((tt, tn), lambda j, i: (i, ga_block + j)),
            pl.BlockSpec((tt, tn), lambda j, i: (i, gb_block + j)),
            pl.BlockSpec((1, tn), lambda j, i: (0, j)),
            pl.BlockSpec((1, tn), lambda j, i: (0, nj + j)),
        ],
        out_specs=[out, out, out],
        out_shape=[jax.ShapeDtypeStruct((t, d), BF16)] * 3,
        args=(y_lru, y_pool, w_lru_up, w_pool_upb, proj, proj, b_gate, b_gate), carry=carry)


def _wo_norm(mix, w_o, x, g2, g3, carry=None):
    t, d = x.shape
    tt = _tile(t, 512)

    def body(mix_ref, w_ref, x_ref, g2_ref, g3_ref, m_ref, x2_ref, h3_ref):
        m = _dot_nn(mix_ref[...], w_ref[...])
        m_ref[...] = m
        mhat, _ = _rms_hat(m)
        x2 = x_ref[...] + mhat * g2_ref[...]
        x2_ref[...] = x2
        xhat, _ = _rms_hat(x2)
        h3_ref[...] = (xhat * g3_ref[...]).astype(BF16)

    row = pl.BlockSpec((tt, d), lambda i: (i, 0))
    vec = pl.BlockSpec((1, d), lambda i: (0, 0))
    return _call(
        body, name="wo_norm", grid=(t // tt,),
        in_specs=[row, pl.BlockSpec((d, d), lambda i: (0, 0)), row, vec, vec],
        out_specs=[row, row, row],
        out_shape=[
            jax.ShapeDtypeStruct((t, d), F32),
            jax.ShapeDtypeStruct((t, d), F32),
            jax.ShapeDtypeStruct((t, d), BF16),
        ],
        args=(mix, w_o, x, g2, g3), carry=carry)


def _ff1(h3, w_ff1b, carry=None):
    t, d = h3.shape
    nb, _, tn = w_ff1b.shape
    tt = _tile(t, 2048)

    def body(h_ref, w_ref, rf_ref):
        rf_ref[...] = jnp.maximum(_dot_nn(h_ref[...], w_ref[...]), 0.0).astype(BF16)

    out = pl.BlockSpec((tt, tn), lambda i, j: (i, j))
    return _call(
        body, name="ff1", grid=(t // tt, nb),
        in_specs=[pl.BlockSpec((tt, d), lambda i, j: (i, 0)), pl.BlockSpec((None, d, tn), lambda i, j: (j, 0, 0))],
        out_specs=[out],
        out_shape=[jax.ShapeDtypeStruct((t, nb * tn), BF16)],
        args=(h3, w_ff1b), carry=carry)


def _ff2_loss(rf, w_ff2, x2, g4, target):
    t, k = rf.shape
    d = x2.shape[1]
    tt, tk = _tile(t, 1024), _tile(k, 1024)
    nk = k // tk

    def body(a_ref, w_ref, x2_ref, g_ref, tg_ref, dy_ref, df_ref, dg_ref, loss_ref, acc):
        i, kk = pl.program_id(0), pl.program_id(1)

        @pl.when(kk == 0)
        def _():
            acc[...] = jnp.zeros_like(acc)

        @pl.when((i == 0) & (kk == 0))
        def _():
            dg_ref[...] = jnp.zeros_like(dg_ref)
            loss_ref[...] = jnp.zeros_like(loss_ref)

        rf_tile = a_ref[...]
        acc[...] += _dot_nn(rf_tile * rf_tile, w_ref[...])

        @pl.when(kk == nk - 1)
        def _():
            def tail(rows):
                fhat, r = _rms_hat(acc[rows, :])
                g = g_ref[...]
                e = x2_ref[rows, :] + fhat * g - tg_ref[rows, :]
                loss_ref[...] += 0.5 * jnp.sum(jnp.mean(e * e, axis=-1, keepdims=True))
                dy = e * (1.0 / d)
                dy_ref[rows, :] = dy.astype(BF16)
                df, dg = _rms_bwd(dy, fhat, r, g)
                df_ref[rows, :] = df.astype(BF16)
                dg_ref[...] += dg

            _row_chunks(tt, tail)

    row = pl.BlockSpec((tt, d), lambda i, kk: (i, 0))
    vec = pl.BlockSpec((1, d), lambda i, kk: (0, 0))
    return _call(
        body, name="ff2_loss", grid=(t // tt, nk),
        in_specs=[
            pl.BlockSpec((tt, tk), lambda i, kk: (i, kk)),
            pl.BlockSpec((tk, d), lambda i, kk: (kk, 0)),
            row, vec, row,
        ],
        out_specs=[row, row, vec, pl.BlockSpec((1, 128), lambda i, kk: (0, 0))],
        out_shape=[
            jax.ShapeDtypeStruct((t, d), BF16),
            jax.ShapeDtypeStruct((t, d), BF16),
            jax.ShapeDtypeStruct((1, d), F32),
            jax.ShapeDtypeStruct((1, 128), F32),
        ],
        scratch_shapes=[pltpu.VMEM((tt, d), F32)],
        args=(rf, w_ff2, x2, g4, target))[0]


def _ff2_bwd(df, w_ff2, rf, carry=None):
    t, d = df.shape
    n = w_ff2.shape[0]
    tt, tn = _tile(t, 2048), _tile(n, 512)

    def body(df_ref, w_ref, rf_ref, out_ref):
        d_act = _dot_nt(df_ref[...], w_ref[...])
        out_ref[...] = (d_act * (2.0 * rf_ref[...].astype(F32))).astype(BF16)

    blk = pl.BlockSpec((tt, tn), lambda i, j: (i, j))
    return _call(
        body, name="ff2_bwd", grid=(t // tt, n // tn),
        in_specs=[pl.BlockSpec((tt, d), lambda i, j: (i, 0)), pl.BlockSpec((tn, d), lambda i, j: (j, 0)), blk],
        out_specs=[blk],
        out_shape=[jax.ShapeDtypeStruct((t, n), BF16)],
        args=(df, w_ff2, rf), carry=carry)


def _wgrad(a, b, name, prev=None, row_off=0, rows=None, carry=None, square_a=False):
    t, m = a.shape
    n = b.shape[1]
    rows = m if rows is None else rows
    tm, tk = _tile(m, 512), _tile(t, 2048)
    nk = t // tk
    assert row_off % tm == 0
    off = row_off // tm

    def body(*refs):
        a_ref, b_ref = refs[0], refs[1]
        o32_ref, o16_ref, acc = refs[-3], refs[-2], refs[-1]
        kk = pl.program_id(1)

        @pl.when(kk == 0)
        def _():
            acc[...] = jnp.zeros_like(acc)

        a_tile = a_ref[...]
        acc[...] += _dot_tn(a_tile * a_tile if square_a else a_tile, b_ref[...])

        @pl.when(kk == nk - 1)
        def _():
            o32_ref[...] = acc[...]
            o16_ref[...] = acc[...].astype(BF16)

    in_specs = [pl.BlockSpec((tk, tm), lambda i, kk: (kk, i)), pl.BlockSpec((tk, n), lambda i, kk: (kk, 0))]
    args = [a, b]
    aliases = {}
    if prev is not None:
        in_specs += [ANY, ANY]
        args += list(prev)
        aliases = {2: 0, 3: 1}
    out = pl.BlockSpec((tm, n), lambda i, kk: (off + i, 0))
    return _call(
        body, name=name, grid=(m // tm, nk),
        in_specs=in_specs, out_specs=[out, out],
        out_shape=[jax.ShapeDtypeStruct((rows, n), F32), jax.ShapeDtypeStruct((rows, n), BF16)],
        scratch_shapes=[pltpu.VMEM((tm, n), F32)],
        aliases=aliases, args=args, carry=carry)


def _wgrad_parts(parts, b, name, carry=None):
    t, n = b.shape
    tm = 512
    bounds = []
    lo = 0
    for part in parts:
        assert part.shape[0] == t and part.shape[1] % tm == 0
        bounds.append((lo, lo + part.shape[1] // tm))
        lo += part.shape[1] // tm
    nm = lo
    np_ = len(parts)

    def body(*refs):
        p_refs, b_ref, o32_ref, o16_ref = refs[:np_], refs[np_], refs[np_ + 1], refs[np_ + 2]
        i = pl.program_id(0)
        for (lo_p, hi_p), p_ref in zip(bounds, p_refs):
            @pl.when((i >= lo_p) & (i < hi_p))
            def _(p_ref=p_ref):
                res = _dot_tn(p_ref[...], b_ref[...])
                o32_ref[...] = res
                o16_ref[...] = res.astype(BF16)

    def part_spec(lo_p, hi_p):
        return pl.BlockSpec((t, tm), lambda i: (0, jnp.clip(i - lo_p, 0, hi_p - lo_p - 1)))

    out = pl.BlockSpec((tm, n), lambda i: (i, 0))
    return _call(
        body, name=name, grid=(nm,),
        in_specs=[part_spec(lo_p, hi_p) for lo_p, hi_p in bounds] + [pl.BlockSpec((t, n), lambda i: (0, 0))],
        out_specs=[out, out],
        out_shape=[jax.ShapeDtypeStruct((nm * tm, n), F32), jax.ShapeDtypeStruct((nm * tm, n), BF16)],
        args=(*parts, b), carry=carry)


def _wgrad_cols(a, b, bw, tn, name, carry=None):
    t, m = a.shape
    n = b.shape[1]
    per_step = tn // bw

    def body(a_ref, b_ref, o32_ref, o16_ref):
        res = _dot_tn(a_ref[...], b_ref[...])
        for blk in range(per_step):
            part = res[:, blk * bw:(blk + 1) * bw]
            o32_ref[blk] = part
            o16_ref[blk] = part.astype(BF16)

    out = pl.BlockSpec((per_step, m, bw), lambda j: (j, 0, 0))
    return _call(
        body, name=name, grid=(n // tn,),
        in_specs=[pl.BlockSpec((t, m), lambda j: (0, 0)), pl.BlockSpec((t, tn), lambda j: (0, j))],
        out_specs=[out, out],
        out_shape=[jax.ShapeDtypeStruct((n // bw, m, bw), F32), jax.ShapeDtypeStruct((n // bw, m, bw), BF16)],
        args=(a, b), carry=carry)


def _ff1_bwd_norms(d_f1, w_ff1b, dy, x2, g3, m, g2, carry=None):
    t, k = d_f1.shape
    d = x2.shape[1]
    bw = w_ff1b.shape[2]
    per_step = 2
    tt, tk = _tile(t, 1024), per_step * bw
    nk = k // tk

    def body(a_ref, w_ref, dy_ref, x2_ref, g3_ref, m_ref, g2_ref, dx2_ref, dm_ref, dg3_ref, dg2_ref, acc):
        i, kk = pl.program_id(0), pl.program_id(1)

        @pl.when(kk == 0)
        def _():
            acc[...] = jnp.zeros_like(acc)

        @pl.when((i == 0) & (kk == 0))
        def _():
            dg3_ref[...] = jnp.zeros_like(dg3_ref)
            dg2_ref[...] = jnp.zeros_like(dg2_ref)

        a_tile = a_ref[...]
        for b in range(per_step):
            acc[...] += _dot_nt(a_tile[:, b * bw:(b + 1) * bw], w_ref[b])

        @pl.when(kk == nk - 1)
        def _():
            def tail(rows):
                xhat, r3 = _rms_hat(x2_ref[rows, :])
                dx, dg3 = _rms_bwd(acc[rows, :], xhat, r3, g3_ref[...])
                dx2 = dy_ref[rows, :].astype(F32) + dx
                dx2_ref[rows, :] = dx2
                dg3_ref[...] += dg3
                mhat, r2 = _rms_hat(m_ref[rows, :])
                dm, dg2 = _rms_bwd(dx2, mhat, r2, g2_ref[...])
                dm_ref[rows, :] = dm.astype(BF16)
                dg2_ref[...] += dg2

            _row_chunks(tt, tail)

    row = pl.BlockSpec((tt, d), lambda i, kk: (i, 0))
    vec = pl.BlockSpec((1, d), lambda i, kk: (0, 0))
    return _call(
        body, name="ff1_bwd_norms", grid=(t // tt, nk),
        in_specs=[
            pl.BlockSpec((tt, tk), lambda i, kk: (i, kk)),
            pl.BlockSpec((per_step, d, bw), lambda i, kk: (kk, 0, 0)),
            row, row, vec, row, vec,
        ],
        out_specs=[row, row, vec, vec],
        out_shape=[
            jax.ShapeDtypeStruct((t, d), F32),
            jax.ShapeDtypeStruct((t, d), BF16),
            jax.ShapeDtypeStruct((1, d), F32),
            jax.ShapeDtypeStruct((1, d), F32),
        ],
        scratch_shapes=[pltpu.VMEM((tt, d), F32)],
        args=(d_f1, w_ff1b, dy, x2, g3, m, g2), carry=carry)


def _wo_bwd_mix(dm, w_o, br_a, br_b, proj, b_gate, ga_block, gb_block, carry=None):
    t, d = dm.shape
    tt, tn = _tile(t, 1024), 512
    nj = d // tn

    def body(dm_ref, w_ref, bra_ref, brb_ref, ga_ref, gb_ref, ba_ref, bb_ref,
             dbra_ref, dbrb_ref, dga_ref, dgb_ref, dba_ref, dbb_ref):
        i = pl.program_id(1)

        @pl.when(i == 0)
        def _():
            dba_ref[...] = jnp.zeros_like(dba_ref)
            dbb_ref[...] = jnp.zeros_like(dbb_ref)

        d_mix = _dot_nt(dm_ref[...], w_ref[...])
        ga = _sig(ga_ref[...].astype(F32) + ba_ref[...])
        gb = _sig(gb_ref[...].astype(F32) + bb_ref[...])
        dbra_ref[...] = (d_mix * ga).astype(BF16)
        dbrb_ref[...] = (d_mix * gb).astype(BF16)
        dga = d_mix * bra_ref[...].astype(F32) * (ga * (1.0 - ga))
        dgb = d_mix * brb_ref[...].astype(F32) * (gb * (1.0 - gb))
        dga_ref[...] = dga.astype(BF16)
        dgb_ref[...] = dgb.astype(BF16)
        dba_ref[...] += jnp.sum(dga, axis=0, keepdims=True)
        dbb_ref[...] += jnp.sum(dgb, axis=0, keepdims=True)

    blk = pl.BlockSpec((tt, tn), lambda j, i: (i, j))
    vec = pl.BlockSpec((1, tn), lambda j, i: (0, j))
    return _call(
        body, name="wo_bwd_mix", grid=(nj, t // tt),
        in_specs=[
            pl.BlockSpec((tt, d), lambda j, i: (i, 0)),
            pl.BlockSpec((tn, d), lambda j, i: (j, 0)),
            blk, blk,
            pl.BlockSpec((tt, tn), lambda j, i: (i, ga_block + j)),
            pl.BlockSpec((tt, tn), lambda j, i: (i, gb_block + j)),
            vec,
            pl.BlockSpec((1, tn), lambda j, i: (0, nj + j)),
        ],
        out_specs=[blk, blk, blk, blk, vec, vec],
        out_shape=[jax.ShapeDtypeStruct((t, d), BF16)] * 4 + [jax.ShapeDtypeStruct((1, d), F32)] * 2,
        args=(dm, w_o, br_a, br_b, proj, proj, b_gate, b_gate), carry=carry)


def _lru_up_bwd(d_br_a, w_lru_up, proj, h, g_block, carry=None):
    t, d = d_br_a.shape
    tt, tn = _tile(t, 1024), 512

    def body(a_ref, w_ref, g_ref, h_ref, dh_ref, dg_ref):
        d_y = _dot_nt(a_ref[...], w_ref[...])
        gel, gel_grad = _gelu_and_grad(g_ref[...].astype(F32))
        dh_ref[...] = d_y * gel
        dg_ref[...] = (d_y * h_ref[...] * gel_grad).astype(BF16)

    blk = pl.BlockSpec((tt, tn), lambda i, j: (i, j))
    return _call(
        body, name="lru_up_bwd", grid=(t // tt, d // tn),
        in_specs=[
            pl.BlockSpec((tt, d), lambda i, j: (i, 0)),
            pl.BlockSpec((tn, d), lambda i, j: (j, 0)),
            pl.BlockSpec((tt, tn), lambda i, j: (i, g_block + j)),
            blk,
        ],
        out_specs=[blk, blk],
        out_shape=[jax.ShapeDtypeStruct((t, d), F32), jax.ShapeDtypeStruct((t, d), BF16)],
        args=(d_br_a, w_lru_up, proj, h), carry=carry)


def _lru_bwd(dh, h, saved, proj, conv_w, w_a, w_x, lam, carry=None):
    t, dr = dh.shape
    cb = LRU_CB
    hd = LRU_HEAD_DIM
    per = cb // hd
    tc = _tile(t, 256)
    ncb, ntc = dr // cb, t // tc

    def body(dh_ref, h_ref, hp_ref, saved_ref, xp_ref, cw_ref, wa_ref, wx_ref,
             lam_ref, dxp_ref, dwa_ref, dba_ref, dwx_ref, dbx_ref, dlam_ref, dcw_ref, dcb_ref,
             nextd_s, anext_s, gnext_s, tmp_s, wa_s, wx_s):
        c = pl.program_id(1)
        rc = ntc - 1 - c

        @pl.when(c == 0)
        def _():
            nextd_s[...] = jnp.zeros_like(nextd_s)
            anext_s[...] = jnp.zeros_like(anext_s)
            gnext_s[...] = jnp.zeros_like(gnext_s)
            for ref in (dwa_ref, dba_ref, dwx_ref, dbx_ref, dlam_ref, dcw_ref, dcb_ref):
                ref[...] = jnp.zeros_like(ref)
            _fill_block_diag(wa_ref, wa_s)
            _fill_block_diag(wx_ref, wx_s)

        xc, r, i, a, mult = [saved_ref[:, k * cb:(k + 1) * cb] for k in range(N_LRU_SAVED)]
        wa, wx, lam = wa_s[...], wx_s[...], lam_ref[...]
        xcb = xc.astype(BF16)
        sp = _softplus_neg(lam)
        row = lax.broadcasted_iota(jnp.int32, xc.shape, 0)
        h = h_ref[...]
        hp = jnp.where(rc == 0, 0.0, hp_ref[...])
        hprev = jnp.where(row >= 1, pltpu.roll(h, 1, 0), pltpu.roll(hp, 1, 0))

        def up(v, nv, j):
            return jnp.where(row < tc - j, pltpu.roll(v, tc - j, 0), nv)

        av, bv = _scan_rows(up(a, anext_s[...], 1), dh_ref[...], reverse=True)
        gt = av * gnext_s[...] + bv
        tmp_s[...] = gt
        gnext_s[...] = tmp_s[0:1, :]
        tmp_s[...] = a
        anext_s[...] = tmp_s[0:1, :]

        da = gt * hprev
        ixc = i * xc
        d_mult = gt * ixc
        d_i = gt * mult * xc
        d_xc = gt * mult * i
        d_log_a = da * a - d_mult * (a * a) / mult
        d_pre_r = (d_log_a * ((-LRU_C) * sp)) * (r * (1.0 - r))
        d_pre_i = d_i * (i * (1.0 - i))
        d_sp = jnp.sum(d_log_a * ((-LRU_C) * r), axis=0, keepdims=True)
        dlam_ref[...] += d_sp * (-1.0 / (1.0 + jnp.exp(lam)))
        dpr = d_pre_r.astype(BF16)
        dpi = d_pre_i.astype(BF16)
        dba_ref[...] += jnp.sum(d_pre_r, axis=0, keepdims=True)
        dbx_ref[...] += jnp.sum(d_pre_i, axis=0, keepdims=True)
        pa = _dot_tn(xcb, dpr)
        px = _dot_tn(xcb, dpi)
        for k in range(per):
            dwa_ref[k] += pa[k * hd:(k + 1) * hd, k * hd:(k + 1) * hd]
            dwx_ref[k] += px[k * hd:(k + 1) * hd, k * hd:(k + 1) * hd]
        d_xc = d_xc + _dot_nt(dpr, wa) + _dot_nt(dpi, wx)

        nxt = nextd_s[...]
        xp = xp_ref[...].astype(F32)
        dxp = cw_ref[3:4, :] * d_xc
        dcw_ref[3:4, :] += jnp.sum(xp * d_xc, axis=0, keepdims=True)
        for j in (1, 2, 3):
            uj = up(d_xc, pltpu.roll(nxt, tc - j, 0), j)
            dxp = dxp + cw_ref[3 - j:4 - j, :] * uj
            dcw_ref[3 - j:4 - j, :] += jnp.sum(xp * uj, axis=0, keepdims=True)
        dcb_ref[...] += jnp.sum(d_xc, axis=0, keepdims=True)
        nextd_s[...] = d_xc
        dxp_ref[...] = dxp.astype(BF16)

    vec = pl.BlockSpec((1, cb), lambda j, c: (0, j))
    blk = pl.BlockSpec((tc, cb), lambda j, c: (ntc - 1 - c, j))
    mat = pl.BlockSpec((per, hd, hd), lambda j, c: (j, 0, 0))
    cwb = pl.BlockSpec((4, cb), lambda j, c: (0, j))
    return _call(
        body, name="lru_bwd", grid=(ncb, ntc),
        in_specs=[
            blk, blk,
            pl.BlockSpec((tc, cb), lambda j, c: (jnp.maximum(ntc - 2 - c, 0), j)),
            pl.BlockSpec((tc, N_LRU_SAVED * cb), lambda j, c: (ntc - 1 - c, j)),
            blk, cwb, mat, mat, vec,
        ],
        out_specs=[blk, mat, vec, mat, vec, vec, cwb, vec],
        out_shape=[
            jax.ShapeDtypeStruct((t, dr), BF16),
            jax.ShapeDtypeStruct(w_a.shape, F32),
            jax.ShapeDtypeStruct((1, dr), F32),
            jax.ShapeDtypeStruct(w_x.shape, F32),
            jax.ShapeDtypeStruct((1, dr), F32),
            jax.ShapeDtypeStruct((1, dr), F32),
            jax.ShapeDtypeStruct((4, dr), F32),
            jax.ShapeDtypeStruct((1, dr), F32),
        ],
        scratch_shapes=[
            pltpu.VMEM((tc, cb), F32),
            pltpu.VMEM((1, cb), F32),
            pltpu.VMEM((1, cb), F32),
            pltpu.VMEM((tc, cb), F32),
            pltpu.VMEM((cb, cb), BF16),
            pltpu.VMEM((cb, cb), BF16),
        ],
        args=(dh, h, h, saved, proj, conv_w, w_a, w_x, lam), carry=carry)


def _pool_bwd(d_br_b, w_pool_upb, p, pool_w, pool_scale):
    t, d = d_br_b.shape
    nwb, dp, _ = w_pool_upb.shape
    tc = _tile(t, 256)
    ntc = t // tc
    ng = len(POOL_WINDOWS)

    def body(db_ref, wu_ref, p_ref, w_ref, sc_ref, dx_ref, dw_ref, dsc_ref, nz, n2, n4, n8, dp_s, dy_s):
        c = pl.program_id(0)
        rc = ntc - 1 - c

        @pl.when(c == 0)
        def _():
            for s in (nz, n2, n4, n8):
                s[...] = jnp.zeros_like(s)
            dw_ref[...] = jnp.zeros_like(dw_ref)
            dsc_ref[...] = jnp.zeros_like(dsc_ref)

        wu = jnp.concatenate([wu_ref[b] for b in range(nwb)], axis=1)
        dy_s[...] = _dot_nt(db_ref[...], wu)
        for g in range(ng):
            sl = slice(g * POOL_GROUP_DIM, (g + 1) * POOL_GROUP_DIM)
            pg = p_ref[:, sl]
            dyg = dy_s[:, sl]
            wg = w_ref[g].astype(BF16)
            q = _dot_nn(pg, wg)
            dsc_ref[:, sl] += jnp.sum(dyg * q, axis=0, keepdims=True)
            dpw = (dyg * sc_ref[:, sl]).astype(BF16)
            dw_ref[g] += _dot_tn(pg, dpw)
            dp_s[:, sl] = _dot_nt(dpw, wg)

        dpv = dp_s[...]
        row = lax.broadcasted_iota(jnp.int32, dpv.shape, 0)
        col = lax.broadcasted_iota(jnp.int32, dpv.shape, 1)
        win = _pool_select(col, POOL_WINDOWS)
        cnt = jnp.minimum(rc * tc + row + 1, win).astype(F32)
        z = dpv / cnt

        def up(v, nv, j):
            return jnp.where(row < tc - j, pltpu.roll(v, tc - j, 0), pltpu.roll(nv[...], tc - j, 0))

        u2 = z + up(z, nz, 1)
        u4 = u2 + up(u2, n2, 2)
        u8 = u4 + up(u4, n4, 4)
        u16 = u8 + up(u8, n8, 8)
        nz[...] = z
        n2[...] = u2
        n4[...] = u4
        n8[...] = u8
        dx_ref[...] = (_pool_select(col, (u2, u4, u8, u16)) - dpv).astype(BF16)

    blk = pl.BlockSpec((tc, dp), lambda c: (ntc - 1 - c, 0))
    full_w = pl.BlockSpec(pool_w.shape, lambda c: (0, 0, 0))
    vec = pl.BlockSpec((1, dp), lambda c: (0, 0))
    return _call(
        body, name="pool_bwd", grid=(ntc,),
        in_specs=[pl.BlockSpec((tc, d), lambda c: (ntc - 1 - c, 0)),
                  pl.BlockSpec(w_pool_upb.shape, lambda c: (0, 0, 0)), blk, full_w, vec],
        out_specs=[blk, full_w, vec],
        out_shape=[
            jax.ShapeDtypeStruct((t, dp), BF16),
            jax.ShapeDtypeStruct(pool_w.shape, F32),
            jax.ShapeDtypeStruct((1, dp), F32),
        ],
        scratch_shapes=[pltpu.VMEM((tc, dp), F32)] * 6,
        args=(d_br_b, w_pool_upb, p, pool_w, pool_scale))[0]


def _win_bwd_norm(parts, w_int, dx2, x, g1, carry=None):
    t, d = x.shape
    tk = 512
    tt = _tile(t, 1024)
    bounds = []
    k0 = 0
    for part in parts:
        assert part.shape[1] % tk == 0
        bounds.append((k0, k0 + part.shape[1] // tk))
        k0 += part.shape[1] // tk
    nk = k0
    assert nk * tk == w_int.shape[0]
    np_ = len(parts)

    def body(*refs):
        p_refs = refs[:np_]
        w_ref, dx2_ref, x_ref, g_ref, gx_ref, dg_ref, acc = refs[np_:]
        i, kk = pl.program_id(0), pl.program_id(1)

        @pl.when(kk == 0)
        def _():
            acc[...] = jnp.zeros_like(acc)

        @pl.when((i == 0) & (kk == 0))
        def _():
            dg_ref[...] = jnp.zeros_like(dg_ref)

        for (lo, hi), p_ref in zip(bounds, p_refs):
            @pl.when((kk >= lo) & (kk < hi))
            def _(p_ref=p_ref):
                acc[...] += _dot_nn(p_ref[...], w_ref[...])

        @pl.when(kk == nk - 1)
        def _():
            def tail(rows):
                xhat, r = _rms_hat(x_ref[rows, :])
                dx, dg = _rms_bwd(acc[rows, :], xhat, r, g_ref[...])
                gx_ref[rows, :] = dx2_ref[rows, :] + dx
                dg_ref[...] += dg

            _row_chunks(tt, tail)

    def part_spec(lo, hi):
        return pl.BlockSpec((tt, tk), lambda i, kk: (i, jnp.clip(kk - lo, 0, hi - lo - 1)))

    row = pl.BlockSpec((tt, d), lambda i, kk: (i, 0))
    vec = pl.BlockSpec((1, d), lambda i, kk: (0, 0))
    return _call(
        body, name="win_bwd_norm", grid=(t // tt, nk),
        in_specs=[part_spec(lo, hi) for lo, hi in bounds]
        + [pl.BlockSpec((tk, d), lambda i, kk: (kk, 0)), row, row, vec],
        out_specs=[row, vec],
        out_shape=[jax.ShapeDtypeStruct((t, d), F32), jax.ShapeDtypeStruct((1, d), F32)],
        scratch_shapes=[pltpu.VMEM((tt, d), F32)],
        args=(*parts, w_int, dx2, x, g1), carry=carry)


def _adam_math(w, g, m, v):
    m = ADAM_B1 * m + (1.0 - ADAM_B1) * g
    v = ADAM_B2 * v + (1.0 - ADAM_B2) * (g * g)
    m_hat = m / (1.0 - ADAM_B1 ** ADAM_STEP)
    v_hat = v / (1.0 - ADAM_B2 ** ADAM_STEP)
    delta = -ADAM_LR * (m_hat / (jnp.sqrt(v_hat) + ADAM_EPS) + ADAM_WD * w)
    return delta, m, v


def _adamw_big(ws, gs, ms, vs):
    n = len(ws)
    nb = 4
    pair = [isinstance(g, tuple) for g in gs]

    def body(*refs):
        p = 0
        ins = []
        for a in range(n):
            k = 5 if pair[a] else 4
            ins.append(refs[p:p + k])
            p += k
        for a in range(n):
            g_out, d_ref, nm_ref, nv_ref = refs[p + 4 * a:p + 4 * a + 4]
            if pair[a]:
                w_ref, own_ref, recv_ref, m_ref, v_ref = ins[a]
                g = own_ref[...]
                for k in range(3):
                    g = g + recv_ref[k].astype(F32)
            else:
                w_ref, g_ref, m_ref, v_ref = ins[a]
                g = g_ref[...]
            dl, m, v = _adam_math(w_ref[...], g, m_ref[...], v_ref[...])
            g_out[...] = g
            d_ref[...] = dl
            nm_ref[...] = m
            nv_ref[...] = v

    in_specs, out_specs, out_shape, args = [], [], [], []
    for a, (w, g, m, v) in enumerate(zip(ws, gs, ms, vs)):
        rows, cols = w.shape
        blk = pl.BlockSpec((rows // nb, cols), lambda i: (i, 0))
        if pair[a]:
            in_specs += [blk, pl.BlockSpec((None, rows // nb, cols), lambda i: (0, i, 0)),
                         pl.BlockSpec((3, rows // nb, cols), lambda i: (0, i, 0)), blk, blk]
            args += [w, g[0], g[1], m, v]
        else:
            in_specs += [blk] * 4
            args += [w, g, m, v]
        out_specs += [blk] * 4
        out_shape += [jax.ShapeDtypeStruct(w.shape, F32)] * 4
    outs = _call(body, name="adamw_big", grid=(nb,), in_specs=in_specs, out_specs=out_specs,
                 out_shape=out_shape, args=args)[0]
    return [tuple(outs[4 * a:4 * a + 4]) for a in range(n)]


SMALL_ORDER = ("norm_mix_pre", "norm_mix_post", "norm_mlp_pre", "norm_mlp_post", "b_gate", "conv_w", "conv_b",
               "lru_w_a", "lru_b_a", "lru_w_x", "lru_b_x", "lru_lambda", "pool_w", "pool_scale")
VEC_ROW = dict(norm_mix_pre=0, norm_mix_post=1, norm_mlp_pre=2, norm_mlp_post=3, conv_b=6, lru_b_a=7,
               lru_b_x=8, lru_lambda=9)
ROW_B_GATE, ROW_POOL_SCALE, ROW_CONV_W, ROW_LOSS, N_VEC_ROWS = 4, 10, 11, 15, 16


def _adamw_small(vec_parts, g_pool, g_wa, g_wx, me, params):
    d = vec_parts.shape[2]
    names = SMALL_ORDER
    n = len(names)
    cw_cols = params["conv_w"][0].shape[2]

    def body(me_ref, vec_ref, vecc_ref, gp_ref, gwa_ref, gwx_ref, *refs):
        wmv = refs[:3 * n]
        loss_ref = refs[3 * n]
        outs = refs[3 * n + 1:3 * n + 1 + 4 * n]
        vs, vsc = refs[3 * n + 1 + 4 * n:]
        acc, accc = vec_ref[0], vecc_ref[0]
        for k in range(1, N_DEV):
            acc = acc + vec_ref[k]
            accc = accc + vecc_ref[k]
        vs[...] = acc
        vsc[...] = accc
        loss_ref[...] = vs[ROW_LOSS:ROW_LOSS + 1, 0:128]

        def upd(a, g, idx):
            w_ref, m_ref, v_ref = wmv[3 * a:3 * a + 3]
            g_ref, d_ref, nm_ref, nv_ref = outs[4 * a:4 * a + 4]
            dl, m, v = _adam_math(w_ref[idx], g, m_ref[idx], v_ref[idx])
            g_ref[idx] = g
            d_ref[idx] = dl
            nm_ref[idx] = m
            nv_ref[idx] = v

        for a, name in enumerate(names):
            if name in VEC_ROW:
                r = VEC_ROW[name]
                upd(a, vs[r:r + 1, :], (slice(None), slice(None)))
            elif name == "b_gate":
                for half in range(2):
                    r = ROW_B_GATE + half
                    upd(a, vs[r:r + 1, :], (slice(None), slice(half * d, (half + 1) * d)))
            elif name == "pool_scale":
                width = params[name][0].shape[1]
                upd(a, vs[ROW_POOL_SCALE:ROW_POOL_SCALE + 1, 0:width], (slice(None), slice(None)))
            elif name == "conv_w":
                upd(a, vsc[ROW_CONV_W:ROW_CONV_W + 4, :], (0,))
            elif name == "pool_w":
                upd(a, gp_ref[...], (Ellipsis,))
            elif name == "lru_w_a":
                upd(a, gwa_ref[...], (Ellipsis,))
            elif name == "lru_w_x":
                upd(a, gwx_ref[...], (Ellipsis,))
            else:
                raise ValueError(name)

    def whole(shape):
        nd = len(shape)
        return pl.BlockSpec(tuple(shape), lambda i, me_ref: (0,) * nd)

    in_specs = [
        whole(vec_parts.shape),
        pl.BlockSpec((N_DEV, N_VEC_ROWS, cw_cols), lambda i, me_ref: (0, 0, me_ref[0])),
        whole(g_pool.shape), whole(g_wa.shape), whole(g_wx.shape),
    ]
    args = [vec_parts, vec_parts, g_pool, g_wa, g_wx]
    out_specs = [whole((1, 128))]
    out_shape = [jax.ShapeDtypeStruct((1, 128), F32)]
    for name in names:
        for arr in params[name]:
            in_specs.append(whole(arr.shape))
            args.append(arr)
        shp = params[name][0].shape
        out_specs += [whole(shp)] * 4
        out_shape += [jax.ShapeDtypeStruct(shp, F32)] * 4
    grid_spec = pltpu.PrefetchScalarGridSpec(
        num_scalar_prefetch=1, grid=(1,), in_specs=in_specs, out_specs=out_specs,
        scratch_shapes=[pltpu.VMEM((N_VEC_ROWS, d), F32), pltpu.VMEM((N_VEC_ROWS, cw_cols), F32)])
    outs = pl.pallas_call(
        body, name="adamw_small", grid_spec=grid_spec, out_shape=out_shape,
        compiler_params=pltpu.CompilerParams(
            dimension_semantics=("arbitrary",), vmem_limit_bytes=V7X_VMEM_LIMIT_BYTES),
    )(me, *_in_hbm(args))
    return outs[0], {name: tuple(outs[1 + 4 * a:5 + 4 * a]) for a, name in enumerate(names)}


def _rs_sum(fulls, recvs, shard_ids, slot_ids, name):
    n = len(fulls)

    def body(sh_ref, sl_ref, *refs):
        s = pl.program_id(0)
        for a in range(n):
            full_ref, recv_ref = refs[2 * a], refs[2 * a + 1]
            own_ref, send_ref = refs[2 * n + 2 * a], refs[2 * n + 2 * a + 1]
            v = full_ref[...] + recv_ref[...].astype(F32)

            @pl.when(s == 0)
            def _(own_ref=own_ref, v=v):
                own_ref[...] = v

            @pl.when(s > 0)
            def _(send_ref=send_ref, v=v):
                send_ref[...] = v.astype(send_ref.dtype)

    in_specs, out_specs, out_shape, args = [], [], [], []
    for full, recv in zip(fulls, recvs):
        r, rest = recv.shape[1], tuple(recv.shape[2:])
        zeros = (0,) * len(rest)
        in_specs += [
            pl.BlockSpec((r,) + rest, lambda s, sh, sl, zeros=zeros: (sh[s],) + zeros),
            pl.BlockSpec((None, r) + rest, lambda s, sh, sl, zeros=zeros: (sl[s], 0) + zeros),
        ]
        out_specs += [
            pl.BlockSpec((None, r) + rest, lambda s, sh, sl, zeros=zeros: (0, 0) + zeros),
            pl.BlockSpec((None, r) + rest, lambda s, sh, sl, zeros=zeros: (jnp.maximum(s - 1, 0), 0) + zeros),
        ]
        out_shape += [jax.ShapeDtypeStruct((1, r) + rest, F32), jax.ShapeDtypeStruct((3, r) + rest, recv.dtype)]
        args += [full, recv]
    grid_spec = pltpu.PrefetchScalarGridSpec(
        num_scalar_prefetch=2, grid=(4,), in_specs=in_specs, out_specs=out_specs)
    outs = pl.pallas_call(
        body,
        name=name,
        grid_spec=grid_spec,
        out_shape=out_shape,
        compiler_params=pltpu.CompilerParams(
            dimension_semantics=("arbitrary",), vmem_limit_bytes=V7X_VMEM_LIMIT_BYTES),
    )(shard_ids, slot_ids, *_in_hbm(args))
    return [(outs[2 * a], outs[2 * a + 1]) for a in range(n)]


def _finals(pairs, name, carry=None):
    nb = 4
    n = len(pairs)

    def body(*refs):
        for a in range(n):
            own_ref, recv_ref = refs[2 * a], refs[2 * a + 1]
            acc = own_ref[...]
            for k in range(3):
                acc = acc + recv_ref[k].astype(F32)
            refs[2 * n + a][...] = acc

    in_specs, out_specs, out_shape, args = [], [], [], []
    for own, recv in pairs:
        _, rows, cols = own.shape
        in_specs += [pl.BlockSpec((None, rows // nb, cols), lambda i: (0, i, 0)),
                     pl.BlockSpec((3, rows // nb, cols), lambda i: (0, i, 0))]
        args += [own, recv]
        out_specs.append(pl.BlockSpec((rows // nb, cols), lambda i: (i, 0)))
        out_shape.append(jax.ShapeDtypeStruct((rows, cols), F32))
    return _call(body, name=name, grid=(nb,), in_specs=in_specs, out_specs=out_specs,
                 out_shape=out_shape, args=args, carry=carry)


def _rs_sums(fulls_f32, recv1, tag):
    x, y, c = _place()
    qs = jnp.stack([2 * x + y, 2 * (1 - x) + y, 2 * x + (1 - y), 2 * (1 - x) + (1 - y)]).astype(jnp.int32)
    shard_ids = 2 * qs + c
    return _rs_sum(fulls_f32, recv1, shard_ids, qs, "rs_sum_" + tag)


def _rs_level1(fulls_f32, fulls_send, tag):
    recv1 = _run_plan(_rs_sibling_plan(fulls_send), "rs_sibling_" + tag)
    return _rs_sums(fulls_f32, recv1, tag)


def _rows(g):
    return g.reshape(g.shape[0] * g.shape[1], g.shape[2])


def kernel(x, norm_mix_pre, norm_mix_post, norm_mlp_pre, norm_mlp_post, w_in, b_gate, conv_w, conv_b, lru_w_a, lru_b_a, lru_w_x, lru_b_x, lru_lambda, pool_w, pool_scale, w_lru_up, w_pool_up, w_o, w_ff1, w_ff2, loss_target, m_norm_mix_pre, m_norm_mix_post, m_norm_mlp_pre, m_norm_mlp_post, m_w_in, m_b_gate, m_conv_w, m_conv_b, m_lru_w_a, m_lru_b_a, m_lru_w_x, m_lru_b_x, m_lru_lambda, m_pool_w, m_pool_scale, m_w_lru_up, m_w_pool_up, m_w_o, m_w_ff1, m_w_ff2, v_norm_mix_pre, v_norm_mix_post, v_norm_mlp_pre, v_norm_mlp_post, v_w_in, v_b_gate, v_conv_w, v_conv_b, v_lru_w_a, v_lru_b_a, v_lru_w_x, v_lru_b_x, v_lru_lambda, v_pool_w, v_pool_scale, v_w_lru_up, v_w_pool_up, v_w_o, v_w_ff1, v_w_ff2):
    t, d = x.shape[1], x.shape[2]
    d_rnn = conv_b.shape[1]
    d_pool = pool_scale.shape[1]
    per = LRU_CB // LRU_HEAD_DIM
    xi, yi, ci = _place()
    me = 4 * xi + 2 * yi + ci

    x2d = x[0]
    tgt = loss_target[0]

    s_in = w_in[0].T.astype(BF16)
    s_lu = w_lru_up[0].astype(BF16)
    s_pu = w_pool_up[0].astype(BF16)
    s_o = w_o[0].astype(BF16)
    s_f1 = w_ff1[0].astype(BF16)
    s_f2 = w_ff2[0].astype(BF16)
    s_cw = jnp.pad(conv_w[0], ((0, 4), (0, 0)))

    g_in, g_cw = _run_plan(_ag_plan([s_in, s_cw]), "ag_w_in")
    w_int = _rows(g_in)
    conv_w_full = jnp.transpose(g_cw[:, :4, :], (1, 0, 2)).reshape(4, d_rnn)

    wa_bd, wx_bd = lru_w_a[0], lru_w_x[0]
    pw = pool_w[0]
    pw_bf = pw.astype(BF16)

    pool_block = (2 * d_rnn) // d_pool
    ga_block = (2 * d_rnn + d_pool) // 512
    gb_block = ga_block + d // 512
    g_block = d_rnn // 512

    r_f1, r_f2 = s_f1.shape[0], s_f2.shape[0]
    f1_cut = r_f1 // 4
    f2_cut = (3 * r_f2) // 8
    plan = _join([_ag_plan([s_lu, s_pu]), _ag_plan([s_f1], pieces=[(0, f1_cut)])])
    (proj, h1), got = _norm_proj(x2d, norm_mix_pre, w_int, carry=plan)
    (g_lu, g_pu), (g_f1,) = plan.split(got)
    plan = _join([_ag_plan([s_f1], pieces=[(f1_cut, r_f1 - f1_cut)], bufs=[g_f1]), _ag_plan([s_o])])
    (y_lru, h, lru_saved), got = _lru_fwd(
        proj, conv_w_full, conv_b, wa_bd, lru_b_a, wx_bd, lru_b_x, lru_lambda, carry=plan)
    (g_f1,), (g_o,) = plan.split(got)
    w_lu, w_og = _rows(g_lu), _rows(g_o)
    y_pool, p = _pool_fwd(proj, pw_bf, pool_scale, pool_block)
    (br_a, br_b, mix), (g_f2,) = _branch_mix(
        y_lru, y_pool, w_lu, g_pu, proj, b_gate, ga_block, gb_block,
        carry=_ag_plan([s_f2], pieces=[(0, f2_cut)]))
    (m, x2, h3), _ = _wo_norm(mix, w_og, x2d, norm_mix_post, norm_mlp_pre)
    (rf,), (g_f2,) = _ff1(
        h3, g_f1, carry=_ag_plan([s_f2], pieces=[(f2_cut, r_f2 - f2_cut)], bufs=[g_f2]))
    w_f2 = _rows(g_f2)
    dy, df, dg4, loss_part = _ff2_loss(rf, w_f2, x2, norm_mlp_post, tgt)

    (gw_ff2_32, gw_ff2_16), _ = _wgrad(rf, df, "wgrad_ff2", square_a=True)
    (d_f1,), r1_ff2 = _ff2_bwd(df, w_f2, rf, carry=_rs_sibling_plan([gw_ff2_16]))
    ((own_ff2, send_ff2),) = _rs_sums([gw_ff2_32], r1_ff2, "ff2")
    cut2 = (5 * send_ff2.shape[1]) // 16
    (gw_ff1_32, gw_ff1_16), (r2_ff2,) = _wgrad_cols(
        h3, d_f1, s_f1.shape[1], s_f1.shape[1], "wgrad_ff1",
        carry=_rs_chips_plan([send_ff2], pieces=[(0, cut2)]))
    plan = _join([_rs_chips_plan([send_ff2], pieces=[(cut2, send_ff2.shape[1] - cut2)], bufs=[r2_ff2]),
                  _rs_sibling_plan([gw_ff1_16])])
    (dx2, dm, dg3, dg2), got = _ff1_bwd_norms(d_f1, g_f1, dy, x2, norm_mlp_pre, m, norm_mix_post, carry=plan)
    (r2_ff2,), r1_ff1 = plan.split(got)
    ((own_ff1, send_ff1),) = _rs_sums([gw_ff1_32], r1_ff1, "ff1")
    own_ff1, send_ff1 = own_ff1.reshape((1,) + s_f1.shape), send_ff1.reshape((3,) + s_f1.shape)
    cut = send_ff1.shape[1] // 4
    (gw_o_32, gw_o_16), _ = _wgrad(mix, dm, "wgrad_o")
    (d_br_a, d_br_b, p_ga, p_gb, dbg_a, dbg_b), (r2_ff1,) = _wo_bwd_mix(
        dm, w_og, br_a, br_b, proj, b_gate, ga_block, gb_block,
        carry=_rs_chips_plan([send_ff1], pieces=[(0, cut)]))
    (gw_lu_32, gw_lu_16), _ = _wgrad(y_lru, d_br_a, "wgrad_lru_up")
    (gw_pu_32, gw_pu_16), _ = _wgrad_cols(y_pool, d_br_b, s_pu.shape[1], d, "wgrad_pool_up")
    (dh, p_g), r1_mid = _lru_up_bwd(
        d_br_a, w_lu, proj, h, g_block,
        carry=_rs_sibling_plan([gw_o_16, gw_lu_16, gw_pu_16]))
    mid = _rs_sums([gw_o_32, gw_lu_32, gw_pu_32], r1_mid, "mid")
    plan = _join([_rs_chips_plan([send_ff1], pieces=[(cut, send_ff1.shape[1] - cut)], bufs=[r2_ff1]),
                  _rs_chips_plan([mid[0][1]])])
    (p_x, dwa, db_a, dwx, db_x, dlam, dconv_w, dconv_b), got = _lru_bwd(
        dh, h, lru_saved, proj, conv_w_full, wa_bd, wx_bd, lru_lambda, carry=plan)
    (r2_ff1,), (r2_o,) = plan.split(got)
    p_p, dpool_w, dpool_scale = _pool_bwd(d_br_b, g_pu, p, pw, pool_scale)
    parts = [p_x, p_g, p_p, p_ga, p_gb]
    gw_in, (r2_lu, r2_pu) = _wgrad_parts(
        parts, h1, "wgrad_in", carry=_rs_chips_plan([mid[1][1], mid[2][1]]))
    r2_mid = [r2_o, r2_lu, r2_pu]
    tail = _rs_level1([gw_in[0], dpool_w.reshape(N_DEV, -1, POOL_GROUP_DIM), dwa, dwx],
                      [gw_in[1], dpool_w.reshape(N_DEV, -1, POOL_GROUP_DIM), dwa, dwx], "in")
    (grad_x, dg1), r2_tail = _win_bwd_norm(parts, w_int, dx2, x2d, norm_mix_pre,
                                           carry=_rs_chips_plan([s for _, s in tail]))

    def flat2(a):
        return a.reshape(a.shape[0], -1, a.shape[-1])

    fin_small, _ = _finals([
        (flat2(tail[1][0]), flat2(r2_tail[1])), (flat2(tail[2][0]), flat2(r2_tail[2])),
        (flat2(tail[3][0]), flat2(r2_tail[3])),
    ], "rs_finals_small")

    def pad_row(a):
        return jnp.pad(a, ((0, 0), (0, d - a.shape[1])))

    vecs = jnp.concatenate([dg1, dg2, dg3, dg4, dbg_a, dbg_b, dconv_b, db_a, db_x, dlam,
                            pad_row(dpool_scale), dconv_w, pad_row(loss_part)], axis=0)
    assert vecs.shape[0] == N_VEC_ROWS
    vec_parts, g_pool, g_wa, g_wx = _run_plan(_ag_plan([vecs] + fin_small), "ag_tail")

    big_names = ["w_in", "w_lru_up", "w_pool_up", "w_o", "w_ff1", "w_ff2"]
    big_w = [w_in[0].T, w_lru_up[0], w_pool_up[0], w_o[0], w_ff1[0], w_ff2[0]]
    big_g = [(tail[0][0], r2_tail[0]), (mid[1][0], r2_mid[1]),
             (mid[2][0].reshape((1,) + s_pu.shape), r2_mid[2].reshape((3,) + s_pu.shape)),
             (mid[0][0], r2_mid[0]), (own_ff1, r2_ff1), (own_ff2, r2_ff2)]
    big_m = [m_w_in[0].T, m_w_lru_up[0], m_w_pool_up[0], m_w_o[0], m_w_ff1[0], m_w_ff2[0]]
    big_v = [v_w_in[0].T, v_w_lru_up[0], v_w_pool_up[0], v_w_o[0], v_w_ff1[0], v_w_ff2[0]]
    big_out = _adamw_big(big_w, big_g, big_m, big_v)
    big_out[0] = tuple(o.T for o in big_out[0])

    small = dict(
        norm_mix_pre=(norm_mix_pre, m_norm_mix_pre, v_norm_mix_pre),
        norm_mix_post=(norm_mix_post, m_norm_mix_post, v_norm_mix_post),
        norm_mlp_pre=(norm_mlp_pre, m_norm_mlp_pre, v_norm_mlp_pre),
        norm_mlp_post=(norm_mlp_post, m_norm_mlp_post, v_norm_mlp_post),
        b_gate=(b_gate, m_b_gate, v_b_gate), conv_w=(conv_w, m_conv_w, v_conv_w),
        conv_b=(conv_b, m_conv_b, v_conv_b), lru_w_a=(lru_w_a, m_lru_w_a, v_lru_w_a),
        lru_b_a=(lru_b_a, m_lru_b_a, v_lru_b_a), lru_w_x=(lru_w_x, m_lru_w_x, v_lru_w_x),
        lru_b_x=(lru_b_x, m_lru_b_x, v_lru_b_x), lru_lambda=(lru_lambda, m_lru_lambda, v_lru_lambda),
        pool_w=(pool_w, m_pool_w, v_pool_w), pool_scale=(pool_scale, m_pool_scale, v_pool_scale))
    loss_row, small_out = _adamw_small(
        vec_parts, g_pool.reshape(pool_w.shape), g_wa.reshape(lru_w_a.shape), g_wx.reshape(lru_w_x.shape),
        jnp.reshape(me, (1,)).astype(jnp.int32), small)
    grads = {n: o[0] for n, o in small_out.items()}
    delta = {n: o[1] for n, o in small_out.items()}
    new_m = {n: o[2] for n, o in small_out.items()}
    new_v = {n: o[3] for n, o in small_out.items()}

    for name, (g, dl, nm, nv) in zip(big_names, big_out):
        grads[name], delta[name], new_m[name], new_v[name] = g[None], dl[None], nm[None], nv[None]

    loss = loss_row[0, 0]
    order = ["norm_mix_pre", "norm_mix_post", "norm_mlp_pre", "norm_mlp_post", "w_in", "b_gate", "conv_w",
             "conv_b", "lru_w_a", "lru_b_a", "lru_w_x", "lru_b_x", "lru_lambda", "pool_w", "pool_scale",
             "w_lru_up", "w_pool_up", "w_o", "w_ff1", "w_ff2"]
    return (loss, grad_x[None], *[grads[n] for n in order], *[delta[n] for n in order],
            *[new_m[n] for n in order], *[new_v[n] for n in order])
```

```python
import functools
import math
import operator
import types

import jax
import jax.numpy as jnp
from jax import lax
from jax.experimental import pallas as pl
from jax.experimental.pallas import tpu as pltpu

F32 = jnp.float32
BF16 = jnp.bfloat16
NORM_EPS = 1e-6
LRU_C = 8.0
N_LRU_HEADS = 16
LRU_HEAD_DIM = 64
POOL_WINDOWS = (2, 4, 8, 16)
POOL_GROUP_DIM = 128
ADAM_LR = 0.001
ADAM_B1 = 0.9
ADAM_B2 = 0.999
ADAM_EPS = 1e-08
ADAM_WD = 0.01
ADAM_STEP = 10
N_DEV = 8
V7X_VMEM_LIMIT_BYTES = 56 * 1024 * 1024
LRU_CB = 256
MESH = pl.DeviceIdType.MESH
ANY = pl.BlockSpec(memory_space=pl.ANY)


def _tile(n, pref):
    t = min(n, pref)
    assert n % t == 0, (n, pref)
    return t


def _dot_nn(a, b):
    return lax.dot_general(a, b, (((1,), (0,)), ((), ())), preferred_element_type=F32)


def _dot_nt(a, b):
    return lax.dot_general(a, b, (((1,), (1,)), ((), ())), preferred_element_type=F32)


def _dot_tn(a, b):
    return lax.dot_general(a, b, (((0,), (0,)), ((), ())), preferred_element_type=F32)


def _row_chunks(n_rows, fn, chunk=256):
    chunk = min(chunk, n_rows)
    assert n_rows % chunk == 0

    def step(r, carry):
        fn(pl.ds(pl.multiple_of(r * chunk, chunk), chunk))
        return carry

    lax.fori_loop(0, n_rows // chunk, step, 0)


def _late_copies(i, tt, pairs, sems):
    rows = pl.ds(pl.multiple_of(i * tt, tt), tt)
    return [pltpu.make_async_copy(hbm.at[rows], buf, sems.at[j]) for j, (hbm, buf) in enumerate(pairs)]


def _sig(x):
    return 1.0 / (1.0 + jnp.exp(-x))


def _rms_hat(x):
    r = lax.rsqrt(jnp.mean(x * x, axis=-1, keepdims=True) + NORM_EPS)
    return x * r, r


def _rms_bwd(dn, xhat, r, g):
    q = dn * g
    dx = r * (q - xhat * jnp.mean(q * xhat, axis=-1, keepdims=True))
    dg = jnp.sum(dn * xhat, axis=0, keepdims=True)
    return dx, dg


_GELU_K = math.sqrt(2.0 / math.pi)
_GELU_C = 0.044715


def _gelu_and_grad(g):
    t = jnp.tanh(_GELU_K * (g + _GELU_C * g * g * g))
    val = 0.5 * g * (1.0 + t)
    grad = 0.5 * (1.0 + t) + 0.5 * g * (1.0 - t * t) * (_GELU_K * (1.0 + 3.0 * _GELU_C * g * g))
    return val, grad


def _softplus_neg(lam):
    z = -lam
    e = jnp.exp(-jnp.abs(z))
    u = 1.0 + e
    d = u - 1.0
    l1p = jnp.where(d == 0.0, e, jnp.log(u) * (e / jnp.where(d == 0.0, 1.0, d)))
    return jnp.maximum(z, 0.0) + l1p


def _lru_gates(xc, wa, ba, wx, bx, lam):
    xcb = xc.astype(BF16)
    r = _sig(_dot_nn(xcb, wa) + ba)
    i = _sig(_dot_nn(xcb, wx) + bx)
    sp = _softplus_neg(lam)
    log_a = (-LRU_C) * r * sp
    a = jnp.exp(log_a)
    mult = jnp.sqrt(-jnp.tanh(log_a) * (1.0 + a * a))
    return xcb, r, i, sp, log_a, a, mult


def _place():
    return lax.axis_index("x"), lax.axis_index("y"), lax.axis_index("c")


def _ag_plan(shards, pieces=None, bufs=None):
    na = len(shards)
    n_kinds = 7

    def parts(ins, outs, sems):
        send_sems, recv_sems, local_sems = sems
        x, y, c = _place()
        me, sibling = (x, y, c), (x, y, 1 - c)
        x_nb, y_nb, diag = (1 - x, y), (x, 1 - y), (1 - x, 1 - y)
        relay_src = (c * (1 - x) + (1 - c) * x, c * y + (1 - c) * (1 - y))
        relay_dst = (c * x + (1 - c) * (1 - x), c * (1 - y) + (1 - c) * y)

        def own(a):
            return ins[a] if pieces is None else ins[a].at[pl.ds(*pieces[a])]

        def slot(a, px, py, pc):
            idx = 4 * px + 2 * py + pc
            return outs[a].at[idx] if pieces is None else outs[a].at[idx, pl.ds(*pieces[a])]

        def copy(a, k, block, to, src=None):
            return pltpu.make_async_remote_copy(
                src_ref=slot(a, *block) if src is None else src,
                dst_ref=slot(a, *block),
                send_sem=send_sems.at[a * n_kinds + k],
                recv_sem=recv_sems.at[a * n_kinds + k],
                device_id=to,
                device_id_type=MESH,
            )

        mine = [pltpu.make_async_copy(own(a), slot(a, *me), local_sems.at[a]) for a in range(na)]
        first, second, third = [], [], []
        for a in range(na):
            first += [copy(a, 0, me, sibling, src=own(a)), copy(a, 1, me, (*x_nb, c), src=own(a)),
                      copy(a, 2, me, (*y_nb, c), src=own(a))]
            second += [copy(a, 3, (*relay_src, c), (*relay_dst, c)), copy(a, 4, (*x_nb, c), sibling),
                       copy(a, 5, (*y_nb, c), sibling)]
            third.append(copy(a, 6, (*diag, c), sibling))
        return sibling, c, x_nb, y_nb, diag, copy, mine, first, second, third

    def start(ins, outs, sems):
        _, _, _, _, _, _, mine, first, _, _ = parts(ins, outs, sems)
        for cp in mine + first:
            cp.start()

    def middle(ins, outs, sems):
        _, c, x_nb, y_nb, _, copy, _, _, second, _ = parts(ins, outs, sems)
        for a in range(na):
            copy(a, 1, (*x_nb, c), (*x_nb, c)).wait_recv()
            copy(a, 2, (*y_nb, c), (*y_nb, c)).wait_recv()
        for cp in second:
            cp.start()

    def finish(ins, outs, sems):
        sibling, c, x_nb, y_nb, diag, copy, mine, first, second, third = parts(ins, outs, sems)
        for a in range(na):
            copy(a, 3, (*diag, c), (*diag, c)).wait_recv()
            third[a].start()
        for a in range(na):
            copy(a, 0, sibling, sibling).wait_recv()
            copy(a, 4, (*x_nb, 1 - c), sibling).wait_recv()
            copy(a, 5, (*y_nb, 1 - c), sibling).wait_recv()
            copy(a, 6, (*diag, 1 - c), sibling).wait_recv()
        for cp in first + second + third:
            cp.wait_send()
        for cp in mine:
            cp.wait()

    return types.SimpleNamespace(
        ins=list(shards) + list(bufs or []),
        out_shapes=[jax.ShapeDtypeStruct((N_DEV,) + s.shape, s.dtype) for s in shards],
        sems=[pltpu.SemaphoreType.DMA((n_kinds * na,)), pltpu.SemaphoreType.DMA((n_kinds * na,)),
              pltpu.SemaphoreType.DMA((na,))],
        aliases=[(na + a, a) for a in range(na)] if bufs else [],
        peers=frozenset({"sibling", "neighbours"}), start=start, middle=middle, finish=finish)


def _rs_sibling_plan(fulls):
    na = len(fulls)
    rs = [f.shape[0] // N_DEV for f in fulls]

    def copies(ins, outs, sems):
        send_sems, recv_sems = sems
        x, y, c = _place()
        out = []
        for a in range(na):
            for q in range(4):
                shard = 2 * q + (1 - c)
                out.append(pltpu.make_async_remote_copy(
                    src_ref=ins[a].at[pl.ds(shard * rs[a], rs[a])],
                    dst_ref=outs[a].at[q],
                    send_sem=send_sems.at[a * 4 + q],
                    recv_sem=recv_sems.at[a * 4 + q],
                    device_id=(x, y, 1 - c),
                    device_id_type=MESH,
                ))
        return out

    def start(ins, outs, sems):
        for cp in copies(ins, outs, sems):
            cp.start()

    def finish(ins, outs, sems):
        for cp in copies(ins, outs, sems):
            cp.wait()

    return types.SimpleNamespace(
        ins=list(fulls),
        out_shapes=[jax.ShapeDtypeStruct((4, r) + f.shape[1:], f.dtype) for r, f in zip(rs, fulls)],
        sems=[pltpu.SemaphoreType.DMA((4 * na,)), pltpu.SemaphoreType.DMA((4 * na,))],
        peers=frozenset({"sibling"}), start=start, finish=finish)


def _rs_chips_plan(sends, pieces=None, bufs=None):
    na = len(sends)

    def copies(ins, outs, sems):
        send_sems, recv_sems = sems
        x, y, c = _place()
        chips = [(1 - x, y), (x, 1 - y), (1 - x, 1 - y)]
        out = []
        for a in range(na):
            for k, chip in enumerate(chips):
                rows = (k,) if pieces is None else (k, pl.ds(*pieces[a]))
                out.append(pltpu.make_async_remote_copy(
                    src_ref=ins[a].at[rows],
                    dst_ref=outs[a].at[rows],
                    send_sem=send_sems.at[a * 3 + k],
                    recv_sem=recv_sems.at[a * 3 + k],
                    device_id=(*chip, c),
                    device_id_type=MESH,
                ))
        return out

    def start(ins, outs, sems):
        for cp in copies(ins, outs, sems):
            cp.start()

    def finish(ins, outs, sems):
        for cp in copies(ins, outs, sems):
            cp.wait()

    return types.SimpleNamespace(
        ins=list(sends) + list(bufs or []),
        out_shapes=[jax.ShapeDtypeStruct(s.shape, s.dtype) for s in sends],
        sems=[pltpu.SemaphoreType.DMA((3 * na,)), pltpu.SemaphoreType.DMA((3 * na,))],
        aliases=[(na + a, a) for a in range(na)] if bufs else [],
        peers=frozenset({"chips"}), start=start, finish=finish)


def _join(plans):
    ins, outs, sems, aliases, offs = [], [], [], [], []
    for p in plans:
        offs.append((len(ins), len(outs), len(sems)))
        aliases += [(len(ins) + ci, len(outs) + co) for ci, co in getattr(p, "aliases", [])]
        ins += p.ins
        outs += p.out_shapes
        sems += p.sems

    def cut(p, off, i, o, s):
        return (i[off[0]:off[0] + len(p.ins)], o[off[1]:off[1] + len(p.out_shapes)],
                s[off[2]:off[2] + len(p.sems)])

    def start(i, o, s):
        for p, off in zip(plans, offs):
            p.start(*cut(p, off, i, o, s))

    def middle(i, o, s):
        for p, off in zip(plans, offs):
            if getattr(p, "middle", None) is not None:
                p.middle(*cut(p, off, i, o, s))

    def finish(i, o, s):
        for p, off in zip(plans, offs):
            p.finish(*cut(p, off, i, o, s))

    def split(results):
        return [list(results[off[1]:off[1] + len(p.out_shapes)]) for p, off in zip(plans, offs)]

    return types.SimpleNamespace(ins=ins, out_shapes=outs, sems=sems, aliases=aliases,
                                 peers=frozenset().union(*[p.peers for p in plans]),
                                 start=start, middle=middle, finish=finish, split=split)


COLLECTIVE_ID = {frozenset({"sibling"}): 0, frozenset({"chips"}): 1, frozenset({"sibling", "chips"}): 2,
                 frozenset({"sibling", "neighbours"}): 3}


def _handshake(peers):
    x, y, c = _place()
    devs = []
    if "sibling" in peers:
        devs.append((x, y, 1 - c))
    if "neighbours" in peers:
        devs += [(1 - x, y, c), (x, 1 - y, c)]
    if "chips" in peers:
        assert "neighbours" not in peers
        devs += [(1 - x, y, c), (x, 1 - y, c), (1 - x, 1 - y, c)]
    barrier = pltpu.get_barrier_semaphore()
    for dev in devs:
        pl.semaphore_signal(barrier, inc=1, device_id=dev, device_id_type=MESH)
    pl.semaphore_wait(barrier, len(devs))


def _in_hbm(args):
    return [pltpu.with_memory_space_constraint(a, pltpu.HBM) for a in args]


def _run_plan(plan, name):
    n_in, n_out = len(plan.ins), len(plan.out_shapes)

    def body(*refs):
        ins, outs, sems = refs[:n_in], refs[n_in:n_in + n_out], refs[n_in + n_out:]
        _handshake(plan.peers)
        plan.start(ins, outs, sems)
        if getattr(plan, "middle", None) is not None:
            plan.middle(ins, outs, sems)
        plan.finish(ins, outs, sems)

    return pl.pallas_call(
        body,
        name=name,
        in_specs=[ANY] * n_in,
        out_specs=[ANY] * n_out,
        out_shape=plan.out_shapes,
        scratch_shapes=plan.sems,
        input_output_aliases=dict(getattr(plan, "aliases", [])),
        compiler_params=pltpu.CompilerParams(collective_id=COLLECTIVE_ID[plan.peers]),
    )(*_in_hbm(plan.ins))


def _call(body, *, name, grid, in_specs, out_specs, out_shape, args, scratch_shapes=(), aliases=None,
          carry=None):
    n_in, n_out, n_scr = len(in_specs), len(out_shape), len(scratch_shapes)
    params = pltpu.CompilerParams(
        dimension_semantics=("arbitrary",) * len(grid), vmem_limit_bytes=V7X_VMEM_LIMIT_BYTES)
    if carry is None:
        outs = pl.pallas_call(
            body, name=name, grid=grid, in_specs=list(in_specs), out_specs=list(out_specs),
            out_shape=list(out_shape), scratch_shapes=list(scratch_shapes),
            input_output_aliases=aliases or {}, compiler_params=params)(*_in_hbm(args))
        return list(outs), []
    c_in, c_out = len(carry.ins), len(carry.out_shapes)

    def full(*refs):
        p = 0
        ins = refs[p:p + n_in]
        p += n_in
        cins = refs[p:p + c_in]
        p += c_in
        outs = refs[p:p + n_out]
        p += n_out
        couts = refs[p:p + c_out]
        p += c_out
        scr = refs[p:p + n_scr]
        csems = refs[p + n_scr:]
        ids = [pl.program_id(a) for a in range(len(grid))]
        first = functools.reduce(operator.and_, [i == 0 for i in ids])
        last = functools.reduce(operator.and_, [i == g - 1 for i, g in zip(ids, grid)])

        @pl.when(first)
        def _():
            _handshake(carry.peers)
            carry.start(cins, couts, csems)

        if getattr(carry, "middle", None) is not None:
            n_steps = math.prod(grid)
            flat = functools.reduce(lambda acc, ig: acc * ig[1] + ig[0], zip(ids, grid), 0)

            @pl.when(flat == (2 * n_steps) // 3)
            def _():
                carry.middle(cins, couts, csems)

        body(*ins, *outs, *scr)

        @pl.when(last)
        def _():
            carry.finish(cins, couts, csems)

    all_aliases = dict(aliases or {})
    all_aliases.update({n_in + ci: n_out + co for ci, co in getattr(carry, "aliases", [])})
    params = pltpu.CompilerParams(
        dimension_semantics=("arbitrary",) * len(grid), vmem_limit_bytes=V7X_VMEM_LIMIT_BYTES,
        collective_id=COLLECTIVE_ID[carry.peers])
    outs = pl.pallas_call(
        full, name=name, grid=grid,
        in_specs=list(in_specs) + [ANY] * c_in,
        out_specs=list(out_specs) + [ANY] * c_out,
        out_shape=list(out_shape) + list(carry.out_shapes),
        scratch_shapes=list(scratch_shapes) + list(carry.sems),
        input_output_aliases=all_aliases, compiler_params=params)(*_in_hbm(args), *_in_hbm(carry.ins))
    return list(outs[:n_out]), list(outs[n_out:])


def _norm_proj(x, g1, w_int, carry=None):
    t, d = x.shape
    n = w_int.shape[0]
    tt, tn = _tile(t, 2048), _tile(n, 512)

    def body(x_ref, g_ref, w_ref, proj_ref, h1_ref, h1_s):
        @pl.when(pl.program_id(1) == 0)
        def _():
            def norm_rows(rows):
                xhat, _ = _rms_hat(x_ref[rows, :])
                h = (xhat * g_ref[...]).astype(BF16)
                h1_s[rows, :] = h
                h1_ref[rows, :] = h

            _row_chunks(tt, norm_rows)

        proj_ref[...] = _dot_nt(h1_s[...], w_ref[...]).astype(BF16)

    return _call(
        body, name="norm_proj", grid=(t // tt, n // tn),
        in_specs=[
            pl.BlockSpec((tt, d), lambda i, j: (i, 0)),
            pl.BlockSpec((1, d), lambda i, j: (0, 0)),
            pl.BlockSpec((tn, d), lambda i, j: (j, 0)),
        ],
        out_specs=[
            pl.BlockSpec((tt, tn), lambda i, j: (i, j)),
            pl.BlockSpec((tt, d), lambda i, j: (i, 0)),
        ],
        out_shape=[jax.ShapeDtypeStruct((t, n), BF16), jax.ShapeDtypeStruct((t, d), BF16)],
        scratch_shapes=[pltpu.VMEM((tt, d), BF16)],
        args=(x, g1, w_int), carry=carry)


def _scan_rows(av, bv, reverse):
    tc = av.shape[0]
    row = lax.broadcasted_iota(jnp.int32, av.shape, 0)
    s = 1
    while s < tc:
        if s < 8:
            keep = (row < tc - s) if reverse else (row >= s)
            shift = (tc - s) if reverse else s
            a_sh = jnp.where(keep, pltpu.roll(av, shift, 0), 1.0)
            b_sh = jnp.where(keep, pltpu.roll(bv, shift, 0), 0.0)
            bv = av * b_sh + bv
            av = av * a_sh
        elif reverse:
            bv = jnp.concatenate([av[:tc - s] * bv[s:] + bv[:tc - s], bv[tc - s:]], axis=0)
            av = jnp.concatenate([av[:tc - s] * av[s:], av[tc - s:]], axis=0)
        else:
            bv = jnp.concatenate([bv[:s], av[s:] * bv[:tc - s] + bv[s:]], axis=0)
            av = jnp.concatenate([av[:s], av[s:] * av[:tc - s]], axis=0)
        s *= 2
    return av, bv


N_LRU_SAVED = 5


def _fill_block_diag(w_ref, bd_ref):
    bd_ref[...] = jnp.zeros_like(bd_ref)
    hd = LRU_HEAD_DIM
    for k in range(w_ref.shape[0]):
        bd_ref[k * hd:(k + 1) * hd, k * hd:(k + 1) * hd] = w_ref[k].astype(BF16)


def _lru_fwd(proj, conv_w, conv_b, w_a, b_a, w_x, b_x, lam, carry=None):
    t = proj.shape[0]
    dr = conv_b.shape[1]
    cb = LRU_CB
    tc = _tile(t, 256)
    ncb, ntc = dr // cb, t // tc

    def body(xp_ref, g_ref, cw_ref, cb_ref, wa_ref, ba_ref, wx_ref, bx_ref, lam_ref,
             y_ref, h_ref, saved_ref, prevx_s, hlast_s, wa_s, wx_s):
        c = pl.program_id(1)

        @pl.when(c == 0)
        def _():
            prevx_s[...] = jnp.zeros_like(prevx_s)
            hlast_s[...] = jnp.zeros_like(hlast_s)
            _fill_block_diag(wa_ref, wa_s)
            _fill_block_diag(wx_ref, wx_s)

        x = xp_ref[...].astype(F32)
        prev = prevx_s[...]
        row = lax.broadcasted_iota(jnp.int32, x.shape, 0)

        def sh(j):
            return jnp.where(row >= j, pltpu.roll(x, j, 0), pltpu.roll(prev, j, 0))

        xc = (cb_ref[...] + cw_ref[0:1, :] * sh(3) + cw_ref[1:2, :] * sh(2)
              + cw_ref[2:3, :] * sh(1) + cw_ref[3:4, :] * x)
        prevx_s[...] = x
        _, r, i, _, _, a, mult = _lru_gates(xc, wa_s[...], ba_ref[...], wx_s[...], bx_ref[...],
                                            lam_ref[...])
        for k, val in enumerate((xc, r, i, a, mult)):
            saved_ref[:, k * cb:(k + 1) * cb] = val
        av, bv = _scan_rows(a, mult * (i * xc), reverse=False)
        h = av * hlast_s[...] + bv
        h_ref[...] = h
        hlast_s[...] = h_ref[tc - 1:tc, :]
        gel, _ = _gelu_and_grad(g_ref[...].astype(F32))
        y_ref[...] = (h * gel).astype(BF16)

    vec = pl.BlockSpec((1, cb), lambda j, c: (0, j))
    blk = pl.BlockSpec((tc, cb), lambda j, c: (c, j))
    mat = pl.BlockSpec((cb // LRU_HEAD_DIM, LRU_HEAD_DIM, LRU_HEAD_DIM), lambda j, c: (j, 0, 0))
    return _call(
        body, name="lru_fwd", grid=(ncb, ntc),
        in_specs=[
            blk,
            pl.BlockSpec((tc, cb), lambda j, c: (c, ncb + j)),
            pl.BlockSpec((4, cb), lambda j, c: (0, j)),
            vec, mat, vec, mat, vec, vec,
        ],
        out_specs=[blk, blk, pl.BlockSpec((tc, N_LRU_SAVED * cb), lambda j, c: (c, j))],
        out_shape=[jax.ShapeDtypeStruct((t, dr), BF16), jax.ShapeDtypeStruct((t, dr), F32),
                   jax.ShapeDtypeStruct((t, N_LRU_SAVED * dr), F32)],
        scratch_shapes=[pltpu.VMEM((tc, cb), F32), pltpu.VMEM((1, cb), F32),
                        pltpu.VMEM((cb, cb), BF16), pltpu.VMEM((cb, cb), BF16)],
        args=(proj, proj, conv_w, conv_b, w_a, b_a, w_x, b_x, lam), carry=carry)


def _pool_select(col, vals):
    out = vals[3]
    for g in (2, 1, 0):
        out = jnp.where(col < (g + 1) * POOL_GROUP_DIM, vals[g], out)
    return out


def _pool_fwd(proj, pool_w, pool_scale, col_block):
    t = proj.shape[0]
    dp = pool_scale.shape[1]
    tc = _tile(t, 256)
    ntc = t // tc

    def body(x_ref, w_ref, sc_ref, y_ref, p_ref, px, p2, p4, p8):
        c = pl.program_id(0)

        @pl.when(c == 0)
        def _():
            for s in (px, p2, p4, p8):
                s[...] = jnp.zeros_like(s)

        x = x_ref[...].astype(F32)
        row = lax.broadcasted_iota(jnp.int32, x.shape, 0)
        col = lax.broadcasted_iota(jnp.int32, x.shape, 1)

        def sh(v, pv, j):
            return jnp.where(row >= j, pltpu.roll(v, j, 0), pltpu.roll(pv[...], j, 0))

        s2 = x + sh(x, px, 1)
        s4 = s2 + sh(s2, p2, 2)
        s8 = s4 + sh(s4, p4, 4)
        s16 = s8 + sh(s8, p8, 8)
        px[...] = x
        p2[...] = s2
        p4[...] = s4
        p8[...] = s8
        wsum = _pool_select(col, (s2, s4, s8, s16))
        win = _pool_select(col, POOL_WINDOWS)
        cnt = jnp.minimum(c * tc + row + 1, win).astype(F32)
        p = wsum / cnt - x
        pb = p.astype(BF16)
        p_ref[...] = pb
        for g in range(len(POOL_WINDOWS)):
            sl = slice(g * POOL_GROUP_DIM, (g + 1) * POOL_GROUP_DIM)
            yg = _dot_nn(pb[:, sl], w_ref[g]) * sc_ref[:, sl]
            y_ref[:, sl] = yg.astype(BF16)

    return _call(
        body, name="pool_fwd", grid=(ntc,),
        in_specs=[
            pl.BlockSpec((tc, dp), lambda c: (c, col_block)),
            pl.BlockSpec(pool_w.shape, lambda c: (0, 0, 0)),
            pl.BlockSpec((1, dp), lambda c: (0, 0)),
        ],
        out_specs=[pl.BlockSpec((tc, dp), lambda c: (c, 0))] * 2,
        out_shape=[jax.ShapeDtypeStruct((t, dp), BF16)] * 2,
        scratch_shapes=[pltpu.VMEM((tc, dp), F32)] * 4,
        args=(proj, pool_w, pool_scale))[0]


def _branch_mix(y_lru, y_pool, w_lru_up, w_pool_upb, proj, b_gate, ga_block, gb_block, carry=None):
    t, d = y_lru.shape
    dp = y_pool.shape[1]
    bw = w_pool_upb.shape[2]
    tt, tn = _tile(t, 1024), 512
    nj = d // tn

    def body(yl_ref, yp_ref, wl_ref, wp_ref, ga_ref, gb_ref, ba_ref, bb_ref, bra_ref, brb_ref, mix_ref):
        br_a = _dot_nn(yl_ref[...], wl_ref[...])
        wp = jnp.concatenate([wp_ref[b] for b in range(tn // bw)], axis=1)
        br_b = _dot_nn(yp_ref[...], wp)
        bra_ref[...] = br_a.astype(BF16)
        brb_ref[...] = br_b.astype(BF16)
        ga = _sig(ga_ref[...].astype(F32) + ba_ref[...])
        gb = _sig(gb_ref[...].astype(F32) + bb_ref[...])
        mix_ref[...] = (ga * br_a + gb * br_b).astype(BF16)

    out = pl.BlockSpec((tt, tn), lambda j, i: (i, j))
    return _call(
        body, name="branch_mix", grid=(nj, t // tt),
        in_specs=[
            pl.BlockSpec((tt, d), lambda j, i: (i, 0)),
            pl.BlockSpec((tt, dp), lambda j, i: (i, 0)),
            pl.BlockSpec((d, tn), lambda j, i: (0, j)),
            pl.BlockSpec((tn // bw, dp, bw), lambda j, i: (j, 0, 0)),
            pl.BlockSpec((tt, tn), lambda j, i: (i, ga_block + j)),
            pl.BlockSpec((tt, tn), lambda j, i: (i, gb_block + j)),
            pl.BlockSpec((1, tn), lambda j, i: (0, j)),
            pl.BlockSpec((1, tn), lambda j, i: (0, nj + j)),
        ],
        out_specs=[out, out, out],
        out_shape=[jax.ShapeDtypeStruct((t, d), BF16)] * 3,
        args=(y_lru, y_pool, w_lru_up, w_pool_upb, proj, proj, b_gate, b_gate), carry=carry)


def _wo_norm(mix, w_o, x, g2, g3, carry=None):
    t, d = x.shape
    tt = _tile(t, 512)

    def body(mix_ref, w_ref, x_ref, g2_ref, g3_ref, m_ref, x2_ref, h3_ref):
        m = _dot_nn(mix_ref[...], w_ref[...])
        m_ref[...] = m
        mhat, _ = _rms_hat(m)
        x2 = x_ref[...] + mhat * g2_ref[...]
        x2_ref[...] = x2
        xhat, _ = _rms_hat(x2)
        h3_ref[...] = (xhat * g3_ref[...]).astype(BF16)

    row = pl.BlockSpec((tt, d), lambda i: (i, 0))
    vec = pl.BlockSpec((1, d), lambda i: (0, 0))
    return _call(
        body, name="wo_norm", grid=(t // tt,),
        in_specs=[row, pl.BlockSpec((d, d), lambda i: (0, 0)), row, vec, vec],
        out_specs=[row, row, row],
        out_shape=[
            jax.ShapeDtypeStruct((t, d), F32),
            jax.ShapeDtypeStruct((t, d), F32),
            jax.ShapeDtypeStruct((t, d), BF16),
        ],
        args=(mix, w_o, x, g2, g3), carry=carry)


def _ff1(h3, w_ff1b, carry=None):
    t, d = h3.shape
    nb, _, tn = w_ff1b.shape
    tt = _tile(t, 2048)

    def body(h_ref, w_ref, rf_ref):
        rf_ref[...] = jnp.maximum(_dot_nn(h_ref[...], w_ref[...]), 0.0).astype(BF16)

    out = pl.BlockSpec((tt, tn), lambda i, j: (i, j))
    return _call(
        body, name="ff1", grid=(t // tt, nb),
        in_specs=[pl.BlockSpec((tt, d), lambda i, j: (i, 0)), pl.BlockSpec((None, d, tn), lambda i, j: (j, 0, 0))],
        out_specs=[out],
        out_shape=[jax.ShapeDtypeStruct((t, nb * tn), BF16)],
        args=(h3, w_ff1b), carry=carry)


def _ff2_loss(rf, w_ff2, x2, g4, target):
    t, k = rf.shape
    d = x2.shape[1]
    tt, tk = _tile(t, 1024), _tile(k, 1024)
    nk = k // tk

    def body(a_ref, w_ref, x2_ref, g_ref, tg_ref, dy_ref, df_ref, dg_ref, loss_ref, acc):
        i, kk = pl.program_id(0), pl.program_id(1)

        @pl.when(kk == 0)
        def _():
            acc[...] = jnp.zeros_like(acc)

        @pl.when((i == 0) & (kk == 0))
        def _():
            dg_ref[...] = jnp.zeros_like(dg_ref)
            loss_ref[...] = jnp.zeros_like(loss_ref)

        rf_tile = a_ref[...]
        acc[...] += _dot_nn(rf_tile * rf_tile, w_ref[...])

        @pl.when(kk == nk - 1)
        def _():
            def tail(rows):
                fhat, r = _rms_hat(acc[rows, :])
                g = g_ref[...]
                e = x2_ref[rows, :] + fhat * g - tg_ref[rows, :]
                loss_ref[...] += 0.5 * jnp.sum(jnp.mean(e * e, axis=-1, keepdims=True))
                dy = e * (1.0 / d)
                dy_ref[rows, :] = dy.astype(BF16)
                df, dg = _rms_bwd(dy, fhat, r, g)
                df_ref[rows, :] = df.astype(BF16)
                dg_ref[...] += dg

            _row_chunks(tt, tail)

    row = pl.BlockSpec((tt, d), lambda i, kk: (i, 0))
    vec = pl.BlockSpec((1, d), lambda i, kk: (0, 0))
    return _call(
        body, name="ff2_loss", grid=(t // tt, nk),
        in_specs=[
            pl.BlockSpec((tt, tk), lambda i, kk: (i, kk)),
            pl.BlockSpec((tk, d), lambda i, kk: (kk, 0)),
            row, vec, row,
        ],
        out_specs=[row, row, vec, pl.BlockSpec((1, 128), lambda i, kk: (0, 0))],
        out_shape=[
            jax.ShapeDtypeStruct((t, d), BF16),
            jax.ShapeDtypeStruct((t, d), BF16),
            jax.ShapeDtypeStruct((1, d), F32),
            jax.ShapeDtypeStruct((1, 128), F32),
        ],
        scratch_shapes=[pltpu.VMEM((tt, d), F32)],
        args=(rf, w_ff2, x2, g4, target))[0]


def _ff2_bwd(df, w_ff2, rf, carry=None):
    t, d = df.shape
    n = w_ff2.shape[0]
    tt, tn = _tile(t, 2048), _tile(n, 512)

    def body(df_ref, w_ref, rf_ref, out_ref):
        d_act = _dot_nt(df_ref[...], w_ref[...])
        out_ref[...] = (d_act * (2.0 * rf_ref[...].astype(F32))).astype(BF16)

    blk = pl.BlockSpec((tt, tn), lambda i, j: (i, j))
    return _call(
        body, name="ff2_bwd", grid=(t // tt, n // tn),
        in_specs=[pl.BlockSpec((tt, d), lambda i, j: (i, 0)), pl.BlockSpec((tn, d), lambda i, j: (j, 0)), blk],
        out_specs=[blk],
        out_shape=[jax.ShapeDtypeStruct((t, n), BF16)],
        args=(df, w_ff2, rf), carry=carry)


def _wgrad(a, b, name, prev=None, row_off=0, rows=None, carry=None, square_a=False):
    t, m = a.shape
    n = b.shape[1]
    rows = m if rows is None else rows
    tm, tk = _tile(m, 512), _tile(t, 2048)
    nk = t // tk
    assert row_off % tm == 0
    off = row_off // tm

    def body(*refs):
        a_ref, b_ref = refs[0], refs[1]
        o32_ref, o16_ref, acc = refs[-3], refs[-2], refs[-1]
        kk = pl.program_id(1)

        @pl.when(kk == 0)
        def _():
            acc[...] = jnp.zeros_like(acc)

        a_tile = a_ref[...]
        acc[...] += _dot_tn(a_tile * a_tile if square_a else a_tile, b_ref[...])

        @pl.when(kk == nk - 1)
        def _():
            o32_ref[...] = acc[...]
            o16_ref[...] = acc[...].astype(BF16)

    in_specs = [pl.BlockSpec((tk, tm), lambda i, kk: (kk, i)), pl.BlockSpec((tk, n), lambda i, kk: (kk, 0))]
    args = [a, b]
    aliases = {}
    if prev is not None:
        in_specs += [ANY, ANY]
        args += list(prev)
        aliases = {2: 0, 3: 1}
    out = pl.BlockSpec((tm, n), lambda i, kk: (off + i, 0))
    return _call(
        body, name=name, grid=(m // tm, nk),
        in_specs=in_specs, out_specs=[out, out],
        out_shape=[jax.ShapeDtypeStruct((rows, n), F32), jax.ShapeDtypeStruct((rows, n), BF16)],
        scratch_shapes=[pltpu.VMEM((tm, n), F32)],
        aliases=aliases, args=args, carry=carry)


def _wgrad_parts(parts, b, name, carry=None):
    t, n = b.shape
    tm = 512
    bounds = []
    lo = 0
    for part in parts:
        assert part.shape[0] == t and part.shape[1] % tm == 0
        bounds.append((lo, lo + part.shape[1] // tm))
        lo += part.shape[1] // tm
    nm = lo
    np_ = len(parts)

    def body(*refs):
        p_refs, b_ref, o32_ref, o16_ref = refs[:np_], refs[np_], refs[np_ + 1], refs[np_ + 2]
        i = pl.program_id(0)
        for (lo_p, hi_p), p_ref in zip(bounds, p_refs):
            @pl.when((i >= lo_p) & (i < hi_p))
            def _(p_ref=p_ref):
                res = _dot_tn(p_ref[...], b_ref[...])
                o32_ref[...] = res
                o16_ref[...] = res.astype(BF16)

    def part_spec(lo_p, hi_p):
        return pl.BlockSpec((t, tm), lambda i: (0, jnp.clip(i - lo_p, 0, hi_p - lo_p - 1)))

    out = pl.BlockSpec((tm, n), lambda i: (i, 0))
    return _call(
        body, name=name, grid=(nm,),
        in_specs=[part_spec(lo_p, hi_p) for lo_p, hi_p in bounds] + [pl.BlockSpec((t, n), lambda i: (0, 0))],
        out_specs=[out, out],
        out_shape=[jax.ShapeDtypeStruct((nm * tm, n), F32), jax.ShapeDtypeStruct((nm * tm, n), BF16)],
        args=(*parts, b), carry=carry)


def _wgrad_cols(a, b, bw, tn, name, carry=None):
    t, m = a.shape
    n = b.shape[1]
    per_step = tn // bw

    def body(a_ref, b_ref, o32_ref, o16_ref):
        res = _dot_tn(a_ref[...], b_ref[...])
        for blk in range(per_step):
            part = res[:, blk * bw:(blk + 1) * bw]
            o32_ref[blk] = part
            o16_ref[blk] = part.astype(BF16)

    out = pl.BlockSpec((per_step, m, bw), lambda j: (j, 0, 0))
    return _call(
        body, name=name, grid=(n // tn,),
        in_specs=[pl.BlockSpec((t, m), lambda j: (0, 0)), pl.BlockSpec((t, tn), lambda j: (0, j))],
        out_specs=[out, out],
        out_shape=[jax.ShapeDtypeStruct((n // bw, m, bw), F32), jax.ShapeDtypeStruct((n // bw, m, bw), BF16)],
        args=(a, b), carry=carry)


def _ff1_bwd_norms(d_f1, w_ff1b, dy, x2, g3, m, g2, carry=None):
    t, k = d_f1.shape
    d = x2.shape[1]
    bw = w_ff1b.shape[2]
    per_step = 2
    tt, tk = _tile(t, 1024), per_step * bw
    nk = k // tk

    def body(a_ref, w_ref, dy_hbm, x2_hbm, g3_ref, m_hbm, g2_ref, dx2_ref, dm_ref, dg3_ref, dg2_ref, acc,
             dy_ref, x2_ref, m_ref, late_sems):
        i, kk = pl.program_id(0), pl.program_id(1)
        late = _late_copies(i, tt, [(dy_hbm, dy_ref), (x2_hbm, x2_ref), (m_hbm, m_ref)], late_sems)

        @pl.when(kk == 0)
        def _():
            acc[...] = jnp.zeros_like(acc)
            for cp in late:
                cp.start()

        @pl.when((i == 0) & (kk == 0))
        def _():
            dg3_ref[...] = jnp.zeros_like(dg3_ref)
            dg2_ref[...] = jnp.zeros_like(dg2_ref)

        a_tile = a_ref[...]
        for b in range(per_step):
            acc[...] += _dot_nt(a_tile[:, b * bw:(b + 1) * bw], w_ref[b])

        @pl.when(kk == nk - 1)
        def _():
            for cp in late:
                cp.wait()

            def tail(rows):
                xhat, r3 = _rms_hat(x2_ref[rows, :])
                dx, dg3 = _rms_bwd(acc[rows, :], xhat, r3, g3_ref[...])
                dx2 = dy_ref[rows, :].astype(F32) + dx
                dx2_ref[rows, :] = dx2
                dg3_ref[...] += dg3
                mhat, r2 = _rms_hat(m_ref[rows, :])
                dm, dg2 = _rms_bwd(dx2, mhat, r2, g2_ref[...])
                dm_ref[rows, :] = dm.astype(BF16)
                dg2_ref[...] += dg2

            _row_chunks(tt, tail)

    row = pl.BlockSpec((tt, d), lambda i, kk: (i, 0))
    vec = pl.BlockSpec((1, d), lambda i, kk: (0, 0))
    return _call(
        body, name="ff1_bwd_norms", grid=(t // tt, nk),
        in_specs=[
            pl.BlockSpec((tt, tk), lambda i, kk: (i, kk)),
            pl.BlockSpec((per_step, d, bw), lambda i, kk: (kk, 0, 0)),
            ANY, ANY, vec, ANY, vec,
        ],
        out_specs=[row, row, vec, vec],
        out_shape=[
            jax.ShapeDtypeStruct((t, d), F32),
            jax.ShapeDtypeStruct((t, d), BF16),
            jax.ShapeDtypeStruct((1, d), F32),
            jax.ShapeDtypeStruct((1, d), F32),
        ],
        scratch_shapes=[pltpu.VMEM((tt, d), F32), pltpu.VMEM((tt, d), dy.dtype), pltpu.VMEM((tt, d), F32),
                        pltpu.VMEM((tt, d), F32), pltpu.SemaphoreType.DMA((3,))],
        args=(d_f1, w_ff1b, dy, x2, g3, m, g2), carry=carry)


def _wo_bwd_mix(dm, w_o, br_a, br_b, proj, b_gate, ga_block, gb_block, carry=None):
    t, d = dm.shape
    tt, tn = _tile(t, 1024), 512
    nj = d // tn

    def body(dm_ref, w_ref, bra_ref, brb_ref, ga_ref, gb_ref, ba_ref, bb_ref,
             dbra_ref, dbrb_ref, dga_ref, dgb_ref, dba_ref, dbb_ref):
        i = pl.program_id(1)

        @pl.when(i == 0)
        def _():
            dba_ref[...] = jnp.zeros_like(dba_ref)
            dbb_ref[...] = jnp.zeros_like(dbb_ref)

        d_mix = _dot_nt(dm_ref[...], w_ref[...])
        ga = _sig(ga_ref[...].astype(F32) + ba_ref[...])
        gb = _sig(gb_ref[...].astype(F32) + bb_ref[...])
        dbra_ref[...] = (d_mix * ga).astype(BF16)
        dbrb_ref[...] = (d_mix * gb).astype(BF16)
        dga = d_mix * bra_ref[...].astype(F32) * (ga * (1.0 - ga))
        dgb = d_mix * brb_ref[...].astype(F32) * (gb * (1.0 - gb))
        dga_ref[...] = dga.astype(BF16)
        dgb_ref[...] = dgb.astype(BF16)
        dba_ref[...] += jnp.sum(dga, axis=0, keepdims=True)
        dbb_ref[...] += jnp.sum(dgb, axis=0, keepdims=True)

    blk = pl.BlockSpec((tt, tn), lambda j, i: (i, j))
    vec = pl.BlockSpec((1, tn), lambda j, i: (0, j))
    return _call(
        body, name="wo_bwd_mix", grid=(nj, t // tt),
        in_specs=[
            pl.BlockSpec((tt, d), lambda j, i: (i, 0)),
            pl.BlockSpec((tn, d), lambda j, i: (j, 0)),
            blk, blk,
            pl.BlockSpec((tt, tn), lambda j, i: (i, ga_block + j)),
            pl.BlockSpec((tt, tn), lambda j, i: (i, gb_block + j)),
            vec,
            pl.BlockSpec((1, tn), lambda j, i: (0, nj + j)),
        ],
        out_specs=[blk, blk, blk, blk, vec, vec],
        out_shape=[jax.ShapeDtypeStruct((t, d), BF16)] * 4 + [jax.ShapeDtypeStruct((1, d), F32)] * 2,
        args=(dm, w_o, br_a, br_b, proj, proj, b_gate, b_gate), carry=carry)


def _lru_up_bwd(d_br_a, w_lru_up, proj, h, g_block, carry=None):
    t, d = d_br_a.shape
    tt, tn = _tile(t, 1024), 512

    def body(a_ref, w_ref, g_ref, h_ref, dh_ref, dg_ref):
        d_y = _dot_nt(a_ref[...], w_ref[...])
        gel, gel_grad = _gelu_and_grad(g_ref[...].astype(F32))
        dh_ref[...] = d_y * gel
        dg_ref[...] = (d_y * h_ref[...] * gel_grad).astype(BF16)

    blk = pl.BlockSpec((tt, tn), lambda i, j: (i, j))
    return _call(
        body, name="lru_up_bwd", grid=(t // tt, d // tn),
        in_specs=[
            pl.BlockSpec((tt, d), lambda i, j: (i, 0)),
            pl.BlockSpec((tn, d), lambda i, j: (j, 0)),
            pl.BlockSpec((tt, tn), lambda i, j: (i, g_block + j)),
            blk,
        ],
        out_specs=[blk, blk],
        out_shape=[jax.ShapeDtypeStruct((t, d), F32), jax.ShapeDtypeStruct((t, d), BF16)],
        args=(d_br_a, w_lru_up, proj, h), carry=carry)


def _lru_bwd(dh, h, saved, proj, conv_w, w_a, w_x, lam, carry=None):
    t, dr = dh.shape
    cb = LRU_CB
    hd = LRU_HEAD_DIM
    per = cb // hd
    tc = _tile(t, 256)
    ncb, ntc = dr // cb, t // tc

    def body(dh_ref, h_ref, hp_ref, saved_ref, xp_ref, cw_ref, wa_ref, wx_ref,
             lam_ref, dxp_ref, dwa_ref, dba_ref, dwx_ref, dbx_ref, dlam_ref, dcw_ref, dcb_ref,
             nextd_s, anext_s, gnext_s, tmp_s, wa_s, wx_s):
        c = pl.program_id(1)
        rc = ntc - 1 - c

        @pl.when(c == 0)
        def _():
            nextd_s[...] = jnp.zeros_like(nextd_s)
            anext_s[...] = jnp.zeros_like(anext_s)
            gnext_s[...] = jnp.zeros_like(gnext_s)
            for ref in (dwa_ref, dba_ref, dwx_ref, dbx_ref, dlam_ref, dcw_ref, dcb_ref):
                ref[...] = jnp.zeros_like(ref)
            _fill_block_diag(wa_ref, wa_s)
            _fill_block_diag(wx_ref, wx_s)

        xc, r, i, a, mult = [saved_ref[:, k * cb:(k + 1) * cb] for k in range(N_LRU_SAVED)]
        wa, wx, lam = wa_s[...], wx_s[...], lam_ref[...]
        xcb = xc.astype(BF16)
        sp = _softplus_neg(lam)
        row = lax.broadcasted_iota(jnp.int32, xc.shape, 0)
        h = h_ref[...]
        hp = jnp.where(rc == 0, 0.0, hp_ref[...])
        hprev = jnp.where(row >= 1, pltpu.roll(h, 1, 0), pltpu.roll(hp, 1, 0))

        def up(v, nv, j):
            return jnp.where(row < tc - j, pltpu.roll(v, tc - j, 0), nv)

        av, bv = _scan_rows(up(a, anext_s[...], 1), dh_ref[...], reverse=True)
        gt = av * gnext_s[...] + bv
        tmp_s[...] = gt
        gnext_s[...] = tmp_s[0:1, :]
        tmp_s[...] = a
        anext_s[...] = tmp_s[0:1, :]

        da = gt * hprev
        ixc = i * xc
        d_mult = gt * ixc
        d_i = gt * mult * xc
        d_xc = gt * mult * i
        d_log_a = da * a - d_mult * (a * a) / mult
        d_pre_r = (d_log_a * ((-LRU_C) * sp)) * (r * (1.0 - r))
        d_pre_i = d_i * (i * (1.0 - i))
        d_sp = jnp.sum(d_log_a * ((-LRU_C) * r), axis=0, keepdims=True)
        dlam_ref[...] += d_sp * (-1.0 / (1.0 + jnp.exp(lam)))
        dpr = d_pre_r.astype(BF16)
        dpi = d_pre_i.astype(BF16)
        dba_ref[...] += jnp.sum(d_pre_r, axis=0, keepdims=True)
        dbx_ref[...] += jnp.sum(d_pre_i, axis=0, keepdims=True)
        pa = _dot_tn(xcb, dpr)
        px = _dot_tn(xcb, dpi)
        for k in range(per):
            dwa_ref[k] += pa[k * hd:(k + 1) * hd, k * hd:(k + 1) * hd]
            dwx_ref[k] += px[k * hd:(k + 1) * hd, k * hd:(k + 1) * hd]
        d_xc = d_xc + _dot_nt(dpr, wa) + _dot_nt(dpi, wx)

        nxt = nextd_s[...]
        xp = xp_ref[...].astype(F32)
        dxp = cw_ref[3:4, :] * d_xc
        dcw_ref[3:4, :] += jnp.sum(xp * d_xc, axis=0, keepdims=True)
        for j in (1, 2, 3):
            uj = up(d_xc, pltpu.roll(nxt, tc - j, 0), j)
            dxp = dxp + cw_ref[3 - j:4 - j, :] * uj
            dcw_ref[3 - j:4 - j, :] += jnp.sum(xp * uj, axis=0, keepdims=True)
        dcb_ref[...] += jnp.sum(d_xc, axis=0, keepdims=True)
        nextd_s[...] = d_xc
        dxp_ref[...] = dxp.astype(BF16)

    vec = pl.BlockSpec((1, cb), lambda j, c: (0, j))
    blk = pl.BlockSpec((tc, cb), lambda j, c: (ntc - 1 - c, j))
    mat = pl.BlockSpec((per, hd, hd), lambda j, c: (j, 0, 0))
    cwb = pl.BlockSpec((4, cb), lambda j, c: (0, j))
    return _call(
        body, name="lru_bwd", grid=(ncb, ntc),
        in_specs=[
            blk, blk,
            pl.BlockSpec((tc, cb), lambda j, c: (jnp.maximum(ntc - 2 - c, 0), j)),
            pl.BlockSpec((tc, N_LRU_SAVED * cb), lambda j, c: (ntc - 1 - c, j)),
            blk, cwb, mat, mat, vec,
        ],
        out_specs=[blk, mat, vec, mat, vec, vec, cwb, vec],
        out_shape=[
            jax.ShapeDtypeStruct((t, dr), BF16),
            jax.ShapeDtypeStruct(w_a.shape, F32),
            jax.ShapeDtypeStruct((1, dr), F32),
            jax.ShapeDtypeStruct(w_x.shape, F32),
            jax.ShapeDtypeStruct((1, dr), F32),
            jax.ShapeDtypeStruct((1, dr), F32),
            jax.ShapeDtypeStruct((4, dr), F32),
            jax.ShapeDtypeStruct((1, dr), F32),
        ],
        scratch_shapes=[
            pltpu.VMEM((tc, cb), F32),
            pltpu.VMEM((1, cb), F32),
            pltpu.VMEM((1, cb), F32),
            pltpu.VMEM((tc, cb), F32),
            pltpu.VMEM((cb, cb), BF16),
            pltpu.VMEM((cb, cb), BF16),
        ],
        args=(dh, h, h, saved, proj, conv_w, w_a, w_x, lam), carry=carry)


def _pool_bwd(d_br_b, w_pool_upb, p, pool_w, pool_scale):
    t, d = d_br_b.shape
    nwb, dp, _ = w_pool_upb.shape
    tc = _tile(t, 256)
    ntc = t // tc
    ng = len(POOL_WINDOWS)

    def body(db_ref, wu_ref, p_ref, w_ref, sc_ref, dx_ref, dw_ref, dsc_ref, nz, n2, n4, n8, dp_s, dy_s):
        c = pl.program_id(0)
        rc = ntc - 1 - c

        @pl.when(c == 0)
        def _():
            for s in (nz, n2, n4, n8):
                s[...] = jnp.zeros_like(s)
            dw_ref[...] = jnp.zeros_like(dw_ref)
            dsc_ref[...] = jnp.zeros_like(dsc_ref)

        wu = jnp.concatenate([wu_ref[b] for b in range(nwb)], axis=1)
        dy_s[...] = _dot_nt(db_ref[...], wu)
        for g in range(ng):
            sl = slice(g * POOL_GROUP_DIM, (g + 1) * POOL_GROUP_DIM)
            pg = p_ref[:, sl]
            dyg = dy_s[:, sl]
            wg = w_ref[g].astype(BF16)
            q = _dot_nn(pg, wg)
            dsc_ref[:, sl] += jnp.sum(dyg * q, axis=0, keepdims=True)
            dpw = (dyg * sc_ref[:, sl]).astype(BF16)
            dw_ref[g] += _dot_tn(pg, dpw)
            dp_s[:, sl] = _dot_nt(dpw, wg)

        dpv = dp_s[...]
        row = lax.broadcasted_iota(jnp.int32, dpv.shape, 0)
        col = lax.broadcasted_iota(jnp.int32, dpv.shape, 1)
        win = _pool_select(col, POOL_WINDOWS)
        cnt = jnp.minimum(rc * tc + row + 1, win).astype(F32)
        z = dpv / cnt

        def up(v, nv, j):
            return jnp.where(row < tc - j, pltpu.roll(v, tc - j, 0), pltpu.roll(nv[...], tc - j, 0))

        u2 = z + up(z, nz, 1)
        u4 = u2 + up(u2, n2, 2)
        u8 = u4 + up(u4, n4, 4)
        u16 = u8 + up(u8, n8, 8)
        nz[...] = z
        n2[...] = u2
        n4[...] = u4
        n8[...] = u8
        dx_ref[...] = (_pool_select(col, (u2, u4, u8, u16)) - dpv).astype(BF16)

    blk = pl.BlockSpec((tc, dp), lambda c: (ntc - 1 - c, 0))
    full_w = pl.BlockSpec(pool_w.shape, lambda c: (0, 0, 0))
    vec = pl.BlockSpec((1, dp), lambda c: (0, 0))
    return _call(
        body, name="pool_bwd", grid=(ntc,),
        in_specs=[pl.BlockSpec((tc, d), lambda c: (ntc - 1 - c, 0)),
                  pl.BlockSpec(w_pool_upb.shape, lambda c: (0, 0, 0)), blk, full_w, vec],
        out_specs=[blk, full_w, vec],
        out_shape=[
            jax.ShapeDtypeStruct((t, dp), BF16),
            jax.ShapeDtypeStruct(pool_w.shape, F32),
            jax.ShapeDtypeStruct((1, dp), F32),
        ],
        scratch_shapes=[pltpu.VMEM((tc, dp), F32)] * 6,
        args=(d_br_b, w_pool_upb, p, pool_w, pool_scale))[0]


def _win_bwd_norm(parts, w_int, dx2, x, g1, carry=None):
    t, d = x.shape
    tk = 512
    tt = _tile(t, 1024)
    bounds = []
    k0 = 0
    for part in parts:
        assert part.shape[1] % tk == 0
        bounds.append((k0, k0 + part.shape[1] // tk))
        k0 += part.shape[1] // tk
    nk = k0
    assert nk * tk == w_int.shape[0]
    np_ = len(parts)

    def body(*refs):
        p_refs = refs[:np_]
        w_ref, dx2_hbm, x_hbm, g_ref, gx_ref, dg_ref, acc, dx2_ref, x_ref, late_sems = refs[np_:]
        i, kk = pl.program_id(0), pl.program_id(1)
        late = _late_copies(i, tt, [(dx2_hbm, dx2_ref), (x_hbm, x_ref)], late_sems)

        @pl.when(kk == 0)
        def _():
            acc[...] = jnp.zeros_like(acc)
            for cp in late:
                cp.start()

        @pl.when((i == 0) & (kk == 0))
        def _():
            dg_ref[...] = jnp.zeros_like(dg_ref)

        for (lo, hi), p_ref in zip(bounds, p_refs):
            @pl.when((kk >= lo) & (kk < hi))
            def _(p_ref=p_ref):
                acc[...] += _dot_nn(p_ref[...], w_ref[...])

        @pl.when(kk == nk - 1)
        def _():
            for cp in late:
                cp.wait()

            def tail(rows):
                xhat, r = _rms_hat(x_ref[rows, :])
                dx, dg = _rms_bwd(acc[rows, :], xhat, r, g_ref[...])
                gx_ref[rows, :] = dx2_ref[rows, :] + dx
                dg_ref[...] += dg

            _row_chunks(tt, tail)

    def part_spec(lo, hi):
        return pl.BlockSpec((tt, tk), lambda i, kk: (i, jnp.clip(kk - lo, 0, hi - lo - 1)))

    row = pl.BlockSpec((tt, d), lambda i, kk: (i, 0))
    vec = pl.BlockSpec((1, d), lambda i, kk: (0, 0))
    return _call(
        body, name="win_bwd_norm", grid=(t // tt, nk),
        in_specs=[part_spec(lo, hi) for lo, hi in bounds]
        + [pl.BlockSpec((tk, d), lambda i, kk: (kk, 0)), ANY, ANY, vec],
        out_specs=[row, vec],
        out_shape=[jax.ShapeDtypeStruct((t, d), F32), jax.ShapeDtypeStruct((1, d), F32)],
        scratch_shapes=[pltpu.VMEM((tt, d), F32), pltpu.VMEM((tt, d), F32), pltpu.VMEM((tt, d), F32),
                        pltpu.SemaphoreType.DMA((2,))],
        args=(*parts, w_int, dx2, x, g1), carry=carry)


def _adam_math(w, g, m, v):
    m = ADAM_B1 * m + (1.0 - ADAM_B1) * g
    v = ADAM_B2 * v + (1.0 - ADAM_B2) * (g * g)
    m_hat = m / (1.0 - ADAM_B1 ** ADAM_STEP)
    v_hat = v / (1.0 - ADAM_B2 ** ADAM_STEP)
    delta = -ADAM_LR * (m_hat / (jnp.sqrt(v_hat) + ADAM_EPS) + ADAM_WD * w)
    return delta, m, v


def _adamw_big(ws, gs, ms, vs):
    n = len(ws)
    nb = 4
    pair = [isinstance(g, tuple) for g in gs]

    def body(*refs):
        p = 0
        ins = []
        for a in range(n):
            k = 5 if pair[a] else 4
            ins.append(refs[p:p + k])
            p += k
        for a in range(n):
            g_out, d_ref, nm_ref, nv_ref = refs[p + 4 * a:p + 4 * a + 4]
            if pair[a]:
                w_ref, own_ref, recv_ref, m_ref, v_ref = ins[a]
                g = own_ref[...]
                for k in range(3):
                    g = g + recv_ref[k].astype(F32)
            else:
                w_ref, g_ref, m_ref, v_ref = ins[a]
                g = g_ref[...]
            dl, m, v = _adam_math(w_ref[...], g, m_ref[...], v_ref[...])
            g_out[...] = g
            d_ref[...] = dl
            nm_ref[...] = m
            nv_ref[...] = v

    in_specs, out_specs, out_shape, args = [], [], [], []
    for a, (w, g, m, v) in enumerate(zip(ws, gs, ms, vs)):
        rows, cols = w.shape
        blk = pl.BlockSpec((rows // nb, cols), lambda i: (i, 0))
        if pair[a]:
            in_specs += [blk, pl.BlockSpec((None, rows // nb, cols), lambda i: (0, i, 0)),
                         pl.BlockSpec((3, rows // nb, cols), lambda i: (0, i, 0)), blk, blk]
            args += [w, g[0], g[1], m, v]
        else:
            in_specs += [blk] * 4
            args += [w, g, m, v]
        out_specs += [blk] * 4
        out_shape += [jax.ShapeDtypeStruct(w.shape, F32)] * 4
    outs = _call(body, name="adamw_big", grid=(nb,), in_specs=in_specs, out_specs=out_specs,
                 out_shape=out_shape, args=args)[0]
    return [tuple(outs[4 * a:4 * a + 4]) for a in range(n)]


SMALL_ORDER = ("norm_mix_pre", "norm_mix_post", "norm_mlp_pre", "norm_mlp_post", "b_gate", "conv_w", "conv_b",
               "lru_w_a", "lru_b_a", "lru_w_x", "lru_b_x", "lru_lambda", "pool_w", "pool_scale")
VEC_ROW = dict(norm_mix_pre=0, norm_mix_post=1, norm_mlp_pre=2, norm_mlp_post=3, conv_b=6, lru_b_a=7,
               lru_b_x=8, lru_lambda=9)
ROW_B_GATE, ROW_POOL_SCALE, ROW_CONV_W, ROW_LOSS, N_VEC_ROWS = 4, 10, 11, 15, 16


def _adamw_small(vec_parts, g_pool, g_wa, g_wx, me, params):
    d = vec_parts.shape[2]
    names = SMALL_ORDER
    n = len(names)
    cw_cols = params["conv_w"][0].shape[2]

    def body(me_ref, vec_ref, vecc_ref, gp_ref, gwa_ref, gwx_ref, *refs):
        wmv = refs[:3 * n]
        loss_ref = refs[3 * n]
        outs = refs[3 * n + 1:3 * n + 1 + 4 * n]
        vs, vsc = refs[3 * n + 1 + 4 * n:]
        acc, accc = vec_ref[0], vecc_ref[0]
        for k in range(1, N_DEV):
            acc = acc + vec_ref[k]
            accc = accc + vecc_ref[k]
        vs[...] = acc
        vsc[...] = accc
        loss_ref[...] = vs[ROW_LOSS:ROW_LOSS + 1, 0:128]

        def upd(a, g, idx):
            w_ref, m_ref, v_ref = wmv[3 * a:3 * a + 3]
            g_ref, d_ref, nm_ref, nv_ref = outs[4 * a:4 * a + 4]
            dl, m, v = _adam_math(w_ref[idx], g, m_ref[idx], v_ref[idx])
            g_ref[idx] = g
            d_ref[idx] = dl
            nm_ref[idx] = m
            nv_ref[idx] = v

        for a, name in enumerate(names):
            if name in VEC_ROW:
                r = VEC_ROW[name]
                upd(a, vs[r:r + 1, :], (slice(None), slice(None)))
            elif name == "b_gate":
                for half in range(2):
                    r = ROW_B_GATE + half
                    upd(a, vs[r:r + 1, :], (slice(None), slice(half * d, (half + 1) * d)))
            elif name == "pool_scale":
                width = params[name][0].shape[1]
                upd(a, vs[ROW_POOL_SCALE:ROW_POOL_SCALE + 1, 0:width], (slice(None), slice(None)))
            elif name == "conv_w":
                upd(a, vsc[ROW_CONV_W:ROW_CONV_W + 4, :], (0,))
            elif name == "pool_w":
                upd(a, gp_ref[...], (Ellipsis,))
            elif name == "lru_w_a":
                upd(a, gwa_ref[...], (Ellipsis,))
            elif name == "lru_w_x":
                upd(a, gwx_ref[...], (Ellipsis,))
            else:
                raise ValueError(name)

    def whole(shape):
        nd = len(shape)
        return pl.BlockSpec(tuple(shape), lambda i, me_ref: (0,) * nd)

    in_specs = [
        whole(vec_parts.shape),
        pl.BlockSpec((N_DEV, N_VEC_ROWS, cw_cols), lambda i, me_ref: (0, 0, me_ref[0])),
        whole(g_pool.shape), whole(g_wa.shape), whole(g_wx.shape),
    ]
    args = [vec_parts, vec_parts, g_pool, g_wa, g_wx]
    out_specs = [whole((1, 128))]
    out_shape = [jax.ShapeDtypeStruct((1, 128), F32)]
    for name in names:
        for arr in params[name]:
            in_specs.append(whole(arr.shape))
            args.append(arr)
        shp = params[name][0].shape
        out_specs += [whole(shp)] * 4
        out_shape += [jax.ShapeDtypeStruct(shp, F32)] * 4
    grid_spec = pltpu.PrefetchScalarGridSpec(
        num_scalar_prefetch=1, grid=(1,), in_specs=in_specs, out_specs=out_specs,
        scratch_shapes=[pltpu.VMEM((N_VEC_ROWS, d), F32), pltpu.VMEM((N_VEC_ROWS, cw_cols), F32)])
    outs = pl.pallas_call(
        body, name="adamw_small", grid_spec=grid_spec, out_shape=out_shape,
        compiler_params=pltpu.CompilerParams(
            dimension_semantics=("arbitrary",), vmem_limit_bytes=V7X_VMEM_LIMIT_BYTES),
    )(me, *_in_hbm(args))
    return outs[0], {name: tuple(outs[1 + 4 * a:5 + 4 * a]) for a, name in enumerate(names)}


def _rs_sum(fulls, recvs, shard_ids, slot_ids, name):
    n = len(fulls)

    def body(sh_ref, sl_ref, *refs):
        s = pl.program_id(0)
        for a in range(n):
            full_ref, recv_ref = refs[2 * a], refs[2 * a + 1]
            own_ref, send_ref = refs[2 * n + 2 * a], refs[2 * n + 2 * a + 1]
            v = full_ref[...] + recv_ref[...].astype(F32)

            @pl.when(s == 0)
            def _(own_ref=own_ref, v=v):
                own_ref[...] = v

            @pl.when(s > 0)
            def _(send_ref=send_ref, v=v):
                send_ref[...] = v.astype(send_ref.dtype)

    in_specs, out_specs, out_shape, args = [], [], [], []
    for full, recv in zip(fulls, recvs):
        r, rest = recv.shape[1], tuple(recv.shape[2:])
        zeros = (0,) * len(rest)
        in_specs += [
            pl.BlockSpec((r,) + rest, lambda s, sh, sl, zeros=zeros: (sh[s],) + zeros),
            pl.BlockSpec((None, r) + rest, lambda s, sh, sl, zeros=zeros: (sl[s], 0) + zeros),
        ]
        out_specs += [
            pl.BlockSpec((None, r) + rest, lambda s, sh, sl, zeros=zeros: (0, 0) + zeros),
            pl.BlockSpec((None, r) + rest, lambda s, sh, sl, zeros=zeros: (jnp.maximum(s - 1, 0), 0) + zeros),
        ]
        out_shape += [jax.ShapeDtypeStruct((1, r) + rest, F32), jax.ShapeDtypeStruct((3, r) + rest, recv.dtype)]
        args += [full, recv]
    grid_spec = pltpu.PrefetchScalarGridSpec(
        num_scalar_prefetch=2, grid=(4,), in_specs=in_specs, out_specs=out_specs)
    outs = pl.pallas_call(
        body,
        name=name,
        grid_spec=grid_spec,
        out_shape=out_shape,
        compiler_params=pltpu.CompilerParams(
            dimension_semantics=("arbitrary",), vmem_limit_bytes=V7X_VMEM_LIMIT_BYTES),
    )(shard_ids, slot_ids, *_in_hbm(args))
    return [(outs[2 * a], outs[2 * a + 1]) for a in range(n)]


def _finals(pairs, name, carry=None):
    nb = 4
    n = len(pairs)

    def body(*refs):
        for a in range(n):
            own_ref, recv_ref = refs[2 * a], refs[2 * a + 1]
            acc = own_ref[...]
            for k in range(3):
                acc = acc + recv_ref[k].astype(F32)
            refs[2 * n + a][...] = acc

    in_specs, out_specs, out_shape, args = [], [], [], []
    for own, recv in pairs:
        _, rows, cols = own.shape
        in_specs += [pl.BlockSpec((None, rows // nb, cols), lambda i: (0, i, 0)),
                     pl.BlockSpec((3, rows // nb, cols), lambda i: (0, i, 0))]
        args += [own, recv]
        out_specs.append(pl.BlockSpec((rows // nb, cols), lambda i: (i, 0)))
        out_shape.append(jax.ShapeDtypeStruct((rows, cols), F32))
    return _call(body, name=name, grid=(nb,), in_specs=in_specs, out_specs=out_specs,
                 out_shape=out_shape, args=args, carry=carry)


def _rs_sums(fulls_f32, recv1, tag):
    x, y, c = _place()
    qs = jnp.stack([2 * x + y, 2 * (1 - x) + y, 2 * x + (1 - y), 2 * (1 - x) + (1 - y)]).astype(jnp.int32)
    shard_ids = 2 * qs + c
    return _rs_sum(fulls_f32, recv1, shard_ids, qs, "rs_sum_" + tag)


def _rs_level1(fulls_f32, fulls_send, tag):
    recv1 = _run_plan(_rs_sibling_plan(fulls_send), "rs_sibling_" + tag)
    return _rs_sums(fulls_f32, recv1, tag)


def _rows(g):
    return g.reshape(g.shape[0] * g.shape[1], g.shape[2])


def kernel(x, norm_mix_pre, norm_mix_post, norm_mlp_pre, norm_mlp_post, w_in, b_gate, conv_w, conv_b, lru_w_a, lru_b_a, lru_w_x, lru_b_x, lru_lambda, pool_w, pool_scale, w_lru_up, w_pool_up, w_o, w_ff1, w_ff2, loss_target, m_norm_mix_pre, m_norm_mix_post, m_norm_mlp_pre, m_norm_mlp_post, m_w_in, m_b_gate, m_conv_w, m_conv_b, m_lru_w_a, m_lru_b_a, m_lru_w_x, m_lru_b_x, m_lru_lambda, m_pool_w, m_pool_scale, m_w_lru_up, m_w_pool_up, m_w_o, m_w_ff1, m_w_ff2, v_norm_mix_pre, v_norm_mix_post, v_norm_mlp_pre, v_norm_mlp_post, v_w_in, v_b_gate, v_conv_w, v_conv_b, v_lru_w_a, v_lru_b_a, v_lru_w_x, v_lru_b_x, v_lru_lambda, v_pool_w, v_pool_scale, v_w_lru_up, v_w_pool_up, v_w_o, v_w_ff1, v_w_ff2):
    t, d = x.shape[1], x.shape[2]
    d_rnn = conv_b.shape[1]
    d_pool = pool_scale.shape[1]
    per = LRU_CB // LRU_HEAD_DIM
    xi, yi, ci = _place()
    me = 4 * xi + 2 * yi + ci

    x2d = x[0]
    tgt = loss_target[0]

    s_in = w_in[0].T.astype(BF16)
    s_lu = w_lru_up[0].astype(BF16)
    s_pu = w_pool_up[0].astype(BF16)
    s_o = w_o[0].astype(BF16)
    s_f1 = w_ff1[0].astype(BF16)
    s_f2 = w_ff2[0].astype(BF16)
    s_cw = jnp.pad(conv_w[0], ((0, 4), (0, 0)))

    g_in, g_cw = _run_plan(_ag_plan([s_in, s_cw]), "ag_w_in")
    w_int = _rows(g_in)
    conv_w_full = jnp.transpose(g_cw[:, :4, :], (1, 0, 2)).reshape(4, d_rnn)

    wa_bd, wx_bd = lru_w_a[0], lru_w_x[0]
    pw = pool_w[0]
    pw_bf = pw.astype(BF16)

    pool_block = (2 * d_rnn) // d_pool
    ga_block = (2 * d_rnn + d_pool) // 512
    gb_block = ga_block + d // 512
    g_block = d_rnn // 512

    r_f1, r_f2 = s_f1.shape[0], s_f2.shape[0]
    f1_cut = r_f1 // 4
    f2_cut = (3 * r_f2) // 8
    plan = _join([_ag_plan([s_lu, s_pu]), _ag_plan([s_f1], pieces=[(0, f1_cut)])])
    (proj, h1), got = _norm_proj(x2d, norm_mix_pre, w_int, carry=plan)
    (g_lu, g_pu), (g_f1,) = plan.split(got)
    plan = _join([_ag_plan([s_f1], pieces=[(f1_cut, r_f1 - f1_cut)], bufs=[g_f1]), _ag_plan([s_o])])
    (y_lru, h, lru_saved), got = _lru_fwd(
        proj, conv_w_full, conv_b, wa_bd, lru_b_a, wx_bd, lru_b_x, lru_lambda, carry=plan)
    (g_f1,), (g_o,) = plan.split(got)
    w_lu, w_og = _rows(g_lu), _rows(g_o)
    y_pool, p = _pool_fwd(proj, pw_bf, pool_scale, pool_block)
    (br_a, br_b, mix), (g_f2,) = _branch_mix(
        y_lru, y_pool, w_lu, g_pu, proj, b_gate, ga_block, gb_block,
        carry=_ag_plan([s_f2], pieces=[(0, f2_cut)]))
    (m, x2, h3), _ = _wo_norm(mix, w_og, x2d, norm_mix_post, norm_mlp_pre)
    (rf,), (g_f2,) = _ff1(
        h3, g_f1, carry=_ag_plan([s_f2], pieces=[(f2_cut, r_f2 - f2_cut)], bufs=[g_f2]))
    w_f2 = _rows(g_f2)
    dy, df, dg4, loss_part = _ff2_loss(rf, w_f2, x2, norm_mlp_post, tgt)

    (gw_ff2_32, gw_ff2_16), _ = _wgrad(rf, df, "wgrad_ff2", square_a=True)
    (d_f1,), r1_ff2 = _ff2_bwd(df, w_f2, rf, carry=_rs_sibling_plan([gw_ff2_16]))
    ((own_ff2, send_ff2),) = _rs_sums([gw_ff2_32], r1_ff2, "ff2")
    cut2 = (5 * send_ff2.shape[1]) // 16
    (gw_ff1_32, gw_ff1_16), (r2_ff2,) = _wgrad_cols(
        h3, d_f1, s_f1.shape[1], s_f1.shape[1], "wgrad_ff1",
        carry=_rs_chips_plan([send_ff2], pieces=[(0, cut2)]))
    plan = _join([_rs_chips_plan([send_ff2], pieces=[(cut2, send_ff2.shape[1] - cut2)], bufs=[r2_ff2]),
                  _rs_sibling_plan([gw_ff1_16])])
    (dx2, dm, dg3, dg2), got = _ff1_bwd_norms(d_f1, g_f1, dy, x2, norm_mlp_pre, m, norm_mix_post, carry=plan)
    (r2_ff2,), r1_ff1 = plan.split(got)
    ((own_ff1, send_ff1),) = _rs_sums([gw_ff1_32], r1_ff1, "ff1")
    own_ff1, send_ff1 = own_ff1.reshape((1,) + s_f1.shape), send_ff1.reshape((3,) + s_f1.shape)
    cut = send_ff1.shape[1] // 4
    (gw_o_32, gw_o_16), _ = _wgrad(mix, dm, "wgrad_o")
    (d_br_a, d_br_b, p_ga, p_gb, dbg_a, dbg_b), (r2_ff1,) = _wo_bwd_mix(
        dm, w_og, br_a, br_b, proj, b_gate, ga_block, gb_block,
        carry=_rs_chips_plan([send_ff1], pieces=[(0, cut)]))
    (gw_lu_32, gw_lu_16), _ = _wgrad(y_lru, d_br_a, "wgrad_lru_up")
    (gw_pu_32, gw_pu_16), _ = _wgrad_cols(y_pool, d_br_b, s_pu.shape[1], d, "wgrad_pool_up")
    (dh, p_g), r1_mid = _lru_up_bwd(
        d_br_a, w_lu, proj, h, g_block,
        carry=_rs_sibling_plan([gw_o_16, gw_lu_16, gw_pu_16]))
    mid = _rs_sums([gw_o_32, gw_lu_32, gw_pu_32], r1_mid, "mid")
    plan = _join([_rs_chips_plan([send_ff1], pieces=[(cut, send_ff1.shape[1] - cut)], bufs=[r2_ff1]),
                  _rs_chips_plan([mid[0][1]])])
    (p_x, dwa, db_a, dwx, db_x, dlam, dconv_w, dconv_b), got = _lru_bwd(
        dh, h, lru_saved, proj, conv_w_full, wa_bd, wx_bd, lru_lambda, carry=plan)
    (r2_ff1,), (r2_o,) = plan.split(got)
    p_p, dpool_w, dpool_scale = _pool_bwd(d_br_b, g_pu, p, pw, pool_scale)
    parts = [p_x, p_g, p_p, p_ga, p_gb]
    gw_in, (r2_lu, r2_pu) = _wgrad_parts(
        parts, h1, "wgrad_in", carry=_rs_chips_plan([mid[1][1], mid[2][1]]))
    r2_mid = [r2_o, r2_lu, r2_pu]
    tail = _rs_level1([gw_in[0], dpool_w.reshape(N_DEV, -1, POOL_GROUP_DIM), dwa, dwx],
                      [gw_in[1], dpool_w.reshape(N_DEV, -1, POOL_GROUP_DIM), dwa, dwx], "in")
    (grad_x, dg1), r2_tail = _win_bwd_norm(parts, w_int, dx2, x2d, norm_mix_pre,
                                           carry=_rs_chips_plan([s for _, s in tail]))

    def flat2(a):
        return a.reshape(a.shape[0], -1, a.shape[-1])

    fin_small, _ = _finals([
        (flat2(tail[1][0]), flat2(r2_tail[1])), (flat2(tail[2][0]), flat2(r2_tail[2])),
        (flat2(tail[3][0]), flat2(r2_tail[3])),
    ], "rs_finals_small")

    def pad_row(a):
        return jnp.pad(a, ((0, 0), (0, d - a.shape[1])))

    vecs = jnp.concatenate([dg1, dg2, dg3, dg4, dbg_a, dbg_b, dconv_b, db_a, db_x, dlam,
                            pad_row(dpool_scale), dconv_w, pad_row(loss_part)], axis=0)
    assert vecs.shape[0] == N_VEC_ROWS
    vec_parts, g_pool, g_wa, g_wx = _run_plan(_ag_plan([vecs] + fin_small), "ag_tail")

    big_names = ["w_in", "w_lru_up", "w_pool_up", "w_o", "w_ff1", "w_ff2"]
    big_w = [w_in[0].T, w_lru_up[0], w_pool_up[0], w_o[0], w_ff1[0], w_ff2[0]]
    big_g = [(tail[0][0], r2_tail[0]), (mid[1][0], r2_mid[1]),
             (mid[2][0].reshape((1,) + s_pu.shape), r2_mid[2].reshape((3,) + s_pu.shape)),
             (mid[0][0], r2_mid[0]), (own_ff1, r2_ff1), (own_ff2, r2_ff2)]
    big_m = [m_w_in[0].T, m_w_lru_up[0], m_w_pool_up[0], m_w_o[0], m_w_ff1[0], m_w_ff2[0]]
    big_v = [v_w_in[0].T, v_w_lru_up[0], v_w_pool_up[0], v_w_o[0], v_w_ff1[0], v_w_ff2[0]]
    big_out = _adamw_big(big_w, big_g, big_m, big_v)
    big_out[0] = tuple(o.T for o in big_out[0])

    small = dict(
        norm_mix_pre=(norm_mix_pre, m_norm_mix_pre, v_norm_mix_pre),
        norm_mix_post=(norm_mix_post, m_norm_mix_post, v_norm_mix_post),
        norm_mlp_pre=(norm_mlp_pre, m_norm_mlp_pre, v_norm_mlp_pre),
        norm_mlp_post=(norm_mlp_post, m_norm_mlp_post, v_norm_mlp_post),
        b_gate=(b_gate, m_b_gate, v_b_gate), conv_w=(conv_w, m_conv_w, v_conv_w),
        conv_b=(conv_b, m_conv_b, v_conv_b), lru_w_a=(lru_w_a, m_lru_w_a, v_lru_w_a),
        lru_b_a=(lru_b_a, m_lru_b_a, v_lru_b_a), lru_w_x=(lru_w_x, m_lru_w_x, v_lru_w_x),
        lru_b_x=(lru_b_x, m_lru_b_x, v_lru_b_x), lru_lambda=(lru_lambda, m_lru_lambda, v_lru_lambda),
        pool_w=(pool_w, m_pool_w, v_pool_w), pool_scale=(pool_scale, m_pool_scale, v_pool_scale))
    loss_row, small_out = _adamw_small(
        vec_parts, g_pool.reshape(pool_w.shape), g_wa.reshape(lru_w_a.shape), g_wx.reshape(lru_w_x.shape),
        jnp.reshape(me, (1,)).astype(jnp.int32), small)
    grads = {n: o[0] for n, o in small_out.items()}
    delta = {n: o[1] for n, o in small_out.items()}
    new_m = {n: o[2] for n, o in small_out.items()}
    new_v = {n: o[3] for n, o in small_out.items()}

    for name, (g, dl, nm, nv) in zip(big_names, big_out):
        grads[name], delta[name], new_m[name], new_v[name] = g[None], dl[None], nm[None], nv[None]

    loss = loss_row[0, 0]
    order = ["norm_mix_pre", "norm_mix_post", "norm_mlp_pre", "norm_mlp_post", "w_in", "b_gate", "conv_w",
             "conv_b", "lru_w_a", "lru_b_a", "lru_w_x", "lru_b_x", "lru_lambda", "pool_w", "pool_scale",
             "w_lru_up", "w_pool_up", "w_o", "w_ff1", "w_ff2"]
    return (loss, grad_x[None], *[grads[n] for n in order], *[delta[n] for n in order],
            *[new_m[n] for n in order], *[new_v[n] for n in order])
```

```python
import functools
import math
import operator
import types

import jax
import jax.numpy as jnp
from jax import lax
from jax.experimental import pallas as pl
from jax.experimental.pallas import tpu as pltpu

F32 = jnp.float32
BF16 = jnp.bfloat16
NORM_EPS = 1e-6
LRU_C = 8.0
N_LRU_HEADS = 16
LRU_HEAD_DIM = 64
POOL_WINDOWS = (2, 4, 8, 16)
POOL_GROUP_DIM = 128
ADAM_LR = 0.001
ADAM_B1 = 0.9
ADAM_B2 = 0.999
ADAM_EPS = 1e-08
ADAM_WD = 0.01
ADAM_STEP = 10
N_DEV = 8
V7X_VMEM_LIMIT_BYTES = 56 * 1024 * 1024
LRU_CB = 256
MESH = pl.DeviceIdType.MESH
ANY = pl.BlockSpec(memory_space=pl.ANY)


def _tile(n, pref):
    t = min(n, pref)
    assert n % t == 0, (n, pref)
    return t


def _dot_nn(a, b):
    return lax.dot_general(a, b, (((1,), (0,)), ((), ())), preferred_element_type=F32)


def _dot_nt(a, b):
    return lax.dot_general(a, b, (((1,), (1,)), ((), ())), preferred_element_type=F32)


def _dot_tn(a, b):
    return lax.dot_general(a, b, (((0,), (0,)), ((), ())), preferred_element_type=F32)


def _row_chunks(n_rows, fn, chunk=256):
    chunk = min(chunk, n_rows)
    assert n_rows % chunk == 0

    def step(r, carry):
        fn(pl.ds(pl.multiple_of(r * chunk, chunk), chunk))
        return carry

    lax.fori_loop(0, n_rows // chunk, step, 0)


def _late_copies(i, tt, pairs, sems):
    rows = pl.ds(pl.multiple_of(i * tt, tt), tt)
    return [pltpu.make_async_copy(hbm.at[rows], buf, sems.at[j]) for j, (hbm, buf) in enumerate(pairs)]


def _sig(x):
    return 1.0 / (1.0 + jnp.exp(-x))


def _rms_hat(x):
    r = lax.rsqrt(jnp.mean(x * x, axis=-1, keepdims=True) + NORM_EPS)
    return x * r, r


def _rms_bwd(dn, xhat, r, g):
    q = dn * g
    dx = r * (q - xhat * jnp.mean(q * xhat, axis=-1, keepdims=True))
    dg = jnp.sum(dn * xhat, axis=0, keepdims=True)
    return dx, dg


_GELU_K = math.sqrt(2.0 / math.pi)
_GELU_C = 0.044715


def _gelu_and_grad(g):
    t = jnp.tanh(_GELU_K * (g + _GELU_C * g * g * g))
    val = 0.5 * g * (1.0 + t)
    grad = 0.5 * (1.0 + t) + 0.5 * g * (1.0 - t * t) * (_GELU_K * (1.0 + 3.0 * _GELU_C * g * g))
    return val, grad


def _softplus_neg(lam):
    z = -lam
    e = jnp.exp(-jnp.abs(z))
    u = 1.0 + e
    d = u - 1.0
    l1p = jnp.where(d == 0.0, e, jnp.log(u) * (e / jnp.where(d == 0.0, 1.0, d)))
    return jnp.maximum(z, 0.0) + l1p


def _lru_gates(xc, wa, ba, wx, bx, lam):
    xcb = xc.astype(BF16)
    r = _sig(_dot_nn(xcb, wa) + ba)
    i = _sig(_dot_nn(xcb, wx) + bx)
    sp = _softplus_neg(lam)
    log_a = (-LRU_C) * r * sp
    a = jnp.exp(log_a)
    mult = jnp.sqrt(-jnp.tanh(log_a) * (1.0 + a * a))
    return xcb, r, i, sp, log_a, a, mult


def _place():
    return lax.axis_index("x"), lax.axis_index("y"), lax.axis_index("c")


def _ag_plan(shards, pieces=None, bufs=None):
    na = len(shards)
    n_kinds = 7

    def parts(ins, outs, sems):
        send_sems, recv_sems, local_sems = sems
        x, y, c = _place()
        me, sibling = (x, y, c), (x, y, 1 - c)
        x_nb, y_nb, diag = (1 - x, y), (x, 1 - y), (1 - x, 1 - y)
        relay_src = (c * (1 - x) + (1 - c) * x, c * y + (1 - c) * (1 - y))
        relay_dst = (c * x + (1 - c) * (1 - x), c * (1 - y) + (1 - c) * y)

        def own(a):
            return ins[a] if pieces is None else ins[a].at[pl.ds(*pieces[a])]

        def slot(a, px, py, pc):
            idx = 4 * px + 2 * py + pc
            return outs[a].at[idx] if pieces is None else outs[a].at[idx, pl.ds(*pieces[a])]

        def copy(a, k, block, to, src=None):
            return pltpu.make_async_remote_copy(
                src_ref=slot(a, *block) if src is None else src,
                dst_ref=slot(a, *block),
                send_sem=send_sems.at[a * n_kinds + k],
                recv_sem=recv_sems.at[a * n_kinds + k],
                device_id=to,
                device_id_type=MESH,
            )

        mine = [pltpu.make_async_copy(own(a), slot(a, *me), local_sems.at[a]) for a in range(na)]
        first, second, third = [], [], []
        for a in range(na):
            first += [copy(a, 0, me, sibling, src=own(a)), copy(a, 1, me, (*x_nb, c), src=own(a)),
                      copy(a, 2, me, (*y_nb, c), src=own(a))]
            second += [copy(a, 3, (*relay_src, c), (*relay_dst, c)), copy(a, 4, (*x_nb, c), sibling),
                       copy(a, 5, (*y_nb, c), sibling)]
            third.append(copy(a, 6, (*diag, c), sibling))
        return sibling, c, x_nb, y_nb, diag, copy, mine, first, second, third

    def start(ins, outs, sems):
        _, _, _, _, _, _, mine, first, _, _ = parts(ins, outs, sems)
        for cp in mine + first:
            cp.start()

    def middle(ins, outs, sems):
        _, c, x_nb, y_nb, _, copy, _, _, second, _ = parts(ins, outs, sems)
        for a in range(na):
            copy(a, 1, (*x_nb, c), (*x_nb, c)).wait_recv()
            copy(a, 2, (*y_nb, c), (*y_nb, c)).wait_recv()
        for cp in second:
            cp.start()

    def finish(ins, outs, sems):
        sibling, c, x_nb, y_nb, diag, copy, mine, first, second, third = parts(ins, outs, sems)
        for a in range(na):
            copy(a, 3, (*diag, c), (*diag, c)).wait_recv()
            third[a].start()
        for a in range(na):
            copy(a, 0, sibling, sibling).wait_recv()
            copy(a, 4, (*x_nb, 1 - c), sibling).wait_recv()
            copy(a, 5, (*y_nb, 1 - c), sibling).wait_recv()
            copy(a, 6, (*diag, 1 - c), sibling).wait_recv()
        for cp in first + second + third:
            cp.wait_send()
        for cp in mine:
            cp.wait()

    return types.SimpleNamespace(
        ins=list(shards) + list(bufs or []),
        out_shapes=[jax.ShapeDtypeStruct((N_DEV,) + s.shape, s.dtype) for s in shards],
        sems=[pltpu.SemaphoreType.DMA((n_kinds * na,)), pltpu.SemaphoreType.DMA((n_kinds * na,)),
              pltpu.SemaphoreType.DMA((na,))],
        aliases=[(na + a, a) for a in range(na)] if bufs else [],
        peers=frozenset({"sibling", "neighbours"}), start=start, middle=middle, finish=finish)


def _rs_sibling_plan(fulls):
    na = len(fulls)
    rs = [f.shape[0] // N_DEV for f in fulls]

    def copies(ins, outs, sems):
        send_sems, recv_sems = sems
        x, y, c = _place()
        out = []
        for a in range(na):
            for q in range(4):
                shard = 2 * q + (1 - c)
                out.append(pltpu.make_async_remote_copy(
                    src_ref=ins[a].at[pl.ds(shard * rs[a], rs[a])],
                    dst_ref=outs[a].at[q],
                    send_sem=send_sems.at[a * 4 + q],
                    recv_sem=recv_sems.at[a * 4 + q],
                    device_id=(x, y, 1 - c),
                    device_id_type=MESH,
                ))
        return out

    def start(ins, outs, sems):
        for cp in copies(ins, outs, sems):
            cp.start()

    def finish(ins, outs, sems):
        for cp in copies(ins, outs, sems):
            cp.wait()

    return types.SimpleNamespace(
        ins=list(fulls),
        out_shapes=[jax.ShapeDtypeStruct((4, r) + f.shape[1:], f.dtype) for r, f in zip(rs, fulls)],
        sems=[pltpu.SemaphoreType.DMA((4 * na,)), pltpu.SemaphoreType.DMA((4 * na,))],
        peers=frozenset({"sibling"}), start=start, finish=finish)


def _rs_chips_plan(sends, pieces=None, bufs=None):
    na = len(sends)

    def copies(ins, outs, sems):
        send_sems, recv_sems = sems
        x, y, c = _place()
        chips = [(1 - x, y), (x, 1 - y), (1 - x, 1 - y)]
        out = []
        for a in range(na):
            for k, chip in enumerate(chips):
                rows = (k,) if pieces is None else (k, pl.ds(*pieces[a]))
                out.append(pltpu.make_async_remote_copy(
                    src_ref=ins[a].at[rows],
                    dst_ref=outs[a].at[rows],
                    send_sem=send_sems.at[a * 3 + k],
                    recv_sem=recv_sems.at[a * 3 + k],
                    device_id=(*chip, c),
                    device_id_type=MESH,
                ))
        return out

    def start(ins, outs, sems):
        for cp in copies(ins, outs, sems):
            cp.start()

    def finish(ins, outs, sems):
        for cp in copies(ins, outs, sems):
            cp.wait()

    return types.SimpleNamespace(
        ins=list(sends) + list(bufs or []),
        out_shapes=[jax.ShapeDtypeStruct(s.shape, s.dtype) for s in sends],
        sems=[pltpu.SemaphoreType.DMA((3 * na,)), pltpu.SemaphoreType.DMA((3 * na,))],
        aliases=[(na + a, a) for a in range(na)] if bufs else [],
        peers=frozenset({"chips"}), start=start, finish=finish)


def _join(plans):
    ins, outs, sems, aliases, offs = [], [], [], [], []
    for p in plans:
        offs.append((len(ins), len(outs), len(sems)))
        aliases += [(len(ins) + ci, len(outs) + co) for ci, co in getattr(p, "aliases", [])]
        ins += p.ins
        outs += p.out_shapes
        sems += p.sems

    def cut(p, off, i, o, s):
        return (i[off[0]:off[0] + len(p.ins)], o[off[1]:off[1] + len(p.out_shapes)],
                s[off[2]:off[2] + len(p.sems)])

    def start(i, o, s):
        for p, off in zip(plans, offs):
            p.start(*cut(p, off, i, o, s))

    def middle(i, o, s):
        for p, off in zip(plans, offs):
            if getattr(p, "middle", None) is not None:
                p.middle(*cut(p, off, i, o, s))

    def finish(i, o, s):
        for p, off in zip(plans, offs):
            p.finish(*cut(p, off, i, o, s))

    def split(results):
        return [list(results[off[1]:off[1] + len(p.out_shapes)]) for p, off in zip(plans, offs)]

    return types.SimpleNamespace(ins=ins, out_shapes=outs, sems=sems, aliases=aliases,
                                 peers=frozenset().union(*[p.peers for p in plans]),
                                 start=start, middle=middle, finish=finish, split=split)


COLLECTIVE_ID = {frozenset({"sibling"}): 0, frozenset({"chips"}): 1, frozenset({"sibling", "chips"}): 2,
                 frozenset({"sibling", "neighbours"}): 3}


def _handshake(peers):
    x, y, c = _place()
    devs = []
    if "sibling" in peers:
        devs.append((x, y, 1 - c))
    if "neighbours" in peers:
        devs += [(1 - x, y, c), (x, 1 - y, c)]
    if "chips" in peers:
        assert "neighbours" not in peers
        devs += [(1 - x, y, c), (x, 1 - y, c), (1 - x, 1 - y, c)]
    barrier = pltpu.get_barrier_semaphore()
    for dev in devs:
        pl.semaphore_signal(barrier, inc=1, device_id=dev, device_id_type=MESH)
    pl.semaphore_wait(barrier, len(devs))


def _in_hbm(args):
    return [pltpu.with_memory_space_constraint(a, pltpu.HBM) for a in args]


def _run_plan(plan, name):
    n_in, n_out = len(plan.ins), len(plan.out_shapes)

    def body(*refs):
        ins, outs, sems = refs[:n_in], refs[n_in:n_in + n_out], refs[n_in + n_out:]
        _handshake(plan.peers)
        plan.start(ins, outs, sems)
        if getattr(plan, "middle", None) is not None:
            plan.middle(ins, outs, sems)
        plan.finish(ins, outs, sems)

    return pl.pallas_call(
        body,
        name=name,
        in_specs=[ANY] * n_in,
        out_specs=[ANY] * n_out,
        out_shape=plan.out_shapes,
        scratch_shapes=plan.sems,
        input_output_aliases=dict(getattr(plan, "aliases", [])),
        compiler_params=pltpu.CompilerParams(collective_id=COLLECTIVE_ID[plan.peers]),
    )(*_in_hbm(plan.ins))


def _call(body, *, name, grid, in_specs, out_specs, out_shape, args, scratch_shapes=(), aliases=None,
          carry=None):
    n_in, n_out, n_scr = len(in_specs), len(out_shape), len(scratch_shapes)
    params = pltpu.CompilerParams(
        dimension_semantics=("arbitrary",) * len(grid), vmem_limit_bytes=V7X_VMEM_LIMIT_BYTES)
    if carry is None:
        outs = pl.pallas_call(
            body, name=name, grid=grid, in_specs=list(in_specs), out_specs=list(out_specs),
            out_shape=list(out_shape), scratch_shapes=list(scratch_shapes),
            input_output_aliases=aliases or {}, compiler_params=params)(*_in_hbm(args))
        return list(outs), []
    c_in, c_out = len(carry.ins), len(carry.out_shapes)

    def full(*refs):
        p = 0
        ins = refs[p:p + n_in]
        p += n_in
        cins = refs[p:p + c_in]
        p += c_in
        outs = refs[p:p + n_out]
        p += n_out
        couts = refs[p:p + c_out]
        p += c_out
        scr = refs[p:p + n_scr]
        csems = refs[p + n_scr:]
        ids = [pl.program_id(a) for a in range(len(grid))]
        first = functools.reduce(operator.and_, [i == 0 for i in ids])
        last = functools.reduce(operator.and_, [i == g - 1 for i, g in zip(ids, grid)])

        @pl.when(first)
        def _():
            _handshake(carry.peers)
            carry.start(cins, couts, csems)

        if getattr(carry, "middle", None) is not None:
            n_steps = math.prod(grid)
            flat = functools.reduce(lambda acc, ig: acc * ig[1] + ig[0], zip(ids, grid), 0)

            @pl.when(flat == (2 * n_steps) // 3)
            def _():
                carry.middle(cins, couts, csems)

        body(*ins, *outs, *scr)

        @pl.when(last)
        def _():
            carry.finish(cins, couts, csems)

    all_aliases = dict(aliases or {})
    all_aliases.update({n_in + ci: n_out + co for ci, co in getattr(carry, "aliases", [])})
    params = pltpu.CompilerParams(
        dimension_semantics=("arbitrary",) * len(grid), vmem_limit_bytes=V7X_VMEM_LIMIT_BYTES,
        collective_id=COLLECTIVE_ID[carry.peers])
    outs = pl.pallas_call(
        full, name=name, grid=grid,
        in_specs=list(in_specs) + [ANY] * c_in,
        out_specs=list(out_specs) + [ANY] * c_out,
        out_shape=list(out_shape) + list(carry.out_shapes),
        scratch_shapes=list(scratch_shapes) + list(carry.sems),
        input_output_aliases=all_aliases, compiler_params=params)(*_in_hbm(args), *_in_hbm(carry.ins))
    return list(outs[:n_out]), list(outs[n_out:])


def _norm_proj(x, g1, w_int, carry=None):
    t, d = x.shape
    n = w_int.shape[0]
    tt, tn = _tile(t, 2048), _tile(n, 512)

    def body(x_ref, g_ref, w_ref, proj_ref, h1_ref, h1_s):
        @pl.when(pl.program_id(1) == 0)
        def _():
            def norm_rows(rows):
                xhat, _ = _rms_hat(x_ref[rows, :])
                h = (xhat * g_ref[...]).astype(BF16)
                h1_s[rows, :] = h
                h1_ref[rows, :] = h

            _row_chunks(tt, norm_rows)

        proj_ref[...] = _dot_nt(h1_s[...], w_ref[...]).astype(BF16)

    return _call(
        body, name="norm_proj", grid=(t // tt, n // tn),
        in_specs=[
            pl.BlockSpec((tt, d), lambda i, j: (i, 0)),
            pl.BlockSpec((1, d), lambda i, j: (0, 0)),
            pl.BlockSpec((tn, d), lambda i, j: (j, 0)),
        ],
        out_specs=[
            pl.BlockSpec((tt, tn), lambda i, j: (i, j)),
            pl.BlockSpec((tt, d), lambda i, j: (i, 0)),
        ],
        out_shape=[jax.ShapeDtypeStruct((t, n), BF16), jax.ShapeDtypeStruct((t, d), BF16)],
        scratch_shapes=[pltpu.VMEM((tt, d), BF16)],
        args=(x, g1, w_int), carry=carry)


def _scan_rows(av, bv, reverse):
    tc = av.shape[0]
    row = lax.broadcasted_iota(jnp.int32, av.shape, 0)
    s = 1
    while s < tc:
        if s < 8:
            keep = (row < tc - s) if reverse else (row >= s)
            shift = (tc - s) if reverse else s
            a_sh = jnp.where(keep, pltpu.roll(av, shift, 0), 1.0)
            b_sh = jnp.where(keep, pltpu.roll(bv, shift, 0), 0.0)
            bv = av * b_sh + bv
            av = av * a_sh
        elif reverse:
            bv = jnp.concatenate([av[:tc - s] * bv[s:] + bv[:tc - s], bv[tc - s:]], axis=0)
            av = jnp.concatenate([av[:tc - s] * av[s:], av[tc - s:]], axis=0)
        else:
            bv = jnp.concatenate([bv[:s], av[s:] * bv[:tc - s] + bv[s:]], axis=0)
            av = jnp.concatenate([av[:s], av[s:] * av[:tc - s]], axis=0)
        s *= 2
    return av, bv


N_LRU_SAVED = 5


def _fill_block_diag(w_ref, bd_ref):
    bd_ref[...] = jnp.zeros_like(bd_ref)
    hd = LRU_HEAD_DIM
    for k in range(w_ref.shape[0]):
        bd_ref[k * hd:(k + 1) * hd, k * hd:(k + 1) * hd] = w_ref[k].astype(BF16)


def _lru_fwd(proj, conv_w, conv_b, w_a, b_a, w_x, b_x, lam, carry=None):
    t = proj.shape[0]
    dr = conv_b.shape[1]
    cb = LRU_CB
    tc = _tile(t, 256)
    ncb, ntc = dr // cb, t // tc

    def body(xp_ref, g_ref, cw_ref, cb_ref, wa_ref, ba_ref, wx_ref, bx_ref, lam_ref,
             y_ref, h_ref, saved_ref, prevx_s, hlast_s, wa_s, wx_s):
        c = pl.program_id(1)

        @pl.when(c == 0)
        def _():
            prevx_s[...] = jnp.zeros_like(prevx_s)
            hlast_s[...] = jnp.zeros_like(hlast_s)
            _fill_block_diag(wa_ref, wa_s)
            _fill_block_diag(wx_ref, wx_s)

        x = xp_ref[...].astype(F32)
        prev = prevx_s[...]
        row = lax.broadcasted_iota(jnp.int32, x.shape, 0)

        def sh(j):
            return jnp.where(row >= j, pltpu.roll(x, j, 0), pltpu.roll(prev, j, 0))

        xc = (cb_ref[...] + cw_ref[0:1, :] * sh(3) + cw_ref[1:2, :] * sh(2)
              + cw_ref[2:3, :] * sh(1) + cw_ref[3:4, :] * x)
        prevx_s[...] = x
        _, r, i, _, _, a, mult = _lru_gates(xc, wa_s[...], ba_ref[...], wx_s[...], bx_ref[...],
                                            lam_ref[...])
        for k, val in enumerate((xc, r, i, a, mult)):
            saved_ref[:, k * cb:(k + 1) * cb] = val
        av, bv = _scan_rows(a, mult * (i * xc), reverse=False)
        h = av * hlast_s[...] + bv
        h_ref[...] = h
        hlast_s[...] = h_ref[tc - 1:tc, :]
        gel, _ = _gelu_and_grad(g_ref[...].astype(F32))
        y_ref[...] = (h * gel).astype(BF16)

    vec = pl.BlockSpec((1, cb), lambda j, c: (0, j))
    blk = pl.BlockSpec((tc, cb), lambda j, c: (c, j))
    mat = pl.BlockSpec((cb // LRU_HEAD_DIM, LRU_HEAD_DIM, LRU_HEAD_DIM), lambda j, c: (j, 0, 0))
    return _call(
        body, name="lru_fwd", grid=(ncb, ntc),
        in_specs=[
            blk,
            pl.BlockSpec((tc, cb), lambda j, c: (c, ncb + j)),
            pl.BlockSpec((4, cb), lambda j, c: (0, j)),
            vec, mat, vec, mat, vec, vec,
        ],
        out_specs=[blk, blk, pl.BlockSpec((tc, N_LRU_SAVED * cb), lambda j, c: (c, j))],
        out_shape=[jax.ShapeDtypeStruct((t, dr), BF16), jax.ShapeDtypeStruct((t, dr), F32),
                   jax.ShapeDtypeStruct((t, N_LRU_SAVED * dr), F32)],
        scratch_shapes=[pltpu.VMEM((tc, cb), F32), pltpu.VMEM((1, cb), F32),
                        pltpu.VMEM((cb, cb), BF16), pltpu.VMEM((cb, cb), BF16)],
        args=(proj, proj, conv_w, conv_b, w_a, b_a, w_x, b_x, lam), carry=carry)


def _pool_select(col, vals):
    out = vals[3]
    for g in (2, 1, 0):
        out = jnp.where(col < (g + 1) * POOL_GROUP_DIM, vals[g], out)
    return out


def _pool_fwd(proj, pool_w, pool_scale, col_block):
    t = proj.shape[0]
    dp = pool_scale.shape[1]
    tc = _tile(t, 256)
    ntc = t // tc

    def body(x_ref, w_ref, sc_ref, y_ref, p_ref, px, p2, p4, p8):
        c = pl.program_id(0)

        @pl.when(c == 0)
        def _():
            for s in (px, p2, p4, p8):
                s[...] = jnp.zeros_like(s)

        x = x_ref[...].astype(F32)
        row = lax.broadcasted_iota(jnp.int32, x.shape, 0)
        col = lax.broadcasted_iota(jnp.int32, x.shape, 1)

        def sh(v, pv, j):
            return jnp.where(row >= j, pltpu.roll(v, j, 0), pltpu.roll(pv[...], j, 0))

        s2 = x + sh(x, px, 1)
        s4 = s2 + sh(s2, p2, 2)
        s8 = s4 + sh(s4, p4, 4)
        s16 = s8 + sh(s8, p8, 8)
        px[...] = x
        p2[...] = s2
        p4[...] = s4
        p8[...] = s8
        wsum = _pool_select(col, (s2, s4, s8, s16))
        win = _pool_select(col, POOL_WINDOWS)
        cnt = jnp.minimum(c * tc + row + 1, win).astype(F32)
        p = wsum / cnt - x
        pb = p.astype(BF16)
        p_ref[...] = pb
        for g in range(len(POOL_WINDOWS)):
            sl = slice(g * POOL_GROUP_DIM, (g + 1) * POOL_GROUP_DIM)
            yg = _dot_nn(pb[:, sl], w_ref[g]) * sc_ref[:, sl]
            y_ref[:, sl] = yg.astype(BF16)

    return _call(
        body, name="pool_fwd", grid=(ntc,),
        in_specs=[
            pl.BlockSpec((tc, dp), lambda c: (c, col_block)),
            pl.BlockSpec(pool_w.shape, lambda c: (0, 0, 0)),
            pl.BlockSpec((1, dp), lambda c: (0, 0)),
        ],
        out_specs=[pl.BlockSpec((tc, dp), lambda c: (c, 0))] * 2,
        out_shape=[jax.ShapeDtypeStruct((t, dp), BF16)] * 2,
        scratch_shapes=[pltpu.VMEM((tc, dp), F32)] * 4,
        args=(proj, pool_w, pool_scale))[0]


def _branch_mix(y_lru, y_pool, w_lru_up, w_pool_upb, proj, b_gate, ga_block, gb_block, carry=None):
    t, d = y_lru.shape
    dp = y_pool.shape[1]
    bw = w_pool_upb.shape[2]
    tt, tn = _tile(t, 1024), 512
    nj = d // tn

    def body(yl_ref, yp_ref, wl_ref, wp_ref, ga_ref, gb_ref, ba_ref, bb_ref, bra_ref, brb_ref, mix_ref):
        br_a = _dot_nn(yl_ref[...], wl_ref[...])
        wp = jnp.concatenate([wp_ref[b] for b in range(tn // bw)], axis=1)
        br_b = _dot_nn(yp_ref[...], wp)
        bra_ref[...] = br_a.astype(BF16)
        brb_ref[...] = br_b.astype(BF16)
        ga = _sig(ga_ref[...].astype(F32) + ba_ref[...])
        gb = _sig(gb_ref[...].astype(F32) + bb_ref[...])
        mix_ref[...] = (ga * br_a + gb * br_b).astype(BF16)

    out = pl.BlockSpec((tt, tn), lambda j, i: (i, j))
    return _call(
        body, name="branch_mix", grid=(nj, t // tt),
        in_specs=[
            pl.BlockSpec((tt, d), lambda j, i: (i, 0)),
            pl.BlockSpec((tt, dp), lambda j, i: (i, 0)),
            pl.BlockSpec((d, tn), lambda j, i: (0, j)),
            pl.BlockSpec((tn // bw, dp, bw), lambda j, i: (j, 0, 0)),
            pl.BlockSpec((tt, tn), lambda j, i: (i, ga_block + j)),
            pl.BlockSpec((tt, tn), lambda j, i: (i, gb_block + j)),
            pl.BlockSpec((1, tn), lambda j, i: (0, j)),
            pl.BlockSpec((1, tn), lambda j, i: (0, nj + j)),
        ],
        out_specs=[out, out, out],
        out_shape=[jax.ShapeDtypeStruct((t, d), BF16)] * 3,
        args=(y_lru, y_pool, w_lru_up, w_pool_upb, proj, proj, b_gate, b_gate), carry=carry)


def _wo_norm(mix, w_o, x, g2, g3, carry=None):
    t, d = x.shape
    tt = _tile(t, 512)

    def body(mix_ref, w_ref, x_ref, g2_ref, g3_ref, m_ref, x2_ref, h3_ref):
        m = _dot_nn(mix_ref[...], w_ref[...])
        m_ref[...] = m
        mhat, _ = _rms_hat(m)
        x2 = x_ref[...] + mhat * g2_ref[...]
        x2_ref[...] = x2
        xhat, _ = _rms_hat(x2)
        h3_ref[...] = (xhat * g3_ref[...]).astype(BF16)

    row = pl.BlockSpec((tt, d), lambda i: (i, 0))
    vec = pl.BlockSpec((1, d), lambda i: (0, 0))
    return _call(
        body, name="wo_norm", grid=(t // tt,),
        in_specs=[row, pl.BlockSpec((d, d), lambda i: (0, 0)), row, vec, vec],
        out_specs=[row, row, row],
        out_shape=[
            jax.ShapeDtypeStruct((t, d), F32),
            jax.ShapeDtypeStruct((t, d), F32),
            jax.ShapeDtypeStruct((t, d), BF16),
        ],
        args=(mix, w_o, x, g2, g3), carry=carry)


def _ff1(h3, w_ff1b, carry=None):
    t, d = h3.shape
    nb, _, tn = w_ff1b.shape
    tt = _tile(t, 2048)

    def body(h_ref, w_ref, rf_ref):
        rf_ref[...] = jnp.maximum(_dot_nn(h_ref[...], w_ref[...]), 0.0).astype(BF16)

    out = pl.BlockSpec((tt, tn), lambda i, j: (i, j))
    return _call(
        body, name="ff1", grid=(t // tt, nb),
        in_specs=[pl.BlockSpec((tt, d), lambda i, j: (i, 0)), pl.BlockSpec((None, d, tn), lambda i, j: (j, 0, 0))],
        out_specs=[out],
        out_shape=[jax.ShapeDtypeStruct((t, nb * tn), BF16)],
        args=(h3, w_ff1b), carry=carry)


def _ff2_loss(rf, w_ff2, x2, g4, target):
    t, k = rf.shape
    d = x2.shape[1]
    tt, tk = _tile(t, 1024), _tile(k, 1024)
    nk = k // tk

    def body(a_ref, w_ref, x2_ref, g_ref, tg_ref, dy_ref, df_ref, dg_ref, loss_ref, acc):
        i, kk = pl.program_id(0), pl.program_id(1)

        @pl.when(kk == 0)
        def _():
            acc[...] = jnp.zeros_like(acc)

        @pl.when((i == 0) & (kk == 0))
        def _():
            dg_ref[...] = jnp.zeros_like(dg_ref)
            loss_ref[...] = jnp.zeros_like(loss_ref)

        rf_tile = a_ref[...]
        acc[...] += _dot_nn(rf_tile * rf_tile, w_ref[...])

        @pl.when(kk == nk - 1)
        def _():
            def tail(rows):
                fhat, r = _rms_hat(acc[rows, :])
                g = g_ref[...]
                e = x2_ref[rows, :] + fhat * g - tg_ref[rows, :]
                loss_ref[...] += 0.5 * jnp.sum(jnp.mean(e * e, axis=-1, keepdims=True))
                dy = e * (1.0 / d)
                dy_ref[rows, :] = dy.astype(BF16)
                df, dg = _rms_bwd(dy, fhat, r, g)
                df_ref[rows, :] = df.astype(BF16)
                dg_ref[...] += dg

            _row_chunks(tt, tail)

    row = pl.BlockSpec((tt, d), lambda i, kk: (i, 0))
    vec = pl.BlockSpec((1, d), lambda i, kk: (0, 0))
    return _call(
        body, name="ff2_loss", grid=(t // tt, nk),
        in_specs=[
            pl.BlockSpec((tt, tk), lambda i, kk: (i, kk)),
            pl.BlockSpec((tk, d), lambda i, kk: (kk, 0)),
            row, vec, row,
        ],
        out_specs=[row, row, vec, pl.BlockSpec((1, 128), lambda i, kk: (0, 0))],
        out_shape=[
            jax.ShapeDtypeStruct((t, d), BF16),
            jax.ShapeDtypeStruct((t, d), BF16),
            jax.ShapeDtypeStruct((1, d), F32),
            jax.ShapeDtypeStruct((1, 128), F32),
        ],
        scratch_shapes=[pltpu.VMEM((tt, d), F32)],
        args=(rf, w_ff2, x2, g4, target))[0]


def _ff2_bwd(df, w_ff2, rf, carry=None):
    t, d = df.shape
    n = w_ff2.shape[0]
    tt, tn = _tile(t, 2048), _tile(n, 512)

    def body(df_ref, w_ref, rf_ref, out_ref):
        d_act = _dot_nt(df_ref[...], w_ref[...])
        out_ref[...] = (d_act * (2.0 * rf_ref[...].astype(F32))).astype(BF16)

    blk = pl.BlockSpec((tt, tn), lambda i, j: (i, j))
    return _call(
        body, name="ff2_bwd", grid=(t // tt, n // tn),
        in_specs=[pl.BlockSpec((tt, d), lambda i, j: (i, 0)), pl.BlockSpec((tn, d), lambda i, j: (j, 0)), blk],
        out_specs=[blk],
        out_shape=[jax.ShapeDtypeStruct((t, n), BF16)],
        args=(df, w_ff2, rf), carry=carry)


def _wgrad(a, b, name, prev=None, row_off=0, rows=None, carry=None, square_a=False):
    t, m = a.shape
    n = b.shape[1]
    rows = m if rows is None else rows
    tm, tk = _tile(m, 512), _tile(t, 2048)
    nk = t // tk
    assert row_off % tm == 0
    off = row_off // tm

    def body(*refs):
        a_ref, b_ref = refs[0], refs[1]
        o32_ref, o16_ref, acc = refs[-3], refs[-2], refs[-1]
        kk = pl.program_id(1)

        @pl.when(kk == 0)
        def _():
            acc[...] = jnp.zeros_like(acc)

        a_tile = a_ref[...]
        acc[...] += _dot_tn(a_tile * a_tile if square_a else a_tile, b_ref[...])

        @pl.when(kk == nk - 1)
        def _():
            o32_ref[...] = acc[...]
            o16_ref[...] = acc[...].astype(BF16)

    in_specs = [pl.BlockSpec((tk, tm), lambda i, kk: (kk, i)), pl.BlockSpec((tk, n), lambda i, kk: (kk, 0))]
    args = [a, b]
    aliases = {}
    if prev is not None:
        in_specs += [ANY, ANY]
        args += list(prev)
        aliases = {2: 0, 3: 1}
    out = pl.BlockSpec((tm, n), lambda i, kk: (off + i, 0))
    return _call(
        body, name=name, grid=(m // tm, nk),
        in_specs=in_specs, out_specs=[out, out],
        out_shape=[jax.ShapeDtypeStruct((rows, n), F32), jax.ShapeDtypeStruct((rows, n), BF16)],
        scratch_shapes=[pltpu.VMEM((tm, n), F32)],
        aliases=aliases, args=args, carry=carry)


def _wgrad_parts(parts, b, name, carry=None):
    t, n = b.shape
    tm = 512
    bounds = []
    lo = 0
    for part in parts:
        assert part.shape[0] == t and part.shape[1] % tm == 0
        bounds.append((lo, lo + part.shape[1] // tm))
        lo += part.shape[1] // tm
    nm = lo
    np_ = len(parts)

    def body(*refs):
        p_refs, b_ref, o32_ref, o16_ref = refs[:np_], refs[np_], refs[np_ + 1], refs[np_ + 2]
        i = pl.program_id(0)
        for (lo_p, hi_p), p_ref in zip(bounds, p_refs):
            @pl.when((i >= lo_p) & (i < hi_p))
            def _(p_ref=p_ref):
                res = _dot_tn(p_ref[...], b_ref[...])
                o32_ref[...] = res
                o16_ref[...] = res.astype(BF16)

    def part_spec(lo_p, hi_p):
        return pl.BlockSpec((t, tm), lambda i: (0, jnp.clip(i - lo_p, 0, hi_p - lo_p - 1)))

    out = pl.BlockSpec((tm, n), lambda i: (i, 0))
    return _call(
        body, name=name, grid=(nm,),
        in_specs=[part_spec(lo_p, hi_p) for lo_p, hi_p in bounds] + [pl.BlockSpec((t, n), lambda i: (0, 0))],
        out_specs=[out, out],
        out_shape=[jax.ShapeDtypeStruct((nm * tm, n), F32), jax.ShapeDtypeStruct((nm * tm, n), BF16)],
        args=(*parts, b), carry=carry)


def _wgrad_cols(a, b, bw, tn, name, carry=None):
    t, m = a.shape
    n = b.shape[1]
    per_step = tn // bw

    def body(a_ref, b_ref, o32_ref, o16_ref):
        res = _dot_tn(a_ref[...], b_ref[...])
        for blk in range(per_step):
            part = res[:, blk * bw:(blk + 1) * bw]
            o32_ref[blk] = part
            o16_ref[blk] = part.astype(BF16)

    out = pl.BlockSpec((per_step, m, bw), lambda j: (j, 0, 0))
    return _call(
        body, name=name, grid=(n // tn,),
        in_specs=[pl.BlockSpec((t, m), lambda j: (0, 0)), pl.BlockSpec((t, tn), lambda j: (0, j))],
        out_specs=[out, out],
        out_shape=[jax.ShapeDtypeStruct((n // bw, m, bw), F32), jax.ShapeDtypeStruct((n // bw, m, bw), BF16)],
        args=(a, b), carry=carry)


def _ff1_bwd_norms(d_f1, w_ff1b, dy, x2, g3, m, g2, carry=None):
    t, k = d_f1.shape
    d = x2.shape[1]
    assert dy.dtype == BF16 and x2.dtype == F32
    bw = w_ff1b.shape[2]
    per_step = 2
    tt, tk = _tile(t, 1024), per_step * bw
    nk = k // tk

    def body(a_ref, w_ref, dy_hbm, x2_hbm, g3_ref, m_hbm, g2_ref, dx2_hbm, dm_hbm, dg3_ref, dg2_ref, acc,
             dy_ref, x2_ref, m_ref, late_sems, out_sems):
        i, kk = pl.program_id(0), pl.program_id(1)
        late = _late_copies(i, tt, [(dy_hbm, dy_ref), (x2_hbm, x2_ref), (m_hbm, m_ref)], late_sems)

        @pl.when(kk == 0)
        def _():
            acc[...] = jnp.zeros_like(acc)
            for cp in late:
                cp.start()

        @pl.when((i == 0) & (kk == 0))
        def _():
            dg3_ref[...] = jnp.zeros_like(dg3_ref)
            dg2_ref[...] = jnp.zeros_like(dg2_ref)

        a_tile = a_ref[...]
        for b in range(per_step):
            acc[...] += _dot_nt(a_tile[:, b * bw:(b + 1) * bw], w_ref[b])

        @pl.when(kk == nk - 1)
        def _():
            for cp in late:
                cp.wait()

            def tail(rows):
                xhat, r3 = _rms_hat(x2_ref[rows, :])
                dx, dg3 = _rms_bwd(acc[rows, :], xhat, r3, g3_ref[...])
                dx2 = dy_ref[rows, :].astype(F32) + dx
                x2_ref[rows, :] = dx2
                dg3_ref[...] += dg3
                mhat, r2 = _rms_hat(m_ref[rows, :])
                dm, dg2 = _rms_bwd(dx2, mhat, r2, g2_ref[...])
                dy_ref[rows, :] = dm.astype(BF16)
                dg2_ref[...] += dg2

            _row_chunks(tt, tail)
            tile = pl.ds(pl.multiple_of(i * tt, tt), tt)
            outs = [pltpu.make_async_copy(x2_ref, dx2_hbm.at[tile], out_sems.at[0]),
                    pltpu.make_async_copy(dy_ref, dm_hbm.at[tile], out_sems.at[1])]
            for cp in outs:
                cp.start()
            for cp in outs:
                cp.wait()

    vec = pl.BlockSpec((1, d), lambda i, kk: (0, 0))
    return _call(
        body, name="ff1_bwd_norms", grid=(t // tt, nk),
        in_specs=[
            pl.BlockSpec((tt, tk), lambda i, kk: (i, kk)),
            pl.BlockSpec((per_step, d, bw), lambda i, kk: (kk, 0, 0)),
            ANY, ANY, vec, ANY, vec,
        ],
        out_specs=[ANY, ANY, vec, vec],
        out_shape=[
            jax.ShapeDtypeStruct((t, d), F32),
            jax.ShapeDtypeStruct((t, d), BF16),
            jax.ShapeDtypeStruct((1, d), F32),
            jax.ShapeDtypeStruct((1, d), F32),
        ],
        scratch_shapes=[pltpu.VMEM((tt, d), F32), pltpu.VMEM((tt, d), dy.dtype), pltpu.VMEM((tt, d), F32),
                        pltpu.VMEM((tt, d), F32), pltpu.SemaphoreType.DMA((3,)), pltpu.SemaphoreType.DMA((2,))],
        args=(d_f1, w_ff1b, dy, x2, g3, m, g2), carry=carry)


def _wo_bwd_mix(dm, w_o, br_a, br_b, proj, b_gate, ga_block, gb_block, carry=None):
    t, d = dm.shape
    tt, tn = _tile(t, 1024), 512
    nj = d // tn

    def body(dm_ref, w_ref, bra_ref, brb_ref, ga_ref, gb_ref, ba_ref, bb_ref,
             dbra_ref, dbrb_ref, dga_ref, dgb_ref, dba_ref, dbb_ref):
        i = pl.program_id(1)

        @pl.when(i == 0)
        def _():
            dba_ref[...] = jnp.zeros_like(dba_ref)
            dbb_ref[...] = jnp.zeros_like(dbb_ref)

        d_mix = _dot_nt(dm_ref[...], w_ref[...])
        ga = _sig(ga_ref[...].astype(F32) + ba_ref[...])
        gb = _sig(gb_ref[...].astype(F32) + bb_ref[...])
        dbra_ref[...] = (d_mix * ga).astype(BF16)
        dbrb_ref[...] = (d_mix * gb).astype(BF16)
        dga = d_mix * bra_ref[...].astype(F32) * (ga * (1.0 - ga))
        dgb = d_mix * brb_ref[...].astype(F32) * (gb * (1.0 - gb))
        dga_ref[...] = dga.astype(BF16)
        dgb_ref[...] = dgb.astype(BF16)
        dba_ref[...] += jnp.sum(dga, axis=0, keepdims=True)
        dbb_ref[...] += jnp.sum(dgb, axis=0, keepdims=True)

    blk = pl.BlockSpec((tt, tn), lambda j, i: (i, j))
    vec = pl.BlockSpec((1, tn), lambda j, i: (0, j))
    return _call(
        body, name="wo_bwd_mix", grid=(nj, t // tt),
        in_specs=[
            pl.BlockSpec((tt, d), lambda j, i: (i, 0)),
            pl.BlockSpec((tn, d), lambda j, i: (j, 0)),
            blk, blk,
            pl.BlockSpec((tt, tn), lambda j, i: (i, ga_block + j)),
            pl.BlockSpec((tt, tn), lambda j, i: (i, gb_block + j)),
            vec,
            pl.BlockSpec((1, tn), lambda j, i: (0, nj + j)),
        ],
        out_specs=[blk, blk, blk, blk, vec, vec],
        out_shape=[jax.ShapeDtypeStruct((t, d), BF16)] * 4 + [jax.ShapeDtypeStruct((1, d), F32)] * 2,
        args=(dm, w_o, br_a, br_b, proj, proj, b_gate, b_gate), carry=carry)


def _lru_up_bwd(d_br_a, w_lru_up, proj, h, g_block, carry=None):
    t, d = d_br_a.shape
    tt, tn = _tile(t, 1024), 512

    def body(a_ref, w_ref, g_ref, h_ref, dh_ref, dg_ref):
        d_y = _dot_nt(a_ref[...], w_ref[...])
        gel, gel_grad = _gelu_and_grad(g_ref[...].astype(F32))
        dh_ref[...] = d_y * gel
        dg_ref[...] = (d_y * h_ref[...] * gel_grad).astype(BF16)

    blk = pl.BlockSpec((tt, tn), lambda i, j: (i, j))
    return _call(
        body, name="lru_up_bwd", grid=(t // tt, d // tn),
        in_specs=[
            pl.BlockSpec((tt, d), lambda i, j: (i, 0)),
            pl.BlockSpec((tn, d), lambda i, j: (j, 0)),
            pl.BlockSpec((tt, tn), lambda i, j: (i, g_block + j)),
            blk,
        ],
        out_specs=[blk, blk],
        out_shape=[jax.ShapeDtypeStruct((t, d), F32), jax.ShapeDtypeStruct((t, d), BF16)],
        args=(d_br_a, w_lru_up, proj, h), carry=carry)


def _lru_bwd(dh, h, saved, proj, conv_w, w_a, w_x, lam, carry=None):
    t, dr = dh.shape
    cb = LRU_CB
    hd = LRU_HEAD_DIM
    per = cb // hd
    tc = _tile(t, 256)
    ncb, ntc = dr // cb, t // tc

    def body(dh_ref, h_ref, hp_ref, saved_ref, xp_ref, cw_ref, wa_ref, wx_ref,
             lam_ref, dxp_ref, dwa_ref, dba_ref, dwx_ref, dbx_ref, dlam_ref, dcw_ref, dcb_ref,
             nextd_s, anext_s, gnext_s, tmp_s, wa_s, wx_s):
        c = pl.program_id(1)
        rc = ntc - 1 - c

        @pl.when(c == 0)
        def _():
            nextd_s[...] = jnp.zeros_like(nextd_s)
            anext_s[...] = jnp.zeros_like(anext_s)
            gnext_s[...] = jnp.zeros_like(gnext_s)
            for ref in (dwa_ref, dba_ref, dwx_ref, dbx_ref, dlam_ref, dcw_ref, dcb_ref):
                ref[...] = jnp.zeros_like(ref)
            _fill_block_diag(wa_ref, wa_s)
            _fill_block_diag(wx_ref, wx_s)

        xc, r, i, a, mult = [saved_ref[:, k * cb:(k + 1) * cb] for k in range(N_LRU_SAVED)]
        wa, wx, lam = wa_s[...], wx_s[...], lam_ref[...]
        xcb = xc.astype(BF16)
        sp = _softplus_neg(lam)
        row = lax.broadcasted_iota(jnp.int32, xc.shape, 0)
        h = h_ref[...]
        hp = jnp.where(rc == 0, 0.0, hp_ref[...])
        hprev = jnp.where(row >= 1, pltpu.roll(h, 1, 0), pltpu.roll(hp, 1, 0))

        def up(v, nv, j):
            return jnp.where(row < tc - j, pltpu.roll(v, tc - j, 0), nv)

        av, bv = _scan_rows(up(a, anext_s[...], 1), dh_ref[...], reverse=True)
        gt = av * gnext_s[...] + bv
        tmp_s[...] = gt
        gnext_s[...] = tmp_s[0:1, :]
        tmp_s[...] = a
        anext_s[...] = tmp_s[0:1, :]

        da = gt * hprev
        ixc = i * xc
        d_mult = gt * ixc
        d_i = gt * mult * xc
        d_xc = gt * mult * i
        d_log_a = da * a - d_mult * (a * a) / mult
        d_pre_r = (d_log_a * ((-LRU_C) * sp)) * (r * (1.0 - r))
        d_pre_i = d_i * (i * (1.0 - i))
        d_sp = jnp.sum(d_log_a * ((-LRU_C) * r), axis=0, keepdims=True)
        dlam_ref[...] += d_sp * (-1.0 / (1.0 + jnp.exp(lam)))
        dpr = d_pre_r.astype(BF16)
        dpi = d_pre_i.astype(BF16)
        dba_ref[...] += jnp.sum(d_pre_r, axis=0, keepdims=True)
        dbx_ref[...] += jnp.sum(d_pre_i, axis=0, keepdims=True)
        pa = _dot_tn(xcb, dpr)
        px = _dot_tn(xcb, dpi)
        for k in range(per):
            dwa_ref[k] += pa[k * hd:(k + 1) * hd, k * hd:(k + 1) * hd]
            dwx_ref[k] += px[k * hd:(k + 1) * hd, k * hd:(k + 1) * hd]
        d_xc = d_xc + _dot_nt(dpr, wa) + _dot_nt(dpi, wx)

        nxt = nextd_s[...]
        xp = xp_ref[...].astype(F32)
        dxp = cw_ref[3:4, :] * d_xc
        dcw_ref[3:4, :] += jnp.sum(xp * d_xc, axis=0, keepdims=True)
        for j in (1, 2, 3):
            uj = up(d_xc, pltpu.roll(nxt, tc - j, 0), j)
            dxp = dxp + cw_ref[3 - j:4 - j, :] * uj
            dcw_ref[3 - j:4 - j, :] += jnp.sum(xp * uj, axis=0, keepdims=True)
        dcb_ref[...] += jnp.sum(d_xc, axis=0, keepdims=True)
        nextd_s[...] = d_xc
        dxp_ref[...] = dxp.astype(BF16)

    vec = pl.BlockSpec((1, cb), lambda j, c: (0, j))
    blk = pl.BlockSpec((tc, cb), lambda j, c: (ntc - 1 - c, j))
    mat = pl.BlockSpec((per, hd, hd), lambda j, c: (j, 0, 0))
    cwb = pl.BlockSpec((4, cb), lambda j, c: (0, j))
    return _call(
        body, name="lru_bwd", grid=(ncb, ntc),
        in_specs=[
            blk, blk,
            pl.BlockSpec((tc, cb), lambda j, c: (jnp.maximum(ntc - 2 - c, 0), j)),
            pl.BlockSpec((tc, N_LRU_SAVED * cb), lambda j, c: (ntc - 1 - c, j)),
            blk, cwb, mat, mat, vec,
        ],
        out_specs=[blk, mat, vec, mat, vec, vec, cwb, vec],
        out_shape=[
            jax.ShapeDtypeStruct((t, dr), BF16),
            jax.ShapeDtypeStruct(w_a.shape, F32),
            jax.ShapeDtypeStruct((1, dr), F32),
            jax.ShapeDtypeStruct(w_x.shape, F32),
            jax.ShapeDtypeStruct((1, dr), F32),
            jax.ShapeDtypeStruct((1, dr), F32),
            jax.ShapeDtypeStruct((4, dr), F32),
            jax.ShapeDtypeStruct((1, dr), F32),
        ],
        scratch_shapes=[
            pltpu.VMEM((tc, cb), F32),
            pltpu.VMEM((1, cb), F32),
            pltpu.VMEM((1, cb), F32),
            pltpu.VMEM((tc, cb), F32),
            pltpu.VMEM((cb, cb), BF16),
            pltpu.VMEM((cb, cb), BF16),
        ],
        args=(dh, h, h, saved, proj, conv_w, w_a, w_x, lam), carry=carry)


def _pool_bwd(d_br_b, w_pool_upb, p, pool_w, pool_scale):
    t, d = d_br_b.shape
    nwb, dp, _ = w_pool_upb.shape
    tc = _tile(t, 256)
    ntc = t // tc
    ng = len(POOL_WINDOWS)

    def body(db_ref, wu_ref, p_ref, w_ref, sc_ref, dx_ref, dw_ref, dsc_ref, nz, n2, n4, n8, dp_s, dy_s):
        c = pl.program_id(0)
        rc = ntc - 1 - c

        @pl.when(c == 0)
        def _():
            for s in (nz, n2, n4, n8):
                s[...] = jnp.zeros_like(s)
            dw_ref[...] = jnp.zeros_like(dw_ref)
            dsc_ref[...] = jnp.zeros_like(dsc_ref)

        wu = jnp.concatenate([wu_ref[b] for b in range(nwb)], axis=1)
        dy_s[...] = _dot_nt(db_ref[...], wu)
        for g in range(ng):
            sl = slice(g * POOL_GROUP_DIM, (g + 1) * POOL_GROUP_DIM)
            pg = p_ref[:, sl]
            dyg = dy_s[:, sl]
            wg = w_ref[g].astype(BF16)
            q = _dot_nn(pg, wg)
            dsc_ref[:, sl] += jnp.sum(dyg * q, axis=0, keepdims=True)
            dpw = (dyg * sc_ref[:, sl]).astype(BF16)
            dw_ref[g] += _dot_tn(pg, dpw)
            dp_s[:, sl] = _dot_nt(dpw, wg)

        dpv = dp_s[...]
        row = lax.broadcasted_iota(jnp.int32, dpv.shape, 0)
        col = lax.broadcasted_iota(jnp.int32, dpv.shape, 1)
        win = _pool_select(col, POOL_WINDOWS)
        cnt = jnp.minimum(rc * tc + row + 1, win).astype(F32)
        z = dpv / cnt

        def up(v, nv, j):
            return jnp.where(row < tc - j, pltpu.roll(v, tc - j, 0), pltpu.roll(nv[...], tc - j, 0))

        u2 = z + up(z, nz, 1)
        u4 = u2 + up(u2, n2, 2)
        u8 = u4 + up(u4, n4, 4)
        u16 = u8 + up(u8, n8, 8)
        nz[...] = z
        n2[...] = u2
        n4[...] = u4
        n8[...] = u8
        dx_ref[...] = (_pool_select(col, (u2, u4, u8, u16)) - dpv).astype(BF16)

    blk = pl.BlockSpec((tc, dp), lambda c: (ntc - 1 - c, 0))
    full_w = pl.BlockSpec(pool_w.shape, lambda c: (0, 0, 0))
    vec = pl.BlockSpec((1, dp), lambda c: (0, 0))
    return _call(
        body, name="pool_bwd", grid=(ntc,),
        in_specs=[pl.BlockSpec((tc, d), lambda c: (ntc - 1 - c, 0)),
                  pl.BlockSpec(w_pool_upb.shape, lambda c: (0, 0, 0)), blk, full_w, vec],
        out_specs=[blk, full_w, vec],
        out_shape=[
            jax.ShapeDtypeStruct((t, dp), BF16),
            jax.ShapeDtypeStruct(pool_w.shape, F32),
            jax.ShapeDtypeStruct((1, dp), F32),
        ],
        scratch_shapes=[pltpu.VMEM((tc, dp), F32)] * 6,
        args=(d_br_b, w_pool_upb, p, pool_w, pool_scale))[0]


def _win_bwd_norm(parts, w_int, dx2, x, g1, carry=None):
    t, d = x.shape
    tk = 512
    tt = _tile(t, 1024)
    bounds = []
    k0 = 0
    for part in parts:
        assert part.shape[1] % tk == 0
        bounds.append((k0, k0 + part.shape[1] // tk))
        k0 += part.shape[1] // tk
    nk = k0
    assert nk * tk == w_int.shape[0]
    np_ = len(parts)

    def body(*refs):
        p_refs = refs[:np_]
        w_ref, dx2_hbm, x_hbm, g_ref, gx_hbm, dg_ref, acc, dx2_ref, x_ref, late_sems, out_sem = refs[np_:]
        i, kk = pl.program_id(0), pl.program_id(1)
        late = _late_copies(i, tt, [(dx2_hbm, dx2_ref), (x_hbm, x_ref)], late_sems)

        @pl.when(kk == 0)
        def _():
            acc[...] = jnp.zeros_like(acc)
            for cp in late:
                cp.start()

        @pl.when((i == 0) & (kk == 0))
        def _():
            dg_ref[...] = jnp.zeros_like(dg_ref)

        for (lo, hi), p_ref in zip(bounds, p_refs):
            @pl.when((kk >= lo) & (kk < hi))
            def _(p_ref=p_ref):
                acc[...] += _dot_nn(p_ref[...], w_ref[...])

        @pl.when(kk == nk - 1)
        def _():
            for cp in late:
                cp.wait()

            def tail(rows):
                xhat, r = _rms_hat(x_ref[rows, :])
                dx, dg = _rms_bwd(acc[rows, :], xhat, r, g_ref[...])
                dx2_ref[rows, :] = dx2_ref[rows, :] + dx
                dg_ref[...] += dg

            _row_chunks(tt, tail)
            out = pltpu.make_async_copy(
                dx2_ref, gx_hbm.at[pl.ds(pl.multiple_of(i * tt, tt), tt)], out_sem.at[0])
            out.start()
            out.wait()

    def part_spec(lo, hi):
        return pl.BlockSpec((tt, tk), lambda i, kk: (i, jnp.clip(kk - lo, 0, hi - lo - 1)))

    vec = pl.BlockSpec((1, d), lambda i, kk: (0, 0))
    return _call(
        body, name="win_bwd_norm", grid=(t // tt, nk),
        in_specs=[part_spec(lo, hi) for lo, hi in bounds]
        + [pl.BlockSpec((tk, d), lambda i, kk: (kk, 0)), ANY, ANY, vec],
        out_specs=[ANY, vec],
        out_shape=[jax.ShapeDtypeStruct((t, d), F32), jax.ShapeDtypeStruct((1, d), F32)],
        scratch_shapes=[pltpu.VMEM((tt, d), F32), pltpu.VMEM((tt, d), F32), pltpu.VMEM((tt, d), F32),
                        pltpu.SemaphoreType.DMA((2,)), pltpu.SemaphoreType.DMA((1,))],
        args=(*parts, w_int, dx2, x, g1), carry=carry)


def _adam_math(w, g, m, v):
    m = ADAM_B1 * m + (1.0 - ADAM_B1) * g
    v = ADAM_B2 * v + (1.0 - ADAM_B2) * (g * g)
    m_hat = m / (1.0 - ADAM_B1 ** ADAM_STEP)
    v_hat = v / (1.0 - ADAM_B2 ** ADAM_STEP)
    delta = -ADAM_LR * (m_hat / (jnp.sqrt(v_hat) + ADAM_EPS) + ADAM_WD * w)
    return delta, m, v


def _adamw_big(ws, gs, ms, vs):
    n = len(ws)
    nb = 4
    pair = [isinstance(g, tuple) for g in gs]

    def body(*refs):
        p = 0
        ins = []
        for a in range(n):
            k = 5 if pair[a] else 4
            ins.append(refs[p:p + k])
            p += k
        for a in range(n):
            g_out, d_ref, nm_ref, nv_ref = refs[p + 4 * a:p + 4 * a + 4]
            if pair[a]:
                w_ref, own_ref, recv_ref, m_ref, v_ref = ins[a]
                g = own_ref[...]
                for k in range(3):
                    g = g + recv_ref[k].astype(F32)
            else:
                w_ref, g_ref, m_ref, v_ref = ins[a]
                g = g_ref[...]
            dl, m, v = _adam_math(w_ref[...], g, m_ref[...], v_ref[...])
            g_out[...] = g
            d_ref[...] = dl
            nm_ref[...] = m
            nv_ref[...] = v

    in_specs, out_specs, out_shape, args = [], [], [], []
    for a, (w, g, m, v) in enumerate(zip(ws, gs, ms, vs)):
        rows, cols = w.shape
        blk = pl.BlockSpec((rows // nb, cols), lambda i: (i, 0))
        if pair[a]:
            in_specs += [blk, pl.BlockSpec((None, rows // nb, cols), lambda i: (0, i, 0)),
                         pl.BlockSpec((3, rows // nb, cols), lambda i: (0, i, 0)), blk, blk]
            args += [w, g[0], g[1], m, v]
        else:
            in_specs += [blk] * 4
            args += [w, g, m, v]
        out_specs += [blk] * 4
        out_shape += [jax.ShapeDtypeStruct(w.shape, F32)] * 4
    outs = _call(body, name="adamw_big", grid=(nb,), in_specs=in_specs, out_specs=out_specs,
                 out_shape=out_shape, args=args)[0]
    return [tuple(outs[4 * a:4 * a + 4]) for a in range(n)]


SMALL_ORDER = ("norm_mix_pre", "norm_mix_post", "norm_mlp_pre", "norm_mlp_post", "b_gate", "conv_w", "conv_b",
               "lru_w_a", "lru_b_a", "lru_w_x", "lru_b_x", "lru_lambda", "pool_w", "pool_scale")
VEC_ROW = dict(norm_mix_pre=0, norm_mix_post=1, norm_mlp_pre=2, norm_mlp_post=3, conv_b=6, lru_b_a=7,
               lru_b_x=8, lru_lambda=9)
ROW_B_GATE, ROW_POOL_SCALE, ROW_CONV_W, ROW_LOSS, N_VEC_ROWS = 4, 10, 11, 15, 16


def _adamw_small(vec_parts, g_pool, g_wa, g_wx, me, params):
    d = vec_parts.shape[2]
    names = SMALL_ORDER
    n = len(names)
    cw_cols = params["conv_w"][0].shape[2]

    def body(me_ref, vec_ref, vecc_ref, gp_ref, gwa_ref, gwx_ref, *refs):
        wmv = refs[:3 * n]
        loss_ref = refs[3 * n]
        outs = refs[3 * n + 1:3 * n + 1 + 4 * n]
        vs, vsc = refs[3 * n + 1 + 4 * n:]
        acc, accc = vec_ref[0], vecc_ref[0]
        for k in range(1, N_DEV):
            acc = acc + vec_ref[k]
            accc = accc + vecc_ref[k]
        vs[...] = acc
        vsc[...] = accc
        loss_ref[...] = vs[ROW_LOSS:ROW_LOSS + 1, 0:128]

        def upd(a, g, idx):
            w_ref, m_ref, v_ref = wmv[3 * a:3 * a + 3]
            g_ref, d_ref, nm_ref, nv_ref = outs[4 * a:4 * a + 4]
            dl, m, v = _adam_math(w_ref[idx], g, m_ref[idx], v_ref[idx])
            g_ref[idx] = g
            d_ref[idx] = dl
            nm_ref[idx] = m
            nv_ref[idx] = v

        for a, name in enumerate(names):
            if name in VEC_ROW:
                r = VEC_ROW[name]
                upd(a, vs[r:r + 1, :], (slice(None), slice(None)))
            elif name == "b_gate":
                for half in range(2):
                    r = ROW_B_GATE + half
                    upd(a, vs[r:r + 1, :], (slice(None), slice(half * d, (half + 1) * d)))
            elif name == "pool_scale":
                width = params[name][0].shape[1]
                upd(a, vs[ROW_POOL_SCALE:ROW_POOL_SCALE + 1, 0:width], (slice(None), slice(None)))
            elif name == "conv_w":
                upd(a, vsc[ROW_CONV_W:ROW_CONV_W + 4, :], (0,))
            elif name == "pool_w":
                upd(a, gp_ref[...], (Ellipsis,))
            elif name == "lru_w_a":
                upd(a, gwa_ref[...], (Ellipsis,))
            elif name == "lru_w_x":
                upd(a, gwx_ref[...], (Ellipsis,))
            else:
                raise ValueError(name)

    def whole(shape):
        nd = len(shape)
        return pl.BlockSpec(tuple(shape), lambda i, me_ref: (0,) * nd)

    in_specs = [
        whole(vec_parts.shape),
        pl.BlockSpec((N_DEV, N_VEC_ROWS, cw_cols), lambda i, me_ref: (0, 0, me_ref[0])),
        whole(g_pool.shape), whole(g_wa.shape), whole(g_wx.shape),
    ]
    args = [vec_parts, vec_parts, g_pool, g_wa, g_wx]
    out_specs = [whole((1, 128))]
    out_shape = [jax.ShapeDtypeStruct((1, 128), F32)]
    for name in names:
        for arr in params[name]:
            in_specs.append(whole(arr.shape))
            args.append(arr)
        shp = params[name][0].shape
        out_specs += [whole(shp)] * 4
        out_shape += [jax.ShapeDtypeStruct(shp, F32)] * 4
    grid_spec = pltpu.PrefetchScalarGridSpec(
        num_scalar_prefetch=1, grid=(1,), in_specs=in_specs, out_specs=out_specs,
        scratch_shapes=[pltpu.VMEM((N_VEC_ROWS, d), F32), pltpu.VMEM((N_VEC_ROWS, cw_cols), F32)])
    outs = pl.pallas_call(
        body, name="adamw_small", grid_spec=grid_spec, out_shape=out_shape,
        compiler_params=pltpu.CompilerParams(
            dimension_semantics=("arbitrary",), vmem_limit_bytes=V7X_VMEM_LIMIT_BYTES),
    )(me, *_in_hbm(args))
    return outs[0], {name: tuple(outs[1 + 4 * a:5 + 4 * a]) for a, name in enumerate(names)}


def _rs_sum(fulls, recvs, shard_ids, slot_ids, name):
    n = len(fulls)

    def body(sh_ref, sl_ref, *refs):
        s = pl.program_id(0)
        for a in range(n):
            full_ref, recv_ref = refs[2 * a], refs[2 * a + 1]
            own_ref, send_ref = refs[2 * n + 2 * a], refs[2 * n + 2 * a + 1]
            v = full_ref[...] + recv_ref[...].astype(F32)

            @pl.when(s == 0)
            def _(own_ref=own_ref, v=v):
                own_ref[...] = v

            @pl.when(s > 0)
            def _(send_ref=send_ref, v=v):
                send_ref[...] = v.astype(send_ref.dtype)

    in_specs, out_specs, out_shape, args = [], [], [], []
    for full, recv in zip(fulls, recvs):
        r, rest = recv.shape[1], tuple(recv.shape[2:])
        zeros = (0,) * len(rest)
        in_specs += [
            pl.BlockSpec((r,) + rest, lambda s, sh, sl, zeros=zeros: (sh[s],) + zeros),
            pl.BlockSpec((None, r) + rest, lambda s, sh, sl, zeros=zeros: (sl[s], 0) + zeros),
        ]
        out_specs += [
            pl.BlockSpec((None, r) + rest, lambda s, sh, sl, zeros=zeros: (0, 0) + zeros),
            pl.BlockSpec((None, r) + rest, lambda s, sh, sl, zeros=zeros: (jnp.maximum(s - 1, 0), 0) + zeros),
        ]
        out_shape += [jax.ShapeDtypeStruct((1, r) + rest, F32), jax.ShapeDtypeStruct((3, r) + rest, recv.dtype)]
        args += [full, recv]
    grid_spec = pltpu.PrefetchScalarGridSpec(
        num_scalar_prefetch=2, grid=(4,), in_specs=in_specs, out_specs=out_specs)
    outs = pl.pallas_call(
        body,
        name=name,
        grid_spec=grid_spec,
        out_shape=out_shape,
        compiler_params=pltpu.CompilerParams(
            dimension_semantics=("arbitrary",), vmem_limit_bytes=V7X_VMEM_LIMIT_BYTES),
    )(shard_ids, slot_ids, *_in_hbm(args))
    return [(outs[2 * a], outs[2 * a + 1]) for a in range(n)]


def _finals(pairs, name, carry=None):
    nb = 4
    n = len(pairs)

    def body(*refs):
        for a in range(n):
            own_ref, recv_ref = refs[2 * a], refs[2 * a + 1]
            acc = own_ref[...]
            for k in range(3):
                acc = acc + recv_ref[k].astype(F32)
            refs[2 * n + a][...] = acc

    in_specs, out_specs, out_shape, args = [], [], [], []
    for own, recv in pairs:
        _, rows, cols = own.shape
        in_specs += [pl.BlockSpec((None, rows // nb, cols), lambda i: (0, i, 0)),
                     pl.BlockSpec((3, rows // nb, cols), lambda i: (0, i, 0))]
        args += [own, recv]
        out_specs.append(pl.BlockSpec((rows // nb, cols), lambda i: (i, 0)))
        out_shape.append(jax.ShapeDtypeStruct((rows, cols), F32))
    return _call(body, name=name, grid=(nb,), in_specs=in_specs, out_specs=out_specs,
                 out_shape=out_shape, args=args, carry=carry)


def _rs_sums(fulls_f32, recv1, tag):
    x, y, c = _place()
    qs = jnp.stack([2 * x + y, 2 * (1 - x) + y, 2 * x + (1 - y), 2 * (1 - x) + (1 - y)]).astype(jnp.int32)
    shard_ids = 2 * qs + c
    return _rs_sum(fulls_f32, recv1, shard_ids, qs, "rs_sum_" + tag)


def _rs_level1(fulls_f32, fulls_send, tag):
    recv1 = _run_plan(_rs_sibling_plan(fulls_send), "rs_sibling_" + tag)
    return _rs_sums(fulls_f32, recv1, tag)


def _rows(g):
    return g.reshape(g.shape[0] * g.shape[1], g.shape[2])


def kernel(x, norm_mix_pre, norm_mix_post, norm_mlp_pre, norm_mlp_post, w_in, b_gate, conv_w, conv_b, lru_w_a, lru_b_a, lru_w_x, lru_b_x, lru_lambda, pool_w, pool_scale, w_lru_up, w_pool_up, w_o, w_ff1, w_ff2, loss_target, m_norm_mix_pre, m_norm_mix_post, m_norm_mlp_pre, m_norm_mlp_post, m_w_in, m_b_gate, m_conv_w, m_conv_b, m_lru_w_a, m_lru_b_a, m_lru_w_x, m_lru_b_x, m_lru_lambda, m_pool_w, m_pool_scale, m_w_lru_up, m_w_pool_up, m_w_o, m_w_ff1, m_w_ff2, v_norm_mix_pre, v_norm_mix_post, v_norm_mlp_pre, v_norm_mlp_post, v_w_in, v_b_gate, v_conv_w, v_conv_b, v_lru_w_a, v_lru_b_a, v_lru_w_x, v_lru_b_x, v_lru_lambda, v_pool_w, v_pool_scale, v_w_lru_up, v_w_pool_up, v_w_o, v_w_ff1, v_w_ff2):
    t, d = x.shape[1], x.shape[2]
    d_rnn = conv_b.shape[1]
    d_pool = pool_scale.shape[1]
    per = LRU_CB // LRU_HEAD_DIM
    xi, yi, ci = _place()
    me = 4 * xi + 2 * yi + ci

    x2d = x[0]
    tgt = loss_target[0]

    s_in = w_in[0].T.astype(BF16)
    s_lu = w_lru_up[0].astype(BF16)
    s_pu = w_pool_up[0].astype(BF16)
    s_o = w_o[0].astype(BF16)
    s_f1 = w_ff1[0].astype(BF16)
    s_f2 = w_ff2[0].astype(BF16)
    s_cw = jnp.pad(conv_w[0], ((0, 4), (0, 0)))

    g_in, g_cw = _run_plan(_ag_plan([s_in, s_cw]), "ag_w_in")
    w_int = _rows(g_in)
    conv_w_full = jnp.transpose(g_cw[:, :4, :], (1, 0, 2)).reshape(4, d_rnn)

    wa_bd, wx_bd = lru_w_a[0], lru_w_x[0]
    pw = pool_w[0]
    pw_bf = pw.astype(BF16)

    pool_block = (2 * d_rnn) // d_pool
    ga_block = (2 * d_rnn + d_pool) // 512
    gb_block = ga_block + d // 512
    g_block = d_rnn // 512

    r_f1, r_f2 = s_f1.shape[0], s_f2.shape[0]
    f1_cut = r_f1 // 4
    f2_cut = (3 * r_f2) // 8
    plan = _join([_ag_plan([s_lu, s_pu]), _ag_plan([s_f1], pieces=[(0, f1_cut)])])
    (proj, h1), got = _norm_proj(x2d, norm_mix_pre, w_int, carry=plan)
    (g_lu, g_pu), (g_f1,) = plan.split(got)
    plan = _join([_ag_plan([s_f1], pieces=[(f1_cut, r_f1 - f1_cut)], bufs=[g_f1]), _ag_plan([s_o])])
    (y_lru, h, lru_saved), got = _lru_fwd(
        proj, conv_w_full, conv_b, wa_bd, lru_b_a, wx_bd, lru_b_x, lru_lambda, carry=plan)
    (g_f1,), (g_o,) = plan.split(got)
    w_lu, w_og = _rows(g_lu), _rows(g_o)
    y_pool, p = _pool_fwd(proj, pw_bf, pool_scale, pool_block)
    (br_a, br_b, mix), (g_f2,) = _branch_mix(
        y_lru, y_pool, w_lu, g_pu, proj, b_gate, ga_block, gb_block,
        carry=_ag_plan([s_f2], pieces=[(0, f2_cut)]))
    (m, x2, h3), _ = _wo_norm(mix, w_og, x2d, norm_mix_post, norm_mlp_pre)
    (rf,), (g_f2,) = _ff1(
        h3, g_f1, carry=_ag_plan([s_f2], pieces=[(f2_cut, r_f2 - f2_cut)], bufs=[g_f2]))
    w_f2 = _rows(g_f2)
    dy, df, dg4, loss_part = _ff2_loss(rf, w_f2, x2, norm_mlp_post, tgt)

    (gw_ff2_32, gw_ff2_16), _ = _wgrad(rf, df, "wgrad_ff2", square_a=True)
    (d_f1,), r1_ff2 = _ff2_bwd(df, w_f2, rf, carry=_rs_sibling_plan([gw_ff2_16]))
    ((own_ff2, send_ff2),) = _rs_sums([gw_ff2_32], r1_ff2, "ff2")
    cut2 = (5 * send_ff2.shape[1]) // 16
    (gw_ff1_32, gw_ff1_16), (r2_ff2,) = _wgrad_cols(
        h3, d_f1, s_f1.shape[1], s_f1.shape[1], "wgrad_ff1",
        carry=_rs_chips_plan([send_ff2], pieces=[(0, cut2)]))
    plan = _join([_rs_chips_plan([send_ff2], pieces=[(cut2, send_ff2.shape[1] - cut2)], bufs=[r2_ff2]),
                  _rs_sibling_plan([gw_ff1_16])])
    (dx2, dm, dg3, dg2), got = _ff1_bwd_norms(d_f1, g_f1, dy, x2, norm_mlp_pre, m, norm_mix_post, carry=plan)
    (r2_ff2,), r1_ff1 = plan.split(got)
    ((own_ff1, send_ff1),) = _rs_sums([gw_ff1_32], r1_ff1, "ff1")
    own_ff1, send_ff1 = own_ff1.reshape((1,) + s_f1.shape), send_ff1.reshape((3,) + s_f1.shape)
    cut = send_ff1.shape[1] // 4
    (gw_o_32, gw_o_16), _ = _wgrad(mix, dm, "wgrad_o")
    (d_br_a, d_br_b, p_ga, p_gb, dbg_a, dbg_b), (r2_ff1,) = _wo_bwd_mix(
        dm, w_og, br_a, br_b, proj, b_gate, ga_block, gb_block,
        carry=_rs_chips_plan([send_ff1], pieces=[(0, cut)]))
    (gw_lu_32, gw_lu_16), _ = _wgrad(y_lru, d_br_a, "wgrad_lru_up")
    (gw_pu_32, gw_pu_16), _ = _wgrad_cols(y_pool, d_br_b, s_pu.shape[1], d, "wgrad_pool_up")
    (dh, p_g), r1_mid = _lru_up_bwd(
        d_br_a, w_lu, proj, h, g_block,
        carry=_rs_sibling_plan([gw_o_16, gw_lu_16, gw_pu_16]))
    mid = _rs_sums([gw_o_32, gw_lu_32, gw_pu_32], r1_mid, "mid")
    plan = _join([_rs_chips_plan([send_ff1], pieces=[(cut, send_ff1.shape[1] - cut)], bufs=[r2_ff1]),
                  _rs_chips_plan([mid[0][1]])])
    (p_x, dwa, db_a, dwx, db_x, dlam, dconv_w, dconv_b), got = _lru_bwd(
        dh, h, lru_saved, proj, conv_w_full, wa_bd, wx_bd, lru_lambda, carry=plan)
    (r2_ff1,), (r2_o,) = plan.split(got)
    p_p, dpool_w, dpool_scale = _pool_bwd(d_br_b, g_pu, p, pw, pool_scale)
    parts = [p_x, p_g, p_p, p_ga, p_gb]
    gw_in, (r2_lu, r2_pu) = _wgrad_parts(
        parts, h1, "wgrad_in", carry=_rs_chips_plan([mid[1][1], mid[2][1]]))
    r2_mid = [r2_o, r2_lu, r2_pu]
    tail = _rs_level1([gw_in[0], dpool_w.reshape(N_DEV, -1, POOL_GROUP_DIM), dwa, dwx],
                      [gw_in[1], dpool_w.reshape(N_DEV, -1, POOL_GROUP_DIM), dwa, dwx], "in")
    (grad_x, dg1), r2_tail = _win_bwd_norm(parts, w_int, dx2, x2d, norm_mix_pre,
                                           carry=_rs_chips_plan([s for _, s in tail]))

    def flat2(a):
        return a.reshape(a.shape[0], -1, a.shape[-1])

    fin_small, _ = _finals([
        (flat2(tail[1][0]), flat2(r2_tail[1])), (flat2(tail[2][0]), flat2(r2_tail[2])),
        (flat2(tail[3][0]), flat2(r2_tail[3])),
    ], "rs_finals_small")

    def pad_row(a):
        return jnp.pad(a, ((0, 0), (0, d - a.shape[1])))

    vecs = jnp.concatenate([dg1, dg2, dg3, dg4, dbg_a, dbg_b, dconv_b, db_a, db_x, dlam,
                            pad_row(dpool_scale), dconv_w, pad_row(loss_part)], axis=0)
    assert vecs.shape[0] == N_VEC_ROWS
    vec_parts, g_pool, g_wa, g_wx = _run_plan(_ag_plan([vecs] + fin_small), "ag_tail")

    big_names = ["w_in", "w_lru_up", "w_pool_up", "w_o", "w_ff1", "w_ff2"]
    big_w = [w_in[0].T, w_lru_up[0], w_pool_up[0], w_o[0], w_ff1[0], w_ff2[0]]
    big_g = [(tail[0][0], r2_tail[0]), (mid[1][0], r2_mid[1]),
             (mid[2][0].reshape((1,) + s_pu.shape), r2_mid[2].reshape((3,) + s_pu.shape)),
             (mid[0][0], r2_mid[0]), (own_ff1, r2_ff1), (own_ff2, r2_ff2)]
    big_m = [m_w_in[0].T, m_w_lru_up[0], m_w_pool_up[0], m_w_o[0], m_w_ff1[0], m_w_ff2[0]]
    big_v = [v_w_in[0].T, v_w_lru_up[0], v_w_pool_up[0], v_w_o[0], v_w_ff1[0], v_w_ff2[0]]
    big_out = _adamw_big(big_w, big_g, big_m, big_v)
    big_out[0] = tuple(o.T for o in big_out[0])

    small = dict(
        norm_mix_pre=(norm_mix_pre, m_norm_mix_pre, v_norm_mix_pre),
        norm_mix_post=(norm_mix_post, m_norm_mix_post, v_norm_mix_post),
        norm_mlp_pre=(norm_mlp_pre, m_norm_mlp_pre, v_norm_mlp_pre),
        norm_mlp_post=(norm_mlp_post, m_norm_mlp_post, v_norm_mlp_post),
        b_gate=(b_gate, m_b_gate, v_b_gate), conv_w=(conv_w, m_conv_w, v_conv_w),
        conv_b=(conv_b, m_conv_b, v_conv_b), lru_w_a=(lru_w_a, m_lru_w_a, v_lru_w_a),
        lru_b_a=(lru_b_a, m_lru_b_a, v_lru_b_a), lru_w_x=(lru_w_x, m_lru_w_x, v_lru_w_x),
        lru_b_x=(lru_b_x, m_lru_b_x, v_lru_b_x), lru_lambda=(lru_lambda, m_lru_lambda, v_lru_lambda),
        pool_w=(pool_w, m_pool_w, v_pool_w), pool_scale=(pool_scale, m_pool_scale, v_pool_scale))
    loss_row, small_out = _adamw_small(
        vec_parts, g_pool.reshape(pool_w.shape), g_wa.reshape(lru_w_a.shape), g_wx.reshape(lru_w_x.shape),
        jnp.reshape(me, (1,)).astype(jnp.int32), small)
    grads = {n: o[0] for n, o in small_out.items()}
    delta = {n: o[1] for n, o in small_out.items()}
    new_m = {n: o[2] for n, o in small_out.items()}
    new_v = {n: o[3] for n, o in small_out.items()}

    for name, (g, dl, nm, nv) in zip(big_names, big_out):
        grads[name], delta[name], new_m[name], new_v[name] = g[None], dl[None], nm[None], nv[None]

    loss = loss_row[0, 0]
    order = ["norm_mix_pre", "norm_mix_post", "norm_mlp_pre", "norm_mlp_post", "w_in", "b_gate", "conv_w",
             "conv_b", "lru_w_a", "lru_b_a", "lru_w_x", "lru_b_x", "lru_lambda", "pool_w", "pool_scale",
             "w_lru_up", "w_pool_up", "w_o", "w_ff1", "w_ff2"]
    return (loss, grad_x[None], *[grads[n] for n in order], *[delta[n] for n in order],
            *[new_m[n] for n in order], *[new_v[n] for n in order])
```

```python
import functools
import math
import operator
import types

import jax
import jax.numpy as jnp
from jax import lax
from jax.experimental import pallas as pl
from jax.experimental.pallas import tpu as pltpu

F32 = jnp.float32
BF16 = jnp.bfloat16
NORM_EPS = 1e-6
LRU_C = 8.0
N_LRU_HEADS = 16
LRU_HEAD_DIM = 64
POOL_WINDOWS = (2, 4, 8, 16)
POOL_GROUP_DIM = 128
ADAM_LR = 0.001
ADAM_B1 = 0.9
ADAM_B2 = 0.999
ADAM_EPS = 1e-08
ADAM_WD = 0.01
ADAM_STEP = 10
N_DEV = 8
V7X_VMEM_LIMIT_BYTES = 56 * 1024 * 1024
LRU_CB = 256
MESH = pl.DeviceIdType.MESH
ANY = pl.BlockSpec(memory_space=pl.ANY)


def _tile(n, pref):
    t = min(n, pref)
    assert n % t == 0, (n, pref)
    return t


def _dot_nn(a, b):
    return lax.dot_general(a, b, (((1,), (0,)), ((), ())), preferred_element_type=F32)


def _dot_nt(a, b):
    return lax.dot_general(a, b, (((1,), (1,)), ((), ())), preferred_element_type=F32)


def _dot_tn(a, b):
    return lax.dot_general(a, b, (((0,), (0,)), ((), ())), preferred_element_type=F32)


def _row_chunks(n_rows, fn, chunk=256):
    chunk = min(chunk, n_rows)
    assert n_rows % chunk == 0

    def step(r, carry):
        fn(pl.ds(pl.multiple_of(r * chunk, chunk), chunk))
        return carry

    lax.fori_loop(0, n_rows // chunk, step, 0)


def _late_copies(i, tt, pairs, sems):
    rows = pl.ds(pl.multiple_of(i * tt, tt), tt)
    return [pltpu.make_async_copy(hbm.at[rows], buf, sems.at[j]) for j, (hbm, buf) in enumerate(pairs)]


def _sig(x):
    return 1.0 / (1.0 + jnp.exp(-x))


def _rms_hat(x):
    r = lax.rsqrt(jnp.mean(x * x, axis=-1, keepdims=True) + NORM_EPS)
    return x * r, r


def _rms_bwd(dn, xhat, r, g):
    q = dn * g
    dx = r * (q - xhat * jnp.mean(q * xhat, axis=-1, keepdims=True))
    dg = jnp.sum(dn * xhat, axis=0, keepdims=True)
    return dx, dg


_GELU_K = math.sqrt(2.0 / math.pi)
_GELU_C = 0.044715


def _gelu_and_grad(g):
    t = jnp.tanh(_GELU_K * (g + _GELU_C * g * g * g))
    val = 0.5 * g * (1.0 + t)
    grad = 0.5 * (1.0 + t) + 0.5 * g * (1.0 - t * t) * (_GELU_K * (1.0 + 3.0 * _GELU_C * g * g))
    return val, grad


def _softplus_neg(lam):
    z = -lam
    e = jnp.exp(-jnp.abs(z))
    u = 1.0 + e
    d = u - 1.0
    l1p = jnp.where(d == 0.0, e, jnp.log(u) * (e / jnp.where(d == 0.0, 1.0, d)))
    return jnp.maximum(z, 0.0) + l1p


def _lru_gates(xc, wa, ba, wx, bx, lam):
    xcb = xc.astype(BF16)
    r = _sig(_dot_nn(xcb, wa) + ba)
    i = _sig(_dot_nn(xcb, wx) + bx)
    sp = _softplus_neg(lam)
    log_a = (-LRU_C) * r * sp
    a = jnp.exp(log_a)
    mult = jnp.sqrt(-jnp.tanh(log_a) * (1.0 + a * a))
    return xcb, r, i, sp, log_a, a, mult


def _place():
    return lax.axis_index("x"), lax.axis_index("y"), lax.axis_index("c")


def _ag_plan(shards, pieces=None, bufs=None):
    na = len(shards)
    n_kinds = 7

    def parts(ins, outs, sems):
        send_sems, recv_sems, local_sems = sems
        x, y, c = _place()
        me, sibling = (x, y, c), (x, y, 1 - c)
        x_nb, y_nb, diag = (1 - x, y), (x, 1 - y), (1 - x, 1 - y)
        relay_src = (c * (1 - x) + (1 - c) * x, c * y + (1 - c) * (1 - y))
        relay_dst = (c * x + (1 - c) * (1 - x), c * (1 - y) + (1 - c) * y)

        def own(a):
            return ins[a] if pieces is None else ins[a].at[pl.ds(*pieces[a])]

        def slot(a, px, py, pc):
            idx = 4 * px + 2 * py + pc
            return outs[a].at[idx] if pieces is None else outs[a].at[idx, pl.ds(*pieces[a])]

        def copy(a, k, block, to, src=None):
            return pltpu.make_async_remote_copy(
                src_ref=slot(a, *block) if src is None else src,
                dst_ref=slot(a, *block),
                send_sem=send_sems.at[a * n_kinds + k],
                recv_sem=recv_sems.at[a * n_kinds + k],
                device_id=to,
                device_id_type=MESH,
            )

        mine = [pltpu.make_async_copy(own(a), slot(a, *me), local_sems.at[a]) for a in range(na)]
        first, second, third = [], [], []
        for a in range(na):
            first += [copy(a, 0, me, sibling, src=own(a)), copy(a, 1, me, (*x_nb, c), src=own(a)),
                      copy(a, 2, me, (*y_nb, c), src=own(a))]
            second += [copy(a, 3, (*relay_src, c), (*relay_dst, c)), copy(a, 4, (*x_nb, c), sibling),
                       copy(a, 5, (*y_nb, c), sibling)]
            third.append(copy(a, 6, (*diag, c), sibling))
        return sibling, c, x_nb, y_nb, diag, copy, mine, first, second, third

    def start(ins, outs, sems):
        _, _, _, _, _, _, mine, first, _, _ = parts(ins, outs, sems)
        for cp in mine + first:
            cp.start()

    def middle(ins, outs, sems):
        _, c, x_nb, y_nb, _, copy, _, _, second, _ = parts(ins, outs, sems)
        for a in range(na):
            copy(a, 1, (*x_nb, c), (*x_nb, c)).wait_recv()
            copy(a, 2, (*y_nb, c), (*y_nb, c)).wait_recv()
        for cp in second:
            cp.start()

    def finish(ins, outs, sems):
        sibling, c, x_nb, y_nb, diag, copy, mine, first, second, third = parts(ins, outs, sems)
        for a in range(na):
            copy(a, 3, (*diag, c), (*diag, c)).wait_recv()
            third[a].start()
        for a in range(na):
            copy(a, 0, sibling, sibling).wait_recv()
            copy(a, 4, (*x_nb, 1 - c), sibling).wait_recv()
            copy(a, 5, (*y_nb, 1 - c), sibling).wait_recv()
            copy(a, 6, (*diag, 1 - c), sibling).wait_recv()
        for cp in first + second + third:
            cp.wait_send()
        for cp in mine:
            cp.wait()

    return types.SimpleNamespace(
        ins=list(shards) + list(bufs or []),
        out_shapes=[jax.ShapeDtypeStruct((N_DEV,) + s.shape, s.dtype) for s in shards],
        sems=[pltpu.SemaphoreType.DMA((n_kinds * na,)), pltpu.SemaphoreType.DMA((n_kinds * na,)),
              pltpu.SemaphoreType.DMA((na,))],
        aliases=[(na + a, a) for a in range(na)] if bufs else [],
        peers=frozenset({"sibling", "neighbours"}), start=start, middle=middle, finish=finish)


def _rs_sibling_plan(fulls):
    na = len(fulls)
    rs = [f.shape[0] // N_DEV for f in fulls]

    def copies(ins, outs, sems):
        send_sems, recv_sems = sems
        x, y, c = _place()
        out = []
        for a in range(na):
            for q in range(4):
                shard = 2 * q + (1 - c)
                out.append(pltpu.make_async_remote_copy(
                    src_ref=ins[a].at[pl.ds(shard * rs[a], rs[a])],
                    dst_ref=outs[a].at[q],
                    send_sem=send_sems.at[a * 4 + q],
                    recv_sem=recv_sems.at[a * 4 + q],
                    device_id=(x, y, 1 - c),
                    device_id_type=MESH,
                ))
        return out

    def start(ins, outs, sems):
        for cp in copies(ins, outs, sems):
            cp.start()

    def finish(ins, outs, sems):
        for cp in copies(ins, outs, sems):
            cp.wait()

    return types.SimpleNamespace(
        ins=list(fulls),
        out_shapes=[jax.ShapeDtypeStruct((4, r) + f.shape[1:], f.dtype) for r, f in zip(rs, fulls)],
        sems=[pltpu.SemaphoreType.DMA((4 * na,)), pltpu.SemaphoreType.DMA((4 * na,))],
        peers=frozenset({"sibling"}), start=start, finish=finish)


def _rs_chips_plan(sends, pieces=None, bufs=None):
    na = len(sends)

    def copies(ins, outs, sems):
        send_sems, recv_sems = sems
        x, y, c = _place()
        chips = [(1 - x, y), (x, 1 - y), (1 - x, 1 - y)]
        out = []
        for a in range(na):
            for k, chip in enumerate(chips):
                rows = (k,) if pieces is None else (k, pl.ds(*pieces[a]))
                out.append(pltpu.make_async_remote_copy(
                    src_ref=ins[a].at[rows],
                    dst_ref=outs[a].at[rows],
                    send_sem=send_sems.at[a * 3 + k],
                    recv_sem=recv_sems.at[a * 3 + k],
                    device_id=(*chip, c),
                    device_id_type=MESH,
                ))
        return out

    def start(ins, outs, sems):
        for cp in copies(ins, outs, sems):
            cp.start()

    def finish(ins, outs, sems):
        for cp in copies(ins, outs, sems):
            cp.wait()

    return types.SimpleNamespace(
        ins=list(sends) + list(bufs or []),
        out_shapes=[jax.ShapeDtypeStruct(s.shape, s.dtype) for s in sends],
        sems=[pltpu.SemaphoreType.DMA((3 * na,)), pltpu.SemaphoreType.DMA((3 * na,))],
        aliases=[(na + a, a) for a in range(na)] if bufs else [],
        peers=frozenset({"chips"}), start=start, finish=finish)


def _join(plans):
    ins, outs, sems, aliases, offs = [], [], [], [], []
    for p in plans:
        offs.append((len(ins), len(outs), len(sems)))
        aliases += [(len(ins) + ci, len(outs) + co) for ci, co in getattr(p, "aliases", [])]
        ins += p.ins
        outs += p.out_shapes
        sems += p.sems

    def cut(p, off, i, o, s):
        return (i[off[0]:off[0] + len(p.ins)], o[off[1]:off[1] + len(p.out_shapes)],
                s[off[2]:off[2] + len(p.sems)])

    def start(i, o, s):
        for p, off in zip(plans, offs):
            p.start(*cut(p, off, i, o, s))

    def middle(i, o, s):
        for p, off in zip(plans, offs):
            if getattr(p, "middle", None) is not None:
                p.middle(*cut(p, off, i, o, s))

    def finish(i, o, s):
        for p, off in zip(plans, offs):
            p.finish(*cut(p, off, i, o, s))

    def split(results):
        return [list(results[off[1]:off[1] + len(p.out_shapes)]) for p, off in zip(plans, offs)]

    return types.SimpleNamespace(ins=ins, out_shapes=outs, sems=sems, aliases=aliases,
                                 peers=frozenset().union(*[p.peers for p in plans]),
                                 start=start, middle=middle, finish=finish, split=split)


COLLECTIVE_ID = {frozenset({"sibling"}): 0, frozenset({"chips"}): 1, frozenset({"sibling", "chips"}): 2,
                 frozenset({"sibling", "neighbours"}): 3}


def _handshake(peers):
    x, y, c = _place()
    devs = []
    if "sibling" in peers:
        devs.append((x, y, 1 - c))
    if "neighbours" in peers:
        devs += [(1 - x, y, c), (x, 1 - y, c)]
    if "chips" in peers:
        assert "neighbours" not in peers
        devs += [(1 - x, y, c), (x, 1 - y, c), (1 - x, 1 - y, c)]
    barrier = pltpu.get_barrier_semaphore()
    for dev in devs:
        pl.semaphore_signal(barrier, inc=1, device_id=dev, device_id_type=MESH)
    pl.semaphore_wait(barrier, len(devs))


def _in_hbm(args):
    return [pltpu.with_memory_space_constraint(a, pltpu.HBM) for a in args]


def _run_plan(plan, name):
    n_in, n_out = len(plan.ins), len(plan.out_shapes)

    def body(*refs):
        ins, outs, sems = refs[:n_in], refs[n_in:n_in + n_out], refs[n_in + n_out:]
        _handshake(plan.peers)
        plan.start(ins, outs, sems)
        if getattr(plan, "middle", None) is not None:
            plan.middle(ins, outs, sems)
        plan.finish(ins, outs, sems)

    return pl.pallas_call(
        body,
        name=name,
        in_specs=[ANY] * n_in,
        out_specs=[ANY] * n_out,
        out_shape=plan.out_shapes,
        scratch_shapes=plan.sems,
        input_output_aliases=dict(getattr(plan, "aliases", [])),
        compiler_params=pltpu.CompilerParams(collective_id=COLLECTIVE_ID[plan.peers]),
    )(*_in_hbm(plan.ins))


def _call(body, *, name, grid, in_specs, out_specs, out_shape, args, scratch_shapes=(), aliases=None,
          carry=None):
    n_in, n_out, n_scr = len(in_specs), len(out_shape), len(scratch_shapes)
    params = pltpu.CompilerParams(
        dimension_semantics=("arbitrary",) * len(grid), vmem_limit_bytes=V7X_VMEM_LIMIT_BYTES)
    if carry is None:
        outs = pl.pallas_call(
            body, name=name, grid=grid, in_specs=list(in_specs), out_specs=list(out_specs),
            out_shape=list(out_shape), scratch_shapes=list(scratch_shapes),
            input_output_aliases=aliases or {}, compiler_params=params)(*_in_hbm(args))
        return list(outs), []
    c_in, c_out = len(carry.ins), len(carry.out_shapes)

    def full(*refs):
        p = 0
        ins = refs[p:p + n_in]
        p += n_in
        cins = refs[p:p + c_in]
        p += c_in
        outs = refs[p:p + n_out]
        p += n_out
        couts = refs[p:p + c_out]
        p += c_out
        scr = refs[p:p + n_scr]
        csems = refs[p + n_scr:]
        ids = [pl.program_id(a) for a in range(len(grid))]
        first = functools.reduce(operator.and_, [i == 0 for i in ids])
        last = functools.reduce(operator.and_, [i == g - 1 for i, g in zip(ids, grid)])

        @pl.when(first)
        def _():
            _handshake(carry.peers)
            carry.start(cins, couts, csems)

        if getattr(carry, "middle", None) is not None:
            n_steps = math.prod(grid)
            flat = functools.reduce(lambda acc, ig: acc * ig[1] + ig[0], zip(ids, grid), 0)

            @pl.when(flat == (2 * n_steps) // 3)
            def _():
                carry.middle(cins, couts, csems)

        body(*ins, *outs, *scr)

        @pl.when(last)
        def _():
            carry.finish(cins, couts, csems)

    all_aliases = dict(aliases or {})
    all_aliases.update({n_in + ci: n_out + co for ci, co in getattr(carry, "aliases", [])})
    params = pltpu.CompilerParams(
        dimension_semantics=("arbitrary",) * len(grid), vmem_limit_bytes=V7X_VMEM_LIMIT_BYTES,
        collective_id=COLLECTIVE_ID[carry.peers])
    outs = pl.pallas_call(
        full, name=name, grid=grid,
        in_specs=list(in_specs) + [ANY] * c_in,
        out_specs=list(out_specs) + [ANY] * c_out,
        out_shape=list(out_shape) + list(carry.out_shapes),
        scratch_shapes=list(scratch_shapes) + list(carry.sems),
        input_output_aliases=all_aliases, compiler_params=params)(*_in_hbm(args), *_in_hbm(carry.ins))
    return list(outs[:n_out]), list(outs[n_out:])


def _norm_proj(x, g1, w_int, carry=None):
    t, d = x.shape
    n = w_int.shape[0]
    tt, tn = _tile(t, 2048), _tile(n, 512)

    chunk = 256
    n_chunks = tt // chunk
    nj = n // tn

    def body(x_hbm, g_ref, w_ref, proj_ref, h1_hbm, h1_s, x_s, x_sems, h1_sem):
        i, j = pl.program_id(0), pl.program_id(1)
        h1_out = pltpu.make_async_copy(h1_s, h1_hbm.at[pl.ds(pl.multiple_of(i * tt, tt), tt)], h1_sem.at[0])

        @pl.when(j == 0)
        def _():
            fetch = [pltpu.make_async_copy(
                x_hbm.at[pl.ds(pl.multiple_of(i * tt, tt) + c * chunk, chunk)],
                x_s.at[pl.ds(c * chunk, chunk)], x_sems.at[c]) for c in range(n_chunks)]
            for cp in fetch:
                cp.start()
            for c, cp in enumerate(fetch):
                cp.wait()
                rows = pl.ds(c * chunk, chunk)
                xhat, _ = _rms_hat(x_s[rows, :])
                h1_s[rows, :] = (xhat * g_ref[...]).astype(BF16)
            h1_out.start()

        proj_ref[...] = _dot_nt(h1_s[...], w_ref[...]).astype(BF16)

        @pl.when(j == nj - 1)
        def _():
            h1_out.wait()

    return _call(
        body, name="norm_proj", grid=(t // tt, nj),
        in_specs=[
            ANY,
            pl.BlockSpec((1, d), lambda i, j: (0, 0)),
            pl.BlockSpec((tn, d), lambda i, j: (j, 0)),
        ],
        out_specs=[pl.BlockSpec((tt, tn), lambda i, j: (i, j)), ANY],
        out_shape=[jax.ShapeDtypeStruct((t, n), BF16), jax.ShapeDtypeStruct((t, d), BF16)],
        scratch_shapes=[pltpu.VMEM((tt, d), BF16), pltpu.VMEM((tt, d), F32),
                        pltpu.SemaphoreType.DMA((n_chunks,)), pltpu.SemaphoreType.DMA((1,))],
        args=(x, g1, w_int), carry=carry)


def _scan_rows(av, bv, reverse):
    tc = av.shape[0]
    row = lax.broadcasted_iota(jnp.int32, av.shape, 0)
    s = 1
    while s < tc:
        if s < 8:
            keep = (row < tc - s) if reverse else (row >= s)
            shift = (tc - s) if reverse else s
            a_sh = jnp.where(keep, pltpu.roll(av, shift, 0), 1.0)
            b_sh = jnp.where(keep, pltpu.roll(bv, shift, 0), 0.0)
            bv = av * b_sh + bv
            av = av * a_sh
        elif reverse:
            bv = jnp.concatenate([av[:tc - s] * bv[s:] + bv[:tc - s], bv[tc - s:]], axis=0)
            av = jnp.concatenate([av[:tc - s] * av[s:], av[tc - s:]], axis=0)
        else:
            bv = jnp.concatenate([bv[:s], av[s:] * bv[:tc - s] + bv[s:]], axis=0)
            av = jnp.concatenate([av[:s], av[s:] * av[:tc - s]], axis=0)
        s *= 2
    return av, bv


N_LRU_SAVED = 5


def _fill_block_diag(w_ref, bd_ref):
    bd_ref[...] = jnp.zeros_like(bd_ref)
    hd = LRU_HEAD_DIM
    for k in range(w_ref.shape[0]):
        bd_ref[k * hd:(k + 1) * hd, k * hd:(k + 1) * hd] = w_ref[k].astype(BF16)


def _lru_fwd(proj, conv_w, conv_b, w_a, b_a, w_x, b_x, lam, carry=None):
    t = proj.shape[0]
    dr = conv_b.shape[1]
    cb = LRU_CB
    tc = _tile(t, 256)
    ncb, ntc = dr // cb, t // tc

    def body(xp_ref, g_ref, cw_ref, cb_ref, wa_ref, ba_ref, wx_ref, bx_ref, lam_ref,
             y_ref, h_ref, saved_ref, prevx_s, hlast_s, wa_s, wx_s):
        c = pl.program_id(1)

        @pl.when(c == 0)
        def _():
            prevx_s[...] = jnp.zeros_like(prevx_s)
            hlast_s[...] = jnp.zeros_like(hlast_s)
            _fill_block_diag(wa_ref, wa_s)
            _fill_block_diag(wx_ref, wx_s)

        x = xp_ref[...].astype(F32)
        prev = prevx_s[...]
        row = lax.broadcasted_iota(jnp.int32, x.shape, 0)

        def sh(j):
            return jnp.where(row >= j, pltpu.roll(x, j, 0), pltpu.roll(prev, j, 0))

        xc = (cb_ref[...] + cw_ref[0:1, :] * sh(3) + cw_ref[1:2, :] * sh(2)
              + cw_ref[2:3, :] * sh(1) + cw_ref[3:4, :] * x)
        prevx_s[...] = x
        _, r, i, _, _, a, mult = _lru_gates(xc, wa_s[...], ba_ref[...], wx_s[...], bx_ref[...],
                                            lam_ref[...])
        for k, val in enumerate((xc, r, i, a, mult)):
            saved_ref[:, k * cb:(k + 1) * cb] = val
        av, bv = _scan_rows(a, mult * (i * xc), reverse=False)
        h = av * hlast_s[...] + bv
        h_ref[...] = h
        hlast_s[...] = h_ref[tc - 1:tc, :]
        gel, _ = _gelu_and_grad(g_ref[...].astype(F32))
        y_ref[...] = (h * gel).astype(BF16)

    vec = pl.BlockSpec((1, cb), lambda j, c: (0, j))
    blk = pl.BlockSpec((tc, cb), lambda j, c: (c, j))
    mat = pl.BlockSpec((cb // LRU_HEAD_DIM, LRU_HEAD_DIM, LRU_HEAD_DIM), lambda j, c: (j, 0, 0))
    return _call(
        body, name="lru_fwd", grid=(ncb, ntc),
        in_specs=[
            blk,
            pl.BlockSpec((tc, cb), lambda j, c: (c, ncb + j)),
            pl.BlockSpec((4, cb), lambda j, c: (0, j)),
            vec, mat, vec, mat, vec, vec,
        ],
        out_specs=[blk, blk, pl.BlockSpec((tc, N_LRU_SAVED * cb), lambda j, c: (c, j))],
        out_shape=[jax.ShapeDtypeStruct((t, dr), BF16), jax.ShapeDtypeStruct((t, dr), F32),
                   jax.ShapeDtypeStruct((t, N_LRU_SAVED * dr), F32)],
        scratch_shapes=[pltpu.VMEM((tc, cb), F32), pltpu.VMEM((1, cb), F32),
                        pltpu.VMEM((cb, cb), BF16), pltpu.VMEM((cb, cb), BF16)],
        args=(proj, proj, conv_w, conv_b, w_a, b_a, w_x, b_x, lam), carry=carry)


def _pool_select(col, vals):
    out = vals[3]
    for g in (2, 1, 0):
        out = jnp.where(col < (g + 1) * POOL_GROUP_DIM, vals[g], out)
    return out


def _pool_fwd(proj, pool_w, pool_scale, col_block):
    t = proj.shape[0]
    dp = pool_scale.shape[1]
    tc = _tile(t, 256)
    ntc = t // tc

    def body(x_ref, w_ref, sc_ref, y_ref, p_ref, px, p2, p4, p8):
        c = pl.program_id(0)

        @pl.when(c == 0)
        def _():
            for s in (px, p2, p4, p8):
                s[...] = jnp.zeros_like(s)

        x = x_ref[...].astype(F32)
        row = lax.broadcasted_iota(jnp.int32, x.shape, 0)
        col = lax.broadcasted_iota(jnp.int32, x.shape, 1)

        def sh(v, pv, j):
            return jnp.where(row >= j, pltpu.roll(v, j, 0), pltpu.roll(pv[...], j, 0))

        s2 = x + sh(x, px, 1)
        s4 = s2 + sh(s2, p2, 2)
        s8 = s4 + sh(s4, p4, 4)
        s16 = s8 + sh(s8, p8, 8)
        px[...] = x
        p2[...] = s2
        p4[...] = s4
        p8[...] = s8
        wsum = _pool_select(col, (s2, s4, s8, s16))
        win = _pool_select(col, POOL_WINDOWS)
        cnt = jnp.minimum(c * tc + row + 1, win).astype(F32)
        p = wsum / cnt - x
        pb = p.astype(BF16)
        p_ref[...] = pb
        for g in range(len(POOL_WINDOWS)):
            sl = slice(g * POOL_GROUP_DIM, (g + 1) * POOL_GROUP_DIM)
            yg = _dot_nn(pb[:, sl], w_ref[g]) * sc_ref[:, sl]
            y_ref[:, sl] = yg.astype(BF16)

    return _call(
        body, name="pool_fwd", grid=(ntc,),
        in_specs=[
            pl.BlockSpec((tc, dp), lambda c: (c, col_block)),
            pl.BlockSpec(pool_w.shape, lambda c: (0, 0, 0)),
            pl.BlockSpec((1, dp), lambda c: (0, 0)),
        ],
        out_specs=[pl.BlockSpec((tc, dp), lambda c: (c, 0))] * 2,
        out_shape=[jax.ShapeDtypeStruct((t, dp), BF16)] * 2,
        scratch_shapes=[pltpu.VMEM((tc, dp), F32)] * 4,
        args=(proj, pool_w, pool_scale))[0]


def _branch_mix(y_lru, y_pool, w_lru_up, w_pool_upb, proj, b_gate, ga_block, gb_block, carry=None):
    t, d = y_lru.shape
    dp = y_pool.shape[1]
    bw = w_pool_upb.shape[2]
    tt, tn = _tile(t, 1024), 512
    nj = d // tn

    def body(yl_ref, yp_ref, wl_ref, wp_ref, ga_ref, gb_ref, ba_ref, bb_ref, bra_ref, brb_ref, mix_ref):
        br_a = _dot_nn(yl_ref[...], wl_ref[...])
        wp = jnp.concatenate([wp_ref[b] for b in range(tn // bw)], axis=1)
        br_b = _dot_nn(yp_ref[...], wp)
        bra_ref[...] = br_a.astype(BF16)
        brb_ref[...] = br_b.astype(BF16)
        ga = _sig(ga_ref[...].astype(F32) + ba_ref[...])
        gb = _sig(gb_ref[...].astype(F32) + bb_ref[...])
        mix_ref[...] = (ga * br_a + gb * br_b).astype(BF16)

    out = pl.BlockSpec((tt, tn), lambda j, i: (i, j))
    return _call(
        body, name="branch_mix", grid=(nj, t // tt),
        in_specs=[
            pl.BlockSpec((tt, d), lambda j, i: (i, 0)),
            pl.BlockSpec((tt, dp), lambda j, i: (i, 0)),
            pl.BlockSpec((d, tn), lambda j, i: (0, j)),
            pl.BlockSpec((tn // bw, dp, bw), lambda j, i: (j, 0, 0)),
            pl.BlockSpec((tt, tn), lambda j, i: (i, ga_block + j)),
            pl.BlockSpec((tt, tn), lambda j, i: (i, gb_block + j)),
            pl.BlockSpec((1, tn), lambda j, i: (0, j)),
            pl.BlockSpec((1, tn), lambda j, i: (0, nj + j)),
        ],
        out_specs=[out, out, out],
        out_shape=[jax.ShapeDtypeStruct((t, d), BF16)] * 3,
        args=(y_lru, y_pool, w_lru_up, w_pool_upb, proj, proj, b_gate, b_gate), carry=carry)


def _wo_norm(mix, w_o, x, g2, g3, carry=None):
    t, d = x.shape
    tt = _tile(t, 512)

    def body(mix_ref, w_ref, x_ref, g2_ref, g3_ref, m_ref, x2_ref, h3_ref):
        m = _dot_nn(mix_ref[...], w_ref[...])
        m_ref[...] = m
        mhat, _ = _rms_hat(m)
        x2 = x_ref[...] + mhat * g2_ref[...]
        x2_ref[...] = x2
        xhat, _ = _rms_hat(x2)
        h3_ref[...] = (xhat * g3_ref[...]).astype(BF16)

    row = pl.BlockSpec((tt, d), lambda i: (i, 0))
    vec = pl.BlockSpec((1, d), lambda i: (0, 0))
    return _call(
        body, name="wo_norm", grid=(t // tt,),
        in_specs=[row, pl.BlockSpec((d, d), lambda i: (0, 0)), row, vec, vec],
        out_specs=[row, row, row],
        out_shape=[
            jax.ShapeDtypeStruct((t, d), F32),
            jax.ShapeDtypeStruct((t, d), F32),
            jax.ShapeDtypeStruct((t, d), BF16),
        ],
        args=(mix, w_o, x, g2, g3), carry=carry)


def _ff1(h3, w_ff1b, carry=None):
    t, d = h3.shape
    nb, _, tn = w_ff1b.shape
    tt = _tile(t, 2048)

    def body(h_ref, w_ref, rf_ref):
        rf_ref[...] = jnp.maximum(_dot_nn(h_ref[...], w_ref[...]), 0.0).astype(BF16)

    out = pl.BlockSpec((tt, tn), lambda i, j: (i, j))
    return _call(
        body, name="ff1", grid=(t // tt, nb),
        in_specs=[pl.BlockSpec((tt, d), lambda i, j: (i, 0)), pl.BlockSpec((None, d, tn), lambda i, j: (j, 0, 0))],
        out_specs=[out],
        out_shape=[jax.ShapeDtypeStruct((t, nb * tn), BF16)],
        args=(h3, w_ff1b), carry=carry)


def _ff2_loss(rf, w_ff2, x2, g4, target):
    t, k = rf.shape
    d = x2.shape[1]
    tt, tk = _tile(t, 1024), _tile(k, 1024)
    nk = k // tk

    def body(a_ref, w_ref, x2_ref, g_ref, tg_ref, dy_ref, df_ref, dg_ref, loss_ref, acc):
        i, kk = pl.program_id(0), pl.program_id(1)

        @pl.when(kk == 0)
        def _():
            acc[...] = jnp.zeros_like(acc)

        @pl.when((i == 0) & (kk == 0))
        def _():
            dg_ref[...] = jnp.zeros_like(dg_ref)
            loss_ref[...] = jnp.zeros_like(loss_ref)

        rf_tile = a_ref[...]
        acc[...] += _dot_nn(rf_tile * rf_tile, w_ref[...])

        @pl.when(kk == nk - 1)
        def _():
            def tail(rows):
                fhat, r = _rms_hat(acc[rows, :])
                g = g_ref[...]
                e = x2_ref[rows, :] + fhat * g - tg_ref[rows, :]
                loss_ref[...] += 0.5 * jnp.sum(jnp.mean(e * e, axis=-1, keepdims=True))
                dy = e * (1.0 / d)
                dy_ref[rows, :] = dy.astype(BF16)
                df, dg = _rms_bwd(dy, fhat, r, g)
                df_ref[rows, :] = df.astype(BF16)
                dg_ref[...] += dg

            _row_chunks(tt, tail)

    row = pl.BlockSpec((tt, d), lambda i, kk: (i, 0))
    vec = pl.BlockSpec((1, d), lambda i, kk: (0, 0))
    return _call(
        body, name="ff2_loss", grid=(t // tt, nk),
        in_specs=[
            pl.BlockSpec((tt, tk), lambda i, kk: (i, kk)),
            pl.BlockSpec((tk, d), lambda i, kk: (kk, 0)),
            row, vec, row,
        ],
        out_specs=[row, row, vec, pl.BlockSpec((1, 128), lambda i, kk: (0, 0))],
        out_shape=[
            jax.ShapeDtypeStruct((t, d), BF16),
            jax.ShapeDtypeStruct((t, d), BF16),
            jax.ShapeDtypeStruct((1, d), F32),
            jax.ShapeDtypeStruct((1, 128), F32),
        ],
        scratch_shapes=[pltpu.VMEM((tt, d), F32)],
        args=(rf, w_ff2, x2, g4, target))[0]


def _ff2_bwd(df, w_ff2, rf, carry=None):
    t, d = df.shape
    n = w_ff2.shape[0]
    tt, tn = _tile(t, 2048), _tile(n, 512)

    def body(df_ref, w_ref, rf_ref, out_ref):
        d_act = _dot_nt(df_ref[...], w_ref[...])
        out_ref[...] = (d_act * (2.0 * rf_ref[...].astype(F32))).astype(BF16)

    blk = pl.BlockSpec((tt, tn), lambda i, j: (i, j))
    return _call(
        body, name="ff2_bwd", grid=(t // tt, n // tn),
        in_specs=[pl.BlockSpec((tt, d), lambda i, j: (i, 0)), pl.BlockSpec((tn, d), lambda i, j: (j, 0)), blk],
        out_specs=[blk],
        out_shape=[jax.ShapeDtypeStruct((t, n), BF16)],
        args=(df, w_ff2, rf), carry=carry)


def _wgrad(a, b, name, prev=None, row_off=0, rows=None, carry=None, square_a=False):
    t, m = a.shape
    n = b.shape[1]
    rows = m if rows is None else rows
    tm, tk = _tile(m, 512), _tile(t, 2048)
    nk = t // tk
    assert row_off % tm == 0
    off = row_off // tm

    def body(*refs):
        a_ref, b_ref = refs[0], refs[1]
        o32_ref, o16_ref, acc = refs[-3], refs[-2], refs[-1]
        kk = pl.program_id(1)

        @pl.when(kk == 0)
        def _():
            acc[...] = jnp.zeros_like(acc)

        a_tile = a_ref[...]
        acc[...] += _dot_tn(a_tile * a_tile if square_a else a_tile, b_ref[...])

        @pl.when(kk == nk - 1)
        def _():
            o32_ref[...] = acc[...]
            o16_ref[...] = acc[...].astype(BF16)

    in_specs = [pl.BlockSpec((tk, tm), lambda i, kk: (kk, i)), pl.BlockSpec((tk, n), lambda i, kk: (kk, 0))]
    args = [a, b]
    aliases = {}
    if prev is not None:
        in_specs += [ANY, ANY]
        args += list(prev)
        aliases = {2: 0, 3: 1}
    out = pl.BlockSpec((tm, n), lambda i, kk: (off + i, 0))
    return _call(
        body, name=name, grid=(m // tm, nk),
        in_specs=in_specs, out_specs=[out, out],
        out_shape=[jax.ShapeDtypeStruct((rows, n), F32), jax.ShapeDtypeStruct((rows, n), BF16)],
        scratch_shapes=[pltpu.VMEM((tm, n), F32)],
        aliases=aliases, args=args, carry=carry)


def _wgrad_parts(parts, b, name, carry=None):
    t, n = b.shape
    tm = 512
    bounds = []
    lo = 0
    for part in parts:
        assert part.shape[0] == t and part.shape[1] % tm == 0
        bounds.append((lo, lo + part.shape[1] // tm))
        lo += part.shape[1] // tm
    nm = lo
    np_ = len(parts)

    def body(*refs):
        p_refs, b_ref, o32_ref, o16_ref = refs[:np_], refs[np_], refs[np_ + 1], refs[np_ + 2]
        i = pl.program_id(0)
        for (lo_p, hi_p), p_ref in zip(bounds, p_refs):
            @pl.when((i >= lo_p) & (i < hi_p))
            def _(p_ref=p_ref):
                res = _dot_tn(p_ref[...], b_ref[...])
                o32_ref[...] = res
                o16_ref[...] = res.astype(BF16)

    def part_spec(lo_p, hi_p):
        return pl.BlockSpec((t, tm), lambda i: (0, jnp.clip(i - lo_p, 0, hi_p - lo_p - 1)))

    out = pl.BlockSpec((tm, n), lambda i: (i, 0))
    return _call(
        body, name=name, grid=(nm,),
        in_specs=[part_spec(lo_p, hi_p) for lo_p, hi_p in bounds] + [pl.BlockSpec((t, n), lambda i: (0, 0))],
        out_specs=[out, out],
        out_shape=[jax.ShapeDtypeStruct((nm * tm, n), F32), jax.ShapeDtypeStruct((nm * tm, n), BF16)],
        args=(*parts, b), carry=carry)


def _wgrad_cols(a, b, bw, tn, name, carry=None):
    t, m = a.shape
    n = b.shape[1]
    per_step = tn // bw

    def body(a_ref, b_ref, o32_ref, o16_ref):
        res = _dot_tn(a_ref[...], b_ref[...])
        for blk in range(per_step):
            part = res[:, blk * bw:(blk + 1) * bw]
            o32_ref[blk] = part
            o16_ref[blk] = part.astype(BF16)

    out = pl.BlockSpec((per_step, m, bw), lambda j: (j, 0, 0))
    return _call(
        body, name=name, grid=(n // tn,),
        in_specs=[pl.BlockSpec((t, m), lambda j: (0, 0)), pl.BlockSpec((t, tn), lambda j: (0, j))],
        out_specs=[out, out],
        out_shape=[jax.ShapeDtypeStruct((n // bw, m, bw), F32), jax.ShapeDtypeStruct((n // bw, m, bw), BF16)],
        args=(a, b), carry=carry)


def _ff1_bwd_norms(d_f1, w_ff1b, dy, x2, g3, m, g2, carry=None):
    t, k = d_f1.shape
    d = x2.shape[1]
    assert dy.dtype == BF16 and x2.dtype == F32
    bw = w_ff1b.shape[2]
    per_step = 2
    tt, tk = _tile(t, 1024), per_step * bw
    nk = k // tk

    def body(a_ref, w_ref, dy_hbm, x2_hbm, g3_ref, m_hbm, g2_ref, dx2_hbm, dm_hbm, dg3_ref, dg2_ref, acc,
             dy_ref, x2_ref, m_ref, late_sems, out_sems):
        i, kk = pl.program_id(0), pl.program_id(1)
        late = _late_copies(i, tt, [(dy_hbm, dy_ref), (x2_hbm, x2_ref), (m_hbm, m_ref)], late_sems)

        @pl.when(kk == 0)
        def _():
            acc[...] = jnp.zeros_like(acc)
            for cp in late:
                cp.start()

        @pl.when((i == 0) & (kk == 0))
        def _():
            dg3_ref[...] = jnp.zeros_like(dg3_ref)
            dg2_ref[...] = jnp.zeros_like(dg2_ref)

        a_tile = a_ref[...]
        for b in range(per_step):
            acc[...] += _dot_nt(a_tile[:, b * bw:(b + 1) * bw], w_ref[b])

        @pl.when(kk == nk - 1)
        def _():
            for cp in late:
                cp.wait()

            def tail(rows):
                xhat, r3 = _rms_hat(x2_ref[rows, :])
                dx, dg3 = _rms_bwd(acc[rows, :], xhat, r3, g3_ref[...])
                dx2 = dy_ref[rows, :].astype(F32) + dx
                x2_ref[rows, :] = dx2
                dg3_ref[...] += dg3
                mhat, r2 = _rms_hat(m_ref[rows, :])
                dm, dg2 = _rms_bwd(dx2, mhat, r2, g2_ref[...])
                dy_ref[rows, :] = dm.astype(BF16)
                dg2_ref[...] += dg2

            _row_chunks(tt, tail)
            tile = pl.ds(pl.multiple_of(i * tt, tt), tt)
            outs = [pltpu.make_async_copy(x2_ref, dx2_hbm.at[tile], out_sems.at[0]),
                    pltpu.make_async_copy(dy_ref, dm_hbm.at[tile], out_sems.at[1])]
            for cp in outs:
                cp.start()
            for cp in outs:
                cp.wait()

    vec = pl.BlockSpec((1, d), lambda i, kk: (0, 0))
    return _call(
        body, name="ff1_bwd_norms", grid=(t // tt, nk),
        in_specs=[
            pl.BlockSpec((tt, tk), lambda i, kk: (i, kk)),
            pl.BlockSpec((per_step, d, bw), lambda i, kk: (kk, 0, 0)),
            ANY, ANY, vec, ANY, vec,
        ],
        out_specs=[ANY, ANY, vec, vec],
        out_shape=[
            jax.ShapeDtypeStruct((t, d), F32),
            jax.ShapeDtypeStruct((t, d), BF16),
            jax.ShapeDtypeStruct((1, d), F32),
            jax.ShapeDtypeStruct((1, d), F32),
        ],
        scratch_shapes=[pltpu.VMEM((tt, d), F32), pltpu.VMEM((tt, d), dy.dtype), pltpu.VMEM((tt, d), F32),
                        pltpu.VMEM((tt, d), F32), pltpu.SemaphoreType.DMA((3,)), pltpu.SemaphoreType.DMA((2,))],
        args=(d_f1, w_ff1b, dy, x2, g3, m, g2), carry=carry)


def _wo_bwd_mix(dm, w_o, br_a, br_b, proj, b_gate, ga_block, gb_block, carry=None):
    t, d = dm.shape
    tt, tn = _tile(t, 1024), 512
    nj = d // tn

    def body(dm_ref, w_ref, bra_ref, brb_ref, ga_ref, gb_ref, ba_ref, bb_ref,
             dbra_ref, dbrb_ref, dga_ref, dgb_ref, dba_ref, dbb_ref):
        i = pl.program_id(1)

        @pl.when(i == 0)
        def _():
            dba_ref[...] = jnp.zeros_like(dba_ref)
            dbb_ref[...] = jnp.zeros_like(dbb_ref)

        d_mix = _dot_nt(dm_ref[...], w_ref[...])
        ga = _sig(ga_ref[...].astype(F32) + ba_ref[...])
        gb = _sig(gb_ref[...].astype(F32) + bb_ref[...])
        dbra_ref[...] = (d_mix * ga).astype(BF16)
        dbrb_ref[...] = (d_mix * gb).astype(BF16)
        dga = d_mix * bra_ref[...].astype(F32) * (ga * (1.0 - ga))
        dgb = d_mix * brb_ref[...].astype(F32) * (gb * (1.0 - gb))
        dga_ref[...] = dga.astype(BF16)
        dgb_ref[...] = dgb.astype(BF16)
        dba_ref[...] += jnp.sum(dga, axis=0, keepdims=True)
        dbb_ref[...] += jnp.sum(dgb, axis=0, keepdims=True)

    blk = pl.BlockSpec((tt, tn), lambda j, i: (i, j))
    vec = pl.BlockSpec((1, tn), lambda j, i: (0, j))
    return _call(
        body, name="wo_bwd_mix", grid=(nj, t // tt),
        in_specs=[
            pl.BlockSpec((tt, d), lambda j, i: (i, 0)),
            pl.BlockSpec((tn, d), lambda j, i: (j, 0)),
            blk, blk,
            pl.BlockSpec((tt, tn), lambda j, i: (i, ga_block + j)),
            pl.BlockSpec((tt, tn), lambda j, i: (i, gb_block + j)),
            vec,
            pl.BlockSpec((1, tn), lambda j, i: (0, nj + j)),
        ],
        out_specs=[blk, blk, blk, blk, vec, vec],
        out_shape=[jax.ShapeDtypeStruct((t, d), BF16)] * 4 + [jax.ShapeDtypeStruct((1, d), F32)] * 2,
        args=(dm, w_o, br_a, br_b, proj, proj, b_gate, b_gate), carry=carry)


def _lru_up_bwd(d_br_a, w_lru_up, proj, h, g_block, carry=None):
    t, d = d_br_a.shape
    tt, tn = _tile(t, 1024), 512

    def body(a_ref, w_ref, g_ref, h_ref, dh_ref, dg_ref):
        d_y = _dot_nt(a_ref[...], w_ref[...])
        gel, gel_grad = _gelu_and_grad(g_ref[...].astype(F32))
        dh_ref[...] = d_y * gel
        dg_ref[...] = (d_y * h_ref[...] * gel_grad).astype(BF16)

    blk = pl.BlockSpec((tt, tn), lambda i, j: (i, j))
    return _call(
        body, name="lru_up_bwd", grid=(t // tt, d // tn),
        in_specs=[
            pl.BlockSpec((tt, d), lambda i, j: (i, 0)),
            pl.BlockSpec((tn, d), lambda i, j: (j, 0)),
            pl.BlockSpec((tt, tn), lambda i, j: (i, g_block + j)),
            blk,
        ],
        out_specs=[blk, blk],
        out_shape=[jax.ShapeDtypeStruct((t, d), F32), jax.ShapeDtypeStruct((t, d), BF16)],
        args=(d_br_a, w_lru_up, proj, h), carry=carry)


def _lru_bwd(dh, h, saved, proj, conv_w, w_a, w_x, lam, carry=None):
    t, dr = dh.shape
    cb = LRU_CB
    hd = LRU_HEAD_DIM
    per = cb // hd
    tc = _tile(t, 256)
    ncb, ntc = dr // cb, t // tc

    def body(dh_ref, h_ref, hp_ref, saved_ref, xp_ref, cw_ref, wa_ref, wx_ref,
             lam_ref, dxp_ref, dwa_ref, dba_ref, dwx_ref, dbx_ref, dlam_ref, dcw_ref, dcb_ref,
             nextd_s, anext_s, gnext_s, tmp_s, wa_s, wx_s):
        c = pl.program_id(1)
        rc = ntc - 1 - c

        @pl.when(c == 0)
        def _():
            nextd_s[...] = jnp.zeros_like(nextd_s)
            anext_s[...] = jnp.zeros_like(anext_s)
            gnext_s[...] = jnp.zeros_like(gnext_s)
            for ref in (dwa_ref, dba_ref, dwx_ref, dbx_ref, dlam_ref, dcw_ref, dcb_ref):
                ref[...] = jnp.zeros_like(ref)
            _fill_block_diag(wa_ref, wa_s)
            _fill_block_diag(wx_ref, wx_s)

        xc, r, i, a, mult = [saved_ref[:, k * cb:(k + 1) * cb] for k in range(N_LRU_SAVED)]
        wa, wx, lam = wa_s[...], wx_s[...], lam_ref[...]
        xcb = xc.astype(BF16)
        sp = _softplus_neg(lam)
        row = lax.broadcasted_iota(jnp.int32, xc.shape, 0)
        h = h_ref[...]
        hp = jnp.where(rc == 0, 0.0, hp_ref[...])
        hprev = jnp.where(row >= 1, pltpu.roll(h, 1, 0), pltpu.roll(hp, 1, 0))

        def up(v, nv, j):
            return jnp.where(row < tc - j, pltpu.roll(v, tc - j, 0), nv)

        av, bv = _scan_rows(up(a, anext_s[...], 1), dh_ref[...], reverse=True)
        gt = av * gnext_s[...] + bv
        tmp_s[...] = gt
        gnext_s[...] = tmp_s[0:1, :]
        tmp_s[...] = a
        anext_s[...] = tmp_s[0:1, :]

        da = gt * hprev
        ixc = i * xc
        d_mult = gt * ixc
        d_i = gt * mult * xc
        d_xc = gt * mult * i
        d_log_a = da * a - d_mult * (a * a) / mult
        d_pre_r = (d_log_a * ((-LRU_C) * sp)) * (r * (1.0 - r))
        d_pre_i = d_i * (i * (1.0 - i))
        d_sp = jnp.sum(d_log_a * ((-LRU_C) * r), axis=0, keepdims=True)
        dlam_ref[...] += d_sp * (-1.0 / (1.0 + jnp.exp(lam)))
        dpr = d_pre_r.astype(BF16)
        dpi = d_pre_i.astype(BF16)
        dba_ref[...] += jnp.sum(d_pre_r, axis=0, keepdims=True)
        dbx_ref[...] += jnp.sum(d_pre_i, axis=0, keepdims=True)
        pa = _dot_tn(xcb, dpr)
        px = _dot_tn(xcb, dpi)
        for k in range(per):
            dwa_ref[k] += pa[k * hd:(k + 1) * hd, k * hd:(k + 1) * hd]
            dwx_ref[k] += px[k * hd:(k + 1) * hd, k * hd:(k + 1) * hd]
        d_xc = d_xc + _dot_nt(dpr, wa) + _dot_nt(dpi, wx)

        nxt = nextd_s[...]
        xp = xp_ref[...].astype(F32)
        dxp = cw_ref[3:4, :] * d_xc
        dcw_ref[3:4, :] += jnp.sum(xp * d_xc, axis=0, keepdims=True)
        for j in (1, 2, 3):
            uj = up(d_xc, pltpu.roll(nxt, tc - j, 0), j)
            dxp = dxp + cw_ref[3 - j:4 - j, :] * uj
            dcw_ref[3 - j:4 - j, :] += jnp.sum(xp * uj, axis=0, keepdims=True)
        dcb_ref[...] += jnp.sum(d_xc, axis=0, keepdims=True)
        nextd_s[...] = d_xc
        dxp_ref[...] = dxp.astype(BF16)

    vec = pl.BlockSpec((1, cb), lambda j, c: (0, j))
    blk = pl.BlockSpec((tc, cb), lambda j, c: (ntc - 1 - c, j))
    mat = pl.BlockSpec((per, hd, hd), lambda j, c: (j, 0, 0))
    cwb = pl.BlockSpec((4, cb), lambda j, c: (0, j))
    return _call(
        body, name="lru_bwd", grid=(ncb, ntc),
        in_specs=[
            blk, blk,
            pl.BlockSpec((tc, cb), lambda j, c: (jnp.maximum(ntc - 2 - c, 0), j)),
            pl.BlockSpec((tc, N_LRU_SAVED * cb), lambda j, c: (ntc - 1 - c, j)),
            blk, cwb, mat, mat, vec,
        ],
        out_specs=[blk, mat, vec, mat, vec, vec, cwb, vec],
        out_shape=[
            jax.ShapeDtypeStruct((t, dr), BF16),
            jax.ShapeDtypeStruct(w_a.shape, F32),
            jax.ShapeDtypeStruct((1, dr), F32),
            jax.ShapeDtypeStruct(w_x.shape, F32),
            jax.ShapeDtypeStruct((1, dr), F32),
            jax.ShapeDtypeStruct((1, dr), F32),
            jax.ShapeDtypeStruct((4, dr), F32),
            jax.ShapeDtypeStruct((1, dr), F32),
        ],
        scratch_shapes=[
            pltpu.VMEM((tc, cb), F32),
            pltpu.VMEM((1, cb), F32),
            pltpu.VMEM((1, cb), F32),
            pltpu.VMEM((tc, cb), F32),
            pltpu.VMEM((cb, cb), BF16),
            pltpu.VMEM((cb, cb), BF16),
        ],
        args=(dh, h, h, saved, proj, conv_w, w_a, w_x, lam), carry=carry)


def _pool_bwd(d_br_b, w_pool_upb, p, pool_w, pool_scale):
    t, d = d_br_b.shape
    nwb, dp, _ = w_pool_upb.shape
    tc = _tile(t, 256)
    ntc = t // tc
    ng = len(POOL_WINDOWS)

    def body(db_ref, wu_ref, p_ref, w_ref, sc_ref, dx_ref, dw_ref, dsc_ref, nz, n2, n4, n8, dp_s, dy_s):
        c = pl.program_id(0)
        rc = ntc - 1 - c

        @pl.when(c == 0)
        def _():
            for s in (nz, n2, n4, n8):
                s[...] = jnp.zeros_like(s)
            dw_ref[...] = jnp.zeros_like(dw_ref)
            dsc_ref[...] = jnp.zeros_like(dsc_ref)

        wu = jnp.concatenate([wu_ref[b] for b in range(nwb)], axis=1)
        dy_s[...] = _dot_nt(db_ref[...], wu)
        for g in range(ng):
            sl = slice(g * POOL_GROUP_DIM, (g + 1) * POOL_GROUP_DIM)
            pg = p_ref[:, sl]
            dyg = dy_s[:, sl]
            wg = w_ref[g].astype(BF16)
            q = _dot_nn(pg, wg)
            dsc_ref[:, sl] += jnp.sum(dyg * q, axis=0, keepdims=True)
            dpw = (dyg * sc_ref[:, sl]).astype(BF16)
            dw_ref[g] += _dot_tn(pg, dpw)
            dp_s[:, sl] = _dot_nt(dpw, wg)

        dpv = dp_s[...]
        row = lax.broadcasted_iota(jnp.int32, dpv.shape, 0)
        col = lax.broadcasted_iota(jnp.int32, dpv.shape, 1)
        win = _pool_select(col, POOL_WINDOWS)
        cnt = jnp.minimum(rc * tc + row + 1, win).astype(F32)
        z = dpv / cnt

        def up(v, nv, j):
            return jnp.where(row < tc - j, pltpu.roll(v, tc - j, 0), pltpu.roll(nv[...], tc - j, 0))

        u2 = z + up(z, nz, 1)
        u4 = u2 + up(u2, n2, 2)
        u8 = u4 + up(u4, n4, 4)
        u16 = u8 + up(u8, n8, 8)
        nz[...] = z
        n2[...] = u2
        n4[...] = u4
        n8[...] = u8
        dx_ref[...] = (_pool_select(col, (u2, u4, u8, u16)) - dpv).astype(BF16)

    blk = pl.BlockSpec((tc, dp), lambda c: (ntc - 1 - c, 0))
    full_w = pl.BlockSpec(pool_w.shape, lambda c: (0, 0, 0))
    vec = pl.BlockSpec((1, dp), lambda c: (0, 0))
    return _call(
        body, name="pool_bwd", grid=(ntc,),
        in_specs=[pl.BlockSpec((tc, d), lambda c: (ntc - 1 - c, 0)),
                  pl.BlockSpec(w_pool_upb.shape, lambda c: (0, 0, 0)), blk, full_w, vec],
        out_specs=[blk, full_w, vec],
        out_shape=[
            jax.ShapeDtypeStruct((t, dp), BF16),
            jax.ShapeDtypeStruct(pool_w.shape, F32),
            jax.ShapeDtypeStruct((1, dp), F32),
        ],
        scratch_shapes=[pltpu.VMEM((tc, dp), F32)] * 6,
        args=(d_br_b, w_pool_upb, p, pool_w, pool_scale))[0]


def _win_bwd_norm(parts, w_int, dx2, x, g1, carry=None):
    t, d = x.shape
    tk = 512
    tt = _tile(t, 1024)
    bounds = []
    k0 = 0
    for part in parts:
        assert part.shape[1] % tk == 0
        bounds.append((k0, k0 + part.shape[1] // tk))
        k0 += part.shape[1] // tk
    nk = k0
    assert nk * tk == w_int.shape[0]
    np_ = len(parts)

    def body(*refs):
        p_refs = refs[:np_]
        w_ref, dx2_hbm, x_hbm, g_ref, gx_hbm, dg_ref, acc, dx2_ref, x_ref, late_sems, out_sem = refs[np_:]
        i, kk = pl.program_id(0), pl.program_id(1)
        late = _late_copies(i, tt, [(dx2_hbm, dx2_ref), (x_hbm, x_ref)], late_sems)

        @pl.when(kk == 0)
        def _():
            acc[...] = jnp.zeros_like(acc)
            for cp in late:
                cp.start()

        @pl.when((i == 0) & (kk == 0))
        def _():
            dg_ref[...] = jnp.zeros_like(dg_ref)

        for (lo, hi), p_ref in zip(bounds, p_refs):
            @pl.when((kk >= lo) & (kk < hi))
            def _(p_ref=p_ref):
                acc[...] += _dot_nn(p_ref[...], w_ref[...])

        @pl.when(kk == nk - 1)
        def _():
            for cp in late:
                cp.wait()

            def tail(rows):
                xhat, r = _rms_hat(x_ref[rows, :])
                dx, dg = _rms_bwd(acc[rows, :], xhat, r, g_ref[...])
                dx2_ref[rows, :] = dx2_ref[rows, :] + dx
                dg_ref[...] += dg

            _row_chunks(tt, tail)
            out = pltpu.make_async_copy(
                dx2_ref, gx_hbm.at[pl.ds(pl.multiple_of(i * tt, tt), tt)], out_sem.at[0])
            out.start()
            out.wait()

    def part_spec(lo, hi):
        return pl.BlockSpec((tt, tk), lambda i, kk: (i, jnp.clip(kk - lo, 0, hi - lo - 1)))

    vec = pl.BlockSpec((1, d), lambda i, kk: (0, 0))
    return _call(
        body, name="win_bwd_norm", grid=(t // tt, nk),
        in_specs=[part_spec(lo, hi) for lo, hi in bounds]
        + [pl.BlockSpec((tk, d), lambda i, kk: (kk, 0)), ANY, ANY, vec],
        out_specs=[ANY, vec],
        out_shape=[jax.ShapeDtypeStruct((t, d), F32), jax.ShapeDtypeStruct((1, d), F32)],
        scratch_shapes=[pltpu.VMEM((tt, d), F32), pltpu.VMEM((tt, d), F32), pltpu.VMEM((tt, d), F32),
                        pltpu.SemaphoreType.DMA((2,)), pltpu.SemaphoreType.DMA((1,))],
        args=(*parts, w_int, dx2, x, g1), carry=carry)


def _adam_math(w, g, m, v):
    m = ADAM_B1 * m + (1.0 - ADAM_B1) * g
    v = ADAM_B2 * v + (1.0 - ADAM_B2) * (g * g)
    m_hat = m / (1.0 - ADAM_B1 ** ADAM_STEP)
    v_hat = v / (1.0 - ADAM_B2 ** ADAM_STEP)
    delta = -ADAM_LR * (m_hat / (jnp.sqrt(v_hat) + ADAM_EPS) + ADAM_WD * w)
    return delta, m, v


def _adamw_big(ws, gs, ms, vs):
    n = len(ws)
    nb = 4
    pair = [isinstance(g, tuple) for g in gs]

    def body(*refs):
        p = 0
        ins = []
        for a in range(n):
            k = 5 if pair[a] else 4
            ins.append(refs[p:p + k])
            p += k
        for a in range(n):
            g_out, d_ref, nm_ref, nv_ref = refs[p + 4 * a:p + 4 * a + 4]
            if pair[a]:
                w_ref, own_ref, recv_ref, m_ref, v_ref = ins[a]
                g = own_ref[...]
                for k in range(3):
                    g = g + recv_ref[k].astype(F32)
            else:
                w_ref, g_ref, m_ref, v_ref = ins[a]
                g = g_ref[...]
            dl, m, v = _adam_math(w_ref[...], g, m_ref[...], v_ref[...])
            g_out[...] = g
            d_ref[...] = dl
            nm_ref[...] = m
            nv_ref[...] = v

    in_specs, out_specs, out_shape, args = [], [], [], []
    for a, (w, g, m, v) in enumerate(zip(ws, gs, ms, vs)):
        rows, cols = w.shape
        blk = pl.BlockSpec((rows // nb, cols), lambda i: (i, 0))
        if pair[a]:
            in_specs += [blk, pl.BlockSpec((None, rows // nb, cols), lambda i: (0, i, 0)),
                         pl.BlockSpec((3, rows // nb, cols), lambda i: (0, i, 0)), blk, blk]
            args += [w, g[0], g[1], m, v]
        else:
            in_specs += [blk] * 4
            args += [w, g, m, v]
        out_specs += [blk] * 4
        out_shape += [jax.ShapeDtypeStruct(w.shape, F32)] * 4
    outs = _call(body, name="adamw_big", grid=(nb,), in_specs=in_specs, out_specs=out_specs,
                 out_shape=out_shape, args=args)[0]
    return [tuple(outs[4 * a:4 * a + 4]) for a in range(n)]


SMALL_ORDER = ("norm_mix_pre", "norm_mix_post", "norm_mlp_pre", "norm_mlp_post", "b_gate", "conv_w", "conv_b",
               "lru_w_a", "lru_b_a", "lru_w_x", "lru_b_x", "lru_lambda", "pool_w", "pool_scale")
VEC_ROW = dict(norm_mix_pre=0, norm_mix_post=1, norm_mlp_pre=2, norm_mlp_post=3, conv_b=6, lru_b_a=7,
               lru_b_x=8, lru_lambda=9)
ROW_B_GATE, ROW_POOL_SCALE, ROW_CONV_W, ROW_LOSS, N_VEC_ROWS = 4, 10, 11, 15, 16


def _adamw_small(vec_parts, g_pool, g_wa, g_wx, me, params):
    d = vec_parts.shape[2]
    names = SMALL_ORDER
    n = len(names)
    cw_cols = params["conv_w"][0].shape[2]

    def body(me_ref, vec_ref, vecc_ref, gp_ref, gwa_ref, gwx_ref, *refs):
        wmv = refs[:3 * n]
        loss_ref = refs[3 * n]
        outs = refs[3 * n + 1:3 * n + 1 + 4 * n]
        vs, vsc = refs[3 * n + 1 + 4 * n:]
        acc, accc = vec_ref[0], vecc_ref[0]
        for k in range(1, N_DEV):
            acc = acc + vec_ref[k]
            accc = accc + vecc_ref[k]
        vs[...] = acc
        vsc[...] = accc
        loss_ref[...] = vs[ROW_LOSS:ROW_LOSS + 1, 0:128]

        def upd(a, g, idx):
            w_ref, m_ref, v_ref = wmv[3 * a:3 * a + 3]
            g_ref, d_ref, nm_ref, nv_ref = outs[4 * a:4 * a + 4]
            dl, m, v = _adam_math(w_ref[idx], g, m_ref[idx], v_ref[idx])
            g_ref[idx] = g
            d_ref[idx] = dl
            nm_ref[idx] = m
            nv_ref[idx] = v

        for a, name in enumerate(names):
            if name in VEC_ROW:
                r = VEC_ROW[name]
                upd(a, vs[r:r + 1, :], (slice(None), slice(None)))
            elif name == "b_gate":
                for half in range(2):
                    r = ROW_B_GATE + half
                    upd(a, vs[r:r + 1, :], (slice(None), slice(half * d, (half + 1) * d)))
            elif name == "pool_scale":
                width = params[name][0].shape[1]
                upd(a, vs[ROW_POOL_SCALE:ROW_POOL_SCALE + 1, 0:width], (slice(None), slice(None)))
            elif name == "conv_w":
                upd(a, vsc[ROW_CONV_W:ROW_CONV_W + 4, :], (0,))
            elif name == "pool_w":
                upd(a, gp_ref[...], (Ellipsis,))
            elif name == "lru_w_a":
                upd(a, gwa_ref[...], (Ellipsis,))
            elif name == "lru_w_x":
                upd(a, gwx_ref[...], (Ellipsis,))
            else:
                raise ValueError(name)

    def whole(shape):
        nd = len(shape)
        return pl.BlockSpec(tuple(shape), lambda i, me_ref: (0,) * nd)

    in_specs = [
        whole(vec_parts.shape),
        pl.BlockSpec((N_DEV, N_VEC_ROWS, cw_cols), lambda i, me_ref: (0, 0, me_ref[0])),
        whole(g_pool.shape), whole(g_wa.shape), whole(g_wx.shape),
    ]
    args = [vec_parts, vec_parts, g_pool, g_wa, g_wx]
    out_specs = [whole((1, 128))]
    out_shape = [jax.ShapeDtypeStruct((1, 128), F32)]
    for name in names:
        for arr in params[name]:
            in_specs.append(whole(arr.shape))
            args.append(arr)
        shp = params[name][0].shape
        out_specs += [whole(shp)] * 4
        out_shape += [jax.ShapeDtypeStruct(shp, F32)] * 4
    grid_spec = pltpu.PrefetchScalarGridSpec(
        num_scalar_prefetch=1, grid=(1,), in_specs=in_specs, out_specs=out_specs,
        scratch_shapes=[pltpu.VMEM((N_VEC_ROWS, d), F32), pltpu.VMEM((N_VEC_ROWS, cw_cols), F32)])
    outs = pl.pallas_call(
        body, name="adamw_small", grid_spec=grid_spec, out_shape=out_shape,
        compiler_params=pltpu.CompilerParams(
            dimension_semantics=("arbitrary",), vmem_limit_bytes=V7X_VMEM_LIMIT_BYTES),
    )(me, *_in_hbm(args))
    return outs[0], {name: tuple(outs[1 + 4 * a:5 + 4 * a]) for a, name in enumerate(names)}


def _rs_sum(fulls, recvs, shard_ids, slot_ids, name):
    n = len(fulls)

    def body(sh_ref, sl_ref, *refs):
        s = pl.program_id(0)
        for a in range(n):
            full_ref, recv_ref = refs[2 * a], refs[2 * a + 1]
            own_ref, send_ref = refs[2 * n + 2 * a], refs[2 * n + 2 * a + 1]
            v = full_ref[...] + recv_ref[...].astype(F32)

            @pl.when(s == 0)
            def _(own_ref=own_ref, v=v):
                own_ref[...] = v

            @pl.when(s > 0)
            def _(send_ref=send_ref, v=v):
                send_ref[...] = v.astype(send_ref.dtype)

    in_specs, out_specs, out_shape, args = [], [], [], []
    for full, recv in zip(fulls, recvs):
        r, rest = recv.shape[1], tuple(recv.shape[2:])
        zeros = (0,) * len(rest)
        in_specs += [
            pl.BlockSpec((r,) + rest, lambda s, sh, sl, zeros=zeros: (sh[s],) + zeros),
            pl.BlockSpec((None, r) + rest, lambda s, sh, sl, zeros=zeros: (sl[s], 0) + zeros),
        ]
        out_specs += [
            pl.BlockSpec((None, r) + rest, lambda s, sh, sl, zeros=zeros: (0, 0) + zeros),
            pl.BlockSpec((None, r) + rest, lambda s, sh, sl, zeros=zeros: (jnp.maximum(s - 1, 0), 0) + zeros),
        ]
        out_shape += [jax.ShapeDtypeStruct((1, r) + rest, F32), jax.ShapeDtypeStruct((3, r) + rest, recv.dtype)]
        args += [full, recv]
    grid_spec = pltpu.PrefetchScalarGridSpec(
        num_scalar_prefetch=2, grid=(4,), in_specs=in_specs, out_specs=out_specs)
    outs = pl.pallas_call(
        body,
        name=name,
        grid_spec=grid_spec,
        out_shape=out_shape,
        compiler_params=pltpu.CompilerParams(
            dimension_semantics=("arbitrary",), vmem_limit_bytes=V7X_VMEM_LIMIT_BYTES),
    )(shard_ids, slot_ids, *_in_hbm(args))
    return [(outs[2 * a], outs[2 * a + 1]) for a in range(n)]


def _finals(pairs, name, carry=None):
    nb = 4
    n = len(pairs)

    def body(*refs):
        for a in range(n):
            own_ref, recv_ref = refs[2 * a], refs[2 * a + 1]
            acc = own_ref[...]
            for k in range(3):
                acc = acc + recv_ref[k].astype(F32)
            refs[2 * n + a][...] = acc

    in_specs, out_specs, out_shape, args = [], [], [], []
    for own, recv in pairs:
        _, rows, cols = own.shape
        in_specs += [pl.BlockSpec((None, rows // nb, cols), lambda i: (0, i, 0)),
                     pl.BlockSpec((3, rows // nb, cols), lambda i: (0, i, 0))]
        args += [own, recv]
        out_specs.append(pl.BlockSpec((rows // nb, cols), lambda i: (i, 0)))
        out_shape.append(jax.ShapeDtypeStruct((rows, cols), F32))
    return _call(body, name=name, grid=(nb,), in_specs=in_specs, out_specs=out_specs,
                 out_shape=out_shape, args=args, carry=carry)


def _rs_sums(fulls_f32, recv1, tag):
    x, y, c = _place()
    qs = jnp.stack([2 * x + y, 2 * (1 - x) + y, 2 * x + (1 - y), 2 * (1 - x) + (1 - y)]).astype(jnp.int32)
    shard_ids = 2 * qs + c
    return _rs_sum(fulls_f32, recv1, shard_ids, qs, "rs_sum_" + tag)


def _rs_level1(fulls_f32, fulls_send, tag):
    recv1 = _run_plan(_rs_sibling_plan(fulls_send), "rs_sibling_" + tag)
    return _rs_sums(fulls_f32, recv1, tag)


def _rows(g):
    return g.reshape(g.shape[0] * g.shape[1], g.shape[2])


def kernel(x, norm_mix_pre, norm_mix_post, norm_mlp_pre, norm_mlp_post, w_in, b_gate, conv_w, conv_b, lru_w_a, lru_b_a, lru_w_x, lru_b_x, lru_lambda, pool_w, pool_scale, w_lru_up, w_pool_up, w_o, w_ff1, w_ff2, loss_target, m_norm_mix_pre, m_norm_mix_post, m_norm_mlp_pre, m_norm_mlp_post, m_w_in, m_b_gate, m_conv_w, m_conv_b, m_lru_w_a, m_lru_b_a, m_lru_w_x, m_lru_b_x, m_lru_lambda, m_pool_w, m_pool_scale, m_w_lru_up, m_w_pool_up, m_w_o, m_w_ff1, m_w_ff2, v_norm_mix_pre, v_norm_mix_post, v_norm_mlp_pre, v_norm_mlp_post, v_w_in, v_b_gate, v_conv_w, v_conv_b, v_lru_w_a, v_lru_b_a, v_lru_w_x, v_lru_b_x, v_lru_lambda, v_pool_w, v_pool_scale, v_w_lru_up, v_w_pool_up, v_w_o, v_w_ff1, v_w_ff2):
    t, d = x.shape[1], x.shape[2]
    d_rnn = conv_b.shape[1]
    d_pool = pool_scale.shape[1]
    per = LRU_CB // LRU_HEAD_DIM
    xi, yi, ci = _place()
    me = 4 * xi + 2 * yi + ci

    x2d = x[0]
    tgt = loss_target[0]

    s_in = w_in[0].T.astype(BF16)
    s_lu = w_lru_up[0].astype(BF16)
    s_pu = w_pool_up[0].astype(BF16)
    s_o = w_o[0].astype(BF16)
    s_f1 = w_ff1[0].astype(BF16)
    s_f2 = w_ff2[0].astype(BF16)
    s_cw = jnp.pad(conv_w[0], ((0, 4), (0, 0)))

    g_in, g_cw = _run_plan(_ag_plan([s_in, s_cw]), "ag_w_in")
    w_int = _rows(g_in)
    conv_w_full = jnp.transpose(g_cw[:, :4, :], (1, 0, 2)).reshape(4, d_rnn)

    wa_bd, wx_bd = lru_w_a[0], lru_w_x[0]
    pw = pool_w[0]
    pw_bf = pw.astype(BF16)

    pool_block = (2 * d_rnn) // d_pool
    ga_block = (2 * d_rnn + d_pool) // 512
    gb_block = ga_block + d // 512
    g_block = d_rnn // 512

    r_f1, r_f2 = s_f1.shape[0], s_f2.shape[0]
    f1_cut = r_f1 // 4
    f2_cut = (3 * r_f2) // 8
    plan = _join([_ag_plan([s_lu, s_pu]), _ag_plan([s_f1], pieces=[(0, f1_cut)])])
    (proj, h1), got = _norm_proj(x2d, norm_mix_pre, w_int, carry=plan)
    (g_lu, g_pu), (g_f1,) = plan.split(got)
    plan = _join([_ag_plan([s_f1], pieces=[(f1_cut, r_f1 - f1_cut)], bufs=[g_f1]), _ag_plan([s_o])])
    (y_lru, h, lru_saved), got = _lru_fwd(
        proj, conv_w_full, conv_b, wa_bd, lru_b_a, wx_bd, lru_b_x, lru_lambda, carry=plan)
    (g_f1,), (g_o,) = plan.split(got)
    w_lu, w_og = _rows(g_lu), _rows(g_o)
    y_pool, p = _pool_fwd(proj, pw_bf, pool_scale, pool_block)
    (br_a, br_b, mix), (g_f2,) = _branch_mix(
        y_lru, y_pool, w_lu, g_pu, proj, b_gate, ga_block, gb_block,
        carry=_ag_plan([s_f2], pieces=[(0, f2_cut)]))
    (m, x2, h3), _ = _wo_norm(mix, w_og, x2d, norm_mix_post, norm_mlp_pre)
    (rf,), (g_f2,) = _ff1(
        h3, g_f1, carry=_ag_plan([s_f2], pieces=[(f2_cut, r_f2 - f2_cut)], bufs=[g_f2]))
    w_f2 = _rows(g_f2)
    dy, df, dg4, loss_part = _ff2_loss(rf, w_f2, x2, norm_mlp_post, tgt)

    (gw_ff2_32, gw_ff2_16), _ = _wgrad(rf, df, "wgrad_ff2", square_a=True)
    (d_f1,), r1_ff2 = _ff2_bwd(df, w_f2, rf, carry=_rs_sibling_plan([gw_ff2_16]))
    ((own_ff2, send_ff2),) = _rs_sums([gw_ff2_32], r1_ff2, "ff2")
    cut2 = (5 * send_ff2.shape[1]) // 16
    (gw_ff1_32, gw_ff1_16), (r2_ff2,) = _wgrad_cols(
        h3, d_f1, s_f1.shape[1], s_f1.shape[1], "wgrad_ff1",
        carry=_rs_chips_plan([send_ff2], pieces=[(0, cut2)]))
    plan = _join([_rs_chips_plan([send_ff2], pieces=[(cut2, send_ff2.shape[1] - cut2)], bufs=[r2_ff2]),
                  _rs_sibling_plan([gw_ff1_16])])
    (dx2, dm, dg3, dg2), got = _ff1_bwd_norms(d_f1, g_f1, dy, x2, norm_mlp_pre, m, norm_mix_post, carry=plan)
    (r2_ff2,), r1_ff1 = plan.split(got)
    ((own_ff1, send_ff1),) = _rs_sums([gw_ff1_32], r1_ff1, "ff1")
    own_ff1, send_ff1 = own_ff1.reshape((1,) + s_f1.shape), send_ff1.reshape((3,) + s_f1.shape)
    cut = send_ff1.shape[1] // 4
    (gw_o_32, gw_o_16), _ = _wgrad(mix, dm, "wgrad_o")
    (d_br_a, d_br_b, p_ga, p_gb, dbg_a, dbg_b), (r2_ff1,) = _wo_bwd_mix(
        dm, w_og, br_a, br_b, proj, b_gate, ga_block, gb_block,
        carry=_rs_chips_plan([send_ff1], pieces=[(0, cut)]))
    (gw_lu_32, gw_lu_16), _ = _wgrad(y_lru, d_br_a, "wgrad_lru_up")
    (gw_pu_32, gw_pu_16), _ = _wgrad_cols(y_pool, d_br_b, s_pu.shape[1], d, "wgrad_pool_up")
    (dh, p_g), r1_mid = _lru_up_bwd(
        d_br_a, w_lu, proj, h, g_block,
        carry=_rs_sibling_plan([gw_o_16, gw_lu_16, gw_pu_16]))
    mid = _rs_sums([gw_o_32, gw_lu_32, gw_pu_32], r1_mid, "mid")
    plan = _join([_rs_chips_plan([send_ff1], pieces=[(cut, send_ff1.shape[1] - cut)], bufs=[r2_ff1]),
                  _rs_chips_plan([mid[0][1]])])
    (p_x, dwa, db_a, dwx, db_x, dlam, dconv_w, dconv_b), got = _lru_bwd(
        dh, h, lru_saved, proj, conv_w_full, wa_bd, wx_bd, lru_lambda, carry=plan)
    (r2_ff1,), (r2_o,) = plan.split(got)
    p_p, dpool_w, dpool_scale = _pool_bwd(d_br_b, g_pu, p, pw, pool_scale)
    parts = [p_x, p_g, p_p, p_ga, p_gb]
    gw_in, (r2_lu, r2_pu) = _wgrad_parts(
        parts, h1, "wgrad_in", carry=_rs_chips_plan([mid[1][1], mid[2][1]]))
    r2_mid = [r2_o, r2_lu, r2_pu]
    tail = _rs_level1([gw_in[0], dpool_w.reshape(N_DEV, -1, POOL_GROUP_DIM), dwa, dwx],
                      [gw_in[1], dpool_w.reshape(N_DEV, -1, POOL_GROUP_DIM), dwa, dwx], "in")
    (grad_x, dg1), r2_tail = _win_bwd_norm(parts, w_int, dx2, x2d, norm_mix_pre,
                                           carry=_rs_chips_plan([s for _, s in tail]))

    def flat2(a):
        return a.reshape(a.shape[0], -1, a.shape[-1])

    fin_small, _ = _finals([
        (flat2(tail[1][0]), flat2(r2_tail[1])), (flat2(tail[2][0]), flat2(r2_tail[2])),
        (flat2(tail[3][0]), flat2(r2_tail[3])),
    ], "rs_finals_small")

    def pad_row(a):
        return jnp.pad(a, ((0, 0), (0, d - a.shape[1])))

    vecs = jnp.concatenate([dg1, dg2, dg3, dg4, dbg_a, dbg_b, dconv_b, db_a, db_x, dlam,
                            pad_row(dpool_scale), dconv_w, pad_row(loss_part)], axis=0)
    assert vecs.shape[0] == N_VEC_ROWS
    vec_parts, g_pool, g_wa, g_wx = _run_plan(_ag_plan([vecs] + fin_small), "ag_tail")

    big_names = ["w_in", "w_lru_up", "w_pool_up", "w_o", "w_ff1", "w_ff2"]
    big_w = [w_in[0].T, w_lru_up[0], w_pool_up[0], w_o[0], w_ff1[0], w_ff2[0]]
    big_g = [(tail[0][0], r2_tail[0]), (mid[1][0], r2_mid[1]),
             (mid[2][0].reshape((1,) + s_pu.shape), r2_mid[2].reshape((3,) + s_pu.shape)),
             (mid[0][0], r2_mid[0]), (own_ff1, r2_ff1), (own_ff2, r2_ff2)]
    big_m = [m_w_in[0].T, m_w_lru_up[0], m_w_pool_up[0], m_w_o[0], m_w_ff1[0], m_w_ff2[0]]
    big_v = [v_w_in[0].T, v_w_lru_up[0], v_w_pool_up[0], v_w_o[0], v_w_ff1[0], v_w_ff2[0]]
    big_out = _adamw_big(big_w, big_g, big_m, big_v)
    big_out[0] = tuple(o.T for o in big_out[0])

    small = dict(
        norm_mix_pre=(norm_mix_pre, m_norm_mix_pre, v_norm_mix_pre),
        norm_mix_post=(norm_mix_post, m_norm_mix_post, v_norm_mix_post),
        norm_mlp_pre=(norm_mlp_pre, m_norm_mlp_pre, v_norm_mlp_pre),
        norm_mlp_post=(norm_mlp_post, m_norm_mlp_post, v_norm_mlp_post),
        b_gate=(b_gate, m_b_gate, v_b_gate), conv_w=(conv_w, m_conv_w, v_conv_w),
        conv_b=(conv_b, m_conv_b, v_conv_b), lru_w_a=(lru_w_a, m_lru_w_a, v_lru_w_a),
        lru_b_a=(lru_b_a, m_lru_b_a, v_lru_b_a), lru_w_x=(lru_w_x, m_lru_w_x, v_lru_w_x),
        lru_b_x=(lru_b_x, m_lru_b_x, v_lru_b_x), lru_lambda=(lru_lambda, m_lru_lambda, v_lru_lambda),
        pool_w=(pool_w, m_pool_w, v_pool_w), pool_scale=(pool_scale, m_pool_scale, v_pool_scale))
    loss_row, small_out = _adamw_small(
        vec_parts, g_pool.reshape(pool_w.shape), g_wa.reshape(lru_w_a.shape), g_wx.reshape(lru_w_x.shape),
        jnp.reshape(me, (1,)).astype(jnp.int32), small)
    grads = {n: o[0] for n, o in small_out.items()}
    delta = {n: o[1] for n, o in small_out.items()}
    new_m = {n: o[2] for n, o in small_out.items()}
    new_v = {n: o[3] for n, o in small_out.items()}

    for name, (g, dl, nm, nv) in zip(big_names, big_out):
        grads[name], delta[name], new_m[name], new_v[name] = g[None], dl[None], nm[None], nv[None]

    loss = loss_row[0, 0]
    order = ["norm_mix_pre", "norm_mix_post", "norm_mlp_pre", "norm_mlp_post", "w_in", "b_gate", "conv_w",
             "conv_b", "lru_w_a", "lru_b_a", "lru_w_x", "lru_b_x", "lru_lambda", "pool_w", "pool_scale",
             "w_lru_up", "w_pool_up", "w_o", "w_ff1", "w_ff2"]
    return (loss, grad_x[None], *[grads[n] for n in order], *[delta[n] for n in order],
            *[new_m[n] for n in order], *[new_v[n] for n in order])
```

```python
import functools
import math
import operator
import types

import jax
import jax.numpy as jnp
from jax import lax
from jax.experimental import pallas as pl
from jax.experimental.pallas import tpu as pltpu

F32 = jnp.float32
BF16 = jnp.bfloat16
NORM_EPS = 1e-6
LRU_C = 8.0
N_LRU_HEADS = 16
LRU_HEAD_DIM = 64
POOL_WINDOWS = (2, 4, 8, 16)
POOL_GROUP_DIM = 128
ADAM_LR = 0.001
ADAM_B1 = 0.9
ADAM_B2 = 0.999
ADAM_EPS = 1e-08
ADAM_WD = 0.01
ADAM_STEP = 10
N_DEV = 8
V7X_VMEM_LIMIT_BYTES = 56 * 1024 * 1024
LRU_CB = 256
MESH = pl.DeviceIdType.MESH
ANY = pl.BlockSpec(memory_space=pl.ANY)


def _tile(n, pref):
    t = min(n, pref)
    assert n % t == 0, (n, pref)
    return t


def _dot_nn(a, b):
    return lax.dot_general(a, b, (((1,), (0,)), ((), ())), preferred_element_type=F32)


def _dot_nt(a, b):
    return lax.dot_general(a, b, (((1,), (1,)), ((), ())), preferred_element_type=F32)


def _dot_tn(a, b):
    return lax.dot_general(a, b, (((0,), (0,)), ((), ())), preferred_element_type=F32)


def _row_chunks(n_rows, fn, chunk=256):
    chunk = min(chunk, n_rows)
    assert n_rows % chunk == 0

    def step(r, carry):
        fn(pl.ds(pl.multiple_of(r * chunk, chunk), chunk))
        return carry

    lax.fori_loop(0, n_rows // chunk, step, 0)


def _late_copies(i, tt, pairs, sems):
    rows = pl.ds(pl.multiple_of(i * tt, tt), tt)
    return [pltpu.make_async_copy(hbm.at[rows], buf, sems.at[j]) for j, (hbm, buf) in enumerate(pairs)]


def _sig(x):
    return 1.0 / (1.0 + jnp.exp(-x))


def _rms_hat(x):
    r = lax.rsqrt(jnp.mean(x * x, axis=-1, keepdims=True) + NORM_EPS)
    return x * r, r


def _rms_bwd(dn, xhat, r, g):
    q = dn * g
    dx = r * (q - xhat * jnp.mean(q * xhat, axis=-1, keepdims=True))
    dg = jnp.sum(dn * xhat, axis=0, keepdims=True)
    return dx, dg


_GELU_K = math.sqrt(2.0 / math.pi)
_GELU_C = 0.044715


def _gelu_and_grad(g):
    t = jnp.tanh(_GELU_K * (g + _GELU_C * g * g * g))
    val = 0.5 * g * (1.0 + t)
    grad = 0.5 * (1.0 + t) + 0.5 * g * (1.0 - t * t) * (_GELU_K * (1.0 + 3.0 * _GELU_C * g * g))
    return val, grad


def _softplus_neg(lam):
    z = -lam
    e = jnp.exp(-jnp.abs(z))
    u = 1.0 + e
    d = u - 1.0
    l1p = jnp.where(d == 0.0, e, jnp.log(u) * (e / jnp.where(d == 0.0, 1.0, d)))
    return jnp.maximum(z, 0.0) + l1p


def _lru_gates(xc, wa, ba, wx, bx, lam):
    xcb = xc.astype(BF16)
    r = _sig(_dot_nn(xcb, wa) + ba)
    i = _sig(_dot_nn(xcb, wx) + bx)
    sp = _softplus_neg(lam)
    log_a = (-LRU_C) * r * sp
    a = jnp.exp(log_a)
    mult = jnp.sqrt(-jnp.tanh(log_a) * (1.0 + a * a))
    return xcb, r, i, sp, log_a, a, mult


def _place():
    return lax.axis_index("x"), lax.axis_index("y"), lax.axis_index("c")


def _ag_plan(shards, pieces=None, bufs=None):
    na = len(shards)
    n_kinds = 7

    def parts(ins, outs, sems):
        send_sems, recv_sems, local_sems = sems
        x, y, c = _place()
        me, sibling = (x, y, c), (x, y, 1 - c)
        x_nb, y_nb, diag = (1 - x, y), (x, 1 - y), (1 - x, 1 - y)
        relay_src = (c * (1 - x) + (1 - c) * x, c * y + (1 - c) * (1 - y))
        relay_dst = (c * x + (1 - c) * (1 - x), c * (1 - y) + (1 - c) * y)

        def own(a):
            return ins[a] if pieces is None else ins[a].at[pl.ds(*pieces[a])]

        def slot(a, px, py, pc):
            idx = 4 * px + 2 * py + pc
            return outs[a].at[idx] if pieces is None else outs[a].at[idx, pl.ds(*pieces[a])]

        def copy(a, k, block, to, src=None):
            return pltpu.make_async_remote_copy(
                src_ref=slot(a, *block) if src is None else src,
                dst_ref=slot(a, *block),
                send_sem=send_sems.at[a * n_kinds + k],
                recv_sem=recv_sems.at[a * n_kinds + k],
                device_id=to,
                device_id_type=MESH,
            )

        mine = [pltpu.make_async_copy(own(a), slot(a, *me), local_sems.at[a]) for a in range(na)]
        first, second, third = [], [], []
        for a in range(na):
            first += [copy(a, 0, me, sibling, src=own(a)), copy(a, 1, me, (*x_nb, c), src=own(a)),
                      copy(a, 2, me, (*y_nb, c), src=own(a))]
            second += [copy(a, 3, (*relay_src, c), (*relay_dst, c)), copy(a, 4, (*x_nb, c), sibling),
                       copy(a, 5, (*y_nb, c), sibling)]
            third.append(copy(a, 6, (*diag, c), sibling))
        return sibling, c, x_nb, y_nb, diag, copy, mine, first, second, third

    def start(ins, outs, sems):
        _, _, _, _, _, _, mine, first, _, _ = parts(ins, outs, sems)
        for cp in mine + first:
            cp.start()

    def middle(ins, outs, sems):
        _, c, x_nb, y_nb, _, copy, _, _, second, _ = parts(ins, outs, sems)
        for a in range(na):
            copy(a, 1, (*x_nb, c), (*x_nb, c)).wait_recv()
            copy(a, 2, (*y_nb, c), (*y_nb, c)).wait_recv()
        for cp in second:
            cp.start()

    def finish(ins, outs, sems):
        sibling, c, x_nb, y_nb, diag, copy, mine, first, second, third = parts(ins, outs, sems)
        for a in range(na):
            copy(a, 3, (*diag, c), (*diag, c)).wait_recv()
            third[a].start()
        for a in range(na):
            copy(a, 0, sibling, sibling).wait_recv()
            copy(a, 4, (*x_nb, 1 - c), sibling).wait_recv()
            copy(a, 5, (*y_nb, 1 - c), sibling).wait_recv()
            copy(a, 6, (*diag, 1 - c), sibling).wait_recv()
        for cp in first + second + third:
            cp.wait_send()
        for cp in mine:
            cp.wait()

    return types.SimpleNamespace(
        ins=list(shards) + list(bufs or []),
        out_shapes=[jax.ShapeDtypeStruct((N_DEV,) + s.shape, s.dtype) for s in shards],
        sems=[pltpu.SemaphoreType.DMA((n_kinds * na,)), pltpu.SemaphoreType.DMA((n_kinds * na,)),
              pltpu.SemaphoreType.DMA((na,))],
        aliases=[(na + a, a) for a in range(na)] if bufs else [],
        peers=frozenset({"sibling", "neighbours"}), start=start, middle=middle, finish=finish)


def _rs_sibling_plan(fulls):
    na = len(fulls)
    rs = [f.shape[0] // N_DEV for f in fulls]

    def copies(ins, outs, sems):
        send_sems, recv_sems = sems
        x, y, c = _place()
        out = []
        for a in range(na):
            for q in range(4):
                shard = 2 * q + (1 - c)
                out.append(pltpu.make_async_remote_copy(
                    src_ref=ins[a].at[pl.ds(shard * rs[a], rs[a])],
                    dst_ref=outs[a].at[q],
                    send_sem=send_sems.at[a * 4 + q],
                    recv_sem=recv_sems.at[a * 4 + q],
                    device_id=(x, y, 1 - c),
                    device_id_type=MESH,
                ))
        return out

    def start(ins, outs, sems):
        for cp in copies(ins, outs, sems):
            cp.start()

    def finish(ins, outs, sems):
        for cp in copies(ins, outs, sems):
            cp.wait()

    return types.SimpleNamespace(
        ins=list(fulls),
        out_shapes=[jax.ShapeDtypeStruct((4, r) + f.shape[1:], f.dtype) for r, f in zip(rs, fulls)],
        sems=[pltpu.SemaphoreType.DMA((4 * na,)), pltpu.SemaphoreType.DMA((4 * na,))],
        peers=frozenset({"sibling"}), start=start, finish=finish)


def _rs_chips_plan(sends, pieces=None, bufs=None):
    na = len(sends)

    def copies(ins, outs, sems):
        send_sems, recv_sems = sems
        x, y, c = _place()
        chips = [(1 - x, y), (x, 1 - y), (1 - x, 1 - y)]
        out = []
        for a in range(na):
            for k, chip in enumerate(chips):
                rows = (k,) if pieces is None else (k, pl.ds(*pieces[a]))
                out.append(pltpu.make_async_remote_copy(
                    src_ref=ins[a].at[rows],
                    dst_ref=outs[a].at[rows],
                    send_sem=send_sems.at[a * 3 + k],
                    recv_sem=recv_sems.at[a * 3 + k],
                    device_id=(*chip, c),
                    device_id_type=MESH,
                ))
        return out

    def start(ins, outs, sems):
        for cp in copies(ins, outs, sems):
            cp.start()

    def finish(ins, outs, sems):
        for cp in copies(ins, outs, sems):
            cp.wait()

    return types.SimpleNamespace(
        ins=list(sends) + list(bufs or []),
        out_shapes=[jax.ShapeDtypeStruct(s.shape, s.dtype) for s in sends],
        sems=[pltpu.SemaphoreType.DMA((3 * na,)), pltpu.SemaphoreType.DMA((3 * na,))],
        aliases=[(na + a, a) for a in range(na)] if bufs else [],
        peers=frozenset({"chips"}), start=start, finish=finish)


def _join(plans):
    ins, outs, sems, aliases, offs = [], [], [], [], []
    for p in plans:
        offs.append((len(ins), len(outs), len(sems)))
        aliases += [(len(ins) + ci, len(outs) + co) for ci, co in getattr(p, "aliases", [])]
        ins += p.ins
        outs += p.out_shapes
        sems += p.sems

    def cut(p, off, i, o, s):
        return (i[off[0]:off[0] + len(p.ins)], o[off[1]:off[1] + len(p.out_shapes)],
                s[off[2]:off[2] + len(p.sems)])

    def start(i, o, s):
        for p, off in zip(plans, offs):
            p.start(*cut(p, off, i, o, s))

    def middle(i, o, s):
        for p, off in zip(plans, offs):
            if getattr(p, "middle", None) is not None:
                p.middle(*cut(p, off, i, o, s))

    def finish(i, o, s):
        for p, off in zip(plans, offs):
            p.finish(*cut(p, off, i, o, s))

    def split(results):
        return [list(results[off[1]:off[1] + len(p.out_shapes)]) for p, off in zip(plans, offs)]

    return types.SimpleNamespace(ins=ins, out_shapes=outs, sems=sems, aliases=aliases,
                                 peers=frozenset().union(*[p.peers for p in plans]),
                                 start=start, middle=middle, finish=finish, split=split)


COLLECTIVE_ID = {frozenset({"sibling"}): 0, frozenset({"chips"}): 1, frozenset({"sibling", "chips"}): 2,
                 frozenset({"sibling", "neighbours"}): 3}


def _handshake(peers):
    x, y, c = _place()
    devs = []
    if "sibling" in peers:
        devs.append((x, y, 1 - c))
    if "neighbours" in peers:
        devs += [(1 - x, y, c), (x, 1 - y, c)]
    if "chips" in peers:
        assert "neighbours" not in peers
        devs += [(1 - x, y, c), (x, 1 - y, c), (1 - x, 1 - y, c)]
    barrier = pltpu.get_barrier_semaphore()
    for dev in devs:
        pl.semaphore_signal(barrier, inc=1, device_id=dev, device_id_type=MESH)
    pl.semaphore_wait(barrier, len(devs))


def _in_hbm(args):
    return [pltpu.with_memory_space_constraint(a, pltpu.HBM) for a in args]


def _run_plan(plan, name):
    n_in, n_out = len(plan.ins), len(plan.out_shapes)

    def body(*refs):
        ins, outs, sems = refs[:n_in], refs[n_in:n_in + n_out], refs[n_in + n_out:]
        _handshake(plan.peers)
        plan.start(ins, outs, sems)
        if getattr(plan, "middle", None) is not None:
            plan.middle(ins, outs, sems)
        plan.finish(ins, outs, sems)

    return pl.pallas_call(
        body,
        name=name,
        in_specs=[ANY] * n_in,
        out_specs=[ANY] * n_out,
        out_shape=plan.out_shapes,
        scratch_shapes=plan.sems,
        input_output_aliases=dict(getattr(plan, "aliases", [])),
        compiler_params=pltpu.CompilerParams(collective_id=COLLECTIVE_ID[plan.peers]),
    )(*_in_hbm(plan.ins))


def _call(body, *, name, grid, in_specs, out_specs, out_shape, args, scratch_shapes=(), aliases=None,
          carry=None):
    n_in, n_out, n_scr = len(in_specs), len(out_shape), len(scratch_shapes)
    params = pltpu.CompilerParams(
        dimension_semantics=("arbitrary",) * len(grid), vmem_limit_bytes=V7X_VMEM_LIMIT_BYTES)
    if carry is None:
        outs = pl.pallas_call(
            body, name=name, grid=grid, in_specs=list(in_specs), out_specs=list(out_specs),
            out_shape=list(out_shape), scratch_shapes=list(scratch_shapes),
            input_output_aliases=aliases or {}, compiler_params=params)(*_in_hbm(args))
        return list(outs), []
    c_in, c_out = len(carry.ins), len(carry.out_shapes)

    def full(*refs):
        p = 0
        ins = refs[p:p + n_in]
        p += n_in
        cins = refs[p:p + c_in]
        p += c_in
        outs = refs[p:p + n_out]
        p += n_out
        couts = refs[p:p + c_out]
        p += c_out
        scr = refs[p:p + n_scr]
        csems = refs[p + n_scr:]
        ids = [pl.program_id(a) for a in range(len(grid))]
        first = functools.reduce(operator.and_, [i == 0 for i in ids])
        last = functools.reduce(operator.and_, [i == g - 1 for i, g in zip(ids, grid)])

        @pl.when(first)
        def _():
            _handshake(carry.peers)
            carry.start(cins, couts, csems)

        if getattr(carry, "middle", None) is not None:
            n_steps = math.prod(grid)
            flat = functools.reduce(lambda acc, ig: acc * ig[1] + ig[0], zip(ids, grid), 0)

            @pl.when(flat == (2 * n_steps) // 3)
            def _():
                carry.middle(cins, couts, csems)

        body(*ins, *outs, *scr)

        @pl.when(last)
        def _():
            carry.finish(cins, couts, csems)

    all_aliases = dict(aliases or {})
    all_aliases.update({n_in + ci: n_out + co for ci, co in getattr(carry, "aliases", [])})
    params = pltpu.CompilerParams(
        dimension_semantics=("arbitrary",) * len(grid), vmem_limit_bytes=V7X_VMEM_LIMIT_BYTES,
        collective_id=COLLECTIVE_ID[carry.peers])
    outs = pl.pallas_call(
        full, name=name, grid=grid,
        in_specs=list(in_specs) + [ANY] * c_in,
        out_specs=list(out_specs) + [ANY] * c_out,
        out_shape=list(out_shape) + list(carry.out_shapes),
        scratch_shapes=list(scratch_shapes) + list(carry.sems),
        input_output_aliases=all_aliases, compiler_params=params)(*_in_hbm(args), *_in_hbm(carry.ins))
    return list(outs[:n_out]), list(outs[n_out:])


def _norm_proj(x, g1, w_int, carry=None):
    t, d = x.shape
    n = w_int.shape[0]
    tt, tn = _tile(t, 2048), _tile(n, 512)

    def body(x_ref, g_ref, w_ref, proj_ref, h1_ref, h1_s):
        @pl.when(pl.program_id(1) == 0)
        def _():
            def norm_rows(rows):
                xhat, _ = _rms_hat(x_ref[rows, :])
                h = (xhat * g_ref[...]).astype(BF16)
                h1_s[rows, :] = h
                h1_ref[rows, :] = h

            _row_chunks(tt, norm_rows)

        proj_ref[...] = _dot_nt(h1_s[...], w_ref[...]).astype(BF16)

    return _call(
        body, name="norm_proj", grid=(t // tt, n // tn),
        in_specs=[
            pl.BlockSpec((tt, d), lambda i, j: (i, 0)),
            pl.BlockSpec((1, d), lambda i, j: (0, 0)),
            pl.BlockSpec((tn, d), lambda i, j: (j, 0)),
        ],
        out_specs=[
            pl.BlockSpec((tt, tn), lambda i, j: (i, j)),
            pl.BlockSpec((tt, d), lambda i, j: (i, 0)),
        ],
        out_shape=[jax.ShapeDtypeStruct((t, n), BF16), jax.ShapeDtypeStruct((t, d), BF16)],
        scratch_shapes=[pltpu.VMEM((tt, d), BF16)],
        args=(x, g1, w_int), carry=carry)


def _scan_rows(av, bv, reverse):
    tc = av.shape[0]
    row = lax.broadcasted_iota(jnp.int32, av.shape, 0)
    s = 1
    while s < tc:
        if s < 8:
            keep = (row < tc - s) if reverse else (row >= s)
            shift = (tc - s) if reverse else s
            a_sh = jnp.where(keep, pltpu.roll(av, shift, 0), 1.0)
            b_sh = jnp.where(keep, pltpu.roll(bv, shift, 0), 0.0)
            bv = av * b_sh + bv
            av = av * a_sh
        elif reverse:
            bv = jnp.concatenate([av[:tc - s] * bv[s:] + bv[:tc - s], bv[tc - s:]], axis=0)
            av = jnp.concatenate([av[:tc - s] * av[s:], av[tc - s:]], axis=0)
        else:
            bv = jnp.concatenate([bv[:s], av[s:] * bv[:tc - s] + bv[s:]], axis=0)
            av = jnp.concatenate([av[:s], av[s:] * av[:tc - s]], axis=0)
        s *= 2
    return av, bv


N_LRU_SAVED = 5


def _fill_block_diag(w_ref, bd_ref):
    bd_ref[...] = jnp.zeros_like(bd_ref)
    hd = LRU_HEAD_DIM
    for k in range(w_ref.shape[0]):
        bd_ref[k * hd:(k + 1) * hd, k * hd:(k + 1) * hd] = w_ref[k].astype(BF16)


def _lru_fwd(proj, conv_w, conv_b, w_a, b_a, w_x, b_x, lam, carry=None):
    t = proj.shape[0]
    dr = conv_b.shape[1]
    cb = LRU_CB
    tc = _tile(t, 256)
    ncb, ntc = dr // cb, t // tc

    def body(xp_ref, g_ref, cw_ref, cb_ref, wa_ref, ba_ref, wx_ref, bx_ref, lam_ref,
             y_ref, h_ref, saved_ref, prevx_s, hlast_s, wa_s, wx_s):
        c = pl.program_id(1)

        @pl.when(c == 0)
        def _():
            prevx_s[...] = jnp.zeros_like(prevx_s)
            hlast_s[...] = jnp.zeros_like(hlast_s)
            _fill_block_diag(wa_ref, wa_s)
            _fill_block_diag(wx_ref, wx_s)

        x = xp_ref[...].astype(F32)
        prev = prevx_s[...]
        row = lax.broadcasted_iota(jnp.int32, x.shape, 0)

        def sh(j):
            return jnp.where(row >= j, pltpu.roll(x, j, 0), pltpu.roll(prev, j, 0))

        xc = (cb_ref[...] + cw_ref[0:1, :] * sh(3) + cw_ref[1:2, :] * sh(2)
              + cw_ref[2:3, :] * sh(1) + cw_ref[3:4, :] * x)
        prevx_s[...] = x
        _, r, i, _, _, a, mult = _lru_gates(xc, wa_s[...], ba_ref[...], wx_s[...], bx_ref[...],
                                            lam_ref[...])
        for k, val in enumerate((xc, r, i, a, mult)):
            saved_ref[:, k * cb:(k + 1) * cb] = val
        av, bv = _scan_rows(a, mult * (i * xc), reverse=False)
        h = av * hlast_s[...] + bv
        h_ref[...] = h
        hlast_s[...] = h_ref[tc - 1:tc, :]
        gel, _ = _gelu_and_grad(g_ref[...].astype(F32))
        y_ref[...] = (h * gel).astype(BF16)

    vec = pl.BlockSpec((1, cb), lambda j, c: (0, j))
    blk = pl.BlockSpec((tc, cb), lambda j, c: (c, j))
    mat = pl.BlockSpec((cb // LRU_HEAD_DIM, LRU_HEAD_DIM, LRU_HEAD_DIM), lambda j, c: (j, 0, 0))
    return _call(
        body, name="lru_fwd", grid=(ncb, ntc),
        in_specs=[
            blk,
            pl.BlockSpec((tc, cb), lambda j, c: (c, ncb + j)),
            pl.BlockSpec((4, cb), lambda j, c: (0, j)),
            vec, mat, vec, mat, vec, vec,
        ],
        out_specs=[blk, blk, pl.BlockSpec((tc, N_LRU_SAVED * cb), lambda j, c: (c, j))],
        out_shape=[jax.ShapeDtypeStruct((t, dr), BF16), jax.ShapeDtypeStruct((t, dr), F32),
                   jax.ShapeDtypeStruct((t, N_LRU_SAVED * dr), F32)],
        scratch_shapes=[pltpu.VMEM((tc, cb), F32), pltpu.VMEM((1, cb), F32),
                        pltpu.VMEM((cb, cb), BF16), pltpu.VMEM((cb, cb), BF16)],
        args=(proj, proj, conv_w, conv_b, w_a, b_a, w_x, b_x, lam), carry=carry)


def _pool_select(col, vals):
    out = vals[3]
    for g in (2, 1, 0):
        out = jnp.where(col < (g + 1) * POOL_GROUP_DIM, vals[g], out)
    return out


def _pool_fwd(proj, pool_w, pool_scale, col_block):
    t = proj.shape[0]
    dp = pool_scale.shape[1]
    tc = _tile(t, 256)
    ntc = t // tc

    def body(x_ref, w_ref, sc_ref, y_ref, p_ref, px, p2, p4, p8):
        c = pl.program_id(0)

        @pl.when(c == 0)
        def _():
            for s in (px, p2, p4, p8):
                s[...] = jnp.zeros_like(s)

        x = x_ref[...].astype(F32)
        row = lax.broadcasted_iota(jnp.int32, x.shape, 0)
        col = lax.broadcasted_iota(jnp.int32, x.shape, 1)

        def sh(v, pv, j):
            return jnp.where(row >= j, pltpu.roll(v, j, 0), pltpu.roll(pv[...], j, 0))

        s2 = x + sh(x, px, 1)
        s4 = s2 + sh(s2, p2, 2)
        s8 = s4 + sh(s4, p4, 4)
        s16 = s8 + sh(s8, p8, 8)
        px[...] = x
        p2[...] = s2
        p4[...] = s4
        p8[...] = s8
        wsum = _pool_select(col, (s2, s4, s8, s16))
        win = _pool_select(col, POOL_WINDOWS)
        cnt = jnp.minimum(c * tc + row + 1, win).astype(F32)
        p = wsum / cnt - x
        pb = p.astype(BF16)
        p_ref[...] = pb
        for g in range(len(POOL_WINDOWS)):
            sl = slice(g * POOL_GROUP_DIM, (g + 1) * POOL_GROUP_DIM)
            yg = _dot_nn(pb[:, sl], w_ref[g]) * sc_ref[:, sl]
            y_ref[:, sl] = yg.astype(BF16)

    return _call(
        body, name="pool_fwd", grid=(ntc,),
        in_specs=[
            pl.BlockSpec((tc, dp), lambda c: (c, col_block)),
            pl.BlockSpec(pool_w.shape, lambda c: (0, 0, 0)),
            pl.BlockSpec((1, dp), lambda c: (0, 0)),
        ],
        out_specs=[pl.BlockSpec((tc, dp), lambda c: (c, 0))] * 2,
        out_shape=[jax.ShapeDtypeStruct((t, dp), BF16)] * 2,
        scratch_shapes=[pltpu.VMEM((tc, dp), F32)] * 4,
        args=(proj, pool_w, pool_scale))[0]


def _branch_mix(y_lru, y_pool, w_lru_up, w_pool_upb, proj, b_gate, ga_block, gb_block, carry=None):
    t, d = y_lru.shape
    dp = y_pool.shape[1]
    bw = w_pool_upb.shape[2]
    tt, tn = _tile(t, 1024), 512
    nj = d // tn

    def body(yl_ref, yp_ref, wl_ref, wp_ref, ga_ref, gb_ref, ba_ref, bb_ref, bra_ref, brb_ref, mix_ref):
        br_a = _dot_nn(yl_ref[...], wl_ref[...])
        wp = jnp.concatenate([wp_ref[b] for b in range(tn // bw)], axis=1)
        br_b = _dot_nn(yp_ref[...], wp)
        bra_ref[...] = br_a.astype(BF16)
        brb_ref[...] = br_b.astype(BF16)
        ga = _sig(ga_ref[...].astype(F32) + ba_ref[...])
        gb = _sig(gb_ref[...].astype(F32) + bb_ref[...])
        mix_ref[...] = (ga * br_a + gb * br_b).astype(BF16)

    out = pl.BlockSpec((tt, tn), lambda j, i: (i, j))
    return _call(
        body, name="branch_mix", grid=(nj, t // tt),
        in_specs=[
            pl.BlockSpec((tt, d), lambda j, i: (i, 0)),
            pl.BlockSpec((tt, dp), lambda j, i: (i, 0)),
            pl.BlockSpec((d, tn), lambda j, i: (0, j)),
            pl.BlockSpec((tn // bw, dp, bw), lambda j, i: (j, 0, 0)),
            pl.BlockSpec((tt, tn), lambda j, i: (i, ga_block + j)),
            pl.BlockSpec((tt, tn), lambda j, i: (i, gb_block + j)),
            pl.BlockSpec((1, tn), lambda j, i: (0, j)),
            pl.BlockSpec((1, tn), lambda j, i: (0, nj + j)),
        ],
        out_specs=[out, out, out],
        out_shape=[jax.ShapeDtypeStruct((t, d), BF16)] * 3,
        args=(y_lru, y_pool, w_lru_up, w_pool_upb, proj, proj, b_gate, b_gate), carry=carry)


def _wo_norm(mix, w_o, x, g2, g3, carry=None):
    t, d = x.shape
    tt = _tile(t, 512)

    def body(mix_ref, w_ref, x_ref, g2_ref, g3_ref, m_ref, x2_ref, h3_ref):
        m = _dot_nn(mix_ref[...], w_ref[...])
        m_ref[...] = m
        mhat, _ = _rms_hat(m)
        x2 = x_ref[...] + mhat * g2_ref[...]
        x2_ref[...] = x2
        xhat, _ = _rms_hat(x2)
        h3_ref[...] = (xhat * g3_ref[...]).astype(BF16)

    row = pl.BlockSpec((tt, d), lambda i: (i, 0))
    vec = pl.BlockSpec((1, d), lambda i: (0, 0))
    return _call(
        body, name="wo_norm", grid=(t // tt,),
        in_specs=[row, pl.BlockSpec((d, d), lambda i: (0, 0)), row, vec, vec],
        out_specs=[row, row, row],
        out_shape=[
            jax.ShapeDtypeStruct((t, d), F32),
            jax.ShapeDtypeStruct((t, d), F32),
            jax.ShapeDtypeStruct((t, d), BF16),
        ],
        args=(mix, w_o, x, g2, g3), carry=carry)


def _ff1(h3, w_ff1b, carry=None):
    t, d = h3.shape
    nb, _, tn = w_ff1b.shape
    tt = _tile(t, 2048)

    def body(h_ref, w_ref, rf_ref):
        rf_ref[...] = jnp.maximum(_dot_nn(h_ref[...], w_ref[...]), 0.0).astype(BF16)

    out = pl.BlockSpec((tt, tn), lambda i, j: (i, j))
    return _call(
        body, name="ff1", grid=(t // tt, nb),
        in_specs=[pl.BlockSpec((tt, d), lambda i, j: (i, 0)), pl.BlockSpec((None, d, tn), lambda i, j: (j, 0, 0))],
        out_specs=[out],
        out_shape=[jax.ShapeDtypeStruct((t, nb * tn), BF16)],
        args=(h3, w_ff1b), carry=carry)


def _ff2_loss(rf, w_ff2, x2, g4, target):
    t, k = rf.shape
    d = x2.shape[1]
    tt, tk = _tile(t, 1024), _tile(k, 1024)
    nk = k // tk

    def body(a_ref, w_ref, x2_hbm, g_ref, tg_hbm, dy_ref, df_ref, dg_ref, loss_ref, acc, x2_ref, tg_ref,
             late_sems):
        i, kk = pl.program_id(0), pl.program_id(1)
        late = _late_copies(i, tt, [(x2_hbm, x2_ref), (tg_hbm, tg_ref)], late_sems)

        @pl.when(kk == 0)
        def _():
            acc[...] = jnp.zeros_like(acc)
            for cp in late:
                cp.start()

        @pl.when((i == 0) & (kk == 0))
        def _():
            dg_ref[...] = jnp.zeros_like(dg_ref)
            loss_ref[...] = jnp.zeros_like(loss_ref)

        rf_tile = a_ref[...]
        acc[...] += _dot_nn(rf_tile * rf_tile, w_ref[...])

        @pl.when(kk == nk - 1)
        def _():
            for cp in late:
                cp.wait()

            def tail(rows):
                fhat, r = _rms_hat(acc[rows, :])
                g = g_ref[...]
                e = x2_ref[rows, :] + fhat * g - tg_ref[rows, :]
                loss_ref[...] += 0.5 * jnp.sum(jnp.mean(e * e, axis=-1, keepdims=True))
                dy = e * (1.0 / d)
                dy_ref[rows, :] = dy.astype(BF16)
                df, dg = _rms_bwd(dy, fhat, r, g)
                df_ref[rows, :] = df.astype(BF16)
                dg_ref[...] += dg

            _row_chunks(tt, tail)

    row = pl.BlockSpec((tt, d), lambda i, kk: (i, 0))
    vec = pl.BlockSpec((1, d), lambda i, kk: (0, 0))
    return _call(
        body, name="ff2_loss", grid=(t // tt, nk),
        in_specs=[
            pl.BlockSpec((tt, tk), lambda i, kk: (i, kk)),
            pl.BlockSpec((tk, d), lambda i, kk: (kk, 0)),
            ANY, vec, ANY,
        ],
        out_specs=[row, row, vec, pl.BlockSpec((1, 128), lambda i, kk: (0, 0))],
        out_shape=[
            jax.ShapeDtypeStruct((t, d), BF16),
            jax.ShapeDtypeStruct((t, d), BF16),
            jax.ShapeDtypeStruct((1, d), F32),
            jax.ShapeDtypeStruct((1, 128), F32),
        ],
        scratch_shapes=[pltpu.VMEM((tt, d), F32), pltpu.VMEM((tt, d), x2.dtype), pltpu.VMEM((tt, d), target.dtype),
                        pltpu.SemaphoreType.DMA((2,))],
        args=(rf, w_ff2, x2, g4, target))[0]


def _ff2_bwd(df, w_ff2, rf, carry=None):
    t, d = df.shape
    n = w_ff2.shape[0]
    tt, tn = _tile(t, 2048), _tile(n, 512)

    def body(df_ref, w_ref, rf_ref, out_ref):
        d_act = _dot_nt(df_ref[...], w_ref[...])
        out_ref[...] = (d_act * (2.0 * rf_ref[...].astype(F32))).astype(BF16)

    blk = pl.BlockSpec((tt, tn), lambda i, j: (i, j))
    return _call(
        body, name="ff2_bwd", grid=(t // tt, n // tn),
        in_specs=[pl.BlockSpec((tt, d), lambda i, j: (i, 0)), pl.BlockSpec((tn, d), lambda i, j: (j, 0)), blk],
        out_specs=[blk],
        out_shape=[jax.ShapeDtypeStruct((t, n), BF16)],
        args=(df, w_ff2, rf), carry=carry)


def _wgrad(a, b, name, prev=None, row_off=0, rows=None, carry=None, square_a=False):
    t, m = a.shape
    n = b.shape[1]
    rows = m if rows is None else rows
    tm, tk = _tile(m, 512), _tile(t, 2048)
    nk = t // tk
    assert row_off % tm == 0
    off = row_off // tm

    def body(*refs):
        a_ref, b_ref = refs[0], refs[1]
        o32_ref, o16_ref, acc = refs[-3], refs[-2], refs[-1]
        kk = pl.program_id(1)

        @pl.when(kk == 0)
        def _():
            acc[...] = jnp.zeros_like(acc)

        a_tile = a_ref[...]
        acc[...] += _dot_tn(a_tile * a_tile if square_a else a_tile, b_ref[...])

        @pl.when(kk == nk - 1)
        def _():
            o32_ref[...] = acc[...]
            o16_ref[...] = acc[...].astype(BF16)

    in_specs = [pl.BlockSpec((tk, tm), lambda i, kk: (kk, i)), pl.BlockSpec((tk, n), lambda i, kk: (kk, 0))]
    args = [a, b]
    aliases = {}
    if prev is not None:
        in_specs += [ANY, ANY]
        args += list(prev)
        aliases = {2: 0, 3: 1}
    out = pl.BlockSpec((tm, n), lambda i, kk: (off + i, 0))
    return _call(
        body, name=name, grid=(m // tm, nk),
        in_specs=in_specs, out_specs=[out, out],
        out_shape=[jax.ShapeDtypeStruct((rows, n), F32), jax.ShapeDtypeStruct((rows, n), BF16)],
        scratch_shapes=[pltpu.VMEM((tm, n), F32)],
        aliases=aliases, args=args, carry=carry)


def _wgrad_parts(parts, b, name, carry=None):
    t, n = b.shape
    tm = 512
    bounds = []
    lo = 0
    for part in parts:
        assert part.shape[0] == t and part.shape[1] % tm == 0
        bounds.append((lo, lo + part.shape[1] // tm))
        lo += part.shape[1] // tm
    nm = lo
    np_ = len(parts)

    def body(*refs):
        p_refs, b_ref, o32_ref, o16_ref = refs[:np_], refs[np_], refs[np_ + 1], refs[np_ + 2]
        i = pl.program_id(0)
        for (lo_p, hi_p), p_ref in zip(bounds, p_refs):
            @pl.when((i >= lo_p) & (i < hi_p))
            def _(p_ref=p_ref):
                res = _dot_tn(p_ref[...], b_ref[...])
                o32_ref[...] = res
                o16_ref[...] = res.astype(BF16)

    def part_spec(lo_p, hi_p):
        return pl.BlockSpec((t, tm), lambda i: (0, jnp.clip(i - lo_p, 0, hi_p - lo_p - 1)))

    out = pl.BlockSpec((tm, n), lambda i: (i, 0))
    return _call(
        body, name=name, grid=(nm,),
        in_specs=[part_spec(lo_p, hi_p) for lo_p, hi_p in bounds] + [pl.BlockSpec((t, n), lambda i: (0, 0))],
        out_specs=[out, out],
        out_shape=[jax.ShapeDtypeStruct((nm * tm, n), F32), jax.ShapeDtypeStruct((nm * tm, n), BF16)],
        args=(*parts, b), carry=carry)


def _wgrad_cols(a, b, bw, tn, name, carry=None):
    t, m = a.shape
    n = b.shape[1]
    per_step = tn // bw

    def body(a_ref, b_ref, o32_ref, o16_ref):
        res = _dot_tn(a_ref[...], b_ref[...])
        for blk in range(per_step):
            part = res[:, blk * bw:(blk + 1) * bw]
            o32_ref[blk] = part
            o16_ref[blk] = part.astype(BF16)

    out = pl.BlockSpec((per_step, m, bw), lambda j: (j, 0, 0))
    return _call(
        body, name=name, grid=(n // tn,),
        in_specs=[pl.BlockSpec((t, m), lambda j: (0, 0)), pl.BlockSpec((t, tn), lambda j: (0, j))],
        out_specs=[out, out],
        out_shape=[jax.ShapeDtypeStruct((n // bw, m, bw), F32), jax.ShapeDtypeStruct((n // bw, m, bw), BF16)],
        args=(a, b), carry=carry)


def _ff1_bwd_norms(d_f1, w_ff1b, dy, x2, g3, m, g2, carry=None):
    t, k = d_f1.shape
    d = x2.shape[1]
    assert dy.dtype == BF16 and x2.dtype == F32
    bw = w_ff1b.shape[2]
    per_step = 2
    tt, tk = _tile(t, 1024), per_step * bw
    nk = k // tk

    def body(a_ref, w_ref, dy_hbm, x2_hbm, g3_ref, m_hbm, g2_ref, dx2_hbm, dm_hbm, dg3_ref, dg2_ref, acc,
             dy_ref, x2_ref, m_ref, late_sems, out_sems):
        i, kk = pl.program_id(0), pl.program_id(1)
        late = _late_copies(i, tt, [(dy_hbm, dy_ref), (x2_hbm, x2_ref), (m_hbm, m_ref)], late_sems)

        @pl.when(kk == 0)
        def _():
            acc[...] = jnp.zeros_like(acc)
            for cp in late:
                cp.start()

        @pl.when((i == 0) & (kk == 0))
        def _():
            dg3_ref[...] = jnp.zeros_like(dg3_ref)
            dg2_ref[...] = jnp.zeros_like(dg2_ref)

        a_tile = a_ref[...]
        for b in range(per_step):
            acc[...] += _dot_nt(a_tile[:, b * bw:(b + 1) * bw], w_ref[b])

        @pl.when(kk == nk - 1)
        def _():
            for cp in late:
                cp.wait()

            def tail(rows):
                xhat, r3 = _rms_hat(x2_ref[rows, :])
                dx, dg3 = _rms_bwd(acc[rows, :], xhat, r3, g3_ref[...])
                dx2 = dy_ref[rows, :].astype(F32) + dx
                x2_ref[rows, :] = dx2
                dg3_ref[...] += dg3
                mhat, r2 = _rms_hat(m_ref[rows, :])
                dm, dg2 = _rms_bwd(dx2, mhat, r2, g2_ref[...])
                dy_ref[rows, :] = dm.astype(BF16)
                dg2_ref[...] += dg2

            _row_chunks(tt, tail)
            tile = pl.ds(pl.multiple_of(i * tt, tt), tt)
            outs = [pltpu.make_async_copy(x2_ref, dx2_hbm.at[tile], out_sems.at[0]),
                    pltpu.make_async_copy(dy_ref, dm_hbm.at[tile], out_sems.at[1])]
            for cp in outs:
                cp.start()
            for cp in outs:
                cp.wait()

    vec = pl.BlockSpec((1, d), lambda i, kk: (0, 0))
    return _call(
        body, name="ff1_bwd_norms", grid=(t // tt, nk),
        in_specs=[
            pl.BlockSpec((tt, tk), lambda i, kk: (i, kk)),
            pl.BlockSpec((per_step, d, bw), lambda i, kk: (kk, 0, 0)),
            ANY, ANY, vec, ANY, vec,
        ],
        out_specs=[ANY, ANY, vec, vec],
        out_shape=[
            jax.ShapeDtypeStruct((t, d), F32),
            jax.ShapeDtypeStruct((t, d), BF16),
            jax.ShapeDtypeStruct((1, d), F32),
            jax.ShapeDtypeStruct((1, d), F32),
        ],
        scratch_shapes=[pltpu.VMEM((tt, d), F32), pltpu.VMEM((tt, d), dy.dtype), pltpu.VMEM((tt, d), F32),
                        pltpu.VMEM((tt, d), F32), pltpu.SemaphoreType.DMA((3,)), pltpu.SemaphoreType.DMA((2,))],
        args=(d_f1, w_ff1b, dy, x2, g3, m, g2), carry=carry)


def _wo_bwd_mix(dm, w_o, br_a, br_b, proj, b_gate, ga_block, gb_block, carry=None):
    t, d = dm.shape
    tt, tn = _tile(t, 1024), 512
    nj = d // tn

    def body(dm_ref, w_ref, bra_ref, brb_ref, ga_ref, gb_ref, ba_ref, bb_ref,
             dbra_ref, dbrb_ref, dga_ref, dgb_ref, dba_ref, dbb_ref):
        i = pl.program_id(1)

        @pl.when(i == 0)
        def _():
            dba_ref[...] = jnp.zeros_like(dba_ref)
            dbb_ref[...] = jnp.zeros_like(dbb_ref)

        d_mix = _dot_nt(dm_ref[...], w_ref[...])
        ga = _sig(ga_ref[...].astype(F32) + ba_ref[...])
        gb = _sig(gb_ref[...].astype(F32) + bb_ref[...])
        dbra_ref[...] = (d_mix * ga).astype(BF16)
        dbrb_ref[...] = (d_mix * gb).astype(BF16)
        dga = d_mix * bra_ref[...].astype(F32) * (ga * (1.0 - ga))
        dgb = d_mix * brb_ref[...].astype(F32) * (gb * (1.0 - gb))
        dga_ref[...] = dga.astype(BF16)
        dgb_ref[...] = dgb.astype(BF16)
        dba_ref[...] += jnp.sum(dga, axis=0, keepdims=True)
        dbb_ref[...] += jnp.sum(dgb, axis=0, keepdims=True)

    blk = pl.BlockSpec((tt, tn), lambda j, i: (i, j))
    vec = pl.BlockSpec((1, tn), lambda j, i: (0, j))
    return _call(
        body, name="wo_bwd_mix", grid=(nj, t // tt),
        in_specs=[
            pl.BlockSpec((tt, d), lambda j, i: (i, 0)),
            pl.BlockSpec((tn, d), lambda j, i: (j, 0)),
            blk, blk,
            pl.BlockSpec((tt, tn), lambda j, i: (i, ga_block + j)),
            pl.BlockSpec((tt, tn), lambda j, i: (i, gb_block + j)),
            vec,
            pl.BlockSpec((1, tn), lambda j, i: (0, nj + j)),
        ],
        out_specs=[blk, blk, blk, blk, vec, vec],
        out_shape=[jax.ShapeDtypeStruct((t, d), BF16)] * 4 + [jax.ShapeDtypeStruct((1, d), F32)] * 2,
        args=(dm, w_o, br_a, br_b, proj, proj, b_gate, b_gate), carry=carry)


def _lru_up_bwd(d_br_a, w_lru_up, proj, h, g_block, carry=None):
    t, d = d_br_a.shape
    tt, tn = _tile(t, 1024), 512

    def body(a_ref, w_ref, g_ref, h_ref, dh_ref, dg_ref):
        d_y = _dot_nt(a_ref[...], w_ref[...])
        gel, gel_grad = _gelu_and_grad(g_ref[...].astype(F32))
        dh_ref[...] = d_y * gel
        dg_ref[...] = (d_y * h_ref[...] * gel_grad).astype(BF16)

    blk = pl.BlockSpec((tt, tn), lambda i, j: (i, j))
    return _call(
        body, name="lru_up_bwd", grid=(t // tt, d // tn),
        in_specs=[
            pl.BlockSpec((tt, d), lambda i, j: (i, 0)),
            pl.BlockSpec((tn, d), lambda i, j: (j, 0)),
            pl.BlockSpec((tt, tn), lambda i, j: (i, g_block + j)),
            blk,
        ],
        out_specs=[blk, blk],
        out_shape=[jax.ShapeDtypeStruct((t, d), F32), jax.ShapeDtypeStruct((t, d), BF16)],
        args=(d_br_a, w_lru_up, proj, h), carry=carry)


def _lru_bwd(dh, h, saved, proj, conv_w, w_a, w_x, lam, carry=None):
    t, dr = dh.shape
    cb = LRU_CB
    hd = LRU_HEAD_DIM
    per = cb // hd
    tc = _tile(t, 256)
    ncb, ntc = dr // cb, t // tc

    def body(dh_ref, h_ref, hp_ref, saved_ref, xp_ref, cw_ref, wa_ref, wx_ref,
             lam_ref, dxp_ref, dwa_ref, dba_ref, dwx_ref, dbx_ref, dlam_ref, dcw_ref, dcb_ref,
             nextd_s, anext_s, gnext_s, tmp_s, wa_s, wx_s):
        c = pl.program_id(1)
        rc = ntc - 1 - c

        @pl.when(c == 0)
        def _():
            nextd_s[...] = jnp.zeros_like(nextd_s)
            anext_s[...] = jnp.zeros_like(anext_s)
            gnext_s[...] = jnp.zeros_like(gnext_s)
            for ref in (dwa_ref, dba_ref, dwx_ref, dbx_ref, dlam_ref, dcw_ref, dcb_ref):
                ref[...] = jnp.zeros_like(ref)
            _fill_block_diag(wa_ref, wa_s)
            _fill_block_diag(wx_ref, wx_s)

        xc, r, i, a, mult = [saved_ref[:, k * cb:(k + 1) * cb] for k in range(N_LRU_SAVED)]
        wa, wx, lam = wa_s[...], wx_s[...], lam_ref[...]
        xcb = xc.astype(BF16)
        sp = _softplus_neg(lam)
        row = lax.broadcasted_iota(jnp.int32, xc.shape, 0)
        h = h_ref[...]
        hp = jnp.where(rc == 0, 0.0, hp_ref[...])
        hprev = jnp.where(row >= 1, pltpu.roll(h, 1, 0), pltpu.roll(hp, 1, 0))

        def up(v, nv, j):
            return jnp.where(row < tc - j, pltpu.roll(v, tc - j, 0), nv)

        av, bv = _scan_rows(up(a, anext_s[...], 1), dh_ref[...], reverse=True)
        gt = av * gnext_s[...] + bv
        tmp_s[...] = gt
        gnext_s[...] = tmp_s[0:1, :]
        tmp_s[...] = a
        anext_s[...] = tmp_s[0:1, :]

        da = gt * hprev
        ixc = i * xc
        d_mult = gt * ixc
        d_i = gt * mult * xc
        d_xc = gt * mult * i
        d_log_a = da * a - d_mult * (a * a) / mult
        d_pre_r = (d_log_a * ((-LRU_C) * sp)) * (r * (1.0 - r))
        d_pre_i = d_i * (i * (1.0 - i))
        d_sp = jnp.sum(d_log_a * ((-LRU_C) * r), axis=0, keepdims=True)
        dlam_ref[...] += d_sp * (-1.0 / (1.0 + jnp.exp(lam)))
        dpr = d_pre_r.astype(BF16)
        dpi = d_pre_i.astype(BF16)
        dba_ref[...] += jnp.sum(d_pre_r, axis=0, keepdims=True)
        dbx_ref[...] += jnp.sum(d_pre_i, axis=0, keepdims=True)
        pa = _dot_tn(xcb, dpr)
        px = _dot_tn(xcb, dpi)
        for k in range(per):
            dwa_ref[k] += pa[k * hd:(k + 1) * hd, k * hd:(k + 1) * hd]
            dwx_ref[k] += px[k * hd:(k + 1) * hd, k * hd:(k + 1) * hd]
        d_xc = d_xc + _dot_nt(dpr, wa) + _dot_nt(dpi, wx)

        nxt = nextd_s[...]
        xp = xp_ref[...].astype(F32)
        dxp = cw_ref[3:4, :] * d_xc
        dcw_ref[3:4, :] += jnp.sum(xp * d_xc, axis=0, keepdims=True)
        for j in (1, 2, 3):
            uj = up(d_xc, pltpu.roll(nxt, tc - j, 0), j)
            dxp = dxp + cw_ref[3 - j:4 - j, :] * uj
            dcw_ref[3 - j:4 - j, :] += jnp.sum(xp * uj, axis=0, keepdims=True)
        dcb_ref[...] += jnp.sum(d_xc, axis=0, keepdims=True)
        nextd_s[...] = d_xc
        dxp_ref[...] = dxp.astype(BF16)

    vec = pl.BlockSpec((1, cb), lambda j, c: (0, j))
    blk = pl.BlockSpec((tc, cb), lambda j, c: (ntc - 1 - c, j))
    mat = pl.BlockSpec((per, hd, hd), lambda j, c: (j, 0, 0))
    cwb = pl.BlockSpec((4, cb), lambda j, c: (0, j))
    return _call(
        body, name="lru_bwd", grid=(ncb, ntc),
        in_specs=[
            blk, blk,
            pl.BlockSpec((tc, cb), lambda j, c: (jnp.maximum(ntc - 2 - c, 0), j)),
            pl.BlockSpec((tc, N_LRU_SAVED * cb), lambda j, c: (ntc - 1 - c, j)),
            blk, cwb, mat, mat, vec,
        ],
        out_specs=[blk, mat, vec, mat, vec, vec, cwb, vec],
        out_shape=[
            jax.ShapeDtypeStruct((t, dr), BF16),
            jax.ShapeDtypeStruct(w_a.shape, F32),
            jax.ShapeDtypeStruct((1, dr), F32),
            jax.ShapeDtypeStruct(w_x.shape, F32),
            jax.ShapeDtypeStruct((1, dr), F32),
            jax.ShapeDtypeStruct((1, dr), F32),
            jax.ShapeDtypeStruct((4, dr), F32),
            jax.ShapeDtypeStruct((1, dr), F32),
        ],
        scratch_shapes=[
            pltpu.VMEM((tc, cb), F32),
            pltpu.VMEM((1, cb), F32),
            pltpu.VMEM((1, cb), F32),
            pltpu.VMEM((tc, cb), F32),
            pltpu.VMEM((cb, cb), BF16),
            pltpu.VMEM((cb, cb), BF16),
        ],
        args=(dh, h, h, saved, proj, conv_w, w_a, w_x, lam), carry=carry)


def _pool_bwd(d_br_b, w_pool_upb, p, pool_w, pool_scale):
    t, d = d_br_b.shape
    nwb, dp, _ = w_pool_upb.shape
    tc = _tile(t, 256)
    ntc = t // tc
    ng = len(POOL_WINDOWS)

    def body(db_ref, wu_ref, p_ref, w_ref, sc_ref, dx_ref, dw_ref, dsc_ref, nz, n2, n4, n8, dp_s, dy_s):
        c = pl.program_id(0)
        rc = ntc - 1 - c

        @pl.when(c == 0)
        def _():
            for s in (nz, n2, n4, n8):
                s[...] = jnp.zeros_like(s)
            dw_ref[...] = jnp.zeros_like(dw_ref)
            dsc_ref[...] = jnp.zeros_like(dsc_ref)

        wu = jnp.concatenate([wu_ref[b] for b in range(nwb)], axis=1)
        dy_s[...] = _dot_nt(db_ref[...], wu)
        for g in range(ng):
            sl = slice(g * POOL_GROUP_DIM, (g + 1) * POOL_GROUP_DIM)
            pg = p_ref[:, sl]
            dyg = dy_s[:, sl]
            wg = w_ref[g].astype(BF16)
            q = _dot_nn(pg, wg)
            dsc_ref[:, sl] += jnp.sum(dyg * q, axis=0, keepdims=True)
            dpw = (dyg * sc_ref[:, sl]).astype(BF16)
            dw_ref[g] += _dot_tn(pg, dpw)
            dp_s[:, sl] = _dot_nt(dpw, wg)

        dpv = dp_s[...]
        row = lax.broadcasted_iota(jnp.int32, dpv.shape, 0)
        col = lax.broadcasted_iota(jnp.int32, dpv.shape, 1)
        win = _pool_select(col, POOL_WINDOWS)
        cnt = jnp.minimum(rc * tc + row + 1, win).astype(F32)
        z = dpv / cnt

        def up(v, nv, j):
            return jnp.where(row < tc - j, pltpu.roll(v, tc - j, 0), pltpu.roll(nv[...], tc - j, 0))

        u2 = z + up(z, nz, 1)
        u4 = u2 + up(u2, n2, 2)
        u8 = u4 + up(u4, n4, 4)
        u16 = u8 + up(u8, n8, 8)
        nz[...] = z
        n2[...] = u2
        n4[...] = u4
        n8[...] = u8
        dx_ref[...] = (_pool_select(col, (u2, u4, u8, u16)) - dpv).astype(BF16)

    blk = pl.BlockSpec((tc, dp), lambda c: (ntc - 1 - c, 0))
    full_w = pl.BlockSpec(pool_w.shape, lambda c: (0, 0, 0))
    vec = pl.BlockSpec((1, dp), lambda c: (0, 0))
    return _call(
        body, name="pool_bwd", grid=(ntc,),
        in_specs=[pl.BlockSpec((tc, d), lambda c: (ntc - 1 - c, 0)),
                  pl.BlockSpec(w_pool_upb.shape, lambda c: (0, 0, 0)), blk, full_w, vec],
        out_specs=[blk, full_w, vec],
        out_shape=[
            jax.ShapeDtypeStruct((t, dp), BF16),
            jax.ShapeDtypeStruct(pool_w.shape, F32),
            jax.ShapeDtypeStruct((1, dp), F32),
        ],
        scratch_shapes=[pltpu.VMEM((tc, dp), F32)] * 6,
        args=(d_br_b, w_pool_upb, p, pool_w, pool_scale))[0]


def _win_bwd_norm(parts, w_int, dx2, x, g1, carry=None):
    t, d = x.shape
    tk = 512
    tt = _tile(t, 1024)
    bounds = []
    k0 = 0
    for part in parts:
        assert part.shape[1] % tk == 0
        bounds.append((k0, k0 + part.shape[1] // tk))
        k0 += part.shape[1] // tk
    nk = k0
    assert nk * tk == w_int.shape[0]
    np_ = len(parts)

    def body(*refs):
        p_refs = refs[:np_]
        w_ref, dx2_hbm, x_hbm, g_ref, gx_hbm, dg_ref, acc, dx2_ref, x_ref, late_sems, out_sem = refs[np_:]
        i, kk = pl.program_id(0), pl.program_id(1)
        late = _late_copies(i, tt, [(dx2_hbm, dx2_ref), (x_hbm, x_ref)], late_sems)

        @pl.when(kk == 0)
        def _():
            acc[...] = jnp.zeros_like(acc)
            for cp in late:
                cp.start()

        @pl.when((i == 0) & (kk == 0))
        def _():
            dg_ref[...] = jnp.zeros_like(dg_ref)

        for (lo, hi), p_ref in zip(bounds, p_refs):
            @pl.when((kk >= lo) & (kk < hi))
            def _(p_ref=p_ref):
                acc[...] += _dot_nn(p_ref[...], w_ref[...])

        @pl.when(kk == nk - 1)
        def _():
            for cp in late:
                cp.wait()

            def tail(rows):
                xhat, r = _rms_hat(x_ref[rows, :])
                dx, dg = _rms_bwd(acc[rows, :], xhat, r, g_ref[...])
                dx2_ref[rows, :] = dx2_ref[rows, :] + dx
                dg_ref[...] += dg

            _row_chunks(tt, tail)
            out = pltpu.make_async_copy(
                dx2_ref, gx_hbm.at[pl.ds(pl.multiple_of(i * tt, tt), tt)], out_sem.at[0])
            out.start()
            out.wait()

    def part_spec(lo, hi):
        return pl.BlockSpec((tt, tk), lambda i, kk: (i, jnp.clip(kk - lo, 0, hi - lo - 1)))

    vec = pl.BlockSpec((1, d), lambda i, kk: (0, 0))
    return _call(
        body, name="win_bwd_norm", grid=(t // tt, nk),
        in_specs=[part_spec(lo, hi) for lo, hi in bounds]
        + [pl.BlockSpec((tk, d), lambda i, kk: (kk, 0)), ANY, ANY, vec],
        out_specs=[ANY, vec],
        out_shape=[jax.ShapeDtypeStruct((t, d), F32), jax.ShapeDtypeStruct((1, d), F32)],
        scratch_shapes=[pltpu.VMEM((tt, d), F32), pltpu.VMEM((tt, d), F32), pltpu.VMEM((tt, d), F32),
                        pltpu.SemaphoreType.DMA((2,)), pltpu.SemaphoreType.DMA((1,))],
        args=(*parts, w_int, dx2, x, g1), carry=carry)


def _adam_math(w, g, m, v):
    m = ADAM_B1 * m + (1.0 - ADAM_B1) * g
    v = ADAM_B2 * v + (1.0 - ADAM_B2) * (g * g)
    m_hat = m / (1.0 - ADAM_B1 ** ADAM_STEP)
    v_hat = v / (1.0 - ADAM_B2 ** ADAM_STEP)
    delta = -ADAM_LR * (m_hat / (jnp.sqrt(v_hat) + ADAM_EPS) + ADAM_WD * w)
    return delta, m, v


def _adamw_big(ws, gs, ms, vs):
    n = len(ws)
    nb = 4
    pair = [isinstance(g, tuple) for g in gs]

    def body(*refs):
        p = 0
        ins = []
        for a in range(n):
            k = 5 if pair[a] else 4
            ins.append(refs[p:p + k])
            p += k
        for a in range(n):
            g_out, d_ref, nm_ref, nv_ref = refs[p + 4 * a:p + 4 * a + 4]
            if pair[a]:
                w_ref, own_ref, recv_ref, m_ref, v_ref = ins[a]
                g = own_ref[...]
                for k in range(3):
                    g = g + recv_ref[k].astype(F32)
            else:
                w_ref, g_ref, m_ref, v_ref = ins[a]
                g = g_ref[...]
            dl, m, v = _adam_math(w_ref[...], g, m_ref[...], v_ref[...])
            g_out[...] = g
            d_ref[...] = dl
            nm_ref[...] = m
            nv_ref[...] = v

    in_specs, out_specs, out_shape, args = [], [], [], []
    for a, (w, g, m, v) in enumerate(zip(ws, gs, ms, vs)):
        rows, cols = w.shape
        blk = pl.BlockSpec((rows // nb, cols), lambda i: (i, 0))
        if pair[a]:
            in_specs += [blk, pl.BlockSpec((None, rows // nb, cols), lambda i: (0, i, 0)),
                         pl.BlockSpec((3, rows // nb, cols), lambda i: (0, i, 0)), blk, blk]
            args += [w, g[0], g[1], m, v]
        else:
            in_specs += [blk] * 4
            args += [w, g, m, v]
        out_specs += [blk] * 4
        out_shape += [jax.ShapeDtypeStruct(w.shape, F32)] * 4
    outs = _call(body, name="adamw_big", grid=(nb,), in_specs=in_specs, out_specs=out_specs,
                 out_shape=out_shape, args=args)[0]
    return [tuple(outs[4 * a:4 * a + 4]) for a in range(n)]


SMALL_ORDER = ("norm_mix_pre", "norm_mix_post", "norm_mlp_pre", "norm_mlp_post", "b_gate", "conv_w", "conv_b",
               "lru_w_a", "lru_b_a", "lru_w_x", "lru_b_x", "lru_lambda", "pool_w", "pool_scale")
VEC_ROW = dict(norm_mix_pre=0, norm_mix_post=1, norm_mlp_pre=2, norm_mlp_post=3, conv_b=6, lru_b_a=7,
               lru_b_x=8, lru_lambda=9)
ROW_B_GATE, ROW_POOL_SCALE, ROW_CONV_W, ROW_LOSS, N_VEC_ROWS = 4, 10, 11, 15, 16


def _adamw_small(vec_parts, g_pool, g_wa, g_wx, me, params):
    d = vec_parts.shape[2]
    names = SMALL_ORDER
    n = len(names)
    cw_cols = params["conv_w"][0].shape[2]

    def body(me_ref, vec_ref, vecc_ref, gp_ref, gwa_ref, gwx_ref, *refs):
        wmv = refs[:3 * n]
        loss_ref = refs[3 * n]
        outs = refs[3 * n + 1:3 * n + 1 + 4 * n]
        vs, vsc = refs[3 * n + 1 + 4 * n:]
        acc, accc = vec_ref[0], vecc_ref[0]
        for k in range(1, N_DEV):
            acc = acc + vec_ref[k]
            accc = accc + vecc_ref[k]
        vs[...] = acc
        vsc[...] = accc
        loss_ref[...] = vs[ROW_LOSS:ROW_LOSS + 1, 0:128]

        def upd(a, g, idx):
            w_ref, m_ref, v_ref = wmv[3 * a:3 * a + 3]
            g_ref, d_ref, nm_ref, nv_ref = outs[4 * a:4 * a + 4]
            dl, m, v = _adam_math(w_ref[idx], g, m_ref[idx], v_ref[idx])
            g_ref[idx] = g
            d_ref[idx] = dl
            nm_ref[idx] = m
            nv_ref[idx] = v

        for a, name in enumerate(names):
            if name in VEC_ROW:
                r = VEC_ROW[name]
                upd(a, vs[r:r + 1, :], (slice(None), slice(None)))
            elif name == "b_gate":
                for half in range(2):
                    r = ROW_B_GATE + half
                    upd(a, vs[r:r + 1, :], (slice(None), slice(half * d, (half + 1) * d)))
            elif name == "pool_scale":
                width = params[name][0].shape[1]
                upd(a, vs[ROW_POOL_SCALE:ROW_POOL_SCALE + 1, 0:width], (slice(None), slice(None)))
            elif name == "conv_w":
                upd(a, vsc[ROW_CONV_W:ROW_CONV_W + 4, :], (0,))
            elif name == "pool_w":
                upd(a, gp_ref[...], (Ellipsis,))
            elif name == "lru_w_a":
                upd(a, gwa_ref[...], (Ellipsis,))
            elif name == "lru_w_x":
                upd(a, gwx_ref[...], (Ellipsis,))
            else:
                raise ValueError(name)

    def whole(shape):
        nd = len(shape)
        return pl.BlockSpec(tuple(shape), lambda i, me_ref: (0,) * nd)

    in_specs = [
        whole(vec_parts.shape),
        pl.BlockSpec((N_DEV, N_VEC_ROWS, cw_cols), lambda i, me_ref: (0, 0, me_ref[0])),
        whole(g_pool.shape), whole(g_wa.shape), whole(g_wx.shape),
    ]
    args = [vec_parts, vec_parts, g_pool, g_wa, g_wx]
    out_specs = [whole((1, 128))]
    out_shape = [jax.ShapeDtypeStruct((1, 128), F32)]
    for name in names:
        for arr in params[name]:
            in_specs.append(whole(arr.shape))
            args.append(arr)
        shp = params[name][0].shape
        out_specs += [whole(shp)] * 4
        out_shape += [jax.ShapeDtypeStruct(shp, F32)] * 4
    grid_spec = pltpu.PrefetchScalarGridSpec(
        num_scalar_prefetch=1, grid=(1,), in_specs=in_specs, out_specs=out_specs,
        scratch_shapes=[pltpu.VMEM((N_VEC_ROWS, d), F32), pltpu.VMEM((N_VEC_ROWS, cw_cols), F32)])
    outs = pl.pallas_call(
        body, name="adamw_small", grid_spec=grid_spec, out_shape=out_shape,
        compiler_params=pltpu.CompilerParams(
            dimension_semantics=("arbitrary",), vmem_limit_bytes=V7X_VMEM_LIMIT_BYTES),
    )(me, *_in_hbm(args))
    return outs[0], {name: tuple(outs[1 + 4 * a:5 + 4 * a]) for a, name in enumerate(names)}


def _rs_sum(fulls, recvs, shard_ids, slot_ids, name):
    n = len(fulls)

    def body(sh_ref, sl_ref, *refs):
        s = pl.program_id(0)
        for a in range(n):
            full_ref, recv_ref = refs[2 * a], refs[2 * a + 1]
            own_ref, send_ref = refs[2 * n + 2 * a], refs[2 * n + 2 * a + 1]
            v = full_ref[...] + recv_ref[...].astype(F32)

            @pl.when(s == 0)
            def _(own_ref=own_ref, v=v):
                own_ref[...] = v

            @pl.when(s > 0)
            def _(send_ref=send_ref, v=v):
                send_ref[...] = v.astype(send_ref.dtype)

    in_specs, out_specs, out_shape, args = [], [], [], []
    for full, recv in zip(fulls, recvs):
        r, rest = recv.shape[1], tuple(recv.shape[2:])
        zeros = (0,) * len(rest)
        in_specs += [
            pl.BlockSpec((r,) + rest, lambda s, sh, sl, zeros=zeros: (sh[s],) + zeros),
            pl.BlockSpec((None, r) + rest, lambda s, sh, sl, zeros=zeros: (sl[s], 0) + zeros),
        ]
        out_specs += [
            pl.BlockSpec((None, r) + rest, lambda s, sh, sl, zeros=zeros: (0, 0) + zeros),
            pl.BlockSpec((None, r) + rest, lambda s, sh, sl, zeros=zeros: (jnp.maximum(s - 1, 0), 0) + zeros),
        ]
        out_shape += [jax.ShapeDtypeStruct((1, r) + rest, F32), jax.ShapeDtypeStruct((3, r) + rest, recv.dtype)]
        args += [full, recv]
    grid_spec = pltpu.PrefetchScalarGridSpec(
        num_scalar_prefetch=2, grid=(4,), in_specs=in_specs, out_specs=out_specs)
    outs = pl.pallas_call(
        body,
        name=name,
        grid_spec=grid_spec,
        out_shape=out_shape,
        compiler_params=pltpu.CompilerParams(
            dimension_semantics=("arbitrary",), vmem_limit_bytes=V7X_VMEM_LIMIT_BYTES),
    )(shard_ids, slot_ids, *_in_hbm(args))
    return [(outs[2 * a], outs[2 * a + 1]) for a in range(n)]


def _finals(pairs, name, carry=None):
    nb = 4
    n = len(pairs)

    def body(*refs):
        for a in range(n):
            own_ref, recv_ref = refs[2 * a], refs[2 * a + 1]
            acc = own_ref[...]
            for k in range(3):
                acc = acc + recv_ref[k].astype(F32)
            refs[2 * n + a][...] = acc

    in_specs, out_specs, out_shape, args = [], [], [], []
    for own, recv in pairs:
        _, rows, cols = own.shape
        in_specs += [pl.BlockSpec((None, rows // nb, cols), lambda i: (0, i, 0)),
                     pl.BlockSpec((3, rows // nb, cols), lambda i: (0, i, 0))]
        args += [own, recv]
        out_specs.append(pl.BlockSpec((rows // nb, cols), lambda i: (i, 0)))
        out_shape.append(jax.ShapeDtypeStruct((rows, cols), F32))
    return _call(body, name=name, grid=(nb,), in_specs=in_specs, out_specs=out_specs,
                 out_shape=out_shape, args=args, carry=carry)


def _rs_sums(fulls_f32, recv1, tag):
    x, y, c = _place()
    qs = jnp.stack([2 * x + y, 2 * (1 - x) + y, 2 * x + (1 - y), 2 * (1 - x) + (1 - y)]).astype(jnp.int32)
    shard_ids = 2 * qs + c
    return _rs_sum(fulls_f32, recv1, shard_ids, qs, "rs_sum_" + tag)


def _rs_level1(fulls_f32, fulls_send, tag):
    recv1 = _run_plan(_rs_sibling_plan(fulls_send), "rs_sibling_" + tag)
    return _rs_sums(fulls_f32, recv1, tag)


def _rows(g):
    return g.reshape(g.shape[0] * g.shape[1], g.shape[2])


def kernel(x, norm_mix_pre, norm_mix_post, norm_mlp_pre, norm_mlp_post, w_in, b_gate, conv_w, conv_b, lru_w_a, lru_b_a, lru_w_x, lru_b_x, lru_lambda, pool_w, pool_scale, w_lru_up, w_pool_up, w_o, w_ff1, w_ff2, loss_target, m_norm_mix_pre, m_norm_mix_post, m_norm_mlp_pre, m_norm_mlp_post, m_w_in, m_b_gate, m_conv_w, m_conv_b, m_lru_w_a, m_lru_b_a, m_lru_w_x, m_lru_b_x, m_lru_lambda, m_pool_w, m_pool_scale, m_w_lru_up, m_w_pool_up, m_w_o, m_w_ff1, m_w_ff2, v_norm_mix_pre, v_norm_mix_post, v_norm_mlp_pre, v_norm_mlp_post, v_w_in, v_b_gate, v_conv_w, v_conv_b, v_lru_w_a, v_lru_b_a, v_lru_w_x, v_lru_b_x, v_lru_lambda, v_pool_w, v_pool_scale, v_w_lru_up, v_w_pool_up, v_w_o, v_w_ff1, v_w_ff2):
    t, d = x.shape[1], x.shape[2]
    d_rnn = conv_b.shape[1]
    d_pool = pool_scale.shape[1]
    per = LRU_CB // LRU_HEAD_DIM
    xi, yi, ci = _place()
    me = 4 * xi + 2 * yi + ci

    x2d = x[0]
    tgt = loss_target[0]

    s_in = w_in[0].T.astype(BF16)
    s_lu = w_lru_up[0].astype(BF16)
    s_pu = w_pool_up[0].astype(BF16)
    s_o = w_o[0].astype(BF16)
    s_f1 = w_ff1[0].astype(BF16)
    s_f2 = w_ff2[0].astype(BF16)
    s_cw = jnp.pad(conv_w[0], ((0, 4), (0, 0)))

    g_in, g_cw = _run_plan(_ag_plan([s_in, s_cw]), "ag_w_in")
    w_int = _rows(g_in)
    conv_w_full = jnp.transpose(g_cw[:, :4, :], (1, 0, 2)).reshape(4, d_rnn)

    wa_bd, wx_bd = lru_w_a[0], lru_w_x[0]
    pw = pool_w[0]
    pw_bf = pw.astype(BF16)

    pool_block = (2 * d_rnn) // d_pool
    ga_block = (2 * d_rnn + d_pool) // 512
    gb_block = ga_block + d // 512
    g_block = d_rnn // 512

    r_f1, r_f2 = s_f1.shape[0], s_f2.shape[0]
    f1_cut = r_f1 // 4
    f2_cut = (3 * r_f2) // 8
    plan = _join([_ag_plan([s_lu, s_pu]), _ag_plan([s_f1], pieces=[(0, f1_cut)])])
    (proj, h1), got = _norm_proj(x2d, norm_mix_pre, w_int, carry=plan)
    (g_lu, g_pu), (g_f1,) = plan.split(got)
    plan = _join([_ag_plan([s_f1], pieces=[(f1_cut, r_f1 - f1_cut)], bufs=[g_f1]), _ag_plan([s_o])])
    (y_lru, h, lru_saved), got = _lru_fwd(
        proj, conv_w_full, conv_b, wa_bd, lru_b_a, wx_bd, lru_b_x, lru_lambda, carry=plan)
    (g_f1,), (g_o,) = plan.split(got)
    w_lu, w_og = _rows(g_lu), _rows(g_o)
    y_pool, p = _pool_fwd(proj, pw_bf, pool_scale, pool_block)
    (br_a, br_b, mix), (g_f2,) = _branch_mix(
        y_lru, y_pool, w_lu, g_pu, proj, b_gate, ga_block, gb_block,
        carry=_ag_plan([s_f2], pieces=[(0, f2_cut)]))
    (m, x2, h3), _ = _wo_norm(mix, w_og, x2d, norm_mix_post, norm_mlp_pre)
    (rf,), (g_f2,) = _ff1(
        h3, g_f1, carry=_ag_plan([s_f2], pieces=[(f2_cut, r_f2 - f2_cut)], bufs=[g_f2]))
    w_f2 = _rows(g_f2)
    dy, df, dg4, loss_part = _ff2_loss(rf, w_f2, x2, norm_mlp_post, tgt)

    (gw_ff2_32, gw_ff2_16), _ = _wgrad(rf, df, "wgrad_ff2", square_a=True)
    (d_f1,), r1_ff2 = _ff2_bwd(df, w_f2, rf, carry=_rs_sibling_plan([gw_ff2_16]))
    ((own_ff2, send_ff2),) = _rs_sums([gw_ff2_32], r1_ff2, "ff2")
    cut2 = (5 * send_ff2.shape[1]) // 16
    (gw_ff1_32, gw_ff1_16), (r2_ff2,) = _wgrad_cols(
        h3, d_f1, s_f1.shape[1], s_f1.shape[1], "wgrad_ff1",
        carry=_rs_chips_plan([send_ff2], pieces=[(0, cut2)]))
    plan = _join([_rs_chips_plan([send_ff2], pieces=[(cut2, send_ff2.shape[1] - cut2)], bufs=[r2_ff2]),
                  _rs_sibling_plan([gw_ff1_16])])
    (dx2, dm, dg3, dg2), got = _ff1_bwd_norms(d_f1, g_f1, dy, x2, norm_mlp_pre, m, norm_mix_post, carry=plan)
    (r2_ff2,), r1_ff1 = plan.split(got)
    ((own_ff1, send_ff1),) = _rs_sums([gw_ff1_32], r1_ff1, "ff1")
    own_ff1, send_ff1 = own_ff1.reshape((1,) + s_f1.shape), send_ff1.reshape((3,) + s_f1.shape)
    cut = send_ff1.shape[1] // 4
    (gw_o_32, gw_o_16), _ = _wgrad(mix, dm, "wgrad_o")
    (d_br_a, d_br_b, p_ga, p_gb, dbg_a, dbg_b), (r2_ff1,) = _wo_bwd_mix(
        dm, w_og, br_a, br_b, proj, b_gate, ga_block, gb_block,
        carry=_rs_chips_plan([send_ff1], pieces=[(0, cut)]))
    (gw_lu_32, gw_lu_16), _ = _wgrad(y_lru, d_br_a, "wgrad_lru_up")
    (gw_pu_32, gw_pu_16), _ = _wgrad_cols(y_pool, d_br_b, s_pu.shape[1], d, "wgrad_pool_up")
    (dh, p_g), r1_mid = _lru_up_bwd(
        d_br_a, w_lu, proj, h, g_block,
        carry=_rs_sibling_plan([gw_o_16, gw_lu_16, gw_pu_16]))
    mid = _rs_sums([gw_o_32, gw_lu_32, gw_pu_32], r1_mid, "mid")
    plan = _join([_rs_chips_plan([send_ff1], pieces=[(cut, send_ff1.shape[1] - cut)], bufs=[r2_ff1]),
                  _rs_chips_plan([mid[0][1]])])
    (p_x, dwa, db_a, dwx, db_x, dlam, dconv_w, dconv_b), got = _lru_bwd(
        dh, h, lru_saved, proj, conv_w_full, wa_bd, wx_bd, lru_lambda, carry=plan)
    (r2_ff1,), (r2_o,) = plan.split(got)
    p_p, dpool_w, dpool_scale = _pool_bwd(d_br_b, g_pu, p, pw, pool_scale)
    parts = [p_x, p_g, p_p, p_ga, p_gb]
    gw_in, (r2_lu, r2_pu) = _wgrad_parts(
        parts, h1, "wgrad_in", carry=_rs_chips_plan([mid[1][1], mid[2][1]]))
    r2_mid = [r2_o, r2_lu, r2_pu]
    tail = _rs_level1([gw_in[0], dpool_w.reshape(N_DEV, -1, POOL_GROUP_DIM), dwa, dwx],
                      [gw_in[1], dpool_w.reshape(N_DEV, -1, POOL_GROUP_DIM), dwa, dwx], "in")
    (grad_x, dg1), r2_tail = _win_bwd_norm(parts, w_int, dx2, x2d, norm_mix_pre,
                                           carry=_rs_chips_plan([s for _, s in tail]))

    def flat2(a):
        return a.reshape(a.shape[0], -1, a.shape[-1])

    fin_small, _ = _finals([
        (flat2(tail[1][0]), flat2(r2_tail[1])), (flat2(tail[2][0]), flat2(r2_tail[2])),
        (flat2(tail[3][0]), flat2(r2_tail[3])),
    ], "rs_finals_small")

    def pad_row(a):
        return jnp.pad(a, ((0, 0), (0, d - a.shape[1])))

    vecs = jnp.concatenate([dg1, dg2, dg3, dg4, dbg_a, dbg_b, dconv_b, db_a, db_x, dlam,
                            pad_row(dpool_scale), dconv_w, pad_row(loss_part)], axis=0)
    assert vecs.shape[0] == N_VEC_ROWS
    vec_parts, g_pool, g_wa, g_wx = _run_plan(_ag_plan([vecs] + fin_small), "ag_tail")

    big_names = ["w_in", "w_lru_up", "w_pool_up", "w_o", "w_ff1", "w_ff2"]
    big_w = [w_in[0].T, w_lru_up[0], w_pool_up[0], w_o[0], w_ff1[0], w_ff2[0]]
    big_g = [(tail[0][0], r2_tail[0]), (mid[1][0], r2_mid[1]),
             (mid[2][0].reshape((1,) + s_pu.shape), r2_mid[2].reshape((3,) + s_pu.shape)),
             (mid[0][0], r2_mid[0]), (own_ff1, r2_ff1), (own_ff2, r2_ff2)]
    big_m = [m_w_in[0].T, m_w_lru_up[0], m_w_pool_up[0], m_w_o[0], m_w_ff1[0], m_w_ff2[0]]
    big_v = [v_w_in[0].T, v_w_lru_up[0], v_w_pool_up[0], v_w_o[0], v_w_ff1[0], v_w_ff2[0]]
    big_out = _adamw_big(big_w, big_g, big_m, big_v)
    big_out[0] = tuple(o.T for o in big_out[0])

    small = dict(
        norm_mix_pre=(norm_mix_pre, m_norm_mix_pre, v_norm_mix_pre),
        norm_mix_post=(norm_mix_post, m_norm_mix_post, v_norm_mix_post),
        norm_mlp_pre=(norm_mlp_pre, m_norm_mlp_pre, v_norm_mlp_pre),
        norm_mlp_post=(norm_mlp_post, m_norm_mlp_post, v_norm_mlp_post),
        b_gate=(b_gate, m_b_gate, v_b_gate), conv_w=(conv_w, m_conv_w, v_conv_w),
        conv_b=(conv_b, m_conv_b, v_conv_b), lru_w_a=(lru_w_a, m_lru_w_a, v_lru_w_a),
        lru_b_a=(lru_b_a, m_lru_b_a, v_lru_b_a), lru_w_x=(lru_w_x, m_lru_w_x, v_lru_w_x),
        lru_b_x=(lru_b_x, m_lru_b_x, v_lru_b_x), lru_lambda=(lru_lambda, m_lru_lambda, v_lru_lambda),
        pool_w=(pool_w, m_pool_w, v_pool_w), pool_scale=(pool_scale, m_pool_scale, v_pool_scale))
    loss_row, small_out = _adamw_small(
        vec_parts, g_pool.reshape(pool_w.shape), g_wa.reshape(lru_w_a.shape), g_wx.reshape(lru_w_x.shape),
        jnp.reshape(me, (1,)).astype(jnp.int32), small)
    grads = {n: o[0] for n, o in small_out.items()}
    delta = {n: o[1] for n, o in small_out.items()}
    new_m = {n: o[2] for n, o in small_out.items()}
    new_v = {n: o[3] for n, o in small_out.items()}

    for name, (g, dl, nm, nv) in zip(big_names, big_out):
        grads[name], delta[name], new_m[name], new_v[name] = g[None], dl[None], nm[None], nv[None]

    loss = loss_row[0, 0]
    order = ["norm_mix_pre", "norm_mix_post", "norm_mlp_pre", "norm_mlp_post", "w_in", "b_gate", "conv_w",
             "conv_b", "lru_w_a", "lru_b_a", "lru_w_x", "lru_b_x", "lru_lambda", "pool_w", "pool_scale",
             "w_lru_up", "w_pool_up", "w_o", "w_ff1", "w_ff2"]
    return (loss, grad_x[None], *[grads[n] for n in order], *[delta[n] for n in order],
            *[new_m[n] for n in order], *[new_v[n] for n in order])
```

```python
import functools
import math
import operator
import types

import jax
import jax.numpy as jnp
from jax import lax
from jax.experimental import pallas as pl
from jax.experimental.pallas import tpu as pltpu

F32 = jnp.float32
BF16 = jnp.bfloat16
NORM_EPS = 1e-6
LRU_C = 8.0
N_LRU_HEADS = 16
LRU_HEAD_DIM = 64
POOL_WINDOWS = (2, 4, 8, 16)
POOL_GROUP_DIM = 128
ADAM_LR = 0.001
ADAM_B1 = 0.9
ADAM_B2 = 0.999
ADAM_EPS = 1e-08
ADAM_WD = 0.01
ADAM_STEP = 10
N_DEV = 8
V7X_VMEM_LIMIT_BYTES = 56 * 1024 * 1024
LRU_CB = 256
MESH = pl.DeviceIdType.MESH
ANY = pl.BlockSpec(memory_space=pl.ANY)


def _tile(n, pref):
    t = min(n, pref)
    assert n % t == 0, (n, pref)
    return t


def _dot_nn(a, b):
    return lax.dot_general(a, b, (((1,), (0,)), ((), ())), preferred_element_type=F32)


def _dot_nt(a, b):
    return lax.dot_general(a, b, (((1,), (1,)), ((), ())), preferred_element_type=F32)


def _dot_tn(a, b):
    return lax.dot_general(a, b, (((0,), (0,)), ((), ())), preferred_element_type=F32)


def _row_chunks(n_rows, fn, chunk=256):
    chunk = min(chunk, n_rows)
    assert n_rows % chunk == 0

    def step(r, carry):
        fn(pl.ds(pl.multiple_of(r * chunk, chunk), chunk))
        return carry

    lax.fori_loop(0, n_rows // chunk, step, 0)


def _late_copies(i, tt, pairs, sems):
    rows = pl.ds(pl.multiple_of(i * tt, tt), tt)
    return [pltpu.make_async_copy(hbm.at[rows], buf, sems.at[j]) for j, (hbm, buf) in enumerate(pairs)]


def _sig(x):
    return 1.0 / (1.0 + jnp.exp(-x))


def _rms_hat(x):
    r = lax.rsqrt(jnp.mean(x * x, axis=-1, keepdims=True) + NORM_EPS)
    return x * r, r


def _rms_bwd(dn, xhat, r, g):
    q = dn * g
    dx = r * (q - xhat * jnp.mean(q * xhat, axis=-1, keepdims=True))
    dg = jnp.sum(dn * xhat, axis=0, keepdims=True)
    return dx, dg


_GELU_K = math.sqrt(2.0 / math.pi)
_GELU_C = 0.044715


def _gelu_and_grad(g):
    t = jnp.tanh(_GELU_K * (g + _GELU_C * g * g * g))
    val = 0.5 * g * (1.0 + t)
    grad = 0.5 * (1.0 + t) + 0.5 * g * (1.0 - t * t) * (_GELU_K * (1.0 + 3.0 * _GELU_C * g * g))
    return val, grad


def _softplus_neg(lam):
    z = -lam
    e = jnp.exp(-jnp.abs(z))
    u = 1.0 + e
    d = u - 1.0
    l1p = jnp.where(d == 0.0, e, jnp.log(u) * (e / jnp.where(d == 0.0, 1.0, d)))
    return jnp.maximum(z, 0.0) + l1p


def _lru_gates(xc, wa, ba, wx, bx, lam):
    xcb = xc.astype(BF16)
    r = _sig(_dot_nn(xcb, wa) + ba)
    i = _sig(_dot_nn(xcb, wx) + bx)
    sp = _softplus_neg(lam)
    log_a = (-LRU_C) * r * sp
    a = jnp.exp(log_a)
    mult = jnp.sqrt(-jnp.tanh(log_a) * (1.0 + a * a))
    return xcb, r, i, sp, log_a, a, mult


def _place():
    return lax.axis_index("x"), lax.axis_index("y"), lax.axis_index("c")


def _ag_plan(shards, pieces=None, bufs=None):
    na = len(shards)
    n_kinds = 7

    def parts(ins, outs, sems):
        send_sems, recv_sems, local_sems = sems
        x, y, c = _place()
        me, sibling = (x, y, c), (x, y, 1 - c)
        x_nb, y_nb, diag = (1 - x, y), (x, 1 - y), (1 - x, 1 - y)
        relay_src = (c * (1 - x) + (1 - c) * x, c * y + (1 - c) * (1 - y))
        relay_dst = (c * x + (1 - c) * (1 - x), c * (1 - y) + (1 - c) * y)

        def own(a):
            return ins[a] if pieces is None else ins[a].at[pl.ds(*pieces[a])]

        def slot(a, px, py, pc):
            idx = 4 * px + 2 * py + pc
            return outs[a].at[idx] if pieces is None else outs[a].at[idx, pl.ds(*pieces[a])]

        def copy(a, k, block, to, src=None):
            return pltpu.make_async_remote_copy(
                src_ref=slot(a, *block) if src is None else src,
                dst_ref=slot(a, *block),
                send_sem=send_sems.at[a * n_kinds + k],
                recv_sem=recv_sems.at[a * n_kinds + k],
                device_id=to,
                device_id_type=MESH,
            )

        mine = [pltpu.make_async_copy(own(a), slot(a, *me), local_sems.at[a]) for a in range(na)]
        first, second, third = [], [], []
        for a in range(na):
            first += [copy(a, 0, me, sibling, src=own(a)), copy(a, 1, me, (*x_nb, c), src=own(a)),
                      copy(a, 2, me, (*y_nb, c), src=own(a))]
            second += [copy(a, 3, (*relay_src, c), (*relay_dst, c)), copy(a, 4, (*x_nb, c), sibling),
                       copy(a, 5, (*y_nb, c), sibling)]
            third.append(copy(a, 6, (*diag, c), sibling))
        return sibling, c, x_nb, y_nb, diag, copy, mine, first, second, third

    def start(ins, outs, sems):
        _, _, _, _, _, _, mine, first, _, _ = parts(ins, outs, sems)
        for cp in mine + first:
            cp.start()

    def middle(ins, outs, sems):
        _, c, x_nb, y_nb, _, copy, _, _, second, _ = parts(ins, outs, sems)
        for a in range(na):
            copy(a, 1, (*x_nb, c), (*x_nb, c)).wait_recv()
            copy(a, 2, (*y_nb, c), (*y_nb, c)).wait_recv()
        for cp in second:
            cp.start()

    def finish(ins, outs, sems):
        sibling, c, x_nb, y_nb, diag, copy, mine, first, second, third = parts(ins, outs, sems)
        for a in range(na):
            copy(a, 3, (*diag, c), (*diag, c)).wait_recv()
            third[a].start()
        for a in range(na):
            copy(a, 0, sibling, sibling).wait_recv()
            copy(a, 4, (*x_nb, 1 - c), sibling).wait_recv()
            copy(a, 5, (*y_nb, 1 - c), sibling).wait_recv()
            copy(a, 6, (*diag, 1 - c), sibling).wait_recv()
        for cp in first + second + third:
            cp.wait_send()
        for cp in mine:
            cp.wait()

    return types.SimpleNamespace(
        ins=list(shards) + list(bufs or []),
        out_shapes=[jax.ShapeDtypeStruct((N_DEV,) + s.shape, s.dtype) for s in shards],
        sems=[pltpu.SemaphoreType.DMA((n_kinds * na,)), pltpu.SemaphoreType.DMA((n_kinds * na,)),
              pltpu.SemaphoreType.DMA((na,))],
        aliases=[(na + a, a) for a in range(na)] if bufs else [],
        peers=frozenset({"sibling", "neighbours"}), start=start, middle=middle, finish=finish)


def _rs_sibling_plan(fulls):
    na = len(fulls)
    rs = [f.shape[0] // N_DEV for f in fulls]

    def copies(ins, outs, sems):
        send_sems, recv_sems = sems
        x, y, c = _place()
        out = []
        for a in range(na):
            for q in range(4):
                shard = 2 * q + (1 - c)
                out.append(pltpu.make_async_remote_copy(
                    src_ref=ins[a].at[pl.ds(shard * rs[a], rs[a])],
                    dst_ref=outs[a].at[q],
                    send_sem=send_sems.at[a * 4 + q],
                    recv_sem=recv_sems.at[a * 4 + q],
                    device_id=(x, y, 1 - c),
                    device_id_type=MESH,
                ))
        return out

    def start(ins, outs, sems):
        for cp in copies(ins, outs, sems):
            cp.start()

    def finish(ins, outs, sems):
        for cp in copies(ins, outs, sems):
            cp.wait()

    return types.SimpleNamespace(
        ins=list(fulls),
        out_shapes=[jax.ShapeDtypeStruct((4, r) + f.shape[1:], f.dtype) for r, f in zip(rs, fulls)],
        sems=[pltpu.SemaphoreType.DMA((4 * na,)), pltpu.SemaphoreType.DMA((4 * na,))],
        peers=frozenset({"sibling"}), start=start, finish=finish)


def _rs_chips_plan(sends, pieces=None, bufs=None):
    na = len(sends)

    def copies(ins, outs, sems):
        send_sems, recv_sems = sems
        x, y, c = _place()
        chips = [(1 - x, y), (x, 1 - y), (1 - x, 1 - y)]
        out = []
        for a in range(na):
            for k, chip in enumerate(chips):
                rows = (k,) if pieces is None else (k, pl.ds(*pieces[a]))
                out.append(pltpu.make_async_remote_copy(
                    src_ref=ins[a].at[rows],
                    dst_ref=outs[a].at[rows],
                    send_sem=send_sems.at[a * 3 + k],
                    recv_sem=recv_sems.at[a * 3 + k],
                    device_id=(*chip, c),
                    device_id_type=MESH,
                ))
        return out

    def start(ins, outs, sems):
        for cp in copies(ins, outs, sems):
            cp.start()

    def finish(ins, outs, sems):
        for cp in copies(ins, outs, sems):
            cp.wait()

    return types.SimpleNamespace(
        ins=list(sends) + list(bufs or []),
        out_shapes=[jax.ShapeDtypeStruct(s.shape, s.dtype) for s in sends],
        sems=[pltpu.SemaphoreType.DMA((3 * na,)), pltpu.SemaphoreType.DMA((3 * na,))],
        aliases=[(na + a, a) for a in range(na)] if bufs else [],
        peers=frozenset({"chips"}), start=start, finish=finish)


def _join(plans):
    ins, outs, sems, aliases, offs = [], [], [], [], []
    for p in plans:
        offs.append((len(ins), len(outs), len(sems)))
        aliases += [(len(ins) + ci, len(outs) + co) for ci, co in getattr(p, "aliases", [])]
        ins += p.ins
        outs += p.out_shapes
        sems += p.sems

    def cut(p, off, i, o, s):
        return (i[off[0]:off[0] + len(p.ins)], o[off[1]:off[1] + len(p.out_shapes)],
                s[off[2]:off[2] + len(p.sems)])

    def start(i, o, s):
        for p, off in zip(plans, offs):
            p.start(*cut(p, off, i, o, s))

    def middle(i, o, s):
        for p, off in zip(plans, offs):
            if getattr(p, "middle", None) is not None:
                p.middle(*cut(p, off, i, o, s))

    def finish(i, o, s):
        for p, off in zip(plans, offs):
            p.finish(*cut(p, off, i, o, s))

    def split(results):
        return [list(results[off[1]:off[1] + len(p.out_shapes)]) for p, off in zip(plans, offs)]

    return types.SimpleNamespace(ins=ins, out_shapes=outs, sems=sems, aliases=aliases,
                                 peers=frozenset().union(*[p.peers for p in plans]),
                                 start=start, middle=middle, finish=finish, split=split)


COLLECTIVE_ID = {frozenset({"sibling"}): 0, frozenset({"chips"}): 1, frozenset({"sibling", "chips"}): 2,
                 frozenset({"sibling", "neighbours"}): 3}


def _handshake(peers):
    x, y, c = _place()
    devs = []
    if "sibling" in peers:
        devs.append((x, y, 1 - c))
    if "neighbours" in peers:
        devs += [(1 - x, y, c), (x, 1 - y, c)]
    if "chips" in peers:
        assert "neighbours" not in peers
        devs += [(1 - x, y, c), (x, 1 - y, c), (1 - x, 1 - y, c)]
    barrier = pltpu.get_barrier_semaphore()
    for dev in devs:
        pl.semaphore_signal(barrier, inc=1, device_id=dev, device_id_type=MESH)
    pl.semaphore_wait(barrier, len(devs))


def _in_hbm(args):
    return [pltpu.with_memory_space_constraint(a, pltpu.HBM) for a in args]


def _run_plan(plan, name):
    n_in, n_out = len(plan.ins), len(plan.out_shapes)

    def body(*refs):
        ins, outs, sems = refs[:n_in], refs[n_in:n_in + n_out], refs[n_in + n_out:]
        _handshake(plan.peers)
        plan.start(ins, outs, sems)
        if getattr(plan, "middle", None) is not None:
            plan.middle(ins, outs, sems)
        plan.finish(ins, outs, sems)

    return pl.pallas_call(
        body,
        name=name,
        in_specs=[ANY] * n_in,
        out_specs=[ANY] * n_out,
        out_shape=plan.out_shapes,
        scratch_shapes=plan.sems,
        input_output_aliases=dict(getattr(plan, "aliases", [])),
        compiler_params=pltpu.CompilerParams(collective_id=COLLECTIVE_ID[plan.peers]),
    )(*_in_hbm(plan.ins))


def _call(body, *, name, grid, in_specs, out_specs, out_shape, args, scratch_shapes=(), aliases=None,
          carry=None):
    n_in, n_out, n_scr = len(in_specs), len(out_shape), len(scratch_shapes)
    params = pltpu.CompilerParams(
        dimension_semantics=("arbitrary",) * len(grid), vmem_limit_bytes=V7X_VMEM_LIMIT_BYTES)
    if carry is None:
        outs = pl.pallas_call(
            body, name=name, grid=grid, in_specs=list(in_specs), out_specs=list(out_specs),
            out_shape=list(out_shape), scratch_shapes=list(scratch_shapes),
            input_output_aliases=aliases or {}, compiler_params=params)(*_in_hbm(args))
        return list(outs), []
    c_in, c_out = len(carry.ins), len(carry.out_shapes)

    def full(*refs):
        p = 0
        ins = refs[p:p + n_in]
        p += n_in
        cins = refs[p:p + c_in]
        p += c_in
        outs = refs[p:p + n_out]
        p += n_out
        couts = refs[p:p + c_out]
        p += c_out
        scr = refs[p:p + n_scr]
        csems = refs[p + n_scr:]
        ids = [pl.program_id(a) for a in range(len(grid))]
        first = functools.reduce(operator.and_, [i == 0 for i in ids])
        last = functools.reduce(operator.and_, [i == g - 1 for i, g in zip(ids, grid)])

        @pl.when(first)
        def _():
            _handshake(carry.peers)
            carry.start(cins, couts, csems)

        if getattr(carry, "middle", None) is not None:
            n_steps = math.prod(grid)
            flat = functools.reduce(lambda acc, ig: acc * ig[1] + ig[0], zip(ids, grid), 0)

            @pl.when(flat == (2 * n_steps) // 3)
            def _():
                carry.middle(cins, couts, csems)

        body(*ins, *outs, *scr)

        @pl.when(last)
        def _():
            carry.finish(cins, couts, csems)

    all_aliases = dict(aliases or {})
    all_aliases.update({n_in + ci: n_out + co for ci, co in getattr(carry, "aliases", [])})
    params = pltpu.CompilerParams(
        dimension_semantics=("arbitrary",) * len(grid), vmem_limit_bytes=V7X_VMEM_LIMIT_BYTES,
        collective_id=COLLECTIVE_ID[carry.peers])
    outs = pl.pallas_call(
        full, name=name, grid=grid,
        in_specs=list(in_specs) + [ANY] * c_in,
        out_specs=list(out_specs) + [ANY] * c_out,
        out_shape=list(out_shape) + list(carry.out_shapes),
        scratch_shapes=list(scratch_shapes) + list(carry.sems),
        input_output_aliases=all_aliases, compiler_params=params)(*_in_hbm(args), *_in_hbm(carry.ins))
    return list(outs[:n_out]), list(outs[n_out:])


def _norm_proj(x, g1, w_int, carry=None):
    t, d = x.shape
    n = w_int.shape[0]
    tt, tn = _tile(t, 2048), _tile(n, 512)

    def body(x_ref, g_ref, w_ref, proj_ref, h1_ref, h1_s):
        @pl.when(pl.program_id(1) == 0)
        def _():
            def norm_rows(rows):
                xhat, _ = _rms_hat(x_ref[rows, :])
                h = (xhat * g_ref[...]).astype(BF16)
                h1_s[rows, :] = h
                h1_ref[rows, :] = h

            _row_chunks(tt, norm_rows)

        proj_ref[...] = _dot_nt(h1_s[...], w_ref[...]).astype(BF16)

    return _call(
        body, name="norm_proj", grid=(t // tt, n // tn),
        in_specs=[
            pl.BlockSpec((tt, d), lambda i, j: (i, 0)),
            pl.BlockSpec((1, d), lambda i, j: (0, 0)),
            pl.BlockSpec((tn, d), lambda i, j: (j, 0)),
        ],
        out_specs=[
            pl.BlockSpec((tt, tn), lambda i, j: (i, j)),
            pl.BlockSpec((tt, d), lambda i, j: (i, 0)),
        ],
        out_shape=[jax.ShapeDtypeStruct((t, n), BF16), jax.ShapeDtypeStruct((t, d), BF16)],
        scratch_shapes=[pltpu.VMEM((tt, d), BF16)],
        args=(x, g1, w_int), carry=carry)


def _scan_rows(av, bv, reverse):
    tc = av.shape[0]
    row = lax.broadcasted_iota(jnp.int32, av.shape, 0)
    s = 1
    while s < tc:
        if s < 8:
            keep = (row < tc - s) if reverse else (row >= s)
            shift = (tc - s) if reverse else s
            a_sh = jnp.where(keep, pltpu.roll(av, shift, 0), 1.0)
            b_sh = jnp.where(keep, pltpu.roll(bv, shift, 0), 0.0)
            bv = av * b_sh + bv
            av = av * a_sh
        elif reverse:
            bv = jnp.concatenate([av[:tc - s] * bv[s:] + bv[:tc - s], bv[tc - s:]], axis=0)
            av = jnp.concatenate([av[:tc - s] * av[s:], av[tc - s:]], axis=0)
        else:
            bv = jnp.concatenate([bv[:s], av[s:] * bv[:tc - s] + bv[s:]], axis=0)
            av = jnp.concatenate([av[:s], av[s:] * av[:tc - s]], axis=0)
        s *= 2
    return av, bv


N_LRU_SAVED = 5


def _fill_block_diag(w_ref, bd_ref):
    bd_ref[...] = jnp.zeros_like(bd_ref)
    hd = LRU_HEAD_DIM
    for k in range(w_ref.shape[0]):
        bd_ref[k * hd:(k + 1) * hd, k * hd:(k + 1) * hd] = w_ref[k].astype(BF16)


def _lru_fwd(proj, conv_w, conv_b, w_a, b_a, w_x, b_x, lam, carry=None):
    t = proj.shape[0]
    dr = conv_b.shape[1]
    cb = LRU_CB
    tc = _tile(t, 256)
    ncb, ntc = dr // cb, t // tc

    def body(xp_ref, g_ref, cw_ref, cb_ref, wa_ref, ba_ref, wx_ref, bx_ref, lam_ref,
             y_ref, h_ref, saved_ref, prevx_s, hlast_s, wa_s, wx_s):
        c = pl.program_id(1)

        @pl.when(c == 0)
        def _():
            prevx_s[...] = jnp.zeros_like(prevx_s)
            hlast_s[...] = jnp.zeros_like(hlast_s)
            _fill_block_diag(wa_ref, wa_s)
            _fill_block_diag(wx_ref, wx_s)

        x = xp_ref[...].astype(F32)
        prev = prevx_s[...]
        row = lax.broadcasted_iota(jnp.int32, x.shape, 0)

        def sh(j):
            return jnp.where(row >= j, pltpu.roll(x, j, 0), pltpu.roll(prev, j, 0))

        xc = (cb_ref[...] + cw_ref[0:1, :] * sh(3) + cw_ref[1:2, :] * sh(2)
              + cw_ref[2:3, :] * sh(1) + cw_ref[3:4, :] * x)
        prevx_s[...] = x
        _, r, i, _, _, a, mult = _lru_gates(xc, wa_s[...], ba_ref[...], wx_s[...], bx_ref[...],
                                            lam_ref[...])
        for k, val in enumerate((xc, r, i, a, mult)):
            saved_ref[:, k * cb:(k + 1) * cb] = val
        av, bv = _scan_rows(a, mult * (i * xc), reverse=False)
        h = av * hlast_s[...] + bv
        h_ref[...] = h
        hlast_s[...] = h_ref[tc - 1:tc, :]
        gel, _ = _gelu_and_grad(g_ref[...].astype(F32))
        y_ref[...] = (h * gel).astype(BF16)

    vec = pl.BlockSpec((1, cb), lambda j, c: (0, j))
    blk = pl.BlockSpec((tc, cb), lambda j, c: (c, j))
    mat = pl.BlockSpec((cb // LRU_HEAD_DIM, LRU_HEAD_DIM, LRU_HEAD_DIM), lambda j, c: (j, 0, 0))
    return _call(
        body, name="lru_fwd", grid=(ncb, ntc),
        in_specs=[
            blk,
            pl.BlockSpec((tc, cb), lambda j, c: (c, ncb + j)),
            pl.BlockSpec((4, cb), lambda j, c: (0, j)),
            vec, mat, vec, mat, vec, vec,
        ],
        out_specs=[blk, blk, pl.BlockSpec((tc, N_LRU_SAVED * cb), lambda j, c: (c, j))],
        out_shape=[jax.ShapeDtypeStruct((t, dr), BF16), jax.ShapeDtypeStruct((t, dr), F32),
                   jax.ShapeDtypeStruct((t, N_LRU_SAVED * dr), F32)],
        scratch_shapes=[pltpu.VMEM((tc, cb), F32), pltpu.VMEM((1, cb), F32),
                        pltpu.VMEM((cb, cb), BF16), pltpu.VMEM((cb, cb), BF16)],
        args=(proj, proj, conv_w, conv_b, w_a, b_a, w_x, b_x, lam), carry=carry)


def _pool_select(col, vals):
    out = vals[3]
    for g in (2, 1, 0):
        out = jnp.where(col < (g + 1) * POOL_GROUP_DIM, vals[g], out)
    return out


def _pool_fwd(proj, pool_w, pool_scale, col_block):
    t = proj.shape[0]
    dp = pool_scale.shape[1]
    tc = _tile(t, 256)
    ntc = t // tc

    def body(x_ref, w_ref, sc_ref, y_ref, p_ref, px, p2, p4, p8):
        c = pl.program_id(0)

        @pl.when(c == 0)
        def _():
            for s in (px, p2, p4, p8):
                s[...] = jnp.zeros_like(s)

        x = x_ref[...].astype(F32)
        row = lax.broadcasted_iota(jnp.int32, x.shape, 0)
        col = lax.broadcasted_iota(jnp.int32, x.shape, 1)

        def sh(v, pv, j):
            return jnp.where(row >= j, pltpu.roll(v, j, 0), pltpu.roll(pv[...], j, 0))

        s2 = x + sh(x, px, 1)
        s4 = s2 + sh(s2, p2, 2)
        s8 = s4 + sh(s4, p4, 4)
        s16 = s8 + sh(s8, p8, 8)
        px[...] = x
        p2[...] = s2
        p4[...] = s4
        p8[...] = s8
        wsum = _pool_select(col, (s2, s4, s8, s16))
        win = _pool_select(col, POOL_WINDOWS)
        cnt = jnp.minimum(c * tc + row + 1, win).astype(F32)
        p = wsum / cnt - x
        pb = p.astype(BF16)
        p_ref[...] = pb
        for g in range(len(POOL_WINDOWS)):
            sl = slice(g * POOL_GROUP_DIM, (g + 1) * POOL_GROUP_DIM)
            yg = _dot_nn(pb[:, sl], w_ref[g]) * sc_ref[:, sl]
            y_ref[:, sl] = yg.astype(BF16)

    return _call(
        body, name="pool_fwd", grid=(ntc,),
        in_specs=[
            pl.BlockSpec((tc, dp), lambda c: (c, col_block)),
            pl.BlockSpec(pool_w.shape, lambda c: (0, 0, 0)),
            pl.BlockSpec((1, dp), lambda c: (0, 0)),
        ],
        out_specs=[pl.BlockSpec((tc, dp), lambda c: (c, 0))] * 2,
        out_shape=[jax.ShapeDtypeStruct((t, dp), BF16)] * 2,
        scratch_shapes=[pltpu.VMEM((tc, dp), F32)] * 4,
        args=(proj, pool_w, pool_scale))[0]


def _branch_mix(y_lru, y_pool, w_lru_up, w_pool_upb, proj, b_gate, ga_block, gb_block, carry=None):
    t, d = y_lru.shape
    dp = y_pool.shape[1]
    bw = w_pool_upb.shape[2]
    tt, tn = _tile(t, 1024), 512
    nj = d // tn

    def body(yl_ref, yp_ref, wl_ref, wp_ref, ga_ref, gb_ref, ba_ref, bb_ref, bra_ref, brb_ref, mix_ref):
        br_a = _dot_nn(yl_ref[...], wl_ref[...])
        wp = jnp.concatenate([wp_ref[b] for b in range(tn // bw)], axis=1)
        br_b = _dot_nn(yp_ref[...], wp)
        bra_ref[...] = br_a.astype(BF16)
        brb_ref[...] = br_b.astype(BF16)
        ga = _sig(ga_ref[...].astype(F32) + ba_ref[...])
        gb = _sig(gb_ref[...].astype(F32) + bb_ref[...])
        mix_ref[...] = (ga * br_a + gb * br_b).astype(BF16)

    out = pl.BlockSpec((tt, tn), lambda j, i: (i, j))
    return _call(
        body, name="branch_mix", grid=(nj, t // tt),
        in_specs=[
            pl.BlockSpec((tt, d), lambda j, i: (i, 0)),
            pl.BlockSpec((tt, dp), lambda j, i: (i, 0)),
            pl.BlockSpec((d, tn), lambda j, i: (0, j)),
            pl.BlockSpec((tn // bw, dp, bw), lambda j, i: (j, 0, 0)),
            pl.BlockSpec((tt, tn), lambda j, i: (i, ga_block + j)),
            pl.BlockSpec((tt, tn), lambda j, i: (i, gb_block + j)),
            pl.BlockSpec((1, tn), lambda j, i: (0, j)),
            pl.BlockSpec((1, tn), lambda j, i: (0, nj + j)),
        ],
        out_specs=[out, out, out],
        out_shape=[jax.ShapeDtypeStruct((t, d), BF16)] * 3,
        args=(y_lru, y_pool, w_lru_up, w_pool_upb, proj, proj, b_gate, b_gate), carry=carry)


def _wo_norm(mix, w_o, x, g2, g3, carry=None):
    t, d = x.shape
    tt = _tile(t, 512)

    def body(mix_ref, w_ref, x_ref, g2_ref, g3_ref, m_ref, x2_ref, h3_ref):
        m = _dot_nn(mix_ref[...], w_ref[...])
        m_ref[...] = m
        mhat, _ = _rms_hat(m)
        x2 = x_ref[...] + mhat * g2_ref[...]
        x2_ref[...] = x2
        xhat, _ = _rms_hat(x2)
        h3_ref[...] = (xhat * g3_ref[...]).astype(BF16)

    row = pl.BlockSpec((tt, d), lambda i: (i, 0))
    vec = pl.BlockSpec((1, d), lambda i: (0, 0))
    return _call(
        body, name="wo_norm", grid=(t // tt,),
        in_specs=[row, pl.BlockSpec((d, d), lambda i: (0, 0)), row, vec, vec],
        out_specs=[row, row, row],
        out_shape=[
            jax.ShapeDtypeStruct((t, d), F32),
            jax.ShapeDtypeStruct((t, d), F32),
            jax.ShapeDtypeStruct((t, d), BF16),
        ],
        args=(mix, w_o, x, g2, g3), carry=carry)


def _ff1(h3, w_ff1b, carry):
    t, d = h3.shape
    nb, _, tn = w_ff1b.shape
    n_first = (2 * nb) // 3
    c_in, c_out = len(carry.ins), len(carry.out_shapes)

    def body(h_hbm, w_hbm, *refs):
        cins, rf_hbm = refs[:c_in], refs[c_in]
        couts, csems = refs[c_in + 1:c_in + 1 + c_out], refs[c_in + 1 + c_out:]

        def blocks(first, count):
            def step(h_ref, w_ref, rf_ref):
                rf_ref[...] = jnp.maximum(_dot_nn(h_ref[...], w_ref[0]), 0.0).astype(BF16)

            pltpu.emit_pipeline(
                step, grid=(count,),
                in_specs=[pl.BlockSpec((t, d), lambda j: (0, 0)),
                          pl.BlockSpec((1, d, tn), lambda j: (first + j, 0, 0))],
                out_specs=[pl.BlockSpec((t, tn), lambda j: (0, first + j))])(h_hbm, w_hbm, rf_hbm)

        _handshake(carry.peers)
        carry.start(cins, couts, csems)
        blocks(0, n_first)
        carry.middle(cins, couts, csems)
        blocks(n_first, nb - n_first)
        carry.finish(cins, couts, csems)

    outs = pl.pallas_call(
        body, name="ff1",
        in_specs=[ANY] * (2 + c_in), out_specs=[ANY] * (1 + c_out),
        out_shape=[jax.ShapeDtypeStruct((t, nb * tn), BF16)] + list(carry.out_shapes),
        scratch_shapes=list(carry.sems),
        input_output_aliases={2 + ci: 1 + co for ci, co in getattr(carry, "aliases", [])},
        compiler_params=pltpu.CompilerParams(
            vmem_limit_bytes=V7X_VMEM_LIMIT_BYTES, collective_id=COLLECTIVE_ID[carry.peers]),
    )(*_in_hbm([h3, w_ff1b]), *_in_hbm(carry.ins))
    return list(outs[:1]), list(outs[1:])


def _ff2_loss(rf, w_ff2, x2, g4, target):
    t, k = rf.shape
    d = x2.shape[1]
    tt, tk = _tile(t, 1024), _tile(k, 1024)
    nk = k // tk

    def body(a_ref, w_ref, x2_hbm, g_ref, tg_hbm, dy_ref, df_ref, dg_ref, loss_ref, acc, x2_ref, tg_ref,
             late_sems):
        i, kk = pl.program_id(0), pl.program_id(1)
        late = _late_copies(i, tt, [(x2_hbm, x2_ref), (tg_hbm, tg_ref)], late_sems)

        @pl.when(kk == 0)
        def _():
            acc[...] = jnp.zeros_like(acc)
            for cp in late:
                cp.start()

        @pl.when((i == 0) & (kk == 0))
        def _():
            dg_ref[...] = jnp.zeros_like(dg_ref)
            loss_ref[...] = jnp.zeros_like(loss_ref)

        rf_tile = a_ref[...]
        acc[...] += _dot_nn(rf_tile * rf_tile, w_ref[...])

        @pl.when(kk == nk - 1)
        def _():
            for cp in late:
                cp.wait()

            def tail(rows):
                fhat, r = _rms_hat(acc[rows, :])
                g = g_ref[...]
                e = x2_ref[rows, :] + fhat * g - tg_ref[rows, :]
                loss_ref[...] += 0.5 * jnp.sum(jnp.mean(e * e, axis=-1, keepdims=True))
                dy = e * (1.0 / d)
                dy_ref[rows, :] = dy.astype(BF16)
                df, dg = _rms_bwd(dy, fhat, r, g)
                df_ref[rows, :] = df.astype(BF16)
                dg_ref[...] += dg

            _row_chunks(tt, tail)

    row = pl.BlockSpec((tt, d), lambda i, kk: (i, 0))
    vec = pl.BlockSpec((1, d), lambda i, kk: (0, 0))
    return _call(
        body, name="ff2_loss", grid=(t // tt, nk),
        in_specs=[
            pl.BlockSpec((tt, tk), lambda i, kk: (i, kk)),
            pl.BlockSpec((tk, d), lambda i, kk: (kk, 0)),
            ANY, vec, ANY,
        ],
        out_specs=[row, row, vec, pl.BlockSpec((1, 128), lambda i, kk: (0, 0))],
        out_shape=[
            jax.ShapeDtypeStruct((t, d), BF16),
            jax.ShapeDtypeStruct((t, d), BF16),
            jax.ShapeDtypeStruct((1, d), F32),
            jax.ShapeDtypeStruct((1, 128), F32),
        ],
        scratch_shapes=[pltpu.VMEM((tt, d), F32), pltpu.VMEM((tt, d), x2.dtype), pltpu.VMEM((tt, d), target.dtype),
                        pltpu.SemaphoreType.DMA((2,))],
        args=(rf, w_ff2, x2, g4, target))[0]


def _ff2_bwd(df, w_ff2, rf, carry=None):
    t, d = df.shape
    n = w_ff2.shape[0]
    tt, tn = _tile(t, 2048), _tile(n, 512)

    def body(df_ref, w_ref, rf_ref, out_ref):
        d_act = _dot_nt(df_ref[...], w_ref[...])
        out_ref[...] = (d_act * (2.0 * rf_ref[...].astype(F32))).astype(BF16)

    blk = pl.BlockSpec((tt, tn), lambda i, j: (i, j))
    return _call(
        body, name="ff2_bwd", grid=(t // tt, n // tn),
        in_specs=[pl.BlockSpec((tt, d), lambda i, j: (i, 0)), pl.BlockSpec((tn, d), lambda i, j: (j, 0)), blk],
        out_specs=[blk],
        out_shape=[jax.ShapeDtypeStruct((t, n), BF16)],
        args=(df, w_ff2, rf), carry=carry)


def _wgrad(a, b, name, prev=None, row_off=0, rows=None, carry=None, square_a=False):
    t, m = a.shape
    n = b.shape[1]
    rows = m if rows is None else rows
    tm, tk = _tile(m, 512), _tile(t, 2048)
    nk = t // tk
    assert row_off % tm == 0
    off = row_off // tm

    def body(*refs):
        a_ref, b_ref = refs[0], refs[1]
        o32_ref, o16_ref, acc = refs[-3], refs[-2], refs[-1]
        kk = pl.program_id(1)

        @pl.when(kk == 0)
        def _():
            acc[...] = jnp.zeros_like(acc)

        a_tile = a_ref[...]
        acc[...] += _dot_tn(a_tile * a_tile if square_a else a_tile, b_ref[...])

        @pl.when(kk == nk - 1)
        def _():
            o32_ref[...] = acc[...]
            o16_ref[...] = acc[...].astype(BF16)

    in_specs = [pl.BlockSpec((tk, tm), lambda i, kk: (kk, i)), pl.BlockSpec((tk, n), lambda i, kk: (kk, 0))]
    args = [a, b]
    aliases = {}
    if prev is not None:
        in_specs += [ANY, ANY]
        args += list(prev)
        aliases = {2: 0, 3: 1}
    out = pl.BlockSpec((tm, n), lambda i, kk: (off + i, 0))
    return _call(
        body, name=name, grid=(m // tm, nk),
        in_specs=in_specs, out_specs=[out, out],
        out_shape=[jax.ShapeDtypeStruct((rows, n), F32), jax.ShapeDtypeStruct((rows, n), BF16)],
        scratch_shapes=[pltpu.VMEM((tm, n), F32)],
        aliases=aliases, args=args, carry=carry)


def _wgrad_parts(parts, b, name, carry=None):
    t, n = b.shape
    tm = 512
    bounds = []
    lo = 0
    for part in parts:
        assert part.shape[0] == t and part.shape[1] % tm == 0
        bounds.append((lo, lo + part.shape[1] // tm))
        lo += part.shape[1] // tm
    nm = lo
    np_ = len(parts)

    def body(*refs):
        p_refs, b_ref, o32_ref, o16_ref = refs[:np_], refs[np_], refs[np_ + 1], refs[np_ + 2]
        i = pl.program_id(0)
        for (lo_p, hi_p), p_ref in zip(bounds, p_refs):
            @pl.when((i >= lo_p) & (i < hi_p))
            def _(p_ref=p_ref):
                res = _dot_tn(p_ref[...], b_ref[...])
                o32_ref[...] = res
                o16_ref[...] = res.astype(BF16)

    def part_spec(lo_p, hi_p):
        return pl.BlockSpec((t, tm), lambda i: (0, jnp.clip(i - lo_p, 0, hi_p - lo_p - 1)))

    out = pl.BlockSpec((tm, n), lambda i: (i, 0))
    return _call(
        body, name=name, grid=(nm,),
        in_specs=[part_spec(lo_p, hi_p) for lo_p, hi_p in bounds] + [pl.BlockSpec((t, n), lambda i: (0, 0))],
        out_specs=[out, out],
        out_shape=[jax.ShapeDtypeStruct((nm * tm, n), F32), jax.ShapeDtypeStruct((nm * tm, n), BF16)],
        args=(*parts, b), carry=carry)


def _wgrad_cols(a, b, bw, tn, name, carry=None):
    t, m = a.shape
    n = b.shape[1]
    per_step = tn // bw

    def body(a_ref, b_ref, o32_ref, o16_ref):
        res = _dot_tn(a_ref[...], b_ref[...])
        for blk in range(per_step):
            part = res[:, blk * bw:(blk + 1) * bw]
            o32_ref[blk] = part
            o16_ref[blk] = part.astype(BF16)

    out = pl.BlockSpec((per_step, m, bw), lambda j: (j, 0, 0))
    return _call(
        body, name=name, grid=(n // tn,),
        in_specs=[pl.BlockSpec((t, m), lambda j: (0, 0)), pl.BlockSpec((t, tn), lambda j: (0, j))],
        out_specs=[out, out],
        out_shape=[jax.ShapeDtypeStruct((n // bw, m, bw), F32), jax.ShapeDtypeStruct((n // bw, m, bw), BF16)],
        args=(a, b), carry=carry)


def _ff1_bwd_norms(d_f1, w_ff1b, dy, x2, g3, m, g2, carry=None):
    t, k = d_f1.shape
    d = x2.shape[1]
    assert dy.dtype == BF16 and x2.dtype == F32
    bw = w_ff1b.shape[2]
    per_step = 2
    tt, tk = _tile(t, 1024), per_step * bw
    nk = k // tk

    def body(a_ref, w_ref, dy_hbm, x2_hbm, g3_ref, m_hbm, g2_ref, dx2_hbm, dm_hbm, dg3_ref, dg2_ref, acc,
             dy_ref, x2_ref, m_ref, late_sems, out_sems):
        i, kk = pl.program_id(0), pl.program_id(1)
        late = _late_copies(i, tt, [(dy_hbm, dy_ref), (x2_hbm, x2_ref), (m_hbm, m_ref)], late_sems)

        @pl.when(kk == 0)
        def _():
            acc[...] = jnp.zeros_like(acc)
            for cp in late:
                cp.start()

        @pl.when((i == 0) & (kk == 0))
        def _():
            dg3_ref[...] = jnp.zeros_like(dg3_ref)
            dg2_ref[...] = jnp.zeros_like(dg2_ref)

        a_tile = a_ref[...]
        for b in range(per_step):
            acc[...] += _dot_nt(a_tile[:, b * bw:(b + 1) * bw], w_ref[b])

        @pl.when(kk == nk - 1)
        def _():
            for cp in late:
                cp.wait()

            def tail(rows):
                xhat, r3 = _rms_hat(x2_ref[rows, :])
                dx, dg3 = _rms_bwd(acc[rows, :], xhat, r3, g3_ref[...])
                dx2 = dy_ref[rows, :].astype(F32) + dx
                x2_ref[rows, :] = dx2
                dg3_ref[...] += dg3
                mhat, r2 = _rms_hat(m_ref[rows, :])
                dm, dg2 = _rms_bwd(dx2, mhat, r2, g2_ref[...])
                dy_ref[rows, :] = dm.astype(BF16)
                dg2_ref[...] += dg2

            _row_chunks(tt, tail)
            tile = pl.ds(pl.multiple_of(i * tt, tt), tt)
            outs = [pltpu.make_async_copy(x2_ref, dx2_hbm.at[tile], out_sems.at[0]),
                    pltpu.make_async_copy(dy_ref, dm_hbm.at[tile], out_sems.at[1])]
            for cp in outs:
                cp.start()
            for cp in outs:
                cp.wait()

    vec = pl.BlockSpec((1, d), lambda i, kk: (0, 0))
    return _call(
        body, name="ff1_bwd_norms", grid=(t // tt, nk),
        in_specs=[
            pl.BlockSpec((tt, tk), lambda i, kk: (i, kk)),
            pl.BlockSpec((per_step, d, bw), lambda i, kk: (kk, 0, 0)),
            ANY, ANY, vec, ANY, vec,
        ],
        out_specs=[ANY, ANY, vec, vec],
        out_shape=[
            jax.ShapeDtypeStruct((t, d), F32),
            jax.ShapeDtypeStruct((t, d), BF16),
            jax.ShapeDtypeStruct((1, d), F32),
            jax.ShapeDtypeStruct((1, d), F32),
        ],
        scratch_shapes=[pltpu.VMEM((tt, d), F32), pltpu.VMEM((tt, d), dy.dtype), pltpu.VMEM((tt, d), F32),
                        pltpu.VMEM((tt, d), F32), pltpu.SemaphoreType.DMA((3,)), pltpu.SemaphoreType.DMA((2,))],
        args=(d_f1, w_ff1b, dy, x2, g3, m, g2), carry=carry)


def _wo_bwd_mix(dm, w_o, br_a, br_b, proj, b_gate, ga_block, gb_block, carry=None):
    t, d = dm.shape
    tt, tn = _tile(t, 1024), 512
    nj = d // tn

    def body(dm_ref, w_ref, bra_ref, brb_ref, ga_ref, gb_ref, ba_ref, bb_ref,
             dbra_ref, dbrb_ref, dga_ref, dgb_ref, dba_ref, dbb_ref):
        i = pl.program_id(1)

        @pl.when(i == 0)
        def _():
            dba_ref[...] = jnp.zeros_like(dba_ref)
            dbb_ref[...] = jnp.zeros_like(dbb_ref)

        d_mix = _dot_nt(dm_ref[...], w_ref[...])
        ga = _sig(ga_ref[...].astype(F32) + ba_ref[...])
        gb = _sig(gb_ref[...].astype(F32) + bb_ref[...])
        dbra_ref[...] = (d_mix * ga).astype(BF16)
        dbrb_ref[...] = (d_mix * gb).astype(BF16)
        dga = d_mix * bra_ref[...].astype(F32) * (ga * (1.0 - ga))
        dgb = d_mix * brb_ref[...].astype(F32) * (gb * (1.0 - gb))
        dga_ref[...] = dga.astype(BF16)
        dgb_ref[...] = dgb.astype(BF16)
        dba_ref[...] += jnp.sum(dga, axis=0, keepdims=True)
        dbb_ref[...] += jnp.sum(dgb, axis=0, keepdims=True)

    blk = pl.BlockSpec((tt, tn), lambda j, i: (i, j))
    vec = pl.BlockSpec((1, tn), lambda j, i: (0, j))
    return _call(
        body, name="wo_bwd_mix", grid=(nj, t // tt),
        in_specs=[
            pl.BlockSpec((tt, d), lambda j, i: (i, 0)),
            pl.BlockSpec((tn, d), lambda j, i: (j, 0)),
            blk, blk,
            pl.BlockSpec((tt, tn), lambda j, i: (i, ga_block + j)),
            pl.BlockSpec((tt, tn), lambda j, i: (i, gb_block + j)),
            vec,
            pl.BlockSpec((1, tn), lambda j, i: (0, nj + j)),
        ],
        out_specs=[blk, blk, blk, blk, vec, vec],
        out_shape=[jax.ShapeDtypeStruct((t, d), BF16)] * 4 + [jax.ShapeDtypeStruct((1, d), F32)] * 2,
        args=(dm, w_o, br_a, br_b, proj, proj, b_gate, b_gate), carry=carry)


def _lru_up_bwd(d_br_a, w_lru_up, proj, h, g_block, carry=None):
    t, d = d_br_a.shape
    tt, tn = _tile(t, 1024), 512

    def body(a_ref, w_ref, g_ref, h_ref, dh_ref, dg_ref):
        d_y = _dot_nt(a_ref[...], w_ref[...])
        gel, gel_grad = _gelu_and_grad(g_ref[...].astype(F32))
        dh_ref[...] = d_y * gel
        dg_ref[...] = (d_y * h_ref[...] * gel_grad).astype(BF16)

    blk = pl.BlockSpec((tt, tn), lambda i, j: (i, j))
    return _call(
        body, name="lru_up_bwd", grid=(t // tt, d // tn),
        in_specs=[
            pl.BlockSpec((tt, d), lambda i, j: (i, 0)),
            pl.BlockSpec((tn, d), lambda i, j: (j, 0)),
            pl.BlockSpec((tt, tn), lambda i, j: (i, g_block + j)),
            blk,
        ],
        out_specs=[blk, blk],
        out_shape=[jax.ShapeDtypeStruct((t, d), F32), jax.ShapeDtypeStruct((t, d), BF16)],
        args=(d_br_a, w_lru_up, proj, h), carry=carry)


def _lru_bwd(dh, h, saved, proj, conv_w, w_a, w_x, lam, carry=None):
    t, dr = dh.shape
    cb = LRU_CB
    hd = LRU_HEAD_DIM
    per = cb // hd
    tc = _tile(t, 256)
    ncb, ntc = dr // cb, t // tc

    def body(dh_ref, h_ref, hp_ref, saved_ref, xp_ref, cw_ref, wa_ref, wx_ref,
             lam_ref, dxp_ref, dwa_ref, dba_ref, dwx_ref, dbx_ref, dlam_ref, dcw_ref, dcb_ref,
             nextd_s, anext_s, gnext_s, tmp_s, wa_s, wx_s):
        c = pl.program_id(1)
        rc = ntc - 1 - c

        @pl.when(c == 0)
        def _():
            nextd_s[...] = jnp.zeros_like(nextd_s)
            anext_s[...] = jnp.zeros_like(anext_s)
            gnext_s[...] = jnp.zeros_like(gnext_s)
            for ref in (dwa_ref, dba_ref, dwx_ref, dbx_ref, dlam_ref, dcw_ref, dcb_ref):
                ref[...] = jnp.zeros_like(ref)
            _fill_block_diag(wa_ref, wa_s)
            _fill_block_diag(wx_ref, wx_s)

        xc, r, i, a, mult = [saved_ref[:, k * cb:(k + 1) * cb] for k in range(N_LRU_SAVED)]
        wa, wx, lam = wa_s[...], wx_s[...], lam_ref[...]
        xcb = xc.astype(BF16)
        sp = _softplus_neg(lam)
        row = lax.broadcasted_iota(jnp.int32, xc.shape, 0)
        h = h_ref[...]
        hp = jnp.where(rc == 0, 0.0, hp_ref[...])
        hprev = jnp.where(row >= 1, pltpu.roll(h, 1, 0), pltpu.roll(hp, 1, 0))

        def up(v, nv, j):
            return jnp.where(row < tc - j, pltpu.roll(v, tc - j, 0), nv)

        av, bv = _scan_rows(up(a, anext_s[...], 1), dh_ref[...], reverse=True)
        gt = av * gnext_s[...] + bv
        tmp_s[...] = gt
        gnext_s[...] = tmp_s[0:1, :]
        tmp_s[...] = a
        anext_s[...] = tmp_s[0:1, :]

        da = gt * hprev
        ixc = i * xc
        d_mult = gt * ixc
        d_i = gt * mult * xc
        d_xc = gt * mult * i
        d_log_a = da * a - d_mult * (a * a) / mult
        d_pre_r = (d_log_a * ((-LRU_C) * sp)) * (r * (1.0 - r))
        d_pre_i = d_i * (i * (1.0 - i))
        d_sp = jnp.sum(d_log_a * ((-LRU_C) * r), axis=0, keepdims=True)
        dlam_ref[...] += d_sp * (-1.0 / (1.0 + jnp.exp(lam)))
        dpr = d_pre_r.astype(BF16)
        dpi = d_pre_i.astype(BF16)
        dba_ref[...] += jnp.sum(d_pre_r, axis=0, keepdims=True)
        dbx_ref[...] += jnp.sum(d_pre_i, axis=0, keepdims=True)
        pa = _dot_tn(xcb, dpr)
        px = _dot_tn(xcb, dpi)
        for k in range(per):
            dwa_ref[k] += pa[k * hd:(k + 1) * hd, k * hd:(k + 1) * hd]
            dwx_ref[k] += px[k * hd:(k + 1) * hd, k * hd:(k + 1) * hd]
        d_xc = d_xc + _dot_nt(dpr, wa) + _dot_nt(dpi, wx)

        nxt = nextd_s[...]
        xp = xp_ref[...].astype(F32)
        dxp = cw_ref[3:4, :] * d_xc
        dcw_ref[3:4, :] += jnp.sum(xp * d_xc, axis=0, keepdims=True)
        for j in (1, 2, 3):
            uj = up(d_xc, pltpu.roll(nxt, tc - j, 0), j)
            dxp = dxp + cw_ref[3 - j:4 - j, :] * uj
            dcw_ref[3 - j:4 - j, :] += jnp.sum(xp * uj, axis=0, keepdims=True)
        dcb_ref[...] += jnp.sum(d_xc, axis=0, keepdims=True)
        nextd_s[...] = d_xc
        dxp_ref[...] = dxp.astype(BF16)

    vec = pl.BlockSpec((1, cb), lambda j, c: (0, j))
    blk = pl.BlockSpec((tc, cb), lambda j, c: (ntc - 1 - c, j))
    mat = pl.BlockSpec((per, hd, hd), lambda j, c: (j, 0, 0))
    cwb = pl.BlockSpec((4, cb), lambda j, c: (0, j))
    return _call(
        body, name="lru_bwd", grid=(ncb, ntc),
        in_specs=[
            blk, blk,
            pl.BlockSpec((tc, cb), lambda j, c: (jnp.maximum(ntc - 2 - c, 0), j)),
            pl.BlockSpec((tc, N_LRU_SAVED * cb), lambda j, c: (ntc - 1 - c, j)),
            blk, cwb, mat, mat, vec,
        ],
        out_specs=[blk, mat, vec, mat, vec, vec, cwb, vec],
        out_shape=[
            jax.ShapeDtypeStruct((t, dr), BF16),
            jax.ShapeDtypeStruct(w_a.shape, F32),
            jax.ShapeDtypeStruct((1, dr), F32),
            jax.ShapeDtypeStruct(w_x.shape, F32),
            jax.ShapeDtypeStruct((1, dr), F32),
            jax.ShapeDtypeStruct((1, dr), F32),
            jax.ShapeDtypeStruct((4, dr), F32),
            jax.ShapeDtypeStruct((1, dr), F32),
        ],
        scratch_shapes=[
            pltpu.VMEM((tc, cb), F32),
            pltpu.VMEM((1, cb), F32),
            pltpu.VMEM((1, cb), F32),
            pltpu.VMEM((tc, cb), F32),
            pltpu.VMEM((cb, cb), BF16),
            pltpu.VMEM((cb, cb), BF16),
        ],
        args=(dh, h, h, saved, proj, conv_w, w_a, w_x, lam), carry=carry)


def _pool_bwd(d_br_b, w_pool_upb, p, pool_w, pool_scale):
    t, d = d_br_b.shape
    nwb, dp, _ = w_pool_upb.shape
    tc = _tile(t, 256)
    ntc = t // tc
    ng = len(POOL_WINDOWS)

    def body(db_ref, wu_ref, p_ref, w_ref, sc_ref, dx_ref, dw_ref, dsc_ref, nz, n2, n4, n8, dp_s, dy_s):
        c = pl.program_id(0)
        rc = ntc - 1 - c

        @pl.when(c == 0)
        def _():
            for s in (nz, n2, n4, n8):
                s[...] = jnp.zeros_like(s)
            dw_ref[...] = jnp.zeros_like(dw_ref)
            dsc_ref[...] = jnp.zeros_like(dsc_ref)

        wu = jnp.concatenate([wu_ref[b] for b in range(nwb)], axis=1)
        dy_s[...] = _dot_nt(db_ref[...], wu)
        for g in range(ng):
            sl = slice(g * POOL_GROUP_DIM, (g + 1) * POOL_GROUP_DIM)
            pg = p_ref[:, sl]
            dyg = dy_s[:, sl]
            wg = w_ref[g].astype(BF16)
            q = _dot_nn(pg, wg)
            dsc_ref[:, sl] += jnp.sum(dyg * q, axis=0, keepdims=True)
            dpw = (dyg * sc_ref[:, sl]).astype(BF16)
            dw_ref[g] += _dot_tn(pg, dpw)
            dp_s[:, sl] = _dot_nt(dpw, wg)

        dpv = dp_s[...]
        row = lax.broadcasted_iota(jnp.int32, dpv.shape, 0)
        col = lax.broadcasted_iota(jnp.int32, dpv.shape, 1)
        win = _pool_select(col, POOL_WINDOWS)
        cnt = jnp.minimum(rc * tc + row + 1, win).astype(F32)
        z = dpv / cnt

        def up(v, nv, j):
            return jnp.where(row < tc - j, pltpu.roll(v, tc - j, 0), pltpu.roll(nv[...], tc - j, 0))

        u2 = z + up(z, nz, 1)
        u4 = u2 + up(u2, n2, 2)
        u8 = u4 + up(u4, n4, 4)
        u16 = u8 + up(u8, n8, 8)
        nz[...] = z
        n2[...] = u2
        n4[...] = u4
        n8[...] = u8
        dx_ref[...] = (_pool_select(col, (u2, u4, u8, u16)) - dpv).astype(BF16)

    blk = pl.BlockSpec((tc, dp), lambda c: (ntc - 1 - c, 0))
    full_w = pl.BlockSpec(pool_w.shape, lambda c: (0, 0, 0))
    vec = pl.BlockSpec((1, dp), lambda c: (0, 0))
    return _call(
        body, name="pool_bwd", grid=(ntc,),
        in_specs=[pl.BlockSpec((tc, d), lambda c: (ntc - 1 - c, 0)),
                  pl.BlockSpec(w_pool_upb.shape, lambda c: (0, 0, 0)), blk, full_w, vec],
        out_specs=[blk, full_w, vec],
        out_shape=[
            jax.ShapeDtypeStruct((t, dp), BF16),
            jax.ShapeDtypeStruct(pool_w.shape, F32),
            jax.ShapeDtypeStruct((1, dp), F32),
        ],
        scratch_shapes=[pltpu.VMEM((tc, dp), F32)] * 6,
        args=(d_br_b, w_pool_upb, p, pool_w, pool_scale))[0]


def _win_bwd_norm(parts, w_int, dx2, x, g1, carry=None):
    t, d = x.shape
    tk = 512
    tt = _tile(t, 1024)
    bounds = []
    k0 = 0
    for part in parts:
        assert part.shape[1] % tk == 0
        bounds.append((k0, k0 + part.shape[1] // tk))
        k0 += part.shape[1] // tk
    nk = k0
    assert nk * tk == w_int.shape[0]
    np_ = len(parts)

    def body(*refs):
        p_refs = refs[:np_]
        w_ref, dx2_hbm, x_hbm, g_ref, gx_hbm, dg_ref, acc, dx2_ref, x_ref, late_sems, out_sem = refs[np_:]
        i, kk = pl.program_id(0), pl.program_id(1)
        late = _late_copies(i, tt, [(dx2_hbm, dx2_ref), (x_hbm, x_ref)], late_sems)

        @pl.when(kk == 0)
        def _():
            acc[...] = jnp.zeros_like(acc)
            for cp in late:
                cp.start()

        @pl.when((i == 0) & (kk == 0))
        def _():
            dg_ref[...] = jnp.zeros_like(dg_ref)

        for (lo, hi), p_ref in zip(bounds, p_refs):
            @pl.when((kk >= lo) & (kk < hi))
            def _(p_ref=p_ref):
                acc[...] += _dot_nn(p_ref[...], w_ref[...])

        @pl.when(kk == nk - 1)
        def _():
            for cp in late:
                cp.wait()

            def tail(rows):
                xhat, r = _rms_hat(x_ref[rows, :])
                dx, dg = _rms_bwd(acc[rows, :], xhat, r, g_ref[...])
                dx2_ref[rows, :] = dx2_ref[rows, :] + dx
                dg_ref[...] += dg

            _row_chunks(tt, tail)
            out = pltpu.make_async_copy(
                dx2_ref, gx_hbm.at[pl.ds(pl.multiple_of(i * tt, tt), tt)], out_sem.at[0])
            out.start()
            out.wait()

    def part_spec(lo, hi):
        return pl.BlockSpec((tt, tk), lambda i, kk: (i, jnp.clip(kk - lo, 0, hi - lo - 1)))

    vec = pl.BlockSpec((1, d), lambda i, kk: (0, 0))
    return _call(
        body, name="win_bwd_norm", grid=(t // tt, nk),
        in_specs=[part_spec(lo, hi) for lo, hi in bounds]
        + [pl.BlockSpec((tk, d), lambda i, kk: (kk, 0)), ANY, ANY, vec],
        out_specs=[ANY, vec],
        out_shape=[jax.ShapeDtypeStruct((t, d), F32), jax.ShapeDtypeStruct((1, d), F32)],
        scratch_shapes=[pltpu.VMEM((tt, d), F32), pltpu.VMEM((tt, d), F32), pltpu.VMEM((tt, d), F32),
                        pltpu.SemaphoreType.DMA((2,)), pltpu.SemaphoreType.DMA((1,))],
        args=(*parts, w_int, dx2, x, g1), carry=carry)


def _adam_math(w, g, m, v):
    m = ADAM_B1 * m + (1.0 - ADAM_B1) * g
    v = ADAM_B2 * v + (1.0 - ADAM_B2) * (g * g)
    m_hat = m / (1.0 - ADAM_B1 ** ADAM_STEP)
    v_hat = v / (1.0 - ADAM_B2 ** ADAM_STEP)
    delta = -ADAM_LR * (m_hat / (jnp.sqrt(v_hat) + ADAM_EPS) + ADAM_WD * w)
    return delta, m, v


def _adamw_big(ws, gs, ms, vs):
    n = len(ws)
    nb = 4
    pair = [isinstance(g, tuple) for g in gs]

    def body(*refs):
        p = 0
        ins = []
        for a in range(n):
            k = 5 if pair[a] else 4
            ins.append(refs[p:p + k])
            p += k
        for a in range(n):
            g_out, d_ref, nm_ref, nv_ref = refs[p + 4 * a:p + 4 * a + 4]
            if pair[a]:
                w_ref, own_ref, recv_ref, m_ref, v_ref = ins[a]
                g = own_ref[...]
                for k in range(3):
                    g = g + recv_ref[k].astype(F32)
            else:
                w_ref, g_ref, m_ref, v_ref = ins[a]
                g = g_ref[...]
            dl, m, v = _adam_math(w_ref[...], g, m_ref[...], v_ref[...])
            g_out[...] = g
            d_ref[...] = dl
            nm_ref[...] = m
            nv_ref[...] = v

    in_specs, out_specs, out_shape, args = [], [], [], []
    for a, (w, g, m, v) in enumerate(zip(ws, gs, ms, vs)):
        rows, cols = w.shape
        blk = pl.BlockSpec((rows // nb, cols), lambda i: (i, 0))
        if pair[a]:
            in_specs += [blk, pl.BlockSpec((None, rows // nb, cols), lambda i: (0, i, 0)),
                         pl.BlockSpec((3, rows // nb, cols), lambda i: (0, i, 0)), blk, blk]
            args += [w, g[0], g[1], m, v]
        else:
            in_specs += [blk] * 4
            args += [w, g, m, v]
        out_specs += [blk] * 4
        out_shape += [jax.ShapeDtypeStruct(w.shape, F32)] * 4
    outs = _call(body, name="adamw_big", grid=(nb,), in_specs=in_specs, out_specs=out_specs,
                 out_shape=out_shape, args=args)[0]
    return [tuple(outs[4 * a:4 * a + 4]) for a in range(n)]


SMALL_ORDER = ("norm_mix_pre", "norm_mix_post", "norm_mlp_pre", "norm_mlp_post", "b_gate", "conv_w", "conv_b",
               "lru_w_a", "lru_b_a", "lru_w_x", "lru_b_x", "lru_lambda", "pool_w", "pool_scale")
VEC_ROW = dict(norm_mix_pre=0, norm_mix_post=1, norm_mlp_pre=2, norm_mlp_post=3, conv_b=6, lru_b_a=7,
               lru_b_x=8, lru_lambda=9)
ROW_B_GATE, ROW_POOL_SCALE, ROW_CONV_W, ROW_LOSS, N_VEC_ROWS = 4, 10, 11, 15, 16


def _adamw_small(vec_parts, g_pool, g_wa, g_wx, me, params):
    d = vec_parts.shape[2]
    names = SMALL_ORDER
    n = len(names)
    cw_cols = params["conv_w"][0].shape[2]

    def body(me_ref, vec_ref, vecc_ref, gp_ref, gwa_ref, gwx_ref, *refs):
        wmv = refs[:3 * n]
        loss_ref = refs[3 * n]
        outs = refs[3 * n + 1:3 * n + 1 + 4 * n]
        vs, vsc = refs[3 * n + 1 + 4 * n:]
        acc, accc = vec_ref[0], vecc_ref[0]
        for k in range(1, N_DEV):
            acc = acc + vec_ref[k]
            accc = accc + vecc_ref[k]
        vs[...] = acc
        vsc[...] = accc
        loss_ref[...] = vs[ROW_LOSS:ROW_LOSS + 1, 0:128]

        def upd(a, g, idx):
            w_ref, m_ref, v_ref = wmv[3 * a:3 * a + 3]
            g_ref, d_ref, nm_ref, nv_ref = outs[4 * a:4 * a + 4]
            dl, m, v = _adam_math(w_ref[idx], g, m_ref[idx], v_ref[idx])
            g_ref[idx] = g
            d_ref[idx] = dl
            nm_ref[idx] = m
            nv_ref[idx] = v

        for a, name in enumerate(names):
            if name in VEC_ROW:
                r = VEC_ROW[name]
                upd(a, vs[r:r + 1, :], (slice(None), slice(None)))
            elif name == "b_gate":
                for half in range(2):
                    r = ROW_B_GATE + half
                    upd(a, vs[r:r + 1, :], (slice(None), slice(half * d, (half + 1) * d)))
            elif name == "pool_scale":
                width = params[name][0].shape[1]
                upd(a, vs[ROW_POOL_SCALE:ROW_POOL_SCALE + 1, 0:width], (slice(None), slice(None)))
            elif name == "conv_w":
                upd(a, vsc[ROW_CONV_W:ROW_CONV_W + 4, :], (0,))
            elif name == "pool_w":
                upd(a, gp_ref[...], (Ellipsis,))
            elif name == "lru_w_a":
                upd(a, gwa_ref[...], (Ellipsis,))
            elif name == "lru_w_x":
                upd(a, gwx_ref[...], (Ellipsis,))
            else:
                raise ValueError(name)

    def whole(shape):
        nd = len(shape)
        return pl.BlockSpec(tuple(shape), lambda i, me_ref: (0,) * nd)

    in_specs = [
        whole(vec_parts.shape),
        pl.BlockSpec((N_DEV, N_VEC_ROWS, cw_cols), lambda i, me_ref: (0, 0, me_ref[0])),
        whole(g_pool.shape), whole(g_wa.shape), whole(g_wx.shape),
    ]
    args = [vec_parts, vec_parts, g_pool, g_wa, g_wx]
    out_specs = [whole((1, 128))]
    out_shape = [jax.ShapeDtypeStruct((1, 128), F32)]
    for name in names:
        for arr in params[name]:
            in_specs.append(whole(arr.shape))
            args.append(arr)
        shp = params[name][0].shape
        out_specs += [whole(shp)] * 4
        out_shape += [jax.ShapeDtypeStruct(shp, F32)] * 4
    grid_spec = pltpu.PrefetchScalarGridSpec(
        num_scalar_prefetch=1, grid=(1,), in_specs=in_specs, out_specs=out_specs,
        scratch_shapes=[pltpu.VMEM((N_VEC_ROWS, d), F32), pltpu.VMEM((N_VEC_ROWS, cw_cols), F32)])
    outs = pl.pallas_call(
        body, name="adamw_small", grid_spec=grid_spec, out_shape=out_shape,
        compiler_params=pltpu.CompilerParams(
            dimension_semantics=("arbitrary",), vmem_limit_bytes=V7X_VMEM_LIMIT_BYTES),
    )(me, *_in_hbm(args))
    return outs[0], {name: tuple(outs[1 + 4 * a:5 + 4 * a]) for a, name in enumerate(names)}


def _rs_sum(fulls, recvs, shard_ids, slot_ids, name):
    n = len(fulls)

    def body(sh_ref, sl_ref, *refs):
        s = pl.program_id(0)
        for a in range(n):
            full_ref, recv_ref = refs[2 * a], refs[2 * a + 1]
            own_ref, send_ref = refs[2 * n + 2 * a], refs[2 * n + 2 * a + 1]
            v = full_ref[...] + recv_ref[...].astype(F32)

            @pl.when(s == 0)
            def _(own_ref=own_ref, v=v):
                own_ref[...] = v

            @pl.when(s > 0)
            def _(send_ref=send_ref, v=v):
                send_ref[...] = v.astype(send_ref.dtype)

    in_specs, out_specs, out_shape, args = [], [], [], []
    for full, recv in zip(fulls, recvs):
        r, rest = recv.shape[1], tuple(recv.shape[2:])
        zeros = (0,) * len(rest)
        in_specs += [
            pl.BlockSpec((r,) + rest, lambda s, sh, sl, zeros=zeros: (sh[s],) + zeros),
            pl.BlockSpec((None, r) + rest, lambda s, sh, sl, zeros=zeros: (sl[s], 0) + zeros),
        ]
        out_specs += [
            pl.BlockSpec((None, r) + rest, lambda s, sh, sl, zeros=zeros: (0, 0) + zeros),
            pl.BlockSpec((None, r) + rest, lambda s, sh, sl, zeros=zeros: (jnp.maximum(s - 1, 0), 0) + zeros),
        ]
        out_shape += [jax.ShapeDtypeStruct((1, r) + rest, F32), jax.ShapeDtypeStruct((3, r) + rest, recv.dtype)]
        args += [full, recv]
    grid_spec = pltpu.PrefetchScalarGridSpec(
        num_scalar_prefetch=2, grid=(4,), in_specs=in_specs, out_specs=out_specs)
    outs = pl.pallas_call(
        body,
        name=name,
        grid_spec=grid_spec,
        out_shape=out_shape,
        compiler_params=pltpu.CompilerParams(
            dimension_semantics=("arbitrary",), vmem_limit_bytes=V7X_VMEM_LIMIT_BYTES),
    )(shard_ids, slot_ids, *_in_hbm(args))
    return [(outs[2 * a], outs[2 * a + 1]) for a in range(n)]


def _finals(pairs, name, carry=None):
    nb = 4
    n = len(pairs)

    def body(*refs):
        for a in range(n):
            own_ref, recv_ref = refs[2 * a], refs[2 * a + 1]
            acc = own_ref[...]
            for k in range(3):
                acc = acc + recv_ref[k].astype(F32)
            refs[2 * n + a][...] = acc

    in_specs, out_specs, out_shape, args = [], [], [], []
    for own, recv in pairs:
        _, rows, cols = own.shape
        in_specs += [pl.BlockSpec((None, rows // nb, cols), lambda i: (0, i, 0)),
                     pl.BlockSpec((3, rows // nb, cols), lambda i: (0, i, 0))]
        args += [own, recv]
        out_specs.append(pl.BlockSpec((rows // nb, cols), lambda i: (i, 0)))
        out_shape.append(jax.ShapeDtypeStruct((rows, cols), F32))
    return _call(body, name=name, grid=(nb,), in_specs=in_specs, out_specs=out_specs,
                 out_shape=out_shape, args=args, carry=carry)


def _rs_sums(fulls_f32, recv1, tag):
    x, y, c = _place()
    qs = jnp.stack([2 * x + y, 2 * (1 - x) + y, 2 * x + (1 - y), 2 * (1 - x) + (1 - y)]).astype(jnp.int32)
    shard_ids = 2 * qs + c
    return _rs_sum(fulls_f32, recv1, shard_ids, qs, "rs_sum_" + tag)


def _rs_level1(fulls_f32, fulls_send, tag):
    recv1 = _run_plan(_rs_sibling_plan(fulls_send), "rs_sibling_" + tag)
    return _rs_sums(fulls_f32, recv1, tag)


def _rows(g):
    return g.reshape(g.shape[0] * g.shape[1], g.shape[2])


def kernel(x, norm_mix_pre, norm_mix_post, norm_mlp_pre, norm_mlp_post, w_in, b_gate, conv_w, conv_b, lru_w_a, lru_b_a, lru_w_x, lru_b_x, lru_lambda, pool_w, pool_scale, w_lru_up, w_pool_up, w_o, w_ff1, w_ff2, loss_target, m_norm_mix_pre, m_norm_mix_post, m_norm_mlp_pre, m_norm_mlp_post, m_w_in, m_b_gate, m_conv_w, m_conv_b, m_lru_w_a, m_lru_b_a, m_lru_w_x, m_lru_b_x, m_lru_lambda, m_pool_w, m_pool_scale, m_w_lru_up, m_w_pool_up, m_w_o, m_w_ff1, m_w_ff2, v_norm_mix_pre, v_norm_mix_post, v_norm_mlp_pre, v_norm_mlp_post, v_w_in, v_b_gate, v_conv_w, v_conv_b, v_lru_w_a, v_lru_b_a, v_lru_w_x, v_lru_b_x, v_lru_lambda, v_pool_w, v_pool_scale, v_w_lru_up, v_w_pool_up, v_w_o, v_w_ff1, v_w_ff2):
    t, d = x.shape[1], x.shape[2]
    d_rnn = conv_b.shape[1]
    d_pool = pool_scale.shape[1]
    per = LRU_CB // LRU_HEAD_DIM
    xi, yi, ci = _place()
    me = 4 * xi + 2 * yi + ci

    x2d = x[0]
    tgt = loss_target[0]

    s_in = w_in[0].T.astype(BF16)
    s_lu = w_lru_up[0].astype(BF16)
    s_pu = w_pool_up[0].astype(BF16)
    s_o = w_o[0].astype(BF16)
    s_f1 = w_ff1[0].astype(BF16)
    s_f2 = w_ff2[0].astype(BF16)
    s_cw = jnp.pad(conv_w[0], ((0, 4), (0, 0)))

    g_in, g_cw = _run_plan(_ag_plan([s_in, s_cw]), "ag_w_in")
    w_int = _rows(g_in)
    conv_w_full = jnp.transpose(g_cw[:, :4, :], (1, 0, 2)).reshape(4, d_rnn)

    wa_bd, wx_bd = lru_w_a[0], lru_w_x[0]
    pw = pool_w[0]
    pw_bf = pw.astype(BF16)

    pool_block = (2 * d_rnn) // d_pool
    ga_block = (2 * d_rnn + d_pool) // 512
    gb_block = ga_block + d // 512
    g_block = d_rnn // 512

    r_f1, r_f2 = s_f1.shape[0], s_f2.shape[0]
    f1_cut = r_f1 // 4
    f2_cut = (3 * r_f2) // 8
    plan = _join([_ag_plan([s_lu, s_pu]), _ag_plan([s_f1], pieces=[(0, f1_cut)])])
    (proj, h1), got = _norm_proj(x2d, norm_mix_pre, w_int, carry=plan)
    (g_lu, g_pu), (g_f1,) = plan.split(got)
    plan = _join([_ag_plan([s_f1], pieces=[(f1_cut, r_f1 - f1_cut)], bufs=[g_f1]), _ag_plan([s_o])])
    (y_lru, h, lru_saved), got = _lru_fwd(
        proj, conv_w_full, conv_b, wa_bd, lru_b_a, wx_bd, lru_b_x, lru_lambda, carry=plan)
    (g_f1,), (g_o,) = plan.split(got)
    w_lu, w_og = _rows(g_lu), _rows(g_o)
    y_pool, p = _pool_fwd(proj, pw_bf, pool_scale, pool_block)
    (br_a, br_b, mix), (g_f2,) = _branch_mix(
        y_lru, y_pool, w_lu, g_pu, proj, b_gate, ga_block, gb_block,
        carry=_ag_plan([s_f2], pieces=[(0, f2_cut)]))
    (m, x2, h3), _ = _wo_norm(mix, w_og, x2d, norm_mix_post, norm_mlp_pre)
    (rf,), (g_f2,) = _ff1(
        h3, g_f1, carry=_ag_plan([s_f2], pieces=[(f2_cut, r_f2 - f2_cut)], bufs=[g_f2]))
    w_f2 = _rows(g_f2)
    dy, df, dg4, loss_part = _ff2_loss(rf, w_f2, x2, norm_mlp_post, tgt)

    (gw_ff2_32, gw_ff2_16), _ = _wgrad(rf, df, "wgrad_ff2", square_a=True)
    (d_f1,), r1_ff2 = _ff2_bwd(df, w_f2, rf, carry=_rs_sibling_plan([gw_ff2_16]))
    ((own_ff2, send_ff2),) = _rs_sums([gw_ff2_32], r1_ff2, "ff2")
    cut2 = (5 * send_ff2.shape[1]) // 16
    (gw_ff1_32, gw_ff1_16), (r2_ff2,) = _wgrad_cols(
        h3, d_f1, s_f1.shape[1], s_f1.shape[1], "wgrad_ff1",
        carry=_rs_chips_plan([send_ff2], pieces=[(0, cut2)]))
    plan = _join([_rs_chips_plan([send_ff2], pieces=[(cut2, send_ff2.shape[1] - cut2)], bufs=[r2_ff2]),
                  _rs_sibling_plan([gw_ff1_16])])
    (dx2, dm, dg3, dg2), got = _ff1_bwd_norms(d_f1, g_f1, dy, x2, norm_mlp_pre, m, norm_mix_post, carry=plan)
    (r2_ff2,), r1_ff1 = plan.split(got)
    ((own_ff1, send_ff1),) = _rs_sums([gw_ff1_32], r1_ff1, "ff1")
    own_ff1, send_ff1 = own_ff1.reshape((1,) + s_f1.shape), send_ff1.reshape((3,) + s_f1.shape)
    cut = send_ff1.shape[1] // 4
    (gw_o_32, gw_o_16), _ = _wgrad(mix, dm, "wgrad_o")
    (d_br_a, d_br_b, p_ga, p_gb, dbg_a, dbg_b), (r2_ff1,) = _wo_bwd_mix(
        dm, w_og, br_a, br_b, proj, b_gate, ga_block, gb_block,
        carry=_rs_chips_plan([send_ff1], pieces=[(0, cut)]))
    (gw_lu_32, gw_lu_16), _ = _wgrad(y_lru, d_br_a, "wgrad_lru_up")
    (gw_pu_32, gw_pu_16), _ = _wgrad_cols(y_pool, d_br_b, s_pu.shape[1], d, "wgrad_pool_up")
    (dh, p_g), r1_mid = _lru_up_bwd(
        d_br_a, w_lu, proj, h, g_block,
        carry=_rs_sibling_plan([gw_o_16, gw_lu_16, gw_pu_16]))
    mid = _rs_sums([gw_o_32, gw_lu_32, gw_pu_32], r1_mid, "mid")
    plan = _join([_rs_chips_plan([send_ff1], pieces=[(cut, send_ff1.shape[1] - cut)], bufs=[r2_ff1]),
                  _rs_chips_plan([mid[0][1]])])
    (p_x, dwa, db_a, dwx, db_x, dlam, dconv_w, dconv_b), got = _lru_bwd(
        dh, h, lru_saved, proj, conv_w_full, wa_bd, wx_bd, lru_lambda, carry=plan)
    (r2_ff1,), (r2_o,) = plan.split(got)
    p_p, dpool_w, dpool_scale = _pool_bwd(d_br_b, g_pu, p, pw, pool_scale)
    parts = [p_x, p_g, p_p, p_ga, p_gb]
    gw_in, (r2_lu, r2_pu) = _wgrad_parts(
        parts, h1, "wgrad_in", carry=_rs_chips_plan([mid[1][1], mid[2][1]]))
    r2_mid = [r2_o, r2_lu, r2_pu]
    tail = _rs_level1([gw_in[0], dpool_w.reshape(N_DEV, -1, POOL_GROUP_DIM), dwa, dwx],
                      [gw_in[1], dpool_w.reshape(N_DEV, -1, POOL_GROUP_DIM), dwa, dwx], "in")
    (grad_x, dg1), r2_tail = _win_bwd_norm(parts, w_int, dx2, x2d, norm_mix_pre,
                                           carry=_rs_chips_plan([s for _, s in tail]))

    def flat2(a):
        return a.reshape(a.shape[0], -1, a.shape[-1])

    fin_small, _ = _finals([
        (flat2(tail[1][0]), flat2(r2_tail[1])), (flat2(tail[2][0]), flat2(r2_tail[2])),
        (flat2(tail[3][0]), flat2(r2_tail[3])),
    ], "rs_finals_small")

    def pad_row(a):
        return jnp.pad(a, ((0, 0), (0, d - a.shape[1])))

    vecs = jnp.concatenate([dg1, dg2, dg3, dg4, dbg_a, dbg_b, dconv_b, db_a, db_x, dlam,
                            pad_row(dpool_scale), dconv_w, pad_row(loss_part)], axis=0)
    assert vecs.shape[0] == N_VEC_ROWS
    vec_parts, g_pool, g_wa, g_wx = _run_plan(_ag_plan([vecs] + fin_small), "ag_tail")

    big_names = ["w_in", "w_lru_up", "w_pool_up", "w_o", "w_ff1", "w_ff2"]
    big_w = [w_in[0].T, w_lru_up[0], w_pool_up[0], w_o[0], w_ff1[0], w_ff2[0]]
    big_g = [(tail[0][0], r2_tail[0]), (mid[1][0], r2_mid[1]),
             (mid[2][0].reshape((1,) + s_pu.shape), r2_mid[2].reshape((3,) + s_pu.shape)),
             (mid[0][0], r2_mid[0]), (own_ff1, r2_ff1), (own_ff2, r2_ff2)]
    big_m = [m_w_in[0].T, m_w_lru_up[0], m_w_pool_up[0], m_w_o[0], m_w_ff1[0], m_w_ff2[0]]
    big_v = [v_w_in[0].T, v_w_lru_up[0], v_w_pool_up[0], v_w_o[0], v_w_ff1[0], v_w_ff2[0]]
    big_out = _adamw_big(big_w, big_g, big_m, big_v)
    big_out[0] = tuple(o.T for o in big_out[0])

    small = dict(
        norm_mix_pre=(norm_mix_pre, m_norm_mix_pre, v_norm_mix_pre),
        norm_mix_post=(norm_mix_post, m_norm_mix_post, v_norm_mix_post),
        norm_mlp_pre=(norm_mlp_pre, m_norm_mlp_pre, v_norm_mlp_pre),
        norm_mlp_post=(norm_mlp_post, m_norm_mlp_post, v_norm_mlp_post),
        b_gate=(b_gate, m_b_gate, v_b_gate), conv_w=(conv_w, m_conv_w, v_conv_w),
        conv_b=(conv_b, m_conv_b, v_conv_b), lru_w_a=(lru_w_a, m_lru_w_a, v_lru_w_a),
        lru_b_a=(lru_b_a, m_lru_b_a, v_lru_b_a), lru_w_x=(lru_w_x, m_lru_w_x, v_lru_w_x),
        lru_b_x=(lru_b_x, m_lru_b_x, v_lru_b_x), lru_lambda=(lru_lambda, m_lru_lambda, v_lru_lambda),
        pool_w=(pool_w, m_pool_w, v_pool_w), pool_scale=(pool_scale, m_pool_scale, v_pool_scale))
    loss_row, small_out = _adamw_small(
        vec_parts, g_pool.reshape(pool_w.shape), g_wa.reshape(lru_w_a.shape), g_wx.reshape(lru_w_x.shape),
        jnp.reshape(me, (1,)).astype(jnp.int32), small)
    grads = {n: o[0] for n, o in small_out.items()}
    delta = {n: o[1] for n, o in small_out.items()}
    new_m = {n: o[2] for n, o in small_out.items()}
    new_v = {n: o[3] for n, o in small_out.items()}

    for name, (g, dl, nm, nv) in zip(big_names, big_out):
        grads[name], delta[name], new_m[name], new_v[name] = g[None], dl[None], nm[None], nv[None]

    loss = loss_row[0, 0]
    order = ["norm_mix_pre", "norm_mix_post", "norm_mlp_pre", "norm_mlp_post", "w_in", "b_gate", "conv_w",
             "conv_b", "lru_w_a", "lru_b_a", "lru_w_x", "lru_b_x", "lru_lambda", "pool_w", "pool_scale",
             "w_lru_up", "w_pool_up", "w_o", "w_ff1", "w_ff2"]
    return (loss, grad_x[None], *[grads[n] for n in order], *[delta[n] for n in order],
            *[new_m[n] for n in order], *[new_v[n] for n in order])
```

```python
import functools
import math
import operator
import types

import jax
import jax.numpy as jnp
from jax import lax
from jax.experimental import pallas as pl
from jax.experimental.pallas import tpu as pltpu

F32 = jnp.float32
BF16 = jnp.bfloat16
NORM_EPS = 1e-6
LRU_C = 8.0
N_LRU_HEADS = 16
LRU_HEAD_DIM = 64
POOL_WINDOWS = (2, 4, 8, 16)
POOL_GROUP_DIM = 128
ADAM_LR = 0.001
ADAM_B1 = 0.9
ADAM_B2 = 0.999
ADAM_EPS = 1e-08
ADAM_WD = 0.01
ADAM_STEP = 10
N_DEV = 8
V7X_VMEM_LIMIT_BYTES = 56 * 1024 * 1024
LRU_CB = 256
MESH = pl.DeviceIdType.MESH
ANY = pl.BlockSpec(memory_space=pl.ANY)


def _tile(n, pref):
    t = min(n, pref)
    assert n % t == 0, (n, pref)
    return t


def _dot_nn(a, b):
    return lax.dot_general(a, b, (((1,), (0,)), ((), ())), preferred_element_type=F32)


def _dot_nt(a, b):
    return lax.dot_general(a, b, (((1,), (1,)), ((), ())), preferred_element_type=F32)


def _dot_tn(a, b):
    return lax.dot_general(a, b, (((0,), (0,)), ((), ())), preferred_element_type=F32)


def _row_chunks(n_rows, fn, chunk=256):
    chunk = min(chunk, n_rows)
    assert n_rows % chunk == 0

    def step(r, carry):
        fn(pl.ds(pl.multiple_of(r * chunk, chunk), chunk))
        return carry

    lax.fori_loop(0, n_rows // chunk, step, 0)


def _late_copies(i, tt, pairs, sems):
    rows = pl.ds(pl.multiple_of(i * tt, tt), tt)
    return [pltpu.make_async_copy(hbm.at[rows], buf, sems.at[j]) for j, (hbm, buf) in enumerate(pairs)]


def _sig(x):
    return 1.0 / (1.0 + jnp.exp(-x))


def _rms_hat(x):
    r = lax.rsqrt(jnp.mean(x * x, axis=-1, keepdims=True) + NORM_EPS)
    return x * r, r


def _rms_bwd(dn, xhat, r, g):
    q = dn * g
    dx = r * (q - xhat * jnp.mean(q * xhat, axis=-1, keepdims=True))
    dg = jnp.sum(dn * xhat, axis=0, keepdims=True)
    return dx, dg


_GELU_K = math.sqrt(2.0 / math.pi)
_GELU_C = 0.044715


def _gelu_and_grad(g):
    t = jnp.tanh(_GELU_K * (g + _GELU_C * g * g * g))
    val = 0.5 * g * (1.0 + t)
    grad = 0.5 * (1.0 + t) + 0.5 * g * (1.0 - t * t) * (_GELU_K * (1.0 + 3.0 * _GELU_C * g * g))
    return val, grad


def _softplus_neg(lam):
    z = -lam
    e = jnp.exp(-jnp.abs(z))
    u = 1.0 + e
    d = u - 1.0
    l1p = jnp.where(d == 0.0, e, jnp.log(u) * (e / jnp.where(d == 0.0, 1.0, d)))
    return jnp.maximum(z, 0.0) + l1p


def _lru_gates(xc, wa, ba, wx, bx, lam):
    xcb = xc.astype(BF16)
    r = _sig(_dot_nn(xcb, wa) + ba)
    i = _sig(_dot_nn(xcb, wx) + bx)
    sp = _softplus_neg(lam)
    log_a = (-LRU_C) * r * sp
    a = jnp.exp(log_a)
    mult = jnp.sqrt(-jnp.tanh(log_a) * (1.0 + a * a))
    return xcb, r, i, sp, log_a, a, mult


def _place():
    return lax.axis_index("x"), lax.axis_index("y"), lax.axis_index("c")


def _ag_plan(shards, pieces=None, bufs=None):
    na = len(shards)
    n_kinds = 7

    def parts(ins, outs, sems):
        send_sems, recv_sems, local_sems = sems
        x, y, c = _place()
        me, sibling = (x, y, c), (x, y, 1 - c)
        x_nb, y_nb, diag = (1 - x, y), (x, 1 - y), (1 - x, 1 - y)
        relay_src = (c * (1 - x) + (1 - c) * x, c * y + (1 - c) * (1 - y))
        relay_dst = (c * x + (1 - c) * (1 - x), c * (1 - y) + (1 - c) * y)

        def own(a):
            return ins[a] if pieces is None else ins[a].at[pl.ds(*pieces[a])]

        def slot(a, px, py, pc):
            idx = 4 * px + 2 * py + pc
            return outs[a].at[idx] if pieces is None else outs[a].at[idx, pl.ds(*pieces[a])]

        def copy(a, k, block, to, src=None):
            return pltpu.make_async_remote_copy(
                src_ref=slot(a, *block) if src is None else src,
                dst_ref=slot(a, *block),
                send_sem=send_sems.at[a * n_kinds + k],
                recv_sem=recv_sems.at[a * n_kinds + k],
                device_id=to,
                device_id_type=MESH,
            )

        mine = [pltpu.make_async_copy(own(a), slot(a, *me), local_sems.at[a]) for a in range(na)]
        first, second, third = [], [], []
        for a in range(na):
            first += [copy(a, 0, me, sibling, src=own(a)), copy(a, 1, me, (*x_nb, c), src=own(a)),
                      copy(a, 2, me, (*y_nb, c), src=own(a))]
            second += [copy(a, 3, (*relay_src, c), (*relay_dst, c)), copy(a, 4, (*x_nb, c), sibling),
                       copy(a, 5, (*y_nb, c), sibling)]
            third.append(copy(a, 6, (*diag, c), sibling))
        return sibling, c, x_nb, y_nb, diag, copy, mine, first, second, third

    def start(ins, outs, sems):
        _, _, _, _, _, _, mine, first, _, _ = parts(ins, outs, sems)
        for cp in mine + first:
            cp.start()

    def middle(ins, outs, sems):
        _, c, x_nb, y_nb, _, copy, _, _, second, _ = parts(ins, outs, sems)
        for a in range(na):
            copy(a, 1, (*x_nb, c), (*x_nb, c)).wait_recv()
            copy(a, 2, (*y_nb, c), (*y_nb, c)).wait_recv()
        for cp in second:
            cp.start()

    def finish(ins, outs, sems):
        sibling, c, x_nb, y_nb, diag, copy, mine, first, second, third = parts(ins, outs, sems)
        for a in range(na):
            copy(a, 3, (*diag, c), (*diag, c)).wait_recv()
            third[a].start()
        for a in range(na):
            copy(a, 0, sibling, sibling).wait_recv()
            copy(a, 4, (*x_nb, 1 - c), sibling).wait_recv()
            copy(a, 5, (*y_nb, 1 - c), sibling).wait_recv()
            copy(a, 6, (*diag, 1 - c), sibling).wait_recv()
        for cp in first + second + third:
            cp.wait_send()
        for cp in mine:
            cp.wait()

    return types.SimpleNamespace(
        ins=list(shards) + list(bufs or []),
        out_shapes=[jax.ShapeDtypeStruct((N_DEV,) + s.shape, s.dtype) for s in shards],
        sems=[pltpu.SemaphoreType.DMA((n_kinds * na,)), pltpu.SemaphoreType.DMA((n_kinds * na,)),
              pltpu.SemaphoreType.DMA((na,))],
        aliases=[(na + a, a) for a in range(na)] if bufs else [],
        peers=frozenset({"sibling", "neighbours"}), start=start, middle=middle, finish=finish)


def _rs_sibling_plan(fulls):
    na = len(fulls)
    rs = [f.shape[0] // N_DEV for f in fulls]

    def copies(ins, outs, sems):
        send_sems, recv_sems = sems
        x, y, c = _place()
        out = []
        for a in range(na):
            for q in range(4):
                shard = 2 * q + (1 - c)
                out.append(pltpu.make_async_remote_copy(
                    src_ref=ins[a].at[pl.ds(shard * rs[a], rs[a])],
                    dst_ref=outs[a].at[q],
                    send_sem=send_sems.at[a * 4 + q],
                    recv_sem=recv_sems.at[a * 4 + q],
                    device_id=(x, y, 1 - c),
                    device_id_type=MESH,
                ))
        return out

    def start(ins, outs, sems):
        for cp in copies(ins, outs, sems):
            cp.start()

    def finish(ins, outs, sems):
        for cp in copies(ins, outs, sems):
            cp.wait()

    return types.SimpleNamespace(
        ins=list(fulls),
        out_shapes=[jax.ShapeDtypeStruct((4, r) + f.shape[1:], f.dtype) for r, f in zip(rs, fulls)],
        sems=[pltpu.SemaphoreType.DMA((4 * na,)), pltpu.SemaphoreType.DMA((4 * na,))],
        peers=frozenset({"sibling"}), start=start, finish=finish)


def _rs_chips_plan(sends, pieces=None, bufs=None):
    na = len(sends)

    def copies(ins, outs, sems):
        send_sems, recv_sems = sems
        x, y, c = _place()
        chips = [(1 - x, y), (x, 1 - y), (1 - x, 1 - y)]
        out = []
        for a in range(na):
            for k, chip in enumerate(chips):
                rows = (k,) if pieces is None else (k, pl.ds(*pieces[a]))
                out.append(pltpu.make_async_remote_copy(
                    src_ref=ins[a].at[rows],
                    dst_ref=outs[a].at[rows],
                    send_sem=send_sems.at[a * 3 + k],
                    recv_sem=recv_sems.at[a * 3 + k],
                    device_id=(*chip, c),
                    device_id_type=MESH,
                ))
        return out

    def start(ins, outs, sems):
        for cp in copies(ins, outs, sems):
            cp.start()

    def finish(ins, outs, sems):
        for cp in copies(ins, outs, sems):
            cp.wait()

    return types.SimpleNamespace(
        ins=list(sends) + list(bufs or []),
        out_shapes=[jax.ShapeDtypeStruct(s.shape, s.dtype) for s in sends],
        sems=[pltpu.SemaphoreType.DMA((3 * na,)), pltpu.SemaphoreType.DMA((3 * na,))],
        aliases=[(na + a, a) for a in range(na)] if bufs else [],
        peers=frozenset({"chips"}), start=start, finish=finish)


def _join(plans):
    ins, outs, sems, aliases, offs = [], [], [], [], []
    for p in plans:
        offs.append((len(ins), len(outs), len(sems)))
        aliases += [(len(ins) + ci, len(outs) + co) for ci, co in getattr(p, "aliases", [])]
        ins += p.ins
        outs += p.out_shapes
        sems += p.sems

    def cut(p, off, i, o, s):
        return (i[off[0]:off[0] + len(p.ins)], o[off[1]:off[1] + len(p.out_shapes)],
                s[off[2]:off[2] + len(p.sems)])

    def start(i, o, s):
        for p, off in zip(plans, offs):
            p.start(*cut(p, off, i, o, s))

    def middle(i, o, s):
        for p, off in zip(plans, offs):
            if getattr(p, "middle", None) is not None:
                p.middle(*cut(p, off, i, o, s))

    def finish(i, o, s):
        for p, off in zip(plans, offs):
            p.finish(*cut(p, off, i, o, s))

    def split(results):
        return [list(results[off[1]:off[1] + len(p.out_shapes)]) for p, off in zip(plans, offs)]

    return types.SimpleNamespace(ins=ins, out_shapes=outs, sems=sems, aliases=aliases,
                                 peers=frozenset().union(*[p.peers for p in plans]),
                                 start=start, middle=middle, finish=finish, split=split)


COLLECTIVE_ID = {frozenset({"sibling"}): 0, frozenset({"chips"}): 1, frozenset({"sibling", "chips"}): 2,
                 frozenset({"sibling", "neighbours"}): 3}


def _handshake(peers):
    x, y, c = _place()
    devs = []
    if "sibling" in peers:
        devs.append((x, y, 1 - c))
    if "neighbours" in peers:
        devs += [(1 - x, y, c), (x, 1 - y, c)]
    if "chips" in peers:
        assert "neighbours" not in peers
        devs += [(1 - x, y, c), (x, 1 - y, c), (1 - x, 1 - y, c)]
    barrier = pltpu.get_barrier_semaphore()
    for dev in devs:
        pl.semaphore_signal(barrier, inc=1, device_id=dev, device_id_type=MESH)
    pl.semaphore_wait(barrier, len(devs))


def _in_hbm(args):
    return [pltpu.with_memory_space_constraint(a, pltpu.HBM) for a in args]


def _run_plan(plan, name):
    n_in, n_out = len(plan.ins), len(plan.out_shapes)

    def body(*refs):
        ins, outs, sems = refs[:n_in], refs[n_in:n_in + n_out], refs[n_in + n_out:]
        _handshake(plan.peers)
        plan.start(ins, outs, sems)
        if getattr(plan, "middle", None) is not None:
            plan.middle(ins, outs, sems)
        plan.finish(ins, outs, sems)

    return pl.pallas_call(
        body,
        name=name,
        in_specs=[ANY] * n_in,
        out_specs=[ANY] * n_out,
        out_shape=plan.out_shapes,
        scratch_shapes=plan.sems,
        input_output_aliases=dict(getattr(plan, "aliases", [])),
        compiler_params=pltpu.CompilerParams(collective_id=COLLECTIVE_ID[plan.peers]),
    )(*_in_hbm(plan.ins))


def _call(body, *, name, grid, in_specs, out_specs, out_shape, args, scratch_shapes=(), aliases=None,
          carry=None):
    n_in, n_out, n_scr = len(in_specs), len(out_shape), len(scratch_shapes)
    params = pltpu.CompilerParams(
        dimension_semantics=("arbitrary",) * len(grid), vmem_limit_bytes=V7X_VMEM_LIMIT_BYTES)
    if carry is None:
        outs = pl.pallas_call(
            body, name=name, grid=grid, in_specs=list(in_specs), out_specs=list(out_specs),
            out_shape=list(out_shape), scratch_shapes=list(scratch_shapes),
            input_output_aliases=aliases or {}, compiler_params=params)(*_in_hbm(args))
        return list(outs), []
    c_in, c_out = len(carry.ins), len(carry.out_shapes)

    def full(*refs):
        p = 0
        ins = refs[p:p + n_in]
        p += n_in
        cins = refs[p:p + c_in]
        p += c_in
        outs = refs[p:p + n_out]
        p += n_out
        couts = refs[p:p + c_out]
        p += c_out
        scr = refs[p:p + n_scr]
        csems = refs[p + n_scr:]
        ids = [pl.program_id(a) for a in range(len(grid))]
        first = functools.reduce(operator.and_, [i == 0 for i in ids])
        last = functools.reduce(operator.and_, [i == g - 1 for i, g in zip(ids, grid)])

        @pl.when(first)
        def _():
            _handshake(carry.peers)
            carry.start(cins, couts, csems)

        if getattr(carry, "middle", None) is not None:
            n_steps = math.prod(grid)
            flat = functools.reduce(lambda acc, ig: acc * ig[1] + ig[0], zip(ids, grid), 0)

            @pl.when(flat == (2 * n_steps) // 3)
            def _():
                carry.middle(cins, couts, csems)

        body(*ins, *outs, *scr)

        @pl.when(last)
        def _():
            carry.finish(cins, couts, csems)

    all_aliases = dict(aliases or {})
    all_aliases.update({n_in + ci: n_out + co for ci, co in getattr(carry, "aliases", [])})
    params = pltpu.CompilerParams(
        dimension_semantics=("arbitrary",) * len(grid), vmem_limit_bytes=V7X_VMEM_LIMIT_BYTES,
        collective_id=COLLECTIVE_ID[carry.peers])
    outs = pl.pallas_call(
        full, name=name, grid=grid,
        in_specs=list(in_specs) + [ANY] * c_in,
        out_specs=list(out_specs) + [ANY] * c_out,
        out_shape=list(out_shape) + list(carry.out_shapes),
        scratch_shapes=list(scratch_shapes) + list(carry.sems),
        input_output_aliases=all_aliases, compiler_params=params)(*_in_hbm(args), *_in_hbm(carry.ins))
    return list(outs[:n_out]), list(outs[n_out:])


def _norm_proj(x, g1, w_int, carry=None):
    t, d = x.shape
    n = w_int.shape[0]
    tt, tn = _tile(t, 2048), _tile(n, 512)

    def body(x_ref, g_ref, w_ref, proj_ref, h1_ref, h1_s):
        @pl.when(pl.program_id(1) == 0)
        def _():
            def norm_rows(rows):
                xhat, _ = _rms_hat(x_ref[rows, :])
                h = (xhat * g_ref[...]).astype(BF16)
                h1_s[rows, :] = h
                h1_ref[rows, :] = h

            _row_chunks(tt, norm_rows)

        proj_ref[...] = _dot_nt(h1_s[...], w_ref[...]).astype(BF16)

    return _call(
        body, name="norm_proj", grid=(t // tt, n // tn),
        in_specs=[
            pl.BlockSpec((tt, d), lambda i, j: (i, 0)),
            pl.BlockSpec((1, d), lambda i, j: (0, 0)),
            pl.BlockSpec((tn, d), lambda i, j: (j, 0)),
        ],
        out_specs=[
            pl.BlockSpec((tt, tn), lambda i, j: (i, j)),
            pl.BlockSpec((tt, d), lambda i, j: (i, 0)),
        ],
        out_shape=[jax.ShapeDtypeStruct((t, n), BF16), jax.ShapeDtypeStruct((t, d), BF16)],
        scratch_shapes=[pltpu.VMEM((tt, d), BF16)],
        args=(x, g1, w_int), carry=carry)


def _scan_rows(av, bv, reverse):
    tc = av.shape[0]
    row = lax.broadcasted_iota(jnp.int32, av.shape, 0)
    s = 1
    while s < tc:
        if s < 8:
            keep = (row < tc - s) if reverse else (row >= s)
            shift = (tc - s) if reverse else s
            a_sh = jnp.where(keep, pltpu.roll(av, shift, 0), 1.0)
            b_sh = jnp.where(keep, pltpu.roll(bv, shift, 0), 0.0)
            bv = av * b_sh + bv
            av = av * a_sh
        elif reverse:
            bv = jnp.concatenate([av[:tc - s] * bv[s:] + bv[:tc - s], bv[tc - s:]], axis=0)
            av = jnp.concatenate([av[:tc - s] * av[s:], av[tc - s:]], axis=0)
        else:
            bv = jnp.concatenate([bv[:s], av[s:] * bv[:tc - s] + bv[s:]], axis=0)
            av = jnp.concatenate([av[:s], av[s:] * av[:tc - s]], axis=0)
        s *= 2
    return av, bv


N_LRU_SAVED = 5


def _fill_block_diag(w_ref, bd_ref):
    bd_ref[...] = jnp.zeros_like(bd_ref)
    hd = LRU_HEAD_DIM
    for k in range(w_ref.shape[0]):
        bd_ref[k * hd:(k + 1) * hd, k * hd:(k + 1) * hd] = w_ref[k].astype(BF16)


def _lru_fwd(proj, conv_w, conv_b, w_a, b_a, w_x, b_x, lam, carry=None):
    t = proj.shape[0]
    dr = conv_b.shape[1]
    cb = LRU_CB
    tc = _tile(t, 256)
    ncb, ntc = dr // cb, t // tc

    def body(xp_ref, g_ref, cw_ref, cb_ref, wa_ref, ba_ref, wx_ref, bx_ref, lam_ref,
             y_ref, h_ref, saved_ref, prevx_s, hlast_s, wa_s, wx_s):
        c = pl.program_id(1)

        @pl.when(c == 0)
        def _():
            prevx_s[...] = jnp.zeros_like(prevx_s)
            hlast_s[...] = jnp.zeros_like(hlast_s)
            _fill_block_diag(wa_ref, wa_s)
            _fill_block_diag(wx_ref, wx_s)

        x = xp_ref[...].astype(F32)
        prev = prevx_s[...]
        row = lax.broadcasted_iota(jnp.int32, x.shape, 0)

        def sh(j):
            return jnp.where(row >= j, pltpu.roll(x, j, 0), pltpu.roll(prev, j, 0))

        xc = (cb_ref[...] + cw_ref[0:1, :] * sh(3) + cw_ref[1:2, :] * sh(2)
              + cw_ref[2:3, :] * sh(1) + cw_ref[3:4, :] * x)
        prevx_s[...] = x
        _, r, i, _, _, a, mult = _lru_gates(xc, wa_s[...], ba_ref[...], wx_s[...], bx_ref[...],
                                            lam_ref[...])
        for k, val in enumerate((xc, r, i, a, mult)):
            saved_ref[:, k * cb:(k + 1) * cb] = val
        av, bv = _scan_rows(a, mult * (i * xc), reverse=False)
        h = av * hlast_s[...] + bv
        h_ref[...] = h
        hlast_s[...] = h_ref[tc - 1:tc, :]
        gel, _ = _gelu_and_grad(g_ref[...].astype(F32))
        y_ref[...] = (h * gel).astype(BF16)

    vec = pl.BlockSpec((1, cb), lambda j, c: (0, j))
    blk = pl.BlockSpec((tc, cb), lambda j, c: (c, j))
    mat = pl.BlockSpec((cb // LRU_HEAD_DIM, LRU_HEAD_DIM, LRU_HEAD_DIM), lambda j, c: (j, 0, 0))
    return _call(
        body, name="lru_fwd", grid=(ncb, ntc),
        in_specs=[
            blk,
            pl.BlockSpec((tc, cb), lambda j, c: (c, ncb + j)),
            pl.BlockSpec((4, cb), lambda j, c: (0, j)),
            vec, mat, vec, mat, vec, vec,
        ],
        out_specs=[blk, blk, pl.BlockSpec((tc, N_LRU_SAVED * cb), lambda j, c: (c, j))],
        out_shape=[jax.ShapeDtypeStruct((t, dr), BF16), jax.ShapeDtypeStruct((t, dr), F32),
                   jax.ShapeDtypeStruct((t, N_LRU_SAVED * dr), F32)],
        scratch_shapes=[pltpu.VMEM((tc, cb), F32), pltpu.VMEM((1, cb), F32),
                        pltpu.VMEM((cb, cb), BF16), pltpu.VMEM((cb, cb), BF16)],
        args=(proj, proj, conv_w, conv_b, w_a, b_a, w_x, b_x, lam), carry=carry)


def _pool_select(col, vals):
    out = vals[3]
    for g in (2, 1, 0):
        out = jnp.where(col < (g + 1) * POOL_GROUP_DIM, vals[g], out)
    return out


def _pool_fwd(proj, pool_w, pool_scale, col_block):
    t = proj.shape[0]
    dp = pool_scale.shape[1]
    tc = _tile(t, 256)
    ntc = t // tc

    def body(x_ref, w_ref, sc_ref, y_ref, p_ref, px, p2, p4, p8):
        c = pl.program_id(0)

        @pl.when(c == 0)
        def _():
            for s in (px, p2, p4, p8):
                s[...] = jnp.zeros_like(s)

        x = x_ref[...].astype(F32)
        row = lax.broadcasted_iota(jnp.int32, x.shape, 0)
        col = lax.broadcasted_iota(jnp.int32, x.shape, 1)

        def sh(v, pv, j):
            return jnp.where(row >= j, pltpu.roll(v, j, 0), pltpu.roll(pv[...], j, 0))

        s2 = x + sh(x, px, 1)
        s4 = s2 + sh(s2, p2, 2)
        s8 = s4 + sh(s4, p4, 4)
        s16 = s8 + sh(s8, p8, 8)
        px[...] = x
        p2[...] = s2
        p4[...] = s4
        p8[...] = s8
        wsum = _pool_select(col, (s2, s4, s8, s16))
        win = _pool_select(col, POOL_WINDOWS)
        cnt = jnp.minimum(c * tc + row + 1, win).astype(F32)
        p = wsum / cnt - x
        pb = p.astype(BF16)
        p_ref[...] = pb
        for g in range(len(POOL_WINDOWS)):
            sl = slice(g * POOL_GROUP_DIM, (g + 1) * POOL_GROUP_DIM)
            yg = _dot_nn(pb[:, sl], w_ref[g]) * sc_ref[:, sl]
            y_ref[:, sl] = yg.astype(BF16)

    return _call(
        body, name="pool_fwd", grid=(ntc,),
        in_specs=[
            pl.BlockSpec((tc, dp), lambda c: (c, col_block)),
            pl.BlockSpec(pool_w.shape, lambda c: (0, 0, 0)),
            pl.BlockSpec((1, dp), lambda c: (0, 0)),
        ],
        out_specs=[pl.BlockSpec((tc, dp), lambda c: (c, 0))] * 2,
        out_shape=[jax.ShapeDtypeStruct((t, dp), BF16)] * 2,
        scratch_shapes=[pltpu.VMEM((tc, dp), F32)] * 4,
        args=(proj, pool_w, pool_scale))[0]


def _branch_mix(y_lru, y_pool, w_lru_up, w_pool_upb, proj, b_gate, ga_block, gb_block, carry=None):
    t, d = y_lru.shape
    dp = y_pool.shape[1]
    bw = w_pool_upb.shape[2]
    tt, tn = _tile(t, 1024), 512
    nj = d // tn

    def body(yl_ref, yp_ref, wl_ref, wp_ref, ga_ref, gb_ref, ba_ref, bb_ref, bra_ref, brb_ref, mix_ref):
        br_a = _dot_nn(yl_ref[...], wl_ref[...])
        wp = jnp.concatenate([wp_ref[b] for b in range(tn // bw)], axis=1)
        br_b = _dot_nn(yp_ref[...], wp)
        bra_ref[...] = br_a.astype(BF16)
        brb_ref[...] = br_b.astype(BF16)
        ga = _sig(ga_ref[...].astype(F32) + ba_ref[...])
        gb = _sig(gb_ref[...].astype(F32) + bb_ref[...])
        mix_ref[...] = (ga * br_a + gb * br_b).astype(BF16)

    out = pl.BlockSpec((tt, tn), lambda j, i: (i, j))
    return _call(
        body, name="branch_mix", grid=(nj, t // tt),
        in_specs=[
            pl.BlockSpec((tt, d), lambda j, i: (i, 0)),
            pl.BlockSpec((tt, dp), lambda j, i: (i, 0)),
            pl.BlockSpec((d, tn), lambda j, i: (0, j)),
            pl.BlockSpec((tn // bw, dp, bw), lambda j, i: (j, 0, 0)),
            pl.BlockSpec((tt, tn), lambda j, i: (i, ga_block + j)),
            pl.BlockSpec((tt, tn), lambda j, i: (i, gb_block + j)),
            pl.BlockSpec((1, tn), lambda j, i: (0, j)),
            pl.BlockSpec((1, tn), lambda j, i: (0, nj + j)),
        ],
        out_specs=[out, out, out],
        out_shape=[jax.ShapeDtypeStruct((t, d), BF16)] * 3,
        args=(y_lru, y_pool, w_lru_up, w_pool_upb, proj, proj, b_gate, b_gate), carry=carry)


def _wo_norm(mix, w_o, x, g2, g3, carry=None):
    t, d = x.shape
    tt = _tile(t, 512)

    def body(mix_ref, w_ref, x_ref, g2_ref, g3_ref, m_ref, x2_ref, h3_ref):
        m = _dot_nn(mix_ref[...], w_ref[...])
        m_ref[...] = m
        mhat, _ = _rms_hat(m)
        x2 = x_ref[...] + mhat * g2_ref[...]
        x2_ref[...] = x2
        xhat, _ = _rms_hat(x2)
        h3_ref[...] = (xhat * g3_ref[...]).astype(BF16)

    row = pl.BlockSpec((tt, d), lambda i: (i, 0))
    vec = pl.BlockSpec((1, d), lambda i: (0, 0))
    return _call(
        body, name="wo_norm", grid=(t // tt,),
        in_specs=[row, pl.BlockSpec((d, d), lambda i: (0, 0)), row, vec, vec],
        out_specs=[row, row, row],
        out_shape=[
            jax.ShapeDtypeStruct((t, d), F32),
            jax.ShapeDtypeStruct((t, d), F32),
            jax.ShapeDtypeStruct((t, d), BF16),
        ],
        args=(mix, w_o, x, g2, g3), carry=carry)


def _ff1(h3, w_ff1b, carry=None):
    t, d = h3.shape
    nb, _, tn = w_ff1b.shape
    tt = _tile(t, 2048)

    def body(h_ref, w_ref, rf_ref):
        rf_ref[...] = jnp.maximum(_dot_nn(h_ref[...], w_ref[...]), 0.0).astype(BF16)

    out = pl.BlockSpec((tt, tn), lambda i, j: (i, j))
    return _call(
        body, name="ff1", grid=(t // tt, nb),
        in_specs=[pl.BlockSpec((tt, d), lambda i, j: (i, 0)), pl.BlockSpec((None, d, tn), lambda i, j: (j, 0, 0))],
        out_specs=[out],
        out_shape=[jax.ShapeDtypeStruct((t, nb * tn), BF16)],
        args=(h3, w_ff1b), carry=carry)


def _ff2_loss(rf, w_ff2, x2, g4, target):
    t, k = rf.shape
    d = x2.shape[1]
    tt, tk = _tile(t, 1024), _tile(k, 1024)
    nk = k // tk

    def body(a_ref, w_ref, x2_hbm, g_ref, tg_hbm, dy_ref, df_ref, dg_ref, loss_ref, acc, x2_ref, tg_ref,
             late_sems):
        i, kk = pl.program_id(0), pl.program_id(1)
        late = _late_copies(i, tt, [(x2_hbm, x2_ref), (tg_hbm, tg_ref)], late_sems)

        @pl.when(kk == 0)
        def _():
            acc[...] = jnp.zeros_like(acc)
            for cp in late:
                cp.start()

        @pl.when((i == 0) & (kk == 0))
        def _():
            dg_ref[...] = jnp.zeros_like(dg_ref)
            loss_ref[...] = jnp.zeros_like(loss_ref)

        rf_tile = a_ref[...]
        acc[...] += _dot_nn(rf_tile * rf_tile, w_ref[...])

        @pl.when(kk == nk - 1)
        def _():
            for cp in late:
                cp.wait()

            def tail(rows):
                fhat, r = _rms_hat(acc[rows, :])
                g = g_ref[...]
                e = x2_ref[rows, :] + fhat * g - tg_ref[rows, :]
                loss_ref[...] += 0.5 * jnp.sum(jnp.mean(e * e, axis=-1, keepdims=True))
                dy = e * (1.0 / d)
                dy_ref[rows, :] = dy.astype(BF16)
                df, dg = _rms_bwd(dy, fhat, r, g)
                df_ref[rows, :] = df.astype(BF16)
                dg_ref[...] += dg

            _row_chunks(tt, tail)

    row = pl.BlockSpec((tt, d), lambda i, kk: (i, 0))
    vec = pl.BlockSpec((1, d), lambda i, kk: (0, 0))
    return _call(
        body, name="ff2_loss", grid=(t // tt, nk),
        in_specs=[
            pl.BlockSpec((tt, tk), lambda i, kk: (i, kk)),
            pl.BlockSpec((tk, d), lambda i, kk: (kk, 0)),
            ANY, vec, ANY,
        ],
        out_specs=[row, row, vec, pl.BlockSpec((1, 128), lambda i, kk: (0, 0))],
        out_shape=[
            jax.ShapeDtypeStruct((t, d), BF16),
            jax.ShapeDtypeStruct((t, d), BF16),
            jax.ShapeDtypeStruct((1, d), F32),
            jax.ShapeDtypeStruct((1, 128), F32),
        ],
        scratch_shapes=[pltpu.VMEM((tt, d), F32), pltpu.VMEM((tt, d), x2.dtype), pltpu.VMEM((tt, d), target.dtype),
                        pltpu.SemaphoreType.DMA((2,))],
        args=(rf, w_ff2, x2, g4, target))[0]


def _ff2_bwd(df, w_ff2, rf, carry=None):
    t, d = df.shape
    n = w_ff2.shape[0]
    tt, tn = _tile(t, 2048), _tile(n, 512)

    def body(df_ref, w_ref, rf_ref, out_ref):
        d_act = _dot_nt(df_ref[...], w_ref[...])
        out_ref[...] = (d_act * (2.0 * rf_ref[...].astype(F32))).astype(BF16)

    blk = pl.BlockSpec((tt, tn), lambda i, j: (i, j))
    return _call(
        body, name="ff2_bwd", grid=(t // tt, n // tn),
        in_specs=[pl.BlockSpec((tt, d), lambda i, j: (i, 0)), pl.BlockSpec((tn, d), lambda i, j: (j, 0)), blk],
        out_specs=[blk],
        out_shape=[jax.ShapeDtypeStruct((t, n), BF16)],
        args=(df, w_ff2, rf), carry=carry)


def _wgrad(a, b, name, prev=None, row_off=0, rows=None, carry=None, square_a=False):
    t, m = a.shape
    n = b.shape[1]
    rows = m if rows is None else rows
    tm, tk = _tile(m, 512), _tile(t, 2048)
    nk = t // tk
    assert row_off % tm == 0
    off = row_off // tm

    def body(*refs):
        a_ref, b_ref = refs[0], refs[1]
        o32_ref, o16_ref, acc = refs[-3], refs[-2], refs[-1]
        kk = pl.program_id(1)

        @pl.when(kk == 0)
        def _():
            acc[...] = jnp.zeros_like(acc)

        a_tile = a_ref[...]
        acc[...] += _dot_tn(a_tile * a_tile if square_a else a_tile, b_ref[...])

        @pl.when(kk == nk - 1)
        def _():
            o32_ref[...] = acc[...]
            o16_ref[...] = acc[...].astype(BF16)

    in_specs = [pl.BlockSpec((tk, tm), lambda i, kk: (kk, i)), pl.BlockSpec((tk, n), lambda i, kk: (kk, 0))]
    args = [a, b]
    aliases = {}
    if prev is not None:
        in_specs += [ANY, ANY]
        args += list(prev)
        aliases = {2: 0, 3: 1}
    out = pl.BlockSpec((tm, n), lambda i, kk: (off + i, 0))
    return _call(
        body, name=name, grid=(m // tm, nk),
        in_specs=in_specs, out_specs=[out, out],
        out_shape=[jax.ShapeDtypeStruct((rows, n), F32), jax.ShapeDtypeStruct((rows, n), BF16)],
        scratch_shapes=[pltpu.VMEM((tm, n), F32)],
        aliases=aliases, args=args, carry=carry)


def _wgrad_parts(parts, b, name, carry=None):
    t, n = b.shape
    tm = 512
    bounds = []
    lo = 0
    for part in parts:
        assert part.shape[0] == t and part.shape[1] % tm == 0
        bounds.append((lo, lo + part.shape[1] // tm))
        lo += part.shape[1] // tm
    nm = lo
    np_ = len(parts)

    def body(*refs):
        p_refs, b_ref, o32_ref, o16_ref = refs[:np_], refs[np_], refs[np_ + 1], refs[np_ + 2]
        i = pl.program_id(0)
        for (lo_p, hi_p), p_ref in zip(bounds, p_refs):
            @pl.when((i >= lo_p) & (i < hi_p))
            def _(p_ref=p_ref):
                res = _dot_tn(p_ref[...], b_ref[...])
                o32_ref[...] = res
                o16_ref[...] = res.astype(BF16)

    def part_spec(lo_p, hi_p):
        return pl.BlockSpec((t, tm), lambda i: (0, jnp.clip(i - lo_p, 0, hi_p - lo_p - 1)))

    out = pl.BlockSpec((tm, n), lambda i: (i, 0))
    return _call(
        body, name=name, grid=(nm,),
        in_specs=[part_spec(lo_p, hi_p) for lo_p, hi_p in bounds] + [pl.BlockSpec((t, n), lambda i: (0, 0))],
        out_specs=[out, out],
        out_shape=[jax.ShapeDtypeStruct((nm * tm, n), F32), jax.ShapeDtypeStruct((nm * tm, n), BF16)],
        args=(*parts, b), carry=carry)


def _wgrad_cols(a, b, bw, tn, name, carry=None):
    t, m = a.shape
    n = b.shape[1]
    per_step = tn // bw

    def body(a_ref, b_ref, o32_ref, o16_ref):
        res = _dot_tn(a_ref[...], b_ref[...])
        for blk in range(per_step):
            part = res[:, blk * bw:(blk + 1) * bw]
            o32_ref[blk] = part
            o16_ref[blk] = part.astype(BF16)

    out = pl.BlockSpec((per_step, m, bw), lambda j: (j, 0, 0))
    in_specs = [pl.BlockSpec((t, m), lambda j: (0, 0)), pl.BlockSpec((t, tn), lambda j: (0, j))]
    out_shape = [jax.ShapeDtypeStruct((n // bw, m, bw), F32), jax.ShapeDtypeStruct((n // bw, m, bw), BF16)]
    if carry is None:
        return _call(body, name=name, grid=(n // tn,), in_specs=in_specs, out_specs=[out, out],
                     out_shape=out_shape, args=(a, b))
    assert getattr(carry, "middle", None) is None
    c_in, c_out = len(carry.ins), len(carry.out_shapes)

    def whole(a_hbm, b_hbm, *refs):
        cins, o_hbm = refs[:c_in], refs[c_in:c_in + 2]
        couts, csems = refs[c_in + 2:c_in + 2 + c_out], refs[c_in + 2 + c_out:]
        _handshake(carry.peers)
        carry.start(cins, couts, csems)
        pltpu.emit_pipeline(body, grid=(n // tn,), in_specs=in_specs, out_specs=[out, out])(a_hbm, b_hbm, *o_hbm)
        carry.finish(cins, couts, csems)

    outs = pl.pallas_call(
        whole, name=name,
        in_specs=[ANY] * (2 + c_in), out_specs=[ANY] * (2 + c_out),
        out_shape=out_shape + list(carry.out_shapes), scratch_shapes=list(carry.sems),
        input_output_aliases={2 + ci: 2 + co for ci, co in getattr(carry, "aliases", [])},
        compiler_params=pltpu.CompilerParams(
            vmem_limit_bytes=V7X_VMEM_LIMIT_BYTES, collective_id=COLLECTIVE_ID[carry.peers]),
    )(*_in_hbm([a, b]), *_in_hbm(carry.ins))
    return list(outs[:2]), list(outs[2:])


def _ff1_bwd_norms(d_f1, w_ff1b, dy, x2, g3, m, g2, carry=None):
    t, k = d_f1.shape
    d = x2.shape[1]
    assert dy.dtype == BF16 and x2.dtype == F32
    bw = w_ff1b.shape[2]
    per_step = 2
    tt, tk = _tile(t, 1024), per_step * bw
    nk = k // tk

    def body(a_ref, w_ref, dy_hbm, x2_hbm, g3_ref, m_hbm, g2_ref, dx2_hbm, dm_hbm, dg3_ref, dg2_ref, acc,
             dy_ref, x2_ref, m_ref, late_sems, out_sems):
        i, kk = pl.program_id(0), pl.program_id(1)
        late = _late_copies(i, tt, [(dy_hbm, dy_ref), (x2_hbm, x2_ref), (m_hbm, m_ref)], late_sems)

        @pl.when(kk == 0)
        def _():
            acc[...] = jnp.zeros_like(acc)
            for cp in late:
                cp.start()

        @pl.when((i == 0) & (kk == 0))
        def _():
            dg3_ref[...] = jnp.zeros_like(dg3_ref)
            dg2_ref[...] = jnp.zeros_like(dg2_ref)

        a_tile = a_ref[...]
        for b in range(per_step):
            acc[...] += _dot_nt(a_tile[:, b * bw:(b + 1) * bw], w_ref[b])

        @pl.when(kk == nk - 1)
        def _():
            for cp in late:
                cp.wait()

            def tail(rows):
                xhat, r3 = _rms_hat(x2_ref[rows, :])
                dx, dg3 = _rms_bwd(acc[rows, :], xhat, r3, g3_ref[...])
                dx2 = dy_ref[rows, :].astype(F32) + dx
                x2_ref[rows, :] = dx2
                dg3_ref[...] += dg3
                mhat, r2 = _rms_hat(m_ref[rows, :])
                dm, dg2 = _rms_bwd(dx2, mhat, r2, g2_ref[...])
                dy_ref[rows, :] = dm.astype(BF16)
                dg2_ref[...] += dg2

            _row_chunks(tt, tail)
            tile = pl.ds(pl.multiple_of(i * tt, tt), tt)
            outs = [pltpu.make_async_copy(x2_ref, dx2_hbm.at[tile], out_sems.at[0]),
                    pltpu.make_async_copy(dy_ref, dm_hbm.at[tile], out_sems.at[1])]
            for cp in outs:
                cp.start()
            for cp in outs:
                cp.wait()

    vec = pl.BlockSpec((1, d), lambda i, kk: (0, 0))
    return _call(
        body, name="ff1_bwd_norms", grid=(t // tt, nk),
        in_specs=[
            pl.BlockSpec((tt, tk), lambda i, kk: (i, kk)),
            pl.BlockSpec((per_step, d, bw), lambda i, kk: (kk, 0, 0)),
            ANY, ANY, vec, ANY, vec,
        ],
        out_specs=[ANY, ANY, vec, vec],
        out_shape=[
            jax.ShapeDtypeStruct((t, d), F32),
            jax.ShapeDtypeStruct((t, d), BF16),
            jax.ShapeDtypeStruct((1, d), F32),
            jax.ShapeDtypeStruct((1, d), F32),
        ],
        scratch_shapes=[pltpu.VMEM((tt, d), F32), pltpu.VMEM((tt, d), dy.dtype), pltpu.VMEM((tt, d), F32),
                        pltpu.VMEM((tt, d), F32), pltpu.SemaphoreType.DMA((3,)), pltpu.SemaphoreType.DMA((2,))],
        args=(d_f1, w_ff1b, dy, x2, g3, m, g2), carry=carry)


def _wo_bwd_mix(dm, w_o, br_a, br_b, proj, b_gate, ga_block, gb_block, carry=None):
    t, d = dm.shape
    tt, tn = _tile(t, 1024), 512
    nj = d // tn

    def body(dm_ref, w_ref, bra_ref, brb_ref, ga_ref, gb_ref, ba_ref, bb_ref,
             dbra_ref, dbrb_ref, dga_ref, dgb_ref, dba_ref, dbb_ref):
        i = pl.program_id(1)

        @pl.when(i == 0)
        def _():
            dba_ref[...] = jnp.zeros_like(dba_ref)
            dbb_ref[...] = jnp.zeros_like(dbb_ref)

        d_mix = _dot_nt(dm_ref[...], w_ref[...])
        ga = _sig(ga_ref[...].astype(F32) + ba_ref[...])
        gb = _sig(gb_ref[...].astype(F32) + bb_ref[...])
        dbra_ref[...] = (d_mix * ga).astype(BF16)
        dbrb_ref[...] = (d_mix * gb).astype(BF16)
        dga = d_mix * bra_ref[...].astype(F32) * (ga * (1.0 - ga))
        dgb = d_mix * brb_ref[...].astype(F32) * (gb * (1.0 - gb))
        dga_ref[...] = dga.astype(BF16)
        dgb_ref[...] = dgb.astype(BF16)
        dba_ref[...] += jnp.sum(dga, axis=0, keepdims=True)
        dbb_ref[...] += jnp.sum(dgb, axis=0, keepdims=True)

    blk = pl.BlockSpec((tt, tn), lambda j, i: (i, j))
    vec = pl.BlockSpec((1, tn), lambda j, i: (0, j))
    return _call(
        body, name="wo_bwd_mix", grid=(nj, t // tt),
        in_specs=[
            pl.BlockSpec((tt, d), lambda j, i: (i, 0)),
            pl.BlockSpec((tn, d), lambda j, i: (j, 0)),
            blk, blk,
            pl.BlockSpec((tt, tn), lambda j, i: (i, ga_block + j)),
            pl.BlockSpec((tt, tn), lambda j, i: (i, gb_block + j)),
            vec,
            pl.BlockSpec((1, tn), lambda j, i: (0, nj + j)),
        ],
        out_specs=[blk, blk, blk, blk, vec, vec],
        out_shape=[jax.ShapeDtypeStruct((t, d), BF16)] * 4 + [jax.ShapeDtypeStruct((1, d), F32)] * 2,
        args=(dm, w_o, br_a, br_b, proj, proj, b_gate, b_gate), carry=carry)


def _lru_up_bwd(d_br_a, w_lru_up, proj, h, g_block, carry=None):
    t, d = d_br_a.shape
    tt, tn = _tile(t, 1024), 512

    def body(a_ref, w_ref, g_ref, h_ref, dh_ref, dg_ref):
        d_y = _dot_nt(a_ref[...], w_ref[...])
        gel, gel_grad = _gelu_and_grad(g_ref[...].astype(F32))
        dh_ref[...] = d_y * gel
        dg_ref[...] = (d_y * h_ref[...] * gel_grad).astype(BF16)

    blk = pl.BlockSpec((tt, tn), lambda i, j: (i, j))
    return _call(
        body, name="lru_up_bwd", grid=(t // tt, d // tn),
        in_specs=[
            pl.BlockSpec((tt, d), lambda i, j: (i, 0)),
            pl.BlockSpec((tn, d), lambda i, j: (j, 0)),
            pl.BlockSpec((tt, tn), lambda i, j: (i, g_block + j)),
            blk,
        ],
        out_specs=[blk, blk],
        out_shape=[jax.ShapeDtypeStruct((t, d), F32), jax.ShapeDtypeStruct((t, d), BF16)],
        args=(d_br_a, w_lru_up, proj, h), carry=carry)


def _lru_bwd(dh, h, saved, proj, conv_w, w_a, w_x, lam, carry=None):
    t, dr = dh.shape
    cb = LRU_CB
    hd = LRU_HEAD_DIM
    per = cb // hd
    tc = _tile(t, 256)
    ncb, ntc = dr // cb, t // tc

    def body(dh_ref, h_ref, hp_ref, saved_ref, xp_ref, cw_ref, wa_ref, wx_ref,
             lam_ref, dxp_ref, dwa_ref, dba_ref, dwx_ref, dbx_ref, dlam_ref, dcw_ref, dcb_ref,
             nextd_s, anext_s, gnext_s, tmp_s, wa_s, wx_s):
        c = pl.program_id(1)
        rc = ntc - 1 - c

        @pl.when(c == 0)
        def _():
            nextd_s[...] = jnp.zeros_like(nextd_s)
            anext_s[...] = jnp.zeros_like(anext_s)
            gnext_s[...] = jnp.zeros_like(gnext_s)
            for ref in (dwa_ref, dba_ref, dwx_ref, dbx_ref, dlam_ref, dcw_ref, dcb_ref):
                ref[...] = jnp.zeros_like(ref)
            _fill_block_diag(wa_ref, wa_s)
            _fill_block_diag(wx_ref, wx_s)

        xc, r, i, a, mult = [saved_ref[:, k * cb:(k + 1) * cb] for k in range(N_LRU_SAVED)]
        wa, wx, lam = wa_s[...], wx_s[...], lam_ref[...]
        xcb = xc.astype(BF16)
        sp = _softplus_neg(lam)
        row = lax.broadcasted_iota(jnp.int32, xc.shape, 0)
        h = h_ref[...]
        hp = jnp.where(rc == 0, 0.0, hp_ref[...])
        hprev = jnp.where(row >= 1, pltpu.roll(h, 1, 0), pltpu.roll(hp, 1, 0))

        def up(v, nv, j):
            return jnp.where(row < tc - j, pltpu.roll(v, tc - j, 0), nv)

        av, bv = _scan_rows(up(a, anext_s[...], 1), dh_ref[...], reverse=True)
        gt = av * gnext_s[...] + bv
        tmp_s[...] = gt
        gnext_s[...] = tmp_s[0:1, :]
        tmp_s[...] = a
        anext_s[...] = tmp_s[0:1, :]

        da = gt * hprev
        ixc = i * xc
        d_mult = gt * ixc
        d_i = gt * mult * xc
        d_xc = gt * mult * i
        d_log_a = da * a - d_mult * (a * a) / mult
        d_pre_r = (d_log_a * ((-LRU_C) * sp)) * (r * (1.0 - r))
        d_pre_i = d_i * (i * (1.0 - i))
        d_sp = jnp.sum(d_log_a * ((-LRU_C) * r), axis=0, keepdims=True)
        dlam_ref[...] += d_sp * (-1.0 / (1.0 + jnp.exp(lam)))
        dpr = d_pre_r.astype(BF16)
        dpi = d_pre_i.astype(BF16)
        dba_ref[...] += jnp.sum(d_pre_r, axis=0, keepdims=True)
        dbx_ref[...] += jnp.sum(d_pre_i, axis=0, keepdims=True)
        pa = _dot_tn(xcb, dpr)
        px = _dot_tn(xcb, dpi)
        for k in range(per):
            dwa_ref[k] += pa[k * hd:(k + 1) * hd, k * hd:(k + 1) * hd]
            dwx_ref[k] += px[k * hd:(k + 1) * hd, k * hd:(k + 1) * hd]
        d_xc = d_xc + _dot_nt(dpr, wa) + _dot_nt(dpi, wx)

        nxt = nextd_s[...]
        xp = xp_ref[...].astype(F32)
        dxp = cw_ref[3:4, :] * d_xc
        dcw_ref[3:4, :] += jnp.sum(xp * d_xc, axis=0, keepdims=True)
        for j in (1, 2, 3):
            uj = up(d_xc, pltpu.roll(nxt, tc - j, 0), j)
            dxp = dxp + cw_ref[3 - j:4 - j, :] * uj
            dcw_ref[3 - j:4 - j, :] += jnp.sum(xp * uj, axis=0, keepdims=True)
        dcb_ref[...] += jnp.sum(d_xc, axis=0, keepdims=True)
        nextd_s[...] = d_xc
        dxp_ref[...] = dxp.astype(BF16)

    vec = pl.BlockSpec((1, cb), lambda j, c: (0, j))
    blk = pl.BlockSpec((tc, cb), lambda j, c: (ntc - 1 - c, j))
    mat = pl.BlockSpec((per, hd, hd), lambda j, c: (j, 0, 0))
    cwb = pl.BlockSpec((4, cb), lambda j, c: (0, j))
    return _call(
        body, name="lru_bwd", grid=(ncb, ntc),
        in_specs=[
            blk, blk,
            pl.BlockSpec((tc, cb), lambda j, c: (jnp.maximum(ntc - 2 - c, 0), j)),
            pl.BlockSpec((tc, N_LRU_SAVED * cb), lambda j, c: (ntc - 1 - c, j)),
            blk, cwb, mat, mat, vec,
        ],
        out_specs=[blk, mat, vec, mat, vec, vec, cwb, vec],
        out_shape=[
            jax.ShapeDtypeStruct((t, dr), BF16),
            jax.ShapeDtypeStruct(w_a.shape, F32),
            jax.ShapeDtypeStruct((1, dr), F32),
            jax.ShapeDtypeStruct(w_x.shape, F32),
            jax.ShapeDtypeStruct((1, dr), F32),
            jax.ShapeDtypeStruct((1, dr), F32),
            jax.ShapeDtypeStruct((4, dr), F32),
            jax.ShapeDtypeStruct((1, dr), F32),
        ],
        scratch_shapes=[
            pltpu.VMEM((tc, cb), F32),
            pltpu.VMEM((1, cb), F32),
            pltpu.VMEM((1, cb), F32),
            pltpu.VMEM((tc, cb), F32),
            pltpu.VMEM((cb, cb), BF16),
            pltpu.VMEM((cb, cb), BF16),
        ],
        args=(dh, h, h, saved, proj, conv_w, w_a, w_x, lam), carry=carry)


def _pool_bwd(d_br_b, w_pool_upb, p, pool_w, pool_scale):
    t, d = d_br_b.shape
    nwb, dp, _ = w_pool_upb.shape
    tc = _tile(t, 256)
    ntc = t // tc
    ng = len(POOL_WINDOWS)

    def body(db_ref, wu_ref, p_ref, w_ref, sc_ref, dx_ref, dw_ref, dsc_ref, nz, n2, n4, n8, dp_s, dy_s):
        c = pl.program_id(0)
        rc = ntc - 1 - c

        @pl.when(c == 0)
        def _():
            for s in (nz, n2, n4, n8):
                s[...] = jnp.zeros_like(s)
            dw_ref[...] = jnp.zeros_like(dw_ref)
            dsc_ref[...] = jnp.zeros_like(dsc_ref)

        wu = jnp.concatenate([wu_ref[b] for b in range(nwb)], axis=1)
        dy_s[...] = _dot_nt(db_ref[...], wu)
        for g in range(ng):
            sl = slice(g * POOL_GROUP_DIM, (g + 1) * POOL_GROUP_DIM)
            pg = p_ref[:, sl]
            dyg = dy_s[:, sl]
            wg = w_ref[g].astype(BF16)
            q = _dot_nn(pg, wg)
            dsc_ref[:, sl] += jnp.sum(dyg * q, axis=0, keepdims=True)
            dpw = (dyg * sc_ref[:, sl]).astype(BF16)
            dw_ref[g] += _dot_tn(pg, dpw)
            dp_s[:, sl] = _dot_nt(dpw, wg)

        dpv = dp_s[...]
        row = lax.broadcasted_iota(jnp.int32, dpv.shape, 0)
        col = lax.broadcasted_iota(jnp.int32, dpv.shape, 1)
        win = _pool_select(col, POOL_WINDOWS)
        cnt = jnp.minimum(rc * tc + row + 1, win).astype(F32)
        z = dpv / cnt

        def up(v, nv, j):
            return jnp.where(row < tc - j, pltpu.roll(v, tc - j, 0), pltpu.roll(nv[...], tc - j, 0))

        u2 = z + up(z, nz, 1)
        u4 = u2 + up(u2, n2, 2)
        u8 = u4 + up(u4, n4, 4)
        u16 = u8 + up(u8, n8, 8)
        nz[...] = z
        n2[...] = u2
        n4[...] = u4
        n8[...] = u8
        dx_ref[...] = (_pool_select(col, (u2, u4, u8, u16)) - dpv).astype(BF16)

    blk = pl.BlockSpec((tc, dp), lambda c: (ntc - 1 - c, 0))
    full_w = pl.BlockSpec(pool_w.shape, lambda c: (0, 0, 0))
    vec = pl.BlockSpec((1, dp), lambda c: (0, 0))
    return _call(
        body, name="pool_bwd", grid=(ntc,),
        in_specs=[pl.BlockSpec((tc, d), lambda c: (ntc - 1 - c, 0)),
                  pl.BlockSpec(w_pool_upb.shape, lambda c: (0, 0, 0)), blk, full_w, vec],
        out_specs=[blk, full_w, vec],
        out_shape=[
            jax.ShapeDtypeStruct((t, dp), BF16),
            jax.ShapeDtypeStruct(pool_w.shape, F32),
            jax.ShapeDtypeStruct((1, dp), F32),
        ],
        scratch_shapes=[pltpu.VMEM((tc, dp), F32)] * 6,
        args=(d_br_b, w_pool_upb, p, pool_w, pool_scale))[0]


def _win_bwd_norm(parts, w_int, dx2, x, g1, carry=None):
    t, d = x.shape
    tk = 512
    tt = _tile(t, 1024)
    bounds = []
    k0 = 0
    for part in parts:
        assert part.shape[1] % tk == 0
        bounds.append((k0, k0 + part.shape[1] // tk))
        k0 += part.shape[1] // tk
    nk = k0
    assert nk * tk == w_int.shape[0]
    np_ = len(parts)

    def body(*refs):
        p_refs = refs[:np_]
        w_ref, dx2_hbm, x_hbm, g_ref, gx_hbm, dg_ref, acc, dx2_ref, x_ref, late_sems, out_sem = refs[np_:]
        i, kk = pl.program_id(0), pl.program_id(1)
        late = _late_copies(i, tt, [(dx2_hbm, dx2_ref), (x_hbm, x_ref)], late_sems)

        @pl.when(kk == 0)
        def _():
            acc[...] = jnp.zeros_like(acc)
            for cp in late:
                cp.start()

        @pl.when((i == 0) & (kk == 0))
        def _():
            dg_ref[...] = jnp.zeros_like(dg_ref)

        for (lo, hi), p_ref in zip(bounds, p_refs):
            @pl.when((kk >= lo) & (kk < hi))
            def _(p_ref=p_ref):
                acc[...] += _dot_nn(p_ref[...], w_ref[...])

        @pl.when(kk == nk - 1)
        def _():
            for cp in late:
                cp.wait()

            def tail(rows):
                xhat, r = _rms_hat(x_ref[rows, :])
                dx, dg = _rms_bwd(acc[rows, :], xhat, r, g_ref[...])
                dx2_ref[rows, :] = dx2_ref[rows, :] + dx
                dg_ref[...] += dg

            _row_chunks(tt, tail)
            out = pltpu.make_async_copy(
                dx2_ref, gx_hbm.at[pl.ds(pl.multiple_of(i * tt, tt), tt)], out_sem.at[0])
            out.start()
            out.wait()

    def part_spec(lo, hi):
        return pl.BlockSpec((tt, tk), lambda i, kk: (i, jnp.clip(kk - lo, 0, hi - lo - 1)))

    vec = pl.BlockSpec((1, d), lambda i, kk: (0, 0))
    return _call(
        body, name="win_bwd_norm", grid=(t // tt, nk),
        in_specs=[part_spec(lo, hi) for lo, hi in bounds]
        + [pl.BlockSpec((tk, d), lambda i, kk: (kk, 0)), ANY, ANY, vec],
        out_specs=[ANY, vec],
        out_shape=[jax.ShapeDtypeStruct((t, d), F32), jax.ShapeDtypeStruct((1, d), F32)],
        scratch_shapes=[pltpu.VMEM((tt, d), F32), pltpu.VMEM((tt, d), F32), pltpu.VMEM((tt, d), F32),
                        pltpu.SemaphoreType.DMA((2,)), pltpu.SemaphoreType.DMA((1,))],
        args=(*parts, w_int, dx2, x, g1), carry=carry)


def _adam_math(w, g, m, v):
    m = ADAM_B1 * m + (1.0 - ADAM_B1) * g
    v = ADAM_B2 * v + (1.0 - ADAM_B2) * (g * g)
    m_hat = m / (1.0 - ADAM_B1 ** ADAM_STEP)
    v_hat = v / (1.0 - ADAM_B2 ** ADAM_STEP)
    delta = -ADAM_LR * (m_hat / (jnp.sqrt(v_hat) + ADAM_EPS) + ADAM_WD * w)
    return delta, m, v


def _adamw_big(ws, gs, ms, vs):
    n = len(ws)
    nb = 4
    pair = [isinstance(g, tuple) for g in gs]

    def body(*refs):
        p = 0
        ins = []
        for a in range(n):
            k = 5 if pair[a] else 4
            ins.append(refs[p:p + k])
            p += k
        for a in range(n):
            g_out, d_ref, nm_ref, nv_ref = refs[p + 4 * a:p + 4 * a + 4]
            if pair[a]:
                w_ref, own_ref, recv_ref, m_ref, v_ref = ins[a]
                g = own_ref[...]
                for k in range(3):
                    g = g + recv_ref[k].astype(F32)
            else:
                w_ref, g_ref, m_ref, v_ref = ins[a]
                g = g_ref[...]
            dl, m, v = _adam_math(w_ref[...], g, m_ref[...], v_ref[...])
            g_out[...] = g
            d_ref[...] = dl
            nm_ref[...] = m
            nv_ref[...] = v

    in_specs, out_specs, out_shape, args = [], [], [], []
    for a, (w, g, m, v) in enumerate(zip(ws, gs, ms, vs)):
        rows, cols = w.shape
        blk = pl.BlockSpec((rows // nb, cols), lambda i: (i, 0))
        if pair[a]:
            in_specs += [blk, pl.BlockSpec((None, rows // nb, cols), lambda i: (0, i, 0)),
                         pl.BlockSpec((3, rows // nb, cols), lambda i: (0, i, 0)), blk, blk]
            args += [w, g[0], g[1], m, v]
        else:
            in_specs += [blk] * 4
            args += [w, g, m, v]
        out_specs += [blk] * 4
        out_shape += [jax.ShapeDtypeStruct(w.shape, F32)] * 4
    outs = _call(body, name="adamw_big", grid=(nb,), in_specs=in_specs, out_specs=out_specs,
                 out_shape=out_shape, args=args)[0]
    return [tuple(outs[4 * a:4 * a + 4]) for a in range(n)]


SMALL_ORDER = ("norm_mix_pre", "norm_mix_post", "norm_mlp_pre", "norm_mlp_post", "b_gate", "conv_w", "conv_b",
               "lru_w_a", "lru_b_a", "lru_w_x", "lru_b_x", "lru_lambda", "pool_w", "pool_scale")
VEC_ROW = dict(norm_mix_pre=0, norm_mix_post=1, norm_mlp_pre=2, norm_mlp_post=3, conv_b=6, lru_b_a=7,
               lru_b_x=8, lru_lambda=9)
ROW_B_GATE, ROW_POOL_SCALE, ROW_CONV_W, ROW_LOSS, N_VEC_ROWS = 4, 10, 11, 15, 16


def _adamw_small(vec_parts, g_pool, g_wa, g_wx, me, params):
    d = vec_parts.shape[2]
    names = SMALL_ORDER
    n = len(names)
    cw_cols = params["conv_w"][0].shape[2]

    def body(me_ref, vec_ref, vecc_ref, gp_ref, gwa_ref, gwx_ref, *refs):
        wmv = refs[:3 * n]
        loss_ref = refs[3 * n]
        outs = refs[3 * n + 1:3 * n + 1 + 4 * n]
        vs, vsc = refs[3 * n + 1 + 4 * n:]
        acc, accc = vec_ref[0], vecc_ref[0]
        for k in range(1, N_DEV):
            acc = acc + vec_ref[k]
            accc = accc + vecc_ref[k]
        vs[...] = acc
        vsc[...] = accc
        loss_ref[...] = vs[ROW_LOSS:ROW_LOSS + 1, 0:128]

        def upd(a, g, idx):
            w_ref, m_ref, v_ref = wmv[3 * a:3 * a + 3]
            g_ref, d_ref, nm_ref, nv_ref = outs[4 * a:4 * a + 4]
            dl, m, v = _adam_math(w_ref[idx], g, m_ref[idx], v_ref[idx])
            g_ref[idx] = g
            d_ref[idx] = dl
            nm_ref[idx] = m
            nv_ref[idx] = v

        for a, name in enumerate(names):
            if name in VEC_ROW:
                r = VEC_ROW[name]
                upd(a, vs[r:r + 1, :], (slice(None), slice(None)))
            elif name == "b_gate":
                for half in range(2):
                    r = ROW_B_GATE + half
                    upd(a, vs[r:r + 1, :], (slice(None), slice(half * d, (half + 1) * d)))
            elif name == "pool_scale":
                width = params[name][0].shape[1]
                upd(a, vs[ROW_POOL_SCALE:ROW_POOL_SCALE + 1, 0:width], (slice(None), slice(None)))
            elif name == "conv_w":
                upd(a, vsc[ROW_CONV_W:ROW_CONV_W + 4, :], (0,))
            elif name == "pool_w":
                upd(a, gp_ref[...], (Ellipsis,))
            elif name == "lru_w_a":
                upd(a, gwa_ref[...], (Ellipsis,))
            elif name == "lru_w_x":
                upd(a, gwx_ref[...], (Ellipsis,))
            else:
                raise ValueError(name)

    def whole(shape):
        nd = len(shape)
        return pl.BlockSpec(tuple(shape), lambda i, me_ref: (0,) * nd)

    in_specs = [
        whole(vec_parts.shape),
        pl.BlockSpec((N_DEV, N_VEC_ROWS, cw_cols), lambda i, me_ref: (0, 0, me_ref[0])),
        whole(g_pool.shape), whole(g_wa.shape), whole(g_wx.shape),
    ]
    args = [vec_parts, vec_parts, g_pool, g_wa, g_wx]
    out_specs = [whole((1, 128))]
    out_shape = [jax.ShapeDtypeStruct((1, 128), F32)]
    for name in names:
        for arr in params[name]:
            in_specs.append(whole(arr.shape))
            args.append(arr)
        shp = params[name][0].shape
        out_specs += [whole(shp)] * 4
        out_shape += [jax.ShapeDtypeStruct(shp, F32)] * 4
    grid_spec = pltpu.PrefetchScalarGridSpec(
        num_scalar_prefetch=1, grid=(1,), in_specs=in_specs, out_specs=out_specs,
        scratch_shapes=[pltpu.VMEM((N_VEC_ROWS, d), F32), pltpu.VMEM((N_VEC_ROWS, cw_cols), F32)])
    outs = pl.pallas_call(
        body, name="adamw_small", grid_spec=grid_spec, out_shape=out_shape,
        compiler_params=pltpu.CompilerParams(
            dimension_semantics=("arbitrary",), vmem_limit_bytes=V7X_VMEM_LIMIT_BYTES),
    )(me, *_in_hbm(args))
    return outs[0], {name: tuple(outs[1 + 4 * a:5 + 4 * a]) for a, name in enumerate(names)}


def _rs_sum(fulls, recvs, shard_ids, slot_ids, name):
    n = len(fulls)

    def body(sh_ref, sl_ref, *refs):
        s = pl.program_id(0)
        for a in range(n):
            full_ref, recv_ref = refs[2 * a], refs[2 * a + 1]
            own_ref, send_ref = refs[2 * n + 2 * a], refs[2 * n + 2 * a + 1]
            v = full_ref[...] + recv_ref[...].astype(F32)

            @pl.when(s == 0)
            def _(own_ref=own_ref, v=v):
                own_ref[...] = v

            @pl.when(s > 0)
            def _(send_ref=send_ref, v=v):
                send_ref[...] = v.astype(send_ref.dtype)

    in_specs, out_specs, out_shape, args = [], [], [], []
    for full, recv in zip(fulls, recvs):
        r, rest = recv.shape[1], tuple(recv.shape[2:])
        zeros = (0,) * len(rest)
        in_specs += [
            pl.BlockSpec((r,) + rest, lambda s, sh, sl, zeros=zeros: (sh[s],) + zeros),
            pl.BlockSpec((None, r) + rest, lambda s, sh, sl, zeros=zeros: (sl[s], 0) + zeros),
        ]
        out_specs += [
            pl.BlockSpec((None, r) + rest, lambda s, sh, sl, zeros=zeros: (0, 0) + zeros),
            pl.BlockSpec((None, r) + rest, lambda s, sh, sl, zeros=zeros: (jnp.maximum(s - 1, 0), 0) + zeros),
        ]
        out_shape += [jax.ShapeDtypeStruct((1, r) + rest, F32), jax.ShapeDtypeStruct((3, r) + rest, recv.dtype)]
        args += [full, recv]
    grid_spec = pltpu.PrefetchScalarGridSpec(
        num_scalar_prefetch=2, grid=(4,), in_specs=in_specs, out_specs=out_specs)
    outs = pl.pallas_call(
        body,
        name=name,
        grid_spec=grid_spec,
        out_shape=out_shape,
        compiler_params=pltpu.CompilerParams(
            dimension_semantics=("arbitrary",), vmem_limit_bytes=V7X_VMEM_LIMIT_BYTES),
    )(shard_ids, slot_ids, *_in_hbm(args))
    return [(outs[2 * a], outs[2 * a + 1]) for a in range(n)]


def _finals(pairs, name, carry=None):
    nb = 4
    n = len(pairs)

    def body(*refs):
        for a in range(n):
            own_ref, recv_ref = refs[2 * a], refs[2 * a + 1]
            acc = own_ref[...]
            for k in range(3):
                acc = acc + recv_ref[k].astype(F32)
            refs[2 * n + a][...] = acc

    in_specs, out_specs, out_shape, args = [], [], [], []
    for own, recv in pairs:
        _, rows, cols = own.shape
        in_specs += [pl.BlockSpec((None, rows // nb, cols), lambda i: (0, i, 0)),
                     pl.BlockSpec((3, rows // nb, cols), lambda i: (0, i, 0))]
        args += [own, recv]
        out_specs.append(pl.BlockSpec((rows // nb, cols), lambda i: (i, 0)))
        out_shape.append(jax.ShapeDtypeStruct((rows, cols), F32))
    return _call(body, name=name, grid=(nb,), in_specs=in_specs, out_specs=out_specs,
                 out_shape=out_shape, args=args, carry=carry)


def _rs_sums(fulls_f32, recv1, tag):
    x, y, c = _place()
    qs = jnp.stack([2 * x + y, 2 * (1 - x) + y, 2 * x + (1 - y), 2 * (1 - x) + (1 - y)]).astype(jnp.int32)
    shard_ids = 2 * qs + c
    return _rs_sum(fulls_f32, recv1, shard_ids, qs, "rs_sum_" + tag)


def _rs_level1(fulls_f32, fulls_send, tag):
    recv1 = _run_plan(_rs_sibling_plan(fulls_send), "rs_sibling_" + tag)
    return _rs_sums(fulls_f32, recv1, tag)


def _rows(g):
    return g.reshape(g.shape[0] * g.shape[1], g.shape[2])


def kernel(x, norm_mix_pre, norm_mix_post, norm_mlp_pre, norm_mlp_post, w_in, b_gate, conv_w, conv_b, lru_w_a, lru_b_a, lru_w_x, lru_b_x, lru_lambda, pool_w, pool_scale, w_lru_up, w_pool_up, w_o, w_ff1, w_ff2, loss_target, m_norm_mix_pre, m_norm_mix_post, m_norm_mlp_pre, m_norm_mlp_post, m_w_in, m_b_gate, m_conv_w, m_conv_b, m_lru_w_a, m_lru_b_a, m_lru_w_x, m_lru_b_x, m_lru_lambda, m_pool_w, m_pool_scale, m_w_lru_up, m_w_pool_up, m_w_o, m_w_ff1, m_w_ff2, v_norm_mix_pre, v_norm_mix_post, v_norm_mlp_pre, v_norm_mlp_post, v_w_in, v_b_gate, v_conv_w, v_conv_b, v_lru_w_a, v_lru_b_a, v_lru_w_x, v_lru_b_x, v_lru_lambda, v_pool_w, v_pool_scale, v_w_lru_up, v_w_pool_up, v_w_o, v_w_ff1, v_w_ff2):
    t, d = x.shape[1], x.shape[2]
    d_rnn = conv_b.shape[1]
    d_pool = pool_scale.shape[1]
    per = LRU_CB // LRU_HEAD_DIM
    xi, yi, ci = _place()
    me = 4 * xi + 2 * yi + ci

    x2d = x[0]
    tgt = loss_target[0]

    s_in = w_in[0].T.astype(BF16)
    s_lu = w_lru_up[0].astype(BF16)
    s_pu = w_pool_up[0].astype(BF16)
    s_o = w_o[0].astype(BF16)
    s_f1 = w_ff1[0].astype(BF16)
    s_f2 = w_ff2[0].astype(BF16)
    s_cw = jnp.pad(conv_w[0], ((0, 4), (0, 0)))

    g_in, g_cw = _run_plan(_ag_plan([s_in, s_cw]), "ag_w_in")
    w_int = _rows(g_in)
    conv_w_full = jnp.transpose(g_cw[:, :4, :], (1, 0, 2)).reshape(4, d_rnn)

    wa_bd, wx_bd = lru_w_a[0], lru_w_x[0]
    pw = pool_w[0]
    pw_bf = pw.astype(BF16)

    pool_block = (2 * d_rnn) // d_pool
    ga_block = (2 * d_rnn + d_pool) // 512
    gb_block = ga_block + d // 512
    g_block = d_rnn // 512

    r_f1, r_f2 = s_f1.shape[0], s_f2.shape[0]
    f1_cut = r_f1 // 4
    f2_cut = (3 * r_f2) // 8
    plan = _join([_ag_plan([s_lu, s_pu]), _ag_plan([s_f1], pieces=[(0, f1_cut)])])
    (proj, h1), got = _norm_proj(x2d, norm_mix_pre, w_int, carry=plan)
    (g_lu, g_pu), (g_f1,) = plan.split(got)
    plan = _join([_ag_plan([s_f1], pieces=[(f1_cut, r_f1 - f1_cut)], bufs=[g_f1]), _ag_plan([s_o])])
    (y_lru, h, lru_saved), got = _lru_fwd(
        proj, conv_w_full, conv_b, wa_bd, lru_b_a, wx_bd, lru_b_x, lru_lambda, carry=plan)
    (g_f1,), (g_o,) = plan.split(got)
    w_lu, w_og = _rows(g_lu), _rows(g_o)
    y_pool, p = _pool_fwd(proj, pw_bf, pool_scale, pool_block)
    (br_a, br_b, mix), (g_f2,) = _branch_mix(
        y_lru, y_pool, w_lu, g_pu, proj, b_gate, ga_block, gb_block,
        carry=_ag_plan([s_f2], pieces=[(0, f2_cut)]))
    (m, x2, h3), _ = _wo_norm(mix, w_og, x2d, norm_mix_post, norm_mlp_pre)
    (rf,), (g_f2,) = _ff1(
        h3, g_f1, carry=_ag_plan([s_f2], pieces=[(f2_cut, r_f2 - f2_cut)], bufs=[g_f2]))
    w_f2 = _rows(g_f2)
    dy, df, dg4, loss_part = _ff2_loss(rf, w_f2, x2, norm_mlp_post, tgt)

    (gw_ff2_32, gw_ff2_16), _ = _wgrad(rf, df, "wgrad_ff2", square_a=True)
    (d_f1,), r1_ff2 = _ff2_bwd(df, w_f2, rf, carry=_rs_sibling_plan([gw_ff2_16]))
    ((own_ff2, send_ff2),) = _rs_sums([gw_ff2_32], r1_ff2, "ff2")
    cut2 = (5 * send_ff2.shape[1]) // 16
    (gw_ff1_32, gw_ff1_16), (r2_ff2,) = _wgrad_cols(
        h3, d_f1, s_f1.shape[1], s_f1.shape[1], "wgrad_ff1",
        carry=_rs_chips_plan([send_ff2], pieces=[(0, cut2)]))
    plan = _join([_rs_chips_plan([send_ff2], pieces=[(cut2, send_ff2.shape[1] - cut2)], bufs=[r2_ff2]),
                  _rs_sibling_plan([gw_ff1_16])])
    (dx2, dm, dg3, dg2), got = _ff1_bwd_norms(d_f1, g_f1, dy, x2, norm_mlp_pre, m, norm_mix_post, carry=plan)
    (r2_ff2,), r1_ff1 = plan.split(got)
    ((own_ff1, send_ff1),) = _rs_sums([gw_ff1_32], r1_ff1, "ff1")
    own_ff1, send_ff1 = own_ff1.reshape((1,) + s_f1.shape), send_ff1.reshape((3,) + s_f1.shape)
    cut = send_ff1.shape[1] // 4
    (gw_o_32, gw_o_16), _ = _wgrad(mix, dm, "wgrad_o")
    (d_br_a, d_br_b, p_ga, p_gb, dbg_a, dbg_b), (r2_ff1,) = _wo_bwd_mix(
        dm, w_og, br_a, br_b, proj, b_gate, ga_block, gb_block,
        carry=_rs_chips_plan([send_ff1], pieces=[(0, cut)]))
    (gw_lu_32, gw_lu_16), _ = _wgrad(y_lru, d_br_a, "wgrad_lru_up")
    (gw_pu_32, gw_pu_16), _ = _wgrad_cols(y_pool, d_br_b, s_pu.shape[1], d, "wgrad_pool_up")
    (dh, p_g), r1_mid = _lru_up_bwd(
        d_br_a, w_lu, proj, h, g_block,
        carry=_rs_sibling_plan([gw_o_16, gw_lu_16, gw_pu_16]))
    mid = _rs_sums([gw_o_32, gw_lu_32, gw_pu_32], r1_mid, "mid")
    plan = _join([_rs_chips_plan([send_ff1], pieces=[(cut, send_ff1.shape[1] - cut)], bufs=[r2_ff1]),
                  _rs_chips_plan([mid[0][1]])])
    (p_x, dwa, db_a, dwx, db_x, dlam, dconv_w, dconv_b), got = _lru_bwd(
        dh, h, lru_saved, proj, conv_w_full, wa_bd, wx_bd, lru_lambda, carry=plan)
    (r2_ff1,), (r2_o,) = plan.split(got)
    p_p, dpool_w, dpool_scale = _pool_bwd(d_br_b, g_pu, p, pw, pool_scale)
    parts = [p_x, p_g, p_p, p_ga, p_gb]
    gw_in, (r2_lu, r2_pu) = _wgrad_parts(
        parts, h1, "wgrad_in", carry=_rs_chips_plan([mid[1][1], mid[2][1]]))
    r2_mid = [r2_o, r2_lu, r2_pu]
    tail = _rs_level1([gw_in[0], dpool_w.reshape(N_DEV, -1, POOL_GROUP_DIM), dwa, dwx],
                      [gw_in[1], dpool_w.reshape(N_DEV, -1, POOL_GROUP_DIM), dwa, dwx], "in")
    (grad_x, dg1), r2_tail = _win_bwd_norm(parts, w_int, dx2, x2d, norm_mix_pre,
                                           carry=_rs_chips_plan([s for _, s in tail]))

    def flat2(a):
        return a.reshape(a.shape[0], -1, a.shape[-1])

    fin_small, _ = _finals([
        (flat2(tail[1][0]), flat2(r2_tail[1])), (flat2(tail[2][0]), flat2(r2_tail[2])),
        (flat2(tail[3][0]), flat2(r2_tail[3])),
    ], "rs_finals_small")

    def pad_row(a):
        return jnp.pad(a, ((0, 0), (0, d - a.shape[1])))

    vecs = jnp.concatenate([dg1, dg2, dg3, dg4, dbg_a, dbg_b, dconv_b, db_a, db_x, dlam,
                            pad_row(dpool_scale), dconv_w, pad_row(loss_part)], axis=0)
    assert vecs.shape[0] == N_VEC_ROWS
    vec_parts, g_pool, g_wa, g_wx = _run_plan(_ag_plan([vecs] + fin_small), "ag_tail")

    big_names = ["w_in", "w_lru_up", "w_pool_up", "w_o", "w_ff1", "w_ff2"]
    big_w = [w_in[0].T, w_lru_up[0], w_pool_up[0], w_o[0], w_ff1[0], w_ff2[0]]
    big_g = [(tail[0][0], r2_tail[0]), (mid[1][0], r2_mid[1]),
             (mid[2][0].reshape((1,) + s_pu.shape), r2_mid[2].reshape((3,) + s_pu.shape)),
             (mid[0][0], r2_mid[0]), (own_ff1, r2_ff1), (own_ff2, r2_ff2)]
    big_m = [m_w_in[0].T, m_w_lru_up[0], m_w_pool_up[0], m_w_o[0], m_w_ff1[0], m_w_ff2[0]]
    big_v = [v_w_in[0].T, v_w_lru_up[0], v_w_pool_up[0], v_w_o[0], v_w_ff1[0], v_w_ff2[0]]
    big_out = _adamw_big(big_w, big_g, big_m, big_v)
    big_out[0] = tuple(o.T for o in big_out[0])

    small = dict(
        norm_mix_pre=(norm_mix_pre, m_norm_mix_pre, v_norm_mix_pre),
        norm_mix_post=(norm_mix_post, m_norm_mix_post, v_norm_mix_post),
        norm_mlp_pre=(norm_mlp_pre, m_norm_mlp_pre, v_norm_mlp_pre),
        norm_mlp_post=(norm_mlp_post, m_norm_mlp_post, v_norm_mlp_post),
        b_gate=(b_gate, m_b_gate, v_b_gate), conv_w=(conv_w, m_conv_w, v_conv_w),
        conv_b=(conv_b, m_conv_b, v_conv_b), lru_w_a=(lru_w_a, m_lru_w_a, v_lru_w_a),
        lru_b_a=(lru_b_a, m_lru_b_a, v_lru_b_a), lru_w_x=(lru_w_x, m_lru_w_x, v_lru_w_x),
        lru_b_x=(lru_b_x, m_lru_b_x, v_lru_b_x), lru_lambda=(lru_lambda, m_lru_lambda, v_lru_lambda),
        pool_w=(pool_w, m_pool_w, v_pool_w), pool_scale=(pool_scale, m_pool_scale, v_pool_scale))
    loss_row, small_out = _adamw_small(
        vec_parts, g_pool.reshape(pool_w.shape), g_wa.reshape(lru_w_a.shape), g_wx.reshape(lru_w_x.shape),
        jnp.reshape(me, (1,)).astype(jnp.int32), small)
    grads = {n: o[0] for n, o in small_out.items()}
    delta = {n: o[1] for n, o in small_out.items()}
    new_m = {n: o[2] for n, o in small_out.items()}
    new_v = {n: o[3] for n, o in small_out.items()}

    for name, (g, dl, nm, nv) in zip(big_names, big_out):
        grads[name], delta[name], new_m[name], new_v[name] = g[None], dl[None], nm[None], nv[None]

    loss = loss_row[0, 0]
    order = ["norm_mix_pre", "norm_mix_post", "norm_mlp_pre", "norm_mlp_post", "w_in", "b_gate", "conv_w",
             "conv_b", "lru_w_a", "lru_b_a", "lru_w_x", "lru_b_x", "lru_lambda", "pool_w", "pool_scale",
             "w_lru_up", "w_pool_up", "w_o", "w_ff1", "w_ff2"]
    return (loss, grad_x[None], *[grads[n] for n in order], *[delta[n] for n in order],
            *[new_m[n] for n in order], *[new_v[n] for n in order])
```

```python
import functools
import math
import operator
import types

import jax
import jax.numpy as jnp
from jax import lax
from jax.experimental import pallas as pl
from jax.experimental.pallas import tpu as pltpu

F32 = jnp.float32
BF16 = jnp.bfloat16
NORM_EPS = 1e-6
LRU_C = 8.0
N_LRU_HEADS = 16
LRU_HEAD_DIM = 64
POOL_WINDOWS = (2, 4, 8, 16)
POOL_GROUP_DIM = 128
ADAM_LR = 0.001
ADAM_B1 = 0.9
ADAM_B2 = 0.999
ADAM_EPS = 1e-08
ADAM_WD = 0.01
ADAM_STEP = 10
N_DEV = 8
V7X_VMEM_LIMIT_BYTES = 56 * 1024 * 1024
LRU_CB = 256
MESH = pl.DeviceIdType.MESH
ANY = pl.BlockSpec(memory_space=pl.ANY)


def _tile(n, pref):
    t = min(n, pref)
    assert n % t == 0, (n, pref)
    return t


def _dot_nn(a, b):
    return lax.dot_general(a, b, (((1,), (0,)), ((), ())), preferred_element_type=F32)


def _dot_nt(a, b):
    return lax.dot_general(a, b, (((1,), (1,)), ((), ())), preferred_element_type=F32)


def _dot_tn(a, b):
    return lax.dot_general(a, b, (((0,), (0,)), ((), ())), preferred_element_type=F32)


def _row_chunks(n_rows, fn, chunk=256):
    chunk = min(chunk, n_rows)
    assert n_rows % chunk == 0

    def step(r, carry):
        fn(pl.ds(pl.multiple_of(r * chunk, chunk), chunk))
        return carry

    lax.fori_loop(0, n_rows // chunk, step, 0)


def _late_copies(i, tt, pairs, sems):
    rows = pl.ds(pl.multiple_of(i * tt, tt), tt)
    return [pltpu.make_async_copy(hbm.at[rows], buf, sems.at[j]) for j, (hbm, buf) in enumerate(pairs)]


def _sig(x):
    return 1.0 / (1.0 + jnp.exp(-x))


def _rms_hat(x):
    r = lax.rsqrt(jnp.mean(x * x, axis=-1, keepdims=True) + NORM_EPS)
    return x * r, r


def _rms_bwd(dn, xhat, r, g):
    q = dn * g
    dx = r * (q - xhat * jnp.mean(q * xhat, axis=-1, keepdims=True))
    dg = jnp.sum(dn * xhat, axis=0, keepdims=True)
    return dx, dg


_GELU_K = math.sqrt(2.0 / math.pi)
_GELU_C = 0.044715


def _gelu_and_grad(g):
    t = jnp.tanh(_GELU_K * (g + _GELU_C * g * g * g))
    val = 0.5 * g * (1.0 + t)
    grad = 0.5 * (1.0 + t) + 0.5 * g * (1.0 - t * t) * (_GELU_K * (1.0 + 3.0 * _GELU_C * g * g))
    return val, grad


def _softplus_neg(lam):
    z = -lam
    e = jnp.exp(-jnp.abs(z))
    u = 1.0 + e
    d = u - 1.0
    l1p = jnp.where(d == 0.0, e, jnp.log(u) * (e / jnp.where(d == 0.0, 1.0, d)))
    return jnp.maximum(z, 0.0) + l1p


def _lru_gates(xc, wa, ba, wx, bx, lam):
    xcb = xc.astype(BF16)
    r = _sig(_dot_nn(xcb, wa) + ba)
    i = _sig(_dot_nn(xcb, wx) + bx)
    sp = _softplus_neg(lam)
    log_a = (-LRU_C) * r * sp
    a = jnp.exp(log_a)
    mult = jnp.sqrt(-jnp.tanh(log_a) * (1.0 + a * a))
    return xcb, r, i, sp, log_a, a, mult


def _place():
    return lax.axis_index("x"), lax.axis_index("y"), lax.axis_index("c")


def _ag_plan(shards, pieces=None, bufs=None):
    na = len(shards)
    n_kinds = 7

    def parts(ins, outs, sems):
        send_sems, recv_sems, local_sems = sems
        x, y, c = _place()
        me, sibling = (x, y, c), (x, y, 1 - c)
        x_nb, y_nb, diag = (1 - x, y), (x, 1 - y), (1 - x, 1 - y)
        relay_src = (c * (1 - x) + (1 - c) * x, c * y + (1 - c) * (1 - y))
        relay_dst = (c * x + (1 - c) * (1 - x), c * (1 - y) + (1 - c) * y)

        def own(a):
            return ins[a] if pieces is None else ins[a].at[pl.ds(*pieces[a])]

        def slot(a, px, py, pc):
            idx = 4 * px + 2 * py + pc
            return outs[a].at[idx] if pieces is None else outs[a].at[idx, pl.ds(*pieces[a])]

        def copy(a, k, block, to, src=None):
            return pltpu.make_async_remote_copy(
                src_ref=slot(a, *block) if src is None else src,
                dst_ref=slot(a, *block),
                send_sem=send_sems.at[a * n_kinds + k],
                recv_sem=recv_sems.at[a * n_kinds + k],
                device_id=to,
                device_id_type=MESH,
            )

        mine = [pltpu.make_async_copy(own(a), slot(a, *me), local_sems.at[a]) for a in range(na)]
        first, second, third = [], [], []
        for a in range(na):
            first += [copy(a, 0, me, sibling, src=own(a)), copy(a, 1, me, (*x_nb, c), src=own(a)),
                      copy(a, 2, me, (*y_nb, c), src=own(a))]
            second += [copy(a, 3, (*relay_src, c), (*relay_dst, c)), copy(a, 4, (*x_nb, c), sibling),
                       copy(a, 5, (*y_nb, c), sibling)]
            third.append(copy(a, 6, (*diag, c), sibling))
        return sibling, c, x_nb, y_nb, diag, copy, mine, first, second, third

    def start(ins, outs, sems):
        _, _, _, _, _, _, mine, first, _, _ = parts(ins, outs, sems)
        for cp in mine + first:
            cp.start()

    def middle(ins, outs, sems):
        _, c, x_nb, y_nb, _, copy, _, _, second, _ = parts(ins, outs, sems)
        for a in range(na):
            copy(a, 1, (*x_nb, c), (*x_nb, c)).wait_recv()
            copy(a, 2, (*y_nb, c), (*y_nb, c)).wait_recv()
        for cp in second:
            cp.start()

    def finish(ins, outs, sems):
        sibling, c, x_nb, y_nb, diag, copy, mine, first, second, third = parts(ins, outs, sems)
        for a in range(na):
            copy(a, 3, (*diag, c), (*diag, c)).wait_recv()
            third[a].start()
        for a in range(na):
            copy(a, 0, sibling, sibling).wait_recv()
            copy(a, 4, (*x_nb, 1 - c), sibling).wait_recv()
            copy(a, 5, (*y_nb, 1 - c), sibling).wait_recv()
            copy(a, 6, (*diag, 1 - c), sibling).wait_recv()
        for cp in first + second + third:
            cp.wait_send()
        for cp in mine:
            cp.wait()

    return types.SimpleNamespace(
        ins=list(shards) + list(bufs or []),
        out_shapes=[jax.ShapeDtypeStruct((N_DEV,) + s.shape, s.dtype) for s in shards],
        sems=[pltpu.SemaphoreType.DMA((n_kinds * na,)), pltpu.SemaphoreType.DMA((n_kinds * na,)),
              pltpu.SemaphoreType.DMA((na,))],
        aliases=[(na + a, a) for a in range(na)] if bufs else [],
        peers=frozenset({"sibling", "neighbours"}), start=start, middle=middle, finish=finish)


def _rs_sibling_plan(fulls):
    na = len(fulls)
    rs = [f.shape[0] // N_DEV for f in fulls]

    def copies(ins, outs, sems):
        send_sems, recv_sems = sems
        x, y, c = _place()
        out = []
        for a in range(na):
            for q in range(4):
                shard = 2 * q + (1 - c)
                out.append(pltpu.make_async_remote_copy(
                    src_ref=ins[a].at[pl.ds(shard * rs[a], rs[a])],
                    dst_ref=outs[a].at[q],
                    send_sem=send_sems.at[a * 4 + q],
                    recv_sem=recv_sems.at[a * 4 + q],
                    device_id=(x, y, 1 - c),
                    device_id_type=MESH,
                ))
        return out

    def start(ins, outs, sems):
        for cp in copies(ins, outs, sems):
            cp.start()

    def finish(ins, outs, sems):
        for cp in copies(ins, outs, sems):
            cp.wait()

    return types.SimpleNamespace(
        ins=list(fulls),
        out_shapes=[jax.ShapeDtypeStruct((4, r) + f.shape[1:], f.dtype) for r, f in zip(rs, fulls)],
        sems=[pltpu.SemaphoreType.DMA((4 * na,)), pltpu.SemaphoreType.DMA((4 * na,))],
        peers=frozenset({"sibling"}), start=start, finish=finish)


def _rs_chips_plan(sends, pieces=None, bufs=None):
    na = len(sends)

    def copies(ins, outs, sems):
        send_sems, recv_sems = sems
        x, y, c = _place()
        chips = [(1 - x, y), (x, 1 - y), (1 - x, 1 - y)]
        out = []
        for a in range(na):
            for k, chip in enumerate(chips):
                rows = (k,) if pieces is None else (k, pl.ds(*pieces[a]))
                out.append(pltpu.make_async_remote_copy(
                    src_ref=ins[a].at[rows],
                    dst_ref=outs[a].at[rows],
                    send_sem=send_sems.at[a * 3 + k],
                    recv_sem=recv_sems.at[a * 3 + k],
                    device_id=(*chip, c),
                    device_id_type=MESH,
                ))
        return out

    def start(ins, outs, sems):
        for cp in copies(ins, outs, sems):
            cp.start()

    def finish(ins, outs, sems):
        for cp in copies(ins, outs, sems):
            cp.wait()

    return types.SimpleNamespace(
        ins=list(sends) + list(bufs or []),
        out_shapes=[jax.ShapeDtypeStruct(s.shape, s.dtype) for s in sends],
        sems=[pltpu.SemaphoreType.DMA((3 * na,)), pltpu.SemaphoreType.DMA((3 * na,))],
        aliases=[(na + a, a) for a in range(na)] if bufs else [],
        peers=frozenset({"chips"}), start=start, finish=finish)


def _join(plans):
    ins, outs, sems, aliases, offs = [], [], [], [], []
    for p in plans:
        offs.append((len(ins), len(outs), len(sems)))
        aliases += [(len(ins) + ci, len(outs) + co) for ci, co in getattr(p, "aliases", [])]
        ins += p.ins
        outs += p.out_shapes
        sems += p.sems

    def cut(p, off, i, o, s):
        return (i[off[0]:off[0] + len(p.ins)], o[off[1]:off[1] + len(p.out_shapes)],
                s[off[2]:off[2] + len(p.sems)])

    def start(i, o, s):
        for p, off in zip(plans, offs):
            p.start(*cut(p, off, i, o, s))

    def middle(i, o, s):
        for p, off in zip(plans, offs):
            if getattr(p, "middle", None) is not None:
                p.middle(*cut(p, off, i, o, s))

    def finish(i, o, s):
        for p, off in zip(plans, offs):
            p.finish(*cut(p, off, i, o, s))

    def split(results):
        return [list(results[off[1]:off[1] + len(p.out_shapes)]) for p, off in zip(plans, offs)]

    return types.SimpleNamespace(ins=ins, out_shapes=outs, sems=sems, aliases=aliases,
                                 peers=frozenset().union(*[p.peers for p in plans]),
                                 start=start, middle=middle, finish=finish, split=split)


COLLECTIVE_ID = {frozenset({"sibling"}): 0, frozenset({"chips"}): 1, frozenset({"sibling", "chips"}): 2,
                 frozenset({"sibling", "neighbours"}): 3}


def _handshake(peers):
    x, y, c = _place()
    devs = []
    if "sibling" in peers:
        devs.append((x, y, 1 - c))
    if "neighbours" in peers:
        devs += [(1 - x, y, c), (x, 1 - y, c)]
    if "chips" in peers:
        assert "neighbours" not in peers
        devs += [(1 - x, y, c), (x, 1 - y, c), (1 - x, 1 - y, c)]
    barrier = pltpu.get_barrier_semaphore()
    for dev in devs:
        pl.semaphore_signal(barrier, inc=1, device_id=dev, device_id_type=MESH)
    pl.semaphore_wait(barrier, len(devs))


def _in_hbm(args):
    return [pltpu.with_memory_space_constraint(a, pltpu.HBM) for a in args]


def _run_plan(plan, name):
    n_in, n_out = len(plan.ins), len(plan.out_shapes)

    def body(*refs):
        ins, outs, sems = refs[:n_in], refs[n_in:n_in + n_out], refs[n_in + n_out:]
        _handshake(plan.peers)
        plan.start(ins, outs, sems)
        if getattr(plan, "middle", None) is not None:
            plan.middle(ins, outs, sems)
        plan.finish(ins, outs, sems)

    return pl.pallas_call(
        body,
        name=name,
        in_specs=[ANY] * n_in,
        out_specs=[ANY] * n_out,
        out_shape=plan.out_shapes,
        scratch_shapes=plan.sems,
        input_output_aliases=dict(getattr(plan, "aliases", [])),
        compiler_params=pltpu.CompilerParams(collective_id=COLLECTIVE_ID[plan.peers]),
    )(*_in_hbm(plan.ins))


def _call(body, *, name, grid, in_specs, out_specs, out_shape, args, scratch_shapes=(), aliases=None,
          carry=None):
    n_in, n_out, n_scr = len(in_specs), len(out_shape), len(scratch_shapes)
    params = pltpu.CompilerParams(
        dimension_semantics=("arbitrary",) * len(grid), vmem_limit_bytes=V7X_VMEM_LIMIT_BYTES)
    if carry is None:
        outs = pl.pallas_call(
            body, name=name, grid=grid, in_specs=list(in_specs), out_specs=list(out_specs),
            out_shape=list(out_shape), scratch_shapes=list(scratch_shapes),
            input_output_aliases=aliases or {}, compiler_params=params)(*_in_hbm(args))
        return list(outs), []
    c_in, c_out = len(carry.ins), len(carry.out_shapes)

    def full(*refs):
        p = 0
        ins = refs[p:p + n_in]
        p += n_in
        cins = refs[p:p + c_in]
        p += c_in
        outs = refs[p:p + n_out]
        p += n_out
        couts = refs[p:p + c_out]
        p += c_out
        scr = refs[p:p + n_scr]
        csems = refs[p + n_scr:]
        ids = [pl.program_id(a) for a in range(len(grid))]
        first = functools.reduce(operator.and_, [i == 0 for i in ids])
        last = functools.reduce(operator.and_, [i == g - 1 for i, g in zip(ids, grid)])

        @pl.when(first)
        def _():
            _handshake(carry.peers)
            carry.start(cins, couts, csems)

        if getattr(carry, "middle", None) is not None:
            n_steps = math.prod(grid)
            flat = functools.reduce(lambda acc, ig: acc * ig[1] + ig[0], zip(ids, grid), 0)

            @pl.when(flat == (2 * n_steps) // 3)
            def _():
                carry.middle(cins, couts, csems)

        body(*ins, *outs, *scr)

        @pl.when(last)
        def _():
            carry.finish(cins, couts, csems)

    all_aliases = dict(aliases or {})
    all_aliases.update({n_in + ci: n_out + co for ci, co in getattr(carry, "aliases", [])})
    params = pltpu.CompilerParams(
        dimension_semantics=("arbitrary",) * len(grid), vmem_limit_bytes=V7X_VMEM_LIMIT_BYTES,
        collective_id=COLLECTIVE_ID[carry.peers])
    outs = pl.pallas_call(
        full, name=name, grid=grid,
        in_specs=list(in_specs) + [ANY] * c_in,
        out_specs=list(out_specs) + [ANY] * c_out,
        out_shape=list(out_shape) + list(carry.out_shapes),
        scratch_shapes=list(scratch_shapes) + list(carry.sems),
        input_output_aliases=all_aliases, compiler_params=params)(*_in_hbm(args), *_in_hbm(carry.ins))
    return list(outs[:n_out]), list(outs[n_out:])


def _norm_proj(x, g1, w_int, carry=None):
    t, d = x.shape
    n = w_int.shape[0]
    tt, tn = _tile(t, 2048), _tile(n, 512)

    def body(x_ref, g_ref, w_ref, proj_ref, h1_ref, h1_s):
        @pl.when(pl.program_id(1) == 0)
        def _():
            def norm_rows(rows):
                xhat, _ = _rms_hat(x_ref[rows, :])
                h = (xhat * g_ref[...]).astype(BF16)
                h1_s[rows, :] = h
                h1_ref[rows, :] = h

            _row_chunks(tt, norm_rows)

        proj_ref[...] = _dot_nt(h1_s[...], w_ref[...]).astype(BF16)

    return _call(
        body, name="norm_proj", grid=(t // tt, n // tn),
        in_specs=[
            pl.BlockSpec((tt, d), lambda i, j: (i, 0)),
            pl.BlockSpec((1, d), lambda i, j: (0, 0)),
            pl.BlockSpec((tn, d), lambda i, j: (j, 0)),
        ],
        out_specs=[
            pl.BlockSpec((tt, tn), lambda i, j: (i, j)),
            pl.BlockSpec((tt, d), lambda i, j: (i, 0)),
        ],
        out_shape=[jax.ShapeDtypeStruct((t, n), BF16), jax.ShapeDtypeStruct((t, d), BF16)],
        scratch_shapes=[pltpu.VMEM((tt, d), BF16)],
        args=(x, g1, w_int), carry=carry)


def _scan_rows(av, bv, reverse):
    tc = av.shape[0]
    row = lax.broadcasted_iota(jnp.int32, av.shape, 0)
    s = 1
    while s < tc:
        if s < 8:
            keep = (row < tc - s) if reverse else (row >= s)
            shift = (tc - s) if reverse else s
            a_sh = jnp.where(keep, pltpu.roll(av, shift, 0), 1.0)
            b_sh = jnp.where(keep, pltpu.roll(bv, shift, 0), 0.0)
            bv = av * b_sh + bv
            av = av * a_sh
        elif reverse:
            bv = jnp.concatenate([av[:tc - s] * bv[s:] + bv[:tc - s], bv[tc - s:]], axis=0)
            av = jnp.concatenate([av[:tc - s] * av[s:], av[tc - s:]], axis=0)
        else:
            bv = jnp.concatenate([bv[:s], av[s:] * bv[:tc - s] + bv[s:]], axis=0)
            av = jnp.concatenate([av[:s], av[s:] * av[:tc - s]], axis=0)
        s *= 2
    return av, bv


N_LRU_SAVED = 5


def _fill_block_diag(w_ref, bd_ref):
    bd_ref[...] = jnp.zeros_like(bd_ref)
    hd = LRU_HEAD_DIM
    for k in range(w_ref.shape[0]):
        bd_ref[k * hd:(k + 1) * hd, k * hd:(k + 1) * hd] = w_ref[k].astype(BF16)


def _lru_fwd(proj, conv_w, conv_b, w_a, b_a, w_x, b_x, lam, carry=None):
    t = proj.shape[0]
    dr = conv_b.shape[1]
    cb = LRU_CB
    tc = _tile(t, 256)
    ncb, ntc = dr // cb, t // tc

    def body(xp_ref, g_ref, cw_ref, cb_ref, wa_ref, ba_ref, wx_ref, bx_ref, lam_ref,
             y_ref, h_ref, saved_ref, prevx_s, hlast_s, wa_s, wx_s):
        c = pl.program_id(1)

        @pl.when(c == 0)
        def _():
            prevx_s[...] = jnp.zeros_like(prevx_s)
            hlast_s[...] = jnp.zeros_like(hlast_s)
            _fill_block_diag(wa_ref, wa_s)
            _fill_block_diag(wx_ref, wx_s)

        x = xp_ref[...].astype(F32)
        prev = prevx_s[...]
        row = lax.broadcasted_iota(jnp.int32, x.shape, 0)

        def sh(j):
            return jnp.where(row >= j, pltpu.roll(x, j, 0), pltpu.roll(prev, j, 0))

        xc = (cb_ref[...] + cw_ref[0:1, :] * sh(3) + cw_ref[1:2, :] * sh(2)
              + cw_ref[2:3, :] * sh(1) + cw_ref[3:4, :] * x)
        prevx_s[...] = x
        _, r, i, _, _, a, mult = _lru_gates(xc, wa_s[...], ba_ref[...], wx_s[...], bx_ref[...],
                                            lam_ref[...])
        for k, val in enumerate((xc, r, i, a, mult)):
            saved_ref[:, k * cb:(k + 1) * cb] = val
        av, bv = _scan_rows(a, mult * (i * xc), reverse=False)
        h = av * hlast_s[...] + bv
        h_ref[...] = h
        hlast_s[...] = h_ref[tc - 1:tc, :]
        gel, _ = _gelu_and_grad(g_ref[...].astype(F32))
        y_ref[...] = (h * gel).astype(BF16)

    vec = pl.BlockSpec((1, cb), lambda j, c: (0, j))
    blk = pl.BlockSpec((tc, cb), lambda j, c: (c, j))
    mat = pl.BlockSpec((cb // LRU_HEAD_DIM, LRU_HEAD_DIM, LRU_HEAD_DIM), lambda j, c: (j, 0, 0))
    return _call(
        body, name="lru_fwd", grid=(ncb, ntc),
        in_specs=[
            blk,
            pl.BlockSpec((tc, cb), lambda j, c: (c, ncb + j)),
            pl.BlockSpec((4, cb), lambda j, c: (0, j)),
            vec, mat, vec, mat, vec, vec,
        ],
        out_specs=[blk, blk, pl.BlockSpec((tc, N_LRU_SAVED * cb), lambda j, c: (c, j))],
        out_shape=[jax.ShapeDtypeStruct((t, dr), BF16), jax.ShapeDtypeStruct((t, dr), F32),
                   jax.ShapeDtypeStruct((t, N_LRU_SAVED * dr), F32)],
        scratch_shapes=[pltpu.VMEM((tc, cb), F32), pltpu.VMEM((1, cb), F32),
                        pltpu.VMEM((cb, cb), BF16), pltpu.VMEM((cb, cb), BF16)],
        args=(proj, proj, conv_w, conv_b, w_a, b_a, w_x, b_x, lam), carry=carry)


def _pool_select(col, vals):
    out = vals[3]
    for g in (2, 1, 0):
        out = jnp.where(col < (g + 1) * POOL_GROUP_DIM, vals[g], out)
    return out


def _pool_fwd(proj, pool_w, pool_scale, col_block):
    t = proj.shape[0]
    dp = pool_scale.shape[1]
    tc = _tile(t, 256)
    ntc = t // tc

    def body(x_ref, w_ref, sc_ref, y_ref, p_ref, px, p2, p4, p8):
        c = pl.program_id(0)

        @pl.when(c == 0)
        def _():
            for s in (px, p2, p4, p8):
                s[...] = jnp.zeros_like(s)

        x = x_ref[...].astype(F32)
        row = lax.broadcasted_iota(jnp.int32, x.shape, 0)
        col = lax.broadcasted_iota(jnp.int32, x.shape, 1)

        def sh(v, pv, j):
            return jnp.where(row >= j, pltpu.roll(v, j, 0), pltpu.roll(pv[...], j, 0))

        s2 = x + sh(x, px, 1)
        s4 = s2 + sh(s2, p2, 2)
        s8 = s4 + sh(s4, p4, 4)
        s16 = s8 + sh(s8, p8, 8)
        px[...] = x
        p2[...] = s2
        p4[...] = s4
        p8[...] = s8
        wsum = _pool_select(col, (s2, s4, s8, s16))
        win = _pool_select(col, POOL_WINDOWS)
        cnt = jnp.minimum(c * tc + row + 1, win).astype(F32)
        p = wsum / cnt - x
        pb = p.astype(BF16)
        p_ref[...] = pb
        for g in range(len(POOL_WINDOWS)):
            sl = slice(g * POOL_GROUP_DIM, (g + 1) * POOL_GROUP_DIM)
            yg = _dot_nn(pb[:, sl], w_ref[g]) * sc_ref[:, sl]
            y_ref[:, sl] = yg.astype(BF16)

    return _call(
        body, name="pool_fwd", grid=(ntc,),
        in_specs=[
            pl.BlockSpec((tc, dp), lambda c: (c, col_block)),
            pl.BlockSpec(pool_w.shape, lambda c: (0, 0, 0)),
            pl.BlockSpec((1, dp), lambda c: (0, 0)),
        ],
        out_specs=[pl.BlockSpec((tc, dp), lambda c: (c, 0))] * 2,
        out_shape=[jax.ShapeDtypeStruct((t, dp), BF16)] * 2,
        scratch_shapes=[pltpu.VMEM((tc, dp), F32)] * 4,
        args=(proj, pool_w, pool_scale))[0]


def _branch_mix(y_lru, y_pool, w_lru_up, w_pool_upb, proj, b_gate, ga_block, gb_block, carry=None):
    t, d = y_lru.shape
    dp = y_pool.shape[1]
    bw = w_pool_upb.shape[2]
    tt, tn = _tile(t, 1024), 512
    nj = d // tn

    def body(yl_ref, yp_ref, wl_ref, wp_ref, ga_ref, gb_ref, ba_ref, bb_ref, bra_ref, brb_ref, mix_ref):
        br_a = _dot_nn(yl_ref[...], wl_ref[...])
        wp = jnp.concatenate([wp_ref[b] for b in range(tn // bw)], axis=1)
        br_b = _dot_nn(yp_ref[...], wp)
        bra_ref[...] = br_a.astype(BF16)
        brb_ref[...] = br_b.astype(BF16)
        ga = _sig(ga_ref[...].astype(F32) + ba_ref[...])
        gb = _sig(gb_ref[...].astype(F32) + bb_ref[...])
        mix_ref[...] = (ga * br_a + gb * br_b).astype(BF16)

    out = pl.BlockSpec((tt, tn), lambda j, i: (i, j))
    return _call(
        body, name="branch_mix", grid=(nj, t // tt),
        in_specs=[
            pl.BlockSpec((tt, d), lambda j, i: (i, 0)),
            pl.BlockSpec((tt, dp), lambda j, i: (i, 0)),
            pl.BlockSpec((d, tn), lambda j, i: (0, j)),
            pl.BlockSpec((tn // bw, dp, bw), lambda j, i: (j, 0, 0)),
            pl.BlockSpec((tt, tn), lambda j, i: (i, ga_block + j)),
            pl.BlockSpec((tt, tn), lambda j, i: (i, gb_block + j)),
            pl.BlockSpec((1, tn), lambda j, i: (0, j)),
            pl.BlockSpec((1, tn), lambda j, i: (0, nj + j)),
        ],
        out_specs=[out, out, out],
        out_shape=[jax.ShapeDtypeStruct((t, d), BF16)] * 3,
        args=(y_lru, y_pool, w_lru_up, w_pool_upb, proj, proj, b_gate, b_gate), carry=carry)


def _wo_norm(mix, w_o, x, g2, g3, carry=None):
    t, d = x.shape
    tt = _tile(t, 512)

    def body(mix_ref, w_ref, x_ref, g2_ref, g3_ref, m_ref, x2_ref, h3_ref):
        m = _dot_nn(mix_ref[...], w_ref[...])
        m_ref[...] = m
        mhat, _ = _rms_hat(m)
        x2 = x_ref[...] + mhat * g2_ref[...]
        x2_ref[...] = x2
        xhat, _ = _rms_hat(x2)
        h3_ref[...] = (xhat * g3_ref[...]).astype(BF16)

    row = pl.BlockSpec((tt, d), lambda i: (i, 0))
    vec = pl.BlockSpec((1, d), lambda i: (0, 0))
    return _call(
        body, name="wo_norm", grid=(t // tt,),
        in_specs=[row, pl.BlockSpec((d, d), lambda i: (0, 0)), row, vec, vec],
        out_specs=[row, row, row],
        out_shape=[
            jax.ShapeDtypeStruct((t, d), F32),
            jax.ShapeDtypeStruct((t, d), F32),
            jax.ShapeDtypeStruct((t, d), BF16),
        ],
        args=(mix, w_o, x, g2, g3), carry=carry)


def _ff1(h3, w_ff1b, carry=None):
    t, d = h3.shape
    nb, _, tn = w_ff1b.shape
    tt = _tile(t, 2048)

    def body(h_ref, w_ref, rf_ref):
        rf_ref[...] = jnp.maximum(_dot_nn(h_ref[...], w_ref[...]), 0.0).astype(BF16)

    out = pl.BlockSpec((tt, tn), lambda i, j: (i, j))
    return _call(
        body, name="ff1", grid=(t // tt, nb),
        in_specs=[pl.BlockSpec((tt, d), lambda i, j: (i, 0)), pl.BlockSpec((None, d, tn), lambda i, j: (j, 0, 0))],
        out_specs=[out],
        out_shape=[jax.ShapeDtypeStruct((t, nb * tn), BF16)],
        args=(h3, w_ff1b), carry=carry)


def _ff2_loss(rf, w_ff2, x2, g4, target):
    t, k = rf.shape
    d = x2.shape[1]
    tt, tk = _tile(t, 1024), _tile(k, 1024)
    nk = k // tk

    def body(a_ref, w_ref, x2_hbm, g_ref, tg_hbm, dy_hbm, df_hbm, dg_ref, loss_ref, acc, x2_ref, tg_ref,
             late_sems, dy_ref, df_ref, out_sems):
        i, kk = pl.program_id(0), pl.program_id(1)
        late = _late_copies(i, tt, [(x2_hbm, x2_ref), (tg_hbm, tg_ref)], late_sems)

        @pl.when(kk == 0)
        def _():
            acc[...] = jnp.zeros_like(acc)
            for cp in late:
                cp.start()

        @pl.when((i == 0) & (kk == 0))
        def _():
            dg_ref[...] = jnp.zeros_like(dg_ref)
            loss_ref[...] = jnp.zeros_like(loss_ref)

        rf_tile = a_ref[...]
        acc[...] += _dot_nn(rf_tile * rf_tile, w_ref[...])

        @pl.when(kk == nk - 1)
        def _():
            for cp in late:
                cp.wait()

            def tail(rows):
                fhat, r = _rms_hat(acc[rows, :])
                g = g_ref[...]
                e = x2_ref[rows, :] + fhat * g - tg_ref[rows, :]
                loss_ref[...] += 0.5 * jnp.sum(jnp.mean(e * e, axis=-1, keepdims=True))
                dy = e * (1.0 / d)
                dy_ref[rows, :] = dy.astype(BF16)
                df, dg = _rms_bwd(dy, fhat, r, g)
                df_ref[rows, :] = df.astype(BF16)
                dg_ref[...] += dg
                dst = pl.ds(tile_start + rows.start, rows.size)
                pltpu.make_async_copy(dy_ref.at[rows], dy_hbm.at[dst], out_sems.at[0]).start()
                pltpu.make_async_copy(df_ref.at[rows], df_hbm.at[dst], out_sems.at[1]).start()

            tile_start = pl.multiple_of(i * tt, tt)
            _row_chunks(tt, tail)
            tile = pl.ds(tile_start, tt)
            pltpu.make_async_copy(dy_ref, dy_hbm.at[tile], out_sems.at[0]).wait()
            pltpu.make_async_copy(df_ref, df_hbm.at[tile], out_sems.at[1]).wait()

    vec = pl.BlockSpec((1, d), lambda i, kk: (0, 0))
    return _call(
        body, name="ff2_loss", grid=(t // tt, nk),
        in_specs=[
            pl.BlockSpec((tt, tk), lambda i, kk: (i, kk)),
            pl.BlockSpec((tk, d), lambda i, kk: (kk, 0)),
            ANY, vec, ANY,
        ],
        out_specs=[ANY, ANY, vec, pl.BlockSpec((1, 128), lambda i, kk: (0, 0))],
        out_shape=[
            jax.ShapeDtypeStruct((t, d), BF16),
            jax.ShapeDtypeStruct((t, d), BF16),
            jax.ShapeDtypeStruct((1, d), F32),
            jax.ShapeDtypeStruct((1, 128), F32),
        ],
        scratch_shapes=[pltpu.VMEM((tt, d), F32), pltpu.VMEM((tt, d), x2.dtype), pltpu.VMEM((tt, d), target.dtype),
                        pltpu.SemaphoreType.DMA((2,)), pltpu.VMEM((tt, d), BF16), pltpu.VMEM((tt, d), BF16),
                        pltpu.SemaphoreType.DMA((2,))],
        args=(rf, w_ff2, x2, g4, target))[0]


def _ff2_bwd(df, w_ff2, rf, carry=None):
    t, d = df.shape
    n = w_ff2.shape[0]
    tt, tn = _tile(t, 2048), _tile(n, 512)

    def body(df_ref, w_ref, rf_ref, out_ref):
        d_act = _dot_nt(df_ref[...], w_ref[...])
        out_ref[...] = (d_act * (2.0 * rf_ref[...].astype(F32))).astype(BF16)

    blk = pl.BlockSpec((tt, tn), lambda i, j: (i, j))
    return _call(
        body, name="ff2_bwd", grid=(t // tt, n // tn),
        in_specs=[pl.BlockSpec((tt, d), lambda i, j: (i, 0)), pl.BlockSpec((tn, d), lambda i, j: (j, 0)), blk],
        out_specs=[blk],
        out_shape=[jax.ShapeDtypeStruct((t, n), BF16)],
        args=(df, w_ff2, rf), carry=carry)


def _wgrad(a, b, name, prev=None, row_off=0, rows=None, carry=None, square_a=False):
    t, m = a.shape
    n = b.shape[1]
    rows = m if rows is None else rows
    tm, tk = _tile(m, 512), _tile(t, 2048)
    nk = t // tk
    assert row_off % tm == 0
    off = row_off // tm

    def body(*refs):
        a_ref, b_ref = refs[0], refs[1]
        o32_ref, o16_ref, acc = refs[-3], refs[-2], refs[-1]
        kk = pl.program_id(1)

        @pl.when(kk == 0)
        def _():
            acc[...] = jnp.zeros_like(acc)

        a_tile = a_ref[...]
        acc[...] += _dot_tn(a_tile * a_tile if square_a else a_tile, b_ref[...])

        @pl.when(kk == nk - 1)
        def _():
            o32_ref[...] = acc[...]
            o16_ref[...] = acc[...].astype(BF16)

    in_specs = [pl.BlockSpec((tk, tm), lambda i, kk: (kk, i)), pl.BlockSpec((tk, n), lambda i, kk: (kk, 0))]
    args = [a, b]
    aliases = {}
    if prev is not None:
        in_specs += [ANY, ANY]
        args += list(prev)
        aliases = {2: 0, 3: 1}
    out = pl.BlockSpec((tm, n), lambda i, kk: (off + i, 0))
    return _call(
        body, name=name, grid=(m // tm, nk),
        in_specs=in_specs, out_specs=[out, out],
        out_shape=[jax.ShapeDtypeStruct((rows, n), F32), jax.ShapeDtypeStruct((rows, n), BF16)],
        scratch_shapes=[pltpu.VMEM((tm, n), F32)],
        aliases=aliases, args=args, carry=carry)


def _wgrad_parts(parts, b, name, carry=None):
    t, n = b.shape
    tm = 512
    bounds = []
    lo = 0
    for part in parts:
        assert part.shape[0] == t and part.shape[1] % tm == 0
        bounds.append((lo, lo + part.shape[1] // tm))
        lo += part.shape[1] // tm
    nm = lo
    np_ = len(parts)

    def body(*refs):
        p_refs, b_ref, o32_ref, o16_ref = refs[:np_], refs[np_], refs[np_ + 1], refs[np_ + 2]
        i = pl.program_id(0)
        for (lo_p, hi_p), p_ref in zip(bounds, p_refs):
            @pl.when((i >= lo_p) & (i < hi_p))
            def _(p_ref=p_ref):
                res = _dot_tn(p_ref[...], b_ref[...])
                o32_ref[...] = res
                o16_ref[...] = res.astype(BF16)

    def part_spec(lo_p, hi_p):
        return pl.BlockSpec((t, tm), lambda i: (0, jnp.clip(i - lo_p, 0, hi_p - lo_p - 1)))

    out = pl.BlockSpec((tm, n), lambda i: (i, 0))
    return _call(
        body, name=name, grid=(nm,),
        in_specs=[part_spec(lo_p, hi_p) for lo_p, hi_p in bounds] + [pl.BlockSpec((t, n), lambda i: (0, 0))],
        out_specs=[out, out],
        out_shape=[jax.ShapeDtypeStruct((nm * tm, n), F32), jax.ShapeDtypeStruct((nm * tm, n), BF16)],
        args=(*parts, b), carry=carry)


def _wgrad_cols(a, b, bw, tn, name, carry=None):
    t, m = a.shape
    n = b.shape[1]
    per_step = tn // bw

    def body(a_ref, b_ref, o32_ref, o16_ref):
        res = _dot_tn(a_ref[...], b_ref[...])
        for blk in range(per_step):
            part = res[:, blk * bw:(blk + 1) * bw]
            o32_ref[blk] = part
            o16_ref[blk] = part.astype(BF16)

    out = pl.BlockSpec((per_step, m, bw), lambda j: (j, 0, 0))
    return _call(
        body, name=name, grid=(n // tn,),
        in_specs=[pl.BlockSpec((t, m), lambda j: (0, 0)), pl.BlockSpec((t, tn), lambda j: (0, j))],
        out_specs=[out, out],
        out_shape=[jax.ShapeDtypeStruct((n // bw, m, bw), F32), jax.ShapeDtypeStruct((n // bw, m, bw), BF16)],
        args=(a, b), carry=carry)


def _ff1_bwd_norms(d_f1, w_ff1b, dy, x2, g3, m, g2, carry=None):
    t, k = d_f1.shape
    d = x2.shape[1]
    assert dy.dtype == BF16 and x2.dtype == F32
    bw = w_ff1b.shape[2]
    per_step = 2
    tt, tk = _tile(t, 1024), per_step * bw
    nk = k // tk

    def body(a_ref, w_ref, dy_hbm, x2_hbm, g3_ref, m_hbm, g2_ref, dx2_hbm, dm_hbm, dg3_ref, dg2_ref, acc,
             dy_ref, x2_ref, m_ref, late_sems, out_sems):
        i, kk = pl.program_id(0), pl.program_id(1)
        late = _late_copies(i, tt, [(dy_hbm, dy_ref), (x2_hbm, x2_ref), (m_hbm, m_ref)], late_sems)

        @pl.when(kk == 0)
        def _():
            acc[...] = jnp.zeros_like(acc)
            for cp in late:
                cp.start()

        @pl.when((i == 0) & (kk == 0))
        def _():
            dg3_ref[...] = jnp.zeros_like(dg3_ref)
            dg2_ref[...] = jnp.zeros_like(dg2_ref)

        a_tile = a_ref[...]
        for b in range(per_step):
            acc[...] += _dot_nt(a_tile[:, b * bw:(b + 1) * bw], w_ref[b])

        @pl.when(kk == nk - 1)
        def _():
            for cp in late:
                cp.wait()

            def tail(rows):
                xhat, r3 = _rms_hat(x2_ref[rows, :])
                dx, dg3 = _rms_bwd(acc[rows, :], xhat, r3, g3_ref[...])
                dx2 = dy_ref[rows, :].astype(F32) + dx
                x2_ref[rows, :] = dx2
                dg3_ref[...] += dg3
                mhat, r2 = _rms_hat(m_ref[rows, :])
                dm, dg2 = _rms_bwd(dx2, mhat, r2, g2_ref[...])
                dy_ref[rows, :] = dm.astype(BF16)
                dg2_ref[...] += dg2

            _row_chunks(tt, tail)
            tile = pl.ds(pl.multiple_of(i * tt, tt), tt)
            outs = [pltpu.make_async_copy(x2_ref, dx2_hbm.at[tile], out_sems.at[0]),
                    pltpu.make_async_copy(dy_ref, dm_hbm.at[tile], out_sems.at[1])]
            for cp in outs:
                cp.start()
            for cp in outs:
                cp.wait()

    vec = pl.BlockSpec((1, d), lambda i, kk: (0, 0))
    return _call(
        body, name="ff1_bwd_norms", grid=(t // tt, nk),
        in_specs=[
            pl.BlockSpec((tt, tk), lambda i, kk: (i, kk)),
            pl.BlockSpec((per_step, d, bw), lambda i, kk: (kk, 0, 0)),
            ANY, ANY, vec, ANY, vec,
        ],
        out_specs=[ANY, ANY, vec, vec],
        out_shape=[
            jax.ShapeDtypeStruct((t, d), F32),
            jax.ShapeDtypeStruct((t, d), BF16),
            jax.ShapeDtypeStruct((1, d), F32),
            jax.ShapeDtypeStruct((1, d), F32),
        ],
        scratch_shapes=[pltpu.VMEM((tt, d), F32), pltpu.VMEM((tt, d), dy.dtype), pltpu.VMEM((tt, d), F32),
                        pltpu.VMEM((tt, d), F32), pltpu.SemaphoreType.DMA((3,)), pltpu.SemaphoreType.DMA((2,))],
        args=(d_f1, w_ff1b, dy, x2, g3, m, g2), carry=carry)


def _wo_bwd_mix(dm, w_o, br_a, br_b, proj, b_gate, ga_block, gb_block, carry=None):
    t, d = dm.shape
    tt, tn = _tile(t, 1024), 512
    nj = d // tn

    def body(dm_ref, w_ref, bra_ref, brb_ref, ga_ref, gb_ref, ba_ref, bb_ref,
             dbra_ref, dbrb_ref, dga_ref, dgb_ref, dba_ref, dbb_ref):
        i = pl.program_id(1)

        @pl.when(i == 0)
        def _():
            dba_ref[...] = jnp.zeros_like(dba_ref)
            dbb_ref[...] = jnp.zeros_like(dbb_ref)

        d_mix = _dot_nt(dm_ref[...], w_ref[...])
        ga = _sig(ga_ref[...].astype(F32) + ba_ref[...])
        gb = _sig(gb_ref[...].astype(F32) + bb_ref[...])
        dbra_ref[...] = (d_mix * ga).astype(BF16)
        dbrb_ref[...] = (d_mix * gb).astype(BF16)
        dga = d_mix * bra_ref[...].astype(F32) * (ga * (1.0 - ga))
        dgb = d_mix * brb_ref[...].astype(F32) * (gb * (1.0 - gb))
        dga_ref[...] = dga.astype(BF16)
        dgb_ref[...] = dgb.astype(BF16)
        dba_ref[...] += jnp.sum(dga, axis=0, keepdims=True)
        dbb_ref[...] += jnp.sum(dgb, axis=0, keepdims=True)

    blk = pl.BlockSpec((tt, tn), lambda j, i: (i, j))
    vec = pl.BlockSpec((1, tn), lambda j, i: (0, j))
    return _call(
        body, name="wo_bwd_mix", grid=(nj, t // tt),
        in_specs=[
            pl.BlockSpec((tt, d), lambda j, i: (i, 0)),
            pl.BlockSpec((tn, d), lambda j, i: (j, 0)),
            blk, blk,
            pl.BlockSpec((tt, tn), lambda j, i: (i, ga_block + j)),
            pl.BlockSpec((tt, tn), lambda j, i: (i, gb_block + j)),
            vec,
            pl.BlockSpec((1, tn), lambda j, i: (0, nj + j)),
        ],
        out_specs=[blk, blk, blk, blk, vec, vec],
        out_shape=[jax.ShapeDtypeStruct((t, d), BF16)] * 4 + [jax.ShapeDtypeStruct((1, d), F32)] * 2,
        args=(dm, w_o, br_a, br_b, proj, proj, b_gate, b_gate), carry=carry)


def _lru_up_bwd(d_br_a, w_lru_up, proj, h, g_block, carry=None):
    t, d = d_br_a.shape
    tt, tn = _tile(t, 1024), 512

    def body(a_ref, w_ref, g_ref, h_ref, dh_ref, dg_ref):
        d_y = _dot_nt(a_ref[...], w_ref[...])
        gel, gel_grad = _gelu_and_grad(g_ref[...].astype(F32))
        dh_ref[...] = d_y * gel
        dg_ref[...] = (d_y * h_ref[...] * gel_grad).astype(BF16)

    blk = pl.BlockSpec((tt, tn), lambda i, j: (i, j))
    return _call(
        body, name="lru_up_bwd", grid=(t // tt, d // tn),
        in_specs=[
            pl.BlockSpec((tt, d), lambda i, j: (i, 0)),
            pl.BlockSpec((tn, d), lambda i, j: (j, 0)),
            pl.BlockSpec((tt, tn), lambda i, j: (i, g_block + j)),
            blk,
        ],
        out_specs=[blk, blk],
        out_shape=[jax.ShapeDtypeStruct((t, d), F32), jax.ShapeDtypeStruct((t, d), BF16)],
        args=(d_br_a, w_lru_up, proj, h), carry=carry)


def _lru_bwd(dh, h, saved, proj, conv_w, w_a, w_x, lam, carry=None):
    t, dr = dh.shape
    cb = LRU_CB
    hd = LRU_HEAD_DIM
    per = cb // hd
    tc = _tile(t, 256)
    ncb, ntc = dr // cb, t // tc

    def body(dh_ref, h_ref, hp_ref, saved_ref, xp_ref, cw_ref, wa_ref, wx_ref,
             lam_ref, dxp_ref, dwa_ref, dba_ref, dwx_ref, dbx_ref, dlam_ref, dcw_ref, dcb_ref,
             nextd_s, anext_s, gnext_s, tmp_s, wa_s, wx_s):
        c = pl.program_id(1)
        rc = ntc - 1 - c

        @pl.when(c == 0)
        def _():
            nextd_s[...] = jnp.zeros_like(nextd_s)
            anext_s[...] = jnp.zeros_like(anext_s)
            gnext_s[...] = jnp.zeros_like(gnext_s)
            for ref in (dwa_ref, dba_ref, dwx_ref, dbx_ref, dlam_ref, dcw_ref, dcb_ref):
                ref[...] = jnp.zeros_like(ref)
            _fill_block_diag(wa_ref, wa_s)
            _fill_block_diag(wx_ref, wx_s)

        xc, r, i, a, mult = [saved_ref[:, k * cb:(k + 1) * cb] for k in range(N_LRU_SAVED)]
        wa, wx, lam = wa_s[...], wx_s[...], lam_ref[...]
        xcb = xc.astype(BF16)
        sp = _softplus_neg(lam)
        row = lax.broadcasted_iota(jnp.int32, xc.shape, 0)
        h = h_ref[...]
        hp = jnp.where(rc == 0, 0.0, hp_ref[...])
        hprev = jnp.where(row >= 1, pltpu.roll(h, 1, 0), pltpu.roll(hp, 1, 0))

        def up(v, nv, j):
            return jnp.where(row < tc - j, pltpu.roll(v, tc - j, 0), nv)

        av, bv = _scan_rows(up(a, anext_s[...], 1), dh_ref[...], reverse=True)
        gt = av * gnext_s[...] + bv
        tmp_s[...] = gt
        gnext_s[...] = tmp_s[0:1, :]
        tmp_s[...] = a
        anext_s[...] = tmp_s[0:1, :]

        da = gt * hprev
        ixc = i * xc
        d_mult = gt * ixc
        d_i = gt * mult * xc
        d_xc = gt * mult * i
        d_log_a = da * a - d_mult * (a * a) / mult
        d_pre_r = (d_log_a * ((-LRU_C) * sp)) * (r * (1.0 - r))
        d_pre_i = d_i * (i * (1.0 - i))
        d_sp = jnp.sum(d_log_a * ((-LRU_C) * r), axis=0, keepdims=True)
        dlam_ref[...] += d_sp * (-1.0 / (1.0 + jnp.exp(lam)))
        dpr = d_pre_r.astype(BF16)
        dpi = d_pre_i.astype(BF16)
        dba_ref[...] += jnp.sum(d_pre_r, axis=0, keepdims=True)
        dbx_ref[...] += jnp.sum(d_pre_i, axis=0, keepdims=True)
        pa = _dot_tn(xcb, dpr)
        px = _dot_tn(xcb, dpi)
        for k in range(per):
            dwa_ref[k] += pa[k * hd:(k + 1) * hd, k * hd:(k + 1) * hd]
            dwx_ref[k] += px[k * hd:(k + 1) * hd, k * hd:(k + 1) * hd]
        d_xc = d_xc + _dot_nt(dpr, wa) + _dot_nt(dpi, wx)

        nxt = nextd_s[...]
        xp = xp_ref[...].astype(F32)
        dxp = cw_ref[3:4, :] * d_xc
        dcw_ref[3:4, :] += jnp.sum(xp * d_xc, axis=0, keepdims=True)
        for j in (1, 2, 3):
            uj = up(d_xc, pltpu.roll(nxt, tc - j, 0), j)
            dxp = dxp + cw_ref[3 - j:4 - j, :] * uj
            dcw_ref[3 - j:4 - j, :] += jnp.sum(xp * uj, axis=0, keepdims=True)
        dcb_ref[...] += jnp.sum(d_xc, axis=0, keepdims=True)
        nextd_s[...] = d_xc
        dxp_ref[...] = dxp.astype(BF16)

    vec = pl.BlockSpec((1, cb), lambda j, c: (0, j))
    blk = pl.BlockSpec((tc, cb), lambda j, c: (ntc - 1 - c, j))
    mat = pl.BlockSpec((per, hd, hd), lambda j, c: (j, 0, 0))
    cwb = pl.BlockSpec((4, cb), lambda j, c: (0, j))
    return _call(
        body, name="lru_bwd", grid=(ncb, ntc),
        in_specs=[
            blk, blk,
            pl.BlockSpec((tc, cb), lambda j, c: (jnp.maximum(ntc - 2 - c, 0), j)),
            pl.BlockSpec((tc, N_LRU_SAVED * cb), lambda j, c: (ntc - 1 - c, j)),
            blk, cwb, mat, mat, vec,
        ],
        out_specs=[blk, mat, vec, mat, vec, vec, cwb, vec],
        out_shape=[
            jax.ShapeDtypeStruct((t, dr), BF16),
            jax.ShapeDtypeStruct(w_a.shape, F32),
            jax.ShapeDtypeStruct((1, dr), F32),
            jax.ShapeDtypeStruct(w_x.shape, F32),
            jax.ShapeDtypeStruct((1, dr), F32),
            jax.ShapeDtypeStruct((1, dr), F32),
            jax.ShapeDtypeStruct((4, dr), F32),
            jax.ShapeDtypeStruct((1, dr), F32),
        ],
        scratch_shapes=[
            pltpu.VMEM((tc, cb), F32),
            pltpu.VMEM((1, cb), F32),
            pltpu.VMEM((1, cb), F32),
            pltpu.VMEM((tc, cb), F32),
            pltpu.VMEM((cb, cb), BF16),
            pltpu.VMEM((cb, cb), BF16),
        ],
        args=(dh, h, h, saved, proj, conv_w, w_a, w_x, lam), carry=carry)


def _pool_bwd(d_br_b, w_pool_upb, p, pool_w, pool_scale):
    t, d = d_br_b.shape
    nwb, dp, _ = w_pool_upb.shape
    tc = _tile(t, 256)
    ntc = t // tc
    ng = len(POOL_WINDOWS)

    def body(db_ref, wu_ref, p_ref, w_ref, sc_ref, dx_ref, dw_ref, dsc_ref, nz, n2, n4, n8, dp_s, dy_s):
        c = pl.program_id(0)
        rc = ntc - 1 - c

        @pl.when(c == 0)
        def _():
            for s in (nz, n2, n4, n8):
                s[...] = jnp.zeros_like(s)
            dw_ref[...] = jnp.zeros_like(dw_ref)
            dsc_ref[...] = jnp.zeros_like(dsc_ref)

        wu = jnp.concatenate([wu_ref[b] for b in range(nwb)], axis=1)
        dy_s[...] = _dot_nt(db_ref[...], wu)
        for g in range(ng):
            sl = slice(g * POOL_GROUP_DIM, (g + 1) * POOL_GROUP_DIM)
            pg = p_ref[:, sl]
            dyg = dy_s[:, sl]
            wg = w_ref[g].astype(BF16)
            q = _dot_nn(pg, wg)
            dsc_ref[:, sl] += jnp.sum(dyg * q, axis=0, keepdims=True)
            dpw = (dyg * sc_ref[:, sl]).astype(BF16)
            dw_ref[g] += _dot_tn(pg, dpw)
            dp_s[:, sl] = _dot_nt(dpw, wg)

        dpv = dp_s[...]
        row = lax.broadcasted_iota(jnp.int32, dpv.shape, 0)
        col = lax.broadcasted_iota(jnp.int32, dpv.shape, 1)
        win = _pool_select(col, POOL_WINDOWS)
        cnt = jnp.minimum(rc * tc + row + 1, win).astype(F32)
        z = dpv / cnt

        def up(v, nv, j):
            return jnp.where(row < tc - j, pltpu.roll(v, tc - j, 0), pltpu.roll(nv[...], tc - j, 0))

        u2 = z + up(z, nz, 1)
        u4 = u2 + up(u2, n2, 2)
        u8 = u4 + up(u4, n4, 4)
        u16 = u8 + up(u8, n8, 8)
        nz[...] = z
        n2[...] = u2
        n4[...] = u4
        n8[...] = u8
        dx_ref[...] = (_pool_select(col, (u2, u4, u8, u16)) - dpv).astype(BF16)

    blk = pl.BlockSpec((tc, dp), lambda c: (ntc - 1 - c, 0))
    full_w = pl.BlockSpec(pool_w.shape, lambda c: (0, 0, 0))
    vec = pl.BlockSpec((1, dp), lambda c: (0, 0))
    return _call(
        body, name="pool_bwd", grid=(ntc,),
        in_specs=[pl.BlockSpec((tc, d), lambda c: (ntc - 1 - c, 0)),
                  pl.BlockSpec(w_pool_upb.shape, lambda c: (0, 0, 0)), blk, full_w, vec],
        out_specs=[blk, full_w, vec],
        out_shape=[
            jax.ShapeDtypeStruct((t, dp), BF16),
            jax.ShapeDtypeStruct(pool_w.shape, F32),
            jax.ShapeDtypeStruct((1, dp), F32),
        ],
        scratch_shapes=[pltpu.VMEM((tc, dp), F32)] * 6,
        args=(d_br_b, w_pool_upb, p, pool_w, pool_scale))[0]


def _win_bwd_norm(parts, w_int, dx2, x, g1, carry=None):
    t, d = x.shape
    tk = 512
    tt = _tile(t, 1024)
    bounds = []
    k0 = 0
    for part in parts:
        assert part.shape[1] % tk == 0
        bounds.append((k0, k0 + part.shape[1] // tk))
        k0 += part.shape[1] // tk
    nk = k0
    assert nk * tk == w_int.shape[0]
    np_ = len(parts)

    def body(*refs):
        p_refs = refs[:np_]
        w_ref, dx2_hbm, x_hbm, g_ref, gx_hbm, dg_ref, acc, dx2_ref, x_ref, late_sems, out_sem = refs[np_:]
        i, kk = pl.program_id(0), pl.program_id(1)
        late = _late_copies(i, tt, [(dx2_hbm, dx2_ref), (x_hbm, x_ref)], late_sems)

        @pl.when(kk == 0)
        def _():
            acc[...] = jnp.zeros_like(acc)
            for cp in late:
                cp.start()

        @pl.when((i == 0) & (kk == 0))
        def _():
            dg_ref[...] = jnp.zeros_like(dg_ref)

        for (lo, hi), p_ref in zip(bounds, p_refs):
            @pl.when((kk >= lo) & (kk < hi))
            def _(p_ref=p_ref):
                acc[...] += _dot_nn(p_ref[...], w_ref[...])

        @pl.when(kk == nk - 1)
        def _():
            for cp in late:
                cp.wait()

            def tail(rows):
                xhat, r = _rms_hat(x_ref[rows, :])
                dx, dg = _rms_bwd(acc[rows, :], xhat, r, g_ref[...])
                dx2_ref[rows, :] = dx2_ref[rows, :] + dx
                dg_ref[...] += dg

            _row_chunks(tt, tail)
            out = pltpu.make_async_copy(
                dx2_ref, gx_hbm.at[pl.ds(pl.multiple_of(i * tt, tt), tt)], out_sem.at[0])
            out.start()
            out.wait()

    def part_spec(lo, hi):
        return pl.BlockSpec((tt, tk), lambda i, kk: (i, jnp.clip(kk - lo, 0, hi - lo - 1)))

    vec = pl.BlockSpec((1, d), lambda i, kk: (0, 0))
    return _call(
        body, name="win_bwd_norm", grid=(t // tt, nk),
        in_specs=[part_spec(lo, hi) for lo, hi in bounds]
        + [pl.BlockSpec((tk, d), lambda i, kk: (kk, 0)), ANY, ANY, vec],
        out_specs=[ANY, vec],
        out_shape=[jax.ShapeDtypeStruct((t, d), F32), jax.ShapeDtypeStruct((1, d), F32)],
        scratch_shapes=[pltpu.VMEM((tt, d), F32), pltpu.VMEM((tt, d), F32), pltpu.VMEM((tt, d), F32),
                        pltpu.SemaphoreType.DMA((2,)), pltpu.SemaphoreType.DMA((1,))],
        args=(*parts, w_int, dx2, x, g1), carry=carry)


def _adam_math(w, g, m, v):
    m = ADAM_B1 * m + (1.0 - ADAM_B1) * g
    v = ADAM_B2 * v + (1.0 - ADAM_B2) * (g * g)
    m_hat = m / (1.0 - ADAM_B1 ** ADAM_STEP)
    v_hat = v / (1.0 - ADAM_B2 ** ADAM_STEP)
    delta = -ADAM_LR * (m_hat / (jnp.sqrt(v_hat) + ADAM_EPS) + ADAM_WD * w)
    return delta, m, v


def _adamw_big(ws, gs, ms, vs):
    n = len(ws)
    nb = 4
    pair = [isinstance(g, tuple) for g in gs]

    def body(*refs):
        p = 0
        ins = []
        for a in range(n):
            k = 5 if pair[a] else 4
            ins.append(refs[p:p + k])
            p += k
        for a in range(n):
            g_out, d_ref, nm_ref, nv_ref = refs[p + 4 * a:p + 4 * a + 4]
            if pair[a]:
                w_ref, own_ref, recv_ref, m_ref, v_ref = ins[a]
                g = own_ref[...]
                for k in range(3):
                    g = g + recv_ref[k].astype(F32)
            else:
                w_ref, g_ref, m_ref, v_ref = ins[a]
                g = g_ref[...]
            dl, m, v = _adam_math(w_ref[...], g, m_ref[...], v_ref[...])
            g_out[...] = g
            d_ref[...] = dl
            nm_ref[...] = m
            nv_ref[...] = v

    in_specs, out_specs, out_shape, args = [], [], [], []
    for a, (w, g, m, v) in enumerate(zip(ws, gs, ms, vs)):
        rows, cols = w.shape
        blk = pl.BlockSpec((rows // nb, cols), lambda i: (i, 0))
        if pair[a]:
            in_specs += [blk, pl.BlockSpec((None, rows // nb, cols), lambda i: (0, i, 0)),
                         pl.BlockSpec((3, rows // nb, cols), lambda i: (0, i, 0)), blk, blk]
            args += [w, g[0], g[1], m, v]
        else:
            in_specs += [blk] * 4
            args += [w, g, m, v]
        out_specs += [blk] * 4
        out_shape += [jax.ShapeDtypeStruct(w.shape, F32)] * 4
    outs = _call(body, name="adamw_big", grid=(nb,), in_specs=in_specs, out_specs=out_specs,
                 out_shape=out_shape, args=args)[0]
    return [tuple(outs[4 * a:4 * a + 4]) for a in range(n)]


SMALL_ORDER = ("norm_mix_pre", "norm_mix_post", "norm_mlp_pre", "norm_mlp_post", "b_gate", "conv_w", "conv_b",
               "lru_w_a", "lru_b_a", "lru_w_x", "lru_b_x", "lru_lambda", "pool_w", "pool_scale")
VEC_ROW = dict(norm_mix_pre=0, norm_mix_post=1, norm_mlp_pre=2, norm_mlp_post=3, conv_b=6, lru_b_a=7,
               lru_b_x=8, lru_lambda=9)
ROW_B_GATE, ROW_POOL_SCALE, ROW_CONV_W, ROW_LOSS, N_VEC_ROWS = 4, 10, 11, 15, 16


def _adamw_small(vec_parts, g_pool, g_wa, g_wx, me, params):
    d = vec_parts.shape[2]
    names = SMALL_ORDER
    n = len(names)
    cw_cols = params["conv_w"][0].shape[2]

    def body(me_ref, vec_ref, vecc_ref, gp_ref, gwa_ref, gwx_ref, *refs):
        wmv = refs[:3 * n]
        loss_ref = refs[3 * n]
        outs = refs[3 * n + 1:3 * n + 1 + 4 * n]
        vs, vsc = refs[3 * n + 1 + 4 * n:]
        acc, accc = vec_ref[0], vecc_ref[0]
        for k in range(1, N_DEV):
            acc = acc + vec_ref[k]
            accc = accc + vecc_ref[k]
        vs[...] = acc
        vsc[...] = accc
        loss_ref[...] = vs[ROW_LOSS:ROW_LOSS + 1, 0:128]

        def upd(a, g, idx):
            w_ref, m_ref, v_ref = wmv[3 * a:3 * a + 3]
            g_ref, d_ref, nm_ref, nv_ref = outs[4 * a:4 * a + 4]
            dl, m, v = _adam_math(w_ref[idx], g, m_ref[idx], v_ref[idx])
            g_ref[idx] = g
            d_ref[idx] = dl
            nm_ref[idx] = m
            nv_ref[idx] = v

        for a, name in enumerate(names):
            if name in VEC_ROW:
                r = VEC_ROW[name]
                upd(a, vs[r:r + 1, :], (slice(None), slice(None)))
            elif name == "b_gate":
                for half in range(2):
                    r = ROW_B_GATE + half
                    upd(a, vs[r:r + 1, :], (slice(None), slice(half * d, (half + 1) * d)))
            elif name == "pool_scale":
                width = params[name][0].shape[1]
                upd(a, vs[ROW_POOL_SCALE:ROW_POOL_SCALE + 1, 0:width], (slice(None), slice(None)))
            elif name == "conv_w":
                upd(a, vsc[ROW_CONV_W:ROW_CONV_W + 4, :], (0,))
            elif name == "pool_w":
                upd(a, gp_ref[...], (Ellipsis,))
            elif name == "lru_w_a":
                upd(a, gwa_ref[...], (Ellipsis,))
            elif name == "lru_w_x":
                upd(a, gwx_ref[...], (Ellipsis,))
            else:
                raise ValueError(name)

    def whole(shape):
        nd = len(shape)
        return pl.BlockSpec(tuple(shape), lambda i, me_ref: (0,) * nd)

    in_specs = [
        whole(vec_parts.shape),
        pl.BlockSpec((N_DEV, N_VEC_ROWS, cw_cols), lambda i, me_ref: (0, 0, me_ref[0])),
        whole(g_pool.shape), whole(g_wa.shape), whole(g_wx.shape),
    ]
    args = [vec_parts, vec_parts, g_pool, g_wa, g_wx]
    out_specs = [whole((1, 128))]
    out_shape = [jax.ShapeDtypeStruct((1, 128), F32)]
    for name in names:
        for arr in params[name]:
            in_specs.append(whole(arr.shape))
            args.append(arr)
        shp = params[name][0].shape
        out_specs += [whole(shp)] * 4
        out_shape += [jax.ShapeDtypeStruct(shp, F32)] * 4
    grid_spec = pltpu.PrefetchScalarGridSpec(
        num_scalar_prefetch=1, grid=(1,), in_specs=in_specs, out_specs=out_specs,
        scratch_shapes=[pltpu.VMEM((N_VEC_ROWS, d), F32), pltpu.VMEM((N_VEC_ROWS, cw_cols), F32)])
    outs = pl.pallas_call(
        body, name="adamw_small", grid_spec=grid_spec, out_shape=out_shape,
        compiler_params=pltpu.CompilerParams(
            dimension_semantics=("arbitrary",), vmem_limit_bytes=V7X_VMEM_LIMIT_BYTES),
    )(me, *_in_hbm(args))
    return outs[0], {name: tuple(outs[1 + 4 * a:5 + 4 * a]) for a, name in enumerate(names)}


def _rs_sum(fulls, recvs, shard_ids, slot_ids, name):
    n = len(fulls)

    def body(sh_ref, sl_ref, *refs):
        s = pl.program_id(0)
        for a in range(n):
            full_ref, recv_ref = refs[2 * a], refs[2 * a + 1]
            own_ref, send_ref = refs[2 * n + 2 * a], refs[2 * n + 2 * a + 1]
            v = full_ref[...] + recv_ref[...].astype(F32)

            @pl.when(s == 0)
            def _(own_ref=own_ref, v=v):
                own_ref[...] = v

            @pl.when(s > 0)
            def _(send_ref=send_ref, v=v):
                send_ref[...] = v.astype(send_ref.dtype)

    in_specs, out_specs, out_shape, args = [], [], [], []
    for full, recv in zip(fulls, recvs):
        r, rest = recv.shape[1], tuple(recv.shape[2:])
        zeros = (0,) * len(rest)
        in_specs += [
            pl.BlockSpec((r,) + rest, lambda s, sh, sl, zeros=zeros: (sh[s],) + zeros),
            pl.BlockSpec((None, r) + rest, lambda s, sh, sl, zeros=zeros: (sl[s], 0) + zeros),
        ]
        out_specs += [
            pl.BlockSpec((None, r) + rest, lambda s, sh, sl, zeros=zeros: (0, 0) + zeros),
            pl.BlockSpec((None, r) + rest, lambda s, sh, sl, zeros=zeros: (jnp.maximum(s - 1, 0), 0) + zeros),
        ]
        out_shape += [jax.ShapeDtypeStruct((1, r) + rest, F32), jax.ShapeDtypeStruct((3, r) + rest, recv.dtype)]
        args += [full, recv]
    grid_spec = pltpu.PrefetchScalarGridSpec(
        num_scalar_prefetch=2, grid=(4,), in_specs=in_specs, out_specs=out_specs)
    outs = pl.pallas_call(
        body,
        name=name,
        grid_spec=grid_spec,
        out_shape=out_shape,
        compiler_params=pltpu.CompilerParams(
            dimension_semantics=("arbitrary",), vmem_limit_bytes=V7X_VMEM_LIMIT_BYTES),
    )(shard_ids, slot_ids, *_in_hbm(args))
    return [(outs[2 * a], outs[2 * a + 1]) for a in range(n)]


def _finals(pairs, name, carry=None):
    nb = 4
    n = len(pairs)

    def body(*refs):
        for a in range(n):
            own_ref, recv_ref = refs[2 * a], refs[2 * a + 1]
            acc = own_ref[...]
            for k in range(3):
                acc = acc + recv_ref[k].astype(F32)
            refs[2 * n + a][...] = acc

    in_specs, out_specs, out_shape, args = [], [], [], []
    for own, recv in pairs:
        _, rows, cols = own.shape
        in_specs += [pl.BlockSpec((None, rows // nb, cols), lambda i: (0, i, 0)),
                     pl.BlockSpec((3, rows // nb, cols), lambda i: (0, i, 0))]
        args += [own, recv]
        out_specs.append(pl.BlockSpec((rows // nb, cols), lambda i: (i, 0)))
        out_shape.append(jax.ShapeDtypeStruct((rows, cols), F32))
    return _call(body, name=name, grid=(nb,), in_specs=in_specs, out_specs=out_specs,
                 out_shape=out_shape, args=args, carry=carry)


def _rs_sums(fulls_f32, recv1, tag):
    x, y, c = _place()
    qs = jnp.stack([2 * x + y, 2 * (1 - x) + y, 2 * x + (1 - y), 2 * (1 - x) + (1 - y)]).astype(jnp.int32)
    shard_ids = 2 * qs + c
    return _rs_sum(fulls_f32, recv1, shard_ids, qs, "rs_sum_" + tag)


def _rs_level1(fulls_f32, fulls_send, tag):
    recv1 = _run_plan(_rs_sibling_plan(fulls_send), "rs_sibling_" + tag)
    return _rs_sums(fulls_f32, recv1, tag)


def _rows(g):
    return g.reshape(g.shape[0] * g.shape[1], g.shape[2])


def kernel(x, norm_mix_pre, norm_mix_post, norm_mlp_pre, norm_mlp_post, w_in, b_gate, conv_w, conv_b, lru_w_a, lru_b_a, lru_w_x, lru_b_x, lru_lambda, pool_w, pool_scale, w_lru_up, w_pool_up, w_o, w_ff1, w_ff2, loss_target, m_norm_mix_pre, m_norm_mix_post, m_norm_mlp_pre, m_norm_mlp_post, m_w_in, m_b_gate, m_conv_w, m_conv_b, m_lru_w_a, m_lru_b_a, m_lru_w_x, m_lru_b_x, m_lru_lambda, m_pool_w, m_pool_scale, m_w_lru_up, m_w_pool_up, m_w_o, m_w_ff1, m_w_ff2, v_norm_mix_pre, v_norm_mix_post, v_norm_mlp_pre, v_norm_mlp_post, v_w_in, v_b_gate, v_conv_w, v_conv_b, v_lru_w_a, v_lru_b_a, v_lru_w_x, v_lru_b_x, v_lru_lambda, v_pool_w, v_pool_scale, v_w_lru_up, v_w_pool_up, v_w_o, v_w_ff1, v_w_ff2):
    t, d = x.shape[1], x.shape[2]
    d_rnn = conv_b.shape[1]
    d_pool = pool_scale.shape[1]
    per = LRU_CB // LRU_HEAD_DIM
    xi, yi, ci = _place()
    me = 4 * xi + 2 * yi + ci

    x2d = x[0]
    tgt = loss_target[0]

    s_in = w_in[0].T.astype(BF16)
    s_lu = w_lru_up[0].astype(BF16)
    s_pu = w_pool_up[0].astype(BF16)
    s_o = w_o[0].astype(BF16)
    s_f1 = w_ff1[0].astype(BF16)
    s_f2 = w_ff2[0].astype(BF16)
    s_cw = jnp.pad(conv_w[0], ((0, 4), (0, 0)))

    g_in, g_cw = _run_plan(_ag_plan([s_in, s_cw]), "ag_w_in")
    w_int = _rows(g_in)
    conv_w_full = jnp.transpose(g_cw[:, :4, :], (1, 0, 2)).reshape(4, d_rnn)

    wa_bd, wx_bd = lru_w_a[0], lru_w_x[0]
    pw = pool_w[0]
    pw_bf = pw.astype(BF16)

    pool_block = (2 * d_rnn) // d_pool
    ga_block = (2 * d_rnn + d_pool) // 512
    gb_block = ga_block + d // 512
    g_block = d_rnn // 512

    r_f1, r_f2 = s_f1.shape[0], s_f2.shape[0]
    f1_cut = r_f1 // 4
    f2_cut = (3 * r_f2) // 8
    plan = _join([_ag_plan([s_lu, s_pu]), _ag_plan([s_f1], pieces=[(0, f1_cut)])])
    (proj, h1), got = _norm_proj(x2d, norm_mix_pre, w_int, carry=plan)
    (g_lu, g_pu), (g_f1,) = plan.split(got)
    plan = _join([_ag_plan([s_f1], pieces=[(f1_cut, r_f1 - f1_cut)], bufs=[g_f1]), _ag_plan([s_o])])
    (y_lru, h, lru_saved), got = _lru_fwd(
        proj, conv_w_full, conv_b, wa_bd, lru_b_a, wx_bd, lru_b_x, lru_lambda, carry=plan)
    (g_f1,), (g_o,) = plan.split(got)
    w_lu, w_og = _rows(g_lu), _rows(g_o)
    y_pool, p = _pool_fwd(proj, pw_bf, pool_scale, pool_block)
    (br_a, br_b, mix), (g_f2,) = _branch_mix(
        y_lru, y_pool, w_lu, g_pu, proj, b_gate, ga_block, gb_block,
        carry=_ag_plan([s_f2], pieces=[(0, f2_cut)]))
    (m, x2, h3), _ = _wo_norm(mix, w_og, x2d, norm_mix_post, norm_mlp_pre)
    (rf,), (g_f2,) = _ff1(
        h3, g_f1, carry=_ag_plan([s_f2], pieces=[(f2_cut, r_f2 - f2_cut)], bufs=[g_f2]))
    w_f2 = _rows(g_f2)
    dy, df, dg4, loss_part = _ff2_loss(rf, w_f2, x2, norm_mlp_post, tgt)

    (gw_ff2_32, gw_ff2_16), _ = _wgrad(rf, df, "wgrad_ff2", square_a=True)
    (d_f1,), r1_ff2 = _ff2_bwd(df, w_f2, rf, carry=_rs_sibling_plan([gw_ff2_16]))
    ((own_ff2, send_ff2),) = _rs_sums([gw_ff2_32], r1_ff2, "ff2")
    cut2 = (5 * send_ff2.shape[1]) // 16
    (gw_ff1_32, gw_ff1_16), (r2_ff2,) = _wgrad_cols(
        h3, d_f1, s_f1.shape[1], s_f1.shape[1], "wgrad_ff1",
        carry=_rs_chips_plan([send_ff2], pieces=[(0, cut2)]))
    plan = _join([_rs_chips_plan([send_ff2], pieces=[(cut2, send_ff2.shape[1] - cut2)], bufs=[r2_ff2]),
                  _rs_sibling_plan([gw_ff1_16])])
    (dx2, dm, dg3, dg2), got = _ff1_bwd_norms(d_f1, g_f1, dy, x2, norm_mlp_pre, m, norm_mix_post, carry=plan)
    (r2_ff2,), r1_ff1 = plan.split(got)
    ((own_ff1, send_ff1),) = _rs_sums([gw_ff1_32], r1_ff1, "ff1")
    own_ff1, send_ff1 = own_ff1.reshape((1,) + s_f1.shape), send_ff1.reshape((3,) + s_f1.shape)
    cut = send_ff1.shape[1] // 4
    (gw_o_32, gw_o_16), _ = _wgrad(mix, dm, "wgrad_o")
    (d_br_a, d_br_b, p_ga, p_gb, dbg_a, dbg_b), (r2_ff1,) = _wo_bwd_mix(
        dm, w_og, br_a, br_b, proj, b_gate, ga_block, gb_block,
        carry=_rs_chips_plan([send_ff1], pieces=[(0, cut)]))
    (gw_lu_32, gw_lu_16), _ = _wgrad(y_lru, d_br_a, "wgrad_lru_up")
    (gw_pu_32, gw_pu_16), _ = _wgrad_cols(y_pool, d_br_b, s_pu.shape[1], d, "wgrad_pool_up")
    (dh, p_g), r1_mid = _lru_up_bwd(
        d_br_a, w_lu, proj, h, g_block,
        carry=_rs_sibling_plan([gw_o_16, gw_lu_16, gw_pu_16]))
    mid = _rs_sums([gw_o_32, gw_lu_32, gw_pu_32], r1_mid, "mid")
    plan = _join([_rs_chips_plan([send_ff1], pieces=[(cut, send_ff1.shape[1] - cut)], bufs=[r2_ff1]),
                  _rs_chips_plan([mid[0][1]])])
    (p_x, dwa, db_a, dwx, db_x, dlam, dconv_w, dconv_b), got = _lru_bwd(
        dh, h, lru_saved, proj, conv_w_full, wa_bd, wx_bd, lru_lambda, carry=plan)
    (r2_ff1,), (r2_o,) = plan.split(got)
    p_p, dpool_w, dpool_scale = _pool_bwd(d_br_b, g_pu, p, pw, pool_scale)
    parts = [p_x, p_g, p_p, p_ga, p_gb]
    gw_in, (r2_lu, r2_pu) = _wgrad_parts(
        parts, h1, "wgrad_in", carry=_rs_chips_plan([mid[1][1], mid[2][1]]))
    r2_mid = [r2_o, r2_lu, r2_pu]
    tail = _rs_level1([gw_in[0], dpool_w.reshape(N_DEV, -1, POOL_GROUP_DIM), dwa, dwx],
                      [gw_in[1], dpool_w.reshape(N_DEV, -1, POOL_GROUP_DIM), dwa, dwx], "in")
    (grad_x, dg1), r2_tail = _win_bwd_norm(parts, w_int, dx2, x2d, norm_mix_pre,
                                           carry=_rs_chips_plan([s for _, s in tail]))

    def flat2(a):
        return a.reshape(a.shape[0], -1, a.shape[-1])

    fin_small, _ = _finals([
        (flat2(tail[1][0]), flat2(r2_tail[1])), (flat2(tail[2][0]), flat2(r2_tail[2])),
        (flat2(tail[3][0]), flat2(r2_tail[3])),
    ], "rs_finals_small")

    def pad_row(a):
        return jnp.pad(a, ((0, 0), (0, d - a.shape[1])))

    vecs = jnp.concatenate([dg1, dg2, dg3, dg4, dbg_a, dbg_b, dconv_b, db_a, db_x, dlam,
                            pad_row(dpool_scale), dconv_w, pad_row(loss_part)], axis=0)
    assert vecs.shape[0] == N_VEC_ROWS
    vec_parts, g_pool, g_wa, g_wx = _run_plan(_ag_plan([vecs] + fin_small), "ag_tail")

    big_names = ["w_in", "w_lru_up", "w_pool_up", "w_o", "w_ff1", "w_ff2"]
    big_w = [w_in[0].T, w_lru_up[0], w_pool_up[0], w_o[0], w_ff1[0], w_ff2[0]]
    big_g = [(tail[0][0], r2_tail[0]), (mid[1][0], r2_mid[1]),
             (mid[2][0].reshape((1,) + s_pu.shape), r2_mid[2].reshape((3,) + s_pu.shape)),
             (mid[0][0], r2_mid[0]), (own_ff1, r2_ff1), (own_ff2, r2_ff2)]
    big_m = [m_w_in[0].T, m_w_lru_up[0], m_w_pool_up[0], m_w_o[0], m_w_ff1[0], m_w_ff2[0]]
    big_v = [v_w_in[0].T, v_w_lru_up[0], v_w_pool_up[0], v_w_o[0], v_w_ff1[0], v_w_ff2[0]]
    big_out = _adamw_big(big_w, big_g, big_m, big_v)
    big_out[0] = tuple(o.T for o in big_out[0])

    small = dict(
        norm_mix_pre=(norm_mix_pre, m_norm_mix_pre, v_norm_mix_pre),
        norm_mix_post=(norm_mix_post, m_norm_mix_post, v_norm_mix_post),
        norm_mlp_pre=(norm_mlp_pre, m_norm_mlp_pre, v_norm_mlp_pre),
        norm_mlp_post=(norm_mlp_post, m_norm_mlp_post, v_norm_mlp_post),
        b_gate=(b_gate, m_b_gate, v_b_gate), conv_w=(conv_w, m_conv_w, v_conv_w),
        conv_b=(conv_b, m_conv_b, v_conv_b), lru_w_a=(lru_w_a, m_lru_w_a, v_lru_w_a),
        lru_b_a=(lru_b_a, m_lru_b_a, v_lru_b_a), lru_w_x=(lru_w_x, m_lru_w_x, v_lru_w_x),
        lru_b_x=(lru_b_x, m_lru_b_x, v_lru_b_x), lru_lambda=(lru_lambda, m_lru_lambda, v_lru_lambda),
        pool_w=(pool_w, m_pool_w, v_pool_w), pool_scale=(pool_scale, m_pool_scale, v_pool_scale))
    loss_row, small_out = _adamw_small(
        vec_parts, g_pool.reshape(pool_w.shape), g_wa.reshape(lru_w_a.shape), g_wx.reshape(lru_w_x.shape),
        jnp.reshape(me, (1,)).astype(jnp.int32), small)
    grads = {n: o[0] for n, o in small_out.items()}
    delta = {n: o[1] for n, o in small_out.items()}
    new_m = {n: o[2] for n, o in small_out.items()}
    new_v = {n: o[3] for n, o in small_out.items()}

    for name, (g, dl, nm, nv) in zip(big_names, big_out):
        grads[name], delta[name], new_m[name], new_v[name] = g[None], dl[None], nm[None], nv[None]

    loss = loss_row[0, 0]
    order = ["norm_mix_pre", "norm_mix_post", "norm_mlp_pre", "norm_mlp_post", "w_in", "b_gate", "conv_w",
             "conv_b", "lru_w_a", "lru_b_a", "lru_w_x", "lru_b_x", "lru_lambda", "pool_w", "pool_scale",
             "w_lru_up", "w_pool_up", "w_o", "w_ff1", "w_ff2"]
    return (loss, grad_x[None], *[grads[n] for n in order], *[delta[n] for n in order],
            *[new_m[n] for n in order], *[new_v[n] for n in order])
```

```python
import functools
import math
import operator
import types

import jax
import jax.numpy as jnp
from jax import lax
from jax.experimental import pallas as pl
from jax.experimental.pallas import tpu as pltpu

F32 = jnp.float32
BF16 = jnp.bfloat16
NORM_EPS = 1e-6
LRU_C = 8.0
N_LRU_HEADS = 16
LRU_HEAD_DIM = 64
POOL_WINDOWS = (2, 4, 8, 16)
POOL_GROUP_DIM = 128
ADAM_LR = 0.001
ADAM_B1 = 0.9
ADAM_B2 = 0.999
ADAM_EPS = 1e-08
ADAM_WD = 0.01
ADAM_STEP = 10
N_DEV = 8
V7X_VMEM_LIMIT_BYTES = 56 * 1024 * 1024
LRU_CB = 256
MESH = pl.DeviceIdType.MESH
ANY = pl.BlockSpec(memory_space=pl.ANY)


def _tile(n, pref):
    t = min(n, pref)
    assert n % t == 0, (n, pref)
    return t


def _dot_nn(a, b):
    return lax.dot_general(a, b, (((1,), (0,)), ((), ())), preferred_element_type=F32)


def _dot_nt(a, b):
    return lax.dot_general(a, b, (((1,), (1,)), ((), ())), preferred_element_type=F32)


def _dot_tn(a, b):
    return lax.dot_general(a, b, (((0,), (0,)), ((), ())), preferred_element_type=F32)


def _row_chunks(n_rows, fn, chunk=256):
    chunk = min(chunk, n_rows)
    assert n_rows % chunk == 0

    def step(r, carry):
        fn(pl.ds(pl.multiple_of(r * chunk, chunk), chunk))
        return carry

    lax.fori_loop(0, n_rows // chunk, step, 0)


def _late_copies(i, tt, pairs, sems):
    rows = pl.ds(pl.multiple_of(i * tt, tt), tt)
    return [pltpu.make_async_copy(hbm.at[rows], buf, sems.at[j]) for j, (hbm, buf) in enumerate(pairs)]


def _sig(x):
    return 1.0 / (1.0 + jnp.exp(-x))


def _rms_hat(x):
    r = lax.rsqrt(jnp.mean(x * x, axis=-1, keepdims=True) + NORM_EPS)
    return x * r, r


def _rms_bwd(dn, xhat, r, g):
    q = dn * g
    dx = r * (q - xhat * jnp.mean(q * xhat, axis=-1, keepdims=True))
    dg = jnp.sum(dn * xhat, axis=0, keepdims=True)
    return dx, dg


_GELU_K = math.sqrt(2.0 / math.pi)
_GELU_C = 0.044715


def _gelu_and_grad(g):
    t = jnp.tanh(_GELU_K * (g + _GELU_C * g * g * g))
    val = 0.5 * g * (1.0 + t)
    grad = 0.5 * (1.0 + t) + 0.5 * g * (1.0 - t * t) * (_GELU_K * (1.0 + 3.0 * _GELU_C * g * g))
    return val, grad


def _softplus_neg(lam):
    z = -lam
    e = jnp.exp(-jnp.abs(z))
    u = 1.0 + e
    d = u - 1.0
    l1p = jnp.where(d == 0.0, e, jnp.log(u) * (e / jnp.where(d == 0.0, 1.0, d)))
    return jnp.maximum(z, 0.0) + l1p


def _lru_gates(xc, wa, ba, wx, bx, lam):
    xcb = xc.astype(BF16)
    r = _sig(_dot_nn(xcb, wa) + ba)
    i = _sig(_dot_nn(xcb, wx) + bx)
    sp = _softplus_neg(lam)
    log_a = (-LRU_C) * r * sp
    a = jnp.exp(log_a)
    mult = jnp.sqrt(-jnp.tanh(log_a) * (1.0 + a * a))
    return xcb, r, i, sp, log_a, a, mult


def _place():
    return lax.axis_index("x"), lax.axis_index("y"), lax.axis_index("c")


def _ag_plan(shards, pieces=None, bufs=None):
    na = len(shards)
    n_kinds = 7

    def parts(ins, outs, sems):
        send_sems, recv_sems, local_sems = sems
        x, y, c = _place()
        me, sibling = (x, y, c), (x, y, 1 - c)
        x_nb, y_nb, diag = (1 - x, y), (x, 1 - y), (1 - x, 1 - y)
        relay_src = (c * (1 - x) + (1 - c) * x, c * y + (1 - c) * (1 - y))
        relay_dst = (c * x + (1 - c) * (1 - x), c * (1 - y) + (1 - c) * y)

        def own(a):
            return ins[a] if pieces is None else ins[a].at[pl.ds(*pieces[a])]

        def slot(a, px, py, pc):
            idx = 4 * px + 2 * py + pc
            return outs[a].at[idx] if pieces is None else outs[a].at[idx, pl.ds(*pieces[a])]

        def copy(a, k, block, to, src=None):
            return pltpu.make_async_remote_copy(
                src_ref=slot(a, *block) if src is None else src,
                dst_ref=slot(a, *block),
                send_sem=send_sems.at[a * n_kinds + k],
                recv_sem=recv_sems.at[a * n_kinds + k],
                device_id=to,
                device_id_type=MESH,
            )

        mine = [pltpu.make_async_copy(own(a), slot(a, *me), local_sems.at[a]) for a in range(na)]
        first, second, third = [], [], []
        for a in range(na):
            first += [copy(a, 0, me, sibling, src=own(a)), copy(a, 1, me, (*x_nb, c), src=own(a)),
                      copy(a, 2, me, (*y_nb, c), src=own(a))]
            second += [copy(a, 3, (*relay_src, c), (*relay_dst, c)), copy(a, 4, (*x_nb, c), sibling),
                       copy(a, 5, (*y_nb, c), sibling)]
            third.append(copy(a, 6, (*diag, c), sibling))
        return sibling, c, x_nb, y_nb, diag, copy, mine, first, second, third

    def start(ins, outs, sems):
        _, _, _, _, _, _, mine, first, _, _ = parts(ins, outs, sems)
        for cp in mine + first:
            cp.start()

    def middle(ins, outs, sems):
        _, c, x_nb, y_nb, _, copy, _, _, second, _ = parts(ins, outs, sems)
        for a in range(na):
            copy(a, 1, (*x_nb, c), (*x_nb, c)).wait_recv()
            copy(a, 2, (*y_nb, c), (*y_nb, c)).wait_recv()
        for cp in second:
            cp.start()

    def finish(ins, outs, sems):
        sibling, c, x_nb, y_nb, diag, copy, mine, first, second, third = parts(ins, outs, sems)
        for a in range(na):
            copy(a, 3, (*diag, c), (*diag, c)).wait_recv()
            third[a].start()
        for a in range(na):
            copy(a, 0, sibling, sibling).wait_recv()
            copy(a, 4, (*x_nb, 1 - c), sibling).wait_recv()
            copy(a, 5, (*y_nb, 1 - c), sibling).wait_recv()
            copy(a, 6, (*diag, 1 - c), sibling).wait_recv()
        for cp in first + second + third:
            cp.wait_send()
        for cp in mine:
            cp.wait()

    return types.SimpleNamespace(
        ins=list(shards) + list(bufs or []),
        out_shapes=[jax.ShapeDtypeStruct((N_DEV,) + s.shape, s.dtype) for s in shards],
        sems=[pltpu.SemaphoreType.DMA((n_kinds * na,)), pltpu.SemaphoreType.DMA((n_kinds * na,)),
              pltpu.SemaphoreType.DMA((na,))],
        aliases=[(na + a, a) for a in range(na)] if bufs else [],
        peers=frozenset({"sibling", "neighbours"}), start=start, middle=middle, finish=finish)


def _rs_sibling_plan(fulls):
    na = len(fulls)
    rs = [f.shape[0] // N_DEV for f in fulls]

    def copies(ins, outs, sems):
        send_sems, recv_sems = sems
        x, y, c = _place()
        out = []
        for a in range(na):
            for q in range(4):
                shard = 2 * q + (1 - c)
                out.append(pltpu.make_async_remote_copy(
                    src_ref=ins[a].at[pl.ds(shard * rs[a], rs[a])],
                    dst_ref=outs[a].at[q],
                    send_sem=send_sems.at[a * 4 + q],
                    recv_sem=recv_sems.at[a * 4 + q],
                    device_id=(x, y, 1 - c),
                    device_id_type=MESH,
                ))
        return out

    def start(ins, outs, sems):
        for cp in copies(ins, outs, sems):
            cp.start()

    def finish(ins, outs, sems):
        for cp in copies(ins, outs, sems):
            cp.wait()

    return types.SimpleNamespace(
        ins=list(fulls),
        out_shapes=[jax.ShapeDtypeStruct((4, r) + f.shape[1:], f.dtype) for r, f in zip(rs, fulls)],
        sems=[pltpu.SemaphoreType.DMA((4 * na,)), pltpu.SemaphoreType.DMA((4 * na,))],
        peers=frozenset({"sibling"}), start=start, finish=finish)


def _rs_chips_plan(sends, pieces=None, bufs=None):
    na = len(sends)

    def copies(ins, outs, sems):
        send_sems, recv_sems = sems
        x, y, c = _place()
        chips = [(1 - x, y), (x, 1 - y), (1 - x, 1 - y)]
        out = []
        for a in range(na):
            for k, chip in enumerate(chips):
                rows = (k,) if pieces is None else (k, pl.ds(*pieces[a]))
                out.append(pltpu.make_async_remote_copy(
                    src_ref=ins[a].at[rows],
                    dst_ref=outs[a].at[rows],
                    send_sem=send_sems.at[a * 3 + k],
                    recv_sem=recv_sems.at[a * 3 + k],
                    device_id=(*chip, c),
                    device_id_type=MESH,
                ))
        return out

    def start(ins, outs, sems):
        for cp in copies(ins, outs, sems):
            cp.start()

    def finish(ins, outs, sems):
        for cp in copies(ins, outs, sems):
            cp.wait()

    return types.SimpleNamespace(
        ins=list(sends) + list(bufs or []),
        out_shapes=[jax.ShapeDtypeStruct(s.shape, s.dtype) for s in sends],
        sems=[pltpu.SemaphoreType.DMA((3 * na,)), pltpu.SemaphoreType.DMA((3 * na,))],
        aliases=[(na + a, a) for a in range(na)] if bufs else [],
        peers=frozenset({"chips"}), start=start, finish=finish)


def _join(plans):
    ins, outs, sems, aliases, offs = [], [], [], [], []
    for p in plans:
        offs.append((len(ins), len(outs), len(sems)))
        aliases += [(len(ins) + ci, len(outs) + co) for ci, co in getattr(p, "aliases", [])]
        ins += p.ins
        outs += p.out_shapes
        sems += p.sems

    def cut(p, off, i, o, s):
        return (i[off[0]:off[0] + len(p.ins)], o[off[1]:off[1] + len(p.out_shapes)],
                s[off[2]:off[2] + len(p.sems)])

    def start(i, o, s):
        for p, off in zip(plans, offs):
            p.start(*cut(p, off, i, o, s))

    def middle(i, o, s):
        for p, off in zip(plans, offs):
            if getattr(p, "middle", None) is not None:
                p.middle(*cut(p, off, i, o, s))

    def finish(i, o, s):
        for p, off in zip(plans, offs):
            p.finish(*cut(p, off, i, o, s))

    def split(results):
        return [list(results[off[1]:off[1] + len(p.out_shapes)]) for p, off in zip(plans, offs)]

    return types.SimpleNamespace(ins=ins, out_shapes=outs, sems=sems, aliases=aliases,
                                 peers=frozenset().union(*[p.peers for p in plans]),
                                 start=start, middle=middle, finish=finish, split=split)


COLLECTIVE_ID = {frozenset({"sibling"}): 0, frozenset({"chips"}): 1, frozenset({"sibling", "chips"}): 2,
                 frozenset({"sibling", "neighbours"}): 3}


def _handshake(peers):
    x, y, c = _place()
    devs = []
    if "sibling" in peers:
        devs.append((x, y, 1 - c))
    if "neighbours" in peers:
        devs += [(1 - x, y, c), (x, 1 - y, c)]
    if "chips" in peers:
        assert "neighbours" not in peers
        devs += [(1 - x, y, c), (x, 1 - y, c), (1 - x, 1 - y, c)]
    barrier = pltpu.get_barrier_semaphore()
    for dev in devs:
        pl.semaphore_signal(barrier, inc=1, device_id=dev, device_id_type=MESH)
    pl.semaphore_wait(barrier, len(devs))


def _in_hbm(args):
    return [pltpu.with_memory_space_constraint(a, pltpu.HBM) for a in args]


def _run_plan(plan, name):
    n_in, n_out = len(plan.ins), len(plan.out_shapes)

    def body(*refs):
        ins, outs, sems = refs[:n_in], refs[n_in:n_in + n_out], refs[n_in + n_out:]
        _handshake(plan.peers)
        plan.start(ins, outs, sems)
        if getattr(plan, "middle", None) is not None:
            plan.middle(ins, outs, sems)
        plan.finish(ins, outs, sems)

    return pl.pallas_call(
        body,
        name=name,
        in_specs=[ANY] * n_in,
        out_specs=[ANY] * n_out,
        out_shape=plan.out_shapes,
        scratch_shapes=plan.sems,
        input_output_aliases=dict(getattr(plan, "aliases", [])),
        compiler_params=pltpu.CompilerParams(collective_id=COLLECTIVE_ID[plan.peers]),
    )(*_in_hbm(plan.ins))


def _call(body, *, name, grid, in_specs, out_specs, out_shape, args, scratch_shapes=(), aliases=None,
          carry=None):
    n_in, n_out, n_scr = len(in_specs), len(out_shape), len(scratch_shapes)
    params = pltpu.CompilerParams(
        dimension_semantics=("arbitrary",) * len(grid), vmem_limit_bytes=V7X_VMEM_LIMIT_BYTES)
    if carry is None:
        outs = pl.pallas_call(
            body, name=name, grid=grid, in_specs=list(in_specs), out_specs=list(out_specs),
            out_shape=list(out_shape), scratch_shapes=list(scratch_shapes),
            input_output_aliases=aliases or {}, compiler_params=params)(*_in_hbm(args))
        return list(outs), []
    c_in, c_out = len(carry.ins), len(carry.out_shapes)

    def full(*refs):
        p = 0
        ins = refs[p:p + n_in]
        p += n_in
        cins = refs[p:p + c_in]
        p += c_in
        outs = refs[p:p + n_out]
        p += n_out
        couts = refs[p:p + c_out]
        p += c_out
        scr = refs[p:p + n_scr]
        csems = refs[p + n_scr:]
        ids = [pl.program_id(a) for a in range(len(grid))]
        first = functools.reduce(operator.and_, [i == 0 for i in ids])
        last = functools.reduce(operator.and_, [i == g - 1 for i, g in zip(ids, grid)])

        @pl.when(first)
        def _():
            _handshake(carry.peers)
            carry.start(cins, couts, csems)

        if getattr(carry, "middle", None) is not None:
            n_steps = math.prod(grid)
            flat = functools.reduce(lambda acc, ig: acc * ig[1] + ig[0], zip(ids, grid), 0)

            @pl.when(flat == (2 * n_steps) // 3)
            def _():
                carry.middle(cins, couts, csems)

        body(*ins, *outs, *scr)

        @pl.when(last)
        def _():
            carry.finish(cins, couts, csems)

    all_aliases = dict(aliases or {})
    all_aliases.update({n_in + ci: n_out + co for ci, co in getattr(carry, "aliases", [])})
    params = pltpu.CompilerParams(
        dimension_semantics=("arbitrary",) * len(grid), vmem_limit_bytes=V7X_VMEM_LIMIT_BYTES,
        collective_id=COLLECTIVE_ID[carry.peers])
    outs = pl.pallas_call(
        full, name=name, grid=grid,
        in_specs=list(in_specs) + [ANY] * c_in,
        out_specs=list(out_specs) + [ANY] * c_out,
        out_shape=list(out_shape) + list(carry.out_shapes),
        scratch_shapes=list(scratch_shapes) + list(carry.sems),
        input_output_aliases=all_aliases, compiler_params=params)(*_in_hbm(args), *_in_hbm(carry.ins))
    return list(outs[:n_out]), list(outs[n_out:])


def _norm_proj(x, g1, w_int, carry=None):
    t, d = x.shape
    n = w_int.shape[0]
    tt, tn = _tile(t, 2048), _tile(n, 512)

    def body(x_ref, g_ref, w_ref, proj_ref, h1_ref, h1_s):
        @pl.when(pl.program_id(1) == 0)
        def _():
            def norm_rows(rows):
                xhat, _ = _rms_hat(x_ref[rows, :])
                h = (xhat * g_ref[...]).astype(BF16)
                h1_s[rows, :] = h
                h1_ref[rows, :] = h

            _row_chunks(tt, norm_rows)

        proj_ref[...] = _dot_nt(h1_s[...], w_ref[...]).astype(BF16)

    return _call(
        body, name="norm_proj", grid=(t // tt, n // tn),
        in_specs=[
            pl.BlockSpec((tt, d), lambda i, j: (i, 0)),
            pl.BlockSpec((1, d), lambda i, j: (0, 0)),
            pl.BlockSpec((tn, d), lambda i, j: (j, 0)),
        ],
        out_specs=[
            pl.BlockSpec((tt, tn), lambda i, j: (i, j)),
            pl.BlockSpec((tt, d), lambda i, j: (i, 0)),
        ],
        out_shape=[jax.ShapeDtypeStruct((t, n), BF16), jax.ShapeDtypeStruct((t, d), BF16)],
        scratch_shapes=[pltpu.VMEM((tt, d), BF16)],
        args=(x, g1, w_int), carry=carry)


def _scan_rows(av, bv, reverse):
    tc = av.shape[0]
    row = lax.broadcasted_iota(jnp.int32, av.shape, 0)
    s = 1
    while s < tc:
        if s < 8:
            keep = (row < tc - s) if reverse else (row >= s)
            shift = (tc - s) if reverse else s
            a_sh = jnp.where(keep, pltpu.roll(av, shift, 0), 1.0)
            b_sh = jnp.where(keep, pltpu.roll(bv, shift, 0), 0.0)
            bv = av * b_sh + bv
            av = av * a_sh
        elif reverse:
            bv = jnp.concatenate([av[:tc - s] * bv[s:] + bv[:tc - s], bv[tc - s:]], axis=0)
            av = jnp.concatenate([av[:tc - s] * av[s:], av[tc - s:]], axis=0)
        else:
            bv = jnp.concatenate([bv[:s], av[s:] * bv[:tc - s] + bv[s:]], axis=0)
            av = jnp.concatenate([av[:s], av[s:] * av[:tc - s]], axis=0)
        s *= 2
    return av, bv


N_LRU_SAVED = 5


def _fill_block_diag(w_ref, bd_ref):
    bd_ref[...] = jnp.zeros_like(bd_ref)
    hd = LRU_HEAD_DIM
    for k in range(w_ref.shape[0]):
        bd_ref[k * hd:(k + 1) * hd, k * hd:(k + 1) * hd] = w_ref[k].astype(BF16)


def _lru_fwd(proj, conv_w, conv_b, w_a, b_a, w_x, b_x, lam, carry=None):
    t = proj.shape[0]
    dr = conv_b.shape[1]
    cb = LRU_CB
    tc = _tile(t, 256)
    ncb, ntc = dr // cb, t // tc

    def body(xp_ref, g_ref, cw_ref, cb_ref, wa_ref, ba_ref, wx_ref, bx_ref, lam_ref,
             y_ref, h_ref, saved_ref, prevx_s, hlast_s, wa_s, wx_s):
        c = pl.program_id(1)

        @pl.when(c == 0)
        def _():
            prevx_s[...] = jnp.zeros_like(prevx_s)
            hlast_s[...] = jnp.zeros_like(hlast_s)
            _fill_block_diag(wa_ref, wa_s)
            _fill_block_diag(wx_ref, wx_s)

        x = xp_ref[...].astype(F32)
        prev = prevx_s[...]
        row = lax.broadcasted_iota(jnp.int32, x.shape, 0)

        def sh(j):
            return jnp.where(row >= j, pltpu.roll(x, j, 0), pltpu.roll(prev, j, 0))

        xc = (cb_ref[...] + cw_ref[0:1, :] * sh(3) + cw_ref[1:2, :] * sh(2)
              + cw_ref[2:3, :] * sh(1) + cw_ref[3:4, :] * x)
        prevx_s[...] = x
        _, r, i, _, _, a, mult = _lru_gates(xc, wa_s[...], ba_ref[...], wx_s[...], bx_ref[...],
                                            lam_ref[...])
        for k, val in enumerate((xc, r, i, a, mult)):
            saved_ref[:, k * cb:(k + 1) * cb] = val
        av, bv = _scan_rows(a, mult * (i * xc), reverse=False)
        h = av * hlast_s[...] + bv
        h_ref[...] = h
        hlast_s[...] = h_ref[tc - 1:tc, :]
        gel, _ = _gelu_and_grad(g_ref[...].astype(F32))
        y_ref[...] = (h * gel).astype(BF16)

    vec = pl.BlockSpec((1, cb), lambda j, c: (0, j))
    blk = pl.BlockSpec((tc, cb), lambda j, c: (c, j))
    mat = pl.BlockSpec((cb // LRU_HEAD_DIM, LRU_HEAD_DIM, LRU_HEAD_DIM), lambda j, c: (j, 0, 0))
    return _call(
        body, name="lru_fwd", grid=(ncb, ntc),
        in_specs=[
            blk,
            pl.BlockSpec((tc, cb), lambda j, c: (c, ncb + j)),
            pl.BlockSpec((4, cb), lambda j, c: (0, j)),
            vec, mat, vec, mat, vec, vec,
        ],
        out_specs=[blk, blk, pl.BlockSpec((tc, N_LRU_SAVED * cb), lambda j, c: (c, j))],
        out_shape=[jax.ShapeDtypeStruct((t, dr), BF16), jax.ShapeDtypeStruct((t, dr), F32),
                   jax.ShapeDtypeStruct((t, N_LRU_SAVED * dr), F32)],
        scratch_shapes=[pltpu.VMEM((tc, cb), F32), pltpu.VMEM((1, cb), F32),
                        pltpu.VMEM((cb, cb), BF16), pltpu.VMEM((cb, cb), BF16)],
        args=(proj, proj, conv_w, conv_b, w_a, b_a, w_x, b_x, lam), carry=carry)


def _pool_select(col, vals):
    out = vals[3]
    for g in (2, 1, 0):
        out = jnp.where(col < (g + 1) * POOL_GROUP_DIM, vals[g], out)
    return out


def _pool_fwd(proj, pool_w, pool_scale, col_block):
    t = proj.shape[0]
    dp = pool_scale.shape[1]
    tc = _tile(t, 256)
    ntc = t // tc

    def body(x_ref, w_ref, sc_ref, y_ref, p_ref, px, p2, p4, p8):
        c = pl.program_id(0)

        @pl.when(c == 0)
        def _():
            for s in (px, p2, p4, p8):
                s[...] = jnp.zeros_like(s)

        x = x_ref[...].astype(F32)
        row = lax.broadcasted_iota(jnp.int32, x.shape, 0)
        col = lax.broadcasted_iota(jnp.int32, x.shape, 1)

        def sh(v, pv, j):
            return jnp.where(row >= j, pltpu.roll(v, j, 0), pltpu.roll(pv[...], j, 0))

        s2 = x + sh(x, px, 1)
        s4 = s2 + sh(s2, p2, 2)
        s8 = s4 + sh(s4, p4, 4)
        s16 = s8 + sh(s8, p8, 8)
        px[...] = x
        p2[...] = s2
        p4[...] = s4
        p8[...] = s8
        wsum = _pool_select(col, (s2, s4, s8, s16))
        win = _pool_select(col, POOL_WINDOWS)
        cnt = jnp.minimum(c * tc + row + 1, win).astype(F32)
        p = wsum / cnt - x
        pb = p.astype(BF16)
        p_ref[...] = pb
        for g in range(len(POOL_WINDOWS)):
            sl = slice(g * POOL_GROUP_DIM, (g + 1) * POOL_GROUP_DIM)
            yg = _dot_nn(pb[:, sl], w_ref[g]) * sc_ref[:, sl]
            y_ref[:, sl] = yg.astype(BF16)

    return _call(
        body, name="pool_fwd", grid=(ntc,),
        in_specs=[
            pl.BlockSpec((tc, dp), lambda c: (c, col_block)),
            pl.BlockSpec(pool_w.shape, lambda c: (0, 0, 0)),
            pl.BlockSpec((1, dp), lambda c: (0, 0)),
        ],
        out_specs=[pl.BlockSpec((tc, dp), lambda c: (c, 0))] * 2,
        out_shape=[jax.ShapeDtypeStruct((t, dp), BF16)] * 2,
        scratch_shapes=[pltpu.VMEM((tc, dp), F32)] * 4,
        args=(proj, pool_w, pool_scale))[0]


def _branch_mix(y_lru, y_pool, w_lru_up, w_pool_upb, proj, b_gate, ga_block, gb_block, carry=None):
    t, d = y_lru.shape
    dp = y_pool.shape[1]
    bw = w_pool_upb.shape[2]
    tt, tn = _tile(t, 1024), 512
    nj = d // tn

    def body(yl_ref, yp_ref, wl_ref, wp_ref, ga_ref, gb_ref, ba_ref, bb_ref, bra_ref, brb_ref, mix_ref):
        br_a = _dot_nn(yl_ref[...], wl_ref[...])
        wp = jnp.concatenate([wp_ref[b] for b in range(tn // bw)], axis=1)
        br_b = _dot_nn(yp_ref[...], wp)
        bra_ref[...] = br_a.astype(BF16)
        brb_ref[...] = br_b.astype(BF16)
        ga = _sig(ga_ref[...].astype(F32) + ba_ref[...])
        gb = _sig(gb_ref[...].astype(F32) + bb_ref[...])
        mix_ref[...] = (ga * br_a + gb * br_b).astype(BF16)

    out = pl.BlockSpec((tt, tn), lambda j, i: (i, j))
    return _call(
        body, name="branch_mix", grid=(nj, t // tt),
        in_specs=[
            pl.BlockSpec((tt, d), lambda j, i: (i, 0)),
            pl.BlockSpec((tt, dp), lambda j, i: (i, 0)),
            pl.BlockSpec((d, tn), lambda j, i: (0, j)),
            pl.BlockSpec((tn // bw, dp, bw), lambda j, i: (j, 0, 0)),
            pl.BlockSpec((tt, tn), lambda j, i: (i, ga_block + j)),
            pl.BlockSpec((tt, tn), lambda j, i: (i, gb_block + j)),
            pl.BlockSpec((1, tn), lambda j, i: (0, j)),
            pl.BlockSpec((1, tn), lambda j, i: (0, nj + j)),
        ],
        out_specs=[out, out, out],
        out_shape=[jax.ShapeDtypeStruct((t, d), BF16)] * 3,
        args=(y_lru, y_pool, w_lru_up, w_pool_upb, proj, proj, b_gate, b_gate), carry=carry)


def _wo_norm(mix, w_o, x, g2, g3, carry=None):
    t, d = x.shape
    tt = _tile(t, 512)

    def body(mix_ref, w_ref, x_ref, g2_ref, g3_ref, m_ref, x2_ref, h3_ref):
        m = _dot_nn(mix_ref[...], w_ref[...])
        m_ref[...] = m
        mhat, _ = _rms_hat(m)
        x2 = x_ref[...] + mhat * g2_ref[...]
        x2_ref[...] = x2
        xhat, _ = _rms_hat(x2)
        h3_ref[...] = (xhat * g3_ref[...]).astype(BF16)

    row = pl.BlockSpec((tt, d), lambda i: (i, 0))
    vec = pl.BlockSpec((1, d), lambda i: (0, 0))
    return _call(
        body, name="wo_norm", grid=(t // tt,),
        in_specs=[row, pl.BlockSpec((d, d), lambda i: (0, 0)), row, vec, vec],
        out_specs=[row, row, row],
        out_shape=[
            jax.ShapeDtypeStruct((t, d), F32),
            jax.ShapeDtypeStruct((t, d), F32),
            jax.ShapeDtypeStruct((t, d), BF16),
        ],
        args=(mix, w_o, x, g2, g3), carry=carry)


def _ff1(h3, w_ff1b, carry=None):
    t, d = h3.shape
    nb, _, tn = w_ff1b.shape
    tt = _tile(t, 2048)

    def body(h_ref, w_ref, rf_ref):
        rf_ref[...] = jnp.maximum(_dot_nn(h_ref[...], w_ref[...]), 0.0).astype(BF16)

    out = pl.BlockSpec((tt, tn), lambda i, j: (i, j))
    return _call(
        body, name="ff1", grid=(t // tt, nb),
        in_specs=[pl.BlockSpec((tt, d), lambda i, j: (i, 0)), pl.BlockSpec((None, d, tn), lambda i, j: (j, 0, 0))],
        out_specs=[out],
        out_shape=[jax.ShapeDtypeStruct((t, nb * tn), BF16)],
        args=(h3, w_ff1b), carry=carry)


def _ff2_loss(rf, w_ff2, x2, g4, target):
    t, k = rf.shape
    d = x2.shape[1]
    tt, tk = _tile(t, 1024), _tile(k, 1024)
    nk = k // tk

    def body(a_ref, w_ref, x2_hbm, g_ref, tg_hbm, dy_hbm, df_hbm, dg_ref, loss_ref, acc, x2_ref, tg_ref,
             late_sems, dy_ref, df_ref, out_sems):
        i, kk = pl.program_id(0), pl.program_id(1)
        late = _late_copies(i, tt, [(x2_hbm, x2_ref), (tg_hbm, tg_ref)], late_sems)

        @pl.when(kk == 0)
        def _():
            acc[...] = jnp.zeros_like(acc)
            for cp in late:
                cp.start()

        @pl.when((i == 0) & (kk == 0))
        def _():
            dg_ref[...] = jnp.zeros_like(dg_ref)
            loss_ref[...] = jnp.zeros_like(loss_ref)

        rf_tile = a_ref[...]
        acc[...] += _dot_nn(rf_tile * rf_tile, w_ref[...])

        @pl.when(kk == nk - 1)
        def _():
            for cp in late:
                cp.wait()

            def tail(rows):
                fhat, r = _rms_hat(acc[rows, :])
                g = g_ref[...]
                e = x2_ref[rows, :] + fhat * g - tg_ref[rows, :]
                loss_ref[...] += 0.5 * jnp.sum(jnp.mean(e * e, axis=-1, keepdims=True))
                dy = e * (1.0 / d)
                dy_ref[rows, :] = dy.astype(BF16)
                df, dg = _rms_bwd(dy, fhat, r, g)
                df_ref[rows, :] = df.astype(BF16)
                dg_ref[...] += dg
                dst = pl.ds(tile_start + rows.start, rows.size)
                pltpu.make_async_copy(dy_ref.at[rows], dy_hbm.at[dst], out_sems.at[0]).start()
                pltpu.make_async_copy(df_ref.at[rows], df_hbm.at[dst], out_sems.at[1]).start()

            tile_start = pl.multiple_of(i * tt, tt)
            _row_chunks(tt, tail)
            tile = pl.ds(tile_start, tt)
            pltpu.make_async_copy(dy_ref, dy_hbm.at[tile], out_sems.at[0]).wait()
            pltpu.make_async_copy(df_ref, df_hbm.at[tile], out_sems.at[1]).wait()

    vec = pl.BlockSpec((1, d), lambda i, kk: (0, 0))
    return _call(
        body, name="ff2_loss", grid=(t // tt, nk),
        in_specs=[
            pl.BlockSpec((tt, tk), lambda i, kk: (i, kk)),
            pl.BlockSpec((tk, d), lambda i, kk: (kk, 0)),
            ANY, vec, ANY,
        ],
        out_specs=[ANY, ANY, vec, pl.BlockSpec((1, 128), lambda i, kk: (0, 0))],
        out_shape=[
            jax.ShapeDtypeStruct((t, d), BF16),
            jax.ShapeDtypeStruct((t, d), BF16),
            jax.ShapeDtypeStruct((1, d), F32),
            jax.ShapeDtypeStruct((1, 128), F32),
        ],
        scratch_shapes=[pltpu.VMEM((tt, d), F32), pltpu.VMEM((tt, d), x2.dtype), pltpu.VMEM((tt, d), target.dtype),
                        pltpu.SemaphoreType.DMA((2,)), pltpu.VMEM((tt, d), BF16), pltpu.VMEM((tt, d), BF16),
                        pltpu.SemaphoreType.DMA((2,))],
        args=(rf, w_ff2, x2, g4, target))[0]


def _ff2_bwd(df, w_ff2, rf, carry=None):
    t, d = df.shape
    n = w_ff2.shape[0]
    tt, tn = _tile(t, 2048), _tile(n, 512)

    def body(df_ref, w_ref, rf_ref, out_ref):
        d_act = _dot_nt(df_ref[...], w_ref[...])
        out_ref[...] = (d_act * (2.0 * rf_ref[...].astype(F32))).astype(BF16)

    blk = pl.BlockSpec((tt, tn), lambda i, j: (i, j))
    return _call(
        body, name="ff2_bwd", grid=(t // tt, n // tn),
        in_specs=[pl.BlockSpec((tt, d), lambda i, j: (i, 0)), pl.BlockSpec((tn, d), lambda i, j: (j, 0)), blk],
        out_specs=[blk],
        out_shape=[jax.ShapeDtypeStruct((t, n), BF16)],
        args=(df, w_ff2, rf), carry=carry)


def _wgrad(a, b, name, carry=None, square_a=False):
    t, m = a.shape
    n = b.shape[1]
    tm = _tile(m, 512)
    n_chunks = 4
    ck = t // n_chunks

    def body(a_ref, b_hbm, o32_ref, o16_ref, b_s, b_sems):
        def operand(rows):
            a_tile = a_ref[rows, :]
            return a_tile * a_tile if square_a else a_tile

        def put(res):
            o32_ref[...] = res
            o16_ref[...] = res.astype(BF16)

        @pl.when(pl.program_id(0) == 0)
        def _():
            chunks = [pl.ds(c * ck, ck) for c in range(n_chunks)]
            fetch = [pltpu.make_async_copy(b_hbm.at[rows], b_s.at[rows], b_sems.at[c])
                     for c, rows in enumerate(chunks)]
            for cp in fetch:
                cp.start()
            res = None
            for cp, rows in zip(fetch, chunks):
                cp.wait()
                part = _dot_tn(operand(rows), b_s[rows, :])
                res = part if res is None else res + part
            put(res)

        @pl.when(pl.program_id(0) > 0)
        def _():
            put(_dot_tn(operand(slice(None)), b_s[...]))

    out = pl.BlockSpec((tm, n), lambda i: (i, 0))
    return _call(
        body, name=name, grid=(m // tm,),
        in_specs=[pl.BlockSpec((t, tm), lambda i: (0, i)), ANY], out_specs=[out, out],
        out_shape=[jax.ShapeDtypeStruct((m, n), F32), jax.ShapeDtypeStruct((m, n), BF16)],
        scratch_shapes=[pltpu.VMEM((t, n), b.dtype), pltpu.SemaphoreType.DMA((n_chunks,))],
        args=[a, b], carry=carry)


def _wgrad_parts(parts, b, name, carry=None):
    t, n = b.shape
    tm = 512
    bounds = []
    lo = 0
    for part in parts:
        assert part.shape[0] == t and part.shape[1] % tm == 0
        bounds.append((lo, lo + part.shape[1] // tm))
        lo += part.shape[1] // tm
    nm = lo
    np_ = len(parts)

    def body(*refs):
        p_refs, b_ref, o32_ref, o16_ref = refs[:np_], refs[np_], refs[np_ + 1], refs[np_ + 2]
        i = pl.program_id(0)
        for (lo_p, hi_p), p_ref in zip(bounds, p_refs):
            @pl.when((i >= lo_p) & (i < hi_p))
            def _(p_ref=p_ref):
                res = _dot_tn(p_ref[...], b_ref[...])
                o32_ref[...] = res
                o16_ref[...] = res.astype(BF16)

    def part_spec(lo_p, hi_p):
        return pl.BlockSpec((t, tm), lambda i: (0, jnp.clip(i - lo_p, 0, hi_p - lo_p - 1)))

    out = pl.BlockSpec((tm, n), lambda i: (i, 0))
    return _call(
        body, name=name, grid=(nm,),
        in_specs=[part_spec(lo_p, hi_p) for lo_p, hi_p in bounds] + [pl.BlockSpec((t, n), lambda i: (0, 0))],
        out_specs=[out, out],
        out_shape=[jax.ShapeDtypeStruct((nm * tm, n), F32), jax.ShapeDtypeStruct((nm * tm, n), BF16)],
        args=(*parts, b), carry=carry)


def _wgrad_cols(a, b, bw, tn, name, carry=None):
    t, m = a.shape
    n = b.shape[1]
    per_step = tn // bw

    def body(a_ref, b_ref, o32_ref, o16_ref):
        res = _dot_tn(a_ref[...], b_ref[...])
        for blk in range(per_step):
            part = res[:, blk * bw:(blk + 1) * bw]
            o32_ref[blk] = part
            o16_ref[blk] = part.astype(BF16)

    out = pl.BlockSpec((per_step, m, bw), lambda j: (j, 0, 0))
    return _call(
        body, name=name, grid=(n // tn,),
        in_specs=[pl.BlockSpec((t, m), lambda j: (0, 0)), pl.BlockSpec((t, tn), lambda j: (0, j))],
        out_specs=[out, out],
        out_shape=[jax.ShapeDtypeStruct((n // bw, m, bw), F32), jax.ShapeDtypeStruct((n // bw, m, bw), BF16)],
        args=(a, b), carry=carry)


def _ff1_bwd_norms(d_f1, w_ff1b, dy, x2, g3, m, g2, carry=None):
    t, k = d_f1.shape
    d = x2.shape[1]
    assert dy.dtype == BF16 and x2.dtype == F32
    bw = w_ff1b.shape[2]
    per_step = 2
    tt, tk = _tile(t, 1024), per_step * bw
    nk = k // tk

    def body(a_ref, w_ref, dy_hbm, x2_hbm, g3_ref, m_hbm, g2_ref, dx2_hbm, dm_hbm, dg3_ref, dg2_ref, acc,
             dy_ref, x2_ref, m_ref, late_sems, out_sems):
        i, kk = pl.program_id(0), pl.program_id(1)
        late = _late_copies(i, tt, [(dy_hbm, dy_ref), (x2_hbm, x2_ref), (m_hbm, m_ref)], late_sems)

        @pl.when(kk == 0)
        def _():
            acc[...] = jnp.zeros_like(acc)
            for cp in late:
                cp.start()

        @pl.when((i == 0) & (kk == 0))
        def _():
            dg3_ref[...] = jnp.zeros_like(dg3_ref)
            dg2_ref[...] = jnp.zeros_like(dg2_ref)

        a_tile = a_ref[...]
        for b in range(per_step):
            acc[...] += _dot_nt(a_tile[:, b * bw:(b + 1) * bw], w_ref[b])

        @pl.when(kk == nk - 1)
        def _():
            for cp in late:
                cp.wait()

            def tail(rows):
                xhat, r3 = _rms_hat(x2_ref[rows, :])
                dx, dg3 = _rms_bwd(acc[rows, :], xhat, r3, g3_ref[...])
                dx2 = dy_ref[rows, :].astype(F32) + dx
                x2_ref[rows, :] = dx2
                dg3_ref[...] += dg3
                mhat, r2 = _rms_hat(m_ref[rows, :])
                dm, dg2 = _rms_bwd(dx2, mhat, r2, g2_ref[...])
                dy_ref[rows, :] = dm.astype(BF16)
                dg2_ref[...] += dg2

            _row_chunks(tt, tail)
            tile = pl.ds(pl.multiple_of(i * tt, tt), tt)
            outs = [pltpu.make_async_copy(x2_ref, dx2_hbm.at[tile], out_sems.at[0]),
                    pltpu.make_async_copy(dy_ref, dm_hbm.at[tile], out_sems.at[1])]
            for cp in outs:
                cp.start()
            for cp in outs:
                cp.wait()

    vec = pl.BlockSpec((1, d), lambda i, kk: (0, 0))
    return _call(
        body, name="ff1_bwd_norms", grid=(t // tt, nk),
        in_specs=[
            pl.BlockSpec((tt, tk), lambda i, kk: (i, kk)),
            pl.BlockSpec((per_step, d, bw), lambda i, kk: (kk, 0, 0)),
            ANY, ANY, vec, ANY, vec,
        ],
        out_specs=[ANY, ANY, vec, vec],
        out_shape=[
            jax.ShapeDtypeStruct((t, d), F32),
            jax.ShapeDtypeStruct((t, d), BF16),
            jax.ShapeDtypeStruct((1, d), F32),
            jax.ShapeDtypeStruct((1, d), F32),
        ],
        scratch_shapes=[pltpu.VMEM((tt, d), F32), pltpu.VMEM((tt, d), dy.dtype), pltpu.VMEM((tt, d), F32),
                        pltpu.VMEM((tt, d), F32), pltpu.SemaphoreType.DMA((3,)), pltpu.SemaphoreType.DMA((2,))],
        args=(d_f1, w_ff1b, dy, x2, g3, m, g2), carry=carry)


def _wo_bwd_mix(dm, w_o, br_a, br_b, proj, b_gate, ga_block, gb_block, carry=None):
    t, d = dm.shape
    tt, tn = _tile(t, 1024), 512
    nj = d // tn

    def body(dm_ref, w_ref, bra_ref, brb_ref, ga_ref, gb_ref, ba_ref, bb_ref,
             dbra_ref, dbrb_ref, dga_ref, dgb_ref, dba_ref, dbb_ref):
        i = pl.program_id(1)

        @pl.when(i == 0)
        def _():
            dba_ref[...] = jnp.zeros_like(dba_ref)
            dbb_ref[...] = jnp.zeros_like(dbb_ref)

        d_mix = _dot_nt(dm_ref[...], w_ref[...])
        ga = _sig(ga_ref[...].astype(F32) + ba_ref[...])
        gb = _sig(gb_ref[...].astype(F32) + bb_ref[...])
        dbra_ref[...] = (d_mix * ga).astype(BF16)
        dbrb_ref[...] = (d_mix * gb).astype(BF16)
        dga = d_mix * bra_ref[...].astype(F32) * (ga * (1.0 - ga))
        dgb = d_mix * brb_ref[...].astype(F32) * (gb * (1.0 - gb))
        dga_ref[...] = dga.astype(BF16)
        dgb_ref[...] = dgb.astype(BF16)
        dba_ref[...] += jnp.sum(dga, axis=0, keepdims=True)
        dbb_ref[...] += jnp.sum(dgb, axis=0, keepdims=True)

    blk = pl.BlockSpec((tt, tn), lambda j, i: (i, j))
    vec = pl.BlockSpec((1, tn), lambda j, i: (0, j))
    return _call(
        body, name="wo_bwd_mix", grid=(nj, t // tt),
        in_specs=[
            pl.BlockSpec((tt, d), lambda j, i: (i, 0)),
            pl.BlockSpec((tn, d), lambda j, i: (j, 0)),
            blk, blk,
            pl.BlockSpec((tt, tn), lambda j, i: (i, ga_block + j)),
            pl.BlockSpec((tt, tn), lambda j, i: (i, gb_block + j)),
            vec,
            pl.BlockSpec((1, tn), lambda j, i: (0, nj + j)),
        ],
        out_specs=[blk, blk, blk, blk, vec, vec],
        out_shape=[jax.ShapeDtypeStruct((t, d), BF16)] * 4 + [jax.ShapeDtypeStruct((1, d), F32)] * 2,
        args=(dm, w_o, br_a, br_b, proj, proj, b_gate, b_gate), carry=carry)


def _lru_up_bwd(d_br_a, w_lru_up, proj, h, g_block, carry=None):
    t, d = d_br_a.shape
    tt, tn = _tile(t, 1024), 512

    def body(a_ref, w_ref, g_ref, h_ref, dh_ref, dg_ref):
        d_y = _dot_nt(a_ref[...], w_ref[...])
        gel, gel_grad = _gelu_and_grad(g_ref[...].astype(F32))
        dh_ref[...] = d_y * gel
        dg_ref[...] = (d_y * h_ref[...] * gel_grad).astype(BF16)

    blk = pl.BlockSpec((tt, tn), lambda i, j: (i, j))
    return _call(
        body, name="lru_up_bwd", grid=(t // tt, d // tn),
        in_specs=[
            pl.BlockSpec((tt, d), lambda i, j: (i, 0)),
            pl.BlockSpec((tn, d), lambda i, j: (j, 0)),
            pl.BlockSpec((tt, tn), lambda i, j: (i, g_block + j)),
            blk,
        ],
        out_specs=[blk, blk],
        out_shape=[jax.ShapeDtypeStruct((t, d), F32), jax.ShapeDtypeStruct((t, d), BF16)],
        args=(d_br_a, w_lru_up, proj, h), carry=carry)


def _lru_bwd(dh, h, saved, proj, conv_w, w_a, w_x, lam, carry=None):
    t, dr = dh.shape
    cb = LRU_CB
    hd = LRU_HEAD_DIM
    per = cb // hd
    tc = _tile(t, 256)
    ncb, ntc = dr // cb, t // tc

    def body(dh_ref, h_ref, hp_ref, saved_ref, xp_ref, cw_ref, wa_ref, wx_ref,
             lam_ref, dxp_ref, dwa_ref, dba_ref, dwx_ref, dbx_ref, dlam_ref, dcw_ref, dcb_ref,
             nextd_s, anext_s, gnext_s, tmp_s, wa_s, wx_s):
        c = pl.program_id(1)
        rc = ntc - 1 - c

        @pl.when(c == 0)
        def _():
            nextd_s[...] = jnp.zeros_like(nextd_s)
            anext_s[...] = jnp.zeros_like(anext_s)
            gnext_s[...] = jnp.zeros_like(gnext_s)
            for ref in (dwa_ref, dba_ref, dwx_ref, dbx_ref, dlam_ref, dcw_ref, dcb_ref):
                ref[...] = jnp.zeros_like(ref)
            _fill_block_diag(wa_ref, wa_s)
            _fill_block_diag(wx_ref, wx_s)

        xc, r, i, a, mult = [saved_ref[:, k * cb:(k + 1) * cb] for k in range(N_LRU_SAVED)]
        wa, wx, lam = wa_s[...], wx_s[...], lam_ref[...]
        xcb = xc.astype(BF16)
        sp = _softplus_neg(lam)
        row = lax.broadcasted_iota(jnp.int32, xc.shape, 0)
        h = h_ref[...]
        hp = jnp.where(rc == 0, 0.0, hp_ref[...])
        hprev = jnp.where(row >= 1, pltpu.roll(h, 1, 0), pltpu.roll(hp, 1, 0))

        def up(v, nv, j):
            return jnp.where(row < tc - j, pltpu.roll(v, tc - j, 0), nv)

        av, bv = _scan_rows(up(a, anext_s[...], 1), dh_ref[...], reverse=True)
        gt = av * gnext_s[...] + bv
        tmp_s[...] = gt
        gnext_s[...] = tmp_s[0:1, :]
        tmp_s[...] = a
        anext_s[...] = tmp_s[0:1, :]

        da = gt * hprev
        ixc = i * xc
        d_mult = gt * ixc
        d_i = gt * mult * xc
        d_xc = gt * mult * i
        d_log_a = da * a - d_mult * (a * a) / mult
        d_pre_r = (d_log_a * ((-LRU_C) * sp)) * (r * (1.0 - r))
        d_pre_i = d_i * (i * (1.0 - i))
        d_sp = jnp.sum(d_log_a * ((-LRU_C) * r), axis=0, keepdims=True)
        dlam_ref[...] += d_sp * (-1.0 / (1.0 + jnp.exp(lam)))
        dpr = d_pre_r.astype(BF16)
        dpi = d_pre_i.astype(BF16)
        dba_ref[...] += jnp.sum(d_pre_r, axis=0, keepdims=True)
        dbx_ref[...] += jnp.sum(d_pre_i, axis=0, keepdims=True)
        pa = _dot_tn(xcb, dpr)
        px = _dot_tn(xcb, dpi)
        for k in range(per):
            dwa_ref[k] += pa[k * hd:(k + 1) * hd, k * hd:(k + 1) * hd]
            dwx_ref[k] += px[k * hd:(k + 1) * hd, k * hd:(k + 1) * hd]
        d_xc = d_xc + _dot_nt(dpr, wa) + _dot_nt(dpi, wx)

        nxt = nextd_s[...]
        xp = xp_ref[...].astype(F32)
        dxp = cw_ref[3:4, :] * d_xc
        dcw_ref[3:4, :] += jnp.sum(xp * d_xc, axis=0, keepdims=True)
        for j in (1, 2, 3):
            uj = up(d_xc, pltpu.roll(nxt, tc - j, 0), j)
            dxp = dxp + cw_ref[3 - j:4 - j, :] * uj
            dcw_ref[3 - j:4 - j, :] += jnp.sum(xp * uj, axis=0, keepdims=True)
        dcb_ref[...] += jnp.sum(d_xc, axis=0, keepdims=True)
        nextd_s[...] = d_xc
        dxp_ref[...] = dxp.astype(BF16)

    vec = pl.BlockSpec((1, cb), lambda j, c: (0, j))
    blk = pl.BlockSpec((tc, cb), lambda j, c: (ntc - 1 - c, j))
    mat = pl.BlockSpec((per, hd, hd), lambda j, c: (j, 0, 0))
    cwb = pl.BlockSpec((4, cb), lambda j, c: (0, j))
    return _call(
        body, name="lru_bwd", grid=(ncb, ntc),
        in_specs=[
            blk, blk,
            pl.BlockSpec((tc, cb), lambda j, c: (jnp.maximum(ntc - 2 - c, 0), j)),
            pl.BlockSpec((tc, N_LRU_SAVED * cb), lambda j, c: (ntc - 1 - c, j)),
            blk, cwb, mat, mat, vec,
        ],
        out_specs=[blk, mat, vec, mat, vec, vec, cwb, vec],
        out_shape=[
            jax.ShapeDtypeStruct((t, dr), BF16),
            jax.ShapeDtypeStruct(w_a.shape, F32),
            jax.ShapeDtypeStruct((1, dr), F32),
            jax.ShapeDtypeStruct(w_x.shape, F32),
            jax.ShapeDtypeStruct((1, dr), F32),
            jax.ShapeDtypeStruct((1, dr), F32),
            jax.ShapeDtypeStruct((4, dr), F32),
            jax.ShapeDtypeStruct((1, dr), F32),
        ],
        scratch_shapes=[
            pltpu.VMEM((tc, cb), F32),
            pltpu.VMEM((1, cb), F32),
            pltpu.VMEM((1, cb), F32),
            pltpu.VMEM((tc, cb), F32),
            pltpu.VMEM((cb, cb), BF16),
            pltpu.VMEM((cb, cb), BF16),
        ],
        args=(dh, h, h, saved, proj, conv_w, w_a, w_x, lam), carry=carry)


def _pool_bwd(d_br_b, w_pool_upb, p, pool_w, pool_scale):
    t, d = d_br_b.shape
    nwb, dp, _ = w_pool_upb.shape
    tc = _tile(t, 256)
    ntc = t // tc
    ng = len(POOL_WINDOWS)

    def body(db_ref, wu_ref, p_ref, w_ref, sc_ref, dx_ref, dw_ref, dsc_ref, nz, n2, n4, n8, dp_s, dy_s):
        c = pl.program_id(0)
        rc = ntc - 1 - c

        @pl.when(c == 0)
        def _():
            for s in (nz, n2, n4, n8):
                s[...] = jnp.zeros_like(s)
            dw_ref[...] = jnp.zeros_like(dw_ref)
            dsc_ref[...] = jnp.zeros_like(dsc_ref)

        wu = jnp.concatenate([wu_ref[b] for b in range(nwb)], axis=1)
        dy_s[...] = _dot_nt(db_ref[...], wu)
        for g in range(ng):
            sl = slice(g * POOL_GROUP_DIM, (g + 1) * POOL_GROUP_DIM)
            pg = p_ref[:, sl]
            dyg = dy_s[:, sl]
            wg = w_ref[g].astype(BF16)
            q = _dot_nn(pg, wg)
            dsc_ref[:, sl] += jnp.sum(dyg * q, axis=0, keepdims=True)
            dpw = (dyg * sc_ref[:, sl]).astype(BF16)
            dw_ref[g] += _dot_tn(pg, dpw)
            dp_s[:, sl] = _dot_nt(dpw, wg)

        dpv = dp_s[...]
        row = lax.broadcasted_iota(jnp.int32, dpv.shape, 0)
        col = lax.broadcasted_iota(jnp.int32, dpv.shape, 1)
        win = _pool_select(col, POOL_WINDOWS)
        cnt = jnp.minimum(rc * tc + row + 1, win).astype(F32)
        z = dpv / cnt

        def up(v, nv, j):
            return jnp.where(row < tc - j, pltpu.roll(v, tc - j, 0), pltpu.roll(nv[...], tc - j, 0))

        u2 = z + up(z, nz, 1)
        u4 = u2 + up(u2, n2, 2)
        u8 = u4 + up(u4, n4, 4)
        u16 = u8 + up(u8, n8, 8)
        nz[...] = z
        n2[...] = u2
        n4[...] = u4
        n8[...] = u8
        dx_ref[...] = (_pool_select(col, (u2, u4, u8, u16)) - dpv).astype(BF16)

    blk = pl.BlockSpec((tc, dp), lambda c: (ntc - 1 - c, 0))
    full_w = pl.BlockSpec(pool_w.shape, lambda c: (0, 0, 0))
    vec = pl.BlockSpec((1, dp), lambda c: (0, 0))
    return _call(
        body, name="pool_bwd", grid=(ntc,),
        in_specs=[pl.BlockSpec((tc, d), lambda c: (ntc - 1 - c, 0)),
                  pl.BlockSpec(w_pool_upb.shape, lambda c: (0, 0, 0)), blk, full_w, vec],
        out_specs=[blk, full_w, vec],
        out_shape=[
            jax.ShapeDtypeStruct((t, dp), BF16),
            jax.ShapeDtypeStruct(pool_w.shape, F32),
            jax.ShapeDtypeStruct((1, dp), F32),
        ],
        scratch_shapes=[pltpu.VMEM((tc, dp), F32)] * 6,
        args=(d_br_b, w_pool_upb, p, pool_w, pool_scale))[0]


def _win_bwd_norm(parts, w_int, dx2, x, g1, carry=None):
    t, d = x.shape
    tk = 512
    tt = _tile(t, 1024)
    bounds = []
    k0 = 0
    for part in parts:
        assert part.shape[1] % tk == 0
        bounds.append((k0, k0 + part.shape[1] // tk))
        k0 += part.shape[1] // tk
    nk = k0
    assert nk * tk == w_int.shape[0]
    np_ = len(parts)

    def body(*refs):
        p_refs = refs[:np_]
        w_ref, dx2_hbm, x_hbm, g_ref, gx_hbm, dg_ref, acc, dx2_ref, x_ref, late_sems, out_sem = refs[np_:]
        i, kk = pl.program_id(0), pl.program_id(1)
        late = _late_copies(i, tt, [(dx2_hbm, dx2_ref), (x_hbm, x_ref)], late_sems)

        @pl.when(kk == 0)
        def _():
            acc[...] = jnp.zeros_like(acc)
            for cp in late:
                cp.start()

        @pl.when((i == 0) & (kk == 0))
        def _():
            dg_ref[...] = jnp.zeros_like(dg_ref)

        for (lo, hi), p_ref in zip(bounds, p_refs):
            @pl.when((kk >= lo) & (kk < hi))
            def _(p_ref=p_ref):
                acc[...] += _dot_nn(p_ref[...], w_ref[...])

        @pl.when(kk == nk - 1)
        def _():
            for cp in late:
                cp.wait()

            def tail(rows):
                xhat, r = _rms_hat(x_ref[rows, :])
                dx, dg = _rms_bwd(acc[rows, :], xhat, r, g_ref[...])
                dx2_ref[rows, :] = dx2_ref[rows, :] + dx
                dg_ref[...] += dg

            _row_chunks(tt, tail)
            out = pltpu.make_async_copy(
                dx2_ref, gx_hbm.at[pl.ds(pl.multiple_of(i * tt, tt), tt)], out_sem.at[0])
            out.start()
            out.wait()

    def part_spec(lo, hi):
        return pl.BlockSpec((tt, tk), lambda i, kk: (i, jnp.clip(kk - lo, 0, hi - lo - 1)))

    vec = pl.BlockSpec((1, d), lambda i, kk: (0, 0))
    return _call(
        body, name="win_bwd_norm", grid=(t // tt, nk),
        in_specs=[part_spec(lo, hi) for lo, hi in bounds]
        + [pl.BlockSpec((tk, d), lambda i, kk: (kk, 0)), ANY, ANY, vec],
        out_specs=[ANY, vec],
        out_shape=[jax.ShapeDtypeStruct((t, d), F32), jax.ShapeDtypeStruct((1, d), F32)],
        scratch_shapes=[pltpu.VMEM((tt, d), F32), pltpu.VMEM((tt, d), F32), pltpu.VMEM((tt, d), F32),
                        pltpu.SemaphoreType.DMA((2,)), pltpu.SemaphoreType.DMA((1,))],
        args=(*parts, w_int, dx2, x, g1), carry=carry)


def _adam_math(w, g, m, v):
    m = ADAM_B1 * m + (1.0 - ADAM_B1) * g
    v = ADAM_B2 * v + (1.0 - ADAM_B2) * (g * g)
    m_hat = m / (1.0 - ADAM_B1 ** ADAM_STEP)
    v_hat = v / (1.0 - ADAM_B2 ** ADAM_STEP)
    delta = -ADAM_LR * (m_hat / (jnp.sqrt(v_hat) + ADAM_EPS) + ADAM_WD * w)
    return delta, m, v


def _adamw_big(ws, gs, ms, vs):
    n = len(ws)
    nb = 4
    pair = [isinstance(g, tuple) for g in gs]

    def body(*refs):
        p = 0
        ins = []
        for a in range(n):
            k = 5 if pair[a] else 4
            ins.append(refs[p:p + k])
            p += k
        for a in range(n):
            g_out, d_ref, nm_ref, nv_ref = refs[p + 4 * a:p + 4 * a + 4]
            if pair[a]:
                w_ref, own_ref, recv_ref, m_ref, v_ref = ins[a]
                g = own_ref[...]
                for k in range(3):
                    g = g + recv_ref[k].astype(F32)
            else:
                w_ref, g_ref, m_ref, v_ref = ins[a]
                g = g_ref[...]
            dl, m, v = _adam_math(w_ref[...], g, m_ref[...], v_ref[...])
            g_out[...] = g
            d_ref[...] = dl
            nm_ref[...] = m
            nv_ref[...] = v

    in_specs, out_specs, out_shape, args = [], [], [], []
    for a, (w, g, m, v) in enumerate(zip(ws, gs, ms, vs)):
        rows, cols = w.shape
        blk = pl.BlockSpec((rows // nb, cols), lambda i: (i, 0))
        if pair[a]:
            in_specs += [blk, pl.BlockSpec((None, rows // nb, cols), lambda i: (0, i, 0)),
                         pl.BlockSpec((3, rows // nb, cols), lambda i: (0, i, 0)), blk, blk]
            args += [w, g[0], g[1], m, v]
        else:
            in_specs += [blk] * 4
            args += [w, g, m, v]
        out_specs += [blk] * 4
        out_shape += [jax.ShapeDtypeStruct(w.shape, F32)] * 4
    outs = _call(body, name="adamw_big", grid=(nb,), in_specs=in_specs, out_specs=out_specs,
                 out_shape=out_shape, args=args)[0]
    return [tuple(outs[4 * a:4 * a + 4]) for a in range(n)]


SMALL_ORDER = ("norm_mix_pre", "norm_mix_post", "norm_mlp_pre", "norm_mlp_post", "b_gate", "conv_w", "conv_b",
               "lru_w_a", "lru_b_a", "lru_w_x", "lru_b_x", "lru_lambda", "pool_w", "pool_scale")
VEC_ROW = dict(norm_mix_pre=0, norm_mix_post=1, norm_mlp_pre=2, norm_mlp_post=3, conv_b=6, lru_b_a=7,
               lru_b_x=8, lru_lambda=9)
ROW_B_GATE, ROW_POOL_SCALE, ROW_CONV_W, ROW_LOSS, N_VEC_ROWS = 4, 10, 11, 15, 16


def _adamw_small(vec_parts, g_pool, g_wa, g_wx, me, params):
    d = vec_parts.shape[2]
    names = SMALL_ORDER
    n = len(names)
    cw_cols = params["conv_w"][0].shape[2]

    def body(me_ref, vec_ref, vecc_ref, gp_ref, gwa_ref, gwx_ref, *refs):
        wmv = refs[:3 * n]
        loss_ref = refs[3 * n]
        outs = refs[3 * n + 1:3 * n + 1 + 4 * n]
        vs, vsc = refs[3 * n + 1 + 4 * n:]
        acc, accc = vec_ref[0], vecc_ref[0]
        for k in range(1, N_DEV):
            acc = acc + vec_ref[k]
            accc = accc + vecc_ref[k]
        vs[...] = acc
        vsc[...] = accc
        loss_ref[...] = vs[ROW_LOSS:ROW_LOSS + 1, 0:128]

        def upd(a, g, idx):
            w_ref, m_ref, v_ref = wmv[3 * a:3 * a + 3]
            g_ref, d_ref, nm_ref, nv_ref = outs[4 * a:4 * a + 4]
            dl, m, v = _adam_math(w_ref[idx], g, m_ref[idx], v_ref[idx])
            g_ref[idx] = g
            d_ref[idx] = dl
            nm_ref[idx] = m
            nv_ref[idx] = v

        for a, name in enumerate(names):
            if name in VEC_ROW:
                r = VEC_ROW[name]
                upd(a, vs[r:r + 1, :], (slice(None), slice(None)))
            elif name == "b_gate":
                for half in range(2):
                    r = ROW_B_GATE + half
                    upd(a, vs[r:r + 1, :], (slice(None), slice(half * d, (half + 1) * d)))
            elif name == "pool_scale":
                width = params[name][0].shape[1]
                upd(a, vs[ROW_POOL_SCALE:ROW_POOL_SCALE + 1, 0:width], (slice(None), slice(None)))
            elif name == "conv_w":
                upd(a, vsc[ROW_CONV_W:ROW_CONV_W + 4, :], (0,))
            elif name == "pool_w":
                upd(a, gp_ref[...], (Ellipsis,))
            elif name == "lru_w_a":
                upd(a, gwa_ref[...], (Ellipsis,))
            elif name == "lru_w_x":
                upd(a, gwx_ref[...], (Ellipsis,))
            else:
                raise ValueError(name)

    def whole(shape):
        nd = len(shape)
        return pl.BlockSpec(tuple(shape), lambda i, me_ref: (0,) * nd)

    in_specs = [
        whole(vec_parts.shape),
        pl.BlockSpec((N_DEV, N_VEC_ROWS, cw_cols), lambda i, me_ref: (0, 0, me_ref[0])),
        whole(g_pool.shape), whole(g_wa.shape), whole(g_wx.shape),
    ]
    args = [vec_parts, vec_parts, g_pool, g_wa, g_wx]
    out_specs = [whole((1, 128))]
    out_shape = [jax.ShapeDtypeStruct((1, 128), F32)]
    for name in names:
        for arr in params[name]:
            in_specs.append(whole(arr.shape))
            args.append(arr)
        shp = params[name][0].shape
        out_specs += [whole(shp)] * 4
        out_shape += [jax.ShapeDtypeStruct(shp, F32)] * 4
    grid_spec = pltpu.PrefetchScalarGridSpec(
        num_scalar_prefetch=1, grid=(1,), in_specs=in_specs, out_specs=out_specs,
        scratch_shapes=[pltpu.VMEM((N_VEC_ROWS, d), F32), pltpu.VMEM((N_VEC_ROWS, cw_cols), F32)])
    outs = pl.pallas_call(
        body, name="adamw_small", grid_spec=grid_spec, out_shape=out_shape,
        compiler_params=pltpu.CompilerParams(
            dimension_semantics=("arbitrary",), vmem_limit_bytes=V7X_VMEM_LIMIT_BYTES),
    )(me, *_in_hbm(args))
    return outs[0], {name: tuple(outs[1 + 4 * a:5 + 4 * a]) for a, name in enumerate(names)}


def _rs_sum(fulls, recvs, shard_ids, slot_ids, name):
    n = len(fulls)

    def body(sh_ref, sl_ref, *refs):
        s = pl.program_id(0)
        for a in range(n):
            full_ref, recv_ref = refs[2 * a], refs[2 * a + 1]
            own_ref, send_ref = refs[2 * n + 2 * a], refs[2 * n + 2 * a + 1]
            v = full_ref[...] + recv_ref[...].astype(F32)

            @pl.when(s == 0)
            def _(own_ref=own_ref, v=v):
                own_ref[...] = v

            @pl.when(s > 0)
            def _(send_ref=send_ref, v=v):
                send_ref[...] = v.astype(send_ref.dtype)

    in_specs, out_specs, out_shape, args = [], [], [], []
    for full, recv in zip(fulls, recvs):
        r, rest = recv.shape[1], tuple(recv.shape[2:])
        zeros = (0,) * len(rest)
        in_specs += [
            pl.BlockSpec((r,) + rest, lambda s, sh, sl, zeros=zeros: (sh[s],) + zeros),
            pl.BlockSpec((None, r) + rest, lambda s, sh, sl, zeros=zeros: (sl[s], 0) + zeros),
        ]
        out_specs += [
            pl.BlockSpec((None, r) + rest, lambda s, sh, sl, zeros=zeros: (0, 0) + zeros),
            pl.BlockSpec((None, r) + rest, lambda s, sh, sl, zeros=zeros: (jnp.maximum(s - 1, 0), 0) + zeros),
        ]
        out_shape += [jax.ShapeDtypeStruct((1, r) + rest, F32), jax.ShapeDtypeStruct((3, r) + rest, recv.dtype)]
        args += [full, recv]
    grid_spec = pltpu.PrefetchScalarGridSpec(
        num_scalar_prefetch=2, grid=(4,), in_specs=in_specs, out_specs=out_specs)
    outs = pl.pallas_call(
        body,
        name=name,
        grid_spec=grid_spec,
        out_shape=out_shape,
        compiler_params=pltpu.CompilerParams(
            dimension_semantics=("arbitrary",), vmem_limit_bytes=V7X_VMEM_LIMIT_BYTES),
    )(shard_ids, slot_ids, *_in_hbm(args))
    return [(outs[2 * a], outs[2 * a + 1]) for a in range(n)]


def _finals(pairs, name, carry=None):
    nb = 4
    n = len(pairs)

    def body(*refs):
        for a in range(n):
            own_ref, recv_ref = refs[2 * a], refs[2 * a + 1]
            acc = own_ref[...]
            for k in range(3):
                acc = acc + recv_ref[k].astype(F32)
            refs[2 * n + a][...] = acc

    in_specs, out_specs, out_shape, args = [], [], [], []
    for own, recv in pairs:
        _, rows, cols = own.shape
        in_specs += [pl.BlockSpec((None, rows // nb, cols), lambda i: (0, i, 0)),
                     pl.BlockSpec((3, rows // nb, cols), lambda i: (0, i, 0))]
        args += [own, recv]
        out_specs.append(pl.BlockSpec((rows // nb, cols), lambda i: (i, 0)))
        out_shape.append(jax.ShapeDtypeStruct((rows, cols), F32))
    return _call(body, name=name, grid=(nb,), in_specs=in_specs, out_specs=out_specs,
                 out_shape=out_shape, args=args, carry=carry)


def _rs_sums(fulls_f32, recv1, tag):
    x, y, c = _place()
    qs = jnp.stack([2 * x + y, 2 * (1 - x) + y, 2 * x + (1 - y), 2 * (1 - x) + (1 - y)]).astype(jnp.int32)
    shard_ids = 2 * qs + c
    return _rs_sum(fulls_f32, recv1, shard_ids, qs, "rs_sum_" + tag)


def _rs_level1(fulls_f32, fulls_send, tag):
    recv1 = _run_plan(_rs_sibling_plan(fulls_send), "rs_sibling_" + tag)
    return _rs_sums(fulls_f32, recv1, tag)


def _rows(g):
    return g.reshape(g.shape[0] * g.shape[1], g.shape[2])


def kernel(x, norm_mix_pre, norm_mix_post, norm_mlp_pre, norm_mlp_post, w_in, b_gate, conv_w, conv_b, lru_w_a, lru_b_a, lru_w_x, lru_b_x, lru_lambda, pool_w, pool_scale, w_lru_up, w_pool_up, w_o, w_ff1, w_ff2, loss_target, m_norm_mix_pre, m_norm_mix_post, m_norm_mlp_pre, m_norm_mlp_post, m_w_in, m_b_gate, m_conv_w, m_conv_b, m_lru_w_a, m_lru_b_a, m_lru_w_x, m_lru_b_x, m_lru_lambda, m_pool_w, m_pool_scale, m_w_lru_up, m_w_pool_up, m_w_o, m_w_ff1, m_w_ff2, v_norm_mix_pre, v_norm_mix_post, v_norm_mlp_pre, v_norm_mlp_post, v_w_in, v_b_gate, v_conv_w, v_conv_b, v_lru_w_a, v_lru_b_a, v_lru_w_x, v_lru_b_x, v_lru_lambda, v_pool_w, v_pool_scale, v_w_lru_up, v_w_pool_up, v_w_o, v_w_ff1, v_w_ff2):
    t, d = x.shape[1], x.shape[2]
    d_rnn = conv_b.shape[1]
    d_pool = pool_scale.shape[1]
    per = LRU_CB // LRU_HEAD_DIM
    xi, yi, ci = _place()
    me = 4 * xi + 2 * yi + ci

    x2d = x[0]
    tgt = loss_target[0]

    s_in = w_in[0].T.astype(BF16)
    s_lu = w_lru_up[0].astype(BF16)
    s_pu = w_pool_up[0].astype(BF16)
    s_o = w_o[0].astype(BF16)
    s_f1 = w_ff1[0].astype(BF16)
    s_f2 = w_ff2[0].astype(BF16)
    s_cw = jnp.pad(conv_w[0], ((0, 4), (0, 0)))

    g_in, g_cw = _run_plan(_ag_plan([s_in, s_cw]), "ag_w_in")
    w_int = _rows(g_in)
    conv_w_full = jnp.transpose(g_cw[:, :4, :], (1, 0, 2)).reshape(4, d_rnn)

    wa_bd, wx_bd = lru_w_a[0], lru_w_x[0]
    pw = pool_w[0]
    pw_bf = pw.astype(BF16)

    pool_block = (2 * d_rnn) // d_pool
    ga_block = (2 * d_rnn + d_pool) // 512
    gb_block = ga_block + d // 512
    g_block = d_rnn // 512

    r_f1, r_f2 = s_f1.shape[0], s_f2.shape[0]
    f1_cut = r_f1 // 4
    f2_cut = (3 * r_f2) // 8
    plan = _join([_ag_plan([s_lu, s_pu]), _ag_plan([s_f1], pieces=[(0, f1_cut)])])
    (proj, h1), got = _norm_proj(x2d, norm_mix_pre, w_int, carry=plan)
    (g_lu, g_pu), (g_f1,) = plan.split(got)
    plan = _join([_ag_plan([s_f1], pieces=[(f1_cut, r_f1 - f1_cut)], bufs=[g_f1]), _ag_plan([s_o])])
    (y_lru, h, lru_saved), got = _lru_fwd(
        proj, conv_w_full, conv_b, wa_bd, lru_b_a, wx_bd, lru_b_x, lru_lambda, carry=plan)
    (g_f1,), (g_o,) = plan.split(got)
    w_lu, w_og = _rows(g_lu), _rows(g_o)
    y_pool, p = _pool_fwd(proj, pw_bf, pool_scale, pool_block)
    (br_a, br_b, mix), (g_f2,) = _branch_mix(
        y_lru, y_pool, w_lu, g_pu, proj, b_gate, ga_block, gb_block,
        carry=_ag_plan([s_f2], pieces=[(0, f2_cut)]))
    (m, x2, h3), _ = _wo_norm(mix, w_og, x2d, norm_mix_post, norm_mlp_pre)
    (rf,), (g_f2,) = _ff1(
        h3, g_f1, carry=_ag_plan([s_f2], pieces=[(f2_cut, r_f2 - f2_cut)], bufs=[g_f2]))
    w_f2 = _rows(g_f2)
    dy, df, dg4, loss_part = _ff2_loss(rf, w_f2, x2, norm_mlp_post, tgt)

    (gw_ff2_32, gw_ff2_16), _ = _wgrad(rf, df, "wgrad_ff2", square_a=True)
    (d_f1,), r1_ff2 = _ff2_bwd(df, w_f2, rf, carry=_rs_sibling_plan([gw_ff2_16]))
    ((own_ff2, send_ff2),) = _rs_sums([gw_ff2_32], r1_ff2, "ff2")
    cut2 = (5 * send_ff2.shape[1]) // 16
    (gw_ff1_32, gw_ff1_16), (r2_ff2,) = _wgrad_cols(
        h3, d_f1, s_f1.shape[1], s_f1.shape[1], "wgrad_ff1",
        carry=_rs_chips_plan([send_ff2], pieces=[(0, cut2)]))
    plan = _join([_rs_chips_plan([send_ff2], pieces=[(cut2, send_ff2.shape[1] - cut2)], bufs=[r2_ff2]),
                  _rs_sibling_plan([gw_ff1_16])])
    (dx2, dm, dg3, dg2), got = _ff1_bwd_norms(d_f1, g_f1, dy, x2, norm_mlp_pre, m, norm_mix_post, carry=plan)
    (r2_ff2,), r1_ff1 = plan.split(got)
    ((own_ff1, send_ff1),) = _rs_sums([gw_ff1_32], r1_ff1, "ff1")
    own_ff1, send_ff1 = own_ff1.reshape((1,) + s_f1.shape), send_ff1.reshape((3,) + s_f1.shape)
    cut = send_ff1.shape[1] // 4
    (gw_o_32, gw_o_16), _ = _wgrad(mix, dm, "wgrad_o")
    (d_br_a, d_br_b, p_ga, p_gb, dbg_a, dbg_b), (r2_ff1,) = _wo_bwd_mix(
        dm, w_og, br_a, br_b, proj, b_gate, ga_block, gb_block,
        carry=_rs_chips_plan([send_ff1], pieces=[(0, cut)]))
    (gw_lu_32, gw_lu_16), _ = _wgrad(y_lru, d_br_a, "wgrad_lru_up")
    (gw_pu_32, gw_pu_16), _ = _wgrad_cols(y_pool, d_br_b, s_pu.shape[1], d, "wgrad_pool_up")
    (dh, p_g), r1_mid = _lru_up_bwd(
        d_br_a, w_lu, proj, h, g_block,
        carry=_rs_sibling_plan([gw_o_16, gw_lu_16, gw_pu_16]))
    mid = _rs_sums([gw_o_32, gw_lu_32, gw_pu_32], r1_mid, "mid")
    plan = _join([_rs_chips_plan([send_ff1], pieces=[(cut, send_ff1.shape[1] - cut)], bufs=[r2_ff1]),
                  _rs_chips_plan([mid[0][1]])])
    (p_x, dwa, db_a, dwx, db_x, dlam, dconv_w, dconv_b), got = _lru_bwd(
        dh, h, lru_saved, proj, conv_w_full, wa_bd, wx_bd, lru_lambda, carry=plan)
    (r2_ff1,), (r2_o,) = plan.split(got)
    p_p, dpool_w, dpool_scale = _pool_bwd(d_br_b, g_pu, p, pw, pool_scale)
    parts = [p_x, p_g, p_p, p_ga, p_gb]
    gw_in, (r2_lu, r2_pu) = _wgrad_parts(
        parts, h1, "wgrad_in", carry=_rs_chips_plan([mid[1][1], mid[2][1]]))
    r2_mid = [r2_o, r2_lu, r2_pu]
    tail = _rs_level1([gw_in[0], dpool_w.reshape(N_DEV, -1, POOL_GROUP_DIM), dwa, dwx],
                      [gw_in[1], dpool_w.reshape(N_DEV, -1, POOL_GROUP_DIM), dwa, dwx], "in")
    (grad_x, dg1), r2_tail = _win_bwd_norm(parts, w_int, dx2, x2d, norm_mix_pre,
                                           carry=_rs_chips_plan([s for _, s in tail]))

    def flat2(a):
        return a.reshape(a.shape[0], -1, a.shape[-1])

    fin_small, _ = _finals([
        (flat2(tail[1][0]), flat2(r2_tail[1])), (flat2(tail[2][0]), flat2(r2_tail[2])),
        (flat2(tail[3][0]), flat2(r2_tail[3])),
    ], "rs_finals_small")

    def pad_row(a):
        return jnp.pad(a, ((0, 0), (0, d - a.shape[1])))

    vecs = jnp.concatenate([dg1, dg2, dg3, dg4, dbg_a, dbg_b, dconv_b, db_a, db_x, dlam,
                            pad_row(dpool_scale), dconv_w, pad_row(loss_part)], axis=0)
    assert vecs.shape[0] == N_VEC_ROWS
    vec_parts, g_pool, g_wa, g_wx = _run_plan(_ag_plan([vecs] + fin_small), "ag_tail")

    big_names = ["w_in", "w_lru_up", "w_pool_up", "w_o", "w_ff1", "w_ff2"]
    big_w = [w_in[0].T, w_lru_up[0], w_pool_up[0], w_o[0], w_ff1[0], w_ff2[0]]
    big_g = [(tail[0][0], r2_tail[0]), (mid[1][0], r2_mid[1]),
             (mid[2][0].reshape((1,) + s_pu.shape), r2_mid[2].reshape((3,) + s_pu.shape)),
             (mid[0][0], r2_mid[0]), (own_ff1, r2_ff1), (own_ff2, r2_ff2)]
    big_m = [m_w_in[0].T, m_w_lru_up[0], m_w_pool_up[0], m_w_o[0], m_w_ff1[0], m_w_ff2[0]]
    big_v = [v_w_in[0].T, v_w_lru_up[0], v_w_pool_up[0], v_w_o[0], v_w_ff1[0], v_w_ff2[0]]
    big_out = _adamw_big(big_w, big_g, big_m, big_v)
    big_out[0] = tuple(o.T for o in big_out[0])

    small = dict(
        norm_mix_pre=(norm_mix_pre, m_norm_mix_pre, v_norm_mix_pre),
        norm_mix_post=(norm_mix_post, m_norm_mix_post, v_norm_mix_post),
        norm_mlp_pre=(norm_mlp_pre, m_norm_mlp_pre, v_norm_mlp_pre),
        norm_mlp_post=(norm_mlp_post, m_norm_mlp_post, v_norm_mlp_post),
        b_gate=(b_gate, m_b_gate, v_b_gate), conv_w=(conv_w, m_conv_w, v_conv_w),
        conv_b=(conv_b, m_conv_b, v_conv_b), lru_w_a=(lru_w_a, m_lru_w_a, v_lru_w_a),
        lru_b_a=(lru_b_a, m_lru_b_a, v_lru_b_a), lru_w_x=(lru_w_x, m_lru_w_x, v_lru_w_x),
        lru_b_x=(lru_b_x, m_lru_b_x, v_lru_b_x), lru_lambda=(lru_lambda, m_lru_lambda, v_lru_lambda),
        pool_w=(pool_w, m_pool_w, v_pool_w), pool_scale=(pool_scale, m_pool_scale, v_pool_scale))
    loss_row, small_out = _adamw_small(
        vec_parts, g_pool.reshape(pool_w.shape), g_wa.reshape(lru_w_a.shape), g_wx.reshape(lru_w_x.shape),
        jnp.reshape(me, (1,)).astype(jnp.int32), small)
    grads = {n: o[0] for n, o in small_out.items()}
    delta = {n: o[1] for n, o in small_out.items()}
    new_m = {n: o[2] for n, o in small_out.items()}
    new_v = {n: o[3] for n, o in small_out.items()}

    for name, (g, dl, nm, nv) in zip(big_names, big_out):
        grads[name], delta[name], new_m[name], new_v[name] = g[None], dl[None], nm[None], nv[None]

    loss = loss_row[0, 0]
    order = ["norm_mix_pre", "norm_mix_post", "norm_mlp_pre", "norm_mlp_post", "w_in", "b_gate", "conv_w",
             "conv_b", "lru_w_a", "lru_b_a", "lru_w_x", "lru_b_x", "lru_lambda", "pool_w", "pool_scale",
             "w_lru_up", "w_pool_up", "w_o", "w_ff1", "w_ff2"]
    return (loss, grad_x[None], *[grads[n] for n in order], *[delta[n] for n in order],
            *[new_m[n] for n in order], *[new_v[n] for n in order])
```

```python
import functools
import math
import operator
import types

import jax
import jax.numpy as jnp
from jax import lax
from jax.experimental import pallas as pl
from jax.experimental.pallas import tpu as pltpu

F32 = jnp.float32
BF16 = jnp.bfloat16
NORM_EPS = 1e-6
LRU_C = 8.0
N_LRU_HEADS = 16
LRU_HEAD_DIM = 64
POOL_WINDOWS = (2, 4, 8, 16)
POOL_GROUP_DIM = 128
ADAM_LR = 0.001
ADAM_B1 = 0.9
ADAM_B2 = 0.999
ADAM_EPS = 1e-08
ADAM_WD = 0.01
ADAM_STEP = 10
N_DEV = 8
V7X_VMEM_LIMIT_BYTES = 56 * 1024 * 1024
LRU_CB = 256
MESH = pl.DeviceIdType.MESH
ANY = pl.BlockSpec(memory_space=pl.ANY)


def _tile(n, pref):
    t = min(n, pref)
    assert n % t == 0, (n, pref)
    return t


def _dot_nn(a, b):
    return lax.dot_general(a, b, (((1,), (0,)), ((), ())), preferred_element_type=F32)


def _dot_nt(a, b):
    return lax.dot_general(a, b, (((1,), (1,)), ((), ())), preferred_element_type=F32)


def _dot_tn(a, b):
    return lax.dot_general(a, b, (((0,), (0,)), ((), ())), preferred_element_type=F32)


def _row_chunks(n_rows, fn, chunk=256):
    chunk = min(chunk, n_rows)
    assert n_rows % chunk == 0

    def step(r, carry):
        fn(pl.ds(pl.multiple_of(r * chunk, chunk), chunk))
        return carry

    lax.fori_loop(0, n_rows // chunk, step, 0)


def _late_copies(i, tt, pairs, sems):
    rows = pl.ds(pl.multiple_of(i * tt, tt), tt)
    return [pltpu.make_async_copy(hbm.at[rows], buf, sems.at[j]) for j, (hbm, buf) in enumerate(pairs)]


def _sig(x):
    return 1.0 / (1.0 + jnp.exp(-x))


def _rms_hat(x):
    r = lax.rsqrt(jnp.mean(x * x, axis=-1, keepdims=True) + NORM_EPS)
    return x * r, r


def _rms_bwd(dn, xhat, r, g):
    q = dn * g
    dx = r * (q - xhat * jnp.mean(q * xhat, axis=-1, keepdims=True))
    dg = jnp.sum(dn * xhat, axis=0, keepdims=True)
    return dx, dg


_GELU_K = math.sqrt(2.0 / math.pi)
_GELU_C = 0.044715


def _gelu_and_grad(g):
    t = jnp.tanh(_GELU_K * (g + _GELU_C * g * g * g))
    val = 0.5 * g * (1.0 + t)
    grad = 0.5 * (1.0 + t) + 0.5 * g * (1.0 - t * t) * (_GELU_K * (1.0 + 3.0 * _GELU_C * g * g))
    return val, grad


def _softplus_neg(lam):
    z = -lam
    e = jnp.exp(-jnp.abs(z))
    u = 1.0 + e
    d = u - 1.0
    l1p = jnp.where(d == 0.0, e, jnp.log(u) * (e / jnp.where(d == 0.0, 1.0, d)))
    return jnp.maximum(z, 0.0) + l1p


def _lru_gates(xc, wa, ba, wx, bx, lam):
    xcb = xc.astype(BF16)
    r = _sig(_dot_nn(xcb, wa) + ba)
    i = _sig(_dot_nn(xcb, wx) + bx)
    sp = _softplus_neg(lam)
    log_a = (-LRU_C) * r * sp
    a = jnp.exp(log_a)
    mult = jnp.sqrt(-jnp.tanh(log_a) * (1.0 + a * a))
    return xcb, r, i, sp, log_a, a, mult


def _place():
    return lax.axis_index("x"), lax.axis_index("y"), lax.axis_index("c")


def _ag_plan(shards, pieces=None, bufs=None):
    na = len(shards)
    n_kinds = 7

    def parts(ins, outs, sems):
        send_sems, recv_sems, local_sems = sems
        x, y, c = _place()
        me, sibling = (x, y, c), (x, y, 1 - c)
        x_nb, y_nb, diag = (1 - x, y), (x, 1 - y), (1 - x, 1 - y)
        relay_src = (c * (1 - x) + (1 - c) * x, c * y + (1 - c) * (1 - y))
        relay_dst = (c * x + (1 - c) * (1 - x), c * (1 - y) + (1 - c) * y)

        def own(a):
            return ins[a] if pieces is None else ins[a].at[pl.ds(*pieces[a])]

        def slot(a, px, py, pc):
            idx = 4 * px + 2 * py + pc
            return outs[a].at[idx] if pieces is None else outs[a].at[idx, pl.ds(*pieces[a])]

        def copy(a, k, block, to, src=None):
            return pltpu.make_async_remote_copy(
                src_ref=slot(a, *block) if src is None else src,
                dst_ref=slot(a, *block),
                send_sem=send_sems.at[a * n_kinds + k],
                recv_sem=recv_sems.at[a * n_kinds + k],
                device_id=to,
                device_id_type=MESH,
            )

        mine = [pltpu.make_async_copy(own(a), slot(a, *me), local_sems.at[a]) for a in range(na)]
        first, second, third = [], [], []
        for a in range(na):
            first += [copy(a, 0, me, sibling, src=own(a)), copy(a, 1, me, (*x_nb, c), src=own(a)),
                      copy(a, 2, me, (*y_nb, c), src=own(a))]
            second += [copy(a, 3, (*relay_src, c), (*relay_dst, c)), copy(a, 4, (*x_nb, c), sibling),
                       copy(a, 5, (*y_nb, c), sibling)]
            third.append(copy(a, 6, (*diag, c), sibling))
        return sibling, c, x_nb, y_nb, diag, copy, mine, first, second, third

    def start(ins, outs, sems):
        _, _, _, _, _, _, mine, first, _, _ = parts(ins, outs, sems)
        for cp in mine + first:
            cp.start()

    def middle(ins, outs, sems):
        _, c, x_nb, y_nb, _, copy, _, _, second, _ = parts(ins, outs, sems)
        for a in range(na):
            copy(a, 1, (*x_nb, c), (*x_nb, c)).wait_recv()
            copy(a, 2, (*y_nb, c), (*y_nb, c)).wait_recv()
        for cp in second:
            cp.start()

    def finish(ins, outs, sems):
        sibling, c, x_nb, y_nb, diag, copy, mine, first, second, third = parts(ins, outs, sems)
        for a in range(na):
            copy(a, 3, (*diag, c), (*diag, c)).wait_recv()
            third[a].start()
        for a in range(na):
            copy(a, 0, sibling, sibling).wait_recv()
            copy(a, 4, (*x_nb, 1 - c), sibling).wait_recv()
            copy(a, 5, (*y_nb, 1 - c), sibling).wait_recv()
            copy(a, 6, (*diag, 1 - c), sibling).wait_recv()
        for cp in first + second + third:
            cp.wait_send()
        for cp in mine:
            cp.wait()

    return types.SimpleNamespace(
        ins=list(shards) + list(bufs or []),
        out_shapes=[jax.ShapeDtypeStruct((N_DEV,) + s.shape, s.dtype) for s in shards],
        sems=[pltpu.SemaphoreType.DMA((n_kinds * na,)), pltpu.SemaphoreType.DMA((n_kinds * na,)),
              pltpu.SemaphoreType.DMA((na,))],
        aliases=[(na + a, a) for a in range(na)] if bufs else [],
        peers=frozenset({"sibling", "neighbours"}), start=start, middle=middle, finish=finish)


def _rs_sibling_plan(fulls):
    na = len(fulls)
    rs = [f.shape[0] // N_DEV for f in fulls]

    def copies(ins, outs, sems):
        send_sems, recv_sems = sems
        x, y, c = _place()
        out = []
        for a in range(na):
            for q in range(4):
                shard = 2 * q + (1 - c)
                out.append(pltpu.make_async_remote_copy(
                    src_ref=ins[a].at[pl.ds(shard * rs[a], rs[a])],
                    dst_ref=outs[a].at[q],
                    send_sem=send_sems.at[a * 4 + q],
                    recv_sem=recv_sems.at[a * 4 + q],
                    device_id=(x, y, 1 - c),
                    device_id_type=MESH,
                ))
        return out

    def start(ins, outs, sems):
        for cp in copies(ins, outs, sems):
            cp.start()

    def finish(ins, outs, sems):
        for cp in copies(ins, outs, sems):
            cp.wait()

    return types.SimpleNamespace(
        ins=list(fulls),
        out_shapes=[jax.ShapeDtypeStruct((4, r) + f.shape[1:], f.dtype) for r, f in zip(rs, fulls)],
        sems=[pltpu.SemaphoreType.DMA((4 * na,)), pltpu.SemaphoreType.DMA((4 * na,))],
        peers=frozenset({"sibling"}), start=start, finish=finish)


def _rs_chips_plan(sends, pieces=None, bufs=None):
    na = len(sends)

    def copies(ins, outs, sems):
        send_sems, recv_sems = sems
        x, y, c = _place()
        chips = [(1 - x, y), (x, 1 - y), (1 - x, 1 - y)]
        out = []
        for a in range(na):
            for k, chip in enumerate(chips):
                rows = (k,) if pieces is None else (k, pl.ds(*pieces[a]))
                out.append(pltpu.make_async_remote_copy(
                    src_ref=ins[a].at[rows],
                    dst_ref=outs[a].at[rows],
                    send_sem=send_sems.at[a * 3 + k],
                    recv_sem=recv_sems.at[a * 3 + k],
                    device_id=(*chip, c),
                    device_id_type=MESH,
                ))
        return out

    def start(ins, outs, sems):
        for cp in copies(ins, outs, sems):
            cp.start()

    def finish(ins, outs, sems):
        for cp in copies(ins, outs, sems):
            cp.wait()

    return types.SimpleNamespace(
        ins=list(sends) + list(bufs or []),
        out_shapes=[jax.ShapeDtypeStruct(s.shape, s.dtype) for s in sends],
        sems=[pltpu.SemaphoreType.DMA((3 * na,)), pltpu.SemaphoreType.DMA((3 * na,))],
        aliases=[(na + a, a) for a in range(na)] if bufs else [],
        peers=frozenset({"chips"}), start=start, finish=finish)


def _join(plans):
    ins, outs, sems, aliases, offs = [], [], [], [], []
    for p in plans:
        offs.append((len(ins), len(outs), len(sems)))
        aliases += [(len(ins) + ci, len(outs) + co) for ci, co in getattr(p, "aliases", [])]
        ins += p.ins
        outs += p.out_shapes
        sems += p.sems

    def cut(p, off, i, o, s):
        return (i[off[0]:off[0] + len(p.ins)], o[off[1]:off[1] + len(p.out_shapes)],
                s[off[2]:off[2] + len(p.sems)])

    def start(i, o, s):
        for p, off in zip(plans, offs):
            p.start(*cut(p, off, i, o, s))

    def middle(i, o, s):
        for p, off in zip(plans, offs):
            if getattr(p, "middle", None) is not None:
                p.middle(*cut(p, off, i, o, s))

    def finish(i, o, s):
        for p, off in zip(plans, offs):
            p.finish(*cut(p, off, i, o, s))

    def split(results):
        return [list(results[off[1]:off[1] + len(p.out_shapes)]) for p, off in zip(plans, offs)]

    return types.SimpleNamespace(ins=ins, out_shapes=outs, sems=sems, aliases=aliases,
                                 peers=frozenset().union(*[p.peers for p in plans]),
                                 start=start, middle=middle, finish=finish, split=split)


COLLECTIVE_ID = {frozenset({"sibling"}): 0, frozenset({"chips"}): 1, frozenset({"sibling", "chips"}): 2,
                 frozenset({"sibling", "neighbours"}): 3}


def _handshake(peers):
    x, y, c = _place()
    devs = []
    if "sibling" in peers:
        devs.append((x, y, 1 - c))
    if "neighbours" in peers:
        devs += [(1 - x, y, c), (x, 1 - y, c)]
    if "chips" in peers:
        assert "neighbours" not in peers
        devs += [(1 - x, y, c), (x, 1 - y, c), (1 - x, 1 - y, c)]
    barrier = pltpu.get_barrier_semaphore()
    for dev in devs:
        pl.semaphore_signal(barrier, inc=1, device_id=dev, device_id_type=MESH)
    pl.semaphore_wait(barrier, len(devs))


def _in_hbm(args):
    return [pltpu.with_memory_space_constraint(a, pltpu.HBM) for a in args]


def _run_plan(plan, name):
    n_in, n_out = len(plan.ins), len(plan.out_shapes)

    def body(*refs):
        ins, outs, sems = refs[:n_in], refs[n_in:n_in + n_out], refs[n_in + n_out:]
        _handshake(plan.peers)
        plan.start(ins, outs, sems)
        if getattr(plan, "middle", None) is not None:
            plan.middle(ins, outs, sems)
        plan.finish(ins, outs, sems)

    return pl.pallas_call(
        body,
        name=name,
        in_specs=[ANY] * n_in,
        out_specs=[ANY] * n_out,
        out_shape=plan.out_shapes,
        scratch_shapes=plan.sems,
        input_output_aliases=dict(getattr(plan, "aliases", [])),
        compiler_params=pltpu.CompilerParams(collective_id=COLLECTIVE_ID[plan.peers]),
    )(*_in_hbm(plan.ins))


def _call(body, *, name, grid, in_specs, out_specs, out_shape, args, scratch_shapes=(), aliases=None,
          carry=None):
    n_in, n_out, n_scr = len(in_specs), len(out_shape), len(scratch_shapes)
    params = pltpu.CompilerParams(
        dimension_semantics=("arbitrary",) * len(grid), vmem_limit_bytes=V7X_VMEM_LIMIT_BYTES)
    if carry is None:
        outs = pl.pallas_call(
            body, name=name, grid=grid, in_specs=list(in_specs), out_specs=list(out_specs),
            out_shape=list(out_shape), scratch_shapes=list(scratch_shapes),
            input_output_aliases=aliases or {}, compiler_params=params)(*_in_hbm(args))
        return list(outs), []
    c_in, c_out = len(carry.ins), len(carry.out_shapes)

    def full(*refs):
        p = 0
        ins = refs[p:p + n_in]
        p += n_in
        cins = refs[p:p + c_in]
        p += c_in
        outs = refs[p:p + n_out]
        p += n_out
        couts = refs[p:p + c_out]
        p += c_out
        scr = refs[p:p + n_scr]
        csems = refs[p + n_scr:]
        ids = [pl.program_id(a) for a in range(len(grid))]
        first = functools.reduce(operator.and_, [i == 0 for i in ids])
        last = functools.reduce(operator.and_, [i == g - 1 for i, g in zip(ids, grid)])

        @pl.when(first)
        def _():
            _handshake(carry.peers)
            carry.start(cins, couts, csems)

        if getattr(carry, "middle", None) is not None:
            n_steps = math.prod(grid)
            flat = functools.reduce(lambda acc, ig: acc * ig[1] + ig[0], zip(ids, grid), 0)

            @pl.when(flat == (2 * n_steps) // 3)
            def _():
                carry.middle(cins, couts, csems)

        body(*ins, *outs, *scr)

        @pl.when(last)
        def _():
            carry.finish(cins, couts, csems)

    all_aliases = dict(aliases or {})
    all_aliases.update({n_in + ci: n_out + co for ci, co in getattr(carry, "aliases", [])})
    params = pltpu.CompilerParams(
        dimension_semantics=("arbitrary",) * len(grid), vmem_limit_bytes=V7X_VMEM_LIMIT_BYTES,
        collective_id=COLLECTIVE_ID[carry.peers])
    outs = pl.pallas_call(
        full, name=name, grid=grid,
        in_specs=list(in_specs) + [ANY] * c_in,
        out_specs=list(out_specs) + [ANY] * c_out,
        out_shape=list(out_shape) + list(carry.out_shapes),
        scratch_shapes=list(scratch_shapes) + list(carry.sems),
        input_output_aliases=all_aliases, compiler_params=params)(*_in_hbm(args), *_in_hbm(carry.ins))
    return list(outs[:n_out]), list(outs[n_out:])


def _norm_proj(x, g1, w_int, carry=None):
    t, d = x.shape
    n = w_int.shape[0]
    tt, tn = _tile(t, 2048), _tile(n, 512)

    def body(x_ref, g_ref, w_ref, proj_ref, h1_ref, h1_s):
        @pl.when(pl.program_id(1) == 0)
        def _():
            def norm_rows(rows):
                xhat, _ = _rms_hat(x_ref[rows, :])
                h = (xhat * g_ref[...]).astype(BF16)
                h1_s[rows, :] = h
                h1_ref[rows, :] = h

            _row_chunks(tt, norm_rows)

        proj_ref[...] = _dot_nt(h1_s[...], w_ref[...]).astype(BF16)

    return _call(
        body, name="norm_proj", grid=(t // tt, n // tn),
        in_specs=[
            pl.BlockSpec((tt, d), lambda i, j: (i, 0)),
            pl.BlockSpec((1, d), lambda i, j: (0, 0)),
            pl.BlockSpec((tn, d), lambda i, j: (j, 0)),
        ],
        out_specs=[
            pl.BlockSpec((tt, tn), lambda i, j: (i, j)),
            pl.BlockSpec((tt, d), lambda i, j: (i, 0)),
        ],
        out_shape=[jax.ShapeDtypeStruct((t, n), BF16), jax.ShapeDtypeStruct((t, d), BF16)],
        scratch_shapes=[pltpu.VMEM((tt, d), BF16)],
        args=(x, g1, w_int), carry=carry)


def _scan_rows(av, bv, reverse):
    tc = av.shape[0]
    row = lax.broadcasted_iota(jnp.int32, av.shape, 0)
    s = 1
    while s < tc:
        if s < 8:
            keep = (row < tc - s) if reverse else (row >= s)
            shift = (tc - s) if reverse else s
            a_sh = jnp.where(keep, pltpu.roll(av, shift, 0), 1.0)
            b_sh = jnp.where(keep, pltpu.roll(bv, shift, 0), 0.0)
            bv = av * b_sh + bv
            av = av * a_sh
        elif reverse:
            bv = jnp.concatenate([av[:tc - s] * bv[s:] + bv[:tc - s], bv[tc - s:]], axis=0)
            av = jnp.concatenate([av[:tc - s] * av[s:], av[tc - s:]], axis=0)
        else:
            bv = jnp.concatenate([bv[:s], av[s:] * bv[:tc - s] + bv[s:]], axis=0)
            av = jnp.concatenate([av[:s], av[s:] * av[:tc - s]], axis=0)
        s *= 2
    return av, bv


N_LRU_SAVED = 5


def _fill_block_diag(w_ref, bd_ref):
    bd_ref[...] = jnp.zeros_like(bd_ref)
    hd = LRU_HEAD_DIM
    for k in range(w_ref.shape[0]):
        bd_ref[k * hd:(k + 1) * hd, k * hd:(k + 1) * hd] = w_ref[k].astype(BF16)


def _lru_fwd(proj, conv_w, conv_b, w_a, b_a, w_x, b_x, lam, carry=None):
    t = proj.shape[0]
    dr = conv_b.shape[1]
    cb = LRU_CB
    tc = _tile(t, 256)
    ncb, ntc = dr // cb, t // tc

    def body(xp_ref, g_ref, cw_ref, cb_ref, wa_ref, ba_ref, wx_ref, bx_ref, lam_ref,
             y_ref, h_ref, saved_ref, prevx_s, hlast_s, wa_s, wx_s):
        c = pl.program_id(1)

        @pl.when(c == 0)
        def _():
            prevx_s[...] = jnp.zeros_like(prevx_s)
            hlast_s[...] = jnp.zeros_like(hlast_s)
            _fill_block_diag(wa_ref, wa_s)
            _fill_block_diag(wx_ref, wx_s)

        x = xp_ref[...].astype(F32)
        prev = prevx_s[...]
        row = lax.broadcasted_iota(jnp.int32, x.shape, 0)

        def sh(j):
            return jnp.where(row >= j, pltpu.roll(x, j, 0), pltpu.roll(prev, j, 0))

        xc = (cb_ref[...] + cw_ref[0:1, :] * sh(3) + cw_ref[1:2, :] * sh(2)
              + cw_ref[2:3, :] * sh(1) + cw_ref[3:4, :] * x)
        prevx_s[...] = x
        _, r, i, _, _, a, mult = _lru_gates(xc, wa_s[...], ba_ref[...], wx_s[...], bx_ref[...],
                                            lam_ref[...])
        for k, val in enumerate((xc, r, i, a, mult)):
            saved_ref[:, k * cb:(k + 1) * cb] = val
        av, bv = _scan_rows(a, mult * (i * xc), reverse=False)
        h = av * hlast_s[...] + bv
        h_ref[...] = h
        hlast_s[...] = h_ref[tc - 1:tc, :]
        gel, _ = _gelu_and_grad(g_ref[...].astype(F32))
        y_ref[...] = (h * gel).astype(BF16)

    vec = pl.BlockSpec((1, cb), lambda j, c: (0, j))
    blk = pl.BlockSpec((tc, cb), lambda j, c: (c, j))
    mat = pl.BlockSpec((cb // LRU_HEAD_DIM, LRU_HEAD_DIM, LRU_HEAD_DIM), lambda j, c: (j, 0, 0))
    return _call(
        body, name="lru_fwd", grid=(ncb, ntc),
        in_specs=[
            blk,
            pl.BlockSpec((tc, cb), lambda j, c: (c, ncb + j)),
            pl.BlockSpec((4, cb), lambda j, c: (0, j)),
            vec, mat, vec, mat, vec, vec,
        ],
        out_specs=[blk, blk, pl.BlockSpec((tc, N_LRU_SAVED * cb), lambda j, c: (c, j))],
        out_shape=[jax.ShapeDtypeStruct((t, dr), BF16), jax.ShapeDtypeStruct((t, dr), F32),
                   jax.ShapeDtypeStruct((t, N_LRU_SAVED * dr), F32)],
        scratch_shapes=[pltpu.VMEM((tc, cb), F32), pltpu.VMEM((1, cb), F32),
                        pltpu.VMEM((cb, cb), BF16), pltpu.VMEM((cb, cb), BF16)],
        args=(proj, proj, conv_w, conv_b, w_a, b_a, w_x, b_x, lam), carry=carry)


def _pool_select(col, vals):
    out = vals[3]
    for g in (2, 1, 0):
        out = jnp.where(col < (g + 1) * POOL_GROUP_DIM, vals[g], out)
    return out


def _pool_fwd(proj, pool_w, pool_scale, col_block):
    t = proj.shape[0]
    dp = pool_scale.shape[1]
    tc = _tile(t, 256)
    ntc = t // tc

    def body(x_ref, w_ref, sc_ref, y_ref, p_ref, px, p2, p4, p8):
        c = pl.program_id(0)

        @pl.when(c == 0)
        def _():
            for s in (px, p2, p4, p8):
                s[...] = jnp.zeros_like(s)

        x = x_ref[...].astype(F32)
        row = lax.broadcasted_iota(jnp.int32, x.shape, 0)
        col = lax.broadcasted_iota(jnp.int32, x.shape, 1)

        def sh(v, pv, j):
            return jnp.where(row >= j, pltpu.roll(v, j, 0), pltpu.roll(pv[...], j, 0))

        s2 = x + sh(x, px, 1)
        s4 = s2 + sh(s2, p2, 2)
        s8 = s4 + sh(s4, p4, 4)
        s16 = s8 + sh(s8, p8, 8)
        px[...] = x
        p2[...] = s2
        p4[...] = s4
        p8[...] = s8
        wsum = _pool_select(col, (s2, s4, s8, s16))
        win = _pool_select(col, POOL_WINDOWS)
        cnt = jnp.minimum(c * tc + row + 1, win).astype(F32)
        p = wsum / cnt - x
        pb = p.astype(BF16)
        p_ref[...] = pb
        for g in range(len(POOL_WINDOWS)):
            sl = slice(g * POOL_GROUP_DIM, (g + 1) * POOL_GROUP_DIM)
            yg = _dot_nn(pb[:, sl], w_ref[g]) * sc_ref[:, sl]
            y_ref[:, sl] = yg.astype(BF16)

    return _call(
        body, name="pool_fwd", grid=(ntc,),
        in_specs=[
            pl.BlockSpec((tc, dp), lambda c: (c, col_block)),
            pl.BlockSpec(pool_w.shape, lambda c: (0, 0, 0)),
            pl.BlockSpec((1, dp), lambda c: (0, 0)),
        ],
        out_specs=[pl.BlockSpec((tc, dp), lambda c: (c, 0))] * 2,
        out_shape=[jax.ShapeDtypeStruct((t, dp), BF16)] * 2,
        scratch_shapes=[pltpu.VMEM((tc, dp), F32)] * 4,
        args=(proj, pool_w, pool_scale))[0]


def _branch_mix(y_lru, y_pool, w_lru_up, w_pool_upb, proj, b_gate, ga_block, gb_block, carry=None):
    t, d = y_lru.shape
    dp = y_pool.shape[1]
    bw = w_pool_upb.shape[2]
    tt, tn = _tile(t, 1024), 512
    nj = d // tn

    def body(yl_ref, yp_ref, wl_ref, wp_ref, ga_ref, gb_ref, ba_ref, bb_ref, bra_ref, brb_ref, mix_ref):
        br_a = _dot_nn(yl_ref[...], wl_ref[...])
        wp = jnp.concatenate([wp_ref[b] for b in range(tn // bw)], axis=1)
        br_b = _dot_nn(yp_ref[...], wp)
        bra_ref[...] = br_a.astype(BF16)
        brb_ref[...] = br_b.astype(BF16)
        ga = _sig(ga_ref[...].astype(F32) + ba_ref[...])
        gb = _sig(gb_ref[...].astype(F32) + bb_ref[...])
        mix_ref[...] = (ga * br_a + gb * br_b).astype(BF16)

    out = pl.BlockSpec((tt, tn), lambda j, i: (i, j))
    return _call(
        body, name="branch_mix", grid=(nj, t // tt),
        in_specs=[
            pl.BlockSpec((tt, d), lambda j, i: (i, 0)),
            pl.BlockSpec((tt, dp), lambda j, i: (i, 0)),
            pl.BlockSpec((d, tn), lambda j, i: (0, j)),
            pl.BlockSpec((tn // bw, dp, bw), lambda j, i: (j, 0, 0)),
            pl.BlockSpec((tt, tn), lambda j, i: (i, ga_block + j)),
            pl.BlockSpec((tt, tn), lambda j, i: (i, gb_block + j)),
            pl.BlockSpec((1, tn), lambda j, i: (0, j)),
            pl.BlockSpec((1, tn), lambda j, i: (0, nj + j)),
        ],
        out_specs=[out, out, out],
        out_shape=[jax.ShapeDtypeStruct((t, d), BF16)] * 3,
        args=(y_lru, y_pool, w_lru_up, w_pool_upb, proj, proj, b_gate, b_gate), carry=carry)


def _wo_norm(mix, w_o, x, g2, g3, carry=None):
    t, d = x.shape
    tt = _tile(t, 512)

    def body(mix_ref, w_ref, x_ref, g2_ref, g3_ref, m_ref, x2_ref, h3_ref):
        m = _dot_nn(mix_ref[...], w_ref[...])
        m_ref[...] = m
        mhat, _ = _rms_hat(m)
        x2 = x_ref[...] + mhat * g2_ref[...]
        x2_ref[...] = x2
        xhat, _ = _rms_hat(x2)
        h3_ref[...] = (xhat * g3_ref[...]).astype(BF16)

    row = pl.BlockSpec((tt, d), lambda i: (i, 0))
    vec = pl.BlockSpec((1, d), lambda i: (0, 0))
    return _call(
        body, name="wo_norm", grid=(t // tt,),
        in_specs=[row, pl.BlockSpec((d, d), lambda i: (0, 0)), row, vec, vec],
        out_specs=[row, row, row],
        out_shape=[
            jax.ShapeDtypeStruct((t, d), F32),
            jax.ShapeDtypeStruct((t, d), F32),
            jax.ShapeDtypeStruct((t, d), BF16),
        ],
        args=(mix, w_o, x, g2, g3), carry=carry)


def _ff1(h3, w_ff1b, carry=None):
    t, d = h3.shape
    nb, _, tn = w_ff1b.shape
    tt = _tile(t, 2048)

    def body(h_ref, w_ref, rf_ref):
        rf_ref[...] = jnp.maximum(_dot_nn(h_ref[...], w_ref[...]), 0.0).astype(BF16)

    out = pl.BlockSpec((tt, tn), lambda i, j: (i, j))
    return _call(
        body, name="ff1", grid=(t // tt, nb),
        in_specs=[pl.BlockSpec((tt, d), lambda i, j: (i, 0)), pl.BlockSpec((None, d, tn), lambda i, j: (j, 0, 0))],
        out_specs=[out],
        out_shape=[jax.ShapeDtypeStruct((t, nb * tn), BF16)],
        args=(h3, w_ff1b), carry=carry)


def _ff2_loss(rf, w_ff2, x2, g4, target):
    t, k = rf.shape
    d = x2.shape[1]
    tt, tk = _tile(t, 1024), _tile(k, 1024)
    nk = k // tk

    def body(a_ref, w_ref, x2_hbm, g_ref, tg_hbm, dy_hbm, df_hbm, dg_ref, loss_ref, acc, x2_ref, tg_ref,
             late_sems, dy_ref, df_ref, out_sems):
        i, kk = pl.program_id(0), pl.program_id(1)
        late = _late_copies(i, tt, [(x2_hbm, x2_ref), (tg_hbm, tg_ref)], late_sems)

        @pl.when(kk == 0)
        def _():
            acc[...] = jnp.zeros_like(acc)
            for cp in late:
                cp.start()

        @pl.when((i == 0) & (kk == 0))
        def _():
            dg_ref[...] = jnp.zeros_like(dg_ref)
            loss_ref[...] = jnp.zeros_like(loss_ref)

        rf_tile = a_ref[...]
        acc[...] += _dot_nn(rf_tile * rf_tile, w_ref[...])

        @pl.when(kk == nk - 1)
        def _():
            for cp in late:
                cp.wait()

            def tail(rows):
                fhat, r = _rms_hat(acc[rows, :])
                g = g_ref[...]
                e = x2_ref[rows, :] + fhat * g - tg_ref[rows, :]
                loss_ref[...] += 0.5 * jnp.sum(jnp.mean(e * e, axis=-1, keepdims=True))
                dy = e * (1.0 / d)
                dy_ref[rows, :] = dy.astype(BF16)
                df, dg = _rms_bwd(dy, fhat, r, g)
                df_ref[rows, :] = df.astype(BF16)
                dg_ref[...] += dg
                dst = pl.ds(tile_start + rows.start, rows.size)
                pltpu.make_async_copy(dy_ref.at[rows], dy_hbm.at[dst], out_sems.at[0]).start()
                pltpu.make_async_copy(df_ref.at[rows], df_hbm.at[dst], out_sems.at[1]).start()

            tile_start = pl.multiple_of(i * tt, tt)
            _row_chunks(tt, tail)
            tile = pl.ds(tile_start, tt)
            pltpu.make_async_copy(dy_ref, dy_hbm.at[tile], out_sems.at[0]).wait()
            pltpu.make_async_copy(df_ref, df_hbm.at[tile], out_sems.at[1]).wait()

    vec = pl.BlockSpec((1, d), lambda i, kk: (0, 0))
    return _call(
        body, name="ff2_loss", grid=(t // tt, nk),
        in_specs=[
            pl.BlockSpec((tt, tk), lambda i, kk: (i, kk)),
            pl.BlockSpec((tk, d), lambda i, kk: (kk, 0)),
            ANY, vec, ANY,
        ],
        out_specs=[ANY, ANY, vec, pl.BlockSpec((1, 128), lambda i, kk: (0, 0))],
        out_shape=[
            jax.ShapeDtypeStruct((t, d), BF16),
            jax.ShapeDtypeStruct((t, d), BF16),
            jax.ShapeDtypeStruct((1, d), F32),
            jax.ShapeDtypeStruct((1, 128), F32),
        ],
        scratch_shapes=[pltpu.VMEM((tt, d), F32), pltpu.VMEM((tt, d), x2.dtype), pltpu.VMEM((tt, d), target.dtype),
                        pltpu.SemaphoreType.DMA((2,)), pltpu.VMEM((tt, d), BF16), pltpu.VMEM((tt, d), BF16),
                        pltpu.SemaphoreType.DMA((2,))],
        args=(rf, w_ff2, x2, g4, target))[0]


def _ff2_bwd(df, w_ff2, rf, carry=None):
    t, d = df.shape
    n = w_ff2.shape[0]
    tt, tn = _tile(t, 2048), _tile(n, 512)

    def body(df_ref, w_ref, rf_ref, out_ref):
        d_act = _dot_nt(df_ref[...], w_ref[...])
        out_ref[...] = (d_act * (2.0 * rf_ref[...].astype(F32))).astype(BF16)

    blk = pl.BlockSpec((tt, tn), lambda i, j: (i, j))
    return _call(
        body, name="ff2_bwd", grid=(t // tt, n // tn),
        in_specs=[pl.BlockSpec((tt, d), lambda i, j: (i, 0)), pl.BlockSpec((tn, d), lambda i, j: (j, 0)), blk],
        out_specs=[blk],
        out_shape=[jax.ShapeDtypeStruct((t, n), BF16)],
        args=(df, w_ff2, rf), carry=carry)


def _wgrad(a, b, name, prev=None, row_off=0, rows=None, carry=None, square_a=False):
    t, m = a.shape
    n = b.shape[1]
    rows = m if rows is None else rows
    tm, tk = _tile(m, 512), _tile(t, 2048)
    nk = t // tk
    assert row_off % tm == 0
    off = row_off // tm

    def body(*refs):
        a_ref, b_ref = refs[0], refs[1]
        o32_ref, o16_ref, acc = refs[-3], refs[-2], refs[-1]
        if nk == 1:
            a_all = a_ref[...]
            res = _dot_tn(a_all * a_all if square_a else a_all, b_ref[...])
            o32_ref[...] = res
            o16_ref[...] = res.astype(BF16)
            return
        kk = pl.program_id(1)

        @pl.when(kk == 0)
        def _():
            acc[...] = jnp.zeros_like(acc)

        a_tile = a_ref[...]
        acc[...] += _dot_tn(a_tile * a_tile if square_a else a_tile, b_ref[...])

        @pl.when(kk == nk - 1)
        def _():
            o32_ref[...] = acc[...]
            o16_ref[...] = acc[...].astype(BF16)

    in_specs = [pl.BlockSpec((tk, tm), lambda i, kk: (kk, i)), pl.BlockSpec((tk, n), lambda i, kk: (kk, 0))]
    args = [a, b]
    aliases = {}
    if prev is not None:
        in_specs += [ANY, ANY]
        args += list(prev)
        aliases = {2: 0, 3: 1}
    out = pl.BlockSpec((tm, n), lambda i, kk: (off + i, 0))
    return _call(
        body, name=name, grid=(m // tm, nk),
        in_specs=in_specs, out_specs=[out, out],
        out_shape=[jax.ShapeDtypeStruct((rows, n), F32), jax.ShapeDtypeStruct((rows, n), BF16)],
        scratch_shapes=[pltpu.VMEM((tm, n), F32)],
        aliases=aliases, args=args, carry=carry)


def _wgrad_parts(parts, b, name, carry=None):
    t, n = b.shape
    tm = 512
    bounds = []
    lo = 0
    for part in parts:
        assert part.shape[0] == t and part.shape[1] % tm == 0
        bounds.append((lo, lo + part.shape[1] // tm))
        lo += part.shape[1] // tm
    nm = lo
    np_ = len(parts)

    def body(*refs):
        p_refs, b_ref, o32_ref, o16_ref = refs[:np_], refs[np_], refs[np_ + 1], refs[np_ + 2]
        i = pl.program_id(0)
        for (lo_p, hi_p), p_ref in zip(bounds, p_refs):
            @pl.when((i >= lo_p) & (i < hi_p))
            def _(p_ref=p_ref):
                res = _dot_tn(p_ref[...], b_ref[...])
                o32_ref[...] = res
                o16_ref[...] = res.astype(BF16)

    def part_spec(lo_p, hi_p):
        return pl.BlockSpec((t, tm), lambda i: (0, jnp.clip(i - lo_p, 0, hi_p - lo_p - 1)))

    out = pl.BlockSpec((tm, n), lambda i: (i, 0))
    return _call(
        body, name=name, grid=(nm,),
        in_specs=[part_spec(lo_p, hi_p) for lo_p, hi_p in bounds] + [pl.BlockSpec((t, n), lambda i: (0, 0))],
        out_specs=[out, out],
        out_shape=[jax.ShapeDtypeStruct((nm * tm, n), F32), jax.ShapeDtypeStruct((nm * tm, n), BF16)],
        args=(*parts, b), carry=carry)


def _wgrad_cols(a, b, bw, tn, name, carry=None):
    t, m = a.shape
    n = b.shape[1]
    per_step = tn // bw

    def body(a_ref, b_ref, o32_ref, o16_ref):
        res = _dot_tn(a_ref[...], b_ref[...])
        for blk in range(per_step):
            part = res[:, blk * bw:(blk + 1) * bw]
            o32_ref[blk] = part
            o16_ref[blk] = part.astype(BF16)

    out = pl.BlockSpec((per_step, m, bw), lambda j: (j, 0, 0))
    return _call(
        body, name=name, grid=(n // tn,),
        in_specs=[pl.BlockSpec((t, m), lambda j: (0, 0)), pl.BlockSpec((t, tn), lambda j: (0, j))],
        out_specs=[out, out],
        out_shape=[jax.ShapeDtypeStruct((n // bw, m, bw), F32), jax.ShapeDtypeStruct((n // bw, m, bw), BF16)],
        args=(a, b), carry=carry)


def _ff1_bwd_norms(d_f1, w_ff1b, dy, x2, g3, m, g2, carry=None):
    t, k = d_f1.shape
    d = x2.shape[1]
    assert dy.dtype == BF16 and x2.dtype == F32
    bw = w_ff1b.shape[2]
    per_step = 2
    tt, tk = _tile(t, 1024), per_step * bw
    nk = k // tk

    def body(a_ref, w_ref, dy_hbm, x2_hbm, g3_ref, m_hbm, g2_ref, dx2_hbm, dm_hbm, dg3_ref, dg2_ref, acc,
             dy_ref, x2_ref, m_ref, late_sems, out_sems):
        i, kk = pl.program_id(0), pl.program_id(1)
        late = _late_copies(i, tt, [(dy_hbm, dy_ref), (x2_hbm, x2_ref), (m_hbm, m_ref)], late_sems)

        @pl.when(kk == 0)
        def _():
            acc[...] = jnp.zeros_like(acc)
            for cp in late:
                cp.start()

        @pl.when((i == 0) & (kk == 0))
        def _():
            dg3_ref[...] = jnp.zeros_like(dg3_ref)
            dg2_ref[...] = jnp.zeros_like(dg2_ref)

        a_tile = a_ref[...]
        for b in range(per_step):
            acc[...] += _dot_nt(a_tile[:, b * bw:(b + 1) * bw], w_ref[b])

        @pl.when(kk == nk - 1)
        def _():
            for cp in late:
                cp.wait()

            def tail(rows):
                xhat, r3 = _rms_hat(x2_ref[rows, :])
                dx, dg3 = _rms_bwd(acc[rows, :], xhat, r3, g3_ref[...])
                dx2 = dy_ref[rows, :].astype(F32) + dx
                x2_ref[rows, :] = dx2
                dg3_ref[...] += dg3
                mhat, r2 = _rms_hat(m_ref[rows, :])
                dm, dg2 = _rms_bwd(dx2, mhat, r2, g2_ref[...])
                dy_ref[rows, :] = dm.astype(BF16)
                dg2_ref[...] += dg2

            _row_chunks(tt, tail)
            tile = pl.ds(pl.multiple_of(i * tt, tt), tt)
            outs = [pltpu.make_async_copy(x2_ref, dx2_hbm.at[tile], out_sems.at[0]),
                    pltpu.make_async_copy(dy_ref, dm_hbm.at[tile], out_sems.at[1])]
            for cp in outs:
                cp.start()
            for cp in outs:
                cp.wait()

    vec = pl.BlockSpec((1, d), lambda i, kk: (0, 0))
    return _call(
        body, name="ff1_bwd_norms", grid=(t // tt, nk),
        in_specs=[
            pl.BlockSpec((tt, tk), lambda i, kk: (i, kk)),
            pl.BlockSpec((per_step, d, bw), lambda i, kk: (kk, 0, 0)),
            ANY, ANY, vec, ANY, vec,
        ],
        out_specs=[ANY, ANY, vec, vec],
        out_shape=[
            jax.ShapeDtypeStruct((t, d), F32),
            jax.ShapeDtypeStruct((t, d), BF16),
            jax.ShapeDtypeStruct((1, d), F32),
            jax.ShapeDtypeStruct((1, d), F32),
        ],
        scratch_shapes=[pltpu.VMEM((tt, d), F32), pltpu.VMEM((tt, d), dy.dtype), pltpu.VMEM((tt, d), F32),
                        pltpu.VMEM((tt, d), F32), pltpu.SemaphoreType.DMA((3,)), pltpu.SemaphoreType.DMA((2,))],
        args=(d_f1, w_ff1b, dy, x2, g3, m, g2), carry=carry)


def _wo_bwd_mix(dm, w_o, br_a, br_b, proj, b_gate, ga_block, gb_block, carry=None):
    t, d = dm.shape
    tt, tn = _tile(t, 1024), 512
    nj = d // tn

    def body(dm_ref, w_ref, bra_ref, brb_ref, ga_ref, gb_ref, ba_ref, bb_ref,
             dbra_ref, dbrb_ref, dga_ref, dgb_ref, dba_ref, dbb_ref):
        i = pl.program_id(1)

        @pl.when(i == 0)
        def _():
            dba_ref[...] = jnp.zeros_like(dba_ref)
            dbb_ref[...] = jnp.zeros_like(dbb_ref)

        d_mix = _dot_nt(dm_ref[...], w_ref[...])
        ga = _sig(ga_ref[...].astype(F32) + ba_ref[...])
        gb = _sig(gb_ref[...].astype(F32) + bb_ref[...])
        dbra_ref[...] = (d_mix * ga).astype(BF16)
        dbrb_ref[...] = (d_mix * gb).astype(BF16)
        dga = d_mix * bra_ref[...].astype(F32) * (ga * (1.0 - ga))
        dgb = d_mix * brb_ref[...].astype(F32) * (gb * (1.0 - gb))
        dga_ref[...] = dga.astype(BF16)
        dgb_ref[...] = dgb.astype(BF16)
        dba_ref[...] += jnp.sum(dga, axis=0, keepdims=True)
        dbb_ref[...] += jnp.sum(dgb, axis=0, keepdims=True)

    blk = pl.BlockSpec((tt, tn), lambda j, i: (i, j))
    vec = pl.BlockSpec((1, tn), lambda j, i: (0, j))
    return _call(
        body, name="wo_bwd_mix", grid=(nj, t // tt),
        in_specs=[
            pl.BlockSpec((tt, d), lambda j, i: (i, 0)),
            pl.BlockSpec((tn, d), lambda j, i: (j, 0)),
            blk, blk,
            pl.BlockSpec((tt, tn), lambda j, i: (i, ga_block + j)),
            pl.BlockSpec((tt, tn), lambda j, i: (i, gb_block + j)),
            vec,
            pl.BlockSpec((1, tn), lambda j, i: (0, nj + j)),
        ],
        out_specs=[blk, blk, blk, blk, vec, vec],
        out_shape=[jax.ShapeDtypeStruct((t, d), BF16)] * 4 + [jax.ShapeDtypeStruct((1, d), F32)] * 2,
        args=(dm, w_o, br_a, br_b, proj, proj, b_gate, b_gate), carry=carry)


def _lru_up_bwd(d_br_a, w_lru_up, proj, h, g_block, carry=None):
    t, d = d_br_a.shape
    tt, tn = _tile(t, 1024), 512

    def body(a_ref, w_ref, g_ref, h_ref, dh_ref, dg_ref):
        d_y = _dot_nt(a_ref[...], w_ref[...])
        gel, gel_grad = _gelu_and_grad(g_ref[...].astype(F32))
        dh_ref[...] = d_y * gel
        dg_ref[...] = (d_y * h_ref[...] * gel_grad).astype(BF16)

    blk = pl.BlockSpec((tt, tn), lambda i, j: (i, j))
    return _call(
        body, name="lru_up_bwd", grid=(t // tt, d // tn),
        in_specs=[
            pl.BlockSpec((tt, d), lambda i, j: (i, 0)),
            pl.BlockSpec((tn, d), lambda i, j: (j, 0)),
            pl.BlockSpec((tt, tn), lambda i, j: (i, g_block + j)),
            blk,
        ],
        out_specs=[blk, blk],
        out_shape=[jax.ShapeDtypeStruct((t, d), F32), jax.ShapeDtypeStruct((t, d), BF16)],
        args=(d_br_a, w_lru_up, proj, h), carry=carry)


def _lru_bwd(dh, h, saved, proj, conv_w, w_a, w_x, lam, carry=None):
    t, dr = dh.shape
    cb = LRU_CB
    hd = LRU_HEAD_DIM
    per = cb // hd
    tc = _tile(t, 256)
    ncb, ntc = dr // cb, t // tc

    def body(dh_ref, h_ref, hp_ref, saved_ref, xp_ref, cw_ref, wa_ref, wx_ref,
             lam_ref, dxp_ref, dwa_ref, dba_ref, dwx_ref, dbx_ref, dlam_ref, dcw_ref, dcb_ref,
             nextd_s, anext_s, gnext_s, tmp_s, wa_s, wx_s):
        c = pl.program_id(1)
        rc = ntc - 1 - c

        @pl.when(c == 0)
        def _():
            nextd_s[...] = jnp.zeros_like(nextd_s)
            anext_s[...] = jnp.zeros_like(anext_s)
            gnext_s[...] = jnp.zeros_like(gnext_s)
            for ref in (dwa_ref, dba_ref, dwx_ref, dbx_ref, dlam_ref, dcw_ref, dcb_ref):
                ref[...] = jnp.zeros_like(ref)
            _fill_block_diag(wa_ref, wa_s)
            _fill_block_diag(wx_ref, wx_s)

        xc, r, i, a, mult = [saved_ref[:, k * cb:(k + 1) * cb] for k in range(N_LRU_SAVED)]
        wa, wx, lam = wa_s[...], wx_s[...], lam_ref[...]
        xcb = xc.astype(BF16)
        sp = _softplus_neg(lam)
        row = lax.broadcasted_iota(jnp.int32, xc.shape, 0)
        h = h_ref[...]
        hp = jnp.where(rc == 0, 0.0, hp_ref[...])
        hprev = jnp.where(row >= 1, pltpu.roll(h, 1, 0), pltpu.roll(hp, 1, 0))

        def up(v, nv, j):
            return jnp.where(row < tc - j, pltpu.roll(v, tc - j, 0), nv)

        av, bv = _scan_rows(up(a, anext_s[...], 1), dh_ref[...], reverse=True)
        gt = av * gnext_s[...] + bv
        tmp_s[...] = gt
        gnext_s[...] = tmp_s[0:1, :]
        tmp_s[...] = a
        anext_s[...] = tmp_s[0:1, :]

        da = gt * hprev
        ixc = i * xc
        d_mult = gt * ixc
        d_i = gt * mult * xc
        d_xc = gt * mult * i
        d_log_a = da * a - d_mult * (a * a) / mult
        d_pre_r = (d_log_a * ((-LRU_C) * sp)) * (r * (1.0 - r))
        d_pre_i = d_i * (i * (1.0 - i))
        d_sp = jnp.sum(d_log_a * ((-LRU_C) * r), axis=0, keepdims=True)
        dlam_ref[...] += d_sp * (-1.0 / (1.0 + jnp.exp(lam)))
        dpr = d_pre_r.astype(BF16)
        dpi = d_pre_i.astype(BF16)
        dba_ref[...] += jnp.sum(d_pre_r, axis=0, keepdims=True)
        dbx_ref[...] += jnp.sum(d_pre_i, axis=0, keepdims=True)
        pa = _dot_tn(xcb, dpr)
        px = _dot_tn(xcb, dpi)
        for k in range(per):
            dwa_ref[k] += pa[k * hd:(k + 1) * hd, k * hd:(k + 1) * hd]
            dwx_ref[k] += px[k * hd:(k + 1) * hd, k * hd:(k + 1) * hd]
        d_xc = d_xc + _dot_nt(dpr, wa) + _dot_nt(dpi, wx)

        nxt = nextd_s[...]
        xp = xp_ref[...].astype(F32)
        dxp = cw_ref[3:4, :] * d_xc
        dcw_ref[3:4, :] += jnp.sum(xp * d_xc, axis=0, keepdims=True)
        for j in (1, 2, 3):
            uj = up(d_xc, pltpu.roll(nxt, tc - j, 0), j)
            dxp = dxp + cw_ref[3 - j:4 - j, :] * uj
            dcw_ref[3 - j:4 - j, :] += jnp.sum(xp * uj, axis=0, keepdims=True)
        dcb_ref[...] += jnp.sum(d_xc, axis=0, keepdims=True)
        nextd_s[...] = d_xc
        dxp_ref[...] = dxp.astype(BF16)

    vec = pl.BlockSpec((1, cb), lambda j, c: (0, j))
    blk = pl.BlockSpec((tc, cb), lambda j, c: (ntc - 1 - c, j))
    mat = pl.BlockSpec((per, hd, hd), lambda j, c: (j, 0, 0))
    cwb = pl.BlockSpec((4, cb), lambda j, c: (0, j))
    return _call(
        body, name="lru_bwd", grid=(ncb, ntc),
        in_specs=[
            blk, blk,
            pl.BlockSpec((tc, cb), lambda j, c: (jnp.maximum(ntc - 2 - c, 0), j)),
            pl.BlockSpec((tc, N_LRU_SAVED * cb), lambda j, c: (ntc - 1 - c, j)),
            blk, cwb, mat, mat, vec,
        ],
        out_specs=[blk, mat, vec, mat, vec, vec, cwb, vec],
        out_shape=[
            jax.ShapeDtypeStruct((t, dr), BF16),
            jax.ShapeDtypeStruct(w_a.shape, F32),
            jax.ShapeDtypeStruct((1, dr), F32),
            jax.ShapeDtypeStruct(w_x.shape, F32),
            jax.ShapeDtypeStruct((1, dr), F32),
            jax.ShapeDtypeStruct((1, dr), F32),
            jax.ShapeDtypeStruct((4, dr), F32),
            jax.ShapeDtypeStruct((1, dr), F32),
        ],
        scratch_shapes=[
            pltpu.VMEM((tc, cb), F32),
            pltpu.VMEM((1, cb), F32),
            pltpu.VMEM((1, cb), F32),
            pltpu.VMEM((tc, cb), F32),
            pltpu.VMEM((cb, cb), BF16),
            pltpu.VMEM((cb, cb), BF16),
        ],
        args=(dh, h, h, saved, proj, conv_w, w_a, w_x, lam), carry=carry)


def _pool_bwd(d_br_b, w_pool_upb, p, pool_w, pool_scale):
    t, d = d_br_b.shape
    nwb, dp, _ = w_pool_upb.shape
    tc = _tile(t, 256)
    ntc = t // tc
    ng = len(POOL_WINDOWS)

    def body(db_ref, wu_ref, p_ref, w_ref, sc_ref, dx_ref, dw_ref, dsc_ref, nz, n2, n4, n8, dp_s, dy_s):
        c = pl.program_id(0)
        rc = ntc - 1 - c

        @pl.when(c == 0)
        def _():
            for s in (nz, n2, n4, n8):
                s[...] = jnp.zeros_like(s)
            dw_ref[...] = jnp.zeros_like(dw_ref)
            dsc_ref[...] = jnp.zeros_like(dsc_ref)

        wu = jnp.concatenate([wu_ref[b] for b in range(nwb)], axis=1)
        dy_s[...] = _dot_nt(db_ref[...], wu)
        for g in range(ng):
            sl = slice(g * POOL_GROUP_DIM, (g + 1) * POOL_GROUP_DIM)
            pg = p_ref[:, sl]
            dyg = dy_s[:, sl]
            wg = w_ref[g].astype(BF16)
            q = _dot_nn(pg, wg)
            dsc_ref[:, sl] += jnp.sum(dyg * q, axis=0, keepdims=True)
            dpw = (dyg * sc_ref[:, sl]).astype(BF16)
            dw_ref[g] += _dot_tn(pg, dpw)
            dp_s[:, sl] = _dot_nt(dpw, wg)

        dpv = dp_s[...]
        row = lax.broadcasted_iota(jnp.int32, dpv.shape, 0)
        col = lax.broadcasted_iota(jnp.int32, dpv.shape, 1)
        win = _pool_select(col, POOL_WINDOWS)
        cnt = jnp.minimum(rc * tc + row + 1, win).astype(F32)
        z = dpv / cnt

        def up(v, nv, j):
            return jnp.where(row < tc - j, pltpu.roll(v, tc - j, 0), pltpu.roll(nv[...], tc - j, 0))

        u2 = z + up(z, nz, 1)
        u4 = u2 + up(u2, n2, 2)
        u8 = u4 + up(u4, n4, 4)
        u16 = u8 + up(u8, n8, 8)
        nz[...] = z
        n2[...] = u2
        n4[...] = u4
        n8[...] = u8
        dx_ref[...] = (_pool_select(col, (u2, u4, u8, u16)) - dpv).astype(BF16)

    blk = pl.BlockSpec((tc, dp), lambda c: (ntc - 1 - c, 0))
    full_w = pl.BlockSpec(pool_w.shape, lambda c: (0, 0, 0))
    vec = pl.BlockSpec((1, dp), lambda c: (0, 0))
    return _call(
        body, name="pool_bwd", grid=(ntc,),
        in_specs=[pl.BlockSpec((tc, d), lambda c: (ntc - 1 - c, 0)),
                  pl.BlockSpec(w_pool_upb.shape, lambda c: (0, 0, 0)), blk, full_w, vec],
        out_specs=[blk, full_w, vec],
        out_shape=[
            jax.ShapeDtypeStruct((t, dp), BF16),
            jax.ShapeDtypeStruct(pool_w.shape, F32),
            jax.ShapeDtypeStruct((1, dp), F32),
        ],
        scratch_shapes=[pltpu.VMEM((tc, dp), F32)] * 6,
        args=(d_br_b, w_pool_upb, p, pool_w, pool_scale))[0]


def _win_bwd_norm(parts, w_int, dx2, x, g1, carry=None):
    t, d = x.shape
    tk = 512
    tt = _tile(t, 1024)
    bounds = []
    k0 = 0
    for part in parts:
        assert part.shape[1] % tk == 0
        bounds.append((k0, k0 + part.shape[1] // tk))
        k0 += part.shape[1] // tk
    nk = k0
    assert nk * tk == w_int.shape[0]
    np_ = len(parts)

    def body(*refs):
        p_refs = refs[:np_]
        w_ref, dx2_hbm, x_hbm, g_ref, gx_hbm, dg_ref, acc, dx2_ref, x_ref, late_sems, out_sem = refs[np_:]
        i, kk = pl.program_id(0), pl.program_id(1)
        late = _late_copies(i, tt, [(dx2_hbm, dx2_ref), (x_hbm, x_ref)], late_sems)

        @pl.when(kk == 0)
        def _():
            acc[...] = jnp.zeros_like(acc)
            for cp in late:
                cp.start()

        @pl.when((i == 0) & (kk == 0))
        def _():
            dg_ref[...] = jnp.zeros_like(dg_ref)

        for (lo, hi), p_ref in zip(bounds, p_refs):
            @pl.when((kk >= lo) & (kk < hi))
            def _(p_ref=p_ref):
                acc[...] += _dot_nn(p_ref[...], w_ref[...])

        @pl.when(kk == nk - 1)
        def _():
            for cp in late:
                cp.wait()

            def tail(rows):
                xhat, r = _rms_hat(x_ref[rows, :])
                dx, dg = _rms_bwd(acc[rows, :], xhat, r, g_ref[...])
                dx2_ref[rows, :] = dx2_ref[rows, :] + dx
                dg_ref[...] += dg

            _row_chunks(tt, tail)
            out = pltpu.make_async_copy(
                dx2_ref, gx_hbm.at[pl.ds(pl.multiple_of(i * tt, tt), tt)], out_sem.at[0])
            out.start()
            out.wait()

    def part_spec(lo, hi):
        return pl.BlockSpec((tt, tk), lambda i, kk: (i, jnp.clip(kk - lo, 0, hi - lo - 1)))

    vec = pl.BlockSpec((1, d), lambda i, kk: (0, 0))
    return _call(
        body, name="win_bwd_norm", grid=(t // tt, nk),
        in_specs=[part_spec(lo, hi) for lo, hi in bounds]
        + [pl.BlockSpec((tk, d), lambda i, kk: (kk, 0)), ANY, ANY, vec],
        out_specs=[ANY, vec],
        out_shape=[jax.ShapeDtypeStruct((t, d), F32), jax.ShapeDtypeStruct((1, d), F32)],
        scratch_shapes=[pltpu.VMEM((tt, d), F32), pltpu.VMEM((tt, d), F32), pltpu.VMEM((tt, d), F32),
                        pltpu.SemaphoreType.DMA((2,)), pltpu.SemaphoreType.DMA((1,))],
        args=(*parts, w_int, dx2, x, g1), carry=carry)


def _adam_math(w, g, m, v):
    m = ADAM_B1 * m + (1.0 - ADAM_B1) * g
    v = ADAM_B2 * v + (1.0 - ADAM_B2) * (g * g)
    m_hat = m / (1.0 - ADAM_B1 ** ADAM_STEP)
    v_hat = v / (1.0 - ADAM_B2 ** ADAM_STEP)
    delta = -ADAM_LR * (m_hat / (jnp.sqrt(v_hat) + ADAM_EPS) + ADAM_WD * w)
    return delta, m, v


def _adamw_big(ws, gs, ms, vs):
    n = len(ws)
    nb = 4
    pair = [isinstance(g, tuple) for g in gs]

    def body(*refs):
        p = 0
        ins = []
        for a in range(n):
            k = 5 if pair[a] else 4
            ins.append(refs[p:p + k])
            p += k
        for a in range(n):
            g_out, d_ref, nm_ref, nv_ref = refs[p + 4 * a:p + 4 * a + 4]
            if pair[a]:
                w_ref, own_ref, recv_ref, m_ref, v_ref = ins[a]
                g = own_ref[...]
                for k in range(3):
                    g = g + recv_ref[k].astype(F32)
            else:
                w_ref, g_ref, m_ref, v_ref = ins[a]
                g = g_ref[...]
            dl, m, v = _adam_math(w_ref[...], g, m_ref[...], v_ref[...])
            g_out[...] = g
            d_ref[...] = dl
            nm_ref[...] = m
            nv_ref[...] = v

    in_specs, out_specs, out_shape, args = [], [], [], []
    for a, (w, g, m, v) in enumerate(zip(ws, gs, ms, vs)):
        rows, cols = w.shape
        blk = pl.BlockSpec((rows // nb, cols), lambda i: (i, 0))
        if pair[a]:
            in_specs += [blk, pl.BlockSpec((None, rows // nb, cols), lambda i: (0, i, 0)),
                         pl.BlockSpec((3, rows // nb, cols), lambda i: (0, i, 0)), blk, blk]
            args += [w, g[0], g[1], m, v]
        else:
            in_specs += [blk] * 4
            args += [w, g, m, v]
        out_specs += [blk] * 4
        out_shape += [jax.ShapeDtypeStruct(w.shape, F32)] * 4
    outs = _call(body, name="adamw_big", grid=(nb,), in_specs=in_specs, out_specs=out_specs,
                 out_shape=out_shape, args=args)[0]
    return [tuple(outs[4 * a:4 * a + 4]) for a in range(n)]


SMALL_ORDER = ("norm_mix_pre", "norm_mix_post", "norm_mlp_pre", "norm_mlp_post", "b_gate", "conv_w", "conv_b",
               "lru_w_a", "lru_b_a", "lru_w_x", "lru_b_x", "lru_lambda", "pool_w", "pool_scale")
VEC_ROW = dict(norm_mix_pre=0, norm_mix_post=1, norm_mlp_pre=2, norm_mlp_post=3, conv_b=6, lru_b_a=7,
               lru_b_x=8, lru_lambda=9)
ROW_B_GATE, ROW_POOL_SCALE, ROW_CONV_W, ROW_LOSS, N_VEC_ROWS = 4, 10, 11, 15, 16


def _adamw_small(vec_parts, g_pool, g_wa, g_wx, me, params):
    d = vec_parts.shape[2]
    names = SMALL_ORDER
    n = len(names)
    cw_cols = params["conv_w"][0].shape[2]

    def body(me_ref, vec_ref, vecc_ref, gp_ref, gwa_ref, gwx_ref, *refs):
        wmv = refs[:3 * n]
        loss_ref = refs[3 * n]
        outs = refs[3 * n + 1:3 * n + 1 + 4 * n]
        vs, vsc = refs[3 * n + 1 + 4 * n:]
        acc, accc = vec_ref[0], vecc_ref[0]
        for k in range(1, N_DEV):
            acc = acc + vec_ref[k]
            accc = accc + vecc_ref[k]
        vs[...] = acc
        vsc[...] = accc
        loss_ref[...] = vs[ROW_LOSS:ROW_LOSS + 1, 0:128]

        def upd(a, g, idx):
            w_ref, m_ref, v_ref = wmv[3 * a:3 * a + 3]
            g_ref, d_ref, nm_ref, nv_ref = outs[4 * a:4 * a + 4]
            dl, m, v = _adam_math(w_ref[idx], g, m_ref[idx], v_ref[idx])
            g_ref[idx] = g
            d_ref[idx] = dl
            nm_ref[idx] = m
            nv_ref[idx] = v

        for a, name in enumerate(names):
            if name in VEC_ROW:
                r = VEC_ROW[name]
                upd(a, vs[r:r + 1, :], (slice(None), slice(None)))
            elif name == "b_gate":
                for half in range(2):
                    r = ROW_B_GATE + half
                    upd(a, vs[r:r + 1, :], (slice(None), slice(half * d, (half + 1) * d)))
            elif name == "pool_scale":
                width = params[name][0].shape[1]
                upd(a, vs[ROW_POOL_SCALE:ROW_POOL_SCALE + 1, 0:width], (slice(None), slice(None)))
            elif name == "conv_w":
                upd(a, vsc[ROW_CONV_W:ROW_CONV_W + 4, :], (0,))
            elif name == "pool_w":
                upd(a, gp_ref[...], (Ellipsis,))
            elif name == "lru_w_a":
                upd(a, gwa_ref[...], (Ellipsis,))
            elif name == "lru_w_x":
                upd(a, gwx_ref[...], (Ellipsis,))
            else:
                raise ValueError(name)

    def whole(shape):
        nd = len(shape)
        return pl.BlockSpec(tuple(shape), lambda i, me_ref: (0,) * nd)

    in_specs = [
        whole(vec_parts.shape),
        pl.BlockSpec((N_DEV, N_VEC_ROWS, cw_cols), lambda i, me_ref: (0, 0, me_ref[0])),
        whole(g_pool.shape), whole(g_wa.shape), whole(g_wx.shape),
    ]
    args = [vec_parts, vec_parts, g_pool, g_wa, g_wx]
    out_specs = [whole((1, 128))]
    out_shape = [jax.ShapeDtypeStruct((1, 128), F32)]
    for name in names:
        for arr in params[name]:
            in_specs.append(whole(arr.shape))
            args.append(arr)
        shp = params[name][0].shape
        out_specs += [whole(shp)] * 4
        out_shape += [jax.ShapeDtypeStruct(shp, F32)] * 4
    grid_spec = pltpu.PrefetchScalarGridSpec(
        num_scalar_prefetch=1, grid=(1,), in_specs=in_specs, out_specs=out_specs,
        scratch_shapes=[pltpu.VMEM((N_VEC_ROWS, d), F32), pltpu.VMEM((N_VEC_ROWS, cw_cols), F32)])
    outs = pl.pallas_call(
        body, name="adamw_small", grid_spec=grid_spec, out_shape=out_shape,
        compiler_params=pltpu.CompilerParams(
            dimension_semantics=("arbitrary",), vmem_limit_bytes=V7X_VMEM_LIMIT_BYTES),
    )(me, *_in_hbm(args))
    return outs[0], {name: tuple(outs[1 + 4 * a:5 + 4 * a]) for a, name in enumerate(names)}


def _rs_sum(fulls, recvs, shard_ids, slot_ids, name):
    n = len(fulls)

    def body(sh_ref, sl_ref, *refs):
        s = pl.program_id(0)
        for a in range(n):
            full_ref, recv_ref = refs[2 * a], refs[2 * a + 1]
            own_ref, send_ref = refs[2 * n + 2 * a], refs[2 * n + 2 * a + 1]
            v = full_ref[...] + recv_ref[...].astype(F32)

            @pl.when(s == 0)
            def _(own_ref=own_ref, v=v):
                own_ref[...] = v

            @pl.when(s > 0)
            def _(send_ref=send_ref, v=v):
                send_ref[...] = v.astype(send_ref.dtype)

    in_specs, out_specs, out_shape, args = [], [], [], []
    for full, recv in zip(fulls, recvs):
        r, rest = recv.shape[1], tuple(recv.shape[2:])
        zeros = (0,) * len(rest)
        in_specs += [
            pl.BlockSpec((r,) + rest, lambda s, sh, sl, zeros=zeros: (sh[s],) + zeros),
            pl.BlockSpec((None, r) + rest, lambda s, sh, sl, zeros=zeros: (sl[s], 0) + zeros),
        ]
        out_specs += [
            pl.BlockSpec((None, r) + rest, lambda s, sh, sl, zeros=zeros: (0, 0) + zeros),
            pl.BlockSpec((None, r) + rest, lambda s, sh, sl, zeros=zeros: (jnp.maximum(s - 1, 0), 0) + zeros),
        ]
        out_shape += [jax.ShapeDtypeStruct((1, r) + rest, F32), jax.ShapeDtypeStruct((3, r) + rest, recv.dtype)]
        args += [full, recv]
    grid_spec = pltpu.PrefetchScalarGridSpec(
        num_scalar_prefetch=2, grid=(4,), in_specs=in_specs, out_specs=out_specs)
    outs = pl.pallas_call(
        body,
        name=name,
        grid_spec=grid_spec,
        out_shape=out_shape,
        compiler_params=pltpu.CompilerParams(
            dimension_semantics=("arbitrary",), vmem_limit_bytes=V7X_VMEM_LIMIT_BYTES),
    )(shard_ids, slot_ids, *_in_hbm(args))
    return [(outs[2 * a], outs[2 * a + 1]) for a in range(n)]


def _finals(pairs, name, carry=None):
    nb = 4
    n = len(pairs)

    def body(*refs):
        for a in range(n):
            own_ref, recv_ref = refs[2 * a], refs[2 * a + 1]
            acc = own_ref[...]
            for k in range(3):
                acc = acc + recv_ref[k].astype(F32)
            refs[2 * n + a][...] = acc

    in_specs, out_specs, out_shape, args = [], [], [], []
    for own, recv in pairs:
        _, rows, cols = own.shape
        in_specs += [pl.BlockSpec((None, rows // nb, cols), lambda i: (0, i, 0)),
                     pl.BlockSpec((3, rows // nb, cols), lambda i: (0, i, 0))]
        args += [own, recv]
        out_specs.append(pl.BlockSpec((rows // nb, cols), lambda i: (i, 0)))
        out_shape.append(jax.ShapeDtypeStruct((rows, cols), F32))
    return _call(body, name=name, grid=(nb,), in_specs=in_specs, out_specs=out_specs,
                 out_shape=out_shape, args=args, carry=carry)


def _rs_sums(fulls_f32, recv1, tag):
    x, y, c = _place()
    qs = jnp.stack([2 * x + y, 2 * (1 - x) + y, 2 * x + (1 - y), 2 * (1 - x) + (1 - y)]).astype(jnp.int32)
    shard_ids = 2 * qs + c
    return _rs_sum(fulls_f32, recv1, shard_ids, qs, "rs_sum_" + tag)


def _rs_level1(fulls_f32, fulls_send, tag):
    recv1 = _run_plan(_rs_sibling_plan(fulls_send), "rs_sibling_" + tag)
    return _rs_sums(fulls_f32, recv1, tag)


def _rows(g):
    return g.reshape(g.shape[0] * g.shape[1], g.shape[2])


def kernel(x, norm_mix_pre, norm_mix_post, norm_mlp_pre, norm_mlp_post, w_in, b_gate, conv_w, conv_b, lru_w_a, lru_b_a, lru_w_x, lru_b_x, lru_lambda, pool_w, pool_scale, w_lru_up, w_pool_up, w_o, w_ff1, w_ff2, loss_target, m_norm_mix_pre, m_norm_mix_post, m_norm_mlp_pre, m_norm_mlp_post, m_w_in, m_b_gate, m_conv_w, m_conv_b, m_lru_w_a, m_lru_b_a, m_lru_w_x, m_lru_b_x, m_lru_lambda, m_pool_w, m_pool_scale, m_w_lru_up, m_w_pool_up, m_w_o, m_w_ff1, m_w_ff2, v_norm_mix_pre, v_norm_mix_post, v_norm_mlp_pre, v_norm_mlp_post, v_w_in, v_b_gate, v_conv_w, v_conv_b, v_lru_w_a, v_lru_b_a, v_lru_w_x, v_lru_b_x, v_lru_lambda, v_pool_w, v_pool_scale, v_w_lru_up, v_w_pool_up, v_w_o, v_w_ff1, v_w_ff2):
    t, d = x.shape[1], x.shape[2]
    d_rnn = conv_b.shape[1]
    d_pool = pool_scale.shape[1]
    per = LRU_CB // LRU_HEAD_DIM
    xi, yi, ci = _place()
    me = 4 * xi + 2 * yi + ci

    x2d = x[0]
    tgt = loss_target[0]

    s_in = w_in[0].T.astype(BF16)
    s_lu = w_lru_up[0].astype(BF16)
    s_pu = w_pool_up[0].astype(BF16)
    s_o = w_o[0].astype(BF16)
    s_f1 = w_ff1[0].astype(BF16)
    s_f2 = w_ff2[0].astype(BF16)
    s_cw = jnp.pad(conv_w[0], ((0, 4), (0, 0)))

    g_in, g_cw = _run_plan(_ag_plan([s_in, s_cw]), "ag_w_in")
    w_int = _rows(g_in)
    conv_w_full = jnp.transpose(g_cw[:, :4, :], (1, 0, 2)).reshape(4, d_rnn)

    wa_bd, wx_bd = lru_w_a[0], lru_w_x[0]
    pw = pool_w[0]
    pw_bf = pw.astype(BF16)

    pool_block = (2 * d_rnn) // d_pool
    ga_block = (2 * d_rnn + d_pool) // 512
    gb_block = ga_block + d // 512
    g_block = d_rnn // 512

    r_f1, r_f2 = s_f1.shape[0], s_f2.shape[0]
    f1_cut = r_f1 // 4
    f2_cut = (3 * r_f2) // 8
    plan = _join([_ag_plan([s_lu, s_pu]), _ag_plan([s_f1], pieces=[(0, f1_cut)])])
    (proj, h1), got = _norm_proj(x2d, norm_mix_pre, w_int, carry=plan)
    (g_lu, g_pu), (g_f1,) = plan.split(got)
    plan = _join([_ag_plan([s_f1], pieces=[(f1_cut, r_f1 - f1_cut)], bufs=[g_f1]), _ag_plan([s_o])])
    (y_lru, h, lru_saved), got = _lru_fwd(
        proj, conv_w_full, conv_b, wa_bd, lru_b_a, wx_bd, lru_b_x, lru_lambda, carry=plan)
    (g_f1,), (g_o,) = plan.split(got)
    w_lu, w_og = _rows(g_lu), _rows(g_o)
    y_pool, p = _pool_fwd(proj, pw_bf, pool_scale, pool_block)
    (br_a, br_b, mix), (g_f2,) = _branch_mix(
        y_lru, y_pool, w_lu, g_pu, proj, b_gate, ga_block, gb_block,
        carry=_ag_plan([s_f2], pieces=[(0, f2_cut)]))
    (m, x2, h3), _ = _wo_norm(mix, w_og, x2d, norm_mix_post, norm_mlp_pre)
    (rf,), (g_f2,) = _ff1(
        h3, g_f1, carry=_ag_plan([s_f2], pieces=[(f2_cut, r_f2 - f2_cut)], bufs=[g_f2]))
    w_f2 = _rows(g_f2)
    dy, df, dg4, loss_part = _ff2_loss(rf, w_f2, x2, norm_mlp_post, tgt)

    (gw_ff2_32, gw_ff2_16), _ = _wgrad(rf, df, "wgrad_ff2", square_a=True)
    (d_f1,), r1_ff2 = _ff2_bwd(df, w_f2, rf, carry=_rs_sibling_plan([gw_ff2_16]))
    ((own_ff2, send_ff2),) = _rs_sums([gw_ff2_32], r1_ff2, "ff2")
    cut2 = (5 * send_ff2.shape[1]) // 16
    (gw_ff1_32, gw_ff1_16), (r2_ff2,) = _wgrad_cols(
        h3, d_f1, s_f1.shape[1], s_f1.shape[1], "wgrad_ff1",
        carry=_rs_chips_plan([send_ff2], pieces=[(0, cut2)]))
    plan = _join([_rs_chips_plan([send_ff2], pieces=[(cut2, send_ff2.shape[1] - cut2)], bufs=[r2_ff2]),
                  _rs_sibling_plan([gw_ff1_16])])
    (dx2, dm, dg3, dg2), got = _ff1_bwd_norms(d_f1, g_f1, dy, x2, norm_mlp_pre, m, norm_mix_post, carry=plan)
    (r2_ff2,), r1_ff1 = plan.split(got)
    ((own_ff1, send_ff1),) = _rs_sums([gw_ff1_32], r1_ff1, "ff1")
    own_ff1, send_ff1 = own_ff1.reshape((1,) + s_f1.shape), send_ff1.reshape((3,) + s_f1.shape)
    cut = send_ff1.shape[1] // 4
    (gw_o_32, gw_o_16), _ = _wgrad(mix, dm, "wgrad_o")
    (d_br_a, d_br_b, p_ga, p_gb, dbg_a, dbg_b), (r2_ff1,) = _wo_bwd_mix(
        dm, w_og, br_a, br_b, proj, b_gate, ga_block, gb_block,
        carry=_rs_chips_plan([send_ff1], pieces=[(0, cut)]))
    (gw_lu_32, gw_lu_16), _ = _wgrad(y_lru, d_br_a, "wgrad_lru_up")
    (gw_pu_32, gw_pu_16), _ = _wgrad_cols(y_pool, d_br_b, s_pu.shape[1], d, "wgrad_pool_up")
    (dh, p_g), r1_mid = _lru_up_bwd(
        d_br_a, w_lu, proj, h, g_block,
        carry=_rs_sibling_plan([gw_o_16, gw_lu_16, gw_pu_16]))
    mid = _rs_sums([gw_o_32, gw_lu_32, gw_pu_32], r1_mid, "mid")
    plan = _join([_rs_chips_plan([send_ff1], pieces=[(cut, send_ff1.shape[1] - cut)], bufs=[r2_ff1]),
                  _rs_chips_plan([mid[0][1]])])
    (p_x, dwa, db_a, dwx, db_x, dlam, dconv_w, dconv_b), got = _lru_bwd(
        dh, h, lru_saved, proj, conv_w_full, wa_bd, wx_bd, lru_lambda, carry=plan)
    (r2_ff1,), (r2_o,) = plan.split(got)
    p_p, dpool_w, dpool_scale = _pool_bwd(d_br_b, g_pu, p, pw, pool_scale)
    parts = [p_x, p_g, p_p, p_ga, p_gb]
    gw_in, (r2_lu, r2_pu) = _wgrad_parts(
        parts, h1, "wgrad_in", carry=_rs_chips_plan([mid[1][1], mid[2][1]]))
    r2_mid = [r2_o, r2_lu, r2_pu]
    tail = _rs_level1([gw_in[0], dpool_w.reshape(N_DEV, -1, POOL_GROUP_DIM), dwa, dwx],
                      [gw_in[1], dpool_w.reshape(N_DEV, -1, POOL_GROUP_DIM), dwa, dwx], "in")
    (grad_x, dg1), r2_tail = _win_bwd_norm(parts, w_int, dx2, x2d, norm_mix_pre,
                                           carry=_rs_chips_plan([s for _, s in tail]))

    def flat2(a):
        return a.reshape(a.shape[0], -1, a.shape[-1])

    fin_small, _ = _finals([
        (flat2(tail[1][0]), flat2(r2_tail[1])), (flat2(tail[2][0]), flat2(r2_tail[2])),
        (flat2(tail[3][0]), flat2(r2_tail[3])),
    ], "rs_finals_small")

    def pad_row(a):
        return jnp.pad(a, ((0, 0), (0, d - a.shape[1])))

    vecs = jnp.concatenate([dg1, dg2, dg3, dg4, dbg_a, dbg_b, dconv_b, db_a, db_x, dlam,
                            pad_row(dpool_scale), dconv_w, pad_row(loss_part)], axis=0)
    assert vecs.shape[0] == N_VEC_ROWS
    vec_parts, g_pool, g_wa, g_wx = _run_plan(_ag_plan([vecs] + fin_small), "ag_tail")

    big_names = ["w_in", "w_lru_up", "w_pool_up", "w_o", "w_ff1", "w_ff2"]
    big_w = [w_in[0].T, w_lru_up[0], w_pool_up[0], w_o[0], w_ff1[0], w_ff2[0]]
    big_g = [(tail[0][0], r2_tail[0]), (mid[1][0], r2_mid[1]),
             (mid[2][0].reshape((1,) + s_pu.shape), r2_mid[2].reshape((3,) + s_pu.shape)),
             (mid[0][0], r2_mid[0]), (own_ff1, r2_ff1), (own_ff2, r2_ff2)]
    big_m = [m_w_in[0].T, m_w_lru_up[0], m_w_pool_up[0], m_w_o[0], m_w_ff1[0], m_w_ff2[0]]
    big_v = [v_w_in[0].T, v_w_lru_up[0], v_w_pool_up[0], v_w_o[0], v_w_ff1[0], v_w_ff2[0]]
    big_out = _adamw_big(big_w, big_g, big_m, big_v)
    big_out[0] = tuple(o.T for o in big_out[0])

    small = dict(
        norm_mix_pre=(norm_mix_pre, m_norm_mix_pre, v_norm_mix_pre),
        norm_mix_post=(norm_mix_post, m_norm_mix_post, v_norm_mix_post),
        norm_mlp_pre=(norm_mlp_pre, m_norm_mlp_pre, v_norm_mlp_pre),
        norm_mlp_post=(norm_mlp_post, m_norm_mlp_post, v_norm_mlp_post),
        b_gate=(b_gate, m_b_gate, v_b_gate), conv_w=(conv_w, m_conv_w, v_conv_w),
        conv_b=(conv_b, m_conv_b, v_conv_b), lru_w_a=(lru_w_a, m_lru_w_a, v_lru_w_a),
        lru_b_a=(lru_b_a, m_lru_b_a, v_lru_b_a), lru_w_x=(lru_w_x, m_lru_w_x, v_lru_w_x),
        lru_b_x=(lru_b_x, m_lru_b_x, v_lru_b_x), lru_lambda=(lru_lambda, m_lru_lambda, v_lru_lambda),
        pool_w=(pool_w, m_pool_w, v_pool_w), pool_scale=(pool_scale, m_pool_scale, v_pool_scale))
    loss_row, small_out = _adamw_small(
        vec_parts, g_pool.reshape(pool_w.shape), g_wa.reshape(lru_w_a.shape), g_wx.reshape(lru_w_x.shape),
        jnp.reshape(me, (1,)).astype(jnp.int32), small)
    grads = {n: o[0] for n, o in small_out.items()}
    delta = {n: o[1] for n, o in small_out.items()}
    new_m = {n: o[2] for n, o in small_out.items()}
    new_v = {n: o[3] for n, o in small_out.items()}

    for name, (g, dl, nm, nv) in zip(big_names, big_out):
        grads[name], delta[name], new_m[name], new_v[name] = g[None], dl[None], nm[None], nv[None]

    loss = loss_row[0, 0]
    order = ["norm_mix_pre", "norm_mix_post", "norm_mlp_pre", "norm_mlp_post", "w_in", "b_gate", "conv_w",
             "conv_b", "lru_w_a", "lru_b_a", "lru_w_x", "lru_b_x", "lru_lambda", "pool_w", "pool_scale",
             "w_lru_up", "w_pool_up", "w_o", "w_ff1", "w_ff2"]
    return (loss, grad_x[None], *[grads[n] for n in order], *[delta[n] for n in order],
            *[new_m[n] for n in order], *[new_v[n] for n in order])
```

```python
import functools
import math
import operator
import types

import jax
import jax.numpy as jnp
from jax import lax
from jax.experimental import pallas as pl
from jax.experimental.pallas import tpu as pltpu

F32 = jnp.float32
BF16 = jnp.bfloat16
NORM_EPS = 1e-6
LRU_C = 8.0
N_LRU_HEADS = 16
LRU_HEAD_DIM = 64
POOL_WINDOWS = (2, 4, 8, 16)
POOL_GROUP_DIM = 128
ADAM_LR = 0.001
ADAM_B1 = 0.9
ADAM_B2 = 0.999
ADAM_EPS = 1e-08
ADAM_WD = 0.01
ADAM_STEP = 10
N_DEV = 8
V7X_VMEM_LIMIT_BYTES = 56 * 1024 * 1024
LRU_CB = 256
MESH = pl.DeviceIdType.MESH
ANY = pl.BlockSpec(memory_space=pl.ANY)


def _tile(n, pref):
    t = min(n, pref)
    assert n % t == 0, (n, pref)
    return t


def _dot_nn(a, b):
    return lax.dot_general(a, b, (((1,), (0,)), ((), ())), preferred_element_type=F32)


def _dot_nt(a, b):
    return lax.dot_general(a, b, (((1,), (1,)), ((), ())), preferred_element_type=F32)


def _dot_tn(a, b):
    return lax.dot_general(a, b, (((0,), (0,)), ((), ())), preferred_element_type=F32)


def _row_chunks(n_rows, fn, chunk=256):
    chunk = min(chunk, n_rows)
    assert n_rows % chunk == 0

    def step(r, carry):
        fn(pl.ds(pl.multiple_of(r * chunk, chunk), chunk))
        return carry

    lax.fori_loop(0, n_rows // chunk, step, 0)


def _late_copies(i, tt, pairs, sems):
    rows = pl.ds(pl.multiple_of(i * tt, tt), tt)
    return [pltpu.make_async_copy(hbm.at[rows], buf, sems.at[j]) for j, (hbm, buf) in enumerate(pairs)]


def _sig(x):
    return 1.0 / (1.0 + jnp.exp(-x))


def _rms_hat(x):
    r = lax.rsqrt(jnp.mean(x * x, axis=-1, keepdims=True) + NORM_EPS)
    return x * r, r


def _rms_bwd(dn, xhat, r, g):
    q = dn * g
    dx = r * (q - xhat * jnp.mean(q * xhat, axis=-1, keepdims=True))
    dg = jnp.sum(dn * xhat, axis=0, keepdims=True)
    return dx, dg


_GELU_K = math.sqrt(2.0 / math.pi)
_GELU_C = 0.044715


def _gelu_and_grad(g):
    t = jnp.tanh(_GELU_K * (g + _GELU_C * g * g * g))
    val = 0.5 * g * (1.0 + t)
    grad = 0.5 * (1.0 + t) + 0.5 * g * (1.0 - t * t) * (_GELU_K * (1.0 + 3.0 * _GELU_C * g * g))
    return val, grad


def _softplus_neg(lam):
    z = -lam
    e = jnp.exp(-jnp.abs(z))
    u = 1.0 + e
    d = u - 1.0
    l1p = jnp.where(d == 0.0, e, jnp.log(u) * (e / jnp.where(d == 0.0, 1.0, d)))
    return jnp.maximum(z, 0.0) + l1p


def _lru_gates(xc, wa, ba, wx, bx, lam):
    xcb = xc.astype(BF16)
    r = _sig(_dot_nn(xcb, wa) + ba)
    i = _sig(_dot_nn(xcb, wx) + bx)
    sp = _softplus_neg(lam)
    log_a = (-LRU_C) * r * sp
    a = jnp.exp(log_a)
    mult = jnp.sqrt(-jnp.tanh(log_a) * (1.0 + a * a))
    return xcb, r, i, sp, log_a, a, mult


def _place():
    return lax.axis_index("x"), lax.axis_index("y"), lax.axis_index("c")


def _ag_plan(shards, pieces=None, bufs=None):
    na = len(shards)
    n_kinds = 7

    def parts(ins, outs, sems):
        send_sems, recv_sems, local_sems = sems
        x, y, c = _place()
        me, sibling = (x, y, c), (x, y, 1 - c)
        x_nb, y_nb, diag = (1 - x, y), (x, 1 - y), (1 - x, 1 - y)
        relay_src = (c * (1 - x) + (1 - c) * x, c * y + (1 - c) * (1 - y))
        relay_dst = (c * x + (1 - c) * (1 - x), c * (1 - y) + (1 - c) * y)

        def own(a):
            return ins[a] if pieces is None else ins[a].at[pl.ds(*pieces[a])]

        def slot(a, px, py, pc):
            idx = 4 * px + 2 * py + pc
            return outs[a].at[idx] if pieces is None else outs[a].at[idx, pl.ds(*pieces[a])]

        def copy(a, k, block, to, src=None):
            return pltpu.make_async_remote_copy(
                src_ref=slot(a, *block) if src is None else src,
                dst_ref=slot(a, *block),
                send_sem=send_sems.at[a * n_kinds + k],
                recv_sem=recv_sems.at[a * n_kinds + k],
                device_id=to,
                device_id_type=MESH,
            )

        mine = [pltpu.make_async_copy(own(a), slot(a, *me), local_sems.at[a]) for a in range(na)]
        first, second, third = [], [], []
        for a in range(na):
            first += [copy(a, 0, me, sibling, src=own(a)), copy(a, 1, me, (*x_nb, c), src=own(a)),
                      copy(a, 2, me, (*y_nb, c), src=own(a))]
            second += [copy(a, 3, (*relay_src, c), (*relay_dst, c)), copy(a, 4, (*x_nb, c), sibling),
                       copy(a, 5, (*y_nb, c), sibling)]
            third.append(copy(a, 6, (*diag, c), sibling))
        return sibling, c, x_nb, y_nb, diag, copy, mine, first, second, third

    def start(ins, outs, sems):
        _, _, _, _, _, _, mine, first, _, _ = parts(ins, outs, sems)
        for cp in mine + first:
            cp.start()

    def middle(ins, outs, sems):
        _, c, x_nb, y_nb, _, copy, _, _, second, _ = parts(ins, outs, sems)
        for a in range(na):
            copy(a, 1, (*x_nb, c), (*x_nb, c)).wait_recv()
            copy(a, 2, (*y_nb, c), (*y_nb, c)).wait_recv()
        for cp in second:
            cp.start()

    def finish(ins, outs, sems):
        sibling, c, x_nb, y_nb, diag, copy, mine, first, second, third = parts(ins, outs, sems)
        for a in range(na):
            copy(a, 3, (*diag, c), (*diag, c)).wait_recv()
            third[a].start()
        for a in range(na):
            copy(a, 0, sibling, sibling).wait_recv()
            copy(a, 4, (*x_nb, 1 - c), sibling).wait_recv()
            copy(a, 5, (*y_nb, 1 - c), sibling).wait_recv()
            copy(a, 6, (*diag, 1 - c), sibling).wait_recv()
        for cp in first + second + third:
            cp.wait_send()
        for cp in mine:
            cp.wait()

    return types.SimpleNamespace(
        ins=list(shards) + list(bufs or []),
        out_shapes=[jax.ShapeDtypeStruct((N_DEV,) + s.shape, s.dtype) for s in shards],
        sems=[pltpu.SemaphoreType.DMA((n_kinds * na,)), pltpu.SemaphoreType.DMA((n_kinds * na,)),
              pltpu.SemaphoreType.DMA((na,))],
        aliases=[(na + a, a) for a in range(na)] if bufs else [],
        peers=frozenset({"sibling", "neighbours"}), start=start, middle=middle, finish=finish)


def _rs_sibling_plan(fulls):
    na = len(fulls)
    rs = [f.shape[0] // N_DEV for f in fulls]

    def copies(ins, outs, sems):
        send_sems, recv_sems = sems
        x, y, c = _place()
        out = []
        for a in range(na):
            for q in range(4):
                shard = 2 * q + (1 - c)
                out.append(pltpu.make_async_remote_copy(
                    src_ref=ins[a].at[pl.ds(shard * rs[a], rs[a])],
                    dst_ref=outs[a].at[q],
                    send_sem=send_sems.at[a * 4 + q],
                    recv_sem=recv_sems.at[a * 4 + q],
                    device_id=(x, y, 1 - c),
                    device_id_type=MESH,
                ))
        return out

    def start(ins, outs, sems):
        for cp in copies(ins, outs, sems):
            cp.start()

    def finish(ins, outs, sems):
        for cp in copies(ins, outs, sems):
            cp.wait()

    return types.SimpleNamespace(
        ins=list(fulls),
        out_shapes=[jax.ShapeDtypeStruct((4, r) + f.shape[1:], f.dtype) for r, f in zip(rs, fulls)],
        sems=[pltpu.SemaphoreType.DMA((4 * na,)), pltpu.SemaphoreType.DMA((4 * na,))],
        peers=frozenset({"sibling"}), start=start, finish=finish)


def _rs_chips_plan(sends, pieces=None, bufs=None):
    na = len(sends)

    def copies(ins, outs, sems):
        send_sems, recv_sems = sems
        x, y, c = _place()
        chips = [(1 - x, y), (x, 1 - y), (1 - x, 1 - y)]
        out = []
        for a in range(na):
            for k, chip in enumerate(chips):
                rows = (k,) if pieces is None else (k, pl.ds(*pieces[a]))
                out.append(pltpu.make_async_remote_copy(
                    src_ref=ins[a].at[rows],
                    dst_ref=outs[a].at[rows],
                    send_sem=send_sems.at[a * 3 + k],
                    recv_sem=recv_sems.at[a * 3 + k],
                    device_id=(*chip, c),
                    device_id_type=MESH,
                ))
        return out

    def start(ins, outs, sems):
        for cp in copies(ins, outs, sems):
            cp.start()

    def finish(ins, outs, sems):
        for cp in copies(ins, outs, sems):
            cp.wait()

    return types.SimpleNamespace(
        ins=list(sends) + list(bufs or []),
        out_shapes=[jax.ShapeDtypeStruct(s.shape, s.dtype) for s in sends],
        sems=[pltpu.SemaphoreType.DMA((3 * na,)), pltpu.SemaphoreType.DMA((3 * na,))],
        aliases=[(na + a, a) for a in range(na)] if bufs else [],
        peers=frozenset({"chips"}), start=start, finish=finish)


def _join(plans):
    ins, outs, sems, aliases, offs = [], [], [], [], []
    for p in plans:
        offs.append((len(ins), len(outs), len(sems)))
        aliases += [(len(ins) + ci, len(outs) + co) for ci, co in getattr(p, "aliases", [])]
        ins += p.ins
        outs += p.out_shapes
        sems += p.sems

    def cut(p, off, i, o, s):
        return (i[off[0]:off[0] + len(p.ins)], o[off[1]:off[1] + len(p.out_shapes)],
                s[off[2]:off[2] + len(p.sems)])

    def start(i, o, s):
        for p, off in zip(plans, offs):
            p.start(*cut(p, off, i, o, s))

    def middle(i, o, s):
        for p, off in zip(plans, offs):
            if getattr(p, "middle", None) is not None:
                p.middle(*cut(p, off, i, o, s))

    def finish(i, o, s):
        for p, off in zip(plans, offs):
            p.finish(*cut(p, off, i, o, s))

    def split(results):
        return [list(results[off[1]:off[1] + len(p.out_shapes)]) for p, off in zip(plans, offs)]

    return types.SimpleNamespace(ins=ins, out_shapes=outs, sems=sems, aliases=aliases,
                                 peers=frozenset().union(*[p.peers for p in plans]),
                                 start=start, middle=middle, finish=finish, split=split)


COLLECTIVE_ID = {frozenset({"sibling"}): 0, frozenset({"chips"}): 1, frozenset({"sibling", "chips"}): 2,
                 frozenset({"sibling", "neighbours"}): 3}


def _handshake(peers):
    x, y, c = _place()
    devs = []
    if "sibling" in peers:
        devs.append((x, y, 1 - c))
    if "neighbours" in peers:
        devs += [(1 - x, y, c), (x, 1 - y, c)]
    if "chips" in peers:
        assert "neighbours" not in peers
        devs += [(1 - x, y, c), (x, 1 - y, c), (1 - x, 1 - y, c)]
    barrier = pltpu.get_barrier_semaphore()
    for dev in devs:
        pl.semaphore_signal(barrier, inc=1, device_id=dev, device_id_type=MESH)
    pl.semaphore_wait(barrier, len(devs))


def _in_hbm(args):
    return [pltpu.with_memory_space_constraint(a, pltpu.HBM) for a in args]


def _run_plan(plan, name):
    n_in, n_out = len(plan.ins), len(plan.out_shapes)

    def body(*refs):
        ins, outs, sems = refs[:n_in], refs[n_in:n_in + n_out], refs[n_in + n_out:]
        _handshake(plan.peers)
        plan.start(ins, outs, sems)
        if getattr(plan, "middle", None) is not None:
            plan.middle(ins, outs, sems)
        plan.finish(ins, outs, sems)

    return pl.pallas_call(
        body,
        name=name,
        in_specs=[ANY] * n_in,
        out_specs=[ANY] * n_out,
        out_shape=plan.out_shapes,
        scratch_shapes=plan.sems,
        input_output_aliases=dict(getattr(plan, "aliases", [])),
        compiler_params=pltpu.CompilerParams(collective_id=COLLECTIVE_ID[plan.peers]),
    )(*_in_hbm(plan.ins))


def _call(body, *, name, grid, in_specs, out_specs, out_shape, args, scratch_shapes=(), aliases=None,
          carry=None):
    n_in, n_out, n_scr = len(in_specs), len(out_shape), len(scratch_shapes)
    params = pltpu.CompilerParams(
        dimension_semantics=("arbitrary",) * len(grid), vmem_limit_bytes=V7X_VMEM_LIMIT_BYTES)
    if carry is None:
        outs = pl.pallas_call(
            body, name=name, grid=grid, in_specs=list(in_specs), out_specs=list(out_specs),
            out_shape=list(out_shape), scratch_shapes=list(scratch_shapes),
            input_output_aliases=aliases or {}, compiler_params=params)(*_in_hbm(args))
        return list(outs), []
    c_in, c_out = len(carry.ins), len(carry.out_shapes)

    def full(*refs):
        p = 0
        ins = refs[p:p + n_in]
        p += n_in
        cins = refs[p:p + c_in]
        p += c_in
        outs = refs[p:p + n_out]
        p += n_out
        couts = refs[p:p + c_out]
        p += c_out
        scr = refs[p:p + n_scr]
        csems = refs[p + n_scr:]
        ids = [pl.program_id(a) for a in range(len(grid))]
        first = functools.reduce(operator.and_, [i == 0 for i in ids])
        last = functools.reduce(operator.and_, [i == g - 1 for i, g in zip(ids, grid)])

        @pl.when(first)
        def _():
            _handshake(carry.peers)
            carry.start(cins, couts, csems)

        if getattr(carry, "middle", None) is not None:
            n_steps = math.prod(grid)
            flat = functools.reduce(lambda acc, ig: acc * ig[1] + ig[0], zip(ids, grid), 0)

            @pl.when(flat == (2 * n_steps) // 3)
            def _():
                carry.middle(cins, couts, csems)

        body(*ins, *outs, *scr)

        @pl.when(last)
        def _():
            carry.finish(cins, couts, csems)

    all_aliases = dict(aliases or {})
    all_aliases.update({n_in + ci: n_out + co for ci, co in getattr(carry, "aliases", [])})
    params = pltpu.CompilerParams(
        dimension_semantics=("arbitrary",) * len(grid), vmem_limit_bytes=V7X_VMEM_LIMIT_BYTES,
        collective_id=COLLECTIVE_ID[carry.peers])
    outs = pl.pallas_call(
        full, name=name, grid=grid,
        in_specs=list(in_specs) + [ANY] * c_in,
        out_specs=list(out_specs) + [ANY] * c_out,
        out_shape=list(out_shape) + list(carry.out_shapes),
        scratch_shapes=list(scratch_shapes) + list(carry.sems),
        input_output_aliases=all_aliases, compiler_params=params)(*_in_hbm(args), *_in_hbm(carry.ins))
    return list(outs[:n_out]), list(outs[n_out:])


def _norm_proj(x, g1, w_int, carry=None):
    t, d = x.shape
    n = w_int.shape[0]
    tt, tn = _tile(t, 2048), _tile(n, 512)

    def body(x_ref, g_ref, w_ref, proj_ref, h1_ref, h1_s):
        @pl.when(pl.program_id(1) == 0)
        def _():
            def norm_rows(rows):
                xhat, _ = _rms_hat(x_ref[rows, :])
                h = (xhat * g_ref[...]).astype(BF16)
                h1_s[rows, :] = h
                h1_ref[rows, :] = h

            _row_chunks(tt, norm_rows)

        proj_ref[...] = _dot_nt(h1_s[...], w_ref[...]).astype(BF16)

    return _call(
        body, name="norm_proj", grid=(t // tt, n // tn),
        in_specs=[
            pl.BlockSpec((tt, d), lambda i, j: (i, 0)),
            pl.BlockSpec((1, d), lambda i, j: (0, 0)),
            pl.BlockSpec((tn, d), lambda i, j: (j, 0)),
        ],
        out_specs=[
            pl.BlockSpec((tt, tn), lambda i, j: (i, j)),
            pl.BlockSpec((tt, d), lambda i, j: (i, 0)),
        ],
        out_shape=[jax.ShapeDtypeStruct((t, n), BF16), jax.ShapeDtypeStruct((t, d), BF16)],
        scratch_shapes=[pltpu.VMEM((tt, d), BF16)],
        args=(x, g1, w_int), carry=carry)


def _scan_rows(av, bv, reverse):
    tc = av.shape[0]
    row = lax.broadcasted_iota(jnp.int32, av.shape, 0)
    s = 1
    while s < tc:
        if s < 8:
            keep = (row < tc - s) if reverse else (row >= s)
            shift = (tc - s) if reverse else s
            a_sh = jnp.where(keep, pltpu.roll(av, shift, 0), 1.0)
            b_sh = jnp.where(keep, pltpu.roll(bv, shift, 0), 0.0)
            bv = av * b_sh + bv
            av = av * a_sh
        elif reverse:
            bv = jnp.concatenate([av[:tc - s] * bv[s:] + bv[:tc - s], bv[tc - s:]], axis=0)
            av = jnp.concatenate([av[:tc - s] * av[s:], av[tc - s:]], axis=0)
        else:
            bv = jnp.concatenate([bv[:s], av[s:] * bv[:tc - s] + bv[s:]], axis=0)
            av = jnp.concatenate([av[:s], av[s:] * av[:tc - s]], axis=0)
        s *= 2
    return av, bv


N_LRU_SAVED = 5


def _fill_block_diag(w_ref, bd_ref):
    bd_ref[...] = jnp.zeros_like(bd_ref)
    hd = LRU_HEAD_DIM
    for k in range(w_ref.shape[0]):
        bd_ref[k * hd:(k + 1) * hd, k * hd:(k + 1) * hd] = w_ref[k].astype(BF16)


def _lru_fwd(proj, conv_w, conv_b, w_a, b_a, w_x, b_x, lam, carry=None):
    t = proj.shape[0]
    dr = conv_b.shape[1]
    cb = LRU_CB
    tc = _tile(t, 256)
    ncb, ntc = dr // cb, t // tc

    def body(xp_ref, g_ref, cw_ref, cb_ref, wa_ref, ba_ref, wx_ref, bx_ref, lam_ref,
             y_ref, h_ref, saved_ref, prevx_s, hlast_s, wa_s, wx_s):
        c = pl.program_id(1)

        @pl.when(c == 0)
        def _():
            prevx_s[...] = jnp.zeros_like(prevx_s)
            hlast_s[...] = jnp.zeros_like(hlast_s)
            _fill_block_diag(wa_ref, wa_s)
            _fill_block_diag(wx_ref, wx_s)

        x = xp_ref[...].astype(F32)
        prev = prevx_s[...]
        row = lax.broadcasted_iota(jnp.int32, x.shape, 0)

        def sh(j):
            return jnp.where(row >= j, pltpu.roll(x, j, 0), pltpu.roll(prev, j, 0))

        xc = (cb_ref[...] + cw_ref[0:1, :] * sh(3) + cw_ref[1:2, :] * sh(2)
              + cw_ref[2:3, :] * sh(1) + cw_ref[3:4, :] * x)
        prevx_s[...] = x
        _, r, i, _, _, a, mult = _lru_gates(xc, wa_s[...], ba_ref[...], wx_s[...], bx_ref[...],
                                            lam_ref[...])
        for k, val in enumerate((xc, r, i, a, mult)):
            saved_ref[:, k * cb:(k + 1) * cb] = val
        av, bv = _scan_rows(a, mult * (i * xc), reverse=False)
        h = av * hlast_s[...] + bv
        h_ref[...] = h
        hlast_s[...] = h_ref[tc - 1:tc, :]
        gel, _ = _gelu_and_grad(g_ref[...].astype(F32))
        y_ref[...] = (h * gel).astype(BF16)

    vec = pl.BlockSpec((1, cb), lambda j, c: (0, j))
    blk = pl.BlockSpec((tc, cb), lambda j, c: (c, j))
    mat = pl.BlockSpec((cb // LRU_HEAD_DIM, LRU_HEAD_DIM, LRU_HEAD_DIM), lambda j, c: (j, 0, 0))
    return _call(
        body, name="lru_fwd", grid=(ncb, ntc),
        in_specs=[
            blk,
            pl.BlockSpec((tc, cb), lambda j, c: (c, ncb + j)),
            pl.BlockSpec((4, cb), lambda j, c: (0, j)),
            vec, mat, vec, mat, vec, vec,
        ],
        out_specs=[blk, blk, pl.BlockSpec((tc, N_LRU_SAVED * cb), lambda j, c: (c, j))],
        out_shape=[jax.ShapeDtypeStruct((t, dr), BF16), jax.ShapeDtypeStruct((t, dr), F32),
                   jax.ShapeDtypeStruct((t, N_LRU_SAVED * dr), F32)],
        scratch_shapes=[pltpu.VMEM((tc, cb), F32), pltpu.VMEM((1, cb), F32),
                        pltpu.VMEM((cb, cb), BF16), pltpu.VMEM((cb, cb), BF16)],
        args=(proj, proj, conv_w, conv_b, w_a, b_a, w_x, b_x, lam), carry=carry)


def _pool_select(col, vals):
    out = vals[3]
    for g in (2, 1, 0):
        out = jnp.where(col < (g + 1) * POOL_GROUP_DIM, vals[g], out)
    return out


def _pool_fwd(proj, pool_w, pool_scale, col_block):
    t = proj.shape[0]
    dp = pool_scale.shape[1]
    tc = _tile(t, 256)
    ntc = t // tc

    def body(x_ref, w_ref, sc_ref, y_ref, p_ref, px, p2, p4, p8):
        c = pl.program_id(0)

        @pl.when(c == 0)
        def _():
            for s in (px, p2, p4, p8):
                s[...] = jnp.zeros_like(s)

        x = x_ref[...].astype(F32)
        row = lax.broadcasted_iota(jnp.int32, x.shape, 0)
        col = lax.broadcasted_iota(jnp.int32, x.shape, 1)

        def sh(v, pv, j):
            return jnp.where(row >= j, pltpu.roll(v, j, 0), pltpu.roll(pv[...], j, 0))

        s2 = x + sh(x, px, 1)
        s4 = s2 + sh(s2, p2, 2)
        s8 = s4 + sh(s4, p4, 4)
        s16 = s8 + sh(s8, p8, 8)
        px[...] = x
        p2[...] = s2
        p4[...] = s4
        p8[...] = s8
        wsum = _pool_select(col, (s2, s4, s8, s16))
        win = _pool_select(col, POOL_WINDOWS)
        cnt = jnp.minimum(c * tc + row + 1, win).astype(F32)
        p = wsum / cnt - x
        pb = p.astype(BF16)
        p_ref[...] = pb
        for g in range(len(POOL_WINDOWS)):
            sl = slice(g * POOL_GROUP_DIM, (g + 1) * POOL_GROUP_DIM)
            yg = _dot_nn(pb[:, sl], w_ref[g]) * sc_ref[:, sl]
            y_ref[:, sl] = yg.astype(BF16)

    return _call(
        body, name="pool_fwd", grid=(ntc,),
        in_specs=[
            pl.BlockSpec((tc, dp), lambda c: (c, col_block)),
            pl.BlockSpec(pool_w.shape, lambda c: (0, 0, 0)),
            pl.BlockSpec((1, dp), lambda c: (0, 0)),
        ],
        out_specs=[pl.BlockSpec((tc, dp), lambda c: (c, 0))] * 2,
        out_shape=[jax.ShapeDtypeStruct((t, dp), BF16)] * 2,
        scratch_shapes=[pltpu.VMEM((tc, dp), F32)] * 4,
        args=(proj, pool_w, pool_scale))[0]


def _branch_mix(y_lru, y_pool, w_lru_up, w_pool_upb, proj, b_gate, ga_block, gb_block, carry=None):
    t, d = y_lru.shape
    dp = y_pool.shape[1]
    bw = w_pool_upb.shape[2]
    tt, tn = _tile(t, 1024), 512
    nj = d // tn

    def body(yl_ref, yp_ref, wl_ref, wp_ref, ga_ref, gb_ref, ba_ref, bb_ref, bra_ref, brb_ref, mix_ref):
        br_a = _dot_nn(yl_ref[...], wl_ref[...])
        wp = jnp.concatenate([wp_ref[b] for b in range(tn // bw)], axis=1)
        br_b = _dot_nn(yp_ref[...], wp)
        bra_ref[...] = br_a.astype(BF16)
        brb_ref[...] = br_b.astype(BF16)
        ga = _sig(ga_ref[...].astype(F32) + ba_ref[...])
        gb = _sig(gb_ref[...].astype(F32) + bb_ref[...])
        mix_ref[...] = (ga * br_a + gb * br_b).astype(BF16)

    out = pl.BlockSpec((tt, tn), lambda j, i: (i, j))
    return _call(
        body, name="branch_mix", grid=(nj, t // tt),
        in_specs=[
            pl.BlockSpec((tt, d), lambda j, i: (i, 0)),
            pl.BlockSpec((tt, dp), lambda j, i: (i, 0)),
            pl.BlockSpec((d, tn), lambda j, i: (0, j)),
            pl.BlockSpec((tn // bw, dp, bw), lambda j, i: (j, 0, 0)),
            pl.BlockSpec((tt, tn), lambda j, i: (i, ga_block + j)),
            pl.BlockSpec((tt, tn), lambda j, i: (i, gb_block + j)),
            pl.BlockSpec((1, tn), lambda j, i: (0, j)),
            pl.BlockSpec((1, tn), lambda j, i: (0, nj + j)),
        ],
        out_specs=[out, out, out],
        out_shape=[jax.ShapeDtypeStruct((t, d), BF16)] * 3,
        args=(y_lru, y_pool, w_lru_up, w_pool_upb, proj, proj, b_gate, b_gate), carry=carry)


def _wo_norm(mix, w_o, x, g2, g3, carry=None):
    t, d = x.shape
    tt = _tile(t, 512)

    def body(mix_ref, w_ref, x_ref, g2_ref, g3_ref, m_ref, x2_ref, h3_ref):
        m_ref[...] = _dot_nn(mix_ref[...], w_ref[...])

        def norms(rows):
            mhat, _ = _rms_hat(m_ref[rows, :])
            x2 = x_ref[rows, :] + mhat * g2_ref[...]
            x2_ref[rows, :] = x2
            xhat, _ = _rms_hat(x2)
            h3_ref[rows, :] = (xhat * g3_ref[...]).astype(BF16)

        _row_chunks(tt, norms, chunk=128)

    row = pl.BlockSpec((tt, d), lambda i: (i, 0))
    vec = pl.BlockSpec((1, d), lambda i: (0, 0))
    return _call(
        body, name="wo_norm", grid=(t // tt,),
        in_specs=[row, pl.BlockSpec((d, d), lambda i: (0, 0)), row, vec, vec],
        out_specs=[row, row, row],
        out_shape=[
            jax.ShapeDtypeStruct((t, d), F32),
            jax.ShapeDtypeStruct((t, d), F32),
            jax.ShapeDtypeStruct((t, d), BF16),
        ],
        args=(mix, w_o, x, g2, g3), carry=carry)


def _ff1(h3, w_ff1b, carry=None):
    t, d = h3.shape
    nb, _, tn = w_ff1b.shape
    tt = _tile(t, 2048)

    def body(h_ref, w_ref, rf_ref):
        rf_ref[...] = jnp.maximum(_dot_nn(h_ref[...], w_ref[...]), 0.0).astype(BF16)

    out = pl.BlockSpec((tt, tn), lambda i, j: (i, j))
    return _call(
        body, name="ff1", grid=(t // tt, nb),
        in_specs=[pl.BlockSpec((tt, d), lambda i, j: (i, 0)), pl.BlockSpec((None, d, tn), lambda i, j: (j, 0, 0))],
        out_specs=[out],
        out_shape=[jax.ShapeDtypeStruct((t, nb * tn), BF16)],
        args=(h3, w_ff1b), carry=carry)


def _ff2_loss(rf, w_ff2, x2, g4, target):
    t, k = rf.shape
    d = x2.shape[1]
    tt, tk = _tile(t, 1024), _tile(k, 1024)
    nk = k // tk

    def body(a_ref, w_ref, x2_hbm, g_ref, tg_hbm, dy_hbm, df_hbm, dg_ref, loss_ref, acc, x2_ref, tg_ref,
             late_sems, dy_ref, df_ref, out_sems):
        i, kk = pl.program_id(0), pl.program_id(1)
        late = _late_copies(i, tt, [(x2_hbm, x2_ref), (tg_hbm, tg_ref)], late_sems)

        @pl.when(kk == 0)
        def _():
            acc[...] = jnp.zeros_like(acc)
            for cp in late:
                cp.start()

        @pl.when((i == 0) & (kk == 0))
        def _():
            dg_ref[...] = jnp.zeros_like(dg_ref)
            loss_ref[...] = jnp.zeros_like(loss_ref)

        rf_tile = a_ref[...]
        acc[...] += _dot_nn(rf_tile * rf_tile, w_ref[...])

        @pl.when(kk == nk - 1)
        def _():
            for cp in late:
                cp.wait()

            def tail(rows):
                fhat, r = _rms_hat(acc[rows, :])
                g = g_ref[...]
                e = x2_ref[rows, :] + fhat * g - tg_ref[rows, :]
                loss_ref[...] += 0.5 * jnp.sum(jnp.mean(e * e, axis=-1, keepdims=True))
                dy = e * (1.0 / d)
                dy_ref[rows, :] = dy.astype(BF16)
                df, dg = _rms_bwd(dy, fhat, r, g)
                df_ref[rows, :] = df.astype(BF16)
                dg_ref[...] += dg
                dst = pl.ds(tile_start + rows.start, rows.size)
                pltpu.make_async_copy(dy_ref.at[rows], dy_hbm.at[dst], out_sems.at[0]).start()
                pltpu.make_async_copy(df_ref.at[rows], df_hbm.at[dst], out_sems.at[1]).start()

            tile_start = pl.multiple_of(i * tt, tt)
            _row_chunks(tt, tail)
            tile = pl.ds(tile_start, tt)
            pltpu.make_async_copy(dy_ref, dy_hbm.at[tile], out_sems.at[0]).wait()
            pltpu.make_async_copy(df_ref, df_hbm.at[tile], out_sems.at[1]).wait()

    vec = pl.BlockSpec((1, d), lambda i, kk: (0, 0))
    return _call(
        body, name="ff2_loss", grid=(t // tt, nk),
        in_specs=[
            pl.BlockSpec((tt, tk), lambda i, kk: (i, kk)),
            pl.BlockSpec((tk, d), lambda i, kk: (kk, 0)),
            ANY, vec, ANY,
        ],
        out_specs=[ANY, ANY, vec, pl.BlockSpec((1, 128), lambda i, kk: (0, 0))],
        out_shape=[
            jax.ShapeDtypeStruct((t, d), BF16),
            jax.ShapeDtypeStruct((t, d), BF16),
            jax.ShapeDtypeStruct((1, d), F32),
            jax.ShapeDtypeStruct((1, 128), F32),
        ],
        scratch_shapes=[pltpu.VMEM((tt, d), F32), pltpu.VMEM((tt, d), x2.dtype), pltpu.VMEM((tt, d), target.dtype),
                        pltpu.SemaphoreType.DMA((2,)), pltpu.VMEM((tt, d), BF16), pltpu.VMEM((tt, d), BF16),
                        pltpu.SemaphoreType.DMA((2,))],
        args=(rf, w_ff2, x2, g4, target))[0]


def _ff2_bwd(df, w_ff2, rf, carry=None):
    t, d = df.shape
    n = w_ff2.shape[0]
    tt, tn = _tile(t, 2048), _tile(n, 512)

    def body(df_ref, w_ref, rf_ref, out_ref):
        d_act = _dot_nt(df_ref[...], w_ref[...])
        out_ref[...] = (d_act * (2.0 * rf_ref[...].astype(F32))).astype(BF16)

    blk = pl.BlockSpec((tt, tn), lambda i, j: (i, j))
    return _call(
        body, name="ff2_bwd", grid=(t // tt, n // tn),
        in_specs=[pl.BlockSpec((tt, d), lambda i, j: (i, 0)), pl.BlockSpec((tn, d), lambda i, j: (j, 0)), blk],
        out_specs=[blk],
        out_shape=[jax.ShapeDtypeStruct((t, n), BF16)],
        args=(df, w_ff2, rf), carry=carry)


def _wgrad(a, b, name, prev=None, row_off=0, rows=None, carry=None, square_a=False):
    t, m = a.shape
    n = b.shape[1]
    rows = m if rows is None else rows
    tm, tk = _tile(m, 512), _tile(t, 2048)
    nk = t // tk
    assert row_off % tm == 0
    off = row_off // tm

    def body(*refs):
        a_ref, b_ref = refs[0], refs[1]
        o32_ref, o16_ref, acc = refs[-3], refs[-2], refs[-1]
        kk = pl.program_id(1)

        @pl.when(kk == 0)
        def _():
            acc[...] = jnp.zeros_like(acc)

        a_tile = a_ref[...]
        acc[...] += _dot_tn(a_tile * a_tile if square_a else a_tile, b_ref[...])

        @pl.when(kk == nk - 1)
        def _():
            o32_ref[...] = acc[...]
            o16_ref[...] = acc[...].astype(BF16)

    in_specs = [pl.BlockSpec((tk, tm), lambda i, kk: (kk, i)), pl.BlockSpec((tk, n), lambda i, kk: (kk, 0))]
    args = [a, b]
    aliases = {}
    if prev is not None:
        in_specs += [ANY, ANY]
        args += list(prev)
        aliases = {2: 0, 3: 1}
    out = pl.BlockSpec((tm, n), lambda i, kk: (off + i, 0))
    return _call(
        body, name=name, grid=(m // tm, nk),
        in_specs=in_specs, out_specs=[out, out],
        out_shape=[jax.ShapeDtypeStruct((rows, n), F32), jax.ShapeDtypeStruct((rows, n), BF16)],
        scratch_shapes=[pltpu.VMEM((tm, n), F32)],
        aliases=aliases, args=args, carry=carry)


def _wgrad_parts(parts, b, name, carry=None):
    t, n = b.shape
    tm = 512
    bounds = []
    lo = 0
    for part in parts:
        assert part.shape[0] == t and part.shape[1] % tm == 0
        bounds.append((lo, lo + part.shape[1] // tm))
        lo += part.shape[1] // tm
    nm = lo
    np_ = len(parts)

    def body(*refs):
        p_refs, b_ref, o32_ref, o16_ref = refs[:np_], refs[np_], refs[np_ + 1], refs[np_ + 2]
        i = pl.program_id(0)
        for (lo_p, hi_p), p_ref in zip(bounds, p_refs):
            @pl.when((i >= lo_p) & (i < hi_p))
            def _(p_ref=p_ref):
                res = _dot_tn(p_ref[...], b_ref[...])
                o32_ref[...] = res
                o16_ref[...] = res.astype(BF16)

    def part_spec(lo_p, hi_p):
        return pl.BlockSpec((t, tm), lambda i: (0, jnp.clip(i - lo_p, 0, hi_p - lo_p - 1)))

    out = pl.BlockSpec((tm, n), lambda i: (i, 0))
    return _call(
        body, name=name, grid=(nm,),
        in_specs=[part_spec(lo_p, hi_p) for lo_p, hi_p in bounds] + [pl.BlockSpec((t, n), lambda i: (0, 0))],
        out_specs=[out, out],
        out_shape=[jax.ShapeDtypeStruct((nm * tm, n), F32), jax.ShapeDtypeStruct((nm * tm, n), BF16)],
        args=(*parts, b), carry=carry)


def _wgrad_cols(a, b, bw, tn, name, carry=None):
    t, m = a.shape
    n = b.shape[1]
    per_step = tn // bw

    def body(a_ref, b_ref, o32_ref, o16_ref):
        res = _dot_tn(a_ref[...], b_ref[...])
        for blk in range(per_step):
            part = res[:, blk * bw:(blk + 1) * bw]
            o32_ref[blk] = part
            o16_ref[blk] = part.astype(BF16)

    out = pl.BlockSpec((per_step, m, bw), lambda j: (j, 0, 0))
    return _call(
        body, name=name, grid=(n // tn,),
        in_specs=[pl.BlockSpec((t, m), lambda j: (0, 0)), pl.BlockSpec((t, tn), lambda j: (0, j))],
        out_specs=[out, out],
        out_shape=[jax.ShapeDtypeStruct((n // bw, m, bw), F32), jax.ShapeDtypeStruct((n // bw, m, bw), BF16)],
        args=(a, b), carry=carry)


def _ff1_bwd_norms(d_f1, w_ff1b, dy, x2, g3, m, g2, carry=None):
    t, k = d_f1.shape
    d = x2.shape[1]
    assert dy.dtype == BF16 and x2.dtype == F32
    bw = w_ff1b.shape[2]
    per_step = 2
    tt, tk = _tile(t, 1024), per_step * bw
    nk = k // tk

    def body(a_ref, w_ref, dy_hbm, x2_hbm, g3_ref, m_hbm, g2_ref, dx2_hbm, dm_hbm, dg3_ref, dg2_ref, acc,
             dy_ref, x2_ref, m_ref, late_sems, out_sems):
        i, kk = pl.program_id(0), pl.program_id(1)
        late = _late_copies(i, tt, [(dy_hbm, dy_ref), (x2_hbm, x2_ref), (m_hbm, m_ref)], late_sems)

        @pl.when(kk == 0)
        def _():
            acc[...] = jnp.zeros_like(acc)
            for cp in late:
                cp.start()

        @pl.when((i == 0) & (kk == 0))
        def _():
            dg3_ref[...] = jnp.zeros_like(dg3_ref)
            dg2_ref[...] = jnp.zeros_like(dg2_ref)

        a_tile = a_ref[...]
        for b in range(per_step):
            acc[...] += _dot_nt(a_tile[:, b * bw:(b + 1) * bw], w_ref[b])

        @pl.when(kk == nk - 1)
        def _():
            for cp in late:
                cp.wait()

            def tail(rows):
                xhat, r3 = _rms_hat(x2_ref[rows, :])
                dx, dg3 = _rms_bwd(acc[rows, :], xhat, r3, g3_ref[...])
                dx2 = dy_ref[rows, :].astype(F32) + dx
                x2_ref[rows, :] = dx2
                dg3_ref[...] += dg3
                mhat, r2 = _rms_hat(m_ref[rows, :])
                dm, dg2 = _rms_bwd(dx2, mhat, r2, g2_ref[...])
                dy_ref[rows, :] = dm.astype(BF16)
                dg2_ref[...] += dg2

            _row_chunks(tt, tail)
            tile = pl.ds(pl.multiple_of(i * tt, tt), tt)
            outs = [pltpu.make_async_copy(x2_ref, dx2_hbm.at[tile], out_sems.at[0]),
                    pltpu.make_async_copy(dy_ref, dm_hbm.at[tile], out_sems.at[1])]
            for cp in outs:
                cp.start()
            for cp in outs:
                cp.wait()

    vec = pl.BlockSpec((1, d), lambda i, kk: (0, 0))
    return _call(
        body, name="ff1_bwd_norms", grid=(t // tt, nk),
        in_specs=[
            pl.BlockSpec((tt, tk), lambda i, kk: (i, kk)),
            pl.BlockSpec((per_step, d, bw), lambda i, kk: (kk, 0, 0)),
            ANY, ANY, vec, ANY, vec,
        ],
        out_specs=[ANY, ANY, vec, vec],
        out_shape=[
            jax.ShapeDtypeStruct((t, d), F32),
            jax.ShapeDtypeStruct((t, d), BF16),
            jax.ShapeDtypeStruct((1, d), F32),
            jax.ShapeDtypeStruct((1, d), F32),
        ],
        scratch_shapes=[pltpu.VMEM((tt, d), F32), pltpu.VMEM((tt, d), dy.dtype), pltpu.VMEM((tt, d), F32),
                        pltpu.VMEM((tt, d), F32), pltpu.SemaphoreType.DMA((3,)), pltpu.SemaphoreType.DMA((2,))],
        args=(d_f1, w_ff1b, dy, x2, g3, m, g2), carry=carry)


def _wo_bwd_mix(dm, w_o, br_a, br_b, proj, b_gate, ga_block, gb_block, carry=None):
    t, d = dm.shape
    tt, tn = _tile(t, 1024), 512
    nj = d // tn

    def body(dm_ref, w_ref, bra_ref, brb_ref, ga_ref, gb_ref, ba_ref, bb_ref,
             dbra_ref, dbrb_ref, dga_ref, dgb_ref, dba_ref, dbb_ref):
        i = pl.program_id(1)

        @pl.when(i == 0)
        def _():
            dba_ref[...] = jnp.zeros_like(dba_ref)
            dbb_ref[...] = jnp.zeros_like(dbb_ref)

        d_mix = _dot_nt(dm_ref[...], w_ref[...])
        ga = _sig(ga_ref[...].astype(F32) + ba_ref[...])
        gb = _sig(gb_ref[...].astype(F32) + bb_ref[...])
        dbra_ref[...] = (d_mix * ga).astype(BF16)
        dbrb_ref[...] = (d_mix * gb).astype(BF16)
        dga = d_mix * bra_ref[...].astype(F32) * (ga * (1.0 - ga))
        dgb = d_mix * brb_ref[...].astype(F32) * (gb * (1.0 - gb))
        dga_ref[...] = dga.astype(BF16)
        dgb_ref[...] = dgb.astype(BF16)
        dba_ref[...] += jnp.sum(dga, axis=0, keepdims=True)
        dbb_ref[...] += jnp.sum(dgb, axis=0, keepdims=True)

    blk = pl.BlockSpec((tt, tn), lambda j, i: (i, j))
    vec = pl.BlockSpec((1, tn), lambda j, i: (0, j))
    return _call(
        body, name="wo_bwd_mix", grid=(nj, t // tt),
        in_specs=[
            pl.BlockSpec((tt, d), lambda j, i: (i, 0)),
            pl.BlockSpec((tn, d), lambda j, i: (j, 0)),
            blk, blk,
            pl.BlockSpec((tt, tn), lambda j, i: (i, ga_block + j)),
            pl.BlockSpec((tt, tn), lambda j, i: (i, gb_block + j)),
            vec,
            pl.BlockSpec((1, tn), lambda j, i: (0, nj + j)),
        ],
        out_specs=[blk, blk, blk, blk, vec, vec],
        out_shape=[jax.ShapeDtypeStruct((t, d), BF16)] * 4 + [jax.ShapeDtypeStruct((1, d), F32)] * 2,
        args=(dm, w_o, br_a, br_b, proj, proj, b_gate, b_gate), carry=carry)


def _lru_up_bwd(d_br_a, w_lru_up, proj, h, g_block, carry=None):
    t, d = d_br_a.shape
    tt, tn = _tile(t, 1024), 512

    def body(a_ref, w_ref, g_ref, h_ref, dh_ref, dg_ref):
        d_y = _dot_nt(a_ref[...], w_ref[...])
        gel, gel_grad = _gelu_and_grad(g_ref[...].astype(F32))
        dh_ref[...] = d_y * gel
        dg_ref[...] = (d_y * h_ref[...] * gel_grad).astype(BF16)

    blk = pl.BlockSpec((tt, tn), lambda i, j: (i, j))
    return _call(
        body, name="lru_up_bwd", grid=(t // tt, d // tn),
        in_specs=[
            pl.BlockSpec((tt, d), lambda i, j: (i, 0)),
            pl.BlockSpec((tn, d), lambda i, j: (j, 0)),
            pl.BlockSpec((tt, tn), lambda i, j: (i, g_block + j)),
            blk,
        ],
        out_specs=[blk, blk],
        out_shape=[jax.ShapeDtypeStruct((t, d), F32), jax.ShapeDtypeStruct((t, d), BF16)],
        args=(d_br_a, w_lru_up, proj, h), carry=carry)


def _lru_bwd(dh, h, saved, proj, conv_w, w_a, w_x, lam, carry=None):
    t, dr = dh.shape
    cb = LRU_CB
    hd = LRU_HEAD_DIM
    per = cb // hd
    tc = _tile(t, 256)
    ncb, ntc = dr // cb, t // tc

    def body(dh_ref, h_ref, hp_ref, saved_ref, xp_ref, cw_ref, wa_ref, wx_ref,
             lam_ref, dxp_ref, dwa_ref, dba_ref, dwx_ref, dbx_ref, dlam_ref, dcw_ref, dcb_ref,
             nextd_s, anext_s, gnext_s, tmp_s, wa_s, wx_s):
        c = pl.program_id(1)
        rc = ntc - 1 - c

        @pl.when(c == 0)
        def _():
            nextd_s[...] = jnp.zeros_like(nextd_s)
            anext_s[...] = jnp.zeros_like(anext_s)
            gnext_s[...] = jnp.zeros_like(gnext_s)
            for ref in (dwa_ref, dba_ref, dwx_ref, dbx_ref, dlam_ref, dcw_ref, dcb_ref):
                ref[...] = jnp.zeros_like(ref)
            _fill_block_diag(wa_ref, wa_s)
            _fill_block_diag(wx_ref, wx_s)

        xc, r, i, a, mult = [saved_ref[:, k * cb:(k + 1) * cb] for k in range(N_LRU_SAVED)]
        wa, wx, lam = wa_s[...], wx_s[...], lam_ref[...]
        xcb = xc.astype(BF16)
        sp = _softplus_neg(lam)
        row = lax.broadcasted_iota(jnp.int32, xc.shape, 0)
        h = h_ref[...]
        hp = jnp.where(rc == 0, 0.0, hp_ref[...])
        hprev = jnp.where(row >= 1, pltpu.roll(h, 1, 0), pltpu.roll(hp, 1, 0))

        def up(v, nv, j):
            return jnp.where(row < tc - j, pltpu.roll(v, tc - j, 0), nv)

        av, bv = _scan_rows(up(a, anext_s[...], 1), dh_ref[...], reverse=True)
        gt = av * gnext_s[...] + bv
        tmp_s[...] = gt
        gnext_s[...] = tmp_s[0:1, :]
        tmp_s[...] = a
        anext_s[...] = tmp_s[0:1, :]

        da = gt * hprev
        ixc = i * xc
        d_mult = gt * ixc
        d_i = gt * mult * xc
        d_xc = gt * mult * i
        d_log_a = da * a - d_mult * (a * a) / mult
        d_pre_r = (d_log_a * ((-LRU_C) * sp)) * (r * (1.0 - r))
        d_pre_i = d_i * (i * (1.0 - i))
        d_sp = jnp.sum(d_log_a * ((-LRU_C) * r), axis=0, keepdims=True)
        dlam_ref[...] += d_sp * (-1.0 / (1.0 + jnp.exp(lam)))
        dpr = d_pre_r.astype(BF16)
        dpi = d_pre_i.astype(BF16)
        dba_ref[...] += jnp.sum(d_pre_r, axis=0, keepdims=True)
        dbx_ref[...] += jnp.sum(d_pre_i, axis=0, keepdims=True)
        pa = _dot_tn(xcb, dpr)
        px = _dot_tn(xcb, dpi)
        for k in range(per):
            dwa_ref[k] += pa[k * hd:(k + 1) * hd, k * hd:(k + 1) * hd]
            dwx_ref[k] += px[k * hd:(k + 1) * hd, k * hd:(k + 1) * hd]
        d_xc = d_xc + _dot_nt(dpr, wa) + _dot_nt(dpi, wx)

        nxt = nextd_s[...]
        xp = xp_ref[...].astype(F32)
        dxp = cw_ref[3:4, :] * d_xc
        dcw_ref[3:4, :] += jnp.sum(xp * d_xc, axis=0, keepdims=True)
        for j in (1, 2, 3):
            uj = up(d_xc, pltpu.roll(nxt, tc - j, 0), j)
            dxp = dxp + cw_ref[3 - j:4 - j, :] * uj
            dcw_ref[3 - j:4 - j, :] += jnp.sum(xp * uj, axis=0, keepdims=True)
        dcb_ref[...] += jnp.sum(d_xc, axis=0, keepdims=True)
        nextd_s[...] = d_xc
        dxp_ref[...] = dxp.astype(BF16)

    vec = pl.BlockSpec((1, cb), lambda j, c: (0, j))
    blk = pl.BlockSpec((tc, cb), lambda j, c: (ntc - 1 - c, j))
    mat = pl.BlockSpec((per, hd, hd), lambda j, c: (j, 0, 0))
    cwb = pl.BlockSpec((4, cb), lambda j, c: (0, j))
    return _call(
        body, name="lru_bwd", grid=(ncb, ntc),
        in_specs=[
            blk, blk,
            pl.BlockSpec((tc, cb), lambda j, c: (jnp.maximum(ntc - 2 - c, 0), j)),
            pl.BlockSpec((tc, N_LRU_SAVED * cb), lambda j, c: (ntc - 1 - c, j)),
            blk, cwb, mat, mat, vec,
        ],
        out_specs=[blk, mat, vec, mat, vec, vec, cwb, vec],
        out_shape=[
            jax.ShapeDtypeStruct((t, dr), BF16),
            jax.ShapeDtypeStruct(w_a.shape, F32),
            jax.ShapeDtypeStruct((1, dr), F32),
            jax.ShapeDtypeStruct(w_x.shape, F32),
            jax.ShapeDtypeStruct((1, dr), F32),
            jax.ShapeDtypeStruct((1, dr), F32),
            jax.ShapeDtypeStruct((4, dr), F32),
            jax.ShapeDtypeStruct((1, dr), F32),
        ],
        scratch_shapes=[
            pltpu.VMEM((tc, cb), F32),
            pltpu.VMEM((1, cb), F32),
            pltpu.VMEM((1, cb), F32),
            pltpu.VMEM((tc, cb), F32),
            pltpu.VMEM((cb, cb), BF16),
            pltpu.VMEM((cb, cb), BF16),
        ],
        args=(dh, h, h, saved, proj, conv_w, w_a, w_x, lam), carry=carry)


def _pool_bwd(d_br_b, w_pool_upb, p, pool_w, pool_scale):
    t, d = d_br_b.shape
    nwb, dp, _ = w_pool_upb.shape
    tc = _tile(t, 256)
    ntc = t // tc
    ng = len(POOL_WINDOWS)

    def body(db_ref, wu_ref, p_ref, w_ref, sc_ref, dx_ref, dw_ref, dsc_ref, nz, n2, n4, n8, dp_s, dy_s):
        c = pl.program_id(0)
        rc = ntc - 1 - c

        @pl.when(c == 0)
        def _():
            for s in (nz, n2, n4, n8):
                s[...] = jnp.zeros_like(s)
            dw_ref[...] = jnp.zeros_like(dw_ref)
            dsc_ref[...] = jnp.zeros_like(dsc_ref)

        wu = jnp.concatenate([wu_ref[b] for b in range(nwb)], axis=1)
        dy_s[...] = _dot_nt(db_ref[...], wu)
        for g in range(ng):
            sl = slice(g * POOL_GROUP_DIM, (g + 1) * POOL_GROUP_DIM)
            pg = p_ref[:, sl]
            dyg = dy_s[:, sl]
            wg = w_ref[g].astype(BF16)
            q = _dot_nn(pg, wg)
            dsc_ref[:, sl] += jnp.sum(dyg * q, axis=0, keepdims=True)
            dpw = (dyg * sc_ref[:, sl]).astype(BF16)
            dw_ref[g] += _dot_tn(pg, dpw)
            dp_s[:, sl] = _dot_nt(dpw, wg)

        dpv = dp_s[...]
        row = lax.broadcasted_iota(jnp.int32, dpv.shape, 0)
        col = lax.broadcasted_iota(jnp.int32, dpv.shape, 1)
        win = _pool_select(col, POOL_WINDOWS)
        cnt = jnp.minimum(rc * tc + row + 1, win).astype(F32)
        z = dpv / cnt

        def up(v, nv, j):
            return jnp.where(row < tc - j, pltpu.roll(v, tc - j, 0), pltpu.roll(nv[...], tc - j, 0))

        u2 = z + up(z, nz, 1)
        u4 = u2 + up(u2, n2, 2)
        u8 = u4 + up(u4, n4, 4)
        u16 = u8 + up(u8, n8, 8)
        nz[...] = z
        n2[...] = u2
        n4[...] = u4
        n8[...] = u8
        dx_ref[...] = (_pool_select(col, (u2, u4, u8, u16)) - dpv).astype(BF16)

    blk = pl.BlockSpec((tc, dp), lambda c: (ntc - 1 - c, 0))
    full_w = pl.BlockSpec(pool_w.shape, lambda c: (0, 0, 0))
    vec = pl.BlockSpec((1, dp), lambda c: (0, 0))
    return _call(
        body, name="pool_bwd", grid=(ntc,),
        in_specs=[pl.BlockSpec((tc, d), lambda c: (ntc - 1 - c, 0)),
                  pl.BlockSpec(w_pool_upb.shape, lambda c: (0, 0, 0)), blk, full_w, vec],
        out_specs=[blk, full_w, vec],
        out_shape=[
            jax.ShapeDtypeStruct((t, dp), BF16),
            jax.ShapeDtypeStruct(pool_w.shape, F32),
            jax.ShapeDtypeStruct((1, dp), F32),
        ],
        scratch_shapes=[pltpu.VMEM((tc, dp), F32)] * 6,
        args=(d_br_b, w_pool_upb, p, pool_w, pool_scale))[0]


def _win_bwd_norm(parts, w_int, dx2, x, g1, carry=None):
    t, d = x.shape
    tk = 512
    tt = _tile(t, 1024)
    bounds = []
    k0 = 0
    for part in parts:
        assert part.shape[1] % tk == 0
        bounds.append((k0, k0 + part.shape[1] // tk))
        k0 += part.shape[1] // tk
    nk = k0
    assert nk * tk == w_int.shape[0]
    np_ = len(parts)

    def body(*refs):
        p_refs = refs[:np_]
        w_ref, dx2_hbm, x_hbm, g_ref, gx_hbm, dg_ref, acc, dx2_ref, x_ref, late_sems, out_sem = refs[np_:]
        i, kk = pl.program_id(0), pl.program_id(1)
        late = _late_copies(i, tt, [(dx2_hbm, dx2_ref), (x_hbm, x_ref)], late_sems)

        @pl.when(kk == 0)
        def _():
            acc[...] = jnp.zeros_like(acc)
            for cp in late:
                cp.start()

        @pl.when((i == 0) & (kk == 0))
        def _():
            dg_ref[...] = jnp.zeros_like(dg_ref)

        for (lo, hi), p_ref in zip(bounds, p_refs):
            @pl.when((kk >= lo) & (kk < hi))
            def _(p_ref=p_ref):
                acc[...] += _dot_nn(p_ref[...], w_ref[...])

        @pl.when(kk == nk - 1)
        def _():
            for cp in late:
                cp.wait()

            def tail(rows):
                xhat, r = _rms_hat(x_ref[rows, :])
                dx, dg = _rms_bwd(acc[rows, :], xhat, r, g_ref[...])
                dx2_ref[rows, :] = dx2_ref[rows, :] + dx
                dg_ref[...] += dg

            _row_chunks(tt, tail)
            out = pltpu.make_async_copy(
                dx2_ref, gx_hbm.at[pl.ds(pl.multiple_of(i * tt, tt), tt)], out_sem.at[0])
            out.start()
            out.wait()

    def part_spec(lo, hi):
        return pl.BlockSpec((tt, tk), lambda i, kk: (i, jnp.clip(kk - lo, 0, hi - lo - 1)))

    vec = pl.BlockSpec((1, d), lambda i, kk: (0, 0))
    return _call(
        body, name="win_bwd_norm", grid=(t // tt, nk),
        in_specs=[part_spec(lo, hi) for lo, hi in bounds]
        + [pl.BlockSpec((tk, d), lambda i, kk: (kk, 0)), ANY, ANY, vec],
        out_specs=[ANY, vec],
        out_shape=[jax.ShapeDtypeStruct((t, d), F32), jax.ShapeDtypeStruct((1, d), F32)],
        scratch_shapes=[pltpu.VMEM((tt, d), F32), pltpu.VMEM((tt, d), F32), pltpu.VMEM((tt, d), F32),
                        pltpu.SemaphoreType.DMA((2,)), pltpu.SemaphoreType.DMA((1,))],
        args=(*parts, w_int, dx2, x, g1), carry=carry)


def _adam_math(w, g, m, v):
    m = ADAM_B1 * m + (1.0 - ADAM_B1) * g
    v = ADAM_B2 * v + (1.0 - ADAM_B2) * (g * g)
    m_hat = m / (1.0 - ADAM_B1 ** ADAM_STEP)
    v_hat = v / (1.0 - ADAM_B2 ** ADAM_STEP)
    delta = -ADAM_LR * (m_hat / (jnp.sqrt(v_hat) + ADAM_EPS) + ADAM_WD * w)
    return delta, m, v


def _adamw_big(ws, gs, ms, vs):
    n = len(ws)
    nb = 4
    pair = [isinstance(g, tuple) for g in gs]

    def body(*refs):
        p = 0
        ins = []
        for a in range(n):
            k = 5 if pair[a] else 4
            ins.append(refs[p:p + k])
            p += k
        for a in range(n):
            g_out, d_ref, nm_ref, nv_ref = refs[p + 4 * a:p + 4 * a + 4]
            if pair[a]:
                w_ref, own_ref, recv_ref, m_ref, v_ref = ins[a]
                g = own_ref[...]
                for k in range(3):
                    g = g + recv_ref[k].astype(F32)
            else:
                w_ref, g_ref, m_ref, v_ref = ins[a]
                g = g_ref[...]
            dl, m, v = _adam_math(w_ref[...], g, m_ref[...], v_ref[...])
            g_out[...] = g
            d_ref[...] = dl
            nm_ref[...] = m
            nv_ref[...] = v

    in_specs, out_specs, out_shape, args = [], [], [], []
    for a, (w, g, m, v) in enumerate(zip(ws, gs, ms, vs)):
        rows, cols = w.shape
        blk = pl.BlockSpec((rows // nb, cols), lambda i: (i, 0))
        if pair[a]:
            in_specs += [blk, pl.BlockSpec((None, rows // nb, cols), lambda i: (0, i, 0)),
                         pl.BlockSpec((3, rows // nb, cols), lambda i: (0, i, 0)), blk, blk]
            args += [w, g[0], g[1], m, v]
        else:
            in_specs += [blk] * 4
            args += [w, g, m, v]
        out_specs += [blk] * 4
        out_shape += [jax.ShapeDtypeStruct(w.shape, F32)] * 4
    outs = _call(body, name="adamw_big", grid=(nb,), in_specs=in_specs, out_specs=out_specs,
                 out_shape=out_shape, args=args)[0]
    return [tuple(outs[4 * a:4 * a + 4]) for a in range(n)]


SMALL_ORDER = ("norm_mix_pre", "norm_mix_post", "norm_mlp_pre", "norm_mlp_post", "b_gate", "conv_w", "conv_b",
               "lru_w_a", "lru_b_a", "lru_w_x", "lru_b_x", "lru_lambda", "pool_w", "pool_scale")
VEC_ROW = dict(norm_mix_pre=0, norm_mix_post=1, norm_mlp_pre=2, norm_mlp_post=3, conv_b=6, lru_b_a=7,
               lru_b_x=8, lru_lambda=9)
ROW_B_GATE, ROW_POOL_SCALE, ROW_CONV_W, ROW_LOSS, N_VEC_ROWS = 4, 10, 11, 15, 16


def _adamw_small(vec_parts, g_pool, g_wa, g_wx, me, params):
    d = vec_parts.shape[2]
    names = SMALL_ORDER
    n = len(names)
    cw_cols = params["conv_w"][0].shape[2]

    def body(me_ref, vec_ref, vecc_ref, gp_ref, gwa_ref, gwx_ref, *refs):
        wmv = refs[:3 * n]
        loss_ref = refs[3 * n]
        outs = refs[3 * n + 1:3 * n + 1 + 4 * n]
        vs, vsc = refs[3 * n + 1 + 4 * n:]
        acc, accc = vec_ref[0], vecc_ref[0]
        for k in range(1, N_DEV):
            acc = acc + vec_ref[k]
            accc = accc + vecc_ref[k]
        vs[...] = acc
        vsc[...] = accc
        loss_ref[...] = vs[ROW_LOSS:ROW_LOSS + 1, 0:128]

        def upd(a, g, idx):
            w_ref, m_ref, v_ref = wmv[3 * a:3 * a + 3]
            g_ref, d_ref, nm_ref, nv_ref = outs[4 * a:4 * a + 4]
            dl, m, v = _adam_math(w_ref[idx], g, m_ref[idx], v_ref[idx])
            g_ref[idx] = g
            d_ref[idx] = dl
            nm_ref[idx] = m
            nv_ref[idx] = v

        for a, name in enumerate(names):
            if name in VEC_ROW:
                r = VEC_ROW[name]
                upd(a, vs[r:r + 1, :], (slice(None), slice(None)))
            elif name == "b_gate":
                for half in range(2):
                    r = ROW_B_GATE + half
                    upd(a, vs[r:r + 1, :], (slice(None), slice(half * d, (half + 1) * d)))
            elif name == "pool_scale":
                width = params[name][0].shape[1]
                upd(a, vs[ROW_POOL_SCALE:ROW_POOL_SCALE + 1, 0:width], (slice(None), slice(None)))
            elif name == "conv_w":
                upd(a, vsc[ROW_CONV_W:ROW_CONV_W + 4, :], (0,))
            elif name == "pool_w":
                upd(a, gp_ref[...], (Ellipsis,))
            elif name == "lru_w_a":
                upd(a, gwa_ref[...], (Ellipsis,))
            elif name == "lru_w_x":
                upd(a, gwx_ref[...], (Ellipsis,))
            else:
                raise ValueError(name)

    def whole(shape):
        nd = len(shape)
        return pl.BlockSpec(tuple(shape), lambda i, me_ref: (0,) * nd)

    in_specs = [
        whole(vec_parts.shape),
        pl.BlockSpec((N_DEV, N_VEC_ROWS, cw_cols), lambda i, me_ref: (0, 0, me_ref[0])),
        whole(g_pool.shape), whole(g_wa.shape), whole(g_wx.shape),
    ]
    args = [vec_parts, vec_parts, g_pool, g_wa, g_wx]
    out_specs = [whole((1, 128))]
    out_shape = [jax.ShapeDtypeStruct((1, 128), F32)]
    for name in names:
        for arr in params[name]:
            in_specs.append(whole(arr.shape))
            args.append(arr)
        shp = params[name][0].shape
        out_specs += [whole(shp)] * 4
        out_shape += [jax.ShapeDtypeStruct(shp, F32)] * 4
    grid_spec = pltpu.PrefetchScalarGridSpec(
        num_scalar_prefetch=1, grid=(1,), in_specs=in_specs, out_specs=out_specs,
        scratch_shapes=[pltpu.VMEM((N_VEC_ROWS, d), F32), pltpu.VMEM((N_VEC_ROWS, cw_cols), F32)])
    outs = pl.pallas_call(
        body, name="adamw_small", grid_spec=grid_spec, out_shape=out_shape,
        compiler_params=pltpu.CompilerParams(
            dimension_semantics=("arbitrary",), vmem_limit_bytes=V7X_VMEM_LIMIT_BYTES),
    )(me, *_in_hbm(args))
    return outs[0], {name: tuple(outs[1 + 4 * a:5 + 4 * a]) for a, name in enumerate(names)}


def _rs_sum(fulls, recvs, shard_ids, slot_ids, name):
    n = len(fulls)

    def body(sh_ref, sl_ref, *refs):
        s = pl.program_id(0)
        for a in range(n):
            full_ref, recv_ref = refs[2 * a], refs[2 * a + 1]
            own_ref, send_ref = refs[2 * n + 2 * a], refs[2 * n + 2 * a + 1]
            v = full_ref[...] + recv_ref[...].astype(F32)

            @pl.when(s == 0)
            def _(own_ref=own_ref, v=v):
                own_ref[...] = v

            @pl.when(s > 0)
            def _(send_ref=send_ref, v=v):
                send_ref[...] = v.astype(send_ref.dtype)

    in_specs, out_specs, out_shape, args = [], [], [], []
    for full, recv in zip(fulls, recvs):
        r, rest = recv.shape[1], tuple(recv.shape[2:])
        zeros = (0,) * len(rest)
        in_specs += [
            pl.BlockSpec((r,) + rest, lambda s, sh, sl, zeros=zeros: (sh[s],) + zeros),
            pl.BlockSpec((None, r) + rest, lambda s, sh, sl, zeros=zeros: (sl[s], 0) + zeros),
        ]
        out_specs += [
            pl.BlockSpec((None, r) + rest, lambda s, sh, sl, zeros=zeros: (0, 0) + zeros),
            pl.BlockSpec((None, r) + rest, lambda s, sh, sl, zeros=zeros: (jnp.maximum(s - 1, 0), 0) + zeros),
        ]
        out_shape += [jax.ShapeDtypeStruct((1, r) + rest, F32), jax.ShapeDtypeStruct((3, r) + rest, recv.dtype)]
        args += [full, recv]
    grid_spec = pltpu.PrefetchScalarGridSpec(
        num_scalar_prefetch=2, grid=(4,), in_specs=in_specs, out_specs=out_specs)
    outs = pl.pallas_call(
        body,
        name=name,
        grid_spec=grid_spec,
        out_shape=out_shape,
        compiler_params=pltpu.CompilerParams(
            dimension_semantics=("arbitrary",), vmem_limit_bytes=V7X_VMEM_LIMIT_BYTES),
    )(shard_ids, slot_ids, *_in_hbm(args))
    return [(outs[2 * a], outs[2 * a + 1]) for a in range(n)]


def _finals(pairs, name, carry=None):
    nb = 4
    n = len(pairs)

    def body(*refs):
        for a in range(n):
            own_ref, recv_ref = refs[2 * a], refs[2 * a + 1]
            acc = own_ref[...]
            for k in range(3):
                acc = acc + recv_ref[k].astype(F32)
            refs[2 * n + a][...] = acc

    in_specs, out_specs, out_shape, args = [], [], [], []
    for own, recv in pairs:
        _, rows, cols = own.shape
        in_specs += [pl.BlockSpec((None, rows // nb, cols), lambda i: (0, i, 0)),
                     pl.BlockSpec((3, rows // nb, cols), lambda i: (0, i, 0))]
        args += [own, recv]
        out_specs.append(pl.BlockSpec((rows // nb, cols), lambda i: (i, 0)))
        out_shape.append(jax.ShapeDtypeStruct((rows, cols), F32))
    return _call(body, name=name, grid=(nb,), in_specs=in_specs, out_specs=out_specs,
                 out_shape=out_shape, args=args, carry=carry)


def _rs_sums(fulls_f32, recv1, tag):
    x, y, c = _place()
    qs = jnp.stack([2 * x + y, 2 * (1 - x) + y, 2 * x + (1 - y), 2 * (1 - x) + (1 - y)]).astype(jnp.int32)
    shard_ids = 2 * qs + c
    return _rs_sum(fulls_f32, recv1, shard_ids, qs, "rs_sum_" + tag)


def _rs_level1(fulls_f32, fulls_send, tag):
    recv1 = _run_plan(_rs_sibling_plan(fulls_send), "rs_sibling_" + tag)
    return _rs_sums(fulls_f32, recv1, tag)


def _rows(g):
    return g.reshape(g.shape[0] * g.shape[1], g.shape[2])


def kernel(x, norm_mix_pre, norm_mix_post, norm_mlp_pre, norm_mlp_post, w_in, b_gate, conv_w, conv_b, lru_w_a, lru_b_a, lru_w_x, lru_b_x, lru_lambda, pool_w, pool_scale, w_lru_up, w_pool_up, w_o, w_ff1, w_ff2, loss_target, m_norm_mix_pre, m_norm_mix_post, m_norm_mlp_pre, m_norm_mlp_post, m_w_in, m_b_gate, m_conv_w, m_conv_b, m_lru_w_a, m_lru_b_a, m_lru_w_x, m_lru_b_x, m_lru_lambda, m_pool_w, m_pool_scale, m_w_lru_up, m_w_pool_up, m_w_o, m_w_ff1, m_w_ff2, v_norm_mix_pre, v_norm_mix_post, v_norm_mlp_pre, v_norm_mlp_post, v_w_in, v_b_gate, v_conv_w, v_conv_b, v_lru_w_a, v_lru_b_a, v_lru_w_x, v_lru_b_x, v_lru_lambda, v_pool_w, v_pool_scale, v_w_lru_up, v_w_pool_up, v_w_o, v_w_ff1, v_w_ff2):
    t, d = x.shape[1], x.shape[2]
    d_rnn = conv_b.shape[1]
    d_pool = pool_scale.shape[1]
    per = LRU_CB // LRU_HEAD_DIM
    xi, yi, ci = _place()
    me = 4 * xi + 2 * yi + ci

    x2d = x[0]
    tgt = loss_target[0]

    s_in = w_in[0].T.astype(BF16)
    s_lu = w_lru_up[0].astype(BF16)
    s_pu = w_pool_up[0].astype(BF16)
    s_o = w_o[0].astype(BF16)
    s_f1 = w_ff1[0].astype(BF16)
    s_f2 = w_ff2[0].astype(BF16)
    s_cw = jnp.pad(conv_w[0], ((0, 4), (0, 0)))

    g_in, g_cw = _run_plan(_ag_plan([s_in, s_cw]), "ag_w_in")
    w_int = _rows(g_in)
    conv_w_full = jnp.transpose(g_cw[:, :4, :], (1, 0, 2)).reshape(4, d_rnn)

    wa_bd, wx_bd = lru_w_a[0], lru_w_x[0]
    pw = pool_w[0]
    pw_bf = pw.astype(BF16)

    pool_block = (2 * d_rnn) // d_pool
    ga_block = (2 * d_rnn + d_pool) // 512
    gb_block = ga_block + d // 512
    g_block = d_rnn // 512

    r_f1, r_f2 = s_f1.shape[0], s_f2.shape[0]
    f1_cut = r_f1 // 4
    f2_cut = (3 * r_f2) // 8
    plan = _join([_ag_plan([s_lu, s_pu]), _ag_plan([s_f1], pieces=[(0, f1_cut)])])
    (proj, h1), got = _norm_proj(x2d, norm_mix_pre, w_int, carry=plan)
    (g_lu, g_pu), (g_f1,) = plan.split(got)
    plan = _join([_ag_plan([s_f1], pieces=[(f1_cut, r_f1 - f1_cut)], bufs=[g_f1]), _ag_plan([s_o])])
    (y_lru, h, lru_saved), got = _lru_fwd(
        proj, conv_w_full, conv_b, wa_bd, lru_b_a, wx_bd, lru_b_x, lru_lambda, carry=plan)
    (g_f1,), (g_o,) = plan.split(got)
    w_lu, w_og = _rows(g_lu), _rows(g_o)
    y_pool, p = _pool_fwd(proj, pw_bf, pool_scale, pool_block)
    (br_a, br_b, mix), (g_f2,) = _branch_mix(
        y_lru, y_pool, w_lu, g_pu, proj, b_gate, ga_block, gb_block,
        carry=_ag_plan([s_f2], pieces=[(0, f2_cut)]))
    (m, x2, h3), _ = _wo_norm(mix, w_og, x2d, norm_mix_post, norm_mlp_pre)
    (rf,), (g_f2,) = _ff1(
        h3, g_f1, carry=_ag_plan([s_f2], pieces=[(f2_cut, r_f2 - f2_cut)], bufs=[g_f2]))
    w_f2 = _rows(g_f2)
    dy, df, dg4, loss_part = _ff2_loss(rf, w_f2, x2, norm_mlp_post, tgt)

    (gw_ff2_32, gw_ff2_16), _ = _wgrad(rf, df, "wgrad_ff2", square_a=True)
    (d_f1,), r1_ff2 = _ff2_bwd(df, w_f2, rf, carry=_rs_sibling_plan([gw_ff2_16]))
    ((own_ff2, send_ff2),) = _rs_sums([gw_ff2_32], r1_ff2, "ff2")
    cut2 = (5 * send_ff2.shape[1]) // 16
    (gw_ff1_32, gw_ff1_16), (r2_ff2,) = _wgrad_cols(
        h3, d_f1, s_f1.shape[1], s_f1.shape[1], "wgrad_ff1",
        carry=_rs_chips_plan([send_ff2], pieces=[(0, cut2)]))
    plan = _join([_rs_chips_plan([send_ff2], pieces=[(cut2, send_ff2.shape[1] - cut2)], bufs=[r2_ff2]),
                  _rs_sibling_plan([gw_ff1_16])])
    (dx2, dm, dg3, dg2), got = _ff1_bwd_norms(d_f1, g_f1, dy, x2, norm_mlp_pre, m, norm_mix_post, carry=plan)
    (r2_ff2,), r1_ff1 = plan.split(got)
    ((own_ff1, send_ff1),) = _rs_sums([gw_ff1_32], r1_ff1, "ff1")
    own_ff1, send_ff1 = own_ff1.reshape((1,) + s_f1.shape), send_ff1.reshape((3,) + s_f1.shape)
    cut = send_ff1.shape[1] // 4
    (gw_o_32, gw_o_16), _ = _wgrad(mix, dm, "wgrad_o")
    (d_br_a, d_br_b, p_ga, p_gb, dbg_a, dbg_b), (r2_ff1,) = _wo_bwd_mix(
        dm, w_og, br_a, br_b, proj, b_gate, ga_block, gb_block,
        carry=_rs_chips_plan([send_ff1], pieces=[(0, cut)]))
    (gw_lu_32, gw_lu_16), _ = _wgrad(y_lru, d_br_a, "wgrad_lru_up")
    (gw_pu_32, gw_pu_16), _ = _wgrad_cols(y_pool, d_br_b, s_pu.shape[1], d, "wgrad_pool_up")
    (dh, p_g), r1_mid = _lru_up_bwd(
        d_br_a, w_lu, proj, h, g_block,
        carry=_rs_sibling_plan([gw_o_16, gw_lu_16, gw_pu_16]))
    mid = _rs_sums([gw_o_32, gw_lu_32, gw_pu_32], r1_mid, "mid")
    plan = _join([_rs_chips_plan([send_ff1], pieces=[(cut, send_ff1.shape[1] - cut)], bufs=[r2_ff1]),
                  _rs_chips_plan([mid[0][1]])])
    (p_x, dwa, db_a, dwx, db_x, dlam, dconv_w, dconv_b), got = _lru_bwd(
        dh, h, lru_saved, proj, conv_w_full, wa_bd, wx_bd, lru_lambda, carry=plan)
    (r2_ff1,), (r2_o,) = plan.split(got)
    p_p, dpool_w, dpool_scale = _pool_bwd(d_br_b, g_pu, p, pw, pool_scale)
    parts = [p_x, p_g, p_p, p_ga, p_gb]
    gw_in, (r2_lu, r2_pu) = _wgrad_parts(
        parts, h1, "wgrad_in", carry=_rs_chips_plan([mid[1][1], mid[2][1]]))
    r2_mid = [r2_o, r2_lu, r2_pu]
    tail = _rs_level1([gw_in[0], dpool_w.reshape(N_DEV, -1, POOL_GROUP_DIM), dwa, dwx],
                      [gw_in[1], dpool_w.reshape(N_DEV, -1, POOL_GROUP_DIM), dwa, dwx], "in")
    (grad_x, dg1), r2_tail = _win_bwd_norm(parts, w_int, dx2, x2d, norm_mix_pre,
                                           carry=_rs_chips_plan([s for _, s in tail]))

    def flat2(a):
        return a.reshape(a.shape[0], -1, a.shape[-1])

    fin_small, _ = _finals([
        (flat2(tail[1][0]), flat2(r2_tail[1])), (flat2(tail[2][0]), flat2(r2_tail[2])),
        (flat2(tail[3][0]), flat2(r2_tail[3])),
    ], "rs_finals_small")

    def pad_row(a):
        return jnp.pad(a, ((0, 0), (0, d - a.shape[1])))

    vecs = jnp.concatenate([dg1, dg2, dg3, dg4, dbg_a, dbg_b, dconv_b, db_a, db_x, dlam,
                            pad_row(dpool_scale), dconv_w, pad_row(loss_part)], axis=0)
    assert vecs.shape[0] == N_VEC_ROWS
    vec_parts, g_pool, g_wa, g_wx = _run_plan(_ag_plan([vecs] + fin_small), "ag_tail")

    big_names = ["w_in", "w_lru_up", "w_pool_up", "w_o", "w_ff1", "w_ff2"]
    big_w = [w_in[0].T, w_lru_up[0], w_pool_up[0], w_o[0], w_ff1[0], w_ff2[0]]
    big_g = [(tail[0][0], r2_tail[0]), (mid[1][0], r2_mid[1]),
             (mid[2][0].reshape((1,) + s_pu.shape), r2_mid[2].reshape((3,) + s_pu.shape)),
             (mid[0][0], r2_mid[0]), (own_ff1, r2_ff1), (own_ff2, r2_ff2)]
    big_m = [m_w_in[0].T, m_w_lru_up[0], m_w_pool_up[0], m_w_o[0], m_w_ff1[0], m_w_ff2[0]]
    big_v = [v_w_in[0].T, v_w_lru_up[0], v_w_pool_up[0], v_w_o[0], v_w_ff1[0], v_w_ff2[0]]
    big_out = _adamw_big(big_w, big_g, big_m, big_v)
    big_out[0] = tuple(o.T for o in big_out[0])

    small = dict(
        norm_mix_pre=(norm_mix_pre, m_norm_mix_pre, v_norm_mix_pre),
        norm_mix_post=(norm_mix_post, m_norm_mix_post, v_norm_mix_post),
        norm_mlp_pre=(norm_mlp_pre, m_norm_mlp_pre, v_norm_mlp_pre),
        norm_mlp_post=(norm_mlp_post, m_norm_mlp_post, v_norm_mlp_post),
        b_gate=(b_gate, m_b_gate, v_b_gate), conv_w=(conv_w, m_conv_w, v_conv_w),
        conv_b=(conv_b, m_conv_b, v_conv_b), lru_w_a=(lru_w_a, m_lru_w_a, v_lru_w_a),
        lru_b_a=(lru_b_a, m_lru_b_a, v_lru_b_a), lru_w_x=(lru_w_x, m_lru_w_x, v_lru_w_x),
        lru_b_x=(lru_b_x, m_lru_b_x, v_lru_b_x), lru_lambda=(lru_lambda, m_lru_lambda, v_lru_lambda),
        pool_w=(pool_w, m_pool_w, v_pool_w), pool_scale=(pool_scale, m_pool_scale, v_pool_scale))
    loss_row, small_out = _adamw_small(
        vec_parts, g_pool.reshape(pool_w.shape), g_wa.reshape(lru_w_a.shape), g_wx.reshape(lru_w_x.shape),
        jnp.reshape(me, (1,)).astype(jnp.int32), small)
    grads = {n: o[0] for n, o in small_out.items()}
    delta = {n: o[1] for n, o in small_out.items()}
    new_m = {n: o[2] for n, o in small_out.items()}
    new_v = {n: o[3] for n, o in small_out.items()}

    for name, (g, dl, nm, nv) in zip(big_names, big_out):
        grads[name], delta[name], new_m[name], new_v[name] = g[None], dl[None], nm[None], nv[None]

    loss = loss_row[0, 0]
    order = ["norm_mix_pre", "norm_mix_post", "norm_mlp_pre", "norm_mlp_post", "w_in", "b_gate", "conv_w",
             "conv_b", "lru_w_a", "lru_b_a", "lru_w_x", "lru_b_x", "lru_lambda", "pool_w", "pool_scale",
             "w_lru_up", "w_pool_up", "w_o", "w_ff1", "w_ff2"]
    return (loss, grad_x[None], *[grads[n] for n in order], *[delta[n] for n in order],
            *[new_m[n] for n in order], *[new_v[n] for n in order])
```
